```python
import math
import jax, jax.numpy as jnp
from jax import lax
import numpy as np

D_MODEL = 1024
BATCH = 8
SEQ = 8192
DEPTH = 1

D_PLE = 256
GRID_W = 64
D_SSM = 512
SSM_GROUP = 16
N_SSM_GROUPS = D_SSM // SSM_GROUP
SSM_STATE = 64
DT_MIN = 1e-3
DT_MAX = 1e-1
D_NA = 512
NA_HEADS = 8
NA_HEAD_DIM = D_NA // NA_HEADS
NA_ROWS_MAX = 8
NA_COLS = 16
D_MIX = D_SSM + D_NA
D_IN_PROJ = 2 * D_SSM + 4 * D_NA
EPS = 1e-6

kernel_name = "hybrid_s5_natten_sandwich_ple_encoder"


def rms_norm(x, gain):
    xf = x.astype(jnp.float32)
    y = xf * lax.rsqrt(jnp.mean(xf * xf, axis=-1, keepdims=True) + EPS)
    return (y * gain.astype(jnp.float32)).astype(x.dtype)


def _complex_linear_combine(left, right):
    a1r, a1i, b1r, b1i = left
    a2r, a2i, b2r, b2i = right
    return (a2r * a1r - a2i * a1i,
            a2r * a1i + a2i * a1r,
            a2r * b1r - a2i * b1i + b2r,
            a2r * b1i + a2i * b1r + b2i)


def s5_bidirectional(u, a_re, a_im, log_dt, b_re, b_im, c_re, c_im, d):
    f32 = jnp.float32
    bsz, seqlen, _ = u.shape
    uf = u.astype(f32).reshape(bsz, seqlen, N_SSM_GROUPS, SSM_GROUP)
    y = uf * d.astype(f32)
    for direction in range(2):
        ar = a_re[direction].astype(f32)
        ai = a_im[direction].astype(f32)
        dt = jnp.exp(log_dt[direction].astype(f32))[:, None]
        mag = jnp.exp(dt * ar)
        abar_re = mag * jnp.cos(dt * ai)
        abar_im = mag * jnp.sin(dt * ai)
        num_re = abar_re - 1.0
        num_im = abar_im
        denom = ar * ar + ai * ai
        coef_re = (num_re * ar + num_im * ai) / denom
        coef_im = (num_im * ar - num_re * ai) / denom
        br = b_re[direction].astype(f32)
        bi = b_im[direction].astype(f32)
        bbar_re = coef_re[..., None] * br - coef_im[..., None] * bi
        bbar_im = coef_re[..., None] * bi + coef_im[..., None] * br
        bu_re = jnp.einsum('blgh,gph->blgp', uf, bbar_re)
        bu_im = jnp.einsum('blgh,gph->blgp', uf, bbar_im)
        shp = (1, seqlen, N_SSM_GROUPS, SSM_STATE)
        a_seq_re = jnp.broadcast_to(abar_re, shp)
        a_seq_im = jnp.broadcast_to(abar_im, shp)
        _, _, h_re, h_im = lax.associative_scan(
            _complex_linear_combine, (a_seq_re, a_seq_im, bu_re, bu_im),
            reverse=(direction == 1), axis=1)
        cr = c_re[direction].astype(f32)
        ci = c_im[direction].astype(f32)
        y = y + jnp.einsum('blgp,ghp->blgh', h_re, cr) - jnp.einsum('blgp,ghp->blgh', h_im, ci)
    return y.reshape(bsz, seqlen, D_SSM)


def neighborhood_attention_2d(q, k, v, rpb):
    f32 = jnp.float32
    bsz, seqlen, _ = q.shape
    rows = seqlen // GRID_W
    kh = min(NA_ROWS_MAX, rows)
    shp = (bsz, rows, GRID_W, NA_HEADS, NA_HEAD_DIM)
    q = q.reshape(shp)
    k = k.reshape(shp)
    v = v.reshape(shp)
    r = jnp.arange(rows)
    row_start = jnp.clip(r - kh // 2, 0, rows - kh)
    row_idx = row_start[:, None] + jnp.arange(kh)[None, :]
    k_blk = k[:, row_idx]
    v_blk = v[:, row_idx]
    c = jnp.arange(GRID_W)
    col_start = jnp.clip(c - NA_COLS // 2, 0, GRID_W - NA_COLS)
    col_in = (c[None, :] >= col_start[:, None]) & (c[None, :] < col_start[:, None] + NA_COLS)
    dr = row_idx - r[:, None] + (NA_ROWS_MAX - 1)
    dc = jnp.clip(c[None, :] - c[:, None] + (NA_COLS - 1), 0, 2 * NA_COLS - 2)
    bias = rpb.astype(f32)[:, dr[:, None, :, None], dc[None, :, None, :]]
    scale = NA_HEAD_DIM ** -0.5
    scores = jnp.einsum('brqhd,brikhd->bhrqik', q, k_blk,
                        preferred_element_type=f32) * scale + bias
    scores = jnp.where(col_in[:, None, :], scores, jnp.finfo(f32).min)
    probs = jax.nn.softmax(scores, axis=(-2, -1))
    out = jnp.einsum('bhrqik,brikhd->brqhd', probs.astype(v.dtype), v_blk)
    return out.reshape(bsz, seqlen, D_NA)


def _fwd_setup_inputs(seed: int = 0) -> dict:
    key = jax.random.key(seed)
    ks = jax.random.split(key, 24)
    f32 = jnp.float32
    G, P, H = N_SSM_GROUPS, SSM_STATE, SSM_GROUP
    nrm = lambda kk, shape, s: jax.random.normal(kk, shape, f32) * s
    n_idx = jnp.arange(P, dtype=f32)
    return {
        "x": nrm(ks[0], (BATCH, SEQ, D_MODEL), 1.0),
        "p": nrm(ks[1], (DEPTH, BATCH, SEQ, D_PLE), 1.0),
        "norm_pre": 1.0 + nrm(ks[2], (DEPTH, D_MODEL), 0.02),
        "norm_post": 1.0 + nrm(ks[3], (DEPTH, D_MODEL), 0.02),
        "w_in": nrm(ks[4], (DEPTH, D_MODEL, D_IN_PROJ), D_MODEL ** -0.5),
        "ssm_a_re": -0.5 + nrm(ks[5], (DEPTH, 2, G, P), 0.01),
        "ssm_a_im": math.pi * n_idx + nrm(ks[6], (DEPTH, 2, G, P), 0.01),
        "ssm_log_dt": jax.random.uniform(ks[7], (DEPTH, 2, G), f32,
                                         minval=math.log(DT_MIN), maxval=math.log(DT_MAX)),
        "ssm_b_re": nrm(ks[8], (DEPTH, 2, G, P, H), H ** -0.5),
        "ssm_b_im": nrm(ks[9], (DEPTH, 2, G, P, H), H ** -0.5),
        "ssm_c_re": nrm(ks[10], (DEPTH, 2, G, H, P), P ** -0.5),
        "ssm_c_im": nrm(ks[11], (DEPTH, 2, G, H, P), P ** -0.5),
        "ssm_d": nrm(ks[12], (DEPTH, G, H), 1.0),
        "w_glu": nrm(ks[13], (DEPTH, D_SSM, D_SSM), D_SSM ** -0.5),
        "b_glu": nrm(ks[14], (DEPTH, D_SSM), 0.01),
        "na_rpb": nrm(ks[15], (DEPTH, NA_HEADS, 2 * NA_ROWS_MAX - 1, 2 * NA_COLS - 1), 0.02),
        "w_out": nrm(ks[16], (DEPTH, D_MIX, D_MODEL), D_MIX ** -0.5),
        "w_ple": nrm(ks[17], (DEPTH, D_PLE, D_MODEL), D_PLE ** -0.5),
        "ple_norm": 1.0 + nrm(ks[18], (DEPTH, D_MODEL), 0.02),
        "w_ple_gate": nrm(ks[19], (DEPTH, D_MODEL, D_MODEL), D_MODEL ** -0.5),
    }


def _fwd_reference(x, p, norm_pre, norm_post, w_in, ssm_a_re, ssm_a_im, ssm_log_dt,
              ssm_b_re, ssm_b_im, ssm_c_re, ssm_c_im, ssm_d, w_glu, b_glu,
              na_rpb, w_out, w_ple, ple_norm, w_ple_gate):
    h = x
    splits = [D_SSM, 2 * D_SSM, 2 * D_SSM + D_NA, 2 * D_SSM + 2 * D_NA, 2 * D_SSM + 3 * D_NA]
    for i in range(DEPTH):
        hn = rms_norm(h, norm_pre[i])
        proj = hn @ w_in[i]
        u_s, z_s, q, k, v, z_n = jnp.split(proj, splits, axis=-1)
        y_s = s5_bidirectional(u_s, ssm_a_re[i], ssm_a_im[i], ssm_log_dt[i],
                               ssm_b_re[i], ssm_b_im[i], ssm_c_re[i], ssm_c_im[i], ssm_d[i])
        y_s = jax.nn.gelu(y_s.astype(hn.dtype))
        y_s = y_s * jax.nn.sigmoid(y_s @ w_glu[i] + b_glu[i])
        y_s = y_s * jax.nn.silu(z_s)
        y_n = neighborhood_attention_2d(q, k, v, na_rpb[i]) * jax.nn.silu(z_n)
        mix = jnp.concatenate([y_s, y_n], axis=-1) @ w_out[i]
        h = h + rms_norm(mix, norm_post[i])
        e = rms_norm(p[i] @ w_ple[i], ple_norm[i])
        h = h + jax.nn.sigmoid(h @ w_ple_gate[i]) * e
    return h


import jax as _jax
import jax.numpy as _jnp

TWIN_FORMAT = 'train_step'
FWD_PARAMS = ['x', 'p', 'norm_pre', 'norm_post', 'w_in', 'ssm_a_re', 'ssm_a_im', 'ssm_log_dt', 'ssm_b_re', 'ssm_b_im', 'ssm_c_re', 'ssm_c_im', 'ssm_d', 'w_glu', 'b_glu', 'na_rpb', 'w_out', 'w_ple', 'ple_norm', 'w_ple_gate']
TWIN_WEIGHTS = ['norm_pre', 'norm_post', 'w_in', 'ssm_a_re', 'ssm_a_im', 'ssm_log_dt', 'ssm_b_re', 'ssm_b_im', 'ssm_c_re', 'ssm_c_im', 'ssm_d', 'w_glu', 'b_glu', 'na_rpb', 'w_out', 'w_ple', 'ple_norm', 'w_ple_gate']
TWIN_DIFF_INPUT = 'x'
TWIN_INPUTS = ['x', 'p', 'norm_pre', 'norm_post', 'w_in', 'ssm_a_re', 'ssm_a_im', 'ssm_log_dt', 'ssm_b_re', 'ssm_b_im', 'ssm_c_re', 'ssm_c_im', 'ssm_d', 'w_glu', 'b_glu', 'na_rpb', 'w_out', 'w_ple', 'ple_norm', 'w_ple_gate', 'loss_target', 'm_norm_pre', 'm_norm_post', 'm_w_in', 'm_ssm_a_re', 'm_ssm_a_im', 'm_ssm_log_dt', 'm_ssm_b_re', 'm_ssm_b_im', 'm_ssm_c_re', 'm_ssm_c_im', 'm_ssm_d', 'm_w_glu', 'm_b_glu', 'm_na_rpb', 'm_w_out', 'm_w_ple', 'm_ple_norm', 'm_w_ple_gate', 'v_norm_pre', 'v_norm_post', 'v_w_in', 'v_ssm_a_re', 'v_ssm_a_im', 'v_ssm_log_dt', 'v_ssm_b_re', 'v_ssm_b_im', 'v_ssm_c_re', 'v_ssm_c_im', 'v_ssm_d', 'v_w_glu', 'v_b_glu', 'v_na_rpb', 'v_w_out', 'v_w_ple', 'v_ple_norm', 'v_w_ple_gate']
TWIN_OUTPUTS = ['loss', 'grad_x', 'grad_norm_pre', 'grad_norm_post', 'grad_w_in', 'grad_ssm_a_re', 'grad_ssm_a_im', 'grad_ssm_log_dt', 'grad_ssm_b_re', 'grad_ssm_b_im', 'grad_ssm_c_re', 'grad_ssm_c_im', 'grad_ssm_d', 'grad_w_glu', 'grad_b_glu', 'grad_na_rpb', 'grad_w_out', 'grad_w_ple', 'grad_ple_norm', 'grad_w_ple_gate', 'delta_norm_pre', 'delta_norm_post', 'delta_w_in', 'delta_ssm_a_re', 'delta_ssm_a_im', 'delta_ssm_log_dt', 'delta_ssm_b_re', 'delta_ssm_b_im', 'delta_ssm_c_re', 'delta_ssm_c_im', 'delta_ssm_d', 'delta_w_glu', 'delta_b_glu', 'delta_na_rpb', 'delta_w_out', 'delta_w_ple', 'delta_ple_norm', 'delta_w_ple_gate', 'new_m_norm_pre', 'new_m_norm_post', 'new_m_w_in', 'new_m_ssm_a_re', 'new_m_ssm_a_im', 'new_m_ssm_log_dt', 'new_m_ssm_b_re', 'new_m_ssm_b_im', 'new_m_ssm_c_re', 'new_m_ssm_c_im', 'new_m_ssm_d', 'new_m_w_glu', 'new_m_b_glu', 'new_m_na_rpb', 'new_m_w_out', 'new_m_w_ple', 'new_m_ple_norm', 'new_m_w_ple_gate', 'new_v_norm_pre', 'new_v_norm_post', 'new_v_w_in', 'new_v_ssm_a_re', 'new_v_ssm_a_im', 'new_v_ssm_log_dt', 'new_v_ssm_b_re', 'new_v_ssm_b_im', 'new_v_ssm_c_re', 'new_v_ssm_c_im', 'new_v_ssm_d', 'new_v_w_glu', 'new_v_b_glu', 'new_v_na_rpb', 'new_v_w_out', 'new_v_w_ple', 'new_v_ple_norm', 'new_v_w_ple_gate']
TWIN_LEAF_KINDS = {'loss': 'loss', 'grad_x': 'grad_x', 'grad_norm_pre': 'grad_w', 'grad_norm_post': 'grad_w', 'grad_w_in': 'grad_w', 'grad_ssm_a_re': 'grad_w', 'grad_ssm_a_im': 'grad_w', 'grad_ssm_log_dt': 'grad_w', 'grad_ssm_b_re': 'grad_w', 'grad_ssm_b_im': 'grad_w', 'grad_ssm_c_re': 'grad_w', 'grad_ssm_c_im': 'grad_w', 'grad_ssm_d': 'grad_w', 'grad_w_glu': 'grad_w', 'grad_b_glu': 'grad_w', 'grad_na_rpb': 'grad_w', 'grad_w_out': 'grad_w', 'grad_w_ple': 'grad_w', 'grad_ple_norm': 'grad_w', 'grad_w_ple_gate': 'grad_w', 'delta_norm_pre': 'delta_w', 'delta_norm_post': 'delta_w', 'delta_w_in': 'delta_w', 'delta_ssm_a_re': 'delta_w', 'delta_ssm_a_im': 'delta_w', 'delta_ssm_log_dt': 'delta_w', 'delta_ssm_b_re': 'delta_w', 'delta_ssm_b_im': 'delta_w', 'delta_ssm_c_re': 'delta_w', 'delta_ssm_c_im': 'delta_w', 'delta_ssm_d': 'delta_w', 'delta_w_glu': 'delta_w', 'delta_b_glu': 'delta_w', 'delta_na_rpb': 'delta_w', 'delta_w_out': 'delta_w', 'delta_w_ple': 'delta_w', 'delta_ple_norm': 'delta_w', 'delta_w_ple_gate': 'delta_w', 'new_m_norm_pre': 'new_m', 'new_m_norm_post': 'new_m', 'new_m_w_in': 'new_m', 'new_m_ssm_a_re': 'new_m', 'new_m_ssm_a_im': 'new_m', 'new_m_ssm_log_dt': 'new_m', 'new_m_ssm_b_re': 'new_m', 'new_m_ssm_b_im': 'new_m', 'new_m_ssm_c_re': 'new_m', 'new_m_ssm_c_im': 'new_m', 'new_m_ssm_d': 'new_m', 'new_m_w_glu': 'new_m', 'new_m_b_glu': 'new_m', 'new_m_na_rpb': 'new_m', 'new_m_w_out': 'new_m', 'new_m_w_ple': 'new_m', 'new_m_ple_norm': 'new_m', 'new_m_w_ple_gate': 'new_m', 'new_v_norm_pre': 'new_v', 'new_v_norm_post': 'new_v', 'new_v_w_in': 'new_v', 'new_v_ssm_a_re': 'new_v', 'new_v_ssm_a_im': 'new_v', 'new_v_ssm_log_dt': 'new_v', 'new_v_ssm_b_re': 'new_v', 'new_v_ssm_b_im': 'new_v', 'new_v_ssm_c_re': 'new_v', 'new_v_ssm_c_im': 'new_v', 'new_v_ssm_d': 'new_v', 'new_v_w_glu': 'new_v', 'new_v_b_glu': 'new_v', 'new_v_na_rpb': 'new_v', 'new_v_w_out': 'new_v', 'new_v_w_ple': 'new_v', 'new_v_ple_norm': 'new_v', 'new_v_w_ple_gate': 'new_v'}


def _forward(args):
    return _fwd_reference(*[args[k] for k in FWD_PARAMS])


def _output_shape():
    def fwd():
        inp = _fwd_setup_inputs(0)
        return _fwd_reference(*[inp[k] for k in FWD_PARAMS])
    out = _jax.eval_shape(fwd)
    return out.shape, out.dtype

N_MICROBATCH = 1
ADAM_LR = 0.001
ADAM_B1 = 0.9
ADAM_B2 = 0.999
ADAM_EPS = 1e-08
ADAM_WD = 0.01
ADAM_STEP = 10
PER_EXAMPLE_BATCH_AXIS = {'x': 0, 'p': 1, 'loss_target': 0}
SHARED_INPUTS = []
_WEIGHT_DTYPES = {'norm_pre': _jnp.float32, 'norm_post': _jnp.float32, 'w_in': _jnp.float32, 'ssm_a_re': _jnp.float32, 'ssm_a_im': _jnp.float32, 'ssm_log_dt': _jnp.float32, 'ssm_b_re': _jnp.float32, 'ssm_b_im': _jnp.float32, 'ssm_c_re': _jnp.float32, 'ssm_c_im': _jnp.float32, 'ssm_d': _jnp.float32, 'w_glu': _jnp.float32, 'b_glu': _jnp.float32, 'na_rpb': _jnp.float32, 'w_out': _jnp.float32, 'w_ple': _jnp.float32, 'ple_norm': _jnp.float32, 'w_ple_gate': _jnp.float32}
MOMENT_SCALE = {'norm_pre': 7.883791e-01, 'norm_post': 6.556780e+01, 'w_in': 4.506917e-01, 'ssm_a_re': 6.776809e-02, 'ssm_a_im': 6.832989e-02, 'ssm_log_dt': 5.616295e+01, 'ssm_b_re': 3.035112e-02, 'ssm_b_im': 3.128051e-02, 'ssm_c_re': 6.102854e-02, 'ssm_c_im': 6.212880e-02, 'ssm_d': 1.404001e+00, 'w_glu': 2.199659e-01, 'b_glu': 6.041071e-01, 'na_rpb': 9.817532e-02, 'w_out': 8.737507e-01, 'w_ple': 2.634110e-01, 'ple_norm': 2.060221e+01, 'w_ple_gate': 7.350154e-01}


def _to_microbatches(a, axis):
    t = _jnp.moveaxis(a, axis, 0)
    t = t.reshape((N_MICROBATCH, t.shape[0] // N_MICROBATCH) + t.shape[1:])
    return _jnp.moveaxis(t, 1, axis + 1)


def setup_inputs(seed: int = 0) -> dict:
    inp = _fwd_setup_inputs(seed)
    key = _jax.random.fold_in(_jax.random.key(seed), 7919)
    shape, _ = _output_shape()
    out = dict(inp)
    out["loss_target"] = _jax.random.normal(_jax.random.fold_in(key, 0), shape, _jnp.float32)
    for i, name in enumerate(TWIN_WEIGHTS):
        w = inp[name].astype(_jnp.float32)
        if MOMENT_SCALE is None:
            s = _jnp.sqrt(_jnp.mean(_jnp.square(w)) + 1e-30)
        else:
            s = MOMENT_SCALE[name]
        km, kv = _jax.random.split(_jax.random.fold_in(key, i + 1))
        out[name] = w
        out["m_" + name] = s * _jax.random.normal(km, w.shape, _jnp.float32)
        out["v_" + name] = (s * s) * _jax.random.uniform(kv, w.shape, _jnp.float32, 0.5, 1.5)
    if N_MICROBATCH > 1:
        for name, axis in PER_EXAMPLE_BATCH_AXIS.items():
            out[name] = _to_microbatches(out[name], axis)
    return {'x': out['x'], 'p': out['p'], 'norm_pre': out['norm_pre'], 'norm_post': out['norm_post'], 'w_in': out['w_in'], 'ssm_a_re': out['ssm_a_re'], 'ssm_a_im': out['ssm_a_im'], 'ssm_log_dt': out['ssm_log_dt'], 'ssm_b_re': out['ssm_b_re'], 'ssm_b_im': out['ssm_b_im'], 'ssm_c_re': out['ssm_c_re'], 'ssm_c_im': out['ssm_c_im'], 'ssm_d': out['ssm_d'], 'w_glu': out['w_glu'], 'b_glu': out['b_glu'], 'na_rpb': out['na_rpb'], 'w_out': out['w_out'], 'w_ple': out['w_ple'], 'ple_norm': out['ple_norm'], 'w_ple_gate': out['w_ple_gate'], 'loss_target': out['loss_target'], 'm_norm_pre': out['m_norm_pre'], 'm_norm_post': out['m_norm_post'], 'm_w_in': out['m_w_in'], 'm_ssm_a_re': out['m_ssm_a_re'], 'm_ssm_a_im': out['m_ssm_a_im'], 'm_ssm_log_dt': out['m_ssm_log_dt'], 'm_ssm_b_re': out['m_ssm_b_re'], 'm_ssm_b_im': out['m_ssm_b_im'], 'm_ssm_c_re': out['m_ssm_c_re'], 'm_ssm_c_im': out['m_ssm_c_im'], 'm_ssm_d': out['m_ssm_d'], 'm_w_glu': out['m_w_glu'], 'm_b_glu': out['m_b_glu'], 'm_na_rpb': out['m_na_rpb'], 'm_w_out': out['m_w_out'], 'm_w_ple': out['m_w_ple'], 'm_ple_norm': out['m_ple_norm'], 'm_w_ple_gate': out['m_w_ple_gate'], 'v_norm_pre': out['v_norm_pre'], 'v_norm_post': out['v_norm_post'], 'v_w_in': out['v_w_in'], 'v_ssm_a_re': out['v_ssm_a_re'], 'v_ssm_a_im': out['v_ssm_a_im'], 'v_ssm_log_dt': out['v_ssm_log_dt'], 'v_ssm_b_re': out['v_ssm_b_re'], 'v_ssm_b_im': out['v_ssm_b_im'], 'v_ssm_c_re': out['v_ssm_c_re'], 'v_ssm_c_im': out['v_ssm_c_im'], 'v_ssm_d': out['v_ssm_d'], 'v_w_glu': out['v_w_glu'], 'v_b_glu': out['v_b_glu'], 'v_na_rpb': out['v_na_rpb'], 'v_w_out': out['v_w_out'], 'v_w_ple': out['v_w_ple'], 'v_ple_norm': out['v_ple_norm'], 'v_w_ple_gate': out['v_w_ple_gate']}


def _loss(weights, diff, rest, loss_target):
    with _jax.named_scope("forward"):
        args = {**rest, TWIN_DIFF_INPUT: diff, **{k: w.astype(_WEIGHT_DTYPES[k]) for k, w in weights.items()}}
        y = _forward(args)
    with _jax.named_scope("loss_head"):
        err = _jnp.square(y.astype(_jnp.float32) - loss_target)
        return 0.5 * _jnp.sum(_jnp.mean(err, axis=-1)) if err.ndim else 0.5 * err


def _adamw(w, g, m, v):
    m = ADAM_B1 * m + (1.0 - ADAM_B1) * g
    v = ADAM_B2 * v + (1.0 - ADAM_B2) * _jnp.square(g)
    m_hat = m / (1.0 - ADAM_B1 ** ADAM_STEP)
    v_hat = v / (1.0 - ADAM_B2 ** ADAM_STEP)
    delta = -ADAM_LR * (m_hat / (_jnp.sqrt(v_hat) + ADAM_EPS) + ADAM_WD * w)
    return delta, m, v


def reference(x, p, norm_pre, norm_post, w_in, ssm_a_re, ssm_a_im, ssm_log_dt, ssm_b_re, ssm_b_im, ssm_c_re, ssm_c_im, ssm_d, w_glu, b_glu, na_rpb, w_out, w_ple, ple_norm, w_ple_gate, loss_target, m_norm_pre, m_norm_post, m_w_in, m_ssm_a_re, m_ssm_a_im, m_ssm_log_dt, m_ssm_b_re, m_ssm_b_im, m_ssm_c_re, m_ssm_c_im, m_ssm_d, m_w_glu, m_b_glu, m_na_rpb, m_w_out, m_w_ple, m_ple_norm, m_w_ple_gate, v_norm_pre, v_norm_post, v_w_in, v_ssm_a_re, v_ssm_a_im, v_ssm_log_dt, v_ssm_b_re, v_ssm_b_im, v_ssm_c_re, v_ssm_c_im, v_ssm_d, v_w_glu, v_b_glu, v_na_rpb, v_w_out, v_w_ple, v_ple_norm, v_w_ple_gate):
    given = dict(x=x, p=p, norm_pre=norm_pre, norm_post=norm_post, w_in=w_in, ssm_a_re=ssm_a_re, ssm_a_im=ssm_a_im, ssm_log_dt=ssm_log_dt, ssm_b_re=ssm_b_re, ssm_b_im=ssm_b_im, ssm_c_re=ssm_c_re, ssm_c_im=ssm_c_im, ssm_d=ssm_d, w_glu=w_glu, b_glu=b_glu, na_rpb=na_rpb, w_out=w_out, w_ple=w_ple, ple_norm=ple_norm, w_ple_gate=w_ple_gate, loss_target=loss_target, m_norm_pre=m_norm_pre, m_norm_post=m_norm_post, m_w_in=m_w_in, m_ssm_a_re=m_ssm_a_re, m_ssm_a_im=m_ssm_a_im, m_ssm_log_dt=m_ssm_log_dt, m_ssm_b_re=m_ssm_b_re, m_ssm_b_im=m_ssm_b_im, m_ssm_c_re=m_ssm_c_re, m_ssm_c_im=m_ssm_c_im, m_ssm_d=m_ssm_d, m_w_glu=m_w_glu, m_b_glu=m_b_glu, m_na_rpb=m_na_rpb, m_w_out=m_w_out, m_w_ple=m_w_ple, m_ple_norm=m_ple_norm, m_w_ple_gate=m_w_ple_gate, v_norm_pre=v_norm_pre, v_norm_post=v_norm_post, v_w_in=v_w_in, v_ssm_a_re=v_ssm_a_re, v_ssm_a_im=v_ssm_a_im, v_ssm_log_dt=v_ssm_log_dt, v_ssm_b_re=v_ssm_b_re, v_ssm_b_im=v_ssm_b_im, v_ssm_c_re=v_ssm_c_re, v_ssm_c_im=v_ssm_c_im, v_ssm_d=v_ssm_d, v_w_glu=v_w_glu, v_b_glu=v_b_glu, v_na_rpb=v_na_rpb, v_w_out=v_w_out, v_w_ple=v_w_ple, v_ple_norm=v_ple_norm, v_w_ple_gate=v_w_ple_gate)
    weights = {n: given[n] for n in TWIN_WEIGHTS}
    shared = {n: given[n] for n in SHARED_INPUTS}
    per_example = {n: given[n] for n in ['x', 'p']}
    grad_fn = _jax.value_and_grad(_loss, argnums=(0, 1))

    def one_microbatch(ex, loss_target):
        ex = dict(ex)
        diff = ex.pop(TWIN_DIFF_INPUT)
        return grad_fn(weights, diff, {**shared, **ex}, loss_target)

    if N_MICROBATCH == 1:
        loss, (grad_w, grad_x) = one_microbatch(per_example, given["loss_target"])
    else:
        def body(carry, xs):
            loss_sum, grad_sum = carry
            l_k, (gw_k, gx_k) = one_microbatch(xs[0], xs[1])
            with _jax.named_scope("update"):
                return (loss_sum + l_k, _jax.tree.map(_jnp.add, grad_sum, gw_k)), gx_k

        init = (_jnp.zeros((), _jnp.float32), _jax.tree.map(_jnp.zeros_like, weights))
        (loss, grad_w), grad_x = _jax.lax.scan(body, init, (per_example, given["loss_target"]))
    with _jax.named_scope("update"):
        delta_w, new_m, new_v = {}, {}, {}
        for n in TWIN_WEIGHTS:
            delta_w[n], new_m[n], new_v[n] = _adamw(weights[n], grad_w[n], given["m_" + n], given["v_" + n])
    return (loss, grad_x, *[grad_w[n] for n in TWIN_WEIGHTS], *[delta_w[n] for n in TWIN_WEIGHTS],
            *[new_m[n] for n in TWIN_WEIGHTS], *[new_v[n] for n in TWIN_WEIGHTS])
```

```python
import functools
import math

import jax
import jax.numpy as jnp
import numpy as np
from jax import lax
from jax.experimental import pallas as pl
from jax.experimental.pallas import tpu as pltpu

F32 = jnp.float32
BF16 = jnp.bfloat16

D_MODEL = 1024
D_PLE = 256
GRID_W = 64
D_SSM = 512
SSM_GROUP = 16
N_GROUPS = 32
SSM_STATE = 64
D_NA = 512
NA_HEADS = 8
NA_HEAD_DIM = 64
NA_ROWS = 8
NA_COLS = 16
D_IN_PROJ = 3072
EPS = 1e-6

CHUNK = 16
GROUPS_PER_BLOCK = 8
N_BLOCKS = N_GROUPS // GROUPS_PER_BLOCK
BLOCK_CH = GROUPS_PER_BLOCK * SSM_GROUP
BLOCK_ST = GROUPS_PER_BLOCK * SSM_STATE
CHUNK_W = CHUNK * BLOCK_CH
STATE_W = 4 * BLOCK_ST

N_CHIPS = 4
MESH = pl.DeviceIdType.MESH

ADAM_LR = 0.001
ADAM_B1 = 0.9
ADAM_B2 = 0.999
ADAM_EPS = 1e-08
ADAM_WD = 0.01
ADAM_STEP = 10

VMEM_LIMIT = 52 * 1024 * 1024
HIGHEST = lax.Precision.HIGHEST


def _cparams(sem=None, **kw):
    if sem is not None:
        kw["dimension_semantics"] = sem
    return pltpu.CompilerParams(vmem_limit_bytes=VMEM_LIMIT, **kw)


def _dot(a, b, dims=((1,), (0,))):
    return lax.dot_general(a, b, (dims, ((), ())), preferred_element_type=F32)


def _dot_nt(a, b):
    return _dot(a, b, ((1,), (1,)))


def _dot_tn(a, b):
    return _dot(a, b, ((0,), (0,)))


def _sigmoid(x):
    return 1.0 / (1.0 + jnp.exp(-x))


_GELU_C = math.sqrt(2.0 / math.pi)


def _gelu_parts(x):
    inner = _GELU_C * (x + 0.044715 * (x * x * x))
    t = jnp.tanh(inner)
    return 0.5 * x * (1.0 + t), t


def _gelu_grad(x, t):
    return 0.5 * (1.0 + t) + 0.5 * x * (1.0 - t * t) * (_GELU_C * (1.0 + 3.0 * 0.044715 * x * x))


def _silu_parts(z):
    s = _sigmoid(z)
    return z * s, s


def _silu_grad(z, s):
    return s * (1.0 + z * (1.0 - s))


def _rms(x):
    r = lax.rsqrt(jnp.mean(x * x, axis=-1, keepdims=True) + EPS)
    return x * r, r


def _rms_bwd(dn, n, r):
    return r * (dn - n * jnp.mean(dn * n, axis=-1, keepdims=True))


def _in_proj(x, g_pre, w_in_g, tm=256):
    L = x.shape[0]
    wn = w_in_g.shape[2]

    def body(x_ref, g_ref, w_ref, u_ref, zs_ref, q_ref, k_ref, v_ref, zn_ref):
        n, _ = _rms(x_ref[...])
        hn = (n * g_ref[...]).astype(BF16)
        proj = jnp.concatenate([_dot(hn, w_ref[j]) for j in range(N_CHIPS)], axis=1)
        u_ref[...] = proj[:, 0:512]
        zs_ref[...] = proj[:, 512:1024]
        q_ref[...] = (proj[:, 1024:1536] * (NA_HEAD_DIM ** -0.5)).astype(BF16)
        k_ref[...] = proj[:, 1536:2048].astype(BF16)
        v_ref[...] = proj[:, 2048:2560].astype(BF16)
        zn_ref[...] = proj[:, 2560:3072]

    tok = lambda w, dt: jax.ShapeDtypeStruct((L, w), dt)
    tspec = pl.BlockSpec((tm, 512), lambda i: (i, 0))
    return pl.pallas_call(
        body, name="in_proj", grid=(L // tm,),
        in_specs=[pl.BlockSpec((tm, D_MODEL), lambda i: (i, 0)),
                  pl.BlockSpec((1, D_MODEL), lambda i: (0, 0)),
                  pl.BlockSpec((N_CHIPS, D_MODEL, wn), lambda i: (0, 0, 0))],
        out_specs=[tspec] * 6,
        out_shape=[tok(512, F32), tok(512, F32), tok(512, BF16), tok(512, BF16), tok(512, BF16), tok(512, F32)],
        compiler_params=_cparams(("arbitrary",)),
    )(x, g_pre, w_in_g)


def _ssm_block_params(a_re, a_im, log_dt, b_re, b_im, c_re, c_im, d):
    eye_g = jnp.eye(GROUPS_PER_BLOCK, dtype=F32)[None, None, :, None, :, None]

    def lanes(t):
        return t.reshape(2, N_BLOCKS, 1, BLOCK_ST)

    def expand(t):
        return (t[:, :, :, :, None, :] * eye_g).reshape(2, N_BLOCKS, BLOCK_CH, BLOCK_ST)

    b_shape = (2, N_BLOCKS, GROUPS_PER_BLOCK, SSM_STATE, SSM_GROUP)
    c_shape = (2, N_BLOCKS, GROUPS_PER_BLOCK, SSM_GROUP, SSM_STATE)
    return (lanes(a_re), lanes(a_im), lanes(jnp.broadcast_to(log_dt[..., None], a_re.shape)),
            expand(b_re.reshape(b_shape).transpose(0, 1, 2, 4, 3)), expand(b_im.reshape(b_shape).transpose(0, 1, 2, 4, 3)),
            expand(c_re.reshape(c_shape)), expand(c_im.reshape(c_shape)), d.reshape(N_BLOCKS, 1, BLOCK_CH))


def _ssm_discretise(ar, ai, ldt):
    dt = jnp.exp(ldt)
    mag = jnp.exp(dt * ar)
    abr = mag * jnp.cos(dt * ai)
    abi = mag * jnp.sin(dt * ai)
    num_re = abr - 1.0
    num_im = abi
    denom = ar * ar + ai * ai
    coef_re = (num_re * ar + num_im * ai) / denom
    coef_im = (num_im * ar - num_re * ai) / denom
    return abr, abi, coef_re, coef_im


_POW_ROWS = 24


def _ssm_fill_powers(ar_ref, ai_ref, ldt_ref, br_ref, bi_ref, pw_ref, bbar_ref):
    for d in range(2):
        abr, abi, cfr, cfi = _ssm_discretise(ar_ref[d, 0], ai_ref[d, 0], ldt_ref[d, 0])
        bbar_ref[d, 0] = cfr * br_ref[d, 0] - cfi * bi_ref[d, 0]
        bbar_ref[d, 1] = cfr * bi_ref[d, 0] + cfi * br_ref[d, 0]
        pr, pi = jnp.ones_like(abr), jnp.zeros_like(abi)
        for t in range(CHUNK + 1):
            pw_ref[d, 0, t:t + 1, :] = pr
            pw_ref[d, 1, t:t + 1, :] = pi
            pr, pi = pr * abr - pi * abi, pr * abi + pi * abr


def _dot_hi(a, b, dims=((1,), (0,))):
    return lax.dot_general(a, b, (dims, ((), ())), precision=HIGHEST, preferred_element_type=F32)


def _eye(n):
    return (lax.broadcasted_iota(jnp.int32, (n, n), 0) == lax.broadcasted_iota(jnp.int32, (n, n), 1)).astype(F32)


def _ssm_param_specs():
    vec = pl.BlockSpec((2, 1, 1, BLOCK_ST), lambda b, j: (0, b, 0, 0))
    mat = pl.BlockSpec((2, 1, BLOCK_CH, BLOCK_ST), lambda b, j: (0, b, 0, 0))
    return [vec, vec, vec, mat, mat, mat, mat, pl.BlockSpec((1, 1, BLOCK_CH), lambda b, j: (b, 0, 0))]


def _ssm_chunk_matrices(blk):
    def body(ar_ref, ai_ref, ldt_ref, br_ref, bi_ref, cr_ref, ci_ref, d_ref,
             m_ref, ws_ref, wot_ref, a16_ref, pw_ref, bbar_ref, lag_ref):
        j = pl.program_id(1)

        @pl.when(j == 0)
        def _():
            _ssm_fill_powers(ar_ref, ai_ref, ldt_ref, br_ref, bi_ref, pw_ref, bbar_ref)
            zero_lag = d_ref[0] * _eye(BLOCK_CH)
            for d in range(2):
                for t in range(CHUNK):
                    pr, pi = pw_ref[d, 0, t:t + 1, :], pw_ref[d, 1, t:t + 1, :]
                    xr = bbar_ref[d, 0] * pr - bbar_ref[d, 1] * pi
                    xi = bbar_ref[d, 0] * pi + bbar_ref[d, 1] * pr
                    tap = _dot_hi(xr, cr_ref[d, 0], ((1,), (1,))) - _dot_hi(xi, ci_ref[d, 0], ((1,), (1,)))
                    if t == 0:
                        zero_lag = zero_lag + tap
                    else:
                        lag_ref[CHUNK - 1 + t if d == 0 else CHUNK - 1 - t] = tap
            lag_ref[CHUNK - 1] = zero_lag
            a16_ref[0] = jnp.concatenate([pw_ref[d, ri, CHUNK:CHUNK + 1, :] for d in range(2) for ri in range(2)], axis=1)

        m_ref[0] = jnp.concatenate([lag_ref[jp - j + CHUNK - 1] for jp in range(CHUNK)], axis=1).astype(BF16)

        def power(d, t):
            return pw_ref[d, 0, pl.ds(t, 1), :], pw_ref[d, 1, pl.ds(t, 1), :]

        parts = []
        for d, t in ((0, CHUNK - 1 - j), (1, j)):
            pr, pi = power(d, t)
            parts += [bbar_ref[d, 0] * pr - bbar_ref[d, 1] * pi, bbar_ref[d, 0] * pi + bbar_ref[d, 1] * pr]
        ws_ref[0] = jnp.concatenate(parts, axis=1).astype(BF16)
        parts = []
        for d, t in ((0, j + 1), (1, CHUNK - j)):
            pr, pi = power(d, t)
            parts += [cr_ref[d, 0] * pr - ci_ref[d, 0] * pi, -cr_ref[d, 0] * pi - ci_ref[d, 0] * pr]
        wot_ref[0] = jnp.concatenate(parts, axis=1).astype(BF16)

    row = pl.BlockSpec((1, BLOCK_CH, CHUNK_W), lambda b, j: (b, j, 0))
    mat = jax.ShapeDtypeStruct((N_BLOCKS, CHUNK_W, CHUNK_W), BF16)
    return pl.pallas_call(
        body, name="ssm_chunk_matrices", grid=(N_BLOCKS, CHUNK),
        in_specs=_ssm_param_specs(),
        out_specs=[row, row, row, pl.BlockSpec((1, 1, STATE_W), lambda b, j: (b, 0, 0))],
        out_shape=[mat, mat, mat, jax.ShapeDtypeStruct((N_BLOCKS, 1, STATE_W), F32)],
        scratch_shapes=[pltpu.VMEM((2, 2, _POW_ROWS, BLOCK_ST), F32), pltpu.VMEM((2, 2, BLOCK_CH, BLOCK_ST), F32),
                        pltpu.VMEM((2 * CHUNK, BLOCK_CH, BLOCK_CH), F32)],
        compiler_params=_cparams(("arbitrary", "arbitrary")),
    )(*blk)


def _ssm_chunk_matrices_bwd(blk, d_m, d_ws, d_wot, d_a16):
    def body(ar_ref, ai_ref, ldt_ref, br_ref, bi_ref, cr_ref, ci_ref, d_ref, dm_ref, dws_ref, dwot_ref, da16_ref,
             dar_ref, dai_ref, dldt_ref, dbr_ref, dbi_ref, dcr_ref, dci_ref, dd_ref,
             pw_ref, bbar_ref, dlag_ref, dbbar_ref, dc_ref, dpw_ref):
        j = pl.program_id(1)
        w = BLOCK_ST

        @pl.when(j == 0)
        def _():
            _ssm_fill_powers(ar_ref, ai_ref, ldt_ref, br_ref, bi_ref, pw_ref, bbar_ref)
            for r in (dlag_ref, dbbar_ref, dc_ref, dpw_ref):
                r[...] = jnp.zeros_like(r)

        def x_chain(d, t, dxr, dxi):
            pr, pi = pw_ref[d, 0, pl.ds(t, 1), :], pw_ref[d, 1, pl.ds(t, 1), :]
            bbr, bbi = bbar_ref[d, 0], bbar_ref[d, 1]
            dbbar_ref[d, 0] += dxr * pr + dxi * pi
            dbbar_ref[d, 1] += dxi * pr - dxr * pi
            dpw_ref[d, 0, pl.ds(t, 1), :] += jnp.sum(dxr * bbr + dxi * bbi, axis=0, keepdims=True)
            dpw_ref[d, 1, pl.ds(t, 1), :] += jnp.sum(dxi * bbr - dxr * bbi, axis=0, keepdims=True)

        def z_chain(d, t, dzr, dzi):
            pr, pi = pw_ref[d, 0, pl.ds(t, 1), :], pw_ref[d, 1, pl.ds(t, 1), :]
            c_r, c_i = cr_ref[d, 0], ci_ref[d, 0]
            dc_ref[d, 0] += dzr * pr - dzi * pi
            dc_ref[d, 1] += -dzr * pi - dzi * pr
            dpw_ref[d, 0, pl.ds(t, 1), :] += jnp.sum(dzr * c_r - dzi * c_i, axis=0, keepdims=True)
            dpw_ref[d, 1, pl.ds(t, 1), :] += jnp.sum(-dzr * c_i - dzi * c_r, axis=0, keepdims=True)

        for jp in range(CHUNK):
            dlag_ref[jp - j + CHUNK - 1] += dm_ref[0, :, jp * BLOCK_CH:(jp + 1) * BLOCK_CH]
        x_chain(0, CHUNK - 1 - j, dws_ref[0, :, 0:w], dws_ref[0, :, w:2 * w])
        x_chain(1, j, dws_ref[0, :, 2 * w:3 * w], dws_ref[0, :, 3 * w:4 * w])
        z_chain(0, j + 1, dwot_ref[0, :, 0:w], dwot_ref[0, :, w:2 * w])
        z_chain(1, CHUNK - j, dwot_ref[0, :, 2 * w:3 * w], dwot_ref[0, :, 3 * w:4 * w])

        @pl.when(j == CHUNK - 1)
        def _():
            for d in range(2):
                for t in range(CHUNK):
                    d_tap = dlag_ref[CHUNK - 1 + t if d == 0 else CHUNK - 1 - t]
                    pr, pi = pw_ref[d, 0, t:t + 1, :], pw_ref[d, 1, t:t + 1, :]
                    xr = bbar_ref[d, 0] * pr - bbar_ref[d, 1] * pi
                    xi = bbar_ref[d, 0] * pi + bbar_ref[d, 1] * pr
                    dc_ref[d, 0] += _dot_hi(d_tap, xr, ((0,), (0,)))
                    dc_ref[d, 1] -= _dot_hi(d_tap, xi, ((0,), (0,)))
                    x_chain(d, t, _dot_hi(d_tap, cr_ref[d, 0]), -_dot_hi(d_tap, ci_ref[d, 0]))
            dd_ref[0] = jnp.sum(dlag_ref[CHUNK - 1] * _eye(BLOCK_CH), axis=0, keepdims=True)
            for d in range(2):
                (abr, abi, cfr, cfi), disc_vjp = jax.vjp(_ssm_discretise, ar_ref[d, 0], ai_ref[d, 0], ldt_ref[d, 0])
                dpr = dpw_ref[d, 0, CHUNK:CHUNK + 1, :] + da16_ref[0, :, 2 * d * w:(2 * d + 1) * w]
                dpi = dpw_ref[d, 1, CHUNK:CHUNK + 1, :] + da16_ref[0, :, (2 * d + 1) * w:(2 * d + 2) * w]
                dabr, dabi = jnp.zeros_like(abr), jnp.zeros_like(abi)
                for t in range(CHUNK, 0, -1):
                    qr, qi = pw_ref[d, 0, t - 1:t, :], pw_ref[d, 1, t - 1:t, :]
                    dabr = dabr + dpr * qr + dpi * qi
                    dabi = dabi + dpi * qr - dpr * qi
                    dpr, dpi = (dpr * abr + dpi * abi + dpw_ref[d, 0, t - 1:t, :],
                                dpi * abr - dpr * abi + dpw_ref[d, 1, t - 1:t, :])
                dbbr, dbbi = dbbar_ref[d, 0], dbbar_ref[d, 1]
                b_r, b_i = br_ref[d, 0], bi_ref[d, 0]
                dbr_ref[d, 0] = cfr * dbbr + cfi * dbbi
                dbi_ref[d, 0] = cfr * dbbi - cfi * dbbr
                dcfr = jnp.sum(b_r * dbbr + b_i * dbbi, axis=0, keepdims=True)
                dcfi = jnp.sum(b_r * dbbi - b_i * dbbr, axis=0, keepdims=True)
                dar_ref[d, 0], dai_ref[d, 0], dldt_ref[d, 0] = disc_vjp((dabr, dabi, dcfr, dcfi))
                dcr_ref[d, 0] = dc_ref[d, 0]
                dci_ref[d, 0] = dc_ref[d, 1]

    row = pl.BlockSpec((1, BLOCK_CH, CHUNK_W), lambda b, j: (b, j, 0))
    specs = _ssm_param_specs()
    acc = lambda *s: pltpu.VMEM(s, F32)
    return pl.pallas_call(
        body, name="ssm_chunk_matrices_bwd", grid=(N_BLOCKS, CHUNK),
        in_specs=specs + [row, row, row, pl.BlockSpec((1, 1, STATE_W), lambda b, j: (b, 0, 0))],
        out_specs=specs,
        out_shape=[jax.ShapeDtypeStruct(t.shape, F32) for t in blk],
        scratch_shapes=[acc(2, 2, _POW_ROWS, BLOCK_ST), acc(2, 2, BLOCK_CH, BLOCK_ST), acc(2 * CHUNK, BLOCK_CH, BLOCK_CH),
                        acc(2, 2, BLOCK_CH, BLOCK_ST), acc(2, 2, BLOCK_CH, BLOCK_ST), acc(2, 2, _POW_ROWS, BLOCK_ST)],
        compiler_params=_cparams(("arbitrary", "arbitrary")),
    )(*blk, d_m, d_ws, d_wot, d_a16)


def _to_chunks(t, dtype):
    L = t.shape[0]
    t = t.astype(dtype).reshape(L // CHUNK, CHUNK, N_BLOCKS, BLOCK_CH)
    return t.transpose(2, 0, 1, 3).reshape(N_BLOCKS, L // CHUNK, CHUNK_W)


def _from_chunks(t):
    nc = t.shape[1]
    t = t.reshape(N_BLOCKS, nc, CHUNK, BLOCK_CH).transpose(1, 2, 0, 3)
    return t.reshape(nc * CHUNK, D_SSM)


def _block_matmul(terms, name, out_dtype=F32, tn=1024):
    nc = terms[0][0].shape[1]
    n_out = terms[0][1].shape[1] if terms[0][2] else terms[0][1].shape[2]
    flags = [t[2] for t in terms]

    def body(*refs):
        out_ref = refs[-1]
        acc = None
        for t, transposed in enumerate(flags):
            a = refs[2 * t][0].astype(BF16)
            w = refs[2 * t + 1][0]
            part = _dot_nt(a, w) if transposed else _dot(a, w)
            acc = part if acc is None else acc + part
        out_ref[0] = acc.astype(out_dtype)

    in_specs, args = [], []
    for a, w, transposed in terms:
        k = a.shape[2]
        in_specs.append(pl.BlockSpec((1, nc, k), lambda b, n: (b, 0, 0)))
        if transposed:
            in_specs.append(pl.BlockSpec((1, tn, k), lambda b, n: (b, n, 0)))
        else:
            in_specs.append(pl.BlockSpec((1, k, tn), lambda b, n: (b, 0, n)))
        args += [a, w]
    return pl.pallas_call(
        body, name=name, grid=(N_BLOCKS, n_out // tn), in_specs=in_specs,
        out_specs=pl.BlockSpec((1, nc, tn), lambda b, n: (b, 0, n)),
        out_shape=jax.ShapeDtypeStruct((N_BLOCKS, nc, n_out), out_dtype),
        compiler_params=_cparams(("arbitrary", "arbitrary")),
    )(*args)


def _block_matmul_tn(a, b, name, tile=1024):
    nc, m = a.shape[1], a.shape[2]
    n = b.shape[2]

    def body(a_ref, b_ref, out_ref):
        out_ref[0] = _dot_tn(a_ref[0].astype(BF16), b_ref[0].astype(BF16))

    return pl.pallas_call(
        body, name=name, grid=(N_BLOCKS, m // tile, n // tile),
        in_specs=[pl.BlockSpec((1, nc, tile), lambda blk, i, j: (blk, 0, i)),
                  pl.BlockSpec((1, nc, tile), lambda blk, i, j: (blk, 0, j))],
        out_specs=pl.BlockSpec((1, tile, tile), lambda blk, i, j: (blk, i, j)),
        out_shape=jax.ShapeDtypeStruct((N_BLOCKS, m, n), F32),
        compiler_params=_cparams(("arbitrary", "arbitrary", "arbitrary")),
    )(a, b)


def _cmul(ar, ai, xr, xi):
    return ar * xr - ai * xi, ar * xi + ai * xr


def _cmul_conj(ar, ai, xr, xi):
    return ar * xr + ai * xi, ar * xi - ai * xr


def _ssm_state_scan(s_in, a16):
    nc = s_in.shape[1]
    w = BLOCK_ST

    def body(sin_ref, a_ref, out_ref):
        a = a_ref[0]
        afr, afi, abr, abi = a[:, 0:w], a[:, w:2 * w], a[:, 2 * w:3 * w], a[:, 3 * w:4 * w]

        def step(c, carry):
            fr, fi, br, bi = carry
            cb = nc - 1 - c
            out_ref[0, pl.ds(c, 1), 0:w] = fr
            out_ref[0, pl.ds(c, 1), w:2 * w] = fi
            out_ref[0, pl.ds(cb, 1), 2 * w:3 * w] = br
            out_ref[0, pl.ds(cb, 1), 3 * w:4 * w] = bi
            nfr, nfi = _cmul(afr, afi, fr, fi)
            nbr, nbi = _cmul(abr, abi, br, bi)
            return (nfr + sin_ref[0, pl.ds(c, 1), 0:w], nfi + sin_ref[0, pl.ds(c, 1), w:2 * w],
                    nbr + sin_ref[0, pl.ds(cb, 1), 2 * w:3 * w], nbi + sin_ref[0, pl.ds(cb, 1), 3 * w:4 * w])

        z = jnp.zeros((1, w), F32)
        lax.fori_loop(0, nc, step, (z, z, z, z))

    spec = pl.BlockSpec((1, nc, STATE_W), lambda b: (b, 0, 0))
    return pl.pallas_call(
        body, name="ssm_state_scan", grid=(N_BLOCKS,),
        in_specs=[spec, pl.BlockSpec((1, 1, STATE_W), lambda b: (b, 0, 0))],
        out_specs=spec, out_shape=jax.ShapeDtypeStruct(s_in.shape, F32),
        compiler_params=_cparams(("arbitrary",)),
    )(s_in, a16)


def _ssm_state_scan_bwd(d_prev, s_prev, a16):
    nc = d_prev.shape[1]
    w = BLOCK_ST

    def body(dp_ref, sp_ref, a_ref, g_ref, da_ref):
        a = a_ref[0]
        afr, afi, abr, abi = a[:, 0:w], a[:, w:2 * w], a[:, 2 * w:3 * w], a[:, 3 * w:4 * w]

        def step(i, carry):
            gfr, gfi, gbr, gbi, dafr, dafi, dabr, dabi = carry
            cf = nc - 1 - i
            cb = i
            g_ref[0, pl.ds(cf, 1), 0:w] = gfr
            g_ref[0, pl.ds(cf, 1), w:2 * w] = gfi
            g_ref[0, pl.ds(cb, 1), 2 * w:3 * w] = gbr
            g_ref[0, pl.ds(cb, 1), 3 * w:4 * w] = gbi
            sfr, sfi = sp_ref[0, pl.ds(cf, 1), 0:w], sp_ref[0, pl.ds(cf, 1), w:2 * w]
            sbr, sbi = sp_ref[0, pl.ds(cb, 1), 2 * w:3 * w], sp_ref[0, pl.ds(cb, 1), 3 * w:4 * w]
            dafr = dafr + gfr * sfr + gfi * sfi
            dafi = dafi + gfi * sfr - gfr * sfi
            dabr = dabr + gbr * sbr + gbi * sbi
            dabi = dabi + gbi * sbr - gbr * sbi
            nfr, nfi = _cmul_conj(afr, afi, gfr, gfi)
            nbr, nbi = _cmul_conj(abr, abi, gbr, gbi)
            return (nfr + dp_ref[0, pl.ds(cf, 1), 0:w], nfi + dp_ref[0, pl.ds(cf, 1), w:2 * w],
                    nbr + dp_ref[0, pl.ds(cb, 1), 2 * w:3 * w], nbi + dp_ref[0, pl.ds(cb, 1), 3 * w:4 * w],
                    dafr, dafi, dabr, dabi)

        z = jnp.zeros((1, w), F32)
        res = lax.fori_loop(0, nc, step, (z,) * 8)
        da_ref[0] = jnp.concatenate(res[4:], axis=1)

    spec = pl.BlockSpec((1, nc, STATE_W), lambda b: (b, 0, 0))
    aspec = pl.BlockSpec((1, 1, STATE_W), lambda b: (b, 0, 0))
    return pl.pallas_call(
        body, name="ssm_state_scan_bwd", grid=(N_BLOCKS,),
        in_specs=[spec, spec, aspec], out_specs=[spec, aspec],
        out_shape=[jax.ShapeDtypeStruct(d_prev.shape, F32), jax.ShapeDtypeStruct((N_BLOCKS, 1, STATE_W), F32)],
        compiler_params=_cparams(("arbitrary",)),
    )(d_prev, s_prev, a16)


NA_ROW_BLOCK = 8
NA_KEYS = NA_ROWS * GRID_W


def _na_bias_table(rpb):
    qc = np.arange(GRID_W)[:, None]
    kc = np.arange(GRID_W)[None, :]
    dc = np.clip(kc - qc + NA_COLS - 1, 0, 2 * NA_COLS - 2)
    onehot = jnp.asarray((np.arange(2 * NA_COLS - 1)[:, None, None] == dc[None]).astype(np.float32))
    per_row = jnp.einsum("hrd,dqk->hrqk", rpb, onehot, precision=HIGHEST)
    tab = jnp.stack([per_row[:, o:o + NA_ROWS] for o in range(NA_ROWS)], axis=1)
    return tab.transpose(0, 1, 3, 2, 4).reshape(NA_HEADS, NA_ROWS, GRID_W, NA_KEYS)


def _na_col_mask():
    qc = lax.broadcasted_iota(jnp.int32, (GRID_W, NA_KEYS), 0)
    kc = lax.broadcasted_iota(jnp.int32, (GRID_W, NA_KEYS), 1) % GRID_W
    cs = jnp.clip(qc - NA_COLS // 2, 0, GRID_W - NA_COLS)
    return (kc >= cs) & (kc < cs + NA_COLS)


def _na_probs(qh, kh, bias, mask):
    s = _dot_nt(qh, kh) + bias
    s = jnp.where(mask, s, jnp.finfo(F32).min)
    s = s - jnp.max(s, axis=-1, keepdims=True)
    e = jnp.exp(s)
    return e / jnp.sum(e, axis=-1, keepdims=True)


def _na_fwd(q, k, v, bias_tab):
    L = q.shape[0]
    rows = L // GRID_W
    rb = NA_ROW_BLOCK

    def body(q_ref, k_ref, v_ref, bt_ref, o_ref):
        mask = _na_col_mask()
        r_base = pl.program_id(1) * rb
        for rr in range(rb):
            r = r_base + rr
            rs = jnp.clip(r - NA_ROWS // 2, 0, rows - NA_ROWS)
            dr0 = rs - r + NA_ROWS - 1
            start = pl.multiple_of(rs * GRID_W, GRID_W)
            kb = k_ref[pl.ds(start, NA_KEYS), :]
            vb = v_ref[pl.ds(start, NA_KEYS), :]
            qb = q_ref[rr * GRID_W:(rr + 1) * GRID_W, :]
            outs = []
            for hh in range(2):
                sl = slice(hh * NA_HEAD_DIM, (hh + 1) * NA_HEAD_DIM)
                p = _na_probs(qb[:, sl], kb[:, sl], bt_ref[hh, dr0], mask)
                outs.append(_dot(p.astype(BF16), vb[:, sl]))
            o_ref[rr * GRID_W:(rr + 1) * GRID_W, :] = jnp.concatenate(outs, axis=1)

    return pl.pallas_call(
        body, name="na_fwd", grid=(NA_HEADS // 2, rows // rb),
        in_specs=[pl.BlockSpec((rb * GRID_W, 128), lambda h, r: (r, h)),
                  pl.BlockSpec((L, 128), lambda h, r: (0, h)),
                  pl.BlockSpec((L, 128), lambda h, r: (0, h)),
                  pl.BlockSpec((2, NA_ROWS, GRID_W, NA_KEYS), lambda h, r: (h, 0, 0, 0))],
        out_specs=pl.BlockSpec((rb * GRID_W, 128), lambda h, r: (r, h)),
        out_shape=jax.ShapeDtypeStruct((L, D_NA), F32),
        compiler_params=_cparams(("arbitrary", "arbitrary")),
    )(q, k, v, bias_tab)


def _na_bwd(q, k, v, bias_tab, out, d_out):
    L = q.shape[0]
    rows = L // GRID_W
    rb = NA_ROW_BLOCK

    def body(q_ref, k_ref, v_ref, bt_ref, o_ref, do_ref, dq_ref, dk_ref, dv_ref, dbt_ref):
        @pl.when(pl.program_id(1) == 0)
        def _():
            dk_ref[...] = jnp.zeros_like(dk_ref)
            dv_ref[...] = jnp.zeros_like(dv_ref)
            dbt_ref[...] = jnp.zeros_like(dbt_ref)

        mask = _na_col_mask()
        r_base = pl.program_id(1) * rb
        for rr in range(rb):
            r = r_base + rr
            rs = jnp.clip(r - NA_ROWS // 2, 0, rows - NA_ROWS)
            dr0 = rs - r + NA_ROWS - 1
            start = pl.multiple_of(rs * GRID_W, GRID_W)
            kb = k_ref[pl.ds(start, NA_KEYS), :]
            vb = v_ref[pl.ds(start, NA_KEYS), :]
            rsl = slice(rr * GRID_W, (rr + 1) * GRID_W)
            qb, ob, dob = q_ref[rsl, :], o_ref[rsl, :], do_ref[rsl, :]
            dqs, dks, dvs = [], [], []
            for hh in range(2):
                sl = slice(hh * NA_HEAD_DIM, (hh + 1) * NA_HEAD_DIM)
                p = _na_probs(qb[:, sl], kb[:, sl], bt_ref[hh, dr0], mask)
                doh = dob[:, sl]
                dp = _dot_nt(doh.astype(BF16), vb[:, sl])
                delta = jnp.sum(doh * ob[:, sl], axis=-1, keepdims=True)
                ds = p * (dp - delta)
                dbt_ref[hh, dr0] += ds
                dsb = ds.astype(BF16)
                dqs.append(_dot(dsb, kb[:, sl]) * (NA_HEAD_DIM ** -0.5))
                dks.append(_dot_tn(dsb, qb[:, sl]))
                dvs.append(_dot_tn(p.astype(BF16), doh.astype(BF16)))
            dq_ref[rsl, :] = jnp.concatenate(dqs, axis=1)
            dk_ref[pl.ds(start, NA_KEYS), :] += jnp.concatenate(dks, axis=1)
            dv_ref[pl.ds(start, NA_KEYS), :] += jnp.concatenate(dvs, axis=1)

    tile = pl.BlockSpec((rb * GRID_W, 128), lambda h, r: (r, h))
    full = pl.BlockSpec((L, 128), lambda h, r: (0, h))
    bt = pl.BlockSpec((2, NA_ROWS, GRID_W, NA_KEYS), lambda h, r: (h, 0, 0, 0))
    tok = jax.ShapeDtypeStruct((L, D_NA), F32)
    return pl.pallas_call(
        body, name="na_bwd", grid=(NA_HEADS // 2, rows // rb),
        in_specs=[tile, full, full, bt, tile, tile],
        out_specs=[tile, full, full, bt],
        out_shape=[tok, tok, tok, jax.ShapeDtypeStruct(bias_tab.shape, F32)],
        compiler_params=_cparams(("arbitrary", "arbitrary")),
    )(q, k, v, bias_tab, out, d_out)


def _branch_fwd_values(ys, zs, yn, zn, wglu, bglu):
    g1, t = _gelu_parts(ys)
    lin = _dot(g1.astype(BF16), wglu) + bglu
    sg = _sigmoid(lin)
    ys2 = g1 * sg
    sz, szs = _silu_parts(zs)
    sn, sns = _silu_parts(zn)
    return g1, t, sg, ys2, sz, szs, sn, sns


def _branch_fwd(y_ssm, z_s, y_na, z_n, w_glu, b_glu, tm=512):
    L = y_ssm.shape[0]

    def body(ys_ref, zs_ref, yn_ref, zn_ref, w_ref, b_ref, cat_ref):
        g1, t, sg, ys2, sz, szs, sn, sns = _branch_fwd_values(
            ys_ref[...], zs_ref[...], yn_ref[...], zn_ref[...], w_ref[...], b_ref[...])
        cat_ref[:, 0:512] = (ys2 * sz).astype(BF16)
        cat_ref[:, 512:1024] = (yn_ref[...] * sn).astype(BF16)

    tile = pl.BlockSpec((tm, 512), lambda i: (i, 0))
    return pl.pallas_call(
        body, name="branch_fwd", grid=(L // tm,),
        in_specs=[tile, tile, tile, tile, pl.BlockSpec((512, 512), lambda i: (0, 0)),
                  pl.BlockSpec((1, 512), lambda i: (0, 0))],
        out_specs=pl.BlockSpec((tm, 1024), lambda i: (i, 0)),
        out_shape=jax.ShapeDtypeStruct((L, 1024), BF16),
        compiler_params=_cparams(("arbitrary",)),
    )(y_ssm, z_s, y_na, z_n, w_glu, b_glu)


def _branch_bwd(y_ssm, z_s, y_na, z_n, w_glu, b_glu, d_cat, tm=512):
    L = y_ssm.shape[0]

    def body(ys_ref, zs_ref, yn_ref, zn_ref, w_ref, b_ref, dc_ref, dys_ref, dzs_ref, dyn_ref, dzn_ref, dw_ref, db_ref):
        @pl.when(pl.program_id(0) == 0)
        def _():
            dw_ref[...] = jnp.zeros_like(dw_ref)
            db_ref[...] = jnp.zeros_like(db_ref)

        ys, zs, yn, zn = ys_ref[...], zs_ref[...], yn_ref[...], zn_ref[...]
        w = w_ref[...]
        g1, t, sg, ys2, sz, szs, sn, sns = _branch_fwd_values(ys, zs, yn, zn, w, b_ref[...])
        dys3 = dc_ref[:, 0:512]
        dyn2 = dc_ref[:, 512:1024]
        dzs_ref[...] = (dys3 * ys2 * _silu_grad(zs, szs)).astype(BF16)
        dys2 = dys3 * sz
        dlin = dys2 * g1 * sg * (1.0 - sg)
        dlb = dlin.astype(BF16)
        db_ref[...] += jnp.sum(dlin, axis=0, keepdims=True)
        dw_ref[...] += _dot_tn(g1.astype(BF16), dlb)
        dg1 = dys2 * sg + _dot_nt(dlb, w)
        dys_ref[...] = (dg1 * _gelu_grad(ys, t)).astype(BF16)
        dyn_ref[...] = dyn2 * sn
        dzn_ref[...] = (dyn2 * yn * _silu_grad(zn, sns)).astype(BF16)

    tile = pl.BlockSpec((tm, 512), lambda i: (i, 0))
    wspec = pl.BlockSpec((512, 512), lambda i: (0, 0))
    bspec = pl.BlockSpec((1, 512), lambda i: (0, 0))
    tok = lambda dt: jax.ShapeDtypeStruct((L, 512), dt)
    return pl.pallas_call(
        body, name="branch_bwd", grid=(L // tm,),
        in_specs=[tile, tile, tile, tile, wspec, bspec, pl.BlockSpec((tm, 1024), lambda i: (i, 0))],
        out_specs=[tile, tile, tile, tile, wspec, bspec],
        out_shape=[tok(BF16), tok(BF16), tok(F32), tok(BF16),
                   jax.ShapeDtypeStruct((512, 512), F32), jax.ShapeDtypeStruct((1, 512), F32)],
        compiler_params=_cparams(("arbitrary",)),
    )(y_ssm, z_s, y_na, z_n, w_glu, b_glu, d_cat)


def _head(x, p, target, cat, w_out, g_post, w_ple_g, g_ple, w_pg, tm=256):
    L = x.shape[0]
    pw = w_ple_g.shape[2]

    def body(x_ref, p_ref, t_ref, cat_ref, wo_ref, gpo_ref, wp_ref, gpl_ref, wg_ref,
             loss_ref, dh1_ref, dcat_ref, dwo_ref, dgpo_ref, dwp_ref, dgpl_ref, dwg_ref):
        @pl.when(pl.program_id(0) == 0)
        def _():
            for r in (loss_ref, dwo_ref, dgpo_ref, dwp_ref, dgpl_ref, dwg_ref):
                r[...] = jnp.zeros_like(r)

        cat_b = cat_ref[...]
        wo, wg = wo_ref[...], wg_ref[...]
        g_po, g_pl = gpo_ref[...], gpl_ref[...]
        mix = _dot(cat_b, wo)
        nm, r2 = _rms(mix)
        h1 = x_ref[...] + nm * g_po
        p_b = p_ref[...].astype(BF16)
        ep = jnp.concatenate([_dot(p_b, wp_ref[j]) for j in range(N_CHIPS)], axis=1)
        ne, r3 = _rms(ep)
        e = ne * g_pl
        h1_b = h1.astype(BF16)
        gate = _sigmoid(_dot(h1_b, wg))
        h2 = h1 + gate * e
        diff = h2 - t_ref[...]
        loss_ref[...] += (0.5 / D_MODEL) * jnp.sum(diff * diff).reshape(1, 1)

        dh2 = diff * (1.0 / D_MODEL)
        de = dh2 * gate
        dgl = (dh2 * e * gate * (1.0 - gate)).astype(BF16)
        dwg_ref[...] += _dot_tn(h1_b, dgl)
        dh1 = dh2 + _dot_nt(dgl, wg)
        dgpl_ref[...] += jnp.sum(de * ne, axis=0, keepdims=True)
        dep = _rms_bwd(de * g_pl, ne, r3).astype(BF16)
        for j in range(N_CHIPS):
            dwp_ref[j] += _dot_tn(p_b, dep[:, j * pw:(j + 1) * pw])
        dgpo_ref[...] += jnp.sum(dh1 * nm, axis=0, keepdims=True)
        dmix = _rms_bwd(dh1 * g_po, nm, r2).astype(BF16)
        dwo_ref[...] += _dot_tn(cat_b, dmix)
        dcat_ref[...] = _dot_nt(dmix, wo)
        dh1_ref[...] = dh1

    tile = lambda w: pl.BlockSpec((tm, w), lambda i: (i, 0))
    const = lambda *s: pl.BlockSpec(s, lambda i: (0,) * len(s))
    sds = jax.ShapeDtypeStruct
    return pl.pallas_call(
        body, name="head", grid=(L // tm,),
        in_specs=[tile(D_MODEL), tile(D_PLE), tile(D_MODEL), tile(1024), const(1024, D_MODEL), const(1, D_MODEL),
                  const(N_CHIPS, D_PLE, pw), const(1, D_MODEL), const(D_MODEL, D_MODEL)],
        out_specs=[const(1, 1), tile(D_MODEL), tile(1024), const(1024, D_MODEL), const(1, D_MODEL),
                   const(N_CHIPS, D_PLE, pw), const(1, D_MODEL), const(D_MODEL, D_MODEL)],
        out_shape=[sds((1, 1), F32), sds((L, D_MODEL), F32), sds((L, 1024), F32), sds((1024, D_MODEL), F32),
                   sds((1, D_MODEL), F32), sds((N_CHIPS, D_PLE, pw), F32), sds((1, D_MODEL), F32),
                   sds((D_MODEL, D_MODEL), F32)],
        compiler_params=_cparams(("arbitrary",)),
    )(x, p, target, cat, w_out, g_post, w_ple_g, g_ple, w_pg)


_DPROJ_DTYPES = (BF16, BF16, F32, F32, F32, BF16)


def _dproj_tile(refs):
    return jnp.concatenate([r[...].astype(BF16) for r in refs], axis=1)


def _in_proj_bwd_w(x, g_pre, dparts, tm=512):
    L = x.shape[0]
    wn = D_IN_PROJ // N_CHIPS

    def body(x_ref, g_ref, *refs):
        dw_ref = refs[-1]

        @pl.when(pl.program_id(0) == 0)
        def _():
            dw_ref[...] = jnp.zeros_like(dw_ref)

        n, _ = _rms(x_ref[...])
        hn = (n * g_ref[...]).astype(BF16)
        dproj = _dproj_tile(refs[:-1])
        for j in range(N_CHIPS):
            dw_ref[j] += _dot_tn(hn, dproj[:, j * wn:(j + 1) * wn])

    tile = pl.BlockSpec((tm, 512), lambda i: (i, 0))
    return pl.pallas_call(
        body, name="in_proj_bwd_w", grid=(L // tm,),
        in_specs=[pl.BlockSpec((tm, D_MODEL), lambda i: (i, 0)), pl.BlockSpec((1, D_MODEL), lambda i: (0, 0))] + [tile] * 6,
        out_specs=pl.BlockSpec((N_CHIPS, D_MODEL, wn), lambda i: (0, 0, 0)),
        out_shape=jax.ShapeDtypeStruct((N_CHIPS, D_MODEL, wn), F32),
        compiler_params=_cparams(("arbitrary",)),
    )(x, g_pre, *dparts)


def _in_proj_bwd_x(x, g_pre, w_in_g, d_h1, dparts, tm=512):
    L = x.shape[0]
    wn = w_in_g.shape[2]

    def body(x_ref, g_ref, w_ref, dh1_ref, *refs):
        dx_ref, dg_ref = refs[-2], refs[-1]

        @pl.when(pl.program_id(0) == 0)
        def _():
            dg_ref[...] = jnp.zeros_like(dg_ref)

        n, r = _rms(x_ref[...])
        dproj = _dproj_tile(refs[:-2])
        dhn = _dot_nt(dproj[:, 0:wn], w_ref[0])
        for j in range(1, N_CHIPS):
            dhn = dhn + _dot_nt(dproj[:, j * wn:(j + 1) * wn], w_ref[j])
        dg_ref[...] += jnp.sum(dhn * n, axis=0, keepdims=True)
        dx_ref[...] = dh1_ref[...] + _rms_bwd(dhn * g_ref[...], n, r)

    tile = pl.BlockSpec((tm, 512), lambda i: (i, 0))
    wide = pl.BlockSpec((tm, D_MODEL), lambda i: (i, 0))
    vec = pl.BlockSpec((1, D_MODEL), lambda i: (0, 0))
    return pl.pallas_call(
        body, name="in_proj_bwd_x", grid=(L // tm,),
        in_specs=[wide, vec, pl.BlockSpec((N_CHIPS, D_MODEL, wn), lambda i: (0, 0, 0)), wide] + [tile] * 6,
        out_specs=[wide, vec],
        out_shape=[jax.ShapeDtypeStruct((L, D_MODEL), F32), jax.ShapeDtypeStruct((1, D_MODEL), F32)],
        compiler_params=_cparams(("arbitrary",)),
    )(x, g_pre, w_in_g, d_h1, *dparts)


def _mesh_position():
    x, y, c = lax.axis_index("x"), lax.axis_index("y"), lax.axis_index("c")
    chips = [(1 - x, y), (x, 1 - y), (1 - x, 1 - y)]
    return x, y, c, chips


def _chip_index(cx, cy):
    return 2 * cx + cy


def _hbm_specs(n):
    return [pl.BlockSpec(memory_space=pl.ANY)] * n


def _gather_chips(shards, name):
    n = len(shards)

    def body(*refs):
        ins, outs = refs[:n], refs[n:2 * n]
        send1, recv1, send2, recv2, local = refs[2 * n:]
        x, y, c, chips = _mesh_position()
        me = _chip_index(x, y)
        sibling = (x, y, 1 - c)

        def half(ref, chip, core):
            hr = ref.shape[1] // 2
            return ref.at[chip, pl.ds(core * hr, hr)]

        copies, locals_ = [], []
        for a in range(n):
            lc = pltpu.make_async_copy(ins[a], outs[a].at[me], local.at[a])
            lc.start()
            locals_.append(lc)
            hr = ins[a].shape[0] // 2
            for j, chip in enumerate(chips):
                cp = pltpu.make_async_remote_copy(
                    src_ref=ins[a].at[pl.ds(c * hr, hr)], dst_ref=half(outs[a], me, c),
                    send_sem=send1.at[a, j], recv_sem=recv1.at[a, j], device_id=(*chip, c), device_id_type=MESH)
                cp.start()
                copies.append(cp)
        for a in range(n):
            for j, chip in enumerate(chips):
                landed = half(outs[a], _chip_index(*chip), c)
                pltpu.make_async_remote_copy(
                    src_ref=landed, dst_ref=landed, send_sem=send1.at[a, j], recv_sem=recv1.at[a, j],
                    device_id=(*chip, c), device_id_type=MESH).wait_recv()
                cp = pltpu.make_async_remote_copy(
                    src_ref=landed, dst_ref=landed, send_sem=send2.at[a, j], recv_sem=recv2.at[a, j],
                    device_id=sibling, device_id_type=MESH)
                cp.start()
                copies.append(cp)
        for a in range(n):
            for j, chip in enumerate(chips):
                other = half(outs[a], _chip_index(*chip), 1 - c)
                pltpu.make_async_remote_copy(
                    src_ref=other, dst_ref=other, send_sem=send2.at[a, j], recv_sem=recv2.at[a, j],
                    device_id=sibling, device_id_type=MESH).wait_recv()
        for cp in copies:
            cp.wait_send()
        for lc in locals_:
            lc.wait()

    sem = pltpu.SemaphoreType.DMA
    return pl.pallas_call(
        body, name=name, in_specs=_hbm_specs(n), out_specs=_hbm_specs(n),
        out_shape=[jax.ShapeDtypeStruct((N_CHIPS,) + s.shape, s.dtype) for s in shards],
        scratch_shapes=[sem((n, 3)), sem((n, 3)), sem((n, 3)), sem((n, 3)), sem((n,))],
        compiler_params=pltpu.CompilerParams(has_side_effects=True),
    )(*shards)


def _pair_exchange(grads):
    n = len(grads)

    def body(*refs):
        ins, outs = refs[:n], refs[n:2 * n]
        send, recv = refs[2 * n:]
        x, y, c, _ = _mesh_position()
        copies = []
        for a in range(n):
            hr = ins[a].shape[1] // 2
            cp = pltpu.make_async_remote_copy(
                src_ref=ins[a].at[:, pl.ds((1 - c) * hr, hr)], dst_ref=outs[a],
                send_sem=send.at[a], recv_sem=recv.at[a], device_id=(x, y, 1 - c), device_id_type=MESH)
            cp.start()
            copies.append(cp)
        for cp in copies:
            cp.wait()

    sem = pltpu.SemaphoreType.DMA
    return pl.pallas_call(
        body, name="pair_exchange", in_specs=_hbm_specs(n), out_specs=_hbm_specs(n),
        out_shape=[jax.ShapeDtypeStruct((g.shape[0], g.shape[1] // 2, g.shape[2]), g.dtype) for g in grads],
        scratch_shapes=[sem((n,)), sem((n,))],
        compiler_params=pltpu.CompilerParams(has_side_effects=True),
    )(*grads)


def _pair_add(core, grad, other, tr):
    hr = other.shape[1]
    cdim = other.shape[2]
    nb = hr // tr

    def body(core_ref, g_ref, o_ref, out_ref):
        out_ref[...] = g_ref[...] + o_ref[...]

    return pl.pallas_call(
        body, name="pair_add",
        grid_spec=pltpu.PrefetchScalarGridSpec(
            num_scalar_prefetch=1, grid=(N_CHIPS, nb),
            in_specs=[pl.BlockSpec((1, tr, cdim), lambda j, i, core_ref: (j, core_ref[0] * nb + i, 0)),
                      pl.BlockSpec((1, tr, cdim), lambda j, i, core_ref: (j, i, 0))],
            out_specs=pl.BlockSpec((1, tr, cdim), lambda j, i, core_ref: (j, i, 0))),
        out_shape=jax.ShapeDtypeStruct(other.shape, F32),
        compiler_params=_cparams(("arbitrary", "arbitrary")),
    )(core, grad, other)


def _chip_scatter(parts):
    n = len(parts)

    def body(*refs):
        ins, outs = refs[:n], refs[n:2 * n]
        send, recv, local = refs[2 * n:]
        x, y, c, chips = _mesh_position()
        me = _chip_index(x, y)
        copies = []
        for a in range(n):
            lc = pltpu.make_async_copy(ins[a].at[me], outs[a].at[me], local.at[a])
            lc.start()
            copies.append(lc)
            for j, chip in enumerate(chips):
                cp = pltpu.make_async_remote_copy(
                    src_ref=ins[a].at[_chip_index(*chip)], dst_ref=outs[a].at[me],
                    send_sem=send.at[a, j], recv_sem=recv.at[a, j], device_id=(*chip, c), device_id_type=MESH)
                cp.start()
                copies.append(cp)
        for cp in copies:
            cp.wait()

    sem = pltpu.SemaphoreType.DMA
    return pl.pallas_call(
        body, name="chip_scatter", in_specs=_hbm_specs(n), out_specs=_hbm_specs(n),
        out_shape=[jax.ShapeDtypeStruct(p.shape, p.dtype) for p in parts],
        scratch_shapes=[sem((n, 3)), sem((n, 3)), sem((n,))],
        compiler_params=pltpu.CompilerParams(has_side_effects=True),
    )(*parts)


def _chip_add(recv, tr):
    hr, cdim = recv.shape[1], recv.shape[2]

    def body(r_ref, out_ref):
        out_ref[...] = ((r_ref[0] + r_ref[1]) + r_ref[2]) + r_ref[3]

    return pl.pallas_call(
        body, name="chip_add", grid=(hr // tr,),
        in_specs=[pl.BlockSpec((N_CHIPS, tr, cdim), lambda i: (0, i, 0))],
        out_specs=pl.BlockSpec((tr, cdim), lambda i: (i, 0)),
        out_shape=jax.ShapeDtypeStruct((hr, cdim), F32),
        compiler_params=_cparams(("arbitrary",)),
    )(recv)


def _pair_gather(halves):
    n = len(halves)

    def body(*refs):
        ins, outs = refs[:n], refs[n:2 * n]
        send, recv, local = refs[2 * n:]
        x, y, c, _ = _mesh_position()
        copies = []
        for a in range(n):
            hr = ins[a].shape[0]
            mine = outs[a].at[pl.ds(c * hr, hr)]
            lc = pltpu.make_async_copy(ins[a], mine, local.at[a])
            lc.start()
            copies.append(lc)
            cp = pltpu.make_async_remote_copy(
                src_ref=ins[a], dst_ref=mine, send_sem=send.at[a], recv_sem=recv.at[a],
                device_id=(x, y, 1 - c), device_id_type=MESH)
            cp.start()
            copies.append(cp)
        for cp in copies:
            cp.wait()

    sem = pltpu.SemaphoreType.DMA
    return pl.pallas_call(
        body, name="pair_gather", in_specs=_hbm_specs(n), out_specs=_hbm_specs(n),
        out_shape=[jax.ShapeDtypeStruct((2 * h.shape[0], h.shape[1]), h.dtype) for h in halves],
        scratch_shapes=[sem((n,)), sem((n,)), sem((n,))],
        compiler_params=pltpu.CompilerParams(has_side_effects=True),
    )(*halves)


def _row_tile(rows):
    for t in (512, 256, 128, 64, 32, 16, 8):
        if rows % t == 0:
            return t
    raise ValueError(rows)


def _reduce_scatter(grads):
    core = lax.axis_index("c").astype(jnp.int32).reshape(1)
    others = _pair_exchange(grads)
    pair = [_pair_add(core, g, o, _row_tile(o.shape[1])) for g, o in zip(grads, others)]
    landed = _chip_scatter(pair)
    halves = [_chip_add(r, _row_tile(r.shape[1])) for r in landed]
    return _pair_gather(halves)


def _adamw(w, g, m, v):
    rows, cols = w.shape
    tr = _row_tile(rows) if rows % 8 == 0 else rows

    def body(w_ref, g_ref, m_ref, v_ref, d_ref, nm_ref, nv_ref):
        g_ = g_ref[...]
        m_ = ADAM_B1 * m_ref[...] + (1.0 - ADAM_B1) * g_
        v_ = ADAM_B2 * v_ref[...] + (1.0 - ADAM_B2) * (g_ * g_)
        m_hat = m_ / (1.0 - ADAM_B1 ** ADAM_STEP)
        v_hat = v_ / (1.0 - ADAM_B2 ** ADAM_STEP)
        d_ref[...] = -ADAM_LR * (m_hat / (jnp.sqrt(v_hat) + ADAM_EPS) + ADAM_WD * w_ref[...])
        nm_ref[...] = m_
        nv_ref[...] = v_

    spec = pl.BlockSpec((tr, cols), lambda i: (i, 0))
    shp = jax.ShapeDtypeStruct((rows, cols), F32)
    return pl.pallas_call(
        body, name="adamw", grid=(rows // tr,), in_specs=[spec] * 4, out_specs=[spec] * 3,
        out_shape=[shp] * 3, compiler_params=_cparams(("arbitrary",)),
    )(w, g, m, v)


_SMALL = ["norm_pre", "norm_post", "ssm_a_re", "ssm_a_im", "ssm_log_dt", "ssm_b_re", "ssm_b_im",
          "ssm_c_re", "ssm_c_im", "ssm_d", "b_glu", "na_rpb", "ple_norm"]
_BIG = ["w_in", "w_glu", "w_out", "w_ple", "w_ple_gate"]
_WEIGHTS = ["norm_pre", "norm_post", "w_in", "ssm_a_re", "ssm_a_im", "ssm_log_dt", "ssm_b_re", "ssm_b_im",
            "ssm_c_re", "ssm_c_im", "ssm_d", "w_glu", "b_glu", "na_rpb", "w_out", "w_ple", "ple_norm", "w_ple_gate"]
_SMALL_ROWS = 2176


def _pack_small(tensors):
    flat = jnp.concatenate([tensors[n].reshape(-1) for n in _SMALL])
    flat = jnp.pad(flat, (0, _SMALL_ROWS * 128 - flat.shape[0]))
    return flat.reshape(_SMALL_ROWS, 128)


def _unpack_small(packed, shapes):
    flat = packed.reshape(-1)
    out, off = {}, 0
    for n in _SMALL:
        size = int(np.prod(shapes[n]))
        out[n] = flat[off:off + size].reshape(shapes[n])
        off += size
    return out


def _local_grads(x, p, target, wts, w_in_g, w_glu, w_out, w_ple_g, w_pg):
    ssm_names = ["ssm_a_re", "ssm_a_im", "ssm_log_dt", "ssm_b_re", "ssm_b_im", "ssm_c_re", "ssm_c_im", "ssm_d"]
    ssm_params = [wts[n][0] for n in ssm_names]
    blk, blk_vjp = jax.vjp(_ssm_block_params, *ssm_params)
    m_mat, ws_mat, wot_mat, a16 = _ssm_chunk_matrices(blk)
    bias_tab, bias_vjp = jax.vjp(_na_bias_table, wts["na_rpb"][0])

    u_s, z_s, q, k, v, z_n = _in_proj(x, wts["norm_pre"], w_in_g)
    u_c = _to_chunks(u_s, BF16)
    s_in = _block_matmul([(u_c, ws_mat, False)], "ssm_chunk_states")
    s_prev = _ssm_state_scan(s_in, a16)
    y_ssm = _from_chunks(_block_matmul([(u_c, m_mat, False), (s_prev, wot_mat, True)], "ssm_chunk_out"))
    y_na = _na_fwd(q, k, v, bias_tab)
    cat = _branch_fwd(y_ssm, z_s, y_na, z_n, w_glu, wts["b_glu"])

    (loss, d_h1, d_cat, d_w_out, d_g_post, d_w_ple, d_g_ple, d_w_pg) = _head(
        x, p, target, cat, w_out, wts["norm_post"], w_ple_g, wts["ple_norm"], w_pg)
    d_y_ssm, d_z_s, d_y_na, d_z_n, d_w_glu, d_b_glu = _branch_bwd(
        y_ssm, z_s, y_na, z_n, w_glu, wts["b_glu"], d_cat)
    d_q, d_k, d_v, d_bias_tab = _na_bwd(q, k, v, bias_tab, y_na, d_y_na)

    dy_c = _to_chunks(d_y_ssm, BF16)
    d_prev = _block_matmul([(dy_c, wot_mat, False)], "ssm_bwd_states")
    g_st, d_a16 = _ssm_state_scan_bwd(d_prev, s_prev, a16)
    d_u = _from_chunks(_block_matmul([(dy_c, m_mat, True), (g_st, ws_mat, True)], "ssm_bwd_in", out_dtype=BF16))
    d_m = _block_matmul_tn(u_c, dy_c, "ssm_grad_m")
    d_ws = _block_matmul_tn(u_c, g_st, "ssm_grad_ws")
    d_wot = _block_matmul_tn(dy_c, s_prev, "ssm_grad_wot")
    d_ssm = blk_vjp(tuple(_ssm_chunk_matrices_bwd(blk, d_m, d_ws, d_wot, d_a16)))
    (d_rpb,) = bias_vjp(d_bias_tab)

    dparts = [d_u, d_z_s, d_q, d_k, d_v, d_z_n]
    d_w_in = _in_proj_bwd_w(x, wts["norm_pre"], dparts)
    grad_x, d_g_pre = _in_proj_bwd_x(x, wts["norm_pre"], w_in_g, d_h1, dparts)

    small = {"norm_pre": d_g_pre, "norm_post": d_g_post, "b_glu": d_b_glu, "na_rpb": d_rpb, "ple_norm": d_g_ple}
    for n, g in zip(ssm_names, d_ssm):
        small[n] = g
    big = {"w_in": d_w_in, "w_glu": d_w_glu.reshape(N_CHIPS, 128, 512), "w_out": d_w_out.reshape(N_CHIPS, 256, 1024),
           "w_ple": d_w_ple, "w_ple_gate": d_w_pg.reshape(N_CHIPS, 256, 1024)}
    return loss, grad_x, small, big


def kernel(x, p, norm_pre, norm_post, w_in, ssm_a_re, ssm_a_im, ssm_log_dt, ssm_b_re, ssm_b_im, ssm_c_re, ssm_c_im, ssm_d, w_glu, b_glu, na_rpb, w_out, w_ple, ple_norm, w_ple_gate, loss_target, m_norm_pre, m_norm_post, m_w_in, m_ssm_a_re, m_ssm_a_im, m_ssm_log_dt, m_ssm_b_re, m_ssm_b_im, m_ssm_c_re, m_ssm_c_im, m_ssm_d, m_w_glu, m_b_glu, m_na_rpb, m_w_out, m_w_ple, m_ple_norm, m_w_ple_gate, v_norm_pre, v_norm_post, v_w_in, v_ssm_a_re, v_ssm_a_im, v_ssm_log_dt, v_ssm_b_re, v_ssm_b_im, v_ssm_c_re, v_ssm_c_im, v_ssm_d, v_w_glu, v_b_glu, v_na_rpb, v_w_out, v_w_ple, v_ple_norm, v_w_ple_gate):
    wts = dict(norm_pre=norm_pre, norm_post=norm_post, w_in=w_in, ssm_a_re=ssm_a_re, ssm_a_im=ssm_a_im,
               ssm_log_dt=ssm_log_dt, ssm_b_re=ssm_b_re, ssm_b_im=ssm_b_im, ssm_c_re=ssm_c_re, ssm_c_im=ssm_c_im,
               ssm_d=ssm_d, w_glu=w_glu, b_glu=b_glu, na_rpb=na_rpb, w_out=w_out, w_ple=w_ple, ple_norm=ple_norm,
               w_ple_gate=w_ple_gate)
    mom_m = dict(norm_pre=m_norm_pre, norm_post=m_norm_post, w_in=m_w_in, ssm_a_re=m_ssm_a_re, ssm_a_im=m_ssm_a_im,
                 ssm_log_dt=m_ssm_log_dt, ssm_b_re=m_ssm_b_re, ssm_b_im=m_ssm_b_im, ssm_c_re=m_ssm_c_re,
                 ssm_c_im=m_ssm_c_im, ssm_d=m_ssm_d, w_glu=m_w_glu, b_glu=m_b_glu, na_rpb=m_na_rpb, w_out=m_w_out,
                 w_ple=m_w_ple, ple_norm=m_ple_norm, w_ple_gate=m_w_ple_gate)
    mom_v = dict(norm_pre=v_norm_pre, norm_post=v_norm_post, w_in=v_w_in, ssm_a_re=v_ssm_a_re, ssm_a_im=v_ssm_a_im,
                 ssm_log_dt=v_ssm_log_dt, ssm_b_re=v_ssm_b_re, ssm_b_im=v_ssm_b_im, ssm_c_re=v_ssm_c_re,
                 ssm_c_im=v_ssm_c_im, ssm_d=v_ssm_d, w_glu=v_w_glu, b_glu=v_b_glu, na_rpb=v_na_rpb, w_out=v_w_out,
                 w_ple=v_w_ple, ple_norm=v_ple_norm, w_ple_gate=v_w_ple_gate)

    shards = [wts[n][0].astype(BF16) for n in _BIG]
    w_in_g, w_glu_g, w_out_g, w_ple_g, w_pg_g = _gather_chips(shards, "gather_weights")
    loss_part, grad_x, small, big = _local_grads(
        x[0], p[0, 0], loss_target[0], wts, w_in_g, w_glu_g.reshape(512, 512), w_out_g.reshape(1024, 1024),
        w_ple_g, w_pg_g.reshape(1024, 1024))
    loss = lax.psum(loss_part[0, 0], ("x", "y", "c"))

    small_packed = _pack_small(small).reshape(N_CHIPS, _SMALL_ROWS // N_CHIPS, 128)
    reduced = _reduce_scatter([big[n] for n in _BIG] + [small_packed])
    grads = dict(zip(_BIG, reduced[:-1]))
    (small_all,) = _gather_chips([reduced[-1]], "gather_small_grads")
    small_all = small_all.reshape(_SMALL_ROWS, 128)

    delta, new_m, new_v = {}, {}, {}
    for n in _BIG:
        shp = wts[n].shape
        d_, m_, v_ = _adamw(wts[n][0], grads[n], mom_m[n][0], mom_v[n][0])
        grads[n] = grads[n].reshape(shp)
        delta[n], new_m[n], new_v[n] = d_.reshape(shp), m_.reshape(shp), v_.reshape(shp)
    shapes = {n: wts[n].shape for n in _SMALL}
    d_s, m_s, v_s = _adamw(_pack_small(wts), small_all, _pack_small(mom_m), _pack_small(mom_v))
    for dst, packed in ((grads, small_all), (delta, d_s), (new_m, m_s), (new_v, v_s)):
        dst.update(_unpack_small(packed, shapes))

    return (loss, grad_x[None], *[grads[n] for n in _WEIGHTS], *[delta[n] for n in _WEIGHTS],
            *[new_m[n] for n in _WEIGHTS], *[new_v[n] for n in _WEIGHTS])
```

```python
import functools
import math

import jax
import jax.numpy as jnp
import numpy as np
from jax import lax
from jax.experimental import pallas as pl
from jax.experimental.pallas import tpu as pltpu

F32 = jnp.float32
BF16 = jnp.bfloat16

D_MODEL = 1024
D_PLE = 256
GRID_W = 64
D_SSM = 512
SSM_GROUP = 16
N_GROUPS = 32
SSM_STATE = 64
D_NA = 512
NA_HEADS = 8
NA_HEAD_DIM = 64
NA_ROWS = 8
NA_COLS = 16
D_IN_PROJ = 3072
EPS = 1e-6

CHUNK = 16
GROUPS_PER_BLOCK = 8
N_BLOCKS = N_GROUPS // GROUPS_PER_BLOCK
BLOCK_CH = GROUPS_PER_BLOCK * SSM_GROUP
BLOCK_ST = GROUPS_PER_BLOCK * SSM_STATE
CHUNK_W = CHUNK * BLOCK_CH
STATE_W = 4 * BLOCK_ST

N_CHIPS = 4
MESH = pl.DeviceIdType.MESH

ADAM_LR = 0.001
ADAM_B1 = 0.9
ADAM_B2 = 0.999
ADAM_EPS = 1e-08
ADAM_WD = 0.01
ADAM_STEP = 10

VMEM_LIMIT = 52 * 1024 * 1024
HIGHEST = lax.Precision.HIGHEST


def _cparams(sem=None, **kw):
    if sem is not None:
        kw["dimension_semantics"] = sem
    return pltpu.CompilerParams(vmem_limit_bytes=VMEM_LIMIT, **kw)


def _dot(a, b, dims=((1,), (0,))):
    return lax.dot_general(a, b, (dims, ((), ())), preferred_element_type=F32)


def _dot_nt(a, b):
    return _dot(a, b, ((1,), (1,)))


def _dot_tn(a, b):
    return _dot(a, b, ((0,), (0,)))


def _sigmoid(x):
    return 1.0 / (1.0 + jnp.exp(-x))


_GELU_C = math.sqrt(2.0 / math.pi)


def _gelu_parts(x):
    inner = _GELU_C * (x + 0.044715 * (x * x * x))
    t = jnp.tanh(inner)
    return 0.5 * x * (1.0 + t), t


def _gelu_grad(x, t):
    return 0.5 * (1.0 + t) + 0.5 * x * (1.0 - t * t) * (_GELU_C * (1.0 + 3.0 * 0.044715 * x * x))


def _silu_parts(z):
    s = _sigmoid(z)
    return z * s, s


def _silu_grad(z, s):
    return s * (1.0 + z * (1.0 - s))


def _rms(x):
    r = lax.rsqrt(jnp.mean(x * x, axis=-1, keepdims=True) + EPS)
    return x * r, r


def _rms_bwd(dn, n, r):
    return r * (dn - n * jnp.mean(dn * n, axis=-1, keepdims=True))


def _in_proj(x, g_pre, w_in_g, tm=256):
    L = x.shape[0]
    wn = w_in_g.shape[2]

    def body(x_ref, g_ref, w_ref, u_ref, zs_ref, q_ref, k_ref, v_ref, zn_ref):
        n, _ = _rms(x_ref[...])
        hn = (n * g_ref[...]).astype(BF16)
        proj = jnp.concatenate([_dot(hn, w_ref[j]) for j in range(N_CHIPS)], axis=1)
        u_ref[...] = proj[:, 0:512]
        zs_ref[...] = proj[:, 512:1024]
        q_ref[...] = (proj[:, 1024:1536] * (NA_HEAD_DIM ** -0.5)).astype(BF16)
        k_ref[...] = proj[:, 1536:2048].astype(BF16)
        v_ref[...] = proj[:, 2048:2560].astype(BF16)
        zn_ref[...] = proj[:, 2560:3072]

    tok = lambda w, dt: jax.ShapeDtypeStruct((L, w), dt)
    tspec = pl.BlockSpec((tm, 512), lambda i: (i, 0))
    return pl.pallas_call(
        body, name="in_proj", grid=(L // tm,),
        in_specs=[pl.BlockSpec((tm, D_MODEL), lambda i: (i, 0)),
                  pl.BlockSpec((1, D_MODEL), lambda i: (0, 0)),
                  pl.BlockSpec((N_CHIPS, D_MODEL, wn), lambda i: (0, 0, 0))],
        out_specs=[tspec] * 6,
        out_shape=[tok(512, F32), tok(512, F32), tok(512, BF16), tok(512, BF16), tok(512, BF16), tok(512, F32)],
        compiler_params=_cparams(("arbitrary",)),
    )(x, g_pre, w_in_g)


def _ssm_block_params(a_re, a_im, log_dt, b_re, b_im, c_re, c_im, d):
    eye_g = jnp.eye(GROUPS_PER_BLOCK, dtype=F32)[None, None, :, None, :, None]

    def lanes(t):
        return t.reshape(2, N_BLOCKS, 1, BLOCK_ST)

    def expand(t):
        return (t[:, :, :, :, None, :] * eye_g).reshape(2, N_BLOCKS, BLOCK_CH, BLOCK_ST)

    b_shape = (2, N_BLOCKS, GROUPS_PER_BLOCK, SSM_STATE, SSM_GROUP)
    c_shape = (2, N_BLOCKS, GROUPS_PER_BLOCK, SSM_GROUP, SSM_STATE)
    return (lanes(a_re), lanes(a_im), lanes(jnp.broadcast_to(log_dt[..., None], a_re.shape)),
            expand(b_re.reshape(b_shape).transpose(0, 1, 2, 4, 3)), expand(b_im.reshape(b_shape).transpose(0, 1, 2, 4, 3)),
            expand(c_re.reshape(c_shape)), expand(c_im.reshape(c_shape)), d.reshape(N_BLOCKS, 1, BLOCK_CH))


def _ssm_discretise(ar, ai, ldt):
    dt = jnp.exp(ldt)
    mag = jnp.exp(dt * ar)
    abr = mag * jnp.cos(dt * ai)
    abi = mag * jnp.sin(dt * ai)
    num_re = abr - 1.0
    num_im = abi
    denom = ar * ar + ai * ai
    coef_re = (num_re * ar + num_im * ai) / denom
    coef_im = (num_im * ar - num_re * ai) / denom
    return abr, abi, coef_re, coef_im


_POW_ROWS = 24


def _ssm_fill_powers(ar_ref, ai_ref, ldt_ref, br_ref, bi_ref, pw_ref, bbar_ref):
    for d in range(2):
        abr, abi, cfr, cfi = _ssm_discretise(ar_ref[d, 0], ai_ref[d, 0], ldt_ref[d, 0])
        bbar_ref[d, 0] = cfr * br_ref[d, 0] - cfi * bi_ref[d, 0]
        bbar_ref[d, 1] = cfr * bi_ref[d, 0] + cfi * br_ref[d, 0]
        pr, pi = jnp.ones_like(abr), jnp.zeros_like(abi)
        for t in range(CHUNK + 1):
            pw_ref[d, 0, t:t + 1, :] = pr
            pw_ref[d, 1, t:t + 1, :] = pi
            pr, pi = pr * abr - pi * abi, pr * abi + pi * abr


def _dot_hi(a, b, dims=((1,), (0,))):
    return lax.dot_general(a, b, (dims, ((), ())), precision=HIGHEST, preferred_element_type=F32)


def _eye(n):
    return (lax.broadcasted_iota(jnp.int32, (n, n), 0) == lax.broadcasted_iota(jnp.int32, (n, n), 1)).astype(F32)


def _ssm_param_specs():
    vec = pl.BlockSpec((2, 1, 1, BLOCK_ST), lambda b, j: (0, b, 0, 0))
    mat = pl.BlockSpec((2, 1, BLOCK_CH, BLOCK_ST), lambda b, j: (0, b, 0, 0))
    return [vec, vec, vec, mat, mat, mat, mat, pl.BlockSpec((1, 1, BLOCK_CH), lambda b, j: (b, 0, 0))]


def _ssm_chunk_matrices(blk):
    def body(ar_ref, ai_ref, ldt_ref, br_ref, bi_ref, cr_ref, ci_ref, d_ref,
             m_ref, ws_ref, wot_ref, a16_ref, pw_ref, bbar_ref, lag_ref):
        j = pl.program_id(1)

        @pl.when(j == 0)
        def _():
            _ssm_fill_powers(ar_ref, ai_ref, ldt_ref, br_ref, bi_ref, pw_ref, bbar_ref)
            zero_lag = d_ref[0] * _eye(BLOCK_CH)
            for d in range(2):
                for t in range(CHUNK):
                    pr, pi = pw_ref[d, 0, t:t + 1, :], pw_ref[d, 1, t:t + 1, :]
                    xr = bbar_ref[d, 0] * pr - bbar_ref[d, 1] * pi
                    xi = bbar_ref[d, 0] * pi + bbar_ref[d, 1] * pr
                    tap = _dot_hi(xr, cr_ref[d, 0], ((1,), (1,))) - _dot_hi(xi, ci_ref[d, 0], ((1,), (1,)))
                    if t == 0:
                        zero_lag = zero_lag + tap
                    else:
                        lag_ref[CHUNK - 1 + t if d == 0 else CHUNK - 1 - t] = tap
            lag_ref[CHUNK - 1] = zero_lag
            a16_ref[0] = jnp.concatenate([pw_ref[d, ri, CHUNK:CHUNK + 1, :] for d in range(2) for ri in range(2)], axis=1)

        m_ref[0] = jnp.concatenate([lag_ref[jp - j + CHUNK - 1] for jp in range(CHUNK)], axis=1).astype(BF16)

        def power(d, t):
            return pw_ref[d, 0, pl.ds(t, 1), :], pw_ref[d, 1, pl.ds(t, 1), :]

        parts = []
        for d, t in ((0, CHUNK - 1 - j), (1, j)):
            pr, pi = power(d, t)
            parts += [bbar_ref[d, 0] * pr - bbar_ref[d, 1] * pi, bbar_ref[d, 0] * pi + bbar_ref[d, 1] * pr]
        ws_ref[0] = jnp.concatenate(parts, axis=1).astype(BF16)
        parts = []
        for d, t in ((0, j + 1), (1, CHUNK - j)):
            pr, pi = power(d, t)
            parts += [cr_ref[d, 0] * pr - ci_ref[d, 0] * pi, -cr_ref[d, 0] * pi - ci_ref[d, 0] * pr]
        wot_ref[0] = jnp.concatenate(parts, axis=1).astype(BF16)

    row = pl.BlockSpec((1, BLOCK_CH, CHUNK_W), lambda b, j: (b, j, 0))
    mat = jax.ShapeDtypeStruct((N_BLOCKS, CHUNK_W, CHUNK_W), BF16)
    return pl.pallas_call(
        body, name="ssm_chunk_matrices", grid=(N_BLOCKS, CHUNK),
        in_specs=_ssm_param_specs(),
        out_specs=[row, row, row, pl.BlockSpec((1, 1, STATE_W), lambda b, j: (b, 0, 0))],
        out_shape=[mat, mat, mat, jax.ShapeDtypeStruct((N_BLOCKS, 1, STATE_W), F32)],
        scratch_shapes=[pltpu.VMEM((2, 2, _POW_ROWS, BLOCK_ST), F32), pltpu.VMEM((2, 2, BLOCK_CH, BLOCK_ST), F32),
                        pltpu.VMEM((2 * CHUNK, BLOCK_CH, BLOCK_CH), F32)],
        compiler_params=_cparams(("arbitrary", "arbitrary")),
    )(*blk)


def _ssm_chunk_matrices_bwd(blk, d_m, d_ws, d_wot, d_a16):
    def body(ar_ref, ai_ref, ldt_ref, br_ref, bi_ref, cr_ref, ci_ref, d_ref, dm_ref, dws_ref, dwot_ref, da16_ref,
             dar_ref, dai_ref, dldt_ref, dbr_ref, dbi_ref, dcr_ref, dci_ref, dd_ref,
             pw_ref, bbar_ref, dlag_ref, dbbar_ref, dc_ref, dpw_ref):
        j = pl.program_id(1)
        w = BLOCK_ST

        @pl.when(j == 0)
        def _():
            _ssm_fill_powers(ar_ref, ai_ref, ldt_ref, br_ref, bi_ref, pw_ref, bbar_ref)
            for r in (dlag_ref, dbbar_ref, dc_ref, dpw_ref):
                r[...] = jnp.zeros_like(r)

        def x_chain(d, t, dxr, dxi):
            pr, pi = pw_ref[d, 0, pl.ds(t, 1), :], pw_ref[d, 1, pl.ds(t, 1), :]
            bbr, bbi = bbar_ref[d, 0], bbar_ref[d, 1]
            dbbar_ref[d, 0] += dxr * pr + dxi * pi
            dbbar_ref[d, 1] += dxi * pr - dxr * pi
            dpw_ref[d, 0, pl.ds(t, 1), :] += jnp.sum(dxr * bbr + dxi * bbi, axis=0, keepdims=True)
            dpw_ref[d, 1, pl.ds(t, 1), :] += jnp.sum(dxi * bbr - dxr * bbi, axis=0, keepdims=True)

        def z_chain(d, t, dzr, dzi):
            pr, pi = pw_ref[d, 0, pl.ds(t, 1), :], pw_ref[d, 1, pl.ds(t, 1), :]
            c_r, c_i = cr_ref[d, 0], ci_ref[d, 0]
            dc_ref[d, 0] += dzr * pr - dzi * pi
            dc_ref[d, 1] += -dzr * pi - dzi * pr
            dpw_ref[d, 0, pl.ds(t, 1), :] += jnp.sum(dzr * c_r - dzi * c_i, axis=0, keepdims=True)
            dpw_ref[d, 1, pl.ds(t, 1), :] += jnp.sum(-dzr * c_i - dzi * c_r, axis=0, keepdims=True)

        for jp in range(CHUNK):
            dlag_ref[jp - j + CHUNK - 1] += dm_ref[0, :, jp * BLOCK_CH:(jp + 1) * BLOCK_CH]
        x_chain(0, CHUNK - 1 - j, dws_ref[0, :, 0:w], dws_ref[0, :, w:2 * w])
        x_chain(1, j, dws_ref[0, :, 2 * w:3 * w], dws_ref[0, :, 3 * w:4 * w])
        z_chain(0, j + 1, dwot_ref[0, :, 0:w], dwot_ref[0, :, w:2 * w])
        z_chain(1, CHUNK - j, dwot_ref[0, :, 2 * w:3 * w], dwot_ref[0, :, 3 * w:4 * w])

        @pl.when(j == CHUNK - 1)
        def _():
            for d in range(2):
                for t in range(CHUNK):
                    d_tap = dlag_ref[CHUNK - 1 + t if d == 0 else CHUNK - 1 - t]
                    pr, pi = pw_ref[d, 0, t:t + 1, :], pw_ref[d, 1, t:t + 1, :]
                    xr = bbar_ref[d, 0] * pr - bbar_ref[d, 1] * pi
                    xi = bbar_ref[d, 0] * pi + bbar_ref[d, 1] * pr
                    dc_ref[d, 0] += _dot_hi(d_tap, xr, ((0,), (0,)))
                    dc_ref[d, 1] -= _dot_hi(d_tap, xi, ((0,), (0,)))
                    x_chain(d, t, _dot_hi(d_tap, cr_ref[d, 0]), -_dot_hi(d_tap, ci_ref[d, 0]))
            dd_ref[0] = jnp.sum(dlag_ref[CHUNK - 1] * _eye(BLOCK_CH), axis=0, keepdims=True)
            for d in range(2):
                (abr, abi, cfr, cfi), disc_vjp = jax.vjp(_ssm_discretise, ar_ref[d, 0], ai_ref[d, 0], ldt_ref[d, 0])
                dpr = dpw_ref[d, 0, CHUNK:CHUNK + 1, :] + da16_ref[0, :, 2 * d * w:(2 * d + 1) * w]
                dpi = dpw_ref[d, 1, CHUNK:CHUNK + 1, :] + da16_ref[0, :, (2 * d + 1) * w:(2 * d + 2) * w]
                dabr, dabi = jnp.zeros_like(abr), jnp.zeros_like(abi)
                for t in range(CHUNK, 0, -1):
                    qr, qi = pw_ref[d, 0, t - 1:t, :], pw_ref[d, 1, t - 1:t, :]
                    dabr = dabr + dpr * qr + dpi * qi
                    dabi = dabi + dpi * qr - dpr * qi
                    dpr, dpi = (dpr * abr + dpi * abi + dpw_ref[d, 0, t - 1:t, :],
                                dpi * abr - dpr * abi + dpw_ref[d, 1, t - 1:t, :])
                dbbr, dbbi = dbbar_ref[d, 0], dbbar_ref[d, 1]
                b_r, b_i = br_ref[d, 0], bi_ref[d, 0]
                dbr_ref[d, 0] = cfr * dbbr + cfi * dbbi
                dbi_ref[d, 0] = cfr * dbbi - cfi * dbbr
                dcfr = jnp.sum(b_r * dbbr + b_i * dbbi, axis=0, keepdims=True)
                dcfi = jnp.sum(b_r * dbbi - b_i * dbbr, axis=0, keepdims=True)
                dar_ref[d, 0], dai_ref[d, 0], dldt_ref[d, 0] = disc_vjp((dabr, dabi, dcfr, dcfi))
                dcr_ref[d, 0] = dc_ref[d, 0]
                dci_ref[d, 0] = dc_ref[d, 1]

    row = pl.BlockSpec((1, BLOCK_CH, CHUNK_W), lambda b, j: (b, j, 0))
    specs = _ssm_param_specs()
    acc = lambda *s: pltpu.VMEM(s, F32)
    return pl.pallas_call(
        body, name="ssm_chunk_matrices_bwd", grid=(N_BLOCKS, CHUNK),
        in_specs=specs + [row, row, row, pl.BlockSpec((1, 1, STATE_W), lambda b, j: (b, 0, 0))],
        out_specs=specs,
        out_shape=[jax.ShapeDtypeStruct(t.shape, F32) for t in blk],
        scratch_shapes=[acc(2, 2, _POW_ROWS, BLOCK_ST), acc(2, 2, BLOCK_CH, BLOCK_ST), acc(2 * CHUNK, BLOCK_CH, BLOCK_CH),
                        acc(2, 2, BLOCK_CH, BLOCK_ST), acc(2, 2, BLOCK_CH, BLOCK_ST), acc(2, 2, _POW_ROWS, BLOCK_ST)],
        compiler_params=_cparams(("arbitrary", "arbitrary")),
    )(*blk, d_m, d_ws, d_wot, d_a16)


def _to_chunks(t, dtype):
    L = t.shape[0]
    t = t.astype(dtype).reshape(L // CHUNK, CHUNK, N_BLOCKS, BLOCK_CH)
    return t.transpose(2, 0, 1, 3).reshape(N_BLOCKS, L // CHUNK, CHUNK_W)


def _from_chunks(t):
    nc = t.shape[1]
    t = t.reshape(N_BLOCKS, nc, CHUNK, BLOCK_CH).transpose(1, 2, 0, 3)
    return t.reshape(nc * CHUNK, D_SSM)


def _block_matmul(terms, name, out_dtype=F32, tn=1024):
    nc = terms[0][0].shape[1]
    n_out = terms[0][1].shape[1] if terms[0][2] else terms[0][1].shape[2]
    flags = [t[2] for t in terms]

    def body(*refs):
        out_ref = refs[-1]
        acc = None
        for t, transposed in enumerate(flags):
            a = refs[2 * t][0].astype(BF16)
            w = refs[2 * t + 1][0]
            part = _dot_nt(a, w) if transposed else _dot(a, w)
            acc = part if acc is None else acc + part
        out_ref[0] = acc.astype(out_dtype)

    in_specs, args = [], []
    for a, w, transposed in terms:
        k = a.shape[2]
        in_specs.append(pl.BlockSpec((1, nc, k), lambda b, n: (b, 0, 0)))
        if transposed:
            in_specs.append(pl.BlockSpec((1, tn, k), lambda b, n: (b, n, 0)))
        else:
            in_specs.append(pl.BlockSpec((1, k, tn), lambda b, n: (b, 0, n)))
        args += [a, w]
    return pl.pallas_call(
        body, name=name, grid=(N_BLOCKS, n_out // tn), in_specs=in_specs,
        out_specs=pl.BlockSpec((1, nc, tn), lambda b, n: (b, 0, n)),
        out_shape=jax.ShapeDtypeStruct((N_BLOCKS, nc, n_out), out_dtype),
        compiler_params=_cparams(("arbitrary", "arbitrary")),
    )(*args)


def _block_matmul_tn(a, b, name, tile=1024):
    nc, m = a.shape[1], a.shape[2]
    n = b.shape[2]

    def body(a_ref, b_ref, out_ref):
        out_ref[0] = _dot_tn(a_ref[0].astype(BF16), b_ref[0].astype(BF16))

    return pl.pallas_call(
        body, name=name, grid=(N_BLOCKS, m // tile, n // tile),
        in_specs=[pl.BlockSpec((1, nc, tile), lambda blk, i, j: (blk, 0, i)),
                  pl.BlockSpec((1, nc, tile), lambda blk, i, j: (blk, 0, j))],
        out_specs=pl.BlockSpec((1, tile, tile), lambda blk, i, j: (blk, i, j)),
        out_shape=jax.ShapeDtypeStruct((N_BLOCKS, m, n), F32),
        compiler_params=_cparams(("arbitrary", "arbitrary", "arbitrary")),
    )(a, b)


def _cmul(ar, ai, xr, xi):
    return ar * xr - ai * xi, ar * xi + ai * xr


def _cmul_conj(ar, ai, xr, xi):
    return ar * xr + ai * xi, ar * xi - ai * xr


def _ssm_state_scan(s_in, a16):
    nc = s_in.shape[1]
    w = BLOCK_ST

    def body(sin_ref, a_ref, out_ref):
        a = a_ref[0]
        afr, afi, abr, abi = a[:, 0:w], a[:, w:2 * w], a[:, 2 * w:3 * w], a[:, 3 * w:4 * w]

        def step(c, carry):
            fr, fi, br, bi = carry
            cb = nc - 1 - c
            out_ref[0, pl.ds(c, 1), 0:w] = fr
            out_ref[0, pl.ds(c, 1), w:2 * w] = fi
            out_ref[0, pl.ds(cb, 1), 2 * w:3 * w] = br
            out_ref[0, pl.ds(cb, 1), 3 * w:4 * w] = bi
            nfr, nfi = _cmul(afr, afi, fr, fi)
            nbr, nbi = _cmul(abr, abi, br, bi)
            return (nfr + sin_ref[0, pl.ds(c, 1), 0:w], nfi + sin_ref[0, pl.ds(c, 1), w:2 * w],
                    nbr + sin_ref[0, pl.ds(cb, 1), 2 * w:3 * w], nbi + sin_ref[0, pl.ds(cb, 1), 3 * w:4 * w])

        z = jnp.zeros((1, w), F32)
        lax.fori_loop(0, nc, step, (z, z, z, z))

    spec = pl.BlockSpec((1, nc, STATE_W), lambda b: (b, 0, 0))
    return pl.pallas_call(
        body, name="ssm_state_scan", grid=(N_BLOCKS,),
        in_specs=[spec, pl.BlockSpec((1, 1, STATE_W), lambda b: (b, 0, 0))],
        out_specs=spec, out_shape=jax.ShapeDtypeStruct(s_in.shape, F32),
        compiler_params=_cparams(("arbitrary",)),
    )(s_in, a16)


def _ssm_state_scan_bwd(d_prev, s_prev, a16):
    nc = d_prev.shape[1]
    w = BLOCK_ST

    def body(dp_ref, sp_ref, a_ref, g_ref, da_ref):
        a = a_ref[0]
        afr, afi, abr, abi = a[:, 0:w], a[:, w:2 * w], a[:, 2 * w:3 * w], a[:, 3 * w:4 * w]

        def step(i, carry):
            gfr, gfi, gbr, gbi, dafr, dafi, dabr, dabi = carry
            cf = nc - 1 - i
            cb = i
            g_ref[0, pl.ds(cf, 1), 0:w] = gfr
            g_ref[0, pl.ds(cf, 1), w:2 * w] = gfi
            g_ref[0, pl.ds(cb, 1), 2 * w:3 * w] = gbr
            g_ref[0, pl.ds(cb, 1), 3 * w:4 * w] = gbi
            sfr, sfi = sp_ref[0, pl.ds(cf, 1), 0:w], sp_ref[0, pl.ds(cf, 1), w:2 * w]
            sbr, sbi = sp_ref[0, pl.ds(cb, 1), 2 * w:3 * w], sp_ref[0, pl.ds(cb, 1), 3 * w:4 * w]
            dafr = dafr + gfr * sfr + gfi * sfi
            dafi = dafi + gfi * sfr - gfr * sfi
            dabr = dabr + gbr * sbr + gbi * sbi
            dabi = dabi + gbi * sbr - gbr * sbi
            nfr, nfi = _cmul_conj(afr, afi, gfr, gfi)
            nbr, nbi = _cmul_conj(abr, abi, gbr, gbi)
            return (nfr + dp_ref[0, pl.ds(cf, 1), 0:w], nfi + dp_ref[0, pl.ds(cf, 1), w:2 * w],
                    nbr + dp_ref[0, pl.ds(cb, 1), 2 * w:3 * w], nbi + dp_ref[0, pl.ds(cb, 1), 3 * w:4 * w],
                    dafr, dafi, dabr, dabi)

        z = jnp.zeros((1, w), F32)
        res = lax.fori_loop(0, nc, step, (z,) * 8)
        da_ref[0] = jnp.concatenate(res[4:], axis=1)

    spec = pl.BlockSpec((1, nc, STATE_W), lambda b: (b, 0, 0))
    aspec = pl.BlockSpec((1, 1, STATE_W), lambda b: (b, 0, 0))
    return pl.pallas_call(
        body, name="ssm_state_scan_bwd", grid=(N_BLOCKS,),
        in_specs=[spec, spec, aspec], out_specs=[spec, aspec],
        out_shape=[jax.ShapeDtypeStruct(d_prev.shape, F32), jax.ShapeDtypeStruct((N_BLOCKS, 1, STATE_W), F32)],
        compiler_params=_cparams(("arbitrary",)),
    )(d_prev, s_prev, a16)


NA_PAIR = 2 * GRID_W
NA_WIN_ROWS = NA_ROWS + 2
NA_WIN = NA_WIN_ROWS * GRID_W
NA_PAIRS_PER_STEP = 8
NA_CASES = 5
NA_MASKED = -1e30


def _na_pair_window(m, rows):
    rs0 = jnp.clip(2 * m - NA_ROWS // 2, 0, rows - NA_ROWS)
    ws = jnp.minimum(rs0, rows - NA_WIN_ROWS)
    last = rows // 2 - 1
    case = jnp.where(m == 0, 0, jnp.where(m == 1, 1, jnp.where(m == last - 1, 3, jnp.where(m == last, 4, 2))))
    return ws, case


def _na_bias_table(rpb, rows):
    qc = np.arange(GRID_W)[None, :]
    kc = np.arange(GRID_W)[:, None]
    dc = np.clip(kc - qc + NA_COLS - 1, 0, 2 * NA_COLS - 2)
    cs = np.clip(qc - NA_COLS // 2, 0, GRID_W - NA_COLS)
    col_ok = (kc >= cs) & (kc < cs + NA_COLS)
    col_sel = (np.arange(2 * NA_COLS - 1)[:, None, None] == dc[None]).astype(np.float32)
    row_sel = np.zeros((NA_CASES, 2 * NA_ROWS - 1, NA_WIN_ROWS, 2), np.float32)
    row_ok = np.zeros((NA_CASES, NA_WIN_ROWS, 2), bool)
    last = rows // 2 - 1
    for case, m in enumerate((0, 1, 2, last - 1, last)):
        ws = min(max(2 * m - NA_ROWS // 2, 0), rows - NA_ROWS, rows - NA_WIN_ROWS)
        for rr in range(2):
            r = 2 * m + rr
            rs = min(max(r - NA_ROWS // 2, 0), rows - NA_ROWS)
            for i in range(NA_WIN_ROWS):
                if rs <= ws + i < rs + NA_ROWS:
                    row_ok[case, i, rr] = True
                    row_sel[case, ws + i - r + NA_ROWS - 1, i, rr] = 1.0
    per_row = jnp.einsum("hrd,dkq->hrkq", rpb, jnp.asarray(col_sel), precision=HIGHEST)
    tab = jnp.einsum("crip,hrkq->hcikpq", jnp.asarray(row_sel), per_row, precision=HIGHEST)
    ok = row_ok[:, :, None, :, None] & col_ok[None, None, :, None, :]
    tab = tab + jnp.asarray(np.where(ok, 0.0, NA_MASKED).astype(np.float32))
    return tab.reshape(NA_HEADS, NA_CASES, NA_WIN, NA_PAIR)


def _heads_major(t, dtype):
    return t.astype(dtype).reshape(t.shape[0], NA_HEADS, NA_HEAD_DIM).transpose(1, 0, 2)


def _heads_major_t(t, dtype):
    return t.astype(dtype).reshape(t.shape[0], NA_HEADS, NA_HEAD_DIM).transpose(1, 2, 0)


def _na_scores(k_win, q_t, bias):
    s = _dot(k_win, q_t) + bias
    e = jnp.exp(s - jnp.max(s, axis=0, keepdims=True))
    return e, jnp.sum(e, axis=0, keepdims=True)


def _na_fwd(q_t, k_h, v_t, bias_tab):
    L = k_h.shape[1]
    rows = L // GRID_W
    step_w = NA_PAIRS_PER_STEP * NA_PAIR

    def body(q_ref, k_ref, v_ref, bt_ref, o_ref):
        for pp in range(NA_PAIRS_PER_STEP):
            ws, case = _na_pair_window(pl.program_id(1) * NA_PAIRS_PER_STEP + pp, rows)
            start = pl.multiple_of(ws * GRID_W, NA_PAIR)
            lanes = slice(pp * NA_PAIR, (pp + 1) * NA_PAIR)
            e, l = _na_scores(k_ref[0, pl.ds(start, NA_WIN), :], q_ref[0, :, lanes], bt_ref[0, case])
            o_ref[0, :, lanes] = _dot(v_ref[0, :, pl.ds(start, NA_WIN)], e.astype(BF16)) / l

    q_spec = pl.BlockSpec((1, NA_HEAD_DIM, step_w), lambda h, s: (h, 0, s))
    return pl.pallas_call(
        body, name="na_fwd", grid=(NA_HEADS, L // step_w),
        in_specs=[q_spec, pl.BlockSpec((1, L, NA_HEAD_DIM), lambda h, s: (h, 0, 0)),
                  pl.BlockSpec((1, NA_HEAD_DIM, L), lambda h, s: (h, 0, 0)),
                  pl.BlockSpec((1, NA_CASES, NA_WIN, NA_PAIR), lambda h, s: (h, 0, 0, 0))],
        out_specs=q_spec,
        out_shape=jax.ShapeDtypeStruct((NA_HEADS, NA_HEAD_DIM, L), F32),
        compiler_params=_cparams(("arbitrary", "arbitrary")),
    )(q_t, k_h, v_t, bias_tab)


def _na_bwd(q_t, q_h, k_t, k_h, v_h, bias_tab, out_t, d_out_t, d_out_h):
    L = k_h.shape[1]
    rows = L // GRID_W
    step_w = NA_PAIRS_PER_STEP * NA_PAIR

    def body(qt_ref, qh_ref, kt_ref, kh_ref, vh_ref, bt_ref, ot_ref, dot_ref, doh_ref, dq_ref, dk_ref, dv_ref, dbt_ref):
        @pl.when(pl.program_id(1) == 0)
        def _():
            dk_ref[...] = jnp.zeros_like(dk_ref)
            dv_ref[...] = jnp.zeros_like(dv_ref)
            dbt_ref[...] = jnp.zeros_like(dbt_ref)

        for pp in range(NA_PAIRS_PER_STEP):
            ws, case = _na_pair_window(pl.program_id(1) * NA_PAIRS_PER_STEP + pp, rows)
            start = pl.multiple_of(ws * GRID_W, NA_PAIR)
            lanes = slice(pp * NA_PAIR, (pp + 1) * NA_PAIR)
            win = pl.ds(start, NA_WIN)
            e, l = _na_scores(kh_ref[0, win, :], qt_ref[0, :, lanes], bt_ref[0, case])
            p = e / l
            d_o = dot_ref[0, :, lanes]
            dp = _dot(vh_ref[0, win, :], d_o.astype(BF16))
            delta = jnp.sum(d_o * ot_ref[0, :, lanes], axis=0, keepdims=True)
            ds = p * (dp - delta)
            dbt_ref[0, case] += ds
            dsb = ds.astype(BF16)
            dq_ref[0, :, lanes] = _dot(kt_ref[0, :, win], dsb) * (NA_HEAD_DIM ** -0.5)
            tokens = slice(pp * NA_PAIR, (pp + 1) * NA_PAIR)
            dk_ref[0, win, :] += _dot(dsb, qh_ref[0, tokens, :])
            dv_ref[0, win, :] += _dot(p.astype(BF16), doh_ref[0, tokens, :])

    t_tile = pl.BlockSpec((1, NA_HEAD_DIM, step_w), lambda h, s: (h, 0, s))
    h_tile = pl.BlockSpec((1, step_w, NA_HEAD_DIM), lambda h, s: (h, s, 0))
    t_full = pl.BlockSpec((1, NA_HEAD_DIM, L), lambda h, s: (h, 0, 0))
    h_full = pl.BlockSpec((1, L, NA_HEAD_DIM), lambda h, s: (h, 0, 0))
    bt = pl.BlockSpec((1, NA_CASES, NA_WIN, NA_PAIR), lambda h, s: (h, 0, 0, 0))
    return pl.pallas_call(
        body, name="na_bwd", grid=(NA_HEADS, L // step_w),
        in_specs=[t_tile, h_tile, t_full, h_full, h_full, bt, t_tile, t_tile, h_tile],
        out_specs=[t_tile, h_full, h_full, bt],
        out_shape=[jax.ShapeDtypeStruct((NA_HEADS, NA_HEAD_DIM, L), F32), jax.ShapeDtypeStruct((NA_HEADS, L, NA_HEAD_DIM), F32),
                   jax.ShapeDtypeStruct((NA_HEADS, L, NA_HEAD_DIM), F32), jax.ShapeDtypeStruct(bias_tab.shape, F32)],
        compiler_params=_cparams(("arbitrary", "arbitrary")),
    )(q_t, q_h, k_t, k_h, v_h, bias_tab, out_t, d_out_t, d_out_h)


def _branch_fwd_values(ys, zs, yn, zn, wglu, bglu):
    g1, t = _gelu_parts(ys)
    lin = _dot(g1.astype(BF16), wglu) + bglu
    sg = _sigmoid(lin)
    ys2 = g1 * sg
    sz, szs = _silu_parts(zs)
    sn, sns = _silu_parts(zn)
    return g1, t, sg, ys2, sz, szs, sn, sns


def _branch_fwd(y_ssm, z_s, y_na, z_n, w_glu, b_glu, tm=512):
    L = y_ssm.shape[0]

    def body(ys_ref, zs_ref, yn_ref, zn_ref, w_ref, b_ref, cat_ref):
        g1, t, sg, ys2, sz, szs, sn, sns = _branch_fwd_values(
            ys_ref[...], zs_ref[...], yn_ref[...], zn_ref[...], w_ref[...], b_ref[...])
        cat_ref[:, 0:512] = (ys2 * sz).astype(BF16)
        cat_ref[:, 512:1024] = (yn_ref[...] * sn).astype(BF16)

    tile = pl.BlockSpec((tm, 512), lambda i: (i, 0))
    return pl.pallas_call(
        body, name="branch_fwd", grid=(L // tm,),
        in_specs=[tile, tile, tile, tile, pl.BlockSpec((512, 512), lambda i: (0, 0)),
                  pl.BlockSpec((1, 512), lambda i: (0, 0))],
        out_specs=pl.BlockSpec((tm, 1024), lambda i: (i, 0)),
        out_shape=jax.ShapeDtypeStruct((L, 1024), BF16),
        compiler_params=_cparams(("arbitrary",)),
    )(y_ssm, z_s, y_na, z_n, w_glu, b_glu)


def _branch_bwd(y_ssm, z_s, y_na, z_n, w_glu, b_glu, d_cat, tm=512):
    L = y_ssm.shape[0]

    def body(ys_ref, zs_ref, yn_ref, zn_ref, w_ref, b_ref, dc_ref, dys_ref, dzs_ref, dyn_ref, dzn_ref, dw_ref, db_ref):
        @pl.when(pl.program_id(0) == 0)
        def _():
            dw_ref[...] = jnp.zeros_like(dw_ref)
            db_ref[...] = jnp.zeros_like(db_ref)

        ys, zs, yn, zn = ys_ref[...], zs_ref[...], yn_ref[...], zn_ref[...]
        w = w_ref[...]
        g1, t, sg, ys2, sz, szs, sn, sns = _branch_fwd_values(ys, zs, yn, zn, w, b_ref[...])
        dys3 = dc_ref[:, 0:512]
        dyn2 = dc_ref[:, 512:1024]
        dzs_ref[...] = (dys3 * ys2 * _silu_grad(zs, szs)).astype(BF16)
        dys2 = dys3 * sz
        dlin = dys2 * g1 * sg * (1.0 - sg)
        dlb = dlin.astype(BF16)
        db_ref[...] += jnp.sum(dlin, axis=0, keepdims=True)
        dw_ref[...] += _dot_tn(g1.astype(BF16), dlb)
        dg1 = dys2 * sg + _dot_nt(dlb, w)
        dys_ref[...] = (dg1 * _gelu_grad(ys, t)).astype(BF16)
        dyn_ref[...] = dyn2 * sn
        dzn_ref[...] = (dyn2 * yn * _silu_grad(zn, sns)).astype(BF16)

    tile = pl.BlockSpec((tm, 512), lambda i: (i, 0))
    wspec = pl.BlockSpec((512, 512), lambda i: (0, 0))
    bspec = pl.BlockSpec((1, 512), lambda i: (0, 0))
    tok = lambda dt: jax.ShapeDtypeStruct((L, 512), dt)
    return pl.pallas_call(
        body, name="branch_bwd", grid=(L // tm,),
        in_specs=[tile, tile, tile, tile, wspec, bspec, pl.BlockSpec((tm, 1024), lambda i: (i, 0))],
        out_specs=[tile, tile, tile, tile, wspec, bspec],
        out_shape=[tok(BF16), tok(BF16), tok(F32), tok(BF16),
                   jax.ShapeDtypeStruct((512, 512), F32), jax.ShapeDtypeStruct((1, 512), F32)],
        compiler_params=_cparams(("arbitrary",)),
    )(y_ssm, z_s, y_na, z_n, w_glu, b_glu, d_cat)


def _head(x, p, target, cat, w_out, g_post, w_ple_g, g_ple, w_pg, tm=256):
    L = x.shape[0]
    pw = w_ple_g.shape[2]

    def body(x_ref, p_ref, t_ref, cat_ref, wo_ref, gpo_ref, wp_ref, gpl_ref, wg_ref,
             loss_ref, dh1_ref, dcat_ref, dwo_ref, dgpo_ref, dwp_ref, dgpl_ref, dwg_ref):
        @pl.when(pl.program_id(0) == 0)
        def _():
            for r in (loss_ref, dwo_ref, dgpo_ref, dwp_ref, dgpl_ref, dwg_ref):
                r[...] = jnp.zeros_like(r)

        cat_b = cat_ref[...]
        wo, wg = wo_ref[...], wg_ref[...]
        g_po, g_pl = gpo_ref[...], gpl_ref[...]
        mix = _dot(cat_b, wo)
        nm, r2 = _rms(mix)
        h1 = x_ref[...] + nm * g_po
        p_b = p_ref[...].astype(BF16)
        ep = jnp.concatenate([_dot(p_b, wp_ref[j]) for j in range(N_CHIPS)], axis=1)
        ne, r3 = _rms(ep)
        e = ne * g_pl
        h1_b = h1.astype(BF16)
        gate = _sigmoid(_dot(h1_b, wg))
        h2 = h1 + gate * e
        diff = h2 - t_ref[...]
        loss_ref[...] += (0.5 / D_MODEL) * jnp.sum(diff * diff).reshape(1, 1)

        dh2 = diff * (1.0 / D_MODEL)
        de = dh2 * gate
        dgl = (dh2 * e * gate * (1.0 - gate)).astype(BF16)
        dwg_ref[...] += _dot_tn(h1_b, dgl)
        dh1 = dh2 + _dot_nt(dgl, wg)
        dgpl_ref[...] += jnp.sum(de * ne, axis=0, keepdims=True)
        dep = _rms_bwd(de * g_pl, ne, r3).astype(BF16)
        for j in range(N_CHIPS):
            dwp_ref[j] += _dot_tn(p_b, dep[:, j * pw:(j + 1) * pw])
        dgpo_ref[...] += jnp.sum(dh1 * nm, axis=0, keepdims=True)
        dmix = _rms_bwd(dh1 * g_po, nm, r2).astype(BF16)
        dwo_ref[...] += _dot_tn(cat_b, dmix)
        dcat_ref[...] = _dot_nt(dmix, wo)
        dh1_ref[...] = dh1

    tile = lambda w: pl.BlockSpec((tm, w), lambda i: (i, 0))
    const = lambda *s: pl.BlockSpec(s, lambda i: (0,) * len(s))
    sds = jax.ShapeDtypeStruct
    return pl.pallas_call(
        body, name="head", grid=(L // tm,),
        in_specs=[tile(D_MODEL), tile(D_PLE), tile(D_MODEL), tile(1024), const(1024, D_MODEL), const(1, D_MODEL),
                  const(N_CHIPS, D_PLE, pw), const(1, D_MODEL), const(D_MODEL, D_MODEL)],
        out_specs=[const(1, 1), tile(D_MODEL), tile(1024), const(1024, D_MODEL), const(1, D_MODEL),
                   const(N_CHIPS, D_PLE, pw), const(1, D_MODEL), const(D_MODEL, D_MODEL)],
        out_shape=[sds((1, 1), F32), sds((L, D_MODEL), F32), sds((L, 1024), F32), sds((1024, D_MODEL), F32),
                   sds((1, D_MODEL), F32), sds((N_CHIPS, D_PLE, pw), F32), sds((1, D_MODEL), F32),
                   sds((D_MODEL, D_MODEL), F32)],
        compiler_params=_cparams(("arbitrary",)),
    )(x, p, target, cat, w_out, g_post, w_ple_g, g_ple, w_pg)


_DPROJ_DTYPES = (BF16, BF16, F32, F32, F32, BF16)


def _dproj_tile(refs):
    return jnp.concatenate([r[...].astype(BF16) for r in refs], axis=1)


def _in_proj_bwd_w(x, g_pre, dparts, tm=512):
    L = x.shape[0]
    wn = D_IN_PROJ // N_CHIPS

    def body(x_ref, g_ref, *refs):
        dw_ref = refs[-1]

        @pl.when(pl.program_id(0) == 0)
        def _():
            dw_ref[...] = jnp.zeros_like(dw_ref)

        n, _ = _rms(x_ref[...])
        hn = (n * g_ref[...]).astype(BF16)
        dproj = _dproj_tile(refs[:-1])
        for j in range(N_CHIPS):
            dw_ref[j] += _dot_tn(hn, dproj[:, j * wn:(j + 1) * wn])

    tile = pl.BlockSpec((tm, 512), lambda i: (i, 0))
    return pl.pallas_call(
        body, name="in_proj_bwd_w", grid=(L // tm,),
        in_specs=[pl.BlockSpec((tm, D_MODEL), lambda i: (i, 0)), pl.BlockSpec((1, D_MODEL), lambda i: (0, 0))] + [tile] * 6,
        out_specs=pl.BlockSpec((N_CHIPS, D_MODEL, wn), lambda i: (0, 0, 0)),
        out_shape=jax.ShapeDtypeStruct((N_CHIPS, D_MODEL, wn), F32),
        compiler_params=_cparams(("arbitrary",)),
    )(x, g_pre, *dparts)


def _in_proj_bwd_x(x, g_pre, w_in_g, d_h1, dparts, tm=512):
    L = x.shape[0]
    wn = w_in_g.shape[2]

    def body(x_ref, g_ref, w_ref, dh1_ref, *refs):
        dx_ref, dg_ref = refs[-2], refs[-1]

        @pl.when(pl.program_id(0) == 0)
        def _():
            dg_ref[...] = jnp.zeros_like(dg_ref)

        n, r = _rms(x_ref[...])
        dproj = _dproj_tile(refs[:-2])
        dhn = _dot_nt(dproj[:, 0:wn], w_ref[0])
        for j in range(1, N_CHIPS):
            dhn = dhn + _dot_nt(dproj[:, j * wn:(j + 1) * wn], w_ref[j])
        dg_ref[...] += jnp.sum(dhn * n, axis=0, keepdims=True)
        dx_ref[...] = dh1_ref[...] + _rms_bwd(dhn * g_ref[...], n, r)

    tile = pl.BlockSpec((tm, 512), lambda i: (i, 0))
    wide = pl.BlockSpec((tm, D_MODEL), lambda i: (i, 0))
    vec = pl.BlockSpec((1, D_MODEL), lambda i: (0, 0))
    return pl.pallas_call(
        body, name="in_proj_bwd_x", grid=(L // tm,),
        in_specs=[wide, vec, pl.BlockSpec((N_CHIPS, D_MODEL, wn), lambda i: (0, 0, 0)), wide] + [tile] * 6,
        out_specs=[wide, vec],
        out_shape=[jax.ShapeDtypeStruct((L, D_MODEL), F32), jax.ShapeDtypeStruct((1, D_MODEL), F32)],
        compiler_params=_cparams(("arbitrary",)),
    )(x, g_pre, w_in_g, d_h1, *dparts)


def _mesh_position():
    x, y, c = lax.axis_index("x"), lax.axis_index("y"), lax.axis_index("c")
    chips = [(1 - x, y), (x, 1 - y), (1 - x, 1 - y)]
    return x, y, c, chips


def _chip_index(cx, cy):
    return 2 * cx + cy


def _hbm_specs(n):
    return [pl.BlockSpec(memory_space=pl.ANY)] * n


def _gather_chips(shards, name):
    n = len(shards)

    def body(*refs):
        ins, outs = refs[:n], refs[n:2 * n]
        send1, recv1, send2, recv2, local = refs[2 * n:]
        x, y, c, chips = _mesh_position()
        me = _chip_index(x, y)
        sibling = (x, y, 1 - c)

        def half(ref, chip, core):
            hr = ref.shape[1] // 2
            return ref.at[chip, pl.ds(core * hr, hr)]

        copies, locals_ = [], []
        for a in range(n):
            lc = pltpu.make_async_copy(ins[a], outs[a].at[me], local.at[a])
            lc.start()
            locals_.append(lc)
            hr = ins[a].shape[0] // 2
            for j, chip in enumerate(chips):
                cp = pltpu.make_async_remote_copy(
                    src_ref=ins[a].at[pl.ds(c * hr, hr)], dst_ref=half(outs[a], me, c),
                    send_sem=send1.at[a, j], recv_sem=recv1.at[a, j], device_id=(*chip, c), device_id_type=MESH)
                cp.start()
                copies.append(cp)
        for a in range(n):
            for j, chip in enumerate(chips):
                landed = half(outs[a], _chip_index(*chip), c)
                pltpu.make_async_remote_copy(
                    src_ref=landed, dst_ref=landed, send_sem=send1.at[a, j], recv_sem=recv1.at[a, j],
                    device_id=(*chip, c), device_id_type=MESH).wait_recv()
                cp = pltpu.make_async_remote_copy(
                    src_ref=landed, dst_ref=landed, send_sem=send2.at[a, j], recv_sem=recv2.at[a, j],
                    device_id=sibling, device_id_type=MESH)
                cp.start()
                copies.append(cp)
        for a in range(n):
            for j, chip in enumerate(chips):
                other = half(outs[a], _chip_index(*chip), 1 - c)
                pltpu.make_async_remote_copy(
                    src_ref=other, dst_ref=other, send_sem=send2.at[a, j], recv_sem=recv2.at[a, j],
                    device_id=sibling, device_id_type=MESH).wait_recv()
        for cp in copies:
            cp.wait_send()
        for lc in locals_:
            lc.wait()

    sem = pltpu.SemaphoreType.DMA
    return pl.pallas_call(
        body, name=name, in_specs=_hbm_specs(n), out_specs=_hbm_specs(n),
        out_shape=[jax.ShapeDtypeStruct((N_CHIPS,) + s.shape, s.dtype) for s in shards],
        scratch_shapes=[sem((n, 3)), sem((n, 3)), sem((n, 3)), sem((n, 3)), sem((n,))],
        compiler_params=pltpu.CompilerParams(has_side_effects=True),
    )(*shards)


def _pair_exchange(grads):
    n = len(grads)

    def body(*refs):
        ins, outs = refs[:n], refs[n:2 * n]
        send, recv = refs[2 * n:]
        x, y, c, _ = _mesh_position()
        copies = []
        for a in range(n):
            hr = ins[a].shape[1] // 2
            cp = pltpu.make_async_remote_copy(
                src_ref=ins[a].at[:, pl.ds((1 - c) * hr, hr)], dst_ref=outs[a],
                send_sem=send.at[a], recv_sem=recv.at[a], device_id=(x, y, 1 - c), device_id_type=MESH)
            cp.start()
            copies.append(cp)
        for cp in copies:
            cp.wait()

    sem = pltpu.SemaphoreType.DMA
    return pl.pallas_call(
        body, name="pair_exchange", in_specs=_hbm_specs(n), out_specs=_hbm_specs(n),
        out_shape=[jax.ShapeDtypeStruct((g.shape[0], g.shape[1] // 2, g.shape[2]), g.dtype) for g in grads],
        scratch_shapes=[sem((n,)), sem((n,))],
        compiler_params=pltpu.CompilerParams(has_side_effects=True),
    )(*grads)


def _pair_add(core, grad, other, tr):
    hr = other.shape[1]
    cdim = other.shape[2]
    nb = hr // tr

    def body(core_ref, g_ref, o_ref, out_ref):
        out_ref[...] = g_ref[...] + o_ref[...]

    return pl.pallas_call(
        body, name="pair_add",
        grid_spec=pltpu.PrefetchScalarGridSpec(
            num_scalar_prefetch=1, grid=(N_CHIPS, nb),
            in_specs=[pl.BlockSpec((1, tr, cdim), lambda j, i, core_ref: (j, core_ref[0] * nb + i, 0)),
                      pl.BlockSpec((1, tr, cdim), lambda j, i, core_ref: (j, i, 0))],
            out_specs=pl.BlockSpec((1, tr, cdim), lambda j, i, core_ref: (j, i, 0))),
        out_shape=jax.ShapeDtypeStruct(other.shape, F32),
        compiler_params=_cparams(("arbitrary", "arbitrary")),
    )(core, grad, other)


def _chip_scatter(parts):
    n = len(parts)

    def body(*refs):
        ins, outs = refs[:n], refs[n:2 * n]
        send, recv, local = refs[2 * n:]
        x, y, c, chips = _mesh_position()
        me = _chip_index(x, y)
        copies = []
        for a in range(n):
            lc = pltpu.make_async_copy(ins[a].at[me], outs[a].at[me], local.at[a])
            lc.start()
            copies.append(lc)
            for j, chip in enumerate(chips):
                cp = pltpu.make_async_remote_copy(
                    src_ref=ins[a].at[_chip_index(*chip)], dst_ref=outs[a].at[me],
                    send_sem=send.at[a, j], recv_sem=recv.at[a, j], device_id=(*chip, c), device_id_type=MESH)
                cp.start()
                copies.append(cp)
        for cp in copies:
            cp.wait()

    sem = pltpu.SemaphoreType.DMA
    return pl.pallas_call(
        body, name="chip_scatter", in_specs=_hbm_specs(n), out_specs=_hbm_specs(n),
        out_shape=[jax.ShapeDtypeStruct(p.shape, p.dtype) for p in parts],
        scratch_shapes=[sem((n, 3)), sem((n, 3)), sem((n,))],
        compiler_params=pltpu.CompilerParams(has_side_effects=True),
    )(*parts)


def _chip_add(recv, tr):
    hr, cdim = recv.shape[1], recv.shape[2]

    def body(r_ref, out_ref):
        out_ref[...] = ((r_ref[0] + r_ref[1]) + r_ref[2]) + r_ref[3]

    return pl.pallas_call(
        body, name="chip_add", grid=(hr // tr,),
        in_specs=[pl.BlockSpec((N_CHIPS, tr, cdim), lambda i: (0, i, 0))],
        out_specs=pl.BlockSpec((tr, cdim), lambda i: (i, 0)),
        out_shape=jax.ShapeDtypeStruct((hr, cdim), F32),
        compiler_params=_cparams(("arbitrary",)),
    )(recv)


def _pair_gather(halves):
    n = len(halves)

    def body(*refs):
        ins, outs = refs[:n], refs[n:2 * n]
        send, recv, local = refs[2 * n:]
        x, y, c, _ = _mesh_position()
        copies = []
        for a in range(n):
            hr = ins[a].shape[0]
            mine = outs[a].at[pl.ds(c * hr, hr)]
            lc = pltpu.make_async_copy(ins[a], mine, local.at[a])
            lc.start()
            copies.append(lc)
            cp = pltpu.make_async_remote_copy(
                src_ref=ins[a], dst_ref=mine, send_sem=send.at[a], recv_sem=recv.at[a],
                device_id=(x, y, 1 - c), device_id_type=MESH)
            cp.start()
            copies.append(cp)
        for cp in copies:
            cp.wait()

    sem = pltpu.SemaphoreType.DMA
    return pl.pallas_call(
        body, name="pair_gather", in_specs=_hbm_specs(n), out_specs=_hbm_specs(n),
        out_shape=[jax.ShapeDtypeStruct((2 * h.shape[0], h.shape[1]), h.dtype) for h in halves],
        scratch_shapes=[sem((n,)), sem((n,)), sem((n,))],
        compiler_params=pltpu.CompilerParams(has_side_effects=True),
    )(*halves)


def _row_tile(rows):
    for t in (512, 256, 128, 64, 32, 16, 8):
        if rows % t == 0:
            return t
    raise ValueError(rows)


def _reduce_scatter(grads):
    core = lax.axis_index("c").astype(jnp.int32).reshape(1)
    others = _pair_exchange(grads)
    pair = [_pair_add(core, g, o, _row_tile(o.shape[1])) for g, o in zip(grads, others)]
    landed = _chip_scatter(pair)
    halves = [_chip_add(r, _row_tile(r.shape[1])) for r in landed]
    return _pair_gather(halves)


def _adamw(w, g, m, v):
    rows, cols = w.shape
    tr = _row_tile(rows) if rows % 8 == 0 else rows

    def body(w_ref, g_ref, m_ref, v_ref, d_ref, nm_ref, nv_ref):
        g_ = g_ref[...]
        m_ = ADAM_B1 * m_ref[...] + (1.0 - ADAM_B1) * g_
        v_ = ADAM_B2 * v_ref[...] + (1.0 - ADAM_B2) * (g_ * g_)
        m_hat = m_ / (1.0 - ADAM_B1 ** ADAM_STEP)
        v_hat = v_ / (1.0 - ADAM_B2 ** ADAM_STEP)
        d_ref[...] = -ADAM_LR * (m_hat / (jnp.sqrt(v_hat) + ADAM_EPS) + ADAM_WD * w_ref[...])
        nm_ref[...] = m_
        nv_ref[...] = v_

    spec = pl.BlockSpec((tr, cols), lambda i: (i, 0))
    shp = jax.ShapeDtypeStruct((rows, cols), F32)
    return pl.pallas_call(
        body, name="adamw", grid=(rows // tr,), in_specs=[spec] * 4, out_specs=[spec] * 3,
        out_shape=[shp] * 3, compiler_params=_cparams(("arbitrary",)),
    )(w, g, m, v)


_SMALL = ["norm_pre", "norm_post", "ssm_a_re", "ssm_a_im", "ssm_log_dt", "ssm_b_re", "ssm_b_im",
          "ssm_c_re", "ssm_c_im", "ssm_d", "b_glu", "na_rpb", "ple_norm"]
_BIG = ["w_in", "w_glu", "w_out", "w_ple", "w_ple_gate"]
_WEIGHTS = ["norm_pre", "norm_post", "w_in", "ssm_a_re", "ssm_a_im", "ssm_log_dt", "ssm_b_re", "ssm_b_im",
            "ssm_c_re", "ssm_c_im", "ssm_d", "w_glu", "b_glu", "na_rpb", "w_out", "w_ple", "ple_norm", "w_ple_gate"]
_SMALL_ROWS = 2176


def _pack_small(tensors):
    flat = jnp.concatenate([tensors[n].reshape(-1) for n in _SMALL])
    flat = jnp.pad(flat, (0, _SMALL_ROWS * 128 - flat.shape[0]))
    return flat.reshape(_SMALL_ROWS, 128)


def _unpack_small(packed, shapes):
    flat = packed.reshape(-1)
    out, off = {}, 0
    for n in _SMALL:
        size = int(np.prod(shapes[n]))
        out[n] = flat[off:off + size].reshape(shapes[n])
        off += size
    return out


def _local_grads(x, p, target, wts, w_in_g, w_glu, w_out, w_ple_g, w_pg):
    ssm_names = ["ssm_a_re", "ssm_a_im", "ssm_log_dt", "ssm_b_re", "ssm_b_im", "ssm_c_re", "ssm_c_im", "ssm_d"]
    ssm_params = [wts[n][0] for n in ssm_names]
    blk, blk_vjp = jax.vjp(_ssm_block_params, *ssm_params)
    m_mat, ws_mat, wot_mat, a16 = _ssm_chunk_matrices(blk)
    seq = x.shape[0]
    bias_tab, bias_vjp = jax.vjp(functools.partial(_na_bias_table, rows=seq // GRID_W), wts["na_rpb"][0])

    u_s, z_s, q, k, v, z_n = _in_proj(x, wts["norm_pre"], w_in_g)
    u_c = _to_chunks(u_s, BF16)
    s_in = _block_matmul([(u_c, ws_mat, False)], "ssm_chunk_states")
    s_prev = _ssm_state_scan(s_in, a16)
    y_ssm = _from_chunks(_block_matmul([(u_c, m_mat, False), (s_prev, wot_mat, True)], "ssm_chunk_out"))
    q_t, k_h = _heads_major_t(q, BF16), _heads_major(k, BF16)
    y_na_t = _na_fwd(q_t, k_h, _heads_major_t(v, BF16), bias_tab)
    y_na = y_na_t.transpose(2, 0, 1).reshape(seq, D_NA)
    cat =_branch_fwd(y_ssm, z_s, y_na, z_n, w_glu, wts["b_glu"])

    (loss, d_h1, d_cat, d_w_out, d_g_post, d_w_ple, d_g_ple, d_w_pg) = _head(
        x, p, target, cat, w_out, wts["norm_post"], w_ple_g, wts["ple_norm"], w_pg)
    d_y_ssm, d_z_s, d_y_na, d_z_n, d_w_glu, d_b_glu = _branch_bwd(
        y_ssm, z_s, y_na, z_n, w_glu, wts["b_glu"], d_cat)
    d_q_t, d_k_h, d_v_h, d_bias_tab = _na_bwd(
        q_t, _heads_major(q, BF16), _heads_major_t(k, BF16), k_h, _heads_major(v, BF16), bias_tab, y_na_t,
        _heads_major_t(d_y_na, F32), _heads_major(d_y_na, BF16))
    d_q = d_q_t.transpose(2, 0, 1).reshape(seq, D_NA)
    d_k = d_k_h.transpose(1, 0, 2).reshape(seq, D_NA)
    d_v = d_v_h.transpose(1, 0, 2).reshape(seq, D_NA)

    dy_c = _to_chunks(d_y_ssm, BF16)
    d_prev = _block_matmul([(dy_c, wot_mat, False)], "ssm_bwd_states")
    g_st, d_a16 = _ssm_state_scan_bwd(d_prev, s_prev, a16)
    d_u = _from_chunks(_block_matmul([(dy_c, m_mat, True), (g_st, ws_mat, True)], "ssm_bwd_in", out_dtype=BF16))
    d_m = _block_matmul_tn(u_c, dy_c, "ssm_grad_m")
    d_ws = _block_matmul_tn(u_c, g_st, "ssm_grad_ws")
    d_wot = _block_matmul_tn(dy_c, s_prev, "ssm_grad_wot")
    d_ssm = blk_vjp(tuple(_ssm_chunk_matrices_bwd(blk, d_m, d_ws, d_wot, d_a16)))
    (d_rpb,) = bias_vjp(d_bias_tab)

    dparts = [d_u, d_z_s, d_q, d_k, d_v, d_z_n]
    d_w_in = _in_proj_bwd_w(x, wts["norm_pre"], dparts)
    grad_x, d_g_pre = _in_proj_bwd_x(x, wts["norm_pre"], w_in_g, d_h1, dparts)

    small = {"norm_pre": d_g_pre, "norm_post": d_g_post, "b_glu": d_b_glu, "na_rpb": d_rpb, "ple_norm": d_g_ple}
    for n, g in zip(ssm_names, d_ssm):
        small[n] = g
    big = {"w_in": d_w_in, "w_glu": d_w_glu.reshape(N_CHIPS, 128, 512), "w_out": d_w_out.reshape(N_CHIPS, 256, 1024),
           "w_ple": d_w_ple, "w_ple_gate": d_w_pg.reshape(N_CHIPS, 256, 1024)}
    return loss, grad_x, small, big


def kernel(x, p, norm_pre, norm_post, w_in, ssm_a_re, ssm_a_im, ssm_log_dt, ssm_b_re, ssm_b_im, ssm_c_re, ssm_c_im, ssm_d, w_glu, b_glu, na_rpb, w_out, w_ple, ple_norm, w_ple_gate, loss_target, m_norm_pre, m_norm_post, m_w_in, m_ssm_a_re, m_ssm_a_im, m_ssm_log_dt, m_ssm_b_re, m_ssm_b_im, m_ssm_c_re, m_ssm_c_im, m_ssm_d, m_w_glu, m_b_glu, m_na_rpb, m_w_out, m_w_ple, m_ple_norm, m_w_ple_gate, v_norm_pre, v_norm_post, v_w_in, v_ssm_a_re, v_ssm_a_im, v_ssm_log_dt, v_ssm_b_re, v_ssm_b_im, v_ssm_c_re, v_ssm_c_im, v_ssm_d, v_w_glu, v_b_glu, v_na_rpb, v_w_out, v_w_ple, v_ple_norm, v_w_ple_gate):
    wts = dict(norm_pre=norm_pre, norm_post=norm_post, w_in=w_in, ssm_a_re=ssm_a_re, ssm_a_im=ssm_a_im,
               ssm_log_dt=ssm_log_dt, ssm_b_re=ssm_b_re, ssm_b_im=ssm_b_im, ssm_c_re=ssm_c_re, ssm_c_im=ssm_c_im,
               ssm_d=ssm_d, w_glu=w_glu, b_glu=b_glu, na_rpb=na_rpb, w_out=w_out, w_ple=w_ple, ple_norm=ple_norm,
               w_ple_gate=w_ple_gate)
    mom_m = dict(norm_pre=m_norm_pre, norm_post=m_norm_post, w_in=m_w_in, ssm_a_re=m_ssm_a_re, ssm_a_im=m_ssm_a_im,
                 ssm_log_dt=m_ssm_log_dt, ssm_b_re=m_ssm_b_re, ssm_b_im=m_ssm_b_im, ssm_c_re=m_ssm_c_re,
                 ssm_c_im=m_ssm_c_im, ssm_d=m_ssm_d, w_glu=m_w_glu, b_glu=m_b_glu, na_rpb=m_na_rpb, w_out=m_w_out,
                 w_ple=m_w_ple, ple_norm=m_ple_norm, w_ple_gate=m_w_ple_gate)
    mom_v = dict(norm_pre=v_norm_pre, norm_post=v_norm_post, w_in=v_w_in, ssm_a_re=v_ssm_a_re, ssm_a_im=v_ssm_a_im,
                 ssm_log_dt=v_ssm_log_dt, ssm_b_re=v_ssm_b_re, ssm_b_im=v_ssm_b_im, ssm_c_re=v_ssm_c_re,
                 ssm_c_im=v_ssm_c_im, ssm_d=v_ssm_d, w_glu=v_w_glu, b_glu=v_b_glu, na_rpb=v_na_rpb, w_out=v_w_out,
                 w_ple=v_w_ple, ple_norm=v_ple_norm, w_ple_gate=v_w_ple_gate)

    shards = [wts[n][0].astype(BF16) for n in _BIG]
    w_in_g, w_glu_g, w_out_g, w_ple_g, w_pg_g = _gather_chips(shards, "gather_weights")
    loss_part, grad_x, small, big = _local_grads(
        x[0], p[0, 0], loss_target[0], wts, w_in_g, w_glu_g.reshape(512, 512), w_out_g.reshape(1024, 1024),
        w_ple_g, w_pg_g.reshape(1024, 1024))
    loss = lax.psum(loss_part[0, 0], ("x", "y", "c"))

    small_packed = _pack_small(small).reshape(N_CHIPS, _SMALL_ROWS // N_CHIPS, 128)
    reduced = _reduce_scatter([big[n] for n in _BIG] + [small_packed])
    grads = dict(zip(_BIG, reduced[:-1]))
    (small_all,) = _gather_chips([reduced[-1]], "gather_small_grads")
    small_all = small_all.reshape(_SMALL_ROWS, 128)

    delta, new_m, new_v = {}, {}, {}
    for n in _BIG:
        shp = wts[n].shape
        d_, m_, v_ = _adamw(wts[n][0], grads[n], mom_m[n][0], mom_v[n][0])
        grads[n] = grads[n].reshape(shp)
        delta[n], new_m[n], new_v[n] = d_.reshape(shp), m_.reshape(shp), v_.reshape(shp)
    shapes = {n: wts[n].shape for n in _SMALL}
    d_s, m_s, v_s = _adamw(_pack_small(wts), small_all, _pack_small(mom_m), _pack_small(mom_v))
    for dst, packed in ((grads, small_all), (delta, d_s), (new_m, m_s), (new_v, v_s)):
        dst.update(_unpack_small(packed, shapes))

    return (loss, grad_x[None], *[grads[n] for n in _WEIGHTS], *[delta[n] for n in _WEIGHTS],
            *[new_m[n] for n in _WEIGHTS], *[new_v[n] for n in _WEIGHTS])
```

```python
import functools
import math

import jax
import jax.numpy as jnp
import numpy as np
from jax import lax
from jax.experimental import pallas as pl
from jax.experimental.pallas import tpu as pltpu

F32 = jnp.float32
BF16 = jnp.bfloat16

D_MODEL = 1024
D_PLE = 256
GRID_W = 64
D_SSM = 512
SSM_GROUP = 16
N_GROUPS = 32
SSM_STATE = 64
D_NA = 512
NA_HEADS = 8
NA_HEAD_DIM = 64
NA_ROWS = 8
NA_COLS = 16
D_IN_PROJ = 3072
EPS = 1e-6

CHUNK = 16
GROUPS_PER_BLOCK = 8
N_BLOCKS = N_GROUPS // GROUPS_PER_BLOCK
BLOCK_CH = GROUPS_PER_BLOCK * SSM_GROUP
BLOCK_ST = GROUPS_PER_BLOCK * SSM_STATE
CHUNK_W = CHUNK * BLOCK_CH
STATE_W = 4 * BLOCK_ST

N_CHIPS = 4
MESH = pl.DeviceIdType.MESH

ADAM_LR = 0.001
ADAM_B1 = 0.9
ADAM_B2 = 0.999
ADAM_EPS = 1e-08
ADAM_WD = 0.01
ADAM_STEP = 10

VMEM_LIMIT = 52 * 1024 * 1024
HIGHEST = lax.Precision.HIGHEST


def _cparams(sem=None, **kw):
    if sem is not None:
        kw["dimension_semantics"] = sem
    return pltpu.CompilerParams(vmem_limit_bytes=VMEM_LIMIT, **kw)


def _dot(a, b, dims=((1,), (0,))):
    return lax.dot_general(a, b, (dims, ((), ())), preferred_element_type=F32)


def _dot_nt(a, b):
    return _dot(a, b, ((1,), (1,)))


def _dot_tn(a, b):
    return _dot(a, b, ((0,), (0,)))


def _sigmoid(x):
    return 1.0 / (1.0 + jnp.exp(-x))


_GELU_C = math.sqrt(2.0 / math.pi)


def _gelu_parts(x):
    inner = _GELU_C * (x + 0.044715 * (x * x * x))
    t = jnp.tanh(inner)
    return 0.5 * x * (1.0 + t), t


def _gelu_grad(x, t):
    return 0.5 * (1.0 + t) + 0.5 * x * (1.0 - t * t) * (_GELU_C * (1.0 + 3.0 * 0.044715 * x * x))


def _silu_parts(z):
    s = _sigmoid(z)
    return z * s, s


def _silu_grad(z, s):
    return s * (1.0 + z * (1.0 - s))


def _rms(x):
    r = lax.rsqrt(jnp.mean(x * x, axis=-1, keepdims=True) + EPS)
    return x * r, r


def _rms_bwd(dn, n, r):
    return r * (dn - n * jnp.mean(dn * n, axis=-1, keepdims=True))


def _chunk_scratch(tm):
    return pltpu.VMEM((N_BLOCKS, tm, BLOCK_CH), F32)


def _store_chunks(val, scr, c_ref, dtype):
    nc = scr.shape[1] // CHUNK
    for b in range(N_BLOCKS):
        scr[b] = val[:, b * BLOCK_CH:(b + 1) * BLOCK_CH]
        for j in range(CHUNK):
            c_ref[b, :, j * BLOCK_CH:(j + 1) * BLOCK_CH] = scr[b, pl.ds(j, nc, stride=CHUNK), :].astype(dtype)


def _load_chunks(c_ref, scr):
    nc = scr.shape[1] // CHUNK
    for b in range(N_BLOCKS):
        for j in range(CHUNK):
            scr[b, pl.ds(j, nc, stride=CHUNK), :] = c_ref[b, :, j * BLOCK_CH:(j + 1) * BLOCK_CH].astype(F32)
    return jnp.concatenate([scr[b] for b in range(N_BLOCKS)], axis=1)


def _chunk_spec(tm):
    return pl.BlockSpec((N_BLOCKS, tm // CHUNK, CHUNK_W), lambda i: (0, i, 0))


def _heads_spec(tm):
    return pl.BlockSpec((NA_HEADS, tm, NA_HEAD_DIM), lambda i: (0, i, 0))


def _heads_t_spec(tm):
    return pl.BlockSpec((D_NA, tm), lambda i: (0, i))


def _store_heads(val, h_ref, dtype):
    for h in range(NA_HEADS):
        h_ref[h] = val[:, h * NA_HEAD_DIM:(h + 1) * NA_HEAD_DIM].astype(dtype)


def _load_heads(h_ref):
    return jnp.concatenate([h_ref[h] for h in range(NA_HEADS)], axis=1)


def _in_proj(x, g_pre, w_in_g, tm=256):
    L = x.shape[0]
    wn = w_in_g.shape[2]

    def body(x_ref, g_ref, w_ref, uc_ref, zs_ref, qt_ref, qh_ref, kt_ref, kh_ref, vt_ref, vh_ref, zn_ref, u_scr):
        n, _ = _rms(x_ref[...])
        hn = (n * g_ref[...]).astype(BF16)
        proj = jnp.concatenate([_dot(hn, w_ref[j]) for j in range(N_CHIPS)], axis=1)
        _store_chunks(proj[:, 0:512], u_scr, uc_ref, BF16)
        zs_ref[...] = proj[:, 512:1024]
        q = proj[:, 1024:1536] * (NA_HEAD_DIM ** -0.5)
        for val, t_ref, h_ref in ((q, qt_ref, qh_ref), (proj[:, 1536:2048], kt_ref, kh_ref), (proj[:, 2048:2560], vt_ref, vh_ref)):
            t_ref[...] = val.T.astype(BF16)
            _store_heads(val, h_ref, BF16)
        zn_ref[...] = proj[:, 2560:3072]

    tok = jax.ShapeDtypeStruct((L, 512), F32)
    tr = jax.ShapeDtypeStruct((D_NA, L), BF16)
    hm = jax.ShapeDtypeStruct((NA_HEADS, L, NA_HEAD_DIM), BF16)
    tspec = pl.BlockSpec((tm, 512), lambda i: (i, 0))
    return pl.pallas_call(
        body, name="in_proj", grid=(L // tm,),
        in_specs=[pl.BlockSpec((tm, D_MODEL), lambda i: (i, 0)),
                  pl.BlockSpec((1, D_MODEL), lambda i: (0, 0)),
                  pl.BlockSpec((N_CHIPS, D_MODEL, wn), lambda i: (0, 0, 0))],
        out_specs=[_chunk_spec(tm), tspec] + [_heads_t_spec(tm), _heads_spec(tm)] * 3 + [tspec],
        out_shape=[jax.ShapeDtypeStruct((N_BLOCKS, L // CHUNK, CHUNK_W), BF16), tok, tr, hm, tr, hm, tr, hm, tok],
        scratch_shapes=[_chunk_scratch(tm)],
        compiler_params=_cparams(("arbitrary",)),
    )(x, g_pre, w_in_g)


def _ssm_block_params(a_re, a_im, log_dt, b_re, b_im, c_re, c_im, d):
    eye_g = jnp.eye(GROUPS_PER_BLOCK, dtype=F32)[None, None, :, None, :, None]

    def lanes(t):
        return t.reshape(2, N_BLOCKS, 1, BLOCK_ST)

    def expand(t):
        return (t[:, :, :, :, None, :] * eye_g).reshape(2, N_BLOCKS, BLOCK_CH, BLOCK_ST)

    b_shape = (2, N_BLOCKS, GROUPS_PER_BLOCK, SSM_STATE, SSM_GROUP)
    c_shape = (2, N_BLOCKS, GROUPS_PER_BLOCK, SSM_GROUP, SSM_STATE)
    return (lanes(a_re), lanes(a_im), lanes(jnp.broadcast_to(log_dt[..., None], a_re.shape)),
            expand(b_re.reshape(b_shape).transpose(0, 1, 2, 4, 3)), expand(b_im.reshape(b_shape).transpose(0, 1, 2, 4, 3)),
            expand(c_re.reshape(c_shape)), expand(c_im.reshape(c_shape)), d.reshape(N_BLOCKS, 1, BLOCK_CH))


def _ssm_discretise(ar, ai, ldt):
    dt = jnp.exp(ldt)
    mag = jnp.exp(dt * ar)
    abr = mag * jnp.cos(dt * ai)
    abi = mag * jnp.sin(dt * ai)
    num_re = abr - 1.0
    num_im = abi
    denom = ar * ar + ai * ai
    coef_re = (num_re * ar + num_im * ai) / denom
    coef_im = (num_im * ar - num_re * ai) / denom
    return abr, abi, coef_re, coef_im


_POW_ROWS = 24


def _ssm_fill_powers(ar_ref, ai_ref, ldt_ref, br_ref, bi_ref, pw_ref, bbar_ref):
    for d in range(2):
        abr, abi, cfr, cfi = _ssm_discretise(ar_ref[d, 0], ai_ref[d, 0], ldt_ref[d, 0])
        bbar_ref[d, 0] = cfr * br_ref[d, 0] - cfi * bi_ref[d, 0]
        bbar_ref[d, 1] = cfr * bi_ref[d, 0] + cfi * br_ref[d, 0]
        pr, pi = jnp.ones_like(abr), jnp.zeros_like(abi)
        for t in range(CHUNK + 1):
            pw_ref[d, 0, t:t + 1, :] = pr
            pw_ref[d, 1, t:t + 1, :] = pi
            pr, pi = pr * abr - pi * abi, pr * abi + pi * abr


def _dot_hi(a, b, dims=((1,), (0,))):
    return lax.dot_general(a, b, (dims, ((), ())), precision=HIGHEST, preferred_element_type=F32)


def _eye(n):
    return (lax.broadcasted_iota(jnp.int32, (n, n), 0) == lax.broadcasted_iota(jnp.int32, (n, n), 1)).astype(F32)


def _ssm_param_specs():
    vec = pl.BlockSpec((2, 1, 1, BLOCK_ST), lambda b, j: (0, b, 0, 0))
    mat = pl.BlockSpec((2, 1, BLOCK_CH, BLOCK_ST), lambda b, j: (0, b, 0, 0))
    return [vec, vec, vec, mat, mat, mat, mat, pl.BlockSpec((1, 1, BLOCK_CH), lambda b, j: (b, 0, 0))]


def _ssm_chunk_matrices(blk):
    def body(ar_ref, ai_ref, ldt_ref, br_ref, bi_ref, cr_ref, ci_ref, d_ref,
             m_ref, ws_ref, wot_ref, a16_ref, pw_ref, bbar_ref, lag_ref):
        j = pl.program_id(1)

        @pl.when(j == 0)
        def _():
            _ssm_fill_powers(ar_ref, ai_ref, ldt_ref, br_ref, bi_ref, pw_ref, bbar_ref)
            zero_lag = d_ref[0] * _eye(BLOCK_CH)
            for d in range(2):
                for t in range(CHUNK):
                    pr, pi = pw_ref[d, 0, t:t + 1, :], pw_ref[d, 1, t:t + 1, :]
                    xr = bbar_ref[d, 0] * pr - bbar_ref[d, 1] * pi
                    xi = bbar_ref[d, 0] * pi + bbar_ref[d, 1] * pr
                    tap = _dot_hi(xr, cr_ref[d, 0], ((1,), (1,))) - _dot_hi(xi, ci_ref[d, 0], ((1,), (1,)))
                    if t == 0:
                        zero_lag = zero_lag + tap
                    else:
                        lag_ref[CHUNK - 1 + t if d == 0 else CHUNK - 1 - t] = tap
            lag_ref[CHUNK - 1] = zero_lag
            a16_ref[0] = jnp.concatenate([pw_ref[d, ri, CHUNK:CHUNK + 1, :] for d in range(2) for ri in range(2)], axis=1)

        m_ref[0] = jnp.concatenate([lag_ref[jp - j + CHUNK - 1] for jp in range(CHUNK)], axis=1).astype(BF16)

        def power(d, t):
            return pw_ref[d, 0, pl.ds(t, 1), :], pw_ref[d, 1, pl.ds(t, 1), :]

        parts = []
        for d, t in ((0, CHUNK - 1 - j), (1, j)):
            pr, pi = power(d, t)
            parts += [bbar_ref[d, 0] * pr - bbar_ref[d, 1] * pi, bbar_ref[d, 0] * pi + bbar_ref[d, 1] * pr]
        ws_ref[0] = jnp.concatenate(parts, axis=1).astype(BF16)
        parts = []
        for d, t in ((0, j + 1), (1, CHUNK - j)):
            pr, pi = power(d, t)
            parts += [cr_ref[d, 0] * pr - ci_ref[d, 0] * pi, -cr_ref[d, 0] * pi - ci_ref[d, 0] * pr]
        wot_ref[0] = jnp.concatenate(parts, axis=1).astype(BF16)

    row = pl.BlockSpec((1, BLOCK_CH, CHUNK_W), lambda b, j: (b, j, 0))
    mat = jax.ShapeDtypeStruct((N_BLOCKS, CHUNK_W, CHUNK_W), BF16)
    return pl.pallas_call(
        body, name="ssm_chunk_matrices", grid=(N_BLOCKS, CHUNK),
        in_specs=_ssm_param_specs(),
        out_specs=[row, row, row, pl.BlockSpec((1, 1, STATE_W), lambda b, j: (b, 0, 0))],
        out_shape=[mat, mat, mat, jax.ShapeDtypeStruct((N_BLOCKS, 1, STATE_W), F32)],
        scratch_shapes=[pltpu.VMEM((2, 2, _POW_ROWS, BLOCK_ST), F32), pltpu.VMEM((2, 2, BLOCK_CH, BLOCK_ST), F32),
                        pltpu.VMEM((2 * CHUNK, BLOCK_CH, BLOCK_CH), F32)],
        compiler_params=_cparams(("arbitrary", "arbitrary")),
    )(*blk)


def _ssm_chunk_matrices_bwd(blk, d_m, d_ws, d_wot, d_a16):
    def body(ar_ref, ai_ref, ldt_ref, br_ref, bi_ref, cr_ref, ci_ref, d_ref, dm_ref, dws_ref, dwot_ref, da16_ref,
             dar_ref, dai_ref, dldt_ref, dbr_ref, dbi_ref, dcr_ref, dci_ref, dd_ref,
             pw_ref, bbar_ref, dlag_ref, dbbar_ref, dc_ref, dpw_ref):
        j = pl.program_id(1)
        w = BLOCK_ST

        @pl.when(j == 0)
        def _():
            _ssm_fill_powers(ar_ref, ai_ref, ldt_ref, br_ref, bi_ref, pw_ref, bbar_ref)
            for r in (dlag_ref, dbbar_ref, dc_ref, dpw_ref):
                r[...] = jnp.zeros_like(r)

        def x_chain(d, t, dxr, dxi):
            pr, pi = pw_ref[d, 0, pl.ds(t, 1), :], pw_ref[d, 1, pl.ds(t, 1), :]
            bbr, bbi = bbar_ref[d, 0], bbar_ref[d, 1]
            dbbar_ref[d, 0] += dxr * pr + dxi * pi
            dbbar_ref[d, 1] += dxi * pr - dxr * pi
            dpw_ref[d, 0, pl.ds(t, 1), :] += jnp.sum(dxr * bbr + dxi * bbi, axis=0, keepdims=True)
            dpw_ref[d, 1, pl.ds(t, 1), :] += jnp.sum(dxi * bbr - dxr * bbi, axis=0, keepdims=True)

        def z_chain(d, t, dzr, dzi):
            pr, pi = pw_ref[d, 0, pl.ds(t, 1), :], pw_ref[d, 1, pl.ds(t, 1), :]
            c_r, c_i = cr_ref[d, 0], ci_ref[d, 0]
            dc_ref[d, 0] += dzr * pr - dzi * pi
            dc_ref[d, 1] += -dzr * pi - dzi * pr
            dpw_ref[d, 0, pl.ds(t, 1), :] += jnp.sum(dzr * c_r - dzi * c_i, axis=0, keepdims=True)
            dpw_ref[d, 1, pl.ds(t, 1), :] += jnp.sum(-dzr * c_i - dzi * c_r, axis=0, keepdims=True)

        for jp in range(CHUNK):
            dlag_ref[jp - j + CHUNK - 1] += dm_ref[0, :, jp * BLOCK_CH:(jp + 1) * BLOCK_CH]
        x_chain(0, CHUNK - 1 - j, dws_ref[0, :, 0:w], dws_ref[0, :, w:2 * w])
        x_chain(1, j, dws_ref[0, :, 2 * w:3 * w], dws_ref[0, :, 3 * w:4 * w])
        z_chain(0, j + 1, dwot_ref[0, :, 0:w], dwot_ref[0, :, w:2 * w])
        z_chain(1, CHUNK - j, dwot_ref[0, :, 2 * w:3 * w], dwot_ref[0, :, 3 * w:4 * w])

        @pl.when(j == CHUNK - 1)
        def _():
            for d in range(2):
                for t in range(CHUNK):
                    d_tap = dlag_ref[CHUNK - 1 + t if d == 0 else CHUNK - 1 - t]
                    pr, pi = pw_ref[d, 0, t:t + 1, :], pw_ref[d, 1, t:t + 1, :]
                    xr = bbar_ref[d, 0] * pr - bbar_ref[d, 1] * pi
                    xi = bbar_ref[d, 0] * pi + bbar_ref[d, 1] * pr
                    dc_ref[d, 0] += _dot_hi(d_tap, xr, ((0,), (0,)))
                    dc_ref[d, 1] -= _dot_hi(d_tap, xi, ((0,), (0,)))
                    x_chain(d, t, _dot_hi(d_tap, cr_ref[d, 0]), -_dot_hi(d_tap, ci_ref[d, 0]))
            dd_ref[0] = jnp.sum(dlag_ref[CHUNK - 1] * _eye(BLOCK_CH), axis=0, keepdims=True)
            for d in range(2):
                (abr, abi, cfr, cfi), disc_vjp = jax.vjp(_ssm_discretise, ar_ref[d, 0], ai_ref[d, 0], ldt_ref[d, 0])
                dpr = dpw_ref[d, 0, CHUNK:CHUNK + 1, :] + da16_ref[0, :, 2 * d * w:(2 * d + 1) * w]
                dpi = dpw_ref[d, 1, CHUNK:CHUNK + 1, :] + da16_ref[0, :, (2 * d + 1) * w:(2 * d + 2) * w]
                dabr, dabi = jnp.zeros_like(abr), jnp.zeros_like(abi)
                for t in range(CHUNK, 0, -1):
                    qr, qi = pw_ref[d, 0, t - 1:t, :], pw_ref[d, 1, t - 1:t, :]
                    dabr = dabr + dpr * qr + dpi * qi
                    dabi = dabi + dpi * qr - dpr * qi
                    dpr, dpi = (dpr * abr + dpi * abi + dpw_ref[d, 0, t - 1:t, :],
                                dpi * abr - dpr * abi + dpw_ref[d, 1, t - 1:t, :])
                dbbr, dbbi = dbbar_ref[d, 0], dbbar_ref[d, 1]
                b_r, b_i = br_ref[d, 0], bi_ref[d, 0]
                dbr_ref[d, 0] = cfr * dbbr + cfi * dbbi
                dbi_ref[d, 0] = cfr * dbbi - cfi * dbbr
                dcfr = jnp.sum(b_r * dbbr + b_i * dbbi, axis=0, keepdims=True)
                dcfi = jnp.sum(b_r * dbbi - b_i * dbbr, axis=0, keepdims=True)
                dar_ref[d, 0], dai_ref[d, 0], dldt_ref[d, 0] = disc_vjp((dabr, dabi, dcfr, dcfi))
                dcr_ref[d, 0] = dc_ref[d, 0]
                dci_ref[d, 0] = dc_ref[d, 1]

    row = pl.BlockSpec((1, BLOCK_CH, CHUNK_W), lambda b, j: (b, j, 0))
    specs = _ssm_param_specs()
    acc = lambda *s: pltpu.VMEM(s, F32)
    return pl.pallas_call(
        body, name="ssm_chunk_matrices_bwd", grid=(N_BLOCKS, CHUNK),
        in_specs=specs + [row, row, row, pl.BlockSpec((1, 1, STATE_W), lambda b, j: (b, 0, 0))],
        out_specs=specs,
        out_shape=[jax.ShapeDtypeStruct(t.shape, F32) for t in blk],
        scratch_shapes=[acc(2, 2, _POW_ROWS, BLOCK_ST), acc(2, 2, BLOCK_CH, BLOCK_ST), acc(2 * CHUNK, BLOCK_CH, BLOCK_CH),
                        acc(2, 2, BLOCK_CH, BLOCK_ST), acc(2, 2, BLOCK_CH, BLOCK_ST), acc(2, 2, _POW_ROWS, BLOCK_ST)],
        compiler_params=_cparams(("arbitrary", "arbitrary")),
    )(*blk, d_m, d_ws, d_wot, d_a16)


def _block_matmul(terms, name, out_dtype=F32, tn=1024):
    nc = terms[0][0].shape[1]
    n_out = terms[0][1].shape[1] if terms[0][2] else terms[0][1].shape[2]
    flags = [t[2] for t in terms]

    def body(*refs):
        out_ref = refs[-1]
        acc = None
        for t, transposed in enumerate(flags):
            a = refs[2 * t][0].astype(BF16)
            w = refs[2 * t + 1][0]
            part = _dot_nt(a, w) if transposed else _dot(a, w)
            acc = part if acc is None else acc + part
        out_ref[0] = acc.astype(out_dtype)

    in_specs, args = [], []
    for a, w, transposed in terms:
        k = a.shape[2]
        in_specs.append(pl.BlockSpec((1, nc, k), lambda b, n: (b, 0, 0)))
        if transposed:
            in_specs.append(pl.BlockSpec((1, tn, k), lambda b, n: (b, n, 0)))
        else:
            in_specs.append(pl.BlockSpec((1, k, tn), lambda b, n: (b, 0, n)))
        args += [a, w]
    return pl.pallas_call(
        body, name=name, grid=(N_BLOCKS, n_out // tn), in_specs=in_specs,
        out_specs=pl.BlockSpec((1, nc, tn), lambda b, n: (b, 0, n)),
        out_shape=jax.ShapeDtypeStruct((N_BLOCKS, nc, n_out), out_dtype),
        compiler_params=_cparams(("arbitrary", "arbitrary")),
    )(*args)


def _block_matmul_tn(a, b, name, tile=1024):
    nc, m = a.shape[1], a.shape[2]
    n = b.shape[2]

    def body(a_ref, b_ref, out_ref):
        out_ref[0] = _dot_tn(a_ref[0].astype(BF16), b_ref[0].astype(BF16))

    return pl.pallas_call(
        body, name=name, grid=(N_BLOCKS, m // tile, n // tile),
        in_specs=[pl.BlockSpec((1, nc, tile), lambda blk, i, j: (blk, 0, i)),
                  pl.BlockSpec((1, nc, tile), lambda blk, i, j: (blk, 0, j))],
        out_specs=pl.BlockSpec((1, tile, tile), lambda blk, i, j: (blk, i, j)),
        out_shape=jax.ShapeDtypeStruct((N_BLOCKS, m, n), F32),
        compiler_params=_cparams(("arbitrary", "arbitrary", "arbitrary")),
    )(a, b)


def _cmul(ar, ai, xr, xi):
    return ar * xr - ai * xi, ar * xi + ai * xr


def _cmul_conj(ar, ai, xr, xi):
    return ar * xr + ai * xi, ar * xi - ai * xr


def _ssm_state_scan(s_in, a16):
    nc = s_in.shape[1]
    w = BLOCK_ST

    def body(sin_ref, a_ref, out_ref):
        a = a_ref[0]
        afr, afi, abr, abi = a[:, 0:w], a[:, w:2 * w], a[:, 2 * w:3 * w], a[:, 3 * w:4 * w]

        def step(c, carry):
            fr, fi, br, bi = carry
            cb = nc - 1 - c
            out_ref[0, pl.ds(c, 1), 0:w] = fr
            out_ref[0, pl.ds(c, 1), w:2 * w] = fi
            out_ref[0, pl.ds(cb, 1), 2 * w:3 * w] = br
            out_ref[0, pl.ds(cb, 1), 3 * w:4 * w] = bi
            nfr, nfi = _cmul(afr, afi, fr, fi)
            nbr, nbi = _cmul(abr, abi, br, bi)
            return (nfr + sin_ref[0, pl.ds(c, 1), 0:w], nfi + sin_ref[0, pl.ds(c, 1), w:2 * w],
                    nbr + sin_ref[0, pl.ds(cb, 1), 2 * w:3 * w], nbi + sin_ref[0, pl.ds(cb, 1), 3 * w:4 * w])

        z = jnp.zeros((1, w), F32)
        lax.fori_loop(0, nc, step, (z, z, z, z))

    spec = pl.BlockSpec((1, nc, STATE_W), lambda b: (b, 0, 0))
    return pl.pallas_call(
        body, name="ssm_state_scan", grid=(N_BLOCKS,),
        in_specs=[spec, pl.BlockSpec((1, 1, STATE_W), lambda b: (b, 0, 0))],
        out_specs=spec, out_shape=jax.ShapeDtypeStruct(s_in.shape, F32),
        compiler_params=_cparams(("arbitrary",)),
    )(s_in, a16)


def _ssm_state_scan_bwd(d_prev, s_prev, a16):
    nc = d_prev.shape[1]
    w = BLOCK_ST

    def body(dp_ref, sp_ref, a_ref, g_ref, da_ref):
        a = a_ref[0]
        afr, afi, abr, abi = a[:, 0:w], a[:, w:2 * w], a[:, 2 * w:3 * w], a[:, 3 * w:4 * w]

        def step(i, carry):
            gfr, gfi, gbr, gbi, dafr, dafi, dabr, dabi = carry
            cf = nc - 1 - i
            cb = i
            g_ref[0, pl.ds(cf, 1), 0:w] = gfr
            g_ref[0, pl.ds(cf, 1), w:2 * w] = gfi
            g_ref[0, pl.ds(cb, 1), 2 * w:3 * w] = gbr
            g_ref[0, pl.ds(cb, 1), 3 * w:4 * w] = gbi
            sfr, sfi = sp_ref[0, pl.ds(cf, 1), 0:w], sp_ref[0, pl.ds(cf, 1), w:2 * w]
            sbr, sbi = sp_ref[0, pl.ds(cb, 1), 2 * w:3 * w], sp_ref[0, pl.ds(cb, 1), 3 * w:4 * w]
            dafr = dafr + gfr * sfr + gfi * sfi
            dafi = dafi + gfi * sfr - gfr * sfi
            dabr = dabr + gbr * sbr + gbi * sbi
            dabi = dabi + gbi * sbr - gbr * sbi
            nfr, nfi = _cmul_conj(afr, afi, gfr, gfi)
            nbr, nbi = _cmul_conj(abr, abi, gbr, gbi)
            return (nfr + dp_ref[0, pl.ds(cf, 1), 0:w], nfi + dp_ref[0, pl.ds(cf, 1), w:2 * w],
                    nbr + dp_ref[0, pl.ds(cb, 1), 2 * w:3 * w], nbi + dp_ref[0, pl.ds(cb, 1), 3 * w:4 * w],
                    dafr, dafi, dabr, dabi)

        z = jnp.zeros((1, w), F32)
        res = lax.fori_loop(0, nc, step, (z,) * 8)
        da_ref[0] = jnp.concatenate(res[4:], axis=1)

    spec = pl.BlockSpec((1, nc, STATE_W), lambda b: (b, 0, 0))
    aspec = pl.BlockSpec((1, 1, STATE_W), lambda b: (b, 0, 0))
    return pl.pallas_call(
        body, name="ssm_state_scan_bwd", grid=(N_BLOCKS,),
        in_specs=[spec, spec, aspec], out_specs=[spec, aspec],
        out_shape=[jax.ShapeDtypeStruct(d_prev.shape, F32), jax.ShapeDtypeStruct((N_BLOCKS, 1, STATE_W), F32)],
        compiler_params=_cparams(("arbitrary",)),
    )(d_prev, s_prev, a16)


NA_PAIR = 2 * GRID_W
NA_WIN_ROWS = NA_ROWS + 2
NA_WIN = NA_WIN_ROWS * GRID_W
NA_PAIRS_PER_STEP = 8
NA_CASES = 5
NA_MASKED = -1e30


def _na_pair_window(m, rows):
    rs0 = jnp.clip(2 * m - NA_ROWS // 2, 0, rows - NA_ROWS)
    ws = jnp.minimum(rs0, rows - NA_WIN_ROWS)
    last = rows // 2 - 1
    case = jnp.where(m == 0, 0, jnp.where(m == 1, 1, jnp.where(m == last - 1, 3, jnp.where(m == last, 4, 2))))
    return ws, case


def _na_row_offsets(rows):
    last = rows // 2 - 1
    geom = []
    for m in (0, 1, 2, last - 1, last):
        ws = min(max(2 * m - NA_ROWS // 2, 0), rows - NA_ROWS, rows - NA_WIN_ROWS)
        per_case = []
        for i in range(NA_WIN_ROWS):
            pair = []
            for rr in range(2):
                r = 2 * m + rr
                rs = min(max(r - NA_ROWS // 2, 0), rows - NA_ROWS)
                pair.append(ws + i - r + NA_ROWS - 1 if rs <= ws + i < rs + NA_ROWS else None)
            per_case.append(pair)
        geom.append(per_case)
    return geom


def _na_col_select():
    qc = np.arange(NA_PAIR)[None, :] % GRID_W
    kc = np.arange(GRID_W)[:, None]
    dc = np.clip(kc - qc + NA_COLS - 1, 0, 2 * NA_COLS - 2)
    return jnp.asarray((np.arange(2 * NA_COLS - 1)[:, None, None] == dc[None]).astype(np.float32))


def _na_bias_rows(rpb):
    return jnp.einsum("hrd,dkl->hrkl", rpb, _na_col_select(), precision=HIGHEST)


def _na_col_window():
    qc = lax.broadcasted_iota(jnp.int32, (GRID_W, NA_PAIR), 1) % GRID_W
    kc = lax.broadcasted_iota(jnp.int32, (GRID_W, NA_PAIR), 0)
    cs = jnp.clip(qc - NA_COLS // 2, 0, GRID_W - NA_COLS)
    first_row = lax.broadcasted_iota(jnp.int32, (GRID_W, NA_PAIR), 1) < GRID_W
    return (kc >= cs) & (kc < cs + NA_COLS), first_row


def _na_bias_table(bias_rows, rows):
    geom = _na_row_offsets(rows)

    def body(br_ref, tab_ref):
        col_ok, first_row = _na_col_window()
        masked = jnp.full((GRID_W, NA_PAIR), NA_MASKED, F32)
        for case in range(NA_CASES):
            for i in range(NA_WIN_ROWS):
                d0, d1 = geom[case][i]
                t0 = masked if d0 is None else br_ref[0, d0]
                t1 = masked if d1 is None else br_ref[0, d1]
                tile = jnp.where(col_ok, jnp.where(first_row, t0, t1), NA_MASKED)
                tab_ref[0, case, i * GRID_W:(i + 1) * GRID_W, :] = tile

    return pl.pallas_call(
        body, name="na_bias_table", grid=(NA_HEADS,),
        in_specs=[pl.BlockSpec((1, 2 * NA_ROWS - 1, GRID_W, NA_PAIR), lambda h: (h, 0, 0, 0))],
        out_specs=pl.BlockSpec((1, NA_CASES, NA_WIN, NA_PAIR), lambda h: (h, 0, 0, 0)),
        out_shape=jax.ShapeDtypeStruct((NA_HEADS, NA_CASES, NA_WIN, NA_PAIR), F32),
        compiler_params=_cparams(("arbitrary",)),
    )(bias_rows)


def _na_bias_table_bwd(d_tab, rows):
    geom = _na_row_offsets(rows)

    def body(dt_ref, dbr_ref):
        col_ok, first_row = _na_col_window()
        acc = [None] * (2 * NA_ROWS - 1)
        for case in range(NA_CASES):
            for i in range(NA_WIN_ROWS):
                tile = jnp.where(col_ok, dt_ref[0, case, i * GRID_W:(i + 1) * GRID_W, :], 0.0)
                for rr, d in enumerate(geom[case][i]):
                    if d is not None:
                        part = jnp.where(first_row if rr == 0 else ~first_row, tile, 0.0)
                        acc[d] = part if acc[d] is None else acc[d] + part
        for d, a in enumerate(acc):
            dbr_ref[0, d] = jnp.zeros((GRID_W, NA_PAIR), F32) if a is None else a

    return pl.pallas_call(
        body, name="na_bias_table_bwd", grid=(NA_HEADS,),
        in_specs=[pl.BlockSpec((1, NA_CASES, NA_WIN, NA_PAIR), lambda h: (h, 0, 0, 0))],
        out_specs=pl.BlockSpec((1, 2 * NA_ROWS - 1, GRID_W, NA_PAIR), lambda h: (h, 0, 0, 0)),
        out_shape=jax.ShapeDtypeStruct((NA_HEADS, 2 * NA_ROWS - 1, GRID_W, NA_PAIR), F32),
        compiler_params=_cparams(("arbitrary",)),
    )(d_tab)


def _na_scores(k_win, q_t, bias):
    s = _dot(k_win, q_t) + bias
    e = jnp.exp(s - jnp.max(s, axis=0, keepdims=True))
    return e, jnp.sum(e, axis=0, keepdims=True)


def _na_fwd(q_t, k_h, v_t, bias_tab):
    L = k_h.shape[1]
    rows = L // GRID_W
    step_w = NA_PAIRS_PER_STEP * NA_PAIR

    def body(q_ref, k_ref, v_ref, bt_ref, o_ref):
        for pp in range(NA_PAIRS_PER_STEP):
            ws, case = _na_pair_window(pl.program_id(1) * NA_PAIRS_PER_STEP + pp, rows)
            start = pl.multiple_of(ws * GRID_W, NA_PAIR)
            lanes = slice(pp * NA_PAIR, (pp + 1) * NA_PAIR)
            e, l = _na_scores(k_ref[0, pl.ds(start, NA_WIN), :], q_ref[0, :, lanes], bt_ref[0, case])
            o_ref[0, :, lanes] = _dot(v_ref[0, :, pl.ds(start, NA_WIN)], e.astype(BF16)) / l

    q_spec = pl.BlockSpec((1, NA_HEAD_DIM, step_w), lambda h, s: (h, 0, s))
    return pl.pallas_call(
        body, name="na_fwd", grid=(NA_HEADS, L // step_w),
        in_specs=[q_spec, pl.BlockSpec((1, L, NA_HEAD_DIM), lambda h, s: (h, 0, 0)),
                  pl.BlockSpec((1, NA_HEAD_DIM, L), lambda h, s: (h, 0, 0)),
                  pl.BlockSpec((1, NA_CASES, NA_WIN, NA_PAIR), lambda h, s: (h, 0, 0, 0))],
        out_specs=q_spec,
        out_shape=jax.ShapeDtypeStruct((NA_HEADS, NA_HEAD_DIM, L), F32),
        compiler_params=_cparams(("arbitrary", "arbitrary")),
    )(q_t, k_h, v_t, bias_tab)


def _na_bwd(q_t, q_h, k_t, k_h, v_h, bias_tab, out_t, d_out_t, d_out_h):
    L = k_h.shape[1]
    rows = L // GRID_W
    step_w = NA_PAIRS_PER_STEP * NA_PAIR

    def body(qt_ref, qh_ref, kt_ref, kh_ref, vh_ref, bt_ref, ot_ref, dot_ref, doh_ref, dq_ref, dk_ref, dv_ref, dbt_ref):
        @pl.when(pl.program_id(1) == 0)
        def _():
            dk_ref[...] = jnp.zeros_like(dk_ref)
            dv_ref[...] = jnp.zeros_like(dv_ref)
            dbt_ref[...] = jnp.zeros_like(dbt_ref)

        for pp in range(NA_PAIRS_PER_STEP):
            ws, case = _na_pair_window(pl.program_id(1) * NA_PAIRS_PER_STEP + pp, rows)
            start = pl.multiple_of(ws * GRID_W, NA_PAIR)
            lanes = slice(pp * NA_PAIR, (pp + 1) * NA_PAIR)
            win = pl.ds(start, NA_WIN)
            e, l = _na_scores(kh_ref[0, win, :], qt_ref[0, :, lanes], bt_ref[0, case])
            p = e / l
            d_o = dot_ref[0, :, lanes]
            dp = _dot(vh_ref[0, win, :], d_o.astype(BF16))
            delta = jnp.sum(d_o * ot_ref[0, :, lanes], axis=0, keepdims=True)
            ds = p * (dp - delta)
            dbt_ref[0, case] += ds
            dsb = ds.astype(BF16)
            dq_ref[0, :, lanes] = _dot(kt_ref[0, :, win], dsb) * (NA_HEAD_DIM ** -0.5)
            tokens = slice(pp * NA_PAIR, (pp + 1) * NA_PAIR)
            dk_ref[0, win, :] += _dot(dsb, qh_ref[0, tokens, :])
            dv_ref[0, win, :] += _dot(p.astype(BF16), doh_ref[0, tokens, :])

    t_tile = pl.BlockSpec((1, NA_HEAD_DIM, step_w), lambda h, s: (h, 0, s))
    h_tile = pl.BlockSpec((1, step_w, NA_HEAD_DIM), lambda h, s: (h, s, 0))
    t_full = pl.BlockSpec((1, NA_HEAD_DIM, L), lambda h, s: (h, 0, 0))
    h_full = pl.BlockSpec((1, L, NA_HEAD_DIM), lambda h, s: (h, 0, 0))
    bt = pl.BlockSpec((1, NA_CASES, NA_WIN, NA_PAIR), lambda h, s: (h, 0, 0, 0))
    return pl.pallas_call(
        body, name="na_bwd", grid=(NA_HEADS, L // step_w),
        in_specs=[t_tile, h_tile, t_full, h_full, h_full, bt, t_tile, t_tile, h_tile],
        out_specs=[t_tile, h_full, h_full, bt],
        out_shape=[jax.ShapeDtypeStruct((NA_HEADS, NA_HEAD_DIM, L), F32), jax.ShapeDtypeStruct((NA_HEADS, L, NA_HEAD_DIM), F32),
                   jax.ShapeDtypeStruct((NA_HEADS, L, NA_HEAD_DIM), F32), jax.ShapeDtypeStruct(bias_tab.shape, F32)],
        compiler_params=_cparams(("arbitrary", "arbitrary")),
    )(q_t, q_h, k_t, k_h, v_h, bias_tab, out_t, d_out_t, d_out_h)


def _branch_fwd_values(ys, zs, yn, zn, wglu, bglu):
    g1, t = _gelu_parts(ys)
    lin = _dot(g1.astype(BF16), wglu) + bglu
    sg = _sigmoid(lin)
    ys2 = g1 * sg
    sz, szs = _silu_parts(zs)
    sn, sns = _silu_parts(zn)
    return g1, t, sg, ys2, sz, szs, sn, sns


def _branch_fwd(y_ssm_c, z_s, y_na_t, z_n, w_glu, b_glu, tm=512):
    L = z_s.shape[0]

    def body(ys_ref, zs_ref, yn_ref, zn_ref, w_ref, b_ref, cat_ref, scr):
        yn = yn_ref[...].T
        g1, t, sg, ys2, sz, szs, sn, sns = _branch_fwd_values(
            _load_chunks(ys_ref, scr), zs_ref[...], yn, zn_ref[...], w_ref[...], b_ref[...])
        cat_ref[:, 0:512] = (ys2 * sz).astype(BF16)
        cat_ref[:, 512:1024] = (yn * sn).astype(BF16)

    tile = pl.BlockSpec((tm, 512), lambda i: (i, 0))
    return pl.pallas_call(
        body, name="branch_fwd", grid=(L // tm,),
        in_specs=[_chunk_spec(tm), tile, _heads_t_spec(tm), tile, pl.BlockSpec((512, 512), lambda i: (0, 0)),
                  pl.BlockSpec((1, 512), lambda i: (0, 0))],
        out_specs=pl.BlockSpec((tm, 1024), lambda i: (i, 0)),
        out_shape=jax.ShapeDtypeStruct((L, 1024), BF16),
        scratch_shapes=[_chunk_scratch(tm)],
        compiler_params=_cparams(("arbitrary",)),
    )(y_ssm_c, z_s, y_na_t, z_n, w_glu, b_glu)


def _branch_bwd(y_ssm_c, z_s, y_na_t, z_n, w_glu, b_glu, d_cat, tm=512):
    L = z_s.shape[0]

    def body(ys_ref, zs_ref, yn_ref, zn_ref, w_ref, b_ref, dc_ref,
             dys_ref, dzs_ref, dynt_ref, dynh_ref, dzn_ref, dw_ref, db_ref, scr):
        @pl.when(pl.program_id(0) == 0)
        def _():
            dw_ref[...] = jnp.zeros_like(dw_ref)
            db_ref[...] = jnp.zeros_like(db_ref)

        ys, zs, yn, zn = _load_chunks(ys_ref, scr), zs_ref[...], yn_ref[...].T, zn_ref[...]
        w = w_ref[...]
        g1, t, sg, ys2, sz, szs, sn, sns = _branch_fwd_values(ys, zs, yn, zn, w, b_ref[...])
        dys3 = dc_ref[:, 0:512]
        dyn2 = dc_ref[:, 512:1024]
        dzs_ref[...] = (dys3 * ys2 * _silu_grad(zs, szs)).astype(BF16)
        dys2 = dys3 * sz
        dlin = dys2 * g1 * sg * (1.0 - sg)
        dlb = dlin.astype(BF16)
        db_ref[...] += jnp.sum(dlin, axis=0, keepdims=True)
        dw_ref[...] += _dot_tn(g1.astype(BF16), dlb)
        dg1 = dys2 * sg + _dot_nt(dlb, w)
        _store_chunks(dg1 * _gelu_grad(ys, t), scr, dys_ref, BF16)
        dyn = dyn2 * sn
        dynt_ref[...] = dyn.T
        _store_heads(dyn, dynh_ref, BF16)
        dzn_ref[...] = (dyn2 * yn * _silu_grad(zn, sns)).astype(BF16)

    tile = pl.BlockSpec((tm, 512), lambda i: (i, 0))
    wspec = pl.BlockSpec((512, 512), lambda i: (0, 0))
    bspec = pl.BlockSpec((1, 512), lambda i: (0, 0))
    tok = jax.ShapeDtypeStruct((L, 512), BF16)
    return pl.pallas_call(
        body, name="branch_bwd", grid=(L // tm,),
        in_specs=[_chunk_spec(tm), tile, _heads_t_spec(tm), tile, wspec, bspec, pl.BlockSpec((tm, 1024), lambda i: (i, 0))],
        out_specs=[_chunk_spec(tm), tile, _heads_t_spec(tm), _heads_spec(tm), tile, wspec, bspec],
        out_shape=[jax.ShapeDtypeStruct((N_BLOCKS, L // CHUNK, CHUNK_W), BF16), tok, jax.ShapeDtypeStruct((D_NA, L), F32),
                   jax.ShapeDtypeStruct((NA_HEADS, L, NA_HEAD_DIM), BF16), tok,
                   jax.ShapeDtypeStruct((512, 512), F32), jax.ShapeDtypeStruct((1, 512), F32)],
        scratch_shapes=[_chunk_scratch(tm)],
        compiler_params=_cparams(("arbitrary",)),
    )(y_ssm_c, z_s, y_na_t, z_n, w_glu, b_glu, d_cat)


def _head(x, p, target, cat, w_out, g_post, w_ple_g, g_ple, w_pg, tm=256):
    L = x.shape[0]
    pw = w_ple_g.shape[2]

    def body(x_ref, p_ref, t_ref, cat_ref, wo_ref, gpo_ref, wp_ref, gpl_ref, wg_ref,
             loss_ref, dh1_ref, dcat_ref, dwo_ref, dgpo_ref, dwp_ref, dgpl_ref, dwg_ref):
        @pl.when(pl.program_id(0) == 0)
        def _():
            for r in (loss_ref, dwo_ref, dgpo_ref, dwp_ref, dgpl_ref, dwg_ref):
                r[...] = jnp.zeros_like(r)

        cat_b = cat_ref[...]
        wo, wg = wo_ref[...], wg_ref[...]
        g_po, g_pl = gpo_ref[...], gpl_ref[...]
        mix = _dot(cat_b, wo)
        nm, r2 = _rms(mix)
        h1 = x_ref[...] + nm * g_po
        p_b = p_ref[...].astype(BF16)
        ep = jnp.concatenate([_dot(p_b, wp_ref[j]) for j in range(N_CHIPS)], axis=1)
        ne, r3 = _rms(ep)
        e = ne * g_pl
        h1_b = h1.astype(BF16)
        gate = _sigmoid(_dot(h1_b, wg))
        h2 = h1 + gate * e
        diff = h2 - t_ref[...]
        loss_ref[...] += (0.5 / D_MODEL) * jnp.sum(diff * diff).reshape(1, 1)

        dh2 = diff * (1.0 / D_MODEL)
        de = dh2 * gate
        dgl = (dh2 * e * gate * (1.0 - gate)).astype(BF16)
        dwg_ref[...] += _dot_tn(h1_b, dgl)
        dh1 = dh2 + _dot_nt(dgl, wg)
        dgpl_ref[...] += jnp.sum(de * ne, axis=0, keepdims=True)
        dep = _rms_bwd(de * g_pl, ne, r3).astype(BF16)
        for j in range(N_CHIPS):
            dwp_ref[j] += _dot_tn(p_b, dep[:, j * pw:(j + 1) * pw])
        dgpo_ref[...] += jnp.sum(dh1 * nm, axis=0, keepdims=True)
        dmix = _rms_bwd(dh1 * g_po, nm, r2).astype(BF16)
        dwo_ref[...] += _dot_tn(cat_b, dmix)
        dcat_ref[...] = _dot_nt(dmix, wo)
        dh1_ref[...] = dh1

    tile = lambda w: pl.BlockSpec((tm, w), lambda i: (i, 0))
    const = lambda *s: pl.BlockSpec(s, lambda i: (0,) * len(s))
    sds = jax.ShapeDtypeStruct
    return pl.pallas_call(
        body, name="head", grid=(L // tm,),
        in_specs=[tile(D_MODEL), tile(D_PLE), tile(D_MODEL), tile(1024), const(1024, D_MODEL), const(1, D_MODEL),
                  const(N_CHIPS, D_PLE, pw), const(1, D_MODEL), const(D_MODEL, D_MODEL)],
        out_specs=[const(1, 1), tile(D_MODEL), tile(1024), const(1024, D_MODEL), const(1, D_MODEL),
                   const(N_CHIPS, D_PLE, pw), const(1, D_MODEL), const(D_MODEL, D_MODEL)],
        out_shape=[sds((1, 1), F32), sds((L, D_MODEL), F32), sds((L, 1024), F32), sds((1024, D_MODEL), F32),
                   sds((1, D_MODEL), F32), sds((N_CHIPS, D_PLE, pw), F32), sds((1, D_MODEL), F32),
                   sds((D_MODEL, D_MODEL), F32)],
        compiler_params=_cparams(("arbitrary",)),
    )(x, p, target, cat, w_out, g_post, w_ple_g, g_ple, w_pg)


def _dproj_specs(tm):
    tile = pl.BlockSpec((tm, 512), lambda i: (i, 0))
    return [_chunk_spec(tm), tile, _heads_t_spec(tm), _heads_spec(tm), _heads_spec(tm), tile]


def _dproj_tile(refs, scr):
    du_ref, dzs_ref, dqt_ref, dkh_ref, dvh_ref, dzn_ref = refs
    parts = [_load_chunks(du_ref, scr), dzs_ref[...], dqt_ref[...].T, _load_heads(dkh_ref), _load_heads(dvh_ref), dzn_ref[...]]
    return jnp.concatenate([t.astype(BF16) for t in parts], axis=1)


def _in_proj_bwd_w(x, g_pre, dparts, tm=512):
    L = x.shape[0]
    wn = D_IN_PROJ // N_CHIPS

    def body(x_ref, g_ref, *refs):
        dw_ref, scr = refs[-2], refs[-1]

        @pl.when(pl.program_id(0) == 0)
        def _():
            dw_ref[...] = jnp.zeros_like(dw_ref)

        n, _ = _rms(x_ref[...])
        hn = (n * g_ref[...]).astype(BF16)
        dproj = _dproj_tile(refs[:-2], scr)
        for j in range(N_CHIPS):
            dw_ref[j] += _dot_tn(hn, dproj[:, j * wn:(j + 1) * wn])

    return pl.pallas_call(
        body, name="in_proj_bwd_w", grid=(L // tm,),
        in_specs=[pl.BlockSpec((tm, D_MODEL), lambda i: (i, 0)), pl.BlockSpec((1, D_MODEL), lambda i: (0, 0))] + _dproj_specs(tm),
        out_specs=pl.BlockSpec((N_CHIPS, D_MODEL, wn), lambda i: (0, 0, 0)),
        out_shape=jax.ShapeDtypeStruct((N_CHIPS, D_MODEL, wn), F32),
        scratch_shapes=[_chunk_scratch(tm)],
        compiler_params=_cparams(("arbitrary",)),
    )(x, g_pre, *dparts)


def _in_proj_bwd_x(x, g_pre, w_in_g, d_h1, dparts, tm=512):
    L = x.shape[0]
    wn = w_in_g.shape[2]

    def body(x_ref, g_ref, w_ref, dh1_ref, *refs):
        dx_ref, dg_ref, scr = refs[-3], refs[-2], refs[-1]

        @pl.when(pl.program_id(0) == 0)
        def _():
            dg_ref[...] = jnp.zeros_like(dg_ref)

        n, r = _rms(x_ref[...])
        dproj = _dproj_tile(refs[:-3], scr)
        dhn = _dot_nt(dproj[:, 0:wn], w_ref[0])
        for j in range(1, N_CHIPS):
            dhn = dhn + _dot_nt(dproj[:, j * wn:(j + 1) * wn], w_ref[j])
        dg_ref[...] += jnp.sum(dhn * n, axis=0, keepdims=True)
        dx_ref[...] = dh1_ref[...] + _rms_bwd(dhn * g_ref[...], n, r)

    wide = pl.BlockSpec((tm, D_MODEL), lambda i: (i, 0))
    vec = pl.BlockSpec((1, D_MODEL), lambda i: (0, 0))
    return pl.pallas_call(
        body, name="in_proj_bwd_x", grid=(L // tm,),
        in_specs=[wide, vec, pl.BlockSpec((N_CHIPS, D_MODEL, wn), lambda i: (0, 0, 0)), wide] + _dproj_specs(tm),
        out_specs=[wide, vec],
        out_shape=[jax.ShapeDtypeStruct((L, D_MODEL), F32), jax.ShapeDtypeStruct((1, D_MODEL), F32)],
        scratch_shapes=[_chunk_scratch(tm)],
        compiler_params=_cparams(("arbitrary",)),
    )(x, g_pre, w_in_g, d_h1, *dparts)


def _mesh_position():
    x, y, c = lax.axis_index("x"), lax.axis_index("y"), lax.axis_index("c")
    chips = [(1 - x, y), (x, 1 - y), (1 - x, 1 - y)]
    return x, y, c, chips


def _chip_index(cx, cy):
    return 2 * cx + cy


def _hbm_specs(n):
    return [pl.BlockSpec(memory_space=pl.ANY)] * n


def _gather_chips(shards, name):
    n = len(shards)

    def body(*refs):
        ins, outs = refs[:n], refs[n:2 * n]
        send1, recv1, send2, recv2, local = refs[2 * n:]
        x, y, c, chips = _mesh_position()
        me = _chip_index(x, y)
        sibling = (x, y, 1 - c)

        def half(ref, chip, core):
            hr = ref.shape[1] // 2
            return ref.at[chip, pl.ds(core * hr, hr)]

        copies, locals_ = [], []
        for a in range(n):
            lc = pltpu.make_async_copy(ins[a], outs[a].at[me], local.at[a])
            lc.start()
            locals_.append(lc)
            hr = ins[a].shape[0] // 2
            for j, chip in enumerate(chips):
                cp = pltpu.make_async_remote_copy(
                    src_ref=ins[a].at[pl.ds(c * hr, hr)], dst_ref=half(outs[a], me, c),
                    send_sem=send1.at[a, j], recv_sem=recv1.at[a, j], device_id=(*chip, c), device_id_type=MESH)
                cp.start()
                copies.append(cp)
        for a in range(n):
            for j, chip in enumerate(chips):
                landed = half(outs[a], _chip_index(*chip), c)
                pltpu.make_async_remote_copy(
                    src_ref=landed, dst_ref=landed, send_sem=send1.at[a, j], recv_sem=recv1.at[a, j],
                    device_id=(*chip, c), device_id_type=MESH).wait_recv()
                cp = pltpu.make_async_remote_copy(
                    src_ref=landed, dst_ref=landed, send_sem=send2.at[a, j], recv_sem=recv2.at[a, j],
                    device_id=sibling, device_id_type=MESH)
                cp.start()
                copies.append(cp)
        for a in range(n):
            for j, chip in enumerate(chips):
                other = half(outs[a], _chip_index(*chip), 1 - c)
                pltpu.make_async_remote_copy(
                    src_ref=other, dst_ref=other, send_sem=send2.at[a, j], recv_sem=recv2.at[a, j],
                    device_id=sibling, device_id_type=MESH).wait_recv()
        for cp in copies:
            cp.wait_send()
        for lc in locals_:
            lc.wait()

    sem = pltpu.SemaphoreType.DMA
    return pl.pallas_call(
        body, name=name, in_specs=_hbm_specs(n), out_specs=_hbm_specs(n),
        out_shape=[jax.ShapeDtypeStruct((N_CHIPS,) + s.shape, s.dtype) for s in shards],
        scratch_shapes=[sem((n, 3)), sem((n, 3)), sem((n, 3)), sem((n, 3)), sem((n,))],
        compiler_params=pltpu.CompilerParams(has_side_effects=True),
    )(*shards)


def _pair_exchange(grads):
    n = len(grads)

    def body(*refs):
        ins, outs = refs[:n], refs[n:2 * n]
        send, recv = refs[2 * n:]
        x, y, c, _ = _mesh_position()
        copies = []
        for a in range(n):
            hr = ins[a].shape[1] // 2
            cp = pltpu.make_async_remote_copy(
                src_ref=ins[a].at[:, pl.ds((1 - c) * hr, hr)], dst_ref=outs[a],
                send_sem=send.at[a], recv_sem=recv.at[a], device_id=(x, y, 1 - c), device_id_type=MESH)
            cp.start()
            copies.append(cp)
        for cp in copies:
            cp.wait()

    sem = pltpu.SemaphoreType.DMA
    return pl.pallas_call(
        body, name="pair_exchange", in_specs=_hbm_specs(n), out_specs=_hbm_specs(n),
        out_shape=[jax.ShapeDtypeStruct((g.shape[0], g.shape[1] // 2, g.shape[2]), g.dtype) for g in grads],
        scratch_shapes=[sem((n,)), sem((n,))],
        compiler_params=pltpu.CompilerParams(has_side_effects=True),
    )(*grads)


def _pair_add(core, grad, other, tr, out_dtype):
    hr = other.shape[1]
    cdim = other.shape[2]
    nb = hr // tr

    def body(core_ref, g_ref, o_ref, out_ref):
        out_ref[...] = (g_ref[...] + o_ref[...]).astype(out_dtype)

    return pl.pallas_call(
        body, name="pair_add",
        grid_spec=pltpu.PrefetchScalarGridSpec(
            num_scalar_prefetch=1, grid=(N_CHIPS, nb),
            in_specs=[pl.BlockSpec((1, tr, cdim), lambda j, i, core_ref: (j, core_ref[0] * nb + i, 0)),
                      pl.BlockSpec((1, tr, cdim), lambda j, i, core_ref: (j, i, 0))],
            out_specs=pl.BlockSpec((1, tr, cdim), lambda j, i, core_ref: (j, i, 0))),
        out_shape=jax.ShapeDtypeStruct(other.shape, out_dtype),
        compiler_params=_cparams(("arbitrary", "arbitrary")),
    )(core, grad, other)


def _chip_scatter(parts):
    n = len(parts)

    def body(*refs):
        ins, outs = refs[:n], refs[n:2 * n]
        send, recv, local = refs[2 * n:]
        x, y, c, chips = _mesh_position()
        me = _chip_index(x, y)
        copies = []
        for a in range(n):
            lc = pltpu.make_async_copy(ins[a].at[me], outs[a].at[me], local.at[a])
            lc.start()
            copies.append(lc)
            for j, chip in enumerate(chips):
                cp = pltpu.make_async_remote_copy(
                    src_ref=ins[a].at[_chip_index(*chip)], dst_ref=outs[a].at[me],
                    send_sem=send.at[a, j], recv_sem=recv.at[a, j], device_id=(*chip, c), device_id_type=MESH)
                cp.start()
                copies.append(cp)
        for cp in copies:
            cp.wait()

    sem = pltpu.SemaphoreType.DMA
    return pl.pallas_call(
        body, name="chip_scatter", in_specs=_hbm_specs(n), out_specs=_hbm_specs(n),
        out_shape=[jax.ShapeDtypeStruct(p.shape, p.dtype) for p in parts],
        scratch_shapes=[sem((n, 3)), sem((n, 3)), sem((n,))],
        compiler_params=pltpu.CompilerParams(has_side_effects=True),
    )(*parts)


def _chip_add(recv, tr):
    hr, cdim = recv.shape[1], recv.shape[2]

    def body(r_ref, out_ref):
        out_ref[...] = ((r_ref[0].astype(F32) + r_ref[1].astype(F32)) + r_ref[2].astype(F32)) + r_ref[3].astype(F32)

    return pl.pallas_call(
        body, name="chip_add", grid=(hr // tr,),
        in_specs=[pl.BlockSpec((N_CHIPS, tr, cdim), lambda i: (0, i, 0))],
        out_specs=pl.BlockSpec((tr, cdim), lambda i: (i, 0)),
        out_shape=jax.ShapeDtypeStruct((hr, cdim), F32),
        compiler_params=_cparams(("arbitrary",)),
    )(recv)


def _pair_gather(halves):
    n = len(halves)

    def body(*refs):
        ins, outs = refs[:n], refs[n:2 * n]
        send, recv, local = refs[2 * n:]
        x, y, c, _ = _mesh_position()
        copies = []
        for a in range(n):
            hr = ins[a].shape[0]
            mine = outs[a].at[pl.ds(c * hr, hr)]
            lc = pltpu.make_async_copy(ins[a], mine, local.at[a])
            lc.start()
            copies.append(lc)
            cp = pltpu.make_async_remote_copy(
                src_ref=ins[a], dst_ref=mine, send_sem=send.at[a], recv_sem=recv.at[a],
                device_id=(x, y, 1 - c), device_id_type=MESH)
            cp.start()
            copies.append(cp)
        for cp in copies:
            cp.wait()

    sem = pltpu.SemaphoreType.DMA
    return pl.pallas_call(
        body, name="pair_gather", in_specs=_hbm_specs(n), out_specs=_hbm_specs(n),
        out_shape=[jax.ShapeDtypeStruct((2 * h.shape[0], h.shape[1]), h.dtype) for h in halves],
        scratch_shapes=[sem((n,)), sem((n,)), sem((n,))],
        compiler_params=pltpu.CompilerParams(has_side_effects=True),
    )(*halves)


def _row_tile(rows):
    for t in (512, 256, 128, 64, 32, 16, 8):
        if rows % t == 0:
            return t
    raise ValueError(rows)


def _reduce_scatter(grads, ici_dtypes):
    core = lax.axis_index("c").astype(jnp.int32).reshape(1)
    others = _pair_exchange(grads)
    pair = [_pair_add(core, g, o, _row_tile(o.shape[1]), dt) for g, o, dt in zip(grads, others, ici_dtypes)]
    landed = _chip_scatter(pair)
    halves = [_chip_add(r, _row_tile(r.shape[1])) for r in landed]
    return _pair_gather(halves)


def _adamw(w, g, m, v):
    rows, cols = w.shape
    tr = _row_tile(rows) if rows % 8 == 0 else rows

    def body(w_ref, g_ref, m_ref, v_ref, d_ref, nm_ref, nv_ref):
        g_ = g_ref[...]
        m_ = ADAM_B1 * m_ref[...] + (1.0 - ADAM_B1) * g_
        v_ = ADAM_B2 * v_ref[...] + (1.0 - ADAM_B2) * (g_ * g_)
        m_hat = m_ / (1.0 - ADAM_B1 ** ADAM_STEP)
        v_hat = v_ / (1.0 - ADAM_B2 ** ADAM_STEP)
        d_ref[...] = -ADAM_LR * (m_hat / (jnp.sqrt(v_hat) + ADAM_EPS) + ADAM_WD * w_ref[...])
        nm_ref[...] = m_
        nv_ref[...] = v_

    spec = pl.BlockSpec((tr, cols), lambda i: (i, 0))
    shp = jax.ShapeDtypeStruct((rows, cols), F32)
    return pl.pallas_call(
        body, name="adamw", grid=(rows // tr,), in_specs=[spec] * 4, out_specs=[spec] * 3,
        out_shape=[shp] * 3, compiler_params=_cparams(("arbitrary",)),
    )(w, g, m, v)


_SMALL = ["norm_pre", "norm_post", "ssm_a_re", "ssm_a_im", "ssm_log_dt", "ssm_b_re", "ssm_b_im",
          "ssm_c_re", "ssm_c_im", "ssm_d", "b_glu", "na_rpb", "ple_norm"]
_BIG = ["w_in", "w_glu", "w_out", "w_ple", "w_ple_gate"]
_WEIGHTS = ["norm_pre", "norm_post", "w_in", "ssm_a_re", "ssm_a_im", "ssm_log_dt", "ssm_b_re", "ssm_b_im",
            "ssm_c_re", "ssm_c_im", "ssm_d", "w_glu", "b_glu", "na_rpb", "w_out", "w_ple", "ple_norm", "w_ple_gate"]
_SMALL_ROWS = 2176


def _pack_small(tensors):
    flat = jnp.concatenate([tensors[n].reshape(-1) for n in _SMALL])
    flat = jnp.pad(flat, (0, _SMALL_ROWS * 128 - flat.shape[0]))
    return flat.reshape(_SMALL_ROWS, 128)


def _unpack_small(packed, shapes):
    flat = packed.reshape(-1)
    out, off = {}, 0
    for n in _SMALL:
        size = int(np.prod(shapes[n]))
        out[n] = flat[off:off + size].reshape(shapes[n])
        off += size
    return out


def _local_grads(x, p, target, wts, w_in_g, w_glu, w_out, w_ple_g, w_pg):
    ssm_names = ["ssm_a_re", "ssm_a_im", "ssm_log_dt", "ssm_b_re", "ssm_b_im", "ssm_c_re", "ssm_c_im", "ssm_d"]
    ssm_params = [wts[n][0] for n in ssm_names]
    blk, blk_vjp = jax.vjp(_ssm_block_params, *ssm_params)
    m_mat, ws_mat, wot_mat, a16 = _ssm_chunk_matrices(blk)
    seq = x.shape[0]
    bias_rows, bias_rows_vjp = jax.vjp(_na_bias_rows, wts["na_rpb"][0])
    bias_tab = _na_bias_table(bias_rows, seq // GRID_W)

    heads_t = lambda t: t.reshape(NA_HEADS, NA_HEAD_DIM, seq)
    u_c, z_s, q_t, q_h, k_t, k_h, v_t, v_h, z_n = _in_proj(x, wts["norm_pre"], w_in_g)
    s_in = _block_matmul([(u_c, ws_mat, False)], "ssm_chunk_states")
    s_prev = _ssm_state_scan(s_in, a16)
    y_ssm_c = _block_matmul([(u_c, m_mat, False), (s_prev, wot_mat, True)], "ssm_chunk_out")
    y_na_t = _na_fwd(heads_t(q_t), k_h, heads_t(v_t), bias_tab)
    cat = _branch_fwd(y_ssm_c, z_s, y_na_t.reshape(D_NA, seq), z_n, w_glu, wts["b_glu"])

    (loss, d_h1, d_cat, d_w_out, d_g_post, d_w_ple, d_g_ple, d_w_pg) = _head(
        x, p, target, cat, w_out, wts["norm_post"], w_ple_g, wts["ple_norm"], w_pg)
    dy_c, d_z_s, d_y_na_t, d_y_na_h, d_z_n, d_w_glu, d_b_glu = _branch_bwd(
        y_ssm_c, z_s, y_na_t.reshape(D_NA, seq), z_n, w_glu, wts["b_glu"], d_cat)
    d_q_t, d_k_h, d_v_h, d_bias_tab = _na_bwd(
        heads_t(q_t), q_h, heads_t(k_t), k_h, v_h, bias_tab, y_na_t, heads_t(d_y_na_t), d_y_na_h)

    d_prev = _block_matmul([(dy_c, wot_mat, False)], "ssm_bwd_states")
    g_st, d_a16 = _ssm_state_scan_bwd(d_prev, s_prev, a16)
    d_u_c = _block_matmul([(dy_c, m_mat, True), (g_st, ws_mat, True)], "ssm_bwd_in", out_dtype=BF16)
    d_m = _block_matmul_tn(u_c, dy_c, "ssm_grad_m")
    d_ws = _block_matmul_tn(u_c, g_st, "ssm_grad_ws")
    d_wot = _block_matmul_tn(dy_c, s_prev, "ssm_grad_wot")
    d_ssm = blk_vjp(tuple(_ssm_chunk_matrices_bwd(blk, d_m, d_ws, d_wot, d_a16)))
    (d_rpb,) = bias_rows_vjp(_na_bias_table_bwd(d_bias_tab, seq // GRID_W))

    dparts = [d_u_c, d_z_s, d_q_t.reshape(D_NA, seq), d_k_h, d_v_h, d_z_n]
    d_w_in = _in_proj_bwd_w(x, wts["norm_pre"], dparts)
    grad_x, d_g_pre = _in_proj_bwd_x(x, wts["norm_pre"], w_in_g, d_h1, dparts)

    small = {"norm_pre": d_g_pre, "norm_post": d_g_post, "b_glu": d_b_glu, "na_rpb": d_rpb, "ple_norm": d_g_ple}
    for n, g in zip(ssm_names, d_ssm):
        small[n] = g
    big = {"w_in": d_w_in, "w_glu": d_w_glu.reshape(N_CHIPS, 128, 512), "w_out": d_w_out.reshape(N_CHIPS, 256, 1024),
           "w_ple": d_w_ple, "w_ple_gate": d_w_pg.reshape(N_CHIPS, 256, 1024)}
    return loss, grad_x, small, big


def kernel(x, p, norm_pre, norm_post, w_in, ssm_a_re, ssm_a_im, ssm_log_dt, ssm_b_re, ssm_b_im, ssm_c_re, ssm_c_im, ssm_d, w_glu, b_glu, na_rpb, w_out, w_ple, ple_norm, w_ple_gate, loss_target, m_norm_pre, m_norm_post, m_w_in, m_ssm_a_re, m_ssm_a_im, m_ssm_log_dt, m_ssm_b_re, m_ssm_b_im, m_ssm_c_re, m_ssm_c_im, m_ssm_d, m_w_glu, m_b_glu, m_na_rpb, m_w_out, m_w_ple, m_ple_norm, m_w_ple_gate, v_norm_pre, v_norm_post, v_w_in, v_ssm_a_re, v_ssm_a_im, v_ssm_log_dt, v_ssm_b_re, v_ssm_b_im, v_ssm_c_re, v_ssm_c_im, v_ssm_d, v_w_glu, v_b_glu, v_na_rpb, v_w_out, v_w_ple, v_ple_norm, v_w_ple_gate):
    wts = dict(norm_pre=norm_pre, norm_post=norm_post, w_in=w_in, ssm_a_re=ssm_a_re, ssm_a_im=ssm_a_im,
               ssm_log_dt=ssm_log_dt, ssm_b_re=ssm_b_re, ssm_b_im=ssm_b_im, ssm_c_re=ssm_c_re, ssm_c_im=ssm_c_im,
               ssm_d=ssm_d, w_glu=w_glu, b_glu=b_glu, na_rpb=na_rpb, w_out=w_out, w_ple=w_ple, ple_norm=ple_norm,
               w_ple_gate=w_ple_gate)
    mom_m = dict(norm_pre=m_norm_pre, norm_post=m_norm_post, w_in=m_w_in, ssm_a_re=m_ssm_a_re, ssm_a_im=m_ssm_a_im,
                 ssm_log_dt=m_ssm_log_dt, ssm_b_re=m_ssm_b_re, ssm_b_im=m_ssm_b_im, ssm_c_re=m_ssm_c_re,
                 ssm_c_im=m_ssm_c_im, ssm_d=m_ssm_d, w_glu=m_w_glu, b_glu=m_b_glu, na_rpb=m_na_rpb, w_out=m_w_out,
                 w_ple=m_w_ple, ple_norm=m_ple_norm, w_ple_gate=m_w_ple_gate)
    mom_v = dict(norm_pre=v_norm_pre, norm_post=v_norm_post, w_in=v_w_in, ssm_a_re=v_ssm_a_re, ssm_a_im=v_ssm_a_im,
                 ssm_log_dt=v_ssm_log_dt, ssm_b_re=v_ssm_b_re, ssm_b_im=v_ssm_b_im, ssm_c_re=v_ssm_c_re,
                 ssm_c_im=v_ssm_c_im, ssm_d=v_ssm_d, w_glu=v_w_glu, b_glu=v_b_glu, na_rpb=v_na_rpb, w_out=v_w_out,
                 w_ple=v_w_ple, ple_norm=v_ple_norm, w_ple_gate=v_w_ple_gate)

    shards = [wts[n][0].astype(BF16) for n in _BIG]
    w_in_g, w_glu_g, w_out_g, w_ple_g, w_pg_g = _gather_chips(shards, "gather_weights")
    loss_part, grad_x, small, big = _local_grads(
        x[0], p[0, 0], loss_target[0], wts, w_in_g, w_glu_g.reshape(512, 512), w_out_g.reshape(1024, 1024),
        w_ple_g, w_pg_g.reshape(1024, 1024))
    loss = lax.psum(loss_part[0, 0], ("x", "y", "c"))

    small_packed = _pack_small(small).reshape(N_CHIPS, _SMALL_ROWS // N_CHIPS, 128)
    reduced = _reduce_scatter([big[n] for n in _BIG] + [small_packed], [BF16] * len(_BIG) + [F32])
    grads = dict(zip(_BIG, reduced[:-1]))
    (small_all,) = _gather_chips([reduced[-1]], "gather_small_grads")
    small_all = small_all.reshape(_SMALL_ROWS, 128)

    delta, new_m, new_v = {}, {}, {}
    for n in _BIG:
        shp = wts[n].shape
        d_, m_, v_ = _adamw(wts[n][0], grads[n], mom_m[n][0], mom_v[n][0])
        grads[n] = grads[n].reshape(shp)
        delta[n], new_m[n], new_v[n] = d_.reshape(shp), m_.reshape(shp), v_.reshape(shp)
    shapes = {n: wts[n].shape for n in _SMALL}
    d_s, m_s, v_s = _adamw(_pack_small(wts), small_all, _pack_small(mom_m), _pack_small(mom_v))
    for dst, packed in ((grads, small_all), (delta, d_s), (new_m, m_s), (new_v, v_s)):
        dst.update(_unpack_small(packed, shapes))

    return (loss, grad_x[None], *[grads[n] for n in _WEIGHTS], *[delta[n] for n in _WEIGHTS],
            *[new_m[n] for n in _WEIGHTS], *[new_v[n] for n in _WEIGHTS])
```

```python
import functools
import math

import jax
import jax.numpy as jnp
import numpy as np
from jax import lax
from jax.experimental import pallas as pl
from jax.experimental.pallas import tpu as pltpu

F32 = jnp.float32
BF16 = jnp.bfloat16

D_MODEL = 1024
D_PLE = 256
GRID_W = 64
D_SSM = 512
SSM_GROUP = 16
N_GROUPS = 32
SSM_STATE = 64
D_NA = 512
NA_HEADS = 8
NA_HEAD_DIM = 64
NA_ROWS = 8
NA_COLS = 16
D_IN_PROJ = 3072
EPS = 1e-6

CHUNK = 16
GROUPS_PER_BLOCK = 8
N_BLOCKS = N_GROUPS // GROUPS_PER_BLOCK
BLOCK_CH = GROUPS_PER_BLOCK * SSM_GROUP
BLOCK_ST = GROUPS_PER_BLOCK * SSM_STATE
CHUNK_W = CHUNK * BLOCK_CH
STATE_W = 4 * BLOCK_ST

N_CHIPS = 4
MESH = pl.DeviceIdType.MESH

ADAM_LR = 0.001
ADAM_B1 = 0.9
ADAM_B2 = 0.999
ADAM_EPS = 1e-08
ADAM_WD = 0.01
ADAM_STEP = 10

VMEM_LIMIT = 52 * 1024 * 1024
HIGHEST = lax.Precision.HIGHEST


def _cparams(sem=None, **kw):
    if sem is not None:
        kw["dimension_semantics"] = sem
    return pltpu.CompilerParams(vmem_limit_bytes=VMEM_LIMIT, **kw)


def _dot(a, b, dims=((1,), (0,))):
    return lax.dot_general(a, b, (dims, ((), ())), preferred_element_type=F32)


def _dot_nt(a, b):
    return _dot(a, b, ((1,), (1,)))


def _dot_tn(a, b):
    return _dot(a, b, ((0,), (0,)))


def _sigmoid(x):
    return 1.0 / (1.0 + jnp.exp(-x))


_GELU_C = math.sqrt(2.0 / math.pi)


def _gelu_parts(x):
    inner = _GELU_C * (x + 0.044715 * (x * x * x))
    t = jnp.tanh(inner)
    return 0.5 * x * (1.0 + t), t


def _gelu_grad(x, t):
    return 0.5 * (1.0 + t) + 0.5 * x * (1.0 - t * t) * (_GELU_C * (1.0 + 3.0 * 0.044715 * x * x))


def _silu_parts(z):
    s = _sigmoid(z)
    return z * s, s


def _silu_grad(z, s):
    return s * (1.0 + z * (1.0 - s))


def _rms(x):
    r = lax.rsqrt(jnp.mean(x * x, axis=-1, keepdims=True) + EPS)
    return x * r, r


def _rms_bwd(dn, n, r):
    return r * (dn - n * jnp.mean(dn * n, axis=-1, keepdims=True))


def _chunk_scratch(tm):
    return pltpu.VMEM((N_BLOCKS, tm, BLOCK_CH), F32)


def _store_chunks(val, scr, c_ref, dtype):
    nc = scr.shape[1] // CHUNK
    for b in range(N_BLOCKS):
        scr[b] = val[:, b * BLOCK_CH:(b + 1) * BLOCK_CH]
        for j in range(CHUNK):
            c_ref[b, :, j * BLOCK_CH:(j + 1) * BLOCK_CH] = scr[b, pl.ds(j, nc, stride=CHUNK), :].astype(dtype)


def _load_chunks(c_ref, scr):
    nc = scr.shape[1] // CHUNK
    for b in range(N_BLOCKS):
        for j in range(CHUNK):
            scr[b, pl.ds(j, nc, stride=CHUNK), :] = c_ref[b, :, j * BLOCK_CH:(j + 1) * BLOCK_CH].astype(F32)
    return jnp.concatenate([scr[b] for b in range(N_BLOCKS)], axis=1)


def _chunk_spec(tm):
    return pl.BlockSpec((N_BLOCKS, tm // CHUNK, CHUNK_W), lambda i: (0, i, 0))


def _heads_t_spec(tm):
    return pl.BlockSpec((D_NA, tm), lambda i: (0, i))


def _in_proj(x, g_pre, w_in_g, tm=256):
    L = x.shape[0]
    wn = w_in_g.shape[2]

    def body(x_ref, g_ref, w_ref, uc_ref, zs_ref, qt_ref, q_ref, kt_ref, k_ref, vt_ref, v_ref, zn_ref, u_scr):
        n, _ = _rms(x_ref[...])
        hn = (n * g_ref[...]).astype(BF16)
        proj = jnp.concatenate([_dot(hn, w_ref[j]) for j in range(N_CHIPS)], axis=1)
        _store_chunks(proj[:, 0:512], u_scr, uc_ref, BF16)
        zs_ref[...] = proj[:, 512:1024]
        q = proj[:, 1024:1536] * (NA_HEAD_DIM ** -0.5)
        for val, t_ref, n_ref in ((q, qt_ref, q_ref), (proj[:, 1536:2048], kt_ref, k_ref), (proj[:, 2048:2560], vt_ref, v_ref)):
            t_ref[...] = val.T.astype(BF16)
            n_ref[...] = val.astype(BF16)
        zn_ref[...] = proj[:, 2560:3072]

    tok = jax.ShapeDtypeStruct((L, 512), F32)
    tr = jax.ShapeDtypeStruct((D_NA, L), BF16)
    hm = jax.ShapeDtypeStruct((L, D_NA), BF16)
    tspec = pl.BlockSpec((tm, 512), lambda i: (i, 0))
    return pl.pallas_call(
        body, name="in_proj", grid=(L // tm,),
        in_specs=[pl.BlockSpec((tm, D_MODEL), lambda i: (i, 0)),
                  pl.BlockSpec((1, D_MODEL), lambda i: (0, 0)),
                  pl.BlockSpec((N_CHIPS, D_MODEL, wn), lambda i: (0, 0, 0))],
        out_specs=[_chunk_spec(tm), tspec] + [_heads_t_spec(tm), tspec] * 3 + [tspec],
        out_shape=[jax.ShapeDtypeStruct((N_BLOCKS, L // CHUNK, CHUNK_W), BF16), tok, tr, hm, tr, hm, tr, hm, tok],
        scratch_shapes=[_chunk_scratch(tm)],
        compiler_params=_cparams(("arbitrary",)),
    )(x, g_pre, w_in_g)


def _ssm_block_params(a_re, a_im, log_dt, b_re, b_im, c_re, c_im, d):
    eye_g = jnp.eye(GROUPS_PER_BLOCK, dtype=F32)[None, None, :, None, :, None]

    def lanes(t):
        return t.reshape(2, N_BLOCKS, 1, BLOCK_ST)

    def expand(t):
        return (t[:, :, :, :, None, :] * eye_g).reshape(2, N_BLOCKS, BLOCK_CH, BLOCK_ST)

    b_shape = (2, N_BLOCKS, GROUPS_PER_BLOCK, SSM_STATE, SSM_GROUP)
    c_shape = (2, N_BLOCKS, GROUPS_PER_BLOCK, SSM_GROUP, SSM_STATE)
    return (lanes(a_re), lanes(a_im), lanes(jnp.broadcast_to(log_dt[..., None], a_re.shape)),
            expand(b_re.reshape(b_shape).transpose(0, 1, 2, 4, 3)), expand(b_im.reshape(b_shape).transpose(0, 1, 2, 4, 3)),
            expand(c_re.reshape(c_shape)), expand(c_im.reshape(c_shape)), d.reshape(N_BLOCKS, 1, BLOCK_CH))


def _ssm_discretise(ar, ai, ldt):
    dt = jnp.exp(ldt)
    mag = jnp.exp(dt * ar)
    abr = mag * jnp.cos(dt * ai)
    abi = mag * jnp.sin(dt * ai)
    num_re = abr - 1.0
    num_im = abi
    denom = ar * ar + ai * ai
    coef_re = (num_re * ar + num_im * ai) / denom
    coef_im = (num_im * ar - num_re * ai) / denom
    return abr, abi, coef_re, coef_im


_POW_ROWS = 24


def _ssm_fill_powers(ar_ref, ai_ref, ldt_ref, br_ref, bi_ref, pw_ref, bbar_ref):
    for d in range(2):
        abr, abi, cfr, cfi = _ssm_discretise(ar_ref[d, 0], ai_ref[d, 0], ldt_ref[d, 0])
        bbar_ref[d, 0] = cfr * br_ref[d, 0] - cfi * bi_ref[d, 0]
        bbar_ref[d, 1] = cfr * bi_ref[d, 0] + cfi * br_ref[d, 0]
        pr, pi = jnp.ones_like(abr), jnp.zeros_like(abi)
        for t in range(CHUNK + 1):
            pw_ref[d, 0, t:t + 1, :] = pr
            pw_ref[d, 1, t:t + 1, :] = pi
            pr, pi = pr * abr - pi * abi, pr * abi + pi * abr


def _dot_hi(a, b, dims=((1,), (0,))):
    return lax.dot_general(a, b, (dims, ((), ())), precision=lax.Precision.HIGH, preferred_element_type=F32)


def _ssm_stack_inputs(d, pw_ref, bbar_ref, xs_ref):
    for t in range(CHUNK):
        pr, pi = pw_ref[d, 0, t:t + 1, :], pw_ref[d, 1, t:t + 1, :]
        xs_ref[0, t * BLOCK_CH:(t + 1) * BLOCK_CH, :] = bbar_ref[d, 0] * pr - bbar_ref[d, 1] * pi
        xs_ref[1, t * BLOCK_CH:(t + 1) * BLOCK_CH, :] = bbar_ref[d, 0] * pi + bbar_ref[d, 1] * pr


def _eye(n):
    return (lax.broadcasted_iota(jnp.int32, (n, n), 0) == lax.broadcasted_iota(jnp.int32, (n, n), 1)).astype(F32)


def _ssm_param_specs():
    vec = pl.BlockSpec((2, 1, 1, BLOCK_ST), lambda b, j: (0, b, 0, 0))
    mat = pl.BlockSpec((2, 1, BLOCK_CH, BLOCK_ST), lambda b, j: (0, b, 0, 0))
    return [vec, vec, vec, mat, mat, mat, mat, pl.BlockSpec((1, 1, BLOCK_CH), lambda b, j: (b, 0, 0))]


def _ssm_chunk_matrices(blk):
    def body(ar_ref, ai_ref, ldt_ref, br_ref, bi_ref, cr_ref, ci_ref, d_ref,
             m_ref, ws_ref, wot_ref, a16_ref, pw_ref, bbar_ref, lag_ref, xs_ref):
        j = pl.program_id(1)

        @pl.when(j == 0)
        def _():
            _ssm_fill_powers(ar_ref, ai_ref, ldt_ref, br_ref, bi_ref, pw_ref, bbar_ref)
            zero_lag = d_ref[0] * _eye(BLOCK_CH)
            for d in range(2):
                _ssm_stack_inputs(d, pw_ref, bbar_ref, xs_ref)
                taps = (_dot_hi(xs_ref[0], cr_ref[d, 0], ((1,), (1,)))
                        - _dot_hi(xs_ref[1], ci_ref[d, 0], ((1,), (1,))))
                zero_lag = zero_lag + taps[0:BLOCK_CH]
                for t in range(1, CHUNK):
                    lag_ref[CHUNK - 1 + t if d == 0 else CHUNK - 1 - t] = taps[t * BLOCK_CH:(t + 1) * BLOCK_CH]
            lag_ref[CHUNK - 1] = zero_lag
            a16_ref[0] = jnp.concatenate([pw_ref[d, ri, CHUNK:CHUNK + 1, :] for d in range(2) for ri in range(2)], axis=1)

        m_ref[0] = jnp.concatenate([lag_ref[jp - j + CHUNK - 1] for jp in range(CHUNK)], axis=1).astype(BF16)

        def power(d, t):
            return pw_ref[d, 0, pl.ds(t, 1), :], pw_ref[d, 1, pl.ds(t, 1), :]

        parts = []
        for d, t in ((0, CHUNK - 1 - j), (1, j)):
            pr, pi = power(d, t)
            parts += [bbar_ref[d, 0] * pr - bbar_ref[d, 1] * pi, bbar_ref[d, 0] * pi + bbar_ref[d, 1] * pr]
        ws_ref[0] = jnp.concatenate(parts, axis=1).astype(BF16)
        parts = []
        for d, t in ((0, j + 1), (1, CHUNK - j)):
            pr, pi = power(d, t)
            parts += [cr_ref[d, 0] * pr - ci_ref[d, 0] * pi, -cr_ref[d, 0] * pi - ci_ref[d, 0] * pr]
        wot_ref[0] = jnp.concatenate(parts, axis=1).astype(BF16)

    row = pl.BlockSpec((1, BLOCK_CH, CHUNK_W), lambda b, j: (b, j, 0))
    mat = jax.ShapeDtypeStruct((N_BLOCKS, CHUNK_W, CHUNK_W), BF16)
    return pl.pallas_call(
        body, name="ssm_chunk_matrices", grid=(N_BLOCKS, CHUNK),
        in_specs=_ssm_param_specs(),
        out_specs=[row, row, row, pl.BlockSpec((1, 1, STATE_W), lambda b, j: (b, 0, 0))],
        out_shape=[mat, mat, mat, jax.ShapeDtypeStruct((N_BLOCKS, 1, STATE_W), F32)],
        scratch_shapes=[pltpu.VMEM((2, 2, _POW_ROWS, BLOCK_ST), F32), pltpu.VMEM((2, 2, BLOCK_CH, BLOCK_ST), F32),
                        pltpu.VMEM((2 * CHUNK, BLOCK_CH, BLOCK_CH), F32), pltpu.VMEM((2, CHUNK_W, BLOCK_ST), F32)],
        compiler_params=_cparams(("arbitrary", "arbitrary")),
    )(*blk)


def _ssm_chunk_matrices_bwd(blk, d_m, d_ws, d_wot, d_a16):
    def body(ar_ref, ai_ref, ldt_ref, br_ref, bi_ref, cr_ref, ci_ref, d_ref, dm_ref, dws_ref, dwot_ref, da16_ref,
             dar_ref, dai_ref, dldt_ref, dbr_ref, dbi_ref, dcr_ref, dci_ref, dd_ref,
             pw_ref, bbar_ref, dlag_ref, dbbar_ref, dc_ref, dpw_ref, xs_ref, dts_ref):
        j = pl.program_id(1)
        w = BLOCK_ST

        @pl.when(j == 0)
        def _():
            _ssm_fill_powers(ar_ref, ai_ref, ldt_ref, br_ref, bi_ref, pw_ref, bbar_ref)
            for r in (dlag_ref, dbbar_ref, dc_ref, dpw_ref):
                r[...] = jnp.zeros_like(r)

        def x_chain(d, t, dxr, dxi):
            pr, pi = pw_ref[d, 0, pl.ds(t, 1), :], pw_ref[d, 1, pl.ds(t, 1), :]
            bbr, bbi = bbar_ref[d, 0], bbar_ref[d, 1]
            dbbar_ref[d, 0] += dxr * pr + dxi * pi
            dbbar_ref[d, 1] += dxi * pr - dxr * pi
            dpw_ref[d, 0, pl.ds(t, 1), :] += jnp.sum(dxr * bbr + dxi * bbi, axis=0, keepdims=True)
            dpw_ref[d, 1, pl.ds(t, 1), :] += jnp.sum(dxi * bbr - dxr * bbi, axis=0, keepdims=True)

        def z_chain(d, t, dzr, dzi):
            pr, pi = pw_ref[d, 0, pl.ds(t, 1), :], pw_ref[d, 1, pl.ds(t, 1), :]
            c_r, c_i = cr_ref[d, 0], ci_ref[d, 0]
            dc_ref[d, 0] += dzr * pr - dzi * pi
            dc_ref[d, 1] += -dzr * pi - dzi * pr
            dpw_ref[d, 0, pl.ds(t, 1), :] += jnp.sum(dzr * c_r - dzi * c_i, axis=0, keepdims=True)
            dpw_ref[d, 1, pl.ds(t, 1), :] += jnp.sum(-dzr * c_i - dzi * c_r, axis=0, keepdims=True)

        for jp in range(CHUNK):
            dlag_ref[jp - j + CHUNK - 1] += dm_ref[0, :, jp * BLOCK_CH:(jp + 1) * BLOCK_CH]
        x_chain(0, CHUNK - 1 - j, dws_ref[0, :, 0:w], dws_ref[0, :, w:2 * w])
        x_chain(1, j, dws_ref[0, :, 2 * w:3 * w], dws_ref[0, :, 3 * w:4 * w])
        z_chain(0, j + 1, dwot_ref[0, :, 0:w], dwot_ref[0, :, w:2 * w])
        z_chain(1, CHUNK - j, dwot_ref[0, :, 2 * w:3 * w], dwot_ref[0, :, 3 * w:4 * w])

        @pl.when(j == CHUNK - 1)
        def _():
            for d in range(2):
                _ssm_stack_inputs(d, pw_ref, bbar_ref, xs_ref)
                for t in range(CHUNK):
                    dts_ref[t * BLOCK_CH:(t + 1) * BLOCK_CH, :] = dlag_ref[CHUNK - 1 + t if d == 0 else CHUNK - 1 - t]
                d_taps = dts_ref[...]
                dc_ref[d, 0] += _dot_hi(d_taps, xs_ref[0], ((0,), (0,)))
                dc_ref[d, 1] -= _dot_hi(d_taps, xs_ref[1], ((0,), (0,)))
                xs_ref[0] = _dot_hi(d_taps, cr_ref[d, 0])
                xs_ref[1] = -_dot_hi(d_taps, ci_ref[d, 0])
                for t in range(CHUNK):
                    rows = slice(t * BLOCK_CH, (t + 1) * BLOCK_CH)
                    x_chain(d, t, xs_ref[0, rows, :], xs_ref[1, rows, :])
            dd_ref[0] = jnp.sum(dlag_ref[CHUNK - 1] * _eye(BLOCK_CH), axis=0, keepdims=True)
            for d in range(2):
                (abr, abi, cfr, cfi), disc_vjp = jax.vjp(_ssm_discretise, ar_ref[d, 0], ai_ref[d, 0], ldt_ref[d, 0])
                dpr = dpw_ref[d, 0, CHUNK:CHUNK + 1, :] + da16_ref[0, :, 2 * d * w:(2 * d + 1) * w]
                dpi = dpw_ref[d, 1, CHUNK:CHUNK + 1, :] + da16_ref[0, :, (2 * d + 1) * w:(2 * d + 2) * w]
                dabr, dabi = jnp.zeros_like(abr), jnp.zeros_like(abi)
                for t in range(CHUNK, 0, -1):
                    qr, qi = pw_ref[d, 0, t - 1:t, :], pw_ref[d, 1, t - 1:t, :]
                    dabr = dabr + dpr * qr + dpi * qi
                    dabi = dabi + dpi * qr - dpr * qi
                    dpr, dpi = (dpr * abr + dpi * abi + dpw_ref[d, 0, t - 1:t, :],
                                dpi * abr - dpr * abi + dpw_ref[d, 1, t - 1:t, :])
                dbbr, dbbi = dbbar_ref[d, 0], dbbar_ref[d, 1]
                b_r, b_i = br_ref[d, 0], bi_ref[d, 0]
                dbr_ref[d, 0] = cfr * dbbr + cfi * dbbi
                dbi_ref[d, 0] = cfr * dbbi - cfi * dbbr
                dcfr = jnp.sum(b_r * dbbr + b_i * dbbi, axis=0, keepdims=True)
                dcfi = jnp.sum(b_r * dbbi - b_i * dbbr, axis=0, keepdims=True)
                dar_ref[d, 0], dai_ref[d, 0], dldt_ref[d, 0] = disc_vjp((dabr, dabi, dcfr, dcfi))
                dcr_ref[d, 0] = dc_ref[d, 0]
                dci_ref[d, 0] = dc_ref[d, 1]

    row = pl.BlockSpec((1, BLOCK_CH, CHUNK_W), lambda b, j: (b, j, 0))
    specs = _ssm_param_specs()
    acc = lambda *s: pltpu.VMEM(s, F32)
    return pl.pallas_call(
        body, name="ssm_chunk_matrices_bwd", grid=(N_BLOCKS, CHUNK),
        in_specs=specs + [row, row, row, pl.BlockSpec((1, 1, STATE_W), lambda b, j: (b, 0, 0))],
        out_specs=specs,
        out_shape=[jax.ShapeDtypeStruct(t.shape, F32) for t in blk],
        scratch_shapes=[acc(2, 2, _POW_ROWS, BLOCK_ST), acc(2, 2, BLOCK_CH, BLOCK_ST), acc(2 * CHUNK, BLOCK_CH, BLOCK_CH),
                        acc(2, 2, BLOCK_CH, BLOCK_ST), acc(2, 2, BLOCK_CH, BLOCK_ST), acc(2, 2, _POW_ROWS, BLOCK_ST),
                        acc(2, CHUNK_W, BLOCK_ST), acc(CHUNK_W, BLOCK_CH)],
        compiler_params=_cparams(("arbitrary", "arbitrary")),
    )(*blk, d_m, d_ws, d_wot, d_a16)


def _block_matmul(terms, name, out_dtype=F32, tn=1024):
    nc = terms[0][0].shape[1]
    n_out = terms[0][1].shape[1] if terms[0][2] else terms[0][1].shape[2]
    flags = [t[2] for t in terms]

    def body(*refs):
        out_ref = refs[-1]
        acc = None
        for t, transposed in enumerate(flags):
            a = refs[2 * t][0].astype(BF16)
            w = refs[2 * t + 1][0]
            part = _dot_nt(a, w) if transposed else _dot(a, w)
            acc = part if acc is None else acc + part
        out_ref[0] = acc.astype(out_dtype)

    in_specs, args = [], []
    for a, w, transposed in terms:
        k = a.shape[2]
        in_specs.append(pl.BlockSpec((1, nc, k), lambda b, n: (b, 0, 0)))
        if transposed:
            in_specs.append(pl.BlockSpec((1, tn, k), lambda b, n: (b, n, 0)))
        else:
            in_specs.append(pl.BlockSpec((1, k, tn), lambda b, n: (b, 0, n)))
        args += [a, w]
    return pl.pallas_call(
        body, name=name, grid=(N_BLOCKS, n_out // tn), in_specs=in_specs,
        out_specs=pl.BlockSpec((1, nc, tn), lambda b, n: (b, 0, n)),
        out_shape=jax.ShapeDtypeStruct((N_BLOCKS, nc, n_out), out_dtype),
        compiler_params=_cparams(("arbitrary", "arbitrary")),
    )(*args)


def _block_matmul_tn(a, b, name, tile=1024):
    nc, m = a.shape[1], a.shape[2]
    n = b.shape[2]

    def body(a_ref, b_ref, out_ref):
        out_ref[0] = _dot_tn(a_ref[0].astype(BF16), b_ref[0].astype(BF16))

    return pl.pallas_call(
        body, name=name, grid=(N_BLOCKS, m // tile, n // tile),
        in_specs=[pl.BlockSpec((1, nc, tile), lambda blk, i, j: (blk, 0, i)),
                  pl.BlockSpec((1, nc, tile), lambda blk, i, j: (blk, 0, j))],
        out_specs=pl.BlockSpec((1, tile, tile), lambda blk, i, j: (blk, i, j)),
        out_shape=jax.ShapeDtypeStruct((N_BLOCKS, m, n), F32),
        compiler_params=_cparams(("arbitrary", "arbitrary", "arbitrary")),
    )(a, b)


def _cmul(ar, ai, xr, xi):
    return ar * xr - ai * xi, ar * xi + ai * xr


def _cmul_conj(ar, ai, xr, xi):
    return ar * xr + ai * xi, ar * xi - ai * xr


def _ssm_state_scan(s_in, a16):
    nc = s_in.shape[1]
    w = BLOCK_ST

    def body(sin_ref, a_ref, out_ref):
        a = a_ref[0]
        afr, afi, abr, abi = a[:, 0:w], a[:, w:2 * w], a[:, 2 * w:3 * w], a[:, 3 * w:4 * w]

        def step(c, carry):
            fr, fi, br, bi = carry
            cb = nc - 1 - c
            out_ref[0, pl.ds(c, 1), 0:w] = fr
            out_ref[0, pl.ds(c, 1), w:2 * w] = fi
            out_ref[0, pl.ds(cb, 1), 2 * w:3 * w] = br
            out_ref[0, pl.ds(cb, 1), 3 * w:4 * w] = bi
            nfr, nfi = _cmul(afr, afi, fr, fi)
            nbr, nbi = _cmul(abr, abi, br, bi)
            return (nfr + sin_ref[0, pl.ds(c, 1), 0:w], nfi + sin_ref[0, pl.ds(c, 1), w:2 * w],
                    nbr + sin_ref[0, pl.ds(cb, 1), 2 * w:3 * w], nbi + sin_ref[0, pl.ds(cb, 1), 3 * w:4 * w])

        z = jnp.zeros((1, w), F32)
        lax.fori_loop(0, nc, step, (z, z, z, z))

    spec = pl.BlockSpec((1, nc, STATE_W), lambda b: (b, 0, 0))
    return pl.pallas_call(
        body, name="ssm_state_scan", grid=(N_BLOCKS,),
        in_specs=[spec, pl.BlockSpec((1, 1, STATE_W), lambda b: (b, 0, 0))],
        out_specs=spec, out_shape=jax.ShapeDtypeStruct(s_in.shape, F32),
        compiler_params=_cparams(("arbitrary",)),
    )(s_in, a16)


def _ssm_state_scan_bwd(d_prev, s_prev, a16):
    nc = d_prev.shape[1]
    w = BLOCK_ST

    def body(dp_ref, sp_ref, a_ref, g_ref, da_ref):
        a = a_ref[0]
        afr, afi, abr, abi = a[:, 0:w], a[:, w:2 * w], a[:, 2 * w:3 * w], a[:, 3 * w:4 * w]

        def step(i, carry):
            gfr, gfi, gbr, gbi, dafr, dafi, dabr, dabi = carry
            cf = nc - 1 - i
            cb = i
            g_ref[0, pl.ds(cf, 1), 0:w] = gfr
            g_ref[0, pl.ds(cf, 1), w:2 * w] = gfi
            g_ref[0, pl.ds(cb, 1), 2 * w:3 * w] = gbr
            g_ref[0, pl.ds(cb, 1), 3 * w:4 * w] = gbi
            sfr, sfi = sp_ref[0, pl.ds(cf, 1), 0:w], sp_ref[0, pl.ds(cf, 1), w:2 * w]
            sbr, sbi = sp_ref[0, pl.ds(cb, 1), 2 * w:3 * w], sp_ref[0, pl.ds(cb, 1), 3 * w:4 * w]
            dafr = dafr + gfr * sfr + gfi * sfi
            dafi = dafi + gfi * sfr - gfr * sfi
            dabr = dabr + gbr * sbr + gbi * sbi
            dabi = dabi + gbi * sbr - gbr * sbi
            nfr, nfi = _cmul_conj(afr, afi, gfr, gfi)
            nbr, nbi = _cmul_conj(abr, abi, gbr, gbi)
            return (nfr + dp_ref[0, pl.ds(cf, 1), 0:w], nfi + dp_ref[0, pl.ds(cf, 1), w:2 * w],
                    nbr + dp_ref[0, pl.ds(cb, 1), 2 * w:3 * w], nbi + dp_ref[0, pl.ds(cb, 1), 3 * w:4 * w],
                    dafr, dafi, dabr, dabi)

        z = jnp.zeros((1, w), F32)
        res = lax.fori_loop(0, nc, step, (z,) * 8)
        da_ref[0] = jnp.concatenate(res[4:], axis=1)

    spec = pl.BlockSpec((1, nc, STATE_W), lambda b: (b, 0, 0))
    aspec = pl.BlockSpec((1, 1, STATE_W), lambda b: (b, 0, 0))
    return pl.pallas_call(
        body, name="ssm_state_scan_bwd", grid=(N_BLOCKS,),
        in_specs=[spec, spec, aspec], out_specs=[spec, aspec],
        out_shape=[jax.ShapeDtypeStruct(d_prev.shape, F32), jax.ShapeDtypeStruct((N_BLOCKS, 1, STATE_W), F32)],
        compiler_params=_cparams(("arbitrary",)),
    )(d_prev, s_prev, a16)


NA_PAIR = 2 * GRID_W
NA_WIN_ROWS = NA_ROWS + 2
NA_WIN = NA_WIN_ROWS * GRID_W
NA_PAIRS_PER_STEP = 8
NA_CASES = 5
NA_MASKED = -1e30


def _na_pair_window(m, rows):
    rs0 = jnp.clip(2 * m - NA_ROWS // 2, 0, rows - NA_ROWS)
    ws = jnp.minimum(rs0, rows - NA_WIN_ROWS)
    last = rows // 2 - 1
    case = jnp.where(m == 0, 0, jnp.where(m == 1, 1, jnp.where(m == last - 1, 3, jnp.where(m == last, 4, 2))))
    return ws, case


def _na_row_offsets(rows):
    last = rows // 2 - 1
    geom = []
    for m in (0, 1, 2, last - 1, last):
        ws = min(max(2 * m - NA_ROWS // 2, 0), rows - NA_ROWS, rows - NA_WIN_ROWS)
        per_case = []
        for i in range(NA_WIN_ROWS):
            pair = []
            for rr in range(2):
                r = 2 * m + rr
                rs = min(max(r - NA_ROWS // 2, 0), rows - NA_ROWS)
                pair.append(ws + i - r + NA_ROWS - 1 if rs <= ws + i < rs + NA_ROWS else None)
            per_case.append(pair)
        geom.append(per_case)
    return geom


def _na_col_select():
    qc = np.arange(NA_PAIR)[None, :] % GRID_W
    kc = np.arange(GRID_W)[:, None]
    dc = np.clip(kc - qc + NA_COLS - 1, 0, 2 * NA_COLS - 2)
    return jnp.asarray((np.arange(2 * NA_COLS - 1)[:, None, None] == dc[None]).astype(np.float32))


def _na_bias_rows(rpb):
    return jnp.einsum("hrd,dkl->hrkl", rpb, _na_col_select(), precision=HIGHEST)


def _na_col_window():
    qc = lax.broadcasted_iota(jnp.int32, (GRID_W, NA_PAIR), 1) % GRID_W
    kc = lax.broadcasted_iota(jnp.int32, (GRID_W, NA_PAIR), 0)
    cs = jnp.clip(qc - NA_COLS // 2, 0, GRID_W - NA_COLS)
    first_row = lax.broadcasted_iota(jnp.int32, (GRID_W, NA_PAIR), 1) < GRID_W
    return (kc >= cs) & (kc < cs + NA_COLS), first_row


def _na_bias_table(bias_rows, rows):
    geom = _na_row_offsets(rows)

    def body(br_ref, tab_ref):
        col_ok, first_row = _na_col_window()
        masked = jnp.full((GRID_W, NA_PAIR), NA_MASKED, F32)
        for case in range(NA_CASES):
            for i in range(NA_WIN_ROWS):
                d0, d1 = geom[case][i]
                t0 = masked if d0 is None else br_ref[0, d0]
                t1 = masked if d1 is None else br_ref[0, d1]
                tile = jnp.where(col_ok, jnp.where(first_row, t0, t1), NA_MASKED)
                tab_ref[0, case, i * GRID_W:(i + 1) * GRID_W, :] = tile

    return pl.pallas_call(
        body, name="na_bias_table", grid=(NA_HEADS,),
        in_specs=[pl.BlockSpec((1, 2 * NA_ROWS - 1, GRID_W, NA_PAIR), lambda h: (h, 0, 0, 0))],
        out_specs=pl.BlockSpec((1, NA_CASES, NA_WIN, NA_PAIR), lambda h: (h, 0, 0, 0)),
        out_shape=jax.ShapeDtypeStruct((NA_HEADS, NA_CASES, NA_WIN, NA_PAIR), F32),
        compiler_params=_cparams(("arbitrary",)),
    )(bias_rows)


def _na_bias_table_bwd(d_tab, rows):
    geom = _na_row_offsets(rows)

    def body(dt_ref, dbr_ref):
        col_ok, first_row = _na_col_window()
        acc = [None] * (2 * NA_ROWS - 1)
        for case in range(NA_CASES):
            for i in range(NA_WIN_ROWS):
                tile = jnp.where(col_ok, dt_ref[0, case, i * GRID_W:(i + 1) * GRID_W, :], 0.0)
                for rr, d in enumerate(geom[case][i]):
                    if d is not None:
                        part = jnp.where(first_row if rr == 0 else ~first_row, tile, 0.0)
                        acc[d] = part if acc[d] is None else acc[d] + part
        for d, a in enumerate(acc):
            dbr_ref[0, d] = jnp.zeros((GRID_W, NA_PAIR), F32) if a is None else a

    return pl.pallas_call(
        body, name="na_bias_table_bwd", grid=(NA_HEADS,),
        in_specs=[pl.BlockSpec((1, NA_CASES, NA_WIN, NA_PAIR), lambda h: (h, 0, 0, 0))],
        out_specs=pl.BlockSpec((1, 2 * NA_ROWS - 1, GRID_W, NA_PAIR), lambda h: (h, 0, 0, 0)),
        out_shape=jax.ShapeDtypeStruct((NA_HEADS, 2 * NA_ROWS - 1, GRID_W, NA_PAIR), F32),
        compiler_params=_cparams(("arbitrary",)),
    )(d_tab)


def _na_scores(k_win, q_t, bias):
    s = _dot(k_win, q_t) + bias
    e = jnp.exp(s - jnp.max(s, axis=0, keepdims=True))
    return e, jnp.sum(e, axis=0, keepdims=True)


def _head_rows(t, hh):
    row_head = lax.broadcasted_iota(jnp.int32, t.shape, 0) // NA_HEAD_DIM
    return jnp.where(row_head == hh, t, jnp.zeros_like(t))


def _heads_block_diag(t):
    lane_head = lax.broadcasted_iota(jnp.int32, t.shape, 1) // NA_HEAD_DIM
    zero = jnp.zeros_like(t)
    return jnp.concatenate([jnp.where(lane_head == 0, t, zero), jnp.where(lane_head == 1, t, zero)], axis=0)


def _na_fwd(q_t, k, v_t, bias_tab):
    L = k.shape[0]
    rows = L // GRID_W
    step_w = NA_PAIRS_PER_STEP * NA_PAIR

    def body(q_ref, k_ref, v_ref, bt_ref, o_ref):
        for pp in range(NA_PAIRS_PER_STEP):
            ws, case = _na_pair_window(pl.program_id(1) * NA_PAIRS_PER_STEP + pp, rows)
            start = pl.multiple_of(ws * GRID_W, NA_PAIR)
            lanes = slice(pp * NA_PAIR, (pp + 1) * NA_PAIR)
            k_win = k_ref[pl.ds(start, NA_WIN), :]
            q_pair = q_ref[:, lanes]
            for hh in range(2):
                hrows = slice(hh * NA_HEAD_DIM, (hh + 1) * NA_HEAD_DIM)
                e, l = _na_scores(k_win, _head_rows(q_pair, hh), bt_ref[hh, case])
                o_ref[hrows, lanes] = _dot(v_ref[hrows, pl.ds(start, NA_WIN)], e.astype(BF16)) / l

    q_spec = pl.BlockSpec((NA_PAIR, step_w), lambda h, s: (h, s))
    return pl.pallas_call(
        body, name="na_fwd", grid=(NA_HEADS // 2, L // step_w),
        in_specs=[q_spec, pl.BlockSpec((L, NA_PAIR), lambda h, s: (0, h)),
                  pl.BlockSpec((NA_PAIR, L), lambda h, s: (h, 0)),
                  pl.BlockSpec((2, NA_CASES, NA_WIN, NA_PAIR), lambda h, s: (h, 0, 0, 0))],
        out_specs=q_spec,
        out_shape=jax.ShapeDtypeStruct((D_NA, L), F32),
        compiler_params=_cparams(("arbitrary", "arbitrary")),
    )(q_t, k, v_t, bias_tab)


def _na_bwd(q_t, q, k_t, k, v, bias_tab, out_t, d_out_t, d_out):
    L = k.shape[0]
    rows = L // GRID_W
    step_w = NA_PAIRS_PER_STEP * NA_PAIR

    def body(qt_ref, q_ref, kt_ref, k_ref, v_ref, bt_ref, ot_ref, dot_ref, do_ref, dq_ref, dk_ref, dv_ref, dbt_ref):
        @pl.when(pl.program_id(1) == 0)
        def _():
            dk_ref[...] = jnp.zeros_like(dk_ref)
            dv_ref[...] = jnp.zeros_like(dv_ref)
            dbt_ref[...] = jnp.zeros_like(dbt_ref)

        for pp in range(NA_PAIRS_PER_STEP):
            ws, case = _na_pair_window(pl.program_id(1) * NA_PAIRS_PER_STEP + pp, rows)
            start = pl.multiple_of(ws * GRID_W, NA_PAIR)
            lanes = slice(pp * NA_PAIR, (pp + 1) * NA_PAIR)
            win = pl.ds(start, NA_WIN)
            k_win, v_win = k_ref[win, :], v_ref[win, :]
            q_pair = qt_ref[:, lanes]
            do_pair = dot_ref[:, lanes]
            do_pair_b = do_pair.astype(BF16)
            dss, ps = [], []
            for hh in range(2):
                hrows = slice(hh * NA_HEAD_DIM, (hh + 1) * NA_HEAD_DIM)
                e, l = _na_scores(k_win, _head_rows(q_pair, hh), bt_ref[hh, case])
                p = e / l
                dp = _dot(v_win, _head_rows(do_pair_b, hh))
                delta = jnp.sum(do_pair[hrows] * ot_ref[hrows, lanes], axis=0, keepdims=True)
                ds = p * (dp - delta)
                dbt_ref[hh, case] += ds
                dsb = ds.astype(BF16)
                dq_ref[hrows, lanes] = _dot(kt_ref[hrows, win], dsb) * (NA_HEAD_DIM ** -0.5)
                dss.append(dsb)
                ps.append(p.astype(BF16))
            tokens = slice(pp * NA_PAIR, (pp + 1) * NA_PAIR)
            dk_ref[win, :] += _dot(jnp.concatenate(dss, axis=1), _heads_block_diag(q_ref[tokens, :]))
            dv_ref[win, :] += _dot(jnp.concatenate(ps, axis=1), _heads_block_diag(do_ref[tokens, :]))

    t_tile = pl.BlockSpec((NA_PAIR, step_w), lambda h, s: (h, s))
    tile = pl.BlockSpec((step_w, NA_PAIR), lambda h, s: (s, h))
    t_full = pl.BlockSpec((NA_PAIR, L), lambda h, s: (h, 0))
    full = pl.BlockSpec((L, NA_PAIR), lambda h, s: (0, h))
    bt = pl.BlockSpec((2, NA_CASES, NA_WIN, NA_PAIR), lambda h, s: (h, 0, 0, 0))
    tok = jax.ShapeDtypeStruct((L, D_NA), F32)
    return pl.pallas_call(
        body, name="na_bwd", grid=(NA_HEADS // 2, L // step_w),
        in_specs=[t_tile, tile, t_full, full, full, bt, t_tile, t_tile, tile],
        out_specs=[t_tile, full, full, bt],
        out_shape=[jax.ShapeDtypeStruct((D_NA, L), F32), tok, tok, jax.ShapeDtypeStruct(bias_tab.shape, F32)],
        compiler_params=_cparams(("arbitrary", "arbitrary")),
    )(q_t, q, k_t, k, v, bias_tab, out_t, d_out_t, d_out)


def _branch_fwd_values(ys, zs, yn, zn, wglu, bglu):
    g1, t = _gelu_parts(ys)
    lin = _dot(g1.astype(BF16), wglu) + bglu
    sg = _sigmoid(lin)
    ys2 = g1 * sg
    sz, szs = _silu_parts(zs)
    sn, sns = _silu_parts(zn)
    return g1, t, sg, ys2, sz, szs, sn, sns


def _branch_fwd(y_ssm_c, z_s, y_na_t, z_n, w_glu, b_glu, tm=512):
    L = z_s.shape[0]

    def body(ys_ref, zs_ref, yn_ref, zn_ref, w_ref, b_ref, cat_ref, scr):
        yn = yn_ref[...].T
        g1, t, sg, ys2, sz, szs, sn, sns = _branch_fwd_values(
            _load_chunks(ys_ref, scr), zs_ref[...], yn, zn_ref[...], w_ref[...], b_ref[...])
        cat_ref[:, 0:512] = (ys2 * sz).astype(BF16)
        cat_ref[:, 512:1024] = (yn * sn).astype(BF16)

    tile = pl.BlockSpec((tm, 512), lambda i: (i, 0))
    return pl.pallas_call(
        body, name="branch_fwd", grid=(L // tm,),
        in_specs=[_chunk_spec(tm), tile, _heads_t_spec(tm), tile, pl.BlockSpec((512, 512), lambda i: (0, 0)),
                  pl.BlockSpec((1, 512), lambda i: (0, 0))],
        out_specs=pl.BlockSpec((tm, 1024), lambda i: (i, 0)),
        out_shape=jax.ShapeDtypeStruct((L, 1024), BF16),
        scratch_shapes=[_chunk_scratch(tm)],
        compiler_params=_cparams(("arbitrary",)),
    )(y_ssm_c, z_s, y_na_t, z_n, w_glu, b_glu)


def _branch_bwd(y_ssm_c, z_s, y_na_t, z_n, w_glu, b_glu, d_cat, tm=512):
    L = z_s.shape[0]

    def body(ys_ref, zs_ref, yn_ref, zn_ref, w_ref, b_ref, dc_ref,
             dys_ref, dzs_ref, dynt_ref, dyn_ref, dzn_ref, dw_ref, db_ref, scr):
        @pl.when(pl.program_id(0) == 0)
        def _():
            dw_ref[...] = jnp.zeros_like(dw_ref)
            db_ref[...] = jnp.zeros_like(db_ref)

        ys, zs, yn, zn = _load_chunks(ys_ref, scr), zs_ref[...], yn_ref[...].T, zn_ref[...]
        w = w_ref[...]
        g1, t, sg, ys2, sz, szs, sn, sns = _branch_fwd_values(ys, zs, yn, zn, w, b_ref[...])
        dys3 = dc_ref[:, 0:512]
        dyn2 = dc_ref[:, 512:1024]
        dzs_ref[...] = (dys3 * ys2 * _silu_grad(zs, szs)).astype(BF16)
        dys2 = dys3 * sz
        dlin = dys2 * g1 * sg * (1.0 - sg)
        dlb = dlin.astype(BF16)
        db_ref[...] += jnp.sum(dlin, axis=0, keepdims=True)
        dw_ref[...] += _dot_tn(g1.astype(BF16), dlb)
        dg1 = dys2 * sg + _dot_nt(dlb, w)
        _store_chunks(dg1 * _gelu_grad(ys, t), scr, dys_ref, BF16)
        dyn = dyn2 * sn
        dynt_ref[...] = dyn.T
        dyn_ref[...] = dyn.astype(BF16)
        dzn_ref[...] = (dyn2 * yn * _silu_grad(zn, sns)).astype(BF16)

    tile = pl.BlockSpec((tm, 512), lambda i: (i, 0))
    wspec = pl.BlockSpec((512, 512), lambda i: (0, 0))
    bspec = pl.BlockSpec((1, 512), lambda i: (0, 0))
    tok = jax.ShapeDtypeStruct((L, 512), BF16)
    return pl.pallas_call(
        body, name="branch_bwd", grid=(L // tm,),
        in_specs=[_chunk_spec(tm), tile, _heads_t_spec(tm), tile, wspec, bspec, pl.BlockSpec((tm, 1024), lambda i: (i, 0))],
        out_specs=[_chunk_spec(tm), tile, _heads_t_spec(tm), tile, tile, wspec, bspec],
        out_shape=[jax.ShapeDtypeStruct((N_BLOCKS, L // CHUNK, CHUNK_W), BF16), tok, jax.ShapeDtypeStruct((D_NA, L), F32),
                   tok, tok,
                   jax.ShapeDtypeStruct((512, 512), F32), jax.ShapeDtypeStruct((1, 512), F32)],
        scratch_shapes=[_chunk_scratch(tm)],
        compiler_params=_cparams(("arbitrary",)),
    )(y_ssm_c, z_s, y_na_t, z_n, w_glu, b_glu, d_cat)


def _head(x, p, target, cat, w_out, g_post, w_ple_g, g_ple, w_pg, tm=256):
    L = x.shape[0]
    pw = w_ple_g.shape[2]

    def body(x_ref, p_ref, t_ref, cat_ref, wo_ref, gpo_ref, wp_ref, gpl_ref, wg_ref,
             loss_ref, dh1_ref, dcat_ref, dwo_ref, dgpo_ref, dwp_ref, dgpl_ref, dwg_ref):
        @pl.when(pl.program_id(0) == 0)
        def _():
            for r in (loss_ref, dwo_ref, dgpo_ref, dwp_ref, dgpl_ref, dwg_ref):
                r[...] = jnp.zeros_like(r)

        cat_b = cat_ref[...]
        wo, wg = wo_ref[...], wg_ref[...]
        g_po, g_pl = gpo_ref[...], gpl_ref[...]
        mix = _dot(cat_b, wo)
        nm, r2 = _rms(mix)
        h1 = x_ref[...] + nm * g_po
        p_b = p_ref[...].astype(BF16)
        ep = jnp.concatenate([_dot(p_b, wp_ref[j]) for j in range(N_CHIPS)], axis=1)
        ne, r3 = _rms(ep)
        e = ne * g_pl
        h1_b = h1.astype(BF16)
        gate = _sigmoid(_dot(h1_b, wg))
        h2 = h1 + gate * e
        diff = h2 - t_ref[...]
        loss_ref[...] += (0.5 / D_MODEL) * jnp.sum(diff * diff).reshape(1, 1)

        dh2 = diff * (1.0 / D_MODEL)
        de = dh2 * gate
        dgl = (dh2 * e * gate * (1.0 - gate)).astype(BF16)
        dwg_ref[...] += _dot_tn(h1_b, dgl)
        dh1 = dh2 + _dot_nt(dgl, wg)
        dgpl_ref[...] += jnp.sum(de * ne, axis=0, keepdims=True)
        dep = _rms_bwd(de * g_pl, ne, r3).astype(BF16)
        for j in range(N_CHIPS):
            dwp_ref[j] += _dot_tn(p_b, dep[:, j * pw:(j + 1) * pw])
        dgpo_ref[...] += jnp.sum(dh1 * nm, axis=0, keepdims=True)
        dmix = _rms_bwd(dh1 * g_po, nm, r2).astype(BF16)
        dwo_ref[...] += _dot_tn(cat_b, dmix)
        dcat_ref[...] = _dot_nt(dmix, wo)
        dh1_ref[...] = dh1

    tile = lambda w: pl.BlockSpec((tm, w), lambda i: (i, 0))
    const = lambda *s: pl.BlockSpec(s, lambda i: (0,) * len(s))
    sds = jax.ShapeDtypeStruct
    return pl.pallas_call(
        body, name="head", grid=(L // tm,),
        in_specs=[tile(D_MODEL), tile(D_PLE), tile(D_MODEL), tile(1024), const(1024, D_MODEL), const(1, D_MODEL),
                  const(N_CHIPS, D_PLE, pw), const(1, D_MODEL), const(D_MODEL, D_MODEL)],
        out_specs=[const(1, 1), tile(D_MODEL), tile(1024), const(1024, D_MODEL), const(1, D_MODEL),
                   const(N_CHIPS, D_PLE, pw), const(1, D_MODEL), const(D_MODEL, D_MODEL)],
        out_shape=[sds((1, 1), F32), sds((L, D_MODEL), F32), sds((L, 1024), F32), sds((1024, D_MODEL), F32),
                   sds((1, D_MODEL), F32), sds((N_CHIPS, D_PLE, pw), F32), sds((1, D_MODEL), F32),
                   sds((D_MODEL, D_MODEL), F32)],
        compiler_params=_cparams(("arbitrary",)),
    )(x, p, target, cat, w_out, g_post, w_ple_g, g_ple, w_pg)


def _dproj_specs(tm):
    tile = pl.BlockSpec((tm, 512), lambda i: (i, 0))
    return [_chunk_spec(tm), tile, _heads_t_spec(tm), tile, tile, tile]


def _dproj_tile(refs, scr):
    du_ref, dzs_ref, dqt_ref, dk_ref, dv_ref, dzn_ref = refs
    parts = [_load_chunks(du_ref, scr), dzs_ref[...], dqt_ref[...].T, dk_ref[...], dv_ref[...], dzn_ref[...]]
    return jnp.concatenate([t.astype(BF16) for t in parts], axis=1)


def _in_proj_bwd_w(x, g_pre, dparts, tm=512):
    L = x.shape[0]
    wn = D_IN_PROJ // N_CHIPS

    def body(x_ref, g_ref, *refs):
        dw_ref, scr = refs[-2], refs[-1]

        @pl.when(pl.program_id(0) == 0)
        def _():
            dw_ref[...] = jnp.zeros_like(dw_ref)

        n, _ = _rms(x_ref[...])
        hn = (n * g_ref[...]).astype(BF16)
        dproj = _dproj_tile(refs[:-2], scr)
        for j in range(N_CHIPS):
            dw_ref[j] += _dot_tn(hn, dproj[:, j * wn:(j + 1) * wn])

    return pl.pallas_call(
        body, name="in_proj_bwd_w", grid=(L // tm,),
        in_specs=[pl.BlockSpec((tm, D_MODEL), lambda i: (i, 0)), pl.BlockSpec((1, D_MODEL), lambda i: (0, 0))] + _dproj_specs(tm),
        out_specs=pl.BlockSpec((N_CHIPS, D_MODEL, wn), lambda i: (0, 0, 0)),
        out_shape=jax.ShapeDtypeStruct((N_CHIPS, D_MODEL, wn), F32),
        scratch_shapes=[_chunk_scratch(tm)],
        compiler_params=_cparams(("arbitrary",)),
    )(x, g_pre, *dparts)


def _in_proj_bwd_x(x, g_pre, w_in_g, d_h1, dparts, tm=512):
    L = x.shape[0]
    wn = w_in_g.shape[2]

    def body(x_ref, g_ref, w_ref, dh1_ref, *refs):
        dx_ref, dg_ref, scr = refs[-3], refs[-2], refs[-1]

        @pl.when(pl.program_id(0) == 0)
        def _():
            dg_ref[...] = jnp.zeros_like(dg_ref)

        n, r = _rms(x_ref[...])
        dproj = _dproj_tile(refs[:-3], scr)
        dhn = _dot_nt(dproj[:, 0:wn], w_ref[0])
        for j in range(1, N_CHIPS):
            dhn = dhn + _dot_nt(dproj[:, j * wn:(j + 1) * wn], w_ref[j])
        dg_ref[...] += jnp.sum(dhn * n, axis=0, keepdims=True)
        dx_ref[...] = dh1_ref[...] + _rms_bwd(dhn * g_ref[...], n, r)

    wide = pl.BlockSpec((tm, D_MODEL), lambda i: (i, 0))
    vec = pl.BlockSpec((1, D_MODEL), lambda i: (0, 0))
    return pl.pallas_call(
        body, name="in_proj_bwd_x", grid=(L // tm,),
        in_specs=[wide, vec, pl.BlockSpec((N_CHIPS, D_MODEL, wn), lambda i: (0, 0, 0)), wide] + _dproj_specs(tm),
        out_specs=[wide, vec],
        out_shape=[jax.ShapeDtypeStruct((L, D_MODEL), F32), jax.ShapeDtypeStruct((1, D_MODEL), F32)],
        scratch_shapes=[_chunk_scratch(tm)],
        compiler_params=_cparams(("arbitrary",)),
    )(x, g_pre, w_in_g, d_h1, *dparts)


def _mesh_position():
    x, y, c = lax.axis_index("x"), lax.axis_index("y"), lax.axis_index("c")
    chips = [(1 - x, y), (x, 1 - y), (1 - x, 1 - y)]
    return x, y, c, chips


def _chip_index(cx, cy):
    return 2 * cx + cy


def _hbm_specs(n):
    return [pl.BlockSpec(memory_space=pl.ANY)] * n


def _gather_chips(shards, name):
    n = len(shards)

    def body(*refs):
        ins, outs = refs[:n], refs[n:2 * n]
        send1, recv1, send2, recv2, local = refs[2 * n:]
        x, y, c, chips = _mesh_position()
        me = _chip_index(x, y)
        sibling = (x, y, 1 - c)

        def half(ref, chip, core):
            hr = ref.shape[1] // 2
            return ref.at[chip, pl.ds(core * hr, hr)]

        copies, locals_ = [], []
        for a in range(n):
            lc = pltpu.make_async_copy(ins[a], outs[a].at[me], local.at[a])
            lc.start()
            locals_.append(lc)
            hr = ins[a].shape[0] // 2
            for j, chip in enumerate(chips):
                cp = pltpu.make_async_remote_copy(
                    src_ref=ins[a].at[pl.ds(c * hr, hr)], dst_ref=half(outs[a], me, c),
                    send_sem=send1.at[a, j], recv_sem=recv1.at[a, j], device_id=(*chip, c), device_id_type=MESH)
                cp.start()
                copies.append(cp)
        for a in range(n):
            for j, chip in enumerate(chips):
                landed = half(outs[a], _chip_index(*chip), c)
                pltpu.make_async_remote_copy(
                    src_ref=landed, dst_ref=landed, send_sem=send1.at[a, j], recv_sem=recv1.at[a, j],
                    device_id=(*chip, c), device_id_type=MESH).wait_recv()
                cp = pltpu.make_async_remote_copy(
                    src_ref=landed, dst_ref=landed, send_sem=send2.at[a, j], recv_sem=recv2.at[a, j],
                    device_id=sibling, device_id_type=MESH)
                cp.start()
                copies.append(cp)
        for a in range(n):
            for j, chip in enumerate(chips):
                other = half(outs[a], _chip_index(*chip), 1 - c)
                pltpu.make_async_remote_copy(
                    src_ref=other, dst_ref=other, send_sem=send2.at[a, j], recv_sem=recv2.at[a, j],
                    device_id=sibling, device_id_type=MESH).wait_recv()
        for cp in copies:
            cp.wait_send()
        for lc in locals_:
            lc.wait()

    sem = pltpu.SemaphoreType.DMA
    return pl.pallas_call(
        body, name=name, in_specs=_hbm_specs(n), out_specs=_hbm_specs(n),
        out_shape=[jax.ShapeDtypeStruct((N_CHIPS,) + s.shape, s.dtype) for s in shards],
        scratch_shapes=[sem((n, 3)), sem((n, 3)), sem((n, 3)), sem((n, 3)), sem((n,))],
        compiler_params=pltpu.CompilerParams(has_side_effects=True),
    )(*shards)


def _pair_exchange(grads):
    n = len(grads)

    def body(*refs):
        ins, outs = refs[:n], refs[n:2 * n]
        send, recv = refs[2 * n:]
        x, y, c, _ = _mesh_position()
        copies = []
        for a in range(n):
            hr = ins[a].shape[1] // 2
            cp = pltpu.make_async_remote_copy(
                src_ref=ins[a].at[:, pl.ds((1 - c) * hr, hr)], dst_ref=outs[a],
                send_sem=send.at[a], recv_sem=recv.at[a], device_id=(x, y, 1 - c), device_id_type=MESH)
            cp.start()
            copies.append(cp)
        for cp in copies:
            cp.wait()

    sem = pltpu.SemaphoreType.DMA
    return pl.pallas_call(
        body, name="pair_exchange", in_specs=_hbm_specs(n), out_specs=_hbm_specs(n),
        out_shape=[jax.ShapeDtypeStruct((g.shape[0], g.shape[1] // 2, g.shape[2]), g.dtype) for g in grads],
        scratch_shapes=[sem((n,)), sem((n,))],
        compiler_params=pltpu.CompilerParams(has_side_effects=True),
    )(*grads)


def _pair_add(core, grad, other, tr, out_dtype):
    hr = other.shape[1]
    cdim = other.shape[2]
    nb = hr // tr

    def body(core_ref, g_ref, o_ref, out_ref):
        out_ref[...] = (g_ref[...] + o_ref[...]).astype(out_dtype)

    return pl.pallas_call(
        body, name="pair_add",
        grid_spec=pltpu.PrefetchScalarGridSpec(
            num_scalar_prefetch=1, grid=(N_CHIPS, nb),
            in_specs=[pl.BlockSpec((1, tr, cdim), lambda j, i, core_ref: (j, core_ref[0] * nb + i, 0)),
                      pl.BlockSpec((1, tr, cdim), lambda j, i, core_ref: (j, i, 0))],
            out_specs=pl.BlockSpec((1, tr, cdim), lambda j, i, core_ref: (j, i, 0))),
        out_shape=jax.ShapeDtypeStruct(other.shape, out_dtype),
        compiler_params=_cparams(("arbitrary", "arbitrary")),
    )(core, grad, other)


def _chip_scatter(parts):
    n = len(parts)

    def body(*refs):
        ins, outs = refs[:n], refs[n:2 * n]
        send, recv, local = refs[2 * n:]
        x, y, c, chips = _mesh_position()
        me = _chip_index(x, y)
        copies = []
        for a in range(n):
            lc = pltpu.make_async_copy(ins[a].at[me], outs[a].at[me], local.at[a])
            lc.start()
            copies.append(lc)
            for j, chip in enumerate(chips):
                cp = pltpu.make_async_remote_copy(
                    src_ref=ins[a].at[_chip_index(*chip)], dst_ref=outs[a].at[me],
                    send_sem=send.at[a, j], recv_sem=recv.at[a, j], device_id=(*chip, c), device_id_type=MESH)
                cp.start()
                copies.append(cp)
        for cp in copies:
            cp.wait()

    sem = pltpu.SemaphoreType.DMA
    return pl.pallas_call(
        body, name="chip_scatter", in_specs=_hbm_specs(n), out_specs=_hbm_specs(n),
        out_shape=[jax.ShapeDtypeStruct(p.shape, p.dtype) for p in parts],
        scratch_shapes=[sem((n, 3)), sem((n, 3)), sem((n,))],
        compiler_params=pltpu.CompilerParams(has_side_effects=True),
    )(*parts)


def _chip_add(core, recv, tr):
    hr, cdim = recv.shape[1], recv.shape[2]
    nb = hr // tr

    def body(core_ref, r_ref, out_ref):
        out_ref[...] = ((r_ref[0].astype(F32) + r_ref[1].astype(F32)) + r_ref[2].astype(F32)) + r_ref[3].astype(F32)

    return pl.pallas_call(
        body, name="chip_add",
        grid_spec=pltpu.PrefetchScalarGridSpec(
            num_scalar_prefetch=1, grid=(nb,),
            in_specs=[pl.BlockSpec((N_CHIPS, tr, cdim), lambda i, core_ref: (0, i, 0))],
            out_specs=pl.BlockSpec((tr, cdim), lambda i, core_ref: (core_ref[0] * nb + i, 0))),
        out_shape=jax.ShapeDtypeStruct((2 * hr, cdim), F32),
        compiler_params=_cparams(("arbitrary",)),
    )(core, recv)


def _pair_gather(fulls):
    n = len(fulls)

    def body(*refs):
        outs = refs[n:2 * n]
        send, recv = refs[2 * n:]
        x, y, c, _ = _mesh_position()
        copies = []
        for a in range(n):
            hr = outs[a].shape[0] // 2
            mine = outs[a].at[pl.ds(c * hr, hr)]
            cp = pltpu.make_async_remote_copy(
                src_ref=mine, dst_ref=mine, send_sem=send.at[a], recv_sem=recv.at[a],
                device_id=(x, y, 1 - c), device_id_type=MESH)
            cp.start()
            copies.append(cp)
        for cp in copies:
            cp.wait()

    sem = pltpu.SemaphoreType.DMA
    return pl.pallas_call(
        body, name="pair_gather", in_specs=_hbm_specs(n), out_specs=_hbm_specs(n),
        out_shape=[jax.ShapeDtypeStruct(f.shape, f.dtype) for f in fulls],
        input_output_aliases={a: a for a in range(n)},
        scratch_shapes=[sem((n,)), sem((n,))],
        compiler_params=pltpu.CompilerParams(has_side_effects=True),
    )(*fulls)


def _row_tile(rows):
    for t in (512, 256, 128, 64, 32, 16, 8):
        if rows % t == 0:
            return t
    raise ValueError(rows)


def _reduce_scatter(grads, ici_dtypes):
    core = lax.axis_index("c").astype(jnp.int32).reshape(1)
    others = _pair_exchange(grads)
    pair = [_pair_add(core, g, o, _row_tile(o.shape[1]), dt) for g, o, dt in zip(grads, others, ici_dtypes)]
    landed = _chip_scatter(pair)
    return _pair_gather([_chip_add(core, r, _row_tile(r.shape[1])) for r in landed])


def _adamw(w, g, m, v):
    rows, cols = w.shape
    tr = _row_tile(rows) if rows % 8 == 0 else rows

    def body(w_ref, g_ref, m_ref, v_ref, d_ref, nm_ref, nv_ref):
        g_ = g_ref[...]
        m_ = ADAM_B1 * m_ref[...] + (1.0 - ADAM_B1) * g_
        v_ = ADAM_B2 * v_ref[...] + (1.0 - ADAM_B2) * (g_ * g_)
        m_hat = m_ / (1.0 - ADAM_B1 ** ADAM_STEP)
        v_hat = v_ / (1.0 - ADAM_B2 ** ADAM_STEP)
        d_ref[...] = -ADAM_LR * (m_hat / (jnp.sqrt(v_hat) + ADAM_EPS) + ADAM_WD * w_ref[...])
        nm_ref[...] = m_
        nv_ref[...] = v_

    spec = pl.BlockSpec((tr, cols), lambda i: (i, 0))
    shp = jax.ShapeDtypeStruct((rows, cols), F32)
    return pl.pallas_call(
        body, name="adamw", grid=(rows // tr,), in_specs=[spec] * 4, out_specs=[spec] * 3,
        out_shape=[shp] * 3, compiler_params=_cparams(("arbitrary",)),
    )(w, g, m, v)


_SMALL = ["norm_pre", "norm_post", "ssm_a_re", "ssm_a_im", "ssm_log_dt", "ssm_b_re", "ssm_b_im",
          "ssm_c_re", "ssm_c_im", "ssm_d", "b_glu", "na_rpb", "ple_norm"]
_BIG = ["w_in", "w_glu", "w_out", "w_ple", "w_ple_gate"]
_WEIGHTS = ["norm_pre", "norm_post", "w_in", "ssm_a_re", "ssm_a_im", "ssm_log_dt", "ssm_b_re", "ssm_b_im",
            "ssm_c_re", "ssm_c_im", "ssm_d", "w_glu", "b_glu", "na_rpb", "w_out", "w_ple", "ple_norm", "w_ple_gate"]
_SMALL_ROWS = 2176


def _pack_small(tensors):
    flat = jnp.concatenate([tensors[n].reshape(-1) for n in _SMALL])
    flat = jnp.pad(flat, (0, _SMALL_ROWS * 128 - flat.shape[0]))
    return flat.reshape(_SMALL_ROWS, 128)


def _unpack_small(packed, shapes):
    flat = packed.reshape(-1)
    out, off = {}, 0
    for n in _SMALL:
        size = int(np.prod(shapes[n]))
        out[n] = flat[off:off + size].reshape(shapes[n])
        off += size
    return out


def _local_grads(x, p, target, wts, w_in_g, w_glu, w_out, w_ple_g, w_pg):
    ssm_names = ["ssm_a_re", "ssm_a_im", "ssm_log_dt", "ssm_b_re", "ssm_b_im", "ssm_c_re", "ssm_c_im", "ssm_d"]
    ssm_params = [wts[n][0] for n in ssm_names]
    blk, blk_vjp = jax.vjp(_ssm_block_params, *ssm_params)
    m_mat, ws_mat, wot_mat, a16 = _ssm_chunk_matrices(blk)
    seq = x.shape[0]
    bias_rows, bias_rows_vjp = jax.vjp(_na_bias_rows, wts["na_rpb"][0])
    bias_tab = _na_bias_table(bias_rows, seq // GRID_W)

    u_c, z_s, q_t, q, k_t, k, v_t, v, z_n = _in_proj(x, wts["norm_pre"], w_in_g)
    s_in = _block_matmul([(u_c, ws_mat, False)], "ssm_chunk_states")
    s_prev = _ssm_state_scan(s_in, a16)
    y_ssm_c = _block_matmul([(u_c, m_mat, False), (s_prev, wot_mat, True)], "ssm_chunk_out")
    y_na_t = _na_fwd(q_t, k, v_t, bias_tab)
    cat = _branch_fwd(y_ssm_c, z_s, y_na_t, z_n, w_glu, wts["b_glu"])

    (loss, d_h1, d_cat, d_w_out, d_g_post, d_w_ple, d_g_ple, d_w_pg) = _head(
        x, p, target, cat, w_out, wts["norm_post"], w_ple_g, wts["ple_norm"], w_pg)
    dy_c, d_z_s, d_y_na_t, d_y_na, d_z_n, d_w_glu, d_b_glu = _branch_bwd(
        y_ssm_c, z_s, y_na_t, z_n, w_glu, wts["b_glu"], d_cat)
    d_q_t, d_k, d_v, d_bias_tab = _na_bwd(q_t, q, k_t, k, v, bias_tab, y_na_t, d_y_na_t, d_y_na)

    d_prev = _block_matmul([(dy_c, wot_mat, False)], "ssm_bwd_states")
    g_st, d_a16 = _ssm_state_scan_bwd(d_prev, s_prev, a16)
    d_u_c = _block_matmul([(dy_c, m_mat, True), (g_st, ws_mat, True)], "ssm_bwd_in", out_dtype=BF16)
    d_m = _block_matmul_tn(u_c, dy_c, "ssm_grad_m")
    d_ws = _block_matmul_tn(u_c, g_st, "ssm_grad_ws")
    d_wot = _block_matmul_tn(dy_c, s_prev, "ssm_grad_wot")
    d_ssm = blk_vjp(tuple(_ssm_chunk_matrices_bwd(blk, d_m, d_ws, d_wot, d_a16)))
    (d_rpb,) = bias_rows_vjp(_na_bias_table_bwd(d_bias_tab, seq // GRID_W))

    dparts = [d_u_c, d_z_s, d_q_t, d_k, d_v, d_z_n]
    d_w_in = _in_proj_bwd_w(x, wts["norm_pre"], dparts)
    grad_x, d_g_pre = _in_proj_bwd_x(x, wts["norm_pre"], w_in_g, d_h1, dparts)

    small = {"norm_pre": d_g_pre, "norm_post": d_g_post, "b_glu": d_b_glu, "na_rpb": d_rpb, "ple_norm": d_g_ple}
    for n, g in zip(ssm_names, d_ssm):
        small[n] = g
    big = {"w_in": d_w_in, "w_glu": d_w_glu.reshape(N_CHIPS, 128, 512), "w_out": d_w_out.reshape(N_CHIPS, 256, 1024),
           "w_ple": d_w_ple, "w_ple_gate": d_w_pg.reshape(N_CHIPS, 256, 1024)}
    return loss, grad_x, small, big


def kernel(x, p, norm_pre, norm_post, w_in, ssm_a_re, ssm_a_im, ssm_log_dt, ssm_b_re, ssm_b_im, ssm_c_re, ssm_c_im, ssm_d, w_glu, b_glu, na_rpb, w_out, w_ple, ple_norm, w_ple_gate, loss_target, m_norm_pre, m_norm_post, m_w_in, m_ssm_a_re, m_ssm_a_im, m_ssm_log_dt, m_ssm_b_re, m_ssm_b_im, m_ssm_c_re, m_ssm_c_im, m_ssm_d, m_w_glu, m_b_glu, m_na_rpb, m_w_out, m_w_ple, m_ple_norm, m_w_ple_gate, v_norm_pre, v_norm_post, v_w_in, v_ssm_a_re, v_ssm_a_im, v_ssm_log_dt, v_ssm_b_re, v_ssm_b_im, v_ssm_c_re, v_ssm_c_im, v_ssm_d, v_w_glu, v_b_glu, v_na_rpb, v_w_out, v_w_ple, v_ple_norm, v_w_ple_gate):
    wts = dict(norm_pre=norm_pre, norm_post=norm_post, w_in=w_in, ssm_a_re=ssm_a_re, ssm_a_im=ssm_a_im,
               ssm_log_dt=ssm_log_dt, ssm_b_re=ssm_b_re, ssm_b_im=ssm_b_im, ssm_c_re=ssm_c_re, ssm_c_im=ssm_c_im,
               ssm_d=ssm_d, w_glu=w_glu, b_glu=b_glu, na_rpb=na_rpb, w_out=w_out, w_ple=w_ple, ple_norm=ple_norm,
               w_ple_gate=w_ple_gate)
    mom_m = dict(norm_pre=m_norm_pre, norm_post=m_norm_post, w_in=m_w_in, ssm_a_re=m_ssm_a_re, ssm_a_im=m_ssm_a_im,
                 ssm_log_dt=m_ssm_log_dt, ssm_b_re=m_ssm_b_re, ssm_b_im=m_ssm_b_im, ssm_c_re=m_ssm_c_re,
                 ssm_c_im=m_ssm_c_im, ssm_d=m_ssm_d, w_glu=m_w_glu, b_glu=m_b_glu, na_rpb=m_na_rpb, w_out=m_w_out,
                 w_ple=m_w_ple, ple_norm=m_ple_norm, w_ple_gate=m_w_ple_gate)
    mom_v = dict(norm_pre=v_norm_pre, norm_post=v_norm_post, w_in=v_w_in, ssm_a_re=v_ssm_a_re, ssm_a_im=v_ssm_a_im,
                 ssm_log_dt=v_ssm_log_dt, ssm_b_re=v_ssm_b_re, ssm_b_im=v_ssm_b_im, ssm_c_re=v_ssm_c_re,
                 ssm_c_im=v_ssm_c_im, ssm_d=v_ssm_d, w_glu=v_w_glu, b_glu=v_b_glu, na_rpb=v_na_rpb, w_out=v_w_out,
                 w_ple=v_w_ple, ple_norm=v_ple_norm, w_ple_gate=v_w_ple_gate)

    shards = [wts[n][0].astype(BF16) for n in _BIG]
    w_in_g, w_glu_g, w_out_g, w_ple_g, w_pg_g = _gather_chips(shards, "gather_weights")
    loss_part, grad_x, small, big = _local_grads(
        x[0], p[0, 0], loss_target[0], wts, w_in_g, w_glu_g.reshape(512, 512), w_out_g.reshape(1024, 1024),
        w_ple_g, w_pg_g.reshape(1024, 1024))
    loss = lax.psum(loss_part[0, 0], ("x", "y", "c"))

    small_packed = _pack_small(small).reshape(N_CHIPS, _SMALL_ROWS // N_CHIPS, 128)
    reduced = _reduce_scatter([big[n] for n in _BIG] + [small_packed], [BF16] * len(_BIG) + [F32])
    grads = dict(zip(_BIG, reduced[:-1]))
    (small_all,) = _gather_chips([reduced[-1]], "gather_small_grads")
    small_all = small_all.reshape(_SMALL_ROWS, 128)

    delta, new_m, new_v = {}, {}, {}
    for n in _BIG:
        shp = wts[n].shape
        d_, m_, v_ = _adamw(wts[n][0], grads[n], mom_m[n][0], mom_v[n][0])
        grads[n] = grads[n].reshape(shp)
        delta[n], new_m[n], new_v[n] = d_.reshape(shp), m_.reshape(shp), v_.reshape(shp)
    shapes = {n: wts[n].shape for n in _SMALL}
    d_s, m_s, v_s = _adamw(_pack_small(wts), small_all, _pack_small(mom_m), _pack_small(mom_v))
    for dst, packed in ((grads, small_all), (delta, d_s), (new_m, m_s), (new_v, v_s)):
        dst.update(_unpack_small(packed, shapes))

    return (loss, grad_x[None], *[grads[n] for n in _WEIGHTS], *[delta[n] for n in _WEIGHTS],
            *[new_m[n] for n in _WEIGHTS], *[new_v[n] for n in _WEIGHTS])
```

```python
import functools
import math

import jax
import jax.numpy as jnp
import numpy as np
from jax import lax
from jax.experimental import pallas as pl
from jax.experimental.pallas import tpu as pltpu

F32 = jnp.float32
BF16 = jnp.bfloat16

D_MODEL = 1024
D_PLE = 256
GRID_W = 64
D_SSM = 512
SSM_GROUP = 16
N_GROUPS = 32
SSM_STATE = 64
D_NA = 512
NA_HEADS = 8
NA_HEAD_DIM = 64
NA_ROWS = 8
NA_COLS = 16
D_IN_PROJ = 3072
EPS = 1e-6

CHUNK = 16
GROUPS_PER_BLOCK = 8
N_BLOCKS = N_GROUPS // GROUPS_PER_BLOCK
BLOCK_CH = GROUPS_PER_BLOCK * SSM_GROUP
BLOCK_ST = GROUPS_PER_BLOCK * SSM_STATE
CHUNK_W = CHUNK * BLOCK_CH
STATE_W = 4 * BLOCK_ST

N_CHIPS = 4
MESH = pl.DeviceIdType.MESH

ADAM_LR = 0.001
ADAM_B1 = 0.9
ADAM_B2 = 0.999
ADAM_EPS = 1e-08
ADAM_WD = 0.01
ADAM_STEP = 10

VMEM_LIMIT = 52 * 1024 * 1024
HIGHEST = lax.Precision.HIGHEST


def _cparams(sem=None, **kw):
    if sem is not None:
        kw["dimension_semantics"] = sem
    return pltpu.CompilerParams(vmem_limit_bytes=VMEM_LIMIT, **kw)


def _dot(a, b, dims=((1,), (0,))):
    return lax.dot_general(a, b, (dims, ((), ())), preferred_element_type=F32)


def _dot_nt(a, b):
    return _dot(a, b, ((1,), (1,)))


def _dot_tn(a, b):
    return _dot(a, b, ((0,), (0,)))


def _sigmoid(x):
    return 1.0 / (1.0 + jnp.exp(-x))


_GELU_C = math.sqrt(2.0 / math.pi)


def _gelu_parts(x):
    inner = _GELU_C * (x + 0.044715 * (x * x * x))
    t = jnp.tanh(inner)
    return 0.5 * x * (1.0 + t), t


def _gelu_grad(x, t):
    return 0.5 * (1.0 + t) + 0.5 * x * (1.0 - t * t) * (_GELU_C * (1.0 + 3.0 * 0.044715 * x * x))


def _silu_parts(z):
    s = _sigmoid(z)
    return z * s, s


def _silu_grad(z, s):
    return s * (1.0 + z * (1.0 - s))


def _rms(x):
    r = lax.rsqrt(jnp.mean(x * x, axis=-1, keepdims=True) + EPS)
    return x * r, r


def _rms_bwd(dn, n, r):
    return r * (dn - n * jnp.mean(dn * n, axis=-1, keepdims=True))


def _chunk_scratch(tm):
    return pltpu.VMEM((N_BLOCKS, tm, BLOCK_CH), F32)


def _store_chunks(val, scr, c_ref, dtype):
    nc = scr.shape[1] // CHUNK
    for b in range(N_BLOCKS):
        scr[b] = val[:, b * BLOCK_CH:(b + 1) * BLOCK_CH]
        for j in range(CHUNK):
            c_ref[b, :, j * BLOCK_CH:(j + 1) * BLOCK_CH] = scr[b, pl.ds(j, nc, stride=CHUNK), :].astype(dtype)


def _load_chunks(c_ref, scr):
    nc = scr.shape[1] // CHUNK
    for b in range(N_BLOCKS):
        for j in range(CHUNK):
            scr[b, pl.ds(j, nc, stride=CHUNK), :] = c_ref[b, :, j * BLOCK_CH:(j + 1) * BLOCK_CH].astype(F32)
    return jnp.concatenate([scr[b] for b in range(N_BLOCKS)], axis=1)


def _chunk_spec(tm):
    return pl.BlockSpec((N_BLOCKS, tm // CHUNK, CHUNK_W), lambda i: (0, i, 0))


def _heads_t_spec(tm):
    return pl.BlockSpec((D_NA, tm), lambda i: (0, i))


def _in_proj(x, g_pre, w_in_g, tm=256):
    L = x.shape[0]
    wn = w_in_g.shape[2]

    def body(x_ref, g_ref, w_ref, uc_ref, zs_ref, qt_ref, q_ref, kt_ref, k_ref, vt_ref, v_ref, zn_ref, u_scr):
        n, _ = _rms(x_ref[...])
        hn = (n * g_ref[...]).astype(BF16)
        proj = jnp.concatenate([_dot(hn, w_ref[j]) for j in range(N_CHIPS)], axis=1)
        _store_chunks(proj[:, 0:512], u_scr, uc_ref, BF16)
        zs_ref[...] = proj[:, 512:1024]
        q = proj[:, 1024:1536] * (NA_HEAD_DIM ** -0.5)
        for val, t_ref, n_ref in ((q, qt_ref, q_ref), (proj[:, 1536:2048], kt_ref, k_ref), (proj[:, 2048:2560], vt_ref, v_ref)):
            t_ref[...] = val.T.astype(BF16)
            n_ref[...] = val.astype(BF16)
        zn_ref[...] = proj[:, 2560:3072]

    tok = jax.ShapeDtypeStruct((L, 512), F32)
    tr = jax.ShapeDtypeStruct((D_NA, L), BF16)
    hm = jax.ShapeDtypeStruct((L, D_NA), BF16)
    tspec = pl.BlockSpec((tm, 512), lambda i: (i, 0))
    return pl.pallas_call(
        body, name="in_proj", grid=(L // tm,),
        in_specs=[pl.BlockSpec((tm, D_MODEL), lambda i: (i, 0)),
                  pl.BlockSpec((1, D_MODEL), lambda i: (0, 0)),
                  pl.BlockSpec((N_CHIPS, D_MODEL, wn), lambda i: (0, 0, 0))],
        out_specs=[_chunk_spec(tm), tspec] + [_heads_t_spec(tm), tspec] * 3 + [tspec],
        out_shape=[jax.ShapeDtypeStruct((N_BLOCKS, L // CHUNK, CHUNK_W), BF16), tok, tr, hm, tr, hm, tr, hm, tok],
        scratch_shapes=[_chunk_scratch(tm)],
        compiler_params=_cparams(("arbitrary",)),
    )(x, g_pre, w_in_g)


def _ssm_block_params(a_re, a_im, log_dt, b_re, b_im, c_re, c_im, d):
    eye_g = jnp.eye(GROUPS_PER_BLOCK, dtype=F32)[None, None, :, None, :, None]

    def lanes(t):
        return t.reshape(2, N_BLOCKS, 1, BLOCK_ST)

    def expand(t):
        return (t[:, :, :, :, None, :] * eye_g).reshape(2, N_BLOCKS, BLOCK_CH, BLOCK_ST)

    b_shape = (2, N_BLOCKS, GROUPS_PER_BLOCK, SSM_STATE, SSM_GROUP)
    c_shape = (2, N_BLOCKS, GROUPS_PER_BLOCK, SSM_GROUP, SSM_STATE)
    return (lanes(a_re), lanes(a_im), lanes(jnp.broadcast_to(log_dt[..., None], a_re.shape)),
            expand(b_re.reshape(b_shape).transpose(0, 1, 2, 4, 3)), expand(b_im.reshape(b_shape).transpose(0, 1, 2, 4, 3)),
            expand(c_re.reshape(c_shape)), expand(c_im.reshape(c_shape)), d.reshape(N_BLOCKS, 1, BLOCK_CH))


def _ssm_discretise(ar, ai, ldt):
    dt = jnp.exp(ldt)
    mag = jnp.exp(dt * ar)
    abr = mag * jnp.cos(dt * ai)
    abi = mag * jnp.sin(dt * ai)
    num_re = abr - 1.0
    num_im = abi
    denom = ar * ar + ai * ai
    coef_re = (num_re * ar + num_im * ai) / denom
    coef_im = (num_im * ar - num_re * ai) / denom
    return abr, abi, coef_re, coef_im


_POW_ROWS = 24


def _ssm_fill_powers(ar_ref, ai_ref, ldt_ref, br_ref, bi_ref, pw_ref, bbar_ref):
    for d in range(2):
        abr, abi, cfr, cfi = _ssm_discretise(ar_ref[d, 0], ai_ref[d, 0], ldt_ref[d, 0])
        bbar_ref[d, 0] = cfr * br_ref[d, 0] - cfi * bi_ref[d, 0]
        bbar_ref[d, 1] = cfr * bi_ref[d, 0] + cfi * br_ref[d, 0]
        pr, pi = jnp.ones_like(abr), jnp.zeros_like(abi)
        for t in range(CHUNK + 1):
            pw_ref[d, 0, t:t + 1, :] = pr
            pw_ref[d, 1, t:t + 1, :] = pi
            pr, pi = pr * abr - pi * abi, pr * abi + pi * abr


def _dot_hi(a, b, dims=((1,), (0,))):
    return lax.dot_general(a, b, (dims, ((), ())), precision=lax.Precision.HIGH, preferred_element_type=F32)


def _ssm_stack_inputs(d, pw_ref, bbar_ref, xs_ref):
    for t in range(CHUNK):
        pr, pi = pw_ref[d, 0, t:t + 1, :], pw_ref[d, 1, t:t + 1, :]
        xs_ref[0, t * BLOCK_CH:(t + 1) * BLOCK_CH, :] = bbar_ref[d, 0] * pr - bbar_ref[d, 1] * pi
        xs_ref[1, t * BLOCK_CH:(t + 1) * BLOCK_CH, :] = bbar_ref[d, 0] * pi + bbar_ref[d, 1] * pr


def _eye(n):
    return (lax.broadcasted_iota(jnp.int32, (n, n), 0) == lax.broadcasted_iota(jnp.int32, (n, n), 1)).astype(F32)


def _ssm_param_specs():
    vec = pl.BlockSpec((2, 1, 1, BLOCK_ST), lambda b, j: (0, b, 0, 0))
    mat = pl.BlockSpec((2, 1, BLOCK_CH, BLOCK_ST), lambda b, j: (0, b, 0, 0))
    return [vec, vec, vec, mat, mat, mat, mat, pl.BlockSpec((1, 1, BLOCK_CH), lambda b, j: (b, 0, 0))]


def _ssm_chunk_matrices(blk):
    def body(ar_ref, ai_ref, ldt_ref, br_ref, bi_ref, cr_ref, ci_ref, d_ref,
             m_ref, ws_ref, wot_ref, a16_ref, pw_ref, bbar_ref, lag_ref, xs_ref):
        j = pl.program_id(1)

        @pl.when(j == 0)
        def _():
            _ssm_fill_powers(ar_ref, ai_ref, ldt_ref, br_ref, bi_ref, pw_ref, bbar_ref)
            zero_lag = d_ref[0] * _eye(BLOCK_CH)
            for d in range(2):
                _ssm_stack_inputs(d, pw_ref, bbar_ref, xs_ref)
                taps = (_dot_hi(xs_ref[0], cr_ref[d, 0], ((1,), (1,)))
                        - _dot_hi(xs_ref[1], ci_ref[d, 0], ((1,), (1,))))
                zero_lag = zero_lag + taps[0:BLOCK_CH]
                for t in range(1, CHUNK):
                    lag_ref[CHUNK - 1 + t if d == 0 else CHUNK - 1 - t] = taps[t * BLOCK_CH:(t + 1) * BLOCK_CH]
            lag_ref[CHUNK - 1] = zero_lag
            a16_ref[0] = jnp.concatenate([pw_ref[d, ri, CHUNK:CHUNK + 1, :] for d in range(2) for ri in range(2)], axis=1)

        m_ref[0] = jnp.concatenate([lag_ref[jp - j + CHUNK - 1] for jp in range(CHUNK)], axis=1).astype(BF16)

        def power(d, t):
            return pw_ref[d, 0, pl.ds(t, 1), :], pw_ref[d, 1, pl.ds(t, 1), :]

        parts = []
        for d, t in ((0, CHUNK - 1 - j), (1, j)):
            pr, pi = power(d, t)
            parts += [bbar_ref[d, 0] * pr - bbar_ref[d, 1] * pi, bbar_ref[d, 0] * pi + bbar_ref[d, 1] * pr]
        ws_ref[0] = jnp.concatenate(parts, axis=1).astype(BF16)
        parts = []
        for d, t in ((0, j + 1), (1, CHUNK - j)):
            pr, pi = power(d, t)
            parts += [cr_ref[d, 0] * pr - ci_ref[d, 0] * pi, -cr_ref[d, 0] * pi - ci_ref[d, 0] * pr]
        wot_ref[0] = jnp.concatenate(parts, axis=1).astype(BF16)

    row = pl.BlockSpec((1, BLOCK_CH, CHUNK_W), lambda b, j: (b, j, 0))
    mat = jax.ShapeDtypeStruct((N_BLOCKS, CHUNK_W, CHUNK_W), BF16)
    return pl.pallas_call(
        body, name="ssm_chunk_matrices", grid=(N_BLOCKS, CHUNK),
        in_specs=_ssm_param_specs(),
        out_specs=[row, row, row, pl.BlockSpec((1, 1, STATE_W), lambda b, j: (b, 0, 0))],
        out_shape=[mat, mat, mat, jax.ShapeDtypeStruct((N_BLOCKS, 1, STATE_W), F32)],
        scratch_shapes=[pltpu.VMEM((2, 2, _POW_ROWS, BLOCK_ST), F32), pltpu.VMEM((2, 2, BLOCK_CH, BLOCK_ST), F32),
                        pltpu.VMEM((2 * CHUNK, BLOCK_CH, BLOCK_CH), F32), pltpu.VMEM((2, CHUNK_W, BLOCK_ST), F32)],
        compiler_params=_cparams(("arbitrary", "arbitrary")),
    )(*blk)


def _ssm_chunk_matrices_bwd(blk, d_m, d_ws, d_wot, d_a16):
    def body(ar_ref, ai_ref, ldt_ref, br_ref, bi_ref, cr_ref, ci_ref, d_ref, dm_ref, dws_ref, dwot_ref, da16_ref,
             dar_ref, dai_ref, dldt_ref, dbr_ref, dbi_ref, dcr_ref, dci_ref, dd_ref,
             pw_ref, bbar_ref, dlag_ref, dbbar_ref, dc_ref, dpw_ref, xs_ref, dts_ref):
        j = pl.program_id(1)
        w = BLOCK_ST

        @pl.when(j == 0)
        def _():
            _ssm_fill_powers(ar_ref, ai_ref, ldt_ref, br_ref, bi_ref, pw_ref, bbar_ref)
            for r in (dlag_ref, dbbar_ref, dc_ref, dpw_ref):
                r[...] = jnp.zeros_like(r)

        def x_chain(d, t, dxr, dxi):
            pr, pi = pw_ref[d, 0, pl.ds(t, 1), :], pw_ref[d, 1, pl.ds(t, 1), :]
            bbr, bbi = bbar_ref[d, 0], bbar_ref[d, 1]
            dbbar_ref[d, 0] += dxr * pr + dxi * pi
            dbbar_ref[d, 1] += dxi * pr - dxr * pi
            dpw_ref[d, 0, pl.ds(t, 1), :] += jnp.sum(dxr * bbr + dxi * bbi, axis=0, keepdims=True)
            dpw_ref[d, 1, pl.ds(t, 1), :] += jnp.sum(dxi * bbr - dxr * bbi, axis=0, keepdims=True)

        def z_chain(d, t, dzr, dzi):
            pr, pi = pw_ref[d, 0, pl.ds(t, 1), :], pw_ref[d, 1, pl.ds(t, 1), :]
            c_r, c_i = cr_ref[d, 0], ci_ref[d, 0]
            dc_ref[d, 0] += dzr * pr - dzi * pi
            dc_ref[d, 1] += -dzr * pi - dzi * pr
            dpw_ref[d, 0, pl.ds(t, 1), :] += jnp.sum(dzr * c_r - dzi * c_i, axis=0, keepdims=True)
            dpw_ref[d, 1, pl.ds(t, 1), :] += jnp.sum(-dzr * c_i - dzi * c_r, axis=0, keepdims=True)

        for jp in range(CHUNK):
            dlag_ref[jp - j + CHUNK - 1] += dm_ref[0, :, jp * BLOCK_CH:(jp + 1) * BLOCK_CH]
        x_chain(0, CHUNK - 1 - j, dws_ref[0, :, 0:w], dws_ref[0, :, w:2 * w])
        x_chain(1, j, dws_ref[0, :, 2 * w:3 * w], dws_ref[0, :, 3 * w:4 * w])
        z_chain(0, j + 1, dwot_ref[0, :, 0:w], dwot_ref[0, :, w:2 * w])
        z_chain(1, CHUNK - j, dwot_ref[0, :, 2 * w:3 * w], dwot_ref[0, :, 3 * w:4 * w])

        @pl.when(j == CHUNK - 1)
        def _():
            for d in range(2):
                _ssm_stack_inputs(d, pw_ref, bbar_ref, xs_ref)
                for t in range(CHUNK):
                    dts_ref[t * BLOCK_CH:(t + 1) * BLOCK_CH, :] = dlag_ref[CHUNK - 1 + t if d == 0 else CHUNK - 1 - t]
                d_taps = dts_ref[...]
                dc_ref[d, 0] += _dot_hi(d_taps, xs_ref[0], ((0,), (0,)))
                dc_ref[d, 1] -= _dot_hi(d_taps, xs_ref[1], ((0,), (0,)))
                xs_ref[0] = _dot_hi(d_taps, cr_ref[d, 0])
                xs_ref[1] = -_dot_hi(d_taps, ci_ref[d, 0])
                for t in range(CHUNK):
                    rows = slice(t * BLOCK_CH, (t + 1) * BLOCK_CH)
                    x_chain(d, t, xs_ref[0, rows, :], xs_ref[1, rows, :])
            dd_ref[0] = jnp.sum(dlag_ref[CHUNK - 1] * _eye(BLOCK_CH), axis=0, keepdims=True)
            for d in range(2):
                (abr, abi, cfr, cfi), disc_vjp = jax.vjp(_ssm_discretise, ar_ref[d, 0], ai_ref[d, 0], ldt_ref[d, 0])
                dpr = dpw_ref[d, 0, CHUNK:CHUNK + 1, :] + da16_ref[0, :, 2 * d * w:(2 * d + 1) * w]
                dpi = dpw_ref[d, 1, CHUNK:CHUNK + 1, :] + da16_ref[0, :, (2 * d + 1) * w:(2 * d + 2) * w]
                dabr, dabi = jnp.zeros_like(abr), jnp.zeros_like(abi)
                for t in range(CHUNK, 0, -1):
                    qr, qi = pw_ref[d, 0, t - 1:t, :], pw_ref[d, 1, t - 1:t, :]
                    dabr = dabr + dpr * qr + dpi * qi
                    dabi = dabi + dpi * qr - dpr * qi
                    dpr, dpi = (dpr * abr + dpi * abi + dpw_ref[d, 0, t - 1:t, :],
                                dpi * abr - dpr * abi + dpw_ref[d, 1, t - 1:t, :])
                dbbr, dbbi = dbbar_ref[d, 0], dbbar_ref[d, 1]
                b_r, b_i = br_ref[d, 0], bi_ref[d, 0]
                dbr_ref[d, 0] = cfr * dbbr + cfi * dbbi
                dbi_ref[d, 0] = cfr * dbbi - cfi * dbbr
                dcfr = jnp.sum(b_r * dbbr + b_i * dbbi, axis=0, keepdims=True)
                dcfi = jnp.sum(b_r * dbbi - b_i * dbbr, axis=0, keepdims=True)
                dar_ref[d, 0], dai_ref[d, 0], dldt_ref[d, 0] = disc_vjp((dabr, dabi, dcfr, dcfi))
                dcr_ref[d, 0] = dc_ref[d, 0]
                dci_ref[d, 0] = dc_ref[d, 1]

    row = pl.BlockSpec((1, BLOCK_CH, CHUNK_W), lambda b, j: (b, j, 0))
    specs = _ssm_param_specs()
    acc = lambda *s: pltpu.VMEM(s, F32)
    return pl.pallas_call(
        body, name="ssm_chunk_matrices_bwd", grid=(N_BLOCKS, CHUNK),
        in_specs=specs + [row, row, row, pl.BlockSpec((1, 1, STATE_W), lambda b, j: (b, 0, 0))],
        out_specs=specs,
        out_shape=[jax.ShapeDtypeStruct(t.shape, F32) for t in blk],
        scratch_shapes=[acc(2, 2, _POW_ROWS, BLOCK_ST), acc(2, 2, BLOCK_CH, BLOCK_ST), acc(2 * CHUNK, BLOCK_CH, BLOCK_CH),
                        acc(2, 2, BLOCK_CH, BLOCK_ST), acc(2, 2, BLOCK_CH, BLOCK_ST), acc(2, 2, _POW_ROWS, BLOCK_ST),
                        acc(2, CHUNK_W, BLOCK_ST), acc(CHUNK_W, BLOCK_CH)],
        compiler_params=_cparams(("arbitrary", "arbitrary")),
    )(*blk, d_m, d_ws, d_wot, d_a16)


def _block_matmul(terms, name, out_dtype=F32, tn=1024):
    nc = terms[0][0].shape[1]
    n_out = terms[0][1].shape[1] if terms[0][2] else terms[0][1].shape[2]
    flags = [t[2] for t in terms]

    def body(*refs):
        out_ref = refs[-1]
        acc = None
        for t, transposed in enumerate(flags):
            a = refs[2 * t][0].astype(BF16)
            w = refs[2 * t + 1][0]
            part = _dot_nt(a, w) if transposed else _dot(a, w)
            acc = part if acc is None else acc + part
        out_ref[0] = acc.astype(out_dtype)

    in_specs, args = [], []
    for a, w, transposed in terms:
        k = a.shape[2]
        in_specs.append(pl.BlockSpec((1, nc, k), lambda b, n: (b, 0, 0)))
        if transposed:
            in_specs.append(pl.BlockSpec((1, tn, k), lambda b, n: (b, n, 0)))
        else:
            in_specs.append(pl.BlockSpec((1, k, tn), lambda b, n: (b, 0, n)))
        args += [a, w]
    return pl.pallas_call(
        body, name=name, grid=(N_BLOCKS, n_out // tn), in_specs=in_specs,
        out_specs=pl.BlockSpec((1, nc, tn), lambda b, n: (b, 0, n)),
        out_shape=jax.ShapeDtypeStruct((N_BLOCKS, nc, n_out), out_dtype),
        compiler_params=_cparams(("arbitrary", "arbitrary")),
    )(*args)


def _block_matmul_tn(a, b, name, tile=1024):
    nc, m = a.shape[1], a.shape[2]
    n = b.shape[2]

    def body(a_ref, b_ref, out_ref):
        out_ref[0] = _dot_tn(a_ref[0].astype(BF16), b_ref[0].astype(BF16))

    return pl.pallas_call(
        body, name=name, grid=(N_BLOCKS, m // tile, n // tile),
        in_specs=[pl.BlockSpec((1, nc, tile), lambda blk, i, j: (blk, 0, i)),
                  pl.BlockSpec((1, nc, tile), lambda blk, i, j: (blk, 0, j))],
        out_specs=pl.BlockSpec((1, tile, tile), lambda blk, i, j: (blk, i, j)),
        out_shape=jax.ShapeDtypeStruct((N_BLOCKS, m, n), F32),
        compiler_params=_cparams(("arbitrary", "arbitrary", "arbitrary")),
    )(a, b)


def _cmul(ar, ai, xr, xi):
    return ar * xr - ai * xi, ar * xi + ai * xr


def _cmul_conj(ar, ai, xr, xi):
    return ar * xr + ai * xi, ar * xi - ai * xr


def _ssm_state_scan(s_in, a16):
    nc = s_in.shape[1]
    w = BLOCK_ST

    def body(sin_ref, a_ref, out_ref):
        a = a_ref[0]
        afr, afi, abr, abi = a[:, 0:w], a[:, w:2 * w], a[:, 2 * w:3 * w], a[:, 3 * w:4 * w]

        def step(c, carry):
            fr, fi, br, bi = carry
            cb = nc - 1 - c
            out_ref[0, pl.ds(c, 1), 0:w] = fr
            out_ref[0, pl.ds(c, 1), w:2 * w] = fi
            out_ref[0, pl.ds(cb, 1), 2 * w:3 * w] = br
            out_ref[0, pl.ds(cb, 1), 3 * w:4 * w] = bi
            nfr, nfi = _cmul(afr, afi, fr, fi)
            nbr, nbi = _cmul(abr, abi, br, bi)
            return (nfr + sin_ref[0, pl.ds(c, 1), 0:w], nfi + sin_ref[0, pl.ds(c, 1), w:2 * w],
                    nbr + sin_ref[0, pl.ds(cb, 1), 2 * w:3 * w], nbi + sin_ref[0, pl.ds(cb, 1), 3 * w:4 * w])

        z = jnp.zeros((1, w), F32)
        lax.fori_loop(0, nc, step, (z, z, z, z))

    spec = pl.BlockSpec((1, nc, STATE_W), lambda b: (b, 0, 0))
    return pl.pallas_call(
        body, name="ssm_state_scan", grid=(N_BLOCKS,),
        in_specs=[spec, pl.BlockSpec((1, 1, STATE_W), lambda b: (b, 0, 0))],
        out_specs=spec, out_shape=jax.ShapeDtypeStruct(s_in.shape, F32),
        compiler_params=_cparams(("arbitrary",)),
    )(s_in, a16)


def _ssm_state_scan_bwd(d_prev, s_prev, a16):
    nc = d_prev.shape[1]
    w = BLOCK_ST

    def body(dp_ref, sp_ref, a_ref, g_ref, da_ref):
        a = a_ref[0]
        afr, afi, abr, abi = a[:, 0:w], a[:, w:2 * w], a[:, 2 * w:3 * w], a[:, 3 * w:4 * w]

        def step(i, carry):
            gfr, gfi, gbr, gbi, dafr, dafi, dabr, dabi = carry
            cf = nc - 1 - i
            cb = i
            g_ref[0, pl.ds(cf, 1), 0:w] = gfr
            g_ref[0, pl.ds(cf, 1), w:2 * w] = gfi
            g_ref[0, pl.ds(cb, 1), 2 * w:3 * w] = gbr
            g_ref[0, pl.ds(cb, 1), 3 * w:4 * w] = gbi
            sfr, sfi = sp_ref[0, pl.ds(cf, 1), 0:w], sp_ref[0, pl.ds(cf, 1), w:2 * w]
            sbr, sbi = sp_ref[0, pl.ds(cb, 1), 2 * w:3 * w], sp_ref[0, pl.ds(cb, 1), 3 * w:4 * w]
            dafr = dafr + gfr * sfr + gfi * sfi
            dafi = dafi + gfi * sfr - gfr * sfi
            dabr = dabr + gbr * sbr + gbi * sbi
            dabi = dabi + gbi * sbr - gbr * sbi
            nfr, nfi = _cmul_conj(afr, afi, gfr, gfi)
            nbr, nbi = _cmul_conj(abr, abi, gbr, gbi)
            return (nfr + dp_ref[0, pl.ds(cf, 1), 0:w], nfi + dp_ref[0, pl.ds(cf, 1), w:2 * w],
                    nbr + dp_ref[0, pl.ds(cb, 1), 2 * w:3 * w], nbi + dp_ref[0, pl.ds(cb, 1), 3 * w:4 * w],
                    dafr, dafi, dabr, dabi)

        z = jnp.zeros((1, w), F32)
        res = lax.fori_loop(0, nc, step, (z,) * 8)
        da_ref[0] = jnp.concatenate(res[4:], axis=1)

    spec = pl.BlockSpec((1, nc, STATE_W), lambda b: (b, 0, 0))
    aspec = pl.BlockSpec((1, 1, STATE_W), lambda b: (b, 0, 0))
    return pl.pallas_call(
        body, name="ssm_state_scan_bwd", grid=(N_BLOCKS,),
        in_specs=[spec, spec, aspec], out_specs=[spec, aspec],
        out_shape=[jax.ShapeDtypeStruct(d_prev.shape, F32), jax.ShapeDtypeStruct((N_BLOCKS, 1, STATE_W), F32)],
        compiler_params=_cparams(("arbitrary",)),
    )(d_prev, s_prev, a16)


NA_PAIR = 2 * GRID_W
NA_WIN_ROWS = NA_ROWS + 2
NA_WIN = NA_WIN_ROWS * GRID_W
NA_PAIRS_PER_STEP = 8
NA_CASES = 5
NA_MASKED = -1e30


def _na_pair_window(m, rows):
    rs0 = jnp.clip(2 * m - NA_ROWS // 2, 0, rows - NA_ROWS)
    ws = jnp.minimum(rs0, rows - NA_WIN_ROWS)
    last = rows // 2 - 1
    case = jnp.where(m == 0, 0, jnp.where(m == 1, 1, jnp.where(m == last - 1, 3, jnp.where(m == last, 4, 2))))
    return ws, case


def _na_row_offsets(rows):
    last = rows // 2 - 1
    geom = []
    for m in (0, 1, 2, last - 1, last):
        ws = min(max(2 * m - NA_ROWS // 2, 0), rows - NA_ROWS, rows - NA_WIN_ROWS)
        per_case = []
        for i in range(NA_WIN_ROWS):
            pair = []
            for rr in range(2):
                r = 2 * m + rr
                rs = min(max(r - NA_ROWS // 2, 0), rows - NA_ROWS)
                pair.append(ws + i - r + NA_ROWS - 1 if rs <= ws + i < rs + NA_ROWS else None)
            per_case.append(pair)
        geom.append(per_case)
    return geom


def _na_col_select():
    qc = np.arange(NA_PAIR)[None, :] % GRID_W
    kc = np.arange(GRID_W)[:, None]
    dc = np.clip(kc - qc + NA_COLS - 1, 0, 2 * NA_COLS - 2)
    return jnp.asarray((np.arange(2 * NA_COLS - 1)[:, None, None] == dc[None]).astype(np.float32))


def _na_bias_rows(rpb):
    return jnp.einsum("hrd,dkl->hrkl", rpb, _na_col_select(), precision=HIGHEST)


def _na_col_window():
    qc = lax.broadcasted_iota(jnp.int32, (GRID_W, NA_PAIR), 1) % GRID_W
    kc = lax.broadcasted_iota(jnp.int32, (GRID_W, NA_PAIR), 0)
    cs = jnp.clip(qc - NA_COLS // 2, 0, GRID_W - NA_COLS)
    first_row = lax.broadcasted_iota(jnp.int32, (GRID_W, NA_PAIR), 1) < GRID_W
    return (kc >= cs) & (kc < cs + NA_COLS), first_row


def _na_bias_table(bias_rows, rows):
    geom = _na_row_offsets(rows)

    def body(br_ref, tab_ref):
        col_ok, first_row = _na_col_window()
        masked = jnp.full((GRID_W, NA_PAIR), NA_MASKED, F32)
        for case in range(NA_CASES):
            for i in range(NA_WIN_ROWS):
                d0, d1 = geom[case][i]
                t0 = masked if d0 is None else br_ref[0, d0]
                t1 = masked if d1 is None else br_ref[0, d1]
                tile = jnp.where(col_ok, jnp.where(first_row, t0, t1), NA_MASKED)
                tab_ref[0, case, i * GRID_W:(i + 1) * GRID_W, :] = tile

    return pl.pallas_call(
        body, name="na_bias_table", grid=(NA_HEADS,),
        in_specs=[pl.BlockSpec((1, 2 * NA_ROWS - 1, GRID_W, NA_PAIR), lambda h: (h, 0, 0, 0))],
        out_specs=pl.BlockSpec((1, NA_CASES, NA_WIN, NA_PAIR), lambda h: (h, 0, 0, 0)),
        out_shape=jax.ShapeDtypeStruct((NA_HEADS, NA_CASES, NA_WIN, NA_PAIR), F32),
        compiler_params=_cparams(("arbitrary",)),
    )(bias_rows)


def _na_bias_table_bwd(d_tab, rows):
    geom = _na_row_offsets(rows)

    def body(dt_ref, dbr_ref):
        col_ok, first_row = _na_col_window()
        acc = [None] * (2 * NA_ROWS - 1)
        for case in range(NA_CASES):
            for i in range(NA_WIN_ROWS):
                tile = jnp.where(col_ok, dt_ref[0, case, i * GRID_W:(i + 1) * GRID_W, :], 0.0)
                for rr, d in enumerate(geom[case][i]):
                    if d is not None:
                        part = jnp.where(first_row if rr == 0 else ~first_row, tile, 0.0)
                        acc[d] = part if acc[d] is None else acc[d] + part
        for d, a in enumerate(acc):
            dbr_ref[0, d] = jnp.zeros((GRID_W, NA_PAIR), F32) if a is None else a

    return pl.pallas_call(
        body, name="na_bias_table_bwd", grid=(NA_HEADS,),
        in_specs=[pl.BlockSpec((1, NA_CASES, NA_WIN, NA_PAIR), lambda h: (h, 0, 0, 0))],
        out_specs=pl.BlockSpec((1, 2 * NA_ROWS - 1, GRID_W, NA_PAIR), lambda h: (h, 0, 0, 0)),
        out_shape=jax.ShapeDtypeStruct((NA_HEADS, 2 * NA_ROWS - 1, GRID_W, NA_PAIR), F32),
        compiler_params=_cparams(("arbitrary",)),
    )(d_tab)


NA_BLK = 64


def _na_blocks():
    return [slice(i * NA_BLK, (i + 1) * NA_BLK) for i in range(NA_WIN // NA_BLK)]


def _na_softmax_stats(s_scr, bias_ref, hh, case, e_scr=None):
    m = jnp.full((NA_BLK, NA_PAIR), -jnp.inf, F32)
    for blk in _na_blocks():
        s = s_scr[blk, :] + bias_ref[hh, case, blk, :]
        s_scr[blk, :] = s
        m = jnp.maximum(m, s)
    m = jnp.max(m, axis=0, keepdims=True)
    l = jnp.zeros((NA_BLK, NA_PAIR), F32)
    for blk in _na_blocks():
        e = jnp.exp(s_scr[blk, :] - m)
        l = l + e
        if e_scr is not None:
            e_scr[blk, :] = e.astype(BF16)
    return m, jnp.sum(l, axis=0, keepdims=True)


def _head_rows(t, hh):
    row_head = lax.broadcasted_iota(jnp.int32, t.shape, 0) // NA_HEAD_DIM
    return jnp.where(row_head == hh, t, jnp.zeros_like(t))


def _heads_block_diag(t):
    lane_head = lax.broadcasted_iota(jnp.int32, t.shape, 1) // NA_HEAD_DIM
    zero = jnp.zeros_like(t)
    return jnp.concatenate([jnp.where(lane_head == 0, t, zero), jnp.where(lane_head == 1, t, zero)], axis=0)


def _na_fwd(q_t, k, v_t, bias_tab):
    L = k.shape[0]
    rows = L // GRID_W
    step_w = NA_PAIRS_PER_STEP * NA_PAIR

    def body(q_ref, k_ref, v_ref, bt_ref, o_ref, s_scr, e_scr):
        def pair(pp, carry):
            ws, case = _na_pair_window(pl.program_id(1) * NA_PAIRS_PER_STEP + pp, rows)
            win = pl.ds(pl.multiple_of(ws * GRID_W, NA_PAIR), NA_WIN)
            lanes = pl.ds(pl.multiple_of(pp * NA_PAIR, NA_PAIR), NA_PAIR)
            q_pair = q_ref[:, lanes]
            for hh in range(2):
                hrows = slice(hh * NA_HEAD_DIM, (hh + 1) * NA_HEAD_DIM)
                s_scr[...] = _dot(k_ref[win, :], _head_rows(q_pair, hh))
                m, l = _na_softmax_stats(s_scr, bt_ref, hh, case, e_scr)
                o_ref[hrows, lanes] = _dot(v_ref[hrows, win], e_scr[...]) / l
            return carry

        lax.fori_loop(0, NA_PAIRS_PER_STEP, pair, 0)

    q_spec = pl.BlockSpec((NA_PAIR, step_w), lambda h, s: (h, s))
    return pl.pallas_call(
        body, name="na_fwd", grid=(NA_HEADS // 2, L // step_w),
        in_specs=[q_spec, pl.BlockSpec((L, NA_PAIR), lambda h, s: (0, h)),
                  pl.BlockSpec((NA_PAIR, L), lambda h, s: (h, 0)),
                  pl.BlockSpec((2, NA_CASES, NA_WIN, NA_PAIR), lambda h, s: (h, 0, 0, 0))],
        out_specs=q_spec,
        out_shape=jax.ShapeDtypeStruct((D_NA, L), F32),
        scratch_shapes=[pltpu.VMEM((NA_WIN, NA_PAIR), F32), pltpu.VMEM((NA_WIN, NA_PAIR), BF16)],
        compiler_params=_cparams(("arbitrary", "arbitrary")),
    )(q_t, k, v_t, bias_tab)


def _na_bwd(q_t, q, k_t, k, v, bias_tab, out_t, d_out_t, d_out):
    L = k.shape[0]
    rows = L // GRID_W
    step_w = NA_PAIRS_PER_STEP * NA_PAIR

    def body(qt_ref, q_ref, kt_ref, k_ref, v_ref, bt_ref, ot_ref, dot_ref, do_ref, dq_ref, dk_ref, dv_ref, dbt_ref,
             s_scr, dp_scr, ds_scr, p_scr):
        @pl.when(pl.program_id(1) == 0)
        def _():
            dk_ref[...] = jnp.zeros_like(dk_ref)
            dv_ref[...] = jnp.zeros_like(dv_ref)
            dbt_ref[...] = jnp.zeros_like(dbt_ref)

        def pair(pp, carry):
            ws, case = _na_pair_window(pl.program_id(1) * NA_PAIRS_PER_STEP + pp, rows)
            win = pl.ds(pl.multiple_of(ws * GRID_W, NA_PAIR), NA_WIN)
            lanes = pl.ds(pl.multiple_of(pp * NA_PAIR, NA_PAIR), NA_PAIR)
            q_pair = qt_ref[:, lanes]
            do_pair = dot_ref[:, lanes]
            do_pair_b = do_pair.astype(BF16)
            for hh in range(2):
                hrows = slice(hh * NA_HEAD_DIM, (hh + 1) * NA_HEAD_DIM)
                hlanes = slice(hh * NA_PAIR, (hh + 1) * NA_PAIR)
                s_scr[...] = _dot(k_ref[win, :], _head_rows(q_pair, hh))
                m, l = _na_softmax_stats(s_scr, bt_ref, hh, case)
                inv_l = 1.0 / l
                dp_scr[...] = _dot(v_ref[win, :], _head_rows(do_pair_b, hh))
                delta = jnp.sum(do_pair[hrows] * ot_ref[hrows, lanes], axis=0, keepdims=True)
                for blk in _na_blocks():
                    p = jnp.exp(s_scr[blk, :] - m) * inv_l
                    ds = p * (dp_scr[blk, :] - delta)
                    dbt_ref[hh, case, blk, :] += ds
                    ds_scr[blk, hlanes] = ds.astype(BF16)
                    p_scr[blk, hlanes] = p.astype(BF16)
                dq_ref[hrows, lanes] = _dot(kt_ref[hrows, win], ds_scr[:, hlanes]) * (NA_HEAD_DIM ** -0.5)
            tokens = pl.ds(pl.multiple_of(pp * NA_PAIR, NA_PAIR), NA_PAIR)
            dk_ref[win, :] += _dot(ds_scr[...], _heads_block_diag(q_ref[tokens, :]))
            dv_ref[win, :] += _dot(p_scr[...], _heads_block_diag(do_ref[tokens, :]))
            return carry

        lax.fori_loop(0, NA_PAIRS_PER_STEP, pair, 0)

    t_tile = pl.BlockSpec((NA_PAIR, step_w), lambda h, s: (h, s))
    tile = pl.BlockSpec((step_w, NA_PAIR), lambda h, s: (s, h))
    t_full = pl.BlockSpec((NA_PAIR, L), lambda h, s: (h, 0))
    full = pl.BlockSpec((L, NA_PAIR), lambda h, s: (0, h))
    bt = pl.BlockSpec((2, NA_CASES, NA_WIN, NA_PAIR), lambda h, s: (h, 0, 0, 0))
    tok = jax.ShapeDtypeStruct((L, D_NA), F32)
    return pl.pallas_call(
        body, name="na_bwd", grid=(NA_HEADS // 2, L // step_w),
        in_specs=[t_tile, tile, t_full, full, full, bt, t_tile, t_tile, tile],
        out_specs=[t_tile, full, full, bt],
        out_shape=[jax.ShapeDtypeStruct((D_NA, L), F32), tok, tok, jax.ShapeDtypeStruct(bias_tab.shape, F32)],
        scratch_shapes=[pltpu.VMEM((NA_WIN, NA_PAIR), F32), pltpu.VMEM((NA_WIN, NA_PAIR), F32),
                        pltpu.VMEM((NA_WIN, 2 * NA_PAIR), BF16), pltpu.VMEM((NA_WIN, 2 * NA_PAIR), BF16)],
        compiler_params=_cparams(("arbitrary", "arbitrary")),
    )(q_t, q, k_t, k, v, bias_tab, out_t, d_out_t, d_out)


def _branch_fwd_values(ys, zs, yn, zn, wglu, bglu):
    g1, t = _gelu_parts(ys)
    lin = _dot(g1.astype(BF16), wglu) + bglu
    sg = _sigmoid(lin)
    ys2 = g1 * sg
    sz, szs = _silu_parts(zs)
    sn, sns = _silu_parts(zn)
    return g1, t, sg, ys2, sz, szs, sn, sns


def _branch_fwd(y_ssm_c, z_s, y_na_t, z_n, w_glu, b_glu, tm=512):
    L = z_s.shape[0]

    def body(ys_ref, zs_ref, yn_ref, zn_ref, w_ref, b_ref, cat_ref, scr):
        yn = yn_ref[...].T
        g1, t, sg, ys2, sz, szs, sn, sns = _branch_fwd_values(
            _load_chunks(ys_ref, scr), zs_ref[...], yn, zn_ref[...], w_ref[...], b_ref[...])
        cat_ref[:, 0:512] = (ys2 * sz).astype(BF16)
        cat_ref[:, 512:1024] = (yn * sn).astype(BF16)

    tile = pl.BlockSpec((tm, 512), lambda i: (i, 0))
    return pl.pallas_call(
        body, name="branch_fwd", grid=(L // tm,),
        in_specs=[_chunk_spec(tm), tile, _heads_t_spec(tm), tile, pl.BlockSpec((512, 512), lambda i: (0, 0)),
                  pl.BlockSpec((1, 512), lambda i: (0, 0))],
        out_specs=pl.BlockSpec((tm, 1024), lambda i: (i, 0)),
        out_shape=jax.ShapeDtypeStruct((L, 1024), BF16),
        scratch_shapes=[_chunk_scratch(tm)],
        compiler_params=_cparams(("arbitrary",)),
    )(y_ssm_c, z_s, y_na_t, z_n, w_glu, b_glu)


def _branch_bwd(y_ssm_c, z_s, y_na_t, z_n, w_glu, b_glu, d_cat, tm=512):
    L = z_s.shape[0]

    def body(ys_ref, zs_ref, yn_ref, zn_ref, w_ref, b_ref, dc_ref,
             dys_ref, dzs_ref, dynt_ref, dyn_ref, dzn_ref, dw_ref, db_ref, scr):
        @pl.when(pl.program_id(0) == 0)
        def _():
            dw_ref[...] = jnp.zeros_like(dw_ref)
            db_ref[...] = jnp.zeros_like(db_ref)

        ys, zs, yn, zn = _load_chunks(ys_ref, scr), zs_ref[...], yn_ref[...].T, zn_ref[...]
        w = w_ref[...]
        g1, t, sg, ys2, sz, szs, sn, sns = _branch_fwd_values(ys, zs, yn, zn, w, b_ref[...])
        dys3 = dc_ref[:, 0:512]
        dyn2 = dc_ref[:, 512:1024]
        dzs_ref[...] = (dys3 * ys2 * _silu_grad(zs, szs)).astype(BF16)
        dys2 = dys3 * sz
        dlin = dys2 * g1 * sg * (1.0 - sg)
        dlb = dlin.astype(BF16)
        db_ref[...] += jnp.sum(dlin, axis=0, keepdims=True)
        dw_ref[...] += _dot_tn(g1.astype(BF16), dlb)
        dg1 = dys2 * sg + _dot_nt(dlb, w)
        _store_chunks(dg1 * _gelu_grad(ys, t), scr, dys_ref, BF16)
        dyn = dyn2 * sn
        dynt_ref[...] = dyn.T
        dyn_ref[...] = dyn.astype(BF16)
        dzn_ref[...] = (dyn2 * yn * _silu_grad(zn, sns)).astype(BF16)

    tile = pl.BlockSpec((tm, 512), lambda i: (i, 0))
    wspec = pl.BlockSpec((512, 512), lambda i: (0, 0))
    bspec = pl.BlockSpec((1, 512), lambda i: (0, 0))
    tok = jax.ShapeDtypeStruct((L, 512), BF16)
    return pl.pallas_call(
        body, name="branch_bwd", grid=(L // tm,),
        in_specs=[_chunk_spec(tm), tile, _heads_t_spec(tm), tile, wspec, bspec, pl.BlockSpec((tm, 1024), lambda i: (i, 0))],
        out_specs=[_chunk_spec(tm), tile, _heads_t_spec(tm), tile, tile, wspec, bspec],
        out_shape=[jax.ShapeDtypeStruct((N_BLOCKS, L // CHUNK, CHUNK_W), BF16), tok, jax.ShapeDtypeStruct((D_NA, L), F32),
                   tok, tok,
                   jax.ShapeDtypeStruct((512, 512), F32), jax.ShapeDtypeStruct((1, 512), F32)],
        scratch_shapes=[_chunk_scratch(tm)],
        compiler_params=_cparams(("arbitrary",)),
    )(y_ssm_c, z_s, y_na_t, z_n, w_glu, b_glu, d_cat)


def _head(x, p, target, cat, w_out, g_post, w_ple_g, g_ple, w_pg, tm=256):
    L = x.shape[0]
    pw = w_ple_g.shape[2]

    def body(x_ref, p_ref, t_ref, cat_ref, wo_ref, gpo_ref, wp_ref, gpl_ref, wg_ref,
             loss_ref, dh1_ref, dcat_ref, dwo_ref, dgpo_ref, dwp_ref, dgpl_ref, dwg_ref):
        @pl.when(pl.program_id(0) == 0)
        def _():
            for r in (loss_ref, dwo_ref, dgpo_ref, dwp_ref, dgpl_ref, dwg_ref):
                r[...] = jnp.zeros_like(r)

        cat_b = cat_ref[...]
        wo, wg = wo_ref[...], wg_ref[...]
        g_po, g_pl = gpo_ref[...], gpl_ref[...]
        mix = _dot(cat_b, wo)
        nm, r2 = _rms(mix)
        h1 = x_ref[...] + nm * g_po
        p_b = p_ref[...].astype(BF16)
        ep = jnp.concatenate([_dot(p_b, wp_ref[j]) for j in range(N_CHIPS)], axis=1)
        ne, r3 = _rms(ep)
        e = ne * g_pl
        h1_b = h1.astype(BF16)
        gate = _sigmoid(_dot(h1_b, wg))
        h2 = h1 + gate * e
        diff = h2 - t_ref[...]
        loss_ref[...] += (0.5 / D_MODEL) * jnp.sum(diff * diff).reshape(1, 1)

        dh2 = diff * (1.0 / D_MODEL)
        de = dh2 * gate
        dgl = (dh2 * e * gate * (1.0 - gate)).astype(BF16)
        dwg_ref[...] += _dot_tn(h1_b, dgl)
        dh1 = dh2 + _dot_nt(dgl, wg)
        dgpl_ref[...] += jnp.sum(de * ne, axis=0, keepdims=True)
        dep = _rms_bwd(de * g_pl, ne, r3).astype(BF16)
        for j in range(N_CHIPS):
            dwp_ref[j] += _dot_tn(p_b, dep[:, j * pw:(j + 1) * pw])
        dgpo_ref[...] += jnp.sum(dh1 * nm, axis=0, keepdims=True)
        dmix = _rms_bwd(dh1 * g_po, nm, r2).astype(BF16)
        dwo_ref[...] += _dot_tn(cat_b, dmix)
        dcat_ref[...] = _dot_nt(dmix, wo)
        dh1_ref[...] = dh1

    tile = lambda w: pl.BlockSpec((tm, w), lambda i: (i, 0))
    const = lambda *s: pl.BlockSpec(s, lambda i: (0,) * len(s))
    sds = jax.ShapeDtypeStruct
    return pl.pallas_call(
        body, name="head", grid=(L // tm,),
        in_specs=[tile(D_MODEL), tile(D_PLE), tile(D_MODEL), tile(1024), const(1024, D_MODEL), const(1, D_MODEL),
                  const(N_CHIPS, D_PLE, pw), const(1, D_MODEL), const(D_MODEL, D_MODEL)],
        out_specs=[const(1, 1), tile(D_MODEL), tile(1024), const(1024, D_MODEL), const(1, D_MODEL),
                   const(N_CHIPS, D_PLE, pw), const(1, D_MODEL), const(D_MODEL, D_MODEL)],
        out_shape=[sds((1, 1), F32), sds((L, D_MODEL), F32), sds((L, 1024), F32), sds((1024, D_MODEL), F32),
                   sds((1, D_MODEL), F32), sds((N_CHIPS, D_PLE, pw), F32), sds((1, D_MODEL), F32),
                   sds((D_MODEL, D_MODEL), F32)],
        compiler_params=_cparams(("arbitrary",)),
    )(x, p, target, cat, w_out, g_post, w_ple_g, g_ple, w_pg)


def _dproj_specs(tm):
    tile = pl.BlockSpec((tm, 512), lambda i: (i, 0))
    return [_chunk_spec(tm), tile, _heads_t_spec(tm), tile, tile, tile]


def _dproj_tile(refs, scr):
    du_ref, dzs_ref, dqt_ref, dk_ref, dv_ref, dzn_ref = refs
    parts = [_load_chunks(du_ref, scr), dzs_ref[...], dqt_ref[...].T, dk_ref[...], dv_ref[...], dzn_ref[...]]
    return jnp.concatenate([t.astype(BF16) for t in parts], axis=1)


def _in_proj_bwd_w(x, g_pre, dparts, tm=512):
    L = x.shape[0]
    wn = D_IN_PROJ // N_CHIPS

    def body(x_ref, g_ref, *refs):
        dw_ref, scr = refs[-2], refs[-1]

        @pl.when(pl.program_id(0) == 0)
        def _():
            dw_ref[...] = jnp.zeros_like(dw_ref)

        n, _ = _rms(x_ref[...])
        hn = (n * g_ref[...]).astype(BF16)
        dproj = _dproj_tile(refs[:-2], scr)
        for j in range(N_CHIPS):
            dw_ref[j] += _dot_tn(hn, dproj[:, j * wn:(j + 1) * wn])

    return pl.pallas_call(
        body, name="in_proj_bwd_w", grid=(L // tm,),
        in_specs=[pl.BlockSpec((tm, D_MODEL), lambda i: (i, 0)), pl.BlockSpec((1, D_MODEL), lambda i: (0, 0))] + _dproj_specs(tm),
        out_specs=pl.BlockSpec((N_CHIPS, D_MODEL, wn), lambda i: (0, 0, 0)),
        out_shape=jax.ShapeDtypeStruct((N_CHIPS, D_MODEL, wn), F32),
        scratch_shapes=[_chunk_scratch(tm)],
        compiler_params=_cparams(("arbitrary",)),
    )(x, g_pre, *dparts)


def _in_proj_bwd_x(x, g_pre, w_in_g, d_h1, dparts, tm=512):
    L = x.shape[0]
    wn = w_in_g.shape[2]

    def body(x_ref, g_ref, w_ref, dh1_ref, *refs):
        dx_ref, dg_ref, scr = refs[-3], refs[-2], refs[-1]

        @pl.when(pl.program_id(0) == 0)
        def _():
            dg_ref[...] = jnp.zeros_like(dg_ref)

        n, r = _rms(x_ref[...])
        dproj = _dproj_tile(refs[:-3], scr)
        dhn = _dot_nt(dproj[:, 0:wn], w_ref[0])
        for j in range(1, N_CHIPS):
            dhn = dhn + _dot_nt(dproj[:, j * wn:(j + 1) * wn], w_ref[j])
        dg_ref[...] += jnp.sum(dhn * n, axis=0, keepdims=True)
        dx_ref[...] = dh1_ref[...] + _rms_bwd(dhn * g_ref[...], n, r)

    wide = pl.BlockSpec((tm, D_MODEL), lambda i: (i, 0))
    vec = pl.BlockSpec((1, D_MODEL), lambda i: (0, 0))
    return pl.pallas_call(
        body, name="in_proj_bwd_x", grid=(L // tm,),
        in_specs=[wide, vec, pl.BlockSpec((N_CHIPS, D_MODEL, wn), lambda i: (0, 0, 0)), wide] + _dproj_specs(tm),
        out_specs=[wide, vec],
        out_shape=[jax.ShapeDtypeStruct((L, D_MODEL), F32), jax.ShapeDtypeStruct((1, D_MODEL), F32)],
        scratch_shapes=[_chunk_scratch(tm)],
        compiler_params=_cparams(("arbitrary",)),
    )(x, g_pre, w_in_g, d_h1, *dparts)


def _mesh_position():
    x, y, c = lax.axis_index("x"), lax.axis_index("y"), lax.axis_index("c")
    chips = [(1 - x, y), (x, 1 - y), (1 - x, 1 - y)]
    return x, y, c, chips


def _chip_index(cx, cy):
    return 2 * cx + cy


def _hbm_specs(n):
    return [pl.BlockSpec(memory_space=pl.ANY)] * n


def _gather_chips(shards, name):
    n = len(shards)

    def body(*refs):
        ins, outs = refs[:n], refs[n:2 * n]
        send1, recv1, send2, recv2, send3, recv3 = refs[2 * n:]
        x, y, c, chips = _mesh_position()
        me = _chip_index(x, y)
        sibling = (x, y, 1 - c)

        def half(ref, chip, core):
            hr = ref.shape[1] // 2
            return ref.at[chip, pl.ds(core * hr, hr)]

        copies, own = [], []
        for a in range(n):
            cp = pltpu.make_async_remote_copy(
                src_ref=ins[a], dst_ref=outs[a].at[me], send_sem=send3.at[a], recv_sem=recv3.at[a],
                device_id=sibling, device_id_type=MESH)
            cp.start()
            own.append(cp)
            hr = ins[a].shape[0] // 2
            for j, chip in enumerate(chips):
                cp = pltpu.make_async_remote_copy(
                    src_ref=ins[a].at[pl.ds(c * hr, hr)], dst_ref=half(outs[a], me, c),
                    send_sem=send1.at[a, j], recv_sem=recv1.at[a, j], device_id=(*chip, c), device_id_type=MESH)
                cp.start()
                copies.append(cp)
        for a in range(n):
            for j, chip in enumerate(chips):
                landed = half(outs[a], _chip_index(*chip), c)
                pltpu.make_async_remote_copy(
                    src_ref=landed, dst_ref=landed, send_sem=send1.at[a, j], recv_sem=recv1.at[a, j],
                    device_id=(*chip, c), device_id_type=MESH).wait_recv()
                cp = pltpu.make_async_remote_copy(
                    src_ref=landed, dst_ref=landed, send_sem=send2.at[a, j], recv_sem=recv2.at[a, j],
                    device_id=sibling, device_id_type=MESH)
                cp.start()
                copies.append(cp)
        for a in range(n):
            for j, chip in enumerate(chips):
                other = half(outs[a], _chip_index(*chip), 1 - c)
                pltpu.make_async_remote_copy(
                    src_ref=other, dst_ref=other, send_sem=send2.at[a, j], recv_sem=recv2.at[a, j],
                    device_id=sibling, device_id_type=MESH).wait_recv()
        for cp in copies:
            cp.wait_send()
        for cp in own:
            cp.wait()

    sem = pltpu.SemaphoreType.DMA
    return pl.pallas_call(
        body, name=name, in_specs=_hbm_specs(n), out_specs=_hbm_specs(n),
        out_shape=[jax.ShapeDtypeStruct((N_CHIPS,) + s.shape, s.dtype) for s in shards],
        scratch_shapes=[sem((n, 3)), sem((n, 3)), sem((n, 3)), sem((n, 3)), sem((n,)), sem((n,))],
        compiler_params=pltpu.CompilerParams(has_side_effects=True),
    )(*shards)


def _pair_exchange(grads):
    n = len(grads)

    def body(*refs):
        ins, outs = refs[:n], refs[n:2 * n]
        send, recv = refs[2 * n:]
        x, y, c, _ = _mesh_position()
        copies = []
        for a in range(n):
            hr = ins[a].shape[1] // 2
            cp = pltpu.make_async_remote_copy(
                src_ref=ins[a].at[:, pl.ds((1 - c) * hr, hr)], dst_ref=outs[a],
                send_sem=send.at[a], recv_sem=recv.at[a], device_id=(x, y, 1 - c), device_id_type=MESH)
            cp.start()
            copies.append(cp)
        for cp in copies:
            cp.wait()

    sem = pltpu.SemaphoreType.DMA
    return pl.pallas_call(
        body, name="pair_exchange", in_specs=_hbm_specs(n), out_specs=_hbm_specs(n),
        out_shape=[jax.ShapeDtypeStruct((g.shape[0], g.shape[1] // 2, g.shape[2]), g.dtype) for g in grads],
        scratch_shapes=[sem((n,)), sem((n,))],
        compiler_params=pltpu.CompilerParams(has_side_effects=True),
    )(*grads)


def _pair_add(core, grad, other, tr, out_dtype):
    hr = other.shape[1]
    cdim = other.shape[2]
    nb = hr // tr

    def body(core_ref, g_ref, o_ref, out_ref):
        out_ref[...] = (g_ref[...] + o_ref[...]).astype(out_dtype)

    return pl.pallas_call(
        body, name="pair_add",
        grid_spec=pltpu.PrefetchScalarGridSpec(
            num_scalar_prefetch=1, grid=(N_CHIPS, nb),
            in_specs=[pl.BlockSpec((1, tr, cdim), lambda j, i, core_ref: (j, core_ref[0] * nb + i, 0)),
                      pl.BlockSpec((1, tr, cdim), lambda j, i, core_ref: (j, i, 0))],
            out_specs=pl.BlockSpec((1, tr, cdim), lambda j, i, core_ref: (j, i, 0))),
        out_shape=jax.ShapeDtypeStruct(other.shape, out_dtype),
        compiler_params=_cparams(("arbitrary", "arbitrary")),
    )(core, grad, other)


def _chip_scatter(parts):
    n = len(parts)

    def body(*refs):
        ins, outs = refs[:n], refs[n:2 * n]
        send, recv, load_sem, store_sem = refs[2 * n:2 * n + 4]
        staged = refs[2 * n + 4:]
        x, y, c, chips = _mesh_position()
        me = _chip_index(x, y)
        copies, loads = [], []
        for a in range(n):
            ld = pltpu.make_async_copy(ins[a].at[me], staged[a], load_sem.at[a])
            ld.start()
            loads.append(ld)
            for j, chip in enumerate(chips):
                cp = pltpu.make_async_remote_copy(
                    src_ref=ins[a].at[_chip_index(*chip)], dst_ref=outs[a].at[me],
                    send_sem=send.at[a, j], recv_sem=recv.at[a, j], device_id=(*chip, c), device_id_type=MESH)
                cp.start()
                copies.append(cp)
        for a in range(n):
            loads[a].wait()
            st = pltpu.make_async_copy(staged[a], outs[a].at[me], store_sem.at[a])
            st.start()
            copies.append(st)
        for cp in copies:
            cp.wait()

    sem = pltpu.SemaphoreType.DMA
    return pl.pallas_call(
        body, name="chip_scatter", in_specs=_hbm_specs(n), out_specs=_hbm_specs(n),
        out_shape=[jax.ShapeDtypeStruct(p.shape, p.dtype) for p in parts],
        scratch_shapes=[sem((n, 3)), sem((n, 3)), sem((n,)), sem((n,))] + [pltpu.VMEM(p.shape[1:], p.dtype) for p in parts],
        compiler_params=pltpu.CompilerParams(has_side_effects=True),
    )(*parts)


def _chip_add(core, recv, tr):
    hr, cdim = recv.shape[1], recv.shape[2]
    nb = hr // tr

    def body(core_ref, r_ref, out_ref):
        out_ref[...] = ((r_ref[0].astype(F32) + r_ref[1].astype(F32)) + r_ref[2].astype(F32)) + r_ref[3].astype(F32)

    return pl.pallas_call(
        body, name="chip_add",
        grid_spec=pltpu.PrefetchScalarGridSpec(
            num_scalar_prefetch=1, grid=(nb,),
            in_specs=[pl.BlockSpec((N_CHIPS, tr, cdim), lambda i, core_ref: (0, i, 0))],
            out_specs=pl.BlockSpec((tr, cdim), lambda i, core_ref: (core_ref[0] * nb + i, 0))),
        out_shape=jax.ShapeDtypeStruct((2 * hr, cdim), F32),
        compiler_params=_cparams(("arbitrary",)),
    )(core, recv)


def _pair_gather(fulls):
    n = len(fulls)

    def body(*refs):
        outs = refs[n:2 * n]
        send, recv = refs[2 * n:]
        x, y, c, _ = _mesh_position()
        copies = []
        for a in range(n):
            hr = outs[a].shape[0] // 2
            mine = outs[a].at[pl.ds(c * hr, hr)]
            cp = pltpu.make_async_remote_copy(
                src_ref=mine, dst_ref=mine, send_sem=send.at[a], recv_sem=recv.at[a],
                device_id=(x, y, 1 - c), device_id_type=MESH)
            cp.start()
            copies.append(cp)
        for cp in copies:
            cp.wait()

    sem = pltpu.SemaphoreType.DMA
    return pl.pallas_call(
        body, name="pair_gather", in_specs=_hbm_specs(n), out_specs=_hbm_specs(n),
        out_shape=[jax.ShapeDtypeStruct(f.shape, f.dtype) for f in fulls],
        input_output_aliases={a: a for a in range(n)},
        scratch_shapes=[sem((n,)), sem((n,))],
        compiler_params=pltpu.CompilerParams(has_side_effects=True),
    )(*fulls)


def _row_tile(rows):
    for t in (512, 256, 128, 64, 32, 16, 8):
        if rows % t == 0:
            return t
    raise ValueError(rows)


def _reduce_scatter(grads, ici_dtypes):
    core = lax.axis_index("c").astype(jnp.int32).reshape(1)
    others = _pair_exchange(grads)
    pair = [_pair_add(core, g, o, _row_tile(o.shape[1]), dt) for g, o, dt in zip(grads, others, ici_dtypes)]
    landed = _chip_scatter(pair)
    return _pair_gather([_chip_add(core, r, _row_tile(r.shape[1])) for r in landed])


def _adamw(w, g, m, v):
    rows, cols = w.shape
    tr = _row_tile(rows) if rows % 8 == 0 else rows

    def body(w_ref, g_ref, m_ref, v_ref, d_ref, nm_ref, nv_ref):
        g_ = g_ref[...]
        m_ = ADAM_B1 * m_ref[...] + (1.0 - ADAM_B1) * g_
        v_ = ADAM_B2 * v_ref[...] + (1.0 - ADAM_B2) * (g_ * g_)
        m_hat = m_ / (1.0 - ADAM_B1 ** ADAM_STEP)
        v_hat = v_ / (1.0 - ADAM_B2 ** ADAM_STEP)
        d_ref[...] = -ADAM_LR * (m_hat / (jnp.sqrt(v_hat) + ADAM_EPS) + ADAM_WD * w_ref[...])
        nm_ref[...] = m_
        nv_ref[...] = v_

    spec = pl.BlockSpec((tr, cols), lambda i: (i, 0))
    shp = jax.ShapeDtypeStruct((rows, cols), F32)
    return pl.pallas_call(
        body, name="adamw", grid=(rows // tr,), in_specs=[spec] * 4, out_specs=[spec] * 3,
        out_shape=[shp] * 3, compiler_params=_cparams(("arbitrary",)),
    )(w, g, m, v)


_SMALL = ["norm_pre", "norm_post", "ssm_a_re", "ssm_a_im", "ssm_log_dt", "ssm_b_re", "ssm_b_im",
          "ssm_c_re", "ssm_c_im", "ssm_d", "b_glu", "na_rpb", "ple_norm"]
_BIG = ["w_in", "w_glu", "w_out", "w_ple", "w_ple_gate"]
_WEIGHTS = ["norm_pre", "norm_post", "w_in", "ssm_a_re", "ssm_a_im", "ssm_log_dt", "ssm_b_re", "ssm_b_im",
            "ssm_c_re", "ssm_c_im", "ssm_d", "w_glu", "b_glu", "na_rpb", "w_out", "w_ple", "ple_norm", "w_ple_gate"]
_SMALL_ROWS = 2176


def _pack_small(tensors):
    flat = jnp.concatenate([tensors[n].reshape(-1) for n in _SMALL])
    flat = jnp.pad(flat, (0, _SMALL_ROWS * 128 - flat.shape[0]))
    return flat.reshape(_SMALL_ROWS, 128)


def _unpack_small(packed, shapes):
    flat = packed.reshape(-1)
    out, off = {}, 0
    for n in _SMALL:
        size = int(np.prod(shapes[n]))
        out[n] = flat[off:off + size].reshape(shapes[n])
        off += size
    return out


def _local_grads(x, p, target, wts, w_in_g, w_glu, w_out, w_ple_g, w_pg):
    ssm_names = ["ssm_a_re", "ssm_a_im", "ssm_log_dt", "ssm_b_re", "ssm_b_im", "ssm_c_re", "ssm_c_im", "ssm_d"]
    ssm_params = [wts[n][0] for n in ssm_names]
    blk, blk_vjp = jax.vjp(_ssm_block_params, *ssm_params)
    m_mat, ws_mat, wot_mat, a16 = _ssm_chunk_matrices(blk)
    seq = x.shape[0]
    bias_rows, bias_rows_vjp = jax.vjp(_na_bias_rows, wts["na_rpb"][0])
    bias_tab = _na_bias_table(bias_rows, seq // GRID_W)

    u_c, z_s, q_t, q, k_t, k, v_t, v, z_n = _in_proj(x, wts["norm_pre"], w_in_g)
    s_in = _block_matmul([(u_c, ws_mat, False)], "ssm_chunk_states")
    s_prev = _ssm_state_scan(s_in, a16)
    y_ssm_c = _block_matmul([(u_c, m_mat, False), (s_prev, wot_mat, True)], "ssm_chunk_out")
    y_na_t = _na_fwd(q_t, k, v_t, bias_tab)
    cat = _branch_fwd(y_ssm_c, z_s, y_na_t, z_n, w_glu, wts["b_glu"])

    (loss, d_h1, d_cat, d_w_out, d_g_post, d_w_ple, d_g_ple, d_w_pg) = _head(
        x, p, target, cat, w_out, wts["norm_post"], w_ple_g, wts["ple_norm"], w_pg)
    dy_c, d_z_s, d_y_na_t, d_y_na, d_z_n, d_w_glu, d_b_glu = _branch_bwd(
        y_ssm_c, z_s, y_na_t, z_n, w_glu, wts["b_glu"], d_cat)
    d_q_t, d_k, d_v, d_bias_tab = _na_bwd(q_t, q, k_t, k, v, bias_tab, y_na_t, d_y_na_t, d_y_na)

    d_prev = _block_matmul([(dy_c, wot_mat, False)], "ssm_bwd_states")
    g_st, d_a16 = _ssm_state_scan_bwd(d_prev, s_prev, a16)
    d_u_c = _block_matmul([(dy_c, m_mat, True), (g_st, ws_mat, True)], "ssm_bwd_in", out_dtype=BF16)
    d_m = _block_matmul_tn(u_c, dy_c, "ssm_grad_m")
    d_ws = _block_matmul_tn(u_c, g_st, "ssm_grad_ws")
    d_wot = _block_matmul_tn(dy_c, s_prev, "ssm_grad_wot")
    d_ssm = blk_vjp(tuple(_ssm_chunk_matrices_bwd(blk, d_m, d_ws, d_wot, d_a16)))
    (d_rpb,) = bias_rows_vjp(_na_bias_table_bwd(d_bias_tab, seq // GRID_W))

    dparts = [d_u_c, d_z_s, d_q_t, d_k, d_v, d_z_n]
    d_w_in = _in_proj_bwd_w(x, wts["norm_pre"], dparts)
    grad_x, d_g_pre = _in_proj_bwd_x(x, wts["norm_pre"], w_in_g, d_h1, dparts)

    small = {"norm_pre": d_g_pre, "norm_post": d_g_post, "b_glu": d_b_glu, "na_rpb": d_rpb, "ple_norm": d_g_ple}
    for n, g in zip(ssm_names, d_ssm):
        small[n] = g
    big = {"w_in": d_w_in, "w_glu": d_w_glu.reshape(N_CHIPS, 128, 512), "w_out": d_w_out.reshape(N_CHIPS, 256, 1024),
           "w_ple": d_w_ple, "w_ple_gate": d_w_pg.reshape(N_CHIPS, 256, 1024)}
    return loss, grad_x, small, big


def kernel(x, p, norm_pre, norm_post, w_in, ssm_a_re, ssm_a_im, ssm_log_dt, ssm_b_re, ssm_b_im, ssm_c_re, ssm_c_im, ssm_d, w_glu, b_glu, na_rpb, w_out, w_ple, ple_norm, w_ple_gate, loss_target, m_norm_pre, m_norm_post, m_w_in, m_ssm_a_re, m_ssm_a_im, m_ssm_log_dt, m_ssm_b_re, m_ssm_b_im, m_ssm_c_re, m_ssm_c_im, m_ssm_d, m_w_glu, m_b_glu, m_na_rpb, m_w_out, m_w_ple, m_ple_norm, m_w_ple_gate, v_norm_pre, v_norm_post, v_w_in, v_ssm_a_re, v_ssm_a_im, v_ssm_log_dt, v_ssm_b_re, v_ssm_b_im, v_ssm_c_re, v_ssm_c_im, v_ssm_d, v_w_glu, v_b_glu, v_na_rpb, v_w_out, v_w_ple, v_ple_norm, v_w_ple_gate):
    wts = dict(norm_pre=norm_pre, norm_post=norm_post, w_in=w_in, ssm_a_re=ssm_a_re, ssm_a_im=ssm_a_im,
               ssm_log_dt=ssm_log_dt, ssm_b_re=ssm_b_re, ssm_b_im=ssm_b_im, ssm_c_re=ssm_c_re, ssm_c_im=ssm_c_im,
               ssm_d=ssm_d, w_glu=w_glu, b_glu=b_glu, na_rpb=na_rpb, w_out=w_out, w_ple=w_ple, ple_norm=ple_norm,
               w_ple_gate=w_ple_gate)
    mom_m = dict(norm_pre=m_norm_pre, norm_post=m_norm_post, w_in=m_w_in, ssm_a_re=m_ssm_a_re, ssm_a_im=m_ssm_a_im,
                 ssm_log_dt=m_ssm_log_dt, ssm_b_re=m_ssm_b_re, ssm_b_im=m_ssm_b_im, ssm_c_re=m_ssm_c_re,
                 ssm_c_im=m_ssm_c_im, ssm_d=m_ssm_d, w_glu=m_w_glu, b_glu=m_b_glu, na_rpb=m_na_rpb, w_out=m_w_out,
                 w_ple=m_w_ple, ple_norm=m_ple_norm, w_ple_gate=m_w_ple_gate)
    mom_v = dict(norm_pre=v_norm_pre, norm_post=v_norm_post, w_in=v_w_in, ssm_a_re=v_ssm_a_re, ssm_a_im=v_ssm_a_im,
                 ssm_log_dt=v_ssm_log_dt, ssm_b_re=v_ssm_b_re, ssm_b_im=v_ssm_b_im, ssm_c_re=v_ssm_c_re,
                 ssm_c_im=v_ssm_c_im, ssm_d=v_ssm_d, w_glu=v_w_glu, b_glu=v_b_glu, na_rpb=v_na_rpb, w_out=v_w_out,
                 w_ple=v_w_ple, ple_norm=v_ple_norm, w_ple_gate=v_w_ple_gate)

    shards = [wts[n][0].astype(BF16) for n in _BIG]
    w_in_g, w_glu_g, w_out_g, w_ple_g, w_pg_g = _gather_chips(shards, "gather_weights")
    loss_part, grad_x, small, big = _local_grads(
        x[0], p[0, 0], loss_target[0], wts, w_in_g, w_glu_g.reshape(512, 512), w_out_g.reshape(1024, 1024),
        w_ple_g, w_pg_g.reshape(1024, 1024))
    loss = lax.psum(loss_part[0, 0], ("x", "y", "c"))

    small_packed = _pack_small(small).reshape(N_CHIPS, _SMALL_ROWS // N_CHIPS, 128)
    reduced = _reduce_scatter([big[n] for n in _BIG] + [small_packed], [BF16] * len(_BIG) + [F32])
    grads = dict(zip(_BIG, reduced[:-1]))
    (small_all,) = _gather_chips([reduced[-1]], "gather_small_grads")
    small_all = small_all.reshape(_SMALL_ROWS, 128)

    delta, new_m, new_v = {}, {}, {}
    for n in _BIG:
        shp = wts[n].shape
        d_, m_, v_ = _adamw(wts[n][0], grads[n], mom_m[n][0], mom_v[n][0])
        grads[n] = grads[n].reshape(shp)
        delta[n], new_m[n], new_v[n] = d_.reshape(shp), m_.reshape(shp), v_.reshape(shp)
    shapes = {n: wts[n].shape for n in _SMALL}
    d_s, m_s, v_s = _adamw(_pack_small(wts), small_all, _pack_small(mom_m), _pack_small(mom_v))
    for dst, packed in ((grads, small_all), (delta, d_s), (new_m, m_s), (new_v, v_s)):
        dst.update(_unpack_small(packed, shapes))

    return (loss, grad_x[None], *[grads[n] for n in _WEIGHTS], *[delta[n] for n in _WEIGHTS],
            *[new_m[n] for n in _WEIGHTS], *[new_v[n] for n in _WEIGHTS])
```

```python
import functools
import math

import jax
import jax.numpy as jnp
import numpy as np
from jax import lax
from jax.experimental import pallas as pl
from jax.experimental.pallas import tpu as pltpu

F32 = jnp.float32
BF16 = jnp.bfloat16

D_MODEL = 1024
D_PLE = 256
GRID_W = 64
D_SSM = 512
SSM_GROUP = 16
N_GROUPS = 32
SSM_STATE = 64
D_NA = 512
NA_HEADS = 8
NA_HEAD_DIM = 64
NA_ROWS = 8
NA_COLS = 16
D_IN_PROJ = 3072
EPS = 1e-6

CHUNK = 16
GROUPS_PER_BLOCK = 8
N_BLOCKS = N_GROUPS // GROUPS_PER_BLOCK
BLOCK_CH = GROUPS_PER_BLOCK * SSM_GROUP
BLOCK_ST = GROUPS_PER_BLOCK * SSM_STATE
CHUNK_W = CHUNK * BLOCK_CH
STATE_W = 4 * BLOCK_ST

N_CHIPS = 4
MESH = pl.DeviceIdType.MESH

ADAM_LR = 0.001
ADAM_B1 = 0.9
ADAM_B2 = 0.999
ADAM_EPS = 1e-08
ADAM_WD = 0.01
ADAM_STEP = 10

VMEM_LIMIT = 52 * 1024 * 1024
HIGHEST = lax.Precision.HIGHEST


def _cparams(sem=None, **kw):
    if sem is not None:
        kw["dimension_semantics"] = sem
    return pltpu.CompilerParams(vmem_limit_bytes=VMEM_LIMIT, **kw)


def _dot(a, b, dims=((1,), (0,))):
    return lax.dot_general(a, b, (dims, ((), ())), preferred_element_type=F32)


def _dot_nt(a, b):
    return _dot(a, b, ((1,), (1,)))


def _dot_tn(a, b):
    return _dot(a, b, ((0,), (0,)))


def _sigmoid(x):
    return 1.0 / (1.0 + jnp.exp(-x))


_GELU_C = math.sqrt(2.0 / math.pi)


def _gelu_parts(x):
    inner = _GELU_C * (x + 0.044715 * (x * x * x))
    t = jnp.tanh(inner)
    return 0.5 * x * (1.0 + t), t


def _gelu_grad(x, t):
    return 0.5 * (1.0 + t) + 0.5 * x * (1.0 - t * t) * (_GELU_C * (1.0 + 3.0 * 0.044715 * x * x))


def _silu_parts(z):
    s = _sigmoid(z)
    return z * s, s


def _silu_grad(z, s):
    return s * (1.0 + z * (1.0 - s))


def _rms(x):
    r = lax.rsqrt(jnp.mean(x * x, axis=-1, keepdims=True) + EPS)
    return x * r, r


def _rms_bwd(dn, n, r):
    return r * (dn - n * jnp.mean(dn * n, axis=-1, keepdims=True))


def _chunk_scratch(tm):
    return pltpu.VMEM((N_BLOCKS, tm, BLOCK_CH), F32)


def _store_chunks(val, scr, c_ref, dtype):
    nc = scr.shape[1] // CHUNK
    for b in range(N_BLOCKS):
        scr[b] = val[:, b * BLOCK_CH:(b + 1) * BLOCK_CH]
        for j in range(CHUNK):
            c_ref[b, :, j * BLOCK_CH:(j + 1) * BLOCK_CH] = scr[b, pl.ds(j, nc, stride=CHUNK), :].astype(dtype)


def _load_chunks(c_ref, scr):
    nc = scr.shape[1] // CHUNK
    for b in range(N_BLOCKS):
        for j in range(CHUNK):
            scr[b, pl.ds(j, nc, stride=CHUNK), :] = c_ref[b, :, j * BLOCK_CH:(j + 1) * BLOCK_CH].astype(F32)
    return jnp.concatenate([scr[b] for b in range(N_BLOCKS)], axis=1)


def _chunk_spec(tm):
    return pl.BlockSpec((N_BLOCKS, tm // CHUNK, CHUNK_W), lambda i: (0, i, 0))


def _heads_t_spec(tm):
    return pl.BlockSpec((D_NA, tm), lambda i: (0, i))


def _in_proj(x, g_pre, w_in_g, tm=256):
    L = x.shape[0]
    wn = w_in_g.shape[2]

    def body(x_ref, g_ref, w_ref, uc_ref, zs_ref, qt_ref, q_ref, kt_ref, k_ref, vt_ref, v_ref, zn_ref, u_scr):
        n, _ = _rms(x_ref[...])
        hn = (n * g_ref[...]).astype(BF16)
        proj = jnp.concatenate([_dot(hn, w_ref[j]) for j in range(N_CHIPS)], axis=1)
        _store_chunks(proj[:, 0:512], u_scr, uc_ref, BF16)
        zs_ref[...] = proj[:, 512:1024]
        q = proj[:, 1024:1536] * (NA_HEAD_DIM ** -0.5)
        for val, t_ref, n_ref in ((q, qt_ref, q_ref), (proj[:, 1536:2048], kt_ref, k_ref), (proj[:, 2048:2560], vt_ref, v_ref)):
            t_ref[...] = val.T.astype(BF16)
            n_ref[...] = val.astype(BF16)
        zn_ref[...] = proj[:, 2560:3072]

    tok = jax.ShapeDtypeStruct((L, 512), F32)
    tr = jax.ShapeDtypeStruct((D_NA, L), BF16)
    hm = jax.ShapeDtypeStruct((L, D_NA), BF16)
    tspec = pl.BlockSpec((tm, 512), lambda i: (i, 0))
    return pl.pallas_call(
        body, name="in_proj", grid=(L // tm,),
        in_specs=[pl.BlockSpec((tm, D_MODEL), lambda i: (i, 0)),
                  pl.BlockSpec((1, D_MODEL), lambda i: (0, 0)),
                  pl.BlockSpec((N_CHIPS, D_MODEL, wn), lambda i: (0, 0, 0))],
        out_specs=[_chunk_spec(tm), tspec] + [_heads_t_spec(tm), tspec] * 3 + [tspec],
        out_shape=[jax.ShapeDtypeStruct((N_BLOCKS, L // CHUNK, CHUNK_W), BF16), tok, tr, hm, tr, hm, tr, hm, tok],
        scratch_shapes=[_chunk_scratch(tm)],
        compiler_params=_cparams(("arbitrary",)),
    )(x, g_pre, w_in_g)


def _ssm_block_params(a_re, a_im, log_dt, b_re, b_im, c_re, c_im, d):
    eye_g = jnp.eye(GROUPS_PER_BLOCK, dtype=F32)[None, None, :, None, :, None]

    def lanes(t):
        return t.reshape(2, N_BLOCKS, 1, BLOCK_ST)

    def expand(t):
        return (t[:, :, :, :, None, :] * eye_g).reshape(2, N_BLOCKS, BLOCK_CH, BLOCK_ST)

    b_shape = (2, N_BLOCKS, GROUPS_PER_BLOCK, SSM_STATE, SSM_GROUP)
    c_shape = (2, N_BLOCKS, GROUPS_PER_BLOCK, SSM_GROUP, SSM_STATE)
    return (lanes(a_re), lanes(a_im), lanes(jnp.broadcast_to(log_dt[..., None], a_re.shape)),
            expand(b_re.reshape(b_shape).transpose(0, 1, 2, 4, 3)), expand(b_im.reshape(b_shape).transpose(0, 1, 2, 4, 3)),
            expand(c_re.reshape(c_shape)), expand(c_im.reshape(c_shape)), d.reshape(N_BLOCKS, 1, BLOCK_CH))


def _ssm_discretise(ar, ai, ldt):
    dt = jnp.exp(ldt)
    mag = jnp.exp(dt * ar)
    abr = mag * jnp.cos(dt * ai)
    abi = mag * jnp.sin(dt * ai)
    num_re = abr - 1.0
    num_im = abi
    denom = ar * ar + ai * ai
    coef_re = (num_re * ar + num_im * ai) / denom
    coef_im = (num_im * ar - num_re * ai) / denom
    return abr, abi, coef_re, coef_im


_POW_ROWS = 24


def _ssm_fill_powers(ar_ref, ai_ref, ldt_ref, br_ref, bi_ref, pw_ref, bbar_ref):
    for d in range(2):
        abr, abi, cfr, cfi = _ssm_discretise(ar_ref[d, 0], ai_ref[d, 0], ldt_ref[d, 0])
        bbar_ref[d, 0] = cfr * br_ref[d, 0] - cfi * bi_ref[d, 0]
        bbar_ref[d, 1] = cfr * bi_ref[d, 0] + cfi * br_ref[d, 0]
        pr, pi = jnp.ones_like(abr), jnp.zeros_like(abi)
        for t in range(CHUNK + 1):
            pw_ref[d, 0, t:t + 1, :] = pr
            pw_ref[d, 1, t:t + 1, :] = pi
            pr, pi = pr * abr - pi * abi, pr * abi + pi * abr


def _dot_hi(a, b, dims=((1,), (0,))):
    return lax.dot_general(a, b, (dims, ((), ())), precision=lax.Precision.HIGH, preferred_element_type=F32)


def _ssm_stack_inputs(d, pw_ref, bbar_ref, xs_ref):
    for t in range(CHUNK):
        pr, pi = pw_ref[d, 0, t:t + 1, :], pw_ref[d, 1, t:t + 1, :]
        xs_ref[0, t * BLOCK_CH:(t + 1) * BLOCK_CH, :] = bbar_ref[d, 0] * pr - bbar_ref[d, 1] * pi
        xs_ref[1, t * BLOCK_CH:(t + 1) * BLOCK_CH, :] = bbar_ref[d, 0] * pi + bbar_ref[d, 1] * pr


def _eye(n):
    return (lax.broadcasted_iota(jnp.int32, (n, n), 0) == lax.broadcasted_iota(jnp.int32, (n, n), 1)).astype(F32)


def _ssm_param_specs():
    vec = pl.BlockSpec((2, 1, 1, BLOCK_ST), lambda b, j: (0, b, 0, 0))
    mat = pl.BlockSpec((2, 1, BLOCK_CH, BLOCK_ST), lambda b, j: (0, b, 0, 0))
    return [vec, vec, vec, mat, mat, mat, mat, pl.BlockSpec((1, 1, BLOCK_CH), lambda b, j: (b, 0, 0))]


def _ssm_chunk_matrices(blk, shards):
    n = len(shards)

    def body(*refs):
        ar_ref, ai_ref, ldt_ref, br_ref, bi_ref, cr_ref, ci_ref, d_ref = refs[:8]
        m_ref, ws_ref, wot_ref, a16_ref = refs[8 + n:12 + n]
        pw_ref, bbar_ref, lag_ref, xs_ref = refs[12 + 2 * n:16 + 2 * n]
        gather = _ChipGather(refs[8:8 + n], refs[12 + n:12 + 2 * n], refs[16 + 2 * n:])
        b, j = pl.program_id(0), pl.program_id(1)
        pl.when((b == 0) & (j == 0))(gather.start)
        pl.when((b == N_BLOCKS - 1) & (j == 0))(gather.forward)
        pl.when((b == N_BLOCKS - 1) & (j == CHUNK - 1))(gather.finish)

        @pl.when(j == 0)
        def _():
            _ssm_fill_powers(ar_ref, ai_ref, ldt_ref, br_ref, bi_ref, pw_ref, bbar_ref)
            zero_lag = d_ref[0] * _eye(BLOCK_CH)
            for d in range(2):
                _ssm_stack_inputs(d, pw_ref, bbar_ref, xs_ref)
                taps = (_dot_hi(xs_ref[0], cr_ref[d, 0], ((1,), (1,)))
                        - _dot_hi(xs_ref[1], ci_ref[d, 0], ((1,), (1,))))
                zero_lag = zero_lag + taps[0:BLOCK_CH]
                for t in range(1, CHUNK):
                    lag_ref[CHUNK - 1 + t if d == 0 else CHUNK - 1 - t] = taps[t * BLOCK_CH:(t + 1) * BLOCK_CH]
            lag_ref[CHUNK - 1] = zero_lag
            a16_ref[0] = jnp.concatenate([pw_ref[d, ri, CHUNK:CHUNK + 1, :] for d in range(2) for ri in range(2)], axis=1)

        m_ref[0] = jnp.concatenate([lag_ref[jp - j + CHUNK - 1] for jp in range(CHUNK)], axis=1).astype(BF16)

        def power(d, t):
            return pw_ref[d, 0, pl.ds(t, 1), :], pw_ref[d, 1, pl.ds(t, 1), :]

        parts = []
        for d, t in ((0, CHUNK - 1 - j), (1, j)):
            pr, pi = power(d, t)
            parts += [bbar_ref[d, 0] * pr - bbar_ref[d, 1] * pi, bbar_ref[d, 0] * pi + bbar_ref[d, 1] * pr]
        ws_ref[0] = jnp.concatenate(parts, axis=1).astype(BF16)
        parts = []
        for d, t in ((0, j + 1), (1, CHUNK - j)):
            pr, pi = power(d, t)
            parts += [cr_ref[d, 0] * pr - ci_ref[d, 0] * pi, -cr_ref[d, 0] * pi - ci_ref[d, 0] * pr]
        wot_ref[0] = jnp.concatenate(parts, axis=1).astype(BF16)

    row = pl.BlockSpec((1, BLOCK_CH, CHUNK_W), lambda b, j: (b, j, 0))
    mat = jax.ShapeDtypeStruct((N_BLOCKS, CHUNK_W, CHUNK_W), BF16)
    outs = pl.pallas_call(
        body, name="ssm_chunk_matrices", grid=(N_BLOCKS, CHUNK),
        in_specs=_ssm_param_specs() + _hbm_specs(n),
        out_specs=[row, row, row, pl.BlockSpec((1, 1, STATE_W), lambda b, j: (b, 0, 0))] + _hbm_specs(n),
        out_shape=[mat, mat, mat, jax.ShapeDtypeStruct((N_BLOCKS, 1, STATE_W), F32)] + _gather_out_shapes(shards),
        scratch_shapes=[pltpu.VMEM((2, 2, _POW_ROWS, BLOCK_ST), F32), pltpu.VMEM((2, 2, BLOCK_CH, BLOCK_ST), F32),
                        pltpu.VMEM((2 * CHUNK, BLOCK_CH, BLOCK_CH), F32), pltpu.VMEM((2, CHUNK_W, BLOCK_ST), F32)]
        + _gather_semaphores(n),
        compiler_params=_cparams(("arbitrary", "arbitrary"), has_side_effects=True),
    )(*blk, *shards)
    return outs[:4], outs[4:]


def _ssm_chunk_matrices_bwd(blk, d_m, d_ws, d_wot, d_a16):
    def body(ar_ref, ai_ref, ldt_ref, br_ref, bi_ref, cr_ref, ci_ref, d_ref, dm_ref, dws_ref, dwot_ref, da16_ref,
             dar_ref, dai_ref, dldt_ref, dbr_ref, dbi_ref, dcr_ref, dci_ref, dd_ref,
             pw_ref, bbar_ref, dlag_ref, dbbar_ref, dc_ref, dpw_ref, xs_ref, dts_ref):
        j = pl.program_id(1)
        w = BLOCK_ST

        @pl.when(j == 0)
        def _():
            _ssm_fill_powers(ar_ref, ai_ref, ldt_ref, br_ref, bi_ref, pw_ref, bbar_ref)
            for r in (dlag_ref, dbbar_ref, dc_ref, dpw_ref):
                r[...] = jnp.zeros_like(r)

        def x_chain(d, t, dxr, dxi):
            pr, pi = pw_ref[d, 0, pl.ds(t, 1), :], pw_ref[d, 1, pl.ds(t, 1), :]
            bbr, bbi = bbar_ref[d, 0], bbar_ref[d, 1]
            dbbar_ref[d, 0] += dxr * pr + dxi * pi
            dbbar_ref[d, 1] += dxi * pr - dxr * pi
            dpw_ref[d, 0, pl.ds(t, 1), :] += jnp.sum(dxr * bbr + dxi * bbi, axis=0, keepdims=True)
            dpw_ref[d, 1, pl.ds(t, 1), :] += jnp.sum(dxi * bbr - dxr * bbi, axis=0, keepdims=True)

        def z_chain(d, t, dzr, dzi):
            pr, pi = pw_ref[d, 0, pl.ds(t, 1), :], pw_ref[d, 1, pl.ds(t, 1), :]
            c_r, c_i = cr_ref[d, 0], ci_ref[d, 0]
            dc_ref[d, 0] += dzr * pr - dzi * pi
            dc_ref[d, 1] += -dzr * pi - dzi * pr
            dpw_ref[d, 0, pl.ds(t, 1), :] += jnp.sum(dzr * c_r - dzi * c_i, axis=0, keepdims=True)
            dpw_ref[d, 1, pl.ds(t, 1), :] += jnp.sum(-dzr * c_i - dzi * c_r, axis=0, keepdims=True)

        for jp in range(CHUNK):
            dlag_ref[jp - j + CHUNK - 1] += dm_ref[0, :, jp * BLOCK_CH:(jp + 1) * BLOCK_CH].astype(F32)
        quarter = lambda ref, i: ref[0, :, i * w:(i + 1) * w].astype(F32)
        x_chain(0, CHUNK - 1 - j, quarter(dws_ref, 0), quarter(dws_ref, 1))
        x_chain(1, j, quarter(dws_ref, 2), quarter(dws_ref, 3))
        z_chain(0, j + 1, quarter(dwot_ref, 0), quarter(dwot_ref, 1))
        z_chain(1, CHUNK - j, quarter(dwot_ref, 2), quarter(dwot_ref, 3))

        @pl.when(j == CHUNK - 1)
        def _():
            for d in range(2):
                _ssm_stack_inputs(d, pw_ref, bbar_ref, xs_ref)
                for t in range(CHUNK):
                    dts_ref[t * BLOCK_CH:(t + 1) * BLOCK_CH, :] = dlag_ref[CHUNK - 1 + t if d == 0 else CHUNK - 1 - t]
                d_taps = dts_ref[...]
                dc_ref[d, 0] += _dot_hi(d_taps, xs_ref[0], ((0,), (0,)))
                dc_ref[d, 1] -= _dot_hi(d_taps, xs_ref[1], ((0,), (0,)))
                xs_ref[0] = _dot_hi(d_taps, cr_ref[d, 0])
                xs_ref[1] = -_dot_hi(d_taps, ci_ref[d, 0])
                for t in range(CHUNK):
                    rows = slice(t * BLOCK_CH, (t + 1) * BLOCK_CH)
                    x_chain(d, t, xs_ref[0, rows, :], xs_ref[1, rows, :])
            dd_ref[0] = jnp.sum(dlag_ref[CHUNK - 1] * _eye(BLOCK_CH), axis=0, keepdims=True)
            for d in range(2):
                (abr, abi, cfr, cfi), disc_vjp = jax.vjp(_ssm_discretise, ar_ref[d, 0], ai_ref[d, 0], ldt_ref[d, 0])
                dpr = dpw_ref[d, 0, CHUNK:CHUNK + 1, :] + da16_ref[0, :, 2 * d * w:(2 * d + 1) * w]
                dpi = dpw_ref[d, 1, CHUNK:CHUNK + 1, :] + da16_ref[0, :, (2 * d + 1) * w:(2 * d + 2) * w]
                dabr, dabi = jnp.zeros_like(abr), jnp.zeros_like(abi)
                for t in range(CHUNK, 0, -1):
                    qr, qi = pw_ref[d, 0, t - 1:t, :], pw_ref[d, 1, t - 1:t, :]
                    dabr = dabr + dpr * qr + dpi * qi
                    dabi = dabi + dpi * qr - dpr * qi
                    dpr, dpi = (dpr * abr + dpi * abi + dpw_ref[d, 0, t - 1:t, :],
                                dpi * abr - dpr * abi + dpw_ref[d, 1, t - 1:t, :])
                dbbr, dbbi = dbbar_ref[d, 0], dbbar_ref[d, 1]
                b_r, b_i = br_ref[d, 0], bi_ref[d, 0]
                dbr_ref[d, 0] = cfr * dbbr + cfi * dbbi
                dbi_ref[d, 0] = cfr * dbbi - cfi * dbbr
                dcfr = jnp.sum(b_r * dbbr + b_i * dbbi, axis=0, keepdims=True)
                dcfi = jnp.sum(b_r * dbbi - b_i * dbbr, axis=0, keepdims=True)
                dar_ref[d, 0], dai_ref[d, 0], dldt_ref[d, 0] = disc_vjp((dabr, dabi, dcfr, dcfi))
                dcr_ref[d, 0] = dc_ref[d, 0]
                dci_ref[d, 0] = dc_ref[d, 1]

    row = pl.BlockSpec((1, BLOCK_CH, CHUNK_W), lambda b, j: (b, j, 0))
    specs = _ssm_param_specs()
    acc = lambda *s: pltpu.VMEM(s, F32)
    return pl.pallas_call(
        body, name="ssm_chunk_matrices_bwd", grid=(N_BLOCKS, CHUNK),
        in_specs=specs + [row, row, row, pl.BlockSpec((1, 1, STATE_W), lambda b, j: (b, 0, 0))],
        out_specs=specs,
        out_shape=[jax.ShapeDtypeStruct(t.shape, F32) for t in blk],
        scratch_shapes=[acc(2, 2, _POW_ROWS, BLOCK_ST), acc(2, 2, BLOCK_CH, BLOCK_ST), acc(2 * CHUNK, BLOCK_CH, BLOCK_CH),
                        acc(2, 2, BLOCK_CH, BLOCK_ST), acc(2, 2, BLOCK_CH, BLOCK_ST), acc(2, 2, _POW_ROWS, BLOCK_ST),
                        acc(2, CHUNK_W, BLOCK_ST), acc(CHUNK_W, BLOCK_CH)],
        compiler_params=_cparams(("arbitrary", "arbitrary")),
    )(*blk, d_m, d_ws, d_wot, d_a16)


def _block_matmul(terms, name, out_dtype=F32, tn=1024):
    nc = terms[0][0].shape[1]
    n_out = terms[0][1].shape[1] if terms[0][2] else terms[0][1].shape[2]
    flags = [t[2] for t in terms]

    def body(*refs):
        out_ref = refs[-1]
        acc = None
        for t, transposed in enumerate(flags):
            a = refs[2 * t][0].astype(BF16)
            w = refs[2 * t + 1][0]
            part = _dot_nt(a, w) if transposed else _dot(a, w)
            acc = part if acc is None else acc + part
        out_ref[0] = acc.astype(out_dtype)

    in_specs, args = [], []
    for a, w, transposed in terms:
        k = a.shape[2]
        in_specs.append(pl.BlockSpec((1, nc, k), lambda b, n: (b, 0, 0)))
        if transposed:
            in_specs.append(pl.BlockSpec((1, tn, k), lambda b, n: (b, n, 0)))
        else:
            in_specs.append(pl.BlockSpec((1, k, tn), lambda b, n: (b, 0, n)))
        args += [a, w]
    return pl.pallas_call(
        body, name=name, grid=(N_BLOCKS, n_out // tn), in_specs=in_specs,
        out_specs=pl.BlockSpec((1, nc, tn), lambda b, n: (b, 0, n)),
        out_shape=jax.ShapeDtypeStruct((N_BLOCKS, nc, n_out), out_dtype),
        compiler_params=_cparams(("arbitrary", "arbitrary")),
    )(*args)


def _block_matmul_tn(a, b, name, tile=1024):
    nc, m = a.shape[1], a.shape[2]
    n = b.shape[2]

    def body(a_ref, b_ref, out_ref):
        out_ref[0] = _dot_tn(a_ref[0].astype(BF16), b_ref[0].astype(BF16)).astype(BF16)

    return pl.pallas_call(
        body, name=name, grid=(N_BLOCKS, m // tile, n // tile),
        in_specs=[pl.BlockSpec((1, nc, tile), lambda blk, i, j: (blk, 0, i)),
                  pl.BlockSpec((1, nc, tile), lambda blk, i, j: (blk, 0, j))],
        out_specs=pl.BlockSpec((1, tile, tile), lambda blk, i, j: (blk, i, j)),
        out_shape=jax.ShapeDtypeStruct((N_BLOCKS, m, n), BF16),
        compiler_params=_cparams(("arbitrary", "arbitrary", "arbitrary")),
    )(a, b)


def _cmul(ar, ai, xr, xi):
    return ar * xr - ai * xi, ar * xi + ai * xr


def _cmul_conj(ar, ai, xr, xi):
    return ar * xr + ai * xi, ar * xi - ai * xr


def _ssm_state_scan(s_in, a16):
    nc = s_in.shape[1]
    w = BLOCK_ST

    def body(sin_ref, a_ref, out_ref):
        a = a_ref[0]
        afr, afi, abr, abi = a[:, 0:w], a[:, w:2 * w], a[:, 2 * w:3 * w], a[:, 3 * w:4 * w]

        def step(c, carry):
            fr, fi, br, bi = carry
            cb = nc - 1 - c
            out_ref[0, pl.ds(c, 1), 0:w] = fr
            out_ref[0, pl.ds(c, 1), w:2 * w] = fi
            out_ref[0, pl.ds(cb, 1), 2 * w:3 * w] = br
            out_ref[0, pl.ds(cb, 1), 3 * w:4 * w] = bi
            nfr, nfi = _cmul(afr, afi, fr, fi)
            nbr, nbi = _cmul(abr, abi, br, bi)
            return (nfr + sin_ref[0, pl.ds(c, 1), 0:w], nfi + sin_ref[0, pl.ds(c, 1), w:2 * w],
                    nbr + sin_ref[0, pl.ds(cb, 1), 2 * w:3 * w], nbi + sin_ref[0, pl.ds(cb, 1), 3 * w:4 * w])

        z = jnp.zeros((1, w), F32)
        lax.fori_loop(0, nc, step, (z, z, z, z))

    spec = pl.BlockSpec((1, nc, STATE_W), lambda b: (b, 0, 0))
    return pl.pallas_call(
        body, name="ssm_state_scan", grid=(N_BLOCKS,),
        in_specs=[spec, pl.BlockSpec((1, 1, STATE_W), lambda b: (b, 0, 0))],
        out_specs=spec, out_shape=jax.ShapeDtypeStruct(s_in.shape, F32),
        compiler_params=_cparams(("arbitrary",)),
    )(s_in, a16)


def _ssm_state_scan_bwd(d_prev, s_prev, a16):
    nc = d_prev.shape[1]
    w = BLOCK_ST

    def body(dp_ref, sp_ref, a_ref, g_ref, da_ref):
        a = a_ref[0]
        afr, afi, abr, abi = a[:, 0:w], a[:, w:2 * w], a[:, 2 * w:3 * w], a[:, 3 * w:4 * w]

        def step(i, carry):
            gfr, gfi, gbr, gbi, dafr, dafi, dabr, dabi = carry
            cf = nc - 1 - i
            cb = i
            g_ref[0, pl.ds(cf, 1), 0:w] = gfr
            g_ref[0, pl.ds(cf, 1), w:2 * w] = gfi
            g_ref[0, pl.ds(cb, 1), 2 * w:3 * w] = gbr
            g_ref[0, pl.ds(cb, 1), 3 * w:4 * w] = gbi
            sfr, sfi = sp_ref[0, pl.ds(cf, 1), 0:w], sp_ref[0, pl.ds(cf, 1), w:2 * w]
            sbr, sbi = sp_ref[0, pl.ds(cb, 1), 2 * w:3 * w], sp_ref[0, pl.ds(cb, 1), 3 * w:4 * w]
            dafr = dafr + gfr * sfr + gfi * sfi
            dafi = dafi + gfi * sfr - gfr * sfi
            dabr = dabr + gbr * sbr + gbi * sbi
            dabi = dabi + gbi * sbr - gbr * sbi
            nfr, nfi = _cmul_conj(afr, afi, gfr, gfi)
            nbr, nbi = _cmul_conj(abr, abi, gbr, gbi)
            return (nfr + dp_ref[0, pl.ds(cf, 1), 0:w], nfi + dp_ref[0, pl.ds(cf, 1), w:2 * w],
                    nbr + dp_ref[0, pl.ds(cb, 1), 2 * w:3 * w], nbi + dp_ref[0, pl.ds(cb, 1), 3 * w:4 * w],
                    dafr, dafi, dabr, dabi)

        z = jnp.zeros((1, w), F32)
        res = lax.fori_loop(0, nc, step, (z,) * 8)
        da_ref[0] = jnp.concatenate(res[4:], axis=1)

    spec = pl.BlockSpec((1, nc, STATE_W), lambda b: (b, 0, 0))
    aspec = pl.BlockSpec((1, 1, STATE_W), lambda b: (b, 0, 0))
    return pl.pallas_call(
        body, name="ssm_state_scan_bwd", grid=(N_BLOCKS,),
        in_specs=[spec, spec, aspec], out_specs=[spec, aspec],
        out_shape=[jax.ShapeDtypeStruct(d_prev.shape, F32), jax.ShapeDtypeStruct((N_BLOCKS, 1, STATE_W), F32)],
        compiler_params=_cparams(("arbitrary",)),
    )(d_prev, s_prev, a16)


NA_PAIR = 2 * GRID_W
NA_WIN_ROWS = NA_ROWS + 2
NA_WIN = NA_WIN_ROWS * GRID_W
NA_PAIRS_PER_STEP = 8
NA_CASES = 5
NA_MASKED = -1e30


def _na_pair_window(m, rows):
    rs0 = jnp.clip(2 * m - NA_ROWS // 2, 0, rows - NA_ROWS)
    ws = jnp.minimum(rs0, rows - NA_WIN_ROWS)
    last = rows // 2 - 1
    case = jnp.where(m == 0, 0, jnp.where(m == 1, 1, jnp.where(m == last - 1, 3, jnp.where(m == last, 4, 2))))
    return ws, case


def _na_row_offsets(rows):
    last = rows // 2 - 1
    geom = []
    for m in (0, 1, 2, last - 1, last):
        ws = min(max(2 * m - NA_ROWS // 2, 0), rows - NA_ROWS, rows - NA_WIN_ROWS)
        per_case = []
        for i in range(NA_WIN_ROWS):
            pair = []
            for rr in range(2):
                r = 2 * m + rr
                rs = min(max(r - NA_ROWS // 2, 0), rows - NA_ROWS)
                pair.append(ws + i - r + NA_ROWS - 1 if rs <= ws + i < rs + NA_ROWS else None)
            per_case.append(pair)
        geom.append(per_case)
    return geom


def _na_col_select():
    qc = np.arange(NA_PAIR)[None, :] % GRID_W
    kc = np.arange(GRID_W)[:, None]
    dc = np.clip(kc - qc + NA_COLS - 1, 0, 2 * NA_COLS - 2)
    return jnp.asarray((np.arange(2 * NA_COLS - 1)[:, None, None] == dc[None]).astype(np.float32))


def _na_bias_rows(rpb):
    return jnp.einsum("hrd,dkl->hrkl", rpb, _na_col_select(), precision=HIGHEST)


def _na_col_window():
    qc = lax.broadcasted_iota(jnp.int32, (GRID_W, NA_PAIR), 1) % GRID_W
    kc = lax.broadcasted_iota(jnp.int32, (GRID_W, NA_PAIR), 0)
    cs = jnp.clip(qc - NA_COLS // 2, 0, GRID_W - NA_COLS)
    first_row = lax.broadcasted_iota(jnp.int32, (GRID_W, NA_PAIR), 1) < GRID_W
    return (kc >= cs) & (kc < cs + NA_COLS), first_row


def _na_bias_table(bias_rows, rows):
    geom = _na_row_offsets(rows)

    def body(br_ref, tab_ref):
        col_ok, first_row = _na_col_window()
        masked = jnp.full((GRID_W, NA_PAIR), NA_MASKED, F32)
        for case in range(NA_CASES):
            for i in range(NA_WIN_ROWS):
                d0, d1 = geom[case][i]
                t0 = masked if d0 is None else br_ref[0, d0]
                t1 = masked if d1 is None else br_ref[0, d1]
                tile = jnp.where(col_ok, jnp.where(first_row, t0, t1), NA_MASKED)
                tab_ref[0, case, i * GRID_W:(i + 1) * GRID_W, :] = tile

    return pl.pallas_call(
        body, name="na_bias_table", grid=(NA_HEADS,),
        in_specs=[pl.BlockSpec((1, 2 * NA_ROWS - 1, GRID_W, NA_PAIR), lambda h: (h, 0, 0, 0))],
        out_specs=pl.BlockSpec((1, NA_CASES, NA_WIN, NA_PAIR), lambda h: (h, 0, 0, 0)),
        out_shape=jax.ShapeDtypeStruct((NA_HEADS, NA_CASES, NA_WIN, NA_PAIR), F32),
        compiler_params=_cparams(("arbitrary",)),
    )(bias_rows)


def _na_bias_table_bwd(d_tab, rows):
    geom = _na_row_offsets(rows)

    def body(dt_ref, dbr_ref):
        col_ok, first_row = _na_col_window()
        acc = [None] * (2 * NA_ROWS - 1)
        for case in range(NA_CASES):
            for i in range(NA_WIN_ROWS):
                tile = jnp.where(col_ok, dt_ref[0, case, i * GRID_W:(i + 1) * GRID_W, :], 0.0)
                for rr, d in enumerate(geom[case][i]):
                    if d is not None:
                        part = jnp.where(first_row if rr == 0 else ~first_row, tile, 0.0)
                        acc[d] = part if acc[d] is None else acc[d] + part
        for d, a in enumerate(acc):
            dbr_ref[0, d] = jnp.zeros((GRID_W, NA_PAIR), F32) if a is None else a

    return pl.pallas_call(
        body, name="na_bias_table_bwd", grid=(NA_HEADS,),
        in_specs=[pl.BlockSpec((1, NA_CASES, NA_WIN, NA_PAIR), lambda h: (h, 0, 0, 0))],
        out_specs=pl.BlockSpec((1, 2 * NA_ROWS - 1, GRID_W, NA_PAIR), lambda h: (h, 0, 0, 0)),
        out_shape=jax.ShapeDtypeStruct((NA_HEADS, 2 * NA_ROWS - 1, GRID_W, NA_PAIR), F32),
        compiler_params=_cparams(("arbitrary",)),
    )(d_tab)


NA_BLK = 64


def _na_blocks():
    return [slice(i * NA_BLK, (i + 1) * NA_BLK) for i in range(NA_WIN // NA_BLK)]


def _na_softmax(qk, bias_ref, hh, case):
    m = jnp.full((NA_BLK, NA_PAIR), -jnp.inf, F32)
    scores = []
    for blk in _na_blocks():
        s = qk[blk, :] + bias_ref[hh, case, blk, :]
        scores.append(s)
        m = jnp.maximum(m, s)
    m = jnp.max(m, axis=0, keepdims=True)
    l = jnp.zeros((NA_BLK, NA_PAIR), F32)
    exps = []
    for s in scores:
        e = jnp.exp(s - m)
        exps.append(e)
        l = l + e
    return exps, jnp.sum(l, axis=0, keepdims=True)


def _head_rows(t, hh):
    row_head = lax.broadcasted_iota(jnp.int32, t.shape, 0) // NA_HEAD_DIM
    return jnp.where(row_head == hh, t, jnp.zeros_like(t))


def _heads_block_diag(t):
    lane_head = lax.broadcasted_iota(jnp.int32, t.shape, 1) // NA_HEAD_DIM
    zero = jnp.zeros_like(t)
    return jnp.concatenate([jnp.where(lane_head == 0, t, zero), jnp.where(lane_head == 1, t, zero)], axis=0)


def _na_fwd(q_t, k, v_t, bias_tab):
    L = k.shape[0]
    rows = L // GRID_W
    step_w = NA_PAIRS_PER_STEP * NA_PAIR

    def body(q_ref, k_ref, v_ref, bt_ref, o_ref):
        for pp in range(NA_PAIRS_PER_STEP):
            ws, case = _na_pair_window(pl.program_id(1) * NA_PAIRS_PER_STEP + pp, rows)
            win = pl.ds(pl.multiple_of(ws * GRID_W, NA_PAIR), NA_WIN)
            lanes = slice(pp * NA_PAIR, (pp + 1) * NA_PAIR)
            k_win = k_ref[win, :]
            q_pair = q_ref[:, lanes]
            for hh in range(2):
                hrows = slice(hh * NA_HEAD_DIM, (hh + 1) * NA_HEAD_DIM)
                exps, l = _na_softmax(_dot(k_win, _head_rows(q_pair, hh)), bt_ref, hh, case)
                e = jnp.concatenate([t.astype(BF16) for t in exps], axis=0)
                o_ref[hrows, lanes] = _dot(v_ref[hrows, win], e) / l

    q_spec = pl.BlockSpec((NA_PAIR, step_w), lambda h, s: (h, s))
    return pl.pallas_call(
        body, name="na_fwd", grid=(NA_HEADS // 2, L // step_w),
        in_specs=[q_spec, pl.BlockSpec((L, NA_PAIR), lambda h, s: (0, h)),
                  pl.BlockSpec((NA_PAIR, L), lambda h, s: (h, 0)),
                  pl.BlockSpec((2, NA_CASES, NA_WIN, NA_PAIR), lambda h, s: (h, 0, 0, 0))],
        out_specs=q_spec,
        out_shape=jax.ShapeDtypeStruct((D_NA, L), F32),
        compiler_params=_cparams(("arbitrary", "arbitrary")),
    )(q_t, k, v_t, bias_tab)


def _na_bwd(q_t, q, k_t, k, v, bias_tab, out_t, d_out_t, d_out):
    L = k.shape[0]
    rows = L // GRID_W
    step_w = NA_PAIRS_PER_STEP * NA_PAIR

    def body(qt_ref, q_ref, kt_ref, k_ref, v_ref, bt_ref, ot_ref, dot_ref, do_ref, dq_ref, dk_ref, dv_ref, dbt_ref):
        @pl.when(pl.program_id(1) == 0)
        def _():
            dk_ref[...] = jnp.zeros_like(dk_ref)
            dv_ref[...] = jnp.zeros_like(dv_ref)
            dbt_ref[...] = jnp.zeros_like(dbt_ref)

        for pp in range(NA_PAIRS_PER_STEP):
            ws, case = _na_pair_window(pl.program_id(1) * NA_PAIRS_PER_STEP + pp, rows)
            win = pl.ds(pl.multiple_of(ws * GRID_W, NA_PAIR), NA_WIN)
            lanes = slice(pp * NA_PAIR, (pp + 1) * NA_PAIR)
            k_win, v_win = k_ref[win, :], v_ref[win, :]
            q_pair = qt_ref[:, lanes]
            do_pair = dot_ref[:, lanes]
            do_pair_b = do_pair.astype(BF16)
            dss, ps = [], []
            for hh in range(2):
                hrows = slice(hh * NA_HEAD_DIM, (hh + 1) * NA_HEAD_DIM)
                exps, l = _na_softmax(_dot(k_win, _head_rows(q_pair, hh)), bt_ref, hh, case)
                inv_l = 1.0 / l
                dp = _dot(v_win, _head_rows(do_pair_b, hh))
                delta = jnp.sum(do_pair[hrows] * ot_ref[hrows, lanes], axis=0, keepdims=True)
                ds_blocks, p_blocks = [], []
                for blk, e in zip(_na_blocks(), exps):
                    p = e * inv_l
                    ds = p * (dp[blk, :] - delta)
                    dbt_ref[hh, case, blk, :] += ds
                    ds_blocks.append(ds.astype(BF16))
                    p_blocks.append(p.astype(BF16))
                dsb = jnp.concatenate(ds_blocks, axis=0)
                dq_ref[hrows, lanes] = _dot(kt_ref[hrows, win], dsb) * (NA_HEAD_DIM ** -0.5)
                dss.append(dsb)
                ps.append(jnp.concatenate(p_blocks, axis=0))
            tokens = slice(pp * NA_PAIR, (pp + 1) * NA_PAIR)
            dk_ref[win, :] += _dot(jnp.concatenate(dss, axis=1), _heads_block_diag(q_ref[tokens, :]))
            dv_ref[win, :] += _dot(jnp.concatenate(ps, axis=1), _heads_block_diag(do_ref[tokens, :]))

    t_tile = pl.BlockSpec((NA_PAIR, step_w), lambda h, s: (h, s))
    tile = pl.BlockSpec((step_w, NA_PAIR), lambda h, s: (s, h))
    t_full = pl.BlockSpec((NA_PAIR, L), lambda h, s: (h, 0))
    full = pl.BlockSpec((L, NA_PAIR), lambda h, s: (0, h))
    bt = pl.BlockSpec((2, NA_CASES, NA_WIN, NA_PAIR), lambda h, s: (h, 0, 0, 0))
    tok = jax.ShapeDtypeStruct((L, D_NA), F32)
    return pl.pallas_call(
        body, name="na_bwd", grid=(NA_HEADS // 2, L // step_w),
        in_specs=[t_tile, tile, t_full, full, full, bt, t_tile, t_tile, tile],
        out_specs=[t_tile, full, full, bt],
        out_shape=[jax.ShapeDtypeStruct((D_NA, L), F32), tok, tok, jax.ShapeDtypeStruct(bias_tab.shape, F32)],
        compiler_params=_cparams(("arbitrary", "arbitrary")),
    )(q_t, q, k_t, k, v, bias_tab, out_t, d_out_t, d_out)


def _branch_fwd_values(ys, zs, yn, zn, wglu, bglu):
    g1, t = _gelu_parts(ys)
    lin = _dot(g1.astype(BF16), wglu) + bglu
    sg = _sigmoid(lin)
    ys2 = g1 * sg
    sz, szs = _silu_parts(zs)
    sn, sns = _silu_parts(zn)
    return g1, t, sg, ys2, sz, szs, sn, sns


def _branch_fwd(y_ssm_c, z_s, y_na_t, z_n, w_glu, b_glu, tm=512):
    L = z_s.shape[0]

    def body(ys_ref, zs_ref, yn_ref, zn_ref, w_ref, b_ref, cat_ref, scr):
        yn = yn_ref[...].T
        g1, t, sg, ys2, sz, szs, sn, sns = _branch_fwd_values(
            _load_chunks(ys_ref, scr), zs_ref[...], yn, zn_ref[...], w_ref[...], b_ref[...])
        cat_ref[:, 0:512] = (ys2 * sz).astype(BF16)
        cat_ref[:, 512:1024] = (yn * sn).astype(BF16)

    tile = pl.BlockSpec((tm, 512), lambda i: (i, 0))
    return pl.pallas_call(
        body, name="branch_fwd", grid=(L // tm,),
        in_specs=[_chunk_spec(tm), tile, _heads_t_spec(tm), tile, pl.BlockSpec((512, 512), lambda i: (0, 0)),
                  pl.BlockSpec((1, 512), lambda i: (0, 0))],
        out_specs=pl.BlockSpec((tm, 1024), lambda i: (i, 0)),
        out_shape=jax.ShapeDtypeStruct((L, 1024), BF16),
        scratch_shapes=[_chunk_scratch(tm)],
        compiler_params=_cparams(("arbitrary",)),
    )(y_ssm_c, z_s, y_na_t, z_n, w_glu, b_glu)


def _branch_bwd(y_ssm_c, z_s, y_na_t, z_n, w_glu, b_glu, d_cat, tm=512):
    L = z_s.shape[0]

    def body(ys_ref, zs_ref, yn_ref, zn_ref, w_ref, b_ref, dc_ref,
             dys_ref, dzs_ref, dynt_ref, dyn_ref, dzn_ref, dw_ref, db_ref, scr):
        @pl.when(pl.program_id(0) == 0)
        def _():
            dw_ref[...] = jnp.zeros_like(dw_ref)
            db_ref[...] = jnp.zeros_like(db_ref)

        ys, zs, yn, zn = _load_chunks(ys_ref, scr), zs_ref[...], yn_ref[...].T, zn_ref[...]
        w = w_ref[...]
        g1, t, sg, ys2, sz, szs, sn, sns = _branch_fwd_values(ys, zs, yn, zn, w, b_ref[...])
        dys3 = dc_ref[:, 0:512]
        dyn2 = dc_ref[:, 512:1024]
        dzs_ref[...] = (dys3 * ys2 * _silu_grad(zs, szs)).astype(BF16)
        dys2 = dys3 * sz
        dlin = dys2 * g1 * sg * (1.0 - sg)
        dlb = dlin.astype(BF16)
        db_ref[...] += jnp.sum(dlin, axis=0, keepdims=True)
        dw_ref[...] += _dot_tn(g1.astype(BF16), dlb)
        dg1 = dys2 * sg + _dot_nt(dlb, w)
        _store_chunks(dg1 * _gelu_grad(ys, t), scr, dys_ref, BF16)
        dyn = dyn2 * sn
        dynt_ref[...] = dyn.T
        dyn_ref[...] = dyn.astype(BF16)
        dzn_ref[...] = (dyn2 * yn * _silu_grad(zn, sns)).astype(BF16)

    tile = pl.BlockSpec((tm, 512), lambda i: (i, 0))
    wspec = pl.BlockSpec((512, 512), lambda i: (0, 0))
    bspec = pl.BlockSpec((1, 512), lambda i: (0, 0))
    tok = jax.ShapeDtypeStruct((L, 512), BF16)
    return pl.pallas_call(
        body, name="branch_bwd", grid=(L // tm,),
        in_specs=[_chunk_spec(tm), tile, _heads_t_spec(tm), tile, wspec, bspec, pl.BlockSpec((tm, 1024), lambda i: (i, 0))],
        out_specs=[_chunk_spec(tm), tile, _heads_t_spec(tm), tile, tile, wspec, bspec],
        out_shape=[jax.ShapeDtypeStruct((N_BLOCKS, L // CHUNK, CHUNK_W), BF16), tok, jax.ShapeDtypeStruct((D_NA, L), F32),
                   tok, tok,
                   jax.ShapeDtypeStruct((512, 512), F32), jax.ShapeDtypeStruct((1, 512), F32)],
        scratch_shapes=[_chunk_scratch(tm)],
        compiler_params=_cparams(("arbitrary",)),
    )(y_ssm_c, z_s, y_na_t, z_n, w_glu, b_glu, d_cat)


def _head(x, p, target, cat, w_out, g_post, w_ple_g, g_ple, w_pg, tm=256):
    L = x.shape[0]
    pw = w_ple_g.shape[2]

    def body(x_ref, p_ref, t_ref, cat_ref, wo_ref, gpo_ref, wp_ref, gpl_ref, wg_ref,
             loss_ref, dh1_ref, dcat_ref, dwo_ref, dgpo_ref, dwp_ref, dgpl_ref, dwg_ref):
        @pl.when(pl.program_id(0) == 0)
        def _():
            for r in (loss_ref, dwo_ref, dgpo_ref, dwp_ref, dgpl_ref, dwg_ref):
                r[...] = jnp.zeros_like(r)

        cat_b = cat_ref[...]
        wo, wg = wo_ref[...], wg_ref[...]
        g_po, g_pl = gpo_ref[...], gpl_ref[...]
        mix = _dot(cat_b, wo)
        nm, r2 = _rms(mix)
        h1 = x_ref[...] + nm * g_po
        p_b = p_ref[...].astype(BF16)
        ep = jnp.concatenate([_dot(p_b, wp_ref[j]) for j in range(N_CHIPS)], axis=1)
        ne, r3 = _rms(ep)
        e = ne * g_pl
        h1_b = h1.astype(BF16)
        gate = _sigmoid(_dot(h1_b, wg))
        h2 = h1 + gate * e
        diff = h2 - t_ref[...]
        loss_ref[...] += (0.5 / D_MODEL) * jnp.sum(diff * diff).reshape(1, 1)

        dh2 = diff * (1.0 / D_MODEL)
        de = dh2 * gate
        dgl = (dh2 * e * gate * (1.0 - gate)).astype(BF16)
        dwg_ref[...] += _dot_tn(h1_b, dgl)
        dh1 = dh2 + _dot_nt(dgl, wg)
        dgpl_ref[...] += jnp.sum(de * ne, axis=0, keepdims=True)
        dep = _rms_bwd(de * g_pl, ne, r3).astype(BF16)
        for j in range(N_CHIPS):
            dwp_ref[j] += _dot_tn(p_b, dep[:, j * pw:(j + 1) * pw])
        dgpo_ref[...] += jnp.sum(dh1 * nm, axis=0, keepdims=True)
        dmix = _rms_bwd(dh1 * g_po, nm, r2).astype(BF16)
        dwo_ref[...] += _dot_tn(cat_b, dmix)
        dcat_ref[...] = _dot_nt(dmix, wo)
        dh1_ref[...] = dh1

    tile = lambda w: pl.BlockSpec((tm, w), lambda i: (i, 0))
    const = lambda *s: pl.BlockSpec(s, lambda i: (0,) * len(s))
    sds = jax.ShapeDtypeStruct
    return pl.pallas_call(
        body, name="head", grid=(L // tm,),
        in_specs=[tile(D_MODEL), tile(D_PLE), tile(D_MODEL), tile(1024), const(1024, D_MODEL), const(1, D_MODEL),
                  const(N_CHIPS, D_PLE, pw), const(1, D_MODEL), const(D_MODEL, D_MODEL)],
        out_specs=[const(1, 1), tile(D_MODEL), tile(1024), const(1024, D_MODEL), const(1, D_MODEL),
                   const(N_CHIPS, D_PLE, pw), const(1, D_MODEL), const(D_MODEL, D_MODEL)],
        out_shape=[sds((1, 1), F32), sds((L, D_MODEL), F32), sds((L, 1024), F32), sds((1024, D_MODEL), F32),
                   sds((1, D_MODEL), F32), sds((N_CHIPS, D_PLE, pw), F32), sds((1, D_MODEL), F32),
                   sds((D_MODEL, D_MODEL), F32)],
        compiler_params=_cparams(("arbitrary",)),
    )(x, p, target, cat, w_out, g_post, w_ple_g, g_ple, w_pg)


def _dproj_specs(tm):
    tile = pl.BlockSpec((tm, 512), lambda i: (i, 0))
    return [_chunk_spec(tm), tile, _heads_t_spec(tm), tile, tile, tile]


def _dproj_tile(refs, scr):
    du_ref, dzs_ref, dqt_ref, dk_ref, dv_ref, dzn_ref = refs
    parts = [_load_chunks(du_ref, scr), dzs_ref[...], dqt_ref[...].T, dk_ref[...], dv_ref[...], dzn_ref[...]]
    return jnp.concatenate([t.astype(BF16) for t in parts], axis=1)


def _in_proj_bwd_w(x, g_pre, dparts, tm=512):
    L = x.shape[0]
    wn = D_IN_PROJ // N_CHIPS

    def body(x_ref, g_ref, *refs):
        dw_ref, scr = refs[-2], refs[-1]

        @pl.when(pl.program_id(0) == 0)
        def _():
            dw_ref[...] = jnp.zeros_like(dw_ref)

        n, _ = _rms(x_ref[...])
        hn = (n * g_ref[...]).astype(BF16)
        dproj = _dproj_tile(refs[:-2], scr)
        for j in range(N_CHIPS):
            dw_ref[j] += _dot_tn(hn, dproj[:, j * wn:(j + 1) * wn])

    return pl.pallas_call(
        body, name="in_proj_bwd_w", grid=(L // tm,),
        in_specs=[pl.BlockSpec((tm, D_MODEL), lambda i: (i, 0)), pl.BlockSpec((1, D_MODEL), lambda i: (0, 0))] + _dproj_specs(tm),
        out_specs=pl.BlockSpec((N_CHIPS, D_MODEL, wn), lambda i: (0, 0, 0)),
        out_shape=jax.ShapeDtypeStruct((N_CHIPS, D_MODEL, wn), F32),
        scratch_shapes=[_chunk_scratch(tm)],
        compiler_params=_cparams(("arbitrary",)),
    )(x, g_pre, *dparts)


def _in_proj_bwd_x(x, g_pre, w_in_g, d_h1, dparts, tm=512):
    L = x.shape[0]
    wn = w_in_g.shape[2]

    def body(x_ref, g_ref, w_ref, dh1_ref, *refs):
        dx_ref, dg_ref, scr = refs[-3], refs[-2], refs[-1]

        @pl.when(pl.program_id(0) == 0)
        def _():
            dg_ref[...] = jnp.zeros_like(dg_ref)

        n, r = _rms(x_ref[...])
        dproj = _dproj_tile(refs[:-3], scr)
        dhn = _dot_nt(dproj[:, 0:wn], w_ref[0])
        for j in range(1, N_CHIPS):
            dhn = dhn + _dot_nt(dproj[:, j * wn:(j + 1) * wn], w_ref[j])
        dg_ref[...] += jnp.sum(dhn * n, axis=0, keepdims=True)
        dx_ref[...] = dh1_ref[...] + _rms_bwd(dhn * g_ref[...], n, r)

    wide = pl.BlockSpec((tm, D_MODEL), lambda i: (i, 0))
    vec = pl.BlockSpec((1, D_MODEL), lambda i: (0, 0))
    return pl.pallas_call(
        body, name="in_proj_bwd_x", grid=(L // tm,),
        in_specs=[wide, vec, pl.BlockSpec((N_CHIPS, D_MODEL, wn), lambda i: (0, 0, 0)), wide] + _dproj_specs(tm),
        out_specs=[wide, vec],
        out_shape=[jax.ShapeDtypeStruct((L, D_MODEL), F32), jax.ShapeDtypeStruct((1, D_MODEL), F32)],
        scratch_shapes=[_chunk_scratch(tm)],
        compiler_params=_cparams(("arbitrary",)),
    )(x, g_pre, w_in_g, d_h1, *dparts)


def _mesh_position():
    x, y, c = lax.axis_index("x"), lax.axis_index("y"), lax.axis_index("c")
    chips = [(1 - x, y), (x, 1 - y), (1 - x, 1 - y)]
    return x, y, c, chips


def _chip_index(cx, cy):
    return 2 * cx + cy


def _hbm_specs(n):
    return [pl.BlockSpec(memory_space=pl.ANY)] * n


def _gather_chips(shards, name):
    n = len(shards)

    def body(*refs):
        gather = _ChipGather(refs[:n], refs[n:2 * n], refs[2 * n:])
        gather.start()
        gather.forward()
        gather.finish()

    return pl.pallas_call(
        body, name=name, in_specs=_hbm_specs(n), out_specs=_hbm_specs(n),
        out_shape=_gather_out_shapes(shards), scratch_shapes=_gather_semaphores(n),
        compiler_params=pltpu.CompilerParams(has_side_effects=True),
    )(*shards)


def _gather_out_shapes(shards):
    return [jax.ShapeDtypeStruct((N_CHIPS,) + s.shape, s.dtype) for s in shards]


def _gather_semaphores(n):
    sem = pltpu.SemaphoreType.DMA
    return [sem((n, 3)), sem((n, 3)), sem((n, 3)), sem((n, 3)), sem((n,)), sem((n,))]


class _ChipGather:
    def __init__(self, ins, outs, sems):
        self.ins, self.outs = ins, outs
        self.send1, self.recv1, self.send2, self.recv2, self.send3, self.recv3 = sems
        self.x, self.y, self.c, self.chips = _mesh_position()
        self.me = _chip_index(self.x, self.y)
        self.sibling = (self.x, self.y, 1 - self.c)

    def _half(self, a, chip, core):
        hr = self.outs[a].shape[1] // 2
        return self.outs[a].at[chip, pl.ds(core * hr, hr)]

    def _own(self, a):
        return pltpu.make_async_remote_copy(
            src_ref=self.ins[a], dst_ref=self.outs[a].at[self.me], send_sem=self.send3.at[a], recv_sem=self.recv3.at[a],
            device_id=self.sibling, device_id_type=MESH)

    def _to_chip(self, a, j):
        hr = self.ins[a].shape[0] // 2
        return pltpu.make_async_remote_copy(
            src_ref=self.ins[a].at[pl.ds(self.c * hr, hr)], dst_ref=self._half(a, self.me, self.c),
            send_sem=self.send1.at[a, j], recv_sem=self.recv1.at[a, j], device_id=(*self.chips[j], self.c), device_id_type=MESH)

    def _from_chip(self, a, j):
        landed = self._half(a, _chip_index(*self.chips[j]), self.c)
        return pltpu.make_async_remote_copy(
            src_ref=landed, dst_ref=landed, send_sem=self.send1.at[a, j], recv_sem=self.recv1.at[a, j],
            device_id=(*self.chips[j], self.c), device_id_type=MESH)

    def _to_sibling(self, a, j, core):
        part = self._half(a, _chip_index(*self.chips[j]), core)
        return pltpu.make_async_remote_copy(
            src_ref=part, dst_ref=part, send_sem=self.send2.at[a, j], recv_sem=self.recv2.at[a, j],
            device_id=self.sibling, device_id_type=MESH)

    def _each(self):
        return [(a, j) for a in range(len(self.ins)) for j in range(3)]

    def start(self):
        for a in range(len(self.ins)):
            self._own(a).start()
        for a, j in self._each():
            self._to_chip(a, j).start()

    def forward(self):
        for a, j in self._each():
            self._from_chip(a, j).wait_recv()
            self._to_sibling(a, j, self.c).start()

    def finish(self):
        for a, j in self._each():
            self._to_sibling(a, j, 1 - self.c).wait_recv()
        for a, j in self._each():
            self._to_chip(a, j).wait_send()
            self._to_sibling(a, j, self.c).wait_send()
        for a in range(len(self.ins)):
            self._own(a).wait()


def _pair_exchange(grads):
    n = len(grads)

    def body(*refs):
        ins, outs = refs[:n], refs[n:2 * n]
        send, recv = refs[2 * n:]
        x, y, c, _ = _mesh_position()
        copies = []
        for a in range(n):
            hr = ins[a].shape[1] // 2
            cp = pltpu.make_async_remote_copy(
                src_ref=ins[a].at[:, pl.ds((1 - c) * hr, hr)], dst_ref=outs[a],
                send_sem=send.at[a], recv_sem=recv.at[a], device_id=(x, y, 1 - c), device_id_type=MESH)
            cp.start()
            copies.append(cp)
        for cp in copies:
            cp.wait()

    sem = pltpu.SemaphoreType.DMA
    return pl.pallas_call(
        body, name="pair_exchange", in_specs=_hbm_specs(n), out_specs=_hbm_specs(n),
        out_shape=[jax.ShapeDtypeStruct((g.shape[0], g.shape[1] // 2, g.shape[2]), g.dtype) for g in grads],
        scratch_shapes=[sem((n,)), sem((n,))],
        compiler_params=pltpu.CompilerParams(has_side_effects=True),
    )(*grads)


def _pair_add(core, grad, other, tr, out_dtype):
    hr = other.shape[1]
    cdim = other.shape[2]
    nb = hr // tr

    def body(core_ref, g_ref, o_ref, out_ref):
        out_ref[...] = (g_ref[...] + o_ref[...]).astype(out_dtype)

    return pl.pallas_call(
        body, name="pair_add",
        grid_spec=pltpu.PrefetchScalarGridSpec(
            num_scalar_prefetch=1, grid=(N_CHIPS, nb),
            in_specs=[pl.BlockSpec((1, tr, cdim), lambda j, i, core_ref: (j, core_ref[0] * nb + i, 0)),
                      pl.BlockSpec((1, tr, cdim), lambda j, i, core_ref: (j, i, 0))],
            out_specs=pl.BlockSpec((1, tr, cdim), lambda j, i, core_ref: (j, i, 0))),
        out_shape=jax.ShapeDtypeStruct(other.shape, out_dtype),
        compiler_params=_cparams(("arbitrary", "arbitrary")),
    )(core, grad, other)


def _chip_scatter(parts):
    n = len(parts)

    def body(*refs):
        ins, outs = refs[:n], refs[n:2 * n]
        send, recv, load_sem, store_sem = refs[2 * n:2 * n + 4]
        staged = refs[2 * n + 4:]
        x, y, c, chips = _mesh_position()
        me = _chip_index(x, y)
        copies, loads = [], []
        for a in range(n):
            ld = pltpu.make_async_copy(ins[a].at[me], staged[a], load_sem.at[a])
            ld.start()
            loads.append(ld)
            for j, chip in enumerate(chips):
                cp = pltpu.make_async_remote_copy(
                    src_ref=ins[a].at[_chip_index(*chip)], dst_ref=outs[a].at[me],
                    send_sem=send.at[a, j], recv_sem=recv.at[a, j], device_id=(*chip, c), device_id_type=MESH)
                cp.start()
                copies.append(cp)
        for a in range(n):
            loads[a].wait()
            st = pltpu.make_async_copy(staged[a], outs[a].at[me], store_sem.at[a])
            st.start()
            copies.append(st)
        for cp in copies:
            cp.wait()

    sem = pltpu.SemaphoreType.DMA
    return pl.pallas_call(
        body, name="chip_scatter", in_specs=_hbm_specs(n), out_specs=_hbm_specs(n),
        out_shape=[jax.ShapeDtypeStruct(p.shape, p.dtype) for p in parts],
        scratch_shapes=[sem((n, 3)), sem((n, 3)), sem((n,)), sem((n,))] + [pltpu.VMEM(p.shape[1:], p.dtype) for p in parts],
        compiler_params=pltpu.CompilerParams(has_side_effects=True),
    )(*parts)


def _chip_add(core, recv, tr):
    hr, cdim = recv.shape[1], recv.shape[2]
    nb = hr // tr

    def body(core_ref, r_ref, out_ref):
        out_ref[...] = ((r_ref[0].astype(F32) + r_ref[1].astype(F32)) + r_ref[2].astype(F32)) + r_ref[3].astype(F32)

    return pl.pallas_call(
        body, name="chip_add",
        grid_spec=pltpu.PrefetchScalarGridSpec(
            num_scalar_prefetch=1, grid=(nb,),
            in_specs=[pl.BlockSpec((N_CHIPS, tr, cdim), lambda i, core_ref: (0, i, 0))],
            out_specs=pl.BlockSpec((tr, cdim), lambda i, core_ref: (core_ref[0] * nb + i, 0))),
        out_shape=jax.ShapeDtypeStruct((2 * hr, cdim), F32),
        compiler_params=_cparams(("arbitrary",)),
    )(core, recv)


def _pair_gather(fulls):
    n = len(fulls)

    def body(*refs):
        outs = refs[n:2 * n]
        send, recv = refs[2 * n:]
        x, y, c, _ = _mesh_position()
        copies = []
        for a in range(n):
            hr = outs[a].shape[0] // 2
            mine = outs[a].at[pl.ds(c * hr, hr)]
            cp = pltpu.make_async_remote_copy(
                src_ref=mine, dst_ref=mine, send_sem=send.at[a], recv_sem=recv.at[a],
                device_id=(x, y, 1 - c), device_id_type=MESH)
            cp.start()
            copies.append(cp)
        for cp in copies:
            cp.wait()

    sem = pltpu.SemaphoreType.DMA
    return pl.pallas_call(
        body, name="pair_gather", in_specs=_hbm_specs(n), out_specs=_hbm_specs(n),
        out_shape=[jax.ShapeDtypeStruct(f.shape, f.dtype) for f in fulls],
        input_output_aliases={a: a for a in range(n)},
        scratch_shapes=[sem((n,)), sem((n,))],
        compiler_params=pltpu.CompilerParams(has_side_effects=True),
    )(*fulls)


def _row_tile(rows):
    for t in (512, 256, 128, 64, 32, 16, 8):
        if rows % t == 0:
            return t
    raise ValueError(rows)


def _reduce_scatter(grads, ici_dtypes):
    core = lax.axis_index("c").astype(jnp.int32).reshape(1)
    others = _pair_exchange(grads)
    pair = [_pair_add(core, g, o, _row_tile(o.shape[1]), dt) for g, o, dt in zip(grads, others, ici_dtypes)]
    landed = _chip_scatter(pair)
    return _pair_gather([_chip_add(core, r, _row_tile(r.shape[1])) for r in landed])


def _adamw(w, g, m, v):
    rows, cols = w.shape
    tr = _row_tile(rows) if rows % 8 == 0 else rows

    def body(w_ref, g_ref, m_ref, v_ref, d_ref, nm_ref, nv_ref):
        g_ = g_ref[...]
        m_ = ADAM_B1 * m_ref[...] + (1.0 - ADAM_B1) * g_
        v_ = ADAM_B2 * v_ref[...] + (1.0 - ADAM_B2) * (g_ * g_)
        m_hat = m_ / (1.0 - ADAM_B1 ** ADAM_STEP)
        v_hat = v_ / (1.0 - ADAM_B2 ** ADAM_STEP)
        d_ref[...] = -ADAM_LR * (m_hat / (jnp.sqrt(v_hat) + ADAM_EPS) + ADAM_WD * w_ref[...])
        nm_ref[...] = m_
        nv_ref[...] = v_

    spec = pl.BlockSpec((tr, cols), lambda i: (i, 0))
    shp = jax.ShapeDtypeStruct((rows, cols), F32)
    return pl.pallas_call(
        body, name="adamw", grid=(rows // tr,), in_specs=[spec] * 4, out_specs=[spec] * 3,
        out_shape=[shp] * 3, compiler_params=_cparams(("arbitrary",)),
    )(w, g, m, v)


_SMALL = ["norm_pre", "norm_post", "ssm_a_re", "ssm_a_im", "ssm_log_dt", "ssm_b_re", "ssm_b_im",
          "ssm_c_re", "ssm_c_im", "ssm_d", "b_glu", "na_rpb", "ple_norm"]
_BIG = ["w_in", "w_glu", "w_out", "w_ple", "w_ple_gate"]
_WEIGHTS = ["norm_pre", "norm_post", "w_in", "ssm_a_re", "ssm_a_im", "ssm_log_dt", "ssm_b_re", "ssm_b_im",
            "ssm_c_re", "ssm_c_im", "ssm_d", "w_glu", "b_glu", "na_rpb", "w_out", "w_ple", "ple_norm", "w_ple_gate"]
_SMALL_ROWS = 2176


def _pack_small(tensors, tail=None):
    parts = [tensors[n].reshape(-1) for n in _SMALL] + ([] if tail is None else [tail.reshape(-1)])
    flat = jnp.concatenate(parts)
    flat = jnp.pad(flat, (0, _SMALL_ROWS * 128 - flat.shape[0]))
    return flat.reshape(_SMALL_ROWS, 128)


def _unpack_small(packed, shapes):
    flat = packed.reshape(-1)
    out, off = {}, 0
    for n in _SMALL:
        size = int(np.prod(shapes[n]))
        out[n] = flat[off:off + size].reshape(shapes[n])
        off += size
    return out


def _local_grads(x, p, target, wts):
    ssm_names = ["ssm_a_re", "ssm_a_im", "ssm_log_dt", "ssm_b_re", "ssm_b_im", "ssm_c_re", "ssm_c_im", "ssm_d"]
    ssm_params = [wts[n][0] for n in ssm_names]
    blk, blk_vjp = jax.vjp(_ssm_block_params, *ssm_params)
    (m_mat, ws_mat, wot_mat, a16), gathered = _ssm_chunk_matrices(blk, [wts[n][0].astype(BF16) for n in _BIG])
    w_in_g, w_ple_g = gathered[0], gathered[3]
    w_glu, w_out, w_pg = gathered[1].reshape(512, 512), gathered[2].reshape(1024, 1024), gathered[4].reshape(1024, 1024)
    seq = x.shape[0]
    bias_rows, bias_rows_vjp = jax.vjp(_na_bias_rows, wts["na_rpb"][0])
    bias_tab = _na_bias_table(bias_rows, seq // GRID_W)

    u_c, z_s, q_t, q, k_t, k, v_t, v, z_n = _in_proj(x, wts["norm_pre"], w_in_g)
    s_in = _block_matmul([(u_c, ws_mat, False)], "ssm_chunk_states")
    s_prev = _ssm_state_scan(s_in, a16)
    y_ssm_c = _block_matmul([(u_c, m_mat, False), (s_prev, wot_mat, True)], "ssm_chunk_out")
    y_na_t = _na_fwd(q_t, k, v_t, bias_tab)
    cat = _branch_fwd(y_ssm_c, z_s, y_na_t, z_n, w_glu, wts["b_glu"])

    (loss, d_h1, d_cat, d_w_out, d_g_post, d_w_ple, d_g_ple, d_w_pg) = _head(
        x, p, target, cat, w_out, wts["norm_post"], w_ple_g, wts["ple_norm"], w_pg)
    dy_c, d_z_s, d_y_na_t, d_y_na, d_z_n, d_w_glu, d_b_glu = _branch_bwd(
        y_ssm_c, z_s, y_na_t, z_n, w_glu, wts["b_glu"], d_cat)
    d_q_t, d_k, d_v, d_bias_tab = _na_bwd(q_t, q, k_t, k, v, bias_tab, y_na_t, d_y_na_t, d_y_na)

    d_prev = _block_matmul([(dy_c, wot_mat, False)], "ssm_bwd_states")
    g_st, d_a16 = _ssm_state_scan_bwd(d_prev, s_prev, a16)
    d_u_c = _block_matmul([(dy_c, m_mat, True), (g_st, ws_mat, True)], "ssm_bwd_in", out_dtype=BF16)
    d_m = _block_matmul_tn(u_c, dy_c, "ssm_grad_m")
    d_ws = _block_matmul_tn(u_c, g_st, "ssm_grad_ws")
    d_wot = _block_matmul_tn(dy_c, s_prev, "ssm_grad_wot")
    d_ssm = blk_vjp(tuple(_ssm_chunk_matrices_bwd(blk, d_m, d_ws, d_wot, d_a16)))
    (d_rpb,) = bias_rows_vjp(_na_bias_table_bwd(d_bias_tab, seq // GRID_W))

    dparts = [d_u_c, d_z_s, d_q_t, d_k, d_v, d_z_n]
    d_w_in = _in_proj_bwd_w(x, wts["norm_pre"], dparts)
    grad_x, d_g_pre = _in_proj_bwd_x(x, wts["norm_pre"], w_in_g, d_h1, dparts)

    small = {"norm_pre": d_g_pre, "norm_post": d_g_post, "b_glu": d_b_glu, "na_rpb": d_rpb, "ple_norm": d_g_ple}
    for n, g in zip(ssm_names, d_ssm):
        small[n] = g
    big = {"w_in": d_w_in, "w_glu": d_w_glu.reshape(N_CHIPS, 128, 512), "w_out": d_w_out.reshape(N_CHIPS, 256, 1024),
           "w_ple": d_w_ple, "w_ple_gate": d_w_pg.reshape(N_CHIPS, 256, 1024)}
    return loss, grad_x, small, big


def kernel(x, p, norm_pre, norm_post, w_in, ssm_a_re, ssm_a_im, ssm_log_dt, ssm_b_re, ssm_b_im, ssm_c_re, ssm_c_im, ssm_d, w_glu, b_glu, na_rpb, w_out, w_ple, ple_norm, w_ple_gate, loss_target, m_norm_pre, m_norm_post, m_w_in, m_ssm_a_re, m_ssm_a_im, m_ssm_log_dt, m_ssm_b_re, m_ssm_b_im, m_ssm_c_re, m_ssm_c_im, m_ssm_d, m_w_glu, m_b_glu, m_na_rpb, m_w_out, m_w_ple, m_ple_norm, m_w_ple_gate, v_norm_pre, v_norm_post, v_w_in, v_ssm_a_re, v_ssm_a_im, v_ssm_log_dt, v_ssm_b_re, v_ssm_b_im, v_ssm_c_re, v_ssm_c_im, v_ssm_d, v_w_glu, v_b_glu, v_na_rpb, v_w_out, v_w_ple, v_ple_norm, v_w_ple_gate):
    wts = dict(norm_pre=norm_pre, norm_post=norm_post, w_in=w_in, ssm_a_re=ssm_a_re, ssm_a_im=ssm_a_im,
               ssm_log_dt=ssm_log_dt, ssm_b_re=ssm_b_re, ssm_b_im=ssm_b_im, ssm_c_re=ssm_c_re, ssm_c_im=ssm_c_im,
               ssm_d=ssm_d, w_glu=w_glu, b_glu=b_glu, na_rpb=na_rpb, w_out=w_out, w_ple=w_ple, ple_norm=ple_norm,
               w_ple_gate=w_ple_gate)
    mom_m = dict(norm_pre=m_norm_pre, norm_post=m_norm_post, w_in=m_w_in, ssm_a_re=m_ssm_a_re, ssm_a_im=m_ssm_a_im,
                 ssm_log_dt=m_ssm_log_dt, ssm_b_re=m_ssm_b_re, ssm_b_im=m_ssm_b_im, ssm_c_re=m_ssm_c_re,
                 ssm_c_im=m_ssm_c_im, ssm_d=m_ssm_d, w_glu=m_w_glu, b_glu=m_b_glu, na_rpb=m_na_rpb, w_out=m_w_out,
                 w_ple=m_w_ple, ple_norm=m_ple_norm, w_ple_gate=m_w_ple_gate)
    mom_v = dict(norm_pre=v_norm_pre, norm_post=v_norm_post, w_in=v_w_in, ssm_a_re=v_ssm_a_re, ssm_a_im=v_ssm_a_im,
                 ssm_log_dt=v_ssm_log_dt, ssm_b_re=v_ssm_b_re, ssm_b_im=v_ssm_b_im, ssm_c_re=v_ssm_c_re,
                 ssm_c_im=v_ssm_c_im, ssm_d=v_ssm_d, w_glu=v_w_glu, b_glu=v_b_glu, na_rpb=v_na_rpb, w_out=v_w_out,
                 w_ple=v_w_ple, ple_norm=v_ple_norm, w_ple_gate=v_w_ple_gate)

    loss_part, grad_x, small, big = _local_grads(x[0], p[0, 0], loss_target[0], wts)

    small_packed = _pack_small(small, tail=loss_part).reshape(N_CHIPS, _SMALL_ROWS // N_CHIPS, 128)
    reduced = _reduce_scatter([big[n] for n in _BIG] + [small_packed], [BF16] * len(_BIG) + [F32])
    grads = dict(zip(_BIG, reduced[:-1]))
    (small_all,) = _gather_chips([reduced[-1]], "gather_small_grads")
    small_all = small_all.reshape(_SMALL_ROWS, 128)
    loss = small_all.reshape(-1)[sum(int(np.prod(wts[n].shape)) for n in _SMALL)]

    delta, new_m, new_v = {}, {}, {}
    for n in _BIG:
        shp = wts[n].shape
        d_, m_, v_ = _adamw(wts[n][0], grads[n], mom_m[n][0], mom_v[n][0])
        grads[n] = grads[n].reshape(shp)
        delta[n], new_m[n], new_v[n] = d_.reshape(shp), m_.reshape(shp), v_.reshape(shp)
    shapes = {n: wts[n].shape for n in _SMALL}
    d_s, m_s, v_s = _adamw(_pack_small(wts), small_all, _pack_small(mom_m), _pack_small(mom_v))
    for dst, packed in ((grads, small_all), (delta, d_s), (new_m, m_s), (new_v, v_s)):
        dst.update(_unpack_small(packed, shapes))

    return (loss, grad_x[None], *[grads[n] for n in _WEIGHTS], *[delta[n] for n in _WEIGHTS],
            *[new_m[n] for n in _WEIGHTS], *[new_v[n] for n in _WEIGHTS])
```

```python
import functools
import math

import jax
import jax.numpy as jnp
import numpy as np
from jax import lax
from jax.experimental import pallas as pl
from jax.experimental.pallas import tpu as pltpu

F32 = jnp.float32
BF16 = jnp.bfloat16

D_MODEL = 1024
D_PLE = 256
GRID_W = 64
D_SSM = 512
SSM_GROUP = 16
N_GROUPS = 32
SSM_STATE = 64
D_NA = 512
NA_HEADS = 8
NA_HEAD_DIM = 64
NA_ROWS = 8
NA_COLS = 16
D_IN_PROJ = 3072
EPS = 1e-6

CHUNK = 16
GROUPS_PER_BLOCK = 8
N_BLOCKS = N_GROUPS // GROUPS_PER_BLOCK
BLOCK_CH = GROUPS_PER_BLOCK * SSM_GROUP
BLOCK_ST = GROUPS_PER_BLOCK * SSM_STATE
CHUNK_W = CHUNK * BLOCK_CH
STATE_W = 4 * BLOCK_ST

N_CHIPS = 4
MESH = pl.DeviceIdType.MESH

ADAM_LR = 0.001
ADAM_B1 = 0.9
ADAM_B2 = 0.999
ADAM_EPS = 1e-08
ADAM_WD = 0.01
ADAM_STEP = 10

VMEM_LIMIT = 52 * 1024 * 1024
HIGHEST = lax.Precision.HIGHEST


def _cparams(sem=None, **kw):
    if sem is not None:
        kw["dimension_semantics"] = sem
    return pltpu.CompilerParams(vmem_limit_bytes=VMEM_LIMIT, **kw)


def _dot(a, b, dims=((1,), (0,))):
    return lax.dot_general(a, b, (dims, ((), ())), preferred_element_type=F32)


def _dot_nt(a, b):
    return _dot(a, b, ((1,), (1,)))


def _dot_tn(a, b):
    return _dot(a, b, ((0,), (0,)))


def _sigmoid(x):
    return 1.0 / (1.0 + jnp.exp(-x))


_GELU_C = math.sqrt(2.0 / math.pi)


def _gelu_parts(x):
    inner = _GELU_C * (x + 0.044715 * (x * x * x))
    t = jnp.tanh(inner)
    return 0.5 * x * (1.0 + t), t


def _gelu_grad(x, t):
    return 0.5 * (1.0 + t) + 0.5 * x * (1.0 - t * t) * (_GELU_C * (1.0 + 3.0 * 0.044715 * x * x))


def _silu_parts(z):
    s = _sigmoid(z)
    return z * s, s


def _silu_grad(z, s):
    return s * (1.0 + z * (1.0 - s))


def _rms(x):
    r = lax.rsqrt(jnp.mean(x * x, axis=-1, keepdims=True) + EPS)
    return x * r, r


def _rms_bwd(dn, n, r):
    return r * (dn - n * jnp.mean(dn * n, axis=-1, keepdims=True))


def _chunk_scratch(tm):
    return pltpu.VMEM((N_BLOCKS, tm, BLOCK_CH), F32)


def _store_chunks(val, scr, c_ref, dtype):
    nc = scr.shape[1] // CHUNK
    for b in range(N_BLOCKS):
        scr[b] = val[:, b * BLOCK_CH:(b + 1) * BLOCK_CH]
        for j in range(CHUNK):
            c_ref[b, :, j * BLOCK_CH:(j + 1) * BLOCK_CH] = scr[b, pl.ds(j, nc, stride=CHUNK), :].astype(dtype)


def _load_chunks(c_ref, scr):
    nc = scr.shape[1] // CHUNK
    for b in range(N_BLOCKS):
        for j in range(CHUNK):
            scr[b, pl.ds(j, nc, stride=CHUNK), :] = c_ref[b, :, j * BLOCK_CH:(j + 1) * BLOCK_CH].astype(F32)
    return jnp.concatenate([scr[b] for b in range(N_BLOCKS)], axis=1)


def _chunk_spec(tm):
    return pl.BlockSpec((N_BLOCKS, tm // CHUNK, CHUNK_W), lambda i: (0, i, 0))


def _heads_t_spec(tm):
    return pl.BlockSpec((D_NA, tm), lambda i: (0, i))


def _in_proj(x, g_pre, w_in_g, tm=256):
    L = x.shape[0]
    wn = w_in_g.shape[2]

    def body(x_ref, g_ref, w_ref, uc_ref, zs_ref, qt_ref, q_ref, kt_ref, k_ref, vt_ref, v_ref, zn_ref, u_scr):
        n, _ = _rms(x_ref[...])
        hn = (n * g_ref[...]).astype(BF16)
        proj = jnp.concatenate([_dot(hn, w_ref[j]) for j in range(N_CHIPS)], axis=1)
        _store_chunks(proj[:, 0:512], u_scr, uc_ref, BF16)
        zs_ref[...] = proj[:, 512:1024]
        q = proj[:, 1024:1536] * (NA_HEAD_DIM ** -0.5)
        for val, t_ref, n_ref in ((q, qt_ref, q_ref), (proj[:, 1536:2048], kt_ref, k_ref), (proj[:, 2048:2560], vt_ref, v_ref)):
            t_ref[...] = val.T.astype(BF16)
            n_ref[...] = val.astype(BF16)
        zn_ref[...] = proj[:, 2560:3072]

    tok = jax.ShapeDtypeStruct((L, 512), F32)
    tr = jax.ShapeDtypeStruct((D_NA, L), BF16)
    hm = jax.ShapeDtypeStruct((L, D_NA), BF16)
    tspec = pl.BlockSpec((tm, 512), lambda i: (i, 0))
    return pl.pallas_call(
        body, name="in_proj", grid=(L // tm,),
        in_specs=[pl.BlockSpec((tm, D_MODEL), lambda i: (i, 0)),
                  pl.BlockSpec((1, D_MODEL), lambda i: (0, 0)),
                  pl.BlockSpec((N_CHIPS, D_MODEL, wn), lambda i: (0, 0, 0))],
        out_specs=[_chunk_spec(tm), tspec] + [_heads_t_spec(tm), tspec] * 3 + [tspec],
        out_shape=[jax.ShapeDtypeStruct((N_BLOCKS, L // CHUNK, CHUNK_W), BF16), tok, tr, hm, tr, hm, tr, hm, tok],
        scratch_shapes=[_chunk_scratch(tm)],
        compiler_params=_cparams(("arbitrary",)),
    )(x, g_pre, w_in_g)


def _ssm_block_params(a_re, a_im, log_dt, b_re, b_im, c_re, c_im, d):
    eye_g = jnp.eye(GROUPS_PER_BLOCK, dtype=F32)[None, None, :, None, :, None]

    def lanes(t):
        return t.reshape(2, N_BLOCKS, 1, BLOCK_ST)

    def expand(t):
        return (t[:, :, :, :, None, :] * eye_g).reshape(2, N_BLOCKS, BLOCK_CH, BLOCK_ST)

    b_shape = (2, N_BLOCKS, GROUPS_PER_BLOCK, SSM_STATE, SSM_GROUP)
    c_shape = (2, N_BLOCKS, GROUPS_PER_BLOCK, SSM_GROUP, SSM_STATE)
    return (lanes(a_re), lanes(a_im), lanes(jnp.broadcast_to(log_dt[..., None], a_re.shape)),
            expand(b_re.reshape(b_shape).transpose(0, 1, 2, 4, 3)), expand(b_im.reshape(b_shape).transpose(0, 1, 2, 4, 3)),
            expand(c_re.reshape(c_shape)), expand(c_im.reshape(c_shape)), d.reshape(N_BLOCKS, 1, BLOCK_CH))


def _ssm_discretise(ar, ai, ldt):
    dt = jnp.exp(ldt)
    mag = jnp.exp(dt * ar)
    abr = mag * jnp.cos(dt * ai)
    abi = mag * jnp.sin(dt * ai)
    num_re = abr - 1.0
    num_im = abi
    denom = ar * ar + ai * ai
    coef_re = (num_re * ar + num_im * ai) / denom
    coef_im = (num_im * ar - num_re * ai) / denom
    return abr, abi, coef_re, coef_im


_POW_ROWS = 24


def _ssm_fill_powers(ar_ref, ai_ref, ldt_ref, br_ref, bi_ref, pw_ref, bbar_ref):
    for d in range(2):
        abr, abi, cfr, cfi = _ssm_discretise(ar_ref[d, 0], ai_ref[d, 0], ldt_ref[d, 0])
        bbar_ref[d, 0] = cfr * br_ref[d, 0] - cfi * bi_ref[d, 0]
        bbar_ref[d, 1] = cfr * bi_ref[d, 0] + cfi * br_ref[d, 0]
        pr, pi = jnp.ones_like(abr), jnp.zeros_like(abi)
        for t in range(CHUNK + 1):
            pw_ref[d, 0, t:t + 1, :] = pr
            pw_ref[d, 1, t:t + 1, :] = pi
            pr, pi = pr * abr - pi * abi, pr * abi + pi * abr


def _dot_hi(a, b, dims=((1,), (0,))):
    return lax.dot_general(a, b, (dims, ((), ())), precision=lax.Precision.HIGH, preferred_element_type=F32)


def _ssm_stack_inputs(d, pw_ref, bbar_ref, xs_ref):
    for t in range(CHUNK):
        pr, pi = pw_ref[d, 0, t:t + 1, :], pw_ref[d, 1, t:t + 1, :]
        xs_ref[0, t * BLOCK_CH:(t + 1) * BLOCK_CH, :] = bbar_ref[d, 0] * pr - bbar_ref[d, 1] * pi
        xs_ref[1, t * BLOCK_CH:(t + 1) * BLOCK_CH, :] = bbar_ref[d, 0] * pi + bbar_ref[d, 1] * pr


def _eye(n):
    return (lax.broadcasted_iota(jnp.int32, (n, n), 0) == lax.broadcasted_iota(jnp.int32, (n, n), 1)).astype(F32)


def _ssm_param_specs():
    vec = pl.BlockSpec((2, 1, 1, BLOCK_ST), lambda b, j: (0, b, 0, 0))
    mat = pl.BlockSpec((2, 1, BLOCK_CH, BLOCK_ST), lambda b, j: (0, b, 0, 0))
    return [vec, vec, vec, mat, mat, mat, mat, pl.BlockSpec((1, 1, BLOCK_CH), lambda b, j: (b, 0, 0))]


def _ssm_chunk_matrices(blk, shards):
    n = len(shards)

    def body(*refs):
        ar_ref, ai_ref, ldt_ref, br_ref, bi_ref, cr_ref, ci_ref, d_ref = refs[:8]
        m_ref, ws_ref, wot_ref, a16_ref = refs[8 + n:12 + n]
        pw_ref, bbar_ref, lag_ref, xs_ref = refs[12 + 2 * n:16 + 2 * n]
        gather = _ChipGather(refs[8:8 + n], refs[12 + n:12 + 2 * n], refs[16 + 2 * n:])
        b, j = pl.program_id(0), pl.program_id(1)
        pl.when((b == 0) & (j == 0))(gather.start)
        pl.when((b == N_BLOCKS - 1) & (j == 0))(gather.forward)
        pl.when((b == N_BLOCKS - 1) & (j == CHUNK - 1))(gather.finish)

        @pl.when(j == 0)
        def _():
            _ssm_fill_powers(ar_ref, ai_ref, ldt_ref, br_ref, bi_ref, pw_ref, bbar_ref)
            zero_lag = d_ref[0] * _eye(BLOCK_CH)
            for d in range(2):
                _ssm_stack_inputs(d, pw_ref, bbar_ref, xs_ref)
                taps = (_dot_hi(xs_ref[0], cr_ref[d, 0], ((1,), (1,)))
                        - _dot_hi(xs_ref[1], ci_ref[d, 0], ((1,), (1,))))
                zero_lag = zero_lag + taps[0:BLOCK_CH]
                for t in range(1, CHUNK):
                    lag_ref[CHUNK - 1 + t if d == 0 else CHUNK - 1 - t] = taps[t * BLOCK_CH:(t + 1) * BLOCK_CH]
            lag_ref[CHUNK - 1] = zero_lag
            a16_ref[0] = jnp.concatenate([pw_ref[d, ri, CHUNK:CHUNK + 1, :] for d in range(2) for ri in range(2)], axis=1)

        m_ref[0] = jnp.concatenate([lag_ref[jp - j + CHUNK - 1] for jp in range(CHUNK)], axis=1).astype(BF16)

        def power(d, t):
            return pw_ref[d, 0, pl.ds(t, 1), :], pw_ref[d, 1, pl.ds(t, 1), :]

        parts = []
        for d, t in ((0, CHUNK - 1 - j), (1, j)):
            pr, pi = power(d, t)
            parts += [bbar_ref[d, 0] * pr - bbar_ref[d, 1] * pi, bbar_ref[d, 0] * pi + bbar_ref[d, 1] * pr]
        ws_ref[0] = jnp.concatenate(parts, axis=1).astype(BF16)
        parts = []
        for d, t in ((0, j + 1), (1, CHUNK - j)):
            pr, pi = power(d, t)
            parts += [cr_ref[d, 0] * pr - ci_ref[d, 0] * pi, -cr_ref[d, 0] * pi - ci_ref[d, 0] * pr]
        wot_ref[0] = jnp.concatenate(parts, axis=1).astype(BF16)

    row = pl.BlockSpec((1, BLOCK_CH, CHUNK_W), lambda b, j: (b, j, 0))
    mat = jax.ShapeDtypeStruct((N_BLOCKS, CHUNK_W, CHUNK_W), BF16)
    outs = pl.pallas_call(
        body, name="ssm_chunk_matrices", grid=(N_BLOCKS, CHUNK),
        in_specs=_ssm_param_specs() + _hbm_specs(n),
        out_specs=[row, row, row, pl.BlockSpec((1, 1, STATE_W), lambda b, j: (b, 0, 0))] + _hbm_specs(n),
        out_shape=[mat, mat, mat, jax.ShapeDtypeStruct((N_BLOCKS, 1, STATE_W), F32)] + _gather_out_shapes(shards),
        scratch_shapes=[pltpu.VMEM((2, 2, _POW_ROWS, BLOCK_ST), F32), pltpu.VMEM((2, 2, BLOCK_CH, BLOCK_ST), F32),
                        pltpu.VMEM((2 * CHUNK, BLOCK_CH, BLOCK_CH), F32), pltpu.VMEM((2, CHUNK_W, BLOCK_ST), F32)]
        + _gather_semaphores(n),
        compiler_params=_cparams(("arbitrary", "arbitrary"), has_side_effects=True),
    )(*blk, *shards)
    return outs[:4], outs[4:]


def _ssm_chunk_matrices_bwd(blk, d_m, d_ws, d_wot, d_a16):
    def body(ar_ref, ai_ref, ldt_ref, br_ref, bi_ref, cr_ref, ci_ref, d_ref, dm_ref, dws_ref, dwot_ref, da16_ref,
             dar_ref, dai_ref, dldt_ref, dbr_ref, dbi_ref, dcr_ref, dci_ref, dd_ref,
             pw_ref, bbar_ref, dlag_ref, dbbar_ref, dc_ref, dpw_ref, xs_ref, dts_ref):
        j = pl.program_id(1)
        w = BLOCK_ST

        @pl.when(j == 0)
        def _():
            _ssm_fill_powers(ar_ref, ai_ref, ldt_ref, br_ref, bi_ref, pw_ref, bbar_ref)
            for r in (dlag_ref, dbbar_ref, dc_ref, dpw_ref):
                r[...] = jnp.zeros_like(r)

        def x_chain(d, t, dxr, dxi):
            pr, pi = pw_ref[d, 0, pl.ds(t, 1), :], pw_ref[d, 1, pl.ds(t, 1), :]
            bbr, bbi = bbar_ref[d, 0], bbar_ref[d, 1]
            dbbar_ref[d, 0] += dxr * pr + dxi * pi
            dbbar_ref[d, 1] += dxi * pr - dxr * pi
            dpw_ref[d, 0, pl.ds(t, 1), :] += jnp.sum(dxr * bbr + dxi * bbi, axis=0, keepdims=True)
            dpw_ref[d, 1, pl.ds(t, 1), :] += jnp.sum(dxi * bbr - dxr * bbi, axis=0, keepdims=True)

        def z_chain(d, t, dzr, dzi):
            pr, pi = pw_ref[d, 0, pl.ds(t, 1), :], pw_ref[d, 1, pl.ds(t, 1), :]
            c_r, c_i = cr_ref[d, 0], ci_ref[d, 0]
            dc_ref[d, 0] += dzr * pr - dzi * pi
            dc_ref[d, 1] += -dzr * pi - dzi * pr
            dpw_ref[d, 0, pl.ds(t, 1), :] += jnp.sum(dzr * c_r - dzi * c_i, axis=0, keepdims=True)
            dpw_ref[d, 1, pl.ds(t, 1), :] += jnp.sum(-dzr * c_i - dzi * c_r, axis=0, keepdims=True)

        for jp in range(CHUNK):
            dlag_ref[jp - j + CHUNK - 1] += dm_ref[0, :, jp * BLOCK_CH:(jp + 1) * BLOCK_CH].astype(F32)
        quarter = lambda ref, i: ref[0, :, i * w:(i + 1) * w].astype(F32)
        x_chain(0, CHUNK - 1 - j, quarter(dws_ref, 0), quarter(dws_ref, 1))
        x_chain(1, j, quarter(dws_ref, 2), quarter(dws_ref, 3))
        z_chain(0, j + 1, quarter(dwot_ref, 0), quarter(dwot_ref, 1))
        z_chain(1, CHUNK - j, quarter(dwot_ref, 2), quarter(dwot_ref, 3))

        @pl.when(j == CHUNK - 1)
        def _():
            for d in range(2):
                _ssm_stack_inputs(d, pw_ref, bbar_ref, xs_ref)
                for t in range(CHUNK):
                    dts_ref[t * BLOCK_CH:(t + 1) * BLOCK_CH, :] = dlag_ref[CHUNK - 1 + t if d == 0 else CHUNK - 1 - t]
                d_taps = dts_ref[...]
                dc_ref[d, 0] += _dot_hi(d_taps, xs_ref[0], ((0,), (0,)))
                dc_ref[d, 1] -= _dot_hi(d_taps, xs_ref[1], ((0,), (0,)))
                xs_ref[0] = _dot_hi(d_taps, cr_ref[d, 0])
                xs_ref[1] = -_dot_hi(d_taps, ci_ref[d, 0])
                for t in range(CHUNK):
                    rows = slice(t * BLOCK_CH, (t + 1) * BLOCK_CH)
                    x_chain(d, t, xs_ref[0, rows, :], xs_ref[1, rows, :])
            dd_ref[0] = jnp.sum(dlag_ref[CHUNK - 1] * _eye(BLOCK_CH), axis=0, keepdims=True)
            for d in range(2):
                (abr, abi, cfr, cfi), disc_vjp = jax.vjp(_ssm_discretise, ar_ref[d, 0], ai_ref[d, 0], ldt_ref[d, 0])
                dpr = dpw_ref[d, 0, CHUNK:CHUNK + 1, :] + da16_ref[0, :, 2 * d * w:(2 * d + 1) * w]
                dpi = dpw_ref[d, 1, CHUNK:CHUNK + 1, :] + da16_ref[0, :, (2 * d + 1) * w:(2 * d + 2) * w]
                dabr, dabi = jnp.zeros_like(abr), jnp.zeros_like(abi)
                for t in range(CHUNK, 0, -1):
                    qr, qi = pw_ref[d, 0, t - 1:t, :], pw_ref[d, 1, t - 1:t, :]
                    dabr = dabr + dpr * qr + dpi * qi
                    dabi = dabi + dpi * qr - dpr * qi
                    dpr, dpi = (dpr * abr + dpi * abi + dpw_ref[d, 0, t - 1:t, :],
                                dpi * abr - dpr * abi + dpw_ref[d, 1, t - 1:t, :])
                dbbr, dbbi = dbbar_ref[d, 0], dbbar_ref[d, 1]
                b_r, b_i = br_ref[d, 0], bi_ref[d, 0]
                dbr_ref[d, 0] = cfr * dbbr + cfi * dbbi
                dbi_ref[d, 0] = cfr * dbbi - cfi * dbbr
                dcfr = jnp.sum(b_r * dbbr + b_i * dbbi, axis=0, keepdims=True)
                dcfi = jnp.sum(b_r * dbbi - b_i * dbbr, axis=0, keepdims=True)
                dar_ref[d, 0], dai_ref[d, 0], dldt_ref[d, 0] = disc_vjp((dabr, dabi, dcfr, dcfi))
                dcr_ref[d, 0] = dc_ref[d, 0]
                dci_ref[d, 0] = dc_ref[d, 1]

    row = pl.BlockSpec((1, BLOCK_CH, CHUNK_W), lambda b, j: (b, j, 0))
    specs = _ssm_param_specs()
    acc = lambda *s: pltpu.VMEM(s, F32)
    return pl.pallas_call(
        body, name="ssm_chunk_matrices_bwd", grid=(N_BLOCKS, CHUNK),
        in_specs=specs + [row, row, row, pl.BlockSpec((1, 1, STATE_W), lambda b, j: (b, 0, 0))],
        out_specs=specs,
        out_shape=[jax.ShapeDtypeStruct(t.shape, F32) for t in blk],
        scratch_shapes=[acc(2, 2, _POW_ROWS, BLOCK_ST), acc(2, 2, BLOCK_CH, BLOCK_ST), acc(2 * CHUNK, BLOCK_CH, BLOCK_CH),
                        acc(2, 2, BLOCK_CH, BLOCK_ST), acc(2, 2, BLOCK_CH, BLOCK_ST), acc(2, 2, _POW_ROWS, BLOCK_ST),
                        acc(2, CHUNK_W, BLOCK_ST), acc(CHUNK_W, BLOCK_CH)],
        compiler_params=_cparams(("arbitrary", "arbitrary")),
    )(*blk, d_m, d_ws, d_wot, d_a16)


def _block_matmul(terms, name, out_dtype=F32, tn=1024):
    nc = terms[0][0].shape[1]
    n_out = terms[0][1].shape[1] if terms[0][2] else terms[0][1].shape[2]
    flags = [t[2] for t in terms]

    def body(*refs):
        out_ref = refs[-1]
        acc = None
        for t, transposed in enumerate(flags):
            a = refs[2 * t][0].astype(BF16)
            w = refs[2 * t + 1][0]
            part = _dot_nt(a, w) if transposed else _dot(a, w)
            acc = part if acc is None else acc + part
        out_ref[0] = acc.astype(out_dtype)

    in_specs, args = [], []
    for a, w, transposed in terms:
        k = a.shape[2]
        in_specs.append(pl.BlockSpec((1, nc, k), lambda b, n: (b, 0, 0)))
        if transposed:
            in_specs.append(pl.BlockSpec((1, tn, k), lambda b, n: (b, n, 0)))
        else:
            in_specs.append(pl.BlockSpec((1, k, tn), lambda b, n: (b, 0, n)))
        args += [a, w]
    return pl.pallas_call(
        body, name=name, grid=(N_BLOCKS, n_out // tn), in_specs=in_specs,
        out_specs=pl.BlockSpec((1, nc, tn), lambda b, n: (b, 0, n)),
        out_shape=jax.ShapeDtypeStruct((N_BLOCKS, nc, n_out), out_dtype),
        compiler_params=_cparams(("arbitrary", "arbitrary")),
    )(*args)


def _block_matmul_tn(a, b, name, tile=1024):
    nc, m = a.shape[1], a.shape[2]
    n = b.shape[2]

    def body(a_ref, b_ref, out_ref):
        out_ref[0] = _dot_tn(a_ref[0].astype(BF16), b_ref[0].astype(BF16)).astype(BF16)

    return pl.pallas_call(
        body, name=name, grid=(N_BLOCKS, m // tile, n // tile),
        in_specs=[pl.BlockSpec((1, nc, tile), lambda blk, i, j: (blk, 0, i)),
                  pl.BlockSpec((1, nc, tile), lambda blk, i, j: (blk, 0, j))],
        out_specs=pl.BlockSpec((1, tile, tile), lambda blk, i, j: (blk, i, j)),
        out_shape=jax.ShapeDtypeStruct((N_BLOCKS, m, n), BF16),
        compiler_params=_cparams(("arbitrary", "arbitrary", "arbitrary")),
    )(a, b)


def _cmul(ar, ai, xr, xi):
    return ar * xr - ai * xi, ar * xi + ai * xr


def _cmul_conj(ar, ai, xr, xi):
    return ar * xr + ai * xi, ar * xi - ai * xr


def _ssm_state_scan(s_in, a16):
    nc = s_in.shape[1]
    w = BLOCK_ST

    def body(sin_ref, a_ref, out_ref):
        a = a_ref[0]
        afr, afi, abr, abi = a[:, 0:w], a[:, w:2 * w], a[:, 2 * w:3 * w], a[:, 3 * w:4 * w]

        def step(c, carry):
            fr, fi, br, bi = carry
            cb = nc - 1 - c
            out_ref[0, pl.ds(c, 1), 0:w] = fr
            out_ref[0, pl.ds(c, 1), w:2 * w] = fi
            out_ref[0, pl.ds(cb, 1), 2 * w:3 * w] = br
            out_ref[0, pl.ds(cb, 1), 3 * w:4 * w] = bi
            nfr, nfi = _cmul(afr, afi, fr, fi)
            nbr, nbi = _cmul(abr, abi, br, bi)
            return (nfr + sin_ref[0, pl.ds(c, 1), 0:w], nfi + sin_ref[0, pl.ds(c, 1), w:2 * w],
                    nbr + sin_ref[0, pl.ds(cb, 1), 2 * w:3 * w], nbi + sin_ref[0, pl.ds(cb, 1), 3 * w:4 * w])

        z = jnp.zeros((1, w), F32)
        lax.fori_loop(0, nc, step, (z, z, z, z))

    spec = pl.BlockSpec((1, nc, STATE_W), lambda b: (b, 0, 0))
    return pl.pallas_call(
        body, name="ssm_state_scan", grid=(N_BLOCKS,),
        in_specs=[spec, pl.BlockSpec((1, 1, STATE_W), lambda b: (b, 0, 0))],
        out_specs=spec, out_shape=jax.ShapeDtypeStruct(s_in.shape, F32),
        compiler_params=_cparams(("arbitrary",)),
    )(s_in, a16)


def _ssm_state_scan_bwd(d_prev, s_prev, a16):
    nc = d_prev.shape[1]
    w = BLOCK_ST

    def body(dp_ref, sp_ref, a_ref, g_ref, da_ref):
        a = a_ref[0]
        afr, afi, abr, abi = a[:, 0:w], a[:, w:2 * w], a[:, 2 * w:3 * w], a[:, 3 * w:4 * w]

        def step(i, carry):
            gfr, gfi, gbr, gbi, dafr, dafi, dabr, dabi = carry
            cf = nc - 1 - i
            cb = i
            g_ref[0, pl.ds(cf, 1), 0:w] = gfr
            g_ref[0, pl.ds(cf, 1), w:2 * w] = gfi
            g_ref[0, pl.ds(cb, 1), 2 * w:3 * w] = gbr
            g_ref[0, pl.ds(cb, 1), 3 * w:4 * w] = gbi
            sfr, sfi = sp_ref[0, pl.ds(cf, 1), 0:w], sp_ref[0, pl.ds(cf, 1), w:2 * w]
            sbr, sbi = sp_ref[0, pl.ds(cb, 1), 2 * w:3 * w], sp_ref[0, pl.ds(cb, 1), 3 * w:4 * w]
            dafr = dafr + gfr * sfr + gfi * sfi
            dafi = dafi + gfi * sfr - gfr * sfi
            dabr = dabr + gbr * sbr + gbi * sbi
            dabi = dabi + gbi * sbr - gbr * sbi
            nfr, nfi = _cmul_conj(afr, afi, gfr, gfi)
            nbr, nbi = _cmul_conj(abr, abi, gbr, gbi)
            return (nfr + dp_ref[0, pl.ds(cf, 1), 0:w], nfi + dp_ref[0, pl.ds(cf, 1), w:2 * w],
                    nbr + dp_ref[0, pl.ds(cb, 1), 2 * w:3 * w], nbi + dp_ref[0, pl.ds(cb, 1), 3 * w:4 * w],
                    dafr, dafi, dabr, dabi)

        z = jnp.zeros((1, w), F32)
        res = lax.fori_loop(0, nc, step, (z,) * 8)
        da_ref[0] = jnp.concatenate(res[4:], axis=1)

    spec = pl.BlockSpec((1, nc, STATE_W), lambda b: (b, 0, 0))
    aspec = pl.BlockSpec((1, 1, STATE_W), lambda b: (b, 0, 0))
    return pl.pallas_call(
        body, name="ssm_state_scan_bwd", grid=(N_BLOCKS,),
        in_specs=[spec, spec, aspec], out_specs=[spec, aspec],
        out_shape=[jax.ShapeDtypeStruct(d_prev.shape, F32), jax.ShapeDtypeStruct((N_BLOCKS, 1, STATE_W), F32)],
        compiler_params=_cparams(("arbitrary",)),
    )(d_prev, s_prev, a16)


NA_PAIR = 2 * GRID_W
NA_WIN_ROWS = NA_ROWS + 2
NA_WIN = NA_WIN_ROWS * GRID_W
NA_PAIRS_PER_STEP = 8
NA_CASES = 5
NA_MASKED = -1e30


def _na_pair_window(m, rows):
    rs0 = jnp.clip(2 * m - NA_ROWS // 2, 0, rows - NA_ROWS)
    ws = jnp.minimum(rs0, rows - NA_WIN_ROWS)
    last = rows // 2 - 1
    case = jnp.where(m == 0, 0, jnp.where(m == 1, 1, jnp.where(m == last - 1, 3, jnp.where(m == last, 4, 2))))
    return ws, case


def _na_row_offsets(rows):
    last = rows // 2 - 1
    geom = []
    for m in (0, 1, 2, last - 1, last):
        ws = min(max(2 * m - NA_ROWS // 2, 0), rows - NA_ROWS, rows - NA_WIN_ROWS)
        per_case = []
        for i in range(NA_WIN_ROWS):
            pair = []
            for rr in range(2):
                r = 2 * m + rr
                rs = min(max(r - NA_ROWS // 2, 0), rows - NA_ROWS)
                pair.append(ws + i - r + NA_ROWS - 1 if rs <= ws + i < rs + NA_ROWS else None)
            per_case.append(pair)
        geom.append(per_case)
    return geom


def _na_col_select():
    qc = np.arange(NA_PAIR)[None, :] % GRID_W
    kc = np.arange(GRID_W)[:, None]
    dc = np.clip(kc - qc + NA_COLS - 1, 0, 2 * NA_COLS - 2)
    return jnp.asarray((np.arange(2 * NA_COLS - 1)[:, None, None] == dc[None]).astype(np.float32))


def _na_bias_rows(rpb):
    return jnp.einsum("hrd,dkl->hrkl", rpb, _na_col_select(), precision=HIGHEST)


def _na_col_window():
    qc = lax.broadcasted_iota(jnp.int32, (GRID_W, NA_PAIR), 1) % GRID_W
    kc = lax.broadcasted_iota(jnp.int32, (GRID_W, NA_PAIR), 0)
    cs = jnp.clip(qc - NA_COLS // 2, 0, GRID_W - NA_COLS)
    first_row = lax.broadcasted_iota(jnp.int32, (GRID_W, NA_PAIR), 1) < GRID_W
    return (kc >= cs) & (kc < cs + NA_COLS), first_row


def _na_bias_table(bias_rows, rows):
    geom = _na_row_offsets(rows)

    def body(br_ref, tab_ref):
        col_ok, first_row = _na_col_window()
        masked = jnp.full((GRID_W, NA_PAIR), NA_MASKED, F32)
        for case in range(NA_CASES):
            for i in range(NA_WIN_ROWS):
                d0, d1 = geom[case][i]
                t0 = masked if d0 is None else br_ref[0, d0]
                t1 = masked if d1 is None else br_ref[0, d1]
                tile = jnp.where(col_ok, jnp.where(first_row, t0, t1), NA_MASKED)
                tab_ref[0, case, i * GRID_W:(i + 1) * GRID_W, :] = tile

    return pl.pallas_call(
        body, name="na_bias_table", grid=(NA_HEADS,),
        in_specs=[pl.BlockSpec((1, 2 * NA_ROWS - 1, GRID_W, NA_PAIR), lambda h: (h, 0, 0, 0))],
        out_specs=pl.BlockSpec((1, NA_CASES, NA_WIN, NA_PAIR), lambda h: (h, 0, 0, 0)),
        out_shape=jax.ShapeDtypeStruct((NA_HEADS, NA_CASES, NA_WIN, NA_PAIR), F32),
        compiler_params=_cparams(("arbitrary",)),
    )(bias_rows)


def _na_bias_table_bwd(d_tab, rows):
    geom = _na_row_offsets(rows)

    def body(dt_ref, dbr_ref):
        col_ok, first_row = _na_col_window()
        acc = [None] * (2 * NA_ROWS - 1)
        for case in range(NA_CASES):
            for i in range(NA_WIN_ROWS):
                tile = jnp.where(col_ok, dt_ref[0, case, i * GRID_W:(i + 1) * GRID_W, :], 0.0)
                for rr, d in enumerate(geom[case][i]):
                    if d is not None:
                        part = jnp.where(first_row if rr == 0 else ~first_row, tile, 0.0)
                        acc[d] = part if acc[d] is None else acc[d] + part
        for d, a in enumerate(acc):
            dbr_ref[0, d] = jnp.zeros((GRID_W, NA_PAIR), F32) if a is None else a

    return pl.pallas_call(
        body, name="na_bias_table_bwd", grid=(NA_HEADS,),
        in_specs=[pl.BlockSpec((1, NA_CASES, NA_WIN, NA_PAIR), lambda h: (h, 0, 0, 0))],
        out_specs=pl.BlockSpec((1, 2 * NA_ROWS - 1, GRID_W, NA_PAIR), lambda h: (h, 0, 0, 0)),
        out_shape=jax.ShapeDtypeStruct((NA_HEADS, 2 * NA_ROWS - 1, GRID_W, NA_PAIR), F32),
        compiler_params=_cparams(("arbitrary",)),
    )(d_tab)


NA_BLK = 64


def _na_blocks():
    return [slice(i * NA_BLK, (i + 1) * NA_BLK) for i in range(NA_WIN // NA_BLK)]


def _na_softmax(qk, bias_ref, hh, case):
    m = jnp.full((NA_BLK, NA_PAIR), -jnp.inf, F32)
    scores = []
    for blk in _na_blocks():
        s = qk[blk, :] + bias_ref[hh, case, blk, :]
        scores.append(s)
        m = jnp.maximum(m, s)
    m = jnp.max(m, axis=0, keepdims=True)
    l = jnp.zeros((NA_BLK, NA_PAIR), F32)
    exps = []
    for s in scores:
        e = jnp.exp(s - m)
        exps.append(e)
        l = l + e
    return exps, jnp.sum(l, axis=0, keepdims=True)


def _na_units(step, rows):
    units = []
    for pp in range(NA_PAIRS_PER_STEP):
        ws, case = _na_pair_window(step * NA_PAIRS_PER_STEP + pp, rows)
        win = pl.ds(pl.multiple_of(ws * GRID_W, NA_PAIR), NA_WIN)
        lanes = slice(pp * NA_PAIR, (pp + 1) * NA_PAIR)
        for hh in range(2):
            units.append((pp, hh, case, win, lanes, slice(hh * NA_HEAD_DIM, (hh + 1) * NA_HEAD_DIM)))
    return units


def _na_pipeline(n, before, middle, after, lookahead=2):
    for u in range(min(lookahead, n)):
        for f in before:
            f(u)
    for u in range(n):
        middle(u)
        if u + lookahead < n:
            for f in before:
                f(u + lookahead)
        for f in after:
            f(u)


def _head_rows(t, hh):
    row_head = lax.broadcasted_iota(jnp.int32, t.shape, 0) // NA_HEAD_DIM
    return jnp.where(row_head == hh, t, jnp.zeros_like(t))


def _heads_block_diag(t):
    lane_head = lax.broadcasted_iota(jnp.int32, t.shape, 1) // NA_HEAD_DIM
    zero = jnp.zeros_like(t)
    return jnp.concatenate([jnp.where(lane_head == 0, t, zero), jnp.where(lane_head == 1, t, zero)], axis=0)


def _na_fwd(q_t, k, v_t, bias_tab):
    L = k.shape[0]
    rows = L // GRID_W
    step_w = NA_PAIRS_PER_STEP * NA_PAIR

    def body(q_ref, k_ref, v_ref, bt_ref, o_ref):
        units = _na_units(pl.program_id(1), rows)
        qk, probs = {}, {}

        def scores(u):
            _, hh, _, win, lanes, _ = units[u]
            qk[u] = _dot(k_ref[win, :], _head_rows(q_ref[:, lanes], hh))

        def softmax(u):
            _, hh, case, _, _, _ = units[u]
            exps, l = _na_softmax(qk.pop(u), bt_ref, hh, case)
            probs[u] = jnp.concatenate([t.astype(BF16) for t in exps], axis=0), l

        def output(u):
            _, _, _, win, lanes, hrows = units[u]
            e, l = probs.pop(u)
            o_ref[hrows, lanes] = _dot(v_ref[hrows, win], e) / l

        _na_pipeline(len(units), [scores], softmax, [output])

    q_spec = pl.BlockSpec((NA_PAIR, step_w), lambda h, s: (h, s))
    return pl.pallas_call(
        body, name="na_fwd", grid=(NA_HEADS // 2, L // step_w),
        in_specs=[q_spec, pl.BlockSpec((L, NA_PAIR), lambda h, s: (0, h)),
                  pl.BlockSpec((NA_PAIR, L), lambda h, s: (h, 0)),
                  pl.BlockSpec((2, NA_CASES, NA_WIN, NA_PAIR), lambda h, s: (h, 0, 0, 0))],
        out_specs=q_spec,
        out_shape=jax.ShapeDtypeStruct((D_NA, L), F32),
        compiler_params=_cparams(("arbitrary", "arbitrary")),
    )(q_t, k, v_t, bias_tab)


def _na_bwd(q_t, q, k_t, k, v, bias_tab, out_t, d_out_t, d_out):
    L = k.shape[0]
    rows = L // GRID_W
    step_w = NA_PAIRS_PER_STEP * NA_PAIR

    def body(qt_ref, q_ref, kt_ref, k_ref, v_ref, bt_ref, ot_ref, dot_ref, do_ref, dq_ref, dk_ref, dv_ref, dbt_ref):
        @pl.when(pl.program_id(1) == 0)
        def _():
            dk_ref[...] = jnp.zeros_like(dk_ref)
            dv_ref[...] = jnp.zeros_like(dv_ref)
            dbt_ref[...] = jnp.zeros_like(dbt_ref)

        units = _na_units(pl.program_id(1), rows)
        qk, dp, dsb, pb = {}, {}, {}, {}

        def scores(u):
            _, hh, _, win, lanes, _ = units[u]
            qk[u] = _dot(k_ref[win, :], _head_rows(qt_ref[:, lanes], hh))

        def d_probs(u):
            _, hh, _, win, lanes, _ = units[u]
            dp[u] = _dot(v_ref[win, :], _head_rows(dot_ref[:, lanes].astype(BF16), hh))

        def softmax_bwd(u):
            _, hh, case, _, lanes, hrows = units[u]
            exps, l = _na_softmax(qk.pop(u), bt_ref, hh, case)
            inv_l = 1.0 / l
            delta = jnp.sum(dot_ref[hrows, lanes] * ot_ref[hrows, lanes], axis=0, keepdims=True)
            d_p = dp.pop(u)
            ds_blocks, p_blocks = [], []
            for blk, e in zip(_na_blocks(), exps):
                p = e * inv_l
                ds = p * (d_p[blk, :] - delta)
                dbt_ref[hh, case, blk, :] += ds
                ds_blocks.append(ds.astype(BF16))
                p_blocks.append(p.astype(BF16))
            dsb[u] = jnp.concatenate(ds_blocks, axis=0)
            pb[u] = jnp.concatenate(p_blocks, axis=0)

        def d_query(u):
            _, _, _, win, lanes, hrows = units[u]
            dq_ref[hrows, lanes] = _dot(kt_ref[hrows, win], dsb[u]) * (NA_HEAD_DIM ** -0.5)

        def d_keys_values(u):
            pp, hh, _, win, _, _ = units[u]
            if hh == 1:
                tokens = slice(pp * NA_PAIR, (pp + 1) * NA_PAIR)
                dk_ref[win, :] += _dot(jnp.concatenate([dsb.pop(u - 1), dsb.pop(u)], axis=1), _heads_block_diag(q_ref[tokens, :]))
                dv_ref[win, :] += _dot(jnp.concatenate([pb.pop(u - 1), pb.pop(u)], axis=1), _heads_block_diag(do_ref[tokens, :]))

        _na_pipeline(len(units), [scores, d_probs], softmax_bwd, [d_query, d_keys_values])

    t_tile = pl.BlockSpec((NA_PAIR, step_w), lambda h, s: (h, s))
    tile = pl.BlockSpec((step_w, NA_PAIR), lambda h, s: (s, h))
    t_full = pl.BlockSpec((NA_PAIR, L), lambda h, s: (h, 0))
    full = pl.BlockSpec((L, NA_PAIR), lambda h, s: (0, h))
    bt = pl.BlockSpec((2, NA_CASES, NA_WIN, NA_PAIR), lambda h, s: (h, 0, 0, 0))
    tok = jax.ShapeDtypeStruct((L, D_NA), F32)
    return pl.pallas_call(
        body, name="na_bwd", grid=(NA_HEADS // 2, L // step_w),
        in_specs=[t_tile, tile, t_full, full, full, bt, t_tile, t_tile, tile],
        out_specs=[t_tile, full, full, bt],
        out_shape=[jax.ShapeDtypeStruct((D_NA, L), F32), tok, tok, jax.ShapeDtypeStruct(bias_tab.shape, F32)],
        compiler_params=_cparams(("arbitrary", "arbitrary")),
    )(q_t, q, k_t, k, v, bias_tab, out_t, d_out_t, d_out)


def _branch_fwd_values(ys, zs, yn, zn, wglu, bglu):
    g1, t = _gelu_parts(ys)
    lin = _dot(g1.astype(BF16), wglu) + bglu
    sg = _sigmoid(lin)
    ys2 = g1 * sg
    sz, szs = _silu_parts(zs)
    sn, sns = _silu_parts(zn)
    return g1, t, sg, ys2, sz, szs, sn, sns


def _branch_fwd(y_ssm_c, z_s, y_na_t, z_n, w_glu, b_glu, tm=512):
    L = z_s.shape[0]

    def body(ys_ref, zs_ref, yn_ref, zn_ref, w_ref, b_ref, cat_ref, scr):
        yn = yn_ref[...].T
        g1, t, sg, ys2, sz, szs, sn, sns = _branch_fwd_values(
            _load_chunks(ys_ref, scr), zs_ref[...], yn, zn_ref[...], w_ref[...], b_ref[...])
        cat_ref[:, 0:512] = (ys2 * sz).astype(BF16)
        cat_ref[:, 512:1024] = (yn * sn).astype(BF16)

    tile = pl.BlockSpec((tm, 512), lambda i: (i, 0))
    return pl.pallas_call(
        body, name="branch_fwd", grid=(L // tm,),
        in_specs=[_chunk_spec(tm), tile, _heads_t_spec(tm), tile, pl.BlockSpec((512, 512), lambda i: (0, 0)),
                  pl.BlockSpec((1, 512), lambda i: (0, 0))],
        out_specs=pl.BlockSpec((tm, 1024), lambda i: (i, 0)),
        out_shape=jax.ShapeDtypeStruct((L, 1024), BF16),
        scratch_shapes=[_chunk_scratch(tm)],
        compiler_params=_cparams(("arbitrary",)),
    )(y_ssm_c, z_s, y_na_t, z_n, w_glu, b_glu)


def _branch_bwd(y_ssm_c, z_s, y_na_t, z_n, w_glu, b_glu, d_cat, tm=512):
    L = z_s.shape[0]

    def body(ys_ref, zs_ref, yn_ref, zn_ref, w_ref, b_ref, dc_ref,
             dys_ref, dzs_ref, dynt_ref, dyn_ref, dzn_ref, dw_ref, db_ref, scr):
        @pl.when(pl.program_id(0) == 0)
        def _():
            dw_ref[...] = jnp.zeros_like(dw_ref)
            db_ref[...] = jnp.zeros_like(db_ref)

        ys, zs, yn, zn = _load_chunks(ys_ref, scr), zs_ref[...], yn_ref[...].T, zn_ref[...]
        w = w_ref[...]
        g1, t, sg, ys2, sz, szs, sn, sns = _branch_fwd_values(ys, zs, yn, zn, w, b_ref[...])
        dys3 = dc_ref[:, 0:512]
        dyn2 = dc_ref[:, 512:1024]
        dzs_ref[...] = (dys3 * ys2 * _silu_grad(zs, szs)).astype(BF16)
        dys2 = dys3 * sz
        dlin = dys2 * g1 * sg * (1.0 - sg)
        dlb = dlin.astype(BF16)
        db_ref[...] += jnp.sum(dlin, axis=0, keepdims=True)
        dw_ref[...] += _dot_tn(g1.astype(BF16), dlb)
        dg1 = dys2 * sg + _dot_nt(dlb, w)
        _store_chunks(dg1 * _gelu_grad(ys, t), scr, dys_ref, BF16)
        dyn = dyn2 * sn
        dynt_ref[...] = dyn.T
        dyn_ref[...] = dyn.astype(BF16)
        dzn_ref[...] = (dyn2 * yn * _silu_grad(zn, sns)).astype(BF16)

    tile = pl.BlockSpec((tm, 512), lambda i: (i, 0))
    wspec = pl.BlockSpec((512, 512), lambda i: (0, 0))
    bspec = pl.BlockSpec((1, 512), lambda i: (0, 0))
    tok = jax.ShapeDtypeStruct((L, 512), BF16)
    return pl.pallas_call(
        body, name="branch_bwd", grid=(L // tm,),
        in_specs=[_chunk_spec(tm), tile, _heads_t_spec(tm), tile, wspec, bspec, pl.BlockSpec((tm, 1024), lambda i: (i, 0))],
        out_specs=[_chunk_spec(tm), tile, _heads_t_spec(tm), tile, tile, wspec, bspec],
        out_shape=[jax.ShapeDtypeStruct((N_BLOCKS, L // CHUNK, CHUNK_W), BF16), tok, jax.ShapeDtypeStruct((D_NA, L), F32),
                   tok, tok,
                   jax.ShapeDtypeStruct((512, 512), F32), jax.ShapeDtypeStruct((1, 512), F32)],
        scratch_shapes=[_chunk_scratch(tm)],
        compiler_params=_cparams(("arbitrary",)),
    )(y_ssm_c, z_s, y_na_t, z_n, w_glu, b_glu, d_cat)


def _head(x, p, target, cat, w_out, g_post, w_ple_g, g_ple, w_pg, tm=256):
    L = x.shape[0]
    pw = w_ple_g.shape[2]

    def body(x_ref, p_ref, t_ref, cat_ref, wo_ref, gpo_ref, wp_ref, gpl_ref, wg_ref,
             loss_ref, dh1_ref, dcat_ref, dwo_ref, dgpo_ref, dwp_ref, dgpl_ref, dwg_ref):
        @pl.when(pl.program_id(0) == 0)
        def _():
            for r in (loss_ref, dwo_ref, dgpo_ref, dwp_ref, dgpl_ref, dwg_ref):
                r[...] = jnp.zeros_like(r)

        cat_b = cat_ref[...]
        wo, wg = wo_ref[...], wg_ref[...]
        g_po, g_pl = gpo_ref[...], gpl_ref[...]
        mix = _dot(cat_b, wo)
        nm, r2 = _rms(mix)
        h1 = x_ref[...] + nm * g_po
        p_b = p_ref[...].astype(BF16)
        ep = jnp.concatenate([_dot(p_b, wp_ref[j]) for j in range(N_CHIPS)], axis=1)
        ne, r3 = _rms(ep)
        e = ne * g_pl
        h1_b = h1.astype(BF16)
        gate = _sigmoid(_dot(h1_b, wg))
        h2 = h1 + gate * e
        diff = h2 - t_ref[...]
        loss_ref[...] += (0.5 / D_MODEL) * jnp.sum(diff * diff).reshape(1, 1)

        dh2 = diff * (1.0 / D_MODEL)
        de = dh2 * gate
        dgl = (dh2 * e * gate * (1.0 - gate)).astype(BF16)
        dwg_ref[...] += _dot_tn(h1_b, dgl)
        dh1 = dh2 + _dot_nt(dgl, wg)
        dgpl_ref[...] += jnp.sum(de * ne, axis=0, keepdims=True)
        dep = _rms_bwd(de * g_pl, ne, r3).astype(BF16)
        for j in range(N_CHIPS):
            dwp_ref[j] += _dot_tn(p_b, dep[:, j * pw:(j + 1) * pw])
        dgpo_ref[...] += jnp.sum(dh1 * nm, axis=0, keepdims=True)
        dmix = _rms_bwd(dh1 * g_po, nm, r2).astype(BF16)
        dwo_ref[...] += _dot_tn(cat_b, dmix)
        dcat_ref[...] = _dot_nt(dmix, wo)
        dh1_ref[...] = dh1

    tile = lambda w: pl.BlockSpec((tm, w), lambda i: (i, 0))
    const = lambda *s: pl.BlockSpec(s, lambda i: (0,) * len(s))
    sds = jax.ShapeDtypeStruct
    return pl.pallas_call(
        body, name="head", grid=(L // tm,),
        in_specs=[tile(D_MODEL), tile(D_PLE), tile(D_MODEL), tile(1024), const(1024, D_MODEL), const(1, D_MODEL),
                  const(N_CHIPS, D_PLE, pw), const(1, D_MODEL), const(D_MODEL, D_MODEL)],
        out_specs=[const(1, 1), tile(D_MODEL), tile(1024), const(1024, D_MODEL), const(1, D_MODEL),
                   const(N_CHIPS, D_PLE, pw), const(1, D_MODEL), const(D_MODEL, D_MODEL)],
        out_shape=[sds((1, 1), F32), sds((L, D_MODEL), F32), sds((L, 1024), F32), sds((1024, D_MODEL), F32),
                   sds((1, D_MODEL), F32), sds((N_CHIPS, D_PLE, pw), F32), sds((1, D_MODEL), F32),
                   sds((D_MODEL, D_MODEL), F32)],
        compiler_params=_cparams(("arbitrary",)),
    )(x, p, target, cat, w_out, g_post, w_ple_g, g_ple, w_pg)


def _dproj_specs(tm):
    tile = pl.BlockSpec((tm, 512), lambda i: (i, 0))
    return [_chunk_spec(tm), tile, _heads_t_spec(tm), tile, tile, tile]


def _dproj_tile(refs, scr):
    du_ref, dzs_ref, dqt_ref, dk_ref, dv_ref, dzn_ref = refs
    parts = [_load_chunks(du_ref, scr), dzs_ref[...], dqt_ref[...].T, dk_ref[...], dv_ref[...], dzn_ref[...]]
    return jnp.concatenate([t.astype(BF16) for t in parts], axis=1)


def _in_proj_bwd_w(x, g_pre, dparts, tm=512):
    L = x.shape[0]
    wn = D_IN_PROJ // N_CHIPS

    def body(x_ref, g_ref, *refs):
        dw_ref, scr = refs[-2], refs[-1]

        @pl.when(pl.program_id(0) == 0)
        def _():
            dw_ref[...] = jnp.zeros_like(dw_ref)

        n, _ = _rms(x_ref[...])
        hn = (n * g_ref[...]).astype(BF16)
        dproj = _dproj_tile(refs[:-2], scr)
        for j in range(N_CHIPS):
            dw_ref[j] += _dot_tn(hn, dproj[:, j * wn:(j + 1) * wn])

    return pl.pallas_call(
        body, name="in_proj_bwd_w", grid=(L // tm,),
        in_specs=[pl.BlockSpec((tm, D_MODEL), lambda i: (i, 0)), pl.BlockSpec((1, D_MODEL), lambda i: (0, 0))] + _dproj_specs(tm),
        out_specs=pl.BlockSpec((N_CHIPS, D_MODEL, wn), lambda i: (0, 0, 0)),
        out_shape=jax.ShapeDtypeStruct((N_CHIPS, D_MODEL, wn), F32),
        scratch_shapes=[_chunk_scratch(tm)],
        compiler_params=_cparams(("arbitrary",)),
    )(x, g_pre, *dparts)


def _in_proj_bwd_x(x, g_pre, w_in_g, d_h1, dparts, tm=512):
    L = x.shape[0]
    wn = w_in_g.shape[2]

    def body(x_ref, g_ref, w_ref, dh1_ref, *refs):
        dx_ref, dg_ref, scr = refs[-3], refs[-2], refs[-1]

        @pl.when(pl.program_id(0) == 0)
        def _():
            dg_ref[...] = jnp.zeros_like(dg_ref)

        n, r = _rms(x_ref[...])
        dproj = _dproj_tile(refs[:-3], scr)
        dhn = _dot_nt(dproj[:, 0:wn], w_ref[0])
        for j in range(1, N_CHIPS):
            dhn = dhn + _dot_nt(dproj[:, j * wn:(j + 1) * wn], w_ref[j])
        dg_ref[...] += jnp.sum(dhn * n, axis=0, keepdims=True)
        dx_ref[...] = dh1_ref[...] + _rms_bwd(dhn * g_ref[...], n, r)

    wide = pl.BlockSpec((tm, D_MODEL), lambda i: (i, 0))
    vec = pl.BlockSpec((1, D_MODEL), lambda i: (0, 0))
    return pl.pallas_call(
        body, name="in_proj_bwd_x", grid=(L // tm,),
        in_specs=[wide, vec, pl.BlockSpec((N_CHIPS, D_MODEL, wn), lambda i: (0, 0, 0)), wide] + _dproj_specs(tm),
        out_specs=[wide, vec],
        out_shape=[jax.ShapeDtypeStruct((L, D_MODEL), F32), jax.ShapeDtypeStruct((1, D_MODEL), F32)],
        scratch_shapes=[_chunk_scratch(tm)],
        compiler_params=_cparams(("arbitrary",)),
    )(x, g_pre, w_in_g, d_h1, *dparts)


def _mesh_position():
    x, y, c = lax.axis_index("x"), lax.axis_index("y"), lax.axis_index("c")
    chips = [(1 - x, y), (x, 1 - y), (1 - x, 1 - y)]
    return x, y, c, chips


def _chip_index(cx, cy):
    return 2 * cx + cy


def _hbm_specs(n):
    return [pl.BlockSpec(memory_space=pl.ANY)] * n


def _gather_chips(shards, name):
    n = len(shards)

    def body(*refs):
        gather = _ChipGather(refs[:n], refs[n:2 * n], refs[2 * n:])
        gather.start()
        gather.forward()
        gather.finish()

    return pl.pallas_call(
        body, name=name, in_specs=_hbm_specs(n), out_specs=_hbm_specs(n),
        out_shape=_gather_out_shapes(shards), scratch_shapes=_gather_semaphores(n),
        compiler_params=pltpu.CompilerParams(has_side_effects=True),
    )(*shards)


def _gather_out_shapes(shards):
    return [jax.ShapeDtypeStruct((N_CHIPS,) + s.shape, s.dtype) for s in shards]


def _gather_semaphores(n):
    sem = pltpu.SemaphoreType.DMA
    return [sem((n, 3)), sem((n, 3)), sem((n, 3)), sem((n, 3)), sem((n,)), sem((n,))]


class _ChipGather:
    def __init__(self, ins, outs, sems):
        self.ins, self.outs = ins, outs
        self.send1, self.recv1, self.send2, self.recv2, self.send3, self.recv3 = sems
        self.x, self.y, self.c, self.chips = _mesh_position()
        self.me = _chip_index(self.x, self.y)
        self.sibling = (self.x, self.y, 1 - self.c)

    def _half(self, a, chip, core):
        hr = self.outs[a].shape[1] // 2
        return self.outs[a].at[chip, pl.ds(core * hr, hr)]

    def _own(self, a):
        return pltpu.make_async_remote_copy(
            src_ref=self.ins[a], dst_ref=self.outs[a].at[self.me], send_sem=self.send3.at[a], recv_sem=self.recv3.at[a],
            device_id=self.sibling, device_id_type=MESH)

    def _to_chip(self, a, j):
        hr = self.ins[a].shape[0] // 2
        return pltpu.make_async_remote_copy(
            src_ref=self.ins[a].at[pl.ds(self.c * hr, hr)], dst_ref=self._half(a, self.me, self.c),
            send_sem=self.send1.at[a, j], recv_sem=self.recv1.at[a, j], device_id=(*self.chips[j], self.c), device_id_type=MESH)

    def _from_chip(self, a, j):
        landed = self._half(a, _chip_index(*self.chips[j]), self.c)
        return pltpu.make_async_remote_copy(
            src_ref=landed, dst_ref=landed, send_sem=self.send1.at[a, j], recv_sem=self.recv1.at[a, j],
            device_id=(*self.chips[j], self.c), device_id_type=MESH)

    def _to_sibling(self, a, j, core):
        part = self._half(a, _chip_index(*self.chips[j]), core)
        return pltpu.make_async_remote_copy(
            src_ref=part, dst_ref=part, send_sem=self.send2.at[a, j], recv_sem=self.recv2.at[a, j],
            device_id=self.sibling, device_id_type=MESH)

    def _each(self):
        return [(a, j) for a in range(len(self.ins)) for j in range(3)]

    def start(self):
        for a in range(len(self.ins)):
            self._own(a).start()
        for a, j in self._each():
            self._to_chip(a, j).start()

    def forward(self):
        for a, j in self._each():
            self._from_chip(a, j).wait_recv()
            self._to_sibling(a, j, self.c).start()

    def finish(self):
        for a, j in self._each():
            self._to_sibling(a, j, 1 - self.c).wait_recv()
        for a, j in self._each():
            self._to_chip(a, j).wait_send()
            self._to_sibling(a, j, self.c).wait_send()
        for a in range(len(self.ins)):
            self._own(a).wait()


def _pair_exchange(grads):
    n = len(grads)

    def body(*refs):
        ins, outs = refs[:n], refs[n:2 * n]
        send, recv = refs[2 * n:]
        x, y, c, _ = _mesh_position()
        copies = []
        for a in range(n):
            hr = ins[a].shape[1] // 2
            cp = pltpu.make_async_remote_copy(
                src_ref=ins[a].at[:, pl.ds((1 - c) * hr, hr)], dst_ref=outs[a],
                send_sem=send.at[a], recv_sem=recv.at[a], device_id=(x, y, 1 - c), device_id_type=MESH)
            cp.start()
            copies.append(cp)
        for cp in copies:
            cp.wait()

    sem = pltpu.SemaphoreType.DMA
    return pl.pallas_call(
        body, name="pair_exchange", in_specs=_hbm_specs(n), out_specs=_hbm_specs(n),
        out_shape=[jax.ShapeDtypeStruct((g.shape[0], g.shape[1] // 2, g.shape[2]), g.dtype) for g in grads],
        scratch_shapes=[sem((n,)), sem((n,))],
        compiler_params=pltpu.CompilerParams(has_side_effects=True),
    )(*grads)


def _pair_add(core, grad, other, tr, out_dtype):
    hr = other.shape[1]
    cdim = other.shape[2]
    nb = hr // tr

    def body(core_ref, g_ref, o_ref, out_ref):
        out_ref[...] = (g_ref[...] + o_ref[...]).astype(out_dtype)

    return pl.pallas_call(
        body, name="pair_add",
        grid_spec=pltpu.PrefetchScalarGridSpec(
            num_scalar_prefetch=1, grid=(N_CHIPS, nb),
            in_specs=[pl.BlockSpec((1, tr, cdim), lambda j, i, core_ref: (j, core_ref[0] * nb + i, 0)),
                      pl.BlockSpec((1, tr, cdim), lambda j, i, core_ref: (j, i, 0))],
            out_specs=pl.BlockSpec((1, tr, cdim), lambda j, i, core_ref: (j, i, 0))),
        out_shape=jax.ShapeDtypeStruct(other.shape, out_dtype),
        compiler_params=_cparams(("arbitrary", "arbitrary")),
    )(core, grad, other)


def _chip_scatter(parts):
    n = len(parts)

    def body(*refs):
        ins, outs = refs[:n], refs[n:2 * n]
        send, recv, load_sem, store_sem = refs[2 * n:2 * n + 4]
        staged = refs[2 * n + 4:]
        x, y, c, chips = _mesh_position()
        me = _chip_index(x, y)
        copies, loads = [], []
        for a in range(n):
            ld = pltpu.make_async_copy(ins[a].at[me], staged[a], load_sem.at[a])
            ld.start()
            loads.append(ld)
            for j, chip in enumerate(chips):
                cp = pltpu.make_async_remote_copy(
                    src_ref=ins[a].at[_chip_index(*chip)], dst_ref=outs[a].at[me],
                    send_sem=send.at[a, j], recv_sem=recv.at[a, j], device_id=(*chip, c), device_id_type=MESH)
                cp.start()
                copies.append(cp)
        for a in range(n):
            loads[a].wait()
            st = pltpu.make_async_copy(staged[a], outs[a].at[me], store_sem.at[a])
            st.start()
            copies.append(st)
        for cp in copies:
            cp.wait()

    sem = pltpu.SemaphoreType.DMA
    return pl.pallas_call(
        body, name="chip_scatter", in_specs=_hbm_specs(n), out_specs=_hbm_specs(n),
        out_shape=[jax.ShapeDtypeStruct(p.shape, p.dtype) for p in parts],
        scratch_shapes=[sem((n, 3)), sem((n, 3)), sem((n,)), sem((n,))] + [pltpu.VMEM(p.shape[1:], p.dtype) for p in parts],
        compiler_params=pltpu.CompilerParams(has_side_effects=True),
    )(*parts)


def _chip_add(core, recv, tr):
    hr, cdim = recv.shape[1], recv.shape[2]
    nb = hr // tr

    def body(core_ref, r_ref, out_ref):
        out_ref[...] = ((r_ref[0].astype(F32) + r_ref[1].astype(F32)) + r_ref[2].astype(F32)) + r_ref[3].astype(F32)

    return pl.pallas_call(
        body, name="chip_add",
        grid_spec=pltpu.PrefetchScalarGridSpec(
            num_scalar_prefetch=1, grid=(nb,),
            in_specs=[pl.BlockSpec((N_CHIPS, tr, cdim), lambda i, core_ref: (0, i, 0))],
            out_specs=pl.BlockSpec((tr, cdim), lambda i, core_ref: (core_ref[0] * nb + i, 0))),
        out_shape=jax.ShapeDtypeStruct((2 * hr, cdim), F32),
        compiler_params=_cparams(("arbitrary",)),
    )(core, recv)


def _pair_gather(fulls):
    n = len(fulls)

    def body(*refs):
        outs = refs[n:2 * n]
        send, recv = refs[2 * n:]
        x, y, c, _ = _mesh_position()
        copies = []
        for a in range(n):
            hr = outs[a].shape[0] // 2
            mine = outs[a].at[pl.ds(c * hr, hr)]
            cp = pltpu.make_async_remote_copy(
                src_ref=mine, dst_ref=mine, send_sem=send.at[a], recv_sem=recv.at[a],
                device_id=(x, y, 1 - c), device_id_type=MESH)
            cp.start()
            copies.append(cp)
        for cp in copies:
            cp.wait()

    sem = pltpu.SemaphoreType.DMA
    return pl.pallas_call(
        body, name="pair_gather", in_specs=_hbm_specs(n), out_specs=_hbm_specs(n),
        out_shape=[jax.ShapeDtypeStruct(f.shape, f.dtype) for f in fulls],
        input_output_aliases={a: a for a in range(n)},
        scratch_shapes=[sem((n,)), sem((n,))],
        compiler_params=pltpu.CompilerParams(has_side_effects=True),
    )(*fulls)


def _row_tile(rows):
    for t in (512, 256, 128, 64, 32, 16, 8):
        if rows % t == 0:
            return t
    raise ValueError(rows)


def _reduce_scatter(grads, ici_dtypes):
    core = lax.axis_index("c").astype(jnp.int32).reshape(1)
    others = _pair_exchange(grads)
    pair = [_pair_add(core, g, o, _row_tile(o.shape[1]), dt) for g, o, dt in zip(grads, others, ici_dtypes)]
    landed = _chip_scatter(pair)
    return _pair_gather([_chip_add(core, r, _row_tile(r.shape[1])) for r in landed])


def _adamw(w, g, m, v):
    rows, cols = w.shape
    tr = _row_tile(rows) if rows % 8 == 0 else rows

    def body(w_ref, g_ref, m_ref, v_ref, d_ref, nm_ref, nv_ref):
        g_ = g_ref[...]
        m_ = ADAM_B1 * m_ref[...] + (1.0 - ADAM_B1) * g_
        v_ = ADAM_B2 * v_ref[...] + (1.0 - ADAM_B2) * (g_ * g_)
        m_hat = m_ / (1.0 - ADAM_B1 ** ADAM_STEP)
        v_hat = v_ / (1.0 - ADAM_B2 ** ADAM_STEP)
        d_ref[...] = -ADAM_LR * (m_hat / (jnp.sqrt(v_hat) + ADAM_EPS) + ADAM_WD * w_ref[...])
        nm_ref[...] = m_
        nv_ref[...] = v_

    spec = pl.BlockSpec((tr, cols), lambda i: (i, 0))
    shp = jax.ShapeDtypeStruct((rows, cols), F32)
    return pl.pallas_call(
        body, name="adamw", grid=(rows // tr,), in_specs=[spec] * 4, out_specs=[spec] * 3,
        out_shape=[shp] * 3, compiler_params=_cparams(("arbitrary",)),
    )(w, g, m, v)


_SMALL = ["norm_pre", "norm_post", "ssm_a_re", "ssm_a_im", "ssm_log_dt", "ssm_b_re", "ssm_b_im",
          "ssm_c_re", "ssm_c_im", "ssm_d", "b_glu", "na_rpb", "ple_norm"]
_BIG = ["w_in", "w_glu", "w_out", "w_ple", "w_ple_gate"]
_WEIGHTS = ["norm_pre", "norm_post", "w_in", "ssm_a_re", "ssm_a_im", "ssm_log_dt", "ssm_b_re", "ssm_b_im",
            "ssm_c_re", "ssm_c_im", "ssm_d", "w_glu", "b_glu", "na_rpb", "w_out", "w_ple", "ple_norm", "w_ple_gate"]
_SMALL_ROWS = 2176


def _pack_small(tensors, tail=None):
    parts = [tensors[n].reshape(-1) for n in _SMALL] + ([] if tail is None else [tail.reshape(-1)])
    flat = jnp.concatenate(parts)
    flat = jnp.pad(flat, (0, _SMALL_ROWS * 128 - flat.shape[0]))
    return flat.reshape(_SMALL_ROWS, 128)


def _unpack_small(packed, shapes):
    flat = packed.reshape(-1)
    out, off = {}, 0
    for n in _SMALL:
        size = int(np.prod(shapes[n]))
        out[n] = flat[off:off + size].reshape(shapes[n])
        off += size
    return out


def _local_grads(x, p, target, wts):
    ssm_names = ["ssm_a_re", "ssm_a_im", "ssm_log_dt", "ssm_b_re", "ssm_b_im", "ssm_c_re", "ssm_c_im", "ssm_d"]
    ssm_params = [wts[n][0] for n in ssm_names]
    blk, blk_vjp = jax.vjp(_ssm_block_params, *ssm_params)
    (m_mat, ws_mat, wot_mat, a16), gathered = _ssm_chunk_matrices(blk, [wts[n][0].astype(BF16) for n in _BIG])
    w_in_g, w_ple_g = gathered[0], gathered[3]
    w_glu, w_out, w_pg = gathered[1].reshape(512, 512), gathered[2].reshape(1024, 1024), gathered[4].reshape(1024, 1024)
    seq = x.shape[0]
    bias_rows, bias_rows_vjp = jax.vjp(_na_bias_rows, wts["na_rpb"][0])
    bias_tab = _na_bias_table(bias_rows, seq // GRID_W)

    u_c, z_s, q_t, q, k_t, k, v_t, v, z_n = _in_proj(x, wts["norm_pre"], w_in_g)
    s_in = _block_matmul([(u_c, ws_mat, False)], "ssm_chunk_states")
    s_prev = _ssm_state_scan(s_in, a16)
    y_ssm_c = _block_matmul([(u_c, m_mat, False), (s_prev, wot_mat, True)], "ssm_chunk_out")
    y_na_t = _na_fwd(q_t, k, v_t, bias_tab)
    cat = _branch_fwd(y_ssm_c, z_s, y_na_t, z_n, w_glu, wts["b_glu"])

    (loss, d_h1, d_cat, d_w_out, d_g_post, d_w_ple, d_g_ple, d_w_pg) = _head(
        x, p, target, cat, w_out, wts["norm_post"], w_ple_g, wts["ple_norm"], w_pg)
    dy_c, d_z_s, d_y_na_t, d_y_na, d_z_n, d_w_glu, d_b_glu = _branch_bwd(
        y_ssm_c, z_s, y_na_t, z_n, w_glu, wts["b_glu"], d_cat)
    d_q_t, d_k, d_v, d_bias_tab = _na_bwd(q_t, q, k_t, k, v, bias_tab, y_na_t, d_y_na_t, d_y_na)

    d_prev = _block_matmul([(dy_c, wot_mat, False)], "ssm_bwd_states")
    g_st, d_a16 = _ssm_state_scan_bwd(d_prev, s_prev, a16)
    d_u_c = _block_matmul([(dy_c, m_mat, True), (g_st, ws_mat, True)], "ssm_bwd_in", out_dtype=BF16)
    d_m = _block_matmul_tn(u_c, dy_c, "ssm_grad_m")
    d_ws = _block_matmul_tn(u_c, g_st, "ssm_grad_ws")
    d_wot = _block_matmul_tn(dy_c, s_prev, "ssm_grad_wot")
    d_ssm = blk_vjp(tuple(_ssm_chunk_matrices_bwd(blk, d_m, d_ws, d_wot, d_a16)))
    (d_rpb,) = bias_rows_vjp(_na_bias_table_bwd(d_bias_tab, seq // GRID_W))

    dparts = [d_u_c, d_z_s, d_q_t, d_k, d_v, d_z_n]
    d_w_in = _in_proj_bwd_w(x, wts["norm_pre"], dparts)
    grad_x, d_g_pre = _in_proj_bwd_x(x, wts["norm_pre"], w_in_g, d_h1, dparts)

    small = {"norm_pre": d_g_pre, "norm_post": d_g_post, "b_glu": d_b_glu, "na_rpb": d_rpb, "ple_norm": d_g_ple}
    for n, g in zip(ssm_names, d_ssm):
        small[n] = g
    big = {"w_in": d_w_in, "w_glu": d_w_glu.reshape(N_CHIPS, 128, 512), "w_out": d_w_out.reshape(N_CHIPS, 256, 1024),
           "w_ple": d_w_ple, "w_ple_gate": d_w_pg.reshape(N_CHIPS, 256, 1024)}
    return loss, grad_x, small, big


def kernel(x, p, norm_pre, norm_post, w_in, ssm_a_re, ssm_a_im, ssm_log_dt, ssm_b_re, ssm_b_im, ssm_c_re, ssm_c_im, ssm_d, w_glu, b_glu, na_rpb, w_out, w_ple, ple_norm, w_ple_gate, loss_target, m_norm_pre, m_norm_post, m_w_in, m_ssm_a_re, m_ssm_a_im, m_ssm_log_dt, m_ssm_b_re, m_ssm_b_im, m_ssm_c_re, m_ssm_c_im, m_ssm_d, m_w_glu, m_b_glu, m_na_rpb, m_w_out, m_w_ple, m_ple_norm, m_w_ple_gate, v_norm_pre, v_norm_post, v_w_in, v_ssm_a_re, v_ssm_a_im, v_ssm_log_dt, v_ssm_b_re, v_ssm_b_im, v_ssm_c_re, v_ssm_c_im, v_ssm_d, v_w_glu, v_b_glu, v_na_rpb, v_w_out, v_w_ple, v_ple_norm, v_w_ple_gate):
    wts = dict(norm_pre=norm_pre, norm_post=norm_post, w_in=w_in, ssm_a_re=ssm_a_re, ssm_a_im=ssm_a_im,
               ssm_log_dt=ssm_log_dt, ssm_b_re=ssm_b_re, ssm_b_im=ssm_b_im, ssm_c_re=ssm_c_re, ssm_c_im=ssm_c_im,
               ssm_d=ssm_d, w_glu=w_glu, b_glu=b_glu, na_rpb=na_rpb, w_out=w_out, w_ple=w_ple, ple_norm=ple_norm,
               w_ple_gate=w_ple_gate)
    mom_m = dict(norm_pre=m_norm_pre, norm_post=m_norm_post, w_in=m_w_in, ssm_a_re=m_ssm_a_re, ssm_a_im=m_ssm_a_im,
                 ssm_log_dt=m_ssm_log_dt, ssm_b_re=m_ssm_b_re, ssm_b_im=m_ssm_b_im, ssm_c_re=m_ssm_c_re,
                 ssm_c_im=m_ssm_c_im, ssm_d=m_ssm_d, w_glu=m_w_glu, b_glu=m_b_glu, na_rpb=m_na_rpb, w_out=m_w_out,
                 w_ple=m_w_ple, ple_norm=m_ple_norm, w_ple_gate=m_w_ple_gate)
    mom_v = dict(norm_pre=v_norm_pre, norm_post=v_norm_post, w_in=v_w_in, ssm_a_re=v_ssm_a_re, ssm_a_im=v_ssm_a_im,
                 ssm_log_dt=v_ssm_log_dt, ssm_b_re=v_ssm_b_re, ssm_b_im=v_ssm_b_im, ssm_c_re=v_ssm_c_re,
                 ssm_c_im=v_ssm_c_im, ssm_d=v_ssm_d, w_glu=v_w_glu, b_glu=v_b_glu, na_rpb=v_na_rpb, w_out=v_w_out,
                 w_ple=v_w_ple, ple_norm=v_ple_norm, w_ple_gate=v_w_ple_gate)

    loss_part, grad_x, small, big = _local_grads(x[0], p[0, 0], loss_target[0], wts)

    small_packed = _pack_small(small, tail=loss_part).reshape(N_CHIPS, _SMALL_ROWS // N_CHIPS, 128)
    reduced = _reduce_scatter([big[n] for n in _BIG] + [small_packed], [BF16] * len(_BIG) + [F32])
    grads = dict(zip(_BIG, reduced[:-1]))
    (small_all,) = _gather_chips([reduced[-1]], "gather_small_grads")
    small_all = small_all.reshape(_SMALL_ROWS, 128)
    loss = small_all.reshape(-1)[sum(int(np.prod(wts[n].shape)) for n in _SMALL)]

    delta, new_m, new_v = {}, {}, {}
    for n in _BIG:
        shp = wts[n].shape
        d_, m_, v_ = _adamw(wts[n][0], grads[n], mom_m[n][0], mom_v[n][0])
        grads[n] = grads[n].reshape(shp)
        delta[n], new_m[n], new_v[n] = d_.reshape(shp), m_.reshape(shp), v_.reshape(shp)
    shapes = {n: wts[n].shape for n in _SMALL}
    d_s, m_s, v_s = _adamw(_pack_small(wts), small_all, _pack_small(mom_m), _pack_small(mom_v))
    for dst, packed in ((grads, small_all), (delta, d_s), (new_m, m_s), (new_v, v_s)):
        dst.update(_unpack_small(packed, shapes))

    return (loss, grad_x[None], *[grads[n] for n in _WEIGHTS], *[delta[n] for n in _WEIGHTS],
            *[new_m[n] for n in _WEIGHTS], *[new_v[n] for n in _WEIGHTS])
```

```python
import functools
import math

import jax
import jax.numpy as jnp
import numpy as np
from jax import lax
from jax.experimental import pallas as pl
from jax.experimental.pallas import tpu as pltpu

F32 = jnp.float32
BF16 = jnp.bfloat16

D_MODEL = 1024
D_PLE = 256
GRID_W = 64
D_SSM = 512
SSM_GROUP = 16
N_GROUPS = 32
SSM_STATE = 64
D_NA = 512
NA_HEADS = 8
NA_HEAD_DIM = 64
NA_ROWS = 8
NA_COLS = 16
D_IN_PROJ = 3072
EPS = 1e-6

CHUNK = 16
GROUPS_PER_BLOCK = 8
N_BLOCKS = N_GROUPS // GROUPS_PER_BLOCK
BLOCK_CH = GROUPS_PER_BLOCK * SSM_GROUP
BLOCK_ST = GROUPS_PER_BLOCK * SSM_STATE
CHUNK_W = CHUNK * BLOCK_CH
STATE_W = 4 * BLOCK_ST

N_CHIPS = 4
MESH = pl.DeviceIdType.MESH

ADAM_LR = 0.001
ADAM_B1 = 0.9
ADAM_B2 = 0.999
ADAM_EPS = 1e-08
ADAM_WD = 0.01
ADAM_STEP = 10

VMEM_LIMIT = 52 * 1024 * 1024
HIGHEST = lax.Precision.HIGHEST


def _cparams(sem=None, **kw):
    if sem is not None:
        kw["dimension_semantics"] = sem
    return pltpu.CompilerParams(vmem_limit_bytes=VMEM_LIMIT, **kw)


def _resident(*shape):
    return pl.BlockSpec(shape, lambda *_: (0,) * len(shape), pipeline_mode=pl.Buffered(1))


def _dot(a, b, dims=((1,), (0,))):
    return lax.dot_general(a, b, (dims, ((), ())), preferred_element_type=F32)


def _dot_nt(a, b):
    return _dot(a, b, ((1,), (1,)))


def _dot_tn(a, b):
    return _dot(a, b, ((0,), (0,)))


def _sigmoid(x):
    return 1.0 / (1.0 + jnp.exp(-x))


_GELU_C = math.sqrt(2.0 / math.pi)


def _gelu_parts(x):
    inner = _GELU_C * (x + 0.044715 * (x * x * x))
    t = jnp.tanh(inner)
    return 0.5 * x * (1.0 + t), t


def _gelu_grad(x, t):
    return 0.5 * (1.0 + t) + 0.5 * x * (1.0 - t * t) * (_GELU_C * (1.0 + 3.0 * 0.044715 * x * x))


def _silu_parts(z):
    s = _sigmoid(z)
    return z * s, s


def _silu_grad(z, s):
    return s * (1.0 + z * (1.0 - s))


def _rms(x):
    r = lax.rsqrt(jnp.mean(x * x, axis=-1, keepdims=True) + EPS)
    return x * r, r


def _rms_bwd(dn, n, r):
    return r * (dn - n * jnp.mean(dn * n, axis=-1, keepdims=True))


def _chunk_scratch(tm):
    return pltpu.VMEM((N_BLOCKS, tm, BLOCK_CH), F32)


def _store_chunks(val, scr, c_ref, dtype):
    nc = scr.shape[1] // CHUNK
    for b in range(N_BLOCKS):
        scr[b] = val[:, b * BLOCK_CH:(b + 1) * BLOCK_CH]
        for j in range(CHUNK):
            c_ref[b, :, j * BLOCK_CH:(j + 1) * BLOCK_CH] = scr[b, pl.ds(j, nc, stride=CHUNK), :].astype(dtype)


def _load_chunks(c_ref, scr):
    nc = scr.shape[1] // CHUNK
    for b in range(N_BLOCKS):
        for j in range(CHUNK):
            scr[b, pl.ds(j, nc, stride=CHUNK), :] = c_ref[b, :, j * BLOCK_CH:(j + 1) * BLOCK_CH].astype(F32)
    return jnp.concatenate([scr[b] for b in range(N_BLOCKS)], axis=1)


def _chunk_spec(tm):
    return pl.BlockSpec((N_BLOCKS, tm // CHUNK, CHUNK_W), lambda i: (0, i, 0))


def _heads_t_spec(tm):
    return pl.BlockSpec((D_NA, tm), lambda i: (0, i))


def _in_proj(x, g_pre, w_in_g, tm=512):
    L = x.shape[0]
    wn = w_in_g.shape[2]

    def body(x_ref, g_ref, w_ref, uc_ref, zs_ref, qt_ref, q_ref, kt_ref, k_ref, vt_ref, v_ref, zn_ref, u_scr):
        n, _ = _rms(x_ref[...])
        hn = (n * g_ref[...]).astype(BF16)
        proj = jnp.concatenate([_dot(hn, w_ref[j]) for j in range(N_CHIPS)], axis=1)
        _store_chunks(proj[:, 0:512], u_scr, uc_ref, BF16)
        zs_ref[...] = proj[:, 512:1024]
        q = proj[:, 1024:1536] * (NA_HEAD_DIM ** -0.5)
        for val, t_ref, n_ref in ((q, qt_ref, q_ref), (proj[:, 1536:2048], kt_ref, k_ref), (proj[:, 2048:2560], vt_ref, v_ref)):
            t_ref[...] = val.T.astype(BF16)
            n_ref[...] = val.astype(BF16)
        zn_ref[...] = proj[:, 2560:3072]

    tok = jax.ShapeDtypeStruct((L, 512), F32)
    tr = jax.ShapeDtypeStruct((D_NA, L), BF16)
    hm = jax.ShapeDtypeStruct((L, D_NA), BF16)
    tspec = pl.BlockSpec((tm, 512), lambda i: (i, 0))
    return pl.pallas_call(
        body, name="in_proj", grid=(L // tm,),
        in_specs=[pl.BlockSpec((tm, D_MODEL), lambda i: (i, 0)),
                  _resident(1, D_MODEL), _resident(N_CHIPS, D_MODEL, wn)],
        out_specs=[_chunk_spec(tm), tspec] + [_heads_t_spec(tm), tspec] * 3 + [tspec],
        out_shape=[jax.ShapeDtypeStruct((N_BLOCKS, L // CHUNK, CHUNK_W), BF16), tok, tr, hm, tr, hm, tr, hm, tok],
        scratch_shapes=[_chunk_scratch(tm)],
        compiler_params=_cparams(("arbitrary",)),
    )(x, g_pre, w_in_g)


def _ssm_block_params(a_re, a_im, log_dt, b_re, b_im, c_re, c_im, d):
    eye_g = jnp.eye(GROUPS_PER_BLOCK, dtype=F32)[None, None, :, None, :, None]

    def lanes(t):
        return t.reshape(2, N_BLOCKS, 1, BLOCK_ST)

    def expand(t):
        return (t[:, :, :, :, None, :] * eye_g).reshape(2, N_BLOCKS, BLOCK_CH, BLOCK_ST)

    b_shape = (2, N_BLOCKS, GROUPS_PER_BLOCK, SSM_STATE, SSM_GROUP)
    c_shape = (2, N_BLOCKS, GROUPS_PER_BLOCK, SSM_GROUP, SSM_STATE)
    return (lanes(a_re), lanes(a_im), lanes(jnp.broadcast_to(log_dt[..., None], a_re.shape)),
            expand(b_re.reshape(b_shape).transpose(0, 1, 2, 4, 3)), expand(b_im.reshape(b_shape).transpose(0, 1, 2, 4, 3)),
            expand(c_re.reshape(c_shape)), expand(c_im.reshape(c_shape)), d.reshape(N_BLOCKS, 1, BLOCK_CH))


def _ssm_discretise(ar, ai, ldt):
    dt = jnp.exp(ldt)
    mag = jnp.exp(dt * ar)
    abr = mag * jnp.cos(dt * ai)
    abi = mag * jnp.sin(dt * ai)
    num_re = abr - 1.0
    num_im = abi
    denom = ar * ar + ai * ai
    coef_re = (num_re * ar + num_im * ai) / denom
    coef_im = (num_im * ar - num_re * ai) / denom
    return abr, abi, coef_re, coef_im


_POW_ROWS = 24


def _ssm_fill_powers(ar_ref, ai_ref, ldt_ref, br_ref, bi_ref, pw_ref, bbar_ref):
    for d in range(2):
        abr, abi, cfr, cfi = _ssm_discretise(ar_ref[d, 0], ai_ref[d, 0], ldt_ref[d, 0])
        bbar_ref[d, 0] = cfr * br_ref[d, 0] - cfi * bi_ref[d, 0]
        bbar_ref[d, 1] = cfr * bi_ref[d, 0] + cfi * br_ref[d, 0]
        pr, pi = jnp.ones_like(abr), jnp.zeros_like(abi)
        for t in range(CHUNK + 1):
            pw_ref[d, 0, t:t + 1, :] = pr
            pw_ref[d, 1, t:t + 1, :] = pi
            pr, pi = pr * abr - pi * abi, pr * abi + pi * abr


def _dot_rounded(a, b, dims=((1,), (0,))):
    return _dot(a.astype(BF16), b.astype(BF16), dims)


def _ssm_stack_inputs(d, pw_ref, bbar_ref, xs_ref):
    for t in range(CHUNK):
        pr, pi = pw_ref[d, 0, t:t + 1, :], pw_ref[d, 1, t:t + 1, :]
        xs_ref[0, t * BLOCK_CH:(t + 1) * BLOCK_CH, :] = bbar_ref[d, 0] * pr - bbar_ref[d, 1] * pi
        xs_ref[1, t * BLOCK_CH:(t + 1) * BLOCK_CH, :] = bbar_ref[d, 0] * pi + bbar_ref[d, 1] * pr


def _eye(n):
    return (lax.broadcasted_iota(jnp.int32, (n, n), 0) == lax.broadcasted_iota(jnp.int32, (n, n), 1)).astype(F32)


def _ssm_param_specs():
    vec = pl.BlockSpec((2, 1, 1, BLOCK_ST), lambda b, j: (0, b, 0, 0))
    mat = pl.BlockSpec((2, 1, BLOCK_CH, BLOCK_ST), lambda b, j: (0, b, 0, 0))
    return [vec, vec, vec, mat, mat, mat, mat, pl.BlockSpec((1, 1, BLOCK_CH), lambda b, j: (b, 0, 0))]


def _ssm_chunk_matrices(blk, shards):
    n = len(shards)

    def body(*refs):
        ar_ref, ai_ref, ldt_ref, br_ref, bi_ref, cr_ref, ci_ref, d_ref = refs[:8]
        m_ref, ws_ref, wot_ref, a16_ref = refs[8 + n:12 + n]
        pw_ref, bbar_ref, lag_ref, xs_ref = refs[12 + 2 * n:16 + 2 * n]
        gather = _ChipGather(refs[8:8 + n], refs[12 + n:12 + 2 * n], refs[16 + 2 * n:])
        b, j = pl.program_id(0), pl.program_id(1)
        pl.when((b == 0) & (j == 0))(gather.start)
        pl.when((b == N_BLOCKS - 1) & (j == 0))(gather.forward)
        pl.when((b == N_BLOCKS - 1) & (j == CHUNK - 1))(gather.finish)

        @pl.when(j == 0)
        def _():
            _ssm_fill_powers(ar_ref, ai_ref, ldt_ref, br_ref, bi_ref, pw_ref, bbar_ref)
            zero_lag = d_ref[0] * _eye(BLOCK_CH)
            for d in range(2):
                _ssm_stack_inputs(d, pw_ref, bbar_ref, xs_ref)
                taps = (_dot_rounded(xs_ref[0], cr_ref[d, 0], ((1,), (1,)))
                        - _dot_rounded(xs_ref[1], ci_ref[d, 0], ((1,), (1,))))
                zero_lag = zero_lag + taps[0:BLOCK_CH]
                for t in range(1, CHUNK):
                    lag_ref[CHUNK - 1 + t if d == 0 else CHUNK - 1 - t] = taps[t * BLOCK_CH:(t + 1) * BLOCK_CH]
            lag_ref[CHUNK - 1] = zero_lag
            a16_ref[0] = jnp.concatenate([pw_ref[d, ri, CHUNK:CHUNK + 1, :] for d in range(2) for ri in range(2)], axis=1)

        m_ref[0] = jnp.concatenate([lag_ref[jp - j + CHUNK - 1] for jp in range(CHUNK)], axis=1).astype(BF16)

        def power(d, t):
            return pw_ref[d, 0, pl.ds(t, 1), :], pw_ref[d, 1, pl.ds(t, 1), :]

        parts = []
        for d, t in ((0, CHUNK - 1 - j), (1, j)):
            pr, pi = power(d, t)
            parts += [bbar_ref[d, 0] * pr - bbar_ref[d, 1] * pi, bbar_ref[d, 0] * pi + bbar_ref[d, 1] * pr]
        ws_ref[0] = jnp.concatenate(parts, axis=1).astype(BF16)
        parts = []
        for d, t in ((0, j + 1), (1, CHUNK - j)):
            pr, pi = power(d, t)
            parts += [cr_ref[d, 0] * pr - ci_ref[d, 0] * pi, -cr_ref[d, 0] * pi - ci_ref[d, 0] * pr]
        wot_ref[0] = jnp.concatenate(parts, axis=1).astype(BF16)

    row = pl.BlockSpec((1, BLOCK_CH, CHUNK_W), lambda b, j: (b, j, 0))
    mat = jax.ShapeDtypeStruct((N_BLOCKS, CHUNK_W, CHUNK_W), BF16)
    outs = pl.pallas_call(
        body, name="ssm_chunk_matrices", grid=(N_BLOCKS, CHUNK),
        in_specs=_ssm_param_specs() + _hbm_specs(n),
        out_specs=[row, row, row, pl.BlockSpec((1, 1, STATE_W), lambda b, j: (b, 0, 0))] + _hbm_specs(n),
        out_shape=[mat, mat, mat, jax.ShapeDtypeStruct((N_BLOCKS, 1, STATE_W), F32)] + _gather_out_shapes(shards),
        scratch_shapes=[pltpu.VMEM((2, 2, _POW_ROWS, BLOCK_ST), F32), pltpu.VMEM((2, 2, BLOCK_CH, BLOCK_ST), F32),
                        pltpu.VMEM((2 * CHUNK, BLOCK_CH, BLOCK_CH), F32), pltpu.VMEM((2, CHUNK_W, BLOCK_ST), F32)]
        + _gather_semaphores(n),
        compiler_params=_cparams(("arbitrary", "arbitrary"), has_side_effects=True),
    )(*blk, *shards)
    return outs[:4], outs[4:]


def _ssm_chunk_matrices_bwd(blk, d_m, d_ws, d_wot, d_a16):
    def body(ar_ref, ai_ref, ldt_ref, br_ref, bi_ref, cr_ref, ci_ref, d_ref, dm_ref, dws_ref, dwot_ref, da16_ref,
             dar_ref, dai_ref, dldt_ref, dbr_ref, dbi_ref, dcr_ref, dci_ref, dd_ref,
             pw_ref, bbar_ref, dlag_ref, dbbar_ref, dc_ref, dpw_ref, xs_ref, dts_ref):
        j = pl.program_id(1)
        w = BLOCK_ST

        @pl.when(j == 0)
        def _():
            _ssm_fill_powers(ar_ref, ai_ref, ldt_ref, br_ref, bi_ref, pw_ref, bbar_ref)
            for r in (dlag_ref, dbbar_ref, dc_ref, dpw_ref):
                r[...] = jnp.zeros_like(r)

        def x_chain(d, t, dxr, dxi):
            pr, pi = pw_ref[d, 0, pl.ds(t, 1), :], pw_ref[d, 1, pl.ds(t, 1), :]
            bbr, bbi = bbar_ref[d, 0], bbar_ref[d, 1]
            dbbar_ref[d, 0] += dxr * pr + dxi * pi
            dbbar_ref[d, 1] += dxi * pr - dxr * pi
            dpw_ref[d, 0, pl.ds(t, 1), :] += jnp.sum(dxr * bbr + dxi * bbi, axis=0, keepdims=True)
            dpw_ref[d, 1, pl.ds(t, 1), :] += jnp.sum(dxi * bbr - dxr * bbi, axis=0, keepdims=True)

        def z_chain(d, t, dzr, dzi):
            pr, pi = pw_ref[d, 0, pl.ds(t, 1), :], pw_ref[d, 1, pl.ds(t, 1), :]
            c_r, c_i = cr_ref[d, 0], ci_ref[d, 0]
            dc_ref[d, 0] += dzr * pr - dzi * pi
            dc_ref[d, 1] += -dzr * pi - dzi * pr
            dpw_ref[d, 0, pl.ds(t, 1), :] += jnp.sum(dzr * c_r - dzi * c_i, axis=0, keepdims=True)
            dpw_ref[d, 1, pl.ds(t, 1), :] += jnp.sum(-dzr * c_i - dzi * c_r, axis=0, keepdims=True)

        for jp in range(CHUNK):
            dlag_ref[jp - j + CHUNK - 1] += dm_ref[0, :, jp * BLOCK_CH:(jp + 1) * BLOCK_CH].astype(F32)
        quarter = lambda ref, i: ref[0, :, i * w:(i + 1) * w].astype(F32)
        x_chain(0, CHUNK - 1 - j, quarter(dws_ref, 0), quarter(dws_ref, 1))
        x_chain(1, j, quarter(dws_ref, 2), quarter(dws_ref, 3))
        z_chain(0, j + 1, quarter(dwot_ref, 0), quarter(dwot_ref, 1))
        z_chain(1, CHUNK - j, quarter(dwot_ref, 2), quarter(dwot_ref, 3))

        @pl.when(j == CHUNK - 1)
        def _():
            for d in range(2):
                _ssm_stack_inputs(d, pw_ref, bbar_ref, xs_ref)
                for t in range(CHUNK):
                    dts_ref[t * BLOCK_CH:(t + 1) * BLOCK_CH, :] = dlag_ref[CHUNK - 1 + t if d == 0 else CHUNK - 1 - t]
                d_taps = dts_ref[...]
                dc_ref[d, 0] += _dot_rounded(d_taps, xs_ref[0], ((0,), (0,)))
                dc_ref[d, 1] -= _dot_rounded(d_taps, xs_ref[1], ((0,), (0,)))
                xs_ref[0] = _dot_rounded(d_taps, cr_ref[d, 0])
                xs_ref[1] = -_dot_rounded(d_taps, ci_ref[d, 0])
                for t in range(CHUNK):
                    rows = slice(t * BLOCK_CH, (t + 1) * BLOCK_CH)
                    x_chain(d, t, xs_ref[0, rows, :], xs_ref[1, rows, :])
            dd_ref[0] = jnp.sum(dlag_ref[CHUNK - 1] * _eye(BLOCK_CH), axis=0, keepdims=True)
            for d in range(2):
                (abr, abi, cfr, cfi), disc_vjp = jax.vjp(_ssm_discretise, ar_ref[d, 0], ai_ref[d, 0], ldt_ref[d, 0])
                dpr = dpw_ref[d, 0, CHUNK:CHUNK + 1, :] + da16_ref[0, :, 2 * d * w:(2 * d + 1) * w]
                dpi = dpw_ref[d, 1, CHUNK:CHUNK + 1, :] + da16_ref[0, :, (2 * d + 1) * w:(2 * d + 2) * w]
                dabr, dabi = jnp.zeros_like(abr), jnp.zeros_like(abi)
                for t in range(CHUNK, 0, -1):
                    qr, qi = pw_ref[d, 0, t - 1:t, :], pw_ref[d, 1, t - 1:t, :]
                    dabr = dabr + dpr * qr + dpi * qi
                    dabi = dabi + dpi * qr - dpr * qi
                    dpr, dpi = (dpr * abr + dpi * abi + dpw_ref[d, 0, t - 1:t, :],
                                dpi * abr - dpr * abi + dpw_ref[d, 1, t - 1:t, :])
                dbbr, dbbi = dbbar_ref[d, 0], dbbar_ref[d, 1]
                b_r, b_i = br_ref[d, 0], bi_ref[d, 0]
                dbr_ref[d, 0] = cfr * dbbr + cfi * dbbi
                dbi_ref[d, 0] = cfr * dbbi - cfi * dbbr
                dcfr = jnp.sum(b_r * dbbr + b_i * dbbi, axis=0, keepdims=True)
                dcfi = jnp.sum(b_r * dbbi - b_i * dbbr, axis=0, keepdims=True)
                dar_ref[d, 0], dai_ref[d, 0], dldt_ref[d, 0] = disc_vjp((dabr, dabi, dcfr, dcfi))
                dcr_ref[d, 0] = dc_ref[d, 0]
                dci_ref[d, 0] = dc_ref[d, 1]

    row = pl.BlockSpec((1, BLOCK_CH, CHUNK_W), lambda b, j: (b, j, 0))
    specs = _ssm_param_specs()
    acc = lambda *s: pltpu.VMEM(s, F32)
    return pl.pallas_call(
        body, name="ssm_chunk_matrices_bwd", grid=(N_BLOCKS, CHUNK),
        in_specs=specs + [row, row, row, pl.BlockSpec((1, 1, STATE_W), lambda b, j: (b, 0, 0))],
        out_specs=specs,
        out_shape=[jax.ShapeDtypeStruct(t.shape, F32) for t in blk],
        scratch_shapes=[acc(2, 2, _POW_ROWS, BLOCK_ST), acc(2, 2, BLOCK_CH, BLOCK_ST), acc(2 * CHUNK, BLOCK_CH, BLOCK_CH),
                        acc(2, 2, BLOCK_CH, BLOCK_ST), acc(2, 2, BLOCK_CH, BLOCK_ST), acc(2, 2, _POW_ROWS, BLOCK_ST),
                        acc(2, CHUNK_W, BLOCK_ST), acc(CHUNK_W, BLOCK_CH)],
        compiler_params=_cparams(("arbitrary", "arbitrary")),
    )(*blk, d_m, d_ws, d_wot, d_a16)


def _block_matmul(terms, name, out_dtype=F32, tn=1024):
    nc = terms[0][0].shape[1]
    n_out = terms[0][1].shape[1] if terms[0][2] else terms[0][1].shape[2]
    flags = [t[2] for t in terms]

    def body(*refs):
        out_ref = refs[-1]
        acc = None
        for t, transposed in enumerate(flags):
            a = refs[2 * t][0].astype(BF16)
            w = refs[2 * t + 1][0]
            part = _dot_nt(a, w) if transposed else _dot(a, w)
            acc = part if acc is None else acc + part
        out_ref[0] = acc.astype(out_dtype)

    in_specs, args = [], []
    for a, w, transposed in terms:
        k = a.shape[2]
        in_specs.append(pl.BlockSpec((1, nc, k), lambda b, n: (b, 0, 0)))
        if transposed:
            in_specs.append(pl.BlockSpec((1, tn, k), lambda b, n: (b, n, 0)))
        else:
            in_specs.append(pl.BlockSpec((1, k, tn), lambda b, n: (b, 0, n)))
        args += [a, w]
    return pl.pallas_call(
        body, name=name, grid=(N_BLOCKS, n_out // tn), in_specs=in_specs,
        out_specs=pl.BlockSpec((1, nc, tn), lambda b, n: (b, 0, n)),
        out_shape=jax.ShapeDtypeStruct((N_BLOCKS, nc, n_out), out_dtype),
        compiler_params=_cparams(("arbitrary", "arbitrary")),
    )(*args)


def _block_matmul_tn(a, b, name, tile=1024):
    nc, m = a.shape[1], a.shape[2]
    n = b.shape[2]

    def body(a_ref, b_ref, out_ref):
        out_ref[0] = _dot_tn(a_ref[0].astype(BF16), b_ref[0].astype(BF16)).astype(BF16)

    return pl.pallas_call(
        body, name=name, grid=(N_BLOCKS, m // tile, n // tile),
        in_specs=[pl.BlockSpec((1, nc, tile), lambda blk, i, j: (blk, 0, i)),
                  pl.BlockSpec((1, nc, tile), lambda blk, i, j: (blk, 0, j))],
        out_specs=pl.BlockSpec((1, tile, tile), lambda blk, i, j: (blk, i, j)),
        out_shape=jax.ShapeDtypeStruct((N_BLOCKS, m, n), BF16),
        compiler_params=_cparams(("arbitrary", "arbitrary", "arbitrary")),
    )(a, b)


def _cmul(ar, ai, xr, xi):
    return ar * xr - ai * xi, ar * xi + ai * xr


def _cmul_conj(ar, ai, xr, xi):
    return ar * xr + ai * xi, ar * xi - ai * xr


def _ssm_state_scan(s_in, a16):
    nc = s_in.shape[1]
    w = BLOCK_ST

    def body(sin_ref, a_ref, out_ref):
        a = a_ref[0]
        afr, afi, abr, abi = a[:, 0:w], a[:, w:2 * w], a[:, 2 * w:3 * w], a[:, 3 * w:4 * w]

        def step(c, carry):
            fr, fi, br, bi = carry
            cb = nc - 1 - c
            out_ref[0, pl.ds(c, 1), 0:w] = fr
            out_ref[0, pl.ds(c, 1), w:2 * w] = fi
            out_ref[0, pl.ds(cb, 1), 2 * w:3 * w] = br
            out_ref[0, pl.ds(cb, 1), 3 * w:4 * w] = bi
            nfr, nfi = _cmul(afr, afi, fr, fi)
            nbr, nbi = _cmul(abr, abi, br, bi)
            return (nfr + sin_ref[0, pl.ds(c, 1), 0:w], nfi + sin_ref[0, pl.ds(c, 1), w:2 * w],
                    nbr + sin_ref[0, pl.ds(cb, 1), 2 * w:3 * w], nbi + sin_ref[0, pl.ds(cb, 1), 3 * w:4 * w])

        z = jnp.zeros((1, w), F32)
        lax.fori_loop(0, nc, step, (z, z, z, z))

    spec = pl.BlockSpec((1, nc, STATE_W), lambda b: (b, 0, 0))
    return pl.pallas_call(
        body, name="ssm_state_scan", grid=(N_BLOCKS,),
        in_specs=[spec, pl.BlockSpec((1, 1, STATE_W), lambda b: (b, 0, 0))],
        out_specs=spec, out_shape=jax.ShapeDtypeStruct(s_in.shape, F32),
        compiler_params=_cparams(("arbitrary",)),
    )(s_in, a16)


def _ssm_state_scan_bwd(d_prev, s_prev, a16):
    nc = d_prev.shape[1]
    w = BLOCK_ST

    def body(dp_ref, sp_ref, a_ref, g_ref, da_ref):
        a = a_ref[0]
        afr, afi, abr, abi = a[:, 0:w], a[:, w:2 * w], a[:, 2 * w:3 * w], a[:, 3 * w:4 * w]

        def step(i, carry):
            gfr, gfi, gbr, gbi, dafr, dafi, dabr, dabi = carry
            cf = nc - 1 - i
            cb = i
            g_ref[0, pl.ds(cf, 1), 0:w] = gfr
            g_ref[0, pl.ds(cf, 1), w:2 * w] = gfi
            g_ref[0, pl.ds(cb, 1), 2 * w:3 * w] = gbr
            g_ref[0, pl.ds(cb, 1), 3 * w:4 * w] = gbi
            sfr, sfi = sp_ref[0, pl.ds(cf, 1), 0:w], sp_ref[0, pl.ds(cf, 1), w:2 * w]
            sbr, sbi = sp_ref[0, pl.ds(cb, 1), 2 * w:3 * w], sp_ref[0, pl.ds(cb, 1), 3 * w:4 * w]
            dafr = dafr + gfr * sfr + gfi * sfi
            dafi = dafi + gfi * sfr - gfr * sfi
            dabr = dabr + gbr * sbr + gbi * sbi
            dabi = dabi + gbi * sbr - gbr * sbi
            nfr, nfi = _cmul_conj(afr, afi, gfr, gfi)
            nbr, nbi = _cmul_conj(abr, abi, gbr, gbi)
            return (nfr + dp_ref[0, pl.ds(cf, 1), 0:w], nfi + dp_ref[0, pl.ds(cf, 1), w:2 * w],
                    nbr + dp_ref[0, pl.ds(cb, 1), 2 * w:3 * w], nbi + dp_ref[0, pl.ds(cb, 1), 3 * w:4 * w],
                    dafr, dafi, dabr, dabi)

        z = jnp.zeros((1, w), F32)
        res = lax.fori_loop(0, nc, step, (z,) * 8)
        da_ref[0] = jnp.concatenate(res[4:], axis=1)

    spec = pl.BlockSpec((1, nc, STATE_W), lambda b: (b, 0, 0))
    aspec = pl.BlockSpec((1, 1, STATE_W), lambda b: (b, 0, 0))
    return pl.pallas_call(
        body, name="ssm_state_scan_bwd", grid=(N_BLOCKS,),
        in_specs=[spec, spec, aspec], out_specs=[spec, aspec],
        out_shape=[jax.ShapeDtypeStruct(d_prev.shape, F32), jax.ShapeDtypeStruct((N_BLOCKS, 1, STATE_W), F32)],
        compiler_params=_cparams(("arbitrary",)),
    )(d_prev, s_prev, a16)


NA_PAIR = 2 * GRID_W
NA_WIN_ROWS = NA_ROWS + 2
NA_WIN = NA_WIN_ROWS * GRID_W
NA_PAIRS_PER_STEP = 8
NA_CASES = 5
NA_MASKED = -1e30


def _na_pair_window(m, rows):
    rs0 = jnp.clip(2 * m - NA_ROWS // 2, 0, rows - NA_ROWS)
    ws = jnp.minimum(rs0, rows - NA_WIN_ROWS)
    last = rows // 2 - 1
    case = jnp.where(m == 0, 0, jnp.where(m == 1, 1, jnp.where(m == last - 1, 3, jnp.where(m == last, 4, 2))))
    return ws, case


def _na_row_offsets(rows):
    last = rows // 2 - 1
    geom = []
    for m in (0, 1, 2, last - 1, last):
        ws = min(max(2 * m - NA_ROWS // 2, 0), rows - NA_ROWS, rows - NA_WIN_ROWS)
        per_case = []
        for i in range(NA_WIN_ROWS):
            pair = []
            for rr in range(2):
                r = 2 * m + rr
                rs = min(max(r - NA_ROWS // 2, 0), rows - NA_ROWS)
                pair.append(ws + i - r + NA_ROWS - 1 if rs <= ws + i < rs + NA_ROWS else None)
            per_case.append(pair)
        geom.append(per_case)
    return geom


def _na_col_select():
    qc = np.arange(NA_PAIR)[None, :] % GRID_W
    kc = np.arange(GRID_W)[:, None]
    dc = np.clip(kc - qc + NA_COLS - 1, 0, 2 * NA_COLS - 2)
    return jnp.asarray((np.arange(2 * NA_COLS - 1)[:, None, None] == dc[None]).astype(np.float32))


def _na_bias_rows(rpb):
    return jnp.einsum("hrd,dkl->hrkl", rpb, _na_col_select(), precision=HIGHEST)


def _na_col_window():
    qc = lax.broadcasted_iota(jnp.int32, (GRID_W, NA_PAIR), 1) % GRID_W
    kc = lax.broadcasted_iota(jnp.int32, (GRID_W, NA_PAIR), 0)
    cs = jnp.clip(qc - NA_COLS // 2, 0, GRID_W - NA_COLS)
    first_row = lax.broadcasted_iota(jnp.int32, (GRID_W, NA_PAIR), 1) < GRID_W
    return (kc >= cs) & (kc < cs + NA_COLS), first_row


def _na_bias_table(bias_rows, rows):
    geom = _na_row_offsets(rows)

    def body(br_ref, tab_ref):
        col_ok, first_row = _na_col_window()
        masked = jnp.full((GRID_W, NA_PAIR), NA_MASKED, F32)
        for case in range(NA_CASES):
            for i in range(NA_WIN_ROWS):
                d0, d1 = geom[case][i]
                t0 = masked if d0 is None else br_ref[0, d0]
                t1 = masked if d1 is None else br_ref[0, d1]
                tile = jnp.where(col_ok, jnp.where(first_row, t0, t1), NA_MASKED)
                tab_ref[0, case, i * GRID_W:(i + 1) * GRID_W, :] = tile

    return pl.pallas_call(
        body, name="na_bias_table", grid=(NA_HEADS,),
        in_specs=[pl.BlockSpec((1, 2 * NA_ROWS - 1, GRID_W, NA_PAIR), lambda h: (h, 0, 0, 0))],
        out_specs=pl.BlockSpec((1, NA_CASES, NA_WIN, NA_PAIR), lambda h: (h, 0, 0, 0)),
        out_shape=jax.ShapeDtypeStruct((NA_HEADS, NA_CASES, NA_WIN, NA_PAIR), F32),
        compiler_params=_cparams(("arbitrary",)),
    )(bias_rows)


def _na_bias_table_bwd(d_tab, rows):
    geom = _na_row_offsets(rows)

    def body(dt_ref, dbr_ref):
        col_ok, first_row = _na_col_window()
        acc = [None] * (2 * NA_ROWS - 1)
        for case in range(NA_CASES):
            for i in range(NA_WIN_ROWS):
                tile = jnp.where(col_ok, dt_ref[0, case, i * GRID_W:(i + 1) * GRID_W, :], 0.0)
                for rr, d in enumerate(geom[case][i]):
                    if d is not None:
                        part = jnp.where(first_row if rr == 0 else ~first_row, tile, 0.0)
                        acc[d] = part if acc[d] is None else acc[d] + part
        for d, a in enumerate(acc):
            dbr_ref[0, d] = jnp.zeros((GRID_W, NA_PAIR), F32) if a is None else a

    return pl.pallas_call(
        body, name="na_bias_table_bwd", grid=(NA_HEADS,),
        in_specs=[pl.BlockSpec((1, NA_CASES, NA_WIN, NA_PAIR), lambda h: (h, 0, 0, 0))],
        out_specs=pl.BlockSpec((1, 2 * NA_ROWS - 1, GRID_W, NA_PAIR), lambda h: (h, 0, 0, 0)),
        out_shape=jax.ShapeDtypeStruct((NA_HEADS, 2 * NA_ROWS - 1, GRID_W, NA_PAIR), F32),
        compiler_params=_cparams(("arbitrary",)),
    )(d_tab)


NA_BLK = 64


def _na_blocks():
    return [slice(i * NA_BLK, (i + 1) * NA_BLK) for i in range(NA_WIN // NA_BLK)]


def _na_softmax(qk, bias_ref, hh, case):
    m = jnp.full((NA_BLK, NA_PAIR), -jnp.inf, F32)
    scores = []
    for blk in _na_blocks():
        s = qk[blk, :] + bias_ref[hh, case, blk, :]
        scores.append(s)
        m = jnp.maximum(m, s)
    m = jnp.max(m, axis=0, keepdims=True)
    l = jnp.zeros((NA_BLK, NA_PAIR), F32)
    exps = []
    for s in scores:
        e = jnp.exp(s - m)
        exps.append(e)
        l = l + e
    return exps, jnp.sum(l, axis=0, keepdims=True)


def _na_units(step, rows):
    units = []
    for pp in range(NA_PAIRS_PER_STEP):
        ws, case = _na_pair_window(step * NA_PAIRS_PER_STEP + pp, rows)
        win = pl.ds(pl.multiple_of(ws * GRID_W, NA_PAIR), NA_WIN)
        lanes = slice(pp * NA_PAIR, (pp + 1) * NA_PAIR)
        for hh in range(2):
            units.append((pp, hh, case, win, lanes, slice(hh * NA_HEAD_DIM, (hh + 1) * NA_HEAD_DIM)))
    return units


def _na_pipeline(n, before, middle, after, lookahead=2):
    for u in range(min(lookahead, n)):
        for f in before:
            f(u)
    for u in range(n):
        middle(u)
        if u + lookahead < n:
            for f in before:
                f(u + lookahead)
        for f in after:
            f(u)


def _head_rows(t, hh):
    row_head = lax.broadcasted_iota(jnp.int32, t.shape, 0) // NA_HEAD_DIM
    return jnp.where(row_head == hh, t, jnp.zeros_like(t))


def _heads_block_diag(t):
    lane_head = lax.broadcasted_iota(jnp.int32, t.shape, 1) // NA_HEAD_DIM
    zero = jnp.zeros_like(t)
    return jnp.concatenate([jnp.where(lane_head == 0, t, zero), jnp.where(lane_head == 1, t, zero)], axis=0)


def _na_fwd(q_t, k, v_t, bias_tab):
    L = k.shape[0]
    rows = L // GRID_W
    step_w = NA_PAIRS_PER_STEP * NA_PAIR

    def body(q_ref, k_ref, v_ref, bt_ref, o_ref):
        units = _na_units(pl.program_id(1), rows)
        qk, probs = {}, {}

        def scores(u):
            _, hh, _, win, lanes, _ = units[u]
            qk[u] = _dot(k_ref[win, :], _head_rows(q_ref[:, lanes], hh))

        def softmax(u):
            _, hh, case, _, _, _ = units[u]
            exps, l = _na_softmax(qk.pop(u), bt_ref, hh, case)
            probs[u] = jnp.concatenate([t.astype(BF16) for t in exps], axis=0), l

        def output(u):
            _, _, _, win, lanes, hrows = units[u]
            e, l = probs.pop(u)
            o_ref[hrows, lanes] = _dot(v_ref[hrows, win], e) / l

        _na_pipeline(len(units), [scores], softmax, [output])

    q_spec = pl.BlockSpec((NA_PAIR, step_w), lambda h, s: (h, s))
    return pl.pallas_call(
        body, name="na_fwd", grid=(NA_HEADS // 2, L // step_w),
        in_specs=[q_spec, pl.BlockSpec((L, NA_PAIR), lambda h, s: (0, h)),
                  pl.BlockSpec((NA_PAIR, L), lambda h, s: (h, 0)),
                  pl.BlockSpec((2, NA_CASES, NA_WIN, NA_PAIR), lambda h, s: (h, 0, 0, 0))],
        out_specs=q_spec,
        out_shape=jax.ShapeDtypeStruct((D_NA, L), F32),
        compiler_params=_cparams(("arbitrary", "arbitrary")),
    )(q_t, k, v_t, bias_tab)


def _na_bwd(q_t, q, k_t, k, v, bias_tab, out_t, d_out_t, d_out):
    L = k.shape[0]
    rows = L // GRID_W
    step_w = NA_PAIRS_PER_STEP * NA_PAIR

    def body(qt_ref, q_ref, kt_ref, k_ref, v_ref, bt_ref, ot_ref, dot_ref, do_ref, dq_ref, dk_ref, dv_ref, dbt_ref):
        @pl.when(pl.program_id(1) == 0)
        def _():
            dk_ref[...] = jnp.zeros_like(dk_ref)
            dv_ref[...] = jnp.zeros_like(dv_ref)
            dbt_ref[...] = jnp.zeros_like(dbt_ref)

        units = _na_units(pl.program_id(1), rows)
        qk, dp, dsb, pb = {}, {}, {}, {}

        def scores(u):
            _, hh, _, win, lanes, _ = units[u]
            qk[u] = _dot(k_ref[win, :], _head_rows(qt_ref[:, lanes], hh))

        def d_probs(u):
            _, hh, _, win, lanes, _ = units[u]
            dp[u] = _dot(v_ref[win, :], _head_rows(dot_ref[:, lanes].astype(BF16), hh))

        def softmax_bwd(u):
            _, hh, case, _, lanes, hrows = units[u]
            exps, l = _na_softmax(qk.pop(u), bt_ref, hh, case)
            inv_l = 1.0 / l
            delta = jnp.sum(dot_ref[hrows, lanes] * ot_ref[hrows, lanes], axis=0, keepdims=True)
            d_p = dp.pop(u)
            ds_blocks, p_blocks = [], []
            for blk, e in zip(_na_blocks(), exps):
                p = e * inv_l
                ds = p * (d_p[blk, :] - delta)
                dbt_ref[hh, case, blk, :] += ds
                ds_blocks.append(ds.astype(BF16))
                p_blocks.append(p.astype(BF16))
            dsb[u] = jnp.concatenate(ds_blocks, axis=0)
            pb[u] = jnp.concatenate(p_blocks, axis=0)

        def d_query(u):
            _, _, _, win, lanes, hrows = units[u]
            dq_ref[hrows, lanes] = _dot(kt_ref[hrows, win], dsb[u]) * (NA_HEAD_DIM ** -0.5)

        def d_keys_values(u):
            pp, hh, _, win, _, _ = units[u]
            if hh == 1:
                tokens = slice(pp * NA_PAIR, (pp + 1) * NA_PAIR)
                dk_ref[win, :] += _dot(jnp.concatenate([dsb.pop(u - 1), dsb.pop(u)], axis=1), _heads_block_diag(q_ref[tokens, :]))
                dv_ref[win, :] += _dot(jnp.concatenate([pb.pop(u - 1), pb.pop(u)], axis=1), _heads_block_diag(do_ref[tokens, :]))

        _na_pipeline(len(units), [scores, d_probs], softmax_bwd, [d_query, d_keys_values])

    t_tile = pl.BlockSpec((NA_PAIR, step_w), lambda h, s: (h, s))
    tile = pl.BlockSpec((step_w, NA_PAIR), lambda h, s: (s, h))
    t_full = pl.BlockSpec((NA_PAIR, L), lambda h, s: (h, 0))
    full = pl.BlockSpec((L, NA_PAIR), lambda h, s: (0, h))
    bt = pl.BlockSpec((2, NA_CASES, NA_WIN, NA_PAIR), lambda h, s: (h, 0, 0, 0))
    tok = jax.ShapeDtypeStruct((L, D_NA), F32)
    return pl.pallas_call(
        body, name="na_bwd", grid=(NA_HEADS // 2, L // step_w),
        in_specs=[t_tile, tile, t_full, full, full, bt, t_tile, t_tile, tile],
        out_specs=[t_tile, full, full, bt],
        out_shape=[jax.ShapeDtypeStruct((D_NA, L), F32), tok, tok, jax.ShapeDtypeStruct(bias_tab.shape, F32)],
        compiler_params=_cparams(("arbitrary", "arbitrary")),
    )(q_t, q, k_t, k, v, bias_tab, out_t, d_out_t, d_out)


def _branch_fwd_values(ys, zs, yn, zn, wglu, bglu):
    g1, t = _gelu_parts(ys)
    lin = _dot(g1.astype(BF16), wglu) + bglu
    sg = _sigmoid(lin)
    ys2 = g1 * sg
    sz, szs = _silu_parts(zs)
    sn, sns = _silu_parts(zn)
    return g1, t, sg, ys2, sz, szs, sn, sns


def _branch_fwd(y_ssm_c, z_s, y_na_t, z_n, w_glu, b_glu, tm=512):
    L = z_s.shape[0]

    def body(ys_ref, zs_ref, yn_ref, zn_ref, w_ref, b_ref, cat_ref, scr):
        yn = yn_ref[...].T
        g1, t, sg, ys2, sz, szs, sn, sns = _branch_fwd_values(
            _load_chunks(ys_ref, scr), zs_ref[...], yn, zn_ref[...], w_ref[...], b_ref[...])
        cat_ref[:, 0:512] = (ys2 * sz).astype(BF16)
        cat_ref[:, 512:1024] = (yn * sn).astype(BF16)

    tile = pl.BlockSpec((tm, 512), lambda i: (i, 0))
    return pl.pallas_call(
        body, name="branch_fwd", grid=(L // tm,),
        in_specs=[_chunk_spec(tm), tile, _heads_t_spec(tm), tile, pl.BlockSpec((512, 512), lambda i: (0, 0)),
                  pl.BlockSpec((1, 512), lambda i: (0, 0))],
        out_specs=pl.BlockSpec((tm, 1024), lambda i: (i, 0)),
        out_shape=jax.ShapeDtypeStruct((L, 1024), BF16),
        scratch_shapes=[_chunk_scratch(tm)],
        compiler_params=_cparams(("arbitrary",)),
    )(y_ssm_c, z_s, y_na_t, z_n, w_glu, b_glu)


def _branch_bwd(y_ssm_c, z_s, y_na_t, z_n, w_glu, b_glu, d_cat, tm=512):
    L = z_s.shape[0]

    def body(ys_ref, zs_ref, yn_ref, zn_ref, w_ref, b_ref, dc_ref,
             dys_ref, dzs_ref, dynt_ref, dyn_ref, dzn_ref, dw_ref, db_ref, scr):
        @pl.when(pl.program_id(0) == 0)
        def _():
            dw_ref[...] = jnp.zeros_like(dw_ref)
            db_ref[...] = jnp.zeros_like(db_ref)

        ys, zs, yn, zn = _load_chunks(ys_ref, scr), zs_ref[...], yn_ref[...].T, zn_ref[...]
        w = w_ref[...]
        g1, t, sg, ys2, sz, szs, sn, sns = _branch_fwd_values(ys, zs, yn, zn, w, b_ref[...])
        dys3 = dc_ref[:, 0:512]
        dyn2 = dc_ref[:, 512:1024]
        dzs_ref[...] = (dys3 * ys2 * _silu_grad(zs, szs)).astype(BF16)
        dys2 = dys3 * sz
        dlin = dys2 * g1 * sg * (1.0 - sg)
        dlb = dlin.astype(BF16)
        db_ref[...] += jnp.sum(dlin, axis=0, keepdims=True)
        dw_ref[...] += _dot_tn(g1.astype(BF16), dlb)
        dg1 = dys2 * sg + _dot_nt(dlb, w)
        _store_chunks(dg1 * _gelu_grad(ys, t), scr, dys_ref, BF16)
        dyn = dyn2 * sn
        dynt_ref[...] = dyn.T
        dyn_ref[...] = dyn.astype(BF16)
        dzn_ref[...] = (dyn2 * yn * _silu_grad(zn, sns)).astype(BF16)

    tile = pl.BlockSpec((tm, 512), lambda i: (i, 0))
    wspec = pl.BlockSpec((512, 512), lambda i: (0, 0))
    bspec = pl.BlockSpec((1, 512), lambda i: (0, 0))
    tok = jax.ShapeDtypeStruct((L, 512), BF16)
    return pl.pallas_call(
        body, name="branch_bwd", grid=(L // tm,),
        in_specs=[_chunk_spec(tm), tile, _heads_t_spec(tm), tile, wspec, bspec, pl.BlockSpec((tm, 1024), lambda i: (i, 0))],
        out_specs=[_chunk_spec(tm), tile, _heads_t_spec(tm), tile, tile, wspec, bspec],
        out_shape=[jax.ShapeDtypeStruct((N_BLOCKS, L // CHUNK, CHUNK_W), BF16), tok, jax.ShapeDtypeStruct((D_NA, L), F32),
                   tok, tok,
                   jax.ShapeDtypeStruct((512, 512), F32), jax.ShapeDtypeStruct((1, 512), F32)],
        scratch_shapes=[_chunk_scratch(tm)],
        compiler_params=_cparams(("arbitrary",)),
    )(y_ssm_c, z_s, y_na_t, z_n, w_glu, b_glu, d_cat)


def _head(x, p, target, cat, w_out, g_post, w_ple_g, g_ple, w_pg, tm=512):
    L = x.shape[0]
    pw = w_ple_g.shape[2]

    def body(x_ref, p_ref, t_ref, cat_ref, wo_ref, gpo_ref, wp_ref, gpl_ref, wg_ref,
             loss_ref, dh1_ref, dcat_ref, dwo_ref, dgpo_ref, dwp_ref, dgpl_ref, dwg_ref):
        @pl.when(pl.program_id(0) == 0)
        def _():
            for r in (loss_ref, dwo_ref, dgpo_ref, dwp_ref, dgpl_ref, dwg_ref):
                r[...] = jnp.zeros_like(r)

        cat_b = cat_ref[...]
        wo, wg = wo_ref[...], wg_ref[...]
        g_po, g_pl = gpo_ref[...], gpl_ref[...]
        mix = _dot(cat_b, wo)
        nm, r2 = _rms(mix)
        h1 = x_ref[...] + nm * g_po
        p_b = p_ref[...].astype(BF16)
        ep = jnp.concatenate([_dot(p_b, wp_ref[j]) for j in range(N_CHIPS)], axis=1)
        ne, r3 = _rms(ep)
        e = ne * g_pl
        h1_b = h1.astype(BF16)
        gate = _sigmoid(_dot(h1_b, wg))
        h2 = h1 + gate * e
        diff = h2 - t_ref[...]
        loss_ref[...] += (0.5 / D_MODEL) * jnp.sum(diff * diff).reshape(1, 1)

        dh2 = diff * (1.0 / D_MODEL)
        de = dh2 * gate
        dgl = (dh2 * e * gate * (1.0 - gate)).astype(BF16)
        dwg_ref[...] += _dot_tn(h1_b, dgl)
        dh1 = dh2 + _dot_nt(dgl, wg)
        dgpl_ref[...] += jnp.sum(de * ne, axis=0, keepdims=True)
        dep = _rms_bwd(de * g_pl, ne, r3).astype(BF16)
        for j in range(N_CHIPS):
            dwp_ref[j] += _dot_tn(p_b, dep[:, j * pw:(j + 1) * pw])
        dgpo_ref[...] += jnp.sum(dh1 * nm, axis=0, keepdims=True)
        dmix = _rms_bwd(dh1 * g_po, nm, r2).astype(BF16)
        dwo_ref[...] += _dot_tn(cat_b, dmix)
        dcat_ref[...] = _dot_nt(dmix, wo)
        dh1_ref[...] = dh1

    tile = lambda w: pl.BlockSpec((tm, w), lambda i: (i, 0))
    const = _resident
    sds = jax.ShapeDtypeStruct
    return pl.pallas_call(
        body, name="head", grid=(L // tm,),
        in_specs=[tile(D_MODEL), tile(D_PLE), tile(D_MODEL), tile(1024), const(1024, D_MODEL), const(1, D_MODEL),
                  const(N_CHIPS, D_PLE, pw), const(1, D_MODEL), const(D_MODEL, D_MODEL)],
        out_specs=[const(1, 1), tile(D_MODEL), tile(1024), const(1024, D_MODEL), const(1, D_MODEL),
                   const(N_CHIPS, D_PLE, pw), const(1, D_MODEL), const(D_MODEL, D_MODEL)],
        out_shape=[sds((1, 1), F32), sds((L, D_MODEL), F32), sds((L, 1024), F32), sds((1024, D_MODEL), F32),
                   sds((1, D_MODEL), F32), sds((N_CHIPS, D_PLE, pw), F32), sds((1, D_MODEL), F32),
                   sds((D_MODEL, D_MODEL), F32)],
        compiler_params=_cparams(("arbitrary",)),
    )(x, p, target, cat, w_out, g_post, w_ple_g, g_ple, w_pg)


def _dproj_specs(tm):
    tile = pl.BlockSpec((tm, 512), lambda i: (i, 0))
    return [_chunk_spec(tm), tile, _heads_t_spec(tm), tile, tile, tile]


def _dproj_tile(refs, scr):
    du_ref, dzs_ref, dqt_ref, dk_ref, dv_ref, dzn_ref = refs
    parts = [_load_chunks(du_ref, scr), dzs_ref[...], dqt_ref[...].T, dk_ref[...], dv_ref[...], dzn_ref[...]]
    return jnp.concatenate([t.astype(BF16) for t in parts], axis=1)


def _in_proj_bwd_w(x, g_pre, dparts, tm=512):
    L = x.shape[0]
    wn = D_IN_PROJ // N_CHIPS

    def body(x_ref, g_ref, *refs):
        dw_ref, scr = refs[-2], refs[-1]

        @pl.when(pl.program_id(0) == 0)
        def _():
            dw_ref[...] = jnp.zeros_like(dw_ref)

        n, _ = _rms(x_ref[...])
        hn = (n * g_ref[...]).astype(BF16)
        dproj = _dproj_tile(refs[:-2], scr)
        for j in range(N_CHIPS):
            dw_ref[j] += _dot_tn(hn, dproj[:, j * wn:(j + 1) * wn])

    return pl.pallas_call(
        body, name="in_proj_bwd_w", grid=(L // tm,),
        in_specs=[pl.BlockSpec((tm, D_MODEL), lambda i: (i, 0)), _resident(1, D_MODEL)] + _dproj_specs(tm),
        out_specs=_resident(N_CHIPS, D_MODEL, wn),
        out_shape=jax.ShapeDtypeStruct((N_CHIPS, D_MODEL, wn), F32),
        scratch_shapes=[_chunk_scratch(tm)],
        compiler_params=_cparams(("arbitrary",)),
    )(x, g_pre, *dparts)


def _in_proj_bwd_x(x, g_pre, w_in_g, d_h1, dparts, tm=512):
    L = x.shape[0]
    wn = w_in_g.shape[2]

    def body(x_ref, g_ref, w_ref, dh1_ref, *refs):
        dx_ref, dg_ref, scr = refs[-3], refs[-2], refs[-1]

        @pl.when(pl.program_id(0) == 0)
        def _():
            dg_ref[...] = jnp.zeros_like(dg_ref)

        n, r = _rms(x_ref[...])
        dproj = _dproj_tile(refs[:-3], scr)
        dhn = _dot_nt(dproj[:, 0:wn], w_ref[0])
        for j in range(1, N_CHIPS):
            dhn = dhn + _dot_nt(dproj[:, j * wn:(j + 1) * wn], w_ref[j])
        dg_ref[...] += jnp.sum(dhn * n, axis=0, keepdims=True)
        dx_ref[...] = dh1_ref[...] + _rms_bwd(dhn * g_ref[...], n, r)

    wide = pl.BlockSpec((tm, D_MODEL), lambda i: (i, 0))
    vec = _resident(1, D_MODEL)
    return pl.pallas_call(
        body, name="in_proj_bwd_x", grid=(L // tm,),
        in_specs=[wide, vec, _resident(N_CHIPS, D_MODEL, wn), wide] + _dproj_specs(tm),
        out_specs=[wide, vec],
        out_shape=[jax.ShapeDtypeStruct((L, D_MODEL), F32), jax.ShapeDtypeStruct((1, D_MODEL), F32)],
        scratch_shapes=[_chunk_scratch(tm)],
        compiler_params=_cparams(("arbitrary",)),
    )(x, g_pre, w_in_g, d_h1, *dparts)


def _mesh_position():
    x, y, c = lax.axis_index("x"), lax.axis_index("y"), lax.axis_index("c")
    chips = [(1 - x, y), (x, 1 - y), (1 - x, 1 - y)]
    return x, y, c, chips


def _chip_index(cx, cy):
    return 2 * cx + cy


def _hbm_specs(n):
    return [pl.BlockSpec(memory_space=pl.ANY)] * n


def _gather_chips(shards, name):
    n = len(shards)

    def body(*refs):
        gather = _ChipGather(refs[:n], refs[n:2 * n], refs[2 * n:])
        gather.start()
        gather.forward()
        gather.finish()

    return pl.pallas_call(
        body, name=name, in_specs=_hbm_specs(n), out_specs=_hbm_specs(n),
        out_shape=_gather_out_shapes(shards), scratch_shapes=_gather_semaphores(n),
        compiler_params=pltpu.CompilerParams(has_side_effects=True),
    )(*shards)


def _gather_out_shapes(shards):
    return [jax.ShapeDtypeStruct((N_CHIPS,) + s.shape, s.dtype) for s in shards]


def _gather_semaphores(n):
    sem = pltpu.SemaphoreType.DMA
    return [sem((n, 3)), sem((n, 3)), sem((n, 3)), sem((n, 3)), sem((n,)), sem((n,))]


class _ChipGather:
    def __init__(self, ins, outs, sems):
        self.ins, self.outs = ins, outs
        self.send1, self.recv1, self.send2, self.recv2, self.send3, self.recv3 = sems
        self.x, self.y, self.c, self.chips = _mesh_position()
        self.me = _chip_index(self.x, self.y)
        self.sibling = (self.x, self.y, 1 - self.c)

    def _half(self, a, chip, core):
        hr = self.outs[a].shape[1] // 2
        return self.outs[a].at[chip, pl.ds(core * hr, hr)]

    def _own(self, a):
        return pltpu.make_async_remote_copy(
            src_ref=self.ins[a], dst_ref=self.outs[a].at[self.me], send_sem=self.send3.at[a], recv_sem=self.recv3.at[a],
            device_id=self.sibling, device_id_type=MESH)

    def _to_chip(self, a, j):
        hr = self.ins[a].shape[0] // 2
        return pltpu.make_async_remote_copy(
            src_ref=self.ins[a].at[pl.ds(self.c * hr, hr)], dst_ref=self._half(a, self.me, self.c),
            send_sem=self.send1.at[a, j], recv_sem=self.recv1.at[a, j], device_id=(*self.chips[j], self.c), device_id_type=MESH)

    def _from_chip(self, a, j):
        landed = self._half(a, _chip_index(*self.chips[j]), self.c)
        return pltpu.make_async_remote_copy(
            src_ref=landed, dst_ref=landed, send_sem=self.send1.at[a, j], recv_sem=self.recv1.at[a, j],
            device_id=(*self.chips[j], self.c), device_id_type=MESH)

    def _to_sibling(self, a, j, core):
        part = self._half(a, _chip_index(*self.chips[j]), core)
        return pltpu.make_async_remote_copy(
            src_ref=part, dst_ref=part, send_sem=self.send2.at[a, j], recv_sem=self.recv2.at[a, j],
            device_id=self.sibling, device_id_type=MESH)

    def _each(self):
        return [(a, j) for a in range(len(self.ins)) for j in range(3)]

    def start(self):
        for a in range(len(self.ins)):
            self._own(a).start()
        for a, j in self._each():
            self._to_chip(a, j).start()

    def forward(self):
        for a, j in self._each():
            self._from_chip(a, j).wait_recv()
            self._to_sibling(a, j, self.c).start()

    def finish(self):
        for a, j in self._each():
            self._to_sibling(a, j, 1 - self.c).wait_recv()
        for a, j in self._each():
            self._to_chip(a, j).wait_send()
            self._to_sibling(a, j, self.c).wait_send()
        for a in range(len(self.ins)):
            self._own(a).wait()


def _pair_exchange(grads):
    n = len(grads)

    def body(*refs):
        ins, outs = refs[:n], refs[n:2 * n]
        send, recv = refs[2 * n:]
        x, y, c, _ = _mesh_position()
        copies = []
        for a in range(n):
            hr = ins[a].shape[1] // 2
            cp = pltpu.make_async_remote_copy(
                src_ref=ins[a].at[:, pl.ds((1 - c) * hr, hr)], dst_ref=outs[a],
                send_sem=send.at[a], recv_sem=recv.at[a], device_id=(x, y, 1 - c), device_id_type=MESH)
            cp.start()
            copies.append(cp)
        for cp in copies:
            cp.wait()

    sem = pltpu.SemaphoreType.DMA
    return pl.pallas_call(
        body, name="pair_exchange", in_specs=_hbm_specs(n), out_specs=_hbm_specs(n),
        out_shape=[jax.ShapeDtypeStruct((g.shape[0], g.shape[1] // 2, g.shape[2]), g.dtype) for g in grads],
        scratch_shapes=[sem((n,)), sem((n,))],
        compiler_params=pltpu.CompilerParams(has_side_effects=True),
    )(*grads)


def _pair_add(core, grad, other, tr, out_dtype):
    hr = other.shape[1]
    cdim = other.shape[2]
    nb = hr // tr

    def body(core_ref, g_ref, o_ref, out_ref):
        out_ref[...] = (g_ref[...] + o_ref[...]).astype(out_dtype)

    return pl.pallas_call(
        body, name="pair_add",
        grid_spec=pltpu.PrefetchScalarGridSpec(
            num_scalar_prefetch=1, grid=(N_CHIPS, nb),
            in_specs=[pl.BlockSpec((1, tr, cdim), lambda j, i, core_ref: (j, core_ref[0] * nb + i, 0)),
                      pl.BlockSpec((1, tr, cdim), lambda j, i, core_ref: (j, i, 0))],
            out_specs=pl.BlockSpec((1, tr, cdim), lambda j, i, core_ref: (j, i, 0))),
        out_shape=jax.ShapeDtypeStruct(other.shape, out_dtype),
        compiler_params=_cparams(("arbitrary", "arbitrary")),
    )(core, grad, other)


def _chip_scatter(parts):
    n = len(parts)

    def body(*refs):
        ins, outs = refs[:n], refs[n:2 * n]
        send, recv, load_sem, store_sem = refs[2 * n:2 * n + 4]
        staged = refs[2 * n + 4:]
        x, y, c, chips = _mesh_position()
        me = _chip_index(x, y)
        copies, loads = [], []
        for a in range(n):
            ld = pltpu.make_async_copy(ins[a].at[me], staged[a], load_sem.at[a])
            ld.start()
            loads.append(ld)
            for j, chip in enumerate(chips):
                cp = pltpu.make_async_remote_copy(
                    src_ref=ins[a].at[_chip_index(*chip)], dst_ref=outs[a].at[me],
                    send_sem=send.at[a, j], recv_sem=recv.at[a, j], device_id=(*chip, c), device_id_type=MESH)
                cp.start()
                copies.append(cp)
        for a in range(n):
            loads[a].wait()
            st = pltpu.make_async_copy(staged[a], outs[a].at[me], store_sem.at[a])
            st.start()
            copies.append(st)
        for cp in copies:
            cp.wait()

    sem = pltpu.SemaphoreType.DMA
    return pl.pallas_call(
        body, name="chip_scatter", in_specs=_hbm_specs(n), out_specs=_hbm_specs(n),
        out_shape=[jax.ShapeDtypeStruct(p.shape, p.dtype) for p in parts],
        scratch_shapes=[sem((n, 3)), sem((n, 3)), sem((n,)), sem((n,))] + [pltpu.VMEM(p.shape[1:], p.dtype) for p in parts],
        compiler_params=pltpu.CompilerParams(has_side_effects=True),
    )(*parts)


def _chip_add(core, recv, tr):
    hr, cdim = recv.shape[1], recv.shape[2]
    nb = hr // tr

    def body(core_ref, r_ref, out_ref):
        out_ref[...] = ((r_ref[0].astype(F32) + r_ref[1].astype(F32)) + r_ref[2].astype(F32)) + r_ref[3].astype(F32)

    return pl.pallas_call(
        body, name="chip_add",
        grid_spec=pltpu.PrefetchScalarGridSpec(
            num_scalar_prefetch=1, grid=(nb,),
            in_specs=[pl.BlockSpec((N_CHIPS, tr, cdim), lambda i, core_ref: (0, i, 0))],
            out_specs=pl.BlockSpec((tr, cdim), lambda i, core_ref: (core_ref[0] * nb + i, 0))),
        out_shape=jax.ShapeDtypeStruct((2 * hr, cdim), F32),
        compiler_params=_cparams(("arbitrary",)),
    )(core, recv)


def _pair_gather(fulls):
    n = len(fulls)

    def body(*refs):
        outs = refs[n:2 * n]
        send, recv = refs[2 * n:]
        x, y, c, _ = _mesh_position()
        copies = []
        for a in range(n):
            hr = outs[a].shape[0] // 2
            mine = outs[a].at[pl.ds(c * hr, hr)]
            cp = pltpu.make_async_remote_copy(
                src_ref=mine, dst_ref=mine, send_sem=send.at[a], recv_sem=recv.at[a],
                device_id=(x, y, 1 - c), device_id_type=MESH)
            cp.start()
            copies.append(cp)
        for cp in copies:
            cp.wait()

    sem = pltpu.SemaphoreType.DMA
    return pl.pallas_call(
        body, name="pair_gather", in_specs=_hbm_specs(n), out_specs=_hbm_specs(n),
        out_shape=[jax.ShapeDtypeStruct(f.shape, f.dtype) for f in fulls],
        input_output_aliases={a: a for a in range(n)},
        scratch_shapes=[sem((n,)), sem((n,))],
        compiler_params=pltpu.CompilerParams(has_side_effects=True),
    )(*fulls)


def _row_tile(rows):
    for t in (512, 256, 128, 64, 32, 16, 8):
        if rows % t == 0:
            return t
    raise ValueError(rows)


def _reduce_scatter(grads, ici_dtypes):
    core = lax.axis_index("c").astype(jnp.int32).reshape(1)
    others = _pair_exchange(grads)
    pair = [_pair_add(core, g, o, _row_tile(o.shape[1]), dt) for g, o, dt in zip(grads, others, ici_dtypes)]
    landed = _chip_scatter(pair)
    return _pair_gather([_chip_add(core, r, _row_tile(r.shape[1])) for r in landed])


def _adamw(w, g, m, v):
    rows, cols = w.shape
    tr = _row_tile(rows) if rows % 8 == 0 else rows

    def body(w_ref, g_ref, m_ref, v_ref, d_ref, nm_ref, nv_ref):
        g_ = g_ref[...]
        m_ = ADAM_B1 * m_ref[...] + (1.0 - ADAM_B1) * g_
        v_ = ADAM_B2 * v_ref[...] + (1.0 - ADAM_B2) * (g_ * g_)
        m_hat = m_ / (1.0 - ADAM_B1 ** ADAM_STEP)
        v_hat = v_ / (1.0 - ADAM_B2 ** ADAM_STEP)
        d_ref[...] = -ADAM_LR * (m_hat / (jnp.sqrt(v_hat) + ADAM_EPS) + ADAM_WD * w_ref[...])
        nm_ref[...] = m_
        nv_ref[...] = v_

    spec = pl.BlockSpec((tr, cols), lambda i: (i, 0))
    shp = jax.ShapeDtypeStruct((rows, cols), F32)
    return pl.pallas_call(
        body, name="adamw", grid=(rows // tr,), in_specs=[spec] * 4, out_specs=[spec] * 3,
        out_shape=[shp] * 3, compiler_params=_cparams(("arbitrary",)),
    )(w, g, m, v)


_SMALL = ["norm_pre", "norm_post", "ssm_a_re", "ssm_a_im", "ssm_log_dt", "ssm_b_re", "ssm_b_im",
          "ssm_c_re", "ssm_c_im", "ssm_d", "b_glu", "na_rpb", "ple_norm"]
_BIG = ["w_in", "w_glu", "w_out", "w_ple", "w_ple_gate"]
_WEIGHTS = ["norm_pre", "norm_post", "w_in", "ssm_a_re", "ssm_a_im", "ssm_log_dt", "ssm_b_re", "ssm_b_im",
            "ssm_c_re", "ssm_c_im", "ssm_d", "w_glu", "b_glu", "na_rpb", "w_out", "w_ple", "ple_norm", "w_ple_gate"]
_SMALL_ROWS = 2176


def _pack_small(tensors, tail=None):
    parts = [tensors[n].reshape(-1) for n in _SMALL] + ([] if tail is None else [tail.reshape(-1)])
    flat = jnp.concatenate(parts)
    flat = jnp.pad(flat, (0, _SMALL_ROWS * 128 - flat.shape[0]))
    return flat.reshape(_SMALL_ROWS, 128)


def _unpack_small(packed, shapes):
    flat = packed.reshape(-1)
    out, off = {}, 0
    for n in _SMALL:
        size = int(np.prod(shapes[n]))
        out[n] = flat[off:off + size].reshape(shapes[n])
        off += size
    return out


def _local_grads(x, p, target, wts):
    ssm_names = ["ssm_a_re", "ssm_a_im", "ssm_log_dt", "ssm_b_re", "ssm_b_im", "ssm_c_re", "ssm_c_im", "ssm_d"]
    ssm_params = [wts[n][0] for n in ssm_names]
    blk, blk_vjp = jax.vjp(_ssm_block_params, *ssm_params)
    (m_mat, ws_mat, wot_mat, a16), gathered = _ssm_chunk_matrices(blk, [wts[n][0].astype(BF16) for n in _BIG])
    w_in_g, w_ple_g = gathered[0], gathered[3]
    w_glu, w_out, w_pg = gathered[1].reshape(512, 512), gathered[2].reshape(1024, 1024), gathered[4].reshape(1024, 1024)
    seq = x.shape[0]
    bias_rows, bias_rows_vjp = jax.vjp(_na_bias_rows, wts["na_rpb"][0])
    bias_tab = _na_bias_table(bias_rows, seq // GRID_W)

    u_c, z_s, q_t, q, k_t, k, v_t, v, z_n = _in_proj(x, wts["norm_pre"], w_in_g)
    s_in = _block_matmul([(u_c, ws_mat, False)], "ssm_chunk_states")
    s_prev = _ssm_state_scan(s_in, a16)
    y_ssm_c = _block_matmul([(u_c, m_mat, False), (s_prev, wot_mat, True)], "ssm_chunk_out")
    y_na_t = _na_fwd(q_t, k, v_t, bias_tab)
    cat = _branch_fwd(y_ssm_c, z_s, y_na_t, z_n, w_glu, wts["b_glu"])

    (loss, d_h1, d_cat, d_w_out, d_g_post, d_w_ple, d_g_ple, d_w_pg) = _head(
        x, p, target, cat, w_out, wts["norm_post"], w_ple_g, wts["ple_norm"], w_pg)
    dy_c, d_z_s, d_y_na_t, d_y_na, d_z_n, d_w_glu, d_b_glu = _branch_bwd(
        y_ssm_c, z_s, y_na_t, z_n, w_glu, wts["b_glu"], d_cat)
    d_q_t, d_k, d_v, d_bias_tab = _na_bwd(q_t, q, k_t, k, v, bias_tab, y_na_t, d_y_na_t, d_y_na)

    d_prev = _block_matmul([(dy_c, wot_mat, False)], "ssm_bwd_states")
    g_st, d_a16 = _ssm_state_scan_bwd(d_prev, s_prev, a16)
    d_u_c = _block_matmul([(dy_c, m_mat, True), (g_st, ws_mat, True)], "ssm_bwd_in", out_dtype=BF16)
    d_m = _block_matmul_tn(u_c, dy_c, "ssm_grad_m")
    d_ws = _block_matmul_tn(u_c, g_st, "ssm_grad_ws")
    d_wot = _block_matmul_tn(dy_c, s_prev, "ssm_grad_wot")
    d_ssm = blk_vjp(tuple(_ssm_chunk_matrices_bwd(blk, d_m, d_ws, d_wot, d_a16)))
    (d_rpb,) = bias_rows_vjp(_na_bias_table_bwd(d_bias_tab, seq // GRID_W))

    dparts = [d_u_c, d_z_s, d_q_t, d_k, d_v, d_z_n]
    d_w_in = _in_proj_bwd_w(x, wts["norm_pre"], dparts)
    grad_x, d_g_pre = _in_proj_bwd_x(x, wts["norm_pre"], w_in_g, d_h1, dparts)

    small = {"norm_pre": d_g_pre, "norm_post": d_g_post, "b_glu": d_b_glu, "na_rpb": d_rpb, "ple_norm": d_g_ple}
    for n, g in zip(ssm_names, d_ssm):
        small[n] = g
    big = {"w_in": d_w_in, "w_glu": d_w_glu.reshape(N_CHIPS, 128, 512), "w_out": d_w_out.reshape(N_CHIPS, 256, 1024),
           "w_ple": d_w_ple, "w_ple_gate": d_w_pg.reshape(N_CHIPS, 256, 1024)}
    return loss, grad_x, small, big


def kernel(x, p, norm_pre, norm_post, w_in, ssm_a_re, ssm_a_im, ssm_log_dt, ssm_b_re, ssm_b_im, ssm_c_re, ssm_c_im, ssm_d, w_glu, b_glu, na_rpb, w_out, w_ple, ple_norm, w_ple_gate, loss_target, m_norm_pre, m_norm_post, m_w_in, m_ssm_a_re, m_ssm_a_im, m_ssm_log_dt, m_ssm_b_re, m_ssm_b_im, m_ssm_c_re, m_ssm_c_im, m_ssm_d, m_w_glu, m_b_glu, m_na_rpb, m_w_out, m_w_ple, m_ple_norm, m_w_ple_gate, v_norm_pre, v_norm_post, v_w_in, v_ssm_a_re, v_ssm_a_im, v_ssm_log_dt, v_ssm_b_re, v_ssm_b_im, v_ssm_c_re, v_ssm_c_im, v_ssm_d, v_w_glu, v_b_glu, v_na_rpb, v_w_out, v_w_ple, v_ple_norm, v_w_ple_gate):
    wts = dict(norm_pre=norm_pre, norm_post=norm_post, w_in=w_in, ssm_a_re=ssm_a_re, ssm_a_im=ssm_a_im,
               ssm_log_dt=ssm_log_dt, ssm_b_re=ssm_b_re, ssm_b_im=ssm_b_im, ssm_c_re=ssm_c_re, ssm_c_im=ssm_c_im,
               ssm_d=ssm_d, w_glu=w_glu, b_glu=b_glu, na_rpb=na_rpb, w_out=w_out, w_ple=w_ple, ple_norm=ple_norm,
               w_ple_gate=w_ple_gate)
    mom_m = dict(norm_pre=m_norm_pre, norm_post=m_norm_post, w_in=m_w_in, ssm_a_re=m_ssm_a_re, ssm_a_im=m_ssm_a_im,
                 ssm_log_dt=m_ssm_log_dt, ssm_b_re=m_ssm_b_re, ssm_b_im=m_ssm_b_im, ssm_c_re=m_ssm_c_re,
                 ssm_c_im=m_ssm_c_im, ssm_d=m_ssm_d, w_glu=m_w_glu, b_glu=m_b_glu, na_rpb=m_na_rpb, w_out=m_w_out,
                 w_ple=m_w_ple, ple_norm=m_ple_norm, w_ple_gate=m_w_ple_gate)
    mom_v = dict(norm_pre=v_norm_pre, norm_post=v_norm_post, w_in=v_w_in, ssm_a_re=v_ssm_a_re, ssm_a_im=v_ssm_a_im,
                 ssm_log_dt=v_ssm_log_dt, ssm_b_re=v_ssm_b_re, ssm_b_im=v_ssm_b_im, ssm_c_re=v_ssm_c_re,
                 ssm_c_im=v_ssm_c_im, ssm_d=v_ssm_d, w_glu=v_w_glu, b_glu=v_b_glu, na_rpb=v_na_rpb, w_out=v_w_out,
                 w_ple=v_w_ple, ple_norm=v_ple_norm, w_ple_gate=v_w_ple_gate)

    loss_part, grad_x, small, big = _local_grads(x[0], p[0, 0], loss_target[0], wts)

    small_packed = _pack_small(small, tail=loss_part).reshape(N_CHIPS, _SMALL_ROWS // N_CHIPS, 128)
    reduced = _reduce_scatter([big[n] for n in _BIG] + [small_packed], [BF16] * len(_BIG) + [F32])
    grads = dict(zip(_BIG, reduced[:-1]))
    (small_all,) = _gather_chips([reduced[-1]], "gather_small_grads")
    small_all = small_all.reshape(_SMALL_ROWS, 128)
    loss = small_all.reshape(-1)[sum(int(np.prod(wts[n].shape)) for n in _SMALL)]

    delta, new_m, new_v = {}, {}, {}
    for n in _BIG:
        shp = wts[n].shape
        d_, m_, v_ = _adamw(wts[n][0], grads[n], mom_m[n][0], mom_v[n][0])
        grads[n] = grads[n].reshape(shp)
        delta[n], new_m[n], new_v[n] = d_.reshape(shp), m_.reshape(shp), v_.reshape(shp)
    shapes = {n: wts[n].shape for n in _SMALL}
    d_s, m_s, v_s = _adamw(_pack_small(wts), small_all, _pack_small(mom_m), _pack_small(mom_v))
    for dst, packed in ((grads, small_all), (delta, d_s), (new_m, m_s), (new_v, v_s)):
        dst.update(_unpack_small(packed, shapes))

    return (loss, grad_x[None], *[grads[n] for n in _WEIGHTS], *[delta[n] for n in _WEIGHTS],
            *[new_m[n] for n in _WEIGHTS], *[new_v[n] for n in _WEIGHTS])
```

```python
import functools
import math

import jax
import jax.numpy as jnp
import numpy as np
from jax import lax
from jax.experimental import pallas as pl
from jax.experimental.pallas import tpu as pltpu

F32 = jnp.float32
BF16 = jnp.bfloat16

D_MODEL = 1024
D_PLE = 256
GRID_W = 64
D_SSM = 512
SSM_GROUP = 16
N_GROUPS = 32
SSM_STATE = 64
D_NA = 512
NA_HEADS = 8
NA_HEAD_DIM = 64
NA_ROWS = 8
NA_COLS = 16
D_IN_PROJ = 3072
EPS = 1e-6

CHUNK = 16
GROUPS_PER_BLOCK = 8
N_BLOCKS = N_GROUPS // GROUPS_PER_BLOCK
BLOCK_CH = GROUPS_PER_BLOCK * SSM_GROUP
BLOCK_ST = GROUPS_PER_BLOCK * SSM_STATE
CHUNK_W = CHUNK * BLOCK_CH
STATE_W = 4 * BLOCK_ST

N_CHIPS = 4
MESH = pl.DeviceIdType.MESH

ADAM_LR = 0.001
ADAM_B1 = 0.9
ADAM_B2 = 0.999
ADAM_EPS = 1e-08
ADAM_WD = 0.01
ADAM_STEP = 10

VMEM_LIMIT = 52 * 1024 * 1024
HIGHEST = lax.Precision.HIGHEST


def _cparams(sem=None, **kw):
    if sem is not None:
        kw["dimension_semantics"] = sem
    return pltpu.CompilerParams(vmem_limit_bytes=VMEM_LIMIT, **kw)


def _resident(*shape):
    return pl.BlockSpec(shape, lambda *_: (0,) * len(shape), pipeline_mode=pl.Buffered(1))


def _dot(a, b, dims=((1,), (0,))):
    return lax.dot_general(a, b, (dims, ((), ())), preferred_element_type=F32)


def _dot_nt(a, b):
    return _dot(a, b, ((1,), (1,)))


def _dot_tn(a, b):
    return _dot(a, b, ((0,), (0,)))


def _sigmoid(x):
    return 1.0 / (1.0 + jnp.exp(-x))


_GELU_C = math.sqrt(2.0 / math.pi)


def _gelu_parts(x):
    inner = _GELU_C * (x + 0.044715 * (x * x * x))
    t = jnp.tanh(inner)
    return 0.5 * x * (1.0 + t), t


def _gelu_grad(x, t):
    return 0.5 * (1.0 + t) + 0.5 * x * (1.0 - t * t) * (_GELU_C * (1.0 + 3.0 * 0.044715 * x * x))


def _silu_parts(z):
    s = _sigmoid(z)
    return z * s, s


def _silu_grad(z, s):
    return s * (1.0 + z * (1.0 - s))


def _rms(x):
    r = lax.rsqrt(jnp.mean(x * x, axis=-1, keepdims=True) + EPS)
    return x * r, r


def _rms_bwd(dn, n, r):
    return r * (dn - n * jnp.mean(dn * n, axis=-1, keepdims=True))


def _chunk_scratch(tm):
    return pltpu.VMEM((N_BLOCKS, tm, BLOCK_CH), F32)


def _store_chunks(val, scr, c_ref, dtype):
    nc = scr.shape[1] // CHUNK
    for b in range(N_BLOCKS):
        scr[b] = val[:, b * BLOCK_CH:(b + 1) * BLOCK_CH]
        for j in range(CHUNK):
            c_ref[b, :, j * BLOCK_CH:(j + 1) * BLOCK_CH] = scr[b, pl.ds(j, nc, stride=CHUNK), :].astype(dtype)


def _load_chunks(c_ref, scr):
    nc = scr.shape[1] // CHUNK
    for b in range(N_BLOCKS):
        for j in range(CHUNK):
            scr[b, pl.ds(j, nc, stride=CHUNK), :] = c_ref[b, :, j * BLOCK_CH:(j + 1) * BLOCK_CH].astype(F32)
    return jnp.concatenate([scr[b] for b in range(N_BLOCKS)], axis=1)


def _chunk_spec(tm):
    return pl.BlockSpec((N_BLOCKS, tm // CHUNK, CHUNK_W), lambda i: (0, i, 0))


def _heads_t_spec(tm):
    return pl.BlockSpec((D_NA, tm), lambda i: (0, i))


def _in_proj(x, g_pre, w_in_g, tm=512):
    L = x.shape[0]
    wn = w_in_g.shape[2]

    def body(x_ref, g_ref, w_ref, uc_ref, zs_ref, qt_ref, q_ref, kt_ref, k_ref, vt_ref, v_ref, zn_ref, u_scr):
        n, _ = _rms(x_ref[...])
        hn = (n * g_ref[...]).astype(BF16)
        proj = jnp.concatenate([_dot(hn, w_ref[j]) for j in range(N_CHIPS)], axis=1)
        _store_chunks(proj[:, 0:512], u_scr, uc_ref, BF16)
        zs_ref[...] = proj[:, 512:1024]
        q = proj[:, 1024:1536] * (NA_HEAD_DIM ** -0.5)
        for val, t_ref, n_ref in ((q, qt_ref, q_ref), (proj[:, 1536:2048], kt_ref, k_ref), (proj[:, 2048:2560], vt_ref, v_ref)):
            t_ref[...] = val.T.astype(BF16)
            n_ref[...] = val.astype(BF16)
        zn_ref[...] = proj[:, 2560:3072]

    tok = jax.ShapeDtypeStruct((L, 512), F32)
    tr = jax.ShapeDtypeStruct((D_NA, L), BF16)
    hm = jax.ShapeDtypeStruct((L, D_NA), BF16)
    tspec = pl.BlockSpec((tm, 512), lambda i: (i, 0))
    return pl.pallas_call(
        body, name="in_proj", grid=(L // tm,),
        in_specs=[pl.BlockSpec((tm, D_MODEL), lambda i: (i, 0)),
                  _resident(1, D_MODEL), _resident(N_CHIPS, D_MODEL, wn)],
        out_specs=[_chunk_spec(tm), tspec] + [_heads_t_spec(tm), tspec] * 3 + [tspec],
        out_shape=[jax.ShapeDtypeStruct((N_BLOCKS, L // CHUNK, CHUNK_W), BF16), tok, tr, hm, tr, hm, tr, hm, tok],
        scratch_shapes=[_chunk_scratch(tm)],
        compiler_params=_cparams(("arbitrary",)),
    )(x, g_pre, w_in_g)


def _ssm_block_params(a_re, a_im, log_dt, b_re, b_im, c_re, c_im, d):
    eye_g = jnp.eye(GROUPS_PER_BLOCK, dtype=F32)[None, None, :, None, :, None]

    def lanes(t):
        return t.reshape(2, N_BLOCKS, 1, BLOCK_ST)

    def expand(t):
        return (t[:, :, :, :, None, :] * eye_g).reshape(2, N_BLOCKS, BLOCK_CH, BLOCK_ST)

    b_shape = (2, N_BLOCKS, GROUPS_PER_BLOCK, SSM_STATE, SSM_GROUP)
    c_shape = (2, N_BLOCKS, GROUPS_PER_BLOCK, SSM_GROUP, SSM_STATE)
    return (lanes(a_re), lanes(a_im), lanes(jnp.broadcast_to(log_dt[..., None], a_re.shape)),
            expand(b_re.reshape(b_shape).transpose(0, 1, 2, 4, 3)), expand(b_im.reshape(b_shape).transpose(0, 1, 2, 4, 3)),
            expand(c_re.reshape(c_shape)), expand(c_im.reshape(c_shape)), d.reshape(N_BLOCKS, 1, BLOCK_CH))


def _ssm_discretise(ar, ai, ldt):
    dt = jnp.exp(ldt)
    mag = jnp.exp(dt * ar)
    abr = mag * jnp.cos(dt * ai)
    abi = mag * jnp.sin(dt * ai)
    num_re = abr - 1.0
    num_im = abi
    denom = ar * ar + ai * ai
    coef_re = (num_re * ar + num_im * ai) / denom
    coef_im = (num_im * ar - num_re * ai) / denom
    return abr, abi, coef_re, coef_im


_POW_ROWS = 24


def _ssm_fill_powers(ar_ref, ai_ref, ldt_ref, br_ref, bi_ref, pw_ref, bbar_ref):
    for d in range(2):
        abr, abi, cfr, cfi = _ssm_discretise(ar_ref[d, 0], ai_ref[d, 0], ldt_ref[d, 0])
        bbar_ref[d, 0] = cfr * br_ref[d, 0] - cfi * bi_ref[d, 0]
        bbar_ref[d, 1] = cfr * bi_ref[d, 0] + cfi * br_ref[d, 0]
        pr, pi = jnp.ones_like(abr), jnp.zeros_like(abi)
        for t in range(CHUNK + 1):
            pw_ref[d, 0, t:t + 1, :] = pr
            pw_ref[d, 1, t:t + 1, :] = pi
            pr, pi = pr * abr - pi * abi, pr * abi + pi * abr


def _dot_rounded(a, b, dims=((1,), (0,))):
    return _dot(a.astype(BF16), b.astype(BF16), dims)


def _ssm_stack_inputs(d, pw_ref, bbar_ref, xs_ref):
    for t in range(CHUNK):
        pr, pi = pw_ref[d, 0, t:t + 1, :], pw_ref[d, 1, t:t + 1, :]
        xs_ref[0, t * BLOCK_CH:(t + 1) * BLOCK_CH, :] = bbar_ref[d, 0] * pr - bbar_ref[d, 1] * pi
        xs_ref[1, t * BLOCK_CH:(t + 1) * BLOCK_CH, :] = bbar_ref[d, 0] * pi + bbar_ref[d, 1] * pr


def _eye(n):
    return (lax.broadcasted_iota(jnp.int32, (n, n), 0) == lax.broadcasted_iota(jnp.int32, (n, n), 1)).astype(F32)


def _ssm_param_specs():
    vec = pl.BlockSpec((2, 1, 1, BLOCK_ST), lambda b, j: (0, b, 0, 0))
    mat = pl.BlockSpec((2, 1, BLOCK_CH, BLOCK_ST), lambda b, j: (0, b, 0, 0))
    return [vec, vec, vec, mat, mat, mat, mat, pl.BlockSpec((1, 1, BLOCK_CH), lambda b, j: (b, 0, 0))]


def _ssm_chunk_matrices(blk, shards):
    n = len(shards)

    def body(*refs):
        ar_ref, ai_ref, ldt_ref, br_ref, bi_ref, cr_ref, ci_ref, d_ref = refs[:8]
        m_ref, ws_ref, wot_ref, a16_ref = refs[8 + n:12 + n]
        pw_ref, bbar_ref, lag_ref, xs_ref = refs[12 + 2 * n:16 + 2 * n]
        gather = _ChipGather(refs[8:8 + n], refs[12 + n:12 + 2 * n], refs[16 + 2 * n:])
        b, j = pl.program_id(0), pl.program_id(1)
        pl.when((b == 0) & (j == 0))(gather.start)
        pl.when((b == N_BLOCKS - 1) & (j == 0))(gather.forward)
        pl.when((b == N_BLOCKS - 1) & (j == CHUNK - 1))(gather.finish)

        @pl.when(j == 0)
        def _():
            _ssm_fill_powers(ar_ref, ai_ref, ldt_ref, br_ref, bi_ref, pw_ref, bbar_ref)
            zero_lag = d_ref[0] * _eye(BLOCK_CH)
            for d in range(2):
                _ssm_stack_inputs(d, pw_ref, bbar_ref, xs_ref)
                taps = (_dot_rounded(xs_ref[0], cr_ref[d, 0], ((1,), (1,)))
                        - _dot_rounded(xs_ref[1], ci_ref[d, 0], ((1,), (1,))))
                zero_lag = zero_lag + taps[0:BLOCK_CH]
                for t in range(1, CHUNK):
                    lag_ref[CHUNK - 1 + t if d == 0 else CHUNK - 1 - t] = taps[t * BLOCK_CH:(t + 1) * BLOCK_CH]
            lag_ref[CHUNK - 1] = zero_lag
            a16_ref[0] = jnp.concatenate([pw_ref[d, ri, CHUNK:CHUNK + 1, :] for d in range(2) for ri in range(2)], axis=1)

        m_ref[0] = jnp.concatenate([lag_ref[jp - j + CHUNK - 1] for jp in range(CHUNK)], axis=1).astype(BF16)

        def power(d, t):
            return pw_ref[d, 0, pl.ds(t, 1), :], pw_ref[d, 1, pl.ds(t, 1), :]

        parts = []
        for d, t in ((0, CHUNK - 1 - j), (1, j)):
            pr, pi = power(d, t)
            parts += [bbar_ref[d, 0] * pr - bbar_ref[d, 1] * pi, bbar_ref[d, 0] * pi + bbar_ref[d, 1] * pr]
        ws_ref[0] = jnp.concatenate(parts, axis=1).astype(BF16)
        parts = []
        for d, t in ((0, j + 1), (1, CHUNK - j)):
            pr, pi = power(d, t)
            parts += [cr_ref[d, 0] * pr - ci_ref[d, 0] * pi, -cr_ref[d, 0] * pi - ci_ref[d, 0] * pr]
        wot_ref[0] = jnp.concatenate(parts, axis=1).astype(BF16)

    row = pl.BlockSpec((1, BLOCK_CH, CHUNK_W), lambda b, j: (b, j, 0))
    mat = jax.ShapeDtypeStruct((N_BLOCKS, CHUNK_W, CHUNK_W), BF16)
    outs = pl.pallas_call(
        body, name="ssm_chunk_matrices", grid=(N_BLOCKS, CHUNK),
        in_specs=_ssm_param_specs() + _hbm_specs(n),
        out_specs=[row, row, row, pl.BlockSpec((1, 1, STATE_W), lambda b, j: (b, 0, 0))] + _hbm_specs(n),
        out_shape=[mat, mat, mat, jax.ShapeDtypeStruct((N_BLOCKS, 1, STATE_W), F32)] + _gather_out_shapes(shards),
        scratch_shapes=[pltpu.VMEM((2, 2, _POW_ROWS, BLOCK_ST), F32), pltpu.VMEM((2, 2, BLOCK_CH, BLOCK_ST), F32),
                        pltpu.VMEM((2 * CHUNK, BLOCK_CH, BLOCK_CH), F32), pltpu.VMEM((2, CHUNK_W, BLOCK_ST), F32)]
        + _gather_semaphores(n),
        compiler_params=_cparams(("arbitrary", "arbitrary"), has_side_effects=True),
    )(*blk, *shards)
    return outs[:4], outs[4:]


def _ssm_chunk_matrices_bwd(blk, d_m, d_ws, d_wot, d_a16):
    def body(ar_ref, ai_ref, ldt_ref, br_ref, bi_ref, cr_ref, ci_ref, d_ref, dm_ref, dws_ref, dwot_ref, da16_ref,
             dar_ref, dai_ref, dldt_ref, dbr_ref, dbi_ref, dcr_ref, dci_ref, dd_ref,
             pw_ref, bbar_ref, dlag_ref, dbbar_ref, dc_ref, dpw_ref, xs_ref, dts_ref):
        j = pl.program_id(1)
        w = BLOCK_ST

        @pl.when(j == 0)
        def _():
            _ssm_fill_powers(ar_ref, ai_ref, ldt_ref, br_ref, bi_ref, pw_ref, bbar_ref)
            for r in (dlag_ref, dbbar_ref, dc_ref, dpw_ref):
                r[...] = jnp.zeros_like(r)

        def x_chain(d, t, dxr, dxi):
            pr, pi = pw_ref[d, 0, pl.ds(t, 1), :], pw_ref[d, 1, pl.ds(t, 1), :]
            bbr, bbi = bbar_ref[d, 0], bbar_ref[d, 1]
            dbbar_ref[d, 0] += dxr * pr + dxi * pi
            dbbar_ref[d, 1] += dxi * pr - dxr * pi
            dpw_ref[d, 0, pl.ds(t, 1), :] += jnp.sum(dxr * bbr + dxi * bbi, axis=0, keepdims=True)
            dpw_ref[d, 1, pl.ds(t, 1), :] += jnp.sum(dxi * bbr - dxr * bbi, axis=0, keepdims=True)

        def z_chain(d, t, dzr, dzi):
            pr, pi = pw_ref[d, 0, pl.ds(t, 1), :], pw_ref[d, 1, pl.ds(t, 1), :]
            c_r, c_i = cr_ref[d, 0], ci_ref[d, 0]
            dc_ref[d, 0] += dzr * pr - dzi * pi
            dc_ref[d, 1] += -dzr * pi - dzi * pr
            dpw_ref[d, 0, pl.ds(t, 1), :] += jnp.sum(dzr * c_r - dzi * c_i, axis=0, keepdims=True)
            dpw_ref[d, 1, pl.ds(t, 1), :] += jnp.sum(-dzr * c_i - dzi * c_r, axis=0, keepdims=True)

        for jp in range(CHUNK):
            dlag_ref[jp - j + CHUNK - 1] += dm_ref[0, :, jp * BLOCK_CH:(jp + 1) * BLOCK_CH].astype(F32)
        quarter = lambda ref, i: ref[0, :, i * w:(i + 1) * w].astype(F32)
        x_chain(0, CHUNK - 1 - j, quarter(dws_ref, 0), quarter(dws_ref, 1))
        x_chain(1, j, quarter(dws_ref, 2), quarter(dws_ref, 3))
        z_chain(0, j + 1, quarter(dwot_ref, 0), quarter(dwot_ref, 1))
        z_chain(1, CHUNK - j, quarter(dwot_ref, 2), quarter(dwot_ref, 3))

        @pl.when(j == CHUNK - 1)
        def _():
            for d in range(2):
                _ssm_stack_inputs(d, pw_ref, bbar_ref, xs_ref)
                for t in range(CHUNK):
                    dts_ref[t * BLOCK_CH:(t + 1) * BLOCK_CH, :] = dlag_ref[CHUNK - 1 + t if d == 0 else CHUNK - 1 - t]
                d_taps = dts_ref[...]
                dc_ref[d, 0] += _dot_rounded(d_taps, xs_ref[0], ((0,), (0,)))
                dc_ref[d, 1] -= _dot_rounded(d_taps, xs_ref[1], ((0,), (0,)))
                xs_ref[0] = _dot_rounded(d_taps, cr_ref[d, 0])
                xs_ref[1] = -_dot_rounded(d_taps, ci_ref[d, 0])
                for t in range(CHUNK):
                    rows = slice(t * BLOCK_CH, (t + 1) * BLOCK_CH)
                    x_chain(d, t, xs_ref[0, rows, :], xs_ref[1, rows, :])
            dd_ref[0] = jnp.sum(dlag_ref[CHUNK - 1] * _eye(BLOCK_CH), axis=0, keepdims=True)
            for d in range(2):
                (abr, abi, cfr, cfi), disc_vjp = jax.vjp(_ssm_discretise, ar_ref[d, 0], ai_ref[d, 0], ldt_ref[d, 0])
                dpr = dpw_ref[d, 0, CHUNK:CHUNK + 1, :] + da16_ref[0, :, 2 * d * w:(2 * d + 1) * w]
                dpi = dpw_ref[d, 1, CHUNK:CHUNK + 1, :] + da16_ref[0, :, (2 * d + 1) * w:(2 * d + 2) * w]
                dabr, dabi = jnp.zeros_like(abr), jnp.zeros_like(abi)
                for t in range(CHUNK, 0, -1):
                    qr, qi = pw_ref[d, 0, t - 1:t, :], pw_ref[d, 1, t - 1:t, :]
                    dabr = dabr + dpr * qr + dpi * qi
                    dabi = dabi + dpi * qr - dpr * qi
                    dpr, dpi = (dpr * abr + dpi * abi + dpw_ref[d, 0, t - 1:t, :],
                                dpi * abr - dpr * abi + dpw_ref[d, 1, t - 1:t, :])
                dbbr, dbbi = dbbar_ref[d, 0], dbbar_ref[d, 1]
                b_r, b_i = br_ref[d, 0], bi_ref[d, 0]
                dbr_ref[d, 0] = cfr * dbbr + cfi * dbbi
                dbi_ref[d, 0] = cfr * dbbi - cfi * dbbr
                dcfr = jnp.sum(b_r * dbbr + b_i * dbbi, axis=0, keepdims=True)
                dcfi = jnp.sum(b_r * dbbi - b_i * dbbr, axis=0, keepdims=True)
                dar_ref[d, 0], dai_ref[d, 0], dldt_ref[d, 0] = disc_vjp((dabr, dabi, dcfr, dcfi))
                dcr_ref[d, 0] = dc_ref[d, 0]
                dci_ref[d, 0] = dc_ref[d, 1]

    row = pl.BlockSpec((1, BLOCK_CH, CHUNK_W), lambda b, j: (b, j, 0))
    specs = _ssm_param_specs()
    acc = lambda *s: pltpu.VMEM(s, F32)
    return pl.pallas_call(
        body, name="ssm_chunk_matrices_bwd", grid=(N_BLOCKS, CHUNK),
        in_specs=specs + [row, row, row, pl.BlockSpec((1, 1, STATE_W), lambda b, j: (b, 0, 0))],
        out_specs=specs,
        out_shape=[jax.ShapeDtypeStruct(t.shape, F32) for t in blk],
        scratch_shapes=[acc(2, 2, _POW_ROWS, BLOCK_ST), acc(2, 2, BLOCK_CH, BLOCK_ST), acc(2 * CHUNK, BLOCK_CH, BLOCK_CH),
                        acc(2, 2, BLOCK_CH, BLOCK_ST), acc(2, 2, BLOCK_CH, BLOCK_ST), acc(2, 2, _POW_ROWS, BLOCK_ST),
                        acc(2, CHUNK_W, BLOCK_ST), acc(CHUNK_W, BLOCK_CH)],
        compiler_params=_cparams(("arbitrary", "arbitrary")),
    )(*blk, d_m, d_ws, d_wot, d_a16)


def _block_matmul(terms, name, out_dtype=F32, tn=1024):
    nc = terms[0][0].shape[1]
    n_out = terms[0][1].shape[1] if terms[0][2] else terms[0][1].shape[2]
    flags = [t[2] for t in terms]

    def body(*refs):
        out_ref = refs[-1]
        acc = None
        for t, transposed in enumerate(flags):
            a = refs[2 * t][0].astype(BF16)
            w = refs[2 * t + 1][0]
            part = _dot_nt(a, w) if transposed else _dot(a, w)
            acc = part if acc is None else acc + part
        out_ref[0] = acc.astype(out_dtype)

    in_specs, args = [], []
    for a, w, transposed in terms:
        k = a.shape[2]
        in_specs.append(pl.BlockSpec((1, nc, k), lambda b, n: (b, 0, 0)))
        if transposed:
            in_specs.append(pl.BlockSpec((1, tn, k), lambda b, n: (b, n, 0)))
        else:
            in_specs.append(pl.BlockSpec((1, k, tn), lambda b, n: (b, 0, n)))
        args += [a, w]
    return pl.pallas_call(
        body, name=name, grid=(N_BLOCKS, n_out // tn), in_specs=in_specs,
        out_specs=pl.BlockSpec((1, nc, tn), lambda b, n: (b, 0, n)),
        out_shape=jax.ShapeDtypeStruct((N_BLOCKS, nc, n_out), out_dtype),
        compiler_params=_cparams(("arbitrary", "arbitrary")),
    )(*args)


def _block_matmul_tn(a, b, name, tile=1024):
    nc, m = a.shape[1], a.shape[2]
    n = b.shape[2]

    def body(a_ref, b_ref, out_ref):
        out_ref[0] = _dot_tn(a_ref[0].astype(BF16), b_ref[0].astype(BF16)).astype(BF16)

    return pl.pallas_call(
        body, name=name, grid=(N_BLOCKS, m // tile, n // tile),
        in_specs=[pl.BlockSpec((1, nc, tile), lambda blk, i, j: (blk, 0, i)),
                  pl.BlockSpec((1, nc, tile), lambda blk, i, j: (blk, 0, j))],
        out_specs=pl.BlockSpec((1, tile, tile), lambda blk, i, j: (blk, i, j)),
        out_shape=jax.ShapeDtypeStruct((N_BLOCKS, m, n), BF16),
        compiler_params=_cparams(("arbitrary", "arbitrary", "arbitrary")),
    )(a, b)


def _cmul(ar, ai, xr, xi):
    return ar * xr - ai * xi, ar * xi + ai * xr


def _cmul_conj(ar, ai, xr, xi):
    return ar * xr + ai * xi, ar * xi - ai * xr


def _ssm_state_scan(s_in, a16):
    nc = s_in.shape[1]
    w = BLOCK_ST

    def body(sin_ref, a_ref, out_ref):
        a = a_ref[0]
        afr, afi, abr, abi = a[:, 0:w], a[:, w:2 * w], a[:, 2 * w:3 * w], a[:, 3 * w:4 * w]

        def step(c, carry):
            fr, fi, br, bi = carry
            cb = nc - 1 - c
            out_ref[0, pl.ds(c, 1), 0:w] = fr
            out_ref[0, pl.ds(c, 1), w:2 * w] = fi
            out_ref[0, pl.ds(cb, 1), 2 * w:3 * w] = br
            out_ref[0, pl.ds(cb, 1), 3 * w:4 * w] = bi
            nfr, nfi = _cmul(afr, afi, fr, fi)
            nbr, nbi = _cmul(abr, abi, br, bi)
            return (nfr + sin_ref[0, pl.ds(c, 1), 0:w], nfi + sin_ref[0, pl.ds(c, 1), w:2 * w],
                    nbr + sin_ref[0, pl.ds(cb, 1), 2 * w:3 * w], nbi + sin_ref[0, pl.ds(cb, 1), 3 * w:4 * w])

        z = jnp.zeros((1, w), F32)
        lax.fori_loop(0, nc, step, (z, z, z, z))

    spec = pl.BlockSpec((1, nc, STATE_W), lambda b: (b, 0, 0))
    return pl.pallas_call(
        body, name="ssm_state_scan", grid=(N_BLOCKS,),
        in_specs=[spec, pl.BlockSpec((1, 1, STATE_W), lambda b: (b, 0, 0))],
        out_specs=spec, out_shape=jax.ShapeDtypeStruct(s_in.shape, F32),
        compiler_params=_cparams(("arbitrary",)),
    )(s_in, a16)


def _ssm_state_scan_bwd(d_prev, s_prev, a16):
    nc = d_prev.shape[1]
    w = BLOCK_ST

    def body(dp_ref, sp_ref, a_ref, g_ref, da_ref):
        a = a_ref[0]
        afr, afi, abr, abi = a[:, 0:w], a[:, w:2 * w], a[:, 2 * w:3 * w], a[:, 3 * w:4 * w]

        def step(i, carry):
            gfr, gfi, gbr, gbi, dafr, dafi, dabr, dabi = carry
            cf = nc - 1 - i
            cb = i
            g_ref[0, pl.ds(cf, 1), 0:w] = gfr
            g_ref[0, pl.ds(cf, 1), w:2 * w] = gfi
            g_ref[0, pl.ds(cb, 1), 2 * w:3 * w] = gbr
            g_ref[0, pl.ds(cb, 1), 3 * w:4 * w] = gbi
            sfr, sfi = sp_ref[0, pl.ds(cf, 1), 0:w], sp_ref[0, pl.ds(cf, 1), w:2 * w]
            sbr, sbi = sp_ref[0, pl.ds(cb, 1), 2 * w:3 * w], sp_ref[0, pl.ds(cb, 1), 3 * w:4 * w]
            dafr = dafr + gfr * sfr + gfi * sfi
            dafi = dafi + gfi * sfr - gfr * sfi
            dabr = dabr + gbr * sbr + gbi * sbi
            dabi = dabi + gbi * sbr - gbr * sbi
            nfr, nfi = _cmul_conj(afr, afi, gfr, gfi)
            nbr, nbi = _cmul_conj(abr, abi, gbr, gbi)
            return (nfr + dp_ref[0, pl.ds(cf, 1), 0:w], nfi + dp_ref[0, pl.ds(cf, 1), w:2 * w],
                    nbr + dp_ref[0, pl.ds(cb, 1), 2 * w:3 * w], nbi + dp_ref[0, pl.ds(cb, 1), 3 * w:4 * w],
                    dafr, dafi, dabr, dabi)

        z = jnp.zeros((1, w), F32)
        res = lax.fori_loop(0, nc, step, (z,) * 8)
        da_ref[0] = jnp.concatenate(res[4:], axis=1)

    spec = pl.BlockSpec((1, nc, STATE_W), lambda b: (b, 0, 0))
    aspec = pl.BlockSpec((1, 1, STATE_W), lambda b: (b, 0, 0))
    return pl.pallas_call(
        body, name="ssm_state_scan_bwd", grid=(N_BLOCKS,),
        in_specs=[spec, spec, aspec], out_specs=[spec, aspec],
        out_shape=[jax.ShapeDtypeStruct(d_prev.shape, F32), jax.ShapeDtypeStruct((N_BLOCKS, 1, STATE_W), F32)],
        compiler_params=_cparams(("arbitrary",)),
    )(d_prev, s_prev, a16)


NA_PAIR = 2 * GRID_W
NA_WIN_ROWS = NA_ROWS + 2
NA_WIN = NA_WIN_ROWS * GRID_W
NA_PAIRS_PER_STEP = 8
NA_CASES = 5
NA_MASKED = -1e30


def _na_pair_window(m, rows):
    rs0 = jnp.clip(2 * m - NA_ROWS // 2, 0, rows - NA_ROWS)
    ws = jnp.minimum(rs0, rows - NA_WIN_ROWS)
    last = rows // 2 - 1
    case = jnp.where(m == 0, 0, jnp.where(m == 1, 1, jnp.where(m == last - 1, 3, jnp.where(m == last, 4, 2))))
    return ws, case


def _na_row_offsets(rows):
    last = rows // 2 - 1
    geom = []
    for m in (0, 1, 2, last - 1, last):
        ws = min(max(2 * m - NA_ROWS // 2, 0), rows - NA_ROWS, rows - NA_WIN_ROWS)
        per_case = []
        for i in range(NA_WIN_ROWS):
            pair = []
            for rr in range(2):
                r = 2 * m + rr
                rs = min(max(r - NA_ROWS // 2, 0), rows - NA_ROWS)
                pair.append(ws + i - r + NA_ROWS - 1 if rs <= ws + i < rs + NA_ROWS else None)
            per_case.append(pair)
        geom.append(per_case)
    return geom


def _na_col_select():
    qc = np.arange(NA_PAIR)[None, :] % GRID_W
    kc = np.arange(GRID_W)[:, None]
    dc = np.clip(kc - qc + NA_COLS - 1, 0, 2 * NA_COLS - 2)
    return jnp.asarray((np.arange(2 * NA_COLS - 1)[:, None, None] == dc[None]).astype(np.float32))


def _na_bias_rows(rpb):
    return jnp.einsum("hrd,dkl->hrkl", rpb, _na_col_select(), precision=HIGHEST)


def _na_col_window():
    qc = lax.broadcasted_iota(jnp.int32, (GRID_W, NA_PAIR), 1) % GRID_W
    kc = lax.broadcasted_iota(jnp.int32, (GRID_W, NA_PAIR), 0)
    cs = jnp.clip(qc - NA_COLS // 2, 0, GRID_W - NA_COLS)
    first_row = lax.broadcasted_iota(jnp.int32, (GRID_W, NA_PAIR), 1) < GRID_W
    return (kc >= cs) & (kc < cs + NA_COLS), first_row


def _na_bias_table(bias_rows, rows):
    geom = _na_row_offsets(rows)

    def body(br_ref, tab_ref):
        col_ok, first_row = _na_col_window()
        masked = jnp.full((GRID_W, NA_PAIR), NA_MASKED, F32)
        for case in range(NA_CASES):
            for i in range(NA_WIN_ROWS):
                d0, d1 = geom[case][i]
                t0 = masked if d0 is None else br_ref[0, d0]
                t1 = masked if d1 is None else br_ref[0, d1]
                tile = jnp.where(col_ok, jnp.where(first_row, t0, t1), NA_MASKED)
                tab_ref[0, case, i * GRID_W:(i + 1) * GRID_W, :] = tile

    return pl.pallas_call(
        body, name="na_bias_table", grid=(NA_HEADS,),
        in_specs=[pl.BlockSpec((1, 2 * NA_ROWS - 1, GRID_W, NA_PAIR), lambda h: (h, 0, 0, 0))],
        out_specs=pl.BlockSpec((1, NA_CASES, NA_WIN, NA_PAIR), lambda h: (h, 0, 0, 0)),
        out_shape=jax.ShapeDtypeStruct((NA_HEADS, NA_CASES, NA_WIN, NA_PAIR), F32),
        compiler_params=_cparams(("arbitrary",)),
    )(bias_rows)


def _na_bias_table_bwd(d_tab, rows):
    geom = _na_row_offsets(rows)

    def body(dt_ref, dbr_ref):
        col_ok, first_row = _na_col_window()
        acc = [None] * (2 * NA_ROWS - 1)
        for case in range(NA_CASES):
            for i in range(NA_WIN_ROWS):
                tile = jnp.where(col_ok, dt_ref[0, case, i * GRID_W:(i + 1) * GRID_W, :], 0.0)
                for rr, d in enumerate(geom[case][i]):
                    if d is not None:
                        part = jnp.where(first_row if rr == 0 else ~first_row, tile, 0.0)
                        acc[d] = part if acc[d] is None else acc[d] + part
        for d, a in enumerate(acc):
            dbr_ref[0, d] = jnp.zeros((GRID_W, NA_PAIR), F32) if a is None else a

    return pl.pallas_call(
        body, name="na_bias_table_bwd", grid=(NA_HEADS,),
        in_specs=[pl.BlockSpec((1, NA_CASES, NA_WIN, NA_PAIR), lambda h: (h, 0, 0, 0))],
        out_specs=pl.BlockSpec((1, 2 * NA_ROWS - 1, GRID_W, NA_PAIR), lambda h: (h, 0, 0, 0)),
        out_shape=jax.ShapeDtypeStruct((NA_HEADS, 2 * NA_ROWS - 1, GRID_W, NA_PAIR), F32),
        compiler_params=_cparams(("arbitrary",)),
    )(d_tab)


NA_BLK = 64


def _na_blocks():
    return [slice(i * NA_BLK, (i + 1) * NA_BLK) for i in range(NA_WIN // NA_BLK)]


def _na_softmax(qk, bias_ref, hh, case):
    m = jnp.full((NA_BLK, NA_PAIR), -jnp.inf, F32)
    scores = []
    for blk in _na_blocks():
        s = qk[blk, :] + bias_ref[hh, case, blk, :]
        scores.append(s)
        m = jnp.maximum(m, s)
    m = jnp.max(m, axis=0, keepdims=True)
    l = jnp.zeros((NA_BLK, NA_PAIR), F32)
    exps = []
    for s in scores:
        e = jnp.exp(s - m)
        exps.append(e)
        l = l + e
    return exps, jnp.sum(l, axis=0, keepdims=True)


def _na_units(step, rows):
    units = []
    for pp in range(NA_PAIRS_PER_STEP):
        ws, case = _na_pair_window(step * NA_PAIRS_PER_STEP + pp, rows)
        win = pl.ds(pl.multiple_of(ws * GRID_W, NA_PAIR), NA_WIN)
        lanes = slice(pp * NA_PAIR, (pp + 1) * NA_PAIR)
        for hh in range(2):
            units.append((pp, hh, case, win, lanes, slice(hh * NA_HEAD_DIM, (hh + 1) * NA_HEAD_DIM)))
    return units


def _na_pipeline(n, before, middle, after, lookahead=2):
    for u in range(min(lookahead, n)):
        for f in before:
            f(u)
    for u in range(n):
        middle(u)
        if u + lookahead < n:
            for f in before:
                f(u + lookahead)
        for f in after:
            f(u)


def _head_rows(t, hh):
    row_head = lax.broadcasted_iota(jnp.int32, t.shape, 0) // NA_HEAD_DIM
    return jnp.where(row_head == hh, t, jnp.zeros_like(t))


def _heads_block_diag(t):
    lane_head = lax.broadcasted_iota(jnp.int32, t.shape, 1) // NA_HEAD_DIM
    zero = jnp.zeros_like(t)
    return jnp.concatenate([jnp.where(lane_head == 0, t, zero), jnp.where(lane_head == 1, t, zero)], axis=0)


def _na_fwd(q_t, k, v_t, bias_tab):
    L = k.shape[0]
    rows = L // GRID_W
    step_w = NA_PAIRS_PER_STEP * NA_PAIR

    def body(q_ref, k_ref, v_ref, bt_ref, o_ref):
        units = _na_units(pl.program_id(1), rows)
        qk, probs = {}, {}

        def scores(u):
            _, hh, _, win, lanes, _ = units[u]
            qk[u] = _dot(k_ref[win, :], _head_rows(q_ref[:, lanes], hh))

        def softmax(u):
            _, hh, case, _, _, _ = units[u]
            exps, l = _na_softmax(qk.pop(u), bt_ref, hh, case)
            probs[u] = jnp.concatenate([t.astype(BF16) for t in exps], axis=0), l

        def output(u):
            _, _, _, win, lanes, hrows = units[u]
            e, l = probs.pop(u)
            o_ref[hrows, lanes] = _dot(v_ref[hrows, win], e) / l

        _na_pipeline(len(units), [scores], softmax, [output])

    q_spec = pl.BlockSpec((NA_PAIR, step_w), lambda h, s: (h, s))
    return pl.pallas_call(
        body, name="na_fwd", grid=(NA_HEADS // 2, L // step_w),
        in_specs=[q_spec, pl.BlockSpec((L, NA_PAIR), lambda h, s: (0, h)),
                  pl.BlockSpec((NA_PAIR, L), lambda h, s: (h, 0)),
                  pl.BlockSpec((2, NA_CASES, NA_WIN, NA_PAIR), lambda h, s: (h, 0, 0, 0))],
        out_specs=q_spec,
        out_shape=jax.ShapeDtypeStruct((D_NA, L), F32),
        compiler_params=_cparams(("arbitrary", "arbitrary")),
    )(q_t, k, v_t, bias_tab)


def _na_bwd(q_t, q, k_t, k, v, bias_tab, out_t, d_out_t, d_out):
    L = k.shape[0]
    rows = L // GRID_W
    step_w = NA_PAIRS_PER_STEP * NA_PAIR

    def body(qt_ref, q_ref, kt_ref, k_ref, v_ref, bt_ref, ot_ref, dot_ref, do_ref, dq_ref, dk_ref, dv_ref, dbt_ref):
        @pl.when(pl.program_id(1) == 0)
        def _():
            dk_ref[...] = jnp.zeros_like(dk_ref)
            dv_ref[...] = jnp.zeros_like(dv_ref)
            dbt_ref[...] = jnp.zeros_like(dbt_ref)

        units = _na_units(pl.program_id(1), rows)
        qk, dp, dsb, pb = {}, {}, {}, {}

        def scores(u):
            _, hh, _, win, lanes, _ = units[u]
            qk[u] = _dot(k_ref[win, :], _head_rows(qt_ref[:, lanes], hh))

        def d_probs(u):
            _, hh, _, win, lanes, _ = units[u]
            dp[u] = _dot(v_ref[win, :], _head_rows(dot_ref[:, lanes].astype(BF16), hh))

        def softmax_bwd(u):
            _, hh, case, _, lanes, hrows = units[u]
            exps, l = _na_softmax(qk.pop(u), bt_ref, hh, case)
            inv_l = 1.0 / l
            delta = jnp.sum(dot_ref[hrows, lanes] * ot_ref[hrows, lanes], axis=0, keepdims=True)
            d_p = dp.pop(u)
            ds_blocks, p_blocks = [], []
            for blk, e in zip(_na_blocks(), exps):
                p = e * inv_l
                ds = p * (d_p[blk, :] - delta)
                dbt_ref[hh, case, blk, :] += ds
                ds_blocks.append(ds.astype(BF16))
                p_blocks.append(p.astype(BF16))
            dsb[u] = jnp.concatenate(ds_blocks, axis=0)
            pb[u] = jnp.concatenate(p_blocks, axis=0)

        def d_query(u):
            _, _, _, win, lanes, hrows = units[u]
            dq_ref[hrows, lanes] = _dot(kt_ref[hrows, win], dsb[u]) * (NA_HEAD_DIM ** -0.5)

        def d_keys_values(u):
            pp, hh, _, win, _, _ = units[u]
            if hh == 1:
                tokens = slice(pp * NA_PAIR, (pp + 1) * NA_PAIR)
                dk_ref[win, :] += _dot(jnp.concatenate([dsb.pop(u - 1), dsb.pop(u)], axis=1), _heads_block_diag(q_ref[tokens, :]))
                dv_ref[win, :] += _dot(jnp.concatenate([pb.pop(u - 1), pb.pop(u)], axis=1), _heads_block_diag(do_ref[tokens, :]))

        _na_pipeline(len(units), [scores, d_probs], softmax_bwd, [d_query, d_keys_values])

    t_tile = pl.BlockSpec((NA_PAIR, step_w), lambda h, s: (h, s))
    tile = pl.BlockSpec((step_w, NA_PAIR), lambda h, s: (s, h))
    t_full = pl.BlockSpec((NA_PAIR, L), lambda h, s: (h, 0))
    full = pl.BlockSpec((L, NA_PAIR), lambda h, s: (0, h))
    bt = pl.BlockSpec((2, NA_CASES, NA_WIN, NA_PAIR), lambda h, s: (h, 0, 0, 0))
    tok = jax.ShapeDtypeStruct((L, D_NA), F32)
    return pl.pallas_call(
        body, name="na_bwd", grid=(NA_HEADS // 2, L // step_w),
        in_specs=[t_tile, tile, t_full, full, full, bt, t_tile, t_tile, tile],
        out_specs=[t_tile, full, full, bt],
        out_shape=[jax.ShapeDtypeStruct((D_NA, L), F32), tok, tok, jax.ShapeDtypeStruct(bias_tab.shape, F32)],
        compiler_params=_cparams(("arbitrary", "arbitrary")),
    )(q_t, q, k_t, k, v, bias_tab, out_t, d_out_t, d_out)


def _branch_fwd_values(ys, zs, yn, zn, wglu, bglu):
    g1, t = _gelu_parts(ys)
    lin = _dot(g1.astype(BF16), wglu) + bglu
    sg = _sigmoid(lin)
    ys2 = g1 * sg
    sz, szs = _silu_parts(zs)
    sn, sns = _silu_parts(zn)
    return g1, t, sg, ys2, sz, szs, sn, sns


def _branch_fwd(y_ssm_c, z_s, y_na_t, z_n, w_glu, b_glu, tm=512):
    L = z_s.shape[0]

    def body(ys_ref, zs_ref, yn_ref, zn_ref, w_ref, b_ref, cat_ref, scr):
        yn = yn_ref[...].T
        g1, t, sg, ys2, sz, szs, sn, sns = _branch_fwd_values(
            _load_chunks(ys_ref, scr), zs_ref[...], yn, zn_ref[...], w_ref[...], b_ref[...])
        cat_ref[:, 0:512] = (ys2 * sz).astype(BF16)
        cat_ref[:, 512:1024] = (yn * sn).astype(BF16)

    tile = pl.BlockSpec((tm, 512), lambda i: (i, 0))
    return pl.pallas_call(
        body, name="branch_fwd", grid=(L // tm,),
        in_specs=[_chunk_spec(tm), tile, _heads_t_spec(tm), tile, pl.BlockSpec((512, 512), lambda i: (0, 0)),
                  pl.BlockSpec((1, 512), lambda i: (0, 0))],
        out_specs=pl.BlockSpec((tm, 1024), lambda i: (i, 0)),
        out_shape=jax.ShapeDtypeStruct((L, 1024), BF16),
        scratch_shapes=[_chunk_scratch(tm)],
        compiler_params=_cparams(("arbitrary",)),
    )(y_ssm_c, z_s, y_na_t, z_n, w_glu, b_glu)


def _branch_bwd(y_ssm_c, z_s, y_na_t, z_n, w_glu, b_glu, d_cat, tm=512):
    L = z_s.shape[0]

    def body(ys_ref, zs_ref, yn_ref, zn_ref, w_ref, b_ref, dc_ref,
             dys_ref, dzs_ref, dynt_ref, dyn_ref, dzn_ref, dw_ref, db_ref, scr):
        @pl.when(pl.program_id(0) == 0)
        def _():
            dw_ref[...] = jnp.zeros_like(dw_ref)
            db_ref[...] = jnp.zeros_like(db_ref)

        ys, zs, yn, zn = _load_chunks(ys_ref, scr), zs_ref[...], yn_ref[...].T, zn_ref[...]
        w = w_ref[...]
        g1, t, sg, ys2, sz, szs, sn, sns = _branch_fwd_values(ys, zs, yn, zn, w, b_ref[...])
        dys3 = dc_ref[:, 0:512]
        dyn2 = dc_ref[:, 512:1024]
        dzs_ref[...] = (dys3 * ys2 * _silu_grad(zs, szs)).astype(BF16)
        dys2 = dys3 * sz
        dlin = dys2 * g1 * sg * (1.0 - sg)
        dlb = dlin.astype(BF16)
        db_ref[...] += jnp.sum(dlin, axis=0, keepdims=True)
        dw_ref[...] += _dot_tn(g1.astype(BF16), dlb)
        dg1 = dys2 * sg + _dot_nt(dlb, w)
        _store_chunks(dg1 * _gelu_grad(ys, t), scr, dys_ref, BF16)
        dyn = dyn2 * sn
        dynt_ref[...] = dyn.T
        dyn_ref[...] = dyn.astype(BF16)
        dzn_ref[...] = (dyn2 * yn * _silu_grad(zn, sns)).astype(BF16)

    tile = pl.BlockSpec((tm, 512), lambda i: (i, 0))
    wspec = pl.BlockSpec((512, 512), lambda i: (0, 0))
    bspec = pl.BlockSpec((1, 512), lambda i: (0, 0))
    tok = jax.ShapeDtypeStruct((L, 512), BF16)
    return pl.pallas_call(
        body, name="branch_bwd", grid=(L // tm,),
        in_specs=[_chunk_spec(tm), tile, _heads_t_spec(tm), tile, wspec, bspec, pl.BlockSpec((tm, 1024), lambda i: (i, 0))],
        out_specs=[_chunk_spec(tm), tile, _heads_t_spec(tm), tile, tile, wspec, bspec],
        out_shape=[jax.ShapeDtypeStruct((N_BLOCKS, L // CHUNK, CHUNK_W), BF16), tok, jax.ShapeDtypeStruct((D_NA, L), F32),
                   tok, tok,
                   jax.ShapeDtypeStruct((512, 512), F32), jax.ShapeDtypeStruct((1, 512), F32)],
        scratch_shapes=[_chunk_scratch(tm)],
        compiler_params=_cparams(("arbitrary",)),
    )(y_ssm_c, z_s, y_na_t, z_n, w_glu, b_glu, d_cat)


def _head(x, p, target, cat, w_out, g_post, w_ple_g, g_ple, w_pg, tm=512):
    L = x.shape[0]
    pw = w_ple_g.shape[2]

    def body(x_ref, p_ref, t_ref, cat_ref, wo_ref, gpo_ref, wp_ref, gpl_ref, wg_ref,
             loss_ref, dh1_ref, dcat_ref, dwo_ref, dgpo_ref, dwp_ref, dgpl_ref, dwg_ref):
        @pl.when(pl.program_id(0) == 0)
        def _():
            for r in (loss_ref, dwo_ref, dgpo_ref, dwp_ref, dgpl_ref, dwg_ref):
                r[...] = jnp.zeros_like(r)

        cat_b = cat_ref[...]
        wo, wg = wo_ref[...], wg_ref[...]
        g_po, g_pl = gpo_ref[...], gpl_ref[...]
        mix = _dot(cat_b, wo)
        nm, r2 = _rms(mix)
        h1 = x_ref[...] + nm * g_po
        p_b = p_ref[...].astype(BF16)
        ep = jnp.concatenate([_dot(p_b, wp_ref[j]) for j in range(N_CHIPS)], axis=1)
        ne, r3 = _rms(ep)
        e = ne * g_pl
        h1_b = h1.astype(BF16)
        gate = _sigmoid(_dot(h1_b, wg))
        h2 = h1 + gate * e
        diff = h2 - t_ref[...]
        loss_ref[...] += (0.5 / D_MODEL) * jnp.sum(diff * diff).reshape(1, 1)

        dh2 = diff * (1.0 / D_MODEL)
        de = dh2 * gate
        dgl = (dh2 * e * gate * (1.0 - gate)).astype(BF16)
        dwg_ref[...] += _dot_tn(h1_b, dgl)
        dh1 = dh2 + _dot_nt(dgl, wg)
        dgpl_ref[...] += jnp.sum(de * ne, axis=0, keepdims=True)
        dep = _rms_bwd(de * g_pl, ne, r3).astype(BF16)
        for j in range(N_CHIPS):
            dwp_ref[j] += _dot_tn(p_b, dep[:, j * pw:(j + 1) * pw])
        dgpo_ref[...] += jnp.sum(dh1 * nm, axis=0, keepdims=True)
        dmix = _rms_bwd(dh1 * g_po, nm, r2).astype(BF16)
        dwo_ref[...] += _dot_tn(cat_b, dmix)
        dcat_ref[...] = _dot_nt(dmix, wo)
        dh1_ref[...] = dh1

    tile = lambda w: pl.BlockSpec((tm, w), lambda i: (i, 0))
    const = _resident
    sds = jax.ShapeDtypeStruct
    return pl.pallas_call(
        body, name="head", grid=(L // tm,),
        in_specs=[tile(D_MODEL), tile(D_PLE), tile(D_MODEL), tile(1024), const(1024, D_MODEL), const(1, D_MODEL),
                  const(N_CHIPS, D_PLE, pw), const(1, D_MODEL), const(D_MODEL, D_MODEL)],
        out_specs=[const(1, 1), tile(D_MODEL), tile(1024), const(1024, D_MODEL), const(1, D_MODEL),
                   const(N_CHIPS, D_PLE, pw), const(1, D_MODEL), const(D_MODEL, D_MODEL)],
        out_shape=[sds((1, 1), F32), sds((L, D_MODEL), F32), sds((L, 1024), F32), sds((1024, D_MODEL), F32),
                   sds((1, D_MODEL), F32), sds((N_CHIPS, D_PLE, pw), F32), sds((1, D_MODEL), F32),
                   sds((D_MODEL, D_MODEL), F32)],
        compiler_params=_cparams(("arbitrary",)),
    )(x, p, target, cat, w_out, g_post, w_ple_g, g_ple, w_pg)


def _dproj_specs(tm):
    tile = pl.BlockSpec((tm, 512), lambda i: (i, 0))
    return [_chunk_spec(tm), tile, _heads_t_spec(tm), tile, tile, tile]


def _dproj_tile(refs, scr):
    du_ref, dzs_ref, dqt_ref, dk_ref, dv_ref, dzn_ref = refs
    parts = [_load_chunks(du_ref, scr), dzs_ref[...], dqt_ref[...].T, dk_ref[...], dv_ref[...], dzn_ref[...]]
    return jnp.concatenate([t.astype(BF16) for t in parts], axis=1)


def _in_proj_bwd_w(x, g_pre, dparts, tm=512):
    L = x.shape[0]
    wn = D_IN_PROJ // N_CHIPS

    def body(x_ref, g_ref, *refs):
        dw_ref, scr = refs[-2], refs[-1]

        @pl.when(pl.program_id(0) == 0)
        def _():
            dw_ref[...] = jnp.zeros_like(dw_ref)

        n, _ = _rms(x_ref[...])
        hn = (n * g_ref[...]).astype(BF16)
        dproj = _dproj_tile(refs[:-2], scr)
        for j in range(N_CHIPS):
            dw_ref[j] += _dot_tn(hn, dproj[:, j * wn:(j + 1) * wn])

    return pl.pallas_call(
        body, name="in_proj_bwd_w", grid=(L // tm,),
        in_specs=[pl.BlockSpec((tm, D_MODEL), lambda i: (i, 0)), _resident(1, D_MODEL)] + _dproj_specs(tm),
        out_specs=_resident(N_CHIPS, D_MODEL, wn),
        out_shape=jax.ShapeDtypeStruct((N_CHIPS, D_MODEL, wn), F32),
        scratch_shapes=[_chunk_scratch(tm)],
        compiler_params=_cparams(("arbitrary",)),
    )(x, g_pre, *dparts)


def _in_proj_bwd_x(x, g_pre, w_in_g, d_h1, dparts, tm=512):
    L = x.shape[0]
    wn = w_in_g.shape[2]

    def body(x_ref, g_ref, w_ref, dh1_ref, *refs):
        dx_ref, dg_ref, scr = refs[-3], refs[-2], refs[-1]

        @pl.when(pl.program_id(0) == 0)
        def _():
            dg_ref[...] = jnp.zeros_like(dg_ref)

        n, r = _rms(x_ref[...])
        dproj = _dproj_tile(refs[:-3], scr)
        dhn = _dot_nt(dproj[:, 0:wn], w_ref[0])
        for j in range(1, N_CHIPS):
            dhn = dhn + _dot_nt(dproj[:, j * wn:(j + 1) * wn], w_ref[j])
        dg_ref[...] += jnp.sum(dhn * n, axis=0, keepdims=True)
        dx_ref[...] = dh1_ref[...] + _rms_bwd(dhn * g_ref[...], n, r)

    wide = pl.BlockSpec((tm, D_MODEL), lambda i: (i, 0))
    vec = _resident(1, D_MODEL)
    return pl.pallas_call(
        body, name="in_proj_bwd_x", grid=(L // tm,),
        in_specs=[wide, vec, _resident(N_CHIPS, D_MODEL, wn), wide] + _dproj_specs(tm),
        out_specs=[wide, vec],
        out_shape=[jax.ShapeDtypeStruct((L, D_MODEL), F32), jax.ShapeDtypeStruct((1, D_MODEL), F32)],
        scratch_shapes=[_chunk_scratch(tm)],
        compiler_params=_cparams(("arbitrary",)),
    )(x, g_pre, w_in_g, d_h1, *dparts)


def _mesh_position():
    x, y, c = lax.axis_index("x"), lax.axis_index("y"), lax.axis_index("c")
    chips = [(1 - x, y), (x, 1 - y), (1 - x, 1 - y)]
    return x, y, c, chips


def _chip_index(cx, cy):
    return 2 * cx + cy


def _hbm_specs(n):
    return [pl.BlockSpec(memory_space=pl.ANY)] * n


def _gather_chips(shards, name):
    n = len(shards)

    def body(*refs):
        gather = _ChipGather(refs[:n], refs[n:2 * n], refs[2 * n:])
        gather.start()
        gather.forward()
        gather.finish()

    return pl.pallas_call(
        body, name=name, in_specs=_hbm_specs(n), out_specs=_hbm_specs(n),
        out_shape=_gather_out_shapes(shards), scratch_shapes=_gather_semaphores(n),
        compiler_params=pltpu.CompilerParams(has_side_effects=True),
    )(*shards)


def _gather_out_shapes(shards):
    return [jax.ShapeDtypeStruct((N_CHIPS,) + s.shape, s.dtype) for s in shards]


def _gather_semaphores(n):
    sem = pltpu.SemaphoreType.DMA
    return [sem((n, 3)), sem((n, 3)), sem((n, 3)), sem((n, 3)), sem((n,)), sem((n,))]


class _ChipGather:
    def __init__(self, ins, outs, sems):
        self.ins, self.outs = ins, outs
        self.send1, self.recv1, self.send2, self.recv2, self.send3, self.recv3 = sems
        self.x, self.y, self.c, self.chips = _mesh_position()
        self.me = _chip_index(self.x, self.y)
        self.sibling = (self.x, self.y, 1 - self.c)

    def _half(self, a, chip, core):
        hr = self.outs[a].shape[1] // 2
        return self.outs[a].at[chip, pl.ds(core * hr, hr)]

    def _own(self, a):
        return pltpu.make_async_remote_copy(
            src_ref=self.ins[a], dst_ref=self.outs[a].at[self.me], send_sem=self.send3.at[a], recv_sem=self.recv3.at[a],
            device_id=self.sibling, device_id_type=MESH)

    def _to_chip(self, a, j):
        hr = self.ins[a].shape[0] // 2
        return pltpu.make_async_remote_copy(
            src_ref=self.ins[a].at[pl.ds(self.c * hr, hr)], dst_ref=self._half(a, self.me, self.c),
            send_sem=self.send1.at[a, j], recv_sem=self.recv1.at[a, j], device_id=(*self.chips[j], self.c), device_id_type=MESH)

    def _from_chip(self, a, j):
        landed = self._half(a, _chip_index(*self.chips[j]), self.c)
        return pltpu.make_async_remote_copy(
            src_ref=landed, dst_ref=landed, send_sem=self.send1.at[a, j], recv_sem=self.recv1.at[a, j],
            device_id=(*self.chips[j], self.c), device_id_type=MESH)

    def _to_sibling(self, a, j, core):
        part = self._half(a, _chip_index(*self.chips[j]), core)
        return pltpu.make_async_remote_copy(
            src_ref=part, dst_ref=part, send_sem=self.send2.at[a, j], recv_sem=self.recv2.at[a, j],
            device_id=self.sibling, device_id_type=MESH)

    def _each(self):
        return [(a, j) for a in range(len(self.ins)) for j in range(3)]

    def start(self):
        for a in range(len(self.ins)):
            self._own(a).start()
        for a, j in self._each():
            self._to_chip(a, j).start()

    def forward(self):
        for a, j in self._each():
            self._from_chip(a, j).wait_recv()
            self._to_sibling(a, j, self.c).start()

    def finish(self):
        for a, j in self._each():
            self._to_sibling(a, j, 1 - self.c).wait_recv()
        for a, j in self._each():
            self._to_chip(a, j).wait_send()
            self._to_sibling(a, j, self.c).wait_send()
        for a in range(len(self.ins)):
            self._own(a).wait()


def _pair_exchange(grads):
    n = len(grads)

    def body(*refs):
        ins, outs = refs[:n], refs[n:2 * n]
        send, recv = refs[2 * n:]
        x, y, c, _ = _mesh_position()
        copies = []
        for a in range(n):
            hr = ins[a].shape[1] // 2
            cp = pltpu.make_async_remote_copy(
                src_ref=ins[a].at[:, pl.ds((1 - c) * hr, hr)], dst_ref=outs[a],
                send_sem=send.at[a], recv_sem=recv.at[a], device_id=(x, y, 1 - c), device_id_type=MESH)
            cp.start()
            copies.append(cp)
        for cp in copies:
            cp.wait()

    sem = pltpu.SemaphoreType.DMA
    return pl.pallas_call(
        body, name="pair_exchange", in_specs=_hbm_specs(n), out_specs=_hbm_specs(n),
        out_shape=[jax.ShapeDtypeStruct((g.shape[0], g.shape[1] // 2, g.shape[2]), g.dtype) for g in grads],
        scratch_shapes=[sem((n,)), sem((n,))],
        compiler_params=pltpu.CompilerParams(has_side_effects=True),
    )(*grads)


def _pair_add(core, grad, other, tr, out_dtype):
    hr = other.shape[1]
    cdim = other.shape[2]
    nb = hr // tr

    def body(core_ref, g_ref, o_ref, out_ref):
        out_ref[...] = (g_ref[...] + o_ref[...]).astype(out_dtype)

    return pl.pallas_call(
        body, name="pair_add",
        grid_spec=pltpu.PrefetchScalarGridSpec(
            num_scalar_prefetch=1, grid=(N_CHIPS, nb),
            in_specs=[pl.BlockSpec((1, tr, cdim), lambda j, i, core_ref: (j, core_ref[0] * nb + i, 0)),
                      pl.BlockSpec((1, tr, cdim), lambda j, i, core_ref: (j, i, 0))],
            out_specs=pl.BlockSpec((1, tr, cdim), lambda j, i, core_ref: (j, i, 0))),
        out_shape=jax.ShapeDtypeStruct(other.shape, out_dtype),
        compiler_params=_cparams(("arbitrary", "arbitrary")),
    )(core, grad, other)


def _chip_scatter(parts):
    n = len(parts)

    def body(*refs):
        ins, outs = refs[:n], refs[n:2 * n]
        send, recv, load_sem, store_sem = refs[2 * n:2 * n + 4]
        staged = refs[2 * n + 4:]
        x, y, c, chips = _mesh_position()
        me = _chip_index(x, y)
        copies, loads = [], []
        for a in range(n):
            ld = pltpu.make_async_copy(ins[a].at[me], staged[a], load_sem.at[a])
            ld.start()
            loads.append(ld)
            for j, chip in enumerate(chips):
                cp = pltpu.make_async_remote_copy(
                    src_ref=ins[a].at[_chip_index(*chip)], dst_ref=outs[a].at[me],
                    send_sem=send.at[a, j], recv_sem=recv.at[a, j], device_id=(*chip, c), device_id_type=MESH)
                cp.start()
                copies.append(cp)
        for a in range(n):
            loads[a].wait()
            st = pltpu.make_async_copy(staged[a], outs[a].at[me], store_sem.at[a])
            st.start()
            copies.append(st)
        for cp in copies:
            cp.wait()

    sem = pltpu.SemaphoreType.DMA
    return pl.pallas_call(
        body, name="chip_scatter", in_specs=_hbm_specs(n), out_specs=_hbm_specs(n),
        out_shape=[jax.ShapeDtypeStruct(p.shape, p.dtype) for p in parts],
        scratch_shapes=[sem((n, 3)), sem((n, 3)), sem((n,)), sem((n,))] + [pltpu.VMEM(p.shape[1:], p.dtype) for p in parts],
        compiler_params=pltpu.CompilerParams(has_side_effects=True),
    )(*parts)


def _chip_add(core, recv, tr):
    hr, cdim = recv.shape[1], recv.shape[2]
    nb = hr // tr

    def body(core_ref, r_ref, out_ref):
        out_ref[...] = ((r_ref[0].astype(F32) + r_ref[1].astype(F32)) + r_ref[2].astype(F32)) + r_ref[3].astype(F32)

    return pl.pallas_call(
        body, name="chip_add",
        grid_spec=pltpu.PrefetchScalarGridSpec(
            num_scalar_prefetch=1, grid=(nb,),
            in_specs=[pl.BlockSpec((N_CHIPS, tr, cdim), lambda i, core_ref: (0, i, 0))],
            out_specs=pl.BlockSpec((tr, cdim), lambda i, core_ref: (core_ref[0] * nb + i, 0))),
        out_shape=jax.ShapeDtypeStruct((2 * hr, cdim), F32),
        compiler_params=_cparams(("arbitrary",)),
    )(core, recv)


def _pair_gather(fulls):
    n = len(fulls)

    def body(*refs):
        outs = refs[n:2 * n]
        send, recv = refs[2 * n:]
        x, y, c, _ = _mesh_position()
        copies = []
        for a in range(n):
            hr = outs[a].shape[0] // 2
            mine = outs[a].at[pl.ds(c * hr, hr)]
            cp = pltpu.make_async_remote_copy(
                src_ref=mine, dst_ref=mine, send_sem=send.at[a], recv_sem=recv.at[a],
                device_id=(x, y, 1 - c), device_id_type=MESH)
            cp.start()
            copies.append(cp)
        for cp in copies:
            cp.wait()

    sem = pltpu.SemaphoreType.DMA
    return pl.pallas_call(
        body, name="pair_gather", in_specs=_hbm_specs(n), out_specs=_hbm_specs(n),
        out_shape=[jax.ShapeDtypeStruct(f.shape, f.dtype) for f in fulls],
        input_output_aliases={a: a for a in range(n)},
        scratch_shapes=[sem((n,)), sem((n,))],
        compiler_params=pltpu.CompilerParams(has_side_effects=True),
    )(*fulls)


def _row_tile(rows):
    for t in (512, 256, 128, 64, 32, 16, 8):
        if rows % t == 0:
            return t
    raise ValueError(rows)


def _reduce_scatter(grads, ici_dtypes):
    core = lax.axis_index("c").astype(jnp.int32).reshape(1)
    others = _pair_exchange(grads)
    pair = [_pair_add(core, g, o, _row_tile(o.shape[1]), dt) for g, o, dt in zip(grads, others, ici_dtypes)]
    landed = _chip_scatter(pair)
    return _pair_gather([_chip_add(core, r, _row_tile(r.shape[1])) for r in landed])


def _adamw(w, g, m, v):
    rows, cols = w.shape
    one_block = rows % 8 != 0 or rows * max(cols, 128) * 4 <= (1 << 20)
    tr = rows if one_block else _row_tile(rows)

    def body(w_ref, g_ref, m_ref, v_ref, d_ref, nm_ref, nv_ref):
        g_ = g_ref[...]
        m_ = ADAM_B1 * m_ref[...] + (1.0 - ADAM_B1) * g_
        v_ = ADAM_B2 * v_ref[...] + (1.0 - ADAM_B2) * (g_ * g_)
        m_hat = m_ / (1.0 - ADAM_B1 ** ADAM_STEP)
        v_hat = v_ / (1.0 - ADAM_B2 ** ADAM_STEP)
        d_ref[...] = -ADAM_LR * (m_hat / (jnp.sqrt(v_hat) + ADAM_EPS) + ADAM_WD * w_ref[...])
        nm_ref[...] = m_
        nv_ref[...] = v_

    spec = pl.BlockSpec((tr, cols), lambda i: (i, 0))
    shp = jax.ShapeDtypeStruct((rows, cols), F32)
    return pl.pallas_call(
        body, name="adamw", grid=(rows // tr,), in_specs=[spec] * 4, out_specs=[spec] * 3,
        out_shape=[shp] * 3, compiler_params=_cparams(("arbitrary",)),
    )(w, g, m, v)


_SMALL = ["norm_pre", "norm_post", "ssm_a_re", "ssm_a_im", "ssm_log_dt", "ssm_b_re", "ssm_b_im",
          "ssm_c_re", "ssm_c_im", "ssm_d", "b_glu", "na_rpb", "ple_norm"]
_BIG = ["w_in", "w_glu", "w_out", "w_ple", "w_ple_gate"]
_WEIGHTS = ["norm_pre", "norm_post", "w_in", "ssm_a_re", "ssm_a_im", "ssm_log_dt", "ssm_b_re", "ssm_b_im",
            "ssm_c_re", "ssm_c_im", "ssm_d", "w_glu", "b_glu", "na_rpb", "w_out", "w_ple", "ple_norm", "w_ple_gate"]
_SMALL_ROWS = 2176


def _pack_small(tensors, tail=None):
    parts = [tensors[n].reshape(-1) for n in _SMALL] + ([] if tail is None else [tail.reshape(-1)])
    flat = jnp.concatenate(parts)
    flat = jnp.pad(flat, (0, _SMALL_ROWS * 128 - flat.shape[0]))
    return flat.reshape(_SMALL_ROWS, 128)


def _unpack_small(packed, shapes):
    flat = packed.reshape(-1)
    out, off = {}, 0
    for n in _SMALL:
        size = int(np.prod(shapes[n]))
        out[n] = flat[off:off + size].reshape(shapes[n])
        off += size
    return out


def _local_grads(x, p, target, wts):
    ssm_names = ["ssm_a_re", "ssm_a_im", "ssm_log_dt", "ssm_b_re", "ssm_b_im", "ssm_c_re", "ssm_c_im", "ssm_d"]
    ssm_params = [wts[n][0] for n in ssm_names]
    blk, blk_vjp = jax.vjp(_ssm_block_params, *ssm_params)
    (m_mat, ws_mat, wot_mat, a16), gathered = _ssm_chunk_matrices(blk, [wts[n][0].astype(BF16) for n in _BIG])
    w_in_g, w_ple_g = gathered[0], gathered[3]
    w_glu, w_out, w_pg = gathered[1].reshape(512, 512), gathered[2].reshape(1024, 1024), gathered[4].reshape(1024, 1024)
    seq = x.shape[0]
    bias_rows, bias_rows_vjp = jax.vjp(_na_bias_rows, wts["na_rpb"][0])
    bias_tab = _na_bias_table(bias_rows, seq // GRID_W)

    u_c, z_s, q_t, q, k_t, k, v_t, v, z_n = _in_proj(x, wts["norm_pre"], w_in_g)
    s_in = _block_matmul([(u_c, ws_mat, False)], "ssm_chunk_states")
    s_prev = _ssm_state_scan(s_in, a16)
    y_ssm_c = _block_matmul([(u_c, m_mat, False), (s_prev, wot_mat, True)], "ssm_chunk_out")
    y_na_t = _na_fwd(q_t, k, v_t, bias_tab)
    cat = _branch_fwd(y_ssm_c, z_s, y_na_t, z_n, w_glu, wts["b_glu"])

    (loss, d_h1, d_cat, d_w_out, d_g_post, d_w_ple, d_g_ple, d_w_pg) = _head(
        x, p, target, cat, w_out, wts["norm_post"], w_ple_g, wts["ple_norm"], w_pg)
    dy_c, d_z_s, d_y_na_t, d_y_na, d_z_n, d_w_glu, d_b_glu = _branch_bwd(
        y_ssm_c, z_s, y_na_t, z_n, w_glu, wts["b_glu"], d_cat)
    d_q_t, d_k, d_v, d_bias_tab = _na_bwd(q_t, q, k_t, k, v, bias_tab, y_na_t, d_y_na_t, d_y_na)

    d_prev = _block_matmul([(dy_c, wot_mat, False)], "ssm_bwd_states")
    g_st, d_a16 = _ssm_state_scan_bwd(d_prev, s_prev, a16)
    d_u_c = _block_matmul([(dy_c, m_mat, True), (g_st, ws_mat, True)], "ssm_bwd_in", out_dtype=BF16)
    d_m = _block_matmul_tn(u_c, dy_c, "ssm_grad_m")
    d_ws = _block_matmul_tn(u_c, g_st, "ssm_grad_ws")
    d_wot = _block_matmul_tn(dy_c, s_prev, "ssm_grad_wot")
    d_ssm = blk_vjp(tuple(_ssm_chunk_matrices_bwd(blk, d_m, d_ws, d_wot, d_a16)))
    (d_rpb,) = bias_rows_vjp(_na_bias_table_bwd(d_bias_tab, seq // GRID_W))

    dparts = [d_u_c, d_z_s, d_q_t, d_k, d_v, d_z_n]
    d_w_in = _in_proj_bwd_w(x, wts["norm_pre"], dparts)
    grad_x, d_g_pre = _in_proj_bwd_x(x, wts["norm_pre"], w_in_g, d_h1, dparts)

    small = {"norm_pre": d_g_pre, "norm_post": d_g_post, "b_glu": d_b_glu, "na_rpb": d_rpb, "ple_norm": d_g_ple}
    for n, g in zip(ssm_names, d_ssm):
        small[n] = g
    big = {"w_in": d_w_in, "w_glu": d_w_glu.reshape(N_CHIPS, 128, 512), "w_out": d_w_out.reshape(N_CHIPS, 256, 1024),
           "w_ple": d_w_ple, "w_ple_gate": d_w_pg.reshape(N_CHIPS, 256, 1024)}
    return loss, grad_x, small, big


def kernel(x, p, norm_pre, norm_post, w_in, ssm_a_re, ssm_a_im, ssm_log_dt, ssm_b_re, ssm_b_im, ssm_c_re, ssm_c_im, ssm_d, w_glu, b_glu, na_rpb, w_out, w_ple, ple_norm, w_ple_gate, loss_target, m_norm_pre, m_norm_post, m_w_in, m_ssm_a_re, m_ssm_a_im, m_ssm_log_dt, m_ssm_b_re, m_ssm_b_im, m_ssm_c_re, m_ssm_c_im, m_ssm_d, m_w_glu, m_b_glu, m_na_rpb, m_w_out, m_w_ple, m_ple_norm, m_w_ple_gate, v_norm_pre, v_norm_post, v_w_in, v_ssm_a_re, v_ssm_a_im, v_ssm_log_dt, v_ssm_b_re, v_ssm_b_im, v_ssm_c_re, v_ssm_c_im, v_ssm_d, v_w_glu, v_b_glu, v_na_rpb, v_w_out, v_w_ple, v_ple_norm, v_w_ple_gate):
    wts = dict(norm_pre=norm_pre, norm_post=norm_post, w_in=w_in, ssm_a_re=ssm_a_re, ssm_a_im=ssm_a_im,
               ssm_log_dt=ssm_log_dt, ssm_b_re=ssm_b_re, ssm_b_im=ssm_b_im, ssm_c_re=ssm_c_re, ssm_c_im=ssm_c_im,
               ssm_d=ssm_d, w_glu=w_glu, b_glu=b_glu, na_rpb=na_rpb, w_out=w_out, w_ple=w_ple, ple_norm=ple_norm,
               w_ple_gate=w_ple_gate)
    mom_m = dict(norm_pre=m_norm_pre, norm_post=m_norm_post, w_in=m_w_in, ssm_a_re=m_ssm_a_re, ssm_a_im=m_ssm_a_im,
                 ssm_log_dt=m_ssm_log_dt, ssm_b_re=m_ssm_b_re, ssm_b_im=m_ssm_b_im, ssm_c_re=m_ssm_c_re,
                 ssm_c_im=m_ssm_c_im, ssm_d=m_ssm_d, w_glu=m_w_glu, b_glu=m_b_glu, na_rpb=m_na_rpb, w_out=m_w_out,
                 w_ple=m_w_ple, ple_norm=m_ple_norm, w_ple_gate=m_w_ple_gate)
    mom_v = dict(norm_pre=v_norm_pre, norm_post=v_norm_post, w_in=v_w_in, ssm_a_re=v_ssm_a_re, ssm_a_im=v_ssm_a_im,
                 ssm_log_dt=v_ssm_log_dt, ssm_b_re=v_ssm_b_re, ssm_b_im=v_ssm_b_im, ssm_c_re=v_ssm_c_re,
                 ssm_c_im=v_ssm_c_im, ssm_d=v_ssm_d, w_glu=v_w_glu, b_glu=v_b_glu, na_rpb=v_na_rpb, w_out=v_w_out,
                 w_ple=v_w_ple, ple_norm=v_ple_norm, w_ple_gate=v_w_ple_gate)

    loss_part, grad_x, small, big = _local_grads(x[0], p[0, 0], loss_target[0], wts)

    small_packed = _pack_small(small, tail=loss_part).reshape(N_CHIPS, _SMALL_ROWS // N_CHIPS, 128)
    reduced = _reduce_scatter([big[n] for n in _BIG] + [small_packed], [BF16] * len(_BIG) + [F32])
    grads = dict(zip(_BIG, reduced[:-1]))
    (small_all,) = _gather_chips([reduced[-1]], "gather_small_grads")
    small_all = small_all.reshape(_SMALL_ROWS, 128)
    loss = small_all.reshape(-1)[sum(int(np.prod(wts[n].shape)) for n in _SMALL)]

    delta, new_m, new_v = {}, {}, {}
    for n in _BIG:
        shp = wts[n].shape
        d_, m_, v_ = _adamw(wts[n][0], grads[n], mom_m[n][0], mom_v[n][0])
        grads[n] = grads[n].reshape(shp)
        delta[n], new_m[n], new_v[n] = d_.reshape(shp), m_.reshape(shp), v_.reshape(shp)
    grads.update(_unpack_small(small_all, {n: wts[n].shape for n in _SMALL}))
    for n in _SMALL:
        shp = wts[n].shape
        rows_cols = (int(np.prod(shp[:-1])), shp[-1])
        d_, m_, v_ = _adamw(*[t.reshape(rows_cols) for t in (wts[n], grads[n], mom_m[n], mom_v[n])])
        delta[n], new_m[n], new_v[n] = d_.reshape(shp), m_.reshape(shp), v_.reshape(shp)

    return (loss, grad_x[None], *[grads[n] for n in _WEIGHTS], *[delta[n] for n in _WEIGHTS],
            *[new_m[n] for n in _WEIGHTS], *[new_v[n] for n in _WEIGHTS])
```

```python
import functools
import math

import jax
import jax.numpy as jnp
import numpy as np
from jax import lax
from jax.experimental import pallas as pl
from jax.experimental.pallas import tpu as pltpu

F32 = jnp.float32
BF16 = jnp.bfloat16

D_MODEL = 1024
D_PLE = 256
GRID_W = 64
D_SSM = 512
SSM_GROUP = 16
N_GROUPS = 32
SSM_STATE = 64
D_NA = 512
NA_HEADS = 8
NA_HEAD_DIM = 64
NA_ROWS = 8
NA_COLS = 16
D_IN_PROJ = 3072
EPS = 1e-6

CHUNK = 16
GROUPS_PER_BLOCK = 8
N_BLOCKS = N_GROUPS // GROUPS_PER_BLOCK
BLOCK_CH = GROUPS_PER_BLOCK * SSM_GROUP
BLOCK_ST = GROUPS_PER_BLOCK * SSM_STATE
CHUNK_W = CHUNK * BLOCK_CH
STATE_W = 4 * BLOCK_ST

N_CHIPS = 4
MESH = pl.DeviceIdType.MESH

ADAM_LR = 0.001
ADAM_B1 = 0.9
ADAM_B2 = 0.999
ADAM_EPS = 1e-08
ADAM_WD = 0.01
ADAM_STEP = 10

VMEM_LIMIT = 52 * 1024 * 1024
HIGHEST = lax.Precision.HIGHEST


def _cparams(sem=None, **kw):
    if sem is not None:
        kw["dimension_semantics"] = sem
    return pltpu.CompilerParams(vmem_limit_bytes=VMEM_LIMIT, **kw)


def _resident(*shape):
    return pl.BlockSpec(shape, lambda *_: (0,) * len(shape), pipeline_mode=pl.Buffered(1))


def _dot(a, b, dims=((1,), (0,))):
    return lax.dot_general(a, b, (dims, ((), ())), preferred_element_type=F32)


def _dot_nt(a, b):
    return _dot(a, b, ((1,), (1,)))


def _dot_tn(a, b):
    return _dot(a, b, ((0,), (0,)))


def _sigmoid(x):
    return 1.0 / (1.0 + jnp.exp(-x))


_GELU_C = math.sqrt(2.0 / math.pi)


def _gelu_parts(x):
    inner = _GELU_C * (x + 0.044715 * (x * x * x))
    t = jnp.tanh(inner)
    return 0.5 * x * (1.0 + t), t


def _gelu_grad(x, t):
    return 0.5 * (1.0 + t) + 0.5 * x * (1.0 - t * t) * (_GELU_C * (1.0 + 3.0 * 0.044715 * x * x))


def _silu_parts(z):
    s = _sigmoid(z)
    return z * s, s


def _silu_grad(z, s):
    return s * (1.0 + z * (1.0 - s))


def _rms(x):
    r = lax.rsqrt(jnp.mean(x * x, axis=-1, keepdims=True) + EPS)
    return x * r, r


def _rms_bwd(dn, n, r):
    return r * (dn - n * jnp.mean(dn * n, axis=-1, keepdims=True))


def _chunk_scratch(tm):
    return pltpu.VMEM((N_BLOCKS, tm, BLOCK_CH), F32)


def _store_chunks(val, scr, c_ref, dtype):
    nc = scr.shape[1] // CHUNK
    for b in range(N_BLOCKS):
        scr[b] = val[:, b * BLOCK_CH:(b + 1) * BLOCK_CH]
        for j in range(CHUNK):
            c_ref[b, :, j * BLOCK_CH:(j + 1) * BLOCK_CH] = scr[b, pl.ds(j, nc, stride=CHUNK), :].astype(dtype)


def _load_chunks(c_ref, scr):
    nc = scr.shape[1] // CHUNK
    for b in range(N_BLOCKS):
        for j in range(CHUNK):
            scr[b, pl.ds(j, nc, stride=CHUNK), :] = c_ref[b, :, j * BLOCK_CH:(j + 1) * BLOCK_CH].astype(F32)
    return jnp.concatenate([scr[b] for b in range(N_BLOCKS)], axis=1)


def _chunk_spec(tm):
    return pl.BlockSpec((N_BLOCKS, tm // CHUNK, CHUNK_W), lambda i: (0, i, 0))


def _heads_t_spec(tm):
    return pl.BlockSpec((D_NA, tm), lambda i: (0, i))


def _in_proj(x, g_pre, w_in_g, shards, tm=512):
    L = x.shape[0]
    wn = w_in_g.shape[2]
    n_sh = len(shards)
    steps = L // tm

    def body(*refs):
        x_ref, g_ref, w_ref = refs[:3]
        uc_ref, zs_ref, qt_ref, q_ref, kt_ref, k_ref, vt_ref, v_ref, zn_ref = refs[3 + n_sh:12 + n_sh]
        u_scr = refs[12 + 2 * n_sh]
        gather = _ChipGather(refs[3:3 + n_sh], refs[12 + n_sh:12 + 2 * n_sh], refs[13 + 2 * n_sh:])
        step = pl.program_id(0)
        pl.when(step == 0)(gather.start)
        pl.when(step == steps // 2)(gather.forward)
        pl.when(step == steps - 1)(gather.finish)
        n, _ = _rms(x_ref[...])
        hn = (n * g_ref[...]).astype(BF16)
        proj = jnp.concatenate([_dot(hn, w_ref[j]) for j in range(N_CHIPS)], axis=1)
        _store_chunks(proj[:, 0:512], u_scr, uc_ref, BF16)
        zs_ref[...] = proj[:, 512:1024]
        q = proj[:, 1024:1536] * (NA_HEAD_DIM ** -0.5)
        for val, t_ref, n_ref in ((q, qt_ref, q_ref), (proj[:, 1536:2048], kt_ref, k_ref), (proj[:, 2048:2560], vt_ref, v_ref)):
            t_ref[...] = val.T.astype(BF16)
            n_ref[...] = val.astype(BF16)
        zn_ref[...] = proj[:, 2560:3072]

    tok = jax.ShapeDtypeStruct((L, 512), F32)
    tr = jax.ShapeDtypeStruct((D_NA, L), BF16)
    hm = jax.ShapeDtypeStruct((L, D_NA), BF16)
    tspec = pl.BlockSpec((tm, 512), lambda i: (i, 0))
    outs = pl.pallas_call(
        body, name="in_proj", grid=(steps,),
        in_specs=[pl.BlockSpec((tm, D_MODEL), lambda i: (i, 0)),
                  _resident(1, D_MODEL), _resident(N_CHIPS, D_MODEL, wn)] + _hbm_specs(n_sh),
        out_specs=[_chunk_spec(tm), tspec] + [_heads_t_spec(tm), tspec] * 3 + [tspec] + _hbm_specs(n_sh),
        out_shape=[jax.ShapeDtypeStruct((N_BLOCKS, L // CHUNK, CHUNK_W), BF16), tok, tr, hm, tr, hm, tr, hm, tok]
        + _gather_out_shapes(shards),
        scratch_shapes=[_chunk_scratch(tm)] + _gather_semaphores(n_sh),
        compiler_params=_cparams(("arbitrary",), has_side_effects=True),
    )(x, g_pre, w_in_g, *shards)
    return outs[:9], outs[9:]


def _ssm_block_params(a_re, a_im, log_dt, b_re, b_im, c_re, c_im, d):
    eye_g = jnp.eye(GROUPS_PER_BLOCK, dtype=F32)[None, None, :, None, :, None]

    def lanes(t):
        return t.reshape(2, N_BLOCKS, 1, BLOCK_ST)

    def expand(t):
        return (t[:, :, :, :, None, :] * eye_g).reshape(2, N_BLOCKS, BLOCK_CH, BLOCK_ST)

    b_shape = (2, N_BLOCKS, GROUPS_PER_BLOCK, SSM_STATE, SSM_GROUP)
    c_shape = (2, N_BLOCKS, GROUPS_PER_BLOCK, SSM_GROUP, SSM_STATE)
    return (lanes(a_re), lanes(a_im), lanes(jnp.broadcast_to(log_dt[..., None], a_re.shape)),
            expand(b_re.reshape(b_shape).transpose(0, 1, 2, 4, 3)), expand(b_im.reshape(b_shape).transpose(0, 1, 2, 4, 3)),
            expand(c_re.reshape(c_shape)), expand(c_im.reshape(c_shape)), d.reshape(N_BLOCKS, 1, BLOCK_CH))


def _ssm_discretise(ar, ai, ldt):
    dt = jnp.exp(ldt)
    mag = jnp.exp(dt * ar)
    abr = mag * jnp.cos(dt * ai)
    abi = mag * jnp.sin(dt * ai)
    num_re = abr - 1.0
    num_im = abi
    denom = ar * ar + ai * ai
    coef_re = (num_re * ar + num_im * ai) / denom
    coef_im = (num_im * ar - num_re * ai) / denom
    return abr, abi, coef_re, coef_im


_POW_ROWS = 24


def _ssm_fill_powers(ar_ref, ai_ref, ldt_ref, br_ref, bi_ref, pw_ref, bbar_ref):
    for d in range(2):
        abr, abi, cfr, cfi = _ssm_discretise(ar_ref[d, 0], ai_ref[d, 0], ldt_ref[d, 0])
        bbar_ref[d, 0] = cfr * br_ref[d, 0] - cfi * bi_ref[d, 0]
        bbar_ref[d, 1] = cfr * bi_ref[d, 0] + cfi * br_ref[d, 0]
        pr, pi = jnp.ones_like(abr), jnp.zeros_like(abi)
        for t in range(CHUNK + 1):
            pw_ref[d, 0, t:t + 1, :] = pr
            pw_ref[d, 1, t:t + 1, :] = pi
            pr, pi = pr * abr - pi * abi, pr * abi + pi * abr


def _dot_rounded(a, b, dims=((1,), (0,))):
    return _dot(a.astype(BF16), b.astype(BF16), dims)


def _ssm_stack_inputs(d, pw_ref, bbar_ref, xs_ref):
    for t in range(CHUNK):
        pr, pi = pw_ref[d, 0, t:t + 1, :], pw_ref[d, 1, t:t + 1, :]
        xs_ref[0, t * BLOCK_CH:(t + 1) * BLOCK_CH, :] = bbar_ref[d, 0] * pr - bbar_ref[d, 1] * pi
        xs_ref[1, t * BLOCK_CH:(t + 1) * BLOCK_CH, :] = bbar_ref[d, 0] * pi + bbar_ref[d, 1] * pr


def _eye(n):
    return (lax.broadcasted_iota(jnp.int32, (n, n), 0) == lax.broadcasted_iota(jnp.int32, (n, n), 1)).astype(F32)


def _ssm_param_specs():
    vec = pl.BlockSpec((2, 1, 1, BLOCK_ST), lambda b, j: (0, b, 0, 0))
    mat = pl.BlockSpec((2, 1, BLOCK_CH, BLOCK_ST), lambda b, j: (0, b, 0, 0))
    return [vec, vec, vec, mat, mat, mat, mat, pl.BlockSpec((1, 1, BLOCK_CH), lambda b, j: (b, 0, 0))]


def _ssm_chunk_matrices(blk, shards):
    n = len(shards)

    def body(*refs):
        ar_ref, ai_ref, ldt_ref, br_ref, bi_ref, cr_ref, ci_ref, d_ref = refs[:8]
        m_ref, ws_ref, wot_ref, a16_ref = refs[8 + n:12 + n]
        pw_ref, bbar_ref, lag_ref, xs_ref = refs[12 + 2 * n:16 + 2 * n]
        gather = _ChipGather(refs[8:8 + n], refs[12 + n:12 + 2 * n], refs[16 + 2 * n:])
        b, j = pl.program_id(0), pl.program_id(1)
        pl.when((b == 0) & (j == 0))(gather.start)
        pl.when((b == N_BLOCKS - 1) & (j == 0))(gather.forward)
        pl.when((b == N_BLOCKS - 1) & (j == CHUNK - 1))(gather.finish)

        @pl.when(j == 0)
        def _():
            _ssm_fill_powers(ar_ref, ai_ref, ldt_ref, br_ref, bi_ref, pw_ref, bbar_ref)
            zero_lag = d_ref[0] * _eye(BLOCK_CH)
            for d in range(2):
                _ssm_stack_inputs(d, pw_ref, bbar_ref, xs_ref)
                taps = (_dot_rounded(xs_ref[0], cr_ref[d, 0], ((1,), (1,)))
                        - _dot_rounded(xs_ref[1], ci_ref[d, 0], ((1,), (1,))))
                zero_lag = zero_lag + taps[0:BLOCK_CH]
                for t in range(1, CHUNK):
                    lag_ref[CHUNK - 1 + t if d == 0 else CHUNK - 1 - t] = taps[t * BLOCK_CH:(t + 1) * BLOCK_CH]
            lag_ref[CHUNK - 1] = zero_lag
            a16_ref[0] = jnp.concatenate([pw_ref[d, ri, CHUNK:CHUNK + 1, :] for d in range(2) for ri in range(2)], axis=1)

        m_ref[0] = jnp.concatenate([lag_ref[jp - j + CHUNK - 1] for jp in range(CHUNK)], axis=1).astype(BF16)

        def power(d, t):
            return pw_ref[d, 0, pl.ds(t, 1), :], pw_ref[d, 1, pl.ds(t, 1), :]

        parts = []
        for d, t in ((0, CHUNK - 1 - j), (1, j)):
            pr, pi = power(d, t)
            parts += [bbar_ref[d, 0] * pr - bbar_ref[d, 1] * pi, bbar_ref[d, 0] * pi + bbar_ref[d, 1] * pr]
        ws_ref[0] = jnp.concatenate(parts, axis=1).astype(BF16)
        parts = []
        for d, t in ((0, j + 1), (1, CHUNK - j)):
            pr, pi = power(d, t)
            parts += [cr_ref[d, 0] * pr - ci_ref[d, 0] * pi, -cr_ref[d, 0] * pi - ci_ref[d, 0] * pr]
        wot_ref[0] = jnp.concatenate(parts, axis=1).astype(BF16)

    row = pl.BlockSpec((1, BLOCK_CH, CHUNK_W), lambda b, j: (b, j, 0))
    mat = jax.ShapeDtypeStruct((N_BLOCKS, CHUNK_W, CHUNK_W), BF16)
    outs = pl.pallas_call(
        body, name="ssm_chunk_matrices", grid=(N_BLOCKS, CHUNK),
        in_specs=_ssm_param_specs() + _hbm_specs(n),
        out_specs=[row, row, row, pl.BlockSpec((1, 1, STATE_W), lambda b, j: (b, 0, 0))] + _hbm_specs(n),
        out_shape=[mat, mat, mat, jax.ShapeDtypeStruct((N_BLOCKS, 1, STATE_W), F32)] + _gather_out_shapes(shards),
        scratch_shapes=[pltpu.VMEM((2, 2, _POW_ROWS, BLOCK_ST), F32), pltpu.VMEM((2, 2, BLOCK_CH, BLOCK_ST), F32),
                        pltpu.VMEM((2 * CHUNK, BLOCK_CH, BLOCK_CH), F32), pltpu.VMEM((2, CHUNK_W, BLOCK_ST), F32)]
        + _gather_semaphores(n),
        compiler_params=_cparams(("arbitrary", "arbitrary"), has_side_effects=True),
    )(*blk, *shards)
    return outs[:4], outs[4:]


def _ssm_chunk_matrices_bwd(blk, d_m, d_ws, d_wot, d_a16):
    def body(ar_ref, ai_ref, ldt_ref, br_ref, bi_ref, cr_ref, ci_ref, d_ref, dm_ref, dws_ref, dwot_ref, da16_ref,
             dar_ref, dai_ref, dldt_ref, dbr_ref, dbi_ref, dcr_ref, dci_ref, dd_ref,
             pw_ref, bbar_ref, dlag_ref, dbbar_ref, dc_ref, dpw_ref, xs_ref, dts_ref):
        j = pl.program_id(1)
        w = BLOCK_ST

        @pl.when(j == 0)
        def _():
            _ssm_fill_powers(ar_ref, ai_ref, ldt_ref, br_ref, bi_ref, pw_ref, bbar_ref)
            for r in (dlag_ref, dbbar_ref, dc_ref, dpw_ref):
                r[...] = jnp.zeros_like(r)

        def x_chain(d, t, dxr, dxi):
            pr, pi = pw_ref[d, 0, pl.ds(t, 1), :], pw_ref[d, 1, pl.ds(t, 1), :]
            bbr, bbi = bbar_ref[d, 0], bbar_ref[d, 1]
            dbbar_ref[d, 0] += dxr * pr + dxi * pi
            dbbar_ref[d, 1] += dxi * pr - dxr * pi
            dpw_ref[d, 0, pl.ds(t, 1), :] += jnp.sum(dxr * bbr + dxi * bbi, axis=0, keepdims=True)
            dpw_ref[d, 1, pl.ds(t, 1), :] += jnp.sum(dxi * bbr - dxr * bbi, axis=0, keepdims=True)

        def z_chain(d, t, dzr, dzi):
            pr, pi = pw_ref[d, 0, pl.ds(t, 1), :], pw_ref[d, 1, pl.ds(t, 1), :]
            c_r, c_i = cr_ref[d, 0], ci_ref[d, 0]
            dc_ref[d, 0] += dzr * pr - dzi * pi
            dc_ref[d, 1] += -dzr * pi - dzi * pr
            dpw_ref[d, 0, pl.ds(t, 1), :] += jnp.sum(dzr * c_r - dzi * c_i, axis=0, keepdims=True)
            dpw_ref[d, 1, pl.ds(t, 1), :] += jnp.sum(-dzr * c_i - dzi * c_r, axis=0, keepdims=True)

        for jp in range(CHUNK):
            dlag_ref[jp - j + CHUNK - 1] += dm_ref[0, :, jp * BLOCK_CH:(jp + 1) * BLOCK_CH].astype(F32)
        quarter = lambda ref, i: ref[0, :, i * w:(i + 1) * w].astype(F32)
        x_chain(0, CHUNK - 1 - j, quarter(dws_ref, 0), quarter(dws_ref, 1))
        x_chain(1, j, quarter(dws_ref, 2), quarter(dws_ref, 3))
        z_chain(0, j + 1, quarter(dwot_ref, 0), quarter(dwot_ref, 1))
        z_chain(1, CHUNK - j, quarter(dwot_ref, 2), quarter(dwot_ref, 3))

        @pl.when(j == CHUNK - 1)
        def _():
            for d in range(2):
                _ssm_stack_inputs(d, pw_ref, bbar_ref, xs_ref)
                for t in range(CHUNK):
                    dts_ref[t * BLOCK_CH:(t + 1) * BLOCK_CH, :] = dlag_ref[CHUNK - 1 + t if d == 0 else CHUNK - 1 - t]
                d_taps = dts_ref[...]
                dc_ref[d, 0] += _dot_rounded(d_taps, xs_ref[0], ((0,), (0,)))
                dc_ref[d, 1] -= _dot_rounded(d_taps, xs_ref[1], ((0,), (0,)))
                xs_ref[0] = _dot_rounded(d_taps, cr_ref[d, 0])
                xs_ref[1] = -_dot_rounded(d_taps, ci_ref[d, 0])
                for t in range(CHUNK):
                    rows = slice(t * BLOCK_CH, (t + 1) * BLOCK_CH)
                    x_chain(d, t, xs_ref[0, rows, :], xs_ref[1, rows, :])
            dd_ref[0] = jnp.sum(dlag_ref[CHUNK - 1] * _eye(BLOCK_CH), axis=0, keepdims=True)
            for d in range(2):
                (abr, abi, cfr, cfi), disc_vjp = jax.vjp(_ssm_discretise, ar_ref[d, 0], ai_ref[d, 0], ldt_ref[d, 0])
                dpr = dpw_ref[d, 0, CHUNK:CHUNK + 1, :] + da16_ref[0, :, 2 * d * w:(2 * d + 1) * w]
                dpi = dpw_ref[d, 1, CHUNK:CHUNK + 1, :] + da16_ref[0, :, (2 * d + 1) * w:(2 * d + 2) * w]
                dabr, dabi = jnp.zeros_like(abr), jnp.zeros_like(abi)
                for t in range(CHUNK, 0, -1):
                    qr, qi = pw_ref[d, 0, t - 1:t, :], pw_ref[d, 1, t - 1:t, :]
                    dabr = dabr + dpr * qr + dpi * qi
                    dabi = dabi + dpi * qr - dpr * qi
                    dpr, dpi = (dpr * abr + dpi * abi + dpw_ref[d, 0, t - 1:t, :],
                                dpi * abr - dpr * abi + dpw_ref[d, 1, t - 1:t, :])
                dbbr, dbbi = dbbar_ref[d, 0], dbbar_ref[d, 1]
                b_r, b_i = br_ref[d, 0], bi_ref[d, 0]
                dbr_ref[d, 0] = cfr * dbbr + cfi * dbbi
                dbi_ref[d, 0] = cfr * dbbi - cfi * dbbr
                dcfr = jnp.sum(b_r * dbbr + b_i * dbbi, axis=0, keepdims=True)
                dcfi = jnp.sum(b_r * dbbi - b_i * dbbr, axis=0, keepdims=True)
                dar_ref[d, 0], dai_ref[d, 0], dldt_ref[d, 0] = disc_vjp((dabr, dabi, dcfr, dcfi))
                dcr_ref[d, 0] = dc_ref[d, 0]
                dci_ref[d, 0] = dc_ref[d, 1]

    row = pl.BlockSpec((1, BLOCK_CH, CHUNK_W), lambda b, j: (b, j, 0))
    specs = _ssm_param_specs()
    acc = lambda *s: pltpu.VMEM(s, F32)
    return pl.pallas_call(
        body, name="ssm_chunk_matrices_bwd", grid=(N_BLOCKS, CHUNK),
        in_specs=specs + [row, row, row, pl.BlockSpec((1, 1, STATE_W), lambda b, j: (b, 0, 0))],
        out_specs=specs,
        out_shape=[jax.ShapeDtypeStruct(t.shape, F32) for t in blk],
        scratch_shapes=[acc(2, 2, _POW_ROWS, BLOCK_ST), acc(2, 2, BLOCK_CH, BLOCK_ST), acc(2 * CHUNK, BLOCK_CH, BLOCK_CH),
                        acc(2, 2, BLOCK_CH, BLOCK_ST), acc(2, 2, BLOCK_CH, BLOCK_ST), acc(2, 2, _POW_ROWS, BLOCK_ST),
                        acc(2, CHUNK_W, BLOCK_ST), acc(CHUNK_W, BLOCK_CH)],
        compiler_params=_cparams(("arbitrary", "arbitrary")),
    )(*blk, d_m, d_ws, d_wot, d_a16)


def _block_matmul(terms, name, out_dtype=F32, tn=1024):
    nc = terms[0][0].shape[1]
    n_out = terms[0][1].shape[1] if terms[0][2] else terms[0][1].shape[2]
    flags = [t[2] for t in terms]

    def body(*refs):
        out_ref = refs[-1]
        acc = None
        for t, transposed in enumerate(flags):
            a = refs[2 * t][0].astype(BF16)
            w = refs[2 * t + 1][0]
            part = _dot_nt(a, w) if transposed else _dot(a, w)
            acc = part if acc is None else acc + part
        out_ref[0] = acc.astype(out_dtype)

    in_specs, args = [], []
    for a, w, transposed in terms:
        k = a.shape[2]
        in_specs.append(pl.BlockSpec((1, nc, k), lambda b, n: (b, 0, 0)))
        if transposed:
            in_specs.append(pl.BlockSpec((1, tn, k), lambda b, n: (b, n, 0)))
        else:
            in_specs.append(pl.BlockSpec((1, k, tn), lambda b, n: (b, 0, n)))
        args += [a, w]
    return pl.pallas_call(
        body, name=name, grid=(N_BLOCKS, n_out // tn), in_specs=in_specs,
        out_specs=pl.BlockSpec((1, nc, tn), lambda b, n: (b, 0, n)),
        out_shape=jax.ShapeDtypeStruct((N_BLOCKS, nc, n_out), out_dtype),
        compiler_params=_cparams(("arbitrary", "arbitrary")),
    )(*args)


def _block_matmul_tn(a, b, name, tile=1024):
    nc, m = a.shape[1], a.shape[2]
    n = b.shape[2]

    def body(a_ref, b_ref, out_ref):
        out_ref[0] = _dot_tn(a_ref[0].astype(BF16), b_ref[0].astype(BF16)).astype(BF16)

    return pl.pallas_call(
        body, name=name, grid=(N_BLOCKS, m // tile, n // tile),
        in_specs=[pl.BlockSpec((1, nc, tile), lambda blk, i, j: (blk, 0, i)),
                  pl.BlockSpec((1, nc, tile), lambda blk, i, j: (blk, 0, j))],
        out_specs=pl.BlockSpec((1, tile, tile), lambda blk, i, j: (blk, i, j)),
        out_shape=jax.ShapeDtypeStruct((N_BLOCKS, m, n), BF16),
        compiler_params=_cparams(("arbitrary", "arbitrary", "arbitrary")),
    )(a, b)


def _cmul(ar, ai, xr, xi):
    return ar * xr - ai * xi, ar * xi + ai * xr


def _cmul_conj(ar, ai, xr, xi):
    return ar * xr + ai * xi, ar * xi - ai * xr


def _ssm_state_scan(s_in, a16):
    nc = s_in.shape[1]
    w = BLOCK_ST

    def body(sin_ref, a_ref, out_ref):
        a = a_ref[0]
        afr, afi, abr, abi = a[:, 0:w], a[:, w:2 * w], a[:, 2 * w:3 * w], a[:, 3 * w:4 * w]

        def step(c, carry):
            fr, fi, br, bi = carry
            cb = nc - 1 - c
            out_ref[0, pl.ds(c, 1), 0:w] = fr
            out_ref[0, pl.ds(c, 1), w:2 * w] = fi
            out_ref[0, pl.ds(cb, 1), 2 * w:3 * w] = br
            out_ref[0, pl.ds(cb, 1), 3 * w:4 * w] = bi
            nfr, nfi = _cmul(afr, afi, fr, fi)
            nbr, nbi = _cmul(abr, abi, br, bi)
            return (nfr + sin_ref[0, pl.ds(c, 1), 0:w], nfi + sin_ref[0, pl.ds(c, 1), w:2 * w],
                    nbr + sin_ref[0, pl.ds(cb, 1), 2 * w:3 * w], nbi + sin_ref[0, pl.ds(cb, 1), 3 * w:4 * w])

        z = jnp.zeros((1, w), F32)
        lax.fori_loop(0, nc, step, (z, z, z, z))

    spec = pl.BlockSpec((1, nc, STATE_W), lambda b: (b, 0, 0))
    return pl.pallas_call(
        body, name="ssm_state_scan", grid=(N_BLOCKS,),
        in_specs=[spec, pl.BlockSpec((1, 1, STATE_W), lambda b: (b, 0, 0))],
        out_specs=spec, out_shape=jax.ShapeDtypeStruct(s_in.shape, F32),
        compiler_params=_cparams(("arbitrary",)),
    )(s_in, a16)


def _ssm_state_scan_bwd(d_prev, s_prev, a16):
    nc = d_prev.shape[1]
    w = BLOCK_ST

    def body(dp_ref, sp_ref, a_ref, g_ref, da_ref):
        a = a_ref[0]
        afr, afi, abr, abi = a[:, 0:w], a[:, w:2 * w], a[:, 2 * w:3 * w], a[:, 3 * w:4 * w]

        def step(i, carry):
            gfr, gfi, gbr, gbi, dafr, dafi, dabr, dabi = carry
            cf = nc - 1 - i
            cb = i
            g_ref[0, pl.ds(cf, 1), 0:w] = gfr
            g_ref[0, pl.ds(cf, 1), w:2 * w] = gfi
            g_ref[0, pl.ds(cb, 1), 2 * w:3 * w] = gbr
            g_ref[0, pl.ds(cb, 1), 3 * w:4 * w] = gbi
            sfr, sfi = sp_ref[0, pl.ds(cf, 1), 0:w], sp_ref[0, pl.ds(cf, 1), w:2 * w]
            sbr, sbi = sp_ref[0, pl.ds(cb, 1), 2 * w:3 * w], sp_ref[0, pl.ds(cb, 1), 3 * w:4 * w]
            dafr = dafr + gfr * sfr + gfi * sfi
            dafi = dafi + gfi * sfr - gfr * sfi
            dabr = dabr + gbr * sbr + gbi * sbi
            dabi = dabi + gbi * sbr - gbr * sbi
            nfr, nfi = _cmul_conj(afr, afi, gfr, gfi)
            nbr, nbi = _cmul_conj(abr, abi, gbr, gbi)
            return (nfr + dp_ref[0, pl.ds(cf, 1), 0:w], nfi + dp_ref[0, pl.ds(cf, 1), w:2 * w],
                    nbr + dp_ref[0, pl.ds(cb, 1), 2 * w:3 * w], nbi + dp_ref[0, pl.ds(cb, 1), 3 * w:4 * w],
                    dafr, dafi, dabr, dabi)

        z = jnp.zeros((1, w), F32)
        res = lax.fori_loop(0, nc, step, (z,) * 8)
        da_ref[0] = jnp.concatenate(res[4:], axis=1)

    spec = pl.BlockSpec((1, nc, STATE_W), lambda b: (b, 0, 0))
    aspec = pl.BlockSpec((1, 1, STATE_W), lambda b: (b, 0, 0))
    return pl.pallas_call(
        body, name="ssm_state_scan_bwd", grid=(N_BLOCKS,),
        in_specs=[spec, spec, aspec], out_specs=[spec, aspec],
        out_shape=[jax.ShapeDtypeStruct(d_prev.shape, F32), jax.ShapeDtypeStruct((N_BLOCKS, 1, STATE_W), F32)],
        compiler_params=_cparams(("arbitrary",)),
    )(d_prev, s_prev, a16)


NA_PAIR = 2 * GRID_W
NA_WIN_ROWS = NA_ROWS + 2
NA_WIN = NA_WIN_ROWS * GRID_W
NA_PAIRS_PER_STEP = 8
NA_CASES = 5
NA_MASKED = -1e30


def _na_pair_window(m, rows):
    rs0 = jnp.clip(2 * m - NA_ROWS // 2, 0, rows - NA_ROWS)
    ws = jnp.minimum(rs0, rows - NA_WIN_ROWS)
    last = rows // 2 - 1
    case = jnp.where(m == 0, 0, jnp.where(m == 1, 1, jnp.where(m == last - 1, 3, jnp.where(m == last, 4, 2))))
    return ws, case


def _na_row_offsets(rows):
    last = rows // 2 - 1
    geom = []
    for m in (0, 1, 2, last - 1, last):
        ws = min(max(2 * m - NA_ROWS // 2, 0), rows - NA_ROWS, rows - NA_WIN_ROWS)
        per_case = []
        for i in range(NA_WIN_ROWS):
            pair = []
            for rr in range(2):
                r = 2 * m + rr
                rs = min(max(r - NA_ROWS // 2, 0), rows - NA_ROWS)
                pair.append(ws + i - r + NA_ROWS - 1 if rs <= ws + i < rs + NA_ROWS else None)
            per_case.append(pair)
        geom.append(per_case)
    return geom


def _na_col_select():
    qc = np.arange(NA_PAIR)[None, :] % GRID_W
    kc = np.arange(GRID_W)[:, None]
    dc = np.clip(kc - qc + NA_COLS - 1, 0, 2 * NA_COLS - 2)
    return jnp.asarray((np.arange(2 * NA_COLS - 1)[:, None, None] == dc[None]).astype(np.float32))


def _na_bias_rows(rpb):
    return jnp.einsum("hrd,dkl->hrkl", rpb, _na_col_select(), precision=HIGHEST)


def _na_col_window():
    qc = lax.broadcasted_iota(jnp.int32, (GRID_W, NA_PAIR), 1) % GRID_W
    kc = lax.broadcasted_iota(jnp.int32, (GRID_W, NA_PAIR), 0)
    cs = jnp.clip(qc - NA_COLS // 2, 0, GRID_W - NA_COLS)
    first_row = lax.broadcasted_iota(jnp.int32, (GRID_W, NA_PAIR), 1) < GRID_W
    return (kc >= cs) & (kc < cs + NA_COLS), first_row


def _na_bias_table(bias_rows, rows):
    geom = _na_row_offsets(rows)

    def body(br_ref, tab_ref):
        col_ok, first_row = _na_col_window()
        masked = jnp.full((GRID_W, NA_PAIR), NA_MASKED, F32)
        for case in range(NA_CASES):
            for i in range(NA_WIN_ROWS):
                d0, d1 = geom[case][i]
                t0 = masked if d0 is None else br_ref[0, d0]
                t1 = masked if d1 is None else br_ref[0, d1]
                tile = jnp.where(col_ok, jnp.where(first_row, t0, t1), NA_MASKED)
                tab_ref[0, case, i * GRID_W:(i + 1) * GRID_W, :] = tile

    return pl.pallas_call(
        body, name="na_bias_table", grid=(NA_HEADS,),
        in_specs=[pl.BlockSpec((1, 2 * NA_ROWS - 1, GRID_W, NA_PAIR), lambda h: (h, 0, 0, 0))],
        out_specs=pl.BlockSpec((1, NA_CASES, NA_WIN, NA_PAIR), lambda h: (h, 0, 0, 0)),
        out_shape=jax.ShapeDtypeStruct((NA_HEADS, NA_CASES, NA_WIN, NA_PAIR), F32),
        compiler_params=_cparams(("arbitrary",)),
    )(bias_rows)


def _na_bias_table_bwd(d_tab, rows):
    geom = _na_row_offsets(rows)

    def body(dt_ref, dbr_ref):
        col_ok, first_row = _na_col_window()
        acc = [None] * (2 * NA_ROWS - 1)
        for case in range(NA_CASES):
            for i in range(NA_WIN_ROWS):
                tile = jnp.where(col_ok, dt_ref[0, case, i * GRID_W:(i + 1) * GRID_W, :], 0.0)
                for rr, d in enumerate(geom[case][i]):
                    if d is not None:
                        part = jnp.where(first_row if rr == 0 else ~first_row, tile, 0.0)
                        acc[d] = part if acc[d] is None else acc[d] + part
        for d, a in enumerate(acc):
            dbr_ref[0, d] = jnp.zeros((GRID_W, NA_PAIR), F32) if a is None else a

    return pl.pallas_call(
        body, name="na_bias_table_bwd", grid=(NA_HEADS,),
        in_specs=[pl.BlockSpec((1, NA_CASES, NA_WIN, NA_PAIR), lambda h: (h, 0, 0, 0))],
        out_specs=pl.BlockSpec((1, 2 * NA_ROWS - 1, GRID_W, NA_PAIR), lambda h: (h, 0, 0, 0)),
        out_shape=jax.ShapeDtypeStruct((NA_HEADS, 2 * NA_ROWS - 1, GRID_W, NA_PAIR), F32),
        compiler_params=_cparams(("arbitrary",)),
    )(d_tab)


NA_BLK = 64


def _na_blocks():
    return [slice(i * NA_BLK, (i + 1) * NA_BLK) for i in range(NA_WIN // NA_BLK)]


def _na_softmax(qk, bias_ref, hh, case):
    m = jnp.full((NA_BLK, NA_PAIR), -jnp.inf, F32)
    scores = []
    for blk in _na_blocks():
        s = qk[blk, :] + bias_ref[hh, case, blk, :]
        scores.append(s)
        m = jnp.maximum(m, s)
    m = jnp.max(m, axis=0, keepdims=True)
    l = jnp.zeros((NA_BLK, NA_PAIR), F32)
    exps = []
    for s in scores:
        e = jnp.exp(s - m)
        exps.append(e)
        l = l + e
    return exps, jnp.sum(l, axis=0, keepdims=True)


def _na_units(step, rows):
    units = []
    for pp in range(NA_PAIRS_PER_STEP):
        ws, case = _na_pair_window(step * NA_PAIRS_PER_STEP + pp, rows)
        win = pl.ds(pl.multiple_of(ws * GRID_W, NA_PAIR), NA_WIN)
        lanes = slice(pp * NA_PAIR, (pp + 1) * NA_PAIR)
        for hh in range(2):
            units.append((pp, hh, case, win, lanes, slice(hh * NA_HEAD_DIM, (hh + 1) * NA_HEAD_DIM)))
    return units


def _na_pipeline(n, before, middle, after, lookahead=2):
    for u in range(min(lookahead, n)):
        for f in before:
            f(u)
    for u in range(n):
        middle(u)
        if u + lookahead < n:
            for f in before:
                f(u + lookahead)
        for f in after:
            f(u)


def _head_rows(t, hh):
    row_head = lax.broadcasted_iota(jnp.int32, t.shape, 0) // NA_HEAD_DIM
    return jnp.where(row_head == hh, t, jnp.zeros_like(t))


def _heads_block_diag(t):
    lane_head = lax.broadcasted_iota(jnp.int32, t.shape, 1) // NA_HEAD_DIM
    zero = jnp.zeros_like(t)
    return jnp.concatenate([jnp.where(lane_head == 0, t, zero), jnp.where(lane_head == 1, t, zero)], axis=0)


def _na_fwd(q_t, k, v_t, bias_tab):
    L = k.shape[0]
    rows = L // GRID_W
    step_w = NA_PAIRS_PER_STEP * NA_PAIR

    def body(q_ref, k_ref, v_ref, bt_ref, o_ref):
        units = _na_units(pl.program_id(1), rows)
        qk, probs = {}, {}

        def scores(u):
            _, hh, _, win, lanes, _ = units[u]
            qk[u] = _dot(k_ref[win, :], _head_rows(q_ref[:, lanes], hh))

        def softmax(u):
            _, hh, case, _, _, _ = units[u]
            exps, l = _na_softmax(qk.pop(u), bt_ref, hh, case)
            probs[u] = jnp.concatenate([t.astype(BF16) for t in exps], axis=0), l

        def output(u):
            _, _, _, win, lanes, hrows = units[u]
            e, l = probs.pop(u)
            o_ref[hrows, lanes] = _dot(v_ref[hrows, win], e) / l

        _na_pipeline(len(units), [scores], softmax, [output])

    q_spec = pl.BlockSpec((NA_PAIR, step_w), lambda h, s: (h, s))
    return pl.pallas_call(
        body, name="na_fwd", grid=(NA_HEADS // 2, L // step_w),
        in_specs=[q_spec, pl.BlockSpec((L, NA_PAIR), lambda h, s: (0, h)),
                  pl.BlockSpec((NA_PAIR, L), lambda h, s: (h, 0)),
                  pl.BlockSpec((2, NA_CASES, NA_WIN, NA_PAIR), lambda h, s: (h, 0, 0, 0))],
        out_specs=q_spec,
        out_shape=jax.ShapeDtypeStruct((D_NA, L), F32),
        compiler_params=_cparams(("arbitrary", "arbitrary")),
    )(q_t, k, v_t, bias_tab)


def _na_bwd(q_t, q, k_t, k, v, bias_tab, out_t, d_out_t, d_out):
    L = k.shape[0]
    rows = L // GRID_W
    step_w = NA_PAIRS_PER_STEP * NA_PAIR

    def body(qt_ref, q_ref, kt_ref, k_ref, v_ref, bt_ref, ot_ref, dot_ref, do_ref, dq_ref, dk_ref, dv_ref, dbt_ref):
        @pl.when(pl.program_id(1) == 0)
        def _():
            dk_ref[...] = jnp.zeros_like(dk_ref)
            dv_ref[...] = jnp.zeros_like(dv_ref)
            dbt_ref[...] = jnp.zeros_like(dbt_ref)

        units = _na_units(pl.program_id(1), rows)
        qk, dp, dsb, pb = {}, {}, {}, {}

        def scores(u):
            _, hh, _, win, lanes, _ = units[u]
            qk[u] = _dot(k_ref[win, :], _head_rows(qt_ref[:, lanes], hh))

        def d_probs(u):
            _, hh, _, win, lanes, _ = units[u]
            dp[u] = _dot(v_ref[win, :], _head_rows(dot_ref[:, lanes].astype(BF16), hh))

        def softmax_bwd(u):
            _, hh, case, _, lanes, hrows = units[u]
            exps, l = _na_softmax(qk.pop(u), bt_ref, hh, case)
            inv_l = 1.0 / l
            delta = jnp.sum(dot_ref[hrows, lanes] * ot_ref[hrows, lanes], axis=0, keepdims=True)
            d_p = dp.pop(u)
            ds_blocks, p_blocks = [], []
            for blk, e in zip(_na_blocks(), exps):
                p = e * inv_l
                ds = p * (d_p[blk, :] - delta)
                dbt_ref[hh, case, blk, :] += ds
                ds_blocks.append(ds.astype(BF16))
                p_blocks.append(p.astype(BF16))
            dsb[u] = jnp.concatenate(ds_blocks, axis=0)
            pb[u] = jnp.concatenate(p_blocks, axis=0)

        def d_query(u):
            _, _, _, win, lanes, hrows = units[u]
            dq_ref[hrows, lanes] = _dot(kt_ref[hrows, win], dsb[u]) * (NA_HEAD_DIM ** -0.5)

        def d_keys_values(u):
            pp, hh, _, win, _, _ = units[u]
            if hh == 1:
                tokens = slice(pp * NA_PAIR, (pp + 1) * NA_PAIR)
                dk_ref[win, :] += _dot(jnp.concatenate([dsb.pop(u - 1), dsb.pop(u)], axis=1), _heads_block_diag(q_ref[tokens, :]))
                dv_ref[win, :] += _dot(jnp.concatenate([pb.pop(u - 1), pb.pop(u)], axis=1), _heads_block_diag(do_ref[tokens, :]))

        _na_pipeline(len(units), [scores, d_probs], softmax_bwd, [d_query, d_keys_values])

    t_tile = pl.BlockSpec((NA_PAIR, step_w), lambda h, s: (h, s))
    tile = pl.BlockSpec((step_w, NA_PAIR), lambda h, s: (s, h))
    t_full = pl.BlockSpec((NA_PAIR, L), lambda h, s: (h, 0))
    full = pl.BlockSpec((L, NA_PAIR), lambda h, s: (0, h))
    bt = pl.BlockSpec((2, NA_CASES, NA_WIN, NA_PAIR), lambda h, s: (h, 0, 0, 0))
    tok = jax.ShapeDtypeStruct((L, D_NA), F32)
    return pl.pallas_call(
        body, name="na_bwd", grid=(NA_HEADS // 2, L // step_w),
        in_specs=[t_tile, tile, t_full, full, full, bt, t_tile, t_tile, tile],
        out_specs=[t_tile, full, full, bt],
        out_shape=[jax.ShapeDtypeStruct((D_NA, L), F32), tok, tok, jax.ShapeDtypeStruct(bias_tab.shape, F32)],
        compiler_params=_cparams(("arbitrary", "arbitrary")),
    )(q_t, q, k_t, k, v, bias_tab, out_t, d_out_t, d_out)


def _branch_fwd_values(ys, zs, yn, zn, wglu, bglu):
    g1, t = _gelu_parts(ys)
    lin = _dot(g1.astype(BF16), wglu) + bglu
    sg = _sigmoid(lin)
    ys2 = g1 * sg
    sz, szs = _silu_parts(zs)
    sn, sns = _silu_parts(zn)
    return g1, t, sg, ys2, sz, szs, sn, sns


def _branch_fwd(y_ssm_c, z_s, y_na_t, z_n, w_glu, b_glu, tm=512):
    L = z_s.shape[0]

    def body(ys_ref, zs_ref, yn_ref, zn_ref, w_ref, b_ref, cat_ref, scr):
        yn = yn_ref[...].T
        g1, t, sg, ys2, sz, szs, sn, sns = _branch_fwd_values(
            _load_chunks(ys_ref, scr), zs_ref[...], yn, zn_ref[...], w_ref[...], b_ref[...])
        cat_ref[:, 0:512] = (ys2 * sz).astype(BF16)
        cat_ref[:, 512:1024] = (yn * sn).astype(BF16)

    tile = pl.BlockSpec((tm, 512), lambda i: (i, 0))
    return pl.pallas_call(
        body, name="branch_fwd", grid=(L // tm,),
        in_specs=[_chunk_spec(tm), tile, _heads_t_spec(tm), tile, pl.BlockSpec((512, 512), lambda i: (0, 0)),
                  pl.BlockSpec((1, 512), lambda i: (0, 0))],
        out_specs=pl.BlockSpec((tm, 1024), lambda i: (i, 0)),
        out_shape=jax.ShapeDtypeStruct((L, 1024), BF16),
        scratch_shapes=[_chunk_scratch(tm)],
        compiler_params=_cparams(("arbitrary",)),
    )(y_ssm_c, z_s, y_na_t, z_n, w_glu, b_glu)


def _branch_bwd(y_ssm_c, z_s, y_na_t, z_n, w_glu, b_glu, d_cat, tm=512):
    L = z_s.shape[0]

    def body(ys_ref, zs_ref, yn_ref, zn_ref, w_ref, b_ref, dc_ref,
             dys_ref, dzs_ref, dynt_ref, dyn_ref, dzn_ref, dw_ref, db_ref, scr):
        @pl.when(pl.program_id(0) == 0)
        def _():
            dw_ref[...] = jnp.zeros_like(dw_ref)
            db_ref[...] = jnp.zeros_like(db_ref)

        ys, zs, yn, zn = _load_chunks(ys_ref, scr), zs_ref[...], yn_ref[...].T, zn_ref[...]
        w = w_ref[...]
        g1, t, sg, ys2, sz, szs, sn, sns = _branch_fwd_values(ys, zs, yn, zn, w, b_ref[...])
        dys3 = dc_ref[:, 0:512]
        dyn2 = dc_ref[:, 512:1024]
        dzs_ref[...] = (dys3 * ys2 * _silu_grad(zs, szs)).astype(BF16)
        dys2 = dys3 * sz
        dlin = dys2 * g1 * sg * (1.0 - sg)
        dlb = dlin.astype(BF16)
        db_ref[...] += jnp.sum(dlin, axis=0, keepdims=True)
        dw_ref[...] += _dot_tn(g1.astype(BF16), dlb)
        dg1 = dys2 * sg + _dot_nt(dlb, w)
        _store_chunks(dg1 * _gelu_grad(ys, t), scr, dys_ref, BF16)
        dyn = dyn2 * sn
        dynt_ref[...] = dyn.T
        dyn_ref[...] = dyn.astype(BF16)
        dzn_ref[...] = (dyn2 * yn * _silu_grad(zn, sns)).astype(BF16)

    tile = pl.BlockSpec((tm, 512), lambda i: (i, 0))
    wspec = pl.BlockSpec((512, 512), lambda i: (0, 0))
    bspec = pl.BlockSpec((1, 512), lambda i: (0, 0))
    tok = jax.ShapeDtypeStruct((L, 512), BF16)
    return pl.pallas_call(
        body, name="branch_bwd", grid=(L // tm,),
        in_specs=[_chunk_spec(tm), tile, _heads_t_spec(tm), tile, wspec, bspec, pl.BlockSpec((tm, 1024), lambda i: (i, 0))],
        out_specs=[_chunk_spec(tm), tile, _heads_t_spec(tm), tile, tile, wspec, bspec],
        out_shape=[jax.ShapeDtypeStruct((N_BLOCKS, L // CHUNK, CHUNK_W), BF16), tok, jax.ShapeDtypeStruct((D_NA, L), F32),
                   tok, tok,
                   jax.ShapeDtypeStruct((512, 512), F32), jax.ShapeDtypeStruct((1, 512), F32)],
        scratch_shapes=[_chunk_scratch(tm)],
        compiler_params=_cparams(("arbitrary",)),
    )(y_ssm_c, z_s, y_na_t, z_n, w_glu, b_glu, d_cat)


def _head(x, p, target, cat, w_out, g_post, w_ple_g, g_ple, w_pg, tm=512):
    L = x.shape[0]
    pw = w_ple_g.shape[2]

    def body(x_ref, p_ref, t_ref, cat_ref, wo_ref, gpo_ref, wp_ref, gpl_ref, wg_ref,
             loss_ref, dh1_ref, dcat_ref, dwo_ref, dgpo_ref, dwp_ref, dgpl_ref, dwg_ref):
        @pl.when(pl.program_id(0) == 0)
        def _():
            for r in (loss_ref, dwo_ref, dgpo_ref, dwp_ref, dgpl_ref, dwg_ref):
                r[...] = jnp.zeros_like(r)

        cat_b = cat_ref[...]
        wo, wg = wo_ref[...], wg_ref[...]
        g_po, g_pl = gpo_ref[...], gpl_ref[...]
        mix = _dot(cat_b, wo)
        nm, r2 = _rms(mix)
        h1 = x_ref[...] + nm * g_po
        p_b = p_ref[...].astype(BF16)
        ep = jnp.concatenate([_dot(p_b, wp_ref[j]) for j in range(N_CHIPS)], axis=1)
        ne, r3 = _rms(ep)
        e = ne * g_pl
        h1_b = h1.astype(BF16)
        gate = _sigmoid(_dot(h1_b, wg))
        h2 = h1 + gate * e
        diff = h2 - t_ref[...]
        loss_ref[...] += (0.5 / D_MODEL) * jnp.sum(diff * diff).reshape(1, 1)

        dh2 = diff * (1.0 / D_MODEL)
        de = dh2 * gate
        dgl = (dh2 * e * gate * (1.0 - gate)).astype(BF16)
        dwg_ref[...] += _dot_tn(h1_b, dgl)
        dh1 = dh2 + _dot_nt(dgl, wg)
        dgpl_ref[...] += jnp.sum(de * ne, axis=0, keepdims=True)
        dep = _rms_bwd(de * g_pl, ne, r3).astype(BF16)
        for j in range(N_CHIPS):
            dwp_ref[j] += _dot_tn(p_b, dep[:, j * pw:(j + 1) * pw])
        dgpo_ref[...] += jnp.sum(dh1 * nm, axis=0, keepdims=True)
        dmix = _rms_bwd(dh1 * g_po, nm, r2).astype(BF16)
        dwo_ref[...] += _dot_tn(cat_b, dmix)
        dcat_ref[...] = _dot_nt(dmix, wo)
        dh1_ref[...] = dh1

    tile = lambda w: pl.BlockSpec((tm, w), lambda i: (i, 0))
    const = _resident
    sds = jax.ShapeDtypeStruct
    return pl.pallas_call(
        body, name="head", grid=(L // tm,),
        in_specs=[tile(D_MODEL), tile(D_PLE), tile(D_MODEL), tile(1024), const(1024, D_MODEL), const(1, D_MODEL),
                  const(N_CHIPS, D_PLE, pw), const(1, D_MODEL), const(D_MODEL, D_MODEL)],
        out_specs=[const(1, 1), tile(D_MODEL), tile(1024), const(1024, D_MODEL), const(1, D_MODEL),
                   const(N_CHIPS, D_PLE, pw), const(1, D_MODEL), const(D_MODEL, D_MODEL)],
        out_shape=[sds((1, 1), F32), sds((L, D_MODEL), F32), sds((L, 1024), F32), sds((1024, D_MODEL), F32),
                   sds((1, D_MODEL), F32), sds((N_CHIPS, D_PLE, pw), F32), sds((1, D_MODEL), F32),
                   sds((D_MODEL, D_MODEL), F32)],
        compiler_params=_cparams(("arbitrary",)),
    )(x, p, target, cat, w_out, g_post, w_ple_g, g_ple, w_pg)


def _dproj_specs(tm):
    tile = pl.BlockSpec((tm, 512), lambda i: (i, 0))
    return [_chunk_spec(tm), tile, _heads_t_spec(tm), tile, tile, tile]


def _dproj_tile(refs, scr):
    du_ref, dzs_ref, dqt_ref, dk_ref, dv_ref, dzn_ref = refs
    parts = [_load_chunks(du_ref, scr), dzs_ref[...], dqt_ref[...].T, dk_ref[...], dv_ref[...], dzn_ref[...]]
    return jnp.concatenate([t.astype(BF16) for t in parts], axis=1)


def _in_proj_bwd_w(x, g_pre, dparts, tm=512):
    L = x.shape[0]
    wn = D_IN_PROJ // N_CHIPS

    def body(x_ref, g_ref, *refs):
        dw_ref, scr = refs[-2], refs[-1]

        @pl.when(pl.program_id(0) == 0)
        def _():
            dw_ref[...] = jnp.zeros_like(dw_ref)

        n, _ = _rms(x_ref[...])
        hn = (n * g_ref[...]).astype(BF16)
        dproj = _dproj_tile(refs[:-2], scr)
        for j in range(N_CHIPS):
            dw_ref[j] += _dot_tn(hn, dproj[:, j * wn:(j + 1) * wn])

    return pl.pallas_call(
        body, name="in_proj_bwd_w", grid=(L // tm,),
        in_specs=[pl.BlockSpec((tm, D_MODEL), lambda i: (i, 0)), _resident(1, D_MODEL)] + _dproj_specs(tm),
        out_specs=_resident(N_CHIPS, D_MODEL, wn),
        out_shape=jax.ShapeDtypeStruct((N_CHIPS, D_MODEL, wn), F32),
        scratch_shapes=[_chunk_scratch(tm)],
        compiler_params=_cparams(("arbitrary",)),
    )(x, g_pre, *dparts)


def _in_proj_bwd_x(x, g_pre, w_in_g, d_h1, dparts, tm=512):
    L = x.shape[0]
    wn = w_in_g.shape[2]

    def body(x_ref, g_ref, w_ref, dh1_ref, *refs):
        dx_ref, dg_ref, scr = refs[-3], refs[-2], refs[-1]

        @pl.when(pl.program_id(0) == 0)
        def _():
            dg_ref[...] = jnp.zeros_like(dg_ref)

        n, r = _rms(x_ref[...])
        dproj = _dproj_tile(refs[:-3], scr)
        dhn = _dot_nt(dproj[:, 0:wn], w_ref[0])
        for j in range(1, N_CHIPS):
            dhn = dhn + _dot_nt(dproj[:, j * wn:(j + 1) * wn], w_ref[j])
        dg_ref[...] += jnp.sum(dhn * n, axis=0, keepdims=True)
        dx_ref[...] = dh1_ref[...] + _rms_bwd(dhn * g_ref[...], n, r)

    wide = pl.BlockSpec((tm, D_MODEL), lambda i: (i, 0))
    vec = _resident(1, D_MODEL)
    return pl.pallas_call(
        body, name="in_proj_bwd_x", grid=(L // tm,),
        in_specs=[wide, vec, _resident(N_CHIPS, D_MODEL, wn), wide] + _dproj_specs(tm),
        out_specs=[wide, vec],
        out_shape=[jax.ShapeDtypeStruct((L, D_MODEL), F32), jax.ShapeDtypeStruct((1, D_MODEL), F32)],
        scratch_shapes=[_chunk_scratch(tm)],
        compiler_params=_cparams(("arbitrary",)),
    )(x, g_pre, w_in_g, d_h1, *dparts)


def _mesh_position():
    x, y, c = lax.axis_index("x"), lax.axis_index("y"), lax.axis_index("c")
    chips = [(1 - x, y), (x, 1 - y), (1 - x, 1 - y)]
    return x, y, c, chips


def _chip_index(cx, cy):
    return 2 * cx + cy


def _hbm_specs(n):
    return [pl.BlockSpec(memory_space=pl.ANY)] * n


def _gather_chips(shards, name):
    n = len(shards)

    def body(*refs):
        gather = _ChipGather(refs[:n], refs[n:2 * n], refs[2 * n:])
        gather.start()
        gather.forward()
        gather.finish()

    return pl.pallas_call(
        body, name=name, in_specs=_hbm_specs(n), out_specs=_hbm_specs(n),
        out_shape=_gather_out_shapes(shards), scratch_shapes=_gather_semaphores(n),
        compiler_params=pltpu.CompilerParams(has_side_effects=True),
    )(*shards)


def _gather_out_shapes(shards):
    return [jax.ShapeDtypeStruct((N_CHIPS,) + s.shape, s.dtype) for s in shards]


def _gather_semaphores(n):
    sem = pltpu.SemaphoreType.DMA
    return [sem((n, 3)), sem((n, 3)), sem((n, 3)), sem((n, 3)), sem((n,)), sem((n,))]


class _ChipGather:
    def __init__(self, ins, outs, sems):
        self.ins, self.outs = ins, outs
        self.send1, self.recv1, self.send2, self.recv2, self.send3, self.recv3 = sems
        self.x, self.y, self.c, self.chips = _mesh_position()
        self.me = _chip_index(self.x, self.y)
        self.sibling = (self.x, self.y, 1 - self.c)

    def _half(self, a, chip, core):
        hr = self.outs[a].shape[1] // 2
        return self.outs[a].at[chip, pl.ds(core * hr, hr)]

    def _own(self, a):
        return pltpu.make_async_remote_copy(
            src_ref=self.ins[a], dst_ref=self.outs[a].at[self.me], send_sem=self.send3.at[a], recv_sem=self.recv3.at[a],
            device_id=self.sibling, device_id_type=MESH)

    def _to_chip(self, a, j):
        hr = self.ins[a].shape[0] // 2
        return pltpu.make_async_remote_copy(
            src_ref=self.ins[a].at[pl.ds(self.c * hr, hr)], dst_ref=self._half(a, self.me, self.c),
            send_sem=self.send1.at[a, j], recv_sem=self.recv1.at[a, j], device_id=(*self.chips[j], self.c), device_id_type=MESH)

    def _from_chip(self, a, j):
        landed = self._half(a, _chip_index(*self.chips[j]), self.c)
        return pltpu.make_async_remote_copy(
            src_ref=landed, dst_ref=landed, send_sem=self.send1.at[a, j], recv_sem=self.recv1.at[a, j],
            device_id=(*self.chips[j], self.c), device_id_type=MESH)

    def _to_sibling(self, a, j, core):
        part = self._half(a, _chip_index(*self.chips[j]), core)
        return pltpu.make_async_remote_copy(
            src_ref=part, dst_ref=part, send_sem=self.send2.at[a, j], recv_sem=self.recv2.at[a, j],
            device_id=self.sibling, device_id_type=MESH)

    def _each(self):
        return [(a, j) for a in range(len(self.ins)) for j in range(3)]

    def start(self):
        for a in range(len(self.ins)):
            self._own(a).start()
        for a, j in self._each():
            self._to_chip(a, j).start()

    def forward(self):
        for a, j in self._each():
            self._from_chip(a, j).wait_recv()
            self._to_sibling(a, j, self.c).start()

    def finish(self):
        for a, j in self._each():
            self._to_sibling(a, j, 1 - self.c).wait_recv()
        for a, j in self._each():
            self._to_chip(a, j).wait_send()
            self._to_sibling(a, j, self.c).wait_send()
        for a in range(len(self.ins)):
            self._own(a).wait()


def _pair_exchange(grads):
    n = len(grads)

    def body(*refs):
        ins, outs = refs[:n], refs[n:2 * n]
        send, recv = refs[2 * n:]
        x, y, c, _ = _mesh_position()
        copies = []
        for a in range(n):
            hr = ins[a].shape[1] // 2
            cp = pltpu.make_async_remote_copy(
                src_ref=ins[a].at[:, pl.ds((1 - c) * hr, hr)], dst_ref=outs[a],
                send_sem=send.at[a], recv_sem=recv.at[a], device_id=(x, y, 1 - c), device_id_type=MESH)
            cp.start()
            copies.append(cp)
        for cp in copies:
            cp.wait()

    sem = pltpu.SemaphoreType.DMA
    return pl.pallas_call(
        body, name="pair_exchange", in_specs=_hbm_specs(n), out_specs=_hbm_specs(n),
        out_shape=[jax.ShapeDtypeStruct((g.shape[0], g.shape[1] // 2, g.shape[2]), g.dtype) for g in grads],
        scratch_shapes=[sem((n,)), sem((n,))],
        compiler_params=pltpu.CompilerParams(has_side_effects=True),
    )(*grads)


def _pair_add(core, grad, other, tr, out_dtype):
    hr = other.shape[1]
    cdim = other.shape[2]
    nb = hr // tr

    def body(core_ref, g_ref, o_ref, out_ref):
        out_ref[...] = (g_ref[...] + o_ref[...]).astype(out_dtype)

    return pl.pallas_call(
        body, name="pair_add",
        grid_spec=pltpu.PrefetchScalarGridSpec(
            num_scalar_prefetch=1, grid=(N_CHIPS, nb),
            in_specs=[pl.BlockSpec((1, tr, cdim), lambda j, i, core_ref: (j, core_ref[0] * nb + i, 0)),
                      pl.BlockSpec((1, tr, cdim), lambda j, i, core_ref: (j, i, 0))],
            out_specs=pl.BlockSpec((1, tr, cdim), lambda j, i, core_ref: (j, i, 0))),
        out_shape=jax.ShapeDtypeStruct(other.shape, out_dtype),
        compiler_params=_cparams(("arbitrary", "arbitrary")),
    )(core, grad, other)


def _chip_scatter(parts):
    n = len(parts)

    def body(*refs):
        ins, outs = refs[:n], refs[n:2 * n]
        send, recv, load_sem, store_sem = refs[2 * n:2 * n + 4]
        staged = refs[2 * n + 4:]
        x, y, c, chips = _mesh_position()
        me = _chip_index(x, y)
        copies, loads = [], []
        for a in range(n):
            ld = pltpu.make_async_copy(ins[a].at[me], staged[a], load_sem.at[a])
            ld.start()
            loads.append(ld)
            for j, chip in enumerate(chips):
                cp = pltpu.make_async_remote_copy(
                    src_ref=ins[a].at[_chip_index(*chip)], dst_ref=outs[a].at[me],
                    send_sem=send.at[a, j], recv_sem=recv.at[a, j], device_id=(*chip, c), device_id_type=MESH)
                cp.start()
                copies.append(cp)
        for a in range(n):
            loads[a].wait()
            st = pltpu.make_async_copy(staged[a], outs[a].at[me], store_sem.at[a])
            st.start()
            copies.append(st)
        for cp in copies:
            cp.wait()

    sem = pltpu.SemaphoreType.DMA
    return pl.pallas_call(
        body, name="chip_scatter", in_specs=_hbm_specs(n), out_specs=_hbm_specs(n),
        out_shape=[jax.ShapeDtypeStruct(p.shape, p.dtype) for p in parts],
        scratch_shapes=[sem((n, 3)), sem((n, 3)), sem((n,)), sem((n,))] + [pltpu.VMEM(p.shape[1:], p.dtype) for p in parts],
        compiler_params=pltpu.CompilerParams(has_side_effects=True),
    )(*parts)


def _chip_add(core, recv, tr):
    hr, cdim = recv.shape[1], recv.shape[2]
    nb = hr // tr

    def body(core_ref, r_ref, out_ref):
        out_ref[...] = ((r_ref[0].astype(F32) + r_ref[1].astype(F32)) + r_ref[2].astype(F32)) + r_ref[3].astype(F32)

    return pl.pallas_call(
        body, name="chip_add",
        grid_spec=pltpu.PrefetchScalarGridSpec(
            num_scalar_prefetch=1, grid=(nb,),
            in_specs=[pl.BlockSpec((N_CHIPS, tr, cdim), lambda i, core_ref: (0, i, 0))],
            out_specs=pl.BlockSpec((tr, cdim), lambda i, core_ref: (core_ref[0] * nb + i, 0))),
        out_shape=jax.ShapeDtypeStruct((2 * hr, cdim), F32),
        compiler_params=_cparams(("arbitrary",)),
    )(core, recv)


def _pair_gather(fulls):
    n = len(fulls)

    def body(*refs):
        outs = refs[n:2 * n]
        send, recv = refs[2 * n:]
        x, y, c, _ = _mesh_position()
        copies = []
        for a in range(n):
            hr = outs[a].shape[0] // 2
            mine = outs[a].at[pl.ds(c * hr, hr)]
            cp = pltpu.make_async_remote_copy(
                src_ref=mine, dst_ref=mine, send_sem=send.at[a], recv_sem=recv.at[a],
                device_id=(x, y, 1 - c), device_id_type=MESH)
            cp.start()
            copies.append(cp)
        for cp in copies:
            cp.wait()

    sem = pltpu.SemaphoreType.DMA
    return pl.pallas_call(
        body, name="pair_gather", in_specs=_hbm_specs(n), out_specs=_hbm_specs(n),
        out_shape=[jax.ShapeDtypeStruct(f.shape, f.dtype) for f in fulls],
        input_output_aliases={a: a for a in range(n)},
        scratch_shapes=[sem((n,)), sem((n,))],
        compiler_params=pltpu.CompilerParams(has_side_effects=True),
    )(*fulls)


def _row_tile(rows):
    for t in (512, 256, 128, 64, 32, 16, 8):
        if rows % t == 0:
            return t
    raise ValueError(rows)


def _reduce_scatter(grads, ici_dtypes):
    core = lax.axis_index("c").astype(jnp.int32).reshape(1)
    others = _pair_exchange(grads)
    pair = [_pair_add(core, g, o, min(128, _row_tile(o.shape[1])), dt) for g, o, dt in zip(grads, others, ici_dtypes)]
    landed = _chip_scatter(pair)
    return _pair_gather([_chip_add(core, r, _row_tile(r.shape[1])) for r in landed])


def _adamw(w, g, m, v):
    rows, cols = w.shape
    one_block = rows % 8 != 0 or rows * max(cols, 128) * 4 <= (1 << 20)
    tr = rows if one_block else _row_tile(rows)

    def body(w_ref, g_ref, m_ref, v_ref, d_ref, nm_ref, nv_ref):
        g_ = g_ref[...]
        m_ = ADAM_B1 * m_ref[...] + (1.0 - ADAM_B1) * g_
        v_ = ADAM_B2 * v_ref[...] + (1.0 - ADAM_B2) * (g_ * g_)
        m_hat = m_ / (1.0 - ADAM_B1 ** ADAM_STEP)
        v_hat = v_ / (1.0 - ADAM_B2 ** ADAM_STEP)
        d_ref[...] = -ADAM_LR * (m_hat / (jnp.sqrt(v_hat) + ADAM_EPS) + ADAM_WD * w_ref[...])
        nm_ref[...] = m_
        nv_ref[...] = v_

    spec = pl.BlockSpec((tr, cols), lambda i: (i, 0))
    shp = jax.ShapeDtypeStruct((rows, cols), F32)
    return pl.pallas_call(
        body, name="adamw", grid=(rows // tr,), in_specs=[spec] * 4, out_specs=[spec] * 3,
        out_shape=[shp] * 3, compiler_params=_cparams(("arbitrary",)),
    )(w, g, m, v)


_SMALL = ["norm_pre", "norm_post", "ssm_a_re", "ssm_a_im", "ssm_log_dt", "ssm_b_re", "ssm_b_im",
          "ssm_c_re", "ssm_c_im", "ssm_d", "b_glu", "na_rpb", "ple_norm"]
_BIG = ["w_in", "w_glu", "w_out", "w_ple", "w_ple_gate"]
_WEIGHTS = ["norm_pre", "norm_post", "w_in", "ssm_a_re", "ssm_a_im", "ssm_log_dt", "ssm_b_re", "ssm_b_im",
            "ssm_c_re", "ssm_c_im", "ssm_d", "w_glu", "b_glu", "na_rpb", "w_out", "w_ple", "ple_norm", "w_ple_gate"]
_SMALL_ROWS = 2176


def _pack_small(tensors, tail=None):
    parts = [tensors[n].reshape(-1) for n in _SMALL] + ([] if tail is None else [tail.reshape(-1)])
    flat = jnp.concatenate(parts)
    flat = jnp.pad(flat, (0, _SMALL_ROWS * 128 - flat.shape[0]))
    return flat.reshape(_SMALL_ROWS, 128)


def _unpack_small(packed, shapes):
    flat = packed.reshape(-1)
    out, off = {}, 0
    for n in _SMALL:
        size = int(np.prod(shapes[n]))
        out[n] = flat[off:off + size].reshape(shapes[n])
        off += size
    return out


def _local_grads(x, p, target, wts):
    ssm_names = ["ssm_a_re", "ssm_a_im", "ssm_log_dt", "ssm_b_re", "ssm_b_im", "ssm_c_re", "ssm_c_im", "ssm_d"]
    ssm_params = [wts[n][0] for n in ssm_names]
    blk, blk_vjp = jax.vjp(_ssm_block_params, *ssm_params)
    shard = lambda n: wts[n][0].astype(BF16)
    (m_mat, ws_mat, wot_mat, a16), (w_in_g,) = _ssm_chunk_matrices(blk, [shard("w_in")])
    seq = x.shape[0]
    bias_rows, bias_rows_vjp = jax.vjp(_na_bias_rows, wts["na_rpb"][0])
    bias_tab = _na_bias_table(bias_rows, seq // GRID_W)

    (u_c, z_s, q_t, q, k_t, k, v_t, v, z_n), gathered = _in_proj(
        x, wts["norm_pre"], w_in_g, [shard(n) for n in _BIG if n != "w_in"])
    w_glu, w_out, w_ple_g, w_pg = (gathered[0].reshape(512, 512), gathered[1].reshape(1024, 1024), gathered[2],
                                   gathered[3].reshape(1024, 1024))
    s_in = _block_matmul([(u_c, ws_mat, False)], "ssm_chunk_states")
    s_prev = _ssm_state_scan(s_in, a16)
    y_ssm_c = _block_matmul([(u_c, m_mat, False), (s_prev, wot_mat, True)], "ssm_chunk_out")
    y_na_t = _na_fwd(q_t, k, v_t, bias_tab)
    cat = _branch_fwd(y_ssm_c, z_s, y_na_t, z_n, w_glu, wts["b_glu"])

    (loss, d_h1, d_cat, d_w_out, d_g_post, d_w_ple, d_g_ple, d_w_pg) = _head(
        x, p, target, cat, w_out, wts["norm_post"], w_ple_g, wts["ple_norm"], w_pg)
    dy_c, d_z_s, d_y_na_t, d_y_na, d_z_n, d_w_glu, d_b_glu = _branch_bwd(
        y_ssm_c, z_s, y_na_t, z_n, w_glu, wts["b_glu"], d_cat)
    d_q_t, d_k, d_v, d_bias_tab = _na_bwd(q_t, q, k_t, k, v, bias_tab, y_na_t, d_y_na_t, d_y_na)

    d_prev = _block_matmul([(dy_c, wot_mat, False)], "ssm_bwd_states")
    g_st, d_a16 = _ssm_state_scan_bwd(d_prev, s_prev, a16)
    d_u_c = _block_matmul([(dy_c, m_mat, True), (g_st, ws_mat, True)], "ssm_bwd_in", out_dtype=BF16)
    d_m = _block_matmul_tn(u_c, dy_c, "ssm_grad_m")
    d_ws = _block_matmul_tn(u_c, g_st, "ssm_grad_ws")
    d_wot = _block_matmul_tn(dy_c, s_prev, "ssm_grad_wot")
    d_ssm = blk_vjp(tuple(_ssm_chunk_matrices_bwd(blk, d_m, d_ws, d_wot, d_a16)))
    (d_rpb,) = bias_rows_vjp(_na_bias_table_bwd(d_bias_tab, seq // GRID_W))

    dparts = [d_u_c, d_z_s, d_q_t, d_k, d_v, d_z_n]
    d_w_in = _in_proj_bwd_w(x, wts["norm_pre"], dparts)
    grad_x, d_g_pre = _in_proj_bwd_x(x, wts["norm_pre"], w_in_g, d_h1, dparts)

    small = {"norm_pre": d_g_pre, "norm_post": d_g_post, "b_glu": d_b_glu, "na_rpb": d_rpb, "ple_norm": d_g_ple}
    for n, g in zip(ssm_names, d_ssm):
        small[n] = g
    big = {"w_in": d_w_in, "w_glu": d_w_glu.reshape(N_CHIPS, 128, 512), "w_out": d_w_out.reshape(N_CHIPS, 256, 1024),
           "w_ple": d_w_ple, "w_ple_gate": d_w_pg.reshape(N_CHIPS, 256, 1024)}
    return loss, grad_x, small, big


def kernel(x, p, norm_pre, norm_post, w_in, ssm_a_re, ssm_a_im, ssm_log_dt, ssm_b_re, ssm_b_im, ssm_c_re, ssm_c_im, ssm_d, w_glu, b_glu, na_rpb, w_out, w_ple, ple_norm, w_ple_gate, loss_target, m_norm_pre, m_norm_post, m_w_in, m_ssm_a_re, m_ssm_a_im, m_ssm_log_dt, m_ssm_b_re, m_ssm_b_im, m_ssm_c_re, m_ssm_c_im, m_ssm_d, m_w_glu, m_b_glu, m_na_rpb, m_w_out, m_w_ple, m_ple_norm, m_w_ple_gate, v_norm_pre, v_norm_post, v_w_in, v_ssm_a_re, v_ssm_a_im, v_ssm_log_dt, v_ssm_b_re, v_ssm_b_im, v_ssm_c_re, v_ssm_c_im, v_ssm_d, v_w_glu, v_b_glu, v_na_rpb, v_w_out, v_w_ple, v_ple_norm, v_w_ple_gate):
    wts = dict(norm_pre=norm_pre, norm_post=norm_post, w_in=w_in, ssm_a_re=ssm_a_re, ssm_a_im=ssm_a_im,
               ssm_log_dt=ssm_log_dt, ssm_b_re=ssm_b_re, ssm_b_im=ssm_b_im, ssm_c_re=ssm_c_re, ssm_c_im=ssm_c_im,
               ssm_d=ssm_d, w_glu=w_glu, b_glu=b_glu, na_rpb=na_rpb, w_out=w_out, w_ple=w_ple, ple_norm=ple_norm,
               w_ple_gate=w_ple_gate)
    mom_m = dict(norm_pre=m_norm_pre, norm_post=m_norm_post, w_in=m_w_in, ssm_a_re=m_ssm_a_re, ssm_a_im=m_ssm_a_im,
                 ssm_log_dt=m_ssm_log_dt, ssm_b_re=m_ssm_b_re, ssm_b_im=m_ssm_b_im, ssm_c_re=m_ssm_c_re,
                 ssm_c_im=m_ssm_c_im, ssm_d=m_ssm_d, w_glu=m_w_glu, b_glu=m_b_glu, na_rpb=m_na_rpb, w_out=m_w_out,
                 w_ple=m_w_ple, ple_norm=m_ple_norm, w_ple_gate=m_w_ple_gate)
    mom_v = dict(norm_pre=v_norm_pre, norm_post=v_norm_post, w_in=v_w_in, ssm_a_re=v_ssm_a_re, ssm_a_im=v_ssm_a_im,
                 ssm_log_dt=v_ssm_log_dt, ssm_b_re=v_ssm_b_re, ssm_b_im=v_ssm_b_im, ssm_c_re=v_ssm_c_re,
                 ssm_c_im=v_ssm_c_im, ssm_d=v_ssm_d, w_glu=v_w_glu, b_glu=v_b_glu, na_rpb=v_na_rpb, w_out=v_w_out,
                 w_ple=v_w_ple, ple_norm=v_ple_norm, w_ple_gate=v_w_ple_gate)

    loss_part, grad_x, small, big = _local_grads(x[0], p[0, 0], loss_target[0], wts)

    small_packed = _pack_small(small, tail=loss_part).reshape(N_CHIPS, _SMALL_ROWS // N_CHIPS, 128)
    reduced = _reduce_scatter([big[n] for n in _BIG] + [small_packed], [BF16] * len(_BIG) + [F32])
    grads = dict(zip(_BIG, reduced[:-1]))
    (small_all,) = _gather_chips([reduced[-1]], "gather_small_grads")
    small_all = small_all.reshape(_SMALL_ROWS, 128)
    loss = small_all.reshape(-1)[sum(int(np.prod(wts[n].shape)) for n in _SMALL)]

    delta, new_m, new_v = {}, {}, {}
    for n in _BIG:
        shp = wts[n].shape
        d_, m_, v_ = _adamw(wts[n][0], grads[n], mom_m[n][0], mom_v[n][0])
        grads[n] = grads[n].reshape(shp)
        delta[n], new_m[n], new_v[n] = d_.reshape(shp), m_.reshape(shp), v_.reshape(shp)
    grads.update(_unpack_small(small_all, {n: wts[n].shape for n in _SMALL}))
    for n in _SMALL:
        shp = wts[n].shape
        rows_cols = (int(np.prod(shp[:-1])), shp[-1])
        d_, m_, v_ = _adamw(*[t.reshape(rows_cols) for t in (wts[n], grads[n], mom_m[n], mom_v[n])])
        delta[n], new_m[n], new_v[n] = d_.reshape(shp), m_.reshape(shp), v_.reshape(shp)

    return (loss, grad_x[None], *[grads[n] for n in _WEIGHTS], *[delta[n] for n in _WEIGHTS],
            *[new_m[n] for n in _WEIGHTS], *[new_v[n] for n in _WEIGHTS])
```

```python
import functools
import math

import jax
import jax.numpy as jnp
import numpy as np
from jax import lax
from jax.experimental import pallas as pl
from jax.experimental.pallas import tpu as pltpu

F32 = jnp.float32
BF16 = jnp.bfloat16

D_MODEL = 1024
D_PLE = 256
GRID_W = 64
D_SSM = 512
SSM_GROUP = 16
N_GROUPS = 32
SSM_STATE = 64
D_NA = 512
NA_HEADS = 8
NA_HEAD_DIM = 64
NA_ROWS = 8
NA_COLS = 16
D_IN_PROJ = 3072
EPS = 1e-6

CHUNK = 16
GROUPS_PER_BLOCK = 8
N_BLOCKS = N_GROUPS // GROUPS_PER_BLOCK
BLOCK_CH = GROUPS_PER_BLOCK * SSM_GROUP
BLOCK_ST = GROUPS_PER_BLOCK * SSM_STATE
CHUNK_W = CHUNK * BLOCK_CH
STATE_W = 4 * BLOCK_ST

N_CHIPS = 4
MESH = pl.DeviceIdType.MESH

ADAM_LR = 0.001
ADAM_B1 = 0.9
ADAM_B2 = 0.999
ADAM_EPS = 1e-08
ADAM_WD = 0.01
ADAM_STEP = 10

VMEM_LIMIT = 52 * 1024 * 1024
HIGHEST = lax.Precision.HIGHEST


def _cparams(sem=None, **kw):
    if sem is not None:
        kw["dimension_semantics"] = sem
    return pltpu.CompilerParams(vmem_limit_bytes=VMEM_LIMIT, **kw)


def _resident(*shape):
    return pl.BlockSpec(shape, lambda *_: (0,) * len(shape), pipeline_mode=pl.Buffered(1))


def _dot(a, b, dims=((1,), (0,))):
    return lax.dot_general(a, b, (dims, ((), ())), preferred_element_type=F32)


def _dot_nt(a, b):
    return _dot(a, b, ((1,), (1,)))


def _dot_tn(a, b):
    return _dot(a, b, ((0,), (0,)))


def _sigmoid(x):
    return 1.0 / (1.0 + jnp.exp(-x))


_GELU_C = math.sqrt(2.0 / math.pi)


def _gelu_parts(x):
    inner = _GELU_C * (x + 0.044715 * (x * x * x))
    t = jnp.tanh(inner)
    return 0.5 * x * (1.0 + t), t


def _gelu_grad(x, t):
    return 0.5 * (1.0 + t) + 0.5 * x * (1.0 - t * t) * (_GELU_C * (1.0 + 3.0 * 0.044715 * x * x))


def _silu_parts(z):
    s = _sigmoid(z)
    return z * s, s


def _silu_grad(z, s):
    return s * (1.0 + z * (1.0 - s))


def _rms(x):
    r = lax.rsqrt(jnp.mean(x * x, axis=-1, keepdims=True) + EPS)
    return x * r, r


def _rms_bwd(dn, n, r):
    return r * (dn - n * jnp.mean(dn * n, axis=-1, keepdims=True))


def _chunk_scratch(tm):
    return pltpu.VMEM((N_BLOCKS, tm, BLOCK_CH), F32)


def _store_chunks(val, scr, c_ref, dtype):
    nc = scr.shape[1] // CHUNK
    for b in range(N_BLOCKS):
        scr[b] = val[:, b * BLOCK_CH:(b + 1) * BLOCK_CH]
        for j in range(CHUNK):
            c_ref[b, :, j * BLOCK_CH:(j + 1) * BLOCK_CH] = scr[b, pl.ds(j, nc, stride=CHUNK), :].astype(dtype)


def _load_chunks(c_ref, scr):
    nc = scr.shape[1] // CHUNK
    for b in range(N_BLOCKS):
        for j in range(CHUNK):
            scr[b, pl.ds(j, nc, stride=CHUNK), :] = c_ref[b, :, j * BLOCK_CH:(j + 1) * BLOCK_CH].astype(F32)
    return jnp.concatenate([scr[b] for b in range(N_BLOCKS)], axis=1)


def _chunk_spec(tm):
    return pl.BlockSpec((N_BLOCKS, tm // CHUNK, CHUNK_W), lambda i: (0, i, 0))


def _heads_t_spec(tm):
    return pl.BlockSpec((D_NA, tm), lambda i: (0, i))


def _in_proj(x, g_pre, w_in_g, shards, tm=512):
    L = x.shape[0]
    wn = w_in_g.shape[2]
    n_sh = len(shards)
    steps = L // tm

    def body(*refs):
        x_ref, g_ref, w_ref = refs[:3]
        uc_ref, zs_ref, qt_ref, q_ref, kt_ref, k_ref, vt_ref, v_ref, zn_ref = refs[3 + n_sh:12 + n_sh]
        u_scr = refs[12 + 2 * n_sh]
        gather = _ChipGather(refs[3:3 + n_sh], refs[12 + n_sh:12 + 2 * n_sh], refs[13 + 2 * n_sh:])
        step = pl.program_id(0)
        pl.when(step == 0)(gather.start)
        pl.when(step == steps // 2)(gather.forward)
        pl.when(step == steps - 1)(gather.finish)
        n, _ = _rms(x_ref[...])
        hn = (n * g_ref[...]).astype(BF16)
        proj = jnp.concatenate([_dot(hn, w_ref[j]) for j in range(N_CHIPS)], axis=1)
        _store_chunks(proj[:, 0:512], u_scr, uc_ref, BF16)
        zs_ref[...] = proj[:, 512:1024]
        q = proj[:, 1024:1536] * (NA_HEAD_DIM ** -0.5)
        for val, t_ref, n_ref in ((q, qt_ref, q_ref), (proj[:, 1536:2048], kt_ref, k_ref), (proj[:, 2048:2560], vt_ref, v_ref)):
            t_ref[...] = val.T.astype(BF16)
            n_ref[...] = val.astype(BF16)
        zn_ref[...] = proj[:, 2560:3072]

    tok = jax.ShapeDtypeStruct((L, 512), F32)
    tr = jax.ShapeDtypeStruct((D_NA, L), BF16)
    hm = jax.ShapeDtypeStruct((L, D_NA), BF16)
    tspec = pl.BlockSpec((tm, 512), lambda i: (i, 0))
    outs = pl.pallas_call(
        body, name="in_proj", grid=(steps,),
        in_specs=[pl.BlockSpec((tm, D_MODEL), lambda i: (i, 0)),
                  _resident(1, D_MODEL), _resident(N_CHIPS, D_MODEL, wn)] + _hbm_specs(n_sh),
        out_specs=[_chunk_spec(tm), tspec] + [_heads_t_spec(tm), tspec] * 3 + [tspec] + _hbm_specs(n_sh),
        out_shape=[jax.ShapeDtypeStruct((N_BLOCKS, L // CHUNK, CHUNK_W), BF16), tok, tr, hm, tr, hm, tr, hm, tok]
        + _gather_out_shapes(shards),
        scratch_shapes=[_chunk_scratch(tm)] + _gather_semaphores(n_sh),
        compiler_params=_cparams(("arbitrary",), has_side_effects=True),
    )(x, g_pre, w_in_g, *shards)
    return outs[:9], outs[9:]


def _ssm_block_params(a_re, a_im, log_dt, b_re, b_im, c_re, c_im, d):
    eye_g = jnp.eye(GROUPS_PER_BLOCK, dtype=F32)[None, None, :, None, :, None]

    def lanes(t):
        return t.reshape(2, N_BLOCKS, 1, BLOCK_ST)

    def expand(t):
        return (t[:, :, :, :, None, :] * eye_g).reshape(2, N_BLOCKS, BLOCK_CH, BLOCK_ST)

    b_shape = (2, N_BLOCKS, GROUPS_PER_BLOCK, SSM_STATE, SSM_GROUP)
    c_shape = (2, N_BLOCKS, GROUPS_PER_BLOCK, SSM_GROUP, SSM_STATE)
    return (lanes(a_re), lanes(a_im), lanes(jnp.broadcast_to(log_dt[..., None], a_re.shape)),
            expand(b_re.reshape(b_shape).transpose(0, 1, 2, 4, 3)), expand(b_im.reshape(b_shape).transpose(0, 1, 2, 4, 3)),
            expand(c_re.reshape(c_shape)), expand(c_im.reshape(c_shape)), d.reshape(N_BLOCKS, 1, BLOCK_CH))


def _ssm_discretise(ar, ai, ldt):
    dt = jnp.exp(ldt)
    mag = jnp.exp(dt * ar)
    abr = mag * jnp.cos(dt * ai)
    abi = mag * jnp.sin(dt * ai)
    num_re = abr - 1.0
    num_im = abi
    denom = ar * ar + ai * ai
    coef_re = (num_re * ar + num_im * ai) / denom
    coef_im = (num_im * ar - num_re * ai) / denom
    return abr, abi, coef_re, coef_im


_POW_ROWS = 24


def _ssm_fill_powers(ar_ref, ai_ref, ldt_ref, br_ref, bi_ref, pw_ref, bbar_ref):
    for d in range(2):
        abr, abi, cfr, cfi = _ssm_discretise(ar_ref[d, 0], ai_ref[d, 0], ldt_ref[d, 0])
        bbar_ref[d, 0] = cfr * br_ref[d, 0] - cfi * bi_ref[d, 0]
        bbar_ref[d, 1] = cfr * bi_ref[d, 0] + cfi * br_ref[d, 0]
        pr, pi = jnp.ones_like(abr), jnp.zeros_like(abi)
        for t in range(CHUNK + 1):
            pw_ref[d, 0, t:t + 1, :] = pr
            pw_ref[d, 1, t:t + 1, :] = pi
            pr, pi = pr * abr - pi * abi, pr * abi + pi * abr


def _dot_rounded(a, b, dims=((1,), (0,))):
    return _dot(a.astype(BF16), b.astype(BF16), dims)


def _ssm_stack_inputs(d, pw_ref, bbar_ref, xs_ref):
    for t in range(CHUNK):
        pr, pi = pw_ref[d, 0, t:t + 1, :], pw_ref[d, 1, t:t + 1, :]
        xs_ref[0, t * BLOCK_CH:(t + 1) * BLOCK_CH, :] = bbar_ref[d, 0] * pr - bbar_ref[d, 1] * pi
        xs_ref[1, t * BLOCK_CH:(t + 1) * BLOCK_CH, :] = bbar_ref[d, 0] * pi + bbar_ref[d, 1] * pr


def _eye(n):
    return (lax.broadcasted_iota(jnp.int32, (n, n), 0) == lax.broadcasted_iota(jnp.int32, (n, n), 1)).astype(F32)


def _ssm_param_specs():
    vec = pl.BlockSpec((2, 1, 1, BLOCK_ST), lambda b, j: (0, b, 0, 0))
    mat = pl.BlockSpec((2, 1, BLOCK_CH, BLOCK_ST), lambda b, j: (0, b, 0, 0))
    return [vec, vec, vec, mat, mat, mat, mat, pl.BlockSpec((1, 1, BLOCK_CH), lambda b, j: (b, 0, 0))]


def _ssm_chunk_matrices(blk, shards):
    n = len(shards)

    def body(*refs):
        ar_ref, ai_ref, ldt_ref, br_ref, bi_ref, cr_ref, ci_ref, d_ref = refs[:8]
        m_ref, ws_ref, wot_ref, a16_ref = refs[8 + n:12 + n]
        pw_ref, bbar_ref, lag_ref, xs_ref = refs[12 + 2 * n:16 + 2 * n]
        gather = _ChipGather(refs[8:8 + n], refs[12 + n:12 + 2 * n], refs[16 + 2 * n:])
        b, j = pl.program_id(0), pl.program_id(1)
        pl.when((b == 0) & (j == 0))(gather.start)
        pl.when((b == N_BLOCKS - 1) & (j == 0))(gather.forward)
        pl.when((b == N_BLOCKS - 1) & (j == CHUNK - 1))(gather.finish)

        @pl.when(j == 0)
        def _():
            _ssm_fill_powers(ar_ref, ai_ref, ldt_ref, br_ref, bi_ref, pw_ref, bbar_ref)
            zero_lag = d_ref[0] * _eye(BLOCK_CH)
            for d in range(2):
                _ssm_stack_inputs(d, pw_ref, bbar_ref, xs_ref)
                taps = (_dot_rounded(xs_ref[0], cr_ref[d, 0], ((1,), (1,)))
                        - _dot_rounded(xs_ref[1], ci_ref[d, 0], ((1,), (1,))))
                zero_lag = zero_lag + taps[0:BLOCK_CH]
                for t in range(1, CHUNK):
                    lag_ref[CHUNK - 1 + t if d == 0 else CHUNK - 1 - t] = taps[t * BLOCK_CH:(t + 1) * BLOCK_CH]
            lag_ref[CHUNK - 1] = zero_lag
            a16_ref[0] = jnp.concatenate([pw_ref[d, ri, CHUNK:CHUNK + 1, :] for d in range(2) for ri in range(2)], axis=1)

        m_ref[0] = jnp.concatenate([lag_ref[jp - j + CHUNK - 1] for jp in range(CHUNK)], axis=1).astype(BF16)

        def power(d, t):
            return pw_ref[d, 0, pl.ds(t, 1), :], pw_ref[d, 1, pl.ds(t, 1), :]

        parts = []
        for d, t in ((0, CHUNK - 1 - j), (1, j)):
            pr, pi = power(d, t)
            parts += [bbar_ref[d, 0] * pr - bbar_ref[d, 1] * pi, bbar_ref[d, 0] * pi + bbar_ref[d, 1] * pr]
        ws_ref[0] = jnp.concatenate(parts, axis=1).astype(BF16)
        parts = []
        for d, t in ((0, j + 1), (1, CHUNK - j)):
            pr, pi = power(d, t)
            parts += [cr_ref[d, 0] * pr - ci_ref[d, 0] * pi, -cr_ref[d, 0] * pi - ci_ref[d, 0] * pr]
        wot_ref[0] = jnp.concatenate(parts, axis=1).astype(BF16)

    row = pl.BlockSpec((1, BLOCK_CH, CHUNK_W), lambda b, j: (b, j, 0))
    mat = jax.ShapeDtypeStruct((N_BLOCKS, CHUNK_W, CHUNK_W), BF16)
    outs = pl.pallas_call(
        body, name="ssm_chunk_matrices", grid=(N_BLOCKS, CHUNK),
        in_specs=_ssm_param_specs() + _hbm_specs(n),
        out_specs=[row, row, row, pl.BlockSpec((1, 1, STATE_W), lambda b, j: (b, 0, 0))] + _hbm_specs(n),
        out_shape=[mat, mat, mat, jax.ShapeDtypeStruct((N_BLOCKS, 1, STATE_W), F32)] + _gather_out_shapes(shards),
        scratch_shapes=[pltpu.VMEM((2, 2, _POW_ROWS, BLOCK_ST), F32), pltpu.VMEM((2, 2, BLOCK_CH, BLOCK_ST), F32),
                        pltpu.VMEM((2 * CHUNK, BLOCK_CH, BLOCK_CH), F32), pltpu.VMEM((2, CHUNK_W, BLOCK_ST), F32)]
        + _gather_semaphores(n),
        compiler_params=_cparams(("arbitrary", "arbitrary"), has_side_effects=True),
    )(*blk, *shards)
    return outs[:4], outs[4:]


def _ssm_chunk_matrices_bwd(blk, d_m, d_ws, d_wot, d_a16):
    def body(ar_ref, ai_ref, ldt_ref, br_ref, bi_ref, cr_ref, ci_ref, d_ref, dm_ref, dws_ref, dwot_ref, da16_ref,
             dar_ref, dai_ref, dldt_ref, dbr_ref, dbi_ref, dcr_ref, dci_ref, dd_ref,
             pw_ref, bbar_ref, dlag_ref, dbbar_ref, dc_ref, dpw_ref, xs_ref, dts_ref):
        j = pl.program_id(1)
        w = BLOCK_ST

        @pl.when(j == 0)
        def _():
            _ssm_fill_powers(ar_ref, ai_ref, ldt_ref, br_ref, bi_ref, pw_ref, bbar_ref)
            for r in (dlag_ref, dbbar_ref, dc_ref, dpw_ref):
                r[...] = jnp.zeros_like(r)

        def fold(t):
            return jnp.sum(t.reshape(BLOCK_CH // 8, 8, w), axis=0)

        def d_power(d, ri, t):
            return jnp.sum(dpw_ref[d, ri, t], axis=0, keepdims=True)

        def x_chain(d, t, dxr, dxi):
            pr, pi = pw_ref[d, 0, pl.ds(t, 1), :], pw_ref[d, 1, pl.ds(t, 1), :]
            bbr, bbi = bbar_ref[d, 0], bbar_ref[d, 1]
            dbbar_ref[d, 0] += dxr * pr + dxi * pi
            dbbar_ref[d, 1] += dxi * pr - dxr * pi
            dpw_ref[d, 0, t] += fold(dxr * bbr + dxi * bbi)
            dpw_ref[d, 1, t] += fold(dxi * bbr - dxr * bbi)

        def z_chain(d, t, dzr, dzi):
            pr, pi = pw_ref[d, 0, pl.ds(t, 1), :], pw_ref[d, 1, pl.ds(t, 1), :]
            c_r, c_i = cr_ref[d, 0], ci_ref[d, 0]
            dc_ref[d, 0] += dzr * pr - dzi * pi
            dc_ref[d, 1] += -dzr * pi - dzi * pr
            dpw_ref[d, 0, t] += fold(dzr * c_r - dzi * c_i)
            dpw_ref[d, 1, t] += fold(-dzr * c_i - dzi * c_r)

        for jp in range(CHUNK):
            dlag_ref[jp - j + CHUNK - 1] += dm_ref[0, :, jp * BLOCK_CH:(jp + 1) * BLOCK_CH].astype(F32)
        quarter = lambda ref, i: ref[0, :, i * w:(i + 1) * w].astype(F32)
        x_chain(0, CHUNK - 1 - j, quarter(dws_ref, 0), quarter(dws_ref, 1))
        x_chain(1, j, quarter(dws_ref, 2), quarter(dws_ref, 3))
        z_chain(0, j + 1, quarter(dwot_ref, 0), quarter(dwot_ref, 1))
        z_chain(1, CHUNK - j, quarter(dwot_ref, 2), quarter(dwot_ref, 3))

        @pl.when(j == CHUNK - 1)
        def _():
            for d in range(2):
                _ssm_stack_inputs(d, pw_ref, bbar_ref, xs_ref)
                for t in range(CHUNK):
                    dts_ref[t * BLOCK_CH:(t + 1) * BLOCK_CH, :] = dlag_ref[CHUNK - 1 + t if d == 0 else CHUNK - 1 - t]
                d_taps = dts_ref[...]
                dc_ref[d, 0] += _dot_rounded(d_taps, xs_ref[0], ((0,), (0,)))
                dc_ref[d, 1] -= _dot_rounded(d_taps, xs_ref[1], ((0,), (0,)))
                xs_ref[0] = _dot_rounded(d_taps, cr_ref[d, 0])
                xs_ref[1] = -_dot_rounded(d_taps, ci_ref[d, 0])
                for t in range(CHUNK):
                    rows = slice(t * BLOCK_CH, (t + 1) * BLOCK_CH)
                    x_chain(d, t, xs_ref[0, rows, :], xs_ref[1, rows, :])
            dd_ref[0] = jnp.sum(dlag_ref[CHUNK - 1] * _eye(BLOCK_CH), axis=0, keepdims=True)
            for d in range(2):
                (abr, abi, cfr, cfi), disc_vjp = jax.vjp(_ssm_discretise, ar_ref[d, 0], ai_ref[d, 0], ldt_ref[d, 0])
                dpr = d_power(d, 0, CHUNK) + da16_ref[0, :, 2 * d * w:(2 * d + 1) * w]
                dpi = d_power(d, 1, CHUNK) + da16_ref[0, :, (2 * d + 1) * w:(2 * d + 2) * w]
                dabr, dabi = jnp.zeros_like(abr), jnp.zeros_like(abi)
                for t in range(CHUNK, 0, -1):
                    qr, qi = pw_ref[d, 0, t - 1:t, :], pw_ref[d, 1, t - 1:t, :]
                    dabr = dabr + dpr * qr + dpi * qi
                    dabi = dabi + dpi * qr - dpr * qi
                    dpr, dpi = (dpr * abr + dpi * abi + d_power(d, 0, t - 1),
                                dpi * abr - dpr * abi + d_power(d, 1, t - 1))
                dbbr, dbbi = dbbar_ref[d, 0], dbbar_ref[d, 1]
                b_r, b_i = br_ref[d, 0], bi_ref[d, 0]
                dbr_ref[d, 0] = cfr * dbbr + cfi * dbbi
                dbi_ref[d, 0] = cfr * dbbi - cfi * dbbr
                dcfr = jnp.sum(b_r * dbbr + b_i * dbbi, axis=0, keepdims=True)
                dcfi = jnp.sum(b_r * dbbi - b_i * dbbr, axis=0, keepdims=True)
                dar_ref[d, 0], dai_ref[d, 0], dldt_ref[d, 0] = disc_vjp((dabr, dabi, dcfr, dcfi))
                dcr_ref[d, 0] = dc_ref[d, 0]
                dci_ref[d, 0] = dc_ref[d, 1]

    row = pl.BlockSpec((1, BLOCK_CH, CHUNK_W), lambda b, j: (b, j, 0))
    specs = _ssm_param_specs()
    acc = lambda *s: pltpu.VMEM(s, F32)
    return pl.pallas_call(
        body, name="ssm_chunk_matrices_bwd", grid=(N_BLOCKS, CHUNK),
        in_specs=specs + [row, row, row, pl.BlockSpec((1, 1, STATE_W), lambda b, j: (b, 0, 0))],
        out_specs=specs,
        out_shape=[jax.ShapeDtypeStruct(t.shape, F32) for t in blk],
        scratch_shapes=[acc(2, 2, _POW_ROWS, BLOCK_ST), acc(2, 2, BLOCK_CH, BLOCK_ST), acc(2 * CHUNK, BLOCK_CH, BLOCK_CH),
                        acc(2, 2, BLOCK_CH, BLOCK_ST), acc(2, 2, BLOCK_CH, BLOCK_ST), acc(2, 2, CHUNK + 1, 8, BLOCK_ST),
                        acc(2, CHUNK_W, BLOCK_ST), acc(CHUNK_W, BLOCK_CH)],
        compiler_params=_cparams(("arbitrary", "arbitrary")),
    )(*blk, d_m, d_ws, d_wot, d_a16)


def _block_matmul(terms, name, out_dtype=F32, tn=1024):
    nc = terms[0][0].shape[1]
    n_out = terms[0][1].shape[1] if terms[0][2] else terms[0][1].shape[2]
    flags = [t[2] for t in terms]

    def body(*refs):
        out_ref = refs[-1]
        acc = None
        for t, transposed in enumerate(flags):
            a = refs[2 * t][0].astype(BF16)
            w = refs[2 * t + 1][0]
            part = _dot_nt(a, w) if transposed else _dot(a, w)
            acc = part if acc is None else acc + part
        out_ref[0] = acc.astype(out_dtype)

    in_specs, args = [], []
    for a, w, transposed in terms:
        k = a.shape[2]
        in_specs.append(pl.BlockSpec((1, nc, k), lambda b, n: (b, 0, 0)))
        if transposed:
            in_specs.append(pl.BlockSpec((1, tn, k), lambda b, n: (b, n, 0)))
        else:
            in_specs.append(pl.BlockSpec((1, k, tn), lambda b, n: (b, 0, n)))
        args += [a, w]
    return pl.pallas_call(
        body, name=name, grid=(N_BLOCKS, n_out // tn), in_specs=in_specs,
        out_specs=pl.BlockSpec((1, nc, tn), lambda b, n: (b, 0, n)),
        out_shape=jax.ShapeDtypeStruct((N_BLOCKS, nc, n_out), out_dtype),
        compiler_params=_cparams(("arbitrary", "arbitrary")),
    )(*args)


def _block_matmul_tn(a, b, name, tile=1024):
    nc, m = a.shape[1], a.shape[2]
    n = b.shape[2]

    def body(a_ref, b_ref, out_ref):
        out_ref[0] = _dot_tn(a_ref[0].astype(BF16), b_ref[0].astype(BF16)).astype(BF16)

    return pl.pallas_call(
        body, name=name, grid=(N_BLOCKS, m // tile, n // tile),
        in_specs=[pl.BlockSpec((1, nc, tile), lambda blk, i, j: (blk, 0, i)),
                  pl.BlockSpec((1, nc, tile), lambda blk, i, j: (blk, 0, j))],
        out_specs=pl.BlockSpec((1, tile, tile), lambda blk, i, j: (blk, i, j)),
        out_shape=jax.ShapeDtypeStruct((N_BLOCKS, m, n), BF16),
        compiler_params=_cparams(("arbitrary", "arbitrary", "arbitrary")),
    )(a, b)


def _cmul(ar, ai, xr, xi):
    return ar * xr - ai * xi, ar * xi + ai * xr


def _cmul_conj(ar, ai, xr, xi):
    return ar * xr + ai * xi, ar * xi - ai * xr


def _ssm_state_scan(s_in, a16):
    nc = s_in.shape[1]
    w = BLOCK_ST

    def body(sin_ref, a_ref, out_ref):
        a = a_ref[0]
        afr, afi, abr, abi = a[:, 0:w], a[:, w:2 * w], a[:, 2 * w:3 * w], a[:, 3 * w:4 * w]

        def step(c, carry):
            fr, fi, br, bi = carry
            cb = nc - 1 - c
            out_ref[0, pl.ds(c, 1), 0:w] = fr
            out_ref[0, pl.ds(c, 1), w:2 * w] = fi
            out_ref[0, pl.ds(cb, 1), 2 * w:3 * w] = br
            out_ref[0, pl.ds(cb, 1), 3 * w:4 * w] = bi
            nfr, nfi = _cmul(afr, afi, fr, fi)
            nbr, nbi = _cmul(abr, abi, br, bi)
            return (nfr + sin_ref[0, pl.ds(c, 1), 0:w], nfi + sin_ref[0, pl.ds(c, 1), w:2 * w],
                    nbr + sin_ref[0, pl.ds(cb, 1), 2 * w:3 * w], nbi + sin_ref[0, pl.ds(cb, 1), 3 * w:4 * w])

        z = jnp.zeros((1, w), F32)
        lax.fori_loop(0, nc, step, (z, z, z, z))

    spec = pl.BlockSpec((1, nc, STATE_W), lambda b: (b, 0, 0))
    return pl.pallas_call(
        body, name="ssm_state_scan", grid=(N_BLOCKS,),
        in_specs=[spec, pl.BlockSpec((1, 1, STATE_W), lambda b: (b, 0, 0))],
        out_specs=spec, out_shape=jax.ShapeDtypeStruct(s_in.shape, F32),
        compiler_params=_cparams(("arbitrary",)),
    )(s_in, a16)


def _ssm_state_scan_bwd(d_prev, s_prev, a16):
    nc = d_prev.shape[1]
    w = BLOCK_ST

    def body(dp_ref, sp_ref, a_ref, g_ref, da_ref):
        a = a_ref[0]
        afr, afi, abr, abi = a[:, 0:w], a[:, w:2 * w], a[:, 2 * w:3 * w], a[:, 3 * w:4 * w]

        def step(i, carry):
            gfr, gfi, gbr, gbi, dafr, dafi, dabr, dabi = carry
            cf = nc - 1 - i
            cb = i
            g_ref[0, pl.ds(cf, 1), 0:w] = gfr
            g_ref[0, pl.ds(cf, 1), w:2 * w] = gfi
            g_ref[0, pl.ds(cb, 1), 2 * w:3 * w] = gbr
            g_ref[0, pl.ds(cb, 1), 3 * w:4 * w] = gbi
            sfr, sfi = sp_ref[0, pl.ds(cf, 1), 0:w], sp_ref[0, pl.ds(cf, 1), w:2 * w]
            sbr, sbi = sp_ref[0, pl.ds(cb, 1), 2 * w:3 * w], sp_ref[0, pl.ds(cb, 1), 3 * w:4 * w]
            dafr = dafr + gfr * sfr + gfi * sfi
            dafi = dafi + gfi * sfr - gfr * sfi
            dabr = dabr + gbr * sbr + gbi * sbi
            dabi = dabi + gbi * sbr - gbr * sbi
            nfr, nfi = _cmul_conj(afr, afi, gfr, gfi)
            nbr, nbi = _cmul_conj(abr, abi, gbr, gbi)
            return (nfr + dp_ref[0, pl.ds(cf, 1), 0:w], nfi + dp_ref[0, pl.ds(cf, 1), w:2 * w],
                    nbr + dp_ref[0, pl.ds(cb, 1), 2 * w:3 * w], nbi + dp_ref[0, pl.ds(cb, 1), 3 * w:4 * w],
                    dafr, dafi, dabr, dabi)

        z = jnp.zeros((1, w), F32)
        res = lax.fori_loop(0, nc, step, (z,) * 8)
        da_ref[0] = jnp.concatenate(res[4:], axis=1)

    spec = pl.BlockSpec((1, nc, STATE_W), lambda b: (b, 0, 0))
    aspec = pl.BlockSpec((1, 1, STATE_W), lambda b: (b, 0, 0))
    return pl.pallas_call(
        body, name="ssm_state_scan_bwd", grid=(N_BLOCKS,),
        in_specs=[spec, spec, aspec], out_specs=[spec, aspec],
        out_shape=[jax.ShapeDtypeStruct(d_prev.shape, F32), jax.ShapeDtypeStruct((N_BLOCKS, 1, STATE_W), F32)],
        compiler_params=_cparams(("arbitrary",)),
    )(d_prev, s_prev, a16)


NA_PAIR = 2 * GRID_W
NA_WIN_ROWS = NA_ROWS + 2
NA_WIN = NA_WIN_ROWS * GRID_W
NA_PAIRS_PER_STEP = 8
NA_CASES = 5
NA_MASKED = -1e30


def _na_pair_window(m, rows):
    rs0 = jnp.clip(2 * m - NA_ROWS // 2, 0, rows - NA_ROWS)
    ws = jnp.minimum(rs0, rows - NA_WIN_ROWS)
    last = rows // 2 - 1
    case = jnp.where(m == 0, 0, jnp.where(m == 1, 1, jnp.where(m == last - 1, 3, jnp.where(m == last, 4, 2))))
    return ws, case


def _na_row_offsets(rows):
    last = rows // 2 - 1
    geom = []
    for m in (0, 1, 2, last - 1, last):
        ws = min(max(2 * m - NA_ROWS // 2, 0), rows - NA_ROWS, rows - NA_WIN_ROWS)
        per_case = []
        for i in range(NA_WIN_ROWS):
            pair = []
            for rr in range(2):
                r = 2 * m + rr
                rs = min(max(r - NA_ROWS // 2, 0), rows - NA_ROWS)
                pair.append(ws + i - r + NA_ROWS - 1 if rs <= ws + i < rs + NA_ROWS else None)
            per_case.append(pair)
        geom.append(per_case)
    return geom


def _na_col_select():
    qc = np.arange(NA_PAIR)[None, :] % GRID_W
    kc = np.arange(GRID_W)[:, None]
    dc = np.clip(kc - qc + NA_COLS - 1, 0, 2 * NA_COLS - 2)
    return jnp.asarray((np.arange(2 * NA_COLS - 1)[:, None, None] == dc[None]).astype(np.float32))


def _na_bias_rows(rpb):
    return jnp.einsum("hrd,dkl->hrkl", rpb, _na_col_select(), precision=HIGHEST)


def _na_col_window():
    qc = lax.broadcasted_iota(jnp.int32, (GRID_W, NA_PAIR), 1) % GRID_W
    kc = lax.broadcasted_iota(jnp.int32, (GRID_W, NA_PAIR), 0)
    cs = jnp.clip(qc - NA_COLS // 2, 0, GRID_W - NA_COLS)
    first_row = lax.broadcasted_iota(jnp.int32, (GRID_W, NA_PAIR), 1) < GRID_W
    return (kc >= cs) & (kc < cs + NA_COLS), first_row


def _na_bias_table(bias_rows, rows):
    geom = _na_row_offsets(rows)

    def body(br_ref, tab_ref):
        col_ok, first_row = _na_col_window()
        masked = jnp.full((GRID_W, NA_PAIR), NA_MASKED, F32)
        for case in range(NA_CASES):
            for i in range(NA_WIN_ROWS):
                d0, d1 = geom[case][i]
                t0 = masked if d0 is None else br_ref[0, d0]
                t1 = masked if d1 is None else br_ref[0, d1]
                tile = jnp.where(col_ok, jnp.where(first_row, t0, t1), NA_MASKED)
                tab_ref[0, case, i * GRID_W:(i + 1) * GRID_W, :] = tile

    return pl.pallas_call(
        body, name="na_bias_table", grid=(NA_HEADS,),
        in_specs=[pl.BlockSpec((1, 2 * NA_ROWS - 1, GRID_W, NA_PAIR), lambda h: (h, 0, 0, 0))],
        out_specs=pl.BlockSpec((1, NA_CASES, NA_WIN, NA_PAIR), lambda h: (h, 0, 0, 0)),
        out_shape=jax.ShapeDtypeStruct((NA_HEADS, NA_CASES, NA_WIN, NA_PAIR), F32),
        compiler_params=_cparams(("arbitrary",)),
    )(bias_rows)


def _na_bias_table_bwd(d_tab, rows):
    geom = _na_row_offsets(rows)

    def body(dt_ref, dbr_ref):
        col_ok, first_row = _na_col_window()
        acc = [None] * (2 * NA_ROWS - 1)
        for case in range(NA_CASES):
            for i in range(NA_WIN_ROWS):
                tile = jnp.where(col_ok, dt_ref[0, case, i * GRID_W:(i + 1) * GRID_W, :], 0.0)
                for rr, d in enumerate(geom[case][i]):
                    if d is not None:
                        part = jnp.where(first_row if rr == 0 else ~first_row, tile, 0.0)
                        acc[d] = part if acc[d] is None else acc[d] + part
        for d, a in enumerate(acc):
            dbr_ref[0, d] = jnp.zeros((GRID_W, NA_PAIR), F32) if a is None else a

    return pl.pallas_call(
        body, name="na_bias_table_bwd", grid=(NA_HEADS,),
        in_specs=[pl.BlockSpec((1, NA_CASES, NA_WIN, NA_PAIR), lambda h: (h, 0, 0, 0))],
        out_specs=pl.BlockSpec((1, 2 * NA_ROWS - 1, GRID_W, NA_PAIR), lambda h: (h, 0, 0, 0)),
        out_shape=jax.ShapeDtypeStruct((NA_HEADS, 2 * NA_ROWS - 1, GRID_W, NA_PAIR), F32),
        compiler_params=_cparams(("arbitrary",)),
    )(d_tab)


NA_BLK = 64


def _na_blocks():
    return [slice(i * NA_BLK, (i + 1) * NA_BLK) for i in range(NA_WIN // NA_BLK)]


def _na_softmax(qk, bias_ref, hh, case):
    m = jnp.full((NA_BLK, NA_PAIR), -jnp.inf, F32)
    scores = []
    for blk in _na_blocks():
        s = qk[blk, :] + bias_ref[hh, case, blk, :]
        scores.append(s)
        m = jnp.maximum(m, s)
    m = jnp.max(m, axis=0, keepdims=True)
    l = jnp.zeros((NA_BLK, NA_PAIR), F32)
    exps = []
    for s in scores:
        e = jnp.exp(s - m)
        exps.append(e)
        l = l + e
    return exps, jnp.sum(l, axis=0, keepdims=True)


def _na_units(step, rows):
    units = []
    for pp in range(NA_PAIRS_PER_STEP):
        ws, case = _na_pair_window(step * NA_PAIRS_PER_STEP + pp, rows)
        win = pl.ds(pl.multiple_of(ws * GRID_W, NA_PAIR), NA_WIN)
        lanes = slice(pp * NA_PAIR, (pp + 1) * NA_PAIR)
        for hh in range(2):
            units.append((pp, hh, case, win, lanes, slice(hh * NA_HEAD_DIM, (hh + 1) * NA_HEAD_DIM)))
    return units


def _na_pipeline(n, before, middle, after, lookahead=2):
    for u in range(min(lookahead, n)):
        for f in before:
            f(u)
    for u in range(n):
        middle(u)
        if u + lookahead < n:
            for f in before:
                f(u + lookahead)
        for f in after:
            f(u)


def _head_rows(t, hh):
    row_head = lax.broadcasted_iota(jnp.int32, t.shape, 0) // NA_HEAD_DIM
    return jnp.where(row_head == hh, t, jnp.zeros_like(t))


def _heads_block_diag(t):
    lane_head = lax.broadcasted_iota(jnp.int32, t.shape, 1) // NA_HEAD_DIM
    zero = jnp.zeros_like(t)
    return jnp.concatenate([jnp.where(lane_head == 0, t, zero), jnp.where(lane_head == 1, t, zero)], axis=0)


def _na_fwd(q_t, k, v_t, bias_tab):
    L = k.shape[0]
    rows = L // GRID_W
    step_w = NA_PAIRS_PER_STEP * NA_PAIR

    def body(q_ref, k_ref, v_ref, bt_ref, o_ref):
        units = _na_units(pl.program_id(1), rows)
        qk, probs = {}, {}

        def scores(u):
            _, hh, _, win, lanes, _ = units[u]
            qk[u] = _dot(k_ref[win, :], _head_rows(q_ref[:, lanes], hh))

        def softmax(u):
            _, hh, case, _, _, _ = units[u]
            exps, l = _na_softmax(qk.pop(u), bt_ref, hh, case)
            probs[u] = jnp.concatenate([t.astype(BF16) for t in exps], axis=0), l

        def output(u):
            _, _, _, win, lanes, hrows = units[u]
            e, l = probs.pop(u)
            o_ref[hrows, lanes] = _dot(v_ref[hrows, win], e) / l

        _na_pipeline(len(units), [scores], softmax, [output])

    q_spec = pl.BlockSpec((NA_PAIR, step_w), lambda h, s: (h, s))
    return pl.pallas_call(
        body, name="na_fwd", grid=(NA_HEADS // 2, L // step_w),
        in_specs=[q_spec, pl.BlockSpec((L, NA_PAIR), lambda h, s: (0, h)),
                  pl.BlockSpec((NA_PAIR, L), lambda h, s: (h, 0)),
                  pl.BlockSpec((2, NA_CASES, NA_WIN, NA_PAIR), lambda h, s: (h, 0, 0, 0))],
        out_specs=q_spec,
        out_shape=jax.ShapeDtypeStruct((D_NA, L), F32),
        compiler_params=_cparams(("arbitrary", "arbitrary")),
    )(q_t, k, v_t, bias_tab)


def _na_bwd(q_t, q, k_t, k, v, bias_tab, out_t, d_out_t, d_out):
    L = k.shape[0]
    rows = L // GRID_W
    step_w = NA_PAIRS_PER_STEP * NA_PAIR

    def body(qt_ref, q_ref, kt_ref, k_ref, v_ref, bt_ref, ot_ref, dot_ref, do_ref, dq_ref, dk_ref, dv_ref, dbt_ref):
        @pl.when(pl.program_id(1) == 0)
        def _():
            dk_ref[...] = jnp.zeros_like(dk_ref)
            dv_ref[...] = jnp.zeros_like(dv_ref)
            dbt_ref[...] = jnp.zeros_like(dbt_ref)

        units = _na_units(pl.program_id(1), rows)
        qk, dp, dsb, pb = {}, {}, {}, {}

        def scores(u):
            _, hh, _, win, lanes, _ = units[u]
            qk[u] = _dot(k_ref[win, :], _head_rows(qt_ref[:, lanes], hh))

        def d_probs(u):
            _, hh, _, win, lanes, _ = units[u]
            dp[u] = _dot(v_ref[win, :], _head_rows(dot_ref[:, lanes].astype(BF16), hh))

        def softmax_bwd(u):
            _, hh, case, _, lanes, hrows = units[u]
            exps, l = _na_softmax(qk.pop(u), bt_ref, hh, case)
            inv_l = 1.0 / l
            delta = jnp.sum(dot_ref[hrows, lanes] * ot_ref[hrows, lanes], axis=0, keepdims=True)
            d_p = dp.pop(u)
            ds_blocks, p_blocks = [], []
            for blk, e in zip(_na_blocks(), exps):
                p = e * inv_l
                ds = p * (d_p[blk, :] - delta)
                dbt_ref[hh, case, blk, :] += ds
                ds_blocks.append(ds.astype(BF16))
                p_blocks.append(p.astype(BF16))
            dsb[u] = jnp.concatenate(ds_blocks, axis=0)
            pb[u] = jnp.concatenate(p_blocks, axis=0)

        def d_query(u):
            _, _, _, win, lanes, hrows = units[u]
            dq_ref[hrows, lanes] = _dot(kt_ref[hrows, win], dsb[u]) * (NA_HEAD_DIM ** -0.5)

        def d_keys_values(u):
            pp, hh, _, win, _, _ = units[u]
            if hh == 1:
                tokens = slice(pp * NA_PAIR, (pp + 1) * NA_PAIR)
                dk_ref[win, :] += _dot(jnp.concatenate([dsb.pop(u - 1), dsb.pop(u)], axis=1), _heads_block_diag(q_ref[tokens, :]))
                dv_ref[win, :] += _dot(jnp.concatenate([pb.pop(u - 1), pb.pop(u)], axis=1), _heads_block_diag(do_ref[tokens, :]))

        _na_pipeline(len(units), [scores, d_probs], softmax_bwd, [d_query, d_keys_values])

    t_tile = pl.BlockSpec((NA_PAIR, step_w), lambda h, s: (h, s))
    tile = pl.BlockSpec((step_w, NA_PAIR), lambda h, s: (s, h))
    t_full = pl.BlockSpec((NA_PAIR, L), lambda h, s: (h, 0))
    full = pl.BlockSpec((L, NA_PAIR), lambda h, s: (0, h))
    bt = pl.BlockSpec((2, NA_CASES, NA_WIN, NA_PAIR), lambda h, s: (h, 0, 0, 0))
    tok = jax.ShapeDtypeStruct((L, D_NA), F32)
    return pl.pallas_call(
        body, name="na_bwd", grid=(NA_HEADS // 2, L // step_w),
        in_specs=[t_tile, tile, t_full, full, full, bt, t_tile, t_tile, tile],
        out_specs=[t_tile, full, full, bt],
        out_shape=[jax.ShapeDtypeStruct((D_NA, L), F32), tok, tok, jax.ShapeDtypeStruct(bias_tab.shape, F32)],
        compiler_params=_cparams(("arbitrary", "arbitrary")),
    )(q_t, q, k_t, k, v, bias_tab, out_t, d_out_t, d_out)


def _branch_fwd_values(ys, zs, yn, zn, wglu, bglu):
    g1, t = _gelu_parts(ys)
    lin = _dot(g1.astype(BF16), wglu) + bglu
    sg = _sigmoid(lin)
    ys2 = g1 * sg
    sz, szs = _silu_parts(zs)
    sn, sns = _silu_parts(zn)
    return g1, t, sg, ys2, sz, szs, sn, sns


def _branch_fwd(y_ssm_c, z_s, y_na_t, z_n, w_glu, b_glu, tm=512):
    L = z_s.shape[0]

    def body(ys_ref, zs_ref, yn_ref, zn_ref, w_ref, b_ref, cat_ref, scr):
        yn = yn_ref[...].T
        g1, t, sg, ys2, sz, szs, sn, sns = _branch_fwd_values(
            _load_chunks(ys_ref, scr), zs_ref[...], yn, zn_ref[...], w_ref[...], b_ref[...])
        cat_ref[:, 0:512] = (ys2 * sz).astype(BF16)
        cat_ref[:, 512:1024] = (yn * sn).astype(BF16)

    tile = pl.BlockSpec((tm, 512), lambda i: (i, 0))
    return pl.pallas_call(
        body, name="branch_fwd", grid=(L // tm,),
        in_specs=[_chunk_spec(tm), tile, _heads_t_spec(tm), tile, pl.BlockSpec((512, 512), lambda i: (0, 0)),
                  pl.BlockSpec((1, 512), lambda i: (0, 0))],
        out_specs=pl.BlockSpec((tm, 1024), lambda i: (i, 0)),
        out_shape=jax.ShapeDtypeStruct((L, 1024), BF16),
        scratch_shapes=[_chunk_scratch(tm)],
        compiler_params=_cparams(("arbitrary",)),
    )(y_ssm_c, z_s, y_na_t, z_n, w_glu, b_glu)


def _branch_bwd(y_ssm_c, z_s, y_na_t, z_n, w_glu, b_glu, d_cat, tm=512):
    L = z_s.shape[0]

    def body(ys_ref, zs_ref, yn_ref, zn_ref, w_ref, b_ref, dc_ref,
             dys_ref, dzs_ref, dynt_ref, dyn_ref, dzn_ref, dw_ref, db_ref, scr):
        @pl.when(pl.program_id(0) == 0)
        def _():
            dw_ref[...] = jnp.zeros_like(dw_ref)
            db_ref[...] = jnp.zeros_like(db_ref)

        ys, zs, yn, zn = _load_chunks(ys_ref, scr), zs_ref[...], yn_ref[...].T, zn_ref[...]
        w = w_ref[...]
        g1, t, sg, ys2, sz, szs, sn, sns = _branch_fwd_values(ys, zs, yn, zn, w, b_ref[...])
        dys3 = dc_ref[:, 0:512]
        dyn2 = dc_ref[:, 512:1024]
        dzs_ref[...] = (dys3 * ys2 * _silu_grad(zs, szs)).astype(BF16)
        dys2 = dys3 * sz
        dlin = dys2 * g1 * sg * (1.0 - sg)
        dlb = dlin.astype(BF16)
        db_ref[...] += jnp.sum(dlin, axis=0, keepdims=True)
        dw_ref[...] += _dot_tn(g1.astype(BF16), dlb)
        dg1 = dys2 * sg + _dot_nt(dlb, w)
        _store_chunks(dg1 * _gelu_grad(ys, t), scr, dys_ref, BF16)
        dyn = dyn2 * sn
        dynt_ref[...] = dyn.T
        dyn_ref[...] = dyn.astype(BF16)
        dzn_ref[...] = (dyn2 * yn * _silu_grad(zn, sns)).astype(BF16)

    tile = pl.BlockSpec((tm, 512), lambda i: (i, 0))
    wspec = pl.BlockSpec((512, 512), lambda i: (0, 0))
    bspec = pl.BlockSpec((1, 512), lambda i: (0, 0))
    tok = jax.ShapeDtypeStruct((L, 512), BF16)
    return pl.pallas_call(
        body, name="branch_bwd", grid=(L // tm,),
        in_specs=[_chunk_spec(tm), tile, _heads_t_spec(tm), tile, wspec, bspec, pl.BlockSpec((tm, 1024), lambda i: (i, 0))],
        out_specs=[_chunk_spec(tm), tile, _heads_t_spec(tm), tile, tile, wspec, bspec],
        out_shape=[jax.ShapeDtypeStruct((N_BLOCKS, L // CHUNK, CHUNK_W), BF16), tok, jax.ShapeDtypeStruct((D_NA, L), F32),
                   tok, tok,
                   jax.ShapeDtypeStruct((512, 512), F32), jax.ShapeDtypeStruct((1, 512), F32)],
        scratch_shapes=[_chunk_scratch(tm)],
        compiler_params=_cparams(("arbitrary",)),
    )(y_ssm_c, z_s, y_na_t, z_n, w_glu, b_glu, d_cat)


def _head(x, p, target, cat, w_out, g_post, w_ple_g, g_ple, w_pg, tm=512):
    L = x.shape[0]
    pw = w_ple_g.shape[2]

    def body(x_ref, p_ref, t_ref, cat_ref, wo_ref, gpo_ref, wp_ref, gpl_ref, wg_ref,
             loss_ref, dh1_ref, dcat_ref, dwo_ref, dgpo_ref, dwp_ref, dgpl_ref, dwg_ref):
        @pl.when(pl.program_id(0) == 0)
        def _():
            for r in (loss_ref, dwo_ref, dgpo_ref, dwp_ref, dgpl_ref, dwg_ref):
                r[...] = jnp.zeros_like(r)

        cat_b = cat_ref[...]
        wo, wg = wo_ref[...], wg_ref[...]
        g_po, g_pl = gpo_ref[...], gpl_ref[...]
        mix = _dot(cat_b, wo)
        nm, r2 = _rms(mix)
        h1 = x_ref[...] + nm * g_po
        p_b = p_ref[...].astype(BF16)
        ep = jnp.concatenate([_dot(p_b, wp_ref[j]) for j in range(N_CHIPS)], axis=1)
        ne, r3 = _rms(ep)
        e = ne * g_pl
        h1_b = h1.astype(BF16)
        gate = _sigmoid(_dot(h1_b, wg))
        h2 = h1 + gate * e
        diff = h2 - t_ref[...]
        loss_ref[...] += (0.5 / D_MODEL) * jnp.sum(diff * diff).reshape(1, 1)

        dh2 = diff * (1.0 / D_MODEL)
        de = dh2 * gate
        dgl = (dh2 * e * gate * (1.0 - gate)).astype(BF16)
        dwg_ref[...] += _dot_tn(h1_b, dgl)
        dh1 = dh2 + _dot_nt(dgl, wg)
        dgpl_ref[...] += jnp.sum(de * ne, axis=0, keepdims=True)
        dep = _rms_bwd(de * g_pl, ne, r3).astype(BF16)
        for j in range(N_CHIPS):
            dwp_ref[j] += _dot_tn(p_b, dep[:, j * pw:(j + 1) * pw])
        dgpo_ref[...] += jnp.sum(dh1 * nm, axis=0, keepdims=True)
        dmix = _rms_bwd(dh1 * g_po, nm, r2).astype(BF16)
        dwo_ref[...] += _dot_tn(cat_b, dmix)
        dcat_ref[...] = _dot_nt(dmix, wo)
        dh1_ref[...] = dh1

    tile = lambda w: pl.BlockSpec((tm, w), lambda i: (i, 0))
    const = _resident
    sds = jax.ShapeDtypeStruct
    return pl.pallas_call(
        body, name="head", grid=(L // tm,),
        in_specs=[tile(D_MODEL), tile(D_PLE), tile(D_MODEL), tile(1024), const(1024, D_MODEL), const(1, D_MODEL),
                  const(N_CHIPS, D_PLE, pw), const(1, D_MODEL), const(D_MODEL, D_MODEL)],
        out_specs=[const(1, 1), tile(D_MODEL), tile(1024), const(1024, D_MODEL), const(1, D_MODEL),
                   const(N_CHIPS, D_PLE, pw), const(1, D_MODEL), const(D_MODEL, D_MODEL)],
        out_shape=[sds((1, 1), F32), sds((L, D_MODEL), F32), sds((L, 1024), F32), sds((1024, D_MODEL), F32),
                   sds((1, D_MODEL), F32), sds((N_CHIPS, D_PLE, pw), F32), sds((1, D_MODEL), F32),
                   sds((D_MODEL, D_MODEL), F32)],
        compiler_params=_cparams(("arbitrary",)),
    )(x, p, target, cat, w_out, g_post, w_ple_g, g_ple, w_pg)


def _dproj_specs(tm):
    tile = pl.BlockSpec((tm, 512), lambda i: (i, 0))
    return [_chunk_spec(tm), tile, _heads_t_spec(tm), tile, tile, tile]


def _dproj_tile(refs, scr):
    du_ref, dzs_ref, dqt_ref, dk_ref, dv_ref, dzn_ref = refs
    parts = [_load_chunks(du_ref, scr), dzs_ref[...], dqt_ref[...].T, dk_ref[...], dv_ref[...], dzn_ref[...]]
    return jnp.concatenate([t.astype(BF16) for t in parts], axis=1)


def _in_proj_bwd_w(x, g_col, w_in_g, dparts, tm=512):
    L = x.shape[0]
    wn = D_IN_PROJ // N_CHIPS
    steps = L // tm

    def body(x_ref, g_ref, w_ref, *refs):
        dw_ref, dg_ref, scr = refs[-3], refs[-2], refs[-1]

        @pl.when(pl.program_id(0) == 0)
        def _():
            dw_ref[...] = jnp.zeros_like(dw_ref)

        n, _ = _rms(x_ref[...])
        nb = n.astype(BF16)
        dproj = _dproj_tile(refs[:-3], scr)
        for j in range(N_CHIPS):
            dw_ref[j] += _dot_tn(nb, dproj[:, j * wn:(j + 1) * wn])

        @pl.when(pl.program_id(0) == steps - 1)
        def _():
            g = g_ref[...]
            dg = jnp.zeros_like(g)
            for j in range(N_CHIPS):
                a = dw_ref[j]
                dg = dg + jnp.sum(a * w_ref[j].astype(F32), axis=1, keepdims=True)
                dw_ref[j] = a * g
            dg_ref[...] = dg

    return pl.pallas_call(
        body, name="in_proj_bwd_w", grid=(steps,),
        in_specs=[pl.BlockSpec((tm, D_MODEL), lambda i: (i, 0)), _resident(D_MODEL, 1), _resident(N_CHIPS, D_MODEL, wn)]
        + _dproj_specs(tm),
        out_specs=[_resident(N_CHIPS, D_MODEL, wn), _resident(D_MODEL, 1)],
        out_shape=[jax.ShapeDtypeStruct((N_CHIPS, D_MODEL, wn), F32), jax.ShapeDtypeStruct((D_MODEL, 1), F32)],
        scratch_shapes=[_chunk_scratch(tm)],
        compiler_params=_cparams(("arbitrary",)),
    )(x, g_col, w_in_g, *dparts)


def _in_proj_bwd_x(x, g_pre, w_in_g, d_h1, dparts, pair_sums, tm=512):
    L = x.shape[0]
    wn = w_in_g.shape[2]
    n_ps = len(pair_sums)
    steps = L // tm

    def body(*refs):
        x_ref, g_ref, w_ref, dh1_ref = refs[:4]
        dparts_refs = refs[4:10]
        dx_ref = refs[10 + n_ps]
        scr = refs[11 + 2 * n_ps]
        scatter = _ChipScatter(refs[10:10 + n_ps], refs[11 + n_ps:11 + 2 * n_ps], refs[12 + 2 * n_ps:16 + 2 * n_ps],
                               refs[16 + 2 * n_ps:])
        pl.when(pl.program_id(0) == 0)(scatter.start)
        pl.when(pl.program_id(0) == steps - 1)(scatter.finish)

        n, r = _rms(x_ref[...])
        dproj = _dproj_tile(dparts_refs, scr)
        dhn = _dot_nt(dproj[:, 0:wn], w_ref[0])
        for j in range(1, N_CHIPS):
            dhn = dhn + _dot_nt(dproj[:, j * wn:(j + 1) * wn], w_ref[j])
        dx_ref[...] = dh1_ref[...] + _rms_bwd(dhn * g_ref[...], n, r)

    wide = pl.BlockSpec((tm, D_MODEL), lambda i: (i, 0))
    outs = pl.pallas_call(
        body, name="in_proj_bwd_x", grid=(steps,),
        in_specs=[wide, _resident(1, D_MODEL), _resident(N_CHIPS, D_MODEL, wn), wide] + _dproj_specs(tm) + _hbm_specs(n_ps),
        out_specs=[wide] + _hbm_specs(n_ps),
        out_shape=[jax.ShapeDtypeStruct((L, D_MODEL), F32)] + [jax.ShapeDtypeStruct(p.shape, p.dtype) for p in pair_sums],
        scratch_shapes=[_chunk_scratch(tm)] + _scatter_scratch(pair_sums),
        compiler_params=_cparams(("arbitrary",), has_side_effects=True),
    )(x, g_pre, w_in_g, d_h1, *dparts, *pair_sums)
    return outs[0], outs[1:]


def _mesh_position():
    x, y, c = lax.axis_index("x"), lax.axis_index("y"), lax.axis_index("c")
    chips = [(1 - x, y), (x, 1 - y), (1 - x, 1 - y)]
    return x, y, c, chips


def _chip_index(cx, cy):
    return 2 * cx + cy


def _hbm_specs(n):
    return [pl.BlockSpec(memory_space=pl.ANY)] * n


def _gather_chips(shards, name):
    n = len(shards)

    def body(*refs):
        gather = _ChipGather(refs[:n], refs[n:2 * n], refs[2 * n:])
        gather.start()
        gather.forward()
        gather.finish()

    return pl.pallas_call(
        body, name=name, in_specs=_hbm_specs(n), out_specs=_hbm_specs(n),
        out_shape=_gather_out_shapes(shards), scratch_shapes=_gather_semaphores(n),
        compiler_params=pltpu.CompilerParams(has_side_effects=True),
    )(*shards)


def _gather_out_shapes(shards):
    return [jax.ShapeDtypeStruct((N_CHIPS,) + s.shape, s.dtype) for s in shards]


def _gather_semaphores(n):
    sem = pltpu.SemaphoreType.DMA
    return [sem((n, 3)), sem((n, 3)), sem((n, 3)), sem((n, 3)), sem((n,)), sem((n,))]


class _ChipGather:
    def __init__(self, ins, outs, sems):
        self.ins, self.outs = ins, outs
        self.send1, self.recv1, self.send2, self.recv2, self.send3, self.recv3 = sems
        self.x, self.y, self.c, self.chips = _mesh_position()
        self.me = _chip_index(self.x, self.y)
        self.sibling = (self.x, self.y, 1 - self.c)

    def _half(self, a, chip, core):
        hr = self.outs[a].shape[1] // 2
        return self.outs[a].at[chip, pl.ds(core * hr, hr)]

    def _own(self, a):
        return pltpu.make_async_remote_copy(
            src_ref=self.ins[a], dst_ref=self.outs[a].at[self.me], send_sem=self.send3.at[a], recv_sem=self.recv3.at[a],
            device_id=self.sibling, device_id_type=MESH)

    def _to_chip(self, a, j):
        hr = self.ins[a].shape[0] // 2
        return pltpu.make_async_remote_copy(
            src_ref=self.ins[a].at[pl.ds(self.c * hr, hr)], dst_ref=self._half(a, self.me, self.c),
            send_sem=self.send1.at[a, j], recv_sem=self.recv1.at[a, j], device_id=(*self.chips[j], self.c), device_id_type=MESH)

    def _from_chip(self, a, j):
        landed = self._half(a, _chip_index(*self.chips[j]), self.c)
        return pltpu.make_async_remote_copy(
            src_ref=landed, dst_ref=landed, send_sem=self.send1.at[a, j], recv_sem=self.recv1.at[a, j],
            device_id=(*self.chips[j], self.c), device_id_type=MESH)

    def _to_sibling(self, a, j, core):
        part = self._half(a, _chip_index(*self.chips[j]), core)
        return pltpu.make_async_remote_copy(
            src_ref=part, dst_ref=part, send_sem=self.send2.at[a, j], recv_sem=self.recv2.at[a, j],
            device_id=self.sibling, device_id_type=MESH)

    def _each(self):
        return [(a, j) for a in range(len(self.ins)) for j in range(3)]

    def start(self):
        for a in range(len(self.ins)):
            self._own(a).start()
        for a, j in self._each():
            self._to_chip(a, j).start()

    def forward(self):
        for a, j in self._each():
            self._from_chip(a, j).wait_recv()
            self._to_sibling(a, j, self.c).start()

    def finish(self):
        for a, j in self._each():
            self._to_sibling(a, j, 1 - self.c).wait_recv()
        for a, j in self._each():
            self._to_chip(a, j).wait_send()
            self._to_sibling(a, j, self.c).wait_send()
        for a in range(len(self.ins)):
            self._own(a).wait()


def _pair_exchange(grads):
    n = len(grads)

    def body(*refs):
        ins, outs = refs[:n], refs[n:2 * n]
        send, recv = refs[2 * n:]
        x, y, c, _ = _mesh_position()
        copies = []
        for a in range(n):
            hr = ins[a].shape[1] // 2
            cp = pltpu.make_async_remote_copy(
                src_ref=ins[a].at[:, pl.ds((1 - c) * hr, hr)], dst_ref=outs[a],
                send_sem=send.at[a], recv_sem=recv.at[a], device_id=(x, y, 1 - c), device_id_type=MESH)
            cp.start()
            copies.append(cp)
        for cp in copies:
            cp.wait()

    sem = pltpu.SemaphoreType.DMA
    return pl.pallas_call(
        body, name="pair_exchange", in_specs=_hbm_specs(n), out_specs=_hbm_specs(n),
        out_shape=[jax.ShapeDtypeStruct((g.shape[0], g.shape[1] // 2, g.shape[2]), g.dtype) for g in grads],
        scratch_shapes=[sem((n,)), sem((n,))],
        compiler_params=pltpu.CompilerParams(has_side_effects=True),
    )(*grads)


def _pair_add(core, grad, other, tr, out_dtype):
    hr = other.shape[1]
    cdim = other.shape[2]
    nb = hr // tr

    def body(core_ref, g_ref, o_ref, out_ref):
        out_ref[...] = (g_ref[...] + o_ref[...]).astype(out_dtype)

    return pl.pallas_call(
        body, name="pair_add",
        grid_spec=pltpu.PrefetchScalarGridSpec(
            num_scalar_prefetch=1, grid=(N_CHIPS, nb),
            in_specs=[pl.BlockSpec((1, tr, cdim), lambda j, i, core_ref: (j, core_ref[0] * nb + i, 0)),
                      pl.BlockSpec((1, tr, cdim), lambda j, i, core_ref: (j, i, 0))],
            out_specs=pl.BlockSpec((1, tr, cdim), lambda j, i, core_ref: (j, i, 0))),
        out_shape=jax.ShapeDtypeStruct(other.shape, out_dtype),
        compiler_params=_cparams(("arbitrary", "arbitrary")),
    )(core, grad, other)


def _scatter_scratch(parts):
    sem = pltpu.SemaphoreType.DMA
    n = len(parts)
    return [sem((n, 3)), sem((n, 3)), sem((n,)), sem((n,))] + [pltpu.VMEM(p.shape[1:], p.dtype) for p in parts]


class _ChipScatter:
    def __init__(self, ins, outs, sems, staged):
        self.ins, self.outs, self.staged = ins, outs, staged
        self.send, self.recv, self.load_sem, self.store_sem = sems
        self.x, self.y, self.c, self.chips = _mesh_position()
        self.me = _chip_index(self.x, self.y)

    def _load(self, a):
        return pltpu.make_async_copy(self.ins[a].at[self.me], self.staged[a], self.load_sem.at[a])

    def _store(self, a):
        return pltpu.make_async_copy(self.staged[a], self.outs[a].at[self.me], self.store_sem.at[a])

    def _to_chip(self, a, j):
        return pltpu.make_async_remote_copy(
            src_ref=self.ins[a].at[_chip_index(*self.chips[j])], dst_ref=self.outs[a].at[self.me],
            send_sem=self.send.at[a, j], recv_sem=self.recv.at[a, j], device_id=(*self.chips[j], self.c), device_id_type=MESH)

    def start(self):
        for a in range(len(self.ins)):
            self._load(a).start()
            for j in range(3):
                self._to_chip(a, j).start()

    def finish(self):
        for a in range(len(self.ins)):
            self._load(a).wait()
            self._store(a).start()
        for a in range(len(self.ins)):
            for j in range(3):
                self._to_chip(a, j).wait()
            self._store(a).wait()


def _chip_add(core, recv, tr):
    hr, cdim = recv.shape[1], recv.shape[2]
    nb = hr // tr

    def body(core_ref, r_ref, out_ref):
        out_ref[...] = ((r_ref[0].astype(F32) + r_ref[1].astype(F32)) + r_ref[2].astype(F32)) + r_ref[3].astype(F32)

    return pl.pallas_call(
        body, name="chip_add",
        grid_spec=pltpu.PrefetchScalarGridSpec(
            num_scalar_prefetch=1, grid=(nb,),
            in_specs=[pl.BlockSpec((N_CHIPS, tr, cdim), lambda i, core_ref: (0, i, 0))],
            out_specs=pl.BlockSpec((tr, cdim), lambda i, core_ref: (core_ref[0] * nb + i, 0))),
        out_shape=jax.ShapeDtypeStruct((2 * hr, cdim), F32),
        compiler_params=_cparams(("arbitrary",)),
    )(core, recv)


def _pair_gather(fulls):
    n = len(fulls)

    def body(*refs):
        outs = refs[n:2 * n]
        send, recv = refs[2 * n:]
        x, y, c, _ = _mesh_position()
        copies = []
        for a in range(n):
            hr = outs[a].shape[0] // 2
            mine = outs[a].at[pl.ds(c * hr, hr)]
            cp = pltpu.make_async_remote_copy(
                src_ref=mine, dst_ref=mine, send_sem=send.at[a], recv_sem=recv.at[a],
                device_id=(x, y, 1 - c), device_id_type=MESH)
            cp.start()
            copies.append(cp)
        for cp in copies:
            cp.wait()

    sem = pltpu.SemaphoreType.DMA
    return pl.pallas_call(
        body, name="pair_gather", in_specs=_hbm_specs(n), out_specs=_hbm_specs(n),
        out_shape=[jax.ShapeDtypeStruct(f.shape, f.dtype) for f in fulls],
        input_output_aliases={a: a for a in range(n)},
        scratch_shapes=[sem((n,)), sem((n,))],
        compiler_params=pltpu.CompilerParams(has_side_effects=True),
    )(*fulls)


def _row_tile(rows):
    for t in (512, 256, 128, 64, 32, 16, 8):
        if rows % t == 0:
            return t
    raise ValueError(rows)


def _pair_sums(core, grads, ici_dtypes):
    others = _pair_exchange(grads)
    return [_pair_add(core, g, o, _row_tile(o.shape[1]), dt) for g, o, dt in zip(grads, others, ici_dtypes)]


def _finish_reduce(core, landed):
    return _pair_gather([_chip_add(core, r, _row_tile(r.shape[1])) for r in landed])


def _adamw(w, g, m, v):
    rows, cols = w.shape
    one_block = rows % 8 != 0 or rows * max(cols, 128) * 4 <= (1 << 20)
    tr = rows if one_block else _row_tile(rows)

    def body(w_ref, g_ref, m_ref, v_ref, d_ref, nm_ref, nv_ref):
        g_ = g_ref[...]
        m_ = ADAM_B1 * m_ref[...] + (1.0 - ADAM_B1) * g_
        v_ = ADAM_B2 * v_ref[...] + (1.0 - ADAM_B2) * (g_ * g_)
        m_hat = m_ / (1.0 - ADAM_B1 ** ADAM_STEP)
        v_hat = v_ / (1.0 - ADAM_B2 ** ADAM_STEP)
        d_ref[...] = -ADAM_LR * (m_hat / (jnp.sqrt(v_hat) + ADAM_EPS) + ADAM_WD * w_ref[...])
        nm_ref[...] = m_
        nv_ref[...] = v_

    spec = pl.BlockSpec((tr, cols), lambda i: (i, 0))
    shp = jax.ShapeDtypeStruct((rows, cols), F32)
    return pl.pallas_call(
        body, name="adamw", grid=(rows // tr,), in_specs=[spec] * 4, out_specs=[spec] * 3,
        out_shape=[shp] * 3, compiler_params=_cparams(("arbitrary",)),
    )(w, g, m, v)


_SMALL = ["norm_pre", "norm_post", "ssm_a_re", "ssm_a_im", "ssm_log_dt", "ssm_b_re", "ssm_b_im",
          "ssm_c_re", "ssm_c_im", "ssm_d", "b_glu", "na_rpb", "ple_norm"]
_BIG = ["w_in", "w_glu", "w_out", "w_ple", "w_ple_gate"]
_WEIGHTS = ["norm_pre", "norm_post", "w_in", "ssm_a_re", "ssm_a_im", "ssm_log_dt", "ssm_b_re", "ssm_b_im",
            "ssm_c_re", "ssm_c_im", "ssm_d", "w_glu", "b_glu", "na_rpb", "w_out", "w_ple", "ple_norm", "w_ple_gate"]
_SMALL_ROWS = 2176


def _pack_small(tensors, tail=None):
    parts = [tensors[n].reshape(-1) for n in _SMALL] + ([] if tail is None else [tail.reshape(-1)])
    flat = jnp.concatenate(parts)
    flat = jnp.pad(flat, (0, _SMALL_ROWS * 128 - flat.shape[0]))
    return flat.reshape(_SMALL_ROWS, 128)


def _unpack_small(packed, shapes):
    flat = packed.reshape(-1)
    out, off = {}, 0
    for n in _SMALL:
        size = int(np.prod(shapes[n]))
        out[n] = flat[off:off + size].reshape(shapes[n])
        off += size
    return out


def _local_grads(x, p, target, wts):
    ssm_names = ["ssm_a_re", "ssm_a_im", "ssm_log_dt", "ssm_b_re", "ssm_b_im", "ssm_c_re", "ssm_c_im", "ssm_d"]
    ssm_params = [wts[n][0] for n in ssm_names]
    blk, blk_vjp = jax.vjp(_ssm_block_params, *ssm_params)
    shard = lambda n: wts[n][0].astype(BF16)
    (m_mat, ws_mat, wot_mat, a16), (w_in_g,) = _ssm_chunk_matrices(blk, [shard("w_in")])
    seq = x.shape[0]
    bias_rows, bias_rows_vjp = jax.vjp(_na_bias_rows, wts["na_rpb"][0])
    bias_tab = _na_bias_table(bias_rows, seq // GRID_W)

    (u_c, z_s, q_t, q, k_t, k, v_t, v, z_n), gathered = _in_proj(
        x, wts["norm_pre"], w_in_g, [shard(n) for n in _BIG if n != "w_in"])
    w_glu, w_out, w_ple_g, w_pg = (gathered[0].reshape(512, 512), gathered[1].reshape(1024, 1024), gathered[2],
                                   gathered[3].reshape(1024, 1024))
    s_in = _block_matmul([(u_c, ws_mat, False)], "ssm_chunk_states")
    s_prev = _ssm_state_scan(s_in, a16)
    y_ssm_c = _block_matmul([(u_c, m_mat, False), (s_prev, wot_mat, True)], "ssm_chunk_out")
    y_na_t = _na_fwd(q_t, k, v_t, bias_tab)
    cat = _branch_fwd(y_ssm_c, z_s, y_na_t, z_n, w_glu, wts["b_glu"])

    (loss, d_h1, d_cat, d_w_out, d_g_post, d_w_ple, d_g_ple, d_w_pg) = _head(
        x, p, target, cat, w_out, wts["norm_post"], w_ple_g, wts["ple_norm"], w_pg)
    dy_c, d_z_s, d_y_na_t, d_y_na, d_z_n, d_w_glu, d_b_glu = _branch_bwd(
        y_ssm_c, z_s, y_na_t, z_n, w_glu, wts["b_glu"], d_cat)
    d_q_t, d_k, d_v, d_bias_tab = _na_bwd(q_t, q, k_t, k, v, bias_tab, y_na_t, d_y_na_t, d_y_na)

    d_prev = _block_matmul([(dy_c, wot_mat, False)], "ssm_bwd_states")
    g_st, d_a16 = _ssm_state_scan_bwd(d_prev, s_prev, a16)
    d_u_c = _block_matmul([(dy_c, m_mat, True), (g_st, ws_mat, True)], "ssm_bwd_in", out_dtype=BF16)
    d_m = _block_matmul_tn(u_c, dy_c, "ssm_grad_m")
    d_ws = _block_matmul_tn(u_c, g_st, "ssm_grad_ws")
    d_wot = _block_matmul_tn(dy_c, s_prev, "ssm_grad_wot")
    d_ssm = blk_vjp(tuple(_ssm_chunk_matrices_bwd(blk, d_m, d_ws, d_wot, d_a16)))
    (d_rpb,) = bias_rows_vjp(_na_bias_table_bwd(d_bias_tab, seq // GRID_W))

    dparts = [d_u_c, d_z_s, d_q_t, d_k, d_v, d_z_n]
    d_w_in, d_g_pre = _in_proj_bwd_w(x, wts["norm_pre"].reshape(D_MODEL, 1), w_in_g, dparts)

    small = {"norm_pre": d_g_pre, "norm_post": d_g_post, "b_glu": d_b_glu, "na_rpb": d_rpb, "ple_norm": d_g_ple}
    for n, g in zip(ssm_names, d_ssm):
        small[n] = g
    big = {"w_in": d_w_in, "w_glu": d_w_glu.reshape(N_CHIPS, 128, 512), "w_out": d_w_out.reshape(N_CHIPS, 256, 1024),
           "w_ple": d_w_ple, "w_ple_gate": d_w_pg.reshape(N_CHIPS, 256, 1024)}
    return loss, small, big, (x, wts["norm_pre"], w_in_g, d_h1, dparts)


def kernel(x, p, norm_pre, norm_post, w_in, ssm_a_re, ssm_a_im, ssm_log_dt, ssm_b_re, ssm_b_im, ssm_c_re, ssm_c_im, ssm_d, w_glu, b_glu, na_rpb, w_out, w_ple, ple_norm, w_ple_gate, loss_target, m_norm_pre, m_norm_post, m_w_in, m_ssm_a_re, m_ssm_a_im, m_ssm_log_dt, m_ssm_b_re, m_ssm_b_im, m_ssm_c_re, m_ssm_c_im, m_ssm_d, m_w_glu, m_b_glu, m_na_rpb, m_w_out, m_w_ple, m_ple_norm, m_w_ple_gate, v_norm_pre, v_norm_post, v_w_in, v_ssm_a_re, v_ssm_a_im, v_ssm_log_dt, v_ssm_b_re, v_ssm_b_im, v_ssm_c_re, v_ssm_c_im, v_ssm_d, v_w_glu, v_b_glu, v_na_rpb, v_w_out, v_w_ple, v_ple_norm, v_w_ple_gate):
    wts = dict(norm_pre=norm_pre, norm_post=norm_post, w_in=w_in, ssm_a_re=ssm_a_re, ssm_a_im=ssm_a_im,
               ssm_log_dt=ssm_log_dt, ssm_b_re=ssm_b_re, ssm_b_im=ssm_b_im, ssm_c_re=ssm_c_re, ssm_c_im=ssm_c_im,
               ssm_d=ssm_d, w_glu=w_glu, b_glu=b_glu, na_rpb=na_rpb, w_out=w_out, w_ple=w_ple, ple_norm=ple_norm,
               w_ple_gate=w_ple_gate)
    mom_m = dict(norm_pre=m_norm_pre, norm_post=m_norm_post, w_in=m_w_in, ssm_a_re=m_ssm_a_re, ssm_a_im=m_ssm_a_im,
                 ssm_log_dt=m_ssm_log_dt, ssm_b_re=m_ssm_b_re, ssm_b_im=m_ssm_b_im, ssm_c_re=m_ssm_c_re,
                 ssm_c_im=m_ssm_c_im, ssm_d=m_ssm_d, w_glu=m_w_glu, b_glu=m_b_glu, na_rpb=m_na_rpb, w_out=m_w_out,
                 w_ple=m_w_ple, ple_norm=m_ple_norm, w_ple_gate=m_w_ple_gate)
    mom_v = dict(norm_pre=v_norm_pre, norm_post=v_norm_post, w_in=v_w_in, ssm_a_re=v_ssm_a_re, ssm_a_im=v_ssm_a_im,
                 ssm_log_dt=v_ssm_log_dt, ssm_b_re=v_ssm_b_re, ssm_b_im=v_ssm_b_im, ssm_c_re=v_ssm_c_re,
                 ssm_c_im=v_ssm_c_im, ssm_d=v_ssm_d, w_glu=v_w_glu, b_glu=v_b_glu, na_rpb=v_na_rpb, w_out=v_w_out,
                 w_ple=v_w_ple, ple_norm=v_ple_norm, w_ple_gate=v_w_ple_gate)

    loss_part, small, big, input_grad_args = _local_grads(x[0], p[0, 0], loss_target[0], wts)

    core = lax.axis_index("c").astype(jnp.int32).reshape(1)
    small_packed = _pack_small(small, tail=loss_part).reshape(N_CHIPS, _SMALL_ROWS // N_CHIPS, 128)
    pair = _pair_sums(core, [big[n] for n in _BIG] + [small_packed], [BF16] * len(_BIG) + [F32])
    grad_x, landed = _in_proj_bwd_x(*input_grad_args, pair)
    reduced = _finish_reduce(core, landed)
    grads = dict(zip(_BIG, reduced[:-1]))
    (small_all,) = _gather_chips([reduced[-1]], "gather_small_grads")
    small_all = small_all.reshape(_SMALL_ROWS, 128)
    loss = small_all.reshape(-1)[sum(int(np.prod(wts[n].shape)) for n in _SMALL)]

    delta, new_m, new_v = {}, {}, {}
    for n in _BIG:
        shp = wts[n].shape
        d_, m_, v_ = _adamw(wts[n][0], grads[n], mom_m[n][0], mom_v[n][0])
        grads[n] = grads[n].reshape(shp)
        delta[n], new_m[n], new_v[n] = d_.reshape(shp), m_.reshape(shp), v_.reshape(shp)
    grads.update(_unpack_small(small_all, {n: wts[n].shape for n in _SMALL}))
    for n in _SMALL:
        shp = wts[n].shape
        rows_cols = (int(np.prod(shp[:-1])), shp[-1])
        d_, m_, v_ = _adamw(*[t.reshape(rows_cols) for t in (wts[n], grads[n], mom_m[n], mom_v[n])])
        delta[n], new_m[n], new_v[n] = d_.reshape(shp), m_.reshape(shp), v_.reshape(shp)

    return (loss, grad_x[None], *[grads[n] for n in _WEIGHTS], *[delta[n] for n in _WEIGHTS],
            *[new_m[n] for n in _WEIGHTS], *[new_v[n] for n in _WEIGHTS])
```

```python
import math

import jax
import jax.numpy as jnp
import numpy as np
from jax import lax
from jax.experimental import pallas as pl
from jax.experimental.pallas import tpu as pltpu

F32 = jnp.float32
BF16 = jnp.bfloat16

D_MODEL = 1024
D_PLE = 256
GRID_W = 64
D_SSM = 512
SSM_GROUP = 16
N_GROUPS = 32
SSM_STATE = 64
D_NA = 512
NA_HEADS = 8
NA_HEAD_DIM = 64
NA_ROWS = 8
NA_COLS = 16
D_IN_PROJ = 3072
EPS = 1e-6

CHUNK = 16
GROUPS_PER_BLOCK = 8
N_BLOCKS = N_GROUPS // GROUPS_PER_BLOCK
BLOCK_CH = GROUPS_PER_BLOCK * SSM_GROUP
BLOCK_ST = GROUPS_PER_BLOCK * SSM_STATE
CHUNK_W = CHUNK * BLOCK_CH
STATE_W = 4 * BLOCK_ST

N_CHIPS = 4
MESH = pl.DeviceIdType.MESH

ADAM_LR = 0.001
ADAM_B1 = 0.9
ADAM_B2 = 0.999
ADAM_EPS = 1e-08
ADAM_WD = 0.01
ADAM_STEP = 10

VMEM_LIMIT = 52 * 1024 * 1024
HIGHEST = lax.Precision.HIGHEST


def _cparams(sem=None, **kw):
    if sem is not None:
        kw["dimension_semantics"] = sem
    return pltpu.CompilerParams(vmem_limit_bytes=VMEM_LIMIT, **kw)


def _resident(*shape):
    return pl.BlockSpec(shape, lambda *_: (0,) * len(shape), pipeline_mode=pl.Buffered(1))


def _dot(a, b, dims=((1,), (0,))):
    return lax.dot_general(a, b, (dims, ((), ())), preferred_element_type=F32)


def _dot_nt(a, b):
    return _dot(a, b, ((1,), (1,)))


def _dot_tn(a, b):
    return _dot(a, b, ((0,), (0,)))


def _sigmoid(x):
    return 1.0 / (1.0 + jnp.exp(-x))


_GELU_C = math.sqrt(2.0 / math.pi)


def _gelu_parts(x):
    inner = _GELU_C * (x + 0.044715 * (x * x * x))
    t = jnp.tanh(inner)
    return 0.5 * x * (1.0 + t), t


def _gelu_grad(x, t):
    return 0.5 * (1.0 + t) + 0.5 * x * (1.0 - t * t) * (_GELU_C * (1.0 + 3.0 * 0.044715 * x * x))


def _silu_parts(z):
    s = _sigmoid(z)
    return z * s, s


def _silu_grad(z, s):
    return s * (1.0 + z * (1.0 - s))


def _rms(x):
    r = lax.rsqrt(jnp.mean(x * x, axis=-1, keepdims=True) + EPS)
    return x * r, r


def _rms_bwd(dn, n, r):
    return r * (dn - n * jnp.mean(dn * n, axis=-1, keepdims=True))


def _chunk_scratch(tm):
    return pltpu.VMEM((N_BLOCKS, tm, BLOCK_CH), F32)


def _store_chunks(val, scr, c_ref, dtype):
    nc = scr.shape[1] // CHUNK
    for b in range(N_BLOCKS):
        scr[b] = val[:, b * BLOCK_CH:(b + 1) * BLOCK_CH]
        for j in range(CHUNK):
            c_ref[b, :, j * BLOCK_CH:(j + 1) * BLOCK_CH] = scr[b, pl.ds(j, nc, stride=CHUNK), :].astype(dtype)


def _load_chunks(c_ref, scr):
    nc = scr.shape[1] // CHUNK
    for b in range(N_BLOCKS):
        for j in range(CHUNK):
            scr[b, pl.ds(j, nc, stride=CHUNK), :] = c_ref[b, :, j * BLOCK_CH:(j + 1) * BLOCK_CH].astype(F32)
    return jnp.concatenate([scr[b] for b in range(N_BLOCKS)], axis=1)


def _chunk_spec(tm):
    return pl.BlockSpec((N_BLOCKS, tm // CHUNK, CHUNK_W), lambda i: (0, i, 0))


def _heads_t_spec(tm):
    return pl.BlockSpec((D_NA, tm), lambda i: (0, i))


def _in_proj(x, g_pre, w_in_g, shards, tm=512):
    L = x.shape[0]
    wn = w_in_g.shape[2]
    n_sh = len(shards)
    steps = L // tm

    def body(*refs):
        x_ref, g_ref, w_ref = refs[:3]
        uc_ref, zs_ref, qt_ref, q_ref, kt_ref, k_ref, vt_ref, v_ref, zn_ref = refs[3 + n_sh:12 + n_sh]
        u_scr = refs[12 + 2 * n_sh]
        gather = _ChipGather(refs[3:3 + n_sh], refs[12 + n_sh:12 + 2 * n_sh], refs[13 + 2 * n_sh:])
        step = pl.program_id(0)
        pl.when(step == 0)(gather.start)
        pl.when(step == steps // 2)(gather.forward)
        pl.when(step == steps - 1)(gather.finish)
        n, _ = _rms(x_ref[...])
        hn = (n * g_ref[...]).astype(BF16)
        proj = jnp.concatenate([_dot(hn, w_ref[j]) for j in range(N_CHIPS)], axis=1)
        _store_chunks(proj[:, 0:512], u_scr, uc_ref, BF16)
        zs_ref[...] = proj[:, 512:1024]
        q = proj[:, 1024:1536] * (NA_HEAD_DIM ** -0.5)
        for val, t_ref, n_ref in ((q, qt_ref, q_ref), (proj[:, 1536:2048], kt_ref, k_ref), (proj[:, 2048:2560], vt_ref, v_ref)):
            t_ref[...] = val.T.astype(BF16)
            n_ref[...] = val.astype(BF16)
        zn_ref[...] = proj[:, 2560:3072]

    tok = jax.ShapeDtypeStruct((L, 512), F32)
    tr = jax.ShapeDtypeStruct((D_NA, L), BF16)
    hm = jax.ShapeDtypeStruct((L, D_NA), BF16)
    tspec = pl.BlockSpec((tm, 512), lambda i: (i, 0))
    outs = pl.pallas_call(
        body, name="in_proj", grid=(steps,),
        in_specs=[pl.BlockSpec((tm, D_MODEL), lambda i: (i, 0)),
                  _resident(1, D_MODEL), _resident(N_CHIPS, D_MODEL, wn)] + _hbm_specs(n_sh),
        out_specs=[_chunk_spec(tm), tspec] + [_heads_t_spec(tm), tspec] * 3 + [tspec] + _hbm_specs(n_sh),
        out_shape=[jax.ShapeDtypeStruct((N_BLOCKS, L // CHUNK, CHUNK_W), BF16), tok, tr, hm, tr, hm, tr, hm, tok]
        + _gather_out_shapes(shards),
        scratch_shapes=[_chunk_scratch(tm)] + _gather_semaphores(n_sh),
        compiler_params=_cparams(("arbitrary",), has_side_effects=True),
    )(x, g_pre, w_in_g, *shards)
    return outs[:9], outs[9:]


def _ssm_block_params(a_re, a_im, log_dt, b_re, b_im, c_re, c_im, d):
    def lanes(t):
        return t.reshape(2, N_BLOCKS, 1, BLOCK_ST)

    rows = (2, N_BLOCKS, BLOCK_CH, SSM_STATE)
    b_rows = lambda t: t.reshape(2, N_BLOCKS, GROUPS_PER_BLOCK, SSM_STATE, SSM_GROUP).transpose(0, 1, 2, 4, 3).reshape(rows)
    return (lanes(a_re), lanes(a_im), lanes(jnp.broadcast_to(log_dt[..., None], a_re.shape)),
            b_rows(b_re), b_rows(b_im), c_re.reshape(rows), c_im.reshape(rows), d.reshape(N_BLOCKS, 1, BLOCK_CH))


def _ssm_group_mask():
    row_g = lax.broadcasted_iota(jnp.int32, (BLOCK_CH, BLOCK_ST), 0) // SSM_GROUP
    lane_g = lax.broadcasted_iota(jnp.int32, (BLOCK_CH, BLOCK_ST), 1) // SSM_STATE
    return row_g == lane_g


def _ssm_state_select():
    p = lax.broadcasted_iota(jnp.int32, (SSM_STATE, BLOCK_ST), 0)
    lane_p = lax.broadcasted_iota(jnp.int32, (SSM_STATE, BLOCK_ST), 1) % SSM_STATE
    return (p == lane_p).astype(F32)


def _ssm_expand_blocks(compact_refs, full_refs):
    mask, select = _ssm_group_mask(), _ssm_state_select()
    for c_ref, f_ref in zip(compact_refs, full_refs):
        for d in range(2):
            tiled = lax.dot_general(c_ref[d, 0], select, ((((1,), (0,))), ((), ())), precision=HIGHEST,
                                    preferred_element_type=F32)
            f_ref[d, 0] = jnp.where(mask, tiled, 0.0)


def _ssm_collapse_block(t):
    return lax.dot_general(jnp.where(_ssm_group_mask(), t, 0.0), _ssm_state_select(), ((((1,), (1,))), ((), ())),
                           precision=HIGHEST, preferred_element_type=F32)


def _ssm_discretise(ar, ai, ldt):
    dt = jnp.exp(ldt)
    mag = jnp.exp(dt * ar)
    abr = mag * jnp.cos(dt * ai)
    abi = mag * jnp.sin(dt * ai)
    num_re = abr - 1.0
    num_im = abi
    denom = ar * ar + ai * ai
    coef_re = (num_re * ar + num_im * ai) / denom
    coef_im = (num_im * ar - num_re * ai) / denom
    return abr, abi, coef_re, coef_im


_POW_ROWS = 24


def _ssm_fill_powers(ar_ref, ai_ref, ldt_ref, br_ref, bi_ref, pw_ref, bbar_ref):
    for d in range(2):
        abr, abi, cfr, cfi = _ssm_discretise(ar_ref[d, 0], ai_ref[d, 0], ldt_ref[d, 0])
        bbar_ref[d, 0] = cfr * br_ref[d, 0] - cfi * bi_ref[d, 0]
        bbar_ref[d, 1] = cfr * bi_ref[d, 0] + cfi * br_ref[d, 0]
        pr, pi = jnp.ones_like(abr), jnp.zeros_like(abi)
        for t in range(CHUNK + 1):
            pw_ref[d, 0, t:t + 1, :] = pr
            pw_ref[d, 1, t:t + 1, :] = pi
            pr, pi = pr * abr - pi * abi, pr * abi + pi * abr


def _dot_rounded(a, b, dims=((1,), (0,))):
    return _dot(a.astype(BF16), b.astype(BF16), dims)


def _ssm_stack_inputs(d, pw_ref, bbar_ref, xs_ref):
    for t in range(CHUNK):
        pr, pi = pw_ref[d, 0, t:t + 1, :], pw_ref[d, 1, t:t + 1, :]
        xs_ref[0, t * BLOCK_CH:(t + 1) * BLOCK_CH, :] = bbar_ref[d, 0] * pr - bbar_ref[d, 1] * pi
        xs_ref[1, t * BLOCK_CH:(t + 1) * BLOCK_CH, :] = bbar_ref[d, 0] * pi + bbar_ref[d, 1] * pr


def _eye(n):
    return (lax.broadcasted_iota(jnp.int32, (n, n), 0) == lax.broadcasted_iota(jnp.int32, (n, n), 1)).astype(F32)


def _ssm_param_specs():
    vec = pl.BlockSpec((2, 1, 1, BLOCK_ST), lambda b, j: (0, b, 0, 0))
    mat = pl.BlockSpec((2, 1, BLOCK_CH, SSM_STATE), lambda b, j: (0, b, 0, 0))
    return [vec, vec, vec, mat, mat, mat, mat, pl.BlockSpec((1, 1, BLOCK_CH), lambda b, j: (b, 0, 0))]


def _ssm_block_scratch():
    return [pltpu.VMEM((2, 1, BLOCK_CH, BLOCK_ST), F32)] * 4


def _ssm_chunk_matrices(blk, shards):
    n = len(shards)

    def body(*refs):
        ar_ref, ai_ref, ldt_ref = refs[:3]
        d_ref = refs[7]
        m_ref, ws_ref, wot_ref, a16_ref = refs[8 + n:12 + n]
        pw_ref, bbar_ref, lag_ref, xs_ref = refs[12 + 2 * n:16 + 2 * n]
        br_ref, bi_ref, cr_ref, ci_ref = refs[16 + 2 * n:20 + 2 * n]
        gather = _ChipGather(refs[8:8 + n], refs[12 + n:12 + 2 * n], refs[20 + 2 * n:])
        b, j = pl.program_id(0), pl.program_id(1)
        pl.when((b == 0) & (j == 0))(gather.start)
        pl.when((b == N_BLOCKS - 1) & (j == 0))(gather.forward)
        pl.when((b == N_BLOCKS - 1) & (j == CHUNK - 1))(gather.finish)

        @pl.when(j == 0)
        def _():
            _ssm_expand_blocks(refs[3:7], (br_ref, bi_ref, cr_ref, ci_ref))
            _ssm_fill_powers(ar_ref, ai_ref, ldt_ref, br_ref, bi_ref, pw_ref, bbar_ref)
            zero_lag = d_ref[0] * _eye(BLOCK_CH)
            for d in range(2):
                _ssm_stack_inputs(d, pw_ref, bbar_ref, xs_ref)
                taps = (_dot_rounded(xs_ref[0], cr_ref[d, 0], ((1,), (1,)))
                        - _dot_rounded(xs_ref[1], ci_ref[d, 0], ((1,), (1,))))
                zero_lag = zero_lag + taps[0:BLOCK_CH]
                for t in range(1, CHUNK):
                    lag_ref[CHUNK - 1 + t if d == 0 else CHUNK - 1 - t] = taps[t * BLOCK_CH:(t + 1) * BLOCK_CH]
            lag_ref[CHUNK - 1] = zero_lag
            a16_ref[0] = jnp.concatenate([pw_ref[d, ri, CHUNK:CHUNK + 1, :] for d in range(2) for ri in range(2)], axis=1)

        m_ref[0] = jnp.concatenate([lag_ref[jp - j + CHUNK - 1] for jp in range(CHUNK)], axis=1).astype(BF16)

        def power(d, t):
            return pw_ref[d, 0, pl.ds(t, 1), :], pw_ref[d, 1, pl.ds(t, 1), :]

        parts = []
        for d, t in ((0, CHUNK - 1 - j), (1, j)):
            pr, pi = power(d, t)
            parts += [bbar_ref[d, 0] * pr - bbar_ref[d, 1] * pi, bbar_ref[d, 0] * pi + bbar_ref[d, 1] * pr]
        ws_ref[0] = jnp.concatenate(parts, axis=1).astype(BF16)
        parts = []
        for d, t in ((0, j + 1), (1, CHUNK - j)):
            pr, pi = power(d, t)
            parts += [cr_ref[d, 0] * pr - ci_ref[d, 0] * pi, -cr_ref[d, 0] * pi - ci_ref[d, 0] * pr]
        wot_ref[0] = jnp.concatenate(parts, axis=1).astype(BF16)

    row = pl.BlockSpec((1, BLOCK_CH, CHUNK_W), lambda b, j: (b, j, 0))
    mat = jax.ShapeDtypeStruct((N_BLOCKS, CHUNK_W, CHUNK_W), BF16)
    outs = pl.pallas_call(
        body, name="ssm_chunk_matrices", grid=(N_BLOCKS, CHUNK),
        in_specs=_ssm_param_specs() + _hbm_specs(n),
        out_specs=[row, row, row, pl.BlockSpec((1, 1, STATE_W), lambda b, j: (b, 0, 0))] + _hbm_specs(n),
        out_shape=[mat, mat, mat, jax.ShapeDtypeStruct((N_BLOCKS, 1, STATE_W), F32)] + _gather_out_shapes(shards),
        scratch_shapes=[pltpu.VMEM((2, 2, _POW_ROWS, BLOCK_ST), F32), pltpu.VMEM((2, 2, BLOCK_CH, BLOCK_ST), F32),
                        pltpu.VMEM((2 * CHUNK, BLOCK_CH, BLOCK_CH), F32), pltpu.VMEM((2, CHUNK_W, BLOCK_ST), F32)]
        + _ssm_block_scratch() + _gather_semaphores(n),
        compiler_params=_cparams(("arbitrary", "arbitrary"), has_side_effects=True),
    )(*blk, *shards)
    return outs[:4], outs[4:]


def _ssm_chunk_matrices_bwd(blk, d_m, d_ws, d_wot, d_a16):
    def body(ar_ref, ai_ref, ldt_ref, brc_ref, bic_ref, crc_ref, cic_ref, d_ref, dm_ref, dws_ref, dwot_ref, da16_ref,
             dar_ref, dai_ref, dldt_ref, dbr_ref, dbi_ref, dcr_ref, dci_ref, dd_ref,
             pw_ref, bbar_ref, dlag_ref, dbbar_ref, dc_ref, dpw_ref, xs_ref, dts_ref, br_ref, bi_ref, cr_ref, ci_ref):
        j = pl.program_id(1)
        w = BLOCK_ST

        @pl.when(j == 0)
        def _():
            _ssm_expand_blocks((brc_ref, bic_ref, crc_ref, cic_ref), (br_ref, bi_ref, cr_ref, ci_ref))
            _ssm_fill_powers(ar_ref, ai_ref, ldt_ref, br_ref, bi_ref, pw_ref, bbar_ref)
            for r in (dlag_ref, dbbar_ref, dc_ref, dpw_ref):
                r[...] = jnp.zeros_like(r)

        def fold(t):
            return jnp.sum(t.reshape(BLOCK_CH // 8, 8, w), axis=0)

        def d_power(d, ri, t):
            return jnp.sum(dpw_ref[d, ri, t], axis=0, keepdims=True)

        def x_chain(d, t, dxr, dxi):
            pr, pi = pw_ref[d, 0, pl.ds(t, 1), :], pw_ref[d, 1, pl.ds(t, 1), :]
            bbr, bbi = bbar_ref[d, 0], bbar_ref[d, 1]
            dbbar_ref[d, 0] += dxr * pr + dxi * pi
            dbbar_ref[d, 1] += dxi * pr - dxr * pi
            dpw_ref[d, 0, t] += fold(dxr * bbr + dxi * bbi)
            dpw_ref[d, 1, t] += fold(dxi * bbr - dxr * bbi)

        def z_chain(d, t, dzr, dzi):
            pr, pi = pw_ref[d, 0, pl.ds(t, 1), :], pw_ref[d, 1, pl.ds(t, 1), :]
            c_r, c_i = cr_ref[d, 0], ci_ref[d, 0]
            dc_ref[d, 0] += dzr * pr - dzi * pi
            dc_ref[d, 1] += -dzr * pi - dzi * pr
            dpw_ref[d, 0, t] += fold(dzr * c_r - dzi * c_i)
            dpw_ref[d, 1, t] += fold(-dzr * c_i - dzi * c_r)

        for jp in range(CHUNK):
            dlag_ref[jp - j + CHUNK - 1] += dm_ref[0, :, jp * BLOCK_CH:(jp + 1) * BLOCK_CH].astype(F32)
        quarter = lambda ref, i: ref[0, :, i * w:(i + 1) * w].astype(F32)
        x_chain(0, CHUNK - 1 - j, quarter(dws_ref, 0), quarter(dws_ref, 1))
        x_chain(1, j, quarter(dws_ref, 2), quarter(dws_ref, 3))
        z_chain(0, j + 1, quarter(dwot_ref, 0), quarter(dwot_ref, 1))
        z_chain(1, CHUNK - j, quarter(dwot_ref, 2), quarter(dwot_ref, 3))

        @pl.when(j == CHUNK - 1)
        def _():
            for d in range(2):
                _ssm_stack_inputs(d, pw_ref, bbar_ref, xs_ref)
                for t in range(CHUNK):
                    dts_ref[t * BLOCK_CH:(t + 1) * BLOCK_CH, :] = dlag_ref[CHUNK - 1 + t if d == 0 else CHUNK - 1 - t]
                d_taps = dts_ref[...]
                dc_ref[d, 0] += _dot_rounded(d_taps, xs_ref[0], ((0,), (0,)))
                dc_ref[d, 1] -= _dot_rounded(d_taps, xs_ref[1], ((0,), (0,)))
                xs_ref[0] = _dot_rounded(d_taps, cr_ref[d, 0])
                xs_ref[1] = -_dot_rounded(d_taps, ci_ref[d, 0])
                for t in range(CHUNK):
                    rows = slice(t * BLOCK_CH, (t + 1) * BLOCK_CH)
                    x_chain(d, t, xs_ref[0, rows, :], xs_ref[1, rows, :])
            dd_ref[0] = jnp.sum(dlag_ref[CHUNK - 1] * _eye(BLOCK_CH), axis=0, keepdims=True)
            for d in range(2):
                (abr, abi, cfr, cfi), disc_vjp = jax.vjp(_ssm_discretise, ar_ref[d, 0], ai_ref[d, 0], ldt_ref[d, 0])
                dpr = d_power(d, 0, CHUNK) + da16_ref[0, :, 2 * d * w:(2 * d + 1) * w]
                dpi = d_power(d, 1, CHUNK) + da16_ref[0, :, (2 * d + 1) * w:(2 * d + 2) * w]
                dabr, dabi = jnp.zeros_like(abr), jnp.zeros_like(abi)
                for t in range(CHUNK, 0, -1):
                    qr, qi = pw_ref[d, 0, t - 1:t, :], pw_ref[d, 1, t - 1:t, :]
                    dabr = dabr + dpr * qr + dpi * qi
                    dabi = dabi + dpi * qr - dpr * qi
                    dpr, dpi = (dpr * abr + dpi * abi + d_power(d, 0, t - 1),
                                dpi * abr - dpr * abi + d_power(d, 1, t - 1))
                dbbr, dbbi = dbbar_ref[d, 0], dbbar_ref[d, 1]
                b_r, b_i = br_ref[d, 0], bi_ref[d, 0]
                dbr_ref[d, 0] = _ssm_collapse_block(cfr * dbbr + cfi * dbbi)
                dbi_ref[d, 0] = _ssm_collapse_block(cfr * dbbi - cfi * dbbr)
                dcfr = jnp.sum(b_r * dbbr + b_i * dbbi, axis=0, keepdims=True)
                dcfi = jnp.sum(b_r * dbbi - b_i * dbbr, axis=0, keepdims=True)
                dar_ref[d, 0], dai_ref[d, 0], dldt_ref[d, 0] = disc_vjp((dabr, dabi, dcfr, dcfi))
                dcr_ref[d, 0] = _ssm_collapse_block(dc_ref[d, 0])
                dci_ref[d, 0] = _ssm_collapse_block(dc_ref[d, 1])

    row = pl.BlockSpec((1, BLOCK_CH, CHUNK_W), lambda b, j: (b, j, 0))
    specs = _ssm_param_specs()
    acc = lambda *s: pltpu.VMEM(s, F32)
    return pl.pallas_call(
        body, name="ssm_chunk_matrices_bwd", grid=(N_BLOCKS, CHUNK),
        in_specs=specs + [row, row, row, pl.BlockSpec((1, 1, STATE_W), lambda b, j: (b, 0, 0))],
        out_specs=specs,
        out_shape=[jax.ShapeDtypeStruct(t.shape, F32) for t in blk],
        scratch_shapes=[acc(2, 2, _POW_ROWS, BLOCK_ST), acc(2, 2, BLOCK_CH, BLOCK_ST), acc(2 * CHUNK, BLOCK_CH, BLOCK_CH),
                        acc(2, 2, BLOCK_CH, BLOCK_ST), acc(2, 2, BLOCK_CH, BLOCK_ST), acc(2, 2, CHUNK + 1, 8, BLOCK_ST),
                        acc(2, CHUNK_W, BLOCK_ST), acc(CHUNK_W, BLOCK_CH)] + _ssm_block_scratch(),
        compiler_params=_cparams(("arbitrary", "arbitrary")),
    )(*blk, d_m, d_ws, d_wot, d_a16)


def _block_matmul(terms, name, out_dtype=F32, tn=1024):
    nc = terms[0][0].shape[1]
    n_out = terms[0][1].shape[1] if terms[0][2] else terms[0][1].shape[2]
    flags = [t[2] for t in terms]

    def body(*refs):
        out_ref = refs[-1]
        acc = None
        for t, transposed in enumerate(flags):
            a = refs[2 * t][0].astype(BF16)
            w = refs[2 * t + 1][0]
            part = _dot_nt(a, w) if transposed else _dot(a, w)
            acc = part if acc is None else acc + part
        out_ref[0] = acc.astype(out_dtype)

    in_specs, args = [], []
    for a, w, transposed in terms:
        k = a.shape[2]
        in_specs.append(pl.BlockSpec((1, nc, k), lambda b, n: (b, 0, 0)))
        if transposed:
            in_specs.append(pl.BlockSpec((1, tn, k), lambda b, n: (b, n, 0)))
        else:
            in_specs.append(pl.BlockSpec((1, k, tn), lambda b, n: (b, 0, n)))
        args += [a, w]
    return pl.pallas_call(
        body, name=name, grid=(N_BLOCKS, n_out // tn), in_specs=in_specs,
        out_specs=pl.BlockSpec((1, nc, tn), lambda b, n: (b, 0, n)),
        out_shape=jax.ShapeDtypeStruct((N_BLOCKS, nc, n_out), out_dtype),
        compiler_params=_cparams(("arbitrary", "arbitrary")),
    )(*args)


def _block_matmul_tn(a, b, name, tile=1024):
    nc, m = a.shape[1], a.shape[2]
    n = b.shape[2]

    def body(a_ref, b_ref, out_ref):
        out_ref[0] = _dot_tn(a_ref[0].astype(BF16), b_ref[0].astype(BF16)).astype(BF16)

    return pl.pallas_call(
        body, name=name, grid=(N_BLOCKS, m // tile, n // tile),
        in_specs=[pl.BlockSpec((1, nc, tile), lambda blk, i, j: (blk, 0, i)),
                  pl.BlockSpec((1, nc, tile), lambda blk, i, j: (blk, 0, j))],
        out_specs=pl.BlockSpec((1, tile, tile), lambda blk, i, j: (blk, i, j)),
        out_shape=jax.ShapeDtypeStruct((N_BLOCKS, m, n), BF16),
        compiler_params=_cparams(("arbitrary", "arbitrary", "arbitrary")),
    )(a, b)


def _cmul(ar, ai, xr, xi):
    return ar * xr - ai * xi, ar * xi + ai * xr


def _cmul_conj(ar, ai, xr, xi):
    return ar * xr + ai * xi, ar * xi - ai * xr


def _ssm_state_scan(s_in, a16):
    nc = s_in.shape[1]
    w = BLOCK_ST

    def body(sin_ref, a_ref, out_ref):
        a = a_ref[0]
        afr, afi, abr, abi = a[:, 0:w], a[:, w:2 * w], a[:, 2 * w:3 * w], a[:, 3 * w:4 * w]

        def step(c, carry):
            fr, fi, br, bi = carry
            cb = nc - 1 - c
            out_ref[0, pl.ds(c, 1), 0:w] = fr
            out_ref[0, pl.ds(c, 1), w:2 * w] = fi
            out_ref[0, pl.ds(cb, 1), 2 * w:3 * w] = br
            out_ref[0, pl.ds(cb, 1), 3 * w:4 * w] = bi
            nfr, nfi = _cmul(afr, afi, fr, fi)
            nbr, nbi = _cmul(abr, abi, br, bi)
            return (nfr + sin_ref[0, pl.ds(c, 1), 0:w], nfi + sin_ref[0, pl.ds(c, 1), w:2 * w],
                    nbr + sin_ref[0, pl.ds(cb, 1), 2 * w:3 * w], nbi + sin_ref[0, pl.ds(cb, 1), 3 * w:4 * w])

        z = jnp.zeros((1, w), F32)
        lax.fori_loop(0, nc, step, (z, z, z, z))

    spec = pl.BlockSpec((1, nc, STATE_W), lambda b: (b, 0, 0))
    return pl.pallas_call(
        body, name="ssm_state_scan", grid=(N_BLOCKS,),
        in_specs=[spec, pl.BlockSpec((1, 1, STATE_W), lambda b: (b, 0, 0))],
        out_specs=spec, out_shape=jax.ShapeDtypeStruct(s_in.shape, F32),
        compiler_params=_cparams(("arbitrary",)),
    )(s_in, a16)


def _ssm_state_scan_bwd(d_prev, s_prev, a16):
    nc = d_prev.shape[1]
    w = BLOCK_ST

    def body(dp_ref, sp_ref, a_ref, g_ref, da_ref):
        a = a_ref[0]
        afr, afi, abr, abi = a[:, 0:w], a[:, w:2 * w], a[:, 2 * w:3 * w], a[:, 3 * w:4 * w]

        def step(i, carry):
            gfr, gfi, gbr, gbi, dafr, dafi, dabr, dabi = carry
            cf = nc - 1 - i
            cb = i
            g_ref[0, pl.ds(cf, 1), 0:w] = gfr
            g_ref[0, pl.ds(cf, 1), w:2 * w] = gfi
            g_ref[0, pl.ds(cb, 1), 2 * w:3 * w] = gbr
            g_ref[0, pl.ds(cb, 1), 3 * w:4 * w] = gbi
            sfr, sfi = sp_ref[0, pl.ds(cf, 1), 0:w], sp_ref[0, pl.ds(cf, 1), w:2 * w]
            sbr, sbi = sp_ref[0, pl.ds(cb, 1), 2 * w:3 * w], sp_ref[0, pl.ds(cb, 1), 3 * w:4 * w]
            dafr = dafr + gfr * sfr + gfi * sfi
            dafi = dafi + gfi * sfr - gfr * sfi
            dabr = dabr + gbr * sbr + gbi * sbi
            dabi = dabi + gbi * sbr - gbr * sbi
            nfr, nfi = _cmul_conj(afr, afi, gfr, gfi)
            nbr, nbi = _cmul_conj(abr, abi, gbr, gbi)
            return (nfr + dp_ref[0, pl.ds(cf, 1), 0:w], nfi + dp_ref[0, pl.ds(cf, 1), w:2 * w],
                    nbr + dp_ref[0, pl.ds(cb, 1), 2 * w:3 * w], nbi + dp_ref[0, pl.ds(cb, 1), 3 * w:4 * w],
                    dafr, dafi, dabr, dabi)

        z = jnp.zeros((1, w), F32)
        res = lax.fori_loop(0, nc, step, (z,) * 8)
        da_ref[0] = jnp.concatenate(res[4:], axis=1)

    spec = pl.BlockSpec((1, nc, STATE_W), lambda b: (b, 0, 0))
    aspec = pl.BlockSpec((1, 1, STATE_W), lambda b: (b, 0, 0))
    return pl.pallas_call(
        body, name="ssm_state_scan_bwd", grid=(N_BLOCKS,),
        in_specs=[spec, spec, aspec], out_specs=[spec, aspec],
        out_shape=[jax.ShapeDtypeStruct(d_prev.shape, F32), jax.ShapeDtypeStruct((N_BLOCKS, 1, STATE_W), F32)],
        compiler_params=_cparams(("arbitrary",)),
    )(d_prev, s_prev, a16)


NA_PAIR = 2 * GRID_W
NA_WIN_ROWS = NA_ROWS + 2
NA_WIN = NA_WIN_ROWS * GRID_W
NA_PAIRS_PER_STEP = 8
NA_CASES = 5
NA_MASKED = -1e30


def _na_pair_window(m, rows):
    rs0 = jnp.clip(2 * m - NA_ROWS // 2, 0, rows - NA_ROWS)
    ws = jnp.minimum(rs0, rows - NA_WIN_ROWS)
    last = rows // 2 - 1
    case = jnp.where(m == 0, 0, jnp.where(m == 1, 1, jnp.where(m == last - 1, 3, jnp.where(m == last, 4, 2))))
    return ws, case


def _na_row_offsets(rows):
    last = rows // 2 - 1
    geom = []
    for m in (0, 1, 2, last - 1, last):
        ws = min(max(2 * m - NA_ROWS // 2, 0), rows - NA_ROWS, rows - NA_WIN_ROWS)
        per_case = []
        for i in range(NA_WIN_ROWS):
            pair = []
            for rr in range(2):
                r = 2 * m + rr
                rs = min(max(r - NA_ROWS // 2, 0), rows - NA_ROWS)
                pair.append(ws + i - r + NA_ROWS - 1 if rs <= ws + i < rs + NA_ROWS else None)
            per_case.append(pair)
        geom.append(per_case)
    return geom


def _na_col_select():
    qc = np.arange(NA_PAIR)[None, :] % GRID_W
    kc = np.arange(GRID_W)[:, None]
    dc = np.clip(kc - qc + NA_COLS - 1, 0, 2 * NA_COLS - 2)
    return jnp.asarray((np.arange(2 * NA_COLS - 1)[:, None, None] == dc[None]).astype(np.float32))


def _na_bias_rows(rpb):
    return jnp.einsum("hrd,dkl->hrkl", rpb, _na_col_select(), precision=HIGHEST)


def _na_col_window():
    qc = lax.broadcasted_iota(jnp.int32, (GRID_W, NA_PAIR), 1) % GRID_W
    kc = lax.broadcasted_iota(jnp.int32, (GRID_W, NA_PAIR), 0)
    cs = jnp.clip(qc - NA_COLS // 2, 0, GRID_W - NA_COLS)
    first_row = lax.broadcasted_iota(jnp.int32, (GRID_W, NA_PAIR), 1) < GRID_W
    return (kc >= cs) & (kc < cs + NA_COLS), first_row


def _na_bias_table(bias_rows, rows):
    geom = _na_row_offsets(rows)

    def body(br_ref, tab_ref):
        col_ok, first_row = _na_col_window()
        masked = jnp.full((GRID_W, NA_PAIR), NA_MASKED, F32)
        for case in range(NA_CASES):
            for i in range(NA_WIN_ROWS):
                d0, d1 = geom[case][i]
                t0 = masked if d0 is None else br_ref[0, d0]
                t1 = masked if d1 is None else br_ref[0, d1]
                tile = jnp.where(col_ok, jnp.where(first_row, t0, t1), NA_MASKED)
                tab_ref[0, case, i * GRID_W:(i + 1) * GRID_W, :] = tile

    return pl.pallas_call(
        body, name="na_bias_table", grid=(NA_HEADS,),
        in_specs=[pl.BlockSpec((1, 2 * NA_ROWS - 1, GRID_W, NA_PAIR), lambda h: (h, 0, 0, 0))],
        out_specs=pl.BlockSpec((1, NA_CASES, NA_WIN, NA_PAIR), lambda h: (h, 0, 0, 0)),
        out_shape=jax.ShapeDtypeStruct((NA_HEADS, NA_CASES, NA_WIN, NA_PAIR), F32),
        compiler_params=_cparams(("arbitrary",)),
    )(bias_rows)


def _na_bias_table_bwd(d_tab, rows):
    geom = _na_row_offsets(rows)

    def body(dt_ref, dbr_ref):
        col_ok, first_row = _na_col_window()
        acc = [None] * (2 * NA_ROWS - 1)
        for case in range(NA_CASES):
            for i in range(NA_WIN_ROWS):
                tile = jnp.where(col_ok, dt_ref[0, case, i * GRID_W:(i + 1) * GRID_W, :], 0.0)
                for rr, d in enumerate(geom[case][i]):
                    if d is not None:
                        part = jnp.where(first_row if rr == 0 else ~first_row, tile, 0.0)
                        acc[d] = part if acc[d] is None else acc[d] + part
        for d, a in enumerate(acc):
            dbr_ref[0, d] = jnp.zeros((GRID_W, NA_PAIR), F32) if a is None else a

    return pl.pallas_call(
        body, name="na_bias_table_bwd", grid=(NA_HEADS,),
        in_specs=[pl.BlockSpec((1, NA_CASES, NA_WIN, NA_PAIR), lambda h: (h, 0, 0, 0))],
        out_specs=pl.BlockSpec((1, 2 * NA_ROWS - 1, GRID_W, NA_PAIR), lambda h: (h, 0, 0, 0)),
        out_shape=jax.ShapeDtypeStruct((NA_HEADS, 2 * NA_ROWS - 1, GRID_W, NA_PAIR), F32),
        compiler_params=_cparams(("arbitrary",)),
    )(d_tab)


NA_BLK = 64


def _na_blocks():
    return [slice(i * NA_BLK, (i + 1) * NA_BLK) for i in range(NA_WIN // NA_BLK)]


def _na_softmax(qk, bias_ref, hh, case):
    m = jnp.full((NA_BLK, NA_PAIR), -jnp.inf, F32)
    scores = []
    for blk in _na_blocks():
        s = qk[blk, :] + bias_ref[hh, case, blk, :]
        scores.append(s)
        m = jnp.maximum(m, s)
    m = jnp.max(m, axis=0, keepdims=True)
    l = jnp.zeros((NA_BLK, NA_PAIR), F32)
    exps = []
    for s in scores:
        e = jnp.exp(s - m)
        exps.append(e)
        l = l + e
    return exps, jnp.sum(l, axis=0, keepdims=True)


def _na_units(step, rows):
    units = []
    for pp in range(NA_PAIRS_PER_STEP):
        ws, case = _na_pair_window(step * NA_PAIRS_PER_STEP + pp, rows)
        win = pl.ds(pl.multiple_of(ws * GRID_W, NA_PAIR), NA_WIN)
        lanes = slice(pp * NA_PAIR, (pp + 1) * NA_PAIR)
        for hh in range(2):
            units.append((pp, hh, case, win, lanes, slice(hh * NA_HEAD_DIM, (hh + 1) * NA_HEAD_DIM)))
    return units


def _na_pipeline(n, before, middle, after, lookahead=2):
    for u in range(min(lookahead, n)):
        for f in before:
            f(u)
    for u in range(n):
        middle(u)
        if u + lookahead < n:
            for f in before:
                f(u + lookahead)
        for f in after:
            f(u)


def _head_rows(t, hh):
    row_head = lax.broadcasted_iota(jnp.int32, t.shape, 0) // NA_HEAD_DIM
    return jnp.where(row_head == hh, t, jnp.zeros_like(t))


def _heads_block_diag(t):
    lane_head = lax.broadcasted_iota(jnp.int32, t.shape, 1) // NA_HEAD_DIM
    zero = jnp.zeros_like(t)
    return jnp.concatenate([jnp.where(lane_head == 0, t, zero), jnp.where(lane_head == 1, t, zero)], axis=0)


def _na_fwd(q_t, k, v_t, bias_tab):
    L = k.shape[0]
    rows = L // GRID_W
    step_w = NA_PAIRS_PER_STEP * NA_PAIR

    def body(q_ref, k_ref, v_ref, bt_ref, o_ref):
        units = _na_units(pl.program_id(1), rows)
        qk, probs = {}, {}

        def scores(u):
            _, hh, _, win, lanes, _ = units[u]
            qk[u] = _dot(k_ref[win, :], _head_rows(q_ref[:, lanes], hh))

        def softmax(u):
            _, hh, case, _, _, _ = units[u]
            exps, l = _na_softmax(qk.pop(u), bt_ref, hh, case)
            probs[u] = jnp.concatenate([t.astype(BF16) for t in exps], axis=0), l

        def output(u):
            _, _, _, win, lanes, hrows = units[u]
            e, l = probs.pop(u)
            o_ref[hrows, lanes] = _dot(v_ref[hrows, win], e) / l

        _na_pipeline(len(units), [scores], softmax, [output])

    q_spec = pl.BlockSpec((NA_PAIR, step_w), lambda h, s: (h, s))
    return pl.pallas_call(
        body, name="na_fwd", grid=(NA_HEADS // 2, L // step_w),
        in_specs=[q_spec, pl.BlockSpec((L, NA_PAIR), lambda h, s: (0, h)),
                  pl.BlockSpec((NA_PAIR, L), lambda h, s: (h, 0)),
                  pl.BlockSpec((2, NA_CASES, NA_WIN, NA_PAIR), lambda h, s: (h, 0, 0, 0))],
        out_specs=q_spec,
        out_shape=jax.ShapeDtypeStruct((D_NA, L), F32),
        compiler_params=_cparams(("arbitrary", "arbitrary")),
    )(q_t, k, v_t, bias_tab)


def _na_bwd(q_t, q, k_t, k, v, bias_tab, out_t, d_out_t, d_out):
    L = k.shape[0]
    rows = L // GRID_W
    step_w = NA_PAIRS_PER_STEP * NA_PAIR

    def body(qt_ref, q_ref, kt_ref, k_ref, v_ref, bt_ref, ot_ref, dot_ref, do_ref, dq_ref, dk_ref, dv_ref, dbt_ref):
        @pl.when(pl.program_id(1) == 0)
        def _():
            dk_ref[...] = jnp.zeros_like(dk_ref)
            dv_ref[...] = jnp.zeros_like(dv_ref)
            dbt_ref[...] = jnp.zeros_like(dbt_ref)

        units = _na_units(pl.program_id(1), rows)
        qk, dp, dsb, pb = {}, {}, {}, {}

        def scores(u):
            _, hh, _, win, lanes, _ = units[u]
            qk[u] = _dot(k_ref[win, :], _head_rows(qt_ref[:, lanes], hh))

        def d_probs(u):
            _, hh, _, win, lanes, _ = units[u]
            dp[u] = _dot(v_ref[win, :], _head_rows(dot_ref[:, lanes].astype(BF16), hh))

        def softmax_bwd(u):
            _, hh, case, _, lanes, hrows = units[u]
            exps, l = _na_softmax(qk.pop(u), bt_ref, hh, case)
            inv_l = 1.0 / l
            delta = jnp.sum(dot_ref[hrows, lanes] * ot_ref[hrows, lanes], axis=0, keepdims=True)
            d_p = dp.pop(u)
            ds_blocks, p_blocks = [], []
            for blk, e in zip(_na_blocks(), exps):
                p = e * inv_l
                ds = p * (d_p[blk, :] - delta)
                dbt_ref[hh, case, blk, :] += ds
                ds_blocks.append(ds.astype(BF16))
                p_blocks.append(p.astype(BF16))
            dsb[u] = jnp.concatenate(ds_blocks, axis=0)
            pb[u] = jnp.concatenate(p_blocks, axis=0)

        def d_query(u):
            _, _, _, win, lanes, hrows = units[u]
            dq_ref[hrows, lanes] = _dot(kt_ref[hrows, win], dsb[u]) * (NA_HEAD_DIM ** -0.5)

        def d_keys_values(u):
            pp, hh, _, win, _, _ = units[u]
            if hh == 1:
                tokens = slice(pp * NA_PAIR, (pp + 1) * NA_PAIR)
                dk_ref[win, :] += _dot(jnp.concatenate([dsb.pop(u - 1), dsb.pop(u)], axis=1), _heads_block_diag(q_ref[tokens, :]))
                dv_ref[win, :] += _dot(jnp.concatenate([pb.pop(u - 1), pb.pop(u)], axis=1), _heads_block_diag(do_ref[tokens, :]))

        _na_pipeline(len(units), [scores, d_probs], softmax_bwd, [d_query, d_keys_values])

    t_tile = pl.BlockSpec((NA_PAIR, step_w), lambda h, s: (h, s))
    tile = pl.BlockSpec((step_w, NA_PAIR), lambda h, s: (s, h))
    t_full = pl.BlockSpec((NA_PAIR, L), lambda h, s: (h, 0))
    full = pl.BlockSpec((L, NA_PAIR), lambda h, s: (0, h))
    bt = pl.BlockSpec((2, NA_CASES, NA_WIN, NA_PAIR), lambda h, s: (h, 0, 0, 0))
    tok = jax.ShapeDtypeStruct((L, D_NA), F32)
    return pl.pallas_call(
        body, name="na_bwd", grid=(NA_HEADS // 2, L // step_w),
        in_specs=[t_tile, tile, t_full, full, full, bt, t_tile, t_tile, tile],
        out_specs=[t_tile, full, full, bt],
        out_shape=[jax.ShapeDtypeStruct((D_NA, L), F32), tok, tok, jax.ShapeDtypeStruct(bias_tab.shape, F32)],
        compiler_params=_cparams(("arbitrary", "arbitrary")),
    )(q_t, q, k_t, k, v, bias_tab, out_t, d_out_t, d_out)


def _branch_fwd_values(ys, zs, yn, zn, wglu, bglu):
    g1, t = _gelu_parts(ys)
    lin = _dot(g1.astype(BF16), wglu) + bglu
    sg = _sigmoid(lin)
    ys2 = g1 * sg
    sz, szs = _silu_parts(zs)
    sn, sns = _silu_parts(zn)
    return g1, t, sg, ys2, sz, szs, sn, sns


def _branch_fwd(y_ssm_c, z_s, y_na_t, z_n, w_glu, b_glu, tm=512):
    L = z_s.shape[0]

    def body(ys_ref, zs_ref, yn_ref, zn_ref, w_ref, b_ref, cat_ref, scr):
        yn = yn_ref[...].T
        g1, t, sg, ys2, sz, szs, sn, sns = _branch_fwd_values(
            _load_chunks(ys_ref, scr), zs_ref[...], yn, zn_ref[...], w_ref[...], b_ref[...])
        cat_ref[:, 0:512] = (ys2 * sz).astype(BF16)
        cat_ref[:, 512:1024] = (yn * sn).astype(BF16)

    tile = pl.BlockSpec((tm, 512), lambda i: (i, 0))
    return pl.pallas_call(
        body, name="branch_fwd", grid=(L // tm,),
        in_specs=[_chunk_spec(tm), tile, _heads_t_spec(tm), tile, pl.BlockSpec((512, 512), lambda i: (0, 0)),
                  pl.BlockSpec((1, 512), lambda i: (0, 0))],
        out_specs=pl.BlockSpec((tm, 1024), lambda i: (i, 0)),
        out_shape=jax.ShapeDtypeStruct((L, 1024), BF16),
        scratch_shapes=[_chunk_scratch(tm)],
        compiler_params=_cparams(("arbitrary",)),
    )(y_ssm_c, z_s, y_na_t, z_n, w_glu, b_glu)


def _branch_bwd(y_ssm_c, z_s, y_na_t, z_n, w_glu, b_glu, d_cat, tm=512):
    L = z_s.shape[0]

    def body(ys_ref, zs_ref, yn_ref, zn_ref, w_ref, b_ref, dc_ref,
             dys_ref, dzs_ref, dynt_ref, dyn_ref, dzn_ref, dw_ref, db_ref, scr):
        @pl.when(pl.program_id(0) == 0)
        def _():
            dw_ref[...] = jnp.zeros_like(dw_ref)
            db_ref[...] = jnp.zeros_like(db_ref)

        ys, zs, yn, zn = _load_chunks(ys_ref, scr), zs_ref[...], yn_ref[...].T, zn_ref[...]
        w = w_ref[...]
        g1, t, sg, ys2, sz, szs, sn, sns = _branch_fwd_values(ys, zs, yn, zn, w, b_ref[...])
        dys3 = dc_ref[:, 0:512]
        dyn2 = dc_ref[:, 512:1024]
        dzs_ref[...] = (dys3 * ys2 * _silu_grad(zs, szs)).astype(BF16)
        dys2 = dys3 * sz
        dlin = dys2 * g1 * sg * (1.0 - sg)
        dlb = dlin.astype(BF16)
        db_ref[...] += jnp.sum(dlin, axis=0, keepdims=True)
        dw_ref[...] += _dot_tn(g1.astype(BF16), dlb)
        dg1 = dys2 * sg + _dot_nt(dlb, w)
        _store_chunks(dg1 * _gelu_grad(ys, t), scr, dys_ref, BF16)
        dyn = dyn2 * sn
        dynt_ref[...] = dyn.T
        dyn_ref[...] = dyn.astype(BF16)
        dzn_ref[...] = (dyn2 * yn * _silu_grad(zn, sns)).astype(BF16)

    tile = pl.BlockSpec((tm, 512), lambda i: (i, 0))
    wspec = pl.BlockSpec((512, 512), lambda i: (0, 0))
    bspec = pl.BlockSpec((1, 512), lambda i: (0, 0))
    tok = jax.ShapeDtypeStruct((L, 512), BF16)
    return pl.pallas_call(
        body, name="branch_bwd", grid=(L // tm,),
        in_specs=[_chunk_spec(tm), tile, _heads_t_spec(tm), tile, wspec, bspec, pl.BlockSpec((tm, 1024), lambda i: (i, 0))],
        out_specs=[_chunk_spec(tm), tile, _heads_t_spec(tm), tile, tile, wspec, bspec],
        out_shape=[jax.ShapeDtypeStruct((N_BLOCKS, L // CHUNK, CHUNK_W), BF16), tok, jax.ShapeDtypeStruct((D_NA, L), F32),
                   tok, tok,
                   jax.ShapeDtypeStruct((512, 512), F32), jax.ShapeDtypeStruct((1, 512), F32)],
        scratch_shapes=[_chunk_scratch(tm)],
        compiler_params=_cparams(("arbitrary",)),
    )(y_ssm_c, z_s, y_na_t, z_n, w_glu, b_glu, d_cat)


def _head(x, p, target, cat, w_out, g_post, w_ple_g, g_ple, w_pg, tm=512):
    L = x.shape[0]
    pw = w_ple_g.shape[2]

    def body(x_ref, p_ref, t_ref, cat_ref, wo_ref, gpo_ref, wp_ref, gpl_ref, wg_ref,
             loss_ref, dh1_ref, dcat_ref, dwo_ref, dgpo_ref, dwp_ref, dgpl_ref, dwg_ref):
        @pl.when(pl.program_id(0) == 0)
        def _():
            for r in (loss_ref, dwo_ref, dgpo_ref, dwp_ref, dgpl_ref, dwg_ref):
                r[...] = jnp.zeros_like(r)

        cat_b = cat_ref[...]
        wo, wg = wo_ref[...], wg_ref[...]
        g_po, g_pl = gpo_ref[...], gpl_ref[...]
        mix = _dot(cat_b, wo)
        nm, r2 = _rms(mix)
        h1 = x_ref[...] + nm * g_po
        p_b = p_ref[...].astype(BF16)
        ep = jnp.concatenate([_dot(p_b, wp_ref[j]) for j in range(N_CHIPS)], axis=1)
        ne, r3 = _rms(ep)
        e = ne * g_pl
        h1_b = h1.astype(BF16)
        gate = _sigmoid(_dot(h1_b, wg))
        h2 = h1 + gate * e
        diff = h2 - t_ref[...]
        loss_ref[...] += (0.5 / D_MODEL) * jnp.sum(diff * diff).reshape(1, 1)

        dh2 = diff * (1.0 / D_MODEL)
        de = dh2 * gate
        dgl = (dh2 * e * gate * (1.0 - gate)).astype(BF16)
        dwg_ref[...] += _dot_tn(h1_b, dgl)
        dh1 = dh2 + _dot_nt(dgl, wg)
        dgpl_ref[...] += jnp.sum(de * ne, axis=0, keepdims=True)
        dep = _rms_bwd(de * g_pl, ne, r3).astype(BF16)
        for j in range(N_CHIPS):
            dwp_ref[j] += _dot_tn(p_b, dep[:, j * pw:(j + 1) * pw])
        dgpo_ref[...] += jnp.sum(dh1 * nm, axis=0, keepdims=True)
        dmix = _rms_bwd(dh1 * g_po, nm, r2).astype(BF16)
        dwo_ref[...] += _dot_tn(cat_b, dmix)
        dcat_ref[...] = _dot_nt(dmix, wo)
        dh1_ref[...] = dh1

    tile = lambda w: pl.BlockSpec((tm, w), lambda i: (i, 0))
    const = _resident
    sds = jax.ShapeDtypeStruct
    return pl.pallas_call(
        body, name="head", grid=(L // tm,),
        in_specs=[tile(D_MODEL), tile(D_PLE), tile(D_MODEL), tile(1024), const(1024, D_MODEL), const(1, D_MODEL),
                  const(N_CHIPS, D_PLE, pw), const(1, D_MODEL), const(D_MODEL, D_MODEL)],
        out_specs=[const(1, 1), tile(D_MODEL), tile(1024), const(1024, D_MODEL), const(1, D_MODEL),
                   const(N_CHIPS, D_PLE, pw), const(1, D_MODEL), const(D_MODEL, D_MODEL)],
        out_shape=[sds((1, 1), F32), sds((L, D_MODEL), F32), sds((L, 1024), F32), sds((1024, D_MODEL), F32),
                   sds((1, D_MODEL), F32), sds((N_CHIPS, D_PLE, pw), F32), sds((1, D_MODEL), F32),
                   sds((D_MODEL, D_MODEL), F32)],
        compiler_params=_cparams(("arbitrary",)),
    )(x, p, target, cat, w_out, g_post, w_ple_g, g_ple, w_pg)


def _dproj_specs(tm):
    tile = pl.BlockSpec((tm, 512), lambda i: (i, 0))
    return [_chunk_spec(tm), tile, _heads_t_spec(tm), tile, tile, tile]


def _dproj_tile(refs, scr):
    du_ref, dzs_ref, dqt_ref, dk_ref, dv_ref, dzn_ref = refs
    parts = [_load_chunks(du_ref, scr), dzs_ref[...], dqt_ref[...].T, dk_ref[...], dv_ref[...], dzn_ref[...]]
    return jnp.concatenate([t.astype(BF16) for t in parts], axis=1)


def _in_proj_bwd_w(x, g_col, w_in_g, dparts, tm=512):
    L = x.shape[0]
    wn = D_IN_PROJ // N_CHIPS
    steps = L // tm

    def body(x_ref, g_ref, w_ref, *refs):
        dw_ref, dg_ref, scr = refs[-3], refs[-2], refs[-1]

        @pl.when(pl.program_id(0) == 0)
        def _():
            dw_ref[...] = jnp.zeros_like(dw_ref)

        n, _ = _rms(x_ref[...])
        nb = n.astype(BF16)
        dproj = _dproj_tile(refs[:-3], scr)
        for j in range(N_CHIPS):
            dw_ref[j] += _dot_tn(nb, dproj[:, j * wn:(j + 1) * wn])

        @pl.when(pl.program_id(0) == steps - 1)
        def _():
            g = g_ref[...]
            dg = jnp.zeros_like(g)
            for j in range(N_CHIPS):
                a = dw_ref[j]
                dg = dg + jnp.sum(a * w_ref[j].astype(F32), axis=1, keepdims=True)
                dw_ref[j] = a * g
            dg_ref[...] = dg

    return pl.pallas_call(
        body, name="in_proj_bwd_w", grid=(steps,),
        in_specs=[pl.BlockSpec((tm, D_MODEL), lambda i: (i, 0)), _resident(D_MODEL, 1), _resident(N_CHIPS, D_MODEL, wn)]
        + _dproj_specs(tm),
        out_specs=[_resident(N_CHIPS, D_MODEL, wn), _resident(D_MODEL, 1)],
        out_shape=[jax.ShapeDtypeStruct((N_CHIPS, D_MODEL, wn), F32), jax.ShapeDtypeStruct((D_MODEL, 1), F32)],
        scratch_shapes=[_chunk_scratch(tm)],
        compiler_params=_cparams(("arbitrary",)),
    )(x, g_col, w_in_g, *dparts)


def _in_proj_bwd_x(x, g_pre, w_in_g, d_h1, dparts, pair_sums, tm=512):
    L = x.shape[0]
    wn = w_in_g.shape[2]
    n_ps = len(pair_sums)
    steps = L // tm

    def body(*refs):
        x_ref, g_ref, w_ref, dh1_ref = refs[:4]
        dparts_refs = refs[4:10]
        dx_ref = refs[10 + n_ps]
        scr = refs[11 + 2 * n_ps]
        scatter = _ChipScatter(refs[10:10 + n_ps], refs[11 + n_ps:11 + 2 * n_ps], refs[12 + 2 * n_ps:16 + 2 * n_ps],
                               refs[16 + 2 * n_ps:])
        pl.when(pl.program_id(0) == 0)(scatter.start)
        pl.when(pl.program_id(0) == steps - 1)(scatter.finish)

        n, r = _rms(x_ref[...])
        dproj = _dproj_tile(dparts_refs, scr)
        dhn = _dot_nt(dproj[:, 0:wn], w_ref[0])
        for j in range(1, N_CHIPS):
            dhn = dhn + _dot_nt(dproj[:, j * wn:(j + 1) * wn], w_ref[j])
        dx_ref[...] = dh1_ref[...] + _rms_bwd(dhn * g_ref[...], n, r)

    wide = pl.BlockSpec((tm, D_MODEL), lambda i: (i, 0))
    outs = pl.pallas_call(
        body, name="in_proj_bwd_x", grid=(steps,),
        in_specs=[wide, _resident(1, D_MODEL), _resident(N_CHIPS, D_MODEL, wn), wide] + _dproj_specs(tm) + _hbm_specs(n_ps),
        out_specs=[wide] + _hbm_specs(n_ps),
        out_shape=[jax.ShapeDtypeStruct((L, D_MODEL), F32)] + [jax.ShapeDtypeStruct(p.shape, p.dtype) for p in pair_sums],
        scratch_shapes=[_chunk_scratch(tm)] + _scatter_scratch(pair_sums),
        compiler_params=_cparams(("arbitrary",), has_side_effects=True),
    )(x, g_pre, w_in_g, d_h1, *dparts, *pair_sums)
    return outs[0], outs[1:]


def _mesh_position():
    x, y, c = lax.axis_index("x"), lax.axis_index("y"), lax.axis_index("c")
    chips = [(1 - x, y), (x, 1 - y), (1 - x, 1 - y)]
    return x, y, c, chips


def _chip_index(cx, cy):
    return 2 * cx + cy


def _hbm_specs(n):
    return [pl.BlockSpec(memory_space=pl.ANY)] * n


def _gather_chips(shards, name):
    n = len(shards)

    def body(*refs):
        gather = _ChipGather(refs[:n], refs[n:2 * n], refs[2 * n:])
        gather.start()
        gather.forward()
        gather.finish()

    return pl.pallas_call(
        body, name=name, in_specs=_hbm_specs(n), out_specs=_hbm_specs(n),
        out_shape=_gather_out_shapes(shards), scratch_shapes=_gather_semaphores(n),
        compiler_params=pltpu.CompilerParams(has_side_effects=True),
    )(*shards)


def _gather_out_shapes(shards):
    return [jax.ShapeDtypeStruct((N_CHIPS,) + s.shape, s.dtype) for s in shards]


def _gather_semaphores(n):
    sem = pltpu.SemaphoreType.DMA
    return [sem((n, 3)), sem((n, 3)), sem((n, 3)), sem((n, 3)), sem((n,)), sem((n,))]


class _ChipGather:
    def __init__(self, ins, outs, sems):
        self.ins, self.outs = ins, outs
        self.send1, self.recv1, self.send2, self.recv2, self.send3, self.recv3 = sems
        self.x, self.y, self.c, self.chips = _mesh_position()
        self.me = _chip_index(self.x, self.y)
        self.sibling = (self.x, self.y, 1 - self.c)

    def _half(self, a, chip, core):
        hr = self.outs[a].shape[1] // 2
        return self.outs[a].at[chip, pl.ds(core * hr, hr)]

    def _own(self, a):
        return pltpu.make_async_remote_copy(
            src_ref=self.ins[a], dst_ref=self.outs[a].at[self.me], send_sem=self.send3.at[a], recv_sem=self.recv3.at[a],
            device_id=self.sibling, device_id_type=MESH)

    def _to_chip(self, a, j):
        hr = self.ins[a].shape[0] // 2
        return pltpu.make_async_remote_copy(
            src_ref=self.ins[a].at[pl.ds(self.c * hr, hr)], dst_ref=self._half(a, self.me, self.c),
            send_sem=self.send1.at[a, j], recv_sem=self.recv1.at[a, j], device_id=(*self.chips[j], self.c), device_id_type=MESH)

    def _from_chip(self, a, j):
        landed = self._half(a, _chip_index(*self.chips[j]), self.c)
        return pltpu.make_async_remote_copy(
            src_ref=landed, dst_ref=landed, send_sem=self.send1.at[a, j], recv_sem=self.recv1.at[a, j],
            device_id=(*self.chips[j], self.c), device_id_type=MESH)

    def _to_sibling(self, a, j, core):
        part = self._half(a, _chip_index(*self.chips[j]), core)
        return pltpu.make_async_remote_copy(
            src_ref=part, dst_ref=part, send_sem=self.send2.at[a, j], recv_sem=self.recv2.at[a, j],
            device_id=self.sibling, device_id_type=MESH)

    def _each(self):
        return [(a, j) for a in range(len(self.ins)) for j in range(3)]

    def start(self):
        for a in range(len(self.ins)):
            self._own(a).start()
        for a, j in self._each():
            self._to_chip(a, j).start()

    def forward(self):
        for a, j in self._each():
            self._from_chip(a, j).wait_recv()
            self._to_sibling(a, j, self.c).start()

    def finish(self):
        for a, j in self._each():
            self._to_sibling(a, j, 1 - self.c).wait_recv()
        for a, j in self._each():
            self._to_chip(a, j).wait_send()
            self._to_sibling(a, j, self.c).wait_send()
        for a in range(len(self.ins)):
            self._own(a).wait()


def _pair_exchange(grads):
    n = len(grads)

    def body(*refs):
        ins, outs = refs[:n], refs[n:2 * n]
        send, recv = refs[2 * n:]
        x, y, c, _ = _mesh_position()
        copies = []
        for a in range(n):
            hr = ins[a].shape[1] // 2
            cp = pltpu.make_async_remote_copy(
                src_ref=ins[a].at[:, pl.ds((1 - c) * hr, hr)], dst_ref=outs[a],
                send_sem=send.at[a], recv_sem=recv.at[a], device_id=(x, y, 1 - c), device_id_type=MESH)
            cp.start()
            copies.append(cp)
        for cp in copies:
            cp.wait()

    sem = pltpu.SemaphoreType.DMA
    return pl.pallas_call(
        body, name="pair_exchange", in_specs=_hbm_specs(n), out_specs=_hbm_specs(n),
        out_shape=[jax.ShapeDtypeStruct((g.shape[0], g.shape[1] // 2, g.shape[2]), g.dtype) for g in grads],
        scratch_shapes=[sem((n,)), sem((n,))],
        compiler_params=pltpu.CompilerParams(has_side_effects=True),
    )(*grads)


def _pair_add(core, grad, other, tr, out_dtype):
    hr = other.shape[1]
    cdim = other.shape[2]
    nb = hr // tr

    def body(core_ref, g_ref, o_ref, out_ref):
        out_ref[...] = (g_ref[...] + o_ref[...]).astype(out_dtype)

    return pl.pallas_call(
        body, name="pair_add",
        grid_spec=pltpu.PrefetchScalarGridSpec(
            num_scalar_prefetch=1, grid=(N_CHIPS, nb),
            in_specs=[pl.BlockSpec((1, tr, cdim), lambda j, i, core_ref: (j, core_ref[0] * nb + i, 0)),
                      pl.BlockSpec((1, tr, cdim), lambda j, i, core_ref: (j, i, 0))],
            out_specs=pl.BlockSpec((1, tr, cdim), lambda j, i, core_ref: (j, i, 0))),
        out_shape=jax.ShapeDtypeStruct(other.shape, out_dtype),
        compiler_params=_cparams(("arbitrary", "arbitrary")),
    )(core, grad, other)


def _scatter_scratch(parts):
    sem = pltpu.SemaphoreType.DMA
    n = len(parts)
    return [sem((n, 3)), sem((n, 3)), sem((n,)), sem((n,))] + [pltpu.VMEM(p.shape[1:], p.dtype) for p in parts]


class _ChipScatter:
    def __init__(self, ins, outs, sems, staged):
        self.ins, self.outs, self.staged = ins, outs, staged
        self.send, self.recv, self.load_sem, self.store_sem = sems
        self.x, self.y, self.c, self.chips = _mesh_position()
        self.me = _chip_index(self.x, self.y)

    def _load(self, a):
        return pltpu.make_async_copy(self.ins[a].at[self.me], self.staged[a], self.load_sem.at[a])

    def _store(self, a):
        return pltpu.make_async_copy(self.staged[a], self.outs[a].at[self.me], self.store_sem.at[a])

    def _to_chip(self, a, j):
        return pltpu.make_async_remote_copy(
            src_ref=self.ins[a].at[_chip_index(*self.chips[j])], dst_ref=self.outs[a].at[self.me],
            send_sem=self.send.at[a, j], recv_sem=self.recv.at[a, j], device_id=(*self.chips[j], self.c), device_id_type=MESH)

    def start(self):
        for a in range(len(self.ins)):
            self._load(a).start()
            for j in range(3):
                self._to_chip(a, j).start()

    def finish(self):
        for a in range(len(self.ins)):
            self._load(a).wait()
            self._store(a).start()
        for a in range(len(self.ins)):
            for j in range(3):
                self._to_chip(a, j).wait()
            self._store(a).wait()


def _chip_add(core, recv, tr):
    hr, cdim = recv.shape[1], recv.shape[2]
    nb = hr // tr

    def body(core_ref, r_ref, out_ref):
        out_ref[...] = ((r_ref[0].astype(F32) + r_ref[1].astype(F32)) + r_ref[2].astype(F32)) + r_ref[3].astype(F32)

    return pl.pallas_call(
        body, name="chip_add",
        grid_spec=pltpu.PrefetchScalarGridSpec(
            num_scalar_prefetch=1, grid=(nb,),
            in_specs=[pl.BlockSpec((N_CHIPS, tr, cdim), lambda i, core_ref: (0, i, 0))],
            out_specs=pl.BlockSpec((tr, cdim), lambda i, core_ref: (core_ref[0] * nb + i, 0))),
        out_shape=jax.ShapeDtypeStruct((2 * hr, cdim), F32),
        compiler_params=_cparams(("arbitrary",)),
    )(core, recv)


def _pair_gather(fulls):
    n = len(fulls)

    def body(*refs):
        outs = refs[n:2 * n]
        send, recv = refs[2 * n:]
        x, y, c, _ = _mesh_position()
        copies = []
        for a in range(n):
            hr = outs[a].shape[0] // 2
            mine = outs[a].at[pl.ds(c * hr, hr)]
            cp = pltpu.make_async_remote_copy(
                src_ref=mine, dst_ref=mine, send_sem=send.at[a], recv_sem=recv.at[a],
                device_id=(x, y, 1 - c), device_id_type=MESH)
            cp.start()
            copies.append(cp)
        for cp in copies:
            cp.wait()

    sem = pltpu.SemaphoreType.DMA
    return pl.pallas_call(
        body, name="pair_gather", in_specs=_hbm_specs(n), out_specs=_hbm_specs(n),
        out_shape=[jax.ShapeDtypeStruct(f.shape, f.dtype) for f in fulls],
        input_output_aliases={a: a for a in range(n)},
        scratch_shapes=[sem((n,)), sem((n,))],
        compiler_params=pltpu.CompilerParams(has_side_effects=True),
    )(*fulls)


def _row_tile(rows):
    for t in (512, 256, 128, 64, 32, 16, 8):
        if rows % t == 0:
            return t
    raise ValueError(rows)


def _pair_sums(core, grads, ici_dtypes):
    others = _pair_exchange(grads)
    return [_pair_add(core, g, o, _row_tile(o.shape[1]), dt) for g, o, dt in zip(grads, others, ici_dtypes)]


def _finish_reduce(core, landed):
    return _pair_gather([_chip_add(core, r, _row_tile(r.shape[1])) for r in landed])


def _adamw(w, g, m, v):
    rows, cols = w.shape
    one_block = rows % 8 != 0 or rows * max(cols, 128) * 4 <= (1 << 20)
    tr = rows if one_block else _row_tile(rows)

    def body(w_ref, g_ref, m_ref, v_ref, d_ref, nm_ref, nv_ref):
        g_ = g_ref[...]
        m_ = ADAM_B1 * m_ref[...] + (1.0 - ADAM_B1) * g_
        v_ = ADAM_B2 * v_ref[...] + (1.0 - ADAM_B2) * (g_ * g_)
        m_hat = m_ / (1.0 - ADAM_B1 ** ADAM_STEP)
        v_hat = v_ / (1.0 - ADAM_B2 ** ADAM_STEP)
        d_ref[...] = -ADAM_LR * (m_hat / (jnp.sqrt(v_hat) + ADAM_EPS) + ADAM_WD * w_ref[...])
        nm_ref[...] = m_
        nv_ref[...] = v_

    spec = pl.BlockSpec((tr, cols), lambda i: (i, 0))
    shp = jax.ShapeDtypeStruct((rows, cols), F32)
    return pl.pallas_call(
        body, name="adamw", grid=(rows // tr,), in_specs=[spec] * 4, out_specs=[spec] * 3,
        out_shape=[shp] * 3, compiler_params=_cparams(("arbitrary",)),
    )(w, g, m, v)


_SMALL = ["norm_pre", "norm_post", "ssm_a_re", "ssm_a_im", "ssm_log_dt", "ssm_b_re", "ssm_b_im",
          "ssm_c_re", "ssm_c_im", "ssm_d", "b_glu", "na_rpb", "ple_norm"]
_BIG = ["w_in", "w_glu", "w_out", "w_ple", "w_ple_gate"]
_WEIGHTS = ["norm_pre", "norm_post", "w_in", "ssm_a_re", "ssm_a_im", "ssm_log_dt", "ssm_b_re", "ssm_b_im",
            "ssm_c_re", "ssm_c_im", "ssm_d", "w_glu", "b_glu", "na_rpb", "w_out", "w_ple", "ple_norm", "w_ple_gate"]
_SMALL_ROWS = 2176


def _pack_small(tensors, tail=None):
    parts = [tensors[n].reshape(-1) for n in _SMALL] + ([] if tail is None else [tail.reshape(-1)])
    flat = jnp.concatenate(parts)
    flat = jnp.pad(flat, (0, _SMALL_ROWS * 128 - flat.shape[0]))
    return flat.reshape(_SMALL_ROWS, 128)


def _unpack_small(packed, shapes):
    flat = packed.reshape(-1)
    out, off = {}, 0
    for n in _SMALL:
        size = int(np.prod(shapes[n]))
        out[n] = flat[off:off + size].reshape(shapes[n])
        off += size
    return out


def _local_grads(x, p, target, wts):
    ssm_names = ["ssm_a_re", "ssm_a_im", "ssm_log_dt", "ssm_b_re", "ssm_b_im", "ssm_c_re", "ssm_c_im", "ssm_d"]
    ssm_params = [wts[n][0] for n in ssm_names]
    blk, blk_vjp = jax.vjp(_ssm_block_params, *ssm_params)
    shard = lambda n: wts[n][0].astype(BF16)
    (m_mat, ws_mat, wot_mat, a16), (w_in_g,) = _ssm_chunk_matrices(blk, [shard("w_in")])
    seq = x.shape[0]
    bias_rows, bias_rows_vjp = jax.vjp(_na_bias_rows, wts["na_rpb"][0])
    bias_tab = _na_bias_table(bias_rows, seq // GRID_W)

    (u_c, z_s, q_t, q, k_t, k, v_t, v, z_n), gathered = _in_proj(
        x, wts["norm_pre"], w_in_g, [shard(n) for n in _BIG if n != "w_in"])
    w_glu, w_out, w_ple_g, w_pg = (gathered[0].reshape(512, 512), gathered[1].reshape(1024, 1024), gathered[2],
                                   gathered[3].reshape(1024, 1024))
    s_in = _block_matmul([(u_c, ws_mat, False)], "ssm_chunk_states")
    s_prev = _ssm_state_scan(s_in, a16)
    y_ssm_c = _block_matmul([(u_c, m_mat, False), (s_prev, wot_mat, True)], "ssm_chunk_out")
    y_na_t = _na_fwd(q_t, k, v_t, bias_tab)
    cat = _branch_fwd(y_ssm_c, z_s, y_na_t, z_n, w_glu, wts["b_glu"])

    (loss, d_h1, d_cat, d_w_out, d_g_post, d_w_ple, d_g_ple, d_w_pg) = _head(
        x, p, target, cat, w_out, wts["norm_post"], w_ple_g, wts["ple_norm"], w_pg)
    dy_c, d_z_s, d_y_na_t, d_y_na, d_z_n, d_w_glu, d_b_glu = _branch_bwd(
        y_ssm_c, z_s, y_na_t, z_n, w_glu, wts["b_glu"], d_cat)
    d_q_t, d_k, d_v, d_bias_tab = _na_bwd(q_t, q, k_t, k, v, bias_tab, y_na_t, d_y_na_t, d_y_na)

    d_prev = _block_matmul([(dy_c, wot_mat, False)], "ssm_bwd_states")
    g_st, d_a16 = _ssm_state_scan_bwd(d_prev, s_prev, a16)
    d_u_c = _block_matmul([(dy_c, m_mat, True), (g_st, ws_mat, True)], "ssm_bwd_in", out_dtype=BF16)
    d_m = _block_matmul_tn(u_c, dy_c, "ssm_grad_m")
    d_ws = _block_matmul_tn(u_c, g_st, "ssm_grad_ws")
    d_wot = _block_matmul_tn(dy_c, s_prev, "ssm_grad_wot")
    d_ssm = blk_vjp(tuple(_ssm_chunk_matrices_bwd(blk, d_m, d_ws, d_wot, d_a16)))
    (d_rpb,) = bias_rows_vjp(_na_bias_table_bwd(d_bias_tab, seq // GRID_W))

    dparts = [d_u_c, d_z_s, d_q_t, d_k, d_v, d_z_n]
    d_w_in, d_g_pre = _in_proj_bwd_w(x, wts["norm_pre"].reshape(D_MODEL, 1), w_in_g, dparts)

    small = {"norm_pre": d_g_pre, "norm_post": d_g_post, "b_glu": d_b_glu, "na_rpb": d_rpb, "ple_norm": d_g_ple}
    for n, g in zip(ssm_names, d_ssm):
        small[n] = g
    big = {"w_in": d_w_in, "w_glu": d_w_glu.reshape(N_CHIPS, 128, 512), "w_out": d_w_out.reshape(N_CHIPS, 256, 1024),
           "w_ple": d_w_ple, "w_ple_gate": d_w_pg.reshape(N_CHIPS, 256, 1024)}
    return loss, small, big, (x, wts["norm_pre"], w_in_g, d_h1, dparts)


def kernel(x, p, norm_pre, norm_post, w_in, ssm_a_re, ssm_a_im, ssm_log_dt, ssm_b_re, ssm_b_im, ssm_c_re, ssm_c_im, ssm_d, w_glu, b_glu, na_rpb, w_out, w_ple, ple_norm, w_ple_gate, loss_target, m_norm_pre, m_norm_post, m_w_in, m_ssm_a_re, m_ssm_a_im, m_ssm_log_dt, m_ssm_b_re, m_ssm_b_im, m_ssm_c_re, m_ssm_c_im, m_ssm_d, m_w_glu, m_b_glu, m_na_rpb, m_w_out, m_w_ple, m_ple_norm, m_w_ple_gate, v_norm_pre, v_norm_post, v_w_in, v_ssm_a_re, v_ssm_a_im, v_ssm_log_dt, v_ssm_b_re, v_ssm_b_im, v_ssm_c_re, v_ssm_c_im, v_ssm_d, v_w_glu, v_b_glu, v_na_rpb, v_w_out, v_w_ple, v_ple_norm, v_w_ple_gate):
    wts = dict(norm_pre=norm_pre, norm_post=norm_post, w_in=w_in, ssm_a_re=ssm_a_re, ssm_a_im=ssm_a_im,
               ssm_log_dt=ssm_log_dt, ssm_b_re=ssm_b_re, ssm_b_im=ssm_b_im, ssm_c_re=ssm_c_re, ssm_c_im=ssm_c_im,
               ssm_d=ssm_d, w_glu=w_glu, b_glu=b_glu, na_rpb=na_rpb, w_out=w_out, w_ple=w_ple, ple_norm=ple_norm,
               w_ple_gate=w_ple_gate)
    mom_m = dict(norm_pre=m_norm_pre, norm_post=m_norm_post, w_in=m_w_in, ssm_a_re=m_ssm_a_re, ssm_a_im=m_ssm_a_im,
                 ssm_log_dt=m_ssm_log_dt, ssm_b_re=m_ssm_b_re, ssm_b_im=m_ssm_b_im, ssm_c_re=m_ssm_c_re,
                 ssm_c_im=m_ssm_c_im, ssm_d=m_ssm_d, w_glu=m_w_glu, b_glu=m_b_glu, na_rpb=m_na_rpb, w_out=m_w_out,
                 w_ple=m_w_ple, ple_norm=m_ple_norm, w_ple_gate=m_w_ple_gate)
    mom_v = dict(norm_pre=v_norm_pre, norm_post=v_norm_post, w_in=v_w_in, ssm_a_re=v_ssm_a_re, ssm_a_im=v_ssm_a_im,
                 ssm_log_dt=v_ssm_log_dt, ssm_b_re=v_ssm_b_re, ssm_b_im=v_ssm_b_im, ssm_c_re=v_ssm_c_re,
                 ssm_c_im=v_ssm_c_im, ssm_d=v_ssm_d, w_glu=v_w_glu, b_glu=v_b_glu, na_rpb=v_na_rpb, w_out=v_w_out,
                 w_ple=v_w_ple, ple_norm=v_ple_norm, w_ple_gate=v_w_ple_gate)

    loss_part, small, big, input_grad_args = _local_grads(x[0], p[0, 0], loss_target[0], wts)

    core = lax.axis_index("c").astype(jnp.int32).reshape(1)
    small_packed = _pack_small(small, tail=loss_part).reshape(N_CHIPS, _SMALL_ROWS // N_CHIPS, 128)
    pair = _pair_sums(core, [big[n] for n in _BIG] + [small_packed], [BF16] * len(_BIG) + [F32])
    grad_x, landed = _in_proj_bwd_x(*input_grad_args, pair)
    reduced = _finish_reduce(core, landed)
    grads = dict(zip(_BIG, reduced[:-1]))
    (small_all,) = _gather_chips([reduced[-1]], "gather_small_grads")
    small_all = small_all.reshape(_SMALL_ROWS, 128)
    loss = small_all.reshape(-1)[sum(int(np.prod(wts[n].shape)) for n in _SMALL)]

    delta, new_m, new_v = {}, {}, {}
    for n in _BIG:
        shp = wts[n].shape
        d_, m_, v_ = _adamw(wts[n][0], grads[n], mom_m[n][0], mom_v[n][0])
        grads[n] = grads[n].reshape(shp)
        delta[n], new_m[n], new_v[n] = d_.reshape(shp), m_.reshape(shp), v_.reshape(shp)
    grads.update(_unpack_small(small_all, {n: wts[n].shape for n in _SMALL}))
    for n in _SMALL:
        shp = wts[n].shape
        rows_cols = (int(np.prod(shp[:-1])), shp[-1])
        d_, m_, v_ = _adamw(*[t.reshape(rows_cols) for t in (wts[n], grads[n], mom_m[n], mom_v[n])])
        delta[n], new_m[n], new_v[n] = d_.reshape(shp), m_.reshape(shp), v_.reshape(shp)

    return (loss, grad_x[None], *[grads[n] for n in _WEIGHTS], *[delta[n] for n in _WEIGHTS],
            *[new_m[n] for n in _WEIGHTS], *[new_v[n] for n in _WEIGHTS])
```

```python
import math

import jax
import jax.numpy as jnp
import numpy as np
from jax import lax
from jax.experimental import pallas as pl
from jax.experimental.pallas import tpu as pltpu

F32 = jnp.float32
BF16 = jnp.bfloat16

D_MODEL = 1024
D_PLE = 256
GRID_W = 64
D_SSM = 512
SSM_GROUP = 16
N_GROUPS = 32
SSM_STATE = 64
D_NA = 512
NA_HEADS = 8
NA_HEAD_DIM = 64
NA_ROWS = 8
NA_COLS = 16
D_IN_PROJ = 3072
EPS = 1e-6

CHUNK = 16
GROUPS_PER_BLOCK = 8
N_BLOCKS = N_GROUPS // GROUPS_PER_BLOCK
BLOCK_CH = GROUPS_PER_BLOCK * SSM_GROUP
BLOCK_ST = GROUPS_PER_BLOCK * SSM_STATE
CHUNK_W = CHUNK * BLOCK_CH
STATE_W = 4 * BLOCK_ST

N_CHIPS = 4
MESH = pl.DeviceIdType.MESH

ADAM_LR = 0.001
ADAM_B1 = 0.9
ADAM_B2 = 0.999
ADAM_EPS = 1e-08
ADAM_WD = 0.01
ADAM_STEP = 10

VMEM_LIMIT = 52 * 1024 * 1024
HIGHEST = lax.Precision.HIGHEST


def _cparams(sem=None, **kw):
    if sem is not None:
        kw["dimension_semantics"] = sem
    return pltpu.CompilerParams(vmem_limit_bytes=VMEM_LIMIT, **kw)


def _resident(*shape):
    return pl.BlockSpec(shape, lambda *_: (0,) * len(shape), pipeline_mode=pl.Buffered(1))


def _dot(a, b, dims=((1,), (0,))):
    return lax.dot_general(a, b, (dims, ((), ())), preferred_element_type=F32)


def _dot_nt(a, b):
    return _dot(a, b, ((1,), (1,)))


def _dot_tn(a, b):
    return _dot(a, b, ((0,), (0,)))


def _sigmoid(x):
    return 1.0 / (1.0 + jnp.exp(-x))


_GELU_C = math.sqrt(2.0 / math.pi)


def _gelu_parts(x):
    inner = _GELU_C * (x + 0.044715 * (x * x * x))
    t = jnp.tanh(inner)
    return 0.5 * x * (1.0 + t), t


def _gelu_grad(x, t):
    return 0.5 * (1.0 + t) + 0.5 * x * (1.0 - t * t) * (_GELU_C * (1.0 + 3.0 * 0.044715 * x * x))


def _silu_parts(z):
    s = _sigmoid(z)
    return z * s, s


def _silu_grad(z, s):
    return s * (1.0 + z * (1.0 - s))


def _rms(x):
    r = lax.rsqrt(jnp.mean(x * x, axis=-1, keepdims=True) + EPS)
    return x * r, r


def _rms_bwd(dn, n, r):
    return r * (dn - n * jnp.mean(dn * n, axis=-1, keepdims=True))


def _chunk_scratch(tm):
    return pltpu.VMEM((N_BLOCKS, tm, BLOCK_CH), F32)


def _store_chunks(val, scr, c_ref, dtype, row0=0):
    rows = val.shape[0]
    nc, c0 = rows // CHUNK, row0 // CHUNK
    for b in range(N_BLOCKS):
        scr[b, row0:row0 + rows, :] = val[:, b * BLOCK_CH:(b + 1) * BLOCK_CH]
        for j in range(CHUNK):
            c_ref[b, c0:c0 + nc, j * BLOCK_CH:(j + 1) * BLOCK_CH] = scr[b, pl.ds(row0 + j, nc, stride=CHUNK), :].astype(dtype)


def _load_chunks(c_ref, scr):
    nc = scr.shape[1] // CHUNK
    for b in range(N_BLOCKS):
        for j in range(CHUNK):
            scr[b, pl.ds(j, nc, stride=CHUNK), :] = c_ref[b, :, j * BLOCK_CH:(j + 1) * BLOCK_CH].astype(F32)
    return jnp.concatenate([scr[b] for b in range(N_BLOCKS)], axis=1)


def _chunk_spec(tm):
    return pl.BlockSpec((N_BLOCKS, tm // CHUNK, CHUNK_W), lambda i: (0, i, 0))


def _heads_t_spec(tm):
    return pl.BlockSpec((D_NA, tm), lambda i: (0, i))


def _in_proj(x, g_pre, w_in_g, shards, tm=512):
    L = x.shape[0]
    wn = w_in_g.shape[2]
    n_sh = len(shards)
    steps = L // tm

    def body(*refs):
        x_ref, g_ref, w_ref = refs[:3]
        uc_ref, zs_ref, qt_ref, q_ref, kt_ref, k_ref, vt_ref, v_ref, zn_ref = refs[3 + n_sh:12 + n_sh]
        u_scr = refs[12 + 2 * n_sh]
        gather = _ChipGather(refs[3:3 + n_sh], refs[12 + n_sh:12 + 2 * n_sh], refs[13 + 2 * n_sh:])
        step = pl.program_id(0)
        pl.when(step == 0)(gather.start)
        pl.when(step == steps // 2)(gather.forward)
        pl.when(step == steps - 1)(gather.finish)
        halves = [slice(0, tm // 2), slice(tm // 2, tm)]
        hn = [(_rms(x_ref[rows, :])[0] * g_ref[...]).astype(BF16) for rows in halves]
        projs = [jnp.concatenate([_dot(h, w_ref[j]) for j in range(N_CHIPS)], axis=1) for h in hn]
        for rows, proj in zip(halves, projs):
            _store_chunks(proj[:, 0:512], u_scr, uc_ref, BF16, row0=rows.start)
            zs_ref[rows, :] = proj[:, 512:1024]
            q = proj[:, 1024:1536] * (NA_HEAD_DIM ** -0.5)
            for val, t_ref, n_ref in ((q, qt_ref, q_ref), (proj[:, 1536:2048], kt_ref, k_ref), (proj[:, 2048:2560], vt_ref, v_ref)):
                t_ref[:, rows] = val.T.astype(BF16)
                n_ref[rows, :] = val.astype(BF16)
            zn_ref[rows, :] = proj[:, 2560:3072]

    tok = jax.ShapeDtypeStruct((L, 512), F32)
    tr = jax.ShapeDtypeStruct((D_NA, L), BF16)
    hm = jax.ShapeDtypeStruct((L, D_NA), BF16)
    tspec = pl.BlockSpec((tm, 512), lambda i: (i, 0))
    outs = pl.pallas_call(
        body, name="in_proj", grid=(steps,),
        in_specs=[pl.BlockSpec((tm, D_MODEL), lambda i: (i, 0)),
                  _resident(1, D_MODEL), _resident(N_CHIPS, D_MODEL, wn)] + _hbm_specs(n_sh),
        out_specs=[_chunk_spec(tm), tspec] + [_heads_t_spec(tm), tspec] * 3 + [tspec] + _hbm_specs(n_sh),
        out_shape=[jax.ShapeDtypeStruct((N_BLOCKS, L // CHUNK, CHUNK_W), BF16), tok, tr, hm, tr, hm, tr, hm, tok]
        + _gather_out_shapes(shards),
        scratch_shapes=[_chunk_scratch(tm)] + _gather_semaphores(n_sh),
        compiler_params=_cparams(("arbitrary",), has_side_effects=True),
    )(x, g_pre, w_in_g, *shards)
    return outs[:9], outs[9:]


def _ssm_block_params(a_re, a_im, log_dt, b_re, b_im, c_re, c_im, d):
    def lanes(t):
        return t.reshape(2, N_BLOCKS, 1, BLOCK_ST)

    rows = (2, N_BLOCKS, BLOCK_CH, SSM_STATE)
    b_rows = lambda t: t.reshape(2, N_BLOCKS, GROUPS_PER_BLOCK, SSM_STATE, SSM_GROUP).transpose(0, 1, 2, 4, 3).reshape(rows)
    return (lanes(a_re), lanes(a_im), lanes(jnp.broadcast_to(log_dt[..., None], a_re.shape)),
            b_rows(b_re), b_rows(b_im), c_re.reshape(rows), c_im.reshape(rows), d.reshape(N_BLOCKS, 1, BLOCK_CH))


def _ssm_group_mask():
    row_g = lax.broadcasted_iota(jnp.int32, (BLOCK_CH, BLOCK_ST), 0) // SSM_GROUP
    lane_g = lax.broadcasted_iota(jnp.int32, (BLOCK_CH, BLOCK_ST), 1) // SSM_STATE
    return row_g == lane_g


def _ssm_state_select():
    p = lax.broadcasted_iota(jnp.int32, (SSM_STATE, BLOCK_ST), 0)
    lane_p = lax.broadcasted_iota(jnp.int32, (SSM_STATE, BLOCK_ST), 1) % SSM_STATE
    return (p == lane_p).astype(F32)


def _ssm_expand_blocks(compact_refs, full_refs):
    mask, select = _ssm_group_mask(), _ssm_state_select()
    for c_ref, f_ref in zip(compact_refs, full_refs):
        for d in range(2):
            tiled = lax.dot_general(c_ref[d, 0], select, ((((1,), (0,))), ((), ())), precision=HIGHEST,
                                    preferred_element_type=F32)
            f_ref[d, 0] = jnp.where(mask, tiled, 0.0)


def _ssm_collapse_block(t):
    return lax.dot_general(jnp.where(_ssm_group_mask(), t, 0.0), _ssm_state_select(), ((((1,), (1,))), ((), ())),
                           precision=HIGHEST, preferred_element_type=F32)


def _ssm_discretise(ar, ai, ldt):
    dt = jnp.exp(ldt)
    mag = jnp.exp(dt * ar)
    abr = mag * jnp.cos(dt * ai)
    abi = mag * jnp.sin(dt * ai)
    num_re = abr - 1.0
    num_im = abi
    denom = ar * ar + ai * ai
    coef_re = (num_re * ar + num_im * ai) / denom
    coef_im = (num_im * ar - num_re * ai) / denom
    return abr, abi, coef_re, coef_im


_POW_ROWS = 24


def _ssm_fill_powers(ar_ref, ai_ref, ldt_ref, br_ref, bi_ref, pw_ref, bbar_ref):
    for d in range(2):
        abr, abi, cfr, cfi = _ssm_discretise(ar_ref[d, 0], ai_ref[d, 0], ldt_ref[d, 0])
        bbar_ref[d, 0] = cfr * br_ref[d, 0] - cfi * bi_ref[d, 0]
        bbar_ref[d, 1] = cfr * bi_ref[d, 0] + cfi * br_ref[d, 0]
        pr, pi = jnp.ones_like(abr), jnp.zeros_like(abi)
        for t in range(CHUNK + 1):
            pw_ref[d, 0, t:t + 1, :] = pr
            pw_ref[d, 1, t:t + 1, :] = pi
            pr, pi = pr * abr - pi * abi, pr * abi + pi * abr


def _dot_rounded(a, b, dims=((1,), (0,))):
    return _dot(a.astype(BF16), b.astype(BF16), dims)


def _ssm_stack_inputs(d, pw_ref, bbar_ref, xs_ref):
    for t in range(CHUNK):
        pr, pi = pw_ref[d, 0, t:t + 1, :], pw_ref[d, 1, t:t + 1, :]
        xs_ref[0, t * BLOCK_CH:(t + 1) * BLOCK_CH, :] = bbar_ref[d, 0] * pr - bbar_ref[d, 1] * pi
        xs_ref[1, t * BLOCK_CH:(t + 1) * BLOCK_CH, :] = bbar_ref[d, 0] * pi + bbar_ref[d, 1] * pr


def _eye(n):
    return (lax.broadcasted_iota(jnp.int32, (n, n), 0) == lax.broadcasted_iota(jnp.int32, (n, n), 1)).astype(F32)


def _ssm_param_specs():
    vec = pl.BlockSpec((2, 1, 1, BLOCK_ST), lambda b, j: (0, b, 0, 0))
    mat = pl.BlockSpec((2, 1, BLOCK_CH, SSM_STATE), lambda b, j: (0, b, 0, 0))
    return [vec, vec, vec, mat, mat, mat, mat, pl.BlockSpec((1, 1, BLOCK_CH), lambda b, j: (b, 0, 0))]


def _ssm_block_scratch():
    return [pltpu.VMEM((2, 1, BLOCK_CH, BLOCK_ST), F32)] * 4


def _ssm_chunk_matrices(blk, shards):
    n = len(shards)

    def body(*refs):
        ar_ref, ai_ref, ldt_ref = refs[:3]
        d_ref = refs[7]
        m_ref, ws_ref, wot_ref, a16_ref = refs[8 + n:12 + n]
        pw_ref, bbar_ref, lag_ref, xs_ref = refs[12 + 2 * n:16 + 2 * n]
        br_ref, bi_ref, cr_ref, ci_ref = refs[16 + 2 * n:20 + 2 * n]
        gather = _ChipGather(refs[8:8 + n], refs[12 + n:12 + 2 * n], refs[20 + 2 * n:])
        b, j = pl.program_id(0), pl.program_id(1)
        pl.when((b == 0) & (j == 0))(gather.start)
        pl.when((b == N_BLOCKS - 1) & (j == 0))(gather.forward)
        pl.when((b == N_BLOCKS - 1) & (j == CHUNK - 1))(gather.finish)

        @pl.when(j == 0)
        def _():
            _ssm_expand_blocks(refs[3:7], (br_ref, bi_ref, cr_ref, ci_ref))
            _ssm_fill_powers(ar_ref, ai_ref, ldt_ref, br_ref, bi_ref, pw_ref, bbar_ref)
            zero_lag = d_ref[0] * _eye(BLOCK_CH)
            for d in range(2):
                _ssm_stack_inputs(d, pw_ref, bbar_ref, xs_ref)
                taps = (_dot_rounded(xs_ref[0], cr_ref[d, 0], ((1,), (1,)))
                        - _dot_rounded(xs_ref[1], ci_ref[d, 0], ((1,), (1,))))
                zero_lag = zero_lag + taps[0:BLOCK_CH]
                for t in range(1, CHUNK):
                    lag_ref[CHUNK - 1 + t if d == 0 else CHUNK - 1 - t] = taps[t * BLOCK_CH:(t + 1) * BLOCK_CH]
            lag_ref[CHUNK - 1] = zero_lag
            a16_ref[0] = jnp.concatenate([pw_ref[d, ri, CHUNK:CHUNK + 1, :] for d in range(2) for ri in range(2)], axis=1)

        m_ref[0] = jnp.concatenate([lag_ref[jp - j + CHUNK - 1] for jp in range(CHUNK)], axis=1).astype(BF16)

        def power(d, t):
            return pw_ref[d, 0, pl.ds(t, 1), :], pw_ref[d, 1, pl.ds(t, 1), :]

        parts = []
        for d, t in ((0, CHUNK - 1 - j), (1, j)):
            pr, pi = power(d, t)
            parts += [bbar_ref[d, 0] * pr - bbar_ref[d, 1] * pi, bbar_ref[d, 0] * pi + bbar_ref[d, 1] * pr]
        ws_ref[0] = jnp.concatenate(parts, axis=1).astype(BF16)
        parts = []
        for d, t in ((0, j + 1), (1, CHUNK - j)):
            pr, pi = power(d, t)
            parts += [cr_ref[d, 0] * pr - ci_ref[d, 0] * pi, -cr_ref[d, 0] * pi - ci_ref[d, 0] * pr]
        wot_ref[0] = jnp.concatenate(parts, axis=1).astype(BF16)

    row = pl.BlockSpec((1, BLOCK_CH, CHUNK_W), lambda b, j: (b, j, 0))
    mat = jax.ShapeDtypeStruct((N_BLOCKS, CHUNK_W, CHUNK_W), BF16)
    outs = pl.pallas_call(
        body, name="ssm_chunk_matrices", grid=(N_BLOCKS, CHUNK),
        in_specs=_ssm_param_specs() + _hbm_specs(n),
        out_specs=[row, row, row, pl.BlockSpec((1, 1, STATE_W), lambda b, j: (b, 0, 0))] + _hbm_specs(n),
        out_shape=[mat, mat, mat, jax.ShapeDtypeStruct((N_BLOCKS, 1, STATE_W), F32)] + _gather_out_shapes(shards),
        scratch_shapes=[pltpu.VMEM((2, 2, _POW_ROWS, BLOCK_ST), F32), pltpu.VMEM((2, 2, BLOCK_CH, BLOCK_ST), F32),
                        pltpu.VMEM((2 * CHUNK, BLOCK_CH, BLOCK_CH), F32), pltpu.VMEM((2, CHUNK_W, BLOCK_ST), F32)]
        + _ssm_block_scratch() + _gather_semaphores(n),
        compiler_params=_cparams(("arbitrary", "arbitrary"), has_side_effects=True),
    )(*blk, *shards)
    return outs[:4], outs[4:]


def _ssm_chunk_matrices_bwd(blk, d_m, d_ws, d_wot, d_a16):
    def body(ar_ref, ai_ref, ldt_ref, brc_ref, bic_ref, crc_ref, cic_ref, d_ref, dm_ref, dws_ref, dwot_ref, da16_ref,
             dar_ref, dai_ref, dldt_ref, dbr_ref, dbi_ref, dcr_ref, dci_ref, dd_ref,
             pw_ref, bbar_ref, dlag_ref, dbbar_ref, dc_ref, dpw_ref, xs_ref, dts_ref, br_ref, bi_ref, cr_ref, ci_ref):
        j = pl.program_id(1)
        w = BLOCK_ST

        @pl.when(j == 0)
        def _():
            _ssm_expand_blocks((brc_ref, bic_ref, crc_ref, cic_ref), (br_ref, bi_ref, cr_ref, ci_ref))
            _ssm_fill_powers(ar_ref, ai_ref, ldt_ref, br_ref, bi_ref, pw_ref, bbar_ref)
            for r in (dlag_ref, dbbar_ref, dc_ref, dpw_ref):
                r[...] = jnp.zeros_like(r)

        def fold(t):
            return jnp.sum(t.reshape(BLOCK_CH // 8, 8, w), axis=0)

        def d_power(d, ri, t):
            return jnp.sum(dpw_ref[d, ri, t], axis=0, keepdims=True)

        def x_chain(d, t, dxr, dxi):
            pr, pi = pw_ref[d, 0, pl.ds(t, 1), :], pw_ref[d, 1, pl.ds(t, 1), :]
            bbr, bbi = bbar_ref[d, 0], bbar_ref[d, 1]
            dbbar_ref[d, 0] += dxr * pr + dxi * pi
            dbbar_ref[d, 1] += dxi * pr - dxr * pi
            dpw_ref[d, 0, t] += fold(dxr * bbr + dxi * bbi)
            dpw_ref[d, 1, t] += fold(dxi * bbr - dxr * bbi)

        def z_chain(d, t, dzr, dzi):
            pr, pi = pw_ref[d, 0, pl.ds(t, 1), :], pw_ref[d, 1, pl.ds(t, 1), :]
            c_r, c_i = cr_ref[d, 0], ci_ref[d, 0]
            dc_ref[d, 0] += dzr * pr - dzi * pi
            dc_ref[d, 1] += -dzr * pi - dzi * pr
            dpw_ref[d, 0, t] += fold(dzr * c_r - dzi * c_i)
            dpw_ref[d, 1, t] += fold(-dzr * c_i - dzi * c_r)

        for jp in range(CHUNK):
            dlag_ref[jp - j + CHUNK - 1] += dm_ref[0, :, jp * BLOCK_CH:(jp + 1) * BLOCK_CH].astype(F32)
        quarter = lambda ref, i: ref[0, :, i * w:(i + 1) * w].astype(F32)
        x_chain(0, CHUNK - 1 - j, quarter(dws_ref, 0), quarter(dws_ref, 1))
        x_chain(1, j, quarter(dws_ref, 2), quarter(dws_ref, 3))
        z_chain(0, j + 1, quarter(dwot_ref, 0), quarter(dwot_ref, 1))
        z_chain(1, CHUNK - j, quarter(dwot_ref, 2), quarter(dwot_ref, 3))

        @pl.when(j == CHUNK - 1)
        def _():
            for d in range(2):
                _ssm_stack_inputs(d, pw_ref, bbar_ref, xs_ref)
                for t in range(CHUNK):
                    dts_ref[t * BLOCK_CH:(t + 1) * BLOCK_CH, :] = dlag_ref[CHUNK - 1 + t if d == 0 else CHUNK - 1 - t]
                d_taps = dts_ref[...]
                dc_ref[d, 0] += _dot_rounded(d_taps, xs_ref[0], ((0,), (0,)))
                dc_ref[d, 1] -= _dot_rounded(d_taps, xs_ref[1], ((0,), (0,)))
                xs_ref[0] = _dot_rounded(d_taps, cr_ref[d, 0])
                xs_ref[1] = -_dot_rounded(d_taps, ci_ref[d, 0])
                for t in range(CHUNK):
                    rows = slice(t * BLOCK_CH, (t + 1) * BLOCK_CH)
                    x_chain(d, t, xs_ref[0, rows, :], xs_ref[1, rows, :])
            dd_ref[0] = jnp.sum(dlag_ref[CHUNK - 1] * _eye(BLOCK_CH), axis=0, keepdims=True)
            for d in range(2):
                (abr, abi, cfr, cfi), disc_vjp = jax.vjp(_ssm_discretise, ar_ref[d, 0], ai_ref[d, 0], ldt_ref[d, 0])
                dpr = d_power(d, 0, CHUNK) + da16_ref[0, :, 2 * d * w:(2 * d + 1) * w]
                dpi = d_power(d, 1, CHUNK) + da16_ref[0, :, (2 * d + 1) * w:(2 * d + 2) * w]
                dabr, dabi = jnp.zeros_like(abr), jnp.zeros_like(abi)
                for t in range(CHUNK, 0, -1):
                    qr, qi = pw_ref[d, 0, t - 1:t, :], pw_ref[d, 1, t - 1:t, :]
                    dabr = dabr + dpr * qr + dpi * qi
                    dabi = dabi + dpi * qr - dpr * qi
                    dpr, dpi = (dpr * abr + dpi * abi + d_power(d, 0, t - 1),
                                dpi * abr - dpr * abi + d_power(d, 1, t - 1))
                dbbr, dbbi = dbbar_ref[d, 0], dbbar_ref[d, 1]
                b_r, b_i = br_ref[d, 0], bi_ref[d, 0]
                dbr_ref[d, 0] = _ssm_collapse_block(cfr * dbbr + cfi * dbbi)
                dbi_ref[d, 0] = _ssm_collapse_block(cfr * dbbi - cfi * dbbr)
                dcfr = jnp.sum(b_r * dbbr + b_i * dbbi, axis=0, keepdims=True)
                dcfi = jnp.sum(b_r * dbbi - b_i * dbbr, axis=0, keepdims=True)
                dar_ref[d, 0], dai_ref[d, 0], dldt_ref[d, 0] = disc_vjp((dabr, dabi, dcfr, dcfi))
                dcr_ref[d, 0] = _ssm_collapse_block(dc_ref[d, 0])
                dci_ref[d, 0] = _ssm_collapse_block(dc_ref[d, 1])

    row = pl.BlockSpec((1, BLOCK_CH, CHUNK_W), lambda b, j: (b, j, 0))
    specs = _ssm_param_specs()
    acc = lambda *s: pltpu.VMEM(s, F32)
    return pl.pallas_call(
        body, name="ssm_chunk_matrices_bwd", grid=(N_BLOCKS, CHUNK),
        in_specs=specs + [row, row, row, pl.BlockSpec((1, 1, STATE_W), lambda b, j: (b, 0, 0))],
        out_specs=specs,
        out_shape=[jax.ShapeDtypeStruct(t.shape, F32) for t in blk],
        scratch_shapes=[acc(2, 2, _POW_ROWS, BLOCK_ST), acc(2, 2, BLOCK_CH, BLOCK_ST), acc(2 * CHUNK, BLOCK_CH, BLOCK_CH),
                        acc(2, 2, BLOCK_CH, BLOCK_ST), acc(2, 2, BLOCK_CH, BLOCK_ST), acc(2, 2, CHUNK + 1, 8, BLOCK_ST),
                        acc(2, CHUNK_W, BLOCK_ST), acc(CHUNK_W, BLOCK_CH)] + _ssm_block_scratch(),
        compiler_params=_cparams(("arbitrary", "arbitrary")),
    )(*blk, d_m, d_ws, d_wot, d_a16)


def _block_matmul(terms, name, out_dtype=F32, tn=1024):
    nc = terms[0][0].shape[1]
    n_out = terms[0][1].shape[1] if terms[0][2] else terms[0][1].shape[2]
    flags = [t[2] for t in terms]

    def body(*refs):
        out_ref = refs[-1]
        acc = None
        for t, transposed in enumerate(flags):
            a = refs[2 * t][0].astype(BF16)
            w = refs[2 * t + 1][0]
            part = _dot_nt(a, w) if transposed else _dot(a, w)
            acc = part if acc is None else acc + part
        out_ref[0] = acc.astype(out_dtype)

    in_specs, args = [], []
    for a, w, transposed in terms:
        k = a.shape[2]
        in_specs.append(pl.BlockSpec((1, nc, k), lambda b, n: (b, 0, 0)))
        if transposed:
            in_specs.append(pl.BlockSpec((1, tn, k), lambda b, n: (b, n, 0)))
        else:
            in_specs.append(pl.BlockSpec((1, k, tn), lambda b, n: (b, 0, n)))
        args += [a, w]
    return pl.pallas_call(
        body, name=name, grid=(N_BLOCKS, n_out // tn), in_specs=in_specs,
        out_specs=pl.BlockSpec((1, nc, tn), lambda b, n: (b, 0, n)),
        out_shape=jax.ShapeDtypeStruct((N_BLOCKS, nc, n_out), out_dtype),
        compiler_params=_cparams(("arbitrary", "arbitrary")),
    )(*args)


def _block_matmul_tn(a, b, name, tile=1024):
    nc, m = a.shape[1], a.shape[2]
    n = b.shape[2]

    def body(a_ref, b_ref, out_ref):
        out_ref[0] = _dot_tn(a_ref[0].astype(BF16), b_ref[0].astype(BF16)).astype(BF16)

    return pl.pallas_call(
        body, name=name, grid=(N_BLOCKS, m // tile, n // tile),
        in_specs=[pl.BlockSpec((1, nc, tile), lambda blk, i, j: (blk, 0, i)),
                  pl.BlockSpec((1, nc, tile), lambda blk, i, j: (blk, 0, j))],
        out_specs=pl.BlockSpec((1, tile, tile), lambda blk, i, j: (blk, i, j)),
        out_shape=jax.ShapeDtypeStruct((N_BLOCKS, m, n), BF16),
        compiler_params=_cparams(("arbitrary", "arbitrary", "arbitrary")),
    )(a, b)


def _cmul(ar, ai, xr, xi):
    return ar * xr - ai * xi, ar * xi + ai * xr


def _cmul_conj(ar, ai, xr, xi):
    return ar * xr + ai * xi, ar * xi - ai * xr


def _ssm_state_scan(s_in, a16):
    nc = s_in.shape[1]
    w = BLOCK_ST

    def body(sin_ref, a_ref, out_ref):
        a = a_ref[0]
        afr, afi, abr, abi = a[:, 0:w], a[:, w:2 * w], a[:, 2 * w:3 * w], a[:, 3 * w:4 * w]

        def step(c, carry):
            fr, fi, br, bi = carry
            cb = nc - 1 - c
            out_ref[0, pl.ds(c, 1), 0:w] = fr
            out_ref[0, pl.ds(c, 1), w:2 * w] = fi
            out_ref[0, pl.ds(cb, 1), 2 * w:3 * w] = br
            out_ref[0, pl.ds(cb, 1), 3 * w:4 * w] = bi
            nfr, nfi = _cmul(afr, afi, fr, fi)
            nbr, nbi = _cmul(abr, abi, br, bi)
            return (nfr + sin_ref[0, pl.ds(c, 1), 0:w], nfi + sin_ref[0, pl.ds(c, 1), w:2 * w],
                    nbr + sin_ref[0, pl.ds(cb, 1), 2 * w:3 * w], nbi + sin_ref[0, pl.ds(cb, 1), 3 * w:4 * w])

        z = jnp.zeros((1, w), F32)
        lax.fori_loop(0, nc, step, (z, z, z, z))

    spec = pl.BlockSpec((1, nc, STATE_W), lambda b: (b, 0, 0))
    return pl.pallas_call(
        body, name="ssm_state_scan", grid=(N_BLOCKS,),
        in_specs=[spec, pl.BlockSpec((1, 1, STATE_W), lambda b: (b, 0, 0))],
        out_specs=spec, out_shape=jax.ShapeDtypeStruct(s_in.shape, F32),
        compiler_params=_cparams(("arbitrary",)),
    )(s_in, a16)


def _ssm_state_scan_bwd(d_prev, s_prev, a16):
    nc = d_prev.shape[1]
    w = BLOCK_ST

    def body(dp_ref, sp_ref, a_ref, g_ref, da_ref):
        a = a_ref[0]
        afr, afi, abr, abi = a[:, 0:w], a[:, w:2 * w], a[:, 2 * w:3 * w], a[:, 3 * w:4 * w]

        def step(i, carry):
            gfr, gfi, gbr, gbi, dafr, dafi, dabr, dabi = carry
            cf = nc - 1 - i
            cb = i
            g_ref[0, pl.ds(cf, 1), 0:w] = gfr
            g_ref[0, pl.ds(cf, 1), w:2 * w] = gfi
            g_ref[0, pl.ds(cb, 1), 2 * w:3 * w] = gbr
            g_ref[0, pl.ds(cb, 1), 3 * w:4 * w] = gbi
            sfr, sfi = sp_ref[0, pl.ds(cf, 1), 0:w], sp_ref[0, pl.ds(cf, 1), w:2 * w]
            sbr, sbi = sp_ref[0, pl.ds(cb, 1), 2 * w:3 * w], sp_ref[0, pl.ds(cb, 1), 3 * w:4 * w]
            dafr = dafr + gfr * sfr + gfi * sfi
            dafi = dafi + gfi * sfr - gfr * sfi
            dabr = dabr + gbr * sbr + gbi * sbi
            dabi = dabi + gbi * sbr - gbr * sbi
            nfr, nfi = _cmul_conj(afr, afi, gfr, gfi)
            nbr, nbi = _cmul_conj(abr, abi, gbr, gbi)
            return (nfr + dp_ref[0, pl.ds(cf, 1), 0:w], nfi + dp_ref[0, pl.ds(cf, 1), w:2 * w],
                    nbr + dp_ref[0, pl.ds(cb, 1), 2 * w:3 * w], nbi + dp_ref[0, pl.ds(cb, 1), 3 * w:4 * w],
                    dafr, dafi, dabr, dabi)

        z = jnp.zeros((1, w), F32)
        res = lax.fori_loop(0, nc, step, (z,) * 8)
        da_ref[0] = jnp.concatenate(res[4:], axis=1)

    spec = pl.BlockSpec((1, nc, STATE_W), lambda b: (b, 0, 0))
    aspec = pl.BlockSpec((1, 1, STATE_W), lambda b: (b, 0, 0))
    return pl.pallas_call(
        body, name="ssm_state_scan_bwd", grid=(N_BLOCKS,),
        in_specs=[spec, spec, aspec], out_specs=[spec, aspec],
        out_shape=[jax.ShapeDtypeStruct(d_prev.shape, F32), jax.ShapeDtypeStruct((N_BLOCKS, 1, STATE_W), F32)],
        compiler_params=_cparams(("arbitrary",)),
    )(d_prev, s_prev, a16)


NA_PAIR = 2 * GRID_W
NA_WIN_ROWS = NA_ROWS + 2
NA_WIN = NA_WIN_ROWS * GRID_W
NA_PAIRS_PER_STEP = 8
NA_CASES = 5
NA_MASKED = -1e30


def _na_pair_window(m, rows):
    rs0 = jnp.clip(2 * m - NA_ROWS // 2, 0, rows - NA_ROWS)
    ws = jnp.minimum(rs0, rows - NA_WIN_ROWS)
    last = rows // 2 - 1
    case = jnp.where(m == 0, 0, jnp.where(m == 1, 1, jnp.where(m == last - 1, 3, jnp.where(m == last, 4, 2))))
    return ws, case


def _na_row_offsets(rows):
    last = rows // 2 - 1
    geom = []
    for m in (0, 1, 2, last - 1, last):
        ws = min(max(2 * m - NA_ROWS // 2, 0), rows - NA_ROWS, rows - NA_WIN_ROWS)
        per_case = []
        for i in range(NA_WIN_ROWS):
            pair = []
            for rr in range(2):
                r = 2 * m + rr
                rs = min(max(r - NA_ROWS // 2, 0), rows - NA_ROWS)
                pair.append(ws + i - r + NA_ROWS - 1 if rs <= ws + i < rs + NA_ROWS else None)
            per_case.append(pair)
        geom.append(per_case)
    return geom


def _na_col_select():
    qc = np.arange(NA_PAIR)[None, :] % GRID_W
    kc = np.arange(GRID_W)[:, None]
    dc = np.clip(kc - qc + NA_COLS - 1, 0, 2 * NA_COLS - 2)
    return jnp.asarray((np.arange(2 * NA_COLS - 1)[:, None, None] == dc[None]).astype(np.float32))


def _na_bias_rows(rpb):
    return jnp.einsum("hrd,dkl->hrkl", rpb, _na_col_select(), precision=HIGHEST)


def _na_col_window():
    qc = lax.broadcasted_iota(jnp.int32, (GRID_W, NA_PAIR), 1) % GRID_W
    kc = lax.broadcasted_iota(jnp.int32, (GRID_W, NA_PAIR), 0)
    cs = jnp.clip(qc - NA_COLS // 2, 0, GRID_W - NA_COLS)
    first_row = lax.broadcasted_iota(jnp.int32, (GRID_W, NA_PAIR), 1) < GRID_W
    return (kc >= cs) & (kc < cs + NA_COLS), first_row


def _na_bias_table(bias_rows, rows):
    geom = _na_row_offsets(rows)

    def body(br_ref, tab_ref):
        col_ok, first_row = _na_col_window()
        masked = jnp.full((GRID_W, NA_PAIR), NA_MASKED, F32)
        for case in range(NA_CASES):
            for i in range(NA_WIN_ROWS):
                d0, d1 = geom[case][i]
                t0 = masked if d0 is None else br_ref[0, d0]
                t1 = masked if d1 is None else br_ref[0, d1]
                tile = jnp.where(col_ok, jnp.where(first_row, t0, t1), NA_MASKED)
                tab_ref[0, case, i * GRID_W:(i + 1) * GRID_W, :] = tile

    return pl.pallas_call(
        body, name="na_bias_table", grid=(NA_HEADS,),
        in_specs=[pl.BlockSpec((1, 2 * NA_ROWS - 1, GRID_W, NA_PAIR), lambda h: (h, 0, 0, 0))],
        out_specs=pl.BlockSpec((1, NA_CASES, NA_WIN, NA_PAIR), lambda h: (h, 0, 0, 0)),
        out_shape=jax.ShapeDtypeStruct((NA_HEADS, NA_CASES, NA_WIN, NA_PAIR), F32),
        compiler_params=_cparams(("arbitrary",)),
    )(bias_rows)


def _na_bias_table_bwd(d_tab, rows):
    geom = _na_row_offsets(rows)

    def body(dt_ref, dbr_ref):
        col_ok, first_row = _na_col_window()
        acc = [None] * (2 * NA_ROWS - 1)
        for case in range(NA_CASES):
            for i in range(NA_WIN_ROWS):
                tile = jnp.where(col_ok, dt_ref[0, case, i * GRID_W:(i + 1) * GRID_W, :], 0.0)
                for rr, d in enumerate(geom[case][i]):
                    if d is not None:
                        part = jnp.where(first_row if rr == 0 else ~first_row, tile, 0.0)
                        acc[d] = part if acc[d] is None else acc[d] + part
        for d, a in enumerate(acc):
            dbr_ref[0, d] = jnp.zeros((GRID_W, NA_PAIR), F32) if a is None else a

    return pl.pallas_call(
        body, name="na_bias_table_bwd", grid=(NA_HEADS,),
        in_specs=[pl.BlockSpec((1, NA_CASES, NA_WIN, NA_PAIR), lambda h: (h, 0, 0, 0))],
        out_specs=pl.BlockSpec((1, 2 * NA_ROWS - 1, GRID_W, NA_PAIR), lambda h: (h, 0, 0, 0)),
        out_shape=jax.ShapeDtypeStruct((NA_HEADS, 2 * NA_ROWS - 1, GRID_W, NA_PAIR), F32),
        compiler_params=_cparams(("arbitrary",)),
    )(d_tab)


NA_BLK = 64


def _na_blocks():
    return [slice(i * NA_BLK, (i + 1) * NA_BLK) for i in range(NA_WIN // NA_BLK)]


def _na_softmax(qk, bias_ref, hh, case):
    m = jnp.full((NA_BLK, NA_PAIR), -jnp.inf, F32)
    scores = []
    for blk in _na_blocks():
        s = qk[blk, :] + bias_ref[hh, case, blk, :]
        scores.append(s)
        m = jnp.maximum(m, s)
    m = jnp.max(m, axis=0, keepdims=True)
    l = jnp.zeros((NA_BLK, NA_PAIR), F32)
    exps = []
    for s in scores:
        e = jnp.exp(s - m)
        exps.append(e)
        l = l + e
    return exps, jnp.sum(l, axis=0, keepdims=True)


def _na_units(step, rows):
    units = []
    for pp in range(NA_PAIRS_PER_STEP):
        ws, case = _na_pair_window(step * NA_PAIRS_PER_STEP + pp, rows)
        win = pl.ds(pl.multiple_of(ws * GRID_W, NA_PAIR), NA_WIN)
        lanes = slice(pp * NA_PAIR, (pp + 1) * NA_PAIR)
        for hh in range(2):
            units.append((pp, hh, case, win, lanes, slice(hh * NA_HEAD_DIM, (hh + 1) * NA_HEAD_DIM)))
    return units


def _na_pipeline(n, before, middle, after, lookahead):
    for u in range(min(lookahead, n)):
        for f in before:
            f(u)
    for u in range(n):
        middle(u)
        if u + lookahead < n:
            for f in before:
                f(u + lookahead)
        for f in after:
            f(u)


def _head_rows(t, hh):
    row_head = lax.broadcasted_iota(jnp.int32, t.shape, 0) // NA_HEAD_DIM
    return jnp.where(row_head == hh, t, jnp.zeros_like(t))


def _heads_block_diag(t):
    lane_head = lax.broadcasted_iota(jnp.int32, t.shape, 1) // NA_HEAD_DIM
    zero = jnp.zeros_like(t)
    return jnp.concatenate([jnp.where(lane_head == 0, t, zero), jnp.where(lane_head == 1, t, zero)], axis=0)


def _na_fwd(q_t, k, v_t, bias_tab):
    L = k.shape[0]
    rows = L // GRID_W
    step_w = NA_PAIRS_PER_STEP * NA_PAIR

    def body(q_ref, k_ref, v_ref, bt_ref, o_ref):
        units = _na_units(pl.program_id(1), rows)
        qk, probs = {}, {}

        def scores(u):
            _, hh, _, win, lanes, _ = units[u]
            qk[u] = _dot(k_ref[win, :], _head_rows(q_ref[:, lanes], hh))

        def softmax(u):
            _, hh, case, _, _, _ = units[u]
            exps, l = _na_softmax(qk.pop(u), bt_ref, hh, case)
            probs[u] = jnp.concatenate([t.astype(BF16) for t in exps], axis=0), l

        def output(u):
            _, _, _, win, lanes, hrows = units[u]
            e, l = probs.pop(u)
            o_ref[hrows, lanes] = _dot(v_ref[hrows, win], e) / l

        _na_pipeline(len(units), [scores], softmax, [output], lookahead=3)

    q_spec = pl.BlockSpec((NA_PAIR, step_w), lambda h, s: (h, s))
    return pl.pallas_call(
        body, name="na_fwd", grid=(NA_HEADS // 2, L // step_w),
        in_specs=[q_spec, pl.BlockSpec((L, NA_PAIR), lambda h, s: (0, h)),
                  pl.BlockSpec((NA_PAIR, L), lambda h, s: (h, 0)),
                  pl.BlockSpec((2, NA_CASES, NA_WIN, NA_PAIR), lambda h, s: (h, 0, 0, 0))],
        out_specs=q_spec,
        out_shape=jax.ShapeDtypeStruct((D_NA, L), F32),
        compiler_params=_cparams(("arbitrary", "arbitrary")),
    )(q_t, k, v_t, bias_tab)


def _na_bwd(q_t, q, k_t, k, v, bias_tab, out_t, d_out_t, d_out):
    L = k.shape[0]
    rows = L // GRID_W
    step_w = NA_PAIRS_PER_STEP * NA_PAIR

    def body(qt_ref, q_ref, kt_ref, k_ref, v_ref, bt_ref, ot_ref, dot_ref, do_ref, dq_ref, dk_ref, dv_ref, dbt_ref):
        @pl.when(pl.program_id(1) == 0)
        def _():
            dk_ref[...] = jnp.zeros_like(dk_ref)
            dv_ref[...] = jnp.zeros_like(dv_ref)
            dbt_ref[...] = jnp.zeros_like(dbt_ref)

        units = _na_units(pl.program_id(1), rows)
        qk, dp, dsb, pb = {}, {}, {}, {}

        def scores(u):
            _, hh, _, win, lanes, _ = units[u]
            qk[u] = _dot(k_ref[win, :], _head_rows(qt_ref[:, lanes], hh))

        def d_probs(u):
            _, hh, _, win, lanes, _ = units[u]
            dp[u] = _dot(v_ref[win, :], _head_rows(dot_ref[:, lanes].astype(BF16), hh))

        def softmax_bwd(u):
            _, hh, case, _, lanes, hrows = units[u]
            exps, l = _na_softmax(qk.pop(u), bt_ref, hh, case)
            inv_l = 1.0 / l
            delta = jnp.sum(dot_ref[hrows, lanes] * ot_ref[hrows, lanes], axis=0, keepdims=True)
            d_p = dp.pop(u)
            ds_blocks, p_blocks = [], []
            for blk, e in zip(_na_blocks(), exps):
                p = e * inv_l
                ds = p * (d_p[blk, :] - delta)
                dbt_ref[hh, case, blk, :] += ds
                ds_blocks.append(ds.astype(BF16))
                p_blocks.append(p.astype(BF16))
            dsb[u] = jnp.concatenate(ds_blocks, axis=0)
            pb[u] = jnp.concatenate(p_blocks, axis=0)

        def d_query(u):
            _, _, _, win, lanes, hrows = units[u]
            dq_ref[hrows, lanes] = _dot(kt_ref[hrows, win], dsb[u]) * (NA_HEAD_DIM ** -0.5)

        def d_keys_values(u):
            pp, hh, _, win, _, _ = units[u]
            if hh == 1:
                tokens = slice(pp * NA_PAIR, (pp + 1) * NA_PAIR)
                dk_ref[win, :] += _dot(jnp.concatenate([dsb.pop(u - 1), dsb.pop(u)], axis=1), _heads_block_diag(q_ref[tokens, :]))
                dv_ref[win, :] += _dot(jnp.concatenate([pb.pop(u - 1), pb.pop(u)], axis=1), _heads_block_diag(do_ref[tokens, :]))

        _na_pipeline(len(units), [scores, d_probs], softmax_bwd, [d_query, d_keys_values], lookahead=2)

    t_tile = pl.BlockSpec((NA_PAIR, step_w), lambda h, s: (h, s))
    tile = pl.BlockSpec((step_w, NA_PAIR), lambda h, s: (s, h))
    t_full = pl.BlockSpec((NA_PAIR, L), lambda h, s: (h, 0))
    full = pl.BlockSpec((L, NA_PAIR), lambda h, s: (0, h))
    bt = pl.BlockSpec((2, NA_CASES, NA_WIN, NA_PAIR), lambda h, s: (h, 0, 0, 0))
    tok = jax.ShapeDtypeStruct((L, D_NA), F32)
    return pl.pallas_call(
        body, name="na_bwd", grid=(NA_HEADS // 2, L // step_w),
        in_specs=[t_tile, tile, t_full, full, full, bt, t_tile, t_tile, tile],
        out_specs=[t_tile, full, full, bt],
        out_shape=[jax.ShapeDtypeStruct((D_NA, L), F32), tok, tok, jax.ShapeDtypeStruct(bias_tab.shape, F32)],
        compiler_params=_cparams(("arbitrary", "arbitrary")),
    )(q_t, q, k_t, k, v, bias_tab, out_t, d_out_t, d_out)


def _branch_fwd_values(ys, zs, yn, zn, wglu, bglu):
    g1, t = _gelu_parts(ys)
    lin = _dot(g1.astype(BF16), wglu) + bglu
    sg = _sigmoid(lin)
    ys2 = g1 * sg
    sz, szs = _silu_parts(zs)
    sn, sns = _silu_parts(zn)
    return g1, t, sg, ys2, sz, szs, sn, sns


def _branch_fwd(y_ssm_c, z_s, y_na_t, z_n, w_glu, b_glu, tm=512):
    L = z_s.shape[0]

    def body(ys_ref, zs_ref, yn_ref, zn_ref, w_ref, b_ref, cat_ref, scr):
        yn = yn_ref[...].T
        g1, t, sg, ys2, sz, szs, sn, sns = _branch_fwd_values(
            _load_chunks(ys_ref, scr), zs_ref[...], yn, zn_ref[...], w_ref[...], b_ref[...])
        cat_ref[:, 0:512] = (ys2 * sz).astype(BF16)
        cat_ref[:, 512:1024] = (yn * sn).astype(BF16)

    tile = pl.BlockSpec((tm, 512), lambda i: (i, 0))
    return pl.pallas_call(
        body, name="branch_fwd", grid=(L // tm,),
        in_specs=[_chunk_spec(tm), tile, _heads_t_spec(tm), tile, pl.BlockSpec((512, 512), lambda i: (0, 0)),
                  pl.BlockSpec((1, 512), lambda i: (0, 0))],
        out_specs=pl.BlockSpec((tm, 1024), lambda i: (i, 0)),
        out_shape=jax.ShapeDtypeStruct((L, 1024), BF16),
        scratch_shapes=[_chunk_scratch(tm)],
        compiler_params=_cparams(("arbitrary",)),
    )(y_ssm_c, z_s, y_na_t, z_n, w_glu, b_glu)


def _branch_bwd(y_ssm_c, z_s, y_na_t, z_n, w_glu, b_glu, d_cat, tm=512):
    L = z_s.shape[0]

    def body(ys_ref, zs_ref, yn_ref, zn_ref, w_ref, b_ref, dc_ref,
             dys_ref, dzs_ref, dynt_ref, dyn_ref, dzn_ref, dw_ref, db_ref, scr):
        @pl.when(pl.program_id(0) == 0)
        def _():
            dw_ref[...] = jnp.zeros_like(dw_ref)
            db_ref[...] = jnp.zeros_like(db_ref)

        ys, zs, yn, zn = _load_chunks(ys_ref, scr), zs_ref[...], yn_ref[...].T, zn_ref[...]
        w = w_ref[...]
        g1, t, sg, ys2, sz, szs, sn, sns = _branch_fwd_values(ys, zs, yn, zn, w, b_ref[...])
        dys3 = dc_ref[:, 0:512]
        dyn2 = dc_ref[:, 512:1024]
        dzs_ref[...] = (dys3 * ys2 * _silu_grad(zs, szs)).astype(BF16)
        dys2 = dys3 * sz
        dlin = dys2 * g1 * sg * (1.0 - sg)
        dlb = dlin.astype(BF16)
        db_ref[...] += jnp.sum(dlin, axis=0, keepdims=True)
        dw_ref[...] += _dot_tn(g1.astype(BF16), dlb)
        dg1 = dys2 * sg + _dot_nt(dlb, w)
        _store_chunks(dg1 * _gelu_grad(ys, t), scr, dys_ref, BF16)
        dyn = dyn2 * sn
        dynt_ref[...] = dyn.T
        dyn_ref[...] = dyn.astype(BF16)
        dzn_ref[...] = (dyn2 * yn * _silu_grad(zn, sns)).astype(BF16)

    tile = pl.BlockSpec((tm, 512), lambda i: (i, 0))
    wspec = pl.BlockSpec((512, 512), lambda i: (0, 0))
    bspec = pl.BlockSpec((1, 512), lambda i: (0, 0))
    tok = jax.ShapeDtypeStruct((L, 512), BF16)
    return pl.pallas_call(
        body, name="branch_bwd", grid=(L // tm,),
        in_specs=[_chunk_spec(tm), tile, _heads_t_spec(tm), tile, wspec, bspec, pl.BlockSpec((tm, 1024), lambda i: (i, 0))],
        out_specs=[_chunk_spec(tm), tile, _heads_t_spec(tm), tile, tile, wspec, bspec],
        out_shape=[jax.ShapeDtypeStruct((N_BLOCKS, L // CHUNK, CHUNK_W), BF16), tok, jax.ShapeDtypeStruct((D_NA, L), F32),
                   tok, tok,
                   jax.ShapeDtypeStruct((512, 512), F32), jax.ShapeDtypeStruct((1, 512), F32)],
        scratch_shapes=[_chunk_scratch(tm)],
        compiler_params=_cparams(("arbitrary",)),
    )(y_ssm_c, z_s, y_na_t, z_n, w_glu, b_glu, d_cat)


def _head(x, p, target, cat, w_out, g_post, w_ple_g, g_ple, w_pg, tm=512):
    L = x.shape[0]
    pw = w_ple_g.shape[2]

    def body(x_ref, p_ref, t_ref, cat_ref, wo_ref, gpo_ref, wp_ref, gpl_ref, wg_ref,
             loss_ref, dh1_ref, dcat_ref, dwo_ref, dgpo_ref, dwp_ref, dgpl_ref, dwg_ref):
        @pl.when(pl.program_id(0) == 0)
        def _():
            for r in (loss_ref, dwo_ref, dgpo_ref, dwp_ref, dgpl_ref, dwg_ref):
                r[...] = jnp.zeros_like(r)

        cat_b = cat_ref[...]
        wo, wg = wo_ref[...], wg_ref[...]
        g_po, g_pl = gpo_ref[...], gpl_ref[...]
        mix = _dot(cat_b, wo)
        p_b = p_ref[...].astype(BF16)
        ep = jnp.concatenate([_dot(p_b, wp_ref[j]) for j in range(N_CHIPS)], axis=1)
        nm, r2 = _rms(mix)
        h1 = x_ref[...] + nm * g_po
        ne, r3 = _rms(ep)
        e = ne * g_pl
        h1_b = h1.astype(BF16)
        gate = _sigmoid(_dot(h1_b, wg))
        h2 = h1 + gate * e
        diff = h2 - t_ref[...]
        loss_ref[...] += (0.5 / D_MODEL) * jnp.sum(diff * diff).reshape(1, 1)

        dh2 = diff * (1.0 / D_MODEL)
        de = dh2 * gate
        dgl = (dh2 * e * gate * (1.0 - gate)).astype(BF16)
        dh1 = dh2 + _dot_nt(dgl, wg)
        dwg_ref[...] += _dot_tn(h1_b, dgl)
        dgpo_ref[...] += jnp.sum(dh1 * nm, axis=0, keepdims=True)
        dmix = _rms_bwd(dh1 * g_po, nm, r2).astype(BF16)
        dcat_ref[...] = _dot_nt(dmix, wo)
        dwo_ref[...] += _dot_tn(cat_b, dmix)
        dh1_ref[...] = dh1
        dgpl_ref[...] += jnp.sum(de * ne, axis=0, keepdims=True)
        dep = _rms_bwd(de * g_pl, ne, r3).astype(BF16)
        for j in range(N_CHIPS):
            dwp_ref[j] += _dot_tn(p_b, dep[:, j * pw:(j + 1) * pw])

    tile = lambda w: pl.BlockSpec((tm, w), lambda i: (i, 0))
    const = _resident
    sds = jax.ShapeDtypeStruct
    return pl.pallas_call(
        body, name="head", grid=(L // tm,),
        in_specs=[tile(D_MODEL), tile(D_PLE), tile(D_MODEL), tile(1024), const(1024, D_MODEL), const(1, D_MODEL),
                  const(N_CHIPS, D_PLE, pw), const(1, D_MODEL), const(D_MODEL, D_MODEL)],
        out_specs=[const(1, 1), tile(D_MODEL), tile(1024), const(1024, D_MODEL), const(1, D_MODEL),
                   const(N_CHIPS, D_PLE, pw), const(1, D_MODEL), const(D_MODEL, D_MODEL)],
        out_shape=[sds((1, 1), F32), sds((L, D_MODEL), F32), sds((L, 1024), F32), sds((1024, D_MODEL), F32),
                   sds((1, D_MODEL), F32), sds((N_CHIPS, D_PLE, pw), F32), sds((1, D_MODEL), F32),
                   sds((D_MODEL, D_MODEL), F32)],
        compiler_params=_cparams(("arbitrary",)),
    )(x, p, target, cat, w_out, g_post, w_ple_g, g_ple, w_pg)


def _dproj_specs(tm):
    tile = pl.BlockSpec((tm, 512), lambda i: (i, 0))
    return [_chunk_spec(tm), tile, _heads_t_spec(tm), tile, tile, tile]


def _dproj_tile(refs, scr):
    du_ref, dzs_ref, dqt_ref, dk_ref, dv_ref, dzn_ref = refs
    parts = [_load_chunks(du_ref, scr), dzs_ref[...], dqt_ref[...].T, dk_ref[...], dv_ref[...], dzn_ref[...]]
    return jnp.concatenate([t.astype(BF16) for t in parts], axis=1)


def _in_proj_bwd_w(x, g_col, w_in_g, dparts, tm=512):
    L = x.shape[0]
    wn = D_IN_PROJ // N_CHIPS
    steps = L // tm

    def body(x_ref, g_ref, w_ref, *refs):
        dw_ref, dg_ref, scr = refs[-3], refs[-2], refs[-1]

        @pl.when(pl.program_id(0) == 0)
        def _():
            dw_ref[...] = jnp.zeros_like(dw_ref)

        n, _ = _rms(x_ref[...])
        nb = n.astype(BF16)
        dproj = _dproj_tile(refs[:-3], scr)
        for j in range(N_CHIPS):
            dw_ref[j] += _dot_tn(nb, dproj[:, j * wn:(j + 1) * wn])

        @pl.when(pl.program_id(0) == steps - 1)
        def _():
            g = g_ref[...]
            dg = jnp.zeros_like(g)
            for j in range(N_CHIPS):
                a = dw_ref[j]
                dg = dg + jnp.sum(a * w_ref[j].astype(F32), axis=1, keepdims=True)
                dw_ref[j] = a * g
            dg_ref[...] = dg

    return pl.pallas_call(
        body, name="in_proj_bwd_w", grid=(steps,),
        in_specs=[pl.BlockSpec((tm, D_MODEL), lambda i: (i, 0)), _resident(D_MODEL, 1), _resident(N_CHIPS, D_MODEL, wn)]
        + _dproj_specs(tm),
        out_specs=[_resident(N_CHIPS, D_MODEL, wn), _resident(D_MODEL, 1)],
        out_shape=[jax.ShapeDtypeStruct((N_CHIPS, D_MODEL, wn), F32), jax.ShapeDtypeStruct((D_MODEL, 1), F32)],
        scratch_shapes=[_chunk_scratch(tm)],
        compiler_params=_cparams(("arbitrary",)),
    )(x, g_col, w_in_g, *dparts)


def _in_proj_bwd_x(x, g_pre, w_in_g, d_h1, dparts, pair_sums, tm=512):
    L = x.shape[0]
    wn = w_in_g.shape[2]
    n_ps = len(pair_sums)
    steps = L // tm

    def body(*refs):
        x_ref, g_ref, w_ref, dh1_ref = refs[:4]
        dparts_refs = refs[4:10]
        dx_ref = refs[10 + n_ps]
        scr = refs[11 + 2 * n_ps]
        scatter = _ChipScatter(refs[10:10 + n_ps], refs[11 + n_ps:11 + 2 * n_ps], refs[12 + 2 * n_ps:16 + 2 * n_ps],
                               refs[16 + 2 * n_ps:])
        pl.when(pl.program_id(0) == 0)(scatter.start)
        pl.when(pl.program_id(0) == steps - 1)(scatter.finish)

        dproj = _dproj_tile(dparts_refs, scr)
        halves = [slice(0, tm // 2), slice(tm // 2, tm)]
        dhns = []
        for rows in halves:
            dhn = _dot_nt(dproj[rows, 0:wn], w_ref[0])
            for j in range(1, N_CHIPS):
                dhn = dhn + _dot_nt(dproj[rows, j * wn:(j + 1) * wn], w_ref[j])
            dhns.append(dhn)
        for rows, dhn in zip(halves, dhns):
            n, r = _rms(x_ref[rows, :])
            dx_ref[rows, :] = dh1_ref[rows, :] + _rms_bwd(dhn * g_ref[...], n, r)

    wide = pl.BlockSpec((tm, D_MODEL), lambda i: (i, 0))
    outs = pl.pallas_call(
        body, name="in_proj_bwd_x", grid=(steps,),
        in_specs=[wide, _resident(1, D_MODEL), _resident(N_CHIPS, D_MODEL, wn), wide] + _dproj_specs(tm) + _hbm_specs(n_ps),
        out_specs=[wide] + _hbm_specs(n_ps),
        out_shape=[jax.ShapeDtypeStruct((L, D_MODEL), F32)] + [jax.ShapeDtypeStruct(p.shape, p.dtype) for p in pair_sums],
        scratch_shapes=[_chunk_scratch(tm)] + _scatter_scratch(pair_sums),
        compiler_params=_cparams(("arbitrary",), has_side_effects=True),
    )(x, g_pre, w_in_g, d_h1, *dparts, *pair_sums)
    return outs[0], outs[1:]


def _mesh_position():
    x, y, c = lax.axis_index("x"), lax.axis_index("y"), lax.axis_index("c")
    chips = [(1 - x, y), (x, 1 - y), (1 - x, 1 - y)]
    return x, y, c, chips


def _chip_index(cx, cy):
    return 2 * cx + cy


def _hbm_specs(n):
    return [pl.BlockSpec(memory_space=pl.ANY)] * n


def _gather_chips(shards, name):
    n = len(shards)

    def body(*refs):
        gather = _ChipGather(refs[:n], refs[n:2 * n], refs[2 * n:])
        gather.start()
        gather.forward()
        gather.finish()

    return pl.pallas_call(
        body, name=name, in_specs=_hbm_specs(n), out_specs=_hbm_specs(n),
        out_shape=_gather_out_shapes(shards), scratch_shapes=_gather_semaphores(n),
        compiler_params=pltpu.CompilerParams(has_side_effects=True),
    )(*shards)


def _gather_out_shapes(shards):
    return [jax.ShapeDtypeStruct((N_CHIPS,) + s.shape, s.dtype) for s in shards]


def _gather_semaphores(n):
    sem = pltpu.SemaphoreType.DMA
    return [sem((n, 3)), sem((n, 3)), sem((n, 3)), sem((n, 3)), sem((n,)), sem((n,))]


class _ChipGather:
    def __init__(self, ins, outs, sems):
        self.ins, self.outs = ins, outs
        self.send1, self.recv1, self.send2, self.recv2, self.send3, self.recv3 = sems
        self.x, self.y, self.c, self.chips = _mesh_position()
        self.me = _chip_index(self.x, self.y)
        self.sibling = (self.x, self.y, 1 - self.c)

    def _half(self, a, chip, core):
        hr = self.outs[a].shape[1] // 2
        return self.outs[a].at[chip, pl.ds(core * hr, hr)]

    def _own(self, a):
        return pltpu.make_async_remote_copy(
            src_ref=self.ins[a], dst_ref=self.outs[a].at[self.me], send_sem=self.send3.at[a], recv_sem=self.recv3.at[a],
            device_id=self.sibling, device_id_type=MESH)

    def _to_chip(self, a, j):
        hr = self.ins[a].shape[0] // 2
        return pltpu.make_async_remote_copy(
            src_ref=self.ins[a].at[pl.ds(self.c * hr, hr)], dst_ref=self._half(a, self.me, self.c),
            send_sem=self.send1.at[a, j], recv_sem=self.recv1.at[a, j], device_id=(*self.chips[j], self.c), device_id_type=MESH)

    def _from_chip(self, a, j):
        landed = self._half(a, _chip_index(*self.chips[j]), self.c)
        return pltpu.make_async_remote_copy(
            src_ref=landed, dst_ref=landed, send_sem=self.send1.at[a, j], recv_sem=self.recv1.at[a, j],
            device_id=(*self.chips[j], self.c), device_id_type=MESH)

    def _to_sibling(self, a, j, core):
        part = self._half(a, _chip_index(*self.chips[j]), core)
        return pltpu.make_async_remote_copy(
            src_ref=part, dst_ref=part, send_sem=self.send2.at[a, j], recv_sem=self.recv2.at[a, j],
            device_id=self.sibling, device_id_type=MESH)

    def _each(self):
        return [(a, j) for a in range(len(self.ins)) for j in range(3)]

    def start(self):
        for a in range(len(self.ins)):
            self._own(a).start()
        for a, j in self._each():
            self._to_chip(a, j).start()

    def forward(self):
        for a, j in self._each():
            self._from_chip(a, j).wait_recv()
            self._to_sibling(a, j, self.c).start()

    def finish(self):
        for a, j in self._each():
            self._to_sibling(a, j, 1 - self.c).wait_recv()
        for a, j in self._each():
            self._to_chip(a, j).wait_send()
            self._to_sibling(a, j, self.c).wait_send()
        for a in range(len(self.ins)):
            self._own(a).wait()


def _pair_exchange(grads):
    n = len(grads)

    def body(*refs):
        ins, outs = refs[:n], refs[n:2 * n]
        send, recv = refs[2 * n:]
        x, y, c, _ = _mesh_position()
        copies = []
        for a in range(n):
            hr = ins[a].shape[1] // 2
            cp = pltpu.make_async_remote_copy(
                src_ref=ins[a].at[:, pl.ds((1 - c) * hr, hr)], dst_ref=outs[a],
                send_sem=send.at[a], recv_sem=recv.at[a], device_id=(x, y, 1 - c), device_id_type=MESH)
            cp.start()
            copies.append(cp)
        for cp in copies:
            cp.wait()

    sem = pltpu.SemaphoreType.DMA
    return pl.pallas_call(
        body, name="pair_exchange", in_specs=_hbm_specs(n), out_specs=_hbm_specs(n),
        out_shape=[jax.ShapeDtypeStruct((g.shape[0], g.shape[1] // 2, g.shape[2]), g.dtype) for g in grads],
        scratch_shapes=[sem((n,)), sem((n,))],
        compiler_params=pltpu.CompilerParams(has_side_effects=True),
    )(*grads)


def _pair_add(core, grad, other, tr, out_dtype):
    hr = other.shape[1]
    cdim = other.shape[2]
    nb = hr // tr

    def body(core_ref, g_ref, o_ref, out_ref):
        out_ref[...] = (g_ref[...] + o_ref[...]).astype(out_dtype)

    return pl.pallas_call(
        body, name="pair_add",
        grid_spec=pltpu.PrefetchScalarGridSpec(
            num_scalar_prefetch=1, grid=(N_CHIPS, nb),
            in_specs=[pl.BlockSpec((1, tr, cdim), lambda j, i, core_ref: (j, core_ref[0] * nb + i, 0)),
                      pl.BlockSpec((1, tr, cdim), lambda j, i, core_ref: (j, i, 0))],
            out_specs=pl.BlockSpec((1, tr, cdim), lambda j, i, core_ref: (j, i, 0))),
        out_shape=jax.ShapeDtypeStruct(other.shape, out_dtype),
        compiler_params=_cparams(("arbitrary", "arbitrary")),
    )(core, grad, other)


def _scatter_scratch(parts):
    sem = pltpu.SemaphoreType.DMA
    n = len(parts)
    return [sem((n, 3)), sem((n, 3)), sem((n,)), sem((n,))] + [pltpu.VMEM(p.shape[1:], p.dtype) for p in parts]


class _ChipScatter:
    def __init__(self, ins, outs, sems, staged):
        self.ins, self.outs, self.staged = ins, outs, staged
        self.send, self.recv, self.load_sem, self.store_sem = sems
        self.x, self.y, self.c, self.chips = _mesh_position()
        self.me = _chip_index(self.x, self.y)

    def _load(self, a):
        return pltpu.make_async_copy(self.ins[a].at[self.me], self.staged[a], self.load_sem.at[a])

    def _store(self, a):
        return pltpu.make_async_copy(self.staged[a], self.outs[a].at[self.me], self.store_sem.at[a])

    def _to_chip(self, a, j):
        return pltpu.make_async_remote_copy(
            src_ref=self.ins[a].at[_chip_index(*self.chips[j])], dst_ref=self.outs[a].at[self.me],
            send_sem=self.send.at[a, j], recv_sem=self.recv.at[a, j], device_id=(*self.chips[j], self.c), device_id_type=MESH)

    def start(self):
        for a in range(len(self.ins)):
            self._load(a).start()
            for j in range(3):
                self._to_chip(a, j).start()

    def finish(self):
        for a in range(len(self.ins)):
            self._load(a).wait()
            self._store(a).start()
        for a in range(len(self.ins)):
            for j in range(3):
                self._to_chip(a, j).wait()
            self._store(a).wait()


def _chip_add(core, recv, tr):
    hr, cdim = recv.shape[1], recv.shape[2]
    nb = hr // tr

    def body(core_ref, r_ref, out_ref):
        out_ref[...] = ((r_ref[0].astype(F32) + r_ref[1].astype(F32)) + r_ref[2].astype(F32)) + r_ref[3].astype(F32)

    return pl.pallas_call(
        body, name="chip_add",
        grid_spec=pltpu.PrefetchScalarGridSpec(
            num_scalar_prefetch=1, grid=(nb,),
            in_specs=[pl.BlockSpec((N_CHIPS, tr, cdim), lambda i, core_ref: (0, i, 0))],
            out_specs=pl.BlockSpec((tr, cdim), lambda i, core_ref: (core_ref[0] * nb + i, 0))),
        out_shape=jax.ShapeDtypeStruct((2 * hr, cdim), F32),
        compiler_params=_cparams(("arbitrary",)),
    )(core, recv)


def _pair_gather(fulls):
    n = len(fulls)

    def body(*refs):
        outs = refs[n:2 * n]
        send, recv = refs[2 * n:]
        x, y, c, _ = _mesh_position()
        copies = []
        for a in range(n):
            hr = outs[a].shape[0] // 2
            mine = outs[a].at[pl.ds(c * hr, hr)]
            cp = pltpu.make_async_remote_copy(
                src_ref=mine, dst_ref=mine, send_sem=send.at[a], recv_sem=recv.at[a],
                device_id=(x, y, 1 - c), device_id_type=MESH)
            cp.start()
            copies.append(cp)
        for cp in copies:
            cp.wait()

    sem = pltpu.SemaphoreType.DMA
    return pl.pallas_call(
        body, name="pair_gather", in_specs=_hbm_specs(n), out_specs=_hbm_specs(n),
        out_shape=[jax.ShapeDtypeStruct(f.shape, f.dtype) for f in fulls],
        input_output_aliases={a: a for a in range(n)},
        scratch_shapes=[sem((n,)), sem((n,))],
        compiler_params=pltpu.CompilerParams(has_side_effects=True),
    )(*fulls)


def _row_tile(rows):
    for t in (512, 256, 128, 64, 32, 16, 8):
        if rows % t == 0:
            return t
    raise ValueError(rows)


def _pair_sums(core, grads, ici_dtypes):
    others = _pair_exchange(grads)
    return [_pair_add(core, g, o, _row_tile(o.shape[1]), dt) for g, o, dt in zip(grads, others, ici_dtypes)]


def _finish_reduce(core, landed):
    return _pair_gather([_chip_add(core, r, _row_tile(r.shape[1])) for r in landed])


def _adamw(w, g, m, v):
    rows, cols = w.shape
    one_block = rows % 8 != 0 or rows * max(cols, 128) * 4 <= (1 << 20)
    tr = rows if one_block else _row_tile(rows)

    def body(w_ref, g_ref, m_ref, v_ref, d_ref, nm_ref, nv_ref):
        g_ = g_ref[...]
        m_ = ADAM_B1 * m_ref[...] + (1.0 - ADAM_B1) * g_
        v_ = ADAM_B2 * v_ref[...] + (1.0 - ADAM_B2) * (g_ * g_)
        m_hat = m_ / (1.0 - ADAM_B1 ** ADAM_STEP)
        v_hat = v_ / (1.0 - ADAM_B2 ** ADAM_STEP)
        d_ref[...] = -ADAM_LR * (m_hat / (jnp.sqrt(v_hat) + ADAM_EPS) + ADAM_WD * w_ref[...])
        nm_ref[...] = m_
        nv_ref[...] = v_

    spec = pl.BlockSpec((tr, cols), lambda i: (i, 0))
    shp = jax.ShapeDtypeStruct((rows, cols), F32)
    return pl.pallas_call(
        body, name="adamw", grid=(rows // tr,), in_specs=[spec] * 4, out_specs=[spec] * 3,
        out_shape=[shp] * 3, compiler_params=_cparams(("arbitrary",)),
    )(w, g, m, v)


_SMALL = ["norm_pre", "norm_post", "ssm_a_re", "ssm_a_im", "ssm_log_dt", "ssm_b_re", "ssm_b_im",
          "ssm_c_re", "ssm_c_im", "ssm_d", "b_glu", "na_rpb", "ple_norm"]
_BIG = ["w_in", "w_glu", "w_out", "w_ple", "w_ple_gate"]
_WEIGHTS = ["norm_pre", "norm_post", "w_in", "ssm_a_re", "ssm_a_im", "ssm_log_dt", "ssm_b_re", "ssm_b_im",
            "ssm_c_re", "ssm_c_im", "ssm_d", "w_glu", "b_glu", "na_rpb", "w_out", "w_ple", "ple_norm", "w_ple_gate"]
_SMALL_ROWS = 2176


def _pack_small(tensors, tail=None):
    parts = [tensors[n].reshape(-1) for n in _SMALL] + ([] if tail is None else [tail.reshape(-1)])
    flat = jnp.concatenate(parts)
    flat = jnp.pad(flat, (0, _SMALL_ROWS * 128 - flat.shape[0]))
    return flat.reshape(_SMALL_ROWS, 128)


def _unpack_small(packed, shapes):
    flat = packed.reshape(-1)
    out, off = {}, 0
    for n in _SMALL:
        size = int(np.prod(shapes[n]))
        out[n] = flat[off:off + size].reshape(shapes[n])
        off += size
    return out


def _local_grads(x, p, target, wts):
    ssm_names = ["ssm_a_re", "ssm_a_im", "ssm_log_dt", "ssm_b_re", "ssm_b_im", "ssm_c_re", "ssm_c_im", "ssm_d"]
    ssm_params = [wts[n][0] for n in ssm_names]
    blk, blk_vjp = jax.vjp(_ssm_block_params, *ssm_params)
    shard = lambda n: wts[n][0].astype(BF16)
    (m_mat, ws_mat, wot_mat, a16), (w_in_g,) = _ssm_chunk_matrices(blk, [shard("w_in")])
    seq = x.shape[0]
    bias_rows, bias_rows_vjp = jax.vjp(_na_bias_rows, wts["na_rpb"][0])
    bias_tab = _na_bias_table(bias_rows, seq // GRID_W)

    (u_c, z_s, q_t, q, k_t, k, v_t, v, z_n), gathered = _in_proj(
        x, wts["norm_pre"], w_in_g, [shard(n) for n in _BIG if n != "w_in"])
    w_glu, w_out, w_ple_g, w_pg = (gathered[0].reshape(512, 512), gathered[1].reshape(1024, 1024), gathered[2],
                                   gathered[3].reshape(1024, 1024))
    s_in = _block_matmul([(u_c, ws_mat, False)], "ssm_chunk_states")
    s_prev = _ssm_state_scan(s_in, a16)
    y_ssm_c = _block_matmul([(u_c, m_mat, False), (s_prev, wot_mat, True)], "ssm_chunk_out")
    y_na_t = _na_fwd(q_t, k, v_t, bias_tab)
    cat = _branch_fwd(y_ssm_c, z_s, y_na_t, z_n, w_glu, wts["b_glu"])

    (loss, d_h1, d_cat, d_w_out, d_g_post, d_w_ple, d_g_ple, d_w_pg) = _head(
        x, p, target, cat, w_out, wts["norm_post"], w_ple_g, wts["ple_norm"], w_pg)
    dy_c, d_z_s, d_y_na_t, d_y_na, d_z_n, d_w_glu, d_b_glu = _branch_bwd(
        y_ssm_c, z_s, y_na_t, z_n, w_glu, wts["b_glu"], d_cat)
    d_q_t, d_k, d_v, d_bias_tab = _na_bwd(q_t, q, k_t, k, v, bias_tab, y_na_t, d_y_na_t, d_y_na)

    d_prev = _block_matmul([(dy_c, wot_mat, False)], "ssm_bwd_states")
    g_st, d_a16 = _ssm_state_scan_bwd(d_prev, s_prev, a16)
    d_u_c = _block_matmul([(dy_c, m_mat, True), (g_st, ws_mat, True)], "ssm_bwd_in", out_dtype=BF16)
    d_m = _block_matmul_tn(u_c, dy_c, "ssm_grad_m")
    d_ws = _block_matmul_tn(u_c, g_st, "ssm_grad_ws")
    d_wot = _block_matmul_tn(dy_c, s_prev, "ssm_grad_wot")
    d_ssm = blk_vjp(tuple(_ssm_chunk_matrices_bwd(blk, d_m, d_ws, d_wot, d_a16)))
    (d_rpb,) = bias_rows_vjp(_na_bias_table_bwd(d_bias_tab, seq // GRID_W))

    dparts = [d_u_c, d_z_s, d_q_t, d_k, d_v, d_z_n]
    d_w_in, d_g_pre = _in_proj_bwd_w(x, wts["norm_pre"].reshape(D_MODEL, 1), w_in_g, dparts)

    small = {"norm_pre": d_g_pre, "norm_post": d_g_post, "b_glu": d_b_glu, "na_rpb": d_rpb, "ple_norm": d_g_ple}
    for n, g in zip(ssm_names, d_ssm):
        small[n] = g
    big = {"w_in": d_w_in, "w_glu": d_w_glu.reshape(N_CHIPS, 128, 512), "w_out": d_w_out.reshape(N_CHIPS, 256, 1024),
           "w_ple": d_w_ple, "w_ple_gate": d_w_pg.reshape(N_CHIPS, 256, 1024)}
    return loss, small, big, (x, wts["norm_pre"], w_in_g, d_h1, dparts)


def kernel(x, p, norm_pre, norm_post, w_in, ssm_a_re, ssm_a_im, ssm_log_dt, ssm_b_re, ssm_b_im, ssm_c_re, ssm_c_im, ssm_d, w_glu, b_glu, na_rpb, w_out, w_ple, ple_norm, w_ple_gate, loss_target, m_norm_pre, m_norm_post, m_w_in, m_ssm_a_re, m_ssm_a_im, m_ssm_log_dt, m_ssm_b_re, m_ssm_b_im, m_ssm_c_re, m_ssm_c_im, m_ssm_d, m_w_glu, m_b_glu, m_na_rpb, m_w_out, m_w_ple, m_ple_norm, m_w_ple_gate, v_norm_pre, v_norm_post, v_w_in, v_ssm_a_re, v_ssm_a_im, v_ssm_log_dt, v_ssm_b_re, v_ssm_b_im, v_ssm_c_re, v_ssm_c_im, v_ssm_d, v_w_glu, v_b_glu, v_na_rpb, v_w_out, v_w_ple, v_ple_norm, v_w_ple_gate):
    wts = dict(norm_pre=norm_pre, norm_post=norm_post, w_in=w_in, ssm_a_re=ssm_a_re, ssm_a_im=ssm_a_im,
               ssm_log_dt=ssm_log_dt, ssm_b_re=ssm_b_re, ssm_b_im=ssm_b_im, ssm_c_re=ssm_c_re, ssm_c_im=ssm_c_im,
               ssm_d=ssm_d, w_glu=w_glu, b_glu=b_glu, na_rpb=na_rpb, w_out=w_out, w_ple=w_ple, ple_norm=ple_norm,
               w_ple_gate=w_ple_gate)
    mom_m = dict(norm_pre=m_norm_pre, norm_post=m_norm_post, w_in=m_w_in, ssm_a_re=m_ssm_a_re, ssm_a_im=m_ssm_a_im,
                 ssm_log_dt=m_ssm_log_dt, ssm_b_re=m_ssm_b_re, ssm_b_im=m_ssm_b_im, ssm_c_re=m_ssm_c_re,
                 ssm_c_im=m_ssm_c_im, ssm_d=m_ssm_d, w_glu=m_w_glu, b_glu=m_b_glu, na_rpb=m_na_rpb, w_out=m_w_out,
                 w_ple=m_w_ple, ple_norm=m_ple_norm, w_ple_gate=m_w_ple_gate)
    mom_v = dict(norm_pre=v_norm_pre, norm_post=v_norm_post, w_in=v_w_in, ssm_a_re=v_ssm_a_re, ssm_a_im=v_ssm_a_im,
                 ssm_log_dt=v_ssm_log_dt, ssm_b_re=v_ssm_b_re, ssm_b_im=v_ssm_b_im, ssm_c_re=v_ssm_c_re,
                 ssm_c_im=v_ssm_c_im, ssm_d=v_ssm_d, w_glu=v_w_glu, b_glu=v_b_glu, na_rpb=v_na_rpb, w_out=v_w_out,
                 w_ple=v_w_ple, ple_norm=v_ple_norm, w_ple_gate=v_w_ple_gate)

    loss_part, small, big, input_grad_args = _local_grads(x[0], p[0, 0], loss_target[0], wts)

    core = lax.axis_index("c").astype(jnp.int32).reshape(1)
    small_packed = _pack_small(small, tail=loss_part).reshape(N_CHIPS, _SMALL_ROWS // N_CHIPS, 128)
    pair = _pair_sums(core, [big[n] for n in _BIG] + [small_packed], [BF16] * len(_BIG) + [F32])
    grad_x, landed = _in_proj_bwd_x(*input_grad_args, pair)
    reduced = _finish_reduce(core, landed)
    grads = dict(zip(_BIG, reduced[:-1]))
    (small_all,) = _gather_chips([reduced[-1]], "gather_small_grads")
    small_all = small_all.reshape(_SMALL_ROWS, 128)
    loss = small_all.reshape(-1)[sum(int(np.prod(wts[n].shape)) for n in _SMALL)]

    delta, new_m, new_v = {}, {}, {}
    for n in _BIG:
        shp = wts[n].shape
        d_, m_, v_ = _adamw(wts[n][0], grads[n], mom_m[n][0], mom_v[n][0])
        grads[n] = grads[n].reshape(shp)
        delta[n], new_m[n], new_v[n] = d_.reshape(shp), m_.reshape(shp), v_.reshape(shp)
    grads.update(_unpack_small(small_all, {n: wts[n].shape for n in _SMALL}))
    for n in _SMALL:
        shp = wts[n].shape
        rows_cols = (int(np.prod(shp[:-1])), shp[-1])
        d_, m_, v_ = _adamw(*[t.reshape(rows_cols) for t in (wts[n], grads[n], mom_m[n], mom_v[n])])
        delta[n], new_m[n], new_v[n] = d_.reshape(shp), m_.reshape(shp), v_.reshape(shp)

    return (loss, grad_x[None], *[grads[n] for n in _WEIGHTS], *[delta[n] for n in _WEIGHTS],
            *[new_m[n] for n in _WEIGHTS], *[new_v[n] for n in _WEIGHTS])
```

```python
import math

import jax
import jax.numpy as jnp
import numpy as np
from jax import lax
from jax.experimental import pallas as pl
from jax.experimental.pallas import tpu as pltpu

F32 = jnp.float32
BF16 = jnp.bfloat16

D_MODEL = 1024
D_PLE = 256
GRID_W = 64
D_SSM = 512
SSM_GROUP = 16
N_GROUPS = 32
SSM_STATE = 64
D_NA = 512
NA_HEADS = 8
NA_HEAD_DIM = 64
NA_ROWS = 8
NA_COLS = 16
D_IN_PROJ = 3072
EPS = 1e-6

CHUNK = 16
GROUPS_PER_BLOCK = 8
N_BLOCKS = N_GROUPS // GROUPS_PER_BLOCK
BLOCK_CH = GROUPS_PER_BLOCK * SSM_GROUP
BLOCK_ST = GROUPS_PER_BLOCK * SSM_STATE
CHUNK_W = CHUNK * BLOCK_CH
STATE_W = 4 * BLOCK_ST

N_CHIPS = 4
MESH = pl.DeviceIdType.MESH

ADAM_LR = 0.001
ADAM_B1 = 0.9
ADAM_B2 = 0.999
ADAM_EPS = 1e-08
ADAM_WD = 0.01
ADAM_STEP = 10

VMEM_LIMIT = 52 * 1024 * 1024
HIGHEST = lax.Precision.HIGHEST


def _cparams(sem=None, **kw):
    if sem is not None:
        kw["dimension_semantics"] = sem
    return pltpu.CompilerParams(vmem_limit_bytes=VMEM_LIMIT, **kw)


def _resident(*shape):
    return pl.BlockSpec(shape, lambda *_: (0,) * len(shape), pipeline_mode=pl.Buffered(1))


def _dot(a, b, dims=((1,), (0,))):
    return lax.dot_general(a, b, (dims, ((), ())), preferred_element_type=F32)


def _dot_nt(a, b):
    return _dot(a, b, ((1,), (1,)))


def _dot_tn(a, b):
    return _dot(a, b, ((0,), (0,)))


def _sigmoid(x):
    return 1.0 / (1.0 + jnp.exp(-x))


_GELU_C = math.sqrt(2.0 / math.pi)


def _gelu_parts(x):
    inner = _GELU_C * (x + 0.044715 * (x * x * x))
    t = jnp.tanh(inner)
    return 0.5 * x * (1.0 + t), t


def _gelu_grad(x, t):
    return 0.5 * (1.0 + t) + 0.5 * x * (1.0 - t * t) * (_GELU_C * (1.0 + 3.0 * 0.044715 * x * x))


def _silu_parts(z):
    s = _sigmoid(z)
    return z * s, s


def _silu_grad(z, s):
    return s * (1.0 + z * (1.0 - s))


def _rms(x):
    r = lax.rsqrt(jnp.mean(x * x, axis=-1, keepdims=True) + EPS)
    return x * r, r


def _rms_bwd(dn, n, r):
    return r * (dn - n * jnp.mean(dn * n, axis=-1, keepdims=True))


def _chunk_scratch(tm):
    return pltpu.VMEM((N_BLOCKS, tm, BLOCK_CH), F32)


def _store_chunks(val, scr, c_ref, dtype, row0=0):
    rows = val.shape[0]
    nc, c0 = rows // CHUNK, row0 // CHUNK
    for b in range(N_BLOCKS):
        scr[b, row0:row0 + rows, :] = val[:, b * BLOCK_CH:(b + 1) * BLOCK_CH]
        for j in range(CHUNK):
            c_ref[b, c0:c0 + nc, j * BLOCK_CH:(j + 1) * BLOCK_CH] = scr[b, pl.ds(row0 + j, nc, stride=CHUNK), :].astype(dtype)


def _load_chunks(c_ref, scr):
    nc = scr.shape[1] // CHUNK
    for b in range(N_BLOCKS):
        for j in range(CHUNK):
            scr[b, pl.ds(j, nc, stride=CHUNK), :] = c_ref[b, :, j * BLOCK_CH:(j + 1) * BLOCK_CH].astype(F32)
    return jnp.concatenate([scr[b] for b in range(N_BLOCKS)], axis=1)


def _chunk_spec(tm):
    return pl.BlockSpec((N_BLOCKS, tm // CHUNK, CHUNK_W), lambda i: (0, i, 0))


def _heads_t_spec(tm):
    return pl.BlockSpec((D_NA, tm), lambda i: (0, i))


def _in_proj(x, g_pre, w_in_g, shards, tm=512):
    L = x.shape[0]
    wn = w_in_g.shape[2]
    n_sh = len(shards)
    steps = L // tm

    def body(*refs):
        x_ref, g_ref, w_ref = refs[:3]
        uc_ref, zs_ref, qt_ref, q_ref, kt_ref, k_ref, vt_ref, v_ref, zn_ref = refs[3 + n_sh:12 + n_sh]
        u_scr = refs[12 + 2 * n_sh]
        gather = _ChipGather(refs[3:3 + n_sh], refs[12 + n_sh:12 + 2 * n_sh], refs[13 + 2 * n_sh:])
        step = pl.program_id(0)
        pl.when(step == 0)(gather.start)
        pl.when(step == steps // 2)(gather.forward)
        pl.when(step == steps - 1)(gather.finish)
        halves = [slice(0, tm // 2), slice(tm // 2, tm)]
        hn = [(_rms(x_ref[rows, :])[0] * g_ref[...]).astype(BF16) for rows in halves]
        projs = [jnp.concatenate([_dot(h, w_ref[j]) for j in range(N_CHIPS)], axis=1) for h in hn]
        for rows, proj in zip(halves, projs):
            _store_chunks(proj[:, 0:512], u_scr, uc_ref, BF16, row0=rows.start)
            zs_ref[rows, :] = proj[:, 512:1024]
            q = proj[:, 1024:1536] * (NA_HEAD_DIM ** -0.5)
            for val, t_ref, n_ref in ((q, qt_ref, q_ref), (proj[:, 1536:2048], kt_ref, k_ref), (proj[:, 2048:2560], vt_ref, v_ref)):
                t_ref[:, rows] = val.T.astype(BF16)
                n_ref[rows, :] = val.astype(BF16)
            zn_ref[rows, :] = proj[:, 2560:3072]

    tok = jax.ShapeDtypeStruct((L, 512), F32)
    tr = jax.ShapeDtypeStruct((D_NA, L), BF16)
    hm = jax.ShapeDtypeStruct((L, D_NA), BF16)
    tspec = pl.BlockSpec((tm, 512), lambda i: (i, 0))
    outs = pl.pallas_call(
        body, name="in_proj", grid=(steps,),
        in_specs=[pl.BlockSpec((tm, D_MODEL), lambda i: (i, 0)),
                  _resident(1, D_MODEL), _resident(N_CHIPS, D_MODEL, wn)] + _hbm_specs(n_sh),
        out_specs=[_chunk_spec(tm), tspec] + [_heads_t_spec(tm), tspec] * 3 + [tspec] + _hbm_specs(n_sh),
        out_shape=[jax.ShapeDtypeStruct((N_BLOCKS, L // CHUNK, CHUNK_W), BF16), tok, tr, hm, tr, hm, tr, hm, tok]
        + _gather_out_shapes(shards),
        scratch_shapes=[_chunk_scratch(tm)] + _gather_semaphores(n_sh),
        compiler_params=_cparams(("arbitrary",), has_side_effects=True),
    )(x, g_pre, w_in_g, *shards)
    return outs[:9], outs[9:]


def _ssm_block_params(a_re, a_im, log_dt, b_re, b_im, c_re, c_im, d):
    def lanes(t):
        return t.reshape(2, N_BLOCKS, 1, BLOCK_ST)

    rows = (2, N_BLOCKS, BLOCK_CH, SSM_STATE)
    b_rows = lambda t: t.reshape(2, N_BLOCKS, GROUPS_PER_BLOCK, SSM_STATE, SSM_GROUP).transpose(0, 1, 2, 4, 3).reshape(rows)
    return (lanes(a_re), lanes(a_im), lanes(jnp.broadcast_to(log_dt[..., None], a_re.shape)),
            b_rows(b_re), b_rows(b_im), c_re.reshape(rows), c_im.reshape(rows), d.reshape(N_BLOCKS, 1, BLOCK_CH))


def _ssm_group_mask():
    row_g = lax.broadcasted_iota(jnp.int32, (BLOCK_CH, BLOCK_ST), 0) // SSM_GROUP
    lane_g = lax.broadcasted_iota(jnp.int32, (BLOCK_CH, BLOCK_ST), 1) // SSM_STATE
    return row_g == lane_g


def _ssm_state_select():
    p = lax.broadcasted_iota(jnp.int32, (SSM_STATE, BLOCK_ST), 0)
    lane_p = lax.broadcasted_iota(jnp.int32, (SSM_STATE, BLOCK_ST), 1) % SSM_STATE
    return (p == lane_p).astype(F32)


def _ssm_expand_blocks(compact_refs, full_refs):
    mask, select = _ssm_group_mask(), _ssm_state_select()
    for c_ref, f_ref in zip(compact_refs, full_refs):
        for d in range(2):
            tiled = lax.dot_general(c_ref[d, 0], select, ((((1,), (0,))), ((), ())), precision=HIGHEST,
                                    preferred_element_type=F32)
            f_ref[d, 0] = jnp.where(mask, tiled, 0.0)


def _ssm_collapse_block(t):
    return lax.dot_general(jnp.where(_ssm_group_mask(), t, 0.0), _ssm_state_select(), ((((1,), (1,))), ((), ())),
                           precision=HIGHEST, preferred_element_type=F32)


def _ssm_discretise(ar, ai, ldt):
    dt = jnp.exp(ldt)
    mag = jnp.exp(dt * ar)
    abr = mag * jnp.cos(dt * ai)
    abi = mag * jnp.sin(dt * ai)
    num_re = abr - 1.0
    num_im = abi
    denom = ar * ar + ai * ai
    coef_re = (num_re * ar + num_im * ai) / denom
    coef_im = (num_im * ar - num_re * ai) / denom
    return abr, abi, coef_re, coef_im


_POW_ROWS = 24


def _ssm_fill_powers(ar_ref, ai_ref, ldt_ref, br_ref, bi_ref, pw_ref, bbar_ref):
    for d in range(2):
        abr, abi, cfr, cfi = _ssm_discretise(ar_ref[d, 0], ai_ref[d, 0], ldt_ref[d, 0])
        bbar_ref[d, 0] = cfr * br_ref[d, 0] - cfi * bi_ref[d, 0]
        bbar_ref[d, 1] = cfr * bi_ref[d, 0] + cfi * br_ref[d, 0]
        pr, pi = jnp.ones_like(abr), jnp.zeros_like(abi)
        for t in range(CHUNK + 1):
            pw_ref[d, 0, t:t + 1, :] = pr
            pw_ref[d, 1, t:t + 1, :] = pi
            pr, pi = pr * abr - pi * abi, pr * abi + pi * abr


def _dot_rounded(a, b, dims=((1,), (0,))):
    return _dot(a.astype(BF16), b.astype(BF16), dims)


def _ssm_stack_inputs(d, pw_ref, bbar_ref, xs_ref):
    for t in range(CHUNK):
        pr, pi = pw_ref[d, 0, t:t + 1, :], pw_ref[d, 1, t:t + 1, :]
        xs_ref[0, t * BLOCK_CH:(t + 1) * BLOCK_CH, :] = bbar_ref[d, 0] * pr - bbar_ref[d, 1] * pi
        xs_ref[1, t * BLOCK_CH:(t + 1) * BLOCK_CH, :] = bbar_ref[d, 0] * pi + bbar_ref[d, 1] * pr


def _eye(n):
    return (lax.broadcasted_iota(jnp.int32, (n, n), 0) == lax.broadcasted_iota(jnp.int32, (n, n), 1)).astype(F32)


def _ssm_param_specs():
    vec = pl.BlockSpec((2, 1, 1, BLOCK_ST), lambda b, j: (0, b, 0, 0))
    mat = pl.BlockSpec((2, 1, BLOCK_CH, SSM_STATE), lambda b, j: (0, b, 0, 0))
    return [vec, vec, vec, mat, mat, mat, mat, pl.BlockSpec((1, 1, BLOCK_CH), lambda b, j: (b, 0, 0))]


def _ssm_block_scratch():
    return [pltpu.VMEM((2, 1, BLOCK_CH, BLOCK_ST), F32)] * 4


def _ssm_chunk_matrices(blk, shards):
    n = len(shards)

    def body(*refs):
        ar_ref, ai_ref, ldt_ref = refs[:3]
        d_ref = refs[7]
        m_ref, ws_ref, wot_ref, a16_ref = refs[8 + n:12 + n]
        pw_ref, bbar_ref, lag_ref, xs_ref = refs[12 + 2 * n:16 + 2 * n]
        br_ref, bi_ref, cr_ref, ci_ref = refs[16 + 2 * n:20 + 2 * n]
        gather = _ChipGather(refs[8:8 + n], refs[12 + n:12 + 2 * n], refs[20 + 2 * n:])
        b, j = pl.program_id(0), pl.program_id(1)
        pl.when((b == 0) & (j == 0))(gather.start)
        pl.when((b == N_BLOCKS - 1) & (j == 0))(gather.forward)
        pl.when((b == N_BLOCKS - 1) & (j == CHUNK - 1))(gather.finish)

        @pl.when(j == 0)
        def _():
            _ssm_expand_blocks(refs[3:7], (br_ref, bi_ref, cr_ref, ci_ref))
            _ssm_fill_powers(ar_ref, ai_ref, ldt_ref, br_ref, bi_ref, pw_ref, bbar_ref)
            zero_lag = d_ref[0] * _eye(BLOCK_CH)
            for d in range(2):
                _ssm_stack_inputs(d, pw_ref, bbar_ref, xs_ref)
                taps = (_dot_rounded(xs_ref[0], cr_ref[d, 0], ((1,), (1,)))
                        - _dot_rounded(xs_ref[1], ci_ref[d, 0], ((1,), (1,))))
                zero_lag = zero_lag + taps[0:BLOCK_CH]
                for t in range(1, CHUNK):
                    lag_ref[CHUNK - 1 + t if d == 0 else CHUNK - 1 - t] = taps[t * BLOCK_CH:(t + 1) * BLOCK_CH]
            lag_ref[CHUNK - 1] = zero_lag
            a16_ref[0] = jnp.concatenate([pw_ref[d, ri, CHUNK:CHUNK + 1, :] for d in range(2) for ri in range(2)], axis=1)

        m_ref[0] = jnp.concatenate([lag_ref[jp - j + CHUNK - 1] for jp in range(CHUNK)], axis=1).astype(BF16)

        def power(d, t):
            return pw_ref[d, 0, pl.ds(t, 1), :], pw_ref[d, 1, pl.ds(t, 1), :]

        parts = []
        for d, t in ((0, CHUNK - 1 - j), (1, j)):
            pr, pi = power(d, t)
            parts += [bbar_ref[d, 0] * pr - bbar_ref[d, 1] * pi, bbar_ref[d, 0] * pi + bbar_ref[d, 1] * pr]
        ws_ref[0] = jnp.concatenate(parts, axis=1).astype(BF16)
        parts = []
        for d, t in ((0, j + 1), (1, CHUNK - j)):
            pr, pi = power(d, t)
            parts += [cr_ref[d, 0] * pr - ci_ref[d, 0] * pi, -cr_ref[d, 0] * pi - ci_ref[d, 0] * pr]
        wot_ref[0] = jnp.concatenate(parts, axis=1).astype(BF16)

    row = pl.BlockSpec((1, BLOCK_CH, CHUNK_W), lambda b, j: (b, j, 0))
    mat = jax.ShapeDtypeStruct((N_BLOCKS, CHUNK_W, CHUNK_W), BF16)
    outs = pl.pallas_call(
        body, name="ssm_chunk_matrices", grid=(N_BLOCKS, CHUNK),
        in_specs=_ssm_param_specs() + _hbm_specs(n),
        out_specs=[row, row, row, pl.BlockSpec((1, 1, STATE_W), lambda b, j: (b, 0, 0))] + _hbm_specs(n),
        out_shape=[mat, mat, mat, jax.ShapeDtypeStruct((N_BLOCKS, 1, STATE_W), F32)] + _gather_out_shapes(shards),
        scratch_shapes=[pltpu.VMEM((2, 2, _POW_ROWS, BLOCK_ST), F32), pltpu.VMEM((2, 2, BLOCK_CH, BLOCK_ST), F32),
                        pltpu.VMEM((2 * CHUNK, BLOCK_CH, BLOCK_CH), F32), pltpu.VMEM((2, CHUNK_W, BLOCK_ST), F32)]
        + _ssm_block_scratch() + _gather_semaphores(n),
        compiler_params=_cparams(("arbitrary", "arbitrary"), has_side_effects=True),
    )(*blk, *shards)
    return outs[:4], outs[4:]


def _ssm_chunk_matrices_bwd(blk, d_m, d_ws, d_wot, d_a16):
    def body(ar_ref, ai_ref, ldt_ref, brc_ref, bic_ref, crc_ref, cic_ref, d_ref, dm_ref, dws_ref, dwot_ref, da16_ref,
             dar_ref, dai_ref, dldt_ref, dbr_ref, dbi_ref, dcr_ref, dci_ref, dd_ref,
             pw_ref, bbar_ref, dlag_ref, dbbar_ref, dc_ref, dpw_ref, xs_ref, dts_ref, br_ref, bi_ref, cr_ref, ci_ref):
        j = pl.program_id(1)
        w = BLOCK_ST

        @pl.when(j == 0)
        def _():
            _ssm_expand_blocks((brc_ref, bic_ref, crc_ref, cic_ref), (br_ref, bi_ref, cr_ref, ci_ref))
            _ssm_fill_powers(ar_ref, ai_ref, ldt_ref, br_ref, bi_ref, pw_ref, bbar_ref)
            for r in (dlag_ref, dbbar_ref, dc_ref, dpw_ref):
                r[...] = jnp.zeros_like(r)

        def fold(t):
            return jnp.sum(t.reshape(BLOCK_CH // 8, 8, w), axis=0)

        def d_power(d, ri, t):
            return jnp.sum(dpw_ref[d, ri, t], axis=0, keepdims=True)

        def x_chain(d, t, dxr, dxi):
            pr, pi = pw_ref[d, 0, pl.ds(t, 1), :], pw_ref[d, 1, pl.ds(t, 1), :]
            bbr, bbi = bbar_ref[d, 0], bbar_ref[d, 1]
            dbbar_ref[d, 0] += dxr * pr + dxi * pi
            dbbar_ref[d, 1] += dxi * pr - dxr * pi
            dpw_ref[d, 0, t] += fold(dxr * bbr + dxi * bbi)
            dpw_ref[d, 1, t] += fold(dxi * bbr - dxr * bbi)

        def z_chain(d, t, dzr, dzi):
            pr, pi = pw_ref[d, 0, pl.ds(t, 1), :], pw_ref[d, 1, pl.ds(t, 1), :]
            c_r, c_i = cr_ref[d, 0], ci_ref[d, 0]
            dc_ref[d, 0] += dzr * pr - dzi * pi
            dc_ref[d, 1] += -dzr * pi - dzi * pr
            dpw_ref[d, 0, t] += fold(dzr * c_r - dzi * c_i)
            dpw_ref[d, 1, t] += fold(-dzr * c_i - dzi * c_r)

        for jp in range(CHUNK):
            dlag_ref[jp - j + CHUNK - 1] += dm_ref[0, :, jp * BLOCK_CH:(jp + 1) * BLOCK_CH].astype(F32)
        quarter = lambda ref, i: ref[0, :, i * w:(i + 1) * w].astype(F32)
        x_chain(0, CHUNK - 1 - j, quarter(dws_ref, 0), quarter(dws_ref, 1))
        x_chain(1, j, quarter(dws_ref, 2), quarter(dws_ref, 3))
        z_chain(0, j + 1, quarter(dwot_ref, 0), quarter(dwot_ref, 1))
        z_chain(1, CHUNK - j, quarter(dwot_ref, 2), quarter(dwot_ref, 3))

        @pl.when(j == CHUNK - 1)
        def _():
            for d in range(2):
                _ssm_stack_inputs(d, pw_ref, bbar_ref, xs_ref)
                for t in range(CHUNK):
                    dts_ref[t * BLOCK_CH:(t + 1) * BLOCK_CH, :] = dlag_ref[CHUNK - 1 + t if d == 0 else CHUNK - 1 - t]
                d_taps = dts_ref[...]
                dc_ref[d, 0] += _dot_rounded(d_taps, xs_ref[0], ((0,), (0,)))
                dc_ref[d, 1] -= _dot_rounded(d_taps, xs_ref[1], ((0,), (0,)))
                xs_ref[0] = _dot_rounded(d_taps, cr_ref[d, 0])
                xs_ref[1] = -_dot_rounded(d_taps, ci_ref[d, 0])
                for t in range(CHUNK):
                    rows = slice(t * BLOCK_CH, (t + 1) * BLOCK_CH)
                    x_chain(d, t, xs_ref[0, rows, :], xs_ref[1, rows, :])
            dd_ref[0] = jnp.sum(dlag_ref[CHUNK - 1] * _eye(BLOCK_CH), axis=0, keepdims=True)
            for d in range(2):
                (abr, abi, cfr, cfi), disc_vjp = jax.vjp(_ssm_discretise, ar_ref[d, 0], ai_ref[d, 0], ldt_ref[d, 0])
                dpr = d_power(d, 0, CHUNK) + da16_ref[0, :, 2 * d * w:(2 * d + 1) * w]
                dpi = d_power(d, 1, CHUNK) + da16_ref[0, :, (2 * d + 1) * w:(2 * d + 2) * w]
                dabr, dabi = jnp.zeros_like(abr), jnp.zeros_like(abi)
                for t in range(CHUNK, 0, -1):
                    qr, qi = pw_ref[d, 0, t - 1:t, :], pw_ref[d, 1, t - 1:t, :]
                    dabr = dabr + dpr * qr + dpi * qi
                    dabi = dabi + dpi * qr - dpr * qi
                    dpr, dpi = (dpr * abr + dpi * abi + d_power(d, 0, t - 1),
                                dpi * abr - dpr * abi + d_power(d, 1, t - 1))
                dbbr, dbbi = dbbar_ref[d, 0], dbbar_ref[d, 1]
                b_r, b_i = br_ref[d, 0], bi_ref[d, 0]
                dbr_ref[d, 0] = _ssm_collapse_block(cfr * dbbr + cfi * dbbi)
                dbi_ref[d, 0] = _ssm_collapse_block(cfr * dbbi - cfi * dbbr)
                dcfr = jnp.sum(b_r * dbbr + b_i * dbbi, axis=0, keepdims=True)
                dcfi = jnp.sum(b_r * dbbi - b_i * dbbr, axis=0, keepdims=True)
                dar_ref[d, 0], dai_ref[d, 0], dldt_ref[d, 0] = disc_vjp((dabr, dabi, dcfr, dcfi))
                dcr_ref[d, 0] = _ssm_collapse_block(dc_ref[d, 0])
                dci_ref[d, 0] = _ssm_collapse_block(dc_ref[d, 1])

    row = pl.BlockSpec((1, BLOCK_CH, CHUNK_W), lambda b, j: (b, j, 0))
    specs = _ssm_param_specs()
    acc = lambda *s: pltpu.VMEM(s, F32)
    return pl.pallas_call(
        body, name="ssm_chunk_matrices_bwd", grid=(N_BLOCKS, CHUNK),
        in_specs=specs + [row, row, row, pl.BlockSpec((1, 1, STATE_W), lambda b, j: (b, 0, 0))],
        out_specs=specs,
        out_shape=[jax.ShapeDtypeStruct(t.shape, F32) for t in blk],
        scratch_shapes=[acc(2, 2, _POW_ROWS, BLOCK_ST), acc(2, 2, BLOCK_CH, BLOCK_ST), acc(2 * CHUNK, BLOCK_CH, BLOCK_CH),
                        acc(2, 2, BLOCK_CH, BLOCK_ST), acc(2, 2, BLOCK_CH, BLOCK_ST), acc(2, 2, CHUNK + 1, 8, BLOCK_ST),
                        acc(2, CHUNK_W, BLOCK_ST), acc(CHUNK_W, BLOCK_CH)] + _ssm_block_scratch(),
        compiler_params=_cparams(("arbitrary", "arbitrary")),
    )(*blk, d_m, d_ws, d_wot, d_a16)


def _block_matmul(terms, name, out_dtype=F32, tn=1024):
    nc = terms[0][0].shape[1]
    n_out = terms[0][1].shape[1] if terms[0][2] else terms[0][1].shape[2]
    flags = [t[2] for t in terms]

    def body(*refs):
        out_ref = refs[-1]
        acc = None
        for t, transposed in enumerate(flags):
            a = refs[2 * t][0].astype(BF16)
            w = refs[2 * t + 1][0]
            part = _dot_nt(a, w) if transposed else _dot(a, w)
            acc = part if acc is None else acc + part
        out_ref[0] = acc.astype(out_dtype)

    in_specs, args = [], []
    for a, w, transposed in terms:
        k = a.shape[2]
        in_specs.append(pl.BlockSpec((1, nc, k), lambda b, n: (b, 0, 0)))
        if transposed:
            in_specs.append(pl.BlockSpec((1, tn, k), lambda b, n: (b, n, 0)))
        else:
            in_specs.append(pl.BlockSpec((1, k, tn), lambda b, n: (b, 0, n)))
        args += [a, w]
    return pl.pallas_call(
        body, name=name, grid=(N_BLOCKS, n_out // tn), in_specs=in_specs,
        out_specs=pl.BlockSpec((1, nc, tn), lambda b, n: (b, 0, n)),
        out_shape=jax.ShapeDtypeStruct((N_BLOCKS, nc, n_out), out_dtype),
        compiler_params=_cparams(("arbitrary", "arbitrary")),
    )(*args)


def _block_matmul_tn(a, b, name, tile=1024):
    nc, m = a.shape[1], a.shape[2]
    n = b.shape[2]

    def body(a_ref, b_ref, out_ref):
        out_ref[0] = _dot_tn(a_ref[0].astype(BF16), b_ref[0].astype(BF16)).astype(BF16)

    return pl.pallas_call(
        body, name=name, grid=(N_BLOCKS, m // tile, n // tile),
        in_specs=[pl.BlockSpec((1, nc, tile), lambda blk, i, j: (blk, 0, i)),
                  pl.BlockSpec((1, nc, tile), lambda blk, i, j: (blk, 0, j))],
        out_specs=pl.BlockSpec((1, tile, tile), lambda blk, i, j: (blk, i, j)),
        out_shape=jax.ShapeDtypeStruct((N_BLOCKS, m, n), BF16),
        compiler_params=_cparams(("arbitrary", "arbitrary", "arbitrary")),
    )(a, b)


def _cmul(ar, ai, xr, xi):
    return ar * xr - ai * xi, ar * xi + ai * xr


def _cmul_conj(ar, ai, xr, xi):
    return ar * xr + ai * xi, ar * xi - ai * xr


_SCAN_UNROLL = 8


def _ssm_state_scan(s_in, a16):
    nc = s_in.shape[1]
    w = BLOCK_ST

    def body(sin_ref, a_ref, out_ref):
        a = a_ref[0]
        afr, afi, abr, abi = a[:, 0:w], a[:, w:2 * w], a[:, 2 * w:3 * w], a[:, 3 * w:4 * w]

        def step(c, carry):
            fr, fi, br, bi = carry
            cb = nc - 1 - c
            out_ref[0, pl.ds(c, 1), 0:w] = fr
            out_ref[0, pl.ds(c, 1), w:2 * w] = fi
            out_ref[0, pl.ds(cb, 1), 2 * w:3 * w] = br
            out_ref[0, pl.ds(cb, 1), 3 * w:4 * w] = bi
            nfr, nfi = _cmul(afr, afi, fr, fi)
            nbr, nbi = _cmul(abr, abi, br, bi)
            return (nfr + sin_ref[0, pl.ds(c, 1), 0:w], nfi + sin_ref[0, pl.ds(c, 1), w:2 * w],
                    nbr + sin_ref[0, pl.ds(cb, 1), 2 * w:3 * w], nbi + sin_ref[0, pl.ds(cb, 1), 3 * w:4 * w])

        def steps(i, carry):
            for k in range(_SCAN_UNROLL):
                carry = step(i * _SCAN_UNROLL + k, carry)
            return carry

        z = jnp.zeros((1, w), F32)
        lax.fori_loop(0, nc // _SCAN_UNROLL, steps, (z, z, z, z))

    spec = pl.BlockSpec((1, nc, STATE_W), lambda b: (b, 0, 0))
    return pl.pallas_call(
        body, name="ssm_state_scan", grid=(N_BLOCKS,),
        in_specs=[spec, pl.BlockSpec((1, 1, STATE_W), lambda b: (b, 0, 0))],
        out_specs=spec, out_shape=jax.ShapeDtypeStruct(s_in.shape, F32),
        compiler_params=_cparams(("arbitrary",)),
    )(s_in, a16)


def _ssm_state_scan_bwd(d_prev, s_prev, a16):
    nc = d_prev.shape[1]
    w = BLOCK_ST

    def body(dp_ref, sp_ref, a_ref, g_ref, da_ref):
        a = a_ref[0]
        afr, afi, abr, abi = a[:, 0:w], a[:, w:2 * w], a[:, 2 * w:3 * w], a[:, 3 * w:4 * w]

        def step(i, carry):
            gfr, gfi, gbr, gbi, dafr, dafi, dabr, dabi = carry
            cf = nc - 1 - i
            cb = i
            g_ref[0, pl.ds(cf, 1), 0:w] = gfr
            g_ref[0, pl.ds(cf, 1), w:2 * w] = gfi
            g_ref[0, pl.ds(cb, 1), 2 * w:3 * w] = gbr
            g_ref[0, pl.ds(cb, 1), 3 * w:4 * w] = gbi
            sfr, sfi = sp_ref[0, pl.ds(cf, 1), 0:w], sp_ref[0, pl.ds(cf, 1), w:2 * w]
            sbr, sbi = sp_ref[0, pl.ds(cb, 1), 2 * w:3 * w], sp_ref[0, pl.ds(cb, 1), 3 * w:4 * w]
            dafr = dafr + gfr * sfr + gfi * sfi
            dafi = dafi + gfi * sfr - gfr * sfi
            dabr = dabr + gbr * sbr + gbi * sbi
            dabi = dabi + gbi * sbr - gbr * sbi
            nfr, nfi = _cmul_conj(afr, afi, gfr, gfi)
            nbr, nbi = _cmul_conj(abr, abi, gbr, gbi)
            return (nfr + dp_ref[0, pl.ds(cf, 1), 0:w], nfi + dp_ref[0, pl.ds(cf, 1), w:2 * w],
                    nbr + dp_ref[0, pl.ds(cb, 1), 2 * w:3 * w], nbi + dp_ref[0, pl.ds(cb, 1), 3 * w:4 * w],
                    dafr, dafi, dabr, dabi)

        def steps(i, carry):
            for k in range(_SCAN_UNROLL):
                carry = step(i * _SCAN_UNROLL + k, carry)
            return carry

        z = jnp.zeros((1, w), F32)
        res = lax.fori_loop(0, nc // _SCAN_UNROLL, steps, (z,) * 8)
        da_ref[0] = jnp.concatenate(res[4:], axis=1)

    spec = pl.BlockSpec((1, nc, STATE_W), lambda b: (b, 0, 0))
    aspec = pl.BlockSpec((1, 1, STATE_W), lambda b: (b, 0, 0))
    return pl.pallas_call(
        body, name="ssm_state_scan_bwd", grid=(N_BLOCKS,),
        in_specs=[spec, spec, aspec], out_specs=[spec, aspec],
        out_shape=[jax.ShapeDtypeStruct(d_prev.shape, F32), jax.ShapeDtypeStruct((N_BLOCKS, 1, STATE_W), F32)],
        compiler_params=_cparams(("arbitrary",)),
    )(d_prev, s_prev, a16)


NA_PAIR = 2 * GRID_W
NA_WIN_ROWS = NA_ROWS + 2
NA_WIN = NA_WIN_ROWS * GRID_W
NA_PAIRS_PER_STEP = 8
NA_CASES = 5
NA_MASKED = -1e30


def _na_pair_window(m, rows):
    rs0 = jnp.clip(2 * m - NA_ROWS // 2, 0, rows - NA_ROWS)
    ws = jnp.minimum(rs0, rows - NA_WIN_ROWS)
    last = rows // 2 - 1
    case = jnp.where(m == 0, 0, jnp.where(m == 1, 1, jnp.where(m == last - 1, 3, jnp.where(m == last, 4, 2))))
    return ws, case


def _na_row_offsets(rows):
    last = rows // 2 - 1
    geom = []
    for m in (0, 1, 2, last - 1, last):
        ws = min(max(2 * m - NA_ROWS // 2, 0), rows - NA_ROWS, rows - NA_WIN_ROWS)
        per_case = []
        for i in range(NA_WIN_ROWS):
            pair = []
            for rr in range(2):
                r = 2 * m + rr
                rs = min(max(r - NA_ROWS // 2, 0), rows - NA_ROWS)
                pair.append(ws + i - r + NA_ROWS - 1 if rs <= ws + i < rs + NA_ROWS else None)
            per_case.append(pair)
        geom.append(per_case)
    return geom


def _na_col_select():
    qc = np.arange(NA_PAIR)[None, :] % GRID_W
    kc = np.arange(GRID_W)[:, None]
    dc = np.clip(kc - qc + NA_COLS - 1, 0, 2 * NA_COLS - 2)
    return jnp.asarray((np.arange(2 * NA_COLS - 1)[:, None, None] == dc[None]).astype(np.float32))


def _na_bias_rows(rpb):
    return jnp.einsum("hrd,dkl->hrkl", rpb, _na_col_select(), precision=HIGHEST)


def _na_col_window():
    qc = lax.broadcasted_iota(jnp.int32, (GRID_W, NA_PAIR), 1) % GRID_W
    kc = lax.broadcasted_iota(jnp.int32, (GRID_W, NA_PAIR), 0)
    cs = jnp.clip(qc - NA_COLS // 2, 0, GRID_W - NA_COLS)
    first_row = lax.broadcasted_iota(jnp.int32, (GRID_W, NA_PAIR), 1) < GRID_W
    return (kc >= cs) & (kc < cs + NA_COLS), first_row


def _na_bias_table(bias_rows, rows):
    geom = _na_row_offsets(rows)

    def body(br_ref, tab_ref):
        col_ok, first_row = _na_col_window()
        masked = jnp.full((GRID_W, NA_PAIR), NA_MASKED, F32)
        for case in range(NA_CASES):
            for i in range(NA_WIN_ROWS):
                d0, d1 = geom[case][i]
                t0 = masked if d0 is None else br_ref[0, d0]
                t1 = masked if d1 is None else br_ref[0, d1]
                tile = jnp.where(col_ok, jnp.where(first_row, t0, t1), NA_MASKED)
                tab_ref[0, case, i * GRID_W:(i + 1) * GRID_W, :] = tile

    return pl.pallas_call(
        body, name="na_bias_table", grid=(NA_HEADS,),
        in_specs=[pl.BlockSpec((1, 2 * NA_ROWS - 1, GRID_W, NA_PAIR), lambda h: (h, 0, 0, 0))],
        out_specs=pl.BlockSpec((1, NA_CASES, NA_WIN, NA_PAIR), lambda h: (h, 0, 0, 0)),
        out_shape=jax.ShapeDtypeStruct((NA_HEADS, NA_CASES, NA_WIN, NA_PAIR), F32),
        compiler_params=_cparams(("arbitrary",)),
    )(bias_rows)


def _na_bias_table_bwd(d_tab, rows):
    geom = _na_row_offsets(rows)

    def body(dt_ref, dbr_ref):
        col_ok, first_row = _na_col_window()
        acc = [None] * (2 * NA_ROWS - 1)
        for case in range(NA_CASES):
            for i in range(NA_WIN_ROWS):
                tile = jnp.where(col_ok, dt_ref[0, case, i * GRID_W:(i + 1) * GRID_W, :], 0.0)
                for rr, d in enumerate(geom[case][i]):
                    if d is not None:
                        part = jnp.where(first_row if rr == 0 else ~first_row, tile, 0.0)
                        acc[d] = part if acc[d] is None else acc[d] + part
        for d, a in enumerate(acc):
            dbr_ref[0, d] = jnp.zeros((GRID_W, NA_PAIR), F32) if a is None else a

    return pl.pallas_call(
        body, name="na_bias_table_bwd", grid=(NA_HEADS,),
        in_specs=[pl.BlockSpec((1, NA_CASES, NA_WIN, NA_PAIR), lambda h: (h, 0, 0, 0))],
        out_specs=pl.BlockSpec((1, 2 * NA_ROWS - 1, GRID_W, NA_PAIR), lambda h: (h, 0, 0, 0)),
        out_shape=jax.ShapeDtypeStruct((NA_HEADS, 2 * NA_ROWS - 1, GRID_W, NA_PAIR), F32),
        compiler_params=_cparams(("arbitrary",)),
    )(d_tab)


NA_BLK = 64


def _na_blocks():
    return [slice(i * NA_BLK, (i + 1) * NA_BLK) for i in range(NA_WIN // NA_BLK)]


def _na_softmax(qk, bias_ref, hh, case):
    m = jnp.full((NA_BLK, NA_PAIR), -jnp.inf, F32)
    scores = []
    for blk in _na_blocks():
        s = qk[blk, :] + bias_ref[hh, case, blk, :]
        scores.append(s)
        m = jnp.maximum(m, s)
    m = jnp.max(m, axis=0, keepdims=True)
    l = jnp.zeros((NA_BLK, NA_PAIR), F32)
    exps = []
    for s in scores:
        e = jnp.exp(s - m)
        exps.append(e)
        l = l + e
    return exps, jnp.sum(l, axis=0, keepdims=True)


def _na_units(step, rows):
    units = []
    for pp in range(NA_PAIRS_PER_STEP):
        ws, case = _na_pair_window(step * NA_PAIRS_PER_STEP + pp, rows)
        win = pl.ds(pl.multiple_of(ws * GRID_W, NA_PAIR), NA_WIN)
        lanes = slice(pp * NA_PAIR, (pp + 1) * NA_PAIR)
        for hh in range(2):
            units.append((pp, hh, case, win, lanes, slice(hh * NA_HEAD_DIM, (hh + 1) * NA_HEAD_DIM)))
    return units


def _na_pipeline(n, before, middle, after, lookahead):
    for u in range(min(lookahead, n)):
        for f in before:
            f(u)
    for u in range(n):
        middle(u)
        if u + lookahead < n:
            for f in before:
                f(u + lookahead)
        for f in after:
            f(u)


def _head_rows(t, hh):
    row_head = lax.broadcasted_iota(jnp.int32, t.shape, 0) // NA_HEAD_DIM
    return jnp.where(row_head == hh, t, jnp.zeros_like(t))


def _heads_block_diag(t):
    lane_head = lax.broadcasted_iota(jnp.int32, t.shape, 1) // NA_HEAD_DIM
    zero = jnp.zeros_like(t)
    return jnp.concatenate([jnp.where(lane_head == 0, t, zero), jnp.where(lane_head == 1, t, zero)], axis=0)


def _na_fwd(q_t, k, v_t, bias_tab):
    L = k.shape[0]
    rows = L // GRID_W
    step_w = NA_PAIRS_PER_STEP * NA_PAIR

    def body(q_ref, k_ref, v_ref, bt_ref, o_ref):
        units = _na_units(pl.program_id(1), rows)
        qk, probs = {}, {}

        def scores(u):
            _, hh, _, win, lanes, _ = units[u]
            qk[u] = _dot(k_ref[win, :], _head_rows(q_ref[:, lanes], hh))

        def softmax(u):
            _, hh, case, _, _, _ = units[u]
            exps, l = _na_softmax(qk.pop(u), bt_ref, hh, case)
            probs[u] = jnp.concatenate([t.astype(BF16) for t in exps], axis=0), l

        def output(u):
            _, _, _, win, lanes, hrows = units[u]
            e, l = probs.pop(u)
            o_ref[hrows, lanes] = _dot(v_ref[hrows, win], e) / l

        _na_pipeline(len(units), [scores], softmax, [output], lookahead=3)

    q_spec = pl.BlockSpec((NA_PAIR, step_w), lambda h, s: (h, s))
    return pl.pallas_call(
        body, name="na_fwd", grid=(NA_HEADS // 2, L // step_w),
        in_specs=[q_spec, pl.BlockSpec((L, NA_PAIR), lambda h, s: (0, h)),
                  pl.BlockSpec((NA_PAIR, L), lambda h, s: (h, 0)),
                  pl.BlockSpec((2, NA_CASES, NA_WIN, NA_PAIR), lambda h, s: (h, 0, 0, 0))],
        out_specs=q_spec,
        out_shape=jax.ShapeDtypeStruct((D_NA, L), F32),
        compiler_params=_cparams(("arbitrary", "arbitrary")),
    )(q_t, k, v_t, bias_tab)


def _na_bwd(q_t, q, k_t, k, v, bias_tab, out_t, d_out_t, d_out):
    L = k.shape[0]
    rows = L // GRID_W
    step_w = NA_PAIRS_PER_STEP * NA_PAIR

    def body(qt_ref, q_ref, kt_ref, k_ref, v_ref, bt_ref, ot_ref, dot_ref, do_ref, dq_ref, dk_ref, dv_ref, dbt_ref):
        @pl.when(pl.program_id(1) == 0)
        def _():
            dk_ref[...] = jnp.zeros_like(dk_ref)
            dv_ref[...] = jnp.zeros_like(dv_ref)
            dbt_ref[...] = jnp.zeros_like(dbt_ref)

        units = _na_units(pl.program_id(1), rows)
        qk, dp, dsb, pb = {}, {}, {}, {}

        def scores(u):
            _, hh, _, win, lanes, _ = units[u]
            qk[u] = _dot(k_ref[win, :], _head_rows(qt_ref[:, lanes], hh))

        def d_probs(u):
            _, hh, _, win, lanes, _ = units[u]
            dp[u] = _dot(v_ref[win, :], _head_rows(dot_ref[:, lanes].astype(BF16), hh))

        def softmax_bwd(u):
            _, hh, case, _, lanes, hrows = units[u]
            exps, l = _na_softmax(qk.pop(u), bt_ref, hh, case)
            inv_l = 1.0 / l
            delta = jnp.sum(dot_ref[hrows, lanes] * ot_ref[hrows, lanes], axis=0, keepdims=True)
            d_p = dp.pop(u)
            ds_blocks, p_blocks = [], []
            for blk, e in zip(_na_blocks(), exps):
                p = e * inv_l
                ds = p * (d_p[blk, :] - delta)
                dbt_ref[hh, case, blk, :] += ds
                ds_blocks.append(ds.astype(BF16))
                p_blocks.append(p.astype(BF16))
            dsb[u] = jnp.concatenate(ds_blocks, axis=0)
            pb[u] = jnp.concatenate(p_blocks, axis=0)

        def d_query(u):
            _, _, _, win, lanes, hrows = units[u]
            dq_ref[hrows, lanes] = _dot(kt_ref[hrows, win], dsb[u]) * (NA_HEAD_DIM ** -0.5)

        def d_keys_values(u):
            pp, hh, _, win, _, _ = units[u]
            if hh == 1:
                tokens = slice(pp * NA_PAIR, (pp + 1) * NA_PAIR)
                dk_ref[win, :] += _dot(jnp.concatenate([dsb.pop(u - 1), dsb.pop(u)], axis=1), _heads_block_diag(q_ref[tokens, :]))
                dv_ref[win, :] += _dot(jnp.concatenate([pb.pop(u - 1), pb.pop(u)], axis=1), _heads_block_diag(do_ref[tokens, :]))

        _na_pipeline(len(units), [scores, d_probs], softmax_bwd, [d_query, d_keys_values], lookahead=2)

    t_tile = pl.BlockSpec((NA_PAIR, step_w), lambda h, s: (h, s))
    tile = pl.BlockSpec((step_w, NA_PAIR), lambda h, s: (s, h))
    t_full = pl.BlockSpec((NA_PAIR, L), lambda h, s: (h, 0))
    full = pl.BlockSpec((L, NA_PAIR), lambda h, s: (0, h))
    bt = pl.BlockSpec((2, NA_CASES, NA_WIN, NA_PAIR), lambda h, s: (h, 0, 0, 0))
    tok = jax.ShapeDtypeStruct((L, D_NA), F32)
    return pl.pallas_call(
        body, name="na_bwd", grid=(NA_HEADS // 2, L // step_w),
        in_specs=[t_tile, tile, t_full, full, full, bt, t_tile, t_tile, tile],
        out_specs=[t_tile, full, full, bt],
        out_shape=[jax.ShapeDtypeStruct((D_NA, L), F32), tok, tok, jax.ShapeDtypeStruct(bias_tab.shape, F32)],
        compiler_params=_cparams(("arbitrary", "arbitrary")),
    )(q_t, q, k_t, k, v, bias_tab, out_t, d_out_t, d_out)


def _branch_fwd_values(ys, zs, yn, zn, wglu, bglu):
    g1, t = _gelu_parts(ys)
    lin = _dot(g1.astype(BF16), wglu) + bglu
    sg = _sigmoid(lin)
    ys2 = g1 * sg
    sz, szs = _silu_parts(zs)
    sn, sns = _silu_parts(zn)
    return g1, t, sg, ys2, sz, szs, sn, sns


def _branch_fwd(y_ssm_c, z_s, y_na_t, z_n, w_glu, b_glu, tm=512):
    L = z_s.shape[0]

    def body(ys_ref, zs_ref, yn_ref, zn_ref, w_ref, b_ref, cat_ref, scr):
        yn = yn_ref[...].T
        g1, t, sg, ys2, sz, szs, sn, sns = _branch_fwd_values(
            _load_chunks(ys_ref, scr), zs_ref[...], yn, zn_ref[...], w_ref[...], b_ref[...])
        cat_ref[:, 0:512] = (ys2 * sz).astype(BF16)
        cat_ref[:, 512:1024] = (yn * sn).astype(BF16)

    tile = pl.BlockSpec((tm, 512), lambda i: (i, 0))
    return pl.pallas_call(
        body, name="branch_fwd", grid=(L // tm,),
        in_specs=[_chunk_spec(tm), tile, _heads_t_spec(tm), tile, pl.BlockSpec((512, 512), lambda i: (0, 0)),
                  pl.BlockSpec((1, 512), lambda i: (0, 0))],
        out_specs=pl.BlockSpec((tm, 1024), lambda i: (i, 0)),
        out_shape=jax.ShapeDtypeStruct((L, 1024), BF16),
        scratch_shapes=[_chunk_scratch(tm)],
        compiler_params=_cparams(("arbitrary",)),
    )(y_ssm_c, z_s, y_na_t, z_n, w_glu, b_glu)


def _branch_bwd(y_ssm_c, z_s, y_na_t, z_n, w_glu, b_glu, d_cat, tm=512):
    L = z_s.shape[0]

    def body(ys_ref, zs_ref, yn_ref, zn_ref, w_ref, b_ref, dc_ref,
             dys_ref, dzs_ref, dynt_ref, dyn_ref, dzn_ref, dw_ref, db_ref, scr):
        @pl.when(pl.program_id(0) == 0)
        def _():
            dw_ref[...] = jnp.zeros_like(dw_ref)
            db_ref[...] = jnp.zeros_like(db_ref)

        ys, zs, yn, zn = _load_chunks(ys_ref, scr), zs_ref[...], yn_ref[...].T, zn_ref[...]
        w = w_ref[...]
        g1, t, sg, ys2, sz, szs, sn, sns = _branch_fwd_values(ys, zs, yn, zn, w, b_ref[...])
        dys3 = dc_ref[:, 0:512]
        dyn2 = dc_ref[:, 512:1024]
        dzs_ref[...] = (dys3 * ys2 * _silu_grad(zs, szs)).astype(BF16)
        dys2 = dys3 * sz
        dlin = dys2 * g1 * sg * (1.0 - sg)
        dlb = dlin.astype(BF16)
        dg1 = dys2 * sg + _dot_nt(dlb, w)
        dw_ref[...] += _dot_tn(g1.astype(BF16), dlb)
        db_ref[...] += jnp.sum(dlin, axis=0, keepdims=True)
        _store_chunks(dg1 * _gelu_grad(ys, t), scr, dys_ref, BF16)
        dyn = dyn2 * sn
        dynt_ref[...] = dyn.T
        dyn_ref[...] = dyn.astype(BF16)
        dzn_ref[...] = (dyn2 * yn * _silu_grad(zn, sns)).astype(BF16)

    tile = pl.BlockSpec((tm, 512), lambda i: (i, 0))
    wspec = pl.BlockSpec((512, 512), lambda i: (0, 0))
    bspec = pl.BlockSpec((1, 512), lambda i: (0, 0))
    tok = jax.ShapeDtypeStruct((L, 512), BF16)
    return pl.pallas_call(
        body, name="branch_bwd", grid=(L // tm,),
        in_specs=[_chunk_spec(tm), tile, _heads_t_spec(tm), tile, wspec, bspec, pl.BlockSpec((tm, 1024), lambda i: (i, 0))],
        out_specs=[_chunk_spec(tm), tile, _heads_t_spec(tm), tile, tile, wspec, bspec],
        out_shape=[jax.ShapeDtypeStruct((N_BLOCKS, L // CHUNK, CHUNK_W), BF16), tok, jax.ShapeDtypeStruct((D_NA, L), F32),
                   tok, tok,
                   jax.ShapeDtypeStruct((512, 512), F32), jax.ShapeDtypeStruct((1, 512), F32)],
        scratch_shapes=[_chunk_scratch(tm)],
        compiler_params=_cparams(("arbitrary",)),
    )(y_ssm_c, z_s, y_na_t, z_n, w_glu, b_glu, d_cat)


def _head(x, p, target, cat, w_out, g_post, w_ple_g, g_ple, w_pg, tm=512):
    L = x.shape[0]
    pw = w_ple_g.shape[2]

    def body(x_ref, p_ref, t_ref, cat_ref, wo_ref, gpo_ref, wp_ref, gpl_ref, wg_ref,
             loss_ref, dh1_ref, dcat_ref, dwo_ref, dgpo_ref, dwp_ref, dgpl_ref, dwg_ref):
        @pl.when(pl.program_id(0) == 0)
        def _():
            for r in (loss_ref, dwo_ref, dgpo_ref, dwp_ref, dgpl_ref, dwg_ref):
                r[...] = jnp.zeros_like(r)

        cat_b = cat_ref[...]
        wo, wg = wo_ref[...], wg_ref[...]
        g_po, g_pl = gpo_ref[...], gpl_ref[...]
        mix = _dot(cat_b, wo)
        p_b = p_ref[...].astype(BF16)
        ep = jnp.concatenate([_dot(p_b, wp_ref[j]) for j in range(N_CHIPS)], axis=1)
        nm, r2 = _rms(mix)
        h1 = x_ref[...] + nm * g_po
        ne, r3 = _rms(ep)
        e = ne * g_pl
        h1_b = h1.astype(BF16)
        gate = _sigmoid(_dot(h1_b, wg))
        h2 = h1 + gate * e
        diff = h2 - t_ref[...]
        loss_ref[...] += (0.5 / D_MODEL) * jnp.sum(diff * diff).reshape(1, 1)

        dh2 = diff * (1.0 / D_MODEL)
        de = dh2 * gate
        dgl = (dh2 * e * gate * (1.0 - gate)).astype(BF16)
        dh1 = dh2 + _dot_nt(dgl, wg)
        dwg_ref[...] += _dot_tn(h1_b, dgl)
        dgpo_ref[...] += jnp.sum(dh1 * nm, axis=0, keepdims=True)
        dmix = _rms_bwd(dh1 * g_po, nm, r2).astype(BF16)
        dcat_ref[...] = _dot_nt(dmix, wo)
        dwo_ref[...] += _dot_tn(cat_b, dmix)
        dh1_ref[...] = dh1
        dgpl_ref[...] += jnp.sum(de * ne, axis=0, keepdims=True)
        dep = _rms_bwd(de * g_pl, ne, r3).astype(BF16)
        for j in range(N_CHIPS):
            dwp_ref[j] += _dot_tn(p_b, dep[:, j * pw:(j + 1) * pw])

    tile = lambda w: pl.BlockSpec((tm, w), lambda i: (i, 0))
    const = _resident
    sds = jax.ShapeDtypeStruct
    return pl.pallas_call(
        body, name="head", grid=(L // tm,),
        in_specs=[tile(D_MODEL), tile(D_PLE), tile(D_MODEL), tile(1024), const(1024, D_MODEL), const(1, D_MODEL),
                  const(N_CHIPS, D_PLE, pw), const(1, D_MODEL), const(D_MODEL, D_MODEL)],
        out_specs=[const(1, 1), tile(D_MODEL), tile(1024), const(1024, D_MODEL), const(1, D_MODEL),
                   const(N_CHIPS, D_PLE, pw), const(1, D_MODEL), const(D_MODEL, D_MODEL)],
        out_shape=[sds((1, 1), F32), sds((L, D_MODEL), F32), sds((L, 1024), F32), sds((1024, D_MODEL), F32),
                   sds((1, D_MODEL), F32), sds((N_CHIPS, D_PLE, pw), F32), sds((1, D_MODEL), F32),
                   sds((D_MODEL, D_MODEL), F32)],
        compiler_params=_cparams(("arbitrary",)),
    )(x, p, target, cat, w_out, g_post, w_ple_g, g_ple, w_pg)


def _dproj_specs(tm):
    tile = pl.BlockSpec((tm, 512), lambda i: (i, 0))
    return [_chunk_spec(tm), tile, _heads_t_spec(tm), tile, tile, tile]


_DPROJ_ORDER = (3, 4, 5, 1, 2, 0)


def _dproj_part(refs, scr, i):
    if i == 0:
        val = _load_chunks(refs[0], scr)
    elif i == 2:
        val = refs[2][...].T
    else:
        val = refs[i][...]
    return val.astype(BF16)


def _dproj_pieces(i, wn):
    lo, hi = 512 * i, 512 * (i + 1)
    pieces = []
    for j in range(N_CHIPS):
        a, b = max(lo, j * wn), min(hi, (j + 1) * wn)
        if a < b:
            pieces.append((j, slice(a - j * wn, b - j * wn), slice(a - lo, b - lo)))
    return pieces


def _in_proj_bwd_w(x, g_col, w_in_g, dparts, tm=512):
    L = x.shape[0]
    wn = D_IN_PROJ // N_CHIPS
    steps = L // tm

    def body(x_ref, g_ref, w_ref, *refs):
        dw_ref, dg_ref, scr = refs[-3], refs[-2], refs[-1]

        @pl.when(pl.program_id(0) == 0)
        def _():
            dw_ref[...] = jnp.zeros_like(dw_ref)

        n, _ = _rms(x_ref[...])
        nb = n.astype(BF16)
        for i in _DPROJ_ORDER:
            part = _dproj_part(refs[:-3], scr, i)
            for j, w_cols, p_cols in _dproj_pieces(i, wn):
                dw_ref[j, :, w_cols] += _dot_tn(nb, part[:, p_cols])

        @pl.when(pl.program_id(0) == steps - 1)
        def _():
            g = g_ref[...]
            dg = jnp.zeros_like(g)
            for j in range(N_CHIPS):
                a = dw_ref[j]
                dg = dg + jnp.sum(a * w_ref[j].astype(F32), axis=1, keepdims=True)
                dw_ref[j] = a * g
            dg_ref[...] = dg

    return pl.pallas_call(
        body, name="in_proj_bwd_w", grid=(steps,),
        in_specs=[pl.BlockSpec((tm, D_MODEL), lambda i: (i, 0)), _resident(D_MODEL, 1), _resident(N_CHIPS, D_MODEL, wn)]
        + _dproj_specs(tm),
        out_specs=[_resident(N_CHIPS, D_MODEL, wn), _resident(D_MODEL, 1)],
        out_shape=[jax.ShapeDtypeStruct((N_CHIPS, D_MODEL, wn), F32), jax.ShapeDtypeStruct((D_MODEL, 1), F32)],
        scratch_shapes=[_chunk_scratch(tm)],
        compiler_params=_cparams(("arbitrary",)),
    )(x, g_col, w_in_g, *dparts)


def _in_proj_bwd_x(x, g_pre, w_in_g, d_h1, dparts, pair_sums, tm=512):
    L = x.shape[0]
    wn = w_in_g.shape[2]
    n_ps = len(pair_sums)
    steps = L // tm

    def body(*refs):
        x_ref, g_ref, w_ref, dh1_ref = refs[:4]
        dparts_refs = refs[4:10]
        dx_ref = refs[10 + n_ps]
        scr = refs[11 + 2 * n_ps]
        scatter = _ChipScatter(refs[10:10 + n_ps], refs[11 + n_ps:11 + 2 * n_ps], refs[12 + 2 * n_ps:16 + 2 * n_ps],
                               refs[16 + 2 * n_ps:])
        pl.when(pl.program_id(0) == 0)(scatter.start)
        pl.when(pl.program_id(0) == steps - 1)(scatter.finish)

        dhn = None
        for i in _DPROJ_ORDER:
            part = _dproj_part(dparts_refs, scr, i)
            for j, w_cols, p_cols in _dproj_pieces(i, wn):
                term = _dot_nt(part[:, p_cols], w_ref[j, :, w_cols])
                dhn = term if dhn is None else dhn + term
        n, r = _rms(x_ref[...])
        dx_ref[...] = dh1_ref[...] + _rms_bwd(dhn * g_ref[...], n, r)

    wide = pl.BlockSpec((tm, D_MODEL), lambda i: (i, 0))
    outs = pl.pallas_call(
        body, name="in_proj_bwd_x", grid=(steps,),
        in_specs=[wide, _resident(1, D_MODEL), _resident(N_CHIPS, D_MODEL, wn), wide] + _dproj_specs(tm) + _hbm_specs(n_ps),
        out_specs=[wide] + _hbm_specs(n_ps),
        out_shape=[jax.ShapeDtypeStruct((L, D_MODEL), F32)] + [jax.ShapeDtypeStruct(p.shape, p.dtype) for p in pair_sums],
        scratch_shapes=[_chunk_scratch(tm)] + _scatter_scratch(pair_sums),
        compiler_params=_cparams(("arbitrary",), has_side_effects=True),
    )(x, g_pre, w_in_g, d_h1, *dparts, *pair_sums)
    return outs[0], outs[1:]


def _mesh_position():
    x, y, c = lax.axis_index("x"), lax.axis_index("y"), lax.axis_index("c")
    chips = [(1 - x, y), (x, 1 - y), (1 - x, 1 - y)]
    return x, y, c, chips


def _chip_index(cx, cy):
    return 2 * cx + cy


def _hbm_specs(n):
    return [pl.BlockSpec(memory_space=pl.ANY)] * n


def _gather_chips(shards, name):
    n = len(shards)

    def body(*refs):
        gather = _ChipGather(refs[:n], refs[n:2 * n], refs[2 * n:])
        gather.start()
        gather.forward()
        gather.finish()

    return pl.pallas_call(
        body, name=name, in_specs=_hbm_specs(n), out_specs=_hbm_specs(n),
        out_shape=_gather_out_shapes(shards), scratch_shapes=_gather_semaphores(n),
        compiler_params=pltpu.CompilerParams(has_side_effects=True),
    )(*shards)


def _gather_out_shapes(shards):
    return [jax.ShapeDtypeStruct((N_CHIPS,) + s.shape, s.dtype) for s in shards]


def _gather_semaphores(n):
    sem = pltpu.SemaphoreType.DMA
    return [sem((n, 3)), sem((n, 3)), sem((n, 3)), sem((n, 3)), sem((n,)), sem((n,))]


class _ChipGather:
    def __init__(self, ins, outs, sems):
        self.ins, self.outs = ins, outs
        self.send1, self.recv1, self.send2, self.recv2, self.send3, self.recv3 = sems
        self.x, self.y, self.c, self.chips = _mesh_position()
        self.me = _chip_index(self.x, self.y)
        self.sibling = (self.x, self.y, 1 - self.c)

    def _half(self, a, chip, core):
        hr = self.outs[a].shape[1] // 2
        return self.outs[a].at[chip, pl.ds(core * hr, hr)]

    def _own(self, a):
        return pltpu.make_async_remote_copy(
            src_ref=self.ins[a], dst_ref=self.outs[a].at[self.me], send_sem=self.send3.at[a], recv_sem=self.recv3.at[a],
            device_id=self.sibling, device_id_type=MESH)

    def _to_chip(self, a, j):
        hr = self.ins[a].shape[0] // 2
        return pltpu.make_async_remote_copy(
            src_ref=self.ins[a].at[pl.ds(self.c * hr, hr)], dst_ref=self._half(a, self.me, self.c),
            send_sem=self.send1.at[a, j], recv_sem=self.recv1.at[a, j], device_id=(*self.chips[j], self.c), device_id_type=MESH)

    def _from_chip(self, a, j):
        landed = self._half(a, _chip_index(*self.chips[j]), self.c)
        return pltpu.make_async_remote_copy(
            src_ref=landed, dst_ref=landed, send_sem=self.send1.at[a, j], recv_sem=self.recv1.at[a, j],
            device_id=(*self.chips[j], self.c), device_id_type=MESH)

    def _to_sibling(self, a, j, core):
        part = self._half(a, _chip_index(*self.chips[j]), core)
        return pltpu.make_async_remote_copy(
            src_ref=part, dst_ref=part, send_sem=self.send2.at[a, j], recv_sem=self.recv2.at[a, j],
            device_id=self.sibling, device_id_type=MESH)

    def _each(self):
        return [(a, j) for a in range(len(self.ins)) for j in range(3)]

    def start(self):
        for a in range(len(self.ins)):
            self._own(a).start()
        for a, j in self._each():
            self._to_chip(a, j).start()

    def forward(self):
        for a, j in self._each():
            self._from_chip(a, j).wait_recv()
            self._to_sibling(a, j, self.c).start()

    def finish(self):
        for a, j in self._each():
            self._to_sibling(a, j, 1 - self.c).wait_recv()
        for a, j in self._each():
            self._to_chip(a, j).wait_send()
            self._to_sibling(a, j, self.c).wait_send()
        for a in range(len(self.ins)):
            self._own(a).wait()


def _pair_exchange(grads):
    n = len(grads)

    def body(*refs):
        ins, outs = refs[:n], refs[n:2 * n]
        send, recv = refs[2 * n:]
        x, y, c, _ = _mesh_position()
        copies = []
        for a in range(n):
            hr = ins[a].shape[1] // 2
            cp = pltpu.make_async_remote_copy(
                src_ref=ins[a].at[:, pl.ds((1 - c) * hr, hr)], dst_ref=outs[a],
                send_sem=send.at[a], recv_sem=recv.at[a], device_id=(x, y, 1 - c), device_id_type=MESH)
            cp.start()
            copies.append(cp)
        for cp in copies:
            cp.wait()

    sem = pltpu.SemaphoreType.DMA
    return pl.pallas_call(
        body, name="pair_exchange", in_specs=_hbm_specs(n), out_specs=_hbm_specs(n),
        out_shape=[jax.ShapeDtypeStruct((g.shape[0], g.shape[1] // 2, g.shape[2]), g.dtype) for g in grads],
        scratch_shapes=[sem((n,)), sem((n,))],
        compiler_params=pltpu.CompilerParams(has_side_effects=True),
    )(*grads)


def _pair_add(core, grad, other, tr, out_dtype):
    hr = other.shape[1]
    cdim = other.shape[2]
    nb = hr // tr

    def body(core_ref, g_ref, o_ref, out_ref):
        out_ref[...] = (g_ref[...] + o_ref[...]).astype(out_dtype)

    return pl.pallas_call(
        body, name="pair_add",
        grid_spec=pltpu.PrefetchScalarGridSpec(
            num_scalar_prefetch=1, grid=(N_CHIPS, nb),
            in_specs=[pl.BlockSpec((1, tr, cdim), lambda j, i, core_ref: (j, core_ref[0] * nb + i, 0)),
                      pl.BlockSpec((1, tr, cdim), lambda j, i, core_ref: (j, i, 0))],
            out_specs=pl.BlockSpec((1, tr, cdim), lambda j, i, core_ref: (j, i, 0))),
        out_shape=jax.ShapeDtypeStruct(other.shape, out_dtype),
        compiler_params=_cparams(("arbitrary", "arbitrary")),
    )(core, grad, other)


def _scatter_scratch(parts):
    sem = pltpu.SemaphoreType.DMA
    n = len(parts)
    return [sem((n, 3)), sem((n, 3)), sem((n,)), sem((n,))] + [pltpu.VMEM(p.shape[1:], p.dtype) for p in parts]


class _ChipScatter:
    def __init__(self, ins, outs, sems, staged):
        self.ins, self.outs, self.staged = ins, outs, staged
        self.send, self.recv, self.load_sem, self.store_sem = sems
        self.x, self.y, self.c, self.chips = _mesh_position()
        self.me = _chip_index(self.x, self.y)

    def _load(self, a):
        return pltpu.make_async_copy(self.ins[a].at[self.me], self.staged[a], self.load_sem.at[a])

    def _store(self, a):
        return pltpu.make_async_copy(self.staged[a], self.outs[a].at[self.me], self.store_sem.at[a])

    def _to_chip(self, a, j):
        return pltpu.make_async_remote_copy(
            src_ref=self.ins[a].at[_chip_index(*self.chips[j])], dst_ref=self.outs[a].at[self.me],
            send_sem=self.send.at[a, j], recv_sem=self.recv.at[a, j], device_id=(*self.chips[j], self.c), device_id_type=MESH)

    def start(self):
        for a in range(len(self.ins)):
            self._load(a).start()
            for j in range(3):
                self._to_chip(a, j).start()

    def finish(self):
        for a in range(len(self.ins)):
            self._load(a).wait()
            self._store(a).start()
        for a in range(len(self.ins)):
            for j in range(3):
                self._to_chip(a, j).wait()
            self._store(a).wait()


def _chip_add(core, recv, tr):
    hr, cdim = recv.shape[1], recv.shape[2]
    nb = hr // tr

    def body(core_ref, r_ref, out_ref):
        out_ref[...] = ((r_ref[0].astype(F32) + r_ref[1].astype(F32)) + r_ref[2].astype(F32)) + r_ref[3].astype(F32)

    return pl.pallas_call(
        body, name="chip_add",
        grid_spec=pltpu.PrefetchScalarGridSpec(
            num_scalar_prefetch=1, grid=(nb,),
            in_specs=[pl.BlockSpec((N_CHIPS, tr, cdim), lambda i, core_ref: (0, i, 0))],
            out_specs=pl.BlockSpec((tr, cdim), lambda i, core_ref: (core_ref[0] * nb + i, 0))),
        out_shape=jax.ShapeDtypeStruct((2 * hr, cdim), F32),
        compiler_params=_cparams(("arbitrary",)),
    )(core, recv)


def _pair_gather(fulls):
    n = len(fulls)

    def body(*refs):
        outs = refs[n:2 * n]
        send, recv = refs[2 * n:]
        x, y, c, _ = _mesh_position()
        copies = []
        for a in range(n):
            hr = outs[a].shape[0] // 2
            mine = outs[a].at[pl.ds(c * hr, hr)]
            cp = pltpu.make_async_remote_copy(
                src_ref=mine, dst_ref=mine, send_sem=send.at[a], recv_sem=recv.at[a],
                device_id=(x, y, 1 - c), device_id_type=MESH)
            cp.start()
            copies.append(cp)
        for cp in copies:
            cp.wait()

    sem = pltpu.SemaphoreType.DMA
    return pl.pallas_call(
        body, name="pair_gather", in_specs=_hbm_specs(n), out_specs=_hbm_specs(n),
        out_shape=[jax.ShapeDtypeStruct(f.shape, f.dtype) for f in fulls],
        input_output_aliases={a: a for a in range(n)},
        scratch_shapes=[sem((n,)), sem((n,))],
        compiler_params=pltpu.CompilerParams(has_side_effects=True),
    )(*fulls)


def _row_tile(rows):
    if rows <= 512:
        return rows
    for t in (512, 256, 128, 64, 32, 16, 8):
        if rows % t == 0:
            return t
    raise ValueError(rows)


def _pair_sums(core, grads, ici_dtypes):
    others = _pair_exchange(grads)
    return [_pair_add(core, g, o, _row_tile(o.shape[1]), dt) for g, o, dt in zip(grads, others, ici_dtypes)]


def _finish_reduce(core, landed):
    return _pair_gather([_chip_add(core, r, _row_tile(r.shape[1])) for r in landed])


def _adamw(w, g, m, v):
    rows, cols = w.shape
    one_block = rows % 8 != 0 or rows * max(cols, 128) * 4 <= (1 << 20)
    tr = rows if one_block else _row_tile(rows)

    def body(w_ref, g_ref, m_ref, v_ref, d_ref, nm_ref, nv_ref):
        g_ = g_ref[...]
        m_ = ADAM_B1 * m_ref[...] + (1.0 - ADAM_B1) * g_
        v_ = ADAM_B2 * v_ref[...] + (1.0 - ADAM_B2) * (g_ * g_)
        m_hat = m_ / (1.0 - ADAM_B1 ** ADAM_STEP)
        v_hat = v_ / (1.0 - ADAM_B2 ** ADAM_STEP)
        d_ref[...] = -ADAM_LR * (m_hat / (jnp.sqrt(v_hat) + ADAM_EPS) + ADAM_WD * w_ref[...])
        nm_ref[...] = m_
        nv_ref[...] = v_

    spec = pl.BlockSpec((tr, cols), lambda i: (i, 0))
    shp = jax.ShapeDtypeStruct((rows, cols), F32)
    return pl.pallas_call(
        body, name="adamw", grid=(rows // tr,), in_specs=[spec] * 4, out_specs=[spec] * 3,
        out_shape=[shp] * 3, compiler_params=_cparams(("arbitrary",)),
    )(w, g, m, v)


_SMALL = ["norm_pre", "norm_post", "ssm_a_re", "ssm_a_im", "ssm_log_dt", "ssm_b_re", "ssm_b_im",
          "ssm_c_re", "ssm_c_im", "ssm_d", "b_glu", "na_rpb", "ple_norm"]
_BIG = ["w_in", "w_glu", "w_out", "w_ple", "w_ple_gate"]
_WEIGHTS = ["norm_pre", "norm_post", "w_in", "ssm_a_re", "ssm_a_im", "ssm_log_dt", "ssm_b_re", "ssm_b_im",
            "ssm_c_re", "ssm_c_im", "ssm_d", "w_glu", "b_glu", "na_rpb", "w_out", "w_ple", "ple_norm", "w_ple_gate"]
_SMALL_ROWS = 2176


def _pack_small(tensors, tail=None):
    parts = [tensors[n].reshape(-1) for n in _SMALL] + ([] if tail is None else [tail.reshape(-1)])
    flat = jnp.concatenate(parts)
    flat = jnp.pad(flat, (0, _SMALL_ROWS * 128 - flat.shape[0]))
    return flat.reshape(_SMALL_ROWS, 128)


def _unpack_small(packed, shapes):
    flat = packed.reshape(-1)
    out, off = {}, 0
    for n in _SMALL:
        size = int(np.prod(shapes[n]))
        out[n] = flat[off:off + size].reshape(shapes[n])
        off += size
    return out


def _local_grads(x, p, target, wts):
    ssm_names = ["ssm_a_re", "ssm_a_im", "ssm_log_dt", "ssm_b_re", "ssm_b_im", "ssm_c_re", "ssm_c_im", "ssm_d"]
    ssm_params = [wts[n][0] for n in ssm_names]
    blk, blk_vjp = jax.vjp(_ssm_block_params, *ssm_params)
    shard = lambda n: wts[n][0].astype(BF16)
    (m_mat, ws_mat, wot_mat, a16), (w_in_g,) = _ssm_chunk_matrices(blk, [shard("w_in")])
    seq = x.shape[0]
    bias_rows, bias_rows_vjp = jax.vjp(_na_bias_rows, wts["na_rpb"][0])
    bias_tab = _na_bias_table(bias_rows, seq // GRID_W)

    (u_c, z_s, q_t, q, k_t, k, v_t, v, z_n), gathered = _in_proj(
        x, wts["norm_pre"], w_in_g, [shard(n) for n in _BIG if n != "w_in"])
    w_glu, w_out, w_ple_g, w_pg = (gathered[0].reshape(512, 512), gathered[1].reshape(1024, 1024), gathered[2],
                                   gathered[3].reshape(1024, 1024))
    s_in = _block_matmul([(u_c, ws_mat, False)], "ssm_chunk_states")
    s_prev = _ssm_state_scan(s_in, a16)
    y_ssm_c = _block_matmul([(u_c, m_mat, False), (s_prev, wot_mat, True)], "ssm_chunk_out")
    y_na_t = _na_fwd(q_t, k, v_t, bias_tab)
    cat = _branch_fwd(y_ssm_c, z_s, y_na_t, z_n, w_glu, wts["b_glu"])

    (loss, d_h1, d_cat, d_w_out, d_g_post, d_w_ple, d_g_ple, d_w_pg) = _head(
        x, p, target, cat, w_out, wts["norm_post"], w_ple_g, wts["ple_norm"], w_pg)
    dy_c, d_z_s, d_y_na_t, d_y_na, d_z_n, d_w_glu, d_b_glu = _branch_bwd(
        y_ssm_c, z_s, y_na_t, z_n, w_glu, wts["b_glu"], d_cat)
    d_q_t, d_k, d_v, d_bias_tab = _na_bwd(q_t, q, k_t, k, v, bias_tab, y_na_t, d_y_na_t, d_y_na)

    d_prev = _block_matmul([(dy_c, wot_mat, False)], "ssm_bwd_states")
    g_st, d_a16 = _ssm_state_scan_bwd(d_prev, s_prev, a16)
    d_u_c = _block_matmul([(dy_c, m_mat, True), (g_st, ws_mat, True)], "ssm_bwd_in", out_dtype=BF16)
    d_m = _block_matmul_tn(u_c, dy_c, "ssm_grad_m")
    d_ws = _block_matmul_tn(u_c, g_st, "ssm_grad_ws")
    d_wot = _block_matmul_tn(dy_c, s_prev, "ssm_grad_wot")
    d_ssm = blk_vjp(tuple(_ssm_chunk_matrices_bwd(blk, d_m, d_ws, d_wot, d_a16)))
    (d_rpb,) = bias_rows_vjp(_na_bias_table_bwd(d_bias_tab, seq // GRID_W))

    dparts = [d_u_c, d_z_s, d_q_t, d_k, d_v, d_z_n]
    d_w_in, d_g_pre = _in_proj_bwd_w(x, wts["norm_pre"].reshape(D_MODEL, 1), w_in_g, dparts)

    small = {"norm_pre": d_g_pre, "norm_post": d_g_post, "b_glu": d_b_glu, "na_rpb": d_rpb, "ple_norm": d_g_ple}
    for n, g in zip(ssm_names, d_ssm):
        small[n] = g
    big = {"w_in": d_w_in, "w_glu": d_w_glu.reshape(N_CHIPS, 128, 512), "w_out": d_w_out.reshape(N_CHIPS, 256, 1024),
           "w_ple": d_w_ple, "w_ple_gate": d_w_pg.reshape(N_CHIPS, 256, 1024)}
    return loss, small, big, (x, wts["norm_pre"], w_in_g, d_h1, dparts)


def kernel(x, p, norm_pre, norm_post, w_in, ssm_a_re, ssm_a_im, ssm_log_dt, ssm_b_re, ssm_b_im, ssm_c_re, ssm_c_im, ssm_d, w_glu, b_glu, na_rpb, w_out, w_ple, ple_norm, w_ple_gate, loss_target, m_norm_pre, m_norm_post, m_w_in, m_ssm_a_re, m_ssm_a_im, m_ssm_log_dt, m_ssm_b_re, m_ssm_b_im, m_ssm_c_re, m_ssm_c_im, m_ssm_d, m_w_glu, m_b_glu, m_na_rpb, m_w_out, m_w_ple, m_ple_norm, m_w_ple_gate, v_norm_pre, v_norm_post, v_w_in, v_ssm_a_re, v_ssm_a_im, v_ssm_log_dt, v_ssm_b_re, v_ssm_b_im, v_ssm_c_re, v_ssm_c_im, v_ssm_d, v_w_glu, v_b_glu, v_na_rpb, v_w_out, v_w_ple, v_ple_norm, v_w_ple_gate):
    wts = dict(norm_pre=norm_pre, norm_post=norm_post, w_in=w_in, ssm_a_re=ssm_a_re, ssm_a_im=ssm_a_im,
               ssm_log_dt=ssm_log_dt, ssm_b_re=ssm_b_re, ssm_b_im=ssm_b_im, ssm_c_re=ssm_c_re, ssm_c_im=ssm_c_im,
               ssm_d=ssm_d, w_glu=w_glu, b_glu=b_glu, na_rpb=na_rpb, w_out=w_out, w_ple=w_ple, ple_norm=ple_norm,
               w_ple_gate=w_ple_gate)
    mom_m = dict(norm_pre=m_norm_pre, norm_post=m_norm_post, w_in=m_w_in, ssm_a_re=m_ssm_a_re, ssm_a_im=m_ssm_a_im,
                 ssm_log_dt=m_ssm_log_dt, ssm_b_re=m_ssm_b_re, ssm_b_im=m_ssm_b_im, ssm_c_re=m_ssm_c_re,
                 ssm_c_im=m_ssm_c_im, ssm_d=m_ssm_d, w_glu=m_w_glu, b_glu=m_b_glu, na_rpb=m_na_rpb, w_out=m_w_out,
                 w_ple=m_w_ple, ple_norm=m_ple_norm, w_ple_gate=m_w_ple_gate)
    mom_v = dict(norm_pre=v_norm_pre, norm_post=v_norm_post, w_in=v_w_in, ssm_a_re=v_ssm_a_re, ssm_a_im=v_ssm_a_im,
                 ssm_log_dt=v_ssm_log_dt, ssm_b_re=v_ssm_b_re, ssm_b_im=v_ssm_b_im, ssm_c_re=v_ssm_c_re,
                 ssm_c_im=v_ssm_c_im, ssm_d=v_ssm_d, w_glu=v_w_glu, b_glu=v_b_glu, na_rpb=v_na_rpb, w_out=v_w_out,
                 w_ple=v_w_ple, ple_norm=v_ple_norm, w_ple_gate=v_w_ple_gate)

    loss_part, small, big, input_grad_args = _local_grads(x[0], p[0, 0], loss_target[0], wts)

    core = lax.axis_index("c").astype(jnp.int32).reshape(1)
    small_packed = _pack_small(small, tail=loss_part).reshape(N_CHIPS, _SMALL_ROWS // N_CHIPS, 128)
    pair = _pair_sums(core, [big[n] for n in _BIG] + [small_packed], [BF16] * len(_BIG) + [F32])
    grad_x, landed = _in_proj_bwd_x(*input_grad_args, pair)
    reduced = _finish_reduce(core, landed)
    grads = dict(zip(_BIG, reduced[:-1]))
    (small_all,) = _gather_chips([reduced[-1]], "gather_small_grads")
    small_all = small_all.reshape(_SMALL_ROWS, 128)
    loss = small_all.reshape(-1)[sum(int(np.prod(wts[n].shape)) for n in _SMALL)]

    delta, new_m, new_v = {}, {}, {}
    for n in _BIG:
        shp = wts[n].shape
        d_, m_, v_ = _adamw(wts[n][0], grads[n], mom_m[n][0], mom_v[n][0])
        grads[n] = grads[n].reshape(shp)
        delta[n], new_m[n], new_v[n] = d_.reshape(shp), m_.reshape(shp), v_.reshape(shp)
    grads.update(_unpack_small(small_all, {n: wts[n].shape for n in _SMALL}))
    for n in _SMALL:
        shp = wts[n].shape
        rows_cols = (int(np.prod(shp[:-1])), shp[-1])
        d_, m_, v_ = _adamw(*[t.reshape(rows_cols) for t in (wts[n], grads[n], mom_m[n], mom_v[n])])
        delta[n], new_m[n], new_v[n] = d_.reshape(shp), m_.reshape(shp), v_.reshape(shp)

    return (loss, grad_x[None], *[grads[n] for n in _WEIGHTS], *[delta[n] for n in _WEIGHTS],
            *[new_m[n] for n in _WEIGHTS], *[new_v[n] for n in _WEIGHTS])
```

```python
import math

import jax
import jax.numpy as jnp
import numpy as np
from jax import lax
from jax.experimental import pallas as pl
from jax.experimental.pallas import tpu as pltpu

F32 = jnp.float32
BF16 = jnp.bfloat16

D_MODEL = 1024
D_PLE = 256
GRID_W = 64
D_SSM = 512
SSM_GROUP = 16
N_GROUPS = 32
SSM_STATE = 64
D_NA = 512
NA_HEADS = 8
NA_HEAD_DIM = 64
NA_ROWS = 8
NA_COLS = 16
D_IN_PROJ = 3072
EPS = 1e-6

CHUNK = 16
GROUPS_PER_BLOCK = 8
N_BLOCKS = N_GROUPS // GROUPS_PER_BLOCK
BLOCK_CH = GROUPS_PER_BLOCK * SSM_GROUP
BLOCK_ST = GROUPS_PER_BLOCK * SSM_STATE
CHUNK_W = CHUNK * BLOCK_CH
STATE_W = 4 * BLOCK_ST

N_CHIPS = 4
MESH = pl.DeviceIdType.MESH

ADAM_LR = 0.001
ADAM_B1 = 0.9
ADAM_B2 = 0.999
ADAM_EPS = 1e-08
ADAM_WD = 0.01
ADAM_STEP = 10

VMEM_LIMIT = 52 * 1024 * 1024
HIGHEST = lax.Precision.HIGHEST


def _cparams(sem=None, **kw):
    if sem is not None:
        kw["dimension_semantics"] = sem
    return pltpu.CompilerParams(vmem_limit_bytes=VMEM_LIMIT, **kw)


def _resident(*shape):
    return pl.BlockSpec(shape, lambda *_: (0,) * len(shape), pipeline_mode=pl.Buffered(1))


def _dot(a, b, dims=((1,), (0,))):
    return lax.dot_general(a, b, (dims, ((), ())), preferred_element_type=F32)


def _dot_nt(a, b):
    return _dot(a, b, ((1,), (1,)))


def _dot_tn(a, b):
    return _dot(a, b, ((0,), (0,)))


def _sigmoid(x):
    return 1.0 / (1.0 + jnp.exp(-x))


_GELU_C = math.sqrt(2.0 / math.pi)


def _gelu_parts(x):
    inner = _GELU_C * (x + 0.044715 * (x * x * x))
    t = jnp.tanh(inner)
    return 0.5 * x * (1.0 + t), t


def _gelu_grad(x, t):
    return 0.5 * (1.0 + t) + 0.5 * x * (1.0 - t * t) * (_GELU_C * (1.0 + 3.0 * 0.044715 * x * x))


def _silu_parts(z):
    s = _sigmoid(z)
    return z * s, s


def _silu_grad(z, s):
    return s * (1.0 + z * (1.0 - s))


def _rms(x):
    r = lax.rsqrt(jnp.mean(x * x, axis=-1, keepdims=True) + EPS)
    return x * r, r


def _rms_bwd(dn, n, r):
    return r * (dn - n * jnp.mean(dn * n, axis=-1, keepdims=True))


def _chunk_scratch(tm):
    return pltpu.VMEM((N_BLOCKS, tm, BLOCK_CH), F32)


def _store_chunks(val, scr, c_ref, dtype, row0=0):
    rows = val.shape[0]
    nc, c0 = rows // CHUNK, row0 // CHUNK
    for b in range(N_BLOCKS):
        scr[b, row0:row0 + rows, :] = val[:, b * BLOCK_CH:(b + 1) * BLOCK_CH]
        for j in range(CHUNK):
            c_ref[b, c0:c0 + nc, j * BLOCK_CH:(j + 1) * BLOCK_CH] = scr[b, pl.ds(row0 + j, nc, stride=CHUNK), :].astype(dtype)


def _load_chunks(c_ref, scr):
    nc = scr.shape[1] // CHUNK
    for b in range(N_BLOCKS):
        for j in range(CHUNK):
            scr[b, pl.ds(j, nc, stride=CHUNK), :] = c_ref[b, :, j * BLOCK_CH:(j + 1) * BLOCK_CH].astype(F32)
    return jnp.concatenate([scr[b] for b in range(N_BLOCKS)], axis=1)


def _chunk_spec(tm):
    return pl.BlockSpec((N_BLOCKS, tm // CHUNK, CHUNK_W), lambda i: (0, i, 0))


def _heads_t_spec(tm):
    return pl.BlockSpec((D_NA, tm), lambda i: (0, i))


def _in_proj(x, g_pre, w_in_g, shards, tm=512):
    L = x.shape[0]
    wn = w_in_g.shape[2]
    n_sh = len(shards)
    steps = L // tm

    def body(*refs):
        x_ref, g_ref, w_ref = refs[:3]
        uc_ref, zs_ref, qt_ref, q_ref, kt_ref, k_ref, vt_ref, v_ref, zn_ref = refs[3 + n_sh:12 + n_sh]
        u_scr = refs[12 + 2 * n_sh]
        gather = _ChipGather(refs[3:3 + n_sh], refs[12 + n_sh:12 + 2 * n_sh], refs[13 + 2 * n_sh:])
        step = pl.program_id(0)
        pl.when(step == 0)(gather.start)
        pl.when(step == steps // 2)(gather.forward)
        pl.when(step == steps - 1)(gather.finish)
        halves = [slice(0, tm // 2), slice(tm // 2, tm)]
        hn = [(_rms(x_ref[rows, :])[0] * g_ref[...]).astype(BF16) for rows in halves]
        projs = [jnp.concatenate([_dot(h, w_ref[j]) for j in range(N_CHIPS)], axis=1) for h in hn]
        for rows, proj in zip(halves, projs):
            _store_chunks(proj[:, 0:512], u_scr, uc_ref, BF16, row0=rows.start)
            zs_ref[rows, :] = proj[:, 512:1024]
            q = proj[:, 1024:1536] * (NA_HEAD_DIM ** -0.5)
            for val, t_ref, n_ref in ((q, qt_ref, q_ref), (proj[:, 1536:2048], kt_ref, k_ref), (proj[:, 2048:2560], vt_ref, v_ref)):
                t_ref[:, rows] = val.T.astype(BF16)
                n_ref[rows, :] = val.astype(BF16)
            zn_ref[rows, :] = proj[:, 2560:3072]

    tok = jax.ShapeDtypeStruct((L, 512), F32)
    tr = jax.ShapeDtypeStruct((D_NA, L), BF16)
    hm = jax.ShapeDtypeStruct((L, D_NA), BF16)
    tspec = pl.BlockSpec((tm, 512), lambda i: (i, 0))
    outs = pl.pallas_call(
        body, name="in_proj", grid=(steps,),
        in_specs=[pl.BlockSpec((tm, D_MODEL), lambda i: (i, 0)),
                  _resident(1, D_MODEL), _resident(N_CHIPS, D_MODEL, wn)] + _hbm_specs(n_sh),
        out_specs=[_chunk_spec(tm), tspec] + [_heads_t_spec(tm), tspec] * 3 + [tspec] + _hbm_specs(n_sh),
        out_shape=[jax.ShapeDtypeStruct((N_BLOCKS, L // CHUNK, CHUNK_W), BF16), tok, tr, hm, tr, hm, tr, hm, tok]
        + _gather_out_shapes(shards),
        scratch_shapes=[_chunk_scratch(tm)] + _gather_semaphores(n_sh),
        compiler_params=_cparams(("arbitrary",), has_side_effects=True),
    )(x, g_pre, w_in_g, *shards)
    return outs[:9], outs[9:]


def _ssm_block_params(a_re, a_im, log_dt, b_re, b_im, c_re, c_im, d):
    def lanes(t):
        return t.reshape(2, N_BLOCKS, 1, BLOCK_ST)

    rows = (2, N_BLOCKS, BLOCK_CH, SSM_STATE)
    b_rows = lambda t: t.reshape(2, N_BLOCKS, GROUPS_PER_BLOCK, SSM_STATE, SSM_GROUP).transpose(0, 1, 2, 4, 3).reshape(rows)
    return (lanes(a_re), lanes(a_im), lanes(jnp.broadcast_to(log_dt[..., None], a_re.shape)),
            b_rows(b_re), b_rows(b_im), c_re.reshape(rows), c_im.reshape(rows), d.reshape(N_BLOCKS, 1, BLOCK_CH))


def _ssm_group_mask():
    row_g = lax.broadcasted_iota(jnp.int32, (BLOCK_CH, BLOCK_ST), 0) // SSM_GROUP
    lane_g = lax.broadcasted_iota(jnp.int32, (BLOCK_CH, BLOCK_ST), 1) // SSM_STATE
    return row_g == lane_g


def _ssm_state_select():
    p = lax.broadcasted_iota(jnp.int32, (SSM_STATE, BLOCK_ST), 0)
    lane_p = lax.broadcasted_iota(jnp.int32, (SSM_STATE, BLOCK_ST), 1) % SSM_STATE
    return (p == lane_p).astype(F32)


def _ssm_expand_blocks(compact_refs, full_refs):
    mask, select = _ssm_group_mask(), _ssm_state_select()
    for c_ref, f_ref in zip(compact_refs, full_refs):
        for d in range(2):
            tiled = lax.dot_general(c_ref[d, 0], select, ((((1,), (0,))), ((), ())), precision=HIGHEST,
                                    preferred_element_type=F32)
            f_ref[d, 0] = jnp.where(mask, tiled, 0.0)


def _ssm_collapse_block(t):
    return lax.dot_general(jnp.where(_ssm_group_mask(), t, 0.0), _ssm_state_select(), ((((1,), (1,))), ((), ())),
                           precision=HIGHEST, preferred_element_type=F32)


def _ssm_discretise(ar, ai, ldt):
    dt = jnp.exp(ldt)
    mag = jnp.exp(dt * ar)
    abr = mag * jnp.cos(dt * ai)
    abi = mag * jnp.sin(dt * ai)
    num_re = abr - 1.0
    num_im = abi
    denom = ar * ar + ai * ai
    coef_re = (num_re * ar + num_im * ai) / denom
    coef_im = (num_im * ar - num_re * ai) / denom
    return abr, abi, coef_re, coef_im


_POW_ROWS = 24


def _ssm_fill_powers(ar_ref, ai_ref, ldt_ref, br_ref, bi_ref, pw_ref, bbar_ref):
    for d in range(2):
        abr, abi, cfr, cfi = _ssm_discretise(ar_ref[d, 0], ai_ref[d, 0], ldt_ref[d, 0])
        bbar_ref[d, 0] = cfr * br_ref[d, 0] - cfi * bi_ref[d, 0]
        bbar_ref[d, 1] = cfr * bi_ref[d, 0] + cfi * br_ref[d, 0]
        pr, pi = jnp.ones_like(abr), jnp.zeros_like(abi)
        for t in range(CHUNK + 1):
            pw_ref[d, 0, t:t + 1, :] = pr
            pw_ref[d, 1, t:t + 1, :] = pi
            pr, pi = pr * abr - pi * abi, pr * abi + pi * abr


def _dot_rounded(a, b, dims=((1,), (0,))):
    return _dot(a.astype(BF16), b.astype(BF16), dims)


def _ssm_stack_inputs(d, pw_ref, bbar_ref, xs_ref):
    for t in range(CHUNK):
        pr, pi = pw_ref[d, 0, t:t + 1, :], pw_ref[d, 1, t:t + 1, :]
        xs_ref[0, t * BLOCK_CH:(t + 1) * BLOCK_CH, :] = bbar_ref[d, 0] * pr - bbar_ref[d, 1] * pi
        xs_ref[1, t * BLOCK_CH:(t + 1) * BLOCK_CH, :] = bbar_ref[d, 0] * pi + bbar_ref[d, 1] * pr


def _eye(n):
    return (lax.broadcasted_iota(jnp.int32, (n, n), 0) == lax.broadcasted_iota(jnp.int32, (n, n), 1)).astype(F32)


def _ssm_param_specs():
    vec = pl.BlockSpec((2, 1, 1, BLOCK_ST), lambda b, j: (0, b, 0, 0))
    mat = pl.BlockSpec((2, 1, BLOCK_CH, SSM_STATE), lambda b, j: (0, b, 0, 0))
    return [vec, vec, vec, mat, mat, mat, mat, pl.BlockSpec((1, 1, BLOCK_CH), lambda b, j: (b, 0, 0))]


def _ssm_block_scratch():
    return [pltpu.VMEM((2, 1, BLOCK_CH, BLOCK_ST), F32)] * 4


def _ssm_chunk_matrices(blk, shards):
    n = len(shards)

    def body(*refs):
        ar_ref, ai_ref, ldt_ref = refs[:3]
        d_ref = refs[7]
        m_ref, ws_ref, wot_ref, a16_ref = refs[8 + n:12 + n]
        pw_ref, bbar_ref, lag_ref, xs_ref = refs[12 + 2 * n:16 + 2 * n]
        br_ref, bi_ref, cr_ref, ci_ref = refs[16 + 2 * n:20 + 2 * n]
        gather = _ChipGather(refs[8:8 + n], refs[12 + n:12 + 2 * n], refs[20 + 2 * n:])
        b, j = pl.program_id(0), pl.program_id(1)
        pl.when((b == 0) & (j == 0))(gather.start)
        pl.when((b == N_BLOCKS - 1) & (j == 0))(gather.forward)
        pl.when((b == N_BLOCKS - 1) & (j == CHUNK - 1))(gather.finish)

        @pl.when(j == 0)
        def _():
            _ssm_expand_blocks(refs[3:7], (br_ref, bi_ref, cr_ref, ci_ref))
            _ssm_fill_powers(ar_ref, ai_ref, ldt_ref, br_ref, bi_ref, pw_ref, bbar_ref)
            zero_lag = d_ref[0] * _eye(BLOCK_CH)
            for d in range(2):
                _ssm_stack_inputs(d, pw_ref, bbar_ref, xs_ref)
                taps = (_dot_rounded(xs_ref[0], cr_ref[d, 0], ((1,), (1,)))
                        - _dot_rounded(xs_ref[1], ci_ref[d, 0], ((1,), (1,))))
                zero_lag = zero_lag + taps[0:BLOCK_CH]
                for t in range(1, CHUNK):
                    lag_ref[CHUNK - 1 + t if d == 0 else CHUNK - 1 - t] = taps[t * BLOCK_CH:(t + 1) * BLOCK_CH]
            lag_ref[CHUNK - 1] = zero_lag
            a16_ref[0] = jnp.concatenate([pw_ref[d, ri, CHUNK:CHUNK + 1, :] for d in range(2) for ri in range(2)], axis=1)

        m_ref[0] = jnp.concatenate([lag_ref[jp - j + CHUNK - 1] for jp in range(CHUNK)], axis=1).astype(BF16)

        def power(d, t):
            return pw_ref[d, 0, pl.ds(t, 1), :], pw_ref[d, 1, pl.ds(t, 1), :]

        parts = []
        for d, t in ((0, CHUNK - 1 - j), (1, j)):
            pr, pi = power(d, t)
            parts += [bbar_ref[d, 0] * pr - bbar_ref[d, 1] * pi, bbar_ref[d, 0] * pi + bbar_ref[d, 1] * pr]
        ws_ref[0] = jnp.concatenate(parts, axis=1).astype(BF16)
        parts = []
        for d, t in ((0, j + 1), (1, CHUNK - j)):
            pr, pi = power(d, t)
            parts += [cr_ref[d, 0] * pr - ci_ref[d, 0] * pi, -cr_ref[d, 0] * pi - ci_ref[d, 0] * pr]
        wot_ref[0] = jnp.concatenate(parts, axis=1).astype(BF16)

    row = pl.BlockSpec((1, BLOCK_CH, CHUNK_W), lambda b, j: (b, j, 0))
    mat = jax.ShapeDtypeStruct((N_BLOCKS, CHUNK_W, CHUNK_W), BF16)
    outs = pl.pallas_call(
        body, name="ssm_chunk_matrices", grid=(N_BLOCKS, CHUNK),
        in_specs=_ssm_param_specs() + _hbm_specs(n),
        out_specs=[row, row, row, pl.BlockSpec((1, 1, STATE_W), lambda b, j: (b, 0, 0))] + _hbm_specs(n),
        out_shape=[mat, mat, mat, jax.ShapeDtypeStruct((N_BLOCKS, 1, STATE_W), F32)] + _gather_out_shapes(shards),
        scratch_shapes=[pltpu.VMEM((2, 2, _POW_ROWS, BLOCK_ST), F32), pltpu.VMEM((2, 2, BLOCK_CH, BLOCK_ST), F32),
                        pltpu.VMEM((2 * CHUNK, BLOCK_CH, BLOCK_CH), F32), pltpu.VMEM((2, CHUNK_W, BLOCK_ST), F32)]
        + _ssm_block_scratch() + _gather_semaphores(n),
        compiler_params=_cparams(("arbitrary", "arbitrary"), has_side_effects=True),
    )(*blk, *shards)
    return outs[:4], outs[4:]


def _ssm_chunk_matrices_bwd(blk, d_m, d_ws, d_wot, d_a16):
    def body(ar_ref, ai_ref, ldt_ref, brc_ref, bic_ref, crc_ref, cic_ref, d_ref, dm_ref, dws_ref, dwot_ref, da16_ref,
             dar_ref, dai_ref, dldt_ref, dbr_ref, dbi_ref, dcr_ref, dci_ref, dd_ref,
             pw_ref, bbar_ref, dlag_ref, dbbar_ref, dc_ref, dpw_ref, xs_ref, dts_ref, br_ref, bi_ref, cr_ref, ci_ref):
        j = pl.program_id(1)
        w = BLOCK_ST

        @pl.when(j == 0)
        def _():
            _ssm_expand_blocks((brc_ref, bic_ref, crc_ref, cic_ref), (br_ref, bi_ref, cr_ref, ci_ref))
            _ssm_fill_powers(ar_ref, ai_ref, ldt_ref, br_ref, bi_ref, pw_ref, bbar_ref)
            for r in (dlag_ref, dbbar_ref, dc_ref, dpw_ref):
                r[...] = jnp.zeros_like(r)

        def fold(t):
            return jnp.sum(t.reshape(BLOCK_CH // 8, 8, w), axis=0)

        def d_power(d, ri, t):
            return jnp.sum(dpw_ref[d, ri, t], axis=0, keepdims=True)

        def x_chain(d, t, dxr, dxi):
            pr, pi = pw_ref[d, 0, pl.ds(t, 1), :], pw_ref[d, 1, pl.ds(t, 1), :]
            bbr, bbi = bbar_ref[d, 0], bbar_ref[d, 1]
            dbbar_ref[d, 0] += dxr * pr + dxi * pi
            dbbar_ref[d, 1] += dxi * pr - dxr * pi
            dpw_ref[d, 0, t] += fold(dxr * bbr + dxi * bbi)
            dpw_ref[d, 1, t] += fold(dxi * bbr - dxr * bbi)

        def z_chain(d, t, dzr, dzi):
            pr, pi = pw_ref[d, 0, pl.ds(t, 1), :], pw_ref[d, 1, pl.ds(t, 1), :]
            c_r, c_i = cr_ref[d, 0], ci_ref[d, 0]
            dc_ref[d, 0] += dzr * pr - dzi * pi
            dc_ref[d, 1] += -dzr * pi - dzi * pr
            dpw_ref[d, 0, t] += fold(dzr * c_r - dzi * c_i)
            dpw_ref[d, 1, t] += fold(-dzr * c_i - dzi * c_r)

        for jp in range(CHUNK):
            dlag_ref[jp - j + CHUNK - 1] += dm_ref[0, :, jp * BLOCK_CH:(jp + 1) * BLOCK_CH].astype(F32)
        quarter = lambda ref, i: ref[0, :, i * w:(i + 1) * w].astype(F32)
        x_chain(0, CHUNK - 1 - j, quarter(dws_ref, 0), quarter(dws_ref, 1))
        x_chain(1, j, quarter(dws_ref, 2), quarter(dws_ref, 3))
        z_chain(0, j + 1, quarter(dwot_ref, 0), quarter(dwot_ref, 1))
        z_chain(1, CHUNK - j, quarter(dwot_ref, 2), quarter(dwot_ref, 3))

        @pl.when(j == CHUNK - 1)
        def _():
            for d in range(2):
                _ssm_stack_inputs(d, pw_ref, bbar_ref, xs_ref)
                for t in range(CHUNK):
                    dts_ref[t * BLOCK_CH:(t + 1) * BLOCK_CH, :] = dlag_ref[CHUNK - 1 + t if d == 0 else CHUNK - 1 - t]
                d_taps = dts_ref[...]
                dc_ref[d, 0] += _dot_rounded(d_taps, xs_ref[0], ((0,), (0,)))
                dc_ref[d, 1] -= _dot_rounded(d_taps, xs_ref[1], ((0,), (0,)))
                xs_ref[0] = _dot_rounded(d_taps, cr_ref[d, 0])
                xs_ref[1] = -_dot_rounded(d_taps, ci_ref[d, 0])
                for t in range(CHUNK):
                    rows = slice(t * BLOCK_CH, (t + 1) * BLOCK_CH)
                    x_chain(d, t, xs_ref[0, rows, :], xs_ref[1, rows, :])
            dd_ref[0] = jnp.sum(dlag_ref[CHUNK - 1] * _eye(BLOCK_CH), axis=0, keepdims=True)
            for d in range(2):
                (abr, abi, cfr, cfi), disc_vjp = jax.vjp(_ssm_discretise, ar_ref[d, 0], ai_ref[d, 0], ldt_ref[d, 0])
                dpr = d_power(d, 0, CHUNK) + da16_ref[0, :, 2 * d * w:(2 * d + 1) * w]
                dpi = d_power(d, 1, CHUNK) + da16_ref[0, :, (2 * d + 1) * w:(2 * d + 2) * w]
                dabr, dabi = jnp.zeros_like(abr), jnp.zeros_like(abi)
                for t in range(CHUNK, 0, -1):
                    qr, qi = pw_ref[d, 0, t - 1:t, :], pw_ref[d, 1, t - 1:t, :]
                    dabr = dabr + dpr * qr + dpi * qi
                    dabi = dabi + dpi * qr - dpr * qi
                    dpr, dpi = (dpr * abr + dpi * abi + d_power(d, 0, t - 1),
                                dpi * abr - dpr * abi + d_power(d, 1, t - 1))
                dbbr, dbbi = dbbar_ref[d, 0], dbbar_ref[d, 1]
                b_r, b_i = br_ref[d, 0], bi_ref[d, 0]
                dbr_ref[d, 0] = _ssm_collapse_block(cfr * dbbr + cfi * dbbi)
                dbi_ref[d, 0] = _ssm_collapse_block(cfr * dbbi - cfi * dbbr)
                dcfr = jnp.sum(b_r * dbbr + b_i * dbbi, axis=0, keepdims=True)
                dcfi = jnp.sum(b_r * dbbi - b_i * dbbr, axis=0, keepdims=True)
                dar_ref[d, 0], dai_ref[d, 0], dldt_ref[d, 0] = disc_vjp((dabr, dabi, dcfr, dcfi))
                dcr_ref[d, 0] = _ssm_collapse_block(dc_ref[d, 0])
                dci_ref[d, 0] = _ssm_collapse_block(dc_ref[d, 1])

    row = pl.BlockSpec((1, BLOCK_CH, CHUNK_W), lambda b, j: (b, j, 0))
    specs = _ssm_param_specs()
    acc = lambda *s: pltpu.VMEM(s, F32)
    return pl.pallas_call(
        body, name="ssm_chunk_matrices_bwd", grid=(N_BLOCKS, CHUNK),
        in_specs=specs + [row, row, row, pl.BlockSpec((1, 1, STATE_W), lambda b, j: (b, 0, 0))],
        out_specs=specs,
        out_shape=[jax.ShapeDtypeStruct(t.shape, F32) for t in blk],
        scratch_shapes=[acc(2, 2, _POW_ROWS, BLOCK_ST), acc(2, 2, BLOCK_CH, BLOCK_ST), acc(2 * CHUNK, BLOCK_CH, BLOCK_CH),
                        acc(2, 2, BLOCK_CH, BLOCK_ST), acc(2, 2, BLOCK_CH, BLOCK_ST), acc(2, 2, CHUNK + 1, 8, BLOCK_ST),
                        acc(2, CHUNK_W, BLOCK_ST), acc(CHUNK_W, BLOCK_CH)] + _ssm_block_scratch(),
        compiler_params=_cparams(("arbitrary", "arbitrary")),
    )(*blk, d_m, d_ws, d_wot, d_a16)


def _block_matmul(terms, name, out_dtype=F32, tn=1024):
    nc = terms[0][0].shape[1]
    n_out = terms[0][1].shape[1] if terms[0][2] else terms[0][1].shape[2]
    flags = [t[2] for t in terms]

    def body(*refs):
        out_ref = refs[-1]
        acc = None
        for t, transposed in enumerate(flags):
            a = refs[2 * t][0].astype(BF16)
            w = refs[2 * t + 1][0]
            part = _dot_nt(a, w) if transposed else _dot(a, w)
            acc = part if acc is None else acc + part
        out_ref[0] = acc.astype(out_dtype)

    in_specs, args = [], []
    for a, w, transposed in terms:
        k = a.shape[2]
        in_specs.append(pl.BlockSpec((1, nc, k), lambda b, n: (b, 0, 0)))
        if transposed:
            in_specs.append(pl.BlockSpec((1, tn, k), lambda b, n: (b, n, 0)))
        else:
            in_specs.append(pl.BlockSpec((1, k, tn), lambda b, n: (b, 0, n)))
        args += [a, w]
    return pl.pallas_call(
        body, name=name, grid=(N_BLOCKS, n_out // tn), in_specs=in_specs,
        out_specs=pl.BlockSpec((1, nc, tn), lambda b, n: (b, 0, n)),
        out_shape=jax.ShapeDtypeStruct((N_BLOCKS, nc, n_out), out_dtype),
        compiler_params=_cparams(("arbitrary", "arbitrary")),
    )(*args)


def _block_matmul_tn(a, b, name, tile=1024):
    nc, m = a.shape[1], a.shape[2]
    n = b.shape[2]

    def body(a_ref, b_ref, out_ref):
        out_ref[0] = _dot_tn(a_ref[0].astype(BF16), b_ref[0].astype(BF16)).astype(BF16)

    return pl.pallas_call(
        body, name=name, grid=(N_BLOCKS, m // tile, n // tile),
        in_specs=[pl.BlockSpec((1, nc, tile), lambda blk, i, j: (blk, 0, i)),
                  pl.BlockSpec((1, nc, tile), lambda blk, i, j: (blk, 0, j))],
        out_specs=pl.BlockSpec((1, tile, tile), lambda blk, i, j: (blk, i, j)),
        out_shape=jax.ShapeDtypeStruct((N_BLOCKS, m, n), BF16),
        compiler_params=_cparams(("arbitrary", "arbitrary", "arbitrary")),
    )(a, b)


def _cmul(ar, ai, xr, xi):
    return ar * xr - ai * xi, ar * xi + ai * xr


def _cmul_conj(ar, ai, xr, xi):
    return ar * xr + ai * xi, ar * xi - ai * xr


_SCAN_UNROLL = 8


def _ssm_state_scan(s_in, a16):
    nc = s_in.shape[1]
    w = BLOCK_ST

    def body(sin_ref, a_ref, out_ref):
        a = a_ref[0]
        afr, afi, abr, abi = a[:, 0:w], a[:, w:2 * w], a[:, 2 * w:3 * w], a[:, 3 * w:4 * w]

        def step(c, carry):
            fr, fi, br, bi = carry
            cb = nc - 1 - c
            out_ref[0, pl.ds(c, 1), 0:w] = fr
            out_ref[0, pl.ds(c, 1), w:2 * w] = fi
            out_ref[0, pl.ds(cb, 1), 2 * w:3 * w] = br
            out_ref[0, pl.ds(cb, 1), 3 * w:4 * w] = bi
            nfr, nfi = _cmul(afr, afi, fr, fi)
            nbr, nbi = _cmul(abr, abi, br, bi)
            return (nfr + sin_ref[0, pl.ds(c, 1), 0:w], nfi + sin_ref[0, pl.ds(c, 1), w:2 * w],
                    nbr + sin_ref[0, pl.ds(cb, 1), 2 * w:3 * w], nbi + sin_ref[0, pl.ds(cb, 1), 3 * w:4 * w])

        def steps(i, carry):
            for k in range(_SCAN_UNROLL):
                carry = step(i * _SCAN_UNROLL + k, carry)
            return carry

        z = jnp.zeros((1, w), F32)
        lax.fori_loop(0, nc // _SCAN_UNROLL, steps, (z, z, z, z))

    spec = pl.BlockSpec((1, nc, STATE_W), lambda b: (b, 0, 0))
    return pl.pallas_call(
        body, name="ssm_state_scan", grid=(N_BLOCKS,),
        in_specs=[spec, pl.BlockSpec((1, 1, STATE_W), lambda b: (b, 0, 0))],
        out_specs=spec, out_shape=jax.ShapeDtypeStruct(s_in.shape, F32),
        compiler_params=_cparams(("arbitrary",)),
    )(s_in, a16)


def _ssm_state_scan_bwd(d_prev, s_prev, a16):
    nc = d_prev.shape[1]
    w = BLOCK_ST

    def body(dp_ref, sp_ref, a_ref, g_ref, da_ref):
        a = a_ref[0]
        afr, afi, abr, abi = a[:, 0:w], a[:, w:2 * w], a[:, 2 * w:3 * w], a[:, 3 * w:4 * w]

        def step(i, carry):
            gfr, gfi, gbr, gbi, dafr, dafi, dabr, dabi = carry
            cf = nc - 1 - i
            cb = i
            g_ref[0, pl.ds(cf, 1), 0:w] = gfr
            g_ref[0, pl.ds(cf, 1), w:2 * w] = gfi
            g_ref[0, pl.ds(cb, 1), 2 * w:3 * w] = gbr
            g_ref[0, pl.ds(cb, 1), 3 * w:4 * w] = gbi
            sfr, sfi = sp_ref[0, pl.ds(cf, 1), 0:w], sp_ref[0, pl.ds(cf, 1), w:2 * w]
            sbr, sbi = sp_ref[0, pl.ds(cb, 1), 2 * w:3 * w], sp_ref[0, pl.ds(cb, 1), 3 * w:4 * w]
            dafr = dafr + gfr * sfr + gfi * sfi
            dafi = dafi + gfi * sfr - gfr * sfi
            dabr = dabr + gbr * sbr + gbi * sbi
            dabi = dabi + gbi * sbr - gbr * sbi
            nfr, nfi = _cmul_conj(afr, afi, gfr, gfi)
            nbr, nbi = _cmul_conj(abr, abi, gbr, gbi)
            return (nfr + dp_ref[0, pl.ds(cf, 1), 0:w], nfi + dp_ref[0, pl.ds(cf, 1), w:2 * w],
                    nbr + dp_ref[0, pl.ds(cb, 1), 2 * w:3 * w], nbi + dp_ref[0, pl.ds(cb, 1), 3 * w:4 * w],
                    dafr, dafi, dabr, dabi)

        def steps(i, carry):
            for k in range(_SCAN_UNROLL):
                carry = step(i * _SCAN_UNROLL + k, carry)
            return carry

        z = jnp.zeros((1, w), F32)
        res = lax.fori_loop(0, nc // _SCAN_UNROLL, steps, (z,) * 8)
        da_ref[0] = jnp.concatenate(res[4:], axis=1)

    spec = pl.BlockSpec((1, nc, STATE_W), lambda b: (b, 0, 0))
    aspec = pl.BlockSpec((1, 1, STATE_W), lambda b: (b, 0, 0))
    return pl.pallas_call(
        body, name="ssm_state_scan_bwd", grid=(N_BLOCKS,),
        in_specs=[spec, spec, aspec], out_specs=[spec, aspec],
        out_shape=[jax.ShapeDtypeStruct(d_prev.shape, F32), jax.ShapeDtypeStruct((N_BLOCKS, 1, STATE_W), F32)],
        compiler_params=_cparams(("arbitrary",)),
    )(d_prev, s_prev, a16)


NA_PAIR = 2 * GRID_W
NA_WIN_ROWS = NA_ROWS + 2
NA_WIN = NA_WIN_ROWS * GRID_W
NA_PAIRS_PER_STEP = 8
NA_CASES = 5
NA_MASKED = -1e30


def _na_pair_window(m, rows):
    rs0 = jnp.clip(2 * m - NA_ROWS // 2, 0, rows - NA_ROWS)
    ws = jnp.minimum(rs0, rows - NA_WIN_ROWS)
    last = rows // 2 - 1
    case = jnp.where(m == 0, 0, jnp.where(m == 1, 1, jnp.where(m == last - 1, 3, jnp.where(m == last, 4, 2))))
    return ws, case


def _na_row_offsets(rows):
    last = rows // 2 - 1
    geom = []
    for m in (0, 1, 2, last - 1, last):
        ws = min(max(2 * m - NA_ROWS // 2, 0), rows - NA_ROWS, rows - NA_WIN_ROWS)
        per_case = []
        for i in range(NA_WIN_ROWS):
            pair = []
            for rr in range(2):
                r = 2 * m + rr
                rs = min(max(r - NA_ROWS // 2, 0), rows - NA_ROWS)
                pair.append(ws + i - r + NA_ROWS - 1 if rs <= ws + i < rs + NA_ROWS else None)
            per_case.append(pair)
        geom.append(per_case)
    return geom


def _na_col_select():
    qc = np.arange(NA_PAIR)[None, :] % GRID_W
    kc = np.arange(GRID_W)[:, None]
    dc = np.clip(kc - qc + NA_COLS - 1, 0, 2 * NA_COLS - 2)
    return jnp.asarray((np.arange(2 * NA_COLS - 1)[:, None, None] == dc[None]).astype(np.float32))


def _na_bias_rows(rpb):
    return jnp.einsum("hrd,dkl->hrkl", rpb, _na_col_select(), precision=HIGHEST)


def _na_col_window():
    qc = lax.broadcasted_iota(jnp.int32, (GRID_W, NA_PAIR), 1) % GRID_W
    kc = lax.broadcasted_iota(jnp.int32, (GRID_W, NA_PAIR), 0)
    cs = jnp.clip(qc - NA_COLS // 2, 0, GRID_W - NA_COLS)
    first_row = lax.broadcasted_iota(jnp.int32, (GRID_W, NA_PAIR), 1) < GRID_W
    return (kc >= cs) & (kc < cs + NA_COLS), first_row


def _na_bias_table(bias_rows, rows):
    geom = _na_row_offsets(rows)

    def body(br_ref, tab_ref):
        col_ok, first_row = _na_col_window()
        masked = jnp.full((GRID_W, NA_PAIR), NA_MASKED, F32)
        for case in range(NA_CASES):
            for i in range(NA_WIN_ROWS):
                d0, d1 = geom[case][i]
                t0 = masked if d0 is None else br_ref[0, d0]
                t1 = masked if d1 is None else br_ref[0, d1]
                tile = jnp.where(col_ok, jnp.where(first_row, t0, t1), NA_MASKED)
                tab_ref[0, case, i * GRID_W:(i + 1) * GRID_W, :] = tile

    return pl.pallas_call(
        body, name="na_bias_table", grid=(NA_HEADS,),
        in_specs=[pl.BlockSpec((1, 2 * NA_ROWS - 1, GRID_W, NA_PAIR), lambda h: (h, 0, 0, 0))],
        out_specs=pl.BlockSpec((1, NA_CASES, NA_WIN, NA_PAIR), lambda h: (h, 0, 0, 0)),
        out_shape=jax.ShapeDtypeStruct((NA_HEADS, NA_CASES, NA_WIN, NA_PAIR), F32),
        compiler_params=_cparams(("arbitrary",)),
    )(bias_rows)


def _na_bias_table_bwd(d_tab, rows):
    geom = _na_row_offsets(rows)

    def body(dt_ref, dbr_ref):
        col_ok, first_row = _na_col_window()
        acc = [None] * (2 * NA_ROWS - 1)
        for case in range(NA_CASES):
            for i in range(NA_WIN_ROWS):
                tile = jnp.where(col_ok, dt_ref[0, case, i * GRID_W:(i + 1) * GRID_W, :], 0.0)
                for rr, d in enumerate(geom[case][i]):
                    if d is not None:
                        part = jnp.where(first_row if rr == 0 else ~first_row, tile, 0.0)
                        acc[d] = part if acc[d] is None else acc[d] + part
        for d, a in enumerate(acc):
            dbr_ref[0, d] = jnp.zeros((GRID_W, NA_PAIR), F32) if a is None else a

    return pl.pallas_call(
        body, name="na_bias_table_bwd", grid=(NA_HEADS,),
        in_specs=[pl.BlockSpec((1, NA_CASES, NA_WIN, NA_PAIR), lambda h: (h, 0, 0, 0))],
        out_specs=pl.BlockSpec((1, 2 * NA_ROWS - 1, GRID_W, NA_PAIR), lambda h: (h, 0, 0, 0)),
        out_shape=jax.ShapeDtypeStruct((NA_HEADS, 2 * NA_ROWS - 1, GRID_W, NA_PAIR), F32),
        compiler_params=_cparams(("arbitrary",)),
    )(d_tab)


NA_BLK = 64


def _na_blocks():
    return [slice(i * NA_BLK, (i + 1) * NA_BLK) for i in range(NA_WIN // NA_BLK)]


def _na_softmax(qk, bias_ref, hh, case):
    m = jnp.full((NA_BLK, NA_PAIR), -jnp.inf, F32)
    scores = []
    for blk in _na_blocks():
        s = qk[blk, :] + bias_ref[hh, case, blk, :]
        scores.append(s)
        m = jnp.maximum(m, s)
    m = jnp.max(m, axis=0, keepdims=True)
    l = jnp.zeros((NA_BLK, NA_PAIR), F32)
    exps = []
    for s in scores:
        e = jnp.exp(s - m)
        exps.append(e)
        l = l + e
    return exps, jnp.sum(l, axis=0, keepdims=True)


def _na_units(step, rows):
    units = []
    for pp in range(NA_PAIRS_PER_STEP):
        ws, case = _na_pair_window(step * NA_PAIRS_PER_STEP + pp, rows)
        win = pl.ds(pl.multiple_of(ws * GRID_W, NA_PAIR), NA_WIN)
        lanes = slice(pp * NA_PAIR, (pp + 1) * NA_PAIR)
        for hh in range(2):
            units.append((pp, hh, case, win, lanes, slice(hh * NA_HEAD_DIM, (hh + 1) * NA_HEAD_DIM)))
    return units


def _na_pipeline(n, before, middle, after, lookahead):
    for u in range(min(lookahead, n)):
        for f in before:
            f(u)
    for u in range(n):
        middle(u)
        if u + lookahead < n:
            for f in before:
                f(u + lookahead)
        for f in after:
            f(u)


def _head_rows(t, hh):
    row_head = lax.broadcasted_iota(jnp.int32, t.shape, 0) // NA_HEAD_DIM
    return jnp.where(row_head == hh, t, jnp.zeros_like(t))


def _heads_block_diag(t):
    lane_head = lax.broadcasted_iota(jnp.int32, t.shape, 1) // NA_HEAD_DIM
    zero = jnp.zeros_like(t)
    return jnp.concatenate([jnp.where(lane_head == 0, t, zero), jnp.where(lane_head == 1, t, zero)], axis=0)


def _na_fwd(q_t, k, v_t, bias_tab):
    L = k.shape[0]
    rows = L // GRID_W
    step_w = NA_PAIRS_PER_STEP * NA_PAIR

    def body(q_ref, k_ref, v_ref, bt_ref, o_ref):
        units = _na_units(pl.program_id(1), rows)
        qk, probs = {}, {}

        def scores(u):
            _, hh, _, win, lanes, _ = units[u]
            qk[u] = _dot(k_ref[win, :], _head_rows(q_ref[:, lanes], hh))

        def softmax(u):
            _, hh, case, _, _, _ = units[u]
            exps, l = _na_softmax(qk.pop(u), bt_ref, hh, case)
            probs[u] = jnp.concatenate([t.astype(BF16) for t in exps], axis=0), l

        def output(u):
            _, _, _, win, lanes, hrows = units[u]
            e, l = probs.pop(u)
            o_ref[hrows, lanes] = _dot(v_ref[hrows, win], e) / l

        _na_pipeline(len(units), [scores], softmax, [output], lookahead=3)

    q_spec = pl.BlockSpec((NA_PAIR, step_w), lambda h, s: (h, s))
    return pl.pallas_call(
        body, name="na_fwd", grid=(NA_HEADS // 2, L // step_w),
        in_specs=[q_spec, pl.BlockSpec((L, NA_PAIR), lambda h, s: (0, h)),
                  pl.BlockSpec((NA_PAIR, L), lambda h, s: (h, 0)),
                  pl.BlockSpec((2, NA_CASES, NA_WIN, NA_PAIR), lambda h, s: (h, 0, 0, 0))],
        out_specs=q_spec,
        out_shape=jax.ShapeDtypeStruct((D_NA, L), F32),
        compiler_params=_cparams(("arbitrary", "arbitrary")),
    )(q_t, k, v_t, bias_tab)


def _na_bwd(q_t, q, k_t, k, v, bias_tab, out_t, d_out_t, d_out):
    L = k.shape[0]
    rows = L // GRID_W
    step_w = NA_PAIRS_PER_STEP * NA_PAIR

    def body(qt_ref, q_ref, kt_ref, k_ref, v_ref, bt_ref, ot_ref, dot_ref, do_ref, dq_ref, dk_ref, dv_ref, dbt_ref):
        @pl.when(pl.program_id(1) == 0)
        def _():
            dk_ref[...] = jnp.zeros_like(dk_ref)
            dv_ref[...] = jnp.zeros_like(dv_ref)
            dbt_ref[...] = jnp.zeros_like(dbt_ref)

        units = _na_units(pl.program_id(1), rows)
        qk, dp, dsb, pb = {}, {}, {}, {}

        def scores(u):
            _, hh, _, win, lanes, _ = units[u]
            qk[u] = _dot(k_ref[win, :], _head_rows(qt_ref[:, lanes], hh))

        def d_probs(u):
            _, hh, _, win, lanes, _ = units[u]
            dp[u] = _dot(v_ref[win, :], _head_rows(dot_ref[:, lanes].astype(BF16), hh))

        def softmax_bwd(u):
            _, hh, case, _, lanes, hrows = units[u]
            exps, l = _na_softmax(qk.pop(u), bt_ref, hh, case)
            inv_l = 1.0 / l
            delta = jnp.sum(dot_ref[hrows, lanes] * ot_ref[hrows, lanes], axis=0, keepdims=True)
            d_p = dp.pop(u)
            ds_blocks, p_blocks = [], []
            for blk, e in zip(_na_blocks(), exps):
                p = e * inv_l
                ds = p * (d_p[blk, :] - delta)
                dbt_ref[hh, case, blk, :] += ds
                ds_blocks.append(ds.astype(BF16))
                p_blocks.append(p.astype(BF16))
            dsb[u] = jnp.concatenate(ds_blocks, axis=0)
            pb[u] = jnp.concatenate(p_blocks, axis=0)

        def d_query(u):
            _, _, _, win, lanes, hrows = units[u]
            dq_ref[hrows, lanes] = _dot(kt_ref[hrows, win], dsb[u]) * (NA_HEAD_DIM ** -0.5)

        def d_keys_values(u):
            pp, hh, _, win, _, _ = units[u]
            if hh == 1:
                tokens = slice(pp * NA_PAIR, (pp + 1) * NA_PAIR)
                dk_ref[win, :] += _dot(jnp.concatenate([dsb.pop(u - 1), dsb.pop(u)], axis=1), _heads_block_diag(q_ref[tokens, :]))
                dv_ref[win, :] += _dot(jnp.concatenate([pb.pop(u - 1), pb.pop(u)], axis=1), _heads_block_diag(do_ref[tokens, :]))

        _na_pipeline(len(units), [scores, d_probs], softmax_bwd, [d_query, d_keys_values], lookahead=2)

    t_tile = pl.BlockSpec((NA_PAIR, step_w), lambda h, s: (h, s))
    tile = pl.BlockSpec((step_w, NA_PAIR), lambda h, s: (s, h))
    t_full = pl.BlockSpec((NA_PAIR, L), lambda h, s: (h, 0))
    full = pl.BlockSpec((L, NA_PAIR), lambda h, s: (0, h))
    bt = pl.BlockSpec((2, NA_CASES, NA_WIN, NA_PAIR), lambda h, s: (h, 0, 0, 0))
    tok = jax.ShapeDtypeStruct((L, D_NA), F32)
    return pl.pallas_call(
        body, name="na_bwd", grid=(NA_HEADS // 2, L // step_w),
        in_specs=[t_tile, tile, t_full, full, full, bt, t_tile, t_tile, tile],
        out_specs=[t_tile, full, full, bt],
        out_shape=[jax.ShapeDtypeStruct((D_NA, L), F32), tok, tok, jax.ShapeDtypeStruct(bias_tab.shape, F32)],
        compiler_params=_cparams(("arbitrary", "arbitrary")),
    )(q_t, q, k_t, k, v, bias_tab, out_t, d_out_t, d_out)


def _branch_fwd_values(ys, zs, yn, zn, wglu, bglu):
    g1, t = _gelu_parts(ys)
    lin = _dot(g1.astype(BF16), wglu) + bglu
    sg = _sigmoid(lin)
    ys2 = g1 * sg
    sz, szs = _silu_parts(zs)
    sn, sns = _silu_parts(zn)
    return g1, t, sg, ys2, sz, szs, sn, sns


def _branch_fwd(y_ssm_c, z_s, y_na_t, z_n, w_glu, b_glu, tm=512):
    L = z_s.shape[0]

    def body(ys_ref, zs_ref, yn_ref, zn_ref, w_ref, b_ref, cat_ref, scr):
        yn = yn_ref[...].T
        g1, t, sg, ys2, sz, szs, sn, sns = _branch_fwd_values(
            _load_chunks(ys_ref, scr), zs_ref[...], yn, zn_ref[...], w_ref[...], b_ref[...])
        cat_ref[:, 0:512] = (ys2 * sz).astype(BF16)
        cat_ref[:, 512:1024] = (yn * sn).astype(BF16)

    tile = pl.BlockSpec((tm, 512), lambda i: (i, 0))
    return pl.pallas_call(
        body, name="branch_fwd", grid=(L // tm,),
        in_specs=[_chunk_spec(tm), tile, _heads_t_spec(tm), tile, pl.BlockSpec((512, 512), lambda i: (0, 0)),
                  pl.BlockSpec((1, 512), lambda i: (0, 0))],
        out_specs=pl.BlockSpec((tm, 1024), lambda i: (i, 0)),
        out_shape=jax.ShapeDtypeStruct((L, 1024), BF16),
        scratch_shapes=[_chunk_scratch(tm)],
        compiler_params=_cparams(("arbitrary",)),
    )(y_ssm_c, z_s, y_na_t, z_n, w_glu, b_glu)


def _branch_bwd(y_ssm_c, z_s, y_na_t, z_n, w_glu, b_glu, d_cat, tm=512):
    L = z_s.shape[0]

    def body(ys_ref, zs_ref, yn_ref, zn_ref, w_ref, b_ref, dc_ref,
             dys_ref, dzs_ref, dynt_ref, dyn_ref, dzn_ref, dw_ref, db_ref, scr):
        @pl.when(pl.program_id(0) == 0)
        def _():
            dw_ref[...] = jnp.zeros_like(dw_ref)
            db_ref[...] = jnp.zeros_like(db_ref)

        ys, zs, yn, zn = _load_chunks(ys_ref, scr), zs_ref[...], yn_ref[...].T, zn_ref[...]
        w = w_ref[...]
        g1, t, sg, ys2, sz, szs, sn, sns = _branch_fwd_values(ys, zs, yn, zn, w, b_ref[...])
        dys3 = dc_ref[:, 0:512].astype(F32)
        dyn2 = dc_ref[:, 512:1024].astype(F32)
        dzs_ref[...] = (dys3 * ys2 * _silu_grad(zs, szs)).astype(BF16)
        dys2 = dys3 * sz
        dlin = dys2 * g1 * sg * (1.0 - sg)
        dlb = dlin.astype(BF16)
        dg1 = dys2 * sg + _dot_nt(dlb, w)
        dw_ref[...] += _dot_tn(g1.astype(BF16), dlb)
        db_ref[...] += jnp.sum(dlin, axis=0, keepdims=True)
        _store_chunks(dg1 * _gelu_grad(ys, t), scr, dys_ref, BF16)
        dyn = dyn2 * sn
        dynt_ref[...] = dyn.T
        dyn_ref[...] = dyn.astype(BF16)
        dzn_ref[...] = (dyn2 * yn * _silu_grad(zn, sns)).astype(BF16)

    tile = pl.BlockSpec((tm, 512), lambda i: (i, 0))
    wspec = pl.BlockSpec((512, 512), lambda i: (0, 0))
    bspec = pl.BlockSpec((1, 512), lambda i: (0, 0))
    tok = jax.ShapeDtypeStruct((L, 512), BF16)
    return pl.pallas_call(
        body, name="branch_bwd", grid=(L // tm,),
        in_specs=[_chunk_spec(tm), tile, _heads_t_spec(tm), tile, wspec, bspec, pl.BlockSpec((tm, 1024), lambda i: (i, 0))],
        out_specs=[_chunk_spec(tm), tile, _heads_t_spec(tm), tile, tile, wspec, bspec],
        out_shape=[jax.ShapeDtypeStruct((N_BLOCKS, L // CHUNK, CHUNK_W), BF16), tok, jax.ShapeDtypeStruct((D_NA, L), F32),
                   tok, tok,
                   jax.ShapeDtypeStruct((512, 512), F32), jax.ShapeDtypeStruct((1, 512), F32)],
        scratch_shapes=[_chunk_scratch(tm)],
        compiler_params=_cparams(("arbitrary",)),
    )(y_ssm_c, z_s, y_na_t, z_n, w_glu, b_glu, d_cat)


def _head(x, p, target, cat, w_out, g_post, w_ple_g, g_ple, w_pg, tm=512):
    L = x.shape[0]
    pw = w_ple_g.shape[2]

    def body(x_ref, p_ref, t_ref, cat_ref, wo_ref, gpo_ref, wp_ref, gpl_ref, wg_ref,
             loss_ref, dh1_ref, dcat_ref, dwo_ref, dgpo_ref, dwp_ref, dgpl_ref, dwg_ref):
        @pl.when(pl.program_id(0) == 0)
        def _():
            for r in (loss_ref, dwo_ref, dgpo_ref, dwp_ref, dgpl_ref, dwg_ref):
                r[...] = jnp.zeros_like(r)

        cat_b = cat_ref[...]
        wo, wg = wo_ref[...], wg_ref[...]
        g_po, g_pl = gpo_ref[...], gpl_ref[...]
        mix = _dot(cat_b, wo)
        p_b = p_ref[...].astype(BF16)
        ep = jnp.concatenate([_dot(p_b, wp_ref[j]) for j in range(N_CHIPS)], axis=1)
        nm, r2 = _rms(mix)
        h1 = x_ref[...] + nm * g_po
        ne, r3 = _rms(ep)
        e = ne * g_pl
        h1_b = h1.astype(BF16)
        gate = _sigmoid(_dot(h1_b, wg))
        h2 = h1 + gate * e
        diff = h2 - t_ref[...]
        loss_ref[...] += (0.5 / D_MODEL) * jnp.sum(diff * diff).reshape(1, 1)

        dh2 = diff * (1.0 / D_MODEL)
        de = dh2 * gate
        dgl = (dh2 * e * gate * (1.0 - gate)).astype(BF16)
        dh1 = dh2 + _dot_nt(dgl, wg)
        dwg_ref[...] += _dot_tn(h1_b, dgl)
        dgpo_ref[...] += jnp.sum(dh1 * nm, axis=0, keepdims=True)
        dmix = _rms_bwd(dh1 * g_po, nm, r2).astype(BF16)
        dcat_ref[...] = _dot_nt(dmix, wo).astype(BF16)
        dwo_ref[...] += _dot_tn(cat_b, dmix)
        dh1_ref[...] = dh1
        dgpl_ref[...] += jnp.sum(de * ne, axis=0, keepdims=True)
        dep = _rms_bwd(de * g_pl, ne, r3).astype(BF16)
        for j in range(N_CHIPS):
            dwp_ref[j] += _dot_tn(p_b, dep[:, j * pw:(j + 1) * pw])

    tile = lambda w: pl.BlockSpec((tm, w), lambda i: (i, 0))
    const = _resident
    sds = jax.ShapeDtypeStruct
    return pl.pallas_call(
        body, name="head", grid=(L // tm,),
        in_specs=[tile(D_MODEL), tile(D_PLE), tile(D_MODEL), tile(1024), const(1024, D_MODEL), const(1, D_MODEL),
                  const(N_CHIPS, D_PLE, pw), const(1, D_MODEL), const(D_MODEL, D_MODEL)],
        out_specs=[const(1, 1), tile(D_MODEL), tile(1024), const(1024, D_MODEL), const(1, D_MODEL),
                   const(N_CHIPS, D_PLE, pw), const(1, D_MODEL), const(D_MODEL, D_MODEL)],
        out_shape=[sds((1, 1), F32), sds((L, D_MODEL), F32), sds((L, 1024), BF16), sds((1024, D_MODEL), F32),
                   sds((1, D_MODEL), F32), sds((N_CHIPS, D_PLE, pw), F32), sds((1, D_MODEL), F32),
                   sds((D_MODEL, D_MODEL), F32)],
        compiler_params=_cparams(("arbitrary",)),
    )(x, p, target, cat, w_out, g_post, w_ple_g, g_ple, w_pg)


def _dproj_specs(tm):
    tile = pl.BlockSpec((tm, 512), lambda i: (i, 0))
    return [_chunk_spec(tm), tile, _heads_t_spec(tm), tile, tile, tile]


_DPROJ_ORDER = (3, 4, 5, 1, 2, 0)


def _dproj_part(refs, scr, i):
    if i == 0:
        val = _load_chunks(refs[0], scr)
    elif i == 2:
        val = refs[2][...].T
    else:
        val = refs[i][...]
    return val.astype(BF16)


def _dproj_pieces(i, wn):
    lo, hi = 512 * i, 512 * (i + 1)
    pieces = []
    for j in range(N_CHIPS):
        a, b = max(lo, j * wn), min(hi, (j + 1) * wn)
        if a < b:
            pieces.append((j, slice(a - j * wn, b - j * wn), slice(a - lo, b - lo)))
    return pieces


def _in_proj_bwd_w(x, g_col, w_in_g, dparts, tm=512):
    L = x.shape[0]
    wn = D_IN_PROJ // N_CHIPS
    steps = L // tm

    def body(x_ref, g_ref, w_ref, *refs):
        dw_ref, dg_ref, scr = refs[-3], refs[-2], refs[-1]

        @pl.when(pl.program_id(0) == 0)
        def _():
            dw_ref[...] = jnp.zeros_like(dw_ref)

        n, _ = _rms(x_ref[...])
        nb = n.astype(BF16)
        for i in _DPROJ_ORDER:
            part = _dproj_part(refs[:-3], scr, i)
            for j, w_cols, p_cols in _dproj_pieces(i, wn):
                dw_ref[j, :, w_cols] += _dot_tn(nb, part[:, p_cols])

        @pl.when(pl.program_id(0) == steps - 1)
        def _():
            g = g_ref[...]
            dg = jnp.zeros_like(g)
            for j in range(N_CHIPS):
                a = dw_ref[j]
                dg = dg + jnp.sum(a * w_ref[j].astype(F32), axis=1, keepdims=True)
                dw_ref[j] = a * g
            dg_ref[...] = dg

    return pl.pallas_call(
        body, name="in_proj_bwd_w", grid=(steps,),
        in_specs=[pl.BlockSpec((tm, D_MODEL), lambda i: (i, 0)), _resident(D_MODEL, 1), _resident(N_CHIPS, D_MODEL, wn)]
        + _dproj_specs(tm),
        out_specs=[_resident(N_CHIPS, D_MODEL, wn), _resident(D_MODEL, 1)],
        out_shape=[jax.ShapeDtypeStruct((N_CHIPS, D_MODEL, wn), F32), jax.ShapeDtypeStruct((D_MODEL, 1), F32)],
        scratch_shapes=[_chunk_scratch(tm)],
        compiler_params=_cparams(("arbitrary",)),
    )(x, g_col, w_in_g, *dparts)


def _in_proj_bwd_x(x, g_pre, w_in_g, d_h1, dparts, pair_sums, tm=512):
    L = x.shape[0]
    wn = w_in_g.shape[2]
    n_ps = len(pair_sums)
    steps = L // tm

    def body(*refs):
        x_ref, g_ref, w_ref, dh1_ref = refs[:4]
        dparts_refs = refs[4:10]
        dx_ref = refs[10 + n_ps]
        scr = refs[11 + 2 * n_ps]
        scatter = _ChipScatter(refs[10:10 + n_ps], refs[11 + n_ps:11 + 2 * n_ps], refs[12 + 2 * n_ps:16 + 2 * n_ps],
                               refs[16 + 2 * n_ps:])
        pl.when(pl.program_id(0) == 0)(scatter.start)
        pl.when(pl.program_id(0) == steps - 1)(scatter.finish)

        dhn = None
        for i in _DPROJ_ORDER:
            part = _dproj_part(dparts_refs, scr, i)
            for j, w_cols, p_cols in _dproj_pieces(i, wn):
                term = _dot_nt(part[:, p_cols], w_ref[j, :, w_cols])
                dhn = term if dhn is None else dhn + term
        n, r = _rms(x_ref[...])
        dx_ref[...] = dh1_ref[...] + _rms_bwd(dhn * g_ref[...], n, r)

    wide = pl.BlockSpec((tm, D_MODEL), lambda i: (i, 0))
    outs = pl.pallas_call(
        body, name="in_proj_bwd_x", grid=(steps,),
        in_specs=[wide, _resident(1, D_MODEL), _resident(N_CHIPS, D_MODEL, wn), wide] + _dproj_specs(tm) + _hbm_specs(n_ps),
        out_specs=[wide] + _hbm_specs(n_ps),
        out_shape=[jax.ShapeDtypeStruct((L, D_MODEL), F32)] + [jax.ShapeDtypeStruct(p.shape, p.dtype) for p in pair_sums],
        scratch_shapes=[_chunk_scratch(tm)] + _scatter_scratch(pair_sums),
        compiler_params=_cparams(("arbitrary",), has_side_effects=True),
    )(x, g_pre, w_in_g, d_h1, *dparts, *pair_sums)
    return outs[0], outs[1:]


def _mesh_position():
    x, y, c = lax.axis_index("x"), lax.axis_index("y"), lax.axis_index("c")
    chips = [(1 - x, y), (x, 1 - y), (1 - x, 1 - y)]
    return x, y, c, chips


def _chip_index(cx, cy):
    return 2 * cx + cy


def _hbm_specs(n):
    return [pl.BlockSpec(memory_space=pl.ANY)] * n


def _gather_chips(shards, name):
    n = len(shards)

    def body(*refs):
        gather = _ChipGather(refs[:n], refs[n:2 * n], refs[2 * n:])
        gather.start()
        gather.forward()
        gather.finish()

    return pl.pallas_call(
        body, name=name, in_specs=_hbm_specs(n), out_specs=_hbm_specs(n),
        out_shape=_gather_out_shapes(shards), scratch_shapes=_gather_semaphores(n),
        compiler_params=pltpu.CompilerParams(has_side_effects=True),
    )(*shards)


def _gather_out_shapes(shards):
    return [jax.ShapeDtypeStruct((N_CHIPS,) + s.shape, s.dtype) for s in shards]


def _gather_semaphores(n):
    sem = pltpu.SemaphoreType.DMA
    return [sem((n, 3)), sem((n, 3)), sem((n, 3)), sem((n, 3)), sem((n,)), sem((n,))]


class _ChipGather:
    def __init__(self, ins, outs, sems):
        self.ins, self.outs = ins, outs
        self.send1, self.recv1, self.send2, self.recv2, self.send3, self.recv3 = sems
        self.x, self.y, self.c, self.chips = _mesh_position()
        self.me = _chip_index(self.x, self.y)
        self.sibling = (self.x, self.y, 1 - self.c)

    def _half(self, a, chip, core):
        hr = self.outs[a].shape[1] // 2
        return self.outs[a].at[chip, pl.ds(core * hr, hr)]

    def _own(self, a):
        return pltpu.make_async_remote_copy(
            src_ref=self.ins[a], dst_ref=self.outs[a].at[self.me], send_sem=self.send3.at[a], recv_sem=self.recv3.at[a],
            device_id=self.sibling, device_id_type=MESH)

    def _to_chip(self, a, j):
        hr = self.ins[a].shape[0] // 2
        return pltpu.make_async_remote_copy(
            src_ref=self.ins[a].at[pl.ds(self.c * hr, hr)], dst_ref=self._half(a, self.me, self.c),
            send_sem=self.send1.at[a, j], recv_sem=self.recv1.at[a, j], device_id=(*self.chips[j], self.c), device_id_type=MESH)

    def _from_chip(self, a, j):
        landed = self._half(a, _chip_index(*self.chips[j]), self.c)
        return pltpu.make_async_remote_copy(
            src_ref=landed, dst_ref=landed, send_sem=self.send1.at[a, j], recv_sem=self.recv1.at[a, j],
            device_id=(*self.chips[j], self.c), device_id_type=MESH)

    def _to_sibling(self, a, j, core):
        part = self._half(a, _chip_index(*self.chips[j]), core)
        return pltpu.make_async_remote_copy(
            src_ref=part, dst_ref=part, send_sem=self.send2.at[a, j], recv_sem=self.recv2.at[a, j],
            device_id=self.sibling, device_id_type=MESH)

    def _each(self):
        return [(a, j) for a in range(len(self.ins)) for j in range(3)]

    def start(self):
        for a in range(len(self.ins)):
            self._own(a).start()
        for a, j in self._each():
            self._to_chip(a, j).start()

    def forward(self):
        for a, j in self._each():
            self._from_chip(a, j).wait_recv()
            self._to_sibling(a, j, self.c).start()

    def finish(self):
        for a, j in self._each():
            self._to_sibling(a, j, 1 - self.c).wait_recv()
        for a, j in self._each():
            self._to_chip(a, j).wait_send()
            self._to_sibling(a, j, self.c).wait_send()
        for a in range(len(self.ins)):
            self._own(a).wait()


def _pair_exchange(grads):
    n = len(grads)

    def body(*refs):
        ins, outs = refs[:n], refs[n:2 * n]
        send, recv = refs[2 * n:]
        x, y, c, _ = _mesh_position()
        copies = []
        for a in range(n):
            hr = ins[a].shape[1] // 2
            cp = pltpu.make_async_remote_copy(
                src_ref=ins[a].at[:, pl.ds((1 - c) * hr, hr)], dst_ref=outs[a],
                send_sem=send.at[a], recv_sem=recv.at[a], device_id=(x, y, 1 - c), device_id_type=MESH)
            cp.start()
            copies.append(cp)
        for cp in copies:
            cp.wait()

    sem = pltpu.SemaphoreType.DMA
    return pl.pallas_call(
        body, name="pair_exchange", in_specs=_hbm_specs(n), out_specs=_hbm_specs(n),
        out_shape=[jax.ShapeDtypeStruct((g.shape[0], g.shape[1] // 2, g.shape[2]), g.dtype) for g in grads],
        scratch_shapes=[sem((n,)), sem((n,))],
        compiler_params=pltpu.CompilerParams(has_side_effects=True),
    )(*grads)


def _pair_add(core, grad, other, tr, out_dtype):
    hr = other.shape[1]
    cdim = other.shape[2]
    nb = hr // tr

    def body(core_ref, g_ref, o_ref, out_ref):
        out_ref[...] = (g_ref[...] + o_ref[...]).astype(out_dtype)

    return pl.pallas_call(
        body, name="pair_add",
        grid_spec=pltpu.PrefetchScalarGridSpec(
            num_scalar_prefetch=1, grid=(N_CHIPS, nb),
            in_specs=[pl.BlockSpec((1, tr, cdim), lambda j, i, core_ref: (j, core_ref[0] * nb + i, 0)),
                      pl.BlockSpec((1, tr, cdim), lambda j, i, core_ref: (j, i, 0))],
            out_specs=pl.BlockSpec((1, tr, cdim), lambda j, i, core_ref: (j, i, 0))),
        out_shape=jax.ShapeDtypeStruct(other.shape, out_dtype),
        compiler_params=_cparams(("arbitrary", "arbitrary")),
    )(core, grad, other)


def _scatter_scratch(parts):
    sem = pltpu.SemaphoreType.DMA
    n = len(parts)
    return [sem((n, 3)), sem((n, 3)), sem((n,)), sem((n,))] + [pltpu.VMEM(p.shape[1:], p.dtype) for p in parts]


class _ChipScatter:
    def __init__(self, ins, outs, sems, staged):
        self.ins, self.outs, self.staged = ins, outs, staged
        self.send, self.recv, self.load_sem, self.store_sem = sems
        self.x, self.y, self.c, self.chips = _mesh_position()
        self.me = _chip_index(self.x, self.y)

    def _load(self, a):
        return pltpu.make_async_copy(self.ins[a].at[self.me], self.staged[a], self.load_sem.at[a])

    def _store(self, a):
        return pltpu.make_async_copy(self.staged[a], self.outs[a].at[self.me], self.store_sem.at[a])

    def _to_chip(self, a, j):
        return pltpu.make_async_remote_copy(
            src_ref=self.ins[a].at[_chip_index(*self.chips[j])], dst_ref=self.outs[a].at[self.me],
            send_sem=self.send.at[a, j], recv_sem=self.recv.at[a, j], device_id=(*self.chips[j], self.c), device_id_type=MESH)

    def start(self):
        for a in range(len(self.ins)):
            self._load(a).start()
            for j in range(3):
                self._to_chip(a, j).start()

    def finish(self):
        for a in range(len(self.ins)):
            self._load(a).wait()
            self._store(a).start()
        for a in range(len(self.ins)):
            for j in range(3):
                self._to_chip(a, j).wait()
            self._store(a).wait()


def _chip_add(core, recv, tr):
    hr, cdim = recv.shape[1], recv.shape[2]
    nb = hr // tr

    def body(core_ref, r_ref, out_ref):
        out_ref[...] = ((r_ref[0].astype(F32) + r_ref[1].astype(F32)) + r_ref[2].astype(F32)) + r_ref[3].astype(F32)

    return pl.pallas_call(
        body, name="chip_add",
        grid_spec=pltpu.PrefetchScalarGridSpec(
            num_scalar_prefetch=1, grid=(nb,),
            in_specs=[pl.BlockSpec((N_CHIPS, tr, cdim), lambda i, core_ref: (0, i, 0))],
            out_specs=pl.BlockSpec((tr, cdim), lambda i, core_ref: (core_ref[0] * nb + i, 0))),
        out_shape=jax.ShapeDtypeStruct((2 * hr, cdim), F32),
        compiler_params=_cparams(("arbitrary",)),
    )(core, recv)


def _pair_gather(fulls):
    n = len(fulls)

    def body(*refs):
        outs = refs[n:2 * n]
        send, recv = refs[2 * n:]
        x, y, c, _ = _mesh_position()
        copies = []
        for a in range(n):
            hr = outs[a].shape[0] // 2
            mine = outs[a].at[pl.ds(c * hr, hr)]
            cp = pltpu.make_async_remote_copy(
                src_ref=mine, dst_ref=mine, send_sem=send.at[a], recv_sem=recv.at[a],
                device_id=(x, y, 1 - c), device_id_type=MESH)
            cp.start()
            copies.append(cp)
        for cp in copies:
            cp.wait()

    sem = pltpu.SemaphoreType.DMA
    return pl.pallas_call(
        body, name="pair_gather", in_specs=_hbm_specs(n), out_specs=_hbm_specs(n),
        out_shape=[jax.ShapeDtypeStruct(f.shape, f.dtype) for f in fulls],
        input_output_aliases={a: a for a in range(n)},
        scratch_shapes=[sem((n,)), sem((n,))],
        compiler_params=pltpu.CompilerParams(has_side_effects=True),
    )(*fulls)


def _row_tile(rows):
    if rows <= 512:
        return rows
    for t in (512, 256, 128, 64, 32, 16, 8):
        if rows % t == 0:
            return t
    raise ValueError(rows)


def _pair_sums(core, grads, ici_dtypes):
    others = _pair_exchange(grads)
    return [_pair_add(core, g, o, _row_tile(o.shape[1]), dt) for g, o, dt in zip(grads, others, ici_dtypes)]


def _finish_reduce(core, landed):
    return _pair_gather([_chip_add(core, r, _row_tile(r.shape[1])) for r in landed])


def _adamw(w, g, m, v):
    rows, cols = w.shape
    one_block = rows % 8 != 0 or rows * max(cols, 128) * 4 <= (1 << 20)
    tr = rows if one_block else _row_tile(rows)

    def body(w_ref, g_ref, m_ref, v_ref, d_ref, nm_ref, nv_ref):
        g_ = g_ref[...]
        m_ = ADAM_B1 * m_ref[...] + (1.0 - ADAM_B1) * g_
        v_ = ADAM_B2 * v_ref[...] + (1.0 - ADAM_B2) * (g_ * g_)
        m_hat = m_ / (1.0 - ADAM_B1 ** ADAM_STEP)
        v_hat = v_ / (1.0 - ADAM_B2 ** ADAM_STEP)
        d_ref[...] = -ADAM_LR * (m_hat / (jnp.sqrt(v_hat) + ADAM_EPS) + ADAM_WD * w_ref[...])
        nm_ref[...] = m_
        nv_ref[...] = v_

    spec = pl.BlockSpec((tr, cols), lambda i: (i, 0))
    shp = jax.ShapeDtypeStruct((rows, cols), F32)
    return pl.pallas_call(
        body, name="adamw", grid=(rows // tr,), in_specs=[spec] * 4, out_specs=[spec] * 3,
        out_shape=[shp] * 3, compiler_params=_cparams(("arbitrary",)),
    )(w, g, m, v)


_SMALL = ["norm_pre", "norm_post", "ssm_a_re", "ssm_a_im", "ssm_log_dt", "ssm_b_re", "ssm_b_im",
          "ssm_c_re", "ssm_c_im", "ssm_d", "b_glu", "na_rpb", "ple_norm"]
_BIG = ["w_in", "w_glu", "w_out", "w_ple", "w_ple_gate"]
_WEIGHTS = ["norm_pre", "norm_post", "w_in", "ssm_a_re", "ssm_a_im", "ssm_log_dt", "ssm_b_re", "ssm_b_im",
            "ssm_c_re", "ssm_c_im", "ssm_d", "w_glu", "b_glu", "na_rpb", "w_out", "w_ple", "ple_norm", "w_ple_gate"]
_SMALL_ROWS = 2176


def _pack_small(tensors, tail=None):
    parts = [tensors[n].reshape(-1) for n in _SMALL] + ([] if tail is None else [tail.reshape(-1)])
    flat = jnp.concatenate(parts)
    flat = jnp.pad(flat, (0, _SMALL_ROWS * 128 - flat.shape[0]))
    return flat.reshape(_SMALL_ROWS, 128)


def _unpack_small(packed, shapes):
    flat = packed.reshape(-1)
    out, off = {}, 0
    for n in _SMALL:
        size = int(np.prod(shapes[n]))
        out[n] = flat[off:off + size].reshape(shapes[n])
        off += size
    return out


def _local_grads(x, p, target, wts):
    ssm_names = ["ssm_a_re", "ssm_a_im", "ssm_log_dt", "ssm_b_re", "ssm_b_im", "ssm_c_re", "ssm_c_im", "ssm_d"]
    ssm_params = [wts[n][0] for n in ssm_names]
    blk, blk_vjp = jax.vjp(_ssm_block_params, *ssm_params)
    shard = lambda n: wts[n][0].astype(BF16)
    (m_mat, ws_mat, wot_mat, a16), (w_in_g,) = _ssm_chunk_matrices(blk, [shard("w_in")])
    seq = x.shape[0]
    bias_rows, bias_rows_vjp = jax.vjp(_na_bias_rows, wts["na_rpb"][0])
    bias_tab = _na_bias_table(bias_rows, seq // GRID_W)

    (u_c, z_s, q_t, q, k_t, k, v_t, v, z_n), gathered = _in_proj(
        x, wts["norm_pre"], w_in_g, [shard(n) for n in _BIG if n != "w_in"])
    w_glu, w_out, w_ple_g, w_pg = (gathered[0].reshape(512, 512), gathered[1].reshape(1024, 1024), gathered[2],
                                   gathered[3].reshape(1024, 1024))
    s_in = _block_matmul([(u_c, ws_mat, False)], "ssm_chunk_states")
    s_prev = _ssm_state_scan(s_in, a16)
    y_ssm_c = _block_matmul([(u_c, m_mat, False), (s_prev, wot_mat, True)], "ssm_chunk_out")
    y_na_t = _na_fwd(q_t, k, v_t, bias_tab)
    cat = _branch_fwd(y_ssm_c, z_s, y_na_t, z_n, w_glu, wts["b_glu"])

    (loss, d_h1, d_cat, d_w_out, d_g_post, d_w_ple, d_g_ple, d_w_pg) = _head(
        x, p, target, cat, w_out, wts["norm_post"], w_ple_g, wts["ple_norm"], w_pg)
    dy_c, d_z_s, d_y_na_t, d_y_na, d_z_n, d_w_glu, d_b_glu = _branch_bwd(
        y_ssm_c, z_s, y_na_t, z_n, w_glu, wts["b_glu"], d_cat)
    d_q_t, d_k, d_v, d_bias_tab = _na_bwd(q_t, q, k_t, k, v, bias_tab, y_na_t, d_y_na_t, d_y_na)

    d_prev = _block_matmul([(dy_c, wot_mat, False)], "ssm_bwd_states")
    g_st, d_a16 = _ssm_state_scan_bwd(d_prev, s_prev, a16)
    d_u_c = _block_matmul([(dy_c, m_mat, True), (g_st, ws_mat, True)], "ssm_bwd_in", out_dtype=BF16)
    d_m = _block_matmul_tn(u_c, dy_c, "ssm_grad_m")
    d_ws = _block_matmul_tn(u_c, g_st, "ssm_grad_ws")
    d_wot = _block_matmul_tn(dy_c, s_prev, "ssm_grad_wot")
    d_ssm = blk_vjp(tuple(_ssm_chunk_matrices_bwd(blk, d_m, d_ws, d_wot, d_a16)))
    (d_rpb,) = bias_rows_vjp(_na_bias_table_bwd(d_bias_tab, seq // GRID_W))

    dparts = [d_u_c, d_z_s, d_q_t, d_k, d_v, d_z_n]
    d_w_in, d_g_pre = _in_proj_bwd_w(x, wts["norm_pre"].reshape(D_MODEL, 1), w_in_g, dparts)

    small = {"norm_pre": d_g_pre, "norm_post": d_g_post, "b_glu": d_b_glu, "na_rpb": d_rpb, "ple_norm": d_g_ple}
    for n, g in zip(ssm_names, d_ssm):
        small[n] = g
    big = {"w_in": d_w_in, "w_glu": d_w_glu.reshape(N_CHIPS, 128, 512), "w_out": d_w_out.reshape(N_CHIPS, 256, 1024),
           "w_ple": d_w_ple, "w_ple_gate": d_w_pg.reshape(N_CHIPS, 256, 1024)}
    return loss, small, big, (x, wts["norm_pre"], w_in_g, d_h1, dparts)


def kernel(x, p, norm_pre, norm_post, w_in, ssm_a_re, ssm_a_im, ssm_log_dt, ssm_b_re, ssm_b_im, ssm_c_re, ssm_c_im, ssm_d, w_glu, b_glu, na_rpb, w_out, w_ple, ple_norm, w_ple_gate, loss_target, m_norm_pre, m_norm_post, m_w_in, m_ssm_a_re, m_ssm_a_im, m_ssm_log_dt, m_ssm_b_re, m_ssm_b_im, m_ssm_c_re, m_ssm_c_im, m_ssm_d, m_w_glu, m_b_glu, m_na_rpb, m_w_out, m_w_ple, m_ple_norm, m_w_ple_gate, v_norm_pre, v_norm_post, v_w_in, v_ssm_a_re, v_ssm_a_im, v_ssm_log_dt, v_ssm_b_re, v_ssm_b_im, v_ssm_c_re, v_ssm_c_im, v_ssm_d, v_w_glu, v_b_glu, v_na_rpb, v_w_out, v_w_ple, v_ple_norm, v_w_ple_gate):
    wts = dict(norm_pre=norm_pre, norm_post=norm_post, w_in=w_in, ssm_a_re=ssm_a_re, ssm_a_im=ssm_a_im,
               ssm_log_dt=ssm_log_dt, ssm_b_re=ssm_b_re, ssm_b_im=ssm_b_im, ssm_c_re=ssm_c_re, ssm_c_im=ssm_c_im,
               ssm_d=ssm_d, w_glu=w_glu, b_glu=b_glu, na_rpb=na_rpb, w_out=w_out, w_ple=w_ple, ple_norm=ple_norm,
               w_ple_gate=w_ple_gate)
    mom_m = dict(norm_pre=m_norm_pre, norm_post=m_norm_post, w_in=m_w_in, ssm_a_re=m_ssm_a_re, ssm_a_im=m_ssm_a_im,
                 ssm_log_dt=m_ssm_log_dt, ssm_b_re=m_ssm_b_re, ssm_b_im=m_ssm_b_im, ssm_c_re=m_ssm_c_re,
                 ssm_c_im=m_ssm_c_im, ssm_d=m_ssm_d, w_glu=m_w_glu, b_glu=m_b_glu, na_rpb=m_na_rpb, w_out=m_w_out,
                 w_ple=m_w_ple, ple_norm=m_ple_norm, w_ple_gate=m_w_ple_gate)
    mom_v = dict(norm_pre=v_norm_pre, norm_post=v_norm_post, w_in=v_w_in, ssm_a_re=v_ssm_a_re, ssm_a_im=v_ssm_a_im,
                 ssm_log_dt=v_ssm_log_dt, ssm_b_re=v_ssm_b_re, ssm_b_im=v_ssm_b_im, ssm_c_re=v_ssm_c_re,
                 ssm_c_im=v_ssm_c_im, ssm_d=v_ssm_d, w_glu=v_w_glu, b_glu=v_b_glu, na_rpb=v_na_rpb, w_out=v_w_out,
                 w_ple=v_w_ple, ple_norm=v_ple_norm, w_ple_gate=v_w_ple_gate)

    loss_part, small, big, input_grad_args = _local_grads(x[0], p[0, 0], loss_target[0], wts)

    core = lax.axis_index("c").astype(jnp.int32).reshape(1)
    small_packed = _pack_small(small, tail=loss_part).reshape(N_CHIPS, _SMALL_ROWS // N_CHIPS, 128)
    pair = _pair_sums(core, [big[n] for n in _BIG] + [small_packed], [BF16] * len(_BIG) + [F32])
    grad_x, landed = _in_proj_bwd_x(*input_grad_args, pair)
    reduced = _finish_reduce(core, landed)
    grads = dict(zip(_BIG, reduced[:-1]))
    (small_all,) = _gather_chips([reduced[-1]], "gather_small_grads")
    small_all = small_all.reshape(_SMALL_ROWS, 128)
    loss = small_all.reshape(-1)[sum(int(np.prod(wts[n].shape)) for n in _SMALL)]

    delta, new_m, new_v = {}, {}, {}
    for n in _BIG:
        shp = wts[n].shape
        d_, m_, v_ = _adamw(wts[n][0], grads[n], mom_m[n][0], mom_v[n][0])
        grads[n] = grads[n].reshape(shp)
        delta[n], new_m[n], new_v[n] = d_.reshape(shp), m_.reshape(shp), v_.reshape(shp)
    grads.update(_unpack_small(small_all, {n: wts[n].shape for n in _SMALL}))
    for n in _SMALL:
        shp = wts[n].shape
        rows_cols = (int(np.prod(shp[:-1])), shp[-1])
        d_, m_, v_ = _adamw(*[t.reshape(rows_cols) for t in (wts[n], grads[n], mom_m[n], mom_v[n])])
        delta[n], new_m[n], new_v[n] = d_.reshape(shp), m_.reshape(shp), v_.reshape(shp)

    return (loss, grad_x[None], *[grads[n] for n in _WEIGHTS], *[delta[n] for n in _WEIGHTS],
            *[new_m[n] for n in _WEIGHTS], *[new_v[n] for n in _WEIGHTS])
```

```python
import math

import jax
import jax.numpy as jnp
import numpy as np
from jax import lax
from jax.experimental import pallas as pl
from jax.experimental.pallas import tpu as pltpu

F32 = jnp.float32
BF16 = jnp.bfloat16

D_MODEL = 1024
D_PLE = 256
GRID_W = 64
D_SSM = 512
SSM_GROUP = 16
N_GROUPS = 32
SSM_STATE = 64
D_NA = 512
NA_HEADS = 8
NA_HEAD_DIM = 64
NA_ROWS = 8
NA_COLS = 16
D_IN_PROJ = 3072
EPS = 1e-6

CHUNK = 16
GROUPS_PER_BLOCK = 8
N_BLOCKS = N_GROUPS // GROUPS_PER_BLOCK
BLOCK_CH = GROUPS_PER_BLOCK * SSM_GROUP
BLOCK_ST = GROUPS_PER_BLOCK * SSM_STATE
CHUNK_W = CHUNK * BLOCK_CH
STATE_W = 4 * BLOCK_ST

N_CHIPS = 4
MESH = pl.DeviceIdType.MESH

ADAM_LR = 0.001
ADAM_B1 = 0.9
ADAM_B2 = 0.999
ADAM_EPS = 1e-08
ADAM_WD = 0.01
ADAM_STEP = 10

VMEM_LIMIT = 52 * 1024 * 1024
HIGHEST = lax.Precision.HIGHEST


def _cparams(sem=None, **kw):
    if sem is not None:
        kw["dimension_semantics"] = sem
    return pltpu.CompilerParams(vmem_limit_bytes=VMEM_LIMIT, **kw)


def _resident(*shape):
    return pl.BlockSpec(shape, lambda *_: (0,) * len(shape), pipeline_mode=pl.Buffered(1))


def _dot(a, b, dims=((1,), (0,))):
    return lax.dot_general(a, b, (dims, ((), ())), preferred_element_type=F32)


def _dot_nt(a, b):
    return _dot(a, b, ((1,), (1,)))


def _dot_tn(a, b):
    return _dot(a, b, ((0,), (0,)))


def _sigmoid(x):
    return 1.0 / (1.0 + jnp.exp(-x))


_GELU_C = math.sqrt(2.0 / math.pi)


def _gelu_parts(x):
    inner = _GELU_C * (x + 0.044715 * (x * x * x))
    t = jnp.tanh(inner)
    return 0.5 * x * (1.0 + t), t


def _gelu_grad(x, t):
    return 0.5 * (1.0 + t) + 0.5 * x * (1.0 - t * t) * (_GELU_C * (1.0 + 3.0 * 0.044715 * x * x))


def _silu_parts(z):
    s = _sigmoid(z)
    return z * s, s


def _silu_grad(z, s):
    return s * (1.0 + z * (1.0 - s))


def _rms(x):
    r = lax.rsqrt(jnp.mean(x * x, axis=-1, keepdims=True) + EPS)
    return x * r, r


def _rms_bwd(dn, n, r):
    return r * (dn - n * jnp.mean(dn * n, axis=-1, keepdims=True))


def _chunk_scratch(tm):
    return pltpu.VMEM((N_BLOCKS, tm, BLOCK_CH), F32)


def _store_chunks(val, scr, c_ref, dtype, row0=0):
    rows = val.shape[0]
    nc, c0 = rows // CHUNK, row0 // CHUNK
    for b in range(N_BLOCKS):
        scr[b, row0:row0 + rows, :] = val[:, b * BLOCK_CH:(b + 1) * BLOCK_CH]
        for j in range(CHUNK):
            c_ref[b, c0:c0 + nc, j * BLOCK_CH:(j + 1) * BLOCK_CH] = scr[b, pl.ds(row0 + j, nc, stride=CHUNK), :].astype(dtype)


def _load_chunks(c_ref, scr):
    nc = scr.shape[1] // CHUNK
    for b in range(N_BLOCKS):
        for j in range(CHUNK):
            scr[b, pl.ds(j, nc, stride=CHUNK), :] = c_ref[b, :, j * BLOCK_CH:(j + 1) * BLOCK_CH].astype(F32)
    return jnp.concatenate([scr[b] for b in range(N_BLOCKS)], axis=1)


def _chunk_spec(tm):
    return pl.BlockSpec((N_BLOCKS, tm // CHUNK, CHUNK_W), lambda i: (0, i, 0))


def _heads_t_spec(tm):
    return pl.BlockSpec((D_NA, tm), lambda i: (0, i))


def _in_proj(x, g_pre, w_in_g, shards, tm=512):
    L = x.shape[0]
    wn = w_in_g.shape[2]
    n_sh = len(shards)
    steps = L // tm

    def body(*refs):
        x_ref, g_ref, w_ref = refs[:3]
        uc_ref, zs_ref, qt_ref, q_ref, kt_ref, k_ref, vt_ref, v_ref, zn_ref = refs[3 + n_sh:12 + n_sh]
        u_scr = refs[12 + 2 * n_sh]
        gather = _ChipGather(refs[3:3 + n_sh], refs[12 + n_sh:12 + 2 * n_sh], refs[13 + 2 * n_sh:])
        step = pl.program_id(0)
        pl.when(step == 0)(gather.start)
        pl.when(step == steps // 2)(gather.forward)
        pl.when(step == steps - 1)(gather.finish)
        halves = [slice(0, tm // 2), slice(tm // 2, tm)]
        hn = [(_rms(x_ref[rows, :])[0] * g_ref[...]).astype(BF16) for rows in halves]
        projs = [jnp.concatenate([_dot(h, w_ref[j]) for j in range(N_CHIPS)], axis=1) for h in hn]
        for rows, proj in zip(halves, projs):
            _store_chunks(proj[:, 0:512], u_scr, uc_ref, BF16, row0=rows.start)
            zs_ref[rows, :] = proj[:, 512:1024]
            q = proj[:, 1024:1536] * (NA_HEAD_DIM ** -0.5)
            for val, t_ref, n_ref in ((q, qt_ref, q_ref), (proj[:, 1536:2048], kt_ref, k_ref), (proj[:, 2048:2560], vt_ref, v_ref)):
                t_ref[:, rows] = val.T.astype(BF16)
                n_ref[rows, :] = val.astype(BF16)
            zn_ref[rows, :] = proj[:, 2560:3072]

    tok = jax.ShapeDtypeStruct((L, 512), F32)
    tr = jax.ShapeDtypeStruct((D_NA, L), BF16)
    hm = jax.ShapeDtypeStruct((L, D_NA), BF16)
    tspec = pl.BlockSpec((tm, 512), lambda i: (i, 0))
    outs = pl.pallas_call(
        body, name="in_proj", grid=(steps,),
        in_specs=[pl.BlockSpec((tm, D_MODEL), lambda i: (i, 0)),
                  _resident(1, D_MODEL), _resident(N_CHIPS, D_MODEL, wn)] + _hbm_specs(n_sh),
        out_specs=[_chunk_spec(tm), tspec] + [_heads_t_spec(tm), tspec] * 3 + [tspec] + _hbm_specs(n_sh),
        out_shape=[jax.ShapeDtypeStruct((N_BLOCKS, L // CHUNK, CHUNK_W), BF16), tok, tr, hm, tr, hm, tr, hm, tok]
        + _gather_out_shapes(shards),
        scratch_shapes=[_chunk_scratch(tm)] + _gather_semaphores(n_sh),
        compiler_params=_cparams(("arbitrary",), has_side_effects=True),
    )(x, g_pre, w_in_g, *shards)
    return outs[:9], outs[9:]


def _ssm_block_params(a_re, a_im, log_dt, b_re, b_im, c_re, c_im, d):
    def lanes(t):
        return t.reshape(2, N_BLOCKS, 1, BLOCK_ST)

    rows = (2, N_BLOCKS, BLOCK_CH, SSM_STATE)
    b_rows = lambda t: t.reshape(2, N_BLOCKS, GROUPS_PER_BLOCK, SSM_STATE, SSM_GROUP).transpose(0, 1, 2, 4, 3).reshape(rows)
    return (lanes(a_re), lanes(a_im), lanes(jnp.broadcast_to(log_dt[..., None], a_re.shape)),
            b_rows(b_re), b_rows(b_im), c_re.reshape(rows), c_im.reshape(rows), d.reshape(N_BLOCKS, 1, BLOCK_CH))


def _ssm_group_mask():
    row_g = lax.broadcasted_iota(jnp.int32, (BLOCK_CH, BLOCK_ST), 0) // SSM_GROUP
    lane_g = lax.broadcasted_iota(jnp.int32, (BLOCK_CH, BLOCK_ST), 1) // SSM_STATE
    return row_g == lane_g


def _ssm_state_select():
    p = lax.broadcasted_iota(jnp.int32, (SSM_STATE, BLOCK_ST), 0)
    lane_p = lax.broadcasted_iota(jnp.int32, (SSM_STATE, BLOCK_ST), 1) % SSM_STATE
    return (p == lane_p).astype(F32)


def _ssm_expand_blocks(compact_refs, full_refs):
    mask, select = _ssm_group_mask(), _ssm_state_select()
    for c_ref, f_ref in zip(compact_refs, full_refs):
        for d in range(2):
            tiled = lax.dot_general(c_ref[d, 0], select, ((((1,), (0,))), ((), ())), precision=HIGHEST,
                                    preferred_element_type=F32)
            f_ref[d, 0] = jnp.where(mask, tiled, 0.0)


def _ssm_collapse_block(t):
    return lax.dot_general(jnp.where(_ssm_group_mask(), t, 0.0), _ssm_state_select(), ((((1,), (1,))), ((), ())),
                           precision=HIGHEST, preferred_element_type=F32)


def _ssm_discretise(ar, ai, ldt):
    dt = jnp.exp(ldt)
    mag = jnp.exp(dt * ar)
    abr = mag * jnp.cos(dt * ai)
    abi = mag * jnp.sin(dt * ai)
    num_re = abr - 1.0
    num_im = abi
    denom = ar * ar + ai * ai
    coef_re = (num_re * ar + num_im * ai) / denom
    coef_im = (num_im * ar - num_re * ai) / denom
    return abr, abi, coef_re, coef_im


_POW_ROWS = 24


def _ssm_fill_powers(ar_ref, ai_ref, ldt_ref, br_ref, bi_ref, pw_ref, bbar_ref):
    for d in range(2):
        abr, abi, cfr, cfi = _ssm_discretise(ar_ref[d, 0], ai_ref[d, 0], ldt_ref[d, 0])
        bbar_ref[d, 0] = cfr * br_ref[d, 0] - cfi * bi_ref[d, 0]
        bbar_ref[d, 1] = cfr * bi_ref[d, 0] + cfi * br_ref[d, 0]
        pr, pi = jnp.ones_like(abr), jnp.zeros_like(abi)
        for t in range(CHUNK + 1):
            pw_ref[d, 0, t:t + 1, :] = pr
            pw_ref[d, 1, t:t + 1, :] = pi
            pr, pi = pr * abr - pi * abi, pr * abi + pi * abr


def _dot_rounded(a, b, dims=((1,), (0,))):
    return _dot(a.astype(BF16), b.astype(BF16), dims)


def _ssm_stack_inputs(d, pw_ref, bbar_ref, xs_ref):
    for t in range(CHUNK):
        pr, pi = pw_ref[d, 0, t:t + 1, :], pw_ref[d, 1, t:t + 1, :]
        xs_ref[0, t * BLOCK_CH:(t + 1) * BLOCK_CH, :] = bbar_ref[d, 0] * pr - bbar_ref[d, 1] * pi
        xs_ref[1, t * BLOCK_CH:(t + 1) * BLOCK_CH, :] = bbar_ref[d, 0] * pi + bbar_ref[d, 1] * pr


def _eye(n):
    return (lax.broadcasted_iota(jnp.int32, (n, n), 0) == lax.broadcasted_iota(jnp.int32, (n, n), 1)).astype(F32)


def _ssm_param_specs():
    vec = pl.BlockSpec((2, 1, 1, BLOCK_ST), lambda b, j: (0, b, 0, 0))
    mat = pl.BlockSpec((2, 1, BLOCK_CH, SSM_STATE), lambda b, j: (0, b, 0, 0))
    return [vec, vec, vec, mat, mat, mat, mat, pl.BlockSpec((1, 1, BLOCK_CH), lambda b, j: (b, 0, 0))]


def _ssm_block_scratch():
    return [pltpu.VMEM((2, 1, BLOCK_CH, BLOCK_ST), F32)] * 4


def _ssm_chunk_matrices(blk, shards):
    n = len(shards)

    def body(*refs):
        ar_ref, ai_ref, ldt_ref = refs[:3]
        d_ref = refs[7]
        m_ref, ws_ref, wot_ref, a16_ref = refs[8 + n:12 + n]
        pw_ref, bbar_ref, lag_ref, xs_ref = refs[12 + 2 * n:16 + 2 * n]
        br_ref, bi_ref, cr_ref, ci_ref = refs[16 + 2 * n:20 + 2 * n]
        gather = _ChipGather(refs[8:8 + n], refs[12 + n:12 + 2 * n], refs[20 + 2 * n:])
        b, j = pl.program_id(0), pl.program_id(1)
        pl.when((b == 0) & (j == 0))(gather.start)
        pl.when((b == N_BLOCKS - 1) & (j == 0))(gather.forward)
        pl.when((b == N_BLOCKS - 1) & (j == CHUNK - 1))(gather.finish)

        @pl.when(j == 0)
        def _():
            _ssm_expand_blocks(refs[3:7], (br_ref, bi_ref, cr_ref, ci_ref))
            _ssm_fill_powers(ar_ref, ai_ref, ldt_ref, br_ref, bi_ref, pw_ref, bbar_ref)
            zero_lag = d_ref[0] * _eye(BLOCK_CH)
            for d in range(2):
                _ssm_stack_inputs(d, pw_ref, bbar_ref, xs_ref)
                taps = (_dot_rounded(xs_ref[0], cr_ref[d, 0], ((1,), (1,)))
                        - _dot_rounded(xs_ref[1], ci_ref[d, 0], ((1,), (1,))))
                zero_lag = zero_lag + taps[0:BLOCK_CH]
                for t in range(1, CHUNK):
                    lag_ref[CHUNK - 1 + t if d == 0 else CHUNK - 1 - t] = taps[t * BLOCK_CH:(t + 1) * BLOCK_CH]
            lag_ref[CHUNK - 1] = zero_lag
            a16_ref[0] = jnp.concatenate([pw_ref[d, ri, CHUNK:CHUNK + 1, :] for d in range(2) for ri in range(2)], axis=1)

        m_ref[0] = jnp.concatenate([lag_ref[jp - j + CHUNK - 1] for jp in range(CHUNK)], axis=1).astype(BF16)

        def power(d, t):
            return pw_ref[d, 0, pl.ds(t, 1), :], pw_ref[d, 1, pl.ds(t, 1), :]

        parts = []
        for d, t in ((0, CHUNK - 1 - j), (1, j)):
            pr, pi = power(d, t)
            parts += [bbar_ref[d, 0] * pr - bbar_ref[d, 1] * pi, bbar_ref[d, 0] * pi + bbar_ref[d, 1] * pr]
        ws_ref[0] = jnp.concatenate(parts, axis=1).astype(BF16)
        parts = []
        for d, t in ((0, j + 1), (1, CHUNK - j)):
            pr, pi = power(d, t)
            parts += [cr_ref[d, 0] * pr - ci_ref[d, 0] * pi, -cr_ref[d, 0] * pi - ci_ref[d, 0] * pr]
        wot_ref[0] = jnp.concatenate(parts, axis=1).astype(BF16)

    row = pl.BlockSpec((1, BLOCK_CH, CHUNK_W), lambda b, j: (b, j, 0))
    mat = jax.ShapeDtypeStruct((N_BLOCKS, CHUNK_W, CHUNK_W), BF16)
    outs = pl.pallas_call(
        body, name="ssm_chunk_matrices", grid=(N_BLOCKS, CHUNK),
        in_specs=_ssm_param_specs() + _hbm_specs(n),
        out_specs=[row, row, row, pl.BlockSpec((1, 1, STATE_W), lambda b, j: (b, 0, 0))] + _hbm_specs(n),
        out_shape=[mat, mat, mat, jax.ShapeDtypeStruct((N_BLOCKS, 1, STATE_W), F32)] + _gather_out_shapes(shards),
        scratch_shapes=[pltpu.VMEM((2, 2, _POW_ROWS, BLOCK_ST), F32), pltpu.VMEM((2, 2, BLOCK_CH, BLOCK_ST), F32),
                        pltpu.VMEM((2 * CHUNK, BLOCK_CH, BLOCK_CH), F32), pltpu.VMEM((2, CHUNK_W, BLOCK_ST), F32)]
        + _ssm_block_scratch() + _gather_semaphores(n),
        compiler_params=_cparams(("arbitrary", "arbitrary"), has_side_effects=True),
    )(*blk, *shards)
    return outs[:4], outs[4:]


def _ssm_chunk_matrices_bwd(blk, d_m, d_ws, d_wot, d_a16):
    def body(ar_ref, ai_ref, ldt_ref, brc_ref, bic_ref, crc_ref, cic_ref, d_ref, dm_ref, dws_ref, dwot_ref, da16_ref,
             dar_ref, dai_ref, dldt_ref, dbr_ref, dbi_ref, dcr_ref, dci_ref, dd_ref,
             pw_ref, bbar_ref, dlag_ref, dbbar_ref, dc_ref, dpw_ref, xs_ref, dts_ref, br_ref, bi_ref, cr_ref, ci_ref):
        j = pl.program_id(1)
        w = BLOCK_ST

        @pl.when(j == 0)
        def _():
            _ssm_expand_blocks((brc_ref, bic_ref, crc_ref, cic_ref), (br_ref, bi_ref, cr_ref, ci_ref))
            _ssm_fill_powers(ar_ref, ai_ref, ldt_ref, br_ref, bi_ref, pw_ref, bbar_ref)
            for r in (dlag_ref, dbbar_ref, dc_ref, dpw_ref):
                r[...] = jnp.zeros_like(r)

        def fold(t):
            return jnp.sum(t.reshape(BLOCK_CH // 8, 8, w), axis=0)

        def d_power(d, ri, t):
            return jnp.sum(dpw_ref[d, ri, t], axis=0, keepdims=True)

        def x_chain(d, t, dxr, dxi):
            pr, pi = pw_ref[d, 0, pl.ds(t, 1), :], pw_ref[d, 1, pl.ds(t, 1), :]
            bbr, bbi = bbar_ref[d, 0], bbar_ref[d, 1]
            dbbar_ref[d, 0] += dxr * pr + dxi * pi
            dbbar_ref[d, 1] += dxi * pr - dxr * pi
            dpw_ref[d, 0, t] += fold(dxr * bbr + dxi * bbi)
            dpw_ref[d, 1, t] += fold(dxi * bbr - dxr * bbi)

        def z_chain(d, t, dzr, dzi):
            pr, pi = pw_ref[d, 0, pl.ds(t, 1), :], pw_ref[d, 1, pl.ds(t, 1), :]
            c_r, c_i = cr_ref[d, 0], ci_ref[d, 0]
            dc_ref[d, 0] += dzr * pr - dzi * pi
            dc_ref[d, 1] += -dzr * pi - dzi * pr
            dpw_ref[d, 0, t] += fold(dzr * c_r - dzi * c_i)
            dpw_ref[d, 1, t] += fold(-dzr * c_i - dzi * c_r)

        for jp in range(CHUNK):
            dlag_ref[jp - j + CHUNK - 1] += dm_ref[0, :, jp * BLOCK_CH:(jp + 1) * BLOCK_CH].astype(F32)
        quarter = lambda ref, i: ref[0, :, i * w:(i + 1) * w].astype(F32)
        x_chain(0, CHUNK - 1 - j, quarter(dws_ref, 0), quarter(dws_ref, 1))
        x_chain(1, j, quarter(dws_ref, 2), quarter(dws_ref, 3))
        z_chain(0, j + 1, quarter(dwot_ref, 0), quarter(dwot_ref, 1))
        z_chain(1, CHUNK - j, quarter(dwot_ref, 2), quarter(dwot_ref, 3))

        @pl.when(j == CHUNK - 1)
        def _():
            for d in range(2):
                _ssm_stack_inputs(d, pw_ref, bbar_ref, xs_ref)
                for t in range(CHUNK):
                    dts_ref[t * BLOCK_CH:(t + 1) * BLOCK_CH, :] = dlag_ref[CHUNK - 1 + t if d == 0 else CHUNK - 1 - t]
                d_taps = dts_ref[...]
                dc_ref[d, 0] += _dot_rounded(d_taps, xs_ref[0], ((0,), (0,)))
                dc_ref[d, 1] -= _dot_rounded(d_taps, xs_ref[1], ((0,), (0,)))
                xs_ref[0] = _dot_rounded(d_taps, cr_ref[d, 0])
                xs_ref[1] = -_dot_rounded(d_taps, ci_ref[d, 0])
                for t in range(CHUNK):
                    rows = slice(t * BLOCK_CH, (t + 1) * BLOCK_CH)
                    x_chain(d, t, xs_ref[0, rows, :], xs_ref[1, rows, :])
            dd_ref[0] = jnp.sum(dlag_ref[CHUNK - 1] * _eye(BLOCK_CH), axis=0, keepdims=True)
            for d in range(2):
                (abr, abi, cfr, cfi), disc_vjp = jax.vjp(_ssm_discretise, ar_ref[d, 0], ai_ref[d, 0], ldt_ref[d, 0])
                dpr = d_power(d, 0, CHUNK) + da16_ref[0, :, 2 * d * w:(2 * d + 1) * w]
                dpi = d_power(d, 1, CHUNK) + da16_ref[0, :, (2 * d + 1) * w:(2 * d + 2) * w]
                dabr, dabi = jnp.zeros_like(abr), jnp.zeros_like(abi)
                for t in range(CHUNK, 0, -1):
                    qr, qi = pw_ref[d, 0, t - 1:t, :], pw_ref[d, 1, t - 1:t, :]
                    dabr = dabr + dpr * qr + dpi * qi
                    dabi = dabi + dpi * qr - dpr * qi
                    dpr, dpi = (dpr * abr + dpi * abi + d_power(d, 0, t - 1),
                                dpi * abr - dpr * abi + d_power(d, 1, t - 1))
                dbbr, dbbi = dbbar_ref[d, 0], dbbar_ref[d, 1]
                b_r, b_i = br_ref[d, 0], bi_ref[d, 0]
                dbr_ref[d, 0] = _ssm_collapse_block(cfr * dbbr + cfi * dbbi)
                dbi_ref[d, 0] = _ssm_collapse_block(cfr * dbbi - cfi * dbbr)
                dcfr = jnp.sum(b_r * dbbr + b_i * dbbi, axis=0, keepdims=True)
                dcfi = jnp.sum(b_r * dbbi - b_i * dbbr, axis=0, keepdims=True)
                dar_ref[d, 0], dai_ref[d, 0], dldt_ref[d, 0] = disc_vjp((dabr, dabi, dcfr, dcfi))
                dcr_ref[d, 0] = _ssm_collapse_block(dc_ref[d, 0])
                dci_ref[d, 0] = _ssm_collapse_block(dc_ref[d, 1])

    row = pl.BlockSpec((1, BLOCK_CH, CHUNK_W), lambda b, j: (b, j, 0))
    specs = _ssm_param_specs()
    acc = lambda *s: pltpu.VMEM(s, F32)
    return pl.pallas_call(
        body, name="ssm_chunk_matrices_bwd", grid=(N_BLOCKS, CHUNK),
        in_specs=specs + [row, row, row, pl.BlockSpec((1, 1, STATE_W), lambda b, j: (b, 0, 0))],
        out_specs=specs,
        out_shape=[jax.ShapeDtypeStruct(t.shape, F32) for t in blk],
        scratch_shapes=[acc(2, 2, _POW_ROWS, BLOCK_ST), acc(2, 2, BLOCK_CH, BLOCK_ST), acc(2 * CHUNK, BLOCK_CH, BLOCK_CH),
                        acc(2, 2, BLOCK_CH, BLOCK_ST), acc(2, 2, BLOCK_CH, BLOCK_ST), acc(2, 2, CHUNK + 1, 8, BLOCK_ST),
                        acc(2, CHUNK_W, BLOCK_ST), acc(CHUNK_W, BLOCK_CH)] + _ssm_block_scratch(),
        compiler_params=_cparams(("arbitrary", "arbitrary")),
    )(*blk, d_m, d_ws, d_wot, d_a16)


def _block_matmul(terms, name, out_dtype=F32, tn=1024):
    nc = terms[0][0].shape[1]
    n_out = terms[0][1].shape[1] if terms[0][2] else terms[0][1].shape[2]
    flags = [t[2] for t in terms]

    def body(*refs):
        out_ref = refs[-1]
        acc = None
        for t, transposed in enumerate(flags):
            a = refs[2 * t][0].astype(BF16)
            w = refs[2 * t + 1][0]
            part = _dot_nt(a, w) if transposed else _dot(a, w)
            acc = part if acc is None else acc + part
        out_ref[0] = acc.astype(out_dtype)

    in_specs, args = [], []
    for a, w, transposed in terms:
        k = a.shape[2]
        in_specs.append(pl.BlockSpec((1, nc, k), lambda b, n: (b, 0, 0)))
        if transposed:
            in_specs.append(pl.BlockSpec((1, tn, k), lambda b, n: (b, n, 0)))
        else:
            in_specs.append(pl.BlockSpec((1, k, tn), lambda b, n: (b, 0, n)))
        args += [a, w]
    return pl.pallas_call(
        body, name=name, grid=(N_BLOCKS, n_out // tn), in_specs=in_specs,
        out_specs=pl.BlockSpec((1, nc, tn), lambda b, n: (b, 0, n)),
        out_shape=jax.ShapeDtypeStruct((N_BLOCKS, nc, n_out), out_dtype),
        compiler_params=_cparams(("arbitrary", "arbitrary")),
    )(*args)


def _block_matmul_tn(a, b, name, tile=1024):
    nc, m = a.shape[1], a.shape[2]
    n = b.shape[2]

    def body(a_ref, b_ref, out_ref):
        a_t = a_ref[0].astype(BF16)
        for j in range(n // tile):
            cols = slice(j * tile, (j + 1) * tile)
            out_ref[0, :, cols] = _dot_tn(a_t, b_ref[0, :, cols].astype(BF16)).astype(BF16)

    return pl.pallas_call(
        body, name=name, grid=(N_BLOCKS, m // tile),
        in_specs=[pl.BlockSpec((1, nc, tile), lambda blk, i: (blk, 0, i)),
                  pl.BlockSpec((1, nc, n), lambda blk, i: (blk, 0, 0))],
        out_specs=pl.BlockSpec((1, tile, n), lambda blk, i: (blk, i, 0)),
        out_shape=jax.ShapeDtypeStruct((N_BLOCKS, m, n), BF16),
        compiler_params=_cparams(("arbitrary", "arbitrary")),
    )(a, b)


def _cmul(ar, ai, xr, xi):
    return ar * xr - ai * xi, ar * xi + ai * xr


def _cmul_conj(ar, ai, xr, xi):
    return ar * xr + ai * xi, ar * xi - ai * xr


_SCAN_UNROLL = 8


def _ssm_state_scan(s_in, a16):
    nc = s_in.shape[1]
    w = BLOCK_ST

    def body(sin_ref, a_ref, out_ref):
        a = a_ref[0]
        afr, afi, abr, abi = a[:, 0:w], a[:, w:2 * w], a[:, 2 * w:3 * w], a[:, 3 * w:4 * w]

        def step(c, carry):
            fr, fi, br, bi = carry
            cb = nc - 1 - c
            out_ref[0, pl.ds(c, 1), 0:w] = fr
            out_ref[0, pl.ds(c, 1), w:2 * w] = fi
            out_ref[0, pl.ds(cb, 1), 2 * w:3 * w] = br
            out_ref[0, pl.ds(cb, 1), 3 * w:4 * w] = bi
            nfr, nfi = _cmul(afr, afi, fr, fi)
            nbr, nbi = _cmul(abr, abi, br, bi)
            return (nfr + sin_ref[0, pl.ds(c, 1), 0:w], nfi + sin_ref[0, pl.ds(c, 1), w:2 * w],
                    nbr + sin_ref[0, pl.ds(cb, 1), 2 * w:3 * w], nbi + sin_ref[0, pl.ds(cb, 1), 3 * w:4 * w])

        def steps(i, carry):
            for k in range(_SCAN_UNROLL):
                carry = step(i * _SCAN_UNROLL + k, carry)
            return carry

        z = jnp.zeros((1, w), F32)
        lax.fori_loop(0, nc // _SCAN_UNROLL, steps, (z, z, z, z))

    spec = pl.BlockSpec((1, nc, STATE_W), lambda b: (b, 0, 0))
    return pl.pallas_call(
        body, name="ssm_state_scan", grid=(N_BLOCKS,),
        in_specs=[spec, pl.BlockSpec((1, 1, STATE_W), lambda b: (b, 0, 0))],
        out_specs=spec, out_shape=jax.ShapeDtypeStruct(s_in.shape, F32),
        compiler_params=_cparams(("arbitrary",)),
    )(s_in, a16)


def _ssm_state_scan_bwd(d_prev, s_prev, a16):
    nc = d_prev.shape[1]
    w = BLOCK_ST

    def body(dp_ref, sp_ref, a_ref, g_ref, da_ref):
        a = a_ref[0]
        afr, afi, abr, abi = a[:, 0:w], a[:, w:2 * w], a[:, 2 * w:3 * w], a[:, 3 * w:4 * w]

        def step(i, carry):
            gfr, gfi, gbr, gbi, dafr, dafi, dabr, dabi = carry
            cf = nc - 1 - i
            cb = i
            g_ref[0, pl.ds(cf, 1), 0:w] = gfr
            g_ref[0, pl.ds(cf, 1), w:2 * w] = gfi
            g_ref[0, pl.ds(cb, 1), 2 * w:3 * w] = gbr
            g_ref[0, pl.ds(cb, 1), 3 * w:4 * w] = gbi
            sfr, sfi = sp_ref[0, pl.ds(cf, 1), 0:w], sp_ref[0, pl.ds(cf, 1), w:2 * w]
            sbr, sbi = sp_ref[0, pl.ds(cb, 1), 2 * w:3 * w], sp_ref[0, pl.ds(cb, 1), 3 * w:4 * w]
            dafr = dafr + gfr * sfr + gfi * sfi
            dafi = dafi + gfi * sfr - gfr * sfi
            dabr = dabr + gbr * sbr + gbi * sbi
            dabi = dabi + gbi * sbr - gbr * sbi
            nfr, nfi = _cmul_conj(afr, afi, gfr, gfi)
            nbr, nbi = _cmul_conj(abr, abi, gbr, gbi)
            return (nfr + dp_ref[0, pl.ds(cf, 1), 0:w], nfi + dp_ref[0, pl.ds(cf, 1), w:2 * w],
                    nbr + dp_ref[0, pl.ds(cb, 1), 2 * w:3 * w], nbi + dp_ref[0, pl.ds(cb, 1), 3 * w:4 * w],
                    dafr, dafi, dabr, dabi)

        def steps(i, carry):
            for k in range(_SCAN_UNROLL):
                carry = step(i * _SCAN_UNROLL + k, carry)
            return carry

        z = jnp.zeros((1, w), F32)
        res = lax.fori_loop(0, nc // _SCAN_UNROLL, steps, (z,) * 8)
        da_ref[0] = jnp.concatenate(res[4:], axis=1)

    spec = pl.BlockSpec((1, nc, STATE_W), lambda b: (b, 0, 0))
    aspec = pl.BlockSpec((1, 1, STATE_W), lambda b: (b, 0, 0))
    return pl.pallas_call(
        body, name="ssm_state_scan_bwd", grid=(N_BLOCKS,),
        in_specs=[spec, spec, aspec], out_specs=[spec, aspec],
        out_shape=[jax.ShapeDtypeStruct(d_prev.shape, F32), jax.ShapeDtypeStruct((N_BLOCKS, 1, STATE_W), F32)],
        compiler_params=_cparams(("arbitrary",)),
    )(d_prev, s_prev, a16)


NA_PAIR = 2 * GRID_W
NA_WIN_ROWS = NA_ROWS + 2
NA_WIN = NA_WIN_ROWS * GRID_W
NA_PAIRS_PER_STEP = 8
NA_CASES = 5
NA_MASKED = -1e30


def _na_pair_window(m, rows):
    rs0 = jnp.clip(2 * m - NA_ROWS // 2, 0, rows - NA_ROWS)
    ws = jnp.minimum(rs0, rows - NA_WIN_ROWS)
    last = rows // 2 - 1
    case = jnp.where(m == 0, 0, jnp.where(m == 1, 1, jnp.where(m == last - 1, 3, jnp.where(m == last, 4, 2))))
    return ws, case


def _na_row_offsets(rows):
    last = rows // 2 - 1
    geom = []
    for m in (0, 1, 2, last - 1, last):
        ws = min(max(2 * m - NA_ROWS // 2, 0), rows - NA_ROWS, rows - NA_WIN_ROWS)
        per_case = []
        for i in range(NA_WIN_ROWS):
            pair = []
            for rr in range(2):
                r = 2 * m + rr
                rs = min(max(r - NA_ROWS // 2, 0), rows - NA_ROWS)
                pair.append(ws + i - r + NA_ROWS - 1 if rs <= ws + i < rs + NA_ROWS else None)
            per_case.append(pair)
        geom.append(per_case)
    return geom


def _na_col_select():
    qc = np.arange(NA_PAIR)[None, :] % GRID_W
    kc = np.arange(GRID_W)[:, None]
    dc = np.clip(kc - qc + NA_COLS - 1, 0, 2 * NA_COLS - 2)
    return jnp.asarray((np.arange(2 * NA_COLS - 1)[:, None, None] == dc[None]).astype(np.float32))


def _na_bias_rows(rpb):
    return jnp.einsum("hrd,dkl->hrkl", rpb, _na_col_select(), precision=HIGHEST)


def _na_col_window():
    qc = lax.broadcasted_iota(jnp.int32, (GRID_W, NA_PAIR), 1) % GRID_W
    kc = lax.broadcasted_iota(jnp.int32, (GRID_W, NA_PAIR), 0)
    cs = jnp.clip(qc - NA_COLS // 2, 0, GRID_W - NA_COLS)
    first_row = lax.broadcasted_iota(jnp.int32, (GRID_W, NA_PAIR), 1) < GRID_W
    return (kc >= cs) & (kc < cs + NA_COLS), first_row


def _na_bias_table(bias_rows, rows):
    geom = _na_row_offsets(rows)

    def body(br_ref, tab_ref):
        col_ok, first_row = _na_col_window()
        masked = jnp.full((GRID_W, NA_PAIR), NA_MASKED, F32)
        for case in range(NA_CASES):
            for i in range(NA_WIN_ROWS):
                d0, d1 = geom[case][i]
                t0 = masked if d0 is None else br_ref[0, d0]
                t1 = masked if d1 is None else br_ref[0, d1]
                tile = jnp.where(col_ok, jnp.where(first_row, t0, t1), NA_MASKED)
                tab_ref[0, case, i * GRID_W:(i + 1) * GRID_W, :] = tile

    return pl.pallas_call(
        body, name="na_bias_table", grid=(NA_HEADS,),
        in_specs=[pl.BlockSpec((1, 2 * NA_ROWS - 1, GRID_W, NA_PAIR), lambda h: (h, 0, 0, 0))],
        out_specs=pl.BlockSpec((1, NA_CASES, NA_WIN, NA_PAIR), lambda h: (h, 0, 0, 0)),
        out_shape=jax.ShapeDtypeStruct((NA_HEADS, NA_CASES, NA_WIN, NA_PAIR), F32),
        compiler_params=_cparams(("arbitrary",)),
    )(bias_rows)


def _na_bias_table_bwd(d_tab, rows):
    geom = _na_row_offsets(rows)

    def body(dt_ref, dbr_ref):
        col_ok, first_row = _na_col_window()
        acc = [None] * (2 * NA_ROWS - 1)
        for case in range(NA_CASES):
            for i in range(NA_WIN_ROWS):
                tile = jnp.where(col_ok, dt_ref[0, case, i * GRID_W:(i + 1) * GRID_W, :], 0.0)
                for rr, d in enumerate(geom[case][i]):
                    if d is not None:
                        part = jnp.where(first_row if rr == 0 else ~first_row, tile, 0.0)
                        acc[d] = part if acc[d] is None else acc[d] + part
        for d, a in enumerate(acc):
            dbr_ref[0, d] = jnp.zeros((GRID_W, NA_PAIR), F32) if a is None else a

    return pl.pallas_call(
        body, name="na_bias_table_bwd", grid=(NA_HEADS,),
        in_specs=[pl.BlockSpec((1, NA_CASES, NA_WIN, NA_PAIR), lambda h: (h, 0, 0, 0))],
        out_specs=pl.BlockSpec((1, 2 * NA_ROWS - 1, GRID_W, NA_PAIR), lambda h: (h, 0, 0, 0)),
        out_shape=jax.ShapeDtypeStruct((NA_HEADS, 2 * NA_ROWS - 1, GRID_W, NA_PAIR), F32),
        compiler_params=_cparams(("arbitrary",)),
    )(d_tab)


NA_BLK = 64


def _na_blocks():
    return [slice(i * NA_BLK, (i + 1) * NA_BLK) for i in range(NA_WIN // NA_BLK)]


def _na_softmax(qk, bias_ref, hh, case):
    m = jnp.full((NA_BLK, NA_PAIR), -jnp.inf, F32)
    scores = []
    for blk in _na_blocks():
        s = qk[blk, :] + bias_ref[hh, case, blk, :]
        scores.append(s)
        m = jnp.maximum(m, s)
    m = jnp.max(m, axis=0, keepdims=True)
    l = jnp.zeros((NA_BLK, NA_PAIR), F32)
    exps = []
    for s in scores:
        e = jnp.exp(s - m)
        exps.append(e)
        l = l + e
    return exps, jnp.sum(l, axis=0, keepdims=True)


def _na_units(step, rows):
    units = []
    for pp in range(NA_PAIRS_PER_STEP):
        ws, case = _na_pair_window(step * NA_PAIRS_PER_STEP + pp, rows)
        win = pl.ds(pl.multiple_of(ws * GRID_W, NA_PAIR), NA_WIN)
        lanes = slice(pp * NA_PAIR, (pp + 1) * NA_PAIR)
        for hh in range(2):
            units.append((pp, hh, case, win, lanes, slice(hh * NA_HEAD_DIM, (hh + 1) * NA_HEAD_DIM)))
    return units


def _na_pipeline(n, before, middle, after, lookahead):
    for u in range(min(lookahead, n)):
        for f in before:
            f(u)
    for u in range(n):
        middle(u)
        if u + lookahead < n:
            for f in before:
                f(u + lookahead)
        for f in after:
            f(u)


def _head_rows(t, hh):
    row_head = lax.broadcasted_iota(jnp.int32, t.shape, 0) // NA_HEAD_DIM
    return jnp.where(row_head == hh, t, jnp.zeros_like(t))


def _heads_block_diag(t):
    lane_head = lax.broadcasted_iota(jnp.int32, t.shape, 1) // NA_HEAD_DIM
    zero = jnp.zeros_like(t)
    return jnp.concatenate([jnp.where(lane_head == 0, t, zero), jnp.where(lane_head == 1, t, zero)], axis=0)


def _na_fwd(q_t, k, v_t, bias_tab):
    L = k.shape[0]
    rows = L // GRID_W
    step_w = NA_PAIRS_PER_STEP * NA_PAIR

    def body(q_ref, k_ref, v_ref, bt_ref, o_ref):
        units = _na_units(pl.program_id(1), rows)
        qk, probs = {}, {}

        def scores(u):
            _, hh, _, win, lanes, _ = units[u]
            qk[u] = _dot(k_ref[win, :], _head_rows(q_ref[:, lanes], hh))

        def softmax(u):
            _, hh, case, _, _, _ = units[u]
            exps, l = _na_softmax(qk.pop(u), bt_ref, hh, case)
            probs[u] = jnp.concatenate([t.astype(BF16) for t in exps], axis=0), l

        def output(u):
            _, _, _, win, lanes, hrows = units[u]
            e, l = probs.pop(u)
            o_ref[hrows, lanes] = _dot(v_ref[hrows, win], e) / l

        _na_pipeline(len(units), [scores], softmax, [output], lookahead=3)

    q_spec = pl.BlockSpec((NA_PAIR, step_w), lambda h, s: (h, s))
    return pl.pallas_call(
        body, name="na_fwd", grid=(NA_HEADS // 2, L // step_w),
        in_specs=[q_spec, pl.BlockSpec((L, NA_PAIR), lambda h, s: (0, h)),
                  pl.BlockSpec((NA_PAIR, L), lambda h, s: (h, 0)),
                  pl.BlockSpec((2, NA_CASES, NA_WIN, NA_PAIR), lambda h, s: (h, 0, 0, 0))],
        out_specs=q_spec,
        out_shape=jax.ShapeDtypeStruct((D_NA, L), F32),
        compiler_params=_cparams(("arbitrary", "arbitrary")),
    )(q_t, k, v_t, bias_tab)


def _na_bwd(q_t, q, k_t, k, v, bias_tab, out_t, d_out_t, d_out):
    L = k.shape[0]
    rows = L // GRID_W
    step_w = NA_PAIRS_PER_STEP * NA_PAIR

    def body(qt_ref, q_ref, kt_ref, k_ref, v_ref, bt_ref, ot_ref, dot_ref, do_ref, dq_ref, dk_ref, dv_ref, dbt_ref):
        @pl.when(pl.program_id(1) == 0)
        def _():
            dk_ref[...] = jnp.zeros_like(dk_ref)
            dv_ref[...] = jnp.zeros_like(dv_ref)
            dbt_ref[...] = jnp.zeros_like(dbt_ref)

        units = _na_units(pl.program_id(1), rows)
        qk, dp, dsb, pb = {}, {}, {}, {}

        def scores(u):
            _, hh, _, win, lanes, _ = units[u]
            qk[u] = _dot(k_ref[win, :], _head_rows(qt_ref[:, lanes], hh))

        def d_probs(u):
            _, hh, _, win, lanes, _ = units[u]
            dp[u] = _dot(v_ref[win, :], _head_rows(dot_ref[:, lanes].astype(BF16), hh))

        def softmax_bwd(u):
            _, hh, case, _, lanes, hrows = units[u]
            exps, l = _na_softmax(qk.pop(u), bt_ref, hh, case)
            inv_l = 1.0 / l
            delta = jnp.sum(dot_ref[hrows, lanes] * ot_ref[hrows, lanes], axis=0, keepdims=True)
            d_p = dp.pop(u)
            ds_blocks, p_blocks = [], []
            for blk, e in zip(_na_blocks(), exps):
                p = e * inv_l
                ds = p * (d_p[blk, :] - delta)
                dbt_ref[hh, case, blk, :] += ds
                ds_blocks.append(ds.astype(BF16))
                p_blocks.append(p.astype(BF16))
            dsb[u] = jnp.concatenate(ds_blocks, axis=0)
            pb[u] = jnp.concatenate(p_blocks, axis=0)

        def d_query(u):
            _, _, _, win, lanes, hrows = units[u]
            dq_ref[hrows, lanes] = _dot(kt_ref[hrows, win], dsb[u]) * (NA_HEAD_DIM ** -0.5)

        def d_keys_values(u):
            pp, hh, _, win, _, _ = units[u]
            if hh == 1:
                tokens = slice(pp * NA_PAIR, (pp + 1) * NA_PAIR)
                dk_ref[win, :] += _dot(jnp.concatenate([dsb.pop(u - 1), dsb.pop(u)], axis=1), _heads_block_diag(q_ref[tokens, :]))
                dv_ref[win, :] += _dot(jnp.concatenate([pb.pop(u - 1), pb.pop(u)], axis=1), _heads_block_diag(do_ref[tokens, :]))

        _na_pipeline(len(units), [scores, d_probs], softmax_bwd, [d_query, d_keys_values], lookahead=2)

    t_tile = pl.BlockSpec((NA_PAIR, step_w), lambda h, s: (h, s))
    tile = pl.BlockSpec((step_w, NA_PAIR), lambda h, s: (s, h))
    t_full = pl.BlockSpec((NA_PAIR, L), lambda h, s: (h, 0))
    full = pl.BlockSpec((L, NA_PAIR), lambda h, s: (0, h))
    bt = pl.BlockSpec((2, NA_CASES, NA_WIN, NA_PAIR), lambda h, s: (h, 0, 0, 0))
    tok = jax.ShapeDtypeStruct((L, D_NA), F32)
    return pl.pallas_call(
        body, name="na_bwd", grid=(NA_HEADS // 2, L // step_w),
        in_specs=[t_tile, tile, t_full, full, full, bt, t_tile, t_tile, tile],
        out_specs=[t_tile, full, full, bt],
        out_shape=[jax.ShapeDtypeStruct((D_NA, L), F32), tok, tok, jax.ShapeDtypeStruct(bias_tab.shape, F32)],
        compiler_params=_cparams(("arbitrary", "arbitrary")),
    )(q_t, q, k_t, k, v, bias_tab, out_t, d_out_t, d_out)


def _branch_fwd_values(ys, zs, yn, zn, wglu, bglu):
    g1, t = _gelu_parts(ys)
    lin = _dot(g1.astype(BF16), wglu) + bglu
    sg = _sigmoid(lin)
    ys2 = g1 * sg
    sz, szs = _silu_parts(zs)
    sn, sns = _silu_parts(zn)
    return g1, t, sg, ys2, sz, szs, sn, sns


def _branch_fwd(y_ssm_c, z_s, y_na_t, z_n, w_glu, b_glu, tm=512):
    L = z_s.shape[0]

    def body(ys_ref, zs_ref, yn_ref, zn_ref, w_ref, b_ref, cat_ref, scr):
        yn = yn_ref[...].T
        g1, t, sg, ys2, sz, szs, sn, sns = _branch_fwd_values(
            _load_chunks(ys_ref, scr), zs_ref[...], yn, zn_ref[...], w_ref[...], b_ref[...])
        cat_ref[:, 0:512] = (ys2 * sz).astype(BF16)
        cat_ref[:, 512:1024] = (yn * sn).astype(BF16)

    tile = pl.BlockSpec((tm, 512), lambda i: (i, 0))
    return pl.pallas_call(
        body, name="branch_fwd", grid=(L // tm,),
        in_specs=[_chunk_spec(tm), tile, _heads_t_spec(tm), tile, pl.BlockSpec((512, 512), lambda i: (0, 0)),
                  pl.BlockSpec((1, 512), lambda i: (0, 0))],
        out_specs=pl.BlockSpec((tm, 1024), lambda i: (i, 0)),
        out_shape=jax.ShapeDtypeStruct((L, 1024), BF16),
        scratch_shapes=[_chunk_scratch(tm)],
        compiler_params=_cparams(("arbitrary",)),
    )(y_ssm_c, z_s, y_na_t, z_n, w_glu, b_glu)


def _branch_bwd(y_ssm_c, z_s, y_na_t, z_n, w_glu, b_glu, d_cat, tm=512):
    L = z_s.shape[0]

    def body(ys_ref, zs_ref, yn_ref, zn_ref, w_ref, b_ref, dc_ref,
             dys_ref, dzs_ref, dynt_ref, dyn_ref, dzn_ref, dw_ref, db_ref, scr):
        @pl.when(pl.program_id(0) == 0)
        def _():
            dw_ref[...] = jnp.zeros_like(dw_ref)
            db_ref[...] = jnp.zeros_like(db_ref)

        ys, zs, yn, zn = _load_chunks(ys_ref, scr), zs_ref[...], yn_ref[...].T, zn_ref[...]
        w = w_ref[...]
        g1, t, sg, ys2, sz, szs, sn, sns = _branch_fwd_values(ys, zs, yn, zn, w, b_ref[...])
        dys3 = dc_ref[:, 0:512]
        dyn2 = dc_ref[:, 512:1024]
        dzs_ref[...] = (dys3 * ys2 * _silu_grad(zs, szs)).astype(BF16)
        dys2 = dys3 * sz
        dlin = dys2 * g1 * sg * (1.0 - sg)
        dlb = dlin.astype(BF16)
        dg1 = dys2 * sg + _dot_nt(dlb, w)
        dw_ref[...] += _dot_tn(g1.astype(BF16), dlb)
        db_ref[...] += jnp.sum(dlin, axis=0, keepdims=True)
        _store_chunks(dg1 * _gelu_grad(ys, t), scr, dys_ref, BF16)
        dyn = dyn2 * sn
        dynt_ref[...] = dyn.T
        dyn_ref[...] = dyn.astype(BF16)
        dzn_ref[...] = (dyn2 * yn * _silu_grad(zn, sns)).astype(BF16)

    tile = pl.BlockSpec((tm, 512), lambda i: (i, 0))
    wspec = pl.BlockSpec((512, 512), lambda i: (0, 0))
    bspec = pl.BlockSpec((1, 512), lambda i: (0, 0))
    tok = jax.ShapeDtypeStruct((L, 512), BF16)
    return pl.pallas_call(
        body, name="branch_bwd", grid=(L // tm,),
        in_specs=[_chunk_spec(tm), tile, _heads_t_spec(tm), tile, wspec, bspec, pl.BlockSpec((tm, 1024), lambda i: (i, 0))],
        out_specs=[_chunk_spec(tm), tile, _heads_t_spec(tm), tile, tile, wspec, bspec],
        out_shape=[jax.ShapeDtypeStruct((N_BLOCKS, L // CHUNK, CHUNK_W), BF16), tok, jax.ShapeDtypeStruct((D_NA, L), F32),
                   tok, tok,
                   jax.ShapeDtypeStruct((512, 512), F32), jax.ShapeDtypeStruct((1, 512), F32)],
        scratch_shapes=[_chunk_scratch(tm)],
        compiler_params=_cparams(("arbitrary",)),
    )(y_ssm_c, z_s, y_na_t, z_n, w_glu, b_glu, d_cat)


def _head(x, p, target, cat, w_out, g_post, w_ple_g, g_ple, w_pg, tm=512):
    L = x.shape[0]
    pw = w_ple_g.shape[2]

    def body(x_ref, p_ref, t_ref, cat_ref, wo_ref, gpo_ref, wp_ref, gpl_ref, wg_ref,
             loss_ref, dh1_ref, dcat_ref, dwo_ref, dgpo_ref, dwp_ref, dgpl_ref, dwg_ref):
        @pl.when(pl.program_id(0) == 0)
        def _():
            for r in (loss_ref, dwo_ref, dgpo_ref, dwp_ref, dgpl_ref, dwg_ref):
                r[...] = jnp.zeros_like(r)

        cat_b = cat_ref[...]
        wo, wg = wo_ref[...], wg_ref[...]
        g_po, g_pl = gpo_ref[...], gpl_ref[...]
        mix = _dot(cat_b, wo)
        p_b = p_ref[...].astype(BF16)
        ep = jnp.concatenate([_dot(p_b, wp_ref[j]) for j in range(N_CHIPS)], axis=1)
        nm, r2 = _rms(mix)
        h1 = x_ref[...] + nm * g_po
        ne, r3 = _rms(ep)
        e = ne * g_pl
        h1_b = h1.astype(BF16)
        gate = _sigmoid(_dot(h1_b, wg))
        h2 = h1 + gate * e
        diff = h2 - t_ref[...]
        loss_ref[...] += (0.5 / D_MODEL) * jnp.sum(diff * diff).reshape(1, 1)

        dh2 = diff * (1.0 / D_MODEL)
        de = dh2 * gate
        dgl = (dh2 * e * gate * (1.0 - gate)).astype(BF16)
        dh1 = dh2 + _dot_nt(dgl, wg)
        dwg_ref[...] += _dot_tn(h1_b, dgl)
        dgpo_ref[...] += jnp.sum(dh1 * nm, axis=0, keepdims=True)
        dmix = _rms_bwd(dh1 * g_po, nm, r2).astype(BF16)
        dcat_ref[...] = _dot_nt(dmix, wo)
        dwo_ref[...] += _dot_tn(cat_b, dmix)
        dh1_ref[...] = dh1
        dgpl_ref[...] += jnp.sum(de * ne, axis=0, keepdims=True)
        dep = _rms_bwd(de * g_pl, ne, r3).astype(BF16)
        for j in range(N_CHIPS):
            dwp_ref[j] += _dot_tn(p_b, dep[:, j * pw:(j + 1) * pw])

    tile = lambda w: pl.BlockSpec((tm, w), lambda i: (i, 0))
    const = _resident
    sds = jax.ShapeDtypeStruct
    return pl.pallas_call(
        body, name="head", grid=(L // tm,),
        in_specs=[tile(D_MODEL), tile(D_PLE), tile(D_MODEL), tile(1024), const(1024, D_MODEL), const(1, D_MODEL),
                  const(N_CHIPS, D_PLE, pw), const(1, D_MODEL), const(D_MODEL, D_MODEL)],
        out_specs=[const(1, 1), tile(D_MODEL), tile(1024), const(1024, D_MODEL), const(1, D_MODEL),
                   const(N_CHIPS, D_PLE, pw), const(1, D_MODEL), const(D_MODEL, D_MODEL)],
        out_shape=[sds((1, 1), F32), sds((L, D_MODEL), F32), sds((L, 1024), F32), sds((1024, D_MODEL), F32),
                   sds((1, D_MODEL), F32), sds((N_CHIPS, D_PLE, pw), F32), sds((1, D_MODEL), F32),
                   sds((D_MODEL, D_MODEL), F32)],
        compiler_params=_cparams(("arbitrary",)),
    )(x, p, target, cat, w_out, g_post, w_ple_g, g_ple, w_pg)


def _dproj_specs(tm):
    tile = pl.BlockSpec((tm, 512), lambda i: (i, 0))
    return [_chunk_spec(tm), tile, _heads_t_spec(tm), tile, tile, tile]


_DPROJ_ORDER = (3, 4, 5, 1, 2, 0)


def _dproj_part(refs, scr, i):
    if i == 0:
        val = _load_chunks(refs[0], scr)
    elif i == 2:
        val = refs[2][...].T
    else:
        val = refs[i][...]
    return val.astype(BF16)


def _dproj_pieces(i, wn):
    lo, hi = 512 * i, 512 * (i + 1)
    pieces = []
    for j in range(N_CHIPS):
        a, b = max(lo, j * wn), min(hi, (j + 1) * wn)
        if a < b:
            pieces.append((j, slice(a - j * wn, b - j * wn), slice(a - lo, b - lo)))
    return pieces


def _in_proj_bwd_w(x, g_col, w_in_g, dparts, tm=512):
    L = x.shape[0]
    wn = D_IN_PROJ // N_CHIPS
    steps = L // tm

    def body(x_ref, g_ref, w_ref, *refs):
        dw_ref, dg_ref, scr = refs[-3], refs[-2], refs[-1]

        @pl.when(pl.program_id(0) == 0)
        def _():
            dw_ref[...] = jnp.zeros_like(dw_ref)

        n, _ = _rms(x_ref[...])
        nb = n.astype(BF16)
        for i in _DPROJ_ORDER:
            part = _dproj_part(refs[:-3], scr, i)
            for j, w_cols, p_cols in _dproj_pieces(i, wn):
                dw_ref[j, :, w_cols] += _dot_tn(nb, part[:, p_cols])

        @pl.when(pl.program_id(0) == steps - 1)
        def _():
            g = g_ref[...]
            dg = jnp.zeros_like(g)
            for j in range(N_CHIPS):
                a = dw_ref[j]
                dg = dg + jnp.sum(a * w_ref[j].astype(F32), axis=1, keepdims=True)
                dw_ref[j] = a * g
            dg_ref[...] = dg

    return pl.pallas_call(
        body, name="in_proj_bwd_w", grid=(steps,),
        in_specs=[pl.BlockSpec((tm, D_MODEL), lambda i: (i, 0)), _resident(D_MODEL, 1), _resident(N_CHIPS, D_MODEL, wn)]
        + _dproj_specs(tm),
        out_specs=[_resident(N_CHIPS, D_MODEL, wn), _resident(D_MODEL, 1)],
        out_shape=[jax.ShapeDtypeStruct((N_CHIPS, D_MODEL, wn), F32), jax.ShapeDtypeStruct((D_MODEL, 1), F32)],
        scratch_shapes=[_chunk_scratch(tm)],
        compiler_params=_cparams(("arbitrary",)),
    )(x, g_col, w_in_g, *dparts)


def _in_proj_bwd_x(x, g_pre, w_in_g, d_h1, dparts, pair_sums, tm=512):
    L = x.shape[0]
    wn = w_in_g.shape[2]
    n_ps = len(pair_sums)
    steps = L // tm

    def body(*refs):
        x_ref, g_ref, w_ref, dh1_ref = refs[:4]
        dparts_refs = refs[4:10]
        dx_ref = refs[10 + n_ps]
        scr = refs[11 + 2 * n_ps]
        scatter = _ChipScatter(refs[10:10 + n_ps], refs[11 + n_ps:11 + 2 * n_ps], refs[12 + 2 * n_ps:16 + 2 * n_ps],
                               refs[16 + 2 * n_ps:])
        pl.when(pl.program_id(0) == 0)(scatter.start)
        pl.when(pl.program_id(0) == steps - 1)(scatter.finish)

        dhn = None
        for i in _DPROJ_ORDER:
            part = _dproj_part(dparts_refs, scr, i)
            for j, w_cols, p_cols in _dproj_pieces(i, wn):
                term = _dot_nt(part[:, p_cols], w_ref[j, :, w_cols])
                dhn = term if dhn is None else dhn + term
        n, r = _rms(x_ref[...])
        dx_ref[...] = dh1_ref[...] + _rms_bwd(dhn * g_ref[...], n, r)

    wide = pl.BlockSpec((tm, D_MODEL), lambda i: (i, 0))
    outs = pl.pallas_call(
        body, name="in_proj_bwd_x", grid=(steps,),
        in_specs=[wide, _resident(1, D_MODEL), _resident(N_CHIPS, D_MODEL, wn), wide] + _dproj_specs(tm) + _hbm_specs(n_ps),
        out_specs=[wide] + _hbm_specs(n_ps),
        out_shape=[jax.ShapeDtypeStruct((L, D_MODEL), F32)] + [jax.ShapeDtypeStruct(p.shape, p.dtype) for p in pair_sums],
        scratch_shapes=[_chunk_scratch(tm)] + _scatter_scratch(pair_sums),
        compiler_params=_cparams(("arbitrary",), has_side_effects=True),
    )(x, g_pre, w_in_g, d_h1, *dparts, *pair_sums)
    return outs[0], outs[1:]


def _mesh_position():
    x, y, c = lax.axis_index("x"), lax.axis_index("y"), lax.axis_index("c")
    chips = [(1 - x, y), (x, 1 - y), (1 - x, 1 - y)]
    return x, y, c, chips


def _chip_index(cx, cy):
    return 2 * cx + cy


def _hbm_specs(n):
    return [pl.BlockSpec(memory_space=pl.ANY)] * n


def _gather_chips(shards, name):
    n = len(shards)

    def body(*refs):
        gather = _ChipGather(refs[:n], refs[n:2 * n], refs[2 * n:])
        gather.start()
        gather.forward()
        gather.finish()

    return pl.pallas_call(
        body, name=name, in_specs=_hbm_specs(n), out_specs=_hbm_specs(n),
        out_shape=_gather_out_shapes(shards), scratch_shapes=_gather_semaphores(n),
        compiler_params=pltpu.CompilerParams(has_side_effects=True),
    )(*shards)


def _gather_out_shapes(shards):
    return [jax.ShapeDtypeStruct((N_CHIPS,) + s.shape, s.dtype) for s in shards]


def _gather_semaphores(n):
    sem = pltpu.SemaphoreType.DMA
    return [sem((n, 3)), sem((n, 3)), sem((n, 3)), sem((n, 3)), sem((n,)), sem((n,))]


class _ChipGather:
    def __init__(self, ins, outs, sems):
        self.ins, self.outs = ins, outs
        self.send1, self.recv1, self.send2, self.recv2, self.send3, self.recv3 = sems
        self.x, self.y, self.c, self.chips = _mesh_position()
        self.me = _chip_index(self.x, self.y)
        self.sibling = (self.x, self.y, 1 - self.c)

    def _half(self, a, chip, core):
        hr = self.outs[a].shape[1] // 2
        return self.outs[a].at[chip, pl.ds(core * hr, hr)]

    def _own(self, a):
        return pltpu.make_async_remote_copy(
            src_ref=self.ins[a], dst_ref=self.outs[a].at[self.me], send_sem=self.send3.at[a], recv_sem=self.recv3.at[a],
            device_id=self.sibling, device_id_type=MESH)

    def _to_chip(self, a, j):
        hr = self.ins[a].shape[0] // 2
        return pltpu.make_async_remote_copy(
            src_ref=self.ins[a].at[pl.ds(self.c * hr, hr)], dst_ref=self._half(a, self.me, self.c),
            send_sem=self.send1.at[a, j], recv_sem=self.recv1.at[a, j], device_id=(*self.chips[j], self.c), device_id_type=MESH)

    def _from_chip(self, a, j):
        landed = self._half(a, _chip_index(*self.chips[j]), self.c)
        return pltpu.make_async_remote_copy(
            src_ref=landed, dst_ref=landed, send_sem=self.send1.at[a, j], recv_sem=self.recv1.at[a, j],
            device_id=(*self.chips[j], self.c), device_id_type=MESH)

    def _to_sibling(self, a, j, core):
        part = self._half(a, _chip_index(*self.chips[j]), core)
        return pltpu.make_async_remote_copy(
            src_ref=part, dst_ref=part, send_sem=self.send2.at[a, j], recv_sem=self.recv2.at[a, j],
            device_id=self.sibling, device_id_type=MESH)

    def _each(self):
        return [(a, j) for a in range(len(self.ins)) for j in range(3)]

    def start(self):
        for a in range(len(self.ins)):
            self._own(a).start()
        for a, j in self._each():
            self._to_chip(a, j).start()

    def forward(self):
        for a, j in self._each():
            self._from_chip(a, j).wait_recv()
            self._to_sibling(a, j, self.c).start()

    def finish(self):
        for a, j in self._each():
            self._to_sibling(a, j, 1 - self.c).wait_recv()
        for a, j in self._each():
            self._to_chip(a, j).wait_send()
            self._to_sibling(a, j, self.c).wait_send()
        for a in range(len(self.ins)):
            self._own(a).wait()


def _pair_exchange(grads):
    n = len(grads)

    def body(*refs):
        ins, outs = refs[:n], refs[n:2 * n]
        send, recv = refs[2 * n:]
        x, y, c, _ = _mesh_position()
        copies = []
        for a in range(n):
            hr = ins[a].shape[1] // 2
            cp = pltpu.make_async_remote_copy(
                src_ref=ins[a].at[:, pl.ds((1 - c) * hr, hr)], dst_ref=outs[a],
                send_sem=send.at[a], recv_sem=recv.at[a], device_id=(x, y, 1 - c), device_id_type=MESH)
            cp.start()
            copies.append(cp)
        for cp in copies:
            cp.wait()

    sem = pltpu.SemaphoreType.DMA
    return pl.pallas_call(
        body, name="pair_exchange", in_specs=_hbm_specs(n), out_specs=_hbm_specs(n),
        out_shape=[jax.ShapeDtypeStruct((g.shape[0], g.shape[1] // 2, g.shape[2]), g.dtype) for g in grads],
        scratch_shapes=[sem((n,)), sem((n,))],
        compiler_params=pltpu.CompilerParams(has_side_effects=True),
    )(*grads)


def _pair_add(core, grad, other, tr, out_dtype):
    hr = other.shape[1]
    cdim = other.shape[2]
    nb = hr // tr

    def body(core_ref, g_ref, o_ref, out_ref):
        out_ref[...] = (g_ref[...] + o_ref[...]).astype(out_dtype)

    return pl.pallas_call(
        body, name="pair_add",
        grid_spec=pltpu.PrefetchScalarGridSpec(
            num_scalar_prefetch=1, grid=(N_CHIPS, nb),
            in_specs=[pl.BlockSpec((1, tr, cdim), lambda j, i, core_ref: (j, core_ref[0] * nb + i, 0)),
                      pl.BlockSpec((1, tr, cdim), lambda j, i, core_ref: (j, i, 0))],
            out_specs=pl.BlockSpec((1, tr, cdim), lambda j, i, core_ref: (j, i, 0))),
        out_shape=jax.ShapeDtypeStruct(other.shape, out_dtype),
        compiler_params=_cparams(("arbitrary", "arbitrary")),
    )(core, grad, other)


def _scatter_scratch(parts):
    sem = pltpu.SemaphoreType.DMA
    n = len(parts)
    return [sem((n, 3)), sem((n, 3)), sem((n,)), sem((n,))] + [pltpu.VMEM(p.shape[1:], p.dtype) for p in parts]


class _ChipScatter:
    def __init__(self, ins, outs, sems, staged):
        self.ins, self.outs, self.staged = ins, outs, staged
        self.send, self.recv, self.load_sem, self.store_sem = sems
        self.x, self.y, self.c, self.chips = _mesh_position()
        self.me = _chip_index(self.x, self.y)

    def _load(self, a):
        return pltpu.make_async_copy(self.ins[a].at[self.me], self.staged[a], self.load_sem.at[a])

    def _store(self, a):
        return pltpu.make_async_copy(self.staged[a], self.outs[a].at[self.me], self.store_sem.at[a])

    def _to_chip(self, a, j):
        return pltpu.make_async_remote_copy(
            src_ref=self.ins[a].at[_chip_index(*self.chips[j])], dst_ref=self.outs[a].at[self.me],
            send_sem=self.send.at[a, j], recv_sem=self.recv.at[a, j], device_id=(*self.chips[j], self.c), device_id_type=MESH)

    def start(self):
        for a in range(len(self.ins)):
            self._load(a).start()
            for j in range(3):
                self._to_chip(a, j).start()

    def finish(self):
        for a in range(len(self.ins)):
            self._load(a).wait()
            self._store(a).start()
        for a in range(len(self.ins)):
            for j in range(3):
                self._to_chip(a, j).wait()
            self._store(a).wait()


def _chip_add(core, recv, tr):
    hr, cdim = recv.shape[1], recv.shape[2]
    nb = hr // tr

    def body(core_ref, r_ref, out_ref):
        out_ref[...] = ((r_ref[0].astype(F32) + r_ref[1].astype(F32)) + r_ref[2].astype(F32)) + r_ref[3].astype(F32)

    return pl.pallas_call(
        body, name="chip_add",
        grid_spec=pltpu.PrefetchScalarGridSpec(
            num_scalar_prefetch=1, grid=(nb,),
            in_specs=[pl.BlockSpec((N_CHIPS, tr, cdim), lambda i, core_ref: (0, i, 0))],
            out_specs=pl.BlockSpec((tr, cdim), lambda i, core_ref: (core_ref[0] * nb + i, 0))),
        out_shape=jax.ShapeDtypeStruct((2 * hr, cdim), F32),
        compiler_params=_cparams(("arbitrary",)),
    )(core, recv)


def _pair_gather(fulls):
    n = len(fulls)

    def body(*refs):
        outs = refs[n:2 * n]
        send, recv = refs[2 * n:]
        x, y, c, _ = _mesh_position()
        copies = []
        for a in range(n):
            hr = outs[a].shape[0] // 2
            mine = outs[a].at[pl.ds(c * hr, hr)]
            cp = pltpu.make_async_remote_copy(
                src_ref=mine, dst_ref=mine, send_sem=send.at[a], recv_sem=recv.at[a],
                device_id=(x, y, 1 - c), device_id_type=MESH)
            cp.start()
            copies.append(cp)
        for cp in copies:
            cp.wait()

    sem = pltpu.SemaphoreType.DMA
    return pl.pallas_call(
        body, name="pair_gather", in_specs=_hbm_specs(n), out_specs=_hbm_specs(n),
        out_shape=[jax.ShapeDtypeStruct(f.shape, f.dtype) for f in fulls],
        input_output_aliases={a: a for a in range(n)},
        scratch_shapes=[sem((n,)), sem((n,))],
        compiler_params=pltpu.CompilerParams(has_side_effects=True),
    )(*fulls)


def _row_tile(rows):
    if rows <= 512:
        return rows
    for t in (512, 256, 128, 64, 32, 16, 8):
        if rows % t == 0:
            return t
    raise ValueError(rows)


def _pair_sums(core, grads, ici_dtypes):
    others = _pair_exchange(grads)
    return [_pair_add(core, g, o, _row_tile(o.shape[1]), dt) for g, o, dt in zip(grads, others, ici_dtypes)]


def _finish_reduce(core, landed):
    return _pair_gather([_chip_add(core, r, _row_tile(r.shape[1])) for r in landed])


def _adamw(w, g, m, v):
    rows, cols = w.shape
    one_block = rows % 8 != 0 or rows * max(cols, 128) * 4 <= (1 << 20)
    tr = rows if one_block else _row_tile(rows)

    def body(w_ref, g_ref, m_ref, v_ref, d_ref, nm_ref, nv_ref):
        g_ = g_ref[...]
        m_ = ADAM_B1 * m_ref[...] + (1.0 - ADAM_B1) * g_
        v_ = ADAM_B2 * v_ref[...] + (1.0 - ADAM_B2) * (g_ * g_)
        m_hat = m_ / (1.0 - ADAM_B1 ** ADAM_STEP)
        v_hat = v_ / (1.0 - ADAM_B2 ** ADAM_STEP)
        d_ref[...] = -ADAM_LR * (m_hat / (jnp.sqrt(v_hat) + ADAM_EPS) + ADAM_WD * w_ref[...])
        nm_ref[...] = m_
        nv_ref[...] = v_

    spec = pl.BlockSpec((tr, cols), lambda i: (i, 0))
    shp = jax.ShapeDtypeStruct((rows, cols), F32)
    return pl.pallas_call(
        body, name="adamw", grid=(rows // tr,), in_specs=[spec] * 4, out_specs=[spec] * 3,
        out_shape=[shp] * 3, compiler_params=_cparams(("arbitrary",)),
    )(w, g, m, v)


_SMALL = ["norm_pre", "norm_post", "ssm_a_re", "ssm_a_im", "ssm_log_dt", "ssm_b_re", "ssm_b_im",
          "ssm_c_re", "ssm_c_im", "ssm_d", "b_glu", "na_rpb", "ple_norm"]
_BIG = ["w_in", "w_glu", "w_out", "w_ple", "w_ple_gate"]
_WEIGHTS = ["norm_pre", "norm_post", "w_in", "ssm_a_re", "ssm_a_im", "ssm_log_dt", "ssm_b_re", "ssm_b_im",
            "ssm_c_re", "ssm_c_im", "ssm_d", "w_glu", "b_glu", "na_rpb", "w_out", "w_ple", "ple_norm", "w_ple_gate"]
_SMALL_ROWS = 2176


def _pack_small(tensors, tail=None):
    parts = [tensors[n].reshape(-1) for n in _SMALL] + ([] if tail is None else [tail.reshape(-1)])
    flat = jnp.concatenate(parts)
    flat = jnp.pad(flat, (0, _SMALL_ROWS * 128 - flat.shape[0]))
    return flat.reshape(_SMALL_ROWS, 128)


def _unpack_small(packed, shapes):
    flat = packed.reshape(-1)
    out, off = {}, 0
    for n in _SMALL:
        size = int(np.prod(shapes[n]))
        out[n] = flat[off:off + size].reshape(shapes[n])
        off += size
    return out


def _local_grads(x, p, target, wts):
    ssm_names = ["ssm_a_re", "ssm_a_im", "ssm_log_dt", "ssm_b_re", "ssm_b_im", "ssm_c_re", "ssm_c_im", "ssm_d"]
    ssm_params = [wts[n][0] for n in ssm_names]
    blk, blk_vjp = jax.vjp(_ssm_block_params, *ssm_params)
    shard = lambda n: wts[n][0].astype(BF16)
    (m_mat, ws_mat, wot_mat, a16), (w_in_g,) = _ssm_chunk_matrices(blk, [shard("w_in")])
    seq = x.shape[0]
    bias_rows, bias_rows_vjp = jax.vjp(_na_bias_rows, wts["na_rpb"][0])
    bias_tab = _na_bias_table(bias_rows, seq // GRID_W)

    (u_c, z_s, q_t, q, k_t, k, v_t, v, z_n), gathered = _in_proj(
        x, wts["norm_pre"], w_in_g, [shard(n) for n in _BIG if n != "w_in"])
    w_glu, w_out, w_ple_g, w_pg = (gathered[0].reshape(512, 512), gathered[1].reshape(1024, 1024), gathered[2],
                                   gathered[3].reshape(1024, 1024))
    s_in = _block_matmul([(u_c, ws_mat, False)], "ssm_chunk_states")
    s_prev = _ssm_state_scan(s_in, a16)
    y_ssm_c = _block_matmul([(u_c, m_mat, False), (s_prev, wot_mat, True)], "ssm_chunk_out")
    y_na_t = _na_fwd(q_t, k, v_t, bias_tab)
    cat = _branch_fwd(y_ssm_c, z_s, y_na_t, z_n, w_glu, wts["b_glu"])

    (loss, d_h1, d_cat, d_w_out, d_g_post, d_w_ple, d_g_ple, d_w_pg) = _head(
        x, p, target, cat, w_out, wts["norm_post"], w_ple_g, wts["ple_norm"], w_pg)
    dy_c, d_z_s, d_y_na_t, d_y_na, d_z_n, d_w_glu, d_b_glu = _branch_bwd(
        y_ssm_c, z_s, y_na_t, z_n, w_glu, wts["b_glu"], d_cat)
    d_q_t, d_k, d_v, d_bias_tab = _na_bwd(q_t, q, k_t, k, v, bias_tab, y_na_t, d_y_na_t, d_y_na)

    d_prev = _block_matmul([(dy_c, wot_mat, False)], "ssm_bwd_states")
    g_st, d_a16 = _ssm_state_scan_bwd(d_prev, s_prev, a16)
    d_u_c = _block_matmul([(dy_c, m_mat, True), (g_st, ws_mat, True)], "ssm_bwd_in", out_dtype=BF16)
    d_m = _block_matmul_tn(u_c, dy_c, "ssm_grad_m")
    d_ws = _block_matmul_tn(u_c, g_st, "ssm_grad_ws")
    d_wot = _block_matmul_tn(dy_c, s_prev, "ssm_grad_wot")
    d_ssm = blk_vjp(tuple(_ssm_chunk_matrices_bwd(blk, d_m, d_ws, d_wot, d_a16)))
    (d_rpb,) = bias_rows_vjp(_na_bias_table_bwd(d_bias_tab, seq // GRID_W))

    dparts = [d_u_c, d_z_s, d_q_t, d_k, d_v, d_z_n]
    d_w_in, d_g_pre = _in_proj_bwd_w(x, wts["norm_pre"].reshape(D_MODEL, 1), w_in_g, dparts)

    small = {"norm_pre": d_g_pre, "norm_post": d_g_post, "b_glu": d_b_glu, "na_rpb": d_rpb, "ple_norm": d_g_ple}
    for n, g in zip(ssm_names, d_ssm):
        small[n] = g
    big = {"w_in": d_w_in, "w_glu": d_w_glu.reshape(N_CHIPS, 128, 512), "w_out": d_w_out.reshape(N_CHIPS, 256, 1024),
           "w_ple": d_w_ple, "w_ple_gate": d_w_pg.reshape(N_CHIPS, 256, 1024)}
    return loss, small, big, (x, wts["norm_pre"], w_in_g, d_h1, dparts)


def kernel(x, p, norm_pre, norm_post, w_in, ssm_a_re, ssm_a_im, ssm_log_dt, ssm_b_re, ssm_b_im, ssm_c_re, ssm_c_im, ssm_d, w_glu, b_glu, na_rpb, w_out, w_ple, ple_norm, w_ple_gate, loss_target, m_norm_pre, m_norm_post, m_w_in, m_ssm_a_re, m_ssm_a_im, m_ssm_log_dt, m_ssm_b_re, m_ssm_b_im, m_ssm_c_re, m_ssm_c_im, m_ssm_d, m_w_glu, m_b_glu, m_na_rpb, m_w_out, m_w_ple, m_ple_norm, m_w_ple_gate, v_norm_pre, v_norm_post, v_w_in, v_ssm_a_re, v_ssm_a_im, v_ssm_log_dt, v_ssm_b_re, v_ssm_b_im, v_ssm_c_re, v_ssm_c_im, v_ssm_d, v_w_glu, v_b_glu, v_na_rpb, v_w_out, v_w_ple, v_ple_norm, v_w_ple_gate):
    wts = dict(norm_pre=norm_pre, norm_post=norm_post, w_in=w_in, ssm_a_re=ssm_a_re, ssm_a_im=ssm_a_im,
               ssm_log_dt=ssm_log_dt, ssm_b_re=ssm_b_re, ssm_b_im=ssm_b_im, ssm_c_re=ssm_c_re, ssm_c_im=ssm_c_im,
               ssm_d=ssm_d, w_glu=w_glu, b_glu=b_glu, na_rpb=na_rpb, w_out=w_out, w_ple=w_ple, ple_norm=ple_norm,
               w_ple_gate=w_ple_gate)
    mom_m = dict(norm_pre=m_norm_pre, norm_post=m_norm_post, w_in=m_w_in, ssm_a_re=m_ssm_a_re, ssm_a_im=m_ssm_a_im,
                 ssm_log_dt=m_ssm_log_dt, ssm_b_re=m_ssm_b_re, ssm_b_im=m_ssm_b_im, ssm_c_re=m_ssm_c_re,
                 ssm_c_im=m_ssm_c_im, ssm_d=m_ssm_d, w_glu=m_w_glu, b_glu=m_b_glu, na_rpb=m_na_rpb, w_out=m_w_out,
                 w_ple=m_w_ple, ple_norm=m_ple_norm, w_ple_gate=m_w_ple_gate)
    mom_v = dict(norm_pre=v_norm_pre, norm_post=v_norm_post, w_in=v_w_in, ssm_a_re=v_ssm_a_re, ssm_a_im=v_ssm_a_im,
                 ssm_log_dt=v_ssm_log_dt, ssm_b_re=v_ssm_b_re, ssm_b_im=v_ssm_b_im, ssm_c_re=v_ssm_c_re,
                 ssm_c_im=v_ssm_c_im, ssm_d=v_ssm_d, w_glu=v_w_glu, b_glu=v_b_glu, na_rpb=v_na_rpb, w_out=v_w_out,
                 w_ple=v_w_ple, ple_norm=v_ple_norm, w_ple_gate=v_w_ple_gate)

    loss_part, small, big, input_grad_args = _local_grads(x[0], p[0, 0], loss_target[0], wts)

    core = lax.axis_index("c").astype(jnp.int32).reshape(1)
    small_packed = _pack_small(small, tail=loss_part).reshape(N_CHIPS, _SMALL_ROWS // N_CHIPS, 128)
    pair = _pair_sums(core, [big[n] for n in _BIG] + [small_packed], [BF16] * len(_BIG) + [F32])
    grad_x, landed = _in_proj_bwd_x(*input_grad_args, pair)
    reduced = _finish_reduce(core, landed)
    grads = dict(zip(_BIG, reduced[:-1]))
    (small_all,) = _gather_chips([reduced[-1]], "gather_small_grads")
    small_all = small_all.reshape(_SMALL_ROWS, 128)
    loss = small_all.reshape(-1)[sum(int(np.prod(wts[n].shape)) for n in _SMALL)]

    delta, new_m, new_v = {}, {}, {}
    for n in _BIG:
        shp = wts[n].shape
        d_, m_, v_ = _adamw(wts[n][0], grads[n], mom_m[n][0], mom_v[n][0])
        grads[n] = grads[n].reshape(shp)
        delta[n], new_m[n], new_v[n] = d_.reshape(shp), m_.reshape(shp), v_.reshape(shp)
    grads.update(_unpack_small(small_all, {n: wts[n].shape for n in _SMALL}))
    for n in _SMALL:
        shp = wts[n].shape
        rows_cols = (int(np.prod(shp[:-1])), shp[-1])
        d_, m_, v_ = _adamw(*[t.reshape(rows_cols) for t in (wts[n], grads[n], mom_m[n], mom_v[n])])
        delta[n], new_m[n], new_v[n] = d_.reshape(shp), m_.reshape(shp), v_.reshape(shp)

    return (loss, grad_x[None], *[grads[n] for n in _WEIGHTS], *[delta[n] for n in _WEIGHTS],
            *[new_m[n] for n in _WEIGHTS], *[new_v[n] for n in _WEIGHTS])
```

```python
import math

import jax
import jax.numpy as jnp
import numpy as np
from jax import lax
from jax.experimental import pallas as pl
from jax.experimental.pallas import tpu as pltpu

F32 = jnp.float32
BF16 = jnp.bfloat16

D_MODEL = 1024
D_PLE = 256
GRID_W = 64
D_SSM = 512
SSM_GROUP = 16
N_GROUPS = 32
SSM_STATE = 64
D_NA = 512
NA_HEADS = 8
NA_HEAD_DIM = 64
NA_ROWS = 8
NA_COLS = 16
D_IN_PROJ = 3072
EPS = 1e-6

CHUNK = 16
GROUPS_PER_BLOCK = 8
N_BLOCKS = N_GROUPS // GROUPS_PER_BLOCK
BLOCK_CH = GROUPS_PER_BLOCK * SSM_GROUP
BLOCK_ST = GROUPS_PER_BLOCK * SSM_STATE
CHUNK_W = CHUNK * BLOCK_CH
STATE_W = 4 * BLOCK_ST

N_CHIPS = 4
MESH = pl.DeviceIdType.MESH

ADAM_LR = 0.001
ADAM_B1 = 0.9
ADAM_B2 = 0.999
ADAM_EPS = 1e-08
ADAM_WD = 0.01
ADAM_STEP = 10

VMEM_LIMIT = 52 * 1024 * 1024
HIGHEST = lax.Precision.HIGHEST


def _cparams(sem=None, **kw):
    if sem is not None:
        kw["dimension_semantics"] = sem
    return pltpu.CompilerParams(vmem_limit_bytes=VMEM_LIMIT, **kw)


def _resident(*shape):
    return pl.BlockSpec(shape, lambda *_: (0,) * len(shape), pipeline_mode=pl.Buffered(1))


def _dot(a, b, dims=((1,), (0,))):
    return lax.dot_general(a, b, (dims, ((), ())), preferred_element_type=F32)


def _dot_nt(a, b):
    return _dot(a, b, ((1,), (1,)))


def _dot_tn(a, b):
    return _dot(a, b, ((0,), (0,)))


def _sigmoid(x):
    return 1.0 / (1.0 + jnp.exp(-x))


_GELU_C = math.sqrt(2.0 / math.pi)


def _gelu_parts(x):
    inner = _GELU_C * (x + 0.044715 * (x * x * x))
    t = jnp.tanh(inner)
    return 0.5 * x * (1.0 + t), t


def _gelu_grad(x, t):
    return 0.5 * (1.0 + t) + 0.5 * x * (1.0 - t * t) * (_GELU_C * (1.0 + 3.0 * 0.044715 * x * x))


def _silu_parts(z):
    s = _sigmoid(z)
    return z * s, s


def _silu_grad(z, s):
    return s * (1.0 + z * (1.0 - s))


def _rms(x):
    r = lax.rsqrt(jnp.mean(x * x, axis=-1, keepdims=True) + EPS)
    return x * r, r


def _rms_bwd(dn, n, r):
    return r * (dn - n * jnp.mean(dn * n, axis=-1, keepdims=True))


def _chunk_scratch(tm):
    return pltpu.VMEM((N_BLOCKS, tm, BLOCK_CH), F32)


def _store_chunks(val, scr, c_ref, dtype, row0=0):
    rows = val.shape[0]
    nc, c0 = rows // CHUNK, row0 // CHUNK
    for b in range(N_BLOCKS):
        scr[b, row0:row0 + rows, :] = val[:, b * BLOCK_CH:(b + 1) * BLOCK_CH]
        for j in range(CHUNK):
            c_ref[b, c0:c0 + nc, j * BLOCK_CH:(j + 1) * BLOCK_CH] = scr[b, pl.ds(row0 + j, nc, stride=CHUNK), :].astype(dtype)


def _load_chunks(c_ref, scr):
    nc = scr.shape[1] // CHUNK
    for b in range(N_BLOCKS):
        for j in range(CHUNK):
            scr[b, pl.ds(j, nc, stride=CHUNK), :] = c_ref[b, :, j * BLOCK_CH:(j + 1) * BLOCK_CH].astype(F32)
    return jnp.concatenate([scr[b] for b in range(N_BLOCKS)], axis=1)


def _chunk_spec(tm):
    return pl.BlockSpec((N_BLOCKS, tm // CHUNK, CHUNK_W), lambda i: (0, i, 0))


def _heads_t_spec(tm):
    return pl.BlockSpec((D_NA, tm), lambda i: (0, i))


def _in_proj(x, g_pre, w_in_g, shards, tm=512):
    L = x.shape[0]
    wn = w_in_g.shape[2]
    n_sh = len(shards)
    steps = L // tm

    def body(*refs):
        x_ref, g_ref, w_ref = refs[:3]
        uc_ref, zs_ref, qt_ref, q_ref, kt_ref, k_ref, vt_ref, v_ref, zn_ref = refs[3 + n_sh:12 + n_sh]
        u_scr = refs[12 + 2 * n_sh]
        gather = _ChipGather(refs[3:3 + n_sh], refs[12 + n_sh:12 + 2 * n_sh], refs[13 + 2 * n_sh:])
        step = pl.program_id(0)
        pl.when(step == 0)(gather.start)
        pl.when(step == steps // 2)(gather.forward)
        pl.when(step == steps - 1)(gather.finish)
        halves = [slice(0, tm // 2), slice(tm // 2, tm)]
        hn = [(_rms(x_ref[rows, :])[0] * g_ref[...]).astype(BF16) for rows in halves]
        projs = [jnp.concatenate([_dot(h, w_ref[j]) for j in range(N_CHIPS)], axis=1) for h in hn]
        for rows, proj in zip(halves, projs):
            _store_chunks(proj[:, 0:512], u_scr, uc_ref, BF16, row0=rows.start)
            zs_ref[rows, :] = proj[:, 512:1024]
            q = proj[:, 1024:1536] * (NA_HEAD_DIM ** -0.5)
            for val, t_ref, n_ref in ((q, qt_ref, q_ref), (proj[:, 1536:2048], kt_ref, k_ref), (proj[:, 2048:2560], vt_ref, v_ref)):
                t_ref[:, rows] = val.T.astype(BF16)
                n_ref[rows, :] = val.astype(BF16)
            zn_ref[rows, :] = proj[:, 2560:3072]

    tok = jax.ShapeDtypeStruct((L, 512), F32)
    tr = jax.ShapeDtypeStruct((D_NA, L), BF16)
    hm = jax.ShapeDtypeStruct((L, D_NA), BF16)
    tspec = pl.BlockSpec((tm, 512), lambda i: (i, 0))
    outs = pl.pallas_call(
        body, name="in_proj", grid=(steps,),
        in_specs=[pl.BlockSpec((tm, D_MODEL), lambda i: (i, 0)),
                  _resident(1, D_MODEL), _resident(N_CHIPS, D_MODEL, wn)] + _hbm_specs(n_sh),
        out_specs=[_chunk_spec(tm), tspec] + [_heads_t_spec(tm), tspec] * 3 + [tspec] + _hbm_specs(n_sh),
        out_shape=[jax.ShapeDtypeStruct((N_BLOCKS, L // CHUNK, CHUNK_W), BF16), tok, tr, hm, tr, hm, tr, hm, tok]
        + _gather_out_shapes(shards),
        scratch_shapes=[_chunk_scratch(tm)] + _gather_semaphores(n_sh),
        compiler_params=_cparams(("arbitrary",), has_side_effects=True),
    )(x, g_pre, w_in_g, *shards)
    return outs[:9], outs[9:]


def _ssm_block_params(a_re, a_im, log_dt, b_re, b_im, c_re, c_im, d):
    def lanes(t):
        return t.reshape(2, N_BLOCKS, 1, BLOCK_ST)

    rows = (2, N_BLOCKS, BLOCK_CH, SSM_STATE)
    b_rows = lambda t: t.reshape(2, N_BLOCKS, GROUPS_PER_BLOCK, SSM_STATE, SSM_GROUP).transpose(0, 1, 2, 4, 3).reshape(rows)
    return (lanes(a_re), lanes(a_im), lanes(jnp.broadcast_to(log_dt[..., None], a_re.shape)),
            b_rows(b_re), b_rows(b_im), c_re.reshape(rows), c_im.reshape(rows), d.reshape(N_BLOCKS, 1, BLOCK_CH))


def _ssm_group_mask():
    row_g = lax.broadcasted_iota(jnp.int32, (BLOCK_CH, BLOCK_ST), 0) // SSM_GROUP
    lane_g = lax.broadcasted_iota(jnp.int32, (BLOCK_CH, BLOCK_ST), 1) // SSM_STATE
    return row_g == lane_g


def _ssm_state_select():
    p = lax.broadcasted_iota(jnp.int32, (SSM_STATE, BLOCK_ST), 0)
    lane_p = lax.broadcasted_iota(jnp.int32, (SSM_STATE, BLOCK_ST), 1) % SSM_STATE
    return (p == lane_p).astype(F32)


def _ssm_expand_blocks(compact_refs, full_refs):
    mask, select = _ssm_group_mask(), _ssm_state_select()
    for c_ref, f_ref in zip(compact_refs, full_refs):
        for d in range(2):
            tiled = lax.dot_general(c_ref[d, 0], select, ((((1,), (0,))), ((), ())), precision=HIGHEST,
                                    preferred_element_type=F32)
            f_ref[d, 0] = jnp.where(mask, tiled, 0.0)


def _ssm_collapse_block(t):
    return lax.dot_general(jnp.where(_ssm_group_mask(), t, 0.0), _ssm_state_select(), ((((1,), (1,))), ((), ())),
                           precision=HIGHEST, preferred_element_type=F32)


def _ssm_discretise(ar, ai, ldt):
    dt = jnp.exp(ldt)
    mag = jnp.exp(dt * ar)
    abr = mag * jnp.cos(dt * ai)
    abi = mag * jnp.sin(dt * ai)
    num_re = abr - 1.0
    num_im = abi
    denom = ar * ar + ai * ai
    coef_re = (num_re * ar + num_im * ai) / denom
    coef_im = (num_im * ar - num_re * ai) / denom
    return abr, abi, coef_re, coef_im


_POW_ROWS = 24


def _ssm_fill_powers(ar_ref, ai_ref, ldt_ref, br_ref, bi_ref, pw_ref, bbar_ref):
    for d in range(2):
        abr, abi, cfr, cfi = _ssm_discretise(ar_ref[d, 0], ai_ref[d, 0], ldt_ref[d, 0])
        bbar_ref[d, 0] = cfr * br_ref[d, 0] - cfi * bi_ref[d, 0]
        bbar_ref[d, 1] = cfr * bi_ref[d, 0] + cfi * br_ref[d, 0]
        pr, pi = jnp.ones_like(abr), jnp.zeros_like(abi)
        for t in range(CHUNK + 1):
            pw_ref[d, 0, t:t + 1, :] = pr
            pw_ref[d, 1, t:t + 1, :] = pi
            pr, pi = pr * abr - pi * abi, pr * abi + pi * abr


def _dot_rounded(a, b, dims=((1,), (0,))):
    return _dot(a.astype(BF16), b.astype(BF16), dims)


def _ssm_stack_inputs(d, pw_ref, bbar_ref, xs_ref):
    for t in range(CHUNK):
        pr, pi = pw_ref[d, 0, t:t + 1, :], pw_ref[d, 1, t:t + 1, :]
        xs_ref[0, t * BLOCK_CH:(t + 1) * BLOCK_CH, :] = bbar_ref[d, 0] * pr - bbar_ref[d, 1] * pi
        xs_ref[1, t * BLOCK_CH:(t + 1) * BLOCK_CH, :] = bbar_ref[d, 0] * pi + bbar_ref[d, 1] * pr


def _eye(n):
    return (lax.broadcasted_iota(jnp.int32, (n, n), 0) == lax.broadcasted_iota(jnp.int32, (n, n), 1)).astype(F32)


def _ssm_param_specs():
    vec = pl.BlockSpec((2, 1, 1, BLOCK_ST), lambda b, j: (0, b, 0, 0))
    mat = pl.BlockSpec((2, 1, BLOCK_CH, SSM_STATE), lambda b, j: (0, b, 0, 0))
    return [vec, vec, vec, mat, mat, mat, mat, pl.BlockSpec((1, 1, BLOCK_CH), lambda b, j: (b, 0, 0))]


def _ssm_block_scratch():
    return [pltpu.VMEM((2, 1, BLOCK_CH, BLOCK_ST), F32)] * 4


def _ssm_chunk_matrices(blk, shards):
    n = len(shards)

    def body(*refs):
        ar_ref, ai_ref, ldt_ref = refs[:3]
        d_ref = refs[7]
        m_ref, ws_ref, wot_ref, a16_ref = refs[8 + n:12 + n]
        pw_ref, bbar_ref, lag_ref, xs_ref = refs[12 + 2 * n:16 + 2 * n]
        br_ref, bi_ref, cr_ref, ci_ref = refs[16 + 2 * n:20 + 2 * n]
        gather = _ChipGather(refs[8:8 + n], refs[12 + n:12 + 2 * n], refs[20 + 2 * n:])
        b, j = pl.program_id(0), pl.program_id(1)
        pl.when((b == 0) & (j == 0))(gather.start)
        pl.when((b == N_BLOCKS - 1) & (j == 0))(gather.forward)
        pl.when((b == N_BLOCKS - 1) & (j == CHUNK - 1))(gather.finish)

        @pl.when(j == 0)
        def _():
            _ssm_expand_blocks(refs[3:7], (br_ref, bi_ref, cr_ref, ci_ref))
            _ssm_fill_powers(ar_ref, ai_ref, ldt_ref, br_ref, bi_ref, pw_ref, bbar_ref)
            zero_lag = d_ref[0] * _eye(BLOCK_CH)
            for d in range(2):
                _ssm_stack_inputs(d, pw_ref, bbar_ref, xs_ref)
                taps = (_dot_rounded(xs_ref[0], cr_ref[d, 0], ((1,), (1,)))
                        - _dot_rounded(xs_ref[1], ci_ref[d, 0], ((1,), (1,))))
                zero_lag = zero_lag + taps[0:BLOCK_CH]
                for t in range(1, CHUNK):
                    lag_ref[CHUNK - 1 + t if d == 0 else CHUNK - 1 - t] = taps[t * BLOCK_CH:(t + 1) * BLOCK_CH]
            lag_ref[CHUNK - 1] = zero_lag
            a16_ref[0] = jnp.concatenate([pw_ref[d, ri, CHUNK:CHUNK + 1, :] for d in range(2) for ri in range(2)], axis=1)

        m_ref[0] = jnp.concatenate([lag_ref[jp - j + CHUNK - 1] for jp in range(CHUNK)], axis=1).astype(BF16)

        def power(d, t):
            return pw_ref[d, 0, pl.ds(t, 1), :], pw_ref[d, 1, pl.ds(t, 1), :]

        parts = []
        for d, t in ((0, CHUNK - 1 - j), (1, j)):
            pr, pi = power(d, t)
            parts += [bbar_ref[d, 0] * pr - bbar_ref[d, 1] * pi, bbar_ref[d, 0] * pi + bbar_ref[d, 1] * pr]
        ws_ref[0] = jnp.concatenate(parts, axis=1).astype(BF16)
        parts = []
        for d, t in ((0, j + 1), (1, CHUNK - j)):
            pr, pi = power(d, t)
            parts += [cr_ref[d, 0] * pr - ci_ref[d, 0] * pi, -cr_ref[d, 0] * pi - ci_ref[d, 0] * pr]
        wot_ref[0] = jnp.concatenate(parts, axis=1).astype(BF16)

    row = pl.BlockSpec((1, BLOCK_CH, CHUNK_W), lambda b, j: (b, j, 0))
    mat = jax.ShapeDtypeStruct((N_BLOCKS, CHUNK_W, CHUNK_W), BF16)
    outs = pl.pallas_call(
        body, name="ssm_chunk_matrices", grid=(N_BLOCKS, CHUNK),
        in_specs=_ssm_param_specs() + _hbm_specs(n),
        out_specs=[row, row, row, pl.BlockSpec((1, 1, STATE_W), lambda b, j: (b, 0, 0))] + _hbm_specs(n),
        out_shape=[mat, mat, mat, jax.ShapeDtypeStruct((N_BLOCKS, 1, STATE_W), F32)] + _gather_out_shapes(shards),
        scratch_shapes=[pltpu.VMEM((2, 2, _POW_ROWS, BLOCK_ST), F32), pltpu.VMEM((2, 2, BLOCK_CH, BLOCK_ST), F32),
                        pltpu.VMEM((2 * CHUNK, BLOCK_CH, BLOCK_CH), F32), pltpu.VMEM((2, CHUNK_W, BLOCK_ST), F32)]
        + _ssm_block_scratch() + _gather_semaphores(n),
        compiler_params=_cparams(("arbitrary", "arbitrary"), has_side_effects=True),
    )(*blk, *shards)
    return outs[:4], outs[4:]


def _ssm_chunk_matrices_bwd(blk, d_m, d_ws, d_wot, d_a16):
    def body(ar_ref, ai_ref, ldt_ref, brc_ref, bic_ref, crc_ref, cic_ref, d_ref, dm_ref, dws_ref, dwot_ref, da16_ref,
             dar_ref, dai_ref, dldt_ref, dbr_ref, dbi_ref, dcr_ref, dci_ref, dd_ref,
             pw_ref, bbar_ref, dlag_ref, dbbar_ref, dc_ref, dpw_ref, xs_ref, dts_ref, br_ref, bi_ref, cr_ref, ci_ref):
        j = pl.program_id(1)
        w = BLOCK_ST

        @pl.when(j == 0)
        def _():
            _ssm_expand_blocks((brc_ref, bic_ref, crc_ref, cic_ref), (br_ref, bi_ref, cr_ref, ci_ref))
            _ssm_fill_powers(ar_ref, ai_ref, ldt_ref, br_ref, bi_ref, pw_ref, bbar_ref)
            for r in (dlag_ref, dbbar_ref, dc_ref, dpw_ref):
                r[...] = jnp.zeros_like(r)

        def fold(t):
            return jnp.sum(t.reshape(BLOCK_CH // 8, 8, w), axis=0)

        def d_power(d, ri, t):
            return jnp.sum(dpw_ref[d, ri, t], axis=0, keepdims=True)

        def x_chain(d, t, dxr, dxi):
            pr, pi = pw_ref[d, 0, pl.ds(t, 1), :], pw_ref[d, 1, pl.ds(t, 1), :]
            bbr, bbi = bbar_ref[d, 0], bbar_ref[d, 1]
            dbbar_ref[d, 0] += dxr * pr + dxi * pi
            dbbar_ref[d, 1] += dxi * pr - dxr * pi
            dpw_ref[d, 0, t] += fold(dxr * bbr + dxi * bbi)
            dpw_ref[d, 1, t] += fold(dxi * bbr - dxr * bbi)

        def z_chain(d, t, dzr, dzi):
            pr, pi = pw_ref[d, 0, pl.ds(t, 1), :], pw_ref[d, 1, pl.ds(t, 1), :]
            c_r, c_i = cr_ref[d, 0], ci_ref[d, 0]
            dc_ref[d, 0] += dzr * pr - dzi * pi
            dc_ref[d, 1] += -dzr * pi - dzi * pr
            dpw_ref[d, 0, t] += fold(dzr * c_r - dzi * c_i)
            dpw_ref[d, 1, t] += fold(-dzr * c_i - dzi * c_r)

        for jp in range(CHUNK):
            dlag_ref[jp - j + CHUNK - 1] += dm_ref[0, :, jp * BLOCK_CH:(jp + 1) * BLOCK_CH].astype(F32)
        quarter = lambda ref, i: ref[0, :, i * w:(i + 1) * w].astype(F32)
        x_chain(0, CHUNK - 1 - j, quarter(dws_ref, 0), quarter(dws_ref, 1))
        x_chain(1, j, quarter(dws_ref, 2), quarter(dws_ref, 3))
        z_chain(0, j + 1, quarter(dwot_ref, 0), quarter(dwot_ref, 1))
        z_chain(1, CHUNK - j, quarter(dwot_ref, 2), quarter(dwot_ref, 3))

        @pl.when(j == CHUNK - 1)
        def _():
            for d in range(2):
                _ssm_stack_inputs(d, pw_ref, bbar_ref, xs_ref)
                for t in range(CHUNK):
                    dts_ref[t * BLOCK_CH:(t + 1) * BLOCK_CH, :] = dlag_ref[CHUNK - 1 + t if d == 0 else CHUNK - 1 - t]
                d_taps = dts_ref[...]
                dc_ref[d, 0] += _dot_rounded(d_taps, xs_ref[0], ((0,), (0,)))
                dc_ref[d, 1] -= _dot_rounded(d_taps, xs_ref[1], ((0,), (0,)))
                xs_ref[0] = _dot_rounded(d_taps, cr_ref[d, 0])
                xs_ref[1] = -_dot_rounded(d_taps, ci_ref[d, 0])
                for t in range(CHUNK):
                    rows = slice(t * BLOCK_CH, (t + 1) * BLOCK_CH)
                    x_chain(d, t, xs_ref[0, rows, :], xs_ref[1, rows, :])
            dd_ref[0] = jnp.sum(dlag_ref[CHUNK - 1] * _eye(BLOCK_CH), axis=0, keepdims=True)
            for d in range(2):
                (abr, abi, cfr, cfi), disc_vjp = jax.vjp(_ssm_discretise, ar_ref[d, 0], ai_ref[d, 0], ldt_ref[d, 0])
                dpr = d_power(d, 0, CHUNK) + da16_ref[0, :, 2 * d * w:(2 * d + 1) * w]
                dpi = d_power(d, 1, CHUNK) + da16_ref[0, :, (2 * d + 1) * w:(2 * d + 2) * w]
                dabr, dabi = jnp.zeros_like(abr), jnp.zeros_like(abi)
                for t in range(CHUNK, 0, -1):
                    qr, qi = pw_ref[d, 0, t - 1:t, :], pw_ref[d, 1, t - 1:t, :]
                    dabr = dabr + dpr * qr + dpi * qi
                    dabi = dabi + dpi * qr - dpr * qi
                    dpr, dpi = (dpr * abr + dpi * abi + d_power(d, 0, t - 1),
                                dpi * abr - dpr * abi + d_power(d, 1, t - 1))
                dbbr, dbbi = dbbar_ref[d, 0], dbbar_ref[d, 1]
                b_r, b_i = br_ref[d, 0], bi_ref[d, 0]
                dbr_ref[d, 0] = _ssm_collapse_block(cfr * dbbr + cfi * dbbi)
                dbi_ref[d, 0] = _ssm_collapse_block(cfr * dbbi - cfi * dbbr)
                dcfr = jnp.sum(b_r * dbbr + b_i * dbbi, axis=0, keepdims=True)
                dcfi = jnp.sum(b_r * dbbi - b_i * dbbr, axis=0, keepdims=True)
                dar_ref[d, 0], dai_ref[d, 0], dldt_ref[d, 0] = disc_vjp((dabr, dabi, dcfr, dcfi))
                dcr_ref[d, 0] = _ssm_collapse_block(dc_ref[d, 0])
                dci_ref[d, 0] = _ssm_collapse_block(dc_ref[d, 1])

    row = pl.BlockSpec((1, BLOCK_CH, CHUNK_W), lambda b, j: (b, j, 0))
    specs = _ssm_param_specs()
    acc = lambda *s: pltpu.VMEM(s, F32)
    return pl.pallas_call(
        body, name="ssm_chunk_matrices_bwd", grid=(N_BLOCKS, CHUNK),
        in_specs=specs + [row, row, row, pl.BlockSpec((1, 1, STATE_W), lambda b, j: (b, 0, 0))],
        out_specs=specs,
        out_shape=[jax.ShapeDtypeStruct(t.shape, F32) for t in blk],
        scratch_shapes=[acc(2, 2, _POW_ROWS, BLOCK_ST), acc(2, 2, BLOCK_CH, BLOCK_ST), acc(2 * CHUNK, BLOCK_CH, BLOCK_CH),
                        acc(2, 2, BLOCK_CH, BLOCK_ST), acc(2, 2, BLOCK_CH, BLOCK_ST), acc(2, 2, CHUNK + 1, 8, BLOCK_ST),
                        acc(2, CHUNK_W, BLOCK_ST), acc(CHUNK_W, BLOCK_CH)] + _ssm_block_scratch(),
        compiler_params=_cparams(("arbitrary", "arbitrary")),
    )(*blk, d_m, d_ws, d_wot, d_a16)


def _block_matmul(terms, name, out_dtype=F32, tn=1024):
    nc = terms[0][0].shape[1]
    n_out = terms[0][1].shape[1] if terms[0][2] else terms[0][1].shape[2]
    flags = [t[2] for t in terms]

    def body(*refs):
        out_ref = refs[-1]
        acc = None
        for t, transposed in enumerate(flags):
            a = refs[2 * t][0].astype(BF16)
            w = refs[2 * t + 1][0]
            part = _dot_nt(a, w) if transposed else _dot(a, w)
            acc = part if acc is None else acc + part
        out_ref[0] = acc.astype(out_dtype)

    in_specs, args = [], []
    for a, w, transposed in terms:
        k = a.shape[2]
        in_specs.append(pl.BlockSpec((1, nc, k), lambda b, n: (b, 0, 0)))
        if transposed:
            in_specs.append(pl.BlockSpec((1, tn, k), lambda b, n: (b, n, 0)))
        else:
            in_specs.append(pl.BlockSpec((1, k, tn), lambda b, n: (b, 0, n)))
        args += [a, w]
    return pl.pallas_call(
        body, name=name, grid=(N_BLOCKS, n_out // tn), in_specs=in_specs,
        out_specs=pl.BlockSpec((1, nc, tn), lambda b, n: (b, 0, n)),
        out_shape=jax.ShapeDtypeStruct((N_BLOCKS, nc, n_out), out_dtype),
        compiler_params=_cparams(("arbitrary", "arbitrary")),
    )(*args)


def _block_matmul_tn(a, b, name, tile=1024):
    nc, m = a.shape[1], a.shape[2]
    n = b.shape[2]

    def body(a_ref, b_ref, out_ref):
        a_t = a_ref[0].astype(BF16)
        for j in range(n // tile):
            cols = slice(j * tile, (j + 1) * tile)
            out_ref[0, :, cols] = _dot_tn(a_t, b_ref[0, :, cols].astype(BF16)).astype(BF16)

    return pl.pallas_call(
        body, name=name, grid=(N_BLOCKS, m // tile),
        in_specs=[pl.BlockSpec((1, nc, tile), lambda blk, i: (blk, 0, i)),
                  pl.BlockSpec((1, nc, n), lambda blk, i: (blk, 0, 0))],
        out_specs=pl.BlockSpec((1, tile, n), lambda blk, i: (blk, i, 0)),
        out_shape=jax.ShapeDtypeStruct((N_BLOCKS, m, n), BF16),
        compiler_params=_cparams(("arbitrary", "arbitrary")),
    )(a, b)


def _cmul(ar, ai, xr, xi):
    return ar * xr - ai * xi, ar * xi + ai * xr


def _cmul_conj(ar, ai, xr, xi):
    return ar * xr + ai * xi, ar * xi - ai * xr


_SCAN_UNROLL = 8


def _ssm_state_scan(s_in, a16):
    nc = s_in.shape[1]
    w = BLOCK_ST

    def body(sin_ref, a_ref, out_ref):
        a = a_ref[0]
        afr, afi, abr, abi = a[:, 0:w], a[:, w:2 * w], a[:, 2 * w:3 * w], a[:, 3 * w:4 * w]

        def step(c, carry):
            fr, fi, br, bi = carry
            cb = nc - 1 - c
            out_ref[0, pl.ds(c, 1), 0:w] = fr
            out_ref[0, pl.ds(c, 1), w:2 * w] = fi
            out_ref[0, pl.ds(cb, 1), 2 * w:3 * w] = br
            out_ref[0, pl.ds(cb, 1), 3 * w:4 * w] = bi
            nfr, nfi = _cmul(afr, afi, fr, fi)
            nbr, nbi = _cmul(abr, abi, br, bi)
            return (nfr + sin_ref[0, pl.ds(c, 1), 0:w], nfi + sin_ref[0, pl.ds(c, 1), w:2 * w],
                    nbr + sin_ref[0, pl.ds(cb, 1), 2 * w:3 * w], nbi + sin_ref[0, pl.ds(cb, 1), 3 * w:4 * w])

        def steps(i, carry):
            for k in range(_SCAN_UNROLL):
                carry = step(i * _SCAN_UNROLL + k, carry)
            return carry

        z = jnp.zeros((1, w), F32)
        lax.fori_loop(0, nc // _SCAN_UNROLL, steps, (z, z, z, z))

    spec = pl.BlockSpec((1, nc, STATE_W), lambda b: (b, 0, 0))
    return pl.pallas_call(
        body, name="ssm_state_scan", grid=(N_BLOCKS,),
        in_specs=[spec, pl.BlockSpec((1, 1, STATE_W), lambda b: (b, 0, 0))],
        out_specs=spec, out_shape=jax.ShapeDtypeStruct(s_in.shape, F32),
        compiler_params=_cparams(("arbitrary",)),
    )(s_in, a16)


def _ssm_state_scan_bwd(d_prev, s_prev, a16):
    nc = d_prev.shape[1]
    w = BLOCK_ST

    def body(dp_ref, sp_ref, a_ref, g_ref, da_ref):
        a = a_ref[0]
        afr, afi, abr, abi = a[:, 0:w], a[:, w:2 * w], a[:, 2 * w:3 * w], a[:, 3 * w:4 * w]

        def step(i, carry):
            gfr, gfi, gbr, gbi, dafr, dafi, dabr, dabi = carry
            cf = nc - 1 - i
            cb = i
            g_ref[0, pl.ds(cf, 1), 0:w] = gfr
            g_ref[0, pl.ds(cf, 1), w:2 * w] = gfi
            g_ref[0, pl.ds(cb, 1), 2 * w:3 * w] = gbr
            g_ref[0, pl.ds(cb, 1), 3 * w:4 * w] = gbi
            sfr, sfi = sp_ref[0, pl.ds(cf, 1), 0:w], sp_ref[0, pl.ds(cf, 1), w:2 * w]
            sbr, sbi = sp_ref[0, pl.ds(cb, 1), 2 * w:3 * w], sp_ref[0, pl.ds(cb, 1), 3 * w:4 * w]
            dafr = dafr + gfr * sfr + gfi * sfi
            dafi = dafi + gfi * sfr - gfr * sfi
            dabr = dabr + gbr * sbr + gbi * sbi
            dabi = dabi + gbi * sbr - gbr * sbi
            nfr, nfi = _cmul_conj(afr, afi, gfr, gfi)
            nbr, nbi = _cmul_conj(abr, abi, gbr, gbi)
            return (nfr + dp_ref[0, pl.ds(cf, 1), 0:w], nfi + dp_ref[0, pl.ds(cf, 1), w:2 * w],
                    nbr + dp_ref[0, pl.ds(cb, 1), 2 * w:3 * w], nbi + dp_ref[0, pl.ds(cb, 1), 3 * w:4 * w],
                    dafr, dafi, dabr, dabi)

        def steps(i, carry):
            for k in range(_SCAN_UNROLL):
                carry = step(i * _SCAN_UNROLL + k, carry)
            return carry

        z = jnp.zeros((1, w), F32)
        res = lax.fori_loop(0, nc // _SCAN_UNROLL, steps, (z,) * 8)
        da_ref[0] = jnp.concatenate(res[4:], axis=1)

    spec = pl.BlockSpec((1, nc, STATE_W), lambda b: (b, 0, 0))
    aspec = pl.BlockSpec((1, 1, STATE_W), lambda b: (b, 0, 0))
    return pl.pallas_call(
        body, name="ssm_state_scan_bwd", grid=(N_BLOCKS,),
        in_specs=[spec, spec, aspec], out_specs=[spec, aspec],
        out_shape=[jax.ShapeDtypeStruct(d_prev.shape, F32), jax.ShapeDtypeStruct((N_BLOCKS, 1, STATE_W), F32)],
        compiler_params=_cparams(("arbitrary",)),
    )(d_prev, s_prev, a16)


NA_PAIR = 2 * GRID_W
NA_WIN_ROWS = NA_ROWS + 2
NA_WIN = NA_WIN_ROWS * GRID_W
NA_PAIRS_PER_STEP = 8
NA_CASES = 5
NA_MASKED = -1e30


def _na_pair_window(m, rows):
    rs0 = jnp.clip(2 * m - NA_ROWS // 2, 0, rows - NA_ROWS)
    ws = jnp.minimum(rs0, rows - NA_WIN_ROWS)
    last = rows // 2 - 1
    case = jnp.where(m == 0, 0, jnp.where(m == 1, 1, jnp.where(m == last - 1, 3, jnp.where(m == last, 4, 2))))
    return ws, case


def _na_row_offsets(rows):
    last = rows // 2 - 1
    geom = []
    for m in (0, 1, 2, last - 1, last):
        ws = min(max(2 * m - NA_ROWS // 2, 0), rows - NA_ROWS, rows - NA_WIN_ROWS)
        per_case = []
        for i in range(NA_WIN_ROWS):
            pair = []
            for rr in range(2):
                r = 2 * m + rr
                rs = min(max(r - NA_ROWS // 2, 0), rows - NA_ROWS)
                pair.append(ws + i - r + NA_ROWS - 1 if rs <= ws + i < rs + NA_ROWS else None)
            per_case.append(pair)
        geom.append(per_case)
    return geom


def _na_col_select():
    qc = np.arange(NA_PAIR)[None, :] % GRID_W
    kc = np.arange(GRID_W)[:, None]
    dc = np.clip(kc - qc + NA_COLS - 1, 0, 2 * NA_COLS - 2)
    return jnp.asarray((np.arange(2 * NA_COLS - 1)[:, None, None] == dc[None]).astype(np.float32))


def _na_bias_rows(rpb):
    return jnp.einsum("hrd,dkl->hrkl", rpb, _na_col_select(), precision=HIGHEST)


def _na_col_window():
    qc = lax.broadcasted_iota(jnp.int32, (GRID_W, NA_PAIR), 1) % GRID_W
    kc = lax.broadcasted_iota(jnp.int32, (GRID_W, NA_PAIR), 0)
    cs = jnp.clip(qc - NA_COLS // 2, 0, GRID_W - NA_COLS)
    first_row = lax.broadcasted_iota(jnp.int32, (GRID_W, NA_PAIR), 1) < GRID_W
    return (kc >= cs) & (kc < cs + NA_COLS), first_row


def _na_bias_table(bias_rows, rows):
    geom = _na_row_offsets(rows)

    def body(br_ref, tab_ref):
        col_ok, first_row = _na_col_window()
        masked = jnp.full((GRID_W, NA_PAIR), NA_MASKED, F32)
        for case in range(NA_CASES):
            for i in range(NA_WIN_ROWS):
                d0, d1 = geom[case][i]
                t0 = masked if d0 is None else br_ref[0, d0]
                t1 = masked if d1 is None else br_ref[0, d1]
                tile = jnp.where(col_ok, jnp.where(first_row, t0, t1), NA_MASKED)
                tab_ref[0, case, i * GRID_W:(i + 1) * GRID_W, :] = tile

    return pl.pallas_call(
        body, name="na_bias_table", grid=(NA_HEADS,),
        in_specs=[pl.BlockSpec((1, 2 * NA_ROWS - 1, GRID_W, NA_PAIR), lambda h: (h, 0, 0, 0))],
        out_specs=pl.BlockSpec((1, NA_CASES, NA_WIN, NA_PAIR), lambda h: (h, 0, 0, 0)),
        out_shape=jax.ShapeDtypeStruct((NA_HEADS, NA_CASES, NA_WIN, NA_PAIR), F32),
        compiler_params=_cparams(("arbitrary",)),
    )(bias_rows)


def _na_bias_table_bwd(d_tab, rows):
    geom = _na_row_offsets(rows)

    def body(dt_ref, dbr_ref):
        col_ok, first_row = _na_col_window()
        acc = [None] * (2 * NA_ROWS - 1)
        for case in range(NA_CASES):
            for i in range(NA_WIN_ROWS):
                tile = jnp.where(col_ok, dt_ref[0, case, i * GRID_W:(i + 1) * GRID_W, :], 0.0)
                for rr, d in enumerate(geom[case][i]):
                    if d is not None:
                        part = jnp.where(first_row if rr == 0 else ~first_row, tile, 0.0)
                        acc[d] = part if acc[d] is None else acc[d] + part
        for d, a in enumerate(acc):
            dbr_ref[0, d] = jnp.zeros((GRID_W, NA_PAIR), F32) if a is None else a

    return pl.pallas_call(
        body, name="na_bias_table_bwd", grid=(NA_HEADS,),
        in_specs=[pl.BlockSpec((1, NA_CASES, NA_WIN, NA_PAIR), lambda h: (h, 0, 0, 0))],
        out_specs=pl.BlockSpec((1, 2 * NA_ROWS - 1, GRID_W, NA_PAIR), lambda h: (h, 0, 0, 0)),
        out_shape=jax.ShapeDtypeStruct((NA_HEADS, 2 * NA_ROWS - 1, GRID_W, NA_PAIR), F32),
        compiler_params=_cparams(("arbitrary",)),
    )(d_tab)


NA_BLK = 64


def _na_blocks():
    return [slice(i * NA_BLK, (i + 1) * NA_BLK) for i in range(NA_WIN // NA_BLK)]


def _na_softmax(qk, bias_ref, hh, case):
    m = jnp.full((NA_BLK, NA_PAIR), -jnp.inf, F32)
    scores = []
    for blk in _na_blocks():
        s = qk[blk, :] + bias_ref[hh, case, blk, :]
        scores.append(s)
        m = jnp.maximum(m, s)
    m = jnp.max(m, axis=0, keepdims=True)
    l = jnp.zeros((NA_BLK, NA_PAIR), F32)
    exps = []
    for s in scores:
        e = jnp.exp(s - m)
        exps.append(e)
        l = l + e
    return exps, jnp.sum(l, axis=0, keepdims=True)


def _na_units(step, rows):
    units = []
    for pp in range(NA_PAIRS_PER_STEP):
        ws, case = _na_pair_window(step * NA_PAIRS_PER_STEP + pp, rows)
        win = pl.ds(pl.multiple_of(ws * GRID_W, NA_PAIR), NA_WIN)
        lanes = slice(pp * NA_PAIR, (pp + 1) * NA_PAIR)
        for hh in range(2):
            units.append((pp, hh, case, win, lanes, slice(hh * NA_HEAD_DIM, (hh + 1) * NA_HEAD_DIM)))
    return units


def _na_pipeline(n, before, middle, after, lookahead):
    for u in range(min(lookahead, n)):
        for f in before:
            f(u)
    for u in range(n):
        middle(u)
        if u + lookahead < n:
            for f in before:
                f(u + lookahead)
        for f in after:
            f(u)


def _head_rows(t, hh):
    row_head = lax.broadcasted_iota(jnp.int32, t.shape, 0) // NA_HEAD_DIM
    return jnp.where(row_head == hh, t, jnp.zeros_like(t))


def _heads_block_diag(t):
    lane_head = lax.broadcasted_iota(jnp.int32, t.shape, 1) // NA_HEAD_DIM
    zero = jnp.zeros_like(t)
    return jnp.concatenate([jnp.where(lane_head == 0, t, zero), jnp.where(lane_head == 1, t, zero)], axis=0)


def _na_fwd(q_t, k, v_t, bias_tab):
    L = k.shape[0]
    rows = L // GRID_W
    step_w = NA_PAIRS_PER_STEP * NA_PAIR

    def body(q_ref, k_ref, v_ref, bt_ref, o_ref):
        units = _na_units(pl.program_id(1), rows)
        qk, probs = {}, {}

        def scores(u):
            _, hh, _, win, lanes, _ = units[u]
            qk[u] = _dot(k_ref[win, :], _head_rows(q_ref[:, lanes], hh))

        def softmax(u):
            _, hh, case, _, _, _ = units[u]
            exps, l = _na_softmax(qk.pop(u), bt_ref, hh, case)
            probs[u] = jnp.concatenate([t.astype(BF16) for t in exps], axis=0), l

        def output(u):
            _, _, _, win, lanes, hrows = units[u]
            e, l = probs.pop(u)
            o_ref[hrows, lanes] = _dot(v_ref[hrows, win], e) / l

        _na_pipeline(len(units), [scores], softmax, [output], lookahead=3)

    q_spec = pl.BlockSpec((NA_PAIR, step_w), lambda h, s: (h, s))
    return pl.pallas_call(
        body, name="na_fwd", grid=(NA_HEADS // 2, L // step_w),
        in_specs=[q_spec, pl.BlockSpec((L, NA_PAIR), lambda h, s: (0, h)),
                  pl.BlockSpec((NA_PAIR, L), lambda h, s: (h, 0)),
                  pl.BlockSpec((2, NA_CASES, NA_WIN, NA_PAIR), lambda h, s: (h, 0, 0, 0))],
        out_specs=q_spec,
        out_shape=jax.ShapeDtypeStruct((D_NA, L), F32),
        compiler_params=_cparams(("arbitrary", "arbitrary")),
    )(q_t, k, v_t, bias_tab)


def _na_bwd(q_t, q, k_t, k, v, bias_tab, out_t, d_out_t, d_out):
    L = k.shape[0]
    rows = L // GRID_W
    step_w = NA_PAIRS_PER_STEP * NA_PAIR

    def body(qt_ref, q_ref, kt_ref, k_ref, v_ref, bt_ref, ot_ref, dot_ref, do_ref, dq_ref, dk_ref, dv_ref, dbt_ref):
        @pl.when(pl.program_id(1) == 0)
        def _():
            dk_ref[...] = jnp.zeros_like(dk_ref)
            dv_ref[...] = jnp.zeros_like(dv_ref)
            dbt_ref[...] = jnp.zeros_like(dbt_ref)

        units = _na_units(pl.program_id(1), rows)
        qk, dp, dsb, pb = {}, {}, {}, {}

        def scores(u):
            _, hh, _, win, lanes, _ = units[u]
            qk[u] = _dot(k_ref[win, :], _head_rows(qt_ref[:, lanes], hh))

        def d_probs(u):
            _, hh, _, win, lanes, _ = units[u]
            dp[u] = _dot(v_ref[win, :], _head_rows(dot_ref[:, lanes].astype(BF16), hh))

        def softmax_bwd(u):
            _, hh, case, _, lanes, hrows = units[u]
            exps, l = _na_softmax(qk.pop(u), bt_ref, hh, case)
            inv_l = 1.0 / l
            delta = jnp.sum(dot_ref[hrows, lanes] * ot_ref[hrows, lanes], axis=0, keepdims=True)
            d_p = dp.pop(u)
            ds_blocks, p_blocks = [], []
            for blk, e in zip(_na_blocks(), exps):
                p = e * inv_l
                ds = p * (d_p[blk, :] - delta)
                dbt_ref[hh, case, blk, :] += ds
                ds_blocks.append(ds.astype(BF16))
                p_blocks.append(p.astype(BF16))
            dsb[u] = jnp.concatenate(ds_blocks, axis=0)
            pb[u] = jnp.concatenate(p_blocks, axis=0)

        def d_query(u):
            _, _, _, win, lanes, hrows = units[u]
            dq_ref[hrows, lanes] = _dot(kt_ref[hrows, win], dsb[u]) * (NA_HEAD_DIM ** -0.5)

        def d_keys_values(u):
            pp, hh, _, win, _, _ = units[u]
            if hh == 1:
                tokens = slice(pp * NA_PAIR, (pp + 1) * NA_PAIR)
                dk_ref[win, :] += _dot(jnp.concatenate([dsb.pop(u - 1), dsb.pop(u)], axis=1), _heads_block_diag(q_ref[tokens, :]))
                dv_ref[win, :] += _dot(jnp.concatenate([pb.pop(u - 1), pb.pop(u)], axis=1), _heads_block_diag(do_ref[tokens, :]))

        _na_pipeline(len(units), [scores, d_probs], softmax_bwd, [d_query, d_keys_values], lookahead=2)

    t_tile = pl.BlockSpec((NA_PAIR, step_w), lambda h, s: (h, s))
    tile = pl.BlockSpec((step_w, NA_PAIR), lambda h, s: (s, h))
    t_full = pl.BlockSpec((NA_PAIR, L), lambda h, s: (h, 0))
    full = pl.BlockSpec((L, NA_PAIR), lambda h, s: (0, h))
    bt = pl.BlockSpec((2, NA_CASES, NA_WIN, NA_PAIR), lambda h, s: (h, 0, 0, 0))
    tok = jax.ShapeDtypeStruct((L, D_NA), F32)
    return pl.pallas_call(
        body, name="na_bwd", grid=(NA_HEADS // 2, L // step_w),
        in_specs=[t_tile, tile, t_full, full, full, bt, t_tile, t_tile, tile],
        out_specs=[t_tile, full, full, bt],
        out_shape=[jax.ShapeDtypeStruct((D_NA, L), F32), tok, tok, jax.ShapeDtypeStruct(bias_tab.shape, F32)],
        compiler_params=_cparams(("arbitrary", "arbitrary")),
    )(q_t, q, k_t, k, v, bias_tab, out_t, d_out_t, d_out)


def _branch_fwd_values(ys, zs, yn, zn, wglu, bglu):
    g1, t = _gelu_parts(ys)
    lin = _dot(g1.astype(BF16), wglu) + bglu
    sg = _sigmoid(lin)
    ys2 = g1 * sg
    sz, szs = _silu_parts(zs)
    sn, sns = _silu_parts(zn)
    return g1, t, sg, ys2, sz, szs, sn, sns


def _branch_fwd(y_ssm_c, z_s, y_na_t, z_n, w_glu, b_glu, tm=512):
    L = z_s.shape[0]

    def body(ys_ref, zs_ref, yn_ref, zn_ref, w_ref, b_ref, cat_ref, scr):
        yn = yn_ref[...].T
        g1, t, sg, ys2, sz, szs, sn, sns = _branch_fwd_values(
            _load_chunks(ys_ref, scr), zs_ref[...], yn, zn_ref[...], w_ref[...], b_ref[...])
        cat_ref[:, 0:512] = (ys2 * sz).astype(BF16)
        cat_ref[:, 512:1024] = (yn * sn).astype(BF16)

    tile = pl.BlockSpec((tm, 512), lambda i: (i, 0))
    return pl.pallas_call(
        body, name="branch_fwd", grid=(L // tm,),
        in_specs=[_chunk_spec(tm), tile, _heads_t_spec(tm), tile, pl.BlockSpec((512, 512), lambda i: (0, 0)),
                  pl.BlockSpec((1, 512), lambda i: (0, 0))],
        out_specs=pl.BlockSpec((tm, 1024), lambda i: (i, 0)),
        out_shape=jax.ShapeDtypeStruct((L, 1024), BF16),
        scratch_shapes=[_chunk_scratch(tm)],
        compiler_params=_cparams(("arbitrary",)),
    )(y_ssm_c, z_s, y_na_t, z_n, w_glu, b_glu)


def _branch_bwd(y_ssm_c, z_s, y_na_t, z_n, w_glu, b_glu, d_cat, tm=512):
    L = z_s.shape[0]

    def body(ys_ref, zs_ref, yn_ref, zn_ref, w_ref, b_ref, dc_ref,
             dys_ref, dzs_ref, dynt_ref, dyn_ref, dzn_ref, dw_ref, db_ref, scr):
        @pl.when(pl.program_id(0) == 0)
        def _():
            dw_ref[...] = jnp.zeros_like(dw_ref)
            db_ref[...] = jnp.zeros_like(db_ref)

        ys, zs, yn, zn = _load_chunks(ys_ref, scr), zs_ref[...], yn_ref[...].T, zn_ref[...]
        w = w_ref[...]
        g1, t, sg, ys2, sz, szs, sn, sns = _branch_fwd_values(ys, zs, yn, zn, w, b_ref[...])
        dys3 = dc_ref[:, 0:512]
        dyn2 = dc_ref[:, 512:1024]
        dzs_ref[...] = (dys3 * ys2 * _silu_grad(zs, szs)).astype(BF16)
        dys2 = dys3 * sz
        dlin = dys2 * g1 * sg * (1.0 - sg)
        dlb = dlin.astype(BF16)
        dg1 = dys2 * sg + _dot_nt(dlb, w)
        dw_ref[...] += _dot_tn(g1.astype(BF16), dlb)
        db_ref[...] += jnp.sum(dlin, axis=0, keepdims=True)
        _store_chunks(dg1 * _gelu_grad(ys, t), scr, dys_ref, BF16)
        dyn = dyn2 * sn
        dynt_ref[...] = dyn.T
        dyn_ref[...] = dyn.astype(BF16)
        dzn_ref[...] = (dyn2 * yn * _silu_grad(zn, sns)).astype(BF16)

    tile = pl.BlockSpec((tm, 512), lambda i: (i, 0))
    wspec = pl.BlockSpec((512, 512), lambda i: (0, 0))
    bspec = pl.BlockSpec((1, 512), lambda i: (0, 0))
    tok = jax.ShapeDtypeStruct((L, 512), BF16)
    return pl.pallas_call(
        body, name="branch_bwd", grid=(L // tm,),
        in_specs=[_chunk_spec(tm), tile, _heads_t_spec(tm), tile, wspec, bspec, pl.BlockSpec((tm, 1024), lambda i: (i, 0))],
        out_specs=[_chunk_spec(tm), tile, _heads_t_spec(tm), tile, tile, wspec, bspec],
        out_shape=[jax.ShapeDtypeStruct((N_BLOCKS, L // CHUNK, CHUNK_W), BF16), tok, jax.ShapeDtypeStruct((D_NA, L), F32),
                   tok, tok,
                   jax.ShapeDtypeStruct((512, 512), F32), jax.ShapeDtypeStruct((1, 512), F32)],
        scratch_shapes=[_chunk_scratch(tm)],
        compiler_params=_cparams(("arbitrary",)),
    )(y_ssm_c, z_s, y_na_t, z_n, w_glu, b_glu, d_cat)


def _head(x, p, target, cat, w_out, g_post, w_ple_g, g_ple, w_pg, tm=512):
    L = x.shape[0]
    pw = w_ple_g.shape[2]

    def body(x_ref, p_ref, t_ref, cat_ref, wo_ref, gpo_ref, wp_ref, gpl_ref, wg_ref,
             loss_ref, dh1_ref, dcat_ref, dwo_ref, dgpo_ref, dwp_ref, dgpl_ref, dwg_ref):
        @pl.when(pl.program_id(0) == 0)
        def _():
            for r in (loss_ref, dwo_ref, dgpo_ref, dwp_ref, dgpl_ref, dwg_ref):
                r[...] = jnp.zeros_like(r)

        cat_b = cat_ref[...]
        wo, wg = wo_ref[...], wg_ref[...]
        g_po, g_pl = gpo_ref[...], gpl_ref[...]
        mix = _dot(cat_b, wo)
        p_b = p_ref[...].astype(BF16)
        ep = jnp.concatenate([_dot(p_b, wp_ref[j]) for j in range(N_CHIPS)], axis=1)
        nm, r2 = _rms(mix)
        h1 = x_ref[...] + nm * g_po
        ne, r3 = _rms(ep)
        e = ne * g_pl
        h1_b = h1.astype(BF16)
        gate = _sigmoid(_dot(h1_b, wg))
        h2 = h1 + gate * e
        diff = h2 - t_ref[...]
        loss_ref[...] += (0.5 / D_MODEL) * jnp.sum(diff * diff).reshape(1, 1)

        dh2 = diff * (1.0 / D_MODEL)
        de = dh2 * gate
        dgl = (dh2 * e * gate * (1.0 - gate)).astype(BF16)
        dh1 = dh2 + _dot_nt(dgl, wg)
        dwg_ref[...] += _dot_tn(h1_b, dgl)
        dgpo_ref[...] += jnp.sum(dh1 * nm, axis=0, keepdims=True)
        dmix = _rms_bwd(dh1 * g_po, nm, r2).astype(BF16)
        dcat_ref[...] = _dot_nt(dmix, wo)
        dwo_ref[...] += _dot_tn(cat_b, dmix)
        dh1_ref[...] = dh1
        dgpl_ref[...] += jnp.sum(de * ne, axis=0, keepdims=True)
        dep = _rms_bwd(de * g_pl, ne, r3).astype(BF16)
        for j in range(N_CHIPS):
            dwp_ref[j] += _dot_tn(p_b, dep[:, j * pw:(j + 1) * pw])

    tile = lambda w: pl.BlockSpec((tm, w), lambda i: (i, 0))
    const = _resident
    sds = jax.ShapeDtypeStruct
    return pl.pallas_call(
        body, name="head", grid=(L // tm,),
        in_specs=[tile(D_MODEL), tile(D_PLE), tile(D_MODEL), tile(1024), const(1024, D_MODEL), const(1, D_MODEL),
                  const(N_CHIPS, D_PLE, pw), const(1, D_MODEL), const(D_MODEL, D_MODEL)],
        out_specs=[const(1, 1), tile(D_MODEL), tile(1024), const(1024, D_MODEL), const(1, D_MODEL),
                   const(N_CHIPS, D_PLE, pw), const(1, D_MODEL), const(D_MODEL, D_MODEL)],
        out_shape=[sds((1, 1), F32), sds((L, D_MODEL), F32), sds((L, 1024), F32), sds((1024, D_MODEL), F32),
                   sds((1, D_MODEL), F32), sds((N_CHIPS, D_PLE, pw), F32), sds((1, D_MODEL), F32),
                   sds((D_MODEL, D_MODEL), F32)],
        compiler_params=_cparams(("arbitrary",)),
    )(x, p, target, cat, w_out, g_post, w_ple_g, g_ple, w_pg)


def _dproj_specs(tm):
    tile = pl.BlockSpec((tm, 512), lambda i: (i, 0))
    return [_chunk_spec(tm), tile, _heads_t_spec(tm), tile, tile, tile]


_DPROJ_ORDER = (3, 4, 5, 1, 2, 0)


def _dproj_part(refs, scr, i):
    if i == 0:
        val = _load_chunks(refs[0], scr)
    elif i == 2:
        val = refs[2][...].T
    else:
        val = refs[i][...]
    return val.astype(BF16)


def _dproj_pieces(i, wn):
    lo, hi = 512 * i, 512 * (i + 1)
    pieces = []
    for j in range(N_CHIPS):
        a, b = max(lo, j * wn), min(hi, (j + 1) * wn)
        if a < b:
            pieces.append((j, slice(a - j * wn, b - j * wn), slice(a - lo, b - lo)))
    return pieces


def _in_proj_bwd_w(x, g_col, w_in_g, dparts, tm=512):
    L = x.shape[0]
    wn = D_IN_PROJ // N_CHIPS
    steps = L // tm

    def body(x_ref, g_ref, w_ref, *refs):
        dw_ref, dg_ref, scr = refs[-3], refs[-2], refs[-1]

        @pl.when(pl.program_id(0) == 0)
        def _():
            dw_ref[...] = jnp.zeros_like(dw_ref)

        n, _ = _rms(x_ref[...])
        nb = n.astype(BF16)
        for i in _DPROJ_ORDER:
            part = _dproj_part(refs[:-3], scr, i)
            for j, w_cols, p_cols in _dproj_pieces(i, wn):
                dw_ref[j, :, w_cols] += _dot_tn(nb, part[:, p_cols])

        @pl.when(pl.program_id(0) == steps - 1)
        def _():
            g = g_ref[...]
            dg = jnp.zeros_like(g)
            for j in range(N_CHIPS):
                a = dw_ref[j]
                dg = dg + jnp.sum(a * w_ref[j].astype(F32), axis=1, keepdims=True)
                dw_ref[j] = a * g
            dg_ref[...] = dg

    return pl.pallas_call(
        body, name="in_proj_bwd_w", grid=(steps,),
        in_specs=[pl.BlockSpec((tm, D_MODEL), lambda i: (i, 0)), _resident(D_MODEL, 1), _resident(N_CHIPS, D_MODEL, wn)]
        + _dproj_specs(tm),
        out_specs=[_resident(N_CHIPS, D_MODEL, wn), _resident(D_MODEL, 1)],
        out_shape=[jax.ShapeDtypeStruct((N_CHIPS, D_MODEL, wn), F32), jax.ShapeDtypeStruct((D_MODEL, 1), F32)],
        scratch_shapes=[_chunk_scratch(tm)],
        compiler_params=_cparams(("arbitrary",)),
    )(x, g_col, w_in_g, *dparts)


def _in_proj_bwd_x(x, g_pre, w_in_g, d_h1, dparts, pair_sums, tm=512):
    L = x.shape[0]
    wn = w_in_g.shape[2]
    n_ps = len(pair_sums)
    steps = L // tm

    def body(*refs):
        x_ref, g_ref, w_ref, dh1_ref = refs[:4]
        dparts_refs = refs[4:10]
        dx_ref = refs[10 + n_ps]
        scr = refs[11 + 2 * n_ps]
        scatter = _ChipScatter(refs[10:10 + n_ps], refs[11 + n_ps:11 + 2 * n_ps], refs[12 + 2 * n_ps:16 + 2 * n_ps],
                               refs[16 + 2 * n_ps:])
        pl.when(pl.program_id(0) == 0)(scatter.start)
        pl.when(pl.program_id(0) == steps - 1)(scatter.finish)

        dhn = None
        for i in _DPROJ_ORDER:
            part = _dproj_part(dparts_refs, scr, i)
            for j, w_cols, p_cols in _dproj_pieces(i, wn):
                term = _dot_nt(part[:, p_cols], w_ref[j, :, w_cols])
                dhn = term if dhn is None else dhn + term
        n, r = _rms(x_ref[...])
        dx_ref[...] = dh1_ref[...] + _rms_bwd(dhn * g_ref[...], n, r)

    wide = pl.BlockSpec((tm, D_MODEL), lambda i: (i, 0))
    outs = pl.pallas_call(
        body, name="in_proj_bwd_x", grid=(steps,),
        in_specs=[wide, _resident(1, D_MODEL), _resident(N_CHIPS, D_MODEL, wn), wide] + _dproj_specs(tm) + _hbm_specs(n_ps),
        out_specs=[wide] + _hbm_specs(n_ps),
        out_shape=[jax.ShapeDtypeStruct((L, D_MODEL), F32)] + [jax.ShapeDtypeStruct(p.shape, p.dtype) for p in pair_sums],
        scratch_shapes=[_chunk_scratch(tm)] + _scatter_scratch(pair_sums),
        compiler_params=_cparams(("arbitrary",), has_side_effects=True),
    )(x, g_pre, w_in_g, d_h1, *dparts, *pair_sums)
    return outs[0], outs[1:]


def _mesh_position():
    x, y, c = lax.axis_index("x"), lax.axis_index("y"), lax.axis_index("c")
    chips = [(1 - x, y), (x, 1 - y), (1 - x, 1 - y)]
    return x, y, c, chips


def _chip_index(cx, cy):
    return 2 * cx + cy


def _hbm_specs(n):
    return [pl.BlockSpec(memory_space=pl.ANY)] * n


def _gather_chips(shards, name):
    n = len(shards)

    def body(*refs):
        gather = _ChipGather(refs[:n], refs[n:2 * n], refs[2 * n:])
        gather.start()
        gather.forward()
        gather.finish()

    return pl.pallas_call(
        body, name=name, in_specs=_hbm_specs(n), out_specs=_hbm_specs(n),
        out_shape=_gather_out_shapes(shards), scratch_shapes=_gather_semaphores(n),
        compiler_params=pltpu.CompilerParams(has_side_effects=True),
    )(*shards)


def _gather_out_shapes(shards):
    return [jax.ShapeDtypeStruct((N_CHIPS,) + s.shape, s.dtype) for s in shards]


def _gather_semaphores(n):
    sem = pltpu.SemaphoreType.DMA
    return [sem((n, 3)), sem((n, 3)), sem((n, 3)), sem((n, 3)), sem((n,)), sem((n,))]


class _ChipGather:
    def __init__(self, ins, outs, sems):
        self.ins, self.outs = ins, outs
        self.send1, self.recv1, self.send2, self.recv2, self.send3, self.recv3 = sems
        self.x, self.y, self.c, self.chips = _mesh_position()
        self.me = _chip_index(self.x, self.y)
        self.sibling = (self.x, self.y, 1 - self.c)

    def _half(self, a, chip, core):
        hr = self.outs[a].shape[1] // 2
        return self.outs[a].at[chip, pl.ds(core * hr, hr)]

    def _own(self, a):
        return pltpu.make_async_remote_copy(
            src_ref=self.ins[a], dst_ref=self.outs[a].at[self.me], send_sem=self.send3.at[a], recv_sem=self.recv3.at[a],
            device_id=self.sibling, device_id_type=MESH)

    def _to_chip(self, a, j):
        hr = self.ins[a].shape[0] // 2
        return pltpu.make_async_remote_copy(
            src_ref=self.ins[a].at[pl.ds(self.c * hr, hr)], dst_ref=self._half(a, self.me, self.c),
            send_sem=self.send1.at[a, j], recv_sem=self.recv1.at[a, j], device_id=(*self.chips[j], self.c), device_id_type=MESH)

    def _from_chip(self, a, j):
        landed = self._half(a, _chip_index(*self.chips[j]), self.c)
        return pltpu.make_async_remote_copy(
            src_ref=landed, dst_ref=landed, send_sem=self.send1.at[a, j], recv_sem=self.recv1.at[a, j],
            device_id=(*self.chips[j], self.c), device_id_type=MESH)

    def _to_sibling(self, a, j, core):
        part = self._half(a, _chip_index(*self.chips[j]), core)
        return pltpu.make_async_remote_copy(
            src_ref=part, dst_ref=part, send_sem=self.send2.at[a, j], recv_sem=self.recv2.at[a, j],
            device_id=self.sibling, device_id_type=MESH)

    def _each(self):
        return [(a, j) for a in range(len(self.ins)) for j in range(3)]

    def start(self):
        for a in range(len(self.ins)):
            self._own(a).start()
        for a, j in self._each():
            self._to_chip(a, j).start()

    def forward(self):
        for a, j in self._each():
            self._from_chip(a, j).wait_recv()
            self._to_sibling(a, j, self.c).start()

    def finish(self):
        for a, j in self._each():
            self._to_sibling(a, j, 1 - self.c).wait_recv()
        for a, j in self._each():
            self._to_chip(a, j).wait_send()
            self._to_sibling(a, j, self.c).wait_send()
        for a in range(len(self.ins)):
            self._own(a).wait()


def _pair_exchange(grads):
    n = len(grads)

    def body(*refs):
        ins, outs = refs[:n], refs[n:2 * n]
        send, recv = refs[2 * n:]
        x, y, c, _ = _mesh_position()
        copies = []
        for a in range(n):
            hr = ins[a].shape[1] // 2
            cp = pltpu.make_async_remote_copy(
                src_ref=ins[a].at[:, pl.ds((1 - c) * hr, hr)], dst_ref=outs[a],
                send_sem=send.at[a], recv_sem=recv.at[a], device_id=(x, y, 1 - c), device_id_type=MESH)
            cp.start()
            copies.append(cp)
        for cp in copies:
            cp.wait()

    sem = pltpu.SemaphoreType.DMA
    return pl.pallas_call(
        body, name="pair_exchange", in_specs=_hbm_specs(n), out_specs=_hbm_specs(n),
        out_shape=[jax.ShapeDtypeStruct((g.shape[0], g.shape[1] // 2, g.shape[2]), g.dtype) for g in grads],
        scratch_shapes=[sem((n,)), sem((n,))],
        compiler_params=pltpu.CompilerParams(has_side_effects=True),
    )(*grads)


def _pair_add(core, grad, other, tr, out_dtype):
    hr = other.shape[1]
    cdim = other.shape[2]
    nb = hr // tr

    def body(core_ref, g_ref, o_ref, out_ref):
        out_ref[...] = (g_ref[...] + o_ref[...]).astype(out_dtype)

    return pl.pallas_call(
        body, name="pair_add",
        grid_spec=pltpu.PrefetchScalarGridSpec(
            num_scalar_prefetch=1, grid=(N_CHIPS, nb),
            in_specs=[pl.BlockSpec((1, tr, cdim), lambda j, i, core_ref: (j, core_ref[0] * nb + i, 0)),
                      pl.BlockSpec((1, tr, cdim), lambda j, i, core_ref: (j, i, 0))],
            out_specs=pl.BlockSpec((1, tr, cdim), lambda j, i, core_ref: (j, i, 0))),
        out_shape=jax.ShapeDtypeStruct(other.shape, out_dtype),
        compiler_params=_cparams(("arbitrary", "arbitrary")),
    )(core, grad, other)


def _scatter_scratch(parts):
    sem = pltpu.SemaphoreType.DMA
    n = len(parts)
    return [sem((n, 3)), sem((n, 3)), sem((n,)), sem((n,))] + [pltpu.VMEM(p.shape[1:], p.dtype) for p in parts]


class _ChipScatter:
    def __init__(self, ins, outs, sems, staged):
        self.ins, self.outs, self.staged = ins, outs, staged
        self.send, self.recv, self.load_sem, self.store_sem = sems
        self.x, self.y, self.c, self.chips = _mesh_position()
        self.me = _chip_index(self.x, self.y)

    def _load(self, a):
        return pltpu.make_async_copy(self.ins[a].at[self.me], self.staged[a], self.load_sem.at[a])

    def _store(self, a):
        return pltpu.make_async_copy(self.staged[a], self.outs[a].at[self.me], self.store_sem.at[a])

    def _to_chip(self, a, j):
        return pltpu.make_async_remote_copy(
            src_ref=self.ins[a].at[_chip_index(*self.chips[j])], dst_ref=self.outs[a].at[self.me],
            send_sem=self.send.at[a, j], recv_sem=self.recv.at[a, j], device_id=(*self.chips[j], self.c), device_id_type=MESH)

    def start(self):
        for a in range(len(self.ins)):
            self._load(a).start()
            for j in range(3):
                self._to_chip(a, j).start()

    def finish(self):
        for a in range(len(self.ins)):
            self._load(a).wait()
            self._store(a).start()
        for a in range(len(self.ins)):
            for j in range(3):
                self._to_chip(a, j).wait()
            self._store(a).wait()


def _chip_add(core, recv, tr):
    hr, cdim = recv.shape[1], recv.shape[2]
    nb = hr // tr

    def body(core_ref, r_ref, out_ref):
        out_ref[...] = ((r_ref[0].astype(F32) + r_ref[1].astype(F32)) + r_ref[2].astype(F32)) + r_ref[3].astype(F32)

    return pl.pallas_call(
        body, name="chip_add",
        grid_spec=pltpu.PrefetchScalarGridSpec(
            num_scalar_prefetch=1, grid=(nb,),
            in_specs=[pl.BlockSpec((N_CHIPS, tr, cdim), lambda i, core_ref: (0, i, 0))],
            out_specs=pl.BlockSpec((tr, cdim), lambda i, core_ref: (core_ref[0] * nb + i, 0))),
        out_shape=jax.ShapeDtypeStruct((2 * hr, cdim), F32),
        compiler_params=_cparams(("arbitrary",)),
    )(core, recv)


def _pair_gather(fulls):
    n = len(fulls)

    def body(*refs):
        outs = refs[n:2 * n]
        send, recv = refs[2 * n:]
        x, y, c, _ = _mesh_position()
        copies = []
        for a in range(n):
            hr = outs[a].shape[0] // 2
            mine = outs[a].at[pl.ds(c * hr, hr)]
            cp = pltpu.make_async_remote_copy(
                src_ref=mine, dst_ref=mine, send_sem=send.at[a], recv_sem=recv.at[a],
                device_id=(x, y, 1 - c), device_id_type=MESH)
            cp.start()
            copies.append(cp)
        for cp in copies:
            cp.wait()

    sem = pltpu.SemaphoreType.DMA
    return pl.pallas_call(
        body, name="pair_gather", in_specs=_hbm_specs(n), out_specs=_hbm_specs(n),
        out_shape=[jax.ShapeDtypeStruct(f.shape, f.dtype) for f in fulls],
        input_output_aliases={a: a for a in range(n)},
        scratch_shapes=[sem((n,)), sem((n,))],
        compiler_params=pltpu.CompilerParams(has_side_effects=True),
    )(*fulls)


def _row_tile(rows):
    if rows <= 512:
        return rows
    for t in (512, 256, 128, 64, 32, 16, 8):
        if rows % t == 0:
            return t
    raise ValueError(rows)


def _pair_sums(core, grads, ici_dtypes):
    others = _pair_exchange(grads)
    return [_pair_add(core, g, o, _row_tile(o.shape[1]), dt) for g, o, dt in zip(grads, others, ici_dtypes)]


def _finish_reduce(core, landed):
    return _pair_gather([_chip_add(core, r, _row_tile(r.shape[1])) for r in landed])


def _adamw(w, g, m, v):
    rows, cols = w.shape
    tr = rows
    if rows % 8 == 0 and rows * max(cols, 128) * 4 > (1 << 17):
        tr = next(t for t in (512, 256, 128, 64, 32, 16, 8) if rows % t == 0 and rows // t >= 4)

    def body(w_ref, g_ref, m_ref, v_ref, d_ref, nm_ref, nv_ref):
        g_ = g_ref[...]
        m_ = ADAM_B1 * m_ref[...] + (1.0 - ADAM_B1) * g_
        v_ = ADAM_B2 * v_ref[...] + (1.0 - ADAM_B2) * (g_ * g_)
        m_hat = m_ / (1.0 - ADAM_B1 ** ADAM_STEP)
        v_hat = v_ / (1.0 - ADAM_B2 ** ADAM_STEP)
        d_ref[...] = -ADAM_LR * (m_hat / (jnp.sqrt(v_hat) + ADAM_EPS) + ADAM_WD * w_ref[...])
        nm_ref[...] = m_
        nv_ref[...] = v_

    spec = pl.BlockSpec((tr, cols), lambda i: (i, 0))
    shp = jax.ShapeDtypeStruct((rows, cols), F32)
    return pl.pallas_call(
        body, name="adamw", grid=(rows // tr,), in_specs=[spec] * 4, out_specs=[spec] * 3,
        out_shape=[shp] * 3, compiler_params=_cparams(("arbitrary",)),
    )(w, g, m, v)


_SMALL = ["norm_pre", "norm_post", "ssm_a_re", "ssm_a_im", "ssm_log_dt", "ssm_b_re", "ssm_b_im",
          "ssm_c_re", "ssm_c_im", "ssm_d", "b_glu", "na_rpb", "ple_norm"]
_BIG = ["w_in", "w_glu", "w_out", "w_ple", "w_ple_gate"]
_WEIGHTS = ["norm_pre", "norm_post", "w_in", "ssm_a_re", "ssm_a_im", "ssm_log_dt", "ssm_b_re", "ssm_b_im",
            "ssm_c_re", "ssm_c_im", "ssm_d", "w_glu", "b_glu", "na_rpb", "w_out", "w_ple", "ple_norm", "w_ple_gate"]
_SMALL_ROWS = 2176


def _pack_small(tensors, tail=None):
    parts = [tensors[n].reshape(-1) for n in _SMALL] + ([] if tail is None else [tail.reshape(-1)])
    flat = jnp.concatenate(parts)
    flat = jnp.pad(flat, (0, _SMALL_ROWS * 128 - flat.shape[0]))
    return flat.reshape(_SMALL_ROWS, 128)


def _unpack_small(packed, shapes):
    flat = packed.reshape(-1)
    out, off = {}, 0
    for n in _SMALL:
        size = int(np.prod(shapes[n]))
        out[n] = flat[off:off + size].reshape(shapes[n])
        off += size
    return out


def _local_grads(x, p, target, wts):
    ssm_names = ["ssm_a_re", "ssm_a_im", "ssm_log_dt", "ssm_b_re", "ssm_b_im", "ssm_c_re", "ssm_c_im", "ssm_d"]
    ssm_params = [wts[n][0] for n in ssm_names]
    blk, blk_vjp = jax.vjp(_ssm_block_params, *ssm_params)
    shard = lambda n: wts[n][0].astype(BF16)
    (m_mat, ws_mat, wot_mat, a16), (w_in_g,) = _ssm_chunk_matrices(blk, [shard("w_in")])
    seq = x.shape[0]
    bias_rows, bias_rows_vjp = jax.vjp(_na_bias_rows, wts["na_rpb"][0])
    bias_tab = _na_bias_table(bias_rows, seq // GRID_W)

    (u_c, z_s, q_t, q, k_t, k, v_t, v, z_n), gathered = _in_proj(
        x, wts["norm_pre"], w_in_g, [shard(n) for n in _BIG if n != "w_in"])
    w_glu, w_out, w_ple_g, w_pg = (gathered[0].reshape(512, 512), gathered[1].reshape(1024, 1024), gathered[2],
                                   gathered[3].reshape(1024, 1024))
    s_in = _block_matmul([(u_c, ws_mat, False)], "ssm_chunk_states")
    s_prev = _ssm_state_scan(s_in, a16)
    y_ssm_c = _block_matmul([(u_c, m_mat, False), (s_prev, wot_mat, True)], "ssm_chunk_out")
    y_na_t = _na_fwd(q_t, k, v_t, bias_tab)
    cat = _branch_fwd(y_ssm_c, z_s, y_na_t, z_n, w_glu, wts["b_glu"])

    (loss, d_h1, d_cat, d_w_out, d_g_post, d_w_ple, d_g_ple, d_w_pg) = _head(
        x, p, target, cat, w_out, wts["norm_post"], w_ple_g, wts["ple_norm"], w_pg)
    dy_c, d_z_s, d_y_na_t, d_y_na, d_z_n, d_w_glu, d_b_glu = _branch_bwd(
        y_ssm_c, z_s, y_na_t, z_n, w_glu, wts["b_glu"], d_cat)
    d_q_t, d_k, d_v, d_bias_tab = _na_bwd(q_t, q, k_t, k, v, bias_tab, y_na_t, d_y_na_t, d_y_na)

    d_prev = _block_matmul([(dy_c, wot_mat, False)], "ssm_bwd_states")
    g_st, d_a16 = _ssm_state_scan_bwd(d_prev, s_prev, a16)
    d_u_c = _block_matmul([(dy_c, m_mat, True), (g_st, ws_mat, True)], "ssm_bwd_in", out_dtype=BF16)
    d_m = _block_matmul_tn(u_c, dy_c, "ssm_grad_m")
    d_ws = _block_matmul_tn(u_c, g_st, "ssm_grad_ws")
    d_wot = _block_matmul_tn(dy_c, s_prev, "ssm_grad_wot")
    d_ssm = blk_vjp(tuple(_ssm_chunk_matrices_bwd(blk, d_m, d_ws, d_wot, d_a16)))
    (d_rpb,) = bias_rows_vjp(_na_bias_table_bwd(d_bias_tab, seq // GRID_W))

    dparts = [d_u_c, d_z_s, d_q_t, d_k, d_v, d_z_n]
    d_w_in, d_g_pre = _in_proj_bwd_w(x, wts["norm_pre"].reshape(D_MODEL, 1), w_in_g, dparts)

    small = {"norm_pre": d_g_pre, "norm_post": d_g_post, "b_glu": d_b_glu, "na_rpb": d_rpb, "ple_norm": d_g_ple}
    for n, g in zip(ssm_names, d_ssm):
        small[n] = g
    big = {"w_in": d_w_in, "w_glu": d_w_glu.reshape(N_CHIPS, 128, 512), "w_out": d_w_out.reshape(N_CHIPS, 256, 1024),
           "w_ple": d_w_ple, "w_ple_gate": d_w_pg.reshape(N_CHIPS, 256, 1024)}
    return loss, small, big, (x, wts["norm_pre"], w_in_g, d_h1, dparts)


def kernel(x, p, norm_pre, norm_post, w_in, ssm_a_re, ssm_a_im, ssm_log_dt, ssm_b_re, ssm_b_im, ssm_c_re, ssm_c_im, ssm_d, w_glu, b_glu, na_rpb, w_out, w_ple, ple_norm, w_ple_gate, loss_target, m_norm_pre, m_norm_post, m_w_in, m_ssm_a_re, m_ssm_a_im, m_ssm_log_dt, m_ssm_b_re, m_ssm_b_im, m_ssm_c_re, m_ssm_c_im, m_ssm_d, m_w_glu, m_b_glu, m_na_rpb, m_w_out, m_w_ple, m_ple_norm, m_w_ple_gate, v_norm_pre, v_norm_post, v_w_in, v_ssm_a_re, v_ssm_a_im, v_ssm_log_dt, v_ssm_b_re, v_ssm_b_im, v_ssm_c_re, v_ssm_c_im, v_ssm_d, v_w_glu, v_b_glu, v_na_rpb, v_w_out, v_w_ple, v_ple_norm, v_w_ple_gate):
    wts = dict(norm_pre=norm_pre, norm_post=norm_post, w_in=w_in, ssm_a_re=ssm_a_re, ssm_a_im=ssm_a_im,
               ssm_log_dt=ssm_log_dt, ssm_b_re=ssm_b_re, ssm_b_im=ssm_b_im, ssm_c_re=ssm_c_re, ssm_c_im=ssm_c_im,
               ssm_d=ssm_d, w_glu=w_glu, b_glu=b_glu, na_rpb=na_rpb, w_out=w_out, w_ple=w_ple, ple_norm=ple_norm,
               w_ple_gate=w_ple_gate)
    mom_m = dict(norm_pre=m_norm_pre, norm_post=m_norm_post, w_in=m_w_in, ssm_a_re=m_ssm_a_re, ssm_a_im=m_ssm_a_im,
                 ssm_log_dt=m_ssm_log_dt, ssm_b_re=m_ssm_b_re, ssm_b_im=m_ssm_b_im, ssm_c_re=m_ssm_c_re,
                 ssm_c_im=m_ssm_c_im, ssm_d=m_ssm_d, w_glu=m_w_glu, b_glu=m_b_glu, na_rpb=m_na_rpb, w_out=m_w_out,
                 w_ple=m_w_ple, ple_norm=m_ple_norm, w_ple_gate=m_w_ple_gate)
    mom_v = dict(norm_pre=v_norm_pre, norm_post=v_norm_post, w_in=v_w_in, ssm_a_re=v_ssm_a_re, ssm_a_im=v_ssm_a_im,
                 ssm_log_dt=v_ssm_log_dt, ssm_b_re=v_ssm_b_re, ssm_b_im=v_ssm_b_im, ssm_c_re=v_ssm_c_re,
                 ssm_c_im=v_ssm_c_im, ssm_d=v_ssm_d, w_glu=v_w_glu, b_glu=v_b_glu, na_rpb=v_na_rpb, w_out=v_w_out,
                 w_ple=v_w_ple, ple_norm=v_ple_norm, w_ple_gate=v_w_ple_gate)

    loss_part, small, big, input_grad_args = _local_grads(x[0], p[0, 0], loss_target[0], wts)

    core = lax.axis_index("c").astype(jnp.int32).reshape(1)
    small_packed = _pack_small(small, tail=loss_part).reshape(N_CHIPS, _SMALL_ROWS // N_CHIPS, 128)
    pair = _pair_sums(core, [big[n] for n in _BIG] + [small_packed], [BF16] * len(_BIG) + [F32])
    grad_x, landed = _in_proj_bwd_x(*input_grad_args, pair)
    reduced = _finish_reduce(core, landed)
    grads = dict(zip(_BIG, reduced[:-1]))
    (small_all,) = _gather_chips([reduced[-1]], "gather_small_grads")
    small_all = small_all.reshape(_SMALL_ROWS, 128)
    loss = small_all.reshape(-1)[sum(int(np.prod(wts[n].shape)) for n in _SMALL)]

    delta, new_m, new_v = {}, {}, {}
    for n in _BIG:
        shp = wts[n].shape
        d_, m_, v_ = _adamw(wts[n][0], grads[n], mom_m[n][0], mom_v[n][0])
        grads[n] = grads[n].reshape(shp)
        delta[n], new_m[n], new_v[n] = d_.reshape(shp), m_.reshape(shp), v_.reshape(shp)
    grads.update(_unpack_small(small_all, {n: wts[n].shape for n in _SMALL}))
    for n in _SMALL:
        shp = wts[n].shape
        rows_cols = (int(np.prod(shp[:-1])), shp[-1])
        d_, m_, v_ = _adamw(*[t.reshape(rows_cols) for t in (wts[n], grads[n], mom_m[n], mom_v[n])])
        delta[n], new_m[n], new_v[n] = d_.reshape(shp), m_.reshape(shp), v_.reshape(shp)

    return (loss, grad_x[None], *[grads[n] for n in _WEIGHTS], *[delta[n] for n in _WEIGHTS],
            *[new_m[n] for n in _WEIGHTS], *[new_v[n] for n in _WEIGHTS])
```

```python
import math

import jax
import jax.numpy as jnp
import numpy as np
from jax import lax
from jax.experimental import pallas as pl
from jax.experimental.pallas import tpu as pltpu

F32 = jnp.float32
BF16 = jnp.bfloat16

D_MODEL = 1024
D_PLE = 256
GRID_W = 64
D_SSM = 512
SSM_GROUP = 16
N_GROUPS = 32
SSM_STATE = 64
D_NA = 512
NA_HEADS = 8
NA_HEAD_DIM = 64
NA_ROWS = 8
NA_COLS = 16
D_IN_PROJ = 3072
EPS = 1e-6

CHUNK = 16
GROUPS_PER_BLOCK = 8
N_BLOCKS = N_GROUPS // GROUPS_PER_BLOCK
BLOCK_CH = GROUPS_PER_BLOCK * SSM_GROUP
BLOCK_ST = GROUPS_PER_BLOCK * SSM_STATE
CHUNK_W = CHUNK * BLOCK_CH
STATE_W = 4 * BLOCK_ST

N_CHIPS = 4
MESH = pl.DeviceIdType.MESH

ADAM_LR = 0.001
ADAM_B1 = 0.9
ADAM_B2 = 0.999
ADAM_EPS = 1e-08
ADAM_WD = 0.01
ADAM_STEP = 10

VMEM_LIMIT = 52 * 1024 * 1024
HIGHEST = lax.Precision.HIGHEST


def _cparams(sem=None, **kw):
    if sem is not None:
        kw["dimension_semantics"] = sem
    return pltpu.CompilerParams(vmem_limit_bytes=VMEM_LIMIT, **kw)


def _resident(*shape):
    return pl.BlockSpec(shape, lambda *_: (0,) * len(shape), pipeline_mode=pl.Buffered(1))


def _dot(a, b, dims=((1,), (0,))):
    return lax.dot_general(a, b, (dims, ((), ())), preferred_element_type=F32)


def _dot_nt(a, b):
    return _dot(a, b, ((1,), (1,)))


def _dot_tn(a, b):
    return _dot(a, b, ((0,), (0,)))


def _sigmoid(x):
    return 1.0 / (1.0 + jnp.exp(-x))


_GELU_C = math.sqrt(2.0 / math.pi)


def _gelu_parts(x):
    inner = _GELU_C * (x + 0.044715 * (x * x * x))
    t = jnp.tanh(inner)
    return 0.5 * x * (1.0 + t), t


def _gelu_grad(x, t):
    return 0.5 * (1.0 + t) + 0.5 * x * (1.0 - t * t) * (_GELU_C * (1.0 + 3.0 * 0.044715 * x * x))


def _silu_parts(z):
    s = _sigmoid(z)
    return z * s, s


def _silu_grad(z, s):
    return s * (1.0 + z * (1.0 - s))


def _rms(x):
    r = lax.rsqrt(jnp.mean(x * x, axis=-1, keepdims=True) + EPS)
    return x * r, r


def _rms_bwd(dn, n, r):
    return r * (dn - n * jnp.mean(dn * n, axis=-1, keepdims=True))


def _chunk_scratch(tm):
    return pltpu.VMEM((N_BLOCKS, tm, BLOCK_CH), F32)


def _store_chunks(val, scr, c_ref, dtype, row0=0):
    rows = val.shape[0]
    nc, c0 = rows // CHUNK, row0 // CHUNK
    for b in range(N_BLOCKS):
        scr[b, row0:row0 + rows, :] = val[:, b * BLOCK_CH:(b + 1) * BLOCK_CH]
        for j in range(CHUNK):
            c_ref[b, c0:c0 + nc, j * BLOCK_CH:(j + 1) * BLOCK_CH] = scr[b, pl.ds(row0 + j, nc, stride=CHUNK), :].astype(dtype)


def _load_chunks(c_ref, scr):
    nc = scr.shape[1] // CHUNK
    for b in range(N_BLOCKS):
        for j in range(CHUNK):
            scr[b, pl.ds(j, nc, stride=CHUNK), :] = c_ref[b, :, j * BLOCK_CH:(j + 1) * BLOCK_CH].astype(F32)
    return jnp.concatenate([scr[b] for b in range(N_BLOCKS)], axis=1)


def _chunk_spec(tm):
    return pl.BlockSpec((N_BLOCKS, tm // CHUNK, CHUNK_W), lambda i: (0, i, 0))


def _heads_t_spec(tm):
    return pl.BlockSpec((D_NA, tm), lambda i: (0, i))


def _in_proj(x, g_pre, w_in_g, shards, tm=512):
    L = x.shape[0]
    wn = w_in_g.shape[2]
    n_sh = len(shards)
    steps = L // tm

    def body(*refs):
        x_ref, g_ref, w_ref = refs[:3]
        uc_ref, zs_ref, qt_ref, q_ref, kt_ref, k_ref, vt_ref, v_ref, zn_ref = refs[3 + n_sh:12 + n_sh]
        u_scr = refs[12 + 2 * n_sh]
        gather = _ChipGather(refs[3:3 + n_sh], refs[12 + n_sh:12 + 2 * n_sh], refs[13 + 2 * n_sh:])
        step = pl.program_id(0)
        pl.when(step == 0)(gather.start)
        pl.when(step == steps // 2)(gather.forward)
        pl.when(step == steps - 1)(gather.finish)
        halves = [slice(0, tm // 2), slice(tm // 2, tm)]
        hn = [(_rms(x_ref[rows, :])[0] * g_ref[...]).astype(BF16) for rows in halves]
        projs = [jnp.concatenate([_dot(h, w_ref[j]) for j in range(N_CHIPS)], axis=1) for h in hn]
        for rows, proj in zip(halves, projs):
            _store_chunks(proj[:, 0:512], u_scr, uc_ref, BF16, row0=rows.start)
            zs_ref[rows, :] = proj[:, 512:1024]
            q = proj[:, 1024:1536] * (NA_HEAD_DIM ** -0.5)
            for val, t_ref, n_ref in ((q, qt_ref, q_ref), (proj[:, 1536:2048], kt_ref, k_ref), (proj[:, 2048:2560], vt_ref, v_ref)):
                t_ref[:, rows] = val.T.astype(BF16)
                n_ref[rows, :] = val.astype(BF16)
            zn_ref[rows, :] = proj[:, 2560:3072]

    tok = jax.ShapeDtypeStruct((L, 512), F32)
    tr = jax.ShapeDtypeStruct((D_NA, L), BF16)
    hm = jax.ShapeDtypeStruct((L, D_NA), BF16)
    tspec = pl.BlockSpec((tm, 512), lambda i: (i, 0))
    outs = pl.pallas_call(
        body, name="in_proj", grid=(steps,),
        in_specs=[pl.BlockSpec((tm, D_MODEL), lambda i: (i, 0)),
                  _resident(1, D_MODEL), _resident(N_CHIPS, D_MODEL, wn)] + _hbm_specs(n_sh),
        out_specs=[_chunk_spec(tm), tspec] + [_heads_t_spec(tm), tspec] * 3 + [tspec] + _hbm_specs(n_sh),
        out_shape=[jax.ShapeDtypeStruct((N_BLOCKS, L // CHUNK, CHUNK_W), BF16), tok, tr, hm, tr, hm, tr, hm, tok]
        + _gather_out_shapes(shards),
        scratch_shapes=[_chunk_scratch(tm)] + _gather_semaphores(n_sh),
        compiler_params=_cparams(("arbitrary",), has_side_effects=True),
    )(x, g_pre, w_in_g, *shards)
    return outs[:9], outs[9:]


def _ssm_block_params(a_re, a_im, log_dt, b_re, b_im, c_re, c_im, d):
    def lanes(t):
        return t.reshape(2, N_BLOCKS, 1, BLOCK_ST)

    rows = (2, N_BLOCKS, BLOCK_CH, SSM_STATE)
    b_rows = lambda t: t.reshape(2, N_BLOCKS, GROUPS_PER_BLOCK, SSM_STATE, SSM_GROUP).transpose(0, 1, 2, 4, 3).reshape(rows)
    return (lanes(a_re), lanes(a_im), lanes(jnp.broadcast_to(log_dt[..., None], a_re.shape)),
            b_rows(b_re), b_rows(b_im), c_re.reshape(rows), c_im.reshape(rows), d.reshape(N_BLOCKS, 1, BLOCK_CH))


def _ssm_group_mask():
    row_g = lax.broadcasted_iota(jnp.int32, (BLOCK_CH, BLOCK_ST), 0) // SSM_GROUP
    lane_g = lax.broadcasted_iota(jnp.int32, (BLOCK_CH, BLOCK_ST), 1) // SSM_STATE
    return row_g == lane_g


def _ssm_state_select():
    p = lax.broadcasted_iota(jnp.int32, (SSM_STATE, BLOCK_ST), 0)
    lane_p = lax.broadcasted_iota(jnp.int32, (SSM_STATE, BLOCK_ST), 1) % SSM_STATE
    return (p == lane_p).astype(F32)


def _ssm_expand_blocks(compact_refs, full_refs):
    mask, select = _ssm_group_mask(), _ssm_state_select()
    for c_ref, f_ref in zip(compact_refs, full_refs):
        for d in range(2):
            tiled = lax.dot_general(c_ref[d, 0], select, ((((1,), (0,))), ((), ())), precision=HIGHEST,
                                    preferred_element_type=F32)
            f_ref[d, 0] = jnp.where(mask, tiled, 0.0)


def _ssm_collapse_block(t):
    return lax.dot_general(jnp.where(_ssm_group_mask(), t, 0.0), _ssm_state_select(), ((((1,), (1,))), ((), ())),
                           precision=HIGHEST, preferred_element_type=F32)


def _ssm_discretise(ar, ai, ldt):
    dt = jnp.exp(ldt)
    mag = jnp.exp(dt * ar)
    abr = mag * jnp.cos(dt * ai)
    abi = mag * jnp.sin(dt * ai)
    num_re = abr - 1.0
    num_im = abi
    denom = ar * ar + ai * ai
    coef_re = (num_re * ar + num_im * ai) / denom
    coef_im = (num_im * ar - num_re * ai) / denom
    return abr, abi, coef_re, coef_im


_POW_ROWS = 24


def _ssm_fill_powers(ar_ref, ai_ref, ldt_ref, br_ref, bi_ref, pw_ref, bbar_ref):
    for d in range(2):
        abr, abi, cfr, cfi = _ssm_discretise(ar_ref[d, 0], ai_ref[d, 0], ldt_ref[d, 0])
        bbar_ref[d, 0] = cfr * br_ref[d, 0] - cfi * bi_ref[d, 0]
        bbar_ref[d, 1] = cfr * bi_ref[d, 0] + cfi * br_ref[d, 0]
        pr, pi = jnp.ones_like(abr), jnp.zeros_like(abi)
        for t in range(CHUNK + 1):
            pw_ref[d, 0, t:t + 1, :] = pr
            pw_ref[d, 1, t:t + 1, :] = pi
            pr, pi = pr * abr - pi * abi, pr * abi + pi * abr


def _dot_rounded(a, b, dims=((1,), (0,))):
    return _dot(a.astype(BF16), b.astype(BF16), dims)


def _ssm_stack_inputs(d, pw_ref, bbar_ref, xs_ref):
    for t in range(CHUNK):
        pr, pi = pw_ref[d, 0, t:t + 1, :], pw_ref[d, 1, t:t + 1, :]
        xs_ref[0, t * BLOCK_CH:(t + 1) * BLOCK_CH, :] = bbar_ref[d, 0] * pr - bbar_ref[d, 1] * pi
        xs_ref[1, t * BLOCK_CH:(t + 1) * BLOCK_CH, :] = bbar_ref[d, 0] * pi + bbar_ref[d, 1] * pr


def _eye(n):
    return (lax.broadcasted_iota(jnp.int32, (n, n), 0) == lax.broadcasted_iota(jnp.int32, (n, n), 1)).astype(F32)


def _ssm_param_specs():
    vec = pl.BlockSpec((2, 1, 1, BLOCK_ST), lambda b, j: (0, b, 0, 0))
    mat = pl.BlockSpec((2, 1, BLOCK_CH, SSM_STATE), lambda b, j: (0, b, 0, 0))
    return [vec, vec, vec, mat, mat, mat, mat, pl.BlockSpec((1, 1, BLOCK_CH), lambda b, j: (b, 0, 0))]


def _ssm_block_scratch():
    return [pltpu.VMEM((2, 1, BLOCK_CH, BLOCK_ST), F32)] * 4


def _ssm_chunk_matrices(blk, shards):
    n = len(shards)

    def body(*refs):
        ar_ref, ai_ref, ldt_ref = refs[:3]
        d_ref = refs[7]
        m_ref, ws_ref, wot_ref, a16_ref = refs[8 + n:12 + n]
        pw_ref, bbar_ref, lag_ref, xs_ref = refs[12 + 2 * n:16 + 2 * n]
        br_ref, bi_ref, cr_ref, ci_ref = refs[16 + 2 * n:20 + 2 * n]
        gather = _ChipGather(refs[8:8 + n], refs[12 + n:12 + 2 * n], refs[20 + 2 * n:])
        b, j = pl.program_id(0), pl.program_id(1)
        pl.when((b == 0) & (j == 0))(gather.start)
        pl.when((b == N_BLOCKS - 1) & (j == 0))(gather.forward)
        pl.when((b == N_BLOCKS - 1) & (j == CHUNK - 1))(gather.finish)

        @pl.when(j == 0)
        def _():
            _ssm_expand_blocks(refs[3:7], (br_ref, bi_ref, cr_ref, ci_ref))
            _ssm_fill_powers(ar_ref, ai_ref, ldt_ref, br_ref, bi_ref, pw_ref, bbar_ref)
            zero_lag = d_ref[0] * _eye(BLOCK_CH)
            for d in range(2):
                _ssm_stack_inputs(d, pw_ref, bbar_ref, xs_ref)
                taps = (_dot_rounded(xs_ref[0], cr_ref[d, 0], ((1,), (1,)))
                        - _dot_rounded(xs_ref[1], ci_ref[d, 0], ((1,), (1,))))
                zero_lag = zero_lag + taps[0:BLOCK_CH]
                for t in range(1, CHUNK):
                    lag_ref[CHUNK - 1 + t if d == 0 else CHUNK - 1 - t] = taps[t * BLOCK_CH:(t + 1) * BLOCK_CH]
            lag_ref[CHUNK - 1] = zero_lag
            a16_ref[0] = jnp.concatenate([pw_ref[d, ri, CHUNK:CHUNK + 1, :] for d in range(2) for ri in range(2)], axis=1)

        m_ref[0] = jnp.concatenate([lag_ref[jp - j + CHUNK - 1] for jp in range(CHUNK)], axis=1).astype(BF16)

        def power(d, t):
            return pw_ref[d, 0, pl.ds(t, 1), :], pw_ref[d, 1, pl.ds(t, 1), :]

        parts = []
        for d, t in ((0, CHUNK - 1 - j), (1, j)):
            pr, pi = power(d, t)
            parts += [bbar_ref[d, 0] * pr - bbar_ref[d, 1] * pi, bbar_ref[d, 0] * pi + bbar_ref[d, 1] * pr]
        ws_ref[0] = jnp.concatenate(parts, axis=1).astype(BF16)
        parts = []
        for d, t in ((0, j + 1), (1, CHUNK - j)):
            pr, pi = power(d, t)
            parts += [cr_ref[d, 0] * pr - ci_ref[d, 0] * pi, -cr_ref[d, 0] * pi - ci_ref[d, 0] * pr]
        wot_ref[0] = jnp.concatenate(parts, axis=1).astype(BF16)

    row = pl.BlockSpec((1, BLOCK_CH, CHUNK_W), lambda b, j: (b, j, 0))
    mat = jax.ShapeDtypeStruct((N_BLOCKS, CHUNK_W, CHUNK_W), BF16)
    outs = pl.pallas_call(
        body, name="ssm_chunk_matrices", grid=(N_BLOCKS, CHUNK),
        in_specs=_ssm_param_specs() + _hbm_specs(n),
        out_specs=[row, row, row, pl.BlockSpec((1, 1, STATE_W), lambda b, j: (b, 0, 0))] + _hbm_specs(n),
        out_shape=[mat, mat, mat, jax.ShapeDtypeStruct((N_BLOCKS, 1, STATE_W), F32)] + _gather_out_shapes(shards),
        scratch_shapes=[pltpu.VMEM((2, 2, _POW_ROWS, BLOCK_ST), F32), pltpu.VMEM((2, 2, BLOCK_CH, BLOCK_ST), F32),
                        pltpu.VMEM((2 * CHUNK, BLOCK_CH, BLOCK_CH), F32), pltpu.VMEM((2, CHUNK_W, BLOCK_ST), F32)]
        + _ssm_block_scratch() + _gather_semaphores(n),
        compiler_params=_cparams(("arbitrary", "arbitrary"), has_side_effects=True),
    )(*blk, *shards)
    return outs[:4], outs[4:]


def _ssm_chunk_matrices_bwd(blk, d_m, d_ws, d_wot, d_a16):
    def body(ar_ref, ai_ref, ldt_ref, brc_ref, bic_ref, crc_ref, cic_ref, d_ref, dm_ref, dws_ref, dwot_ref, da16_ref,
             dar_ref, dai_ref, dldt_ref, dbr_ref, dbi_ref, dcr_ref, dci_ref, dd_ref,
             pw_ref, bbar_ref, dlag_ref, dbbar_ref, dc_ref, dpw_ref, xs_ref, dts_ref, br_ref, bi_ref, cr_ref, ci_ref):
        j = pl.program_id(1)
        w = BLOCK_ST

        @pl.when(j == 0)
        def _():
            _ssm_expand_blocks((brc_ref, bic_ref, crc_ref, cic_ref), (br_ref, bi_ref, cr_ref, ci_ref))
            _ssm_fill_powers(ar_ref, ai_ref, ldt_ref, br_ref, bi_ref, pw_ref, bbar_ref)
            for r in (dlag_ref, dbbar_ref, dc_ref, dpw_ref):
                r[...] = jnp.zeros_like(r)

        def fold(t):
            return jnp.sum(t.reshape(BLOCK_CH // 8, 8, w), axis=0)

        def d_power(d, ri, t):
            return jnp.sum(dpw_ref[d, ri, t], axis=0, keepdims=True)

        def x_chain(d, t, dxr, dxi):
            pr, pi = pw_ref[d, 0, pl.ds(t, 1), :], pw_ref[d, 1, pl.ds(t, 1), :]
            bbr, bbi = bbar_ref[d, 0], bbar_ref[d, 1]
            dbbar_ref[d, 0] += dxr * pr + dxi * pi
            dbbar_ref[d, 1] += dxi * pr - dxr * pi
            dpw_ref[d, 0, t] += fold(dxr * bbr + dxi * bbi)
            dpw_ref[d, 1, t] += fold(dxi * bbr - dxr * bbi)

        def z_chain(d, t, dzr, dzi):
            pr, pi = pw_ref[d, 0, pl.ds(t, 1), :], pw_ref[d, 1, pl.ds(t, 1), :]
            c_r, c_i = cr_ref[d, 0], ci_ref[d, 0]
            dc_ref[d, 0] += dzr * pr - dzi * pi
            dc_ref[d, 1] += -dzr * pi - dzi * pr
            dpw_ref[d, 0, t] += fold(dzr * c_r - dzi * c_i)
            dpw_ref[d, 1, t] += fold(-dzr * c_i - dzi * c_r)

        for jp in range(CHUNK):
            dlag_ref[jp - j + CHUNK - 1] += dm_ref[0, :, jp * BLOCK_CH:(jp + 1) * BLOCK_CH].astype(F32)
        quarter = lambda ref, i: ref[0, :, i * w:(i + 1) * w].astype(F32)
        x_chain(0, CHUNK - 1 - j, quarter(dws_ref, 0), quarter(dws_ref, 1))
        x_chain(1, j, quarter(dws_ref, 2), quarter(dws_ref, 3))
        z_chain(0, j + 1, quarter(dwot_ref, 0), quarter(dwot_ref, 1))
        z_chain(1, CHUNK - j, quarter(dwot_ref, 2), quarter(dwot_ref, 3))

        @pl.when(j == CHUNK - 1)
        def _():
            for d in range(2):
                _ssm_stack_inputs(d, pw_ref, bbar_ref, xs_ref)
                for t in range(CHUNK):
                    dts_ref[t * BLOCK_CH:(t + 1) * BLOCK_CH, :] = dlag_ref[CHUNK - 1 + t if d == 0 else CHUNK - 1 - t]
                d_taps = dts_ref[...]
                dc_ref[d, 0] += _dot_rounded(d_taps, xs_ref[0], ((0,), (0,)))
                dc_ref[d, 1] -= _dot_rounded(d_taps, xs_ref[1], ((0,), (0,)))
                xs_ref[0] = _dot_rounded(d_taps, cr_ref[d, 0])
                xs_ref[1] = -_dot_rounded(d_taps, ci_ref[d, 0])
                for t in range(CHUNK):
                    rows = slice(t * BLOCK_CH, (t + 1) * BLOCK_CH)
                    x_chain(d, t, xs_ref[0, rows, :], xs_ref[1, rows, :])
            dd_ref[0] = jnp.sum(dlag_ref[CHUNK - 1] * _eye(BLOCK_CH), axis=0, keepdims=True)
            for d in range(2):
                (abr, abi, cfr, cfi), disc_vjp = jax.vjp(_ssm_discretise, ar_ref[d, 0], ai_ref[d, 0], ldt_ref[d, 0])
                dpr = d_power(d, 0, CHUNK) + da16_ref[0, :, 2 * d * w:(2 * d + 1) * w]
                dpi = d_power(d, 1, CHUNK) + da16_ref[0, :, (2 * d + 1) * w:(2 * d + 2) * w]
                dabr, dabi = jnp.zeros_like(abr), jnp.zeros_like(abi)
                for t in range(CHUNK, 0, -1):
                    qr, qi = pw_ref[d, 0, t - 1:t, :], pw_ref[d, 1, t - 1:t, :]
                    dabr = dabr + dpr * qr + dpi * qi
                    dabi = dabi + dpi * qr - dpr * qi
                    dpr, dpi = (dpr * abr + dpi * abi + d_power(d, 0, t - 1),
                                dpi * abr - dpr * abi + d_power(d, 1, t - 1))
                dbbr, dbbi = dbbar_ref[d, 0], dbbar_ref[d, 1]
                b_r, b_i = br_ref[d, 0], bi_ref[d, 0]
                dbr_ref[d, 0] = _ssm_collapse_block(cfr * dbbr + cfi * dbbi)
                dbi_ref[d, 0] = _ssm_collapse_block(cfr * dbbi - cfi * dbbr)
                dcfr = jnp.sum(b_r * dbbr + b_i * dbbi, axis=0, keepdims=True)
                dcfi = jnp.sum(b_r * dbbi - b_i * dbbr, axis=0, keepdims=True)
                dar_ref[d, 0], dai_ref[d, 0], dldt_ref[d, 0] = disc_vjp((dabr, dabi, dcfr, dcfi))
                dcr_ref[d, 0] = _ssm_collapse_block(dc_ref[d, 0])
                dci_ref[d, 0] = _ssm_collapse_block(dc_ref[d, 1])

    row = pl.BlockSpec((1, BLOCK_CH, CHUNK_W), lambda b, j: (b, j, 0))
    specs = _ssm_param_specs()
    acc = lambda *s: pltpu.VMEM(s, F32)
    return pl.pallas_call(
        body, name="ssm_chunk_matrices_bwd", grid=(N_BLOCKS, CHUNK),
        in_specs=specs + [row, row, row, pl.BlockSpec((1, 1, STATE_W), lambda b, j: (b, 0, 0))],
        out_specs=specs,
        out_shape=[jax.ShapeDtypeStruct(t.shape, F32) for t in blk],
        scratch_shapes=[acc(2, 2, _POW_ROWS, BLOCK_ST), acc(2, 2, BLOCK_CH, BLOCK_ST), acc(2 * CHUNK, BLOCK_CH, BLOCK_CH),
                        acc(2, 2, BLOCK_CH, BLOCK_ST), acc(2, 2, BLOCK_CH, BLOCK_ST), acc(2, 2, CHUNK + 1, 8, BLOCK_ST),
                        acc(2, CHUNK_W, BLOCK_ST), acc(CHUNK_W, BLOCK_CH)] + _ssm_block_scratch(),
        compiler_params=_cparams(("arbitrary", "arbitrary")),
    )(*blk, d_m, d_ws, d_wot, d_a16)


def _block_matmul(terms, name, out_dtype=F32, tn=1024):
    nc = terms[0][0].shape[1]
    n_out = terms[0][1].shape[1] if terms[0][2] else terms[0][1].shape[2]
    flags = [t[2] for t in terms]

    def body(*refs):
        out_ref = refs[-1]
        acc = None
        for t, transposed in enumerate(flags):
            a = refs[2 * t][0].astype(BF16)
            w = refs[2 * t + 1][0]
            part = _dot_nt(a, w) if transposed else _dot(a, w)
            acc = part if acc is None else acc + part
        out_ref[0] = acc.astype(out_dtype)

    in_specs, args = [], []
    for a, w, transposed in terms:
        k = a.shape[2]
        in_specs.append(pl.BlockSpec((1, nc, k), lambda b, n: (b, 0, 0)))
        if transposed:
            in_specs.append(pl.BlockSpec((1, tn, k), lambda b, n: (b, n, 0)))
        else:
            in_specs.append(pl.BlockSpec((1, k, tn), lambda b, n: (b, 0, n)))
        args += [a, w]
    return pl.pallas_call(
        body, name=name, grid=(N_BLOCKS, n_out // tn), in_specs=in_specs,
        out_specs=pl.BlockSpec((1, nc, tn), lambda b, n: (b, 0, n)),
        out_shape=jax.ShapeDtypeStruct((N_BLOCKS, nc, n_out), out_dtype),
        compiler_params=_cparams(("arbitrary", "arbitrary")),
    )(*args)


def _block_matmul_tn(a, b, name, tile=1024):
    nc, m = a.shape[1], a.shape[2]
    n = b.shape[2]

    def body(a_ref, b_ref, out_ref):
        a_t = a_ref[0].astype(BF16)
        for j in range(n // tile):
            cols = slice(j * tile, (j + 1) * tile)
            out_ref[0, :, cols] = _dot_tn(a_t, b_ref[0, :, cols].astype(BF16)).astype(BF16)

    return pl.pallas_call(
        body, name=name, grid=(N_BLOCKS, m // tile),
        in_specs=[pl.BlockSpec((1, nc, tile), lambda blk, i: (blk, 0, i)),
                  pl.BlockSpec((1, nc, n), lambda blk, i: (blk, 0, 0))],
        out_specs=pl.BlockSpec((1, tile, n), lambda blk, i: (blk, i, 0)),
        out_shape=jax.ShapeDtypeStruct((N_BLOCKS, m, n), BF16),
        compiler_params=_cparams(("arbitrary", "arbitrary")),
    )(a, b)


def _cmul(ar, ai, xr, xi):
    return ar * xr - ai * xi, ar * xi + ai * xr


def _cmul_conj(ar, ai, xr, xi):
    return ar * xr + ai * xi, ar * xi - ai * xr


_SCAN_UNROLL = 8


def _ssm_state_scan(s_in, a16):
    nc = s_in.shape[1]
    w = BLOCK_ST

    def body(sin_ref, a_ref, out_ref):
        a = a_ref[0]
        afr, afi, abr, abi = a[:, 0:w], a[:, w:2 * w], a[:, 2 * w:3 * w], a[:, 3 * w:4 * w]

        def step(c, carry):
            fr, fi, br, bi = carry
            cb = nc - 1 - c
            out_ref[0, pl.ds(c, 1), 0:w] = fr
            out_ref[0, pl.ds(c, 1), w:2 * w] = fi
            out_ref[0, pl.ds(cb, 1), 2 * w:3 * w] = br
            out_ref[0, pl.ds(cb, 1), 3 * w:4 * w] = bi
            nfr, nfi = _cmul(afr, afi, fr, fi)
            nbr, nbi = _cmul(abr, abi, br, bi)
            return (nfr + sin_ref[0, pl.ds(c, 1), 0:w], nfi + sin_ref[0, pl.ds(c, 1), w:2 * w],
                    nbr + sin_ref[0, pl.ds(cb, 1), 2 * w:3 * w], nbi + sin_ref[0, pl.ds(cb, 1), 3 * w:4 * w])

        def steps(i, carry):
            for k in range(_SCAN_UNROLL):
                carry = step(i * _SCAN_UNROLL + k, carry)
            return carry

        z = jnp.zeros((1, w), F32)
        lax.fori_loop(0, nc // _SCAN_UNROLL, steps, (z, z, z, z))

    spec = pl.BlockSpec((1, nc, STATE_W), lambda b: (b, 0, 0))
    return pl.pallas_call(
        body, name="ssm_state_scan", grid=(N_BLOCKS,),
        in_specs=[spec, pl.BlockSpec((1, 1, STATE_W), lambda b: (b, 0, 0))],
        out_specs=spec, out_shape=jax.ShapeDtypeStruct(s_in.shape, F32),
        compiler_params=_cparams(("arbitrary",)),
    )(s_in, a16)


def _ssm_state_scan_bwd(d_prev, s_prev, a16):
    nc = d_prev.shape[1]
    w = BLOCK_ST

    def body(dp_ref, sp_ref, a_ref, g_ref, da_ref):
        a = a_ref[0]
        afr, afi, abr, abi = a[:, 0:w], a[:, w:2 * w], a[:, 2 * w:3 * w], a[:, 3 * w:4 * w]

        def step(i, carry):
            gfr, gfi, gbr, gbi, dafr, dafi, dabr, dabi = carry
            cf = nc - 1 - i
            cb = i
            g_ref[0, pl.ds(cf, 1), 0:w] = gfr
            g_ref[0, pl.ds(cf, 1), w:2 * w] = gfi
            g_ref[0, pl.ds(cb, 1), 2 * w:3 * w] = gbr
            g_ref[0, pl.ds(cb, 1), 3 * w:4 * w] = gbi
            sfr, sfi = sp_ref[0, pl.ds(cf, 1), 0:w], sp_ref[0, pl.ds(cf, 1), w:2 * w]
            sbr, sbi = sp_ref[0, pl.ds(cb, 1), 2 * w:3 * w], sp_ref[0, pl.ds(cb, 1), 3 * w:4 * w]
            dafr = dafr + gfr * sfr + gfi * sfi
            dafi = dafi + gfi * sfr - gfr * sfi
            dabr = dabr + gbr * sbr + gbi * sbi
            dabi = dabi + gbi * sbr - gbr * sbi
            nfr, nfi = _cmul_conj(afr, afi, gfr, gfi)
            nbr, nbi = _cmul_conj(abr, abi, gbr, gbi)
            return (nfr + dp_ref[0, pl.ds(cf, 1), 0:w], nfi + dp_ref[0, pl.ds(cf, 1), w:2 * w],
                    nbr + dp_ref[0, pl.ds(cb, 1), 2 * w:3 * w], nbi + dp_ref[0, pl.ds(cb, 1), 3 * w:4 * w],
                    dafr, dafi, dabr, dabi)

        def steps(i, carry):
            for k in range(_SCAN_UNROLL):
                carry = step(i * _SCAN_UNROLL + k, carry)
            return carry

        z = jnp.zeros((1, w), F32)
        res = lax.fori_loop(0, nc // _SCAN_UNROLL, steps, (z,) * 8)
        da_ref[0] = jnp.concatenate(res[4:], axis=1)

    spec = pl.BlockSpec((1, nc, STATE_W), lambda b: (b, 0, 0))
    aspec = pl.BlockSpec((1, 1, STATE_W), lambda b: (b, 0, 0))
    return pl.pallas_call(
        body, name="ssm_state_scan_bwd", grid=(N_BLOCKS,),
        in_specs=[spec, spec, aspec], out_specs=[spec, aspec],
        out_shape=[jax.ShapeDtypeStruct(d_prev.shape, F32), jax.ShapeDtypeStruct((N_BLOCKS, 1, STATE_W), F32)],
        compiler_params=_cparams(("arbitrary",)),
    )(d_prev, s_prev, a16)


NA_PAIR = 2 * GRID_W
NA_WIN_ROWS = NA_ROWS + 2
NA_WIN = NA_WIN_ROWS * GRID_W
NA_PAIRS_PER_STEP = 8
NA_CASES = 5
NA_MASKED = -1e30


def _na_pair_window(m, rows):
    rs0 = jnp.clip(2 * m - NA_ROWS // 2, 0, rows - NA_ROWS)
    ws = jnp.minimum(rs0, rows - NA_WIN_ROWS)
    last = rows // 2 - 1
    case = jnp.where(m == 0, 0, jnp.where(m == 1, 1, jnp.where(m == last - 1, 3, jnp.where(m == last, 4, 2))))
    return ws, case


def _na_row_offsets(rows):
    last = rows // 2 - 1
    geom = []
    for m in (0, 1, 2, last - 1, last):
        ws = min(max(2 * m - NA_ROWS // 2, 0), rows - NA_ROWS, rows - NA_WIN_ROWS)
        per_case = []
        for i in range(NA_WIN_ROWS):
            pair = []
            for rr in range(2):
                r = 2 * m + rr
                rs = min(max(r - NA_ROWS // 2, 0), rows - NA_ROWS)
                pair.append(ws + i - r + NA_ROWS - 1 if rs <= ws + i < rs + NA_ROWS else None)
            per_case.append(pair)
        geom.append(per_case)
    return geom


def _na_col_select():
    qc = np.arange(NA_PAIR)[None, :] % GRID_W
    kc = np.arange(GRID_W)[:, None]
    dc = np.clip(kc - qc + NA_COLS - 1, 0, 2 * NA_COLS - 2)
    return jnp.asarray((np.arange(2 * NA_COLS - 1)[:, None, None] == dc[None]).astype(np.float32))


def _na_bias_rows(rpb):
    return jnp.einsum("hrd,dkl->hrkl", rpb, _na_col_select(), precision=HIGHEST)


def _na_col_window():
    qc = lax.broadcasted_iota(jnp.int32, (GRID_W, NA_PAIR), 1) % GRID_W
    kc = lax.broadcasted_iota(jnp.int32, (GRID_W, NA_PAIR), 0)
    cs = jnp.clip(qc - NA_COLS // 2, 0, GRID_W - NA_COLS)
    first_row = lax.broadcasted_iota(jnp.int32, (GRID_W, NA_PAIR), 1) < GRID_W
    return (kc >= cs) & (kc < cs + NA_COLS), first_row


def _na_bias_table(bias_rows, rows):
    geom = _na_row_offsets(rows)

    def body(br_ref, tab_ref):
        col_ok, first_row = _na_col_window()
        masked = jnp.full((GRID_W, NA_PAIR), NA_MASKED, F32)
        for case in range(NA_CASES):
            for i in range(NA_WIN_ROWS):
                d0, d1 = geom[case][i]
                t0 = masked if d0 is None else br_ref[0, d0]
                t1 = masked if d1 is None else br_ref[0, d1]
                tile = jnp.where(col_ok, jnp.where(first_row, t0, t1), NA_MASKED)
                tab_ref[0, case, i * GRID_W:(i + 1) * GRID_W, :] = tile

    return pl.pallas_call(
        body, name="na_bias_table", grid=(NA_HEADS,),
        in_specs=[pl.BlockSpec((1, 2 * NA_ROWS - 1, GRID_W, NA_PAIR), lambda h: (h, 0, 0, 0))],
        out_specs=pl.BlockSpec((1, NA_CASES, NA_WIN, NA_PAIR), lambda h: (h, 0, 0, 0)),
        out_shape=jax.ShapeDtypeStruct((NA_HEADS, NA_CASES, NA_WIN, NA_PAIR), F32),
        compiler_params=_cparams(("arbitrary",)),
    )(bias_rows)


def _na_bias_table_bwd(d_tab, rows):
    geom = _na_row_offsets(rows)

    def body(dt_ref, dbr_ref):
        col_ok, first_row = _na_col_window()
        acc = [None] * (2 * NA_ROWS - 1)
        for case in range(NA_CASES):
            for i in range(NA_WIN_ROWS):
                tile = jnp.where(col_ok, dt_ref[0, case, i * GRID_W:(i + 1) * GRID_W, :], 0.0)
                for rr, d in enumerate(geom[case][i]):
                    if d is not None:
                        part = jnp.where(first_row if rr == 0 else ~first_row, tile, 0.0)
                        acc[d] = part if acc[d] is None else acc[d] + part
        for d, a in enumerate(acc):
            dbr_ref[0, d] = jnp.zeros((GRID_W, NA_PAIR), F32) if a is None else a

    return pl.pallas_call(
        body, name="na_bias_table_bwd", grid=(NA_HEADS,),
        in_specs=[pl.BlockSpec((1, NA_CASES, NA_WIN, NA_PAIR), lambda h: (h, 0, 0, 0))],
        out_specs=pl.BlockSpec((1, 2 * NA_ROWS - 1, GRID_W, NA_PAIR), lambda h: (h, 0, 0, 0)),
        out_shape=jax.ShapeDtypeStruct((NA_HEADS, 2 * NA_ROWS - 1, GRID_W, NA_PAIR), F32),
        compiler_params=_cparams(("arbitrary",)),
    )(d_tab)


NA_BLK = 64


def _na_blocks():
    return [slice(i * NA_BLK, (i + 1) * NA_BLK) for i in range(NA_WIN // NA_BLK)]


def _na_softmax(qk, bias_ref, hh, case):
    m = jnp.full((NA_BLK, NA_PAIR), -jnp.inf, F32)
    scores = []
    for blk in _na_blocks():
        s = qk[blk, :] + bias_ref[hh, case, blk, :]
        scores.append(s)
        m = jnp.maximum(m, s)
    m = jnp.max(m, axis=0, keepdims=True)
    l = jnp.zeros((NA_BLK, NA_PAIR), F32)
    exps = []
    for s in scores:
        e = jnp.exp(s - m)
        exps.append(e)
        l = l + e
    return exps, jnp.sum(l, axis=0, keepdims=True)


def _na_units(step, rows):
    units = []
    for pp in range(NA_PAIRS_PER_STEP):
        ws, case = _na_pair_window(step * NA_PAIRS_PER_STEP + pp, rows)
        win = pl.ds(pl.multiple_of(ws * GRID_W, NA_PAIR), NA_WIN)
        lanes = slice(pp * NA_PAIR, (pp + 1) * NA_PAIR)
        for hh in range(2):
            units.append((pp, hh, case, win, lanes, slice(hh * NA_HEAD_DIM, (hh + 1) * NA_HEAD_DIM)))
    return units


def _na_pipeline(n, before, middle, after, lookahead):
    for u in range(min(lookahead, n)):
        for f in before:
            f(u)
    for u in range(n):
        middle(u)
        if u + lookahead < n:
            for f in before:
                f(u + lookahead)
        for f in after:
            f(u)


def _head_rows(t, hh):
    row_head = lax.broadcasted_iota(jnp.int32, t.shape, 0) // NA_HEAD_DIM
    return jnp.where(row_head == hh, t, jnp.zeros_like(t))


def _heads_block_diag(t):
    lane_head = lax.broadcasted_iota(jnp.int32, t.shape, 1) // NA_HEAD_DIM
    zero = jnp.zeros_like(t)
    return jnp.concatenate([jnp.where(lane_head == 0, t, zero), jnp.where(lane_head == 1, t, zero)], axis=0)


def _na_fwd(q_t, k, v_t, bias_tab):
    L = k.shape[0]
    rows = L // GRID_W
    step_w = NA_PAIRS_PER_STEP * NA_PAIR

    def body(q_ref, k_ref, v_ref, bt_ref, o_ref):
        units = _na_units(pl.program_id(1), rows)
        qk, probs = {}, {}

        def scores(u):
            _, hh, _, win, lanes, _ = units[u]
            qk[u] = _dot(k_ref[win, :], _head_rows(q_ref[:, lanes], hh))

        def softmax(u):
            _, hh, case, _, _, _ = units[u]
            exps, l = _na_softmax(qk.pop(u), bt_ref, hh, case)
            probs[u] = jnp.concatenate([t.astype(BF16) for t in exps], axis=0), l

        def output(u):
            _, _, _, win, lanes, hrows = units[u]
            e, l = probs.pop(u)
            o_ref[hrows, lanes] = _dot(v_ref[hrows, win], e) / l

        _na_pipeline(len(units), [scores], softmax, [output], lookahead=3)

    q_spec = pl.BlockSpec((NA_PAIR, step_w), lambda h, s: (h, s))
    return pl.pallas_call(
        body, name="na_fwd", grid=(NA_HEADS // 2, L // step_w),
        in_specs=[q_spec, pl.BlockSpec((L, NA_PAIR), lambda h, s: (0, h)),
                  pl.BlockSpec((NA_PAIR, L), lambda h, s: (h, 0)),
                  pl.BlockSpec((2, NA_CASES, NA_WIN, NA_PAIR), lambda h, s: (h, 0, 0, 0))],
        out_specs=q_spec,
        out_shape=jax.ShapeDtypeStruct((D_NA, L), F32),
        compiler_params=_cparams(("arbitrary", "arbitrary")),
    )(q_t, k, v_t, bias_tab)


def _na_bwd(q_t, q, k_t, k, v, bias_tab, out_t, d_out_t, d_out):
    L = k.shape[0]
    rows = L // GRID_W
    step_w = NA_PAIRS_PER_STEP * NA_PAIR

    def body(qt_ref, q_ref, kt_ref, k_ref, v_ref, bt_ref, ot_ref, dot_ref, do_ref, dq_ref, dk_ref, dv_ref, dbt_ref):
        @pl.when(pl.program_id(1) == 0)
        def _():
            dk_ref[...] = jnp.zeros_like(dk_ref)
            dv_ref[...] = jnp.zeros_like(dv_ref)
            dbt_ref[...] = jnp.zeros_like(dbt_ref)

        units = _na_units(pl.program_id(1), rows)
        qk, dp, dsb, pb = {}, {}, {}, {}

        def scores(u):
            _, hh, _, win, lanes, _ = units[u]
            qk[u] = _dot(k_ref[win, :], _head_rows(qt_ref[:, lanes], hh))

        def d_probs(u):
            _, hh, _, win, lanes, _ = units[u]
            dp[u] = _dot(v_ref[win, :], _head_rows(dot_ref[:, lanes].astype(BF16), hh))

        def softmax_bwd(u):
            _, hh, case, _, lanes, hrows = units[u]
            exps, l = _na_softmax(qk.pop(u), bt_ref, hh, case)
            inv_l = 1.0 / l
            delta = jnp.sum(dot_ref[hrows, lanes] * ot_ref[hrows, lanes], axis=0, keepdims=True)
            d_p = dp.pop(u)
            ds_blocks, p_blocks = [], []
            for blk, e in zip(_na_blocks(), exps):
                p = e * inv_l
                ds = p * (d_p[blk, :] - delta)
                dbt_ref[hh, case, blk, :] += ds
                ds_blocks.append(ds.astype(BF16))
                p_blocks.append(p.astype(BF16))
            dsb[u] = jnp.concatenate(ds_blocks, axis=0)
            pb[u] = jnp.concatenate(p_blocks, axis=0)

        def d_query(u):
            _, _, _, win, lanes, hrows = units[u]
            dq_ref[hrows, lanes] = _dot(kt_ref[hrows, win], dsb[u]) * (NA_HEAD_DIM ** -0.5)

        def d_keys_values(u):
            pp, hh, _, win, _, _ = units[u]
            if hh == 1:
                tokens = slice(pp * NA_PAIR, (pp + 1) * NA_PAIR)
                dk_ref[win, :] += _dot(jnp.concatenate([dsb.pop(u - 1), dsb.pop(u)], axis=1), _heads_block_diag(q_ref[tokens, :]))
                dv_ref[win, :] += _dot(jnp.concatenate([pb.pop(u - 1), pb.pop(u)], axis=1), _heads_block_diag(do_ref[tokens, :]))

        _na_pipeline(len(units), [scores, d_probs], softmax_bwd, [d_query, d_keys_values], lookahead=2)

    t_tile = pl.BlockSpec((NA_PAIR, step_w), lambda h, s: (h, s))
    tile = pl.BlockSpec((step_w, NA_PAIR), lambda h, s: (s, h))
    t_full = pl.BlockSpec((NA_PAIR, L), lambda h, s: (h, 0))
    full = pl.BlockSpec((L, NA_PAIR), lambda h, s: (0, h))
    bt = pl.BlockSpec((2, NA_CASES, NA_WIN, NA_PAIR), lambda h, s: (h, 0, 0, 0))
    tok = jax.ShapeDtypeStruct((L, D_NA), F32)
    return pl.pallas_call(
        body, name="na_bwd", grid=(NA_HEADS // 2, L // step_w),
        in_specs=[t_tile, tile, t_full, full, full, bt, t_tile, t_tile, tile],
        out_specs=[t_tile, full, full, bt],
        out_shape=[jax.ShapeDtypeStruct((D_NA, L), F32), tok, tok, jax.ShapeDtypeStruct(bias_tab.shape, F32)],
        compiler_params=_cparams(("arbitrary", "arbitrary")),
    )(q_t, q, k_t, k, v, bias_tab, out_t, d_out_t, d_out)


def _branch_fwd_values(ys, zs, yn, zn, wglu, bglu):
    g1, t = _gelu_parts(ys)
    lin = _dot(g1.astype(BF16), wglu) + bglu
    sg = _sigmoid(lin)
    ys2 = g1 * sg
    sz, szs = _silu_parts(zs)
    sn, sns = _silu_parts(zn)
    return g1, t, sg, ys2, sz, szs, sn, sns


def _branch_fwd(y_ssm_c, z_s, y_na_t, z_n, w_glu, b_glu, tm=512):
    L = z_s.shape[0]

    def body(ys_ref, zs_ref, yn_ref, zn_ref, w_ref, b_ref, cat_ref, scr):
        yn = yn_ref[...].T
        g1, t, sg, ys2, sz, szs, sn, sns = _branch_fwd_values(
            _load_chunks(ys_ref, scr), zs_ref[...], yn, zn_ref[...], w_ref[...], b_ref[...])
        cat_ref[:, 0:512] = (ys2 * sz).astype(BF16)
        cat_ref[:, 512:1024] = (yn * sn).astype(BF16)

    tile = pl.BlockSpec((tm, 512), lambda i: (i, 0))
    return pl.pallas_call(
        body, name="branch_fwd", grid=(L // tm,),
        in_specs=[_chunk_spec(tm), tile, _heads_t_spec(tm), tile, pl.BlockSpec((512, 512), lambda i: (0, 0)),
                  pl.BlockSpec((1, 512), lambda i: (0, 0))],
        out_specs=pl.BlockSpec((tm, 1024), lambda i: (i, 0)),
        out_shape=jax.ShapeDtypeStruct((L, 1024), BF16),
        scratch_shapes=[_chunk_scratch(tm)],
        compiler_params=_cparams(("arbitrary",)),
    )(y_ssm_c, z_s, y_na_t, z_n, w_glu, b_glu)


def _branch_bwd(y_ssm_c, z_s, y_na_t, z_n, w_glu, b_glu, d_cat, tm=512):
    L = z_s.shape[0]

    def body(ys_ref, zs_ref, yn_ref, zn_ref, w_ref, b_ref, dc_ref,
             dys_ref, dzs_ref, dynt_ref, dyn_ref, dzn_ref, dw_ref, db_ref, scr):
        @pl.when(pl.program_id(0) == 0)
        def _():
            dw_ref[...] = jnp.zeros_like(dw_ref)
            db_ref[...] = jnp.zeros_like(db_ref)

        ys, zs, yn, zn = _load_chunks(ys_ref, scr), zs_ref[...], yn_ref[...].T, zn_ref[...]
        w = w_ref[...]
        g1, t, sg, ys2, sz, szs, sn, sns = _branch_fwd_values(ys, zs, yn, zn, w, b_ref[...])
        dys3 = dc_ref[:, 0:512]
        dyn2 = dc_ref[:, 512:1024]
        dzs_ref[...] = (dys3 * ys2 * _silu_grad(zs, szs)).astype(BF16)
        dys2 = dys3 * sz
        dlin = dys2 * g1 * sg * (1.0 - sg)
        dlb = dlin.astype(BF16)
        dg1 = dys2 * sg + _dot_nt(dlb, w)
        dw_ref[...] += _dot_tn(g1.astype(BF16), dlb)
        db_ref[...] += jnp.sum(dlin, axis=0, keepdims=True)
        _store_chunks(dg1 * _gelu_grad(ys, t), scr, dys_ref, BF16)
        dyn = dyn2 * sn
        dynt_ref[...] = dyn.T
        dyn_ref[...] = dyn.astype(BF16)
        dzn_ref[...] = (dyn2 * yn * _silu_grad(zn, sns)).astype(BF16)

    tile = pl.BlockSpec((tm, 512), lambda i: (i, 0))
    wspec = pl.BlockSpec((512, 512), lambda i: (0, 0))
    bspec = pl.BlockSpec((1, 512), lambda i: (0, 0))
    tok = jax.ShapeDtypeStruct((L, 512), BF16)
    return pl.pallas_call(
        body, name="branch_bwd", grid=(L // tm,),
        in_specs=[_chunk_spec(tm), tile, _heads_t_spec(tm), tile, wspec, bspec, pl.BlockSpec((tm, 1024), lambda i: (i, 0))],
        out_specs=[_chunk_spec(tm), tile, _heads_t_spec(tm), tile, tile, wspec, bspec],
        out_shape=[jax.ShapeDtypeStruct((N_BLOCKS, L // CHUNK, CHUNK_W), BF16), tok, jax.ShapeDtypeStruct((D_NA, L), F32),
                   tok, tok,
                   jax.ShapeDtypeStruct((512, 512), F32), jax.ShapeDtypeStruct((1, 512), F32)],
        scratch_shapes=[_chunk_scratch(tm)],
        compiler_params=_cparams(("arbitrary",)),
    )(y_ssm_c, z_s, y_na_t, z_n, w_glu, b_glu, d_cat)


def _head(x, p, target, cat, w_out, g_post, w_ple_g, g_ple, w_pg, tm=512):
    L = x.shape[0]
    pw = w_ple_g.shape[2]

    def body(x_ref, p_ref, t_ref, cat_ref, wo_ref, gpo_ref, wp_ref, gpl_ref, wg_ref,
             loss_ref, dh1_ref, dcat_ref, dwo_ref, dgpo_ref, dwp_ref, dgpl_ref, dwg_ref):
        @pl.when(pl.program_id(0) == 0)
        def _():
            for r in (loss_ref, dwo_ref, dgpo_ref, dwp_ref, dgpl_ref, dwg_ref):
                r[...] = jnp.zeros_like(r)

        cat_b = cat_ref[...]
        wo, wg = wo_ref[...], wg_ref[...]
        g_po, g_pl = gpo_ref[...], gpl_ref[...]
        mix = _dot(cat_b, wo)
        p_b = p_ref[...].astype(BF16)
        ep = jnp.concatenate([_dot(p_b, wp_ref[j]) for j in range(N_CHIPS)], axis=1)
        nm, r2 = _rms(mix)
        h1 = x_ref[...] + nm * g_po
        ne, r3 = _rms(ep)
        e = ne * g_pl
        h1_b = h1.astype(BF16)
        gate = _sigmoid(_dot(h1_b, wg))
        h2 = h1 + gate * e
        diff = h2 - t_ref[...]
        loss_ref[...] += (0.5 / D_MODEL) * jnp.sum(diff * diff).reshape(1, 1)

        dh2 = diff * (1.0 / D_MODEL)
        de = dh2 * gate
        dgl = (dh2 * e * gate * (1.0 - gate)).astype(BF16)
        dh1 = dh2 + _dot_nt(dgl, wg)
        dwg_ref[...] += _dot_tn(h1_b, dgl)
        dgpo_ref[...] += jnp.sum(dh1 * nm, axis=0, keepdims=True)
        dmix = _rms_bwd(dh1 * g_po, nm, r2).astype(BF16)
        dcat_ref[...] = _dot_nt(dmix, wo)
        dwo_ref[...] += _dot_tn(cat_b, dmix)
        dh1_ref[...] = dh1
        dgpl_ref[...] += jnp.sum(de * ne, axis=0, keepdims=True)
        dep = _rms_bwd(de * g_pl, ne, r3).astype(BF16)
        for j in range(N_CHIPS):
            dwp_ref[j] += _dot_tn(p_b, dep[:, j * pw:(j + 1) * pw])

    tile = lambda w: pl.BlockSpec((tm, w), lambda i: (i, 0))
    const = _resident
    sds = jax.ShapeDtypeStruct
    return pl.pallas_call(
        body, name="head", grid=(L // tm,),
        in_specs=[tile(D_MODEL), tile(D_PLE), tile(D_MODEL), tile(1024), const(1024, D_MODEL), const(1, D_MODEL),
                  const(N_CHIPS, D_PLE, pw), const(1, D_MODEL), const(D_MODEL, D_MODEL)],
        out_specs=[const(1, 1), tile(D_MODEL), tile(1024), const(1024, D_MODEL), const(1, D_MODEL),
                   const(N_CHIPS, D_PLE, pw), const(1, D_MODEL), const(D_MODEL, D_MODEL)],
        out_shape=[sds((1, 1), F32), sds((L, D_MODEL), F32), sds((L, 1024), F32), sds((1024, D_MODEL), F32),
                   sds((1, D_MODEL), F32), sds((N_CHIPS, D_PLE, pw), F32), sds((1, D_MODEL), F32),
                   sds((D_MODEL, D_MODEL), F32)],
        compiler_params=_cparams(("arbitrary",)),
    )(x, p, target, cat, w_out, g_post, w_ple_g, g_ple, w_pg)


def _dproj_specs(tm):
    tile = pl.BlockSpec((tm, 512), lambda i: (i, 0))
    return [_chunk_spec(tm), tile, _heads_t_spec(tm), tile, tile, tile]


_DPROJ_ORDER = (3, 4, 5, 1, 2, 0)


def _dproj_part(refs, scr, i):
    if i == 0:
        val = _load_chunks(refs[0], scr)
    elif i == 2:
        val = refs[2][...].T
    else:
        val = refs[i][...]
    return val.astype(BF16)


def _dproj_pieces(i, wn):
    lo, hi = 512 * i, 512 * (i + 1)
    pieces = []
    for j in range(N_CHIPS):
        a, b = max(lo, j * wn), min(hi, (j + 1) * wn)
        if a < b:
            pieces.append((j, slice(a - j * wn, b - j * wn), slice(a - lo, b - lo)))
    return pieces


def _in_proj_bwd_w(x, g_col, w_in_g, dparts, tm=512):
    L = x.shape[0]
    wn = D_IN_PROJ // N_CHIPS
    steps = L // tm

    def body(x_ref, g_ref, w_ref, *refs):
        dw_ref, dg_ref, scr = refs[-3], refs[-2], refs[-1]

        @pl.when(pl.program_id(0) == 0)
        def _():
            dw_ref[...] = jnp.zeros_like(dw_ref)

        n, _ = _rms(x_ref[...])
        nb = n.astype(BF16)
        for i in _DPROJ_ORDER:
            part = _dproj_part(refs[:-3], scr, i)
            for j, w_cols, p_cols in _dproj_pieces(i, wn):
                dw_ref[j, :, w_cols] += _dot_tn(nb, part[:, p_cols])

        @pl.when(pl.program_id(0) == steps - 1)
        def _():
            g = g_ref[...]
            dg = jnp.zeros_like(g)
            for j in range(N_CHIPS):
                a = dw_ref[j]
                dg = dg + jnp.sum(a * w_ref[j].astype(F32), axis=1, keepdims=True)
                dw_ref[j] = a * g
            dg_ref[...] = dg

    return pl.pallas_call(
        body, name="in_proj_bwd_w", grid=(steps,),
        in_specs=[pl.BlockSpec((tm, D_MODEL), lambda i: (i, 0)), _resident(D_MODEL, 1), _resident(N_CHIPS, D_MODEL, wn)]
        + _dproj_specs(tm),
        out_specs=[_resident(N_CHIPS, D_MODEL, wn), _resident(D_MODEL, 1)],
        out_shape=[jax.ShapeDtypeStruct((N_CHIPS, D_MODEL, wn), F32), jax.ShapeDtypeStruct((D_MODEL, 1), F32)],
        scratch_shapes=[_chunk_scratch(tm)],
        compiler_params=_cparams(("arbitrary",)),
    )(x, g_col, w_in_g, *dparts)


def _in_proj_bwd_x(x, g_pre, w_in_g, d_h1, dparts, pair_sums, tm=512):
    L = x.shape[0]
    wn = w_in_g.shape[2]
    n_ps = len(pair_sums)
    steps = L // tm

    def body(*refs):
        x_ref, g_ref, w_ref, dh1_ref = refs[:4]
        dparts_refs = refs[4:10]
        dx_ref = refs[10 + n_ps]
        scr = refs[11 + 2 * n_ps]
        scatter = _ChipScatter(refs[10:10 + n_ps], refs[11 + n_ps:11 + 2 * n_ps], refs[12 + 2 * n_ps:16 + 2 * n_ps],
                               refs[16 + 2 * n_ps:])
        pl.when(pl.program_id(0) == 0)(scatter.start)
        pl.when(pl.program_id(0) == steps - 1)(scatter.finish)

        dhn = None
        for i in _DPROJ_ORDER:
            part = _dproj_part(dparts_refs, scr, i)
            for j, w_cols, p_cols in _dproj_pieces(i, wn):
                term = _dot_nt(part[:, p_cols], w_ref[j, :, w_cols])
                dhn = term if dhn is None else dhn + term
        n, r = _rms(x_ref[...])
        dx_ref[...] = dh1_ref[...] + _rms_bwd(dhn * g_ref[...], n, r)

    wide = pl.BlockSpec((tm, D_MODEL), lambda i: (i, 0))
    outs = pl.pallas_call(
        body, name="in_proj_bwd_x", grid=(steps,),
        in_specs=[wide, _resident(1, D_MODEL), _resident(N_CHIPS, D_MODEL, wn), wide] + _dproj_specs(tm) + _hbm_specs(n_ps),
        out_specs=[wide] + _hbm_specs(n_ps),
        out_shape=[jax.ShapeDtypeStruct((L, D_MODEL), F32)] + [jax.ShapeDtypeStruct(p.shape, p.dtype) for p in pair_sums],
        scratch_shapes=[_chunk_scratch(tm)] + _scatter_scratch(pair_sums),
        compiler_params=_cparams(("arbitrary",), has_side_effects=True),
    )(x, g_pre, w_in_g, d_h1, *dparts, *pair_sums)
    return outs[0], outs[1:]


def _mesh_position():
    x, y, c = lax.axis_index("x"), lax.axis_index("y"), lax.axis_index("c")
    chips = [(1 - x, y), (x, 1 - y), (1 - x, 1 - y)]
    return x, y, c, chips


def _chip_index(cx, cy):
    return 2 * cx + cy


def _hbm_specs(n):
    return [pl.BlockSpec(memory_space=pl.ANY)] * n


def _gather_chips(shards, name):
    n = len(shards)

    def body(*refs):
        gather = _ChipGather(refs[:n], refs[n:2 * n], refs[2 * n:])
        gather.start()
        gather.forward()
        gather.finish()

    return pl.pallas_call(
        body, name=name, in_specs=_hbm_specs(n), out_specs=_hbm_specs(n),
        out_shape=_gather_out_shapes(shards), scratch_shapes=_gather_semaphores(n),
        compiler_params=pltpu.CompilerParams(has_side_effects=True),
    )(*shards)


def _gather_out_shapes(shards):
    return [jax.ShapeDtypeStruct((N_CHIPS,) + s.shape, s.dtype) for s in shards]


def _gather_semaphores(n):
    sem = pltpu.SemaphoreType.DMA
    return [sem((n, 3)), sem((n, 3)), sem((n, 3)), sem((n, 3)), sem((n,)), sem((n,))]


class _ChipGather:
    def __init__(self, ins, outs, sems):
        self.ins, self.outs = ins, outs
        self.send1, self.recv1, self.send2, self.recv2, self.send3, self.recv3 = sems
        self.x, self.y, self.c, self.chips = _mesh_position()
        self.me = _chip_index(self.x, self.y)
        self.sibling = (self.x, self.y, 1 - self.c)

    def _half(self, a, chip, core):
        hr = self.outs[a].shape[1] // 2
        return self.outs[a].at[chip, pl.ds(core * hr, hr)]

    def _own(self, a):
        return pltpu.make_async_remote_copy(
            src_ref=self.ins[a], dst_ref=self.outs[a].at[self.me], send_sem=self.send3.at[a], recv_sem=self.recv3.at[a],
            device_id=self.sibling, device_id_type=MESH)

    def _to_chip(self, a, j):
        hr = self.ins[a].shape[0] // 2
        return pltpu.make_async_remote_copy(
            src_ref=self.ins[a].at[pl.ds(self.c * hr, hr)], dst_ref=self._half(a, self.me, self.c),
            send_sem=self.send1.at[a, j], recv_sem=self.recv1.at[a, j], device_id=(*self.chips[j], self.c), device_id_type=MESH)

    def _from_chip(self, a, j):
        landed = self._half(a, _chip_index(*self.chips[j]), self.c)
        return pltpu.make_async_remote_copy(
            src_ref=landed, dst_ref=landed, send_sem=self.send1.at[a, j], recv_sem=self.recv1.at[a, j],
            device_id=(*self.chips[j], self.c), device_id_type=MESH)

    def _to_sibling(self, a, j, core):
        part = self._half(a, _chip_index(*self.chips[j]), core)
        return pltpu.make_async_remote_copy(
            src_ref=part, dst_ref=part, send_sem=self.send2.at[a, j], recv_sem=self.recv2.at[a, j],
            device_id=self.sibling, device_id_type=MESH)

    def _each(self):
        return [(a, j) for a in range(len(self.ins)) for j in range(3)]

    def start(self):
        for a in range(len(self.ins)):
            self._own(a).start()
        for a, j in self._each():
            self._to_chip(a, j).start()

    def forward(self):
        for a, j in self._each():
            self._from_chip(a, j).wait_recv()
            self._to_sibling(a, j, self.c).start()

    def finish(self):
        for a, j in self._each():
            self._to_sibling(a, j, 1 - self.c).wait_recv()
        for a, j in self._each():
            self._to_chip(a, j).wait_send()
            self._to_sibling(a, j, self.c).wait_send()
        for a in range(len(self.ins)):
            self._own(a).wait()


def _pair_exchange(grads):
    n = len(grads)

    def body(*refs):
        ins, outs = refs[:n], refs[n:2 * n]
        send, recv = refs[2 * n:]
        x, y, c, _ = _mesh_position()
        copies = []
        for a in range(n):
            hr = ins[a].shape[1] // 2
            cp = pltpu.make_async_remote_copy(
                src_ref=ins[a].at[:, pl.ds((1 - c) * hr, hr)], dst_ref=outs[a],
                send_sem=send.at[a], recv_sem=recv.at[a], device_id=(x, y, 1 - c), device_id_type=MESH)
            cp.start()
            copies.append(cp)
        for cp in copies:
            cp.wait()

    sem = pltpu.SemaphoreType.DMA
    return pl.pallas_call(
        body, name="pair_exchange", in_specs=_hbm_specs(n), out_specs=_hbm_specs(n),
        out_shape=[jax.ShapeDtypeStruct((g.shape[0], g.shape[1] // 2, g.shape[2]), g.dtype) for g in grads],
        scratch_shapes=[sem((n,)), sem((n,))],
        compiler_params=pltpu.CompilerParams(has_side_effects=True),
    )(*grads)


def _pair_add(core, grad, other, tr, out_dtype):
    hr = other.shape[1]
    cdim = other.shape[2]
    nb = hr // tr

    def body(core_ref, g_ref, o_ref, out_ref):
        out_ref[...] = (g_ref[...] + o_ref[...]).astype(out_dtype)

    return pl.pallas_call(
        body, name="pair_add",
        grid_spec=pltpu.PrefetchScalarGridSpec(
            num_scalar_prefetch=1, grid=(N_CHIPS, nb),
            in_specs=[pl.BlockSpec((1, tr, cdim), lambda j, i, core_ref: (j, core_ref[0] * nb + i, 0)),
                      pl.BlockSpec((1, tr, cdim), lambda j, i, core_ref: (j, i, 0))],
            out_specs=pl.BlockSpec((1, tr, cdim), lambda j, i, core_ref: (j, i, 0))),
        out_shape=jax.ShapeDtypeStruct(other.shape, out_dtype),
        compiler_params=_cparams(("arbitrary", "arbitrary")),
    )(core, grad, other)


def _scatter_scratch(parts):
    sem = pltpu.SemaphoreType.DMA
    n = len(parts)
    return [sem((n, 3)), sem((n, 3)), sem((n,)), sem((n,))] + [pltpu.VMEM(p.shape[1:], p.dtype) for p in parts]


class _ChipScatter:
    def __init__(self, ins, outs, sems, staged):
        self.ins, self.outs, self.staged = ins, outs, staged
        self.send, self.recv, self.load_sem, self.store_sem = sems
        self.x, self.y, self.c, self.chips = _mesh_position()
        self.me = _chip_index(self.x, self.y)

    def _load(self, a):
        return pltpu.make_async_copy(self.ins[a].at[self.me], self.staged[a], self.load_sem.at[a])

    def _store(self, a):
        return pltpu.make_async_copy(self.staged[a], self.outs[a].at[self.me], self.store_sem.at[a])

    def _to_chip(self, a, j):
        return pltpu.make_async_remote_copy(
            src_ref=self.ins[a].at[_chip_index(*self.chips[j])], dst_ref=self.outs[a].at[self.me],
            send_sem=self.send.at[a, j], recv_sem=self.recv.at[a, j], device_id=(*self.chips[j], self.c), device_id_type=MESH)

    def start(self):
        for a in range(len(self.ins)):
            self._load(a).start()
            for j in range(3):
                self._to_chip(a, j).start()

    def finish(self):
        for a in range(len(self.ins)):
            self._load(a).wait()
            self._store(a).start()
        for a in range(len(self.ins)):
            for j in range(3):
                self._to_chip(a, j).wait()
            self._store(a).wait()


def _chip_add(core, recv, tr):
    hr, cdim = recv.shape[1], recv.shape[2]
    nb = hr // tr

    def body(core_ref, r_ref, out_ref):
        out_ref[...] = ((r_ref[0].astype(F32) + r_ref[1].astype(F32)) + r_ref[2].astype(F32)) + r_ref[3].astype(F32)

    return pl.pallas_call(
        body, name="chip_add",
        grid_spec=pltpu.PrefetchScalarGridSpec(
            num_scalar_prefetch=1, grid=(nb,),
            in_specs=[pl.BlockSpec((N_CHIPS, tr, cdim), lambda i, core_ref: (0, i, 0))],
            out_specs=pl.BlockSpec((tr, cdim), lambda i, core_ref: (core_ref[0] * nb + i, 0))),
        out_shape=jax.ShapeDtypeStruct((2 * hr, cdim), F32),
        compiler_params=_cparams(("arbitrary",)),
    )(core, recv)


def _pair_gather(fulls):
    n = len(fulls)

    def body(*refs):
        outs = refs[n:2 * n]
        send, recv = refs[2 * n:]
        x, y, c, _ = _mesh_position()
        copies = []
        for a in range(n):
            hr = outs[a].shape[0] // 2
            mine = outs[a].at[pl.ds(c * hr, hr)]
            cp = pltpu.make_async_remote_copy(
                src_ref=mine, dst_ref=mine, send_sem=send.at[a], recv_sem=recv.at[a],
                device_id=(x, y, 1 - c), device_id_type=MESH)
            cp.start()
            copies.append(cp)
        for cp in copies:
            cp.wait()

    sem = pltpu.SemaphoreType.DMA
    return pl.pallas_call(
        body, name="pair_gather", in_specs=_hbm_specs(n), out_specs=_hbm_specs(n),
        out_shape=[jax.ShapeDtypeStruct(f.shape, f.dtype) for f in fulls],
        input_output_aliases={a: a for a in range(n)},
        scratch_shapes=[sem((n,)), sem((n,))],
        compiler_params=pltpu.CompilerParams(has_side_effects=True),
    )(*fulls)


def _row_tile(rows):
    if rows <= 512:
        return rows
    for t in (512, 256, 128, 64, 32, 16, 8):
        if rows % t == 0:
            return t
    raise ValueError(rows)


def _pair_sums(core, grads, ici_dtypes):
    others = _pair_exchange(grads)
    return [_pair_add(core, g, o, _row_tile(o.shape[1]), dt) for g, o, dt in zip(grads, others, ici_dtypes)]


def _finish_reduce(core, landed):
    return _pair_gather([_chip_add(core, r, _row_tile(r.shape[1])) for r in landed])


def _adamw(w, g, m, v):
    rows, cols = w.shape
    one_block = rows % 8 != 0 or rows * max(cols, 128) * 4 <= (1 << 20)
    tr = rows if one_block else _row_tile(rows)

    def body(w_ref, g_ref, m_ref, v_ref, d_ref, nm_ref, nv_ref):
        g_ = g_ref[...]
        m_ = ADAM_B1 * m_ref[...] + (1.0 - ADAM_B1) * g_
        v_ = ADAM_B2 * v_ref[...] + (1.0 - ADAM_B2) * (g_ * g_)
        m_hat = m_ / (1.0 - ADAM_B1 ** ADAM_STEP)
        v_hat = v_ / (1.0 - ADAM_B2 ** ADAM_STEP)
        d_ref[...] = -ADAM_LR * (m_hat / (jnp.sqrt(v_hat) + ADAM_EPS) + ADAM_WD * w_ref[...])
        nm_ref[...] = m_
        nv_ref[...] = v_

    spec = pl.BlockSpec((tr, cols), lambda i: (i, 0))
    shp = jax.ShapeDtypeStruct((rows, cols), F32)
    return pl.pallas_call(
        body, name="adamw", grid=(rows // tr,), in_specs=[spec] * 4, out_specs=[spec] * 3,
        out_shape=[shp] * 3, compiler_params=_cparams(("arbitrary",)),
    )(w, g, m, v)


_SMALL = ["norm_pre", "norm_post", "ssm_a_re", "ssm_a_im", "ssm_log_dt", "ssm_b_re", "ssm_b_im",
          "ssm_c_re", "ssm_c_im", "ssm_d", "b_glu", "na_rpb", "ple_norm"]
_BIG = ["w_in", "w_glu", "w_out", "w_ple", "w_ple_gate"]
_WEIGHTS = ["norm_pre", "norm_post", "w_in", "ssm_a_re", "ssm_a_im", "ssm_log_dt", "ssm_b_re", "ssm_b_im",
            "ssm_c_re", "ssm_c_im", "ssm_d", "w_glu", "b_glu", "na_rpb", "w_out", "w_ple", "ple_norm", "w_ple_gate"]
_SMALL_ROWS = 2176


def _pack_small(tensors, tail=None):
    parts = [tensors[n].reshape(-1) for n in _SMALL] + ([] if tail is None else [tail.reshape(-1)])
    flat = jnp.concatenate(parts)
    flat = jnp.pad(flat, (0, _SMALL_ROWS * 128 - flat.shape[0]))
    return flat.reshape(_SMALL_ROWS, 128)


def _unpack_small(packed, shapes):
    flat = packed.reshape(-1)
    out, off = {}, 0
    for n in _SMALL:
        size = int(np.prod(shapes[n]))
        out[n] = flat[off:off + size].reshape(shapes[n])
        off += size
    return out


def _local_grads(x, p, target, wts):
    ssm_names = ["ssm_a_re", "ssm_a_im", "ssm_log_dt", "ssm_b_re", "ssm_b_im", "ssm_c_re", "ssm_c_im", "ssm_d"]
    ssm_params = [wts[n][0] for n in ssm_names]
    blk, blk_vjp = jax.vjp(_ssm_block_params, *ssm_params)
    shard = lambda n: wts[n][0].astype(BF16)
    (m_mat, ws_mat, wot_mat, a16), (w_in_g,) = _ssm_chunk_matrices(blk, [shard("w_in")])
    seq = x.shape[0]
    bias_rows, bias_rows_vjp = jax.vjp(_na_bias_rows, wts["na_rpb"][0])
    bias_tab = _na_bias_table(bias_rows, seq // GRID_W)

    (u_c, z_s, q_t, q, k_t, k, v_t, v, z_n), gathered = _in_proj(
        x, wts["norm_pre"], w_in_g, [shard(n) for n in _BIG if n != "w_in"])
    w_glu, w_out, w_ple_g, w_pg = (gathered[0].reshape(512, 512), gathered[1].reshape(1024, 1024), gathered[2],
                                   gathered[3].reshape(1024, 1024))
    s_in = _block_matmul([(u_c, ws_mat, False)], "ssm_chunk_states")
    s_prev = _ssm_state_scan(s_in, a16)
    y_ssm_c = _block_matmul([(u_c, m_mat, False), (s_prev, wot_mat, True)], "ssm_chunk_out")
    y_na_t = _na_fwd(q_t, k, v_t, bias_tab)
    cat = _branch_fwd(y_ssm_c, z_s, y_na_t, z_n, w_glu, wts["b_glu"])

    (loss, d_h1, d_cat, d_w_out, d_g_post, d_w_ple, d_g_ple, d_w_pg) = _head(
        x, p, target, cat, w_out, wts["norm_post"], w_ple_g, wts["ple_norm"], w_pg)
    dy_c, d_z_s, d_y_na_t, d_y_na, d_z_n, d_w_glu, d_b_glu = _branch_bwd(
        y_ssm_c, z_s, y_na_t, z_n, w_glu, wts["b_glu"], d_cat)
    d_q_t, d_k, d_v, d_bias_tab = _na_bwd(q_t, q, k_t, k, v, bias_tab, y_na_t, d_y_na_t, d_y_na)

    d_prev = _block_matmul([(dy_c, wot_mat, False)], "ssm_bwd_states")
    g_st, d_a16 = _ssm_state_scan_bwd(d_prev, s_prev, a16)
    d_u_c = _block_matmul([(dy_c, m_mat, True), (g_st, ws_mat, True)], "ssm_bwd_in", out_dtype=BF16)
    d_m = _block_matmul_tn(u_c, dy_c, "ssm_grad_m")
    d_ws = _block_matmul_tn(u_c, g_st, "ssm_grad_ws")
    d_wot = _block_matmul_tn(dy_c, s_prev, "ssm_grad_wot")
    d_ssm = blk_vjp(tuple(_ssm_chunk_matrices_bwd(blk, d_m, d_ws, d_wot, d_a16)))
    (d_rpb,) = bias_rows_vjp(_na_bias_table_bwd(d_bias_tab, seq // GRID_W))

    dparts = [d_u_c, d_z_s, d_q_t, d_k, d_v, d_z_n]
    d_w_in, d_g_pre = _in_proj_bwd_w(x, wts["norm_pre"].reshape(D_MODEL, 1), w_in_g, dparts)

    small = {"norm_pre": d_g_pre, "norm_post": d_g_post, "b_glu": d_b_glu, "na_rpb": d_rpb, "ple_norm": d_g_ple}
    for n, g in zip(ssm_names, d_ssm):
        small[n] = g
    big = {"w_in": d_w_in, "w_glu": d_w_glu.reshape(N_CHIPS, 128, 512), "w_out": d_w_out.reshape(N_CHIPS, 256, 1024),
           "w_ple": d_w_ple, "w_ple_gate": d_w_pg.reshape(N_CHIPS, 256, 1024)}
    return loss, small, big, (x, wts["norm_pre"], w_in_g, d_h1, dparts)


def kernel(x, p, norm_pre, norm_post, w_in, ssm_a_re, ssm_a_im, ssm_log_dt, ssm_b_re, ssm_b_im, ssm_c_re, ssm_c_im, ssm_d, w_glu, b_glu, na_rpb, w_out, w_ple, ple_norm, w_ple_gate, loss_target, m_norm_pre, m_norm_post, m_w_in, m_ssm_a_re, m_ssm_a_im, m_ssm_log_dt, m_ssm_b_re, m_ssm_b_im, m_ssm_c_re, m_ssm_c_im, m_ssm_d, m_w_glu, m_b_glu, m_na_rpb, m_w_out, m_w_ple, m_ple_norm, m_w_ple_gate, v_norm_pre, v_norm_post, v_w_in, v_ssm_a_re, v_ssm_a_im, v_ssm_log_dt, v_ssm_b_re, v_ssm_b_im, v_ssm_c_re, v_ssm_c_im, v_ssm_d, v_w_glu, v_b_glu, v_na_rpb, v_w_out, v_w_ple, v_ple_norm, v_w_ple_gate):
    wts = dict(norm_pre=norm_pre, norm_post=norm_post, w_in=w_in, ssm_a_re=ssm_a_re, ssm_a_im=ssm_a_im,
               ssm_log_dt=ssm_log_dt, ssm_b_re=ssm_b_re, ssm_b_im=ssm_b_im, ssm_c_re=ssm_c_re, ssm_c_im=ssm_c_im,
               ssm_d=ssm_d, w_glu=w_glu, b_glu=b_glu, na_rpb=na_rpb, w_out=w_out, w_ple=w_ple, ple_norm=ple_norm,
               w_ple_gate=w_ple_gate)
    mom_m = dict(norm_pre=m_norm_pre, norm_post=m_norm_post, w_in=m_w_in, ssm_a_re=m_ssm_a_re, ssm_a_im=m_ssm_a_im,
                 ssm_log_dt=m_ssm_log_dt, ssm_b_re=m_ssm_b_re, ssm_b_im=m_ssm_b_im, ssm_c_re=m_ssm_c_re,
                 ssm_c_im=m_ssm_c_im, ssm_d=m_ssm_d, w_glu=m_w_glu, b_glu=m_b_glu, na_rpb=m_na_rpb, w_out=m_w_out,
                 w_ple=m_w_ple, ple_norm=m_ple_norm, w_ple_gate=m_w_ple_gate)
    mom_v = dict(norm_pre=v_norm_pre, norm_post=v_norm_post, w_in=v_w_in, ssm_a_re=v_ssm_a_re, ssm_a_im=v_ssm_a_im,
                 ssm_log_dt=v_ssm_log_dt, ssm_b_re=v_ssm_b_re, ssm_b_im=v_ssm_b_im, ssm_c_re=v_ssm_c_re,
                 ssm_c_im=v_ssm_c_im, ssm_d=v_ssm_d, w_glu=v_w_glu, b_glu=v_b_glu, na_rpb=v_na_rpb, w_out=v_w_out,
                 w_ple=v_w_ple, ple_norm=v_ple_norm, w_ple_gate=v_w_ple_gate)

    loss_part, small, big, input_grad_args = _local_grads(x[0], p[0, 0], loss_target[0], wts)

    core = lax.axis_index("c").astype(jnp.int32).reshape(1)
    small_packed = _pack_small(small, tail=loss_part).reshape(N_CHIPS, _SMALL_ROWS // N_CHIPS, 128)
    pair = _pair_sums(core, [big[n] for n in _BIG] + [small_packed], [BF16] * len(_BIG) + [F32])
    grad_x, landed = _in_proj_bwd_x(*input_grad_args, pair)
    reduced = _finish_reduce(core, landed)
    grads = dict(zip(_BIG, reduced[:-1]))
    (small_all,) = _gather_chips([reduced[-1]], "gather_small_grads")
    small_all = small_all.reshape(_SMALL_ROWS, 128)
    loss = small_all.reshape(-1)[sum(int(np.prod(wts[n].shape)) for n in _SMALL)]

    delta, new_m, new_v = {}, {}, {}
    for n in _BIG:
        shp = wts[n].shape
        d_, m_, v_ = _adamw(wts[n][0], grads[n], mom_m[n][0], mom_v[n][0])
        grads[n] = grads[n].reshape(shp)
        delta[n], new_m[n], new_v[n] = d_.reshape(shp), m_.reshape(shp), v_.reshape(shp)
    grads.update(_unpack_small(small_all, {n: wts[n].shape for n in _SMALL}))
    for n in _SMALL:
        shp = wts[n].shape
        swap = shp[-1] < shp[-2]
        view_shape = shp[:-2] + (shp[-1], shp[-2]) if swap else shp
        rows_cols = (int(np.prod(view_shape[:-1])), view_shape[-1])
        view = lambda t: (jnp.swapaxes(t, -1, -2) if swap else t).reshape(rows_cols)
        back = lambda t: jnp.swapaxes(t.reshape(view_shape), -1, -2) if swap else t.reshape(shp)
        d_, m_, v_ = _adamw(*[view(t) for t in (wts[n], grads[n], mom_m[n], mom_v[n])])
        delta[n], new_m[n], new_v[n] = back(d_), back(m_), back(v_)

    return (loss, grad_x[None], *[grads[n] for n in _WEIGHTS], *[delta[n] for n in _WEIGHTS],
            *[new_m[n] for n in _WEIGHTS], *[new_v[n] for n in _WEIGHTS])
```

```python
import math

import jax
import jax.numpy as jnp
import numpy as np
from jax import lax
from jax.experimental import pallas as pl
from jax.experimental.pallas import tpu as pltpu

F32 = jnp.float32
BF16 = jnp.bfloat16

D_MODEL = 1024
D_PLE = 256
GRID_W = 64
D_SSM = 512
SSM_GROUP = 16
N_GROUPS = 32
SSM_STATE = 64
D_NA = 512
NA_HEADS = 8
NA_HEAD_DIM = 64
NA_ROWS = 8
NA_COLS = 16
D_IN_PROJ = 3072
EPS = 1e-6

CHUNK = 16
GROUPS_PER_BLOCK = 8
N_BLOCKS = N_GROUPS // GROUPS_PER_BLOCK
BLOCK_CH = GROUPS_PER_BLOCK * SSM_GROUP
BLOCK_ST = GROUPS_PER_BLOCK * SSM_STATE
CHUNK_W = CHUNK * BLOCK_CH
STATE_W = 4 * BLOCK_ST

N_CHIPS = 4
MESH = pl.DeviceIdType.MESH

ADAM_LR = 0.001
ADAM_B1 = 0.9
ADAM_B2 = 0.999
ADAM_EPS = 1e-08
ADAM_WD = 0.01
ADAM_STEP = 10

VMEM_LIMIT = 52 * 1024 * 1024
HIGHEST = lax.Precision.HIGHEST


def _cparams(sem=None, **kw):
    if sem is not None:
        kw["dimension_semantics"] = sem
    return pltpu.CompilerParams(vmem_limit_bytes=VMEM_LIMIT, **kw)


def _resident(*shape):
    return pl.BlockSpec(shape, lambda *_: (0,) * len(shape), pipeline_mode=pl.Buffered(1))


def _dot(a, b, dims=((1,), (0,))):
    return lax.dot_general(a, b, (dims, ((), ())), preferred_element_type=F32)


def _dot_nt(a, b):
    return _dot(a, b, ((1,), (1,)))


def _dot_tn(a, b):
    return _dot(a, b, ((0,), (0,)))


def _sigmoid(x):
    return 1.0 / (1.0 + jnp.exp(-x))


_GELU_C = math.sqrt(2.0 / math.pi)


def _gelu_parts(x):
    inner = _GELU_C * (x + 0.044715 * (x * x * x))
    t = jnp.tanh(inner)
    return 0.5 * x * (1.0 + t), t


def _gelu_grad(x, t):
    return 0.5 * (1.0 + t) + 0.5 * x * (1.0 - t * t) * (_GELU_C * (1.0 + 3.0 * 0.044715 * x * x))


def _silu_parts(z):
    s = _sigmoid(z)
    return z * s, s


def _silu_grad(z, s):
    return s * (1.0 + z * (1.0 - s))


def _rms(x):
    r = lax.rsqrt(jnp.mean(x * x, axis=-1, keepdims=True) + EPS)
    return x * r, r


def _rms_bwd(dn, n, r):
    return r * (dn - n * jnp.mean(dn * n, axis=-1, keepdims=True))


def _chunk_scratch(tm):
    return pltpu.VMEM((N_BLOCKS, tm, BLOCK_CH), F32)


def _store_chunks(val, scr, c_ref, dtype, row0=0):
    rows = val.shape[0]
    nc, c0 = rows // CHUNK, row0 // CHUNK
    for b in range(N_BLOCKS):
        scr[b, row0:row0 + rows, :] = val[:, b * BLOCK_CH:(b + 1) * BLOCK_CH]
        for j in range(CHUNK):
            c_ref[b, c0:c0 + nc, j * BLOCK_CH:(j + 1) * BLOCK_CH] = scr[b, pl.ds(row0 + j, nc, stride=CHUNK), :].astype(dtype)


def _load_chunks(c_ref, scr):
    nc = scr.shape[1] // CHUNK
    for b in range(N_BLOCKS):
        for j in range(CHUNK):
            scr[b, pl.ds(j, nc, stride=CHUNK), :] = c_ref[b, :, j * BLOCK_CH:(j + 1) * BLOCK_CH].astype(F32)
    return jnp.concatenate([scr[b] for b in range(N_BLOCKS)], axis=1)


def _chunk_spec(tm):
    return pl.BlockSpec((N_BLOCKS, tm // CHUNK, CHUNK_W), lambda i: (0, i, 0))


def _heads_t_spec(tm):
    return pl.BlockSpec((D_NA, tm), lambda i: (0, i))


def _in_proj(x, g_pre, w_in_g, shards, tm=512):
    L = x.shape[0]
    wn = w_in_g.shape[2]
    n_sh = len(shards)
    steps = L // tm

    def body(*refs):
        x_ref, g_ref, w_ref = refs[:3]
        uc_ref, zs_ref, qt_ref, q_ref, kt_ref, k_ref, vt_ref, v_ref, zn_ref = refs[3 + n_sh:12 + n_sh]
        u_scr = refs[12 + 2 * n_sh]
        gather = _ChipGather(refs[3:3 + n_sh], refs[12 + n_sh:12 + 2 * n_sh], refs[13 + 2 * n_sh:])
        step = pl.program_id(0)
        pl.when(step == 0)(gather.start)
        pl.when(step == steps // 2)(gather.forward)
        pl.when(step == steps - 1)(gather.finish)
        halves = [slice(0, tm // 2), slice(tm // 2, tm)]
        hn = [(_rms(x_ref[rows, :])[0] * g_ref[...]).astype(BF16) for rows in halves]
        projs = [jnp.concatenate([_dot(h, w_ref[j]) for j in range(N_CHIPS)], axis=1) for h in hn]
        for rows, proj in zip(halves, projs):
            _store_chunks(proj[:, 0:512], u_scr, uc_ref, BF16, row0=rows.start)
            zs_ref[rows, :] = proj[:, 512:1024]
            q = proj[:, 1024:1536] * (NA_HEAD_DIM ** -0.5)
            for val, t_ref, n_ref in ((q, qt_ref, q_ref), (proj[:, 1536:2048], kt_ref, k_ref), (proj[:, 2048:2560], vt_ref, v_ref)):
                t_ref[:, rows] = val.T.astype(BF16)
                n_ref[rows, :] = val.astype(BF16)
            zn_ref[rows, :] = proj[:, 2560:3072]

    tok = jax.ShapeDtypeStruct((L, 512), F32)
    tr = jax.ShapeDtypeStruct((D_NA, L), BF16)
    hm = jax.ShapeDtypeStruct((L, D_NA), BF16)
    tspec = pl.BlockSpec((tm, 512), lambda i: (i, 0))
    outs = pl.pallas_call(
        body, name="in_proj", grid=(steps,),
        in_specs=[pl.BlockSpec((tm, D_MODEL), lambda i: (i, 0)),
                  _resident(1, D_MODEL), _resident(N_CHIPS, D_MODEL, wn)] + _hbm_specs(n_sh),
        out_specs=[_chunk_spec(tm), tspec] + [_heads_t_spec(tm), tspec] * 3 + [tspec] + _hbm_specs(n_sh),
        out_shape=[jax.ShapeDtypeStruct((N_BLOCKS, L // CHUNK, CHUNK_W), BF16), tok, tr, hm, tr, hm, tr, hm, tok]
        + _gather_out_shapes(shards),
        scratch_shapes=[_chunk_scratch(tm)] + _gather_semaphores(n_sh),
        compiler_params=_cparams(("arbitrary",), has_side_effects=True),
    )(x, g_pre, w_in_g, *shards)
    return outs[:9], outs[9:]


def _ssm_block_params(a_re, a_im, log_dt, b_re, b_im, c_re, c_im, d):
    def lanes(t):
        return t.reshape(2, N_BLOCKS, 1, BLOCK_ST)

    rows = (2, N_BLOCKS, BLOCK_CH, SSM_STATE)
    b_rows = lambda t: t.reshape(2, N_BLOCKS, GROUPS_PER_BLOCK, SSM_STATE, SSM_GROUP).transpose(0, 1, 2, 4, 3).reshape(rows)
    return (lanes(a_re), lanes(a_im), lanes(jnp.broadcast_to(log_dt[..., None], a_re.shape)),
            b_rows(b_re), b_rows(b_im), c_re.reshape(rows), c_im.reshape(rows), d.reshape(N_BLOCKS, 1, BLOCK_CH))


def _ssm_group_mask():
    row_g = lax.broadcasted_iota(jnp.int32, (BLOCK_CH, BLOCK_ST), 0) // SSM_GROUP
    lane_g = lax.broadcasted_iota(jnp.int32, (BLOCK_CH, BLOCK_ST), 1) // SSM_STATE
    return row_g == lane_g


def _ssm_state_select():
    p = lax.broadcasted_iota(jnp.int32, (SSM_STATE, BLOCK_ST), 0)
    lane_p = lax.broadcasted_iota(jnp.int32, (SSM_STATE, BLOCK_ST), 1) % SSM_STATE
    return (p == lane_p).astype(F32)


def _ssm_expand_blocks(compact_refs, full_refs):
    mask, select = _ssm_group_mask(), _ssm_state_select()
    for c_ref, f_ref in zip(compact_refs, full_refs):
        for d in range(2):
            tiled = lax.dot_general(c_ref[d, 0], select, ((((1,), (0,))), ((), ())), precision=HIGHEST,
                                    preferred_element_type=F32)
            f_ref[d, 0] = jnp.where(mask, tiled, 0.0)


def _ssm_collapse_block(t):
    return lax.dot_general(jnp.where(_ssm_group_mask(), t, 0.0), _ssm_state_select(), ((((1,), (1,))), ((), ())),
                           precision=HIGHEST, preferred_element_type=F32)


def _ssm_discretise(ar, ai, ldt):
    dt = jnp.exp(ldt)
    mag = jnp.exp(dt * ar)
    abr = mag * jnp.cos(dt * ai)
    abi = mag * jnp.sin(dt * ai)
    num_re = abr - 1.0
    num_im = abi
    denom = ar * ar + ai * ai
    coef_re = (num_re * ar + num_im * ai) / denom
    coef_im = (num_im * ar - num_re * ai) / denom
    return abr, abi, coef_re, coef_im


_POW_ROWS = 24


def _ssm_fill_powers(ar_ref, ai_ref, ldt_ref, br_ref, bi_ref, pw_ref, bbar_ref):
    for d in range(2):
        abr, abi, cfr, cfi = _ssm_discretise(ar_ref[d, 0], ai_ref[d, 0], ldt_ref[d, 0])
        bbar_ref[d, 0] = cfr * br_ref[d, 0] - cfi * bi_ref[d, 0]
        bbar_ref[d, 1] = cfr * bi_ref[d, 0] + cfi * br_ref[d, 0]
        pr, pi = jnp.ones_like(abr), jnp.zeros_like(abi)
        for t in range(CHUNK + 1):
            pw_ref[d, 0, t:t + 1, :] = pr
            pw_ref[d, 1, t:t + 1, :] = pi
            pr, pi = pr * abr - pi * abi, pr * abi + pi * abr


def _dot_rounded(a, b, dims=((1,), (0,))):
    return _dot(a.astype(BF16), b.astype(BF16), dims)


def _ssm_stack_inputs(d, pw_ref, bbar_ref, xs_ref):
    for t in range(CHUNK):
        pr, pi = pw_ref[d, 0, t:t + 1, :], pw_ref[d, 1, t:t + 1, :]
        xs_ref[0, t * BLOCK_CH:(t + 1) * BLOCK_CH, :] = bbar_ref[d, 0] * pr - bbar_ref[d, 1] * pi
        xs_ref[1, t * BLOCK_CH:(t + 1) * BLOCK_CH, :] = bbar_ref[d, 0] * pi + bbar_ref[d, 1] * pr


def _eye(n):
    return (lax.broadcasted_iota(jnp.int32, (n, n), 0) == lax.broadcasted_iota(jnp.int32, (n, n), 1)).astype(F32)


def _ssm_param_specs():
    vec = pl.BlockSpec((2, 1, 1, BLOCK_ST), lambda b, j: (0, b, 0, 0))
    mat = pl.BlockSpec((2, 1, BLOCK_CH, SSM_STATE), lambda b, j: (0, b, 0, 0))
    return [vec, vec, vec, mat, mat, mat, mat, pl.BlockSpec((1, 1, BLOCK_CH), lambda b, j: (b, 0, 0))]


def _ssm_block_scratch():
    return [pltpu.VMEM((2, 1, BLOCK_CH, BLOCK_ST), F32)] * 4


def _ssm_chunk_matrices(blk, shards):
    n = len(shards)

    def body(*refs):
        ar_ref, ai_ref, ldt_ref = refs[:3]
        d_ref = refs[7]
        m_ref, ws_ref, wot_ref, a16_ref = refs[8 + n:12 + n]
        pw_ref, bbar_ref, lag_ref, xs_ref = refs[12 + 2 * n:16 + 2 * n]
        br_ref, bi_ref, cr_ref, ci_ref = refs[16 + 2 * n:20 + 2 * n]
        gather = _ChipGather(refs[8:8 + n], refs[12 + n:12 + 2 * n], refs[20 + 2 * n:])
        b, j = pl.program_id(0), pl.program_id(1)
        pl.when((b == 0) & (j == 0))(gather.start)
        pl.when((b == N_BLOCKS - 1) & (j == 0))(gather.forward)
        pl.when((b == N_BLOCKS - 1) & (j == CHUNK - 1))(gather.finish)

        @pl.when(j == 0)
        def _():
            _ssm_expand_blocks(refs[3:7], (br_ref, bi_ref, cr_ref, ci_ref))
            _ssm_fill_powers(ar_ref, ai_ref, ldt_ref, br_ref, bi_ref, pw_ref, bbar_ref)
            zero_lag = d_ref[0] * _eye(BLOCK_CH)
            for d in range(2):
                _ssm_stack_inputs(d, pw_ref, bbar_ref, xs_ref)
                taps = (_dot_rounded(xs_ref[0], cr_ref[d, 0], ((1,), (1,)))
                        - _dot_rounded(xs_ref[1], ci_ref[d, 0], ((1,), (1,))))
                zero_lag = zero_lag + taps[0:BLOCK_CH]
                for t in range(1, CHUNK):
                    lag_ref[CHUNK - 1 + t if d == 0 else CHUNK - 1 - t] = taps[t * BLOCK_CH:(t + 1) * BLOCK_CH]
            lag_ref[CHUNK - 1] = zero_lag
            a16_ref[0] = jnp.concatenate([pw_ref[d, ri, CHUNK:CHUNK + 1, :] for d in range(2) for ri in range(2)], axis=1)

        m_ref[0] = jnp.concatenate([lag_ref[jp - j + CHUNK - 1] for jp in range(CHUNK)], axis=1).astype(BF16)

        def power(d, t):
            return pw_ref[d, 0, pl.ds(t, 1), :], pw_ref[d, 1, pl.ds(t, 1), :]

        parts = []
        for d, t in ((0, CHUNK - 1 - j), (1, j)):
            pr, pi = power(d, t)
            parts += [bbar_ref[d, 0] * pr - bbar_ref[d, 1] * pi, bbar_ref[d, 0] * pi + bbar_ref[d, 1] * pr]
        ws_ref[0] = jnp.concatenate(parts, axis=1).astype(BF16)
        parts = []
        for d, t in ((0, j + 1), (1, CHUNK - j)):
            pr, pi = power(d, t)
            parts += [cr_ref[d, 0] * pr - ci_ref[d, 0] * pi, -cr_ref[d, 0] * pi - ci_ref[d, 0] * pr]
        wot_ref[0] = jnp.concatenate(parts, axis=1).astype(BF16)

    row = pl.BlockSpec((1, BLOCK_CH, CHUNK_W), lambda b, j: (b, j, 0))
    mat = jax.ShapeDtypeStruct((N_BLOCKS, CHUNK_W, CHUNK_W), BF16)
    outs = pl.pallas_call(
        body, name="ssm_chunk_matrices", grid=(N_BLOCKS, CHUNK),
        in_specs=_ssm_param_specs() + _hbm_specs(n),
        out_specs=[row, row, row, pl.BlockSpec((1, 1, STATE_W), lambda b, j: (b, 0, 0))] + _hbm_specs(n),
        out_shape=[mat, mat, mat, jax.ShapeDtypeStruct((N_BLOCKS, 1, STATE_W), F32)] + _gather_out_shapes(shards),
        scratch_shapes=[pltpu.VMEM((2, 2, _POW_ROWS, BLOCK_ST), F32), pltpu.VMEM((2, 2, BLOCK_CH, BLOCK_ST), F32),
                        pltpu.VMEM((2 * CHUNK, BLOCK_CH, BLOCK_CH), F32), pltpu.VMEM((2, CHUNK_W, BLOCK_ST), F32)]
        + _ssm_block_scratch() + _gather_semaphores(n),
        compiler_params=_cparams(("arbitrary", "arbitrary"), has_side_effects=True),
    )(*blk, *shards)
    return outs[:4], outs[4:]


def _ssm_chunk_matrices_bwd(blk, d_m, d_ws, d_wot, d_a16):
    def body(ar_ref, ai_ref, ldt_ref, brc_ref, bic_ref, crc_ref, cic_ref, d_ref, dm_ref, dws_ref, dwot_ref, da16_ref,
             dar_ref, dai_ref, dldt_ref, dbr_ref, dbi_ref, dcr_ref, dci_ref, dd_ref,
             pw_ref, bbar_ref, dlag_ref, dbbar_ref, dc_ref, dpw_ref, xs_ref, dts_ref, br_ref, bi_ref, cr_ref, ci_ref):
        j = pl.program_id(1)
        w = BLOCK_ST

        @pl.when(j == 0)
        def _():
            _ssm_expand_blocks((brc_ref, bic_ref, crc_ref, cic_ref), (br_ref, bi_ref, cr_ref, ci_ref))
            _ssm_fill_powers(ar_ref, ai_ref, ldt_ref, br_ref, bi_ref, pw_ref, bbar_ref)
            for r in (dlag_ref, dbbar_ref, dc_ref, dpw_ref):
                r[...] = jnp.zeros_like(r)

        def fold(t):
            return jnp.sum(t.reshape(BLOCK_CH // 8, 8, w), axis=0)

        def d_power(d, ri, t):
            return jnp.sum(dpw_ref[d, ri, t], axis=0, keepdims=True)

        def x_chain(d, t, dxr, dxi):
            pr, pi = pw_ref[d, 0, pl.ds(t, 1), :], pw_ref[d, 1, pl.ds(t, 1), :]
            bbr, bbi = bbar_ref[d, 0], bbar_ref[d, 1]
            dbbar_ref[d, 0] += dxr * pr + dxi * pi
            dbbar_ref[d, 1] += dxi * pr - dxr * pi
            dpw_ref[d, 0, t] += fold(dxr * bbr + dxi * bbi)
            dpw_ref[d, 1, t] += fold(dxi * bbr - dxr * bbi)

        def z_chain(d, t, dzr, dzi):
            pr, pi = pw_ref[d, 0, pl.ds(t, 1), :], pw_ref[d, 1, pl.ds(t, 1), :]
            c_r, c_i = cr_ref[d, 0], ci_ref[d, 0]
            dc_ref[d, 0] += dzr * pr - dzi * pi
            dc_ref[d, 1] += -dzr * pi - dzi * pr
            dpw_ref[d, 0, t] += fold(dzr * c_r - dzi * c_i)
            dpw_ref[d, 1, t] += fold(-dzr * c_i - dzi * c_r)

        for jp in range(CHUNK):
            dlag_ref[jp - j + CHUNK - 1] += dm_ref[0, :, jp * BLOCK_CH:(jp + 1) * BLOCK_CH].astype(F32)
        quarter = lambda ref, i: ref[0, :, i * w:(i + 1) * w].astype(F32)
        x_chain(0, CHUNK - 1 - j, quarter(dws_ref, 0), quarter(dws_ref, 1))
        x_chain(1, j, quarter(dws_ref, 2), quarter(dws_ref, 3))
        z_chain(0, j + 1, quarter(dwot_ref, 0), quarter(dwot_ref, 1))
        z_chain(1, CHUNK - j, quarter(dwot_ref, 2), quarter(dwot_ref, 3))

        @pl.when(j == CHUNK - 1)
        def _():
            for d in range(2):
                _ssm_stack_inputs(d, pw_ref, bbar_ref, xs_ref)
                for t in range(CHUNK):
                    dts_ref[t * BLOCK_CH:(t + 1) * BLOCK_CH, :] = dlag_ref[CHUNK - 1 + t if d == 0 else CHUNK - 1 - t]
                d_taps = dts_ref[...]
                dc_ref[d, 0] += _dot_rounded(d_taps, xs_ref[0], ((0,), (0,)))
                dc_ref[d, 1] -= _dot_rounded(d_taps, xs_ref[1], ((0,), (0,)))
                xs_ref[0] = _dot_rounded(d_taps, cr_ref[d, 0])
                xs_ref[1] = -_dot_rounded(d_taps, ci_ref[d, 0])
                for t in range(CHUNK):
                    rows = slice(t * BLOCK_CH, (t + 1) * BLOCK_CH)
                    x_chain(d, t, xs_ref[0, rows, :], xs_ref[1, rows, :])
            dd_ref[0] = jnp.sum(dlag_ref[CHUNK - 1] * _eye(BLOCK_CH), axis=0, keepdims=True)
            for d in range(2):
                (abr, abi, cfr, cfi), disc_vjp = jax.vjp(_ssm_discretise, ar_ref[d, 0], ai_ref[d, 0], ldt_ref[d, 0])
                dpr = d_power(d, 0, CHUNK) + da16_ref[0, :, 2 * d * w:(2 * d + 1) * w]
                dpi = d_power(d, 1, CHUNK) + da16_ref[0, :, (2 * d + 1) * w:(2 * d + 2) * w]
                dabr, dabi = jnp.zeros_like(abr), jnp.zeros_like(abi)
                for t in range(CHUNK, 0, -1):
                    qr, qi = pw_ref[d, 0, t - 1:t, :], pw_ref[d, 1, t - 1:t, :]
                    dabr = dabr + dpr * qr + dpi * qi
                    dabi = dabi + dpi * qr - dpr * qi
                    dpr, dpi = (dpr * abr + dpi * abi + d_power(d, 0, t - 1),
                                dpi * abr - dpr * abi + d_power(d, 1, t - 1))
                dbbr, dbbi = dbbar_ref[d, 0], dbbar_ref[d, 1]
                b_r, b_i = br_ref[d, 0], bi_ref[d, 0]
                dbr_ref[d, 0] = _ssm_collapse_block(cfr * dbbr + cfi * dbbi)
                dbi_ref[d, 0] = _ssm_collapse_block(cfr * dbbi - cfi * dbbr)
                dcfr = jnp.sum(b_r * dbbr + b_i * dbbi, axis=0, keepdims=True)
                dcfi = jnp.sum(b_r * dbbi - b_i * dbbr, axis=0, keepdims=True)
                dar_ref[d, 0], dai_ref[d, 0], dldt_ref[d, 0] = disc_vjp((dabr, dabi, dcfr, dcfi))
                dcr_ref[d, 0] = _ssm_collapse_block(dc_ref[d, 0])
                dci_ref[d, 0] = _ssm_collapse_block(dc_ref[d, 1])

    row = pl.BlockSpec((1, BLOCK_CH, CHUNK_W), lambda b, j: (b, j, 0))
    specs = _ssm_param_specs()
    acc = lambda *s: pltpu.VMEM(s, F32)
    return pl.pallas_call(
        body, name="ssm_chunk_matrices_bwd", grid=(N_BLOCKS, CHUNK),
        in_specs=specs + [row, row, row, pl.BlockSpec((1, 1, STATE_W), lambda b, j: (b, 0, 0))],
        out_specs=specs,
        out_shape=[jax.ShapeDtypeStruct(t.shape, F32) for t in blk],
        scratch_shapes=[acc(2, 2, _POW_ROWS, BLOCK_ST), acc(2, 2, BLOCK_CH, BLOCK_ST), acc(2 * CHUNK, BLOCK_CH, BLOCK_CH),
                        acc(2, 2, BLOCK_CH, BLOCK_ST), acc(2, 2, BLOCK_CH, BLOCK_ST), acc(2, 2, CHUNK + 1, 8, BLOCK_ST),
                        acc(2, CHUNK_W, BLOCK_ST), acc(CHUNK_W, BLOCK_CH)] + _ssm_block_scratch(),
        compiler_params=_cparams(("arbitrary", "arbitrary")),
    )(*blk, d_m, d_ws, d_wot, d_a16)


def _block_matmul(terms, name, out_dtype=F32, tn=1024):
    nc = terms[0][0].shape[1]
    n_out = terms[0][1].shape[1] if terms[0][2] else terms[0][1].shape[2]
    flags = [t[2] for t in terms]

    def body(*refs):
        out_ref = refs[-1]
        acc = None
        for t, transposed in enumerate(flags):
            a = refs[2 * t][0].astype(BF16)
            w = refs[2 * t + 1][0]
            part = _dot_nt(a, w) if transposed else _dot(a, w)
            acc = part if acc is None else acc + part
        out_ref[0] = acc.astype(out_dtype)

    in_specs, args = [], []
    for a, w, transposed in terms:
        k = a.shape[2]
        in_specs.append(pl.BlockSpec((1, nc, k), lambda b, n: (b, 0, 0)))
        if transposed:
            in_specs.append(pl.BlockSpec((1, tn, k), lambda b, n: (b, n, 0)))
        else:
            in_specs.append(pl.BlockSpec((1, k, tn), lambda b, n: (b, 0, n)))
        args += [a, w]
    return pl.pallas_call(
        body, name=name, grid=(N_BLOCKS, n_out // tn), in_specs=in_specs,
        out_specs=pl.BlockSpec((1, nc, tn), lambda b, n: (b, 0, n)),
        out_shape=jax.ShapeDtypeStruct((N_BLOCKS, nc, n_out), out_dtype),
        compiler_params=_cparams(("arbitrary", "arbitrary")),
    )(*args)


def _block_matmul_tn(a, b, name, tile=1024):
    nc, m = a.shape[1], a.shape[2]
    n = b.shape[2]

    def body(a_ref, b_ref, out_ref):
        a_t = a_ref[0].astype(BF16)
        for j in range(n // tile):
            cols = slice(j * tile, (j + 1) * tile)
            out_ref[0, :, cols] = _dot_tn(a_t, b_ref[0, :, cols].astype(BF16)).astype(BF16)

    return pl.pallas_call(
        body, name=name, grid=(N_BLOCKS, m // tile),
        in_specs=[pl.BlockSpec((1, nc, tile), lambda blk, i: (blk, 0, i)),
                  pl.BlockSpec((1, nc, n), lambda blk, i: (blk, 0, 0))],
        out_specs=pl.BlockSpec((1, tile, n), lambda blk, i: (blk, i, 0)),
        out_shape=jax.ShapeDtypeStruct((N_BLOCKS, m, n), BF16),
        compiler_params=_cparams(("arbitrary", "arbitrary")),
    )(a, b)


def _cmul(ar, ai, xr, xi):
    return ar * xr - ai * xi, ar * xi + ai * xr


def _cmul_conj(ar, ai, xr, xi):
    return ar * xr + ai * xi, ar * xi - ai * xr


_SCAN_UNROLL = 8


def _ssm_state_scan(s_in, a16):
    nc = s_in.shape[1]
    w = BLOCK_ST

    def body(sin_ref, a_ref, out_ref):
        a = a_ref[0]
        afr, afi, abr, abi = a[:, 0:w], a[:, w:2 * w], a[:, 2 * w:3 * w], a[:, 3 * w:4 * w]

        def step(c, carry):
            fr, fi, br, bi = carry
            cb = nc - 1 - c
            out_ref[0, pl.ds(c, 1), 0:w] = fr
            out_ref[0, pl.ds(c, 1), w:2 * w] = fi
            out_ref[0, pl.ds(cb, 1), 2 * w:3 * w] = br
            out_ref[0, pl.ds(cb, 1), 3 * w:4 * w] = bi
            nfr, nfi = _cmul(afr, afi, fr, fi)
            nbr, nbi = _cmul(abr, abi, br, bi)
            return (nfr + sin_ref[0, pl.ds(c, 1), 0:w], nfi + sin_ref[0, pl.ds(c, 1), w:2 * w],
                    nbr + sin_ref[0, pl.ds(cb, 1), 2 * w:3 * w], nbi + sin_ref[0, pl.ds(cb, 1), 3 * w:4 * w])

        def steps(i, carry):
            for k in range(_SCAN_UNROLL):
                carry = step(i * _SCAN_UNROLL + k, carry)
            return carry

        z = jnp.zeros((1, w), F32)
        lax.fori_loop(0, nc // _SCAN_UNROLL, steps, (z, z, z, z))

    spec = pl.BlockSpec((1, nc, STATE_W), lambda b: (b, 0, 0))
    return pl.pallas_call(
        body, name="ssm_state_scan", grid=(N_BLOCKS,),
        in_specs=[spec, pl.BlockSpec((1, 1, STATE_W), lambda b: (b, 0, 0))],
        out_specs=spec, out_shape=jax.ShapeDtypeStruct(s_in.shape, F32),
        compiler_params=_cparams(("arbitrary",)),
    )(s_in, a16)


def _ssm_state_scan_bwd(d_prev, s_prev, a16):
    nc = d_prev.shape[1]
    w = BLOCK_ST

    def body(dp_ref, sp_ref, a_ref, g_ref, da_ref):
        a = a_ref[0]
        afr, afi, abr, abi = a[:, 0:w], a[:, w:2 * w], a[:, 2 * w:3 * w], a[:, 3 * w:4 * w]

        def step(i, carry):
            gfr, gfi, gbr, gbi, dafr, dafi, dabr, dabi = carry
            cf = nc - 1 - i
            cb = i
            g_ref[0, pl.ds(cf, 1), 0:w] = gfr
            g_ref[0, pl.ds(cf, 1), w:2 * w] = gfi
            g_ref[0, pl.ds(cb, 1), 2 * w:3 * w] = gbr
            g_ref[0, pl.ds(cb, 1), 3 * w:4 * w] = gbi
            sfr, sfi = sp_ref[0, pl.ds(cf, 1), 0:w], sp_ref[0, pl.ds(cf, 1), w:2 * w]
            sbr, sbi = sp_ref[0, pl.ds(cb, 1), 2 * w:3 * w], sp_ref[0, pl.ds(cb, 1), 3 * w:4 * w]
            dafr = dafr + gfr * sfr + gfi * sfi
            dafi = dafi + gfi * sfr - gfr * sfi
            dabr = dabr + gbr * sbr + gbi * sbi
            dabi = dabi + gbi * sbr - gbr * sbi
            nfr, nfi = _cmul_conj(afr, afi, gfr, gfi)
            nbr, nbi = _cmul_conj(abr, abi, gbr, gbi)
            return (nfr + dp_ref[0, pl.ds(cf, 1), 0:w], nfi + dp_ref[0, pl.ds(cf, 1), w:2 * w],
                    nbr + dp_ref[0, pl.ds(cb, 1), 2 * w:3 * w], nbi + dp_ref[0, pl.ds(cb, 1), 3 * w:4 * w],
                    dafr, dafi, dabr, dabi)

        def steps(i, carry):
            for k in range(_SCAN_UNROLL):
                carry = step(i * _SCAN_UNROLL + k, carry)
            return carry

        z = jnp.zeros((1, w), F32)
        res = lax.fori_loop(0, nc // _SCAN_UNROLL, steps, (z,) * 8)
        da_ref[0] = jnp.concatenate(res[4:], axis=1)

    spec = pl.BlockSpec((1, nc, STATE_W), lambda b: (b, 0, 0))
    aspec = pl.BlockSpec((1, 1, STATE_W), lambda b: (b, 0, 0))
    return pl.pallas_call(
        body, name="ssm_state_scan_bwd", grid=(N_BLOCKS,),
        in_specs=[spec, spec, aspec], out_specs=[spec, aspec],
        out_shape=[jax.ShapeDtypeStruct(d_prev.shape, F32), jax.ShapeDtypeStruct((N_BLOCKS, 1, STATE_W), F32)],
        compiler_params=_cparams(("arbitrary",)),
    )(d_prev, s_prev, a16)


NA_PAIR = 2 * GRID_W
NA_WIN_ROWS = NA_ROWS + 2
NA_WIN = NA_WIN_ROWS * GRID_W
NA_PAIRS_PER_STEP = 16
NA_CASES = 5
NA_MASKED = -1e30


def _na_pair_window(m, rows):
    rs0 = jnp.clip(2 * m - NA_ROWS // 2, 0, rows - NA_ROWS)
    ws = jnp.minimum(rs0, rows - NA_WIN_ROWS)
    last = rows // 2 - 1
    case = jnp.where(m == 0, 0, jnp.where(m == 1, 1, jnp.where(m == last - 1, 3, jnp.where(m == last, 4, 2))))
    return ws, case


def _na_row_offsets(rows):
    last = rows // 2 - 1
    geom = []
    for m in (0, 1, 2, last - 1, last):
        ws = min(max(2 * m - NA_ROWS // 2, 0), rows - NA_ROWS, rows - NA_WIN_ROWS)
        per_case = []
        for i in range(NA_WIN_ROWS):
            pair = []
            for rr in range(2):
                r = 2 * m + rr
                rs = min(max(r - NA_ROWS // 2, 0), rows - NA_ROWS)
                pair.append(ws + i - r + NA_ROWS - 1 if rs <= ws + i < rs + NA_ROWS else None)
            per_case.append(pair)
        geom.append(per_case)
    return geom


def _na_col_select():
    qc = np.arange(NA_PAIR)[None, :] % GRID_W
    kc = np.arange(GRID_W)[:, None]
    dc = np.clip(kc - qc + NA_COLS - 1, 0, 2 * NA_COLS - 2)
    return jnp.asarray((np.arange(2 * NA_COLS - 1)[:, None, None] == dc[None]).astype(np.float32))


def _na_bias_rows(rpb):
    return jnp.einsum("hrd,dkl->hrkl", rpb, _na_col_select(), precision=HIGHEST)


def _na_col_window():
    qc = lax.broadcasted_iota(jnp.int32, (GRID_W, NA_PAIR), 1) % GRID_W
    kc = lax.broadcasted_iota(jnp.int32, (GRID_W, NA_PAIR), 0)
    cs = jnp.clip(qc - NA_COLS // 2, 0, GRID_W - NA_COLS)
    first_row = lax.broadcasted_iota(jnp.int32, (GRID_W, NA_PAIR), 1) < GRID_W
    return (kc >= cs) & (kc < cs + NA_COLS), first_row


def _na_bias_table(bias_rows, rows):
    geom = _na_row_offsets(rows)

    def body(br_ref, tab_ref):
        col_ok, first_row = _na_col_window()
        masked = jnp.full((GRID_W, NA_PAIR), NA_MASKED, F32)
        for case in range(NA_CASES):
            for i in range(NA_WIN_ROWS):
                d0, d1 = geom[case][i]
                t0 = masked if d0 is None else br_ref[0, d0]
                t1 = masked if d1 is None else br_ref[0, d1]
                tile = jnp.where(col_ok, jnp.where(first_row, t0, t1), NA_MASKED)
                tab_ref[0, case, i * GRID_W:(i + 1) * GRID_W, :] = tile

    return pl.pallas_call(
        body, name="na_bias_table", grid=(NA_HEADS,),
        in_specs=[pl.BlockSpec((1, 2 * NA_ROWS - 1, GRID_W, NA_PAIR), lambda h: (h, 0, 0, 0))],
        out_specs=pl.BlockSpec((1, NA_CASES, NA_WIN, NA_PAIR), lambda h: (h, 0, 0, 0)),
        out_shape=jax.ShapeDtypeStruct((NA_HEADS, NA_CASES, NA_WIN, NA_PAIR), F32),
        compiler_params=_cparams(("arbitrary",)),
    )(bias_rows)


def _na_bias_table_bwd(d_tab, rows):
    geom = _na_row_offsets(rows)

    def body(dt_ref, dbr_ref):
        col_ok, first_row = _na_col_window()
        acc = [None] * (2 * NA_ROWS - 1)
        for case in range(NA_CASES):
            for i in range(NA_WIN_ROWS):
                tile = jnp.where(col_ok, dt_ref[0, case, i * GRID_W:(i + 1) * GRID_W, :], 0.0)
                for rr, d in enumerate(geom[case][i]):
                    if d is not None:
                        part = jnp.where(first_row if rr == 0 else ~first_row, tile, 0.0)
                        acc[d] = part if acc[d] is None else acc[d] + part
        for d, a in enumerate(acc):
            dbr_ref[0, d] = jnp.zeros((GRID_W, NA_PAIR), F32) if a is None else a

    return pl.pallas_call(
        body, name="na_bias_table_bwd", grid=(NA_HEADS,),
        in_specs=[pl.BlockSpec((1, NA_CASES, NA_WIN, NA_PAIR), lambda h: (h, 0, 0, 0))],
        out_specs=pl.BlockSpec((1, 2 * NA_ROWS - 1, GRID_W, NA_PAIR), lambda h: (h, 0, 0, 0)),
        out_shape=jax.ShapeDtypeStruct((NA_HEADS, 2 * NA_ROWS - 1, GRID_W, NA_PAIR), F32),
        compiler_params=_cparams(("arbitrary",)),
    )(d_tab)


NA_BLK = 64


def _na_blocks():
    return [slice(i * NA_BLK, (i + 1) * NA_BLK) for i in range(NA_WIN // NA_BLK)]


def _na_softmax(qk, bias_ref, hh, case):
    m = jnp.full((NA_BLK, NA_PAIR), -jnp.inf, F32)
    scores = []
    for blk in _na_blocks():
        s = qk[blk, :] + bias_ref[hh, case, blk, :]
        scores.append(s)
        m = jnp.maximum(m, s)
    m = jnp.max(m, axis=0, keepdims=True)
    l = jnp.zeros((NA_BLK, NA_PAIR), F32)
    exps = []
    for s in scores:
        e = jnp.exp(s - m)
        exps.append(e)
        l = l + e
    return exps, jnp.sum(l, axis=0, keepdims=True)


def _na_units(step, rows):
    units = []
    for pp in range(NA_PAIRS_PER_STEP):
        ws, case = _na_pair_window(step * NA_PAIRS_PER_STEP + pp, rows)
        win = pl.ds(pl.multiple_of(ws * GRID_W, NA_PAIR), NA_WIN)
        lanes = slice(pp * NA_PAIR, (pp + 1) * NA_PAIR)
        for hh in range(2):
            units.append((pp, hh, case, win, lanes, slice(hh * NA_HEAD_DIM, (hh + 1) * NA_HEAD_DIM)))
    return units


def _na_pipeline(n, before, middle, after, lookahead):
    for u in range(min(lookahead, n)):
        for f in before:
            f(u)
    for u in range(n):
        middle(u)
        if u + lookahead < n:
            for f in before:
                f(u + lookahead)
        for f in after:
            f(u)


def _head_rows(t, hh):
    row_head = lax.broadcasted_iota(jnp.int32, t.shape, 0) // NA_HEAD_DIM
    return jnp.where(row_head == hh, t, jnp.zeros_like(t))


def _heads_block_diag(t):
    lane_head = lax.broadcasted_iota(jnp.int32, t.shape, 1) // NA_HEAD_DIM
    zero = jnp.zeros_like(t)
    return jnp.concatenate([jnp.where(lane_head == 0, t, zero), jnp.where(lane_head == 1, t, zero)], axis=0)


def _na_fwd(q_t, k, v_t, bias_tab):
    L = k.shape[0]
    rows = L // GRID_W
    step_w = NA_PAIRS_PER_STEP * NA_PAIR

    def body(q_ref, k_ref, v_ref, bt_ref, o_ref):
        units = _na_units(pl.program_id(1), rows)
        qk, probs = {}, {}

        def scores(u):
            _, hh, _, win, lanes, _ = units[u]
            qk[u] = _dot(k_ref[win, :], _head_rows(q_ref[:, lanes], hh))

        def softmax(u):
            _, hh, case, _, _, _ = units[u]
            exps, l = _na_softmax(qk.pop(u), bt_ref, hh, case)
            probs[u] = jnp.concatenate([t.astype(BF16) for t in exps], axis=0), l

        def output(u):
            _, _, _, win, lanes, hrows = units[u]
            e, l = probs.pop(u)
            o_ref[hrows, lanes] = _dot(v_ref[hrows, win], e) / l

        _na_pipeline(len(units), [scores], softmax, [output], lookahead=3)

    q_spec = pl.BlockSpec((NA_PAIR, step_w), lambda h, s: (h, s))
    return pl.pallas_call(
        body, name="na_fwd", grid=(NA_HEADS // 2, L // step_w),
        in_specs=[q_spec, pl.BlockSpec((L, NA_PAIR), lambda h, s: (0, h)),
                  pl.BlockSpec((NA_PAIR, L), lambda h, s: (h, 0)),
                  pl.BlockSpec((2, NA_CASES, NA_WIN, NA_PAIR), lambda h, s: (h, 0, 0, 0))],
        out_specs=q_spec,
        out_shape=jax.ShapeDtypeStruct((D_NA, L), F32),
        compiler_params=_cparams(("arbitrary", "arbitrary")),
    )(q_t, k, v_t, bias_tab)


def _na_bwd(q_t, q, k_t, k, v, bias_tab, out_t, d_out_t, d_out):
    L = k.shape[0]
    rows = L // GRID_W
    step_w = NA_PAIRS_PER_STEP * NA_PAIR

    def body(qt_ref, q_ref, kt_ref, k_ref, v_ref, bt_ref, ot_ref, dot_ref, do_ref, dq_ref, dk_ref, dv_ref, dbt_ref):
        @pl.when(pl.program_id(1) == 0)
        def _():
            dk_ref[...] = jnp.zeros_like(dk_ref)
            dv_ref[...] = jnp.zeros_like(dv_ref)
            dbt_ref[...] = jnp.zeros_like(dbt_ref)

        units = _na_units(pl.program_id(1), rows)
        qk, dp, dsb, pb = {}, {}, {}, {}

        def scores(u):
            _, hh, _, win, lanes, _ = units[u]
            qk[u] = _dot(k_ref[win, :], _head_rows(qt_ref[:, lanes], hh))

        def d_probs(u):
            _, hh, _, win, lanes, _ = units[u]
            dp[u] = _dot(v_ref[win, :], _head_rows(dot_ref[:, lanes].astype(BF16), hh))

        def softmax_bwd(u):
            _, hh, case, _, lanes, hrows = units[u]
            exps, l = _na_softmax(qk.pop(u), bt_ref, hh, case)
            inv_l = 1.0 / l
            delta = jnp.sum(dot_ref[hrows, lanes] * ot_ref[hrows, lanes], axis=0, keepdims=True)
            d_p = dp.pop(u)
            ds_blocks, p_blocks = [], []
            for blk, e in zip(_na_blocks(), exps):
                p = e * inv_l
                ds = p * (d_p[blk, :] - delta)
                dbt_ref[hh, case, blk, :] += ds
                ds_blocks.append(ds.astype(BF16))
                p_blocks.append(p.astype(BF16))
            dsb[u] = jnp.concatenate(ds_blocks, axis=0)
            pb[u] = jnp.concatenate(p_blocks, axis=0)

        def d_query(u):
            _, _, _, win, lanes, hrows = units[u]
            dq_ref[hrows, lanes] = _dot(kt_ref[hrows, win], dsb[u]) * (NA_HEAD_DIM ** -0.5)

        def d_keys_values(u):
            pp, hh, _, win, _, _ = units[u]
            if hh == 1:
                tokens = slice(pp * NA_PAIR, (pp + 1) * NA_PAIR)
                dk_ref[win, :] += _dot(jnp.concatenate([dsb.pop(u - 1), dsb.pop(u)], axis=1), _heads_block_diag(q_ref[tokens, :]))
                dv_ref[win, :] += _dot(jnp.concatenate([pb.pop(u - 1), pb.pop(u)], axis=1), _heads_block_diag(do_ref[tokens, :]))

        _na_pipeline(len(units), [scores, d_probs], softmax_bwd, [d_query, d_keys_values], lookahead=2)

    t_tile = pl.BlockSpec((NA_PAIR, step_w), lambda h, s: (h, s))
    tile = pl.BlockSpec((step_w, NA_PAIR), lambda h, s: (s, h))
    t_full = pl.BlockSpec((NA_PAIR, L), lambda h, s: (h, 0))
    full = pl.BlockSpec((L, NA_PAIR), lambda h, s: (0, h))
    bt = pl.BlockSpec((2, NA_CASES, NA_WIN, NA_PAIR), lambda h, s: (h, 0, 0, 0))
    tok = jax.ShapeDtypeStruct((L, D_NA), F32)
    return pl.pallas_call(
        body, name="na_bwd", grid=(NA_HEADS // 2, L // step_w),
        in_specs=[t_tile, tile, t_full, full, full, bt, t_tile, t_tile, tile],
        out_specs=[t_tile, full, full, bt],
        out_shape=[jax.ShapeDtypeStruct((D_NA, L), F32), tok, tok, jax.ShapeDtypeStruct(bias_tab.shape, F32)],
        compiler_params=_cparams(("arbitrary", "arbitrary")),
    )(q_t, q, k_t, k, v, bias_tab, out_t, d_out_t, d_out)


def _branch_fwd_values(ys, zs, yn, zn, wglu, bglu):
    g1, t = _gelu_parts(ys)
    lin = _dot(g1.astype(BF16), wglu) + bglu
    sg = _sigmoid(lin)
    ys2 = g1 * sg
    sz, szs = _silu_parts(zs)
    sn, sns = _silu_parts(zn)
    return g1, t, sg, ys2, sz, szs, sn, sns


def _branch_fwd(y_ssm_c, z_s, y_na_t, z_n, w_glu, b_glu, tm=512):
    L = z_s.shape[0]

    def body(ys_ref, zs_ref, yn_ref, zn_ref, w_ref, b_ref, cat_ref, scr):
        yn = yn_ref[...].T
        g1, t, sg, ys2, sz, szs, sn, sns = _branch_fwd_values(
            _load_chunks(ys_ref, scr), zs_ref[...], yn, zn_ref[...], w_ref[...], b_ref[...])
        cat_ref[:, 0:512] = (ys2 * sz).astype(BF16)
        cat_ref[:, 512:1024] = (yn * sn).astype(BF16)

    tile = pl.BlockSpec((tm, 512), lambda i: (i, 0))
    return pl.pallas_call(
        body, name="branch_fwd", grid=(L // tm,),
        in_specs=[_chunk_spec(tm), tile, _heads_t_spec(tm), tile, pl.BlockSpec((512, 512), lambda i: (0, 0)),
                  pl.BlockSpec((1, 512), lambda i: (0, 0))],
        out_specs=pl.BlockSpec((tm, 1024), lambda i: (i, 0)),
        out_shape=jax.ShapeDtypeStruct((L, 1024), BF16),
        scratch_shapes=[_chunk_scratch(tm)],
        compiler_params=_cparams(("arbitrary",)),
    )(y_ssm_c, z_s, y_na_t, z_n, w_glu, b_glu)


def _branch_bwd(y_ssm_c, z_s, y_na_t, z_n, w_glu, b_glu, d_cat, tm=512):
    L = z_s.shape[0]

    def body(ys_ref, zs_ref, yn_ref, zn_ref, w_ref, b_ref, dc_ref,
             dys_ref, dzs_ref, dynt_ref, dyn_ref, dzn_ref, dw_ref, db_ref, scr):
        @pl.when(pl.program_id(0) == 0)
        def _():
            dw_ref[...] = jnp.zeros_like(dw_ref)
            db_ref[...] = jnp.zeros_like(db_ref)

        ys, zs, yn, zn = _load_chunks(ys_ref, scr), zs_ref[...], yn_ref[...].T, zn_ref[...]
        w = w_ref[...]
        g1, t, sg, ys2, sz, szs, sn, sns = _branch_fwd_values(ys, zs, yn, zn, w, b_ref[...])
        dys3 = dc_ref[:, 0:512]
        dyn2 = dc_ref[:, 512:1024]
        dzs_ref[...] = (dys3 * ys2 * _silu_grad(zs, szs)).astype(BF16)
        dys2 = dys3 * sz
        dlin = dys2 * g1 * sg * (1.0 - sg)
        dlb = dlin.astype(BF16)
        dg1 = dys2 * sg + _dot_nt(dlb, w)
        dw_ref[...] += _dot_tn(g1.astype(BF16), dlb)
        db_ref[...] += jnp.sum(dlin, axis=0, keepdims=True)
        _store_chunks(dg1 * _gelu_grad(ys, t), scr, dys_ref, BF16)
        dyn = dyn2 * sn
        dynt_ref[...] = dyn.T
        dyn_ref[...] = dyn.astype(BF16)
        dzn_ref[...] = (dyn2 * yn * _silu_grad(zn, sns)).astype(BF16)

    tile = pl.BlockSpec((tm, 512), lambda i: (i, 0))
    wspec = pl.BlockSpec((512, 512), lambda i: (0, 0))
    bspec = pl.BlockSpec((1, 512), lambda i: (0, 0))
    tok = jax.ShapeDtypeStruct((L, 512), BF16)
    return pl.pallas_call(
        body, name="branch_bwd", grid=(L // tm,),
        in_specs=[_chunk_spec(tm), tile, _heads_t_spec(tm), tile, wspec, bspec, pl.BlockSpec((tm, 1024), lambda i: (i, 0))],
        out_specs=[_chunk_spec(tm), tile, _heads_t_spec(tm), tile, tile, wspec, bspec],
        out_shape=[jax.ShapeDtypeStruct((N_BLOCKS, L // CHUNK, CHUNK_W), BF16), tok, jax.ShapeDtypeStruct((D_NA, L), F32),
                   tok, tok,
                   jax.ShapeDtypeStruct((512, 512), F32), jax.ShapeDtypeStruct((1, 512), F32)],
        scratch_shapes=[_chunk_scratch(tm)],
        compiler_params=_cparams(("arbitrary",)),
    )(y_ssm_c, z_s, y_na_t, z_n, w_glu, b_glu, d_cat)


def _head(x, p, target, cat, w_out, g_post, w_ple_g, g_ple, w_pg, tm=512):
    L = x.shape[0]
    pw = w_ple_g.shape[2]

    def body(x_ref, p_ref, t_ref, cat_ref, wo_ref, gpo_ref, wp_ref, gpl_ref, wg_ref,
             loss_ref, dh1_ref, dcat_ref, dwo_ref, dgpo_ref, dwp_ref, dgpl_ref, dwg_ref):
        @pl.when(pl.program_id(0) == 0)
        def _():
            for r in (loss_ref, dwo_ref, dgpo_ref, dwp_ref, dgpl_ref, dwg_ref):
                r[...] = jnp.zeros_like(r)

        cat_b = cat_ref[...]
        wo, wg = wo_ref[...], wg_ref[...]
        g_po, g_pl = gpo_ref[...], gpl_ref[...]
        mix = _dot(cat_b, wo)
        p_b = p_ref[...].astype(BF16)
        ep = jnp.concatenate([_dot(p_b, wp_ref[j]) for j in range(N_CHIPS)], axis=1)
        nm, r2 = _rms(mix)
        h1 = x_ref[...] + nm * g_po
        ne, r3 = _rms(ep)
        e = ne * g_pl
        h1_b = h1.astype(BF16)
        gate = _sigmoid(_dot(h1_b, wg))
        h2 = h1 + gate * e
        diff = h2 - t_ref[...]
        loss_ref[...] += (0.5 / D_MODEL) * jnp.sum(diff * diff).reshape(1, 1)

        dh2 = diff * (1.0 / D_MODEL)
        de = dh2 * gate
        dgl = (dh2 * e * gate * (1.0 - gate)).astype(BF16)
        dh1 = dh2 + _dot_nt(dgl, wg)
        dwg_ref[...] += _dot_tn(h1_b, dgl)
        dgpo_ref[...] += jnp.sum(dh1 * nm, axis=0, keepdims=True)
        dmix = _rms_bwd(dh1 * g_po, nm, r2).astype(BF16)
        dcat_ref[...] = _dot_nt(dmix, wo)
        dwo_ref[...] += _dot_tn(cat_b, dmix)
        dh1_ref[...] = dh1
        dgpl_ref[...] += jnp.sum(de * ne, axis=0, keepdims=True)
        dep = _rms_bwd(de * g_pl, ne, r3).astype(BF16)
        for j in range(N_CHIPS):
            dwp_ref[j] += _dot_tn(p_b, dep[:, j * pw:(j + 1) * pw])

    tile = lambda w: pl.BlockSpec((tm, w), lambda i: (i, 0))
    const = _resident
    sds = jax.ShapeDtypeStruct
    return pl.pallas_call(
        body, name="head", grid=(L // tm,),
        in_specs=[tile(D_MODEL), tile(D_PLE), tile(D_MODEL), tile(1024), const(1024, D_MODEL), const(1, D_MODEL),
                  const(N_CHIPS, D_PLE, pw), const(1, D_MODEL), const(D_MODEL, D_MODEL)],
        out_specs=[const(1, 1), tile(D_MODEL), tile(1024), const(1024, D_MODEL), const(1, D_MODEL),
                   const(N_CHIPS, D_PLE, pw), const(1, D_MODEL), const(D_MODEL, D_MODEL)],
        out_shape=[sds((1, 1), F32), sds((L, D_MODEL), F32), sds((L, 1024), F32), sds((1024, D_MODEL), F32),
                   sds((1, D_MODEL), F32), sds((N_CHIPS, D_PLE, pw), F32), sds((1, D_MODEL), F32),
                   sds((D_MODEL, D_MODEL), F32)],
        compiler_params=_cparams(("arbitrary",)),
    )(x, p, target, cat, w_out, g_post, w_ple_g, g_ple, w_pg)


def _dproj_specs(tm):
    tile = pl.BlockSpec((tm, 512), lambda i: (i, 0))
    return [_chunk_spec(tm), tile, _heads_t_spec(tm), tile, tile, tile]


_DPROJ_ORDER = (3, 4, 5, 1, 2, 0)


def _dproj_part(refs, scr, i):
    if i == 0:
        val = _load_chunks(refs[0], scr)
    elif i == 2:
        val = refs[2][...].T
    else:
        val = refs[i][...]
    return val.astype(BF16)


def _dproj_pieces(i, wn):
    lo, hi = 512 * i, 512 * (i + 1)
    pieces = []
    for j in range(N_CHIPS):
        a, b = max(lo, j * wn), min(hi, (j + 1) * wn)
        if a < b:
            pieces.append((j, slice(a - j * wn, b - j * wn), slice(a - lo, b - lo)))
    return pieces


def _in_proj_bwd_w(x, g_col, w_in_g, dparts, tm=512):
    L = x.shape[0]
    wn = D_IN_PROJ // N_CHIPS
    steps = L // tm

    def body(x_ref, g_ref, w_ref, *refs):
        dw_ref, dg_ref, scr = refs[-3], refs[-2], refs[-1]

        @pl.when(pl.program_id(0) == 0)
        def _():
            dw_ref[...] = jnp.zeros_like(dw_ref)

        n, _ = _rms(x_ref[...])
        nb = n.astype(BF16)
        for i in _DPROJ_ORDER:
            part = _dproj_part(refs[:-3], scr, i)
            for j, w_cols, p_cols in _dproj_pieces(i, wn):
                dw_ref[j, :, w_cols] += _dot_tn(nb, part[:, p_cols])

        @pl.when(pl.program_id(0) == steps - 1)
        def _():
            g = g_ref[...]
            dg = jnp.zeros_like(g)
            for j in range(N_CHIPS):
                a = dw_ref[j]
                dg = dg + jnp.sum(a * w_ref[j].astype(F32), axis=1, keepdims=True)
                dw_ref[j] = a * g
            dg_ref[...] = dg

    return pl.pallas_call(
        body, name="in_proj_bwd_w", grid=(steps,),
        in_specs=[pl.BlockSpec((tm, D_MODEL), lambda i: (i, 0)), _resident(D_MODEL, 1), _resident(N_CHIPS, D_MODEL, wn)]
        + _dproj_specs(tm),
        out_specs=[_resident(N_CHIPS, D_MODEL, wn), _resident(D_MODEL, 1)],
        out_shape=[jax.ShapeDtypeStruct((N_CHIPS, D_MODEL, wn), F32), jax.ShapeDtypeStruct((D_MODEL, 1), F32)],
        scratch_shapes=[_chunk_scratch(tm)],
        compiler_params=_cparams(("arbitrary",)),
    )(x, g_col, w_in_g, *dparts)


def _in_proj_bwd_x(x, g_pre, w_in_g, d_h1, dparts, pair_sums, tm=512):
    L = x.shape[0]
    wn = w_in_g.shape[2]
    n_ps = len(pair_sums)
    steps = L // tm

    def body(*refs):
        x_ref, g_ref, w_ref, dh1_ref = refs[:4]
        dparts_refs = refs[4:10]
        dx_ref = refs[10 + n_ps]
        scr = refs[11 + 2 * n_ps]
        scatter = _ChipScatter(refs[10:10 + n_ps], refs[11 + n_ps:11 + 2 * n_ps], refs[12 + 2 * n_ps:16 + 2 * n_ps],
                               refs[16 + 2 * n_ps:])
        pl.when(pl.program_id(0) == 0)(scatter.start)
        pl.when(pl.program_id(0) == steps - 1)(scatter.finish)

        dhn = None
        for i in _DPROJ_ORDER:
            part = _dproj_part(dparts_refs, scr, i)
            for j, w_cols, p_cols in _dproj_pieces(i, wn):
                term = _dot_nt(part[:, p_cols], w_ref[j, :, w_cols])
                dhn = term if dhn is None else dhn + term
        n, r = _rms(x_ref[...])
        dx_ref[...] = dh1_ref[...] + _rms_bwd(dhn * g_ref[...], n, r)

    wide = pl.BlockSpec((tm, D_MODEL), lambda i: (i, 0))
    outs = pl.pallas_call(
        body, name="in_proj_bwd_x", grid=(steps,),
        in_specs=[wide, _resident(1, D_MODEL), _resident(N_CHIPS, D_MODEL, wn), wide] + _dproj_specs(tm) + _hbm_specs(n_ps),
        out_specs=[wide] + _hbm_specs(n_ps),
        out_shape=[jax.ShapeDtypeStruct((L, D_MODEL), F32)] + [jax.ShapeDtypeStruct(p.shape, p.dtype) for p in pair_sums],
        scratch_shapes=[_chunk_scratch(tm)] + _scatter_scratch(pair_sums),
        compiler_params=_cparams(("arbitrary",), has_side_effects=True),
    )(x, g_pre, w_in_g, d_h1, *dparts, *pair_sums)
    return outs[0], outs[1:]


def _mesh_position():
    x, y, c = lax.axis_index("x"), lax.axis_index("y"), lax.axis_index("c")
    chips = [(1 - x, y), (x, 1 - y), (1 - x, 1 - y)]
    return x, y, c, chips


def _chip_index(cx, cy):
    return 2 * cx + cy


def _hbm_specs(n):
    return [pl.BlockSpec(memory_space=pl.ANY)] * n


def _gather_chips(shards, name):
    n = len(shards)

    def body(*refs):
        gather = _ChipGather(refs[:n], refs[n:2 * n], refs[2 * n:])
        gather.start()
        gather.forward()
        gather.finish()

    return pl.pallas_call(
        body, name=name, in_specs=_hbm_specs(n), out_specs=_hbm_specs(n),
        out_shape=_gather_out_shapes(shards), scratch_shapes=_gather_semaphores(n),
        compiler_params=pltpu.CompilerParams(has_side_effects=True),
    )(*shards)


def _gather_out_shapes(shards):
    return [jax.ShapeDtypeStruct((N_CHIPS,) + s.shape, s.dtype) for s in shards]


def _gather_semaphores(n):
    sem = pltpu.SemaphoreType.DMA
    return [sem((n, 3)), sem((n, 3)), sem((n, 3)), sem((n, 3)), sem((n,)), sem((n,))]


class _ChipGather:
    def __init__(self, ins, outs, sems):
        self.ins, self.outs = ins, outs
        self.send1, self.recv1, self.send2, self.recv2, self.send3, self.recv3 = sems
        self.x, self.y, self.c, self.chips = _mesh_position()
        self.me = _chip_index(self.x, self.y)
        self.sibling = (self.x, self.y, 1 - self.c)

    def _half(self, a, chip, core):
        hr = self.outs[a].shape[1] // 2
        return self.outs[a].at[chip, pl.ds(core * hr, hr)]

    def _own(self, a):
        return pltpu.make_async_remote_copy(
            src_ref=self.ins[a], dst_ref=self.outs[a].at[self.me], send_sem=self.send3.at[a], recv_sem=self.recv3.at[a],
            device_id=self.sibling, device_id_type=MESH)

    def _to_chip(self, a, j):
        hr = self.ins[a].shape[0] // 2
        return pltpu.make_async_remote_copy(
            src_ref=self.ins[a].at[pl.ds(self.c * hr, hr)], dst_ref=self._half(a, self.me, self.c),
            send_sem=self.send1.at[a, j], recv_sem=self.recv1.at[a, j], device_id=(*self.chips[j], self.c), device_id_type=MESH)

    def _from_chip(self, a, j):
        landed = self._half(a, _chip_index(*self.chips[j]), self.c)
        return pltpu.make_async_remote_copy(
            src_ref=landed, dst_ref=landed, send_sem=self.send1.at[a, j], recv_sem=self.recv1.at[a, j],
            device_id=(*self.chips[j], self.c), device_id_type=MESH)

    def _to_sibling(self, a, j, core):
        part = self._half(a, _chip_index(*self.chips[j]), core)
        return pltpu.make_async_remote_copy(
            src_ref=part, dst_ref=part, send_sem=self.send2.at[a, j], recv_sem=self.recv2.at[a, j],
            device_id=self.sibling, device_id_type=MESH)

    def _each(self):
        return [(a, j) for a in range(len(self.ins)) for j in range(3)]

    def start(self):
        for a in range(len(self.ins)):
            self._own(a).start()
        for a, j in self._each():
            self._to_chip(a, j).start()

    def forward(self):
        for a, j in self._each():
            self._from_chip(a, j).wait_recv()
            self._to_sibling(a, j, self.c).start()

    def finish(self):
        for a, j in self._each():
            self._to_sibling(a, j, 1 - self.c).wait_recv()
        for a, j in self._each():
            self._to_chip(a, j).wait_send()
            self._to_sibling(a, j, self.c).wait_send()
        for a in range(len(self.ins)):
            self._own(a).wait()


def _pair_exchange(grads):
    n = len(grads)

    def body(*refs):
        ins, outs = refs[:n], refs[n:2 * n]
        send, recv = refs[2 * n:]
        x, y, c, _ = _mesh_position()
        copies = []
        for a in range(n):
            hr = ins[a].shape[1] // 2
            cp = pltpu.make_async_remote_copy(
                src_ref=ins[a].at[:, pl.ds((1 - c) * hr, hr)], dst_ref=outs[a],
                send_sem=send.at[a], recv_sem=recv.at[a], device_id=(x, y, 1 - c), device_id_type=MESH)
            cp.start()
            copies.append(cp)
        for cp in copies:
            cp.wait()

    sem = pltpu.SemaphoreType.DMA
    return pl.pallas_call(
        body, name="pair_exchange", in_specs=_hbm_specs(n), out_specs=_hbm_specs(n),
        out_shape=[jax.ShapeDtypeStruct((g.shape[0], g.shape[1] // 2, g.shape[2]), g.dtype) for g in grads],
        scratch_shapes=[sem((n,)), sem((n,))],
        compiler_params=pltpu.CompilerParams(has_side_effects=True),
    )(*grads)


def _pair_add(core, grad, other, tr, out_dtype):
    hr = other.shape[1]
    cdim = other.shape[2]
    nb = hr // tr

    def body(core_ref, g_ref, o_ref, out_ref):
        out_ref[...] = (g_ref[...] + o_ref[...]).astype(out_dtype)

    return pl.pallas_call(
        body, name="pair_add",
        grid_spec=pltpu.PrefetchScalarGridSpec(
            num_scalar_prefetch=1, grid=(N_CHIPS, nb),
            in_specs=[pl.BlockSpec((1, tr, cdim), lambda j, i, core_ref: (j, core_ref[0] * nb + i, 0)),
                      pl.BlockSpec((1, tr, cdim), lambda j, i, core_ref: (j, i, 0))],
            out_specs=pl.BlockSpec((1, tr, cdim), lambda j, i, core_ref: (j, i, 0))),
        out_shape=jax.ShapeDtypeStruct(other.shape, out_dtype),
        compiler_params=_cparams(("arbitrary", "arbitrary")),
    )(core, grad, other)


def _scatter_scratch(parts):
    sem = pltpu.SemaphoreType.DMA
    n = len(parts)
    return [sem((n, 3)), sem((n, 3)), sem((n,)), sem((n,))] + [pltpu.VMEM(p.shape[1:], p.dtype) for p in parts]


class _ChipScatter:
    def __init__(self, ins, outs, sems, staged):
        self.ins, self.outs, self.staged = ins, outs, staged
        self.send, self.recv, self.load_sem, self.store_sem = sems
        self.x, self.y, self.c, self.chips = _mesh_position()
        self.me = _chip_index(self.x, self.y)

    def _load(self, a):
        return pltpu.make_async_copy(self.ins[a].at[self.me], self.staged[a], self.load_sem.at[a])

    def _store(self, a):
        return pltpu.make_async_copy(self.staged[a], self.outs[a].at[self.me], self.store_sem.at[a])

    def _to_chip(self, a, j):
        return pltpu.make_async_remote_copy(
            src_ref=self.ins[a].at[_chip_index(*self.chips[j])], dst_ref=self.outs[a].at[self.me],
            send_sem=self.send.at[a, j], recv_sem=self.recv.at[a, j], device_id=(*self.chips[j], self.c), device_id_type=MESH)

    def start(self):
        for a in range(len(self.ins)):
            self._load(a).start()
            for j in range(3):
                self._to_chip(a, j).start()

    def finish(self):
        for a in range(len(self.ins)):
            self._load(a).wait()
            self._store(a).start()
        for a in range(len(self.ins)):
            for j in range(3):
                self._to_chip(a, j).wait()
            self._store(a).wait()


def _chip_add(core, recv, tr):
    hr, cdim = recv.shape[1], recv.shape[2]
    nb = hr // tr

    def body(core_ref, r_ref, out_ref):
        out_ref[...] = ((r_ref[0].astype(F32) + r_ref[1].astype(F32)) + r_ref[2].astype(F32)) + r_ref[3].astype(F32)

    return pl.pallas_call(
        body, name="chip_add",
        grid_spec=pltpu.PrefetchScalarGridSpec(
            num_scalar_prefetch=1, grid=(nb,),
            in_specs=[pl.BlockSpec((N_CHIPS, tr, cdim), lambda i, core_ref: (0, i, 0))],
            out_specs=pl.BlockSpec((tr, cdim), lambda i, core_ref: (core_ref[0] * nb + i, 0))),
        out_shape=jax.ShapeDtypeStruct((2 * hr, cdim), F32),
        compiler_params=_cparams(("arbitrary",)),
    )(core, recv)


def _pair_gather(fulls):
    n = len(fulls)

    def body(*refs):
        outs = refs[n:2 * n]
        send, recv = refs[2 * n:]
        x, y, c, _ = _mesh_position()
        copies = []
        for a in range(n):
            hr = outs[a].shape[0] // 2
            mine = outs[a].at[pl.ds(c * hr, hr)]
            cp = pltpu.make_async_remote_copy(
                src_ref=mine, dst_ref=mine, send_sem=send.at[a], recv_sem=recv.at[a],
                device_id=(x, y, 1 - c), device_id_type=MESH)
            cp.start()
            copies.append(cp)
        for cp in copies:
            cp.wait()

    sem = pltpu.SemaphoreType.DMA
    return pl.pallas_call(
        body, name="pair_gather", in_specs=_hbm_specs(n), out_specs=_hbm_specs(n),
        out_shape=[jax.ShapeDtypeStruct(f.shape, f.dtype) for f in fulls],
        input_output_aliases={a: a for a in range(n)},
        scratch_shapes=[sem((n,)), sem((n,))],
        compiler_params=pltpu.CompilerParams(has_side_effects=True),
    )(*fulls)


def _row_tile(rows):
    if rows <= 512:
        return rows
    for t in (512, 256, 128, 64, 32, 16, 8):
        if rows % t == 0:
            return t
    raise ValueError(rows)


def _pair_sums(core, grads, ici_dtypes):
    others = _pair_exchange(grads)
    return [_pair_add(core, g, o, _row_tile(o.shape[1]), dt) for g, o, dt in zip(grads, others, ici_dtypes)]


def _finish_reduce(core, landed):
    return _pair_gather([_chip_add(core, r, _row_tile(r.shape[1])) for r in landed])


def _adamw(w, g, m, v):
    rows, cols = w.shape
    one_block = rows % 8 != 0 or rows * max(cols, 128) * 4 <= (1 << 20)
    tr = rows if one_block else _row_tile(rows)

    def body(w_ref, g_ref, m_ref, v_ref, d_ref, nm_ref, nv_ref):
        g_ = g_ref[...]
        m_ = ADAM_B1 * m_ref[...] + (1.0 - ADAM_B1) * g_
        v_ = ADAM_B2 * v_ref[...] + (1.0 - ADAM_B2) * (g_ * g_)
        m_hat = m_ / (1.0 - ADAM_B1 ** ADAM_STEP)
        v_hat = v_ / (1.0 - ADAM_B2 ** ADAM_STEP)
        d_ref[...] = -ADAM_LR * (m_hat / (jnp.sqrt(v_hat) + ADAM_EPS) + ADAM_WD * w_ref[...])
        nm_ref[...] = m_
        nv_ref[...] = v_

    spec = pl.BlockSpec((tr, cols), lambda i: (i, 0))
    shp = jax.ShapeDtypeStruct((rows, cols), F32)
    return pl.pallas_call(
        body, name="adamw", grid=(rows // tr,), in_specs=[spec] * 4, out_specs=[spec] * 3,
        out_shape=[shp] * 3, compiler_params=_cparams(("arbitrary",)),
    )(w, g, m, v)


_SMALL = ["norm_pre", "norm_post", "ssm_a_re", "ssm_a_im", "ssm_log_dt", "ssm_b_re", "ssm_b_im",
          "ssm_c_re", "ssm_c_im", "ssm_d", "b_glu", "na_rpb", "ple_norm"]
_BIG = ["w_in", "w_glu", "w_out", "w_ple", "w_ple_gate"]
_WEIGHTS = ["norm_pre", "norm_post", "w_in", "ssm_a_re", "ssm_a_im", "ssm_log_dt", "ssm_b_re", "ssm_b_im",
            "ssm_c_re", "ssm_c_im", "ssm_d", "w_glu", "b_glu", "na_rpb", "w_out", "w_ple", "ple_norm", "w_ple_gate"]
_SMALL_ROWS = 2176


def _pack_small(tensors, tail=None):
    parts = [tensors[n].reshape(-1) for n in _SMALL] + ([] if tail is None else [tail.reshape(-1)])
    flat = jnp.concatenate(parts)
    flat = jnp.pad(flat, (0, _SMALL_ROWS * 128 - flat.shape[0]))
    return flat.reshape(_SMALL_ROWS, 128)


def _unpack_small(packed, shapes):
    flat = packed.reshape(-1)
    out, off = {}, 0
    for n in _SMALL:
        size = int(np.prod(shapes[n]))
        out[n] = flat[off:off + size].reshape(shapes[n])
        off += size
    return out


def _local_grads(x, p, target, wts):
    ssm_names = ["ssm_a_re", "ssm_a_im", "ssm_log_dt", "ssm_b_re", "ssm_b_im", "ssm_c_re", "ssm_c_im", "ssm_d"]
    ssm_params = [wts[n][0] for n in ssm_names]
    blk, blk_vjp = jax.vjp(_ssm_block_params, *ssm_params)
    shard = lambda n: wts[n][0].astype(BF16)
    (m_mat, ws_mat, wot_mat, a16), (w_in_g,) = _ssm_chunk_matrices(blk, [shard("w_in")])
    seq = x.shape[0]
    bias_rows, bias_rows_vjp = jax.vjp(_na_bias_rows, wts["na_rpb"][0])
    bias_tab = _na_bias_table(bias_rows, seq // GRID_W)

    (u_c, z_s, q_t, q, k_t, k, v_t, v, z_n), gathered = _in_proj(
        x, wts["norm_pre"], w_in_g, [shard(n) for n in _BIG if n != "w_in"])
    w_glu, w_out, w_ple_g, w_pg = (gathered[0].reshape(512, 512), gathered[1].reshape(1024, 1024), gathered[2],
                                   gathered[3].reshape(1024, 1024))
    s_in = _block_matmul([(u_c, ws_mat, False)], "ssm_chunk_states")
    s_prev = _ssm_state_scan(s_in, a16)
    y_ssm_c = _block_matmul([(u_c, m_mat, False), (s_prev, wot_mat, True)], "ssm_chunk_out")
    y_na_t = _na_fwd(q_t, k, v_t, bias_tab)
    cat = _branch_fwd(y_ssm_c, z_s, y_na_t, z_n, w_glu, wts["b_glu"])

    (loss, d_h1, d_cat, d_w_out, d_g_post, d_w_ple, d_g_ple, d_w_pg) = _head(
        x, p, target, cat, w_out, wts["norm_post"], w_ple_g, wts["ple_norm"], w_pg)
    dy_c, d_z_s, d_y_na_t, d_y_na, d_z_n, d_w_glu, d_b_glu = _branch_bwd(
        y_ssm_c, z_s, y_na_t, z_n, w_glu, wts["b_glu"], d_cat)
    d_q_t, d_k, d_v, d_bias_tab = _na_bwd(q_t, q, k_t, k, v, bias_tab, y_na_t, d_y_na_t, d_y_na)

    d_prev = _block_matmul([(dy_c, wot_mat, False)], "ssm_bwd_states")
    g_st, d_a16 = _ssm_state_scan_bwd(d_prev, s_prev, a16)
    d_u_c = _block_matmul([(dy_c, m_mat, True), (g_st, ws_mat, True)], "ssm_bwd_in", out_dtype=BF16)
    d_m = _block_matmul_tn(u_c, dy_c, "ssm_grad_m")
    d_ws = _block_matmul_tn(u_c, g_st, "ssm_grad_ws")
    d_wot = _block_matmul_tn(dy_c, s_prev, "ssm_grad_wot")
    d_ssm = blk_vjp(tuple(_ssm_chunk_matrices_bwd(blk, d_m, d_ws, d_wot, d_a16)))
    (d_rpb,) = bias_rows_vjp(_na_bias_table_bwd(d_bias_tab, seq // GRID_W))

    dparts = [d_u_c, d_z_s, d_q_t, d_k, d_v, d_z_n]
    d_w_in, d_g_pre = _in_proj_bwd_w(x, wts["norm_pre"].reshape(D_MODEL, 1), w_in_g, dparts)

    small = {"norm_pre": d_g_pre, "norm_post": d_g_post, "b_glu": d_b_glu, "na_rpb": d_rpb, "ple_norm": d_g_ple}
    for n, g in zip(ssm_names, d_ssm):
        small[n] = g
    big = {"w_in": d_w_in, "w_glu": d_w_glu.reshape(N_CHIPS, 128, 512), "w_out": d_w_out.reshape(N_CHIPS, 256, 1024),
           "w_ple": d_w_ple, "w_ple_gate": d_w_pg.reshape(N_CHIPS, 256, 1024)}
    return loss, small, big, (x, wts["norm_pre"], w_in_g, d_h1, dparts)


def kernel(x, p, norm_pre, norm_post, w_in, ssm_a_re, ssm_a_im, ssm_log_dt, ssm_b_re, ssm_b_im, ssm_c_re, ssm_c_im, ssm_d, w_glu, b_glu, na_rpb, w_out, w_ple, ple_norm, w_ple_gate, loss_target, m_norm_pre, m_norm_post, m_w_in, m_ssm_a_re, m_ssm_a_im, m_ssm_log_dt, m_ssm_b_re, m_ssm_b_im, m_ssm_c_re, m_ssm_c_im, m_ssm_d, m_w_glu, m_b_glu, m_na_rpb, m_w_out, m_w_ple, m_ple_norm, m_w_ple_gate, v_norm_pre, v_norm_post, v_w_in, v_ssm_a_re, v_ssm_a_im, v_ssm_log_dt, v_ssm_b_re, v_ssm_b_im, v_ssm_c_re, v_ssm_c_im, v_ssm_d, v_w_glu, v_b_glu, v_na_rpb, v_w_out, v_w_ple, v_ple_norm, v_w_ple_gate):
    wts = dict(norm_pre=norm_pre, norm_post=norm_post, w_in=w_in, ssm_a_re=ssm_a_re, ssm_a_im=ssm_a_im,
               ssm_log_dt=ssm_log_dt, ssm_b_re=ssm_b_re, ssm_b_im=ssm_b_im, ssm_c_re=ssm_c_re, ssm_c_im=ssm_c_im,
               ssm_d=ssm_d, w_glu=w_glu, b_glu=b_glu, na_rpb=na_rpb, w_out=w_out, w_ple=w_ple, ple_norm=ple_norm,
               w_ple_gate=w_ple_gate)
    mom_m = dict(norm_pre=m_norm_pre, norm_post=m_norm_post, w_in=m_w_in, ssm_a_re=m_ssm_a_re, ssm_a_im=m_ssm_a_im,
                 ssm_log_dt=m_ssm_log_dt, ssm_b_re=m_ssm_b_re, ssm_b_im=m_ssm_b_im, ssm_c_re=m_ssm_c_re,
                 ssm_c_im=m_ssm_c_im, ssm_d=m_ssm_d, w_glu=m_w_glu, b_glu=m_b_glu, na_rpb=m_na_rpb, w_out=m_w_out,
                 w_ple=m_w_ple, ple_norm=m_ple_norm, w_ple_gate=m_w_ple_gate)
    mom_v = dict(norm_pre=v_norm_pre, norm_post=v_norm_post, w_in=v_w_in, ssm_a_re=v_ssm_a_re, ssm_a_im=v_ssm_a_im,
                 ssm_log_dt=v_ssm_log_dt, ssm_b_re=v_ssm_b_re, ssm_b_im=v_ssm_b_im, ssm_c_re=v_ssm_c_re,
                 ssm_c_im=v_ssm_c_im, ssm_d=v_ssm_d, w_glu=v_w_glu, b_glu=v_b_glu, na_rpb=v_na_rpb, w_out=v_w_out,
                 w_ple=v_w_ple, ple_norm=v_ple_norm, w_ple_gate=v_w_ple_gate)

    loss_part, small, big, input_grad_args = _local_grads(x[0], p[0, 0], loss_target[0], wts)

    core = lax.axis_index("c").astype(jnp.int32).reshape(1)
    small_packed = _pack_small(small, tail=loss_part).reshape(N_CHIPS, _SMALL_ROWS // N_CHIPS, 128)
    pair = _pair_sums(core, [big[n] for n in _BIG] + [small_packed], [BF16] * len(_BIG) + [F32])
    grad_x, landed = _in_proj_bwd_x(*input_grad_args, pair)
    reduced = _finish_reduce(core, landed)
    grads = dict(zip(_BIG, reduced[:-1]))
    (small_all,) = _gather_chips([reduced[-1]], "gather_small_grads")
    small_all = small_all.reshape(_SMALL_ROWS, 128)
    loss = small_all.reshape(-1)[sum(int(np.prod(wts[n].shape)) for n in _SMALL)]

    delta, new_m, new_v = {}, {}, {}
    for n in _BIG:
        shp = wts[n].shape
        d_, m_, v_ = _adamw(wts[n][0], grads[n], mom_m[n][0], mom_v[n][0])
        grads[n] = grads[n].reshape(shp)
        delta[n], new_m[n], new_v[n] = d_.reshape(shp), m_.reshape(shp), v_.reshape(shp)
    grads.update(_unpack_small(small_all, {n: wts[n].shape for n in _SMALL}))
    for n in _SMALL:
        shp = wts[n].shape
        swap = shp[-1] < shp[-2]
        view_shape = shp[:-2] + (shp[-1], shp[-2]) if swap else shp
        rows_cols = (int(np.prod(view_shape[:-1])), view_shape[-1])
        view = lambda t: (jnp.swapaxes(t, -1, -2) if swap else t).reshape(rows_cols)
        back = lambda t: jnp.swapaxes(t.reshape(view_shape), -1, -2) if swap else t.reshape(shp)
        d_, m_, v_ = _adamw(*[view(t) for t in (wts[n], grads[n], mom_m[n], mom_v[n])])
        delta[n], new_m[n], new_v[n] = back(d_), back(m_), back(v_)

    return (loss, grad_x[None], *[grads[n] for n in _WEIGHTS], *[delta[n] for n in _WEIGHTS],
            *[new_m[n] for n in _WEIGHTS], *[new_v[n] for n in _WEIGHTS])
```

```python
import math

import jax
import jax.numpy as jnp
import numpy as np
from jax import lax
from jax.experimental import pallas as pl
from jax.experimental.pallas import tpu as pltpu

F32 = jnp.float32
BF16 = jnp.bfloat16

D_MODEL = 1024
D_PLE = 256
GRID_W = 64
D_SSM = 512
SSM_GROUP = 16
N_GROUPS = 32
SSM_STATE = 64
D_NA = 512
NA_HEADS = 8
NA_HEAD_DIM = 64
NA_ROWS = 8
NA_COLS = 16
D_IN_PROJ = 3072
EPS = 1e-6

CHUNK = 16
GROUPS_PER_BLOCK = 8
N_BLOCKS = N_GROUPS // GROUPS_PER_BLOCK
BLOCK_CH = GROUPS_PER_BLOCK * SSM_GROUP
BLOCK_ST = GROUPS_PER_BLOCK * SSM_STATE
CHUNK_W = CHUNK * BLOCK_CH
STATE_W = 4 * BLOCK_ST

N_CHIPS = 4
MESH = pl.DeviceIdType.MESH

ADAM_LR = 0.001
ADAM_B1 = 0.9
ADAM_B2 = 0.999
ADAM_EPS = 1e-08
ADAM_WD = 0.01
ADAM_STEP = 10

VMEM_LIMIT = 52 * 1024 * 1024
HIGHEST = lax.Precision.HIGHEST


def _cparams(sem=None, **kw):
    if sem is not None:
        kw["dimension_semantics"] = sem
    return pltpu.CompilerParams(vmem_limit_bytes=VMEM_LIMIT, **kw)


def _resident(*shape):
    return pl.BlockSpec(shape, lambda *_: (0,) * len(shape), pipeline_mode=pl.Buffered(1))


def _dot(a, b, dims=((1,), (0,))):
    return lax.dot_general(a, b, (dims, ((), ())), preferred_element_type=F32)


def _dot_nt(a, b):
    return _dot(a, b, ((1,), (1,)))


def _dot_tn(a, b):
    return _dot(a, b, ((0,), (0,)))


def _sigmoid(x):
    return 1.0 / (1.0 + jnp.exp(-x))


_GELU_C = math.sqrt(2.0 / math.pi)


def _gelu_parts(x):
    inner = _GELU_C * (x + 0.044715 * (x * x * x))
    t = jnp.tanh(inner)
    return 0.5 * x * (1.0 + t), t


def _gelu_grad(x, t):
    return 0.5 * (1.0 + t) + 0.5 * x * (1.0 - t * t) * (_GELU_C * (1.0 + 3.0 * 0.044715 * x * x))


def _silu_parts(z):
    s = _sigmoid(z)
    return z * s, s


def _silu_grad(z, s):
    return s * (1.0 + z * (1.0 - s))


def _rms(x):
    r = lax.rsqrt(jnp.mean(x * x, axis=-1, keepdims=True) + EPS)
    return x * r, r


def _rms_bwd(dn, n, r):
    return r * (dn - n * jnp.mean(dn * n, axis=-1, keepdims=True))


def _chunk_scratch(tm):
    return pltpu.VMEM((N_BLOCKS, tm, BLOCK_CH), F32)


def _store_chunks(val, scr, c_ref, dtype, row0=0):
    rows = val.shape[0]
    nc, c0 = rows // CHUNK, row0 // CHUNK
    for b in range(N_BLOCKS):
        scr[b, row0:row0 + rows, :] = val[:, b * BLOCK_CH:(b + 1) * BLOCK_CH]
        for j in range(CHUNK):
            c_ref[b, c0:c0 + nc, j * BLOCK_CH:(j + 1) * BLOCK_CH] = scr[b, pl.ds(row0 + j, nc, stride=CHUNK), :].astype(dtype)


def _load_chunks(c_ref, scr):
    nc = scr.shape[1] // CHUNK
    for b in range(N_BLOCKS):
        for j in range(CHUNK):
            scr[b, pl.ds(j, nc, stride=CHUNK), :] = c_ref[b, :, j * BLOCK_CH:(j + 1) * BLOCK_CH].astype(F32)
    return jnp.concatenate([scr[b] for b in range(N_BLOCKS)], axis=1)


def _chunk_spec(tm):
    return pl.BlockSpec((N_BLOCKS, tm // CHUNK, CHUNK_W), lambda i: (0, i, 0))


def _heads_t_spec(tm):
    return pl.BlockSpec((D_NA, tm), lambda i: (0, i))


def _in_proj(x, g_pre, w_in_g, shards, tm=512):
    L = x.shape[0]
    wn = w_in_g.shape[2]
    n_sh = len(shards)
    steps = L // tm

    def body(*refs):
        x_ref, g_ref, w_ref = refs[:3]
        uc_ref, zs_ref, qt_ref, q_ref, kt_ref, k_ref, vt_ref, v_ref, zn_ref = refs[3 + n_sh:12 + n_sh]
        u_scr = refs[12 + 2 * n_sh]
        gather = _ChipGather(refs[3:3 + n_sh], refs[12 + n_sh:12 + 2 * n_sh], refs[13 + 2 * n_sh:])
        step = pl.program_id(0)
        pl.when(step == 0)(gather.start)
        pl.when(step == steps // 2)(gather.forward)
        pl.when(step == steps - 1)(gather.finish)
        halves = [slice(0, tm // 2), slice(tm // 2, tm)]
        hn = [(_rms(x_ref[rows, :])[0] * g_ref[...]).astype(BF16) for rows in halves]
        projs = [jnp.concatenate([_dot(h, w_ref[j]) for j in range(N_CHIPS)], axis=1) for h in hn]
        for rows, proj in zip(halves, projs):
            _store_chunks(proj[:, 0:512], u_scr, uc_ref, BF16, row0=rows.start)
            zs_ref[rows, :] = proj[:, 512:1024]
            q = proj[:, 1024:1536] * (NA_HEAD_DIM ** -0.5)
            for val, t_ref, n_ref in ((q, qt_ref, q_ref), (proj[:, 1536:2048], kt_ref, k_ref), (proj[:, 2048:2560], vt_ref, v_ref)):
                t_ref[:, rows] = val.T.astype(BF16)
                n_ref[rows, :] = val.astype(BF16)
            zn_ref[rows, :] = proj[:, 2560:3072]

    tok = jax.ShapeDtypeStruct((L, 512), F32)
    tr = jax.ShapeDtypeStruct((D_NA, L), BF16)
    hm = jax.ShapeDtypeStruct((L, D_NA), BF16)
    tspec = pl.BlockSpec((tm, 512), lambda i: (i, 0))
    outs = pl.pallas_call(
        body, name="in_proj", grid=(steps,),
        in_specs=[pl.BlockSpec((tm, D_MODEL), lambda i: (i, 0)),
                  _resident(1, D_MODEL), _resident(N_CHIPS, D_MODEL, wn)] + _hbm_specs(n_sh),
        out_specs=[_chunk_spec(tm), tspec] + [_heads_t_spec(tm), tspec] * 3 + [tspec] + _hbm_specs(n_sh),
        out_shape=[jax.ShapeDtypeStruct((N_BLOCKS, L // CHUNK, CHUNK_W), BF16), tok, tr, hm, tr, hm, tr, hm, tok]
        + _gather_out_shapes(shards),
        scratch_shapes=[_chunk_scratch(tm)] + _gather_semaphores(n_sh),
        compiler_params=_cparams(("arbitrary",), has_side_effects=True),
    )(x, g_pre, w_in_g, *shards)
    return outs[:9], outs[9:]


def _ssm_block_params(a_re, a_im, log_dt, b_re, b_im, c_re, c_im, d):
    def lanes(t):
        return t.reshape(2, N_BLOCKS, 1, BLOCK_ST)

    rows = (2, N_BLOCKS, BLOCK_CH, SSM_STATE)
    b_rows = lambda t: t.reshape(2, N_BLOCKS, GROUPS_PER_BLOCK, SSM_STATE, SSM_GROUP).transpose(0, 1, 2, 4, 3).reshape(rows)
    return (lanes(a_re), lanes(a_im), lanes(jnp.broadcast_to(log_dt[..., None], a_re.shape)),
            b_rows(b_re), b_rows(b_im), c_re.reshape(rows), c_im.reshape(rows), d.reshape(N_BLOCKS, 1, BLOCK_CH))


def _ssm_group_mask():
    row_g = lax.broadcasted_iota(jnp.int32, (BLOCK_CH, BLOCK_ST), 0) // SSM_GROUP
    lane_g = lax.broadcasted_iota(jnp.int32, (BLOCK_CH, BLOCK_ST), 1) // SSM_STATE
    return row_g == lane_g


def _ssm_state_select():
    p = lax.broadcasted_iota(jnp.int32, (SSM_STATE, BLOCK_ST), 0)
    lane_p = lax.broadcasted_iota(jnp.int32, (SSM_STATE, BLOCK_ST), 1) % SSM_STATE
    return (p == lane_p).astype(F32)


def _ssm_expand_blocks(compact_refs, full_refs):
    mask, select = _ssm_group_mask(), _ssm_state_select()
    for c_ref, f_ref in zip(compact_refs, full_refs):
        for d in range(2):
            tiled = lax.dot_general(c_ref[d, 0], select, ((((1,), (0,))), ((), ())), precision=HIGHEST,
                                    preferred_element_type=F32)
            f_ref[d, 0] = jnp.where(mask, tiled, 0.0)


def _ssm_collapse_block(t):
    return lax.dot_general(jnp.where(_ssm_group_mask(), t, 0.0), _ssm_state_select(), ((((1,), (1,))), ((), ())),
                           precision=HIGHEST, preferred_element_type=F32)


def _ssm_discretise(ar, ai, ldt):
    dt = jnp.exp(ldt)
    mag = jnp.exp(dt * ar)
    abr = mag * jnp.cos(dt * ai)
    abi = mag * jnp.sin(dt * ai)
    num_re = abr - 1.0
    num_im = abi
    denom = ar * ar + ai * ai
    coef_re = (num_re * ar + num_im * ai) / denom
    coef_im = (num_im * ar - num_re * ai) / denom
    return abr, abi, coef_re, coef_im


_POW_ROWS = 24


def _ssm_fill_powers(ar_ref, ai_ref, ldt_ref, br_ref, bi_ref, pw_ref, bbar_ref):
    for d in range(2):
        abr, abi, cfr, cfi = _ssm_discretise(ar_ref[d, 0], ai_ref[d, 0], ldt_ref[d, 0])
        bbar_ref[d, 0] = cfr * br_ref[d, 0] - cfi * bi_ref[d, 0]
        bbar_ref[d, 1] = cfr * bi_ref[d, 0] + cfi * br_ref[d, 0]
        pr, pi = jnp.ones_like(abr), jnp.zeros_like(abi)
        for t in range(CHUNK + 1):
            pw_ref[d, 0, t:t + 1, :] = pr
            pw_ref[d, 1, t:t + 1, :] = pi
            pr, pi = pr * abr - pi * abi, pr * abi + pi * abr


def _dot_rounded(a, b, dims=((1,), (0,))):
    return _dot(a.astype(BF16), b.astype(BF16), dims)


def _ssm_stack_inputs(d, pw_ref, bbar_ref, xs_ref):
    for t in range(CHUNK):
        pr, pi = pw_ref[d, 0, t:t + 1, :], pw_ref[d, 1, t:t + 1, :]
        xs_ref[0, t * BLOCK_CH:(t + 1) * BLOCK_CH, :] = bbar_ref[d, 0] * pr - bbar_ref[d, 1] * pi
        xs_ref[1, t * BLOCK_CH:(t + 1) * BLOCK_CH, :] = bbar_ref[d, 0] * pi + bbar_ref[d, 1] * pr


def _eye(n):
    return (lax.broadcasted_iota(jnp.int32, (n, n), 0) == lax.broadcasted_iota(jnp.int32, (n, n), 1)).astype(F32)


def _ssm_param_specs():
    vec = pl.BlockSpec((2, 1, 1, BLOCK_ST), lambda b, j: (0, b, 0, 0))
    mat = pl.BlockSpec((2, 1, BLOCK_CH, SSM_STATE), lambda b, j: (0, b, 0, 0))
    return [vec, vec, vec, mat, mat, mat, mat, pl.BlockSpec((1, 1, BLOCK_CH), lambda b, j: (b, 0, 0))]


def _ssm_block_scratch():
    return [pltpu.VMEM((2, 1, BLOCK_CH, BLOCK_ST), F32)] * 4


def _ssm_chunk_matrices(blk, shards):
    n = len(shards)

    def body(*refs):
        ar_ref, ai_ref, ldt_ref = refs[:3]
        d_ref = refs[7]
        m_ref, ws_ref, wot_ref, a16_ref = refs[8 + n:12 + n]
        pw_ref, bbar_ref, lag_ref, xs_ref = refs[12 + 2 * n:16 + 2 * n]
        br_ref, bi_ref, cr_ref, ci_ref = refs[16 + 2 * n:20 + 2 * n]
        gather = _ChipGather(refs[8:8 + n], refs[12 + n:12 + 2 * n], refs[20 + 2 * n:])
        b, j = pl.program_id(0), pl.program_id(1)
        pl.when((b == 0) & (j == 0))(gather.start)
        pl.when((b == N_BLOCKS - 1) & (j == 0))(gather.forward)
        pl.when((b == N_BLOCKS - 1) & (j == CHUNK - 1))(gather.finish)

        @pl.when(j == 0)
        def _():
            _ssm_expand_blocks(refs[3:7], (br_ref, bi_ref, cr_ref, ci_ref))
            _ssm_fill_powers(ar_ref, ai_ref, ldt_ref, br_ref, bi_ref, pw_ref, bbar_ref)
            zero_lag = d_ref[0] * _eye(BLOCK_CH)
            for d in range(2):
                _ssm_stack_inputs(d, pw_ref, bbar_ref, xs_ref)
                taps = (_dot_rounded(xs_ref[0], cr_ref[d, 0], ((1,), (1,)))
                        - _dot_rounded(xs_ref[1], ci_ref[d, 0], ((1,), (1,))))
                zero_lag = zero_lag + taps[0:BLOCK_CH]
                for t in range(1, CHUNK):
                    lag_ref[CHUNK - 1 + t if d == 0 else CHUNK - 1 - t] = taps[t * BLOCK_CH:(t + 1) * BLOCK_CH]
            lag_ref[CHUNK - 1] = zero_lag
            a16_ref[0] = jnp.concatenate([pw_ref[d, ri, CHUNK:CHUNK + 1, :] for d in range(2) for ri in range(2)], axis=1)

        m_ref[0] = jnp.concatenate([lag_ref[jp - j + CHUNK - 1] for jp in range(CHUNK)], axis=1).astype(BF16)

        def power(d, t):
            return pw_ref[d, 0, pl.ds(t, 1), :], pw_ref[d, 1, pl.ds(t, 1), :]

        parts = []
        for d, t in ((0, CHUNK - 1 - j), (1, j)):
            pr, pi = power(d, t)
            parts += [bbar_ref[d, 0] * pr - bbar_ref[d, 1] * pi, bbar_ref[d, 0] * pi + bbar_ref[d, 1] * pr]
        ws_ref[0] = jnp.concatenate(parts, axis=1).astype(BF16)
        parts = []
        for d, t in ((0, j + 1), (1, CHUNK - j)):
            pr, pi = power(d, t)
            parts += [cr_ref[d, 0] * pr - ci_ref[d, 0] * pi, -cr_ref[d, 0] * pi - ci_ref[d, 0] * pr]
        wot_ref[0] = jnp.concatenate(parts, axis=1).astype(BF16)

    row = pl.BlockSpec((1, BLOCK_CH, CHUNK_W), lambda b, j: (b, j, 0))
    mat = jax.ShapeDtypeStruct((N_BLOCKS, CHUNK_W, CHUNK_W), BF16)
    outs = pl.pallas_call(
        body, name="ssm_chunk_matrices", grid=(N_BLOCKS, CHUNK),
        in_specs=_ssm_param_specs() + _hbm_specs(n),
        out_specs=[row, row, row, pl.BlockSpec((1, 1, STATE_W), lambda b, j: (b, 0, 0))] + _hbm_specs(n),
        out_shape=[mat, mat, mat, jax.ShapeDtypeStruct((N_BLOCKS, 1, STATE_W), F32)] + _gather_out_shapes(shards),
        scratch_shapes=[pltpu.VMEM((2, 2, _POW_ROWS, BLOCK_ST), F32), pltpu.VMEM((2, 2, BLOCK_CH, BLOCK_ST), F32),
                        pltpu.VMEM((2 * CHUNK, BLOCK_CH, BLOCK_CH), F32), pltpu.VMEM((2, CHUNK_W, BLOCK_ST), F32)]
        + _ssm_block_scratch() + _gather_semaphores(n),
        compiler_params=_cparams(("arbitrary", "arbitrary"), has_side_effects=True),
    )(*blk, *shards)
    return outs[:4], outs[4:]


def _ssm_chunk_matrices_bwd(blk, d_m, d_ws, d_wot, d_a16):
    def body(ar_ref, ai_ref, ldt_ref, brc_ref, bic_ref, crc_ref, cic_ref, d_ref, dm_ref, dws_ref, dwot_ref, da16_ref,
             dar_ref, dai_ref, dldt_ref, dbr_ref, dbi_ref, dcr_ref, dci_ref, dd_ref,
             pw_ref, bbar_ref, dlag_ref, dbbar_ref, dc_ref, dpw_ref, xs_ref, dts_ref, br_ref, bi_ref, cr_ref, ci_ref):
        j = pl.program_id(1)
        w = BLOCK_ST

        @pl.when(j == 0)
        def _():
            _ssm_expand_blocks((brc_ref, bic_ref, crc_ref, cic_ref), (br_ref, bi_ref, cr_ref, ci_ref))
            _ssm_fill_powers(ar_ref, ai_ref, ldt_ref, br_ref, bi_ref, pw_ref, bbar_ref)
            for r in (dlag_ref, dbbar_ref, dc_ref, dpw_ref):
                r[...] = jnp.zeros_like(r)

        def fold(t):
            return jnp.sum(t.reshape(BLOCK_CH // 8, 8, w), axis=0)

        def d_power(d, ri, t):
            return jnp.sum(dpw_ref[d, ri, t], axis=0, keepdims=True)

        def x_chain(d, t, dxr, dxi):
            pr, pi = pw_ref[d, 0, pl.ds(t, 1), :], pw_ref[d, 1, pl.ds(t, 1), :]
            bbr, bbi = bbar_ref[d, 0], bbar_ref[d, 1]
            dbbar_ref[d, 0] += dxr * pr + dxi * pi
            dbbar_ref[d, 1] += dxi * pr - dxr * pi
            dpw_ref[d, 0, t] += fold(dxr * bbr + dxi * bbi)
            dpw_ref[d, 1, t] += fold(dxi * bbr - dxr * bbi)

        def z_chain(d, t, dzr, dzi):
            pr, pi = pw_ref[d, 0, pl.ds(t, 1), :], pw_ref[d, 1, pl.ds(t, 1), :]
            c_r, c_i = cr_ref[d, 0], ci_ref[d, 0]
            dc_ref[d, 0] += dzr * pr - dzi * pi
            dc_ref[d, 1] += -dzr * pi - dzi * pr
            dpw_ref[d, 0, t] += fold(dzr * c_r - dzi * c_i)
            dpw_ref[d, 1, t] += fold(-dzr * c_i - dzi * c_r)

        for jp in range(CHUNK):
            dlag_ref[jp - j + CHUNK - 1] += dm_ref[0, :, jp * BLOCK_CH:(jp + 1) * BLOCK_CH].astype(F32)
        quarter = lambda ref, i: ref[0, :, i * w:(i + 1) * w].astype(F32)
        x_chain(0, CHUNK - 1 - j, quarter(dws_ref, 0), quarter(dws_ref, 1))
        x_chain(1, j, quarter(dws_ref, 2), quarter(dws_ref, 3))
        z_chain(0, j + 1, quarter(dwot_ref, 0), quarter(dwot_ref, 1))
        z_chain(1, CHUNK - j, quarter(dwot_ref, 2), quarter(dwot_ref, 3))

        @pl.when(j == CHUNK - 1)
        def _():
            for d in range(2):
                _ssm_stack_inputs(d, pw_ref, bbar_ref, xs_ref)
                for t in range(CHUNK):
                    dts_ref[t * BLOCK_CH:(t + 1) * BLOCK_CH, :] = dlag_ref[CHUNK - 1 + t if d == 0 else CHUNK - 1 - t]
                d_taps = dts_ref[...]
                dc_ref[d, 0] += _dot_rounded(d_taps, xs_ref[0], ((0,), (0,)))
                dc_ref[d, 1] -= _dot_rounded(d_taps, xs_ref[1], ((0,), (0,)))
                xs_ref[0] = _dot_rounded(d_taps, cr_ref[d, 0])
                xs_ref[1] = -_dot_rounded(d_taps, ci_ref[d, 0])
                for t in range(CHUNK):
                    rows = slice(t * BLOCK_CH, (t + 1) * BLOCK_CH)
                    x_chain(d, t, xs_ref[0, rows, :], xs_ref[1, rows, :])
            dd_ref[0] = jnp.sum(dlag_ref[CHUNK - 1] * _eye(BLOCK_CH), axis=0, keepdims=True)
            for d in range(2):
                (abr, abi, cfr, cfi), disc_vjp = jax.vjp(_ssm_discretise, ar_ref[d, 0], ai_ref[d, 0], ldt_ref[d, 0])
                dpr = d_power(d, 0, CHUNK) + da16_ref[0, :, 2 * d * w:(2 * d + 1) * w]
                dpi = d_power(d, 1, CHUNK) + da16_ref[0, :, (2 * d + 1) * w:(2 * d + 2) * w]
                dabr, dabi = jnp.zeros_like(abr), jnp.zeros_like(abi)
                for t in range(CHUNK, 0, -1):
                    qr, qi = pw_ref[d, 0, t - 1:t, :], pw_ref[d, 1, t - 1:t, :]
                    dabr = dabr + dpr * qr + dpi * qi
                    dabi = dabi + dpi * qr - dpr * qi
                    dpr, dpi = (dpr * abr + dpi * abi + d_power(d, 0, t - 1),
                                dpi * abr - dpr * abi + d_power(d, 1, t - 1))
                dbbr, dbbi = dbbar_ref[d, 0], dbbar_ref[d, 1]
                b_r, b_i = br_ref[d, 0], bi_ref[d, 0]
                dbr_ref[d, 0] = _ssm_collapse_block(cfr * dbbr + cfi * dbbi)
                dbi_ref[d, 0] = _ssm_collapse_block(cfr * dbbi - cfi * dbbr)
                dcfr = jnp.sum(b_r * dbbr + b_i * dbbi, axis=0, keepdims=True)
                dcfi = jnp.sum(b_r * dbbi - b_i * dbbr, axis=0, keepdims=True)
                dar_ref[d, 0], dai_ref[d, 0], dldt_ref[d, 0] = disc_vjp((dabr, dabi, dcfr, dcfi))
                dcr_ref[d, 0] = _ssm_collapse_block(dc_ref[d, 0])
                dci_ref[d, 0] = _ssm_collapse_block(dc_ref[d, 1])

    row = pl.BlockSpec((1, BLOCK_CH, CHUNK_W), lambda b, j: (b, j, 0))
    specs = _ssm_param_specs()
    acc = lambda *s: pltpu.VMEM(s, F32)
    return pl.pallas_call(
        body, name="ssm_chunk_matrices_bwd", grid=(N_BLOCKS, CHUNK),
        in_specs=specs + [row, row, row, pl.BlockSpec((1, 1, STATE_W), lambda b, j: (b, 0, 0))],
        out_specs=specs,
        out_shape=[jax.ShapeDtypeStruct(t.shape, F32) for t in blk],
        scratch_shapes=[acc(2, 2, _POW_ROWS, BLOCK_ST), acc(2, 2, BLOCK_CH, BLOCK_ST), acc(2 * CHUNK, BLOCK_CH, BLOCK_CH),
                        acc(2, 2, BLOCK_CH, BLOCK_ST), acc(2, 2, BLOCK_CH, BLOCK_ST), acc(2, 2, CHUNK + 1, 8, BLOCK_ST),
                        acc(2, CHUNK_W, BLOCK_ST), acc(CHUNK_W, BLOCK_CH)] + _ssm_block_scratch(),
        compiler_params=_cparams(("arbitrary", "arbitrary")),
    )(*blk, d_m, d_ws, d_wot, d_a16)


def _block_matmul(terms, name, out_dtype=F32, tn=1024):
    nc = terms[0][0].shape[1]
    n_out = terms[0][1].shape[1] if terms[0][2] else terms[0][1].shape[2]
    flags = [t[2] for t in terms]
    sub = min(tn, 1024)

    def body(*refs):
        out_ref = refs[-1]
        lhs = [refs[2 * t][0].astype(BF16) for t in range(len(flags))]
        for h in range(tn // sub):
            cols = slice(h * sub, (h + 1) * sub)
            acc = None
            for t, transposed in enumerate(flags):
                w_ref = refs[2 * t + 1]
                part = _dot_nt(lhs[t], w_ref[0, cols, :]) if transposed else _dot(lhs[t], w_ref[0, :, cols])
                acc = part if acc is None else acc + part
            out_ref[0, :, cols] = acc.astype(out_dtype)

    in_specs, args = [], []
    for a, w, transposed in terms:
        k = a.shape[2]
        in_specs.append(pl.BlockSpec((1, nc, k), lambda b, n: (b, 0, 0)))
        if transposed:
            in_specs.append(pl.BlockSpec((1, tn, k), lambda b, n: (b, n, 0)))
        else:
            in_specs.append(pl.BlockSpec((1, k, tn), lambda b, n: (b, 0, n)))
        args += [a, w]
    return pl.pallas_call(
        body, name=name, grid=(N_BLOCKS, n_out // tn), in_specs=in_specs,
        out_specs=pl.BlockSpec((1, nc, tn), lambda b, n: (b, 0, n)),
        out_shape=jax.ShapeDtypeStruct((N_BLOCKS, nc, n_out), out_dtype),
        compiler_params=_cparams(("arbitrary", "arbitrary")),
    )(*args)


def _block_matmul_tn(a, b, name, tile=1024):
    nc, m = a.shape[1], a.shape[2]
    n = b.shape[2]

    def body(a_ref, b_ref, out_ref):
        a_t = a_ref[0].astype(BF16)
        for j in range(n // tile):
            cols = slice(j * tile, (j + 1) * tile)
            out_ref[0, :, cols] = _dot_tn(a_t, b_ref[0, :, cols].astype(BF16)).astype(BF16)

    return pl.pallas_call(
        body, name=name, grid=(N_BLOCKS, m // tile),
        in_specs=[pl.BlockSpec((1, nc, tile), lambda blk, i: (blk, 0, i)),
                  pl.BlockSpec((1, nc, n), lambda blk, i: (blk, 0, 0))],
        out_specs=pl.BlockSpec((1, tile, n), lambda blk, i: (blk, i, 0)),
        out_shape=jax.ShapeDtypeStruct((N_BLOCKS, m, n), BF16),
        compiler_params=_cparams(("arbitrary", "arbitrary")),
    )(a, b)


def _cmul(ar, ai, xr, xi):
    return ar * xr - ai * xi, ar * xi + ai * xr


def _cmul_conj(ar, ai, xr, xi):
    return ar * xr + ai * xi, ar * xi - ai * xr


_SCAN_UNROLL = 8


def _ssm_state_scan(s_in, a16):
    nc = s_in.shape[1]
    w = BLOCK_ST

    def body(sin_ref, a_ref, out_ref):
        a = a_ref[0]
        afr, afi, abr, abi = a[:, 0:w], a[:, w:2 * w], a[:, 2 * w:3 * w], a[:, 3 * w:4 * w]

        def step(c, carry):
            fr, fi, br, bi = carry
            cb = nc - 1 - c
            out_ref[0, pl.ds(c, 1), 0:w] = fr
            out_ref[0, pl.ds(c, 1), w:2 * w] = fi
            out_ref[0, pl.ds(cb, 1), 2 * w:3 * w] = br
            out_ref[0, pl.ds(cb, 1), 3 * w:4 * w] = bi
            nfr, nfi = _cmul(afr, afi, fr, fi)
            nbr, nbi = _cmul(abr, abi, br, bi)
            return (nfr + sin_ref[0, pl.ds(c, 1), 0:w], nfi + sin_ref[0, pl.ds(c, 1), w:2 * w],
                    nbr + sin_ref[0, pl.ds(cb, 1), 2 * w:3 * w], nbi + sin_ref[0, pl.ds(cb, 1), 3 * w:4 * w])

        def steps(i, carry):
            for k in range(_SCAN_UNROLL):
                carry = step(i * _SCAN_UNROLL + k, carry)
            return carry

        z = jnp.zeros((1, w), F32)
        lax.fori_loop(0, nc // _SCAN_UNROLL, steps, (z, z, z, z))

    spec = pl.BlockSpec((1, nc, STATE_W), lambda b: (b, 0, 0))
    return pl.pallas_call(
        body, name="ssm_state_scan", grid=(N_BLOCKS,),
        in_specs=[spec, pl.BlockSpec((1, 1, STATE_W), lambda b: (b, 0, 0))],
        out_specs=spec, out_shape=jax.ShapeDtypeStruct(s_in.shape, F32),
        compiler_params=_cparams(("arbitrary",)),
    )(s_in, a16)


def _ssm_state_scan_bwd(d_prev, s_prev, a16):
    nc = d_prev.shape[1]
    w = BLOCK_ST

    def body(dp_ref, sp_ref, a_ref, g_ref, da_ref):
        a = a_ref[0]
        afr, afi, abr, abi = a[:, 0:w], a[:, w:2 * w], a[:, 2 * w:3 * w], a[:, 3 * w:4 * w]

        def step(i, carry):
            gfr, gfi, gbr, gbi, dafr, dafi, dabr, dabi = carry
            cf = nc - 1 - i
            cb = i
            g_ref[0, pl.ds(cf, 1), 0:w] = gfr
            g_ref[0, pl.ds(cf, 1), w:2 * w] = gfi
            g_ref[0, pl.ds(cb, 1), 2 * w:3 * w] = gbr
            g_ref[0, pl.ds(cb, 1), 3 * w:4 * w] = gbi
            sfr, sfi = sp_ref[0, pl.ds(cf, 1), 0:w], sp_ref[0, pl.ds(cf, 1), w:2 * w]
            sbr, sbi = sp_ref[0, pl.ds(cb, 1), 2 * w:3 * w], sp_ref[0, pl.ds(cb, 1), 3 * w:4 * w]
            dafr = dafr + gfr * sfr + gfi * sfi
            dafi = dafi + gfi * sfr - gfr * sfi
            dabr = dabr + gbr * sbr + gbi * sbi
            dabi = dabi + gbi * sbr - gbr * sbi
            nfr, nfi = _cmul_conj(afr, afi, gfr, gfi)
            nbr, nbi = _cmul_conj(abr, abi, gbr, gbi)
            return (nfr + dp_ref[0, pl.ds(cf, 1), 0:w], nfi + dp_ref[0, pl.ds(cf, 1), w:2 * w],
                    nbr + dp_ref[0, pl.ds(cb, 1), 2 * w:3 * w], nbi + dp_ref[0, pl.ds(cb, 1), 3 * w:4 * w],
                    dafr, dafi, dabr, dabi)

        def steps(i, carry):
            for k in range(_SCAN_UNROLL):
                carry = step(i * _SCAN_UNROLL + k, carry)
            return carry

        z = jnp.zeros((1, w), F32)
        res = lax.fori_loop(0, nc // _SCAN_UNROLL, steps, (z,) * 8)
        da_ref[0] = jnp.concatenate(res[4:], axis=1)

    spec = pl.BlockSpec((1, nc, STATE_W), lambda b: (b, 0, 0))
    aspec = pl.BlockSpec((1, 1, STATE_W), lambda b: (b, 0, 0))
    return pl.pallas_call(
        body, name="ssm_state_scan_bwd", grid=(N_BLOCKS,),
        in_specs=[spec, spec, aspec], out_specs=[spec, aspec],
        out_shape=[jax.ShapeDtypeStruct(d_prev.shape, F32), jax.ShapeDtypeStruct((N_BLOCKS, 1, STATE_W), F32)],
        compiler_params=_cparams(("arbitrary",)),
    )(d_prev, s_prev, a16)


NA_PAIR = 2 * GRID_W
NA_WIN_ROWS = NA_ROWS + 2
NA_WIN = NA_WIN_ROWS * GRID_W
NA_PAIRS_PER_STEP = 16
NA_CASES = 5
NA_MASKED = -1e30


def _na_pair_window(m, rows):
    rs0 = jnp.clip(2 * m - NA_ROWS // 2, 0, rows - NA_ROWS)
    ws = jnp.minimum(rs0, rows - NA_WIN_ROWS)
    last = rows // 2 - 1
    case = jnp.where(m == 0, 0, jnp.where(m == 1, 1, jnp.where(m == last - 1, 3, jnp.where(m == last, 4, 2))))
    return ws, case


def _na_row_offsets(rows):
    last = rows // 2 - 1
    geom = []
    for m in (0, 1, 2, last - 1, last):
        ws = min(max(2 * m - NA_ROWS // 2, 0), rows - NA_ROWS, rows - NA_WIN_ROWS)
        per_case = []
        for i in range(NA_WIN_ROWS):
            pair = []
            for rr in range(2):
                r = 2 * m + rr
                rs = min(max(r - NA_ROWS // 2, 0), rows - NA_ROWS)
                pair.append(ws + i - r + NA_ROWS - 1 if rs <= ws + i < rs + NA_ROWS else None)
            per_case.append(pair)
        geom.append(per_case)
    return geom


def _na_col_select():
    qc = np.arange(NA_PAIR)[None, :] % GRID_W
    kc = np.arange(GRID_W)[:, None]
    dc = np.clip(kc - qc + NA_COLS - 1, 0, 2 * NA_COLS - 2)
    return jnp.asarray((np.arange(2 * NA_COLS - 1)[:, None, None] == dc[None]).astype(np.float32))


def _na_bias_rows(rpb):
    return jnp.einsum("hrd,dkl->hrkl", rpb, _na_col_select(), precision=HIGHEST)


def _na_col_window():
    qc = lax.broadcasted_iota(jnp.int32, (GRID_W, NA_PAIR), 1) % GRID_W
    kc = lax.broadcasted_iota(jnp.int32, (GRID_W, NA_PAIR), 0)
    cs = jnp.clip(qc - NA_COLS // 2, 0, GRID_W - NA_COLS)
    first_row = lax.broadcasted_iota(jnp.int32, (GRID_W, NA_PAIR), 1) < GRID_W
    return (kc >= cs) & (kc < cs + NA_COLS), first_row


def _na_bias_table(bias_rows, rows):
    geom = _na_row_offsets(rows)

    def body(br_ref, tab_ref):
        col_ok, first_row = _na_col_window()
        masked = jnp.full((GRID_W, NA_PAIR), NA_MASKED, F32)
        for case in range(NA_CASES):
            for i in range(NA_WIN_ROWS):
                d0, d1 = geom[case][i]
                t0 = masked if d0 is None else br_ref[0, d0]
                t1 = masked if d1 is None else br_ref[0, d1]
                tile = jnp.where(col_ok, jnp.where(first_row, t0, t1), NA_MASKED)
                tab_ref[0, case, i * GRID_W:(i + 1) * GRID_W, :] = tile

    return pl.pallas_call(
        body, name="na_bias_table", grid=(NA_HEADS,),
        in_specs=[pl.BlockSpec((1, 2 * NA_ROWS - 1, GRID_W, NA_PAIR), lambda h: (h, 0, 0, 0))],
        out_specs=pl.BlockSpec((1, NA_CASES, NA_WIN, NA_PAIR), lambda h: (h, 0, 0, 0)),
        out_shape=jax.ShapeDtypeStruct((NA_HEADS, NA_CASES, NA_WIN, NA_PAIR), F32),
        compiler_params=_cparams(("arbitrary",)),
    )(bias_rows)


def _na_bias_table_bwd(d_tab, rows):
    geom = _na_row_offsets(rows)

    def body(dt_ref, dbr_ref):
        col_ok, first_row = _na_col_window()
        acc = [None] * (2 * NA_ROWS - 1)
        for case in range(NA_CASES):
            for i in range(NA_WIN_ROWS):
                tile = jnp.where(col_ok, dt_ref[0, case, i * GRID_W:(i + 1) * GRID_W, :], 0.0)
                for rr, d in enumerate(geom[case][i]):
                    if d is not None:
                        part = jnp.where(first_row if rr == 0 else ~first_row, tile, 0.0)
                        acc[d] = part if acc[d] is None else acc[d] + part
        for d, a in enumerate(acc):
            dbr_ref[0, d] = jnp.zeros((GRID_W, NA_PAIR), F32) if a is None else a

    return pl.pallas_call(
        body, name="na_bias_table_bwd", grid=(NA_HEADS,),
        in_specs=[pl.BlockSpec((1, NA_CASES, NA_WIN, NA_PAIR), lambda h: (h, 0, 0, 0))],
        out_specs=pl.BlockSpec((1, 2 * NA_ROWS - 1, GRID_W, NA_PAIR), lambda h: (h, 0, 0, 0)),
        out_shape=jax.ShapeDtypeStruct((NA_HEADS, 2 * NA_ROWS - 1, GRID_W, NA_PAIR), F32),
        compiler_params=_cparams(("arbitrary",)),
    )(d_tab)


NA_BLK = 64


def _na_blocks():
    return [slice(i * NA_BLK, (i + 1) * NA_BLK) for i in range(NA_WIN // NA_BLK)]


def _na_softmax(qk, bias_ref, hh, case):
    m = jnp.full((NA_BLK, NA_PAIR), -jnp.inf, F32)
    scores = []
    for blk in _na_blocks():
        s = qk[blk, :] + bias_ref[hh, case, blk, :]
        scores.append(s)
        m = jnp.maximum(m, s)
    m = jnp.max(m, axis=0, keepdims=True)
    l = jnp.zeros((NA_BLK, NA_PAIR), F32)
    exps = []
    for s in scores:
        e = jnp.exp(s - m)
        exps.append(e)
        l = l + e
    return exps, jnp.sum(l, axis=0, keepdims=True)


def _na_units(step, rows):
    units = []
    for pp in range(NA_PAIRS_PER_STEP):
        ws, case = _na_pair_window(step * NA_PAIRS_PER_STEP + pp, rows)
        win = pl.ds(pl.multiple_of(ws * GRID_W, NA_PAIR), NA_WIN)
        lanes = slice(pp * NA_PAIR, (pp + 1) * NA_PAIR)
        for hh in range(2):
            units.append((pp, hh, case, win, lanes, slice(hh * NA_HEAD_DIM, (hh + 1) * NA_HEAD_DIM)))
    return units


def _na_pipeline(n, before, middle, after, lookahead):
    for u in range(min(lookahead, n)):
        for f in before:
            f(u)
    for u in range(n):
        middle(u)
        if u + lookahead < n:
            for f in before:
                f(u + lookahead)
        for f in after:
            f(u)


def _head_rows(t, hh):
    row_head = lax.broadcasted_iota(jnp.int32, t.shape, 0) // NA_HEAD_DIM
    return jnp.where(row_head == hh, t, jnp.zeros_like(t))


def _heads_block_diag(t):
    lane_head = lax.broadcasted_iota(jnp.int32, t.shape, 1) // NA_HEAD_DIM
    zero = jnp.zeros_like(t)
    return jnp.concatenate([jnp.where(lane_head == 0, t, zero), jnp.where(lane_head == 1, t, zero)], axis=0)


def _na_fwd(q_t, k, v_t, bias_tab):
    L = k.shape[0]
    rows = L // GRID_W
    step_w = NA_PAIRS_PER_STEP * NA_PAIR

    def body(q_ref, k_ref, v_ref, bt_ref, o_ref):
        units = _na_units(pl.program_id(1), rows)
        qk, probs = {}, {}

        def scores(u):
            _, hh, _, win, lanes, _ = units[u]
            qk[u] = _dot(k_ref[win, :], _head_rows(q_ref[:, lanes], hh))

        def softmax(u):
            _, hh, case, _, _, _ = units[u]
            exps, l = _na_softmax(qk.pop(u), bt_ref, hh, case)
            probs[u] = jnp.concatenate([t.astype(BF16) for t in exps], axis=0), l

        def output(u):
            _, _, _, win, lanes, hrows = units[u]
            e, l = probs.pop(u)
            o_ref[hrows, lanes] = _dot(v_ref[hrows, win], e) / l

        _na_pipeline(len(units), [scores], softmax, [output], lookahead=3)

    q_spec = pl.BlockSpec((NA_PAIR, step_w), lambda h, s: (h, s))
    return pl.pallas_call(
        body, name="na_fwd", grid=(NA_HEADS // 2, L // step_w),
        in_specs=[q_spec, pl.BlockSpec((L, NA_PAIR), lambda h, s: (0, h)),
                  pl.BlockSpec((NA_PAIR, L), lambda h, s: (h, 0)),
                  pl.BlockSpec((2, NA_CASES, NA_WIN, NA_PAIR), lambda h, s: (h, 0, 0, 0))],
        out_specs=q_spec,
        out_shape=jax.ShapeDtypeStruct((D_NA, L), F32),
        compiler_params=_cparams(("arbitrary", "arbitrary")),
    )(q_t, k, v_t, bias_tab)


def _na_bwd(q_t, q, k_t, k, v, bias_tab, out_t, d_out_t, d_out):
    L = k.shape[0]
    rows = L // GRID_W
    step_w = NA_PAIRS_PER_STEP * NA_PAIR

    def body(qt_ref, q_ref, kt_ref, k_ref, v_ref, bt_ref, ot_ref, dot_ref, do_ref, dq_ref, dk_ref, dv_ref, dbt_ref):
        @pl.when(pl.program_id(1) == 0)
        def _():
            dk_ref[...] = jnp.zeros_like(dk_ref)
            dv_ref[...] = jnp.zeros_like(dv_ref)
            dbt_ref[...] = jnp.zeros_like(dbt_ref)

        units = _na_units(pl.program_id(1), rows)
        qk, dp, dsb, pb = {}, {}, {}, {}

        def scores(u):
            _, hh, _, win, lanes, _ = units[u]
            qk[u] = _dot(k_ref[win, :], _head_rows(qt_ref[:, lanes], hh))

        def d_probs(u):
            _, hh, _, win, lanes, _ = units[u]
            dp[u] = _dot(v_ref[win, :], _head_rows(dot_ref[:, lanes].astype(BF16), hh))

        def softmax_bwd(u):
            _, hh, case, _, lanes, hrows = units[u]
            exps, l = _na_softmax(qk.pop(u), bt_ref, hh, case)
            inv_l = 1.0 / l
            delta = jnp.sum(dot_ref[hrows, lanes] * ot_ref[hrows, lanes], axis=0, keepdims=True)
            d_p = dp.pop(u)
            ds_blocks, p_blocks = [], []
            for blk, e in zip(_na_blocks(), exps):
                p = e * inv_l
                ds = p * (d_p[blk, :] - delta)
                dbt_ref[hh, case, blk, :] += ds
                ds_blocks.append(ds.astype(BF16))
                p_blocks.append(p.astype(BF16))
            dsb[u] = jnp.concatenate(ds_blocks, axis=0)
            pb[u] = jnp.concatenate(p_blocks, axis=0)

        def d_query(u):
            _, _, _, win, lanes, hrows = units[u]
            dq_ref[hrows, lanes] = _dot(kt_ref[hrows, win], dsb[u]) * (NA_HEAD_DIM ** -0.5)

        def d_keys_values(u):
            pp, hh, _, win, _, _ = units[u]
            if hh == 1:
                tokens = slice(pp * NA_PAIR, (pp + 1) * NA_PAIR)
                dk_ref[win, :] += _dot(jnp.concatenate([dsb.pop(u - 1), dsb.pop(u)], axis=1), _heads_block_diag(q_ref[tokens, :]))
                dv_ref[win, :] += _dot(jnp.concatenate([pb.pop(u - 1), pb.pop(u)], axis=1), _heads_block_diag(do_ref[tokens, :]))

        _na_pipeline(len(units), [scores, d_probs], softmax_bwd, [d_query, d_keys_values], lookahead=2)

    t_tile = pl.BlockSpec((NA_PAIR, step_w), lambda h, s: (h, s))
    tile = pl.BlockSpec((step_w, NA_PAIR), lambda h, s: (s, h))
    t_full = pl.BlockSpec((NA_PAIR, L), lambda h, s: (h, 0))
    full = pl.BlockSpec((L, NA_PAIR), lambda h, s: (0, h))
    bt = pl.BlockSpec((2, NA_CASES, NA_WIN, NA_PAIR), lambda h, s: (h, 0, 0, 0))
    tok = jax.ShapeDtypeStruct((L, D_NA), F32)
    return pl.pallas_call(
        body, name="na_bwd", grid=(NA_HEADS // 2, L // step_w),
        in_specs=[t_tile, tile, t_full, full, full, bt, t_tile, t_tile, tile],
        out_specs=[t_tile, full, full, bt],
        out_shape=[jax.ShapeDtypeStruct((D_NA, L), F32), tok, tok, jax.ShapeDtypeStruct(bias_tab.shape, F32)],
        compiler_params=_cparams(("arbitrary", "arbitrary")),
    )(q_t, q, k_t, k, v, bias_tab, out_t, d_out_t, d_out)


def _branch_fwd_values(ys, zs, yn, zn, wglu, bglu):
    g1, t = _gelu_parts(ys)
    lin = _dot(g1.astype(BF16), wglu) + bglu
    sg = _sigmoid(lin)
    ys2 = g1 * sg
    sz, szs = _silu_parts(zs)
    sn, sns = _silu_parts(zn)
    return g1, t, sg, ys2, sz, szs, sn, sns


def _branch_fwd(y_ssm_c, z_s, y_na_t, z_n, w_glu, b_glu, tm=512):
    L = z_s.shape[0]

    def body(ys_ref, zs_ref, yn_ref, zn_ref, w_ref, b_ref, cat_ref, scr):
        yn = yn_ref[...].T
        g1, t, sg, ys2, sz, szs, sn, sns = _branch_fwd_values(
            _load_chunks(ys_ref, scr), zs_ref[...], yn, zn_ref[...], w_ref[...], b_ref[...])
        cat_ref[:, 0:512] = (ys2 * sz).astype(BF16)
        cat_ref[:, 512:1024] = (yn * sn).astype(BF16)

    tile = pl.BlockSpec((tm, 512), lambda i: (i, 0))
    return pl.pallas_call(
        body, name="branch_fwd", grid=(L // tm,),
        in_specs=[_chunk_spec(tm), tile, _heads_t_spec(tm), tile, pl.BlockSpec((512, 512), lambda i: (0, 0)),
                  pl.BlockSpec((1, 512), lambda i: (0, 0))],
        out_specs=pl.BlockSpec((tm, 1024), lambda i: (i, 0)),
        out_shape=jax.ShapeDtypeStruct((L, 1024), BF16),
        scratch_shapes=[_chunk_scratch(tm)],
        compiler_params=_cparams(("arbitrary",)),
    )(y_ssm_c, z_s, y_na_t, z_n, w_glu, b_glu)


def _branch_bwd(y_ssm_c, z_s, y_na_t, z_n, w_glu, b_glu, d_cat, tm=512):
    L = z_s.shape[0]

    def body(ys_ref, zs_ref, yn_ref, zn_ref, w_ref, b_ref, dc_ref,
             dys_ref, dzs_ref, dynt_ref, dyn_ref, dzn_ref, dw_ref, db_ref, scr):
        @pl.when(pl.program_id(0) == 0)
        def _():
            dw_ref[...] = jnp.zeros_like(dw_ref)
            db_ref[...] = jnp.zeros_like(db_ref)

        ys, zs, yn, zn = _load_chunks(ys_ref, scr), zs_ref[...], yn_ref[...].T, zn_ref[...]
        w = w_ref[...]
        g1, t, sg, ys2, sz, szs, sn, sns = _branch_fwd_values(ys, zs, yn, zn, w, b_ref[...])
        dys3 = dc_ref[:, 0:512]
        dyn2 = dc_ref[:, 512:1024]
        dzs_ref[...] = (dys3 * ys2 * _silu_grad(zs, szs)).astype(BF16)
        dys2 = dys3 * sz
        dlin = dys2 * g1 * sg * (1.0 - sg)
        dlb = dlin.astype(BF16)
        dg1 = dys2 * sg + _dot_nt(dlb, w)
        dw_ref[...] += _dot_tn(g1.astype(BF16), dlb)
        db_ref[...] += jnp.sum(dlin, axis=0, keepdims=True)
        _store_chunks(dg1 * _gelu_grad(ys, t), scr, dys_ref, BF16)
        dyn = dyn2 * sn
        dynt_ref[...] = dyn.T
        dyn_ref[...] = dyn.astype(BF16)
        dzn_ref[...] = (dyn2 * yn * _silu_grad(zn, sns)).astype(BF16)

    tile = pl.BlockSpec((tm, 512), lambda i: (i, 0))
    wspec = pl.BlockSpec((512, 512), lambda i: (0, 0))
    bspec = pl.BlockSpec((1, 512), lambda i: (0, 0))
    tok = jax.ShapeDtypeStruct((L, 512), BF16)
    return pl.pallas_call(
        body, name="branch_bwd", grid=(L // tm,),
        in_specs=[_chunk_spec(tm), tile, _heads_t_spec(tm), tile, wspec, bspec, pl.BlockSpec((tm, 1024), lambda i: (i, 0))],
        out_specs=[_chunk_spec(tm), tile, _heads_t_spec(tm), tile, tile, wspec, bspec],
        out_shape=[jax.ShapeDtypeStruct((N_BLOCKS, L // CHUNK, CHUNK_W), BF16), tok, jax.ShapeDtypeStruct((D_NA, L), F32),
                   tok, tok,
                   jax.ShapeDtypeStruct((512, 512), F32), jax.ShapeDtypeStruct((1, 512), F32)],
        scratch_shapes=[_chunk_scratch(tm)],
        compiler_params=_cparams(("arbitrary",)),
    )(y_ssm_c, z_s, y_na_t, z_n, w_glu, b_glu, d_cat)


def _head(x, p, target, cat, w_out, g_post, w_ple_g, g_ple, w_pg, tm=512):
    L = x.shape[0]
    pw = w_ple_g.shape[2]

    def body(x_ref, p_ref, t_ref, cat_ref, wo_ref, gpo_ref, wp_ref, gpl_ref, wg_ref,
             loss_ref, dh1_ref, dcat_ref, dwo_ref, dgpo_ref, dwp_ref, dgpl_ref, dwg_ref):
        @pl.when(pl.program_id(0) == 0)
        def _():
            for r in (loss_ref, dwo_ref, dgpo_ref, dwp_ref, dgpl_ref, dwg_ref):
                r[...] = jnp.zeros_like(r)

        cat_b = cat_ref[...]
        wo, wg = wo_ref[...], wg_ref[...]
        g_po, g_pl = gpo_ref[...], gpl_ref[...]
        mix = _dot(cat_b, wo)
        p_b = p_ref[...].astype(BF16)
        ep = jnp.concatenate([_dot(p_b, wp_ref[j]) for j in range(N_CHIPS)], axis=1)
        nm, r2 = _rms(mix)
        h1 = x_ref[...] + nm * g_po
        ne, r3 = _rms(ep)
        e = ne * g_pl
        h1_b = h1.astype(BF16)
        gate = _sigmoid(_dot(h1_b, wg))
        h2 = h1 + gate * e
        diff = h2 - t_ref[...]
        loss_ref[...] += (0.5 / D_MODEL) * jnp.sum(diff * diff).reshape(1, 1)

        dh2 = diff * (1.0 / D_MODEL)
        de = dh2 * gate
        dgl = (dh2 * e * gate * (1.0 - gate)).astype(BF16)
        dh1 = dh2 + _dot_nt(dgl, wg)
        dwg_ref[...] += _dot_tn(h1_b, dgl)
        dgpo_ref[...] += jnp.sum(dh1 * nm, axis=0, keepdims=True)
        dmix = _rms_bwd(dh1 * g_po, nm, r2).astype(BF16)
        dcat_ref[...] = _dot_nt(dmix, wo)
        dwo_ref[...] += _dot_tn(cat_b, dmix)
        dh1_ref[...] = dh1
        dgpl_ref[...] += jnp.sum(de * ne, axis=0, keepdims=True)
        dep = _rms_bwd(de * g_pl, ne, r3).astype(BF16)
        for j in range(N_CHIPS):
            dwp_ref[j] += _dot_tn(p_b, dep[:, j * pw:(j + 1) * pw])

    tile = lambda w: pl.BlockSpec((tm, w), lambda i: (i, 0))
    const = _resident
    sds = jax.ShapeDtypeStruct
    return pl.pallas_call(
        body, name="head", grid=(L // tm,),
        in_specs=[tile(D_MODEL), tile(D_PLE), tile(D_MODEL), tile(1024), const(1024, D_MODEL), const(1, D_MODEL),
                  const(N_CHIPS, D_PLE, pw), const(1, D_MODEL), const(D_MODEL, D_MODEL)],
        out_specs=[const(1, 1), tile(D_MODEL), tile(1024), const(1024, D_MODEL), const(1, D_MODEL),
                   const(N_CHIPS, D_PLE, pw), const(1, D_MODEL), const(D_MODEL, D_MODEL)],
        out_shape=[sds((1, 1), F32), sds((L, D_MODEL), F32), sds((L, 1024), F32), sds((1024, D_MODEL), F32),
                   sds((1, D_MODEL), F32), sds((N_CHIPS, D_PLE, pw), F32), sds((1, D_MODEL), F32),
                   sds((D_MODEL, D_MODEL), F32)],
        compiler_params=_cparams(("arbitrary",)),
    )(x, p, target, cat, w_out, g_post, w_ple_g, g_ple, w_pg)


def _dproj_specs(tm):
    tile = pl.BlockSpec((tm, 512), lambda i: (i, 0))
    return [_chunk_spec(tm), tile, _heads_t_spec(tm), tile, tile, tile]


_DPROJ_ORDER = (3, 4, 5, 1, 2, 0)


def _dproj_part(refs, scr, i):
    if i == 0:
        val = _load_chunks(refs[0], scr)
    elif i == 2:
        val = refs[2][...].T
    else:
        val = refs[i][...]
    return val.astype(BF16)


def _dproj_pieces(i, wn):
    lo, hi = 512 * i, 512 * (i + 1)
    pieces = []
    for j in range(N_CHIPS):
        a, b = max(lo, j * wn), min(hi, (j + 1) * wn)
        if a < b:
            pieces.append((j, slice(a - j * wn, b - j * wn), slice(a - lo, b - lo)))
    return pieces


def _in_proj_bwd_w(x, g_col, w_in_g, dparts, tm=512):
    L = x.shape[0]
    wn = D_IN_PROJ // N_CHIPS
    steps = L // tm

    def body(x_ref, g_ref, w_ref, *refs):
        dw_ref, dg_ref, scr = refs[-3], refs[-2], refs[-1]

        @pl.when(pl.program_id(0) == 0)
        def _():
            dw_ref[...] = jnp.zeros_like(dw_ref)

        n, _ = _rms(x_ref[...])
        nb = n.astype(BF16)
        for i in _DPROJ_ORDER:
            part = _dproj_part(refs[:-3], scr, i)
            for j, w_cols, p_cols in _dproj_pieces(i, wn):
                dw_ref[j, :, w_cols] += _dot_tn(nb, part[:, p_cols])

        @pl.when(pl.program_id(0) == steps - 1)
        def _():
            g = g_ref[...]
            dg = jnp.zeros_like(g)
            for j in range(N_CHIPS):
                a = dw_ref[j]
                dg = dg + jnp.sum(a * w_ref[j].astype(F32), axis=1, keepdims=True)
                dw_ref[j] = a * g
            dg_ref[...] = dg

    return pl.pallas_call(
        body, name="in_proj_bwd_w", grid=(steps,),
        in_specs=[pl.BlockSpec((tm, D_MODEL), lambda i: (i, 0)), _resident(D_MODEL, 1), _resident(N_CHIPS, D_MODEL, wn)]
        + _dproj_specs(tm),
        out_specs=[_resident(N_CHIPS, D_MODEL, wn), _resident(D_MODEL, 1)],
        out_shape=[jax.ShapeDtypeStruct((N_CHIPS, D_MODEL, wn), F32), jax.ShapeDtypeStruct((D_MODEL, 1), F32)],
        scratch_shapes=[_chunk_scratch(tm)],
        compiler_params=_cparams(("arbitrary",)),
    )(x, g_col, w_in_g, *dparts)


def _in_proj_bwd_x(x, g_pre, w_in_g, d_h1, dparts, pair_sums, tm=512):
    L = x.shape[0]
    wn = w_in_g.shape[2]
    n_ps = len(pair_sums)
    steps = L // tm

    def body(*refs):
        x_ref, g_ref, w_ref, dh1_ref = refs[:4]
        dparts_refs = refs[4:10]
        dx_ref = refs[10 + n_ps]
        scr = refs[11 + 2 * n_ps]
        scatter = _ChipScatter(refs[10:10 + n_ps], refs[11 + n_ps:11 + 2 * n_ps], refs[12 + 2 * n_ps:16 + 2 * n_ps],
                               refs[16 + 2 * n_ps:])
        pl.when(pl.program_id(0) == 0)(scatter.start)
        pl.when(pl.program_id(0) == steps - 1)(scatter.finish)

        dhn = None
        for i in _DPROJ_ORDER:
            part = _dproj_part(dparts_refs, scr, i)
            for j, w_cols, p_cols in _dproj_pieces(i, wn):
                term = _dot_nt(part[:, p_cols], w_ref[j, :, w_cols])
                dhn = term if dhn is None else dhn + term
        n, r = _rms(x_ref[...])
        dx_ref[...] = dh1_ref[...] + _rms_bwd(dhn * g_ref[...], n, r)

    wide = pl.BlockSpec((tm, D_MODEL), lambda i: (i, 0))
    outs = pl.pallas_call(
        body, name="in_proj_bwd_x", grid=(steps,),
        in_specs=[wide, _resident(1, D_MODEL), _resident(N_CHIPS, D_MODEL, wn), wide] + _dproj_specs(tm) + _hbm_specs(n_ps),
        out_specs=[wide] + _hbm_specs(n_ps),
        out_shape=[jax.ShapeDtypeStruct((L, D_MODEL), F32)] + [jax.ShapeDtypeStruct(p.shape, p.dtype) for p in pair_sums],
        scratch_shapes=[_chunk_scratch(tm)] + _scatter_scratch(pair_sums),
        compiler_params=_cparams(("arbitrary",), has_side_effects=True),
    )(x, g_pre, w_in_g, d_h1, *dparts, *pair_sums)
    return outs[0], outs[1:]


def _mesh_position():
    x, y, c = lax.axis_index("x"), lax.axis_index("y"), lax.axis_index("c")
    chips = [(1 - x, y), (x, 1 - y), (1 - x, 1 - y)]
    return x, y, c, chips


def _chip_index(cx, cy):
    return 2 * cx + cy


def _hbm_specs(n):
    return [pl.BlockSpec(memory_space=pl.ANY)] * n


def _gather_chips(shards, name):
    n = len(shards)

    def body(*refs):
        gather = _ChipGather(refs[:n], refs[n:2 * n], refs[2 * n:])
        gather.start()
        gather.forward()
        gather.finish()

    return pl.pallas_call(
        body, name=name, in_specs=_hbm_specs(n), out_specs=_hbm_specs(n),
        out_shape=_gather_out_shapes(shards), scratch_shapes=_gather_semaphores(n),
        compiler_params=pltpu.CompilerParams(has_side_effects=True),
    )(*shards)


def _gather_out_shapes(shards):
    return [jax.ShapeDtypeStruct((N_CHIPS,) + s.shape, s.dtype) for s in shards]


def _gather_semaphores(n):
    sem = pltpu.SemaphoreType.DMA
    return [sem((n, 3)), sem((n, 3)), sem((n, 3)), sem((n, 3)), sem((n,)), sem((n,))]


class _ChipGather:
    def __init__(self, ins, outs, sems):
        self.ins, self.outs = ins, outs
        self.send1, self.recv1, self.send2, self.recv2, self.send3, self.recv3 = sems
        self.x, self.y, self.c, self.chips = _mesh_position()
        self.me = _chip_index(self.x, self.y)
        self.sibling = (self.x, self.y, 1 - self.c)

    def _half(self, a, chip, core):
        hr = self.outs[a].shape[1] // 2
        return self.outs[a].at[chip, pl.ds(core * hr, hr)]

    def _own(self, a):
        return pltpu.make_async_remote_copy(
            src_ref=self.ins[a], dst_ref=self.outs[a].at[self.me], send_sem=self.send3.at[a], recv_sem=self.recv3.at[a],
            device_id=self.sibling, device_id_type=MESH)

    def _to_chip(self, a, j):
        hr = self.ins[a].shape[0] // 2
        return pltpu.make_async_remote_copy(
            src_ref=self.ins[a].at[pl.ds(self.c * hr, hr)], dst_ref=self._half(a, self.me, self.c),
            send_sem=self.send1.at[a, j], recv_sem=self.recv1.at[a, j], device_id=(*self.chips[j], self.c), device_id_type=MESH)

    def _from_chip(self, a, j):
        landed = self._half(a, _chip_index(*self.chips[j]), self.c)
        return pltpu.make_async_remote_copy(
            src_ref=landed, dst_ref=landed, send_sem=self.send1.at[a, j], recv_sem=self.recv1.at[a, j],
            device_id=(*self.chips[j], self.c), device_id_type=MESH)

    def _to_sibling(self, a, j, core):
        part = self._half(a, _chip_index(*self.chips[j]), core)
        return pltpu.make_async_remote_copy(
            src_ref=part, dst_ref=part, send_sem=self.send2.at[a, j], recv_sem=self.recv2.at[a, j],
            device_id=self.sibling, device_id_type=MESH)

    def _each(self):
        return [(a, j) for a in range(len(self.ins)) for j in range(3)]

    def start(self):
        for a in range(len(self.ins)):
            self._own(a).start()
        for a, j in self._each():
            self._to_chip(a, j).start()

    def forward(self):
        for a, j in self._each():
            self._from_chip(a, j).wait_recv()
            self._to_sibling(a, j, self.c).start()

    def finish(self):
        for a, j in self._each():
            self._to_sibling(a, j, 1 - self.c).wait_recv()
        for a, j in self._each():
            self._to_chip(a, j).wait_send()
            self._to_sibling(a, j, self.c).wait_send()
        for a in range(len(self.ins)):
            self._own(a).wait()


def _pair_exchange(grads):
    n = len(grads)

    def body(*refs):
        ins, outs = refs[:n], refs[n:2 * n]
        send, recv = refs[2 * n:]
        x, y, c, _ = _mesh_position()
        copies = []
        for a in range(n):
            hr = ins[a].shape[1] // 2
            cp = pltpu.make_async_remote_copy(
                src_ref=ins[a].at[:, pl.ds((1 - c) * hr, hr)], dst_ref=outs[a],
                send_sem=send.at[a], recv_sem=recv.at[a], device_id=(x, y, 1 - c), device_id_type=MESH)
            cp.start()
            copies.append(cp)
        for cp in copies:
            cp.wait()

    sem = pltpu.SemaphoreType.DMA
    return pl.pallas_call(
        body, name="pair_exchange", in_specs=_hbm_specs(n), out_specs=_hbm_specs(n),
        out_shape=[jax.ShapeDtypeStruct((g.shape[0], g.shape[1] // 2, g.shape[2]), g.dtype) for g in grads],
        scratch_shapes=[sem((n,)), sem((n,))],
        compiler_params=pltpu.CompilerParams(has_side_effects=True),
    )(*grads)


def _pair_add(core, grad, other, tr, out_dtype):
    hr = other.shape[1]
    cdim = other.shape[2]
    nb = hr // tr

    def body(core_ref, g_ref, o_ref, out_ref):
        out_ref[...] = (g_ref[...] + o_ref[...]).astype(out_dtype)

    return pl.pallas_call(
        body, name="pair_add",
        grid_spec=pltpu.PrefetchScalarGridSpec(
            num_scalar_prefetch=1, grid=(N_CHIPS, nb),
            in_specs=[pl.BlockSpec((1, tr, cdim), lambda j, i, core_ref: (j, core_ref[0] * nb + i, 0)),
                      pl.BlockSpec((1, tr, cdim), lambda j, i, core_ref: (j, i, 0))],
            out_specs=pl.BlockSpec((1, tr, cdim), lambda j, i, core_ref: (j, i, 0))),
        out_shape=jax.ShapeDtypeStruct(other.shape, out_dtype),
        compiler_params=_cparams(("arbitrary", "arbitrary")),
    )(core, grad, other)


def _scatter_scratch(parts):
    sem = pltpu.SemaphoreType.DMA
    n = len(parts)
    return [sem((n, 3)), sem((n, 3)), sem((n,)), sem((n,))] + [pltpu.VMEM(p.shape[1:], p.dtype) for p in parts]


class _ChipScatter:
    def __init__(self, ins, outs, sems, staged):
        self.ins, self.outs, self.staged = ins, outs, staged
        self.send, self.recv, self.load_sem, self.store_sem = sems
        self.x, self.y, self.c, self.chips = _mesh_position()
        self.me = _chip_index(self.x, self.y)

    def _load(self, a):
        return pltpu.make_async_copy(self.ins[a].at[self.me], self.staged[a], self.load_sem.at[a])

    def _store(self, a):
        return pltpu.make_async_copy(self.staged[a], self.outs[a].at[self.me], self.store_sem.at[a])

    def _to_chip(self, a, j):
        return pltpu.make_async_remote_copy(
            src_ref=self.ins[a].at[_chip_index(*self.chips[j])], dst_ref=self.outs[a].at[self.me],
            send_sem=self.send.at[a, j], recv_sem=self.recv.at[a, j], device_id=(*self.chips[j], self.c), device_id_type=MESH)

    def start(self):
        for a in range(len(self.ins)):
            self._load(a).start()
            for j in range(3):
                self._to_chip(a, j).start()

    def finish(self):
        for a in range(len(self.ins)):
            self._load(a).wait()
            self._store(a).start()
        for a in range(len(self.ins)):
            for j in range(3):
                self._to_chip(a, j).wait()
            self._store(a).wait()


def _chip_add(core, recv, tr):
    hr, cdim = recv.shape[1], recv.shape[2]
    nb = hr // tr

    def body(core_ref, r_ref, out_ref):
        out_ref[...] = ((r_ref[0].astype(F32) + r_ref[1].astype(F32)) + r_ref[2].astype(F32)) + r_ref[3].astype(F32)

    return pl.pallas_call(
        body, name="chip_add",
        grid_spec=pltpu.PrefetchScalarGridSpec(
            num_scalar_prefetch=1, grid=(nb,),
            in_specs=[pl.BlockSpec((N_CHIPS, tr, cdim), lambda i, core_ref: (0, i, 0))],
            out_specs=pl.BlockSpec((tr, cdim), lambda i, core_ref: (core_ref[0] * nb + i, 0))),
        out_shape=jax.ShapeDtypeStruct((2 * hr, cdim), F32),
        compiler_params=_cparams(("arbitrary",)),
    )(core, recv)


def _pair_gather(fulls):
    n = len(fulls)

    def body(*refs):
        outs = refs[n:2 * n]
        send, recv = refs[2 * n:]
        x, y, c, _ = _mesh_position()
        copies = []
        for a in range(n):
            hr = outs[a].shape[0] // 2
            mine = outs[a].at[pl.ds(c * hr, hr)]
            cp = pltpu.make_async_remote_copy(
                src_ref=mine, dst_ref=mine, send_sem=send.at[a], recv_sem=recv.at[a],
                device_id=(x, y, 1 - c), device_id_type=MESH)
            cp.start()
            copies.append(cp)
        for cp in copies:
            cp.wait()

    sem = pltpu.SemaphoreType.DMA
    return pl.pallas_call(
        body, name="pair_gather", in_specs=_hbm_specs(n), out_specs=_hbm_specs(n),
        out_shape=[jax.ShapeDtypeStruct(f.shape, f.dtype) for f in fulls],
        input_output_aliases={a: a for a in range(n)},
        scratch_shapes=[sem((n,)), sem((n,))],
        compiler_params=pltpu.CompilerParams(has_side_effects=True),
    )(*fulls)


def _row_tile(rows):
    if rows <= 512:
        return rows
    for t in (512, 256, 128, 64, 32, 16, 8):
        if rows % t == 0:
            return t
    raise ValueError(rows)


def _pair_sums(core, grads, ici_dtypes):
    others = _pair_exchange(grads)
    return [_pair_add(core, g, o, _row_tile(o.shape[1]), dt) for g, o, dt in zip(grads, others, ici_dtypes)]


def _finish_reduce(core, landed):
    return _pair_gather([_chip_add(core, r, _row_tile(r.shape[1])) for r in landed])


def _adamw(w, g, m, v):
    rows, cols = w.shape
    one_block = rows % 8 != 0 or rows * max(cols, 128) * 4 <= (1 << 20)
    tr = rows if one_block else _row_tile(rows)

    def body(w_ref, g_ref, m_ref, v_ref, d_ref, nm_ref, nv_ref):
        g_ = g_ref[...]
        m_ = ADAM_B1 * m_ref[...] + (1.0 - ADAM_B1) * g_
        v_ = ADAM_B2 * v_ref[...] + (1.0 - ADAM_B2) * (g_ * g_)
        m_hat = m_ / (1.0 - ADAM_B1 ** ADAM_STEP)
        v_hat = v_ / (1.0 - ADAM_B2 ** ADAM_STEP)
        d_ref[...] = -ADAM_LR * (m_hat / (jnp.sqrt(v_hat) + ADAM_EPS) + ADAM_WD * w_ref[...])
        nm_ref[...] = m_
        nv_ref[...] = v_

    spec = pl.BlockSpec((tr, cols), lambda i: (i, 0))
    shp = jax.ShapeDtypeStruct((rows, cols), F32)
    return pl.pallas_call(
        body, name="adamw", grid=(rows // tr,), in_specs=[spec] * 4, out_specs=[spec] * 3,
        out_shape=[shp] * 3, compiler_params=_cparams(("arbitrary",)),
    )(w, g, m, v)


_SMALL = ["norm_pre", "norm_post", "ssm_a_re", "ssm_a_im", "ssm_log_dt", "ssm_b_re", "ssm_b_im",
          "ssm_c_re", "ssm_c_im", "ssm_d", "b_glu", "na_rpb", "ple_norm"]
_BIG = ["w_in", "w_glu", "w_out", "w_ple", "w_ple_gate"]
_WEIGHTS = ["norm_pre", "norm_post", "w_in", "ssm_a_re", "ssm_a_im", "ssm_log_dt", "ssm_b_re", "ssm_b_im",
            "ssm_c_re", "ssm_c_im", "ssm_d", "w_glu", "b_glu", "na_rpb", "w_out", "w_ple", "ple_norm", "w_ple_gate"]
_SMALL_ROWS = 2176


def _pack_small(tensors, tail=None):
    parts = [tensors[n].reshape(-1) for n in _SMALL] + ([] if tail is None else [tail.reshape(-1)])
    flat = jnp.concatenate(parts)
    flat = jnp.pad(flat, (0, _SMALL_ROWS * 128 - flat.shape[0]))
    return flat.reshape(_SMALL_ROWS, 128)


def _unpack_small(packed, shapes):
    flat = packed.reshape(-1)
    out, off = {}, 0
    for n in _SMALL:
        size = int(np.prod(shapes[n]))
        out[n] = flat[off:off + size].reshape(shapes[n])
        off += size
    return out


def _local_grads(x, p, target, wts):
    ssm_names = ["ssm_a_re", "ssm_a_im", "ssm_log_dt", "ssm_b_re", "ssm_b_im", "ssm_c_re", "ssm_c_im", "ssm_d"]
    ssm_params = [wts[n][0] for n in ssm_names]
    blk, blk_vjp = jax.vjp(_ssm_block_params, *ssm_params)
    shard = lambda n: wts[n][0].astype(BF16)
    (m_mat, ws_mat, wot_mat, a16), (w_in_g,) = _ssm_chunk_matrices(blk, [shard("w_in")])
    seq = x.shape[0]
    bias_rows, bias_rows_vjp = jax.vjp(_na_bias_rows, wts["na_rpb"][0])
    bias_tab = _na_bias_table(bias_rows, seq // GRID_W)

    (u_c, z_s, q_t, q, k_t, k, v_t, v, z_n), gathered = _in_proj(
        x, wts["norm_pre"], w_in_g, [shard(n) for n in _BIG if n != "w_in"])
    w_glu, w_out, w_ple_g, w_pg = (gathered[0].reshape(512, 512), gathered[1].reshape(1024, 1024), gathered[2],
                                   gathered[3].reshape(1024, 1024))
    s_in = _block_matmul([(u_c, ws_mat, False)], "ssm_chunk_states", tn=2048)
    s_prev = _ssm_state_scan(s_in, a16)
    y_ssm_c = _block_matmul([(u_c, m_mat, False), (s_prev, wot_mat, True)], "ssm_chunk_out")
    y_na_t = _na_fwd(q_t, k, v_t, bias_tab)
    cat = _branch_fwd(y_ssm_c, z_s, y_na_t, z_n, w_glu, wts["b_glu"])

    (loss, d_h1, d_cat, d_w_out, d_g_post, d_w_ple, d_g_ple, d_w_pg) = _head(
        x, p, target, cat, w_out, wts["norm_post"], w_ple_g, wts["ple_norm"], w_pg)
    dy_c, d_z_s, d_y_na_t, d_y_na, d_z_n, d_w_glu, d_b_glu = _branch_bwd(
        y_ssm_c, z_s, y_na_t, z_n, w_glu, wts["b_glu"], d_cat)
    d_q_t, d_k, d_v, d_bias_tab = _na_bwd(q_t, q, k_t, k, v, bias_tab, y_na_t, d_y_na_t, d_y_na)

    d_prev = _block_matmul([(dy_c, wot_mat, False)], "ssm_bwd_states", tn=2048)
    g_st, d_a16 = _ssm_state_scan_bwd(d_prev, s_prev, a16)
    d_u_c = _block_matmul([(dy_c, m_mat, True), (g_st, ws_mat, True)], "ssm_bwd_in", out_dtype=BF16)
    d_m = _block_matmul_tn(u_c, dy_c, "ssm_grad_m")
    d_ws = _block_matmul_tn(u_c, g_st, "ssm_grad_ws")
    d_wot = _block_matmul_tn(dy_c, s_prev, "ssm_grad_wot")
    d_ssm = blk_vjp(tuple(_ssm_chunk_matrices_bwd(blk, d_m, d_ws, d_wot, d_a16)))
    (d_rpb,) = bias_rows_vjp(_na_bias_table_bwd(d_bias_tab, seq // GRID_W))

    dparts = [d_u_c, d_z_s, d_q_t, d_k, d_v, d_z_n]
    d_w_in, d_g_pre = _in_proj_bwd_w(x, wts["norm_pre"].reshape(D_MODEL, 1), w_in_g, dparts)

    small = {"norm_pre": d_g_pre, "norm_post": d_g_post, "b_glu": d_b_glu, "na_rpb": d_rpb, "ple_norm": d_g_ple}
    for n, g in zip(ssm_names, d_ssm):
        small[n] = g
    big = {"w_in": d_w_in, "w_glu": d_w_glu.reshape(N_CHIPS, 128, 512), "w_out": d_w_out.reshape(N_CHIPS, 256, 1024),
           "w_ple": d_w_ple, "w_ple_gate": d_w_pg.reshape(N_CHIPS, 256, 1024)}
    return loss, small, big, (x, wts["norm_pre"], w_in_g, d_h1, dparts)


def kernel(x, p, norm_pre, norm_post, w_in, ssm_a_re, ssm_a_im, ssm_log_dt, ssm_b_re, ssm_b_im, ssm_c_re, ssm_c_im, ssm_d, w_glu, b_glu, na_rpb, w_out, w_ple, ple_norm, w_ple_gate, loss_target, m_norm_pre, m_norm_post, m_w_in, m_ssm_a_re, m_ssm_a_im, m_ssm_log_dt, m_ssm_b_re, m_ssm_b_im, m_ssm_c_re, m_ssm_c_im, m_ssm_d, m_w_glu, m_b_glu, m_na_rpb, m_w_out, m_w_ple, m_ple_norm, m_w_ple_gate, v_norm_pre, v_norm_post, v_w_in, v_ssm_a_re, v_ssm_a_im, v_ssm_log_dt, v_ssm_b_re, v_ssm_b_im, v_ssm_c_re, v_ssm_c_im, v_ssm_d, v_w_glu, v_b_glu, v_na_rpb, v_w_out, v_w_ple, v_ple_norm, v_w_ple_gate):
    wts = dict(norm_pre=norm_pre, norm_post=norm_post, w_in=w_in, ssm_a_re=ssm_a_re, ssm_a_im=ssm_a_im,
               ssm_log_dt=ssm_log_dt, ssm_b_re=ssm_b_re, ssm_b_im=ssm_b_im, ssm_c_re=ssm_c_re, ssm_c_im=ssm_c_im,
               ssm_d=ssm_d, w_glu=w_glu, b_glu=b_glu, na_rpb=na_rpb, w_out=w_out, w_ple=w_ple, ple_norm=ple_norm,
               w_ple_gate=w_ple_gate)
    mom_m = dict(norm_pre=m_norm_pre, norm_post=m_norm_post, w_in=m_w_in, ssm_a_re=m_ssm_a_re, ssm_a_im=m_ssm_a_im,
                 ssm_log_dt=m_ssm_log_dt, ssm_b_re=m_ssm_b_re, ssm_b_im=m_ssm_b_im, ssm_c_re=m_ssm_c_re,
                 ssm_c_im=m_ssm_c_im, ssm_d=m_ssm_d, w_glu=m_w_glu, b_glu=m_b_glu, na_rpb=m_na_rpb, w_out=m_w_out,
                 w_ple=m_w_ple, ple_norm=m_ple_norm, w_ple_gate=m_w_ple_gate)
    mom_v = dict(norm_pre=v_norm_pre, norm_post=v_norm_post, w_in=v_w_in, ssm_a_re=v_ssm_a_re, ssm_a_im=v_ssm_a_im,
                 ssm_log_dt=v_ssm_log_dt, ssm_b_re=v_ssm_b_re, ssm_b_im=v_ssm_b_im, ssm_c_re=v_ssm_c_re,
                 ssm_c_im=v_ssm_c_im, ssm_d=v_ssm_d, w_glu=v_w_glu, b_glu=v_b_glu, na_rpb=v_na_rpb, w_out=v_w_out,
                 w_ple=v_w_ple, ple_norm=v_ple_norm, w_ple_gate=v_w_ple_gate)

    loss_part, small, big, input_grad_args = _local_grads(x[0], p[0, 0], loss_target[0], wts)

    core = lax.axis_index("c").astype(jnp.int32).reshape(1)
    small_packed = _pack_small(small, tail=loss_part).reshape(N_CHIPS, _SMALL_ROWS // N_CHIPS, 128)
    pair = _pair_sums(core, [big[n] for n in _BIG] + [small_packed], [BF16] * len(_BIG) + [F32])
    grad_x, landed = _in_proj_bwd_x(*input_grad_args, pair)
    reduced = _finish_reduce(core, landed)
    grads = dict(zip(_BIG, reduced[:-1]))
    (small_all,) = _gather_chips([reduced[-1]], "gather_small_grads")
    small_all = small_all.reshape(_SMALL_ROWS, 128)
    loss = small_all.reshape(-1)[sum(int(np.prod(wts[n].shape)) for n in _SMALL)]

    delta, new_m, new_v = {}, {}, {}
    for n in _BIG:
        shp = wts[n].shape
        d_, m_, v_ = _adamw(wts[n][0], grads[n], mom_m[n][0], mom_v[n][0])
        grads[n] = grads[n].reshape(shp)
        delta[n], new_m[n], new_v[n] = d_.reshape(shp), m_.reshape(shp), v_.reshape(shp)
    grads.update(_unpack_small(small_all, {n: wts[n].shape for n in _SMALL}))
    for n in _SMALL:
        shp = wts[n].shape
        swap = shp[-1] < shp[-2]
        view_shape = shp[:-2] + (shp[-1], shp[-2]) if swap else shp
        rows_cols = (int(np.prod(view_shape[:-1])), view_shape[-1])
        view = lambda t: (jnp.swapaxes(t, -1, -2) if swap else t).reshape(rows_cols)
        back = lambda t: jnp.swapaxes(t.reshape(view_shape), -1, -2) if swap else t.reshape(shp)
        d_, m_, v_ = _adamw(*[view(t) for t in (wts[n], grads[n], mom_m[n], mom_v[n])])
        delta[n], new_m[n], new_v[n] = back(d_), back(m_), back(v_)

    return (loss, grad_x[None], *[grads[n] for n in _WEIGHTS], *[delta[n] for n in _WEIGHTS],
            *[new_m[n] for n in _WEIGHTS], *[new_v[n] for n in _WEIGHTS])
```

```python
import math

import jax
import jax.numpy as jnp
import numpy as np
from jax import lax
from jax.experimental import pallas as pl
from jax.experimental.pallas import tpu as pltpu

F32 = jnp.float32
BF16 = jnp.bfloat16

D_MODEL = 1024
D_PLE = 256
GRID_W = 64
D_SSM = 512
SSM_GROUP = 16
N_GROUPS = 32
SSM_STATE = 64
D_NA = 512
NA_HEADS = 8
NA_HEAD_DIM = 64
NA_ROWS = 8
NA_COLS = 16
D_IN_PROJ = 3072
EPS = 1e-6

CHUNK = 16
GROUPS_PER_BLOCK = 8
N_BLOCKS = N_GROUPS // GROUPS_PER_BLOCK
BLOCK_CH = GROUPS_PER_BLOCK * SSM_GROUP
BLOCK_ST = GROUPS_PER_BLOCK * SSM_STATE
CHUNK_W = CHUNK * BLOCK_CH
STATE_W = 4 * BLOCK_ST

N_CHIPS = 4
MESH = pl.DeviceIdType.MESH

ADAM_LR = 0.001
ADAM_B1 = 0.9
ADAM_B2 = 0.999
ADAM_EPS = 1e-08
ADAM_WD = 0.01
ADAM_STEP = 10

VMEM_LIMIT = 52 * 1024 * 1024
HIGHEST = lax.Precision.HIGHEST


def _cparams(sem=None, **kw):
    if sem is not None:
        kw["dimension_semantics"] = sem
    return pltpu.CompilerParams(vmem_limit_bytes=VMEM_LIMIT, **kw)


def _resident(*shape):
    return pl.BlockSpec(shape, lambda *_: (0,) * len(shape), pipeline_mode=pl.Buffered(1))


def _dot(a, b, dims=((1,), (0,))):
    return lax.dot_general(a, b, (dims, ((), ())), preferred_element_type=F32)


def _dot_nt(a, b):
    return _dot(a, b, ((1,), (1,)))


def _dot_tn(a, b):
    return _dot(a, b, ((0,), (0,)))


def _sigmoid(x):
    return 1.0 / (1.0 + jnp.exp(-x))


_GELU_C = math.sqrt(2.0 / math.pi)


def _gelu_parts(x):
    inner = _GELU_C * (x + 0.044715 * (x * x * x))
    t = jnp.tanh(inner)
    return 0.5 * x * (1.0 + t), t


def _gelu_grad(x, t):
    return 0.5 * (1.0 + t) + 0.5 * x * (1.0 - t * t) * (_GELU_C * (1.0 + 3.0 * 0.044715 * x * x))


def _silu_parts(z):
    s = _sigmoid(z)
    return z * s, s


def _silu_grad(z, s):
    return s * (1.0 + z * (1.0 - s))


def _rms(x):
    r = lax.rsqrt(jnp.mean(x * x, axis=-1, keepdims=True) + EPS)
    return x * r, r


def _rms_bwd(dn, n, r):
    return r * (dn - n * jnp.mean(dn * n, axis=-1, keepdims=True))


def _chunk_scratch(tm):
    return pltpu.VMEM((N_BLOCKS, tm, BLOCK_CH), F32)


def _store_chunks(val, scr, c_ref, dtype, row0=0):
    rows = val.shape[0]
    nc, c0 = rows // CHUNK, row0 // CHUNK
    for b in range(N_BLOCKS):
        scr[b, row0:row0 + rows, :] = val[:, b * BLOCK_CH:(b + 1) * BLOCK_CH]
        for j in range(CHUNK):
            c_ref[b, c0:c0 + nc, j * BLOCK_CH:(j + 1) * BLOCK_CH] = scr[b, pl.ds(row0 + j, nc, stride=CHUNK), :].astype(dtype)


def _load_chunks(c_ref, scr):
    nc = scr.shape[1] // CHUNK
    for b in range(N_BLOCKS):
        for j in range(CHUNK):
            scr[b, pl.ds(j, nc, stride=CHUNK), :] = c_ref[b, :, j * BLOCK_CH:(j + 1) * BLOCK_CH].astype(F32)
    return jnp.concatenate([scr[b] for b in range(N_BLOCKS)], axis=1)


def _chunk_spec(tm):
    return pl.BlockSpec((N_BLOCKS, tm // CHUNK, CHUNK_W), lambda i: (0, i, 0))


def _heads_t_spec(tm):
    return pl.BlockSpec((D_NA, tm), lambda i: (0, i))


def _in_proj(x, g_pre, w_in_g, shards, tm=512):
    L = x.shape[0]
    wn = w_in_g.shape[2]
    n_sh = len(shards)
    steps = L // tm

    def body(*refs):
        x_ref, g_ref, w_ref = refs[:3]
        uc_ref, zs_ref, qt_ref, q_ref, kt_ref, k_ref, vt_ref, v_ref, zn_ref = refs[3 + n_sh:12 + n_sh]
        u_scr = refs[12 + 2 * n_sh]
        gather = _ChipGather(refs[3:3 + n_sh], refs[12 + n_sh:12 + 2 * n_sh], refs[13 + 2 * n_sh:])
        step = pl.program_id(0)
        pl.when(step == 0)(gather.start)
        pl.when(step == steps // 2)(gather.forward)
        pl.when(step == steps - 1)(gather.finish)
        halves = [slice(0, tm // 2), slice(tm // 2, tm)]
        hn = [(_rms(x_ref[rows, :])[0] * g_ref[...]).astype(BF16) for rows in halves]
        projs = [jnp.concatenate([_dot(h, w_ref[j]) for j in range(N_CHIPS)], axis=1) for h in hn]
        for rows, proj in zip(halves, projs):
            _store_chunks(proj[:, 0:512], u_scr, uc_ref, BF16, row0=rows.start)
            zs_ref[rows, :] = proj[:, 512:1024]
            q = proj[:, 1024:1536] * (NA_HEAD_DIM ** -0.5)
            for val, t_ref, n_ref in ((q, qt_ref, q_ref), (proj[:, 1536:2048], kt_ref, k_ref), (proj[:, 2048:2560], vt_ref, v_ref)):
                t_ref[:, rows] = val.T.astype(BF16)
                n_ref[rows, :] = val.astype(BF16)
            zn_ref[rows, :] = proj[:, 2560:3072]

    tok = jax.ShapeDtypeStruct((L, 512), F32)
    tr = jax.ShapeDtypeStruct((D_NA, L), BF16)
    hm = jax.ShapeDtypeStruct((L, D_NA), BF16)
    tspec = pl.BlockSpec((tm, 512), lambda i: (i, 0))
    outs = pl.pallas_call(
        body, name="in_proj", grid=(steps,),
        in_specs=[pl.BlockSpec((tm, D_MODEL), lambda i: (i, 0)),
                  _resident(1, D_MODEL), _resident(N_CHIPS, D_MODEL, wn)] + _hbm_specs(n_sh),
        out_specs=[_chunk_spec(tm), tspec] + [_heads_t_spec(tm), tspec] * 3 + [tspec] + _hbm_specs(n_sh),
        out_shape=[jax.ShapeDtypeStruct((N_BLOCKS, L // CHUNK, CHUNK_W), BF16), tok, tr, hm, tr, hm, tr, hm, tok]
        + _gather_out_shapes(shards),
        scratch_shapes=[_chunk_scratch(tm)] + _gather_semaphores(n_sh),
        compiler_params=_cparams(("arbitrary",), has_side_effects=True),
    )(x, g_pre, w_in_g, *shards)
    return outs[:9], outs[9:]


def _ssm_block_params(a_re, a_im, log_dt, b_re, b_im, c_re, c_im, d):
    def lanes(t):
        return t.reshape(2, N_BLOCKS, 1, BLOCK_ST)

    rows = (2, N_BLOCKS, BLOCK_CH, SSM_STATE)
    b_rows = lambda t: t.reshape(2, N_BLOCKS, GROUPS_PER_BLOCK, SSM_STATE, SSM_GROUP).transpose(0, 1, 2, 4, 3).reshape(rows)
    return (lanes(a_re), lanes(a_im), lanes(jnp.broadcast_to(log_dt[..., None], a_re.shape)),
            b_rows(b_re), b_rows(b_im), c_re.reshape(rows), c_im.reshape(rows), d.reshape(N_BLOCKS, 1, BLOCK_CH))


def _ssm_group_mask():
    row_g = lax.broadcasted_iota(jnp.int32, (BLOCK_CH, BLOCK_ST), 0) // SSM_GROUP
    lane_g = lax.broadcasted_iota(jnp.int32, (BLOCK_CH, BLOCK_ST), 1) // SSM_STATE
    return row_g == lane_g


def _ssm_state_select():
    p = lax.broadcasted_iota(jnp.int32, (SSM_STATE, BLOCK_ST), 0)
    lane_p = lax.broadcasted_iota(jnp.int32, (SSM_STATE, BLOCK_ST), 1) % SSM_STATE
    return (p == lane_p).astype(F32)


def _ssm_expand_blocks(compact_refs, full_refs):
    mask, select = _ssm_group_mask(), _ssm_state_select()
    for c_ref, f_ref in zip(compact_refs, full_refs):
        for d in range(2):
            tiled = lax.dot_general(c_ref[d, 0], select, ((((1,), (0,))), ((), ())), precision=HIGHEST,
                                    preferred_element_type=F32)
            f_ref[d, 0] = jnp.where(mask, tiled, 0.0)


def _ssm_collapse_block(t):
    return lax.dot_general(jnp.where(_ssm_group_mask(), t, 0.0), _ssm_state_select(), ((((1,), (1,))), ((), ())),
                           precision=HIGHEST, preferred_element_type=F32)


def _ssm_discretise(ar, ai, ldt):
    dt = jnp.exp(ldt)
    mag = jnp.exp(dt * ar)
    abr = mag * jnp.cos(dt * ai)
    abi = mag * jnp.sin(dt * ai)
    num_re = abr - 1.0
    num_im = abi
    denom = ar * ar + ai * ai
    coef_re = (num_re * ar + num_im * ai) / denom
    coef_im = (num_im * ar - num_re * ai) / denom
    return abr, abi, coef_re, coef_im


_POW_ROWS = 24


def _ssm_fill_powers(ar_ref, ai_ref, ldt_ref, br_ref, bi_ref, pw_ref, bbar_ref):
    for d in range(2):
        abr, abi, cfr, cfi = _ssm_discretise(ar_ref[d, 0], ai_ref[d, 0], ldt_ref[d, 0])
        bbar_ref[d, 0] = cfr * br_ref[d, 0] - cfi * bi_ref[d, 0]
        bbar_ref[d, 1] = cfr * bi_ref[d, 0] + cfi * br_ref[d, 0]
        pr, pi = jnp.ones_like(abr), jnp.zeros_like(abi)
        for t in range(CHUNK + 1):
            pw_ref[d, 0, t:t + 1, :] = pr
            pw_ref[d, 1, t:t + 1, :] = pi
            pr, pi = pr * abr - pi * abi, pr * abi + pi * abr


def _dot_rounded(a, b, dims=((1,), (0,))):
    return _dot(a.astype(BF16), b.astype(BF16), dims)


def _ssm_stack_inputs(d, pw_ref, bbar_ref, xs_ref):
    for t in range(CHUNK):
        pr, pi = pw_ref[d, 0, t:t + 1, :], pw_ref[d, 1, t:t + 1, :]
        xs_ref[0, t * BLOCK_CH:(t + 1) * BLOCK_CH, :] = bbar_ref[d, 0] * pr - bbar_ref[d, 1] * pi
        xs_ref[1, t * BLOCK_CH:(t + 1) * BLOCK_CH, :] = bbar_ref[d, 0] * pi + bbar_ref[d, 1] * pr


def _eye(n):
    return (lax.broadcasted_iota(jnp.int32, (n, n), 0) == lax.broadcasted_iota(jnp.int32, (n, n), 1)).astype(F32)


def _ssm_param_specs():
    vec = pl.BlockSpec((2, 1, 1, BLOCK_ST), lambda b, j: (0, b, 0, 0))
    mat = pl.BlockSpec((2, 1, BLOCK_CH, SSM_STATE), lambda b, j: (0, b, 0, 0))
    return [vec, vec, vec, mat, mat, mat, mat, pl.BlockSpec((1, 1, BLOCK_CH), lambda b, j: (b, 0, 0))]


def _ssm_block_scratch():
    return [pltpu.VMEM((2, 1, BLOCK_CH, BLOCK_ST), F32)] * 4


def _ssm_chunk_matrices(blk, shards):
    n = len(shards)

    def body(*refs):
        ar_ref, ai_ref, ldt_ref = refs[:3]
        d_ref = refs[7]
        m_ref, ws_ref, wot_ref, a16_ref = refs[8 + n:12 + n]
        pw_ref, bbar_ref, lag_ref, xs_ref = refs[12 + 2 * n:16 + 2 * n]
        br_ref, bi_ref, cr_ref, ci_ref = refs[16 + 2 * n:20 + 2 * n]
        gather = _ChipGather(refs[8:8 + n], refs[12 + n:12 + 2 * n], refs[20 + 2 * n:])
        b, j = pl.program_id(0), pl.program_id(1)
        pl.when((b == 0) & (j == 0))(gather.start)
        pl.when((b == N_BLOCKS - 1) & (j == 0))(gather.forward)
        pl.when((b == N_BLOCKS - 1) & (j == CHUNK - 1))(gather.finish)

        @pl.when(j == 0)
        def _():
            _ssm_expand_blocks(refs[3:7], (br_ref, bi_ref, cr_ref, ci_ref))
            _ssm_fill_powers(ar_ref, ai_ref, ldt_ref, br_ref, bi_ref, pw_ref, bbar_ref)
            zero_lag = d_ref[0] * _eye(BLOCK_CH)
            for d in range(2):
                _ssm_stack_inputs(d, pw_ref, bbar_ref, xs_ref)
                taps = (_dot_rounded(xs_ref[0], cr_ref[d, 0], ((1,), (1,)))
                        - _dot_rounded(xs_ref[1], ci_ref[d, 0], ((1,), (1,))))
                zero_lag = zero_lag + taps[0:BLOCK_CH]
                for t in range(1, CHUNK):
                    lag_ref[CHUNK - 1 + t if d == 0 else CHUNK - 1 - t] = taps[t * BLOCK_CH:(t + 1) * BLOCK_CH]
            lag_ref[CHUNK - 1] = zero_lag
            a16_ref[0] = jnp.concatenate([pw_ref[d, ri, CHUNK:CHUNK + 1, :] for d in range(2) for ri in range(2)], axis=1)

        m_ref[0] = jnp.concatenate([lag_ref[jp - j + CHUNK - 1] for jp in range(CHUNK)], axis=1).astype(BF16)

        def power(d, t):
            return pw_ref[d, 0, pl.ds(t, 1), :], pw_ref[d, 1, pl.ds(t, 1), :]

        parts = []
        for d, t in ((0, CHUNK - 1 - j), (1, j)):
            pr, pi = power(d, t)
            parts += [bbar_ref[d, 0] * pr - bbar_ref[d, 1] * pi, bbar_ref[d, 0] * pi + bbar_ref[d, 1] * pr]
        ws_ref[0] = jnp.concatenate(parts, axis=1).astype(BF16)
        parts = []
        for d, t in ((0, j + 1), (1, CHUNK - j)):
            pr, pi = power(d, t)
            parts += [cr_ref[d, 0] * pr - ci_ref[d, 0] * pi, -cr_ref[d, 0] * pi - ci_ref[d, 0] * pr]
        wot_ref[0] = jnp.concatenate(parts, axis=1).astype(BF16)

    row = pl.BlockSpec((1, BLOCK_CH, CHUNK_W), lambda b, j: (b, j, 0))
    mat = jax.ShapeDtypeStruct((N_BLOCKS, CHUNK_W, CHUNK_W), BF16)
    outs = pl.pallas_call(
        body, name="ssm_chunk_matrices", grid=(N_BLOCKS, CHUNK),
        in_specs=_ssm_param_specs() + _hbm_specs(n),
        out_specs=[row, row, row, pl.BlockSpec((1, 1, STATE_W), lambda b, j: (b, 0, 0))] + _hbm_specs(n),
        out_shape=[mat, mat, mat, jax.ShapeDtypeStruct((N_BLOCKS, 1, STATE_W), F32)] + _gather_out_shapes(shards),
        scratch_shapes=[pltpu.VMEM((2, 2, _POW_ROWS, BLOCK_ST), F32), pltpu.VMEM((2, 2, BLOCK_CH, BLOCK_ST), F32),
                        pltpu.VMEM((2 * CHUNK, BLOCK_CH, BLOCK_CH), F32), pltpu.VMEM((2, CHUNK_W, BLOCK_ST), F32)]
        + _ssm_block_scratch() + _gather_semaphores(n),
        compiler_params=_cparams(("arbitrary", "arbitrary"), has_side_effects=True),
    )(*blk, *shards)
    return outs[:4], outs[4:]


def _ssm_chunk_matrices_bwd(blk, d_m, d_ws, d_wot, d_a16):
    def body(ar_ref, ai_ref, ldt_ref, brc_ref, bic_ref, crc_ref, cic_ref, d_ref, dm_ref, dws_ref, dwot_ref, da16_ref,
             dar_ref, dai_ref, dldt_ref, dbr_ref, dbi_ref, dcr_ref, dci_ref, dd_ref,
             pw_ref, bbar_ref, dlag_ref, dbbar_ref, dc_ref, dpw_ref, xs_ref, dts_ref, br_ref, bi_ref, cr_ref, ci_ref):
        j = pl.program_id(1)
        w = BLOCK_ST

        @pl.when(j == 0)
        def _():
            _ssm_expand_blocks((brc_ref, bic_ref, crc_ref, cic_ref), (br_ref, bi_ref, cr_ref, ci_ref))
            _ssm_fill_powers(ar_ref, ai_ref, ldt_ref, br_ref, bi_ref, pw_ref, bbar_ref)
            for r in (dlag_ref, dbbar_ref, dc_ref, dpw_ref):
                r[...] = jnp.zeros_like(r)

        def fold(t):
            return jnp.sum(t.reshape(BLOCK_CH // 8, 8, w), axis=0)

        def d_power(d, ri, t):
            return jnp.sum(dpw_ref[d, ri, t], axis=0, keepdims=True)

        def x_chain(d, t, dxr, dxi):
            pr, pi = pw_ref[d, 0, pl.ds(t, 1), :], pw_ref[d, 1, pl.ds(t, 1), :]
            bbr, bbi = bbar_ref[d, 0], bbar_ref[d, 1]
            dbbar_ref[d, 0] += dxr * pr + dxi * pi
            dbbar_ref[d, 1] += dxi * pr - dxr * pi
            dpw_ref[d, 0, t] += fold(dxr * bbr + dxi * bbi)
            dpw_ref[d, 1, t] += fold(dxi * bbr - dxr * bbi)

        def z_chain(d, t, dzr, dzi):
            pr, pi = pw_ref[d, 0, pl.ds(t, 1), :], pw_ref[d, 1, pl.ds(t, 1), :]
            c_r, c_i = cr_ref[d, 0], ci_ref[d, 0]
            dc_ref[d, 0] += dzr * pr - dzi * pi
            dc_ref[d, 1] += -dzr * pi - dzi * pr
            dpw_ref[d, 0, t] += fold(dzr * c_r - dzi * c_i)
            dpw_ref[d, 1, t] += fold(-dzr * c_i - dzi * c_r)

        for jp in range(CHUNK):
            dlag_ref[jp - j + CHUNK - 1] += dm_ref[0, :, jp * BLOCK_CH:(jp + 1) * BLOCK_CH].astype(F32)
        quarter = lambda ref, i: ref[0, :, i * w:(i + 1) * w].astype(F32)
        x_chain(0, CHUNK - 1 - j, quarter(dws_ref, 0), quarter(dws_ref, 1))
        x_chain(1, j, quarter(dws_ref, 2), quarter(dws_ref, 3))
        z_chain(0, j + 1, quarter(dwot_ref, 0), quarter(dwot_ref, 1))
        z_chain(1, CHUNK - j, quarter(dwot_ref, 2), quarter(dwot_ref, 3))

        @pl.when(j == CHUNK - 1)
        def _():
            for d in range(2):
                _ssm_stack_inputs(d, pw_ref, bbar_ref, xs_ref)
                for t in range(CHUNK):
                    dts_ref[t * BLOCK_CH:(t + 1) * BLOCK_CH, :] = dlag_ref[CHUNK - 1 + t if d == 0 else CHUNK - 1 - t]
                d_taps = dts_ref[...]
                dc_ref[d, 0] += _dot_rounded(d_taps, xs_ref[0], ((0,), (0,)))
                dc_ref[d, 1] -= _dot_rounded(d_taps, xs_ref[1], ((0,), (0,)))
                xs_ref[0] = _dot_rounded(d_taps, cr_ref[d, 0])
                xs_ref[1] = -_dot_rounded(d_taps, ci_ref[d, 0])
                for t in range(CHUNK):
                    rows = slice(t * BLOCK_CH, (t + 1) * BLOCK_CH)
                    x_chain(d, t, xs_ref[0, rows, :], xs_ref[1, rows, :])
            dd_ref[0] = jnp.sum(dlag_ref[CHUNK - 1] * _eye(BLOCK_CH), axis=0, keepdims=True)
            for d in range(2):
                (abr, abi, cfr, cfi), disc_vjp = jax.vjp(_ssm_discretise, ar_ref[d, 0], ai_ref[d, 0], ldt_ref[d, 0])
                dpr = d_power(d, 0, CHUNK) + da16_ref[0, :, 2 * d * w:(2 * d + 1) * w]
                dpi = d_power(d, 1, CHUNK) + da16_ref[0, :, (2 * d + 1) * w:(2 * d + 2) * w]
                dabr, dabi = jnp.zeros_like(abr), jnp.zeros_like(abi)
                for t in range(CHUNK, 0, -1):
                    qr, qi = pw_ref[d, 0, t - 1:t, :], pw_ref[d, 1, t - 1:t, :]
                    dabr = dabr + dpr * qr + dpi * qi
                    dabi = dabi + dpi * qr - dpr * qi
                    dpr, dpi = (dpr * abr + dpi * abi + d_power(d, 0, t - 1),
                                dpi * abr - dpr * abi + d_power(d, 1, t - 1))
                dbbr, dbbi = dbbar_ref[d, 0], dbbar_ref[d, 1]
                b_r, b_i = br_ref[d, 0], bi_ref[d, 0]
                dbr_ref[d, 0] = _ssm_collapse_block(cfr * dbbr + cfi * dbbi)
                dbi_ref[d, 0] = _ssm_collapse_block(cfr * dbbi - cfi * dbbr)
                dcfr = jnp.sum(b_r * dbbr + b_i * dbbi, axis=0, keepdims=True)
                dcfi = jnp.sum(b_r * dbbi - b_i * dbbr, axis=0, keepdims=True)
                dar_ref[d, 0], dai_ref[d, 0], dldt_ref[d, 0] = disc_vjp((dabr, dabi, dcfr, dcfi))
                dcr_ref[d, 0] = _ssm_collapse_block(dc_ref[d, 0])
                dci_ref[d, 0] = _ssm_collapse_block(dc_ref[d, 1])

    row = pl.BlockSpec((1, BLOCK_CH, CHUNK_W), lambda b, j: (b, j, 0))
    specs = _ssm_param_specs()
    acc = lambda *s: pltpu.VMEM(s, F32)
    return pl.pallas_call(
        body, name="ssm_chunk_matrices_bwd", grid=(N_BLOCKS, CHUNK),
        in_specs=specs + [row, row, row, pl.BlockSpec((1, 1, STATE_W), lambda b, j: (b, 0, 0))],
        out_specs=specs,
        out_shape=[jax.ShapeDtypeStruct(t.shape, F32) for t in blk],
        scratch_shapes=[acc(2, 2, _POW_ROWS, BLOCK_ST), acc(2, 2, BLOCK_CH, BLOCK_ST), acc(2 * CHUNK, BLOCK_CH, BLOCK_CH),
                        acc(2, 2, BLOCK_CH, BLOCK_ST), acc(2, 2, BLOCK_CH, BLOCK_ST), acc(2, 2, CHUNK + 1, 8, BLOCK_ST),
                        acc(2, CHUNK_W, BLOCK_ST), acc(CHUNK_W, BLOCK_CH)] + _ssm_block_scratch(),
        compiler_params=_cparams(("arbitrary", "arbitrary")),
    )(*blk, d_m, d_ws, d_wot, d_a16)


def _block_matmul(terms, name, out_dtype=F32, tn=1024):
    nc = terms[0][0].shape[1]
    n_out = terms[0][1].shape[1] if terms[0][2] else terms[0][1].shape[2]
    flags = [t[2] for t in terms]
    sub = min(tn, 1024)

    def body(*refs):
        out_ref = refs[-1]
        lhs = [refs[2 * t][0].astype(BF16) for t in range(len(flags))]
        for h in range(tn // sub):
            cols = slice(h * sub, (h + 1) * sub)
            acc = None
            for t, transposed in enumerate(flags):
                w_ref = refs[2 * t + 1]
                part = _dot_nt(lhs[t], w_ref[0, cols, :]) if transposed else _dot(lhs[t], w_ref[0, :, cols])
                acc = part if acc is None else acc + part
            out_ref[0, :, cols] = acc.astype(out_dtype)

    in_specs, args = [], []
    for a, w, transposed in terms:
        k = a.shape[2]
        in_specs.append(pl.BlockSpec((1, nc, k), lambda b, n: (b, 0, 0)))
        if transposed:
            in_specs.append(pl.BlockSpec((1, tn, k), lambda b, n: (b, n, 0)))
        else:
            in_specs.append(pl.BlockSpec((1, k, tn), lambda b, n: (b, 0, n)))
        args += [a, w]
    return pl.pallas_call(
        body, name=name, grid=(N_BLOCKS, n_out // tn), in_specs=in_specs,
        out_specs=pl.BlockSpec((1, nc, tn), lambda b, n: (b, 0, n)),
        out_shape=jax.ShapeDtypeStruct((N_BLOCKS, nc, n_out), out_dtype),
        compiler_params=_cparams(("arbitrary", "arbitrary")),
    )(*args)


def _block_matmul_tn(a, b, name, tile=2048):
    nc, m = a.shape[1], a.shape[2]
    n = b.shape[2]

    def body(a_ref, b_ref, out_ref):
        a_t = a_ref[0].astype(BF16)
        for j in range(n // tile):
            cols = slice(j * tile, (j + 1) * tile)
            out_ref[0, :, cols] = _dot_tn(a_t, b_ref[0, :, cols].astype(BF16)).astype(BF16)

    return pl.pallas_call(
        body, name=name, grid=(N_BLOCKS, m // tile),
        in_specs=[pl.BlockSpec((1, nc, tile), lambda blk, i: (blk, 0, i)),
                  pl.BlockSpec((1, nc, n), lambda blk, i: (blk, 0, 0))],
        out_specs=pl.BlockSpec((1, tile, n), lambda blk, i: (blk, i, 0)),
        out_shape=jax.ShapeDtypeStruct((N_BLOCKS, m, n), BF16),
        compiler_params=_cparams(("arbitrary", "arbitrary")),
    )(a, b)


def _cmul(ar, ai, xr, xi):
    return ar * xr - ai * xi, ar * xi + ai * xr


def _cmul_conj(ar, ai, xr, xi):
    return ar * xr + ai * xi, ar * xi - ai * xr


_SCAN_UNROLL = 8


def _ssm_state_scan(s_in, a16):
    nc = s_in.shape[1]
    w = BLOCK_ST

    def body(sin_ref, a_ref, out_ref):
        a = a_ref[0]
        afr, afi, abr, abi = a[:, 0:w], a[:, w:2 * w], a[:, 2 * w:3 * w], a[:, 3 * w:4 * w]

        def step(c, carry):
            fr, fi, br, bi = carry
            cb = nc - 1 - c
            out_ref[0, pl.ds(c, 1), 0:w] = fr
            out_ref[0, pl.ds(c, 1), w:2 * w] = fi
            out_ref[0, pl.ds(cb, 1), 2 * w:3 * w] = br
            out_ref[0, pl.ds(cb, 1), 3 * w:4 * w] = bi
            nfr, nfi = _cmul(afr, afi, fr, fi)
            nbr, nbi = _cmul(abr, abi, br, bi)
            return (nfr + sin_ref[0, pl.ds(c, 1), 0:w], nfi + sin_ref[0, pl.ds(c, 1), w:2 * w],
                    nbr + sin_ref[0, pl.ds(cb, 1), 2 * w:3 * w], nbi + sin_ref[0, pl.ds(cb, 1), 3 * w:4 * w])

        def steps(i, carry):
            for k in range(_SCAN_UNROLL):
                carry = step(i * _SCAN_UNROLL + k, carry)
            return carry

        z = jnp.zeros((1, w), F32)
        lax.fori_loop(0, nc // _SCAN_UNROLL, steps, (z, z, z, z))

    spec = pl.BlockSpec((1, nc, STATE_W), lambda b: (b, 0, 0))
    return pl.pallas_call(
        body, name="ssm_state_scan", grid=(N_BLOCKS,),
        in_specs=[spec, pl.BlockSpec((1, 1, STATE_W), lambda b: (b, 0, 0))],
        out_specs=spec, out_shape=jax.ShapeDtypeStruct(s_in.shape, F32),
        compiler_params=_cparams(("arbitrary",)),
    )(s_in, a16)


def _ssm_state_scan_bwd(d_prev, s_prev, a16):
    nc = d_prev.shape[1]
    w = BLOCK_ST

    def body(dp_ref, sp_ref, a_ref, g_ref, da_ref):
        a = a_ref[0]
        afr, afi, abr, abi = a[:, 0:w], a[:, w:2 * w], a[:, 2 * w:3 * w], a[:, 3 * w:4 * w]

        def step(i, carry):
            gfr, gfi, gbr, gbi, dafr, dafi, dabr, dabi = carry
            cf = nc - 1 - i
            cb = i
            g_ref[0, pl.ds(cf, 1), 0:w] = gfr
            g_ref[0, pl.ds(cf, 1), w:2 * w] = gfi
            g_ref[0, pl.ds(cb, 1), 2 * w:3 * w] = gbr
            g_ref[0, pl.ds(cb, 1), 3 * w:4 * w] = gbi
            sfr, sfi = sp_ref[0, pl.ds(cf, 1), 0:w], sp_ref[0, pl.ds(cf, 1), w:2 * w]
            sbr, sbi = sp_ref[0, pl.ds(cb, 1), 2 * w:3 * w], sp_ref[0, pl.ds(cb, 1), 3 * w:4 * w]
            dafr = dafr + gfr * sfr + gfi * sfi
            dafi = dafi + gfi * sfr - gfr * sfi
            dabr = dabr + gbr * sbr + gbi * sbi
            dabi = dabi + gbi * sbr - gbr * sbi
            nfr, nfi = _cmul_conj(afr, afi, gfr, gfi)
            nbr, nbi = _cmul_conj(abr, abi, gbr, gbi)
            return (nfr + dp_ref[0, pl.ds(cf, 1), 0:w], nfi + dp_ref[0, pl.ds(cf, 1), w:2 * w],
                    nbr + dp_ref[0, pl.ds(cb, 1), 2 * w:3 * w], nbi + dp_ref[0, pl.ds(cb, 1), 3 * w:4 * w],
                    dafr, dafi, dabr, dabi)

        def steps(i, carry):
            for k in range(_SCAN_UNROLL):
                carry = step(i * _SCAN_UNROLL + k, carry)
            return carry

        z = jnp.zeros((1, w), F32)
        res = lax.fori_loop(0, nc // _SCAN_UNROLL, steps, (z,) * 8)
        da_ref[0] = jnp.concatenate(res[4:], axis=1)

    spec = pl.BlockSpec((1, nc, STATE_W), lambda b: (b, 0, 0))
    aspec = pl.BlockSpec((1, 1, STATE_W), lambda b: (b, 0, 0))
    return pl.pallas_call(
        body, name="ssm_state_scan_bwd", grid=(N_BLOCKS,),
        in_specs=[spec, spec, aspec], out_specs=[spec, aspec],
        out_shape=[jax.ShapeDtypeStruct(d_prev.shape, F32), jax.ShapeDtypeStruct((N_BLOCKS, 1, STATE_W), F32)],
        compiler_params=_cparams(("arbitrary",)),
    )(d_prev, s_prev, a16)


NA_PAIR = 2 * GRID_W
NA_WIN_ROWS = NA_ROWS + 2
NA_WIN = NA_WIN_ROWS * GRID_W
NA_PAIRS_PER_STEP = 16
NA_CASES = 5
NA_MASKED = -1e30


def _na_pair_window(m, rows):
    rs0 = jnp.clip(2 * m - NA_ROWS // 2, 0, rows - NA_ROWS)
    ws = jnp.minimum(rs0, rows - NA_WIN_ROWS)
    last = rows // 2 - 1
    case = jnp.where(m == 0, 0, jnp.where(m == 1, 1, jnp.where(m == last - 1, 3, jnp.where(m == last, 4, 2))))
    return ws, case


def _na_row_offsets(rows):
    last = rows // 2 - 1
    geom = []
    for m in (0, 1, 2, last - 1, last):
        ws = min(max(2 * m - NA_ROWS // 2, 0), rows - NA_ROWS, rows - NA_WIN_ROWS)
        per_case = []
        for i in range(NA_WIN_ROWS):
            pair = []
            for rr in range(2):
                r = 2 * m + rr
                rs = min(max(r - NA_ROWS // 2, 0), rows - NA_ROWS)
                pair.append(ws + i - r + NA_ROWS - 1 if rs <= ws + i < rs + NA_ROWS else None)
            per_case.append(pair)
        geom.append(per_case)
    return geom


def _na_col_select():
    qc = np.arange(NA_PAIR)[None, :] % GRID_W
    kc = np.arange(GRID_W)[:, None]
    dc = np.clip(kc - qc + NA_COLS - 1, 0, 2 * NA_COLS - 2)
    return jnp.asarray((np.arange(2 * NA_COLS - 1)[:, None, None] == dc[None]).astype(np.float32))


def _na_bias_rows(rpb):
    return jnp.einsum("hrd,dkl->hrkl", rpb, _na_col_select(), precision=HIGHEST)


def _na_col_window():
    qc = lax.broadcasted_iota(jnp.int32, (GRID_W, NA_PAIR), 1) % GRID_W
    kc = lax.broadcasted_iota(jnp.int32, (GRID_W, NA_PAIR), 0)
    cs = jnp.clip(qc - NA_COLS // 2, 0, GRID_W - NA_COLS)
    first_row = lax.broadcasted_iota(jnp.int32, (GRID_W, NA_PAIR), 1) < GRID_W
    return (kc >= cs) & (kc < cs + NA_COLS), first_row


def _na_bias_table(bias_rows, rows):
    geom = _na_row_offsets(rows)

    def body(br_ref, tab_ref):
        col_ok, first_row = _na_col_window()
        masked = jnp.full((GRID_W, NA_PAIR), NA_MASKED, F32)
        for case in range(NA_CASES):
            for i in range(NA_WIN_ROWS):
                d0, d1 = geom[case][i]
                t0 = masked if d0 is None else br_ref[0, d0]
                t1 = masked if d1 is None else br_ref[0, d1]
                tile = jnp.where(col_ok, jnp.where(first_row, t0, t1), NA_MASKED)
                tab_ref[0, case, i * GRID_W:(i + 1) * GRID_W, :] = tile

    return pl.pallas_call(
        body, name="na_bias_table", grid=(NA_HEADS,),
        in_specs=[pl.BlockSpec((1, 2 * NA_ROWS - 1, GRID_W, NA_PAIR), lambda h: (h, 0, 0, 0))],
        out_specs=pl.BlockSpec((1, NA_CASES, NA_WIN, NA_PAIR), lambda h: (h, 0, 0, 0)),
        out_shape=jax.ShapeDtypeStruct((NA_HEADS, NA_CASES, NA_WIN, NA_PAIR), F32),
        compiler_params=_cparams(("arbitrary",)),
    )(bias_rows)


def _na_bias_table_bwd(d_tab, rows):
    geom = _na_row_offsets(rows)

    def body(dt_ref, dbr_ref):
        col_ok, first_row = _na_col_window()
        acc = [None] * (2 * NA_ROWS - 1)
        for case in range(NA_CASES):
            for i in range(NA_WIN_ROWS):
                tile = jnp.where(col_ok, dt_ref[0, case, i * GRID_W:(i + 1) * GRID_W, :], 0.0)
                for rr, d in enumerate(geom[case][i]):
                    if d is not None:
                        part = jnp.where(first_row if rr == 0 else ~first_row, tile, 0.0)
                        acc[d] = part if acc[d] is None else acc[d] + part
        for d, a in enumerate(acc):
            dbr_ref[0, d] = jnp.zeros((GRID_W, NA_PAIR), F32) if a is None else a

    return pl.pallas_call(
        body, name="na_bias_table_bwd", grid=(NA_HEADS,),
        in_specs=[pl.BlockSpec((1, NA_CASES, NA_WIN, NA_PAIR), lambda h: (h, 0, 0, 0))],
        out_specs=pl.BlockSpec((1, 2 * NA_ROWS - 1, GRID_W, NA_PAIR), lambda h: (h, 0, 0, 0)),
        out_shape=jax.ShapeDtypeStruct((NA_HEADS, 2 * NA_ROWS - 1, GRID_W, NA_PAIR), F32),
        compiler_params=_cparams(("arbitrary",)),
    )(d_tab)


NA_BLK = 64


def _na_blocks():
    return [slice(i * NA_BLK, (i + 1) * NA_BLK) for i in range(NA_WIN // NA_BLK)]


def _na_softmax(qk, bias_ref, hh, case):
    m = jnp.full((NA_BLK, NA_PAIR), -jnp.inf, F32)
    scores = []
    for blk in _na_blocks():
        s = qk[blk, :] + bias_ref[hh, case, blk, :]
        scores.append(s)
        m = jnp.maximum(m, s)
    m = jnp.max(m, axis=0, keepdims=True)
    l = jnp.zeros((NA_BLK, NA_PAIR), F32)
    exps = []
    for s in scores:
        e = jnp.exp(s - m)
        exps.append(e)
        l = l + e
    return exps, jnp.sum(l, axis=0, keepdims=True)


def _na_units(step, rows):
    units = []
    for pp in range(NA_PAIRS_PER_STEP):
        ws, case = _na_pair_window(step * NA_PAIRS_PER_STEP + pp, rows)
        win = pl.ds(pl.multiple_of(ws * GRID_W, NA_PAIR), NA_WIN)
        lanes = slice(pp * NA_PAIR, (pp + 1) * NA_PAIR)
        for hh in range(2):
            units.append((pp, hh, case, win, lanes, slice(hh * NA_HEAD_DIM, (hh + 1) * NA_HEAD_DIM)))
    return units


def _na_pipeline(n, before, middle, after, lookahead):
    for u in range(min(lookahead, n)):
        for f in before:
            f(u)
    for u in range(n):
        middle(u)
        if u + lookahead < n:
            for f in before:
                f(u + lookahead)
        for f in after:
            f(u)


def _head_rows(t, hh):
    row_head = lax.broadcasted_iota(jnp.int32, t.shape, 0) // NA_HEAD_DIM
    return jnp.where(row_head == hh, t, jnp.zeros_like(t))


def _heads_block_diag(t):
    lane_head = lax.broadcasted_iota(jnp.int32, t.shape, 1) // NA_HEAD_DIM
    zero = jnp.zeros_like(t)
    return jnp.concatenate([jnp.where(lane_head == 0, t, zero), jnp.where(lane_head == 1, t, zero)], axis=0)


def _na_fwd(q_t, k, v_t, bias_tab):
    L = k.shape[0]
    rows = L // GRID_W
    step_w = NA_PAIRS_PER_STEP * NA_PAIR

    def body(q_ref, k_ref, v_ref, bt_ref, o_ref):
        units = _na_units(pl.program_id(1), rows)
        qk, probs = {}, {}

        def scores(u):
            _, hh, _, win, lanes, _ = units[u]
            qk[u] = _dot(k_ref[win, :], _head_rows(q_ref[:, lanes], hh))

        def softmax(u):
            _, hh, case, _, _, _ = units[u]
            exps, l = _na_softmax(qk.pop(u), bt_ref, hh, case)
            probs[u] = jnp.concatenate([t.astype(BF16) for t in exps], axis=0), l

        def output(u):
            _, _, _, win, lanes, hrows = units[u]
            e, l = probs.pop(u)
            o_ref[hrows, lanes] = _dot(v_ref[hrows, win], e) / l

        _na_pipeline(len(units), [scores], softmax, [output], lookahead=3)

    q_spec = pl.BlockSpec((NA_PAIR, step_w), lambda h, s: (h, s))
    return pl.pallas_call(
        body, name="na_fwd", grid=(NA_HEADS // 2, L // step_w),
        in_specs=[q_spec, pl.BlockSpec((L, NA_PAIR), lambda h, s: (0, h)),
                  pl.BlockSpec((NA_PAIR, L), lambda h, s: (h, 0)),
                  pl.BlockSpec((2, NA_CASES, NA_WIN, NA_PAIR), lambda h, s: (h, 0, 0, 0))],
        out_specs=q_spec,
        out_shape=jax.ShapeDtypeStruct((D_NA, L), F32),
        compiler_params=_cparams(("arbitrary", "arbitrary")),
    )(q_t, k, v_t, bias_tab)


def _na_bwd(q_t, q, k_t, k, v, bias_tab, out_t, d_out_t, d_out):
    L = k.shape[0]
    rows = L // GRID_W
    step_w = NA_PAIRS_PER_STEP * NA_PAIR

    def body(qt_ref, q_ref, kt_ref, k_ref, v_ref, bt_ref, ot_ref, dot_ref, do_ref, dq_ref, dk_ref, dv_ref, dbt_ref):
        @pl.when(pl.program_id(1) == 0)
        def _():
            dk_ref[...] = jnp.zeros_like(dk_ref)
            dv_ref[...] = jnp.zeros_like(dv_ref)
            dbt_ref[...] = jnp.zeros_like(dbt_ref)

        units = _na_units(pl.program_id(1), rows)
        qk, dp, dsb, pb = {}, {}, {}, {}

        def scores(u):
            _, hh, _, win, lanes, _ = units[u]
            qk[u] = _dot(k_ref[win, :], _head_rows(qt_ref[:, lanes], hh))

        def d_probs(u):
            _, hh, _, win, lanes, _ = units[u]
            dp[u] = _dot(v_ref[win, :], _head_rows(dot_ref[:, lanes].astype(BF16), hh))

        def softmax_bwd(u):
            _, hh, case, _, lanes, hrows = units[u]
            exps, l = _na_softmax(qk.pop(u), bt_ref, hh, case)
            inv_l = 1.0 / l
            delta = jnp.sum(dot_ref[hrows, lanes] * ot_ref[hrows, lanes], axis=0, keepdims=True)
            d_p = dp.pop(u)
            ds_blocks, p_blocks = [], []
            for blk, e in zip(_na_blocks(), exps):
                p = e * inv_l
                ds = p * (d_p[blk, :] - delta)
                dbt_ref[hh, case, blk, :] += ds
                ds_blocks.append(ds.astype(BF16))
                p_blocks.append(p.astype(BF16))
            dsb[u] = jnp.concatenate(ds_blocks, axis=0)
            pb[u] = jnp.concatenate(p_blocks, axis=0)

        def d_query(u):
            _, _, _, win, lanes, hrows = units[u]
            dq_ref[hrows, lanes] = _dot(kt_ref[hrows, win], dsb[u]) * (NA_HEAD_DIM ** -0.5)

        def d_keys_values(u):
            pp, hh, _, win, _, _ = units[u]
            if hh == 1:
                tokens = slice(pp * NA_PAIR, (pp + 1) * NA_PAIR)
                dk_ref[win, :] += _dot(jnp.concatenate([dsb.pop(u - 1), dsb.pop(u)], axis=1), _heads_block_diag(q_ref[tokens, :]))
                dv_ref[win, :] += _dot(jnp.concatenate([pb.pop(u - 1), pb.pop(u)], axis=1), _heads_block_diag(do_ref[tokens, :]))

        _na_pipeline(len(units), [scores, d_probs], softmax_bwd, [d_query, d_keys_values], lookahead=2)

    t_tile = pl.BlockSpec((NA_PAIR, step_w), lambda h, s: (h, s))
    tile = pl.BlockSpec((step_w, NA_PAIR), lambda h, s: (s, h))
    t_full = pl.BlockSpec((NA_PAIR, L), lambda h, s: (h, 0))
    full = pl.BlockSpec((L, NA_PAIR), lambda h, s: (0, h))
    bt = pl.BlockSpec((2, NA_CASES, NA_WIN, NA_PAIR), lambda h, s: (h, 0, 0, 0))
    tok = jax.ShapeDtypeStruct((L, D_NA), F32)
    return pl.pallas_call(
        body, name="na_bwd", grid=(NA_HEADS // 2, L // step_w),
        in_specs=[t_tile, tile, t_full, full, full, bt, t_tile, t_tile, tile],
        out_specs=[t_tile, full, full, bt],
        out_shape=[jax.ShapeDtypeStruct((D_NA, L), F32), tok, tok, jax.ShapeDtypeStruct(bias_tab.shape, F32)],
        compiler_params=_cparams(("arbitrary", "arbitrary")),
    )(q_t, q, k_t, k, v, bias_tab, out_t, d_out_t, d_out)


def _branch_fwd_values(ys, zs, yn, zn, wglu, bglu):
    g1, t = _gelu_parts(ys)
    lin = _dot(g1.astype(BF16), wglu) + bglu
    sg = _sigmoid(lin)
    ys2 = g1 * sg
    sz, szs = _silu_parts(zs)
    sn, sns = _silu_parts(zn)
    return g1, t, sg, ys2, sz, szs, sn, sns


def _branch_fwd(y_ssm_c, z_s, y_na_t, z_n, w_glu, b_glu, tm=512):
    L = z_s.shape[0]

    def body(ys_ref, zs_ref, yn_ref, zn_ref, w_ref, b_ref, cat_ref, scr):
        yn = yn_ref[...].T
        g1, t, sg, ys2, sz, szs, sn, sns = _branch_fwd_values(
            _load_chunks(ys_ref, scr), zs_ref[...], yn, zn_ref[...], w_ref[...], b_ref[...])
        cat_ref[:, 0:512] = (ys2 * sz).astype(BF16)
        cat_ref[:, 512:1024] = (yn * sn).astype(BF16)

    tile = pl.BlockSpec((tm, 512), lambda i: (i, 0))
    return pl.pallas_call(
        body, name="branch_fwd", grid=(L // tm,),
        in_specs=[_chunk_spec(tm), tile, _heads_t_spec(tm), tile, pl.BlockSpec((512, 512), lambda i: (0, 0)),
                  pl.BlockSpec((1, 512), lambda i: (0, 0))],
        out_specs=pl.BlockSpec((tm, 1024), lambda i: (i, 0)),
        out_shape=jax.ShapeDtypeStruct((L, 1024), BF16),
        scratch_shapes=[_chunk_scratch(tm)],
        compiler_params=_cparams(("arbitrary",)),
    )(y_ssm_c, z_s, y_na_t, z_n, w_glu, b_glu)


def _branch_bwd(y_ssm_c, z_s, y_na_t, z_n, w_glu, b_glu, d_cat, tm=512):
    L = z_s.shape[0]

    def body(ys_ref, zs_ref, yn_ref, zn_ref, w_ref, b_ref, dc_ref,
             dys_ref, dzs_ref, dynt_ref, dyn_ref, dzn_ref, dw_ref, db_ref, scr):
        @pl.when(pl.program_id(0) == 0)
        def _():
            dw_ref[...] = jnp.zeros_like(dw_ref)
            db_ref[...] = jnp.zeros_like(db_ref)

        ys, zs, yn, zn = _load_chunks(ys_ref, scr), zs_ref[...], yn_ref[...].T, zn_ref[...]
        w = w_ref[...]
        g1, t, sg, ys2, sz, szs, sn, sns = _branch_fwd_values(ys, zs, yn, zn, w, b_ref[...])
        dys3 = dc_ref[:, 0:512]
        dyn2 = dc_ref[:, 512:1024]
        dzs_ref[...] = (dys3 * ys2 * _silu_grad(zs, szs)).astype(BF16)
        dys2 = dys3 * sz
        dlin = dys2 * g1 * sg * (1.0 - sg)
        dlb = dlin.astype(BF16)
        dg1 = dys2 * sg + _dot_nt(dlb, w)
        dw_ref[...] += _dot_tn(g1.astype(BF16), dlb)
        db_ref[...] += jnp.sum(dlin, axis=0, keepdims=True)
        _store_chunks(dg1 * _gelu_grad(ys, t), scr, dys_ref, BF16)
        dyn = dyn2 * sn
        dynt_ref[...] = dyn.T
        dyn_ref[...] = dyn.astype(BF16)
        dzn_ref[...] = (dyn2 * yn * _silu_grad(zn, sns)).astype(BF16)

    tile = pl.BlockSpec((tm, 512), lambda i: (i, 0))
    wspec = pl.BlockSpec((512, 512), lambda i: (0, 0))
    bspec = pl.BlockSpec((1, 512), lambda i: (0, 0))
    tok = jax.ShapeDtypeStruct((L, 512), BF16)
    return pl.pallas_call(
        body, name="branch_bwd", grid=(L // tm,),
        in_specs=[_chunk_spec(tm), tile, _heads_t_spec(tm), tile, wspec, bspec, pl.BlockSpec((tm, 1024), lambda i: (i, 0))],
        out_specs=[_chunk_spec(tm), tile, _heads_t_spec(tm), tile, tile, wspec, bspec],
        out_shape=[jax.ShapeDtypeStruct((N_BLOCKS, L // CHUNK, CHUNK_W), BF16), tok, jax.ShapeDtypeStruct((D_NA, L), F32),
                   tok, tok,
                   jax.ShapeDtypeStruct((512, 512), F32), jax.ShapeDtypeStruct((1, 512), F32)],
        scratch_shapes=[_chunk_scratch(tm)],
        compiler_params=_cparams(("arbitrary",)),
    )(y_ssm_c, z_s, y_na_t, z_n, w_glu, b_glu, d_cat)


def _head(x, p, target, cat, w_out, g_post, w_ple_g, g_ple, w_pg, tm=512):
    L = x.shape[0]
    pw = w_ple_g.shape[2]

    def body(x_ref, p_ref, t_ref, cat_ref, wo_ref, gpo_ref, wp_ref, gpl_ref, wg_ref,
             loss_ref, dh1_ref, dcat_ref, dwo_ref, dgpo_ref, dwp_ref, dgpl_ref, dwg_ref):
        @pl.when(pl.program_id(0) == 0)
        def _():
            for r in (loss_ref, dwo_ref, dgpo_ref, dwp_ref, dgpl_ref, dwg_ref):
                r[...] = jnp.zeros_like(r)

        cat_b = cat_ref[...]
        wo, wg = wo_ref[...], wg_ref[...]
        g_po, g_pl = gpo_ref[...], gpl_ref[...]
        mix = _dot(cat_b, wo)
        p_b = p_ref[...].astype(BF16)
        ep = jnp.concatenate([_dot(p_b, wp_ref[j]) for j in range(N_CHIPS)], axis=1)
        nm, r2 = _rms(mix)
        h1 = x_ref[...] + nm * g_po
        ne, r3 = _rms(ep)
        e = ne * g_pl
        h1_b = h1.astype(BF16)
        gate = _sigmoid(_dot(h1_b, wg))
        h2 = h1 + gate * e
        diff = h2 - t_ref[...]
        loss_ref[...] += (0.5 / D_MODEL) * jnp.sum(diff * diff).reshape(1, 1)

        dh2 = diff * (1.0 / D_MODEL)
        de = dh2 * gate
        dgl = (dh2 * e * gate * (1.0 - gate)).astype(BF16)
        dh1 = dh2 + _dot_nt(dgl, wg)
        dwg_ref[...] += _dot_tn(h1_b, dgl)
        dgpo_ref[...] += jnp.sum(dh1 * nm, axis=0, keepdims=True)
        dmix = _rms_bwd(dh1 * g_po, nm, r2).astype(BF16)
        dcat_ref[...] = _dot_nt(dmix, wo)
        dwo_ref[...] += _dot_tn(cat_b, dmix)
        dh1_ref[...] = dh1
        dgpl_ref[...] += jnp.sum(de * ne, axis=0, keepdims=True)
        dep = _rms_bwd(de * g_pl, ne, r3).astype(BF16)
        for j in range(N_CHIPS):
            dwp_ref[j] += _dot_tn(p_b, dep[:, j * pw:(j + 1) * pw])

    tile = lambda w: pl.BlockSpec((tm, w), lambda i: (i, 0))
    const = _resident
    sds = jax.ShapeDtypeStruct
    return pl.pallas_call(
        body, name="head", grid=(L // tm,),
        in_specs=[tile(D_MODEL), tile(D_PLE), tile(D_MODEL), tile(1024), const(1024, D_MODEL), const(1, D_MODEL),
                  const(N_CHIPS, D_PLE, pw), const(1, D_MODEL), const(D_MODEL, D_MODEL)],
        out_specs=[const(1, 1), tile(D_MODEL), tile(1024), const(1024, D_MODEL), const(1, D_MODEL),
                   const(N_CHIPS, D_PLE, pw), const(1, D_MODEL), const(D_MODEL, D_MODEL)],
        out_shape=[sds((1, 1), F32), sds((L, D_MODEL), F32), sds((L, 1024), F32), sds((1024, D_MODEL), F32),
                   sds((1, D_MODEL), F32), sds((N_CHIPS, D_PLE, pw), F32), sds((1, D_MODEL), F32),
                   sds((D_MODEL, D_MODEL), F32)],
        compiler_params=_cparams(("arbitrary",)),
    )(x, p, target, cat, w_out, g_post, w_ple_g, g_ple, w_pg)


def _dproj_specs(tm):
    tile = pl.BlockSpec((tm, 512), lambda i: (i, 0))
    return [_chunk_spec(tm), tile, _heads_t_spec(tm), tile, tile, tile]


_DPROJ_ORDER = (3, 4, 5, 1, 2, 0)


def _dproj_part(refs, scr, i):
    if i == 0:
        val = _load_chunks(refs[0], scr)
    elif i == 2:
        val = refs[2][...].T
    else:
        val = refs[i][...]
    return val.astype(BF16)


def _dproj_pieces(i, wn):
    lo, hi = 512 * i, 512 * (i + 1)
    pieces = []
    for j in range(N_CHIPS):
        a, b = max(lo, j * wn), min(hi, (j + 1) * wn)
        if a < b:
            pieces.append((j, slice(a - j * wn, b - j * wn), slice(a - lo, b - lo)))
    return pieces


def _in_proj_bwd_w(x, g_col, w_in_g, dparts, tm=512):
    L = x.shape[0]
    wn = D_IN_PROJ // N_CHIPS
    steps = L // tm

    def body(x_ref, g_ref, w_ref, *refs):
        dw_ref, dg_ref, scr = refs[-3], refs[-2], refs[-1]

        @pl.when(pl.program_id(0) == 0)
        def _():
            dw_ref[...] = jnp.zeros_like(dw_ref)

        n, _ = _rms(x_ref[...])
        nb = n.astype(BF16)
        for i in _DPROJ_ORDER:
            part = _dproj_part(refs[:-3], scr, i)
            for j, w_cols, p_cols in _dproj_pieces(i, wn):
                dw_ref[j, :, w_cols] += _dot_tn(nb, part[:, p_cols])

        @pl.when(pl.program_id(0) == steps - 1)
        def _():
            g = g_ref[...]
            dg = jnp.zeros_like(g)
            for j in range(N_CHIPS):
                a = dw_ref[j]
                dg = dg + jnp.sum(a * w_ref[j].astype(F32), axis=1, keepdims=True)
                dw_ref[j] = a * g
            dg_ref[...] = dg

    return pl.pallas_call(
        body, name="in_proj_bwd_w", grid=(steps,),
        in_specs=[pl.BlockSpec((tm, D_MODEL), lambda i: (i, 0)), _resident(D_MODEL, 1), _resident(N_CHIPS, D_MODEL, wn)]
        + _dproj_specs(tm),
        out_specs=[_resident(N_CHIPS, D_MODEL, wn), _resident(D_MODEL, 1)],
        out_shape=[jax.ShapeDtypeStruct((N_CHIPS, D_MODEL, wn), F32), jax.ShapeDtypeStruct((D_MODEL, 1), F32)],
        scratch_shapes=[_chunk_scratch(tm)],
        compiler_params=_cparams(("arbitrary",)),
    )(x, g_col, w_in_g, *dparts)


def _in_proj_bwd_x(x, g_pre, w_in_g, d_h1, dparts, pair_sums, tm=512):
    L = x.shape[0]
    wn = w_in_g.shape[2]
    n_ps = len(pair_sums)
    steps = L // tm

    def body(*refs):
        x_ref, g_ref, w_ref, dh1_ref = refs[:4]
        dparts_refs = refs[4:10]
        dx_ref = refs[10 + n_ps]
        scr = refs[11 + 2 * n_ps]
        scatter = _ChipScatter(refs[10:10 + n_ps], refs[11 + n_ps:11 + 2 * n_ps], refs[12 + 2 * n_ps:16 + 2 * n_ps],
                               refs[16 + 2 * n_ps:])
        pl.when(pl.program_id(0) == 0)(scatter.start)
        pl.when(pl.program_id(0) == steps - 1)(scatter.finish)

        dhn = None
        for i in _DPROJ_ORDER:
            part = _dproj_part(dparts_refs, scr, i)
            for j, w_cols, p_cols in _dproj_pieces(i, wn):
                term = _dot_nt(part[:, p_cols], w_ref[j, :, w_cols])
                dhn = term if dhn is None else dhn + term
        n, r = _rms(x_ref[...])
        dx_ref[...] = dh1_ref[...] + _rms_bwd(dhn * g_ref[...], n, r)

    wide = pl.BlockSpec((tm, D_MODEL), lambda i: (i, 0))
    outs = pl.pallas_call(
        body, name="in_proj_bwd_x", grid=(steps,),
        in_specs=[wide, _resident(1, D_MODEL), _resident(N_CHIPS, D_MODEL, wn), wide] + _dproj_specs(tm) + _hbm_specs(n_ps),
        out_specs=[wide] + _hbm_specs(n_ps),
        out_shape=[jax.ShapeDtypeStruct((L, D_MODEL), F32)] + [jax.ShapeDtypeStruct(p.shape, p.dtype) for p in pair_sums],
        scratch_shapes=[_chunk_scratch(tm)] + _scatter_scratch(pair_sums),
        compiler_params=_cparams(("arbitrary",), has_side_effects=True),
    )(x, g_pre, w_in_g, d_h1, *dparts, *pair_sums)
    return outs[0], outs[1:]


def _mesh_position():
    x, y, c = lax.axis_index("x"), lax.axis_index("y"), lax.axis_index("c")
    chips = [(1 - x, y), (x, 1 - y), (1 - x, 1 - y)]
    return x, y, c, chips


def _chip_index(cx, cy):
    return 2 * cx + cy


def _hbm_specs(n):
    return [pl.BlockSpec(memory_space=pl.ANY)] * n


def _gather_chips(shards, name):
    n = len(shards)

    def body(*refs):
        gather = _ChipGather(refs[:n], refs[n:2 * n], refs[2 * n:])
        gather.start()
        gather.forward()
        gather.finish()

    return pl.pallas_call(
        body, name=name, in_specs=_hbm_specs(n), out_specs=_hbm_specs(n),
        out_shape=_gather_out_shapes(shards), scratch_shapes=_gather_semaphores(n),
        compiler_params=pltpu.CompilerParams(has_side_effects=True),
    )(*shards)


def _gather_out_shapes(shards):
    return [jax.ShapeDtypeStruct((N_CHIPS,) + s.shape, s.dtype) for s in shards]


def _gather_semaphores(n):
    sem = pltpu.SemaphoreType.DMA
    return [sem((n, 3)), sem((n, 3)), sem((n, 3)), sem((n, 3)), sem((n,)), sem((n,))]


class _ChipGather:
    def __init__(self, ins, outs, sems):
        self.ins, self.outs = ins, outs
        self.send1, self.recv1, self.send2, self.recv2, self.send3, self.recv3 = sems
        self.x, self.y, self.c, self.chips = _mesh_position()
        self.me = _chip_index(self.x, self.y)
        self.sibling = (self.x, self.y, 1 - self.c)

    def _half(self, a, chip, core):
        hr = self.outs[a].shape[1] // 2
        return self.outs[a].at[chip, pl.ds(core * hr, hr)]

    def _own(self, a):
        return pltpu.make_async_remote_copy(
            src_ref=self.ins[a], dst_ref=self.outs[a].at[self.me], send_sem=self.send3.at[a], recv_sem=self.recv3.at[a],
            device_id=self.sibling, device_id_type=MESH)

    def _to_chip(self, a, j):
        hr = self.ins[a].shape[0] // 2
        return pltpu.make_async_remote_copy(
            src_ref=self.ins[a].at[pl.ds(self.c * hr, hr)], dst_ref=self._half(a, self.me, self.c),
            send_sem=self.send1.at[a, j], recv_sem=self.recv1.at[a, j], device_id=(*self.chips[j], self.c), device_id_type=MESH)

    def _from_chip(self, a, j):
        landed = self._half(a, _chip_index(*self.chips[j]), self.c)
        return pltpu.make_async_remote_copy(
            src_ref=landed, dst_ref=landed, send_sem=self.send1.at[a, j], recv_sem=self.recv1.at[a, j],
            device_id=(*self.chips[j], self.c), device_id_type=MESH)

    def _to_sibling(self, a, j, core):
        part = self._half(a, _chip_index(*self.chips[j]), core)
        return pltpu.make_async_remote_copy(
            src_ref=part, dst_ref=part, send_sem=self.send2.at[a, j], recv_sem=self.recv2.at[a, j],
            device_id=self.sibling, device_id_type=MESH)

    def _each(self):
        return [(a, j) for a in range(len(self.ins)) for j in range(3)]

    def start(self):
        for a in range(len(self.ins)):
            self._own(a).start()
        for a, j in self._each():
            self._to_chip(a, j).start()

    def forward(self):
        for a, j in self._each():
            self._from_chip(a, j).wait_recv()
            self._to_sibling(a, j, self.c).start()

    def finish(self):
        for a, j in self._each():
            self._to_sibling(a, j, 1 - self.c).wait_recv()
        for a, j in self._each():
            self._to_chip(a, j).wait_send()
            self._to_sibling(a, j, self.c).wait_send()
        for a in range(len(self.ins)):
            self._own(a).wait()


def _pair_exchange(grads):
    n = len(grads)

    def body(*refs):
        ins, outs = refs[:n], refs[n:2 * n]
        send, recv = refs[2 * n:]
        x, y, c, _ = _mesh_position()
        copies = []
        for a in range(n):
            hr = ins[a].shape[1] // 2
            cp = pltpu.make_async_remote_copy(
                src_ref=ins[a].at[:, pl.ds((1 - c) * hr, hr)], dst_ref=outs[a],
                send_sem=send.at[a], recv_sem=recv.at[a], device_id=(x, y, 1 - c), device_id_type=MESH)
            cp.start()
            copies.append(cp)
        for cp in copies:
            cp.wait()

    sem = pltpu.SemaphoreType.DMA
    return pl.pallas_call(
        body, name="pair_exchange", in_specs=_hbm_specs(n), out_specs=_hbm_specs(n),
        out_shape=[jax.ShapeDtypeStruct((g.shape[0], g.shape[1] // 2, g.shape[2]), g.dtype) for g in grads],
        scratch_shapes=[sem((n,)), sem((n,))],
        compiler_params=pltpu.CompilerParams(has_side_effects=True),
    )(*grads)


def _pair_add(core, grad, other, tr, out_dtype):
    hr = other.shape[1]
    cdim = other.shape[2]
    nb = hr // tr

    def body(core_ref, g_ref, o_ref, out_ref):
        out_ref[...] = (g_ref[...] + o_ref[...]).astype(out_dtype)

    return pl.pallas_call(
        body, name="pair_add",
        grid_spec=pltpu.PrefetchScalarGridSpec(
            num_scalar_prefetch=1, grid=(N_CHIPS, nb),
            in_specs=[pl.BlockSpec((1, tr, cdim), lambda j, i, core_ref: (j, core_ref[0] * nb + i, 0)),
                      pl.BlockSpec((1, tr, cdim), lambda j, i, core_ref: (j, i, 0))],
            out_specs=pl.BlockSpec((1, tr, cdim), lambda j, i, core_ref: (j, i, 0))),
        out_shape=jax.ShapeDtypeStruct(other.shape, out_dtype),
        compiler_params=_cparams(("arbitrary", "arbitrary")),
    )(core, grad, other)


def _scatter_scratch(parts):
    sem = pltpu.SemaphoreType.DMA
    n = len(parts)
    return [sem((n, 3)), sem((n, 3)), sem((n,)), sem((n,))] + [pltpu.VMEM(p.shape[1:], p.dtype) for p in parts]


class _ChipScatter:
    def __init__(self, ins, outs, sems, staged):
        self.ins, self.outs, self.staged = ins, outs, staged
        self.send, self.recv, self.load_sem, self.store_sem = sems
        self.x, self.y, self.c, self.chips = _mesh_position()
        self.me = _chip_index(self.x, self.y)

    def _load(self, a):
        return pltpu.make_async_copy(self.ins[a].at[self.me], self.staged[a], self.load_sem.at[a])

    def _store(self, a):
        return pltpu.make_async_copy(self.staged[a], self.outs[a].at[self.me], self.store_sem.at[a])

    def _to_chip(self, a, j):
        return pltpu.make_async_remote_copy(
            src_ref=self.ins[a].at[_chip_index(*self.chips[j])], dst_ref=self.outs[a].at[self.me],
            send_sem=self.send.at[a, j], recv_sem=self.recv.at[a, j], device_id=(*self.chips[j], self.c), device_id_type=MESH)

    def start(self):
        for a in range(len(self.ins)):
            self._load(a).start()
            for j in range(3):
                self._to_chip(a, j).start()

    def finish(self):
        for a in range(len(self.ins)):
            self._load(a).wait()
            self._store(a).start()
        for a in range(len(self.ins)):
            for j in range(3):
                self._to_chip(a, j).wait()
            self._store(a).wait()


def _chip_add(core, recv, tr):
    hr, cdim = recv.shape[1], recv.shape[2]
    nb = hr // tr

    def body(core_ref, r_ref, out_ref):
        out_ref[...] = ((r_ref[0].astype(F32) + r_ref[1].astype(F32)) + r_ref[2].astype(F32)) + r_ref[3].astype(F32)

    return pl.pallas_call(
        body, name="chip_add",
        grid_spec=pltpu.PrefetchScalarGridSpec(
            num_scalar_prefetch=1, grid=(nb,),
            in_specs=[pl.BlockSpec((N_CHIPS, tr, cdim), lambda i, core_ref: (0, i, 0))],
            out_specs=pl.BlockSpec((tr, cdim), lambda i, core_ref: (core_ref[0] * nb + i, 0))),
        out_shape=jax.ShapeDtypeStruct((2 * hr, cdim), F32),
        compiler_params=_cparams(("arbitrary",)),
    )(core, recv)


def _pair_gather(fulls):
    n = len(fulls)

    def body(*refs):
        outs = refs[n:2 * n]
        send, recv = refs[2 * n:]
        x, y, c, _ = _mesh_position()
        copies = []
        for a in range(n):
            hr = outs[a].shape[0] // 2
            mine = outs[a].at[pl.ds(c * hr, hr)]
            cp = pltpu.make_async_remote_copy(
                src_ref=mine, dst_ref=mine, send_sem=send.at[a], recv_sem=recv.at[a],
                device_id=(x, y, 1 - c), device_id_type=MESH)
            cp.start()
            copies.append(cp)
        for cp in copies:
            cp.wait()

    sem = pltpu.SemaphoreType.DMA
    return pl.pallas_call(
        body, name="pair_gather", in_specs=_hbm_specs(n), out_specs=_hbm_specs(n),
        out_shape=[jax.ShapeDtypeStruct(f.shape, f.dtype) for f in fulls],
        input_output_aliases={a: a for a in range(n)},
        scratch_shapes=[sem((n,)), sem((n,))],
        compiler_params=pltpu.CompilerParams(has_side_effects=True),
    )(*fulls)


def _row_tile(rows):
    if rows <= 512:
        return rows
    for t in (512, 256, 128, 64, 32, 16, 8):
        if rows % t == 0:
            return t
    raise ValueError(rows)


def _pair_sums(core, grads, ici_dtypes):
    others = _pair_exchange(grads)
    return [_pair_add(core, g, o, _row_tile(o.shape[1]), dt) for g, o, dt in zip(grads, others, ici_dtypes)]


def _finish_reduce(core, landed):
    return _pair_gather([_chip_add(core, r, _row_tile(r.shape[1])) for r in landed])


def _adamw(w, g, m, v):
    rows, cols = w.shape
    one_block = rows % 8 != 0 or rows * max(cols, 128) * 4 <= (1 << 20)
    tr = rows if one_block else _row_tile(rows)

    def body(w_ref, g_ref, m_ref, v_ref, d_ref, nm_ref, nv_ref):
        g_ = g_ref[...]
        m_ = ADAM_B1 * m_ref[...] + (1.0 - ADAM_B1) * g_
        v_ = ADAM_B2 * v_ref[...] + (1.0 - ADAM_B2) * (g_ * g_)
        m_hat = m_ / (1.0 - ADAM_B1 ** ADAM_STEP)
        v_hat = v_ / (1.0 - ADAM_B2 ** ADAM_STEP)
        d_ref[...] = -ADAM_LR * (m_hat / (jnp.sqrt(v_hat) + ADAM_EPS) + ADAM_WD * w_ref[...])
        nm_ref[...] = m_
        nv_ref[...] = v_

    spec = pl.BlockSpec((tr, cols), lambda i: (i, 0))
    shp = jax.ShapeDtypeStruct((rows, cols), F32)
    return pl.pallas_call(
        body, name="adamw", grid=(rows // tr,), in_specs=[spec] * 4, out_specs=[spec] * 3,
        out_shape=[shp] * 3, compiler_params=_cparams(("arbitrary",)),
    )(w, g, m, v)


_SMALL = ["norm_pre", "norm_post", "ssm_a_re", "ssm_a_im", "ssm_log_dt", "ssm_b_re", "ssm_b_im",
          "ssm_c_re", "ssm_c_im", "ssm_d", "b_glu", "na_rpb", "ple_norm"]
_BIG = ["w_in", "w_glu", "w_out", "w_ple", "w_ple_gate"]
_WEIGHTS = ["norm_pre", "norm_post", "w_in", "ssm_a_re", "ssm_a_im", "ssm_log_dt", "ssm_b_re", "ssm_b_im",
            "ssm_c_re", "ssm_c_im", "ssm_d", "w_glu", "b_glu", "na_rpb", "w_out", "w_ple", "ple_norm", "w_ple_gate"]
_SMALL_ROWS = 2176


def _pack_small(tensors, tail=None):
    parts = [tensors[n].reshape(-1) for n in _SMALL] + ([] if tail is None else [tail.reshape(-1)])
    flat = jnp.concatenate(parts)
    flat = jnp.pad(flat, (0, _SMALL_ROWS * 128 - flat.shape[0]))
    return flat.reshape(_SMALL_ROWS, 128)


def _unpack_small(packed, shapes):
    flat = packed.reshape(-1)
    out, off = {}, 0
    for n in _SMALL:
        size = int(np.prod(shapes[n]))
        out[n] = flat[off:off + size].reshape(shapes[n])
        off += size
    return out


def _local_grads(x, p, target, wts):
    ssm_names = ["ssm_a_re", "ssm_a_im", "ssm_log_dt", "ssm_b_re", "ssm_b_im", "ssm_c_re", "ssm_c_im", "ssm_d"]
    ssm_params = [wts[n][0] for n in ssm_names]
    blk, blk_vjp = jax.vjp(_ssm_block_params, *ssm_params)
    shard = lambda n: wts[n][0].astype(BF16)
    (m_mat, ws_mat, wot_mat, a16), (w_in_g,) = _ssm_chunk_matrices(blk, [shard("w_in")])
    seq = x.shape[0]
    bias_rows, bias_rows_vjp = jax.vjp(_na_bias_rows, wts["na_rpb"][0])
    bias_tab = _na_bias_table(bias_rows, seq // GRID_W)

    (u_c, z_s, q_t, q, k_t, k, v_t, v, z_n), gathered = _in_proj(
        x, wts["norm_pre"], w_in_g, [shard(n) for n in _BIG if n != "w_in"])
    w_glu, w_out, w_ple_g, w_pg = (gathered[0].reshape(512, 512), gathered[1].reshape(1024, 1024), gathered[2],
                                   gathered[3].reshape(1024, 1024))
    s_in = _block_matmul([(u_c, ws_mat, False)], "ssm_chunk_states", tn=2048)
    s_prev = _ssm_state_scan(s_in, a16)
    y_ssm_c = _block_matmul([(u_c, m_mat, False), (s_prev, wot_mat, True)], "ssm_chunk_out")
    y_na_t = _na_fwd(q_t, k, v_t, bias_tab)
    cat = _branch_fwd(y_ssm_c, z_s, y_na_t, z_n, w_glu, wts["b_glu"])

    (loss, d_h1, d_cat, d_w_out, d_g_post, d_w_ple, d_g_ple, d_w_pg) = _head(
        x, p, target, cat, w_out, wts["norm_post"], w_ple_g, wts["ple_norm"], w_pg)
    dy_c, d_z_s, d_y_na_t, d_y_na, d_z_n, d_w_glu, d_b_glu = _branch_bwd(
        y_ssm_c, z_s, y_na_t, z_n, w_glu, wts["b_glu"], d_cat)
    d_q_t, d_k, d_v, d_bias_tab = _na_bwd(q_t, q, k_t, k, v, bias_tab, y_na_t, d_y_na_t, d_y_na)

    d_prev = _block_matmul([(dy_c, wot_mat, False)], "ssm_bwd_states", tn=2048)
    g_st, d_a16 = _ssm_state_scan_bwd(d_prev, s_prev, a16)
    d_u_c = _block_matmul([(dy_c, m_mat, True), (g_st, ws_mat, True)], "ssm_bwd_in", out_dtype=BF16)
    d_m = _block_matmul_tn(u_c, dy_c, "ssm_grad_m")
    d_ws = _block_matmul_tn(u_c, g_st, "ssm_grad_ws")
    d_wot = _block_matmul_tn(dy_c, s_prev, "ssm_grad_wot")
    d_ssm = blk_vjp(tuple(_ssm_chunk_matrices_bwd(blk, d_m, d_ws, d_wot, d_a16)))
    (d_rpb,) = bias_rows_vjp(_na_bias_table_bwd(d_bias_tab, seq // GRID_W))

    dparts = [d_u_c, d_z_s, d_q_t, d_k, d_v, d_z_n]
    d_w_in, d_g_pre = _in_proj_bwd_w(x, wts["norm_pre"].reshape(D_MODEL, 1), w_in_g, dparts)

    small = {"norm_pre": d_g_pre, "norm_post": d_g_post, "b_glu": d_b_glu, "na_rpb": d_rpb, "ple_norm": d_g_ple}
    for n, g in zip(ssm_names, d_ssm):
        small[n] = g
    big = {"w_in": d_w_in, "w_glu": d_w_glu.reshape(N_CHIPS, 128, 512), "w_out": d_w_out.reshape(N_CHIPS, 256, 1024),
           "w_ple": d_w_ple, "w_ple_gate": d_w_pg.reshape(N_CHIPS, 256, 1024)}
    return loss, small, big, (x, wts["norm_pre"], w_in_g, d_h1, dparts)


def kernel(x, p, norm_pre, norm_post, w_in, ssm_a_re, ssm_a_im, ssm_log_dt, ssm_b_re, ssm_b_im, ssm_c_re, ssm_c_im, ssm_d, w_glu, b_glu, na_rpb, w_out, w_ple, ple_norm, w_ple_gate, loss_target, m_norm_pre, m_norm_post, m_w_in, m_ssm_a_re, m_ssm_a_im, m_ssm_log_dt, m_ssm_b_re, m_ssm_b_im, m_ssm_c_re, m_ssm_c_im, m_ssm_d, m_w_glu, m_b_glu, m_na_rpb, m_w_out, m_w_ple, m_ple_norm, m_w_ple_gate, v_norm_pre, v_norm_post, v_w_in, v_ssm_a_re, v_ssm_a_im, v_ssm_log_dt, v_ssm_b_re, v_ssm_b_im, v_ssm_c_re, v_ssm_c_im, v_ssm_d, v_w_glu, v_b_glu, v_na_rpb, v_w_out, v_w_ple, v_ple_norm, v_w_ple_gate):
    wts = dict(norm_pre=norm_pre, norm_post=norm_post, w_in=w_in, ssm_a_re=ssm_a_re, ssm_a_im=ssm_a_im,
               ssm_log_dt=ssm_log_dt, ssm_b_re=ssm_b_re, ssm_b_im=ssm_b_im, ssm_c_re=ssm_c_re, ssm_c_im=ssm_c_im,
               ssm_d=ssm_d, w_glu=w_glu, b_glu=b_glu, na_rpb=na_rpb, w_out=w_out, w_ple=w_ple, ple_norm=ple_norm,
               w_ple_gate=w_ple_gate)
    mom_m = dict(norm_pre=m_norm_pre, norm_post=m_norm_post, w_in=m_w_in, ssm_a_re=m_ssm_a_re, ssm_a_im=m_ssm_a_im,
                 ssm_log_dt=m_ssm_log_dt, ssm_b_re=m_ssm_b_re, ssm_b_im=m_ssm_b_im, ssm_c_re=m_ssm_c_re,
                 ssm_c_im=m_ssm_c_im, ssm_d=m_ssm_d, w_glu=m_w_glu, b_glu=m_b_glu, na_rpb=m_na_rpb, w_out=m_w_out,
                 w_ple=m_w_ple, ple_norm=m_ple_norm, w_ple_gate=m_w_ple_gate)
    mom_v = dict(norm_pre=v_norm_pre, norm_post=v_norm_post, w_in=v_w_in, ssm_a_re=v_ssm_a_re, ssm_a_im=v_ssm_a_im,
                 ssm_log_dt=v_ssm_log_dt, ssm_b_re=v_ssm_b_re, ssm_b_im=v_ssm_b_im, ssm_c_re=v_ssm_c_re,
                 ssm_c_im=v_ssm_c_im, ssm_d=v_ssm_d, w_glu=v_w_glu, b_glu=v_b_glu, na_rpb=v_na_rpb, w_out=v_w_out,
                 w_ple=v_w_ple, ple_norm=v_ple_norm, w_ple_gate=v_w_ple_gate)

    loss_part, small, big, input_grad_args = _local_grads(x[0], p[0, 0], loss_target[0], wts)

    core = lax.axis_index("c").astype(jnp.int32).reshape(1)
    small_packed = _pack_small(small, tail=loss_part).reshape(N_CHIPS, _SMALL_ROWS // N_CHIPS, 128)
    pair = _pair_sums(core, [big[n] for n in _BIG] + [small_packed], [BF16] * len(_BIG) + [F32])
    grad_x, landed = _in_proj_bwd_x(*input_grad_args, pair)
    reduced = _finish_reduce(core, landed)
    grads = dict(zip(_BIG, reduced[:-1]))
    (small_all,) = _gather_chips([reduced[-1]], "gather_small_grads")
    small_all = small_all.reshape(_SMALL_ROWS, 128)
    loss = small_all.reshape(-1)[sum(int(np.prod(wts[n].shape)) for n in _SMALL)]

    delta, new_m, new_v = {}, {}, {}
    for n in _BIG:
        shp = wts[n].shape
        d_, m_, v_ = _adamw(wts[n][0], grads[n], mom_m[n][0], mom_v[n][0])
        grads[n] = grads[n].reshape(shp)
        delta[n], new_m[n], new_v[n] = d_.reshape(shp), m_.reshape(shp), v_.reshape(shp)
    grads.update(_unpack_small(small_all, {n: wts[n].shape for n in _SMALL}))
    for n in _SMALL:
        shp = wts[n].shape
        swap = shp[-1] < shp[-2]
        view_shape = shp[:-2] + (shp[-1], shp[-2]) if swap else shp
        rows_cols = (int(np.prod(view_shape[:-1])), view_shape[-1])
        view = lambda t: (jnp.swapaxes(t, -1, -2) if swap else t).reshape(rows_cols)
        back = lambda t: jnp.swapaxes(t.reshape(view_shape), -1, -2) if swap else t.reshape(shp)
        d_, m_, v_ = _adamw(*[view(t) for t in (wts[n], grads[n], mom_m[n], mom_v[n])])
        delta[n], new_m[n], new_v[n] = back(d_), back(m_), back(v_)

    return (loss, grad_x[None], *[grads[n] for n in _WEIGHTS], *[delta[n] for n in _WEIGHTS],
            *[new_m[n] for n in _WEIGHTS], *[new_v[n] for n in _WEIGHTS])
```

```python
import math

import jax
import jax.numpy as jnp
import numpy as np
from jax import lax
from jax.experimental import pallas as pl
from jax.experimental.pallas import tpu as pltpu

F32 = jnp.float32
BF16 = jnp.bfloat16

D_MODEL = 1024
D_PLE = 256
GRID_W = 64
D_SSM = 512
SSM_GROUP = 16
N_GROUPS = 32
SSM_STATE = 64
D_NA = 512
NA_HEADS = 8
NA_HEAD_DIM = 64
NA_ROWS = 8
NA_COLS = 16
D_IN_PROJ = 3072
EPS = 1e-6

CHUNK = 16
GROUPS_PER_BLOCK = 8
N_BLOCKS = N_GROUPS // GROUPS_PER_BLOCK
BLOCK_CH = GROUPS_PER_BLOCK * SSM_GROUP
BLOCK_ST = GROUPS_PER_BLOCK * SSM_STATE
CHUNK_W = CHUNK * BLOCK_CH
STATE_W = 4 * BLOCK_ST

N_CHIPS = 4
MESH = pl.DeviceIdType.MESH

ADAM_LR = 0.001
ADAM_B1 = 0.9
ADAM_B2 = 0.999
ADAM_EPS = 1e-08
ADAM_WD = 0.01
ADAM_STEP = 10

VMEM_LIMIT = 52 * 1024 * 1024
HIGHEST = lax.Precision.HIGHEST


def _cparams(sem=None, **kw):
    if sem is not None:
        kw["dimension_semantics"] = sem
    return pltpu.CompilerParams(vmem_limit_bytes=VMEM_LIMIT, **kw)


def _resident(*shape):
    return pl.BlockSpec(shape, lambda *_: (0,) * len(shape), pipeline_mode=pl.Buffered(1))


def _dot(a, b, dims=((1,), (0,))):
    return lax.dot_general(a, b, (dims, ((), ())), preferred_element_type=F32)


def _dot_nt(a, b):
    return _dot(a, b, ((1,), (1,)))


def _dot_tn(a, b):
    return _dot(a, b, ((0,), (0,)))


def _sigmoid(x):
    return 1.0 / (1.0 + jnp.exp(-x))


_GELU_C = math.sqrt(2.0 / math.pi)


def _gelu_parts(x):
    inner = _GELU_C * (x + 0.044715 * (x * x * x))
    t = jnp.tanh(inner)
    return 0.5 * x * (1.0 + t), t


def _gelu_grad(x, t):
    return 0.5 * (1.0 + t) + 0.5 * x * (1.0 - t * t) * (_GELU_C * (1.0 + 3.0 * 0.044715 * x * x))


def _silu_parts(z):
    s = _sigmoid(z)
    return z * s, s


def _silu_grad(z, s):
    return s * (1.0 + z * (1.0 - s))


def _rms(x):
    r = lax.rsqrt(jnp.mean(x * x, axis=-1, keepdims=True) + EPS)
    return x * r, r


def _rms_bwd(dn, n, r):
    return r * (dn - n * jnp.mean(dn * n, axis=-1, keepdims=True))


def _chunk_scratch(tm):
    return pltpu.VMEM((N_BLOCKS, tm, BLOCK_CH), F32)


def _store_chunks(val, scr, c_ref, dtype, row0=0):
    rows = val.shape[0]
    nc, c0 = rows // CHUNK, row0 // CHUNK
    for b in range(N_BLOCKS):
        scr[b, row0:row0 + rows, :] = val[:, b * BLOCK_CH:(b + 1) * BLOCK_CH]
        for j in range(CHUNK):
            c_ref[b, c0:c0 + nc, j * BLOCK_CH:(j + 1) * BLOCK_CH] = scr[b, pl.ds(row0 + j, nc, stride=CHUNK), :].astype(dtype)


def _load_chunks(c_ref, scr):
    nc = scr.shape[1] // CHUNK
    for b in range(N_BLOCKS):
        for j in range(CHUNK):
            scr[b, pl.ds(j, nc, stride=CHUNK), :] = c_ref[b, :, j * BLOCK_CH:(j + 1) * BLOCK_CH].astype(F32)
    return jnp.concatenate([scr[b] for b in range(N_BLOCKS)], axis=1)


def _chunk_spec(tm):
    return pl.BlockSpec((N_BLOCKS, tm // CHUNK, CHUNK_W), lambda i: (0, i, 0))


def _heads_t_spec(tm):
    return pl.BlockSpec((D_NA, tm), lambda i: (0, i))


def _in_proj(x, g_pre, w_in_g, shards, tm=512):
    L = x.shape[0]
    wn = w_in_g.shape[2]
    n_sh = len(shards)
    steps = L // tm

    def body(*refs):
        x_ref, g_ref, w_ref = refs[:3]
        uc_ref, zs_ref, qt_ref, q_ref, kt_ref, k_ref, vt_ref, v_ref, zn_ref = refs[3 + n_sh:12 + n_sh]
        u_scr = refs[12 + 2 * n_sh]
        gather = _ChipGather(refs[3:3 + n_sh], refs[12 + n_sh:12 + 2 * n_sh], refs[13 + 2 * n_sh:])
        step = pl.program_id(0)
        pl.when(step == 0)(gather.start)
        pl.when(step == steps // 2)(gather.forward)
        pl.when(step == steps - 1)(gather.finish)
        halves = [slice(0, tm // 2), slice(tm // 2, tm)]
        hn = [(_rms(x_ref[rows, :])[0] * g_ref[...]).astype(BF16) for rows in halves]
        projs = [jnp.concatenate([_dot(h, w_ref[j]) for j in range(N_CHIPS)], axis=1) for h in hn]
        for rows, proj in zip(halves, projs):
            _store_chunks(proj[:, 0:512], u_scr, uc_ref, BF16, row0=rows.start)
            zs_ref[rows, :] = proj[:, 512:1024]
            q = proj[:, 1024:1536] * (NA_HEAD_DIM ** -0.5)
            for val, t_ref, n_ref in ((q, qt_ref, q_ref), (proj[:, 1536:2048], kt_ref, k_ref), (proj[:, 2048:2560], vt_ref, v_ref)):
                t_ref[:, rows] = val.T.astype(BF16)
                n_ref[rows, :] = val.astype(BF16)
            zn_ref[rows, :] = proj[:, 2560:3072]

    tok = jax.ShapeDtypeStruct((L, 512), F32)
    tr = jax.ShapeDtypeStruct((D_NA, L), BF16)
    hm = jax.ShapeDtypeStruct((L, D_NA), BF16)
    tspec = pl.BlockSpec((tm, 512), lambda i: (i, 0))
    outs = pl.pallas_call(
        body, name="in_proj", grid=(steps,),
        in_specs=[pl.BlockSpec((tm, D_MODEL), lambda i: (i, 0)),
                  _resident(1, D_MODEL), _resident(N_CHIPS, D_MODEL, wn)] + _hbm_specs(n_sh),
        out_specs=[_chunk_spec(tm), tspec] + [_heads_t_spec(tm), tspec] * 3 + [tspec] + _hbm_specs(n_sh),
        out_shape=[jax.ShapeDtypeStruct((N_BLOCKS, L // CHUNK, CHUNK_W), BF16), tok, tr, hm, tr, hm, tr, hm, tok]
        + _gather_out_shapes(shards),
        scratch_shapes=[_chunk_scratch(tm)] + _gather_semaphores(n_sh),
        compiler_params=_cparams(("arbitrary",), has_side_effects=True),
    )(x, g_pre, w_in_g, *shards)
    return outs[:9], outs[9:]


def _ssm_block_params(a_re, a_im, log_dt, b_re, b_im, c_re, c_im, d):
    def lanes(t):
        return t.reshape(2, N_BLOCKS, 1, BLOCK_ST)

    rows = (2, N_BLOCKS, BLOCK_CH, SSM_STATE)
    b_rows = lambda t: t.reshape(2, N_BLOCKS, GROUPS_PER_BLOCK, SSM_STATE, SSM_GROUP).transpose(0, 1, 2, 4, 3).reshape(rows)
    return (lanes(a_re), lanes(a_im), lanes(jnp.broadcast_to(log_dt[..., None], a_re.shape)),
            b_rows(b_re), b_rows(b_im), c_re.reshape(rows), c_im.reshape(rows), d.reshape(N_BLOCKS, 1, BLOCK_CH))


def _ssm_group_mask():
    row_g = lax.broadcasted_iota(jnp.int32, (BLOCK_CH, BLOCK_ST), 0) // SSM_GROUP
    lane_g = lax.broadcasted_iota(jnp.int32, (BLOCK_CH, BLOCK_ST), 1) // SSM_STATE
    return row_g == lane_g


def _ssm_state_select():
    p = lax.broadcasted_iota(jnp.int32, (SSM_STATE, BLOCK_ST), 0)
    lane_p = lax.broadcasted_iota(jnp.int32, (SSM_STATE, BLOCK_ST), 1) % SSM_STATE
    return (p == lane_p).astype(F32)


def _ssm_expand_blocks(compact_refs, full_refs):
    mask, select = _ssm_group_mask(), _ssm_state_select()
    for c_ref, f_ref in zip(compact_refs, full_refs):
        for d in range(2):
            tiled = lax.dot_general(c_ref[d, 0], select, ((((1,), (0,))), ((), ())), precision=HIGHEST,
                                    preferred_element_type=F32)
            f_ref[d, 0] = jnp.where(mask, tiled, 0.0)


def _ssm_collapse_block(t):
    return lax.dot_general(jnp.where(_ssm_group_mask(), t, 0.0), _ssm_state_select(), ((((1,), (1,))), ((), ())),
                           precision=HIGHEST, preferred_element_type=F32)


def _ssm_discretise(ar, ai, ldt):
    dt = jnp.exp(ldt)
    mag = jnp.exp(dt * ar)
    abr = mag * jnp.cos(dt * ai)
    abi = mag * jnp.sin(dt * ai)
    num_re = abr - 1.0
    num_im = abi
    denom = ar * ar + ai * ai
    coef_re = (num_re * ar + num_im * ai) / denom
    coef_im = (num_im * ar - num_re * ai) / denom
    return abr, abi, coef_re, coef_im


_POW_ROWS = 24


def _ssm_fill_powers(ar_ref, ai_ref, ldt_ref, br_ref, bi_ref, pw_ref, bbar_ref):
    for d in range(2):
        abr, abi, cfr, cfi = _ssm_discretise(ar_ref[d, 0], ai_ref[d, 0], ldt_ref[d, 0])
        bbar_ref[d, 0] = cfr * br_ref[d, 0] - cfi * bi_ref[d, 0]
        bbar_ref[d, 1] = cfr * bi_ref[d, 0] + cfi * br_ref[d, 0]
        pr, pi = jnp.ones_like(abr), jnp.zeros_like(abi)
        for t in range(CHUNK + 1):
            pw_ref[d, 0, t:t + 1, :] = pr
            pw_ref[d, 1, t:t + 1, :] = pi
            pr, pi = pr * abr - pi * abi, pr * abi + pi * abr


def _dot_rounded(a, b, dims=((1,), (0,))):
    return _dot(a.astype(BF16), b.astype(BF16), dims)


def _ssm_stack_inputs(d, pw_ref, bbar_ref, xs_ref):
    for t in range(CHUNK):
        pr, pi = pw_ref[d, 0, t:t + 1, :], pw_ref[d, 1, t:t + 1, :]
        xs_ref[0, t * BLOCK_CH:(t + 1) * BLOCK_CH, :] = bbar_ref[d, 0] * pr - bbar_ref[d, 1] * pi
        xs_ref[1, t * BLOCK_CH:(t + 1) * BLOCK_CH, :] = bbar_ref[d, 0] * pi + bbar_ref[d, 1] * pr


def _eye(n):
    return (lax.broadcasted_iota(jnp.int32, (n, n), 0) == lax.broadcasted_iota(jnp.int32, (n, n), 1)).astype(F32)


def _ssm_param_specs():
    vec = pl.BlockSpec((2, 1, 1, BLOCK_ST), lambda b, j: (0, b, 0, 0))
    mat = pl.BlockSpec((2, 1, BLOCK_CH, SSM_STATE), lambda b, j: (0, b, 0, 0))
    return [vec, vec, vec, mat, mat, mat, mat, pl.BlockSpec((1, 1, BLOCK_CH), lambda b, j: (b, 0, 0))]


def _ssm_block_scratch():
    return [pltpu.VMEM((2, 1, BLOCK_CH, BLOCK_ST), F32)] * 4


def _ssm_chunk_matrices(blk, shards):
    n = len(shards)

    def body(*refs):
        ar_ref, ai_ref, ldt_ref = refs[:3]
        d_ref = refs[7]
        m_ref, ws_ref, wot_ref, a16_ref = refs[8 + n:12 + n]
        pw_ref, bbar_ref, lag_ref, xs_ref = refs[12 + 2 * n:16 + 2 * n]
        br_ref, bi_ref, cr_ref, ci_ref = refs[16 + 2 * n:20 + 2 * n]
        gather = _ChipGather(refs[8:8 + n], refs[12 + n:12 + 2 * n], refs[20 + 2 * n:])
        b, j = pl.program_id(0), pl.program_id(1)
        pl.when((b == 0) & (j == 0))(gather.start)
        pl.when((b == N_BLOCKS - 1) & (j == 0))(gather.forward)
        pl.when((b == N_BLOCKS - 1) & (j == CHUNK - 1))(gather.finish)

        @pl.when(j == 0)
        def _():
            _ssm_expand_blocks(refs[3:7], (br_ref, bi_ref, cr_ref, ci_ref))
            _ssm_fill_powers(ar_ref, ai_ref, ldt_ref, br_ref, bi_ref, pw_ref, bbar_ref)
            zero_lag = d_ref[0] * _eye(BLOCK_CH)
            for d in range(2):
                _ssm_stack_inputs(d, pw_ref, bbar_ref, xs_ref)
                taps = (_dot_rounded(xs_ref[0], cr_ref[d, 0], ((1,), (1,)))
                        - _dot_rounded(xs_ref[1], ci_ref[d, 0], ((1,), (1,))))
                zero_lag = zero_lag + taps[0:BLOCK_CH]
                for t in range(1, CHUNK):
                    lag_ref[CHUNK - 1 + t if d == 0 else CHUNK - 1 - t] = taps[t * BLOCK_CH:(t + 1) * BLOCK_CH]
            lag_ref[CHUNK - 1] = zero_lag
            a16_ref[0] = jnp.concatenate([pw_ref[d, ri, CHUNK:CHUNK + 1, :] for d in range(2) for ri in range(2)], axis=1)

        m_ref[0] = jnp.concatenate([lag_ref[jp - j + CHUNK - 1] for jp in range(CHUNK)], axis=1).astype(BF16)

        def power(d, t):
            return pw_ref[d, 0, pl.ds(t, 1), :], pw_ref[d, 1, pl.ds(t, 1), :]

        parts = []
        for d, t in ((0, CHUNK - 1 - j), (1, j)):
            pr, pi = power(d, t)
            parts += [bbar_ref[d, 0] * pr - bbar_ref[d, 1] * pi, bbar_ref[d, 0] * pi + bbar_ref[d, 1] * pr]
        ws_ref[0] = jnp.concatenate(parts, axis=1).astype(BF16)
        parts = []
        for d, t in ((0, j + 1), (1, CHUNK - j)):
            pr, pi = power(d, t)
            parts += [cr_ref[d, 0] * pr - ci_ref[d, 0] * pi, -cr_ref[d, 0] * pi - ci_ref[d, 0] * pr]
        wot_ref[0] = jnp.concatenate(parts, axis=1).astype(BF16)

    row = pl.BlockSpec((1, BLOCK_CH, CHUNK_W), lambda b, j: (b, j, 0))
    mat = jax.ShapeDtypeStruct((N_BLOCKS, CHUNK_W, CHUNK_W), BF16)
    outs = pl.pallas_call(
        body, name="ssm_chunk_matrices", grid=(N_BLOCKS, CHUNK),
        in_specs=_ssm_param_specs() + _hbm_specs(n),
        out_specs=[row, row, row, pl.BlockSpec((1, 1, STATE_W), lambda b, j: (b, 0, 0))] + _hbm_specs(n),
        out_shape=[mat, mat, mat, jax.ShapeDtypeStruct((N_BLOCKS, 1, STATE_W), F32)] + _gather_out_shapes(shards),
        scratch_shapes=[pltpu.VMEM((2, 2, _POW_ROWS, BLOCK_ST), F32), pltpu.VMEM((2, 2, BLOCK_CH, BLOCK_ST), F32),
                        pltpu.VMEM((2 * CHUNK, BLOCK_CH, BLOCK_CH), F32), pltpu.VMEM((2, CHUNK_W, BLOCK_ST), F32)]
        + _ssm_block_scratch() + _gather_semaphores(n),
        compiler_params=_cparams(("arbitrary", "arbitrary"), has_side_effects=True),
    )(*blk, *shards)
    return outs[:4], outs[4:]


def _ssm_chunk_matrices_bwd(blk, d_m, d_ws, d_wot, d_a16):
    def body(ar_ref, ai_ref, ldt_ref, brc_ref, bic_ref, crc_ref, cic_ref, d_ref, dm_ref, dws_ref, dwot_ref, da16_ref,
             dar_ref, dai_ref, dldt_ref, dbr_ref, dbi_ref, dcr_ref, dci_ref, dd_ref,
             pw_ref, bbar_ref, dlag_ref, dbbar_ref, dc_ref, dpw_ref, xs_ref, dts_ref, br_ref, bi_ref, cr_ref, ci_ref):
        j = pl.program_id(1)
        w = BLOCK_ST

        @pl.when(j == 0)
        def _():
            _ssm_expand_blocks((brc_ref, bic_ref, crc_ref, cic_ref), (br_ref, bi_ref, cr_ref, ci_ref))
            _ssm_fill_powers(ar_ref, ai_ref, ldt_ref, br_ref, bi_ref, pw_ref, bbar_ref)
            for r in (dlag_ref, dbbar_ref, dc_ref, dpw_ref):
                r[...] = jnp.zeros_like(r)

        def fold(t):
            return jnp.sum(t.reshape(BLOCK_CH // 8, 8, w), axis=0)

        def d_power(d, ri, t):
            return jnp.sum(dpw_ref[d, ri, t], axis=0, keepdims=True)

        def x_chain(d, t, dxr, dxi):
            pr, pi = pw_ref[d, 0, pl.ds(t, 1), :], pw_ref[d, 1, pl.ds(t, 1), :]
            bbr, bbi = bbar_ref[d, 0], bbar_ref[d, 1]
            dbbar_ref[d, 0] += dxr * pr + dxi * pi
            dbbar_ref[d, 1] += dxi * pr - dxr * pi
            dpw_ref[d, 0, t] += fold(dxr * bbr + dxi * bbi)
            dpw_ref[d, 1, t] += fold(dxi * bbr - dxr * bbi)

        def z_chain(d, t, dzr, dzi):
            pr, pi = pw_ref[d, 0, pl.ds(t, 1), :], pw_ref[d, 1, pl.ds(t, 1), :]
            c_r, c_i = cr_ref[d, 0], ci_ref[d, 0]
            dc_ref[d, 0] += dzr * pr - dzi * pi
            dc_ref[d, 1] += -dzr * pi - dzi * pr
            dpw_ref[d, 0, t] += fold(dzr * c_r - dzi * c_i)
            dpw_ref[d, 1, t] += fold(-dzr * c_i - dzi * c_r)

        for jp in range(CHUNK):
            dlag_ref[jp - j + CHUNK - 1] += dm_ref[0, :, jp * BLOCK_CH:(jp + 1) * BLOCK_CH].astype(F32)
        quarter = lambda ref, i: ref[0, :, i * w:(i + 1) * w].astype(F32)
        x_chain(0, CHUNK - 1 - j, quarter(dws_ref, 0), quarter(dws_ref, 1))
        x_chain(1, j, quarter(dws_ref, 2), quarter(dws_ref, 3))
        z_chain(0, j + 1, quarter(dwot_ref, 0), quarter(dwot_ref, 1))
        z_chain(1, CHUNK - j, quarter(dwot_ref, 2), quarter(dwot_ref, 3))

        @pl.when(j == CHUNK - 1)
        def _():
            for d in range(2):
                _ssm_stack_inputs(d, pw_ref, bbar_ref, xs_ref)
                for t in range(CHUNK):
                    dts_ref[t * BLOCK_CH:(t + 1) * BLOCK_CH, :] = dlag_ref[CHUNK - 1 + t if d == 0 else CHUNK - 1 - t]
                d_taps = dts_ref[...]
                dc_ref[d, 0] += _dot_rounded(d_taps, xs_ref[0], ((0,), (0,)))
                dc_ref[d, 1] -= _dot_rounded(d_taps, xs_ref[1], ((0,), (0,)))
                xs_ref[0] = _dot_rounded(d_taps, cr_ref[d, 0])
                xs_ref[1] = -_dot_rounded(d_taps, ci_ref[d, 0])
                for t in range(CHUNK):
                    rows = slice(t * BLOCK_CH, (t + 1) * BLOCK_CH)
                    x_chain(d, t, xs_ref[0, rows, :], xs_ref[1, rows, :])
            dd_ref[0] = jnp.sum(dlag_ref[CHUNK - 1] * _eye(BLOCK_CH), axis=0, keepdims=True)
            for d in range(2):
                (abr, abi, cfr, cfi), disc_vjp = jax.vjp(_ssm_discretise, ar_ref[d, 0], ai_ref[d, 0], ldt_ref[d, 0])
                dpr = d_power(d, 0, CHUNK) + da16_ref[0, :, 2 * d * w:(2 * d + 1) * w]
                dpi = d_power(d, 1, CHUNK) + da16_ref[0, :, (2 * d + 1) * w:(2 * d + 2) * w]
                dabr, dabi = jnp.zeros_like(abr), jnp.zeros_like(abi)
                for t in range(CHUNK, 0, -1):
                    qr, qi = pw_ref[d, 0, t - 1:t, :], pw_ref[d, 1, t - 1:t, :]
                    dabr = dabr + dpr * qr + dpi * qi
                    dabi = dabi + dpi * qr - dpr * qi
                    dpr, dpi = (dpr * abr + dpi * abi + d_power(d, 0, t - 1),
                                dpi * abr - dpr * abi + d_power(d, 1, t - 1))
                dbbr, dbbi = dbbar_ref[d, 0], dbbar_ref[d, 1]
                b_r, b_i = br_ref[d, 0], bi_ref[d, 0]
                dbr_ref[d, 0] = _ssm_collapse_block(cfr * dbbr + cfi * dbbi)
                dbi_ref[d, 0] = _ssm_collapse_block(cfr * dbbi - cfi * dbbr)
                dcfr = jnp.sum(b_r * dbbr + b_i * dbbi, axis=0, keepdims=True)
                dcfi = jnp.sum(b_r * dbbi - b_i * dbbr, axis=0, keepdims=True)
                dar_ref[d, 0], dai_ref[d, 0], dldt_ref[d, 0] = disc_vjp((dabr, dabi, dcfr, dcfi))
                dcr_ref[d, 0] = _ssm_collapse_block(dc_ref[d, 0])
                dci_ref[d, 0] = _ssm_collapse_block(dc_ref[d, 1])

    row = pl.BlockSpec((1, BLOCK_CH, CHUNK_W), lambda b, j: (b, j, 0))
    specs = _ssm_param_specs()
    acc = lambda *s: pltpu.VMEM(s, F32)
    return pl.pallas_call(
        body, name="ssm_chunk_matrices_bwd", grid=(N_BLOCKS, CHUNK),
        in_specs=specs + [row, row, row, pl.BlockSpec((1, 1, STATE_W), lambda b, j: (b, 0, 0))],
        out_specs=specs,
        out_shape=[jax.ShapeDtypeStruct(t.shape, F32) for t in blk],
        scratch_shapes=[acc(2, 2, _POW_ROWS, BLOCK_ST), acc(2, 2, BLOCK_CH, BLOCK_ST), acc(2 * CHUNK, BLOCK_CH, BLOCK_CH),
                        acc(2, 2, BLOCK_CH, BLOCK_ST), acc(2, 2, BLOCK_CH, BLOCK_ST), acc(2, 2, CHUNK + 1, 8, BLOCK_ST),
                        acc(2, CHUNK_W, BLOCK_ST), acc(CHUNK_W, BLOCK_CH)] + _ssm_block_scratch(),
        compiler_params=_cparams(("arbitrary", "arbitrary")),
    )(*blk, d_m, d_ws, d_wot, d_a16)


def _block_matmul(terms, name, out_dtype=F32, tn=1024):
    nc = terms[0][0].shape[1]
    n_out = terms[0][1].shape[1] if terms[0][2] else terms[0][1].shape[2]
    flags = [t[2] for t in terms]
    sub = min(tn, 1024)

    def body(*refs):
        out_ref = refs[-1]
        lhs = [refs[2 * t][0].astype(BF16) for t in range(len(flags))]
        for h in range(tn // sub):
            cols = slice(h * sub, (h + 1) * sub)
            acc = None
            for t, transposed in enumerate(flags):
                w_ref = refs[2 * t + 1]
                part = _dot_nt(lhs[t], w_ref[0, cols, :]) if transposed else _dot(lhs[t], w_ref[0, :, cols])
                acc = part if acc is None else acc + part
            out_ref[0, :, cols] = acc.astype(out_dtype)

    in_specs, args = [], []
    for a, w, transposed in terms:
        k = a.shape[2]
        in_specs.append(pl.BlockSpec((1, nc, k), lambda b, n: (b, 0, 0)))
        if transposed:
            in_specs.append(pl.BlockSpec((1, tn, k), lambda b, n: (b, n, 0)))
        else:
            in_specs.append(pl.BlockSpec((1, k, tn), lambda b, n: (b, 0, n)))
        args += [a, w]
    return pl.pallas_call(
        body, name=name, grid=(N_BLOCKS, n_out // tn), in_specs=in_specs,
        out_specs=pl.BlockSpec((1, nc, tn), lambda b, n: (b, 0, n)),
        out_shape=jax.ShapeDtypeStruct((N_BLOCKS, nc, n_out), out_dtype),
        compiler_params=_cparams(("arbitrary", "arbitrary")),
    )(*args)


def _block_matmul_tn(a, b, name, tile=2048):
    nc, m = a.shape[1], a.shape[2]
    n = b.shape[2]

    def body(a_ref, b_ref, out_ref):
        a_t = a_ref[0].astype(BF16)
        for j in range(n // tile):
            cols = slice(j * tile, (j + 1) * tile)
            out_ref[0, :, cols] = _dot_tn(a_t, b_ref[0, :, cols].astype(BF16)).astype(BF16)

    return pl.pallas_call(
        body, name=name, grid=(N_BLOCKS, m // tile),
        in_specs=[pl.BlockSpec((1, nc, tile), lambda blk, i: (blk, 0, i)),
                  pl.BlockSpec((1, nc, n), lambda blk, i: (blk, 0, 0))],
        out_specs=pl.BlockSpec((1, tile, n), lambda blk, i: (blk, i, 0)),
        out_shape=jax.ShapeDtypeStruct((N_BLOCKS, m, n), BF16),
        compiler_params=_cparams(("arbitrary", "arbitrary")),
    )(a, b)


def _cmul(ar, ai, xr, xi):
    return ar * xr - ai * xi, ar * xi + ai * xr


def _cmul_conj(ar, ai, xr, xi):
    return ar * xr + ai * xi, ar * xi - ai * xr


_SCAN_UNROLL = 8


def _ssm_state_scan(s_in, a16):
    nc = s_in.shape[1]
    w = BLOCK_ST

    def body(sin_ref, a_ref, out_ref):
        a = a_ref[0]
        afr, afi, abr, abi = a[:, 0:w], a[:, w:2 * w], a[:, 2 * w:3 * w], a[:, 3 * w:4 * w]

        def step(c, carry):
            fr, fi, br, bi = carry
            cb = nc - 1 - c
            out_ref[0, pl.ds(c, 1), 0:w] = fr
            out_ref[0, pl.ds(c, 1), w:2 * w] = fi
            out_ref[0, pl.ds(cb, 1), 2 * w:3 * w] = br
            out_ref[0, pl.ds(cb, 1), 3 * w:4 * w] = bi
            nfr, nfi = _cmul(afr, afi, fr, fi)
            nbr, nbi = _cmul(abr, abi, br, bi)
            return (nfr + sin_ref[0, pl.ds(c, 1), 0:w], nfi + sin_ref[0, pl.ds(c, 1), w:2 * w],
                    nbr + sin_ref[0, pl.ds(cb, 1), 2 * w:3 * w], nbi + sin_ref[0, pl.ds(cb, 1), 3 * w:4 * w])

        def steps(i, carry):
            for k in range(_SCAN_UNROLL):
                carry = step(i * _SCAN_UNROLL + k, carry)
            return carry

        z = jnp.zeros((1, w), F32)
        lax.fori_loop(0, nc // _SCAN_UNROLL, steps, (z, z, z, z))

    spec = pl.BlockSpec((1, nc, STATE_W), lambda b: (b, 0, 0))
    return pl.pallas_call(
        body, name="ssm_state_scan", grid=(N_BLOCKS,),
        in_specs=[spec, pl.BlockSpec((1, 1, STATE_W), lambda b: (b, 0, 0))],
        out_specs=spec, out_shape=jax.ShapeDtypeStruct(s_in.shape, F32),
        compiler_params=_cparams(("arbitrary",)),
    )(s_in, a16)


def _ssm_state_scan_bwd(d_prev, s_prev, a16):
    nc = d_prev.shape[1]
    w = BLOCK_ST

    def body(dp_ref, sp_ref, a_ref, g_ref, da_ref):
        a = a_ref[0]
        afr, afi, abr, abi = a[:, 0:w], a[:, w:2 * w], a[:, 2 * w:3 * w], a[:, 3 * w:4 * w]

        def step(i, carry):
            gfr, gfi, gbr, gbi, dafr, dafi, dabr, dabi = carry
            cf = nc - 1 - i
            cb = i
            g_ref[0, pl.ds(cf, 1), 0:w] = gfr
            g_ref[0, pl.ds(cf, 1), w:2 * w] = gfi
            g_ref[0, pl.ds(cb, 1), 2 * w:3 * w] = gbr
            g_ref[0, pl.ds(cb, 1), 3 * w:4 * w] = gbi
            sfr, sfi = sp_ref[0, pl.ds(cf, 1), 0:w], sp_ref[0, pl.ds(cf, 1), w:2 * w]
            sbr, sbi = sp_ref[0, pl.ds(cb, 1), 2 * w:3 * w], sp_ref[0, pl.ds(cb, 1), 3 * w:4 * w]
            dafr = dafr + gfr * sfr + gfi * sfi
            dafi = dafi + gfi * sfr - gfr * sfi
            dabr = dabr + gbr * sbr + gbi * sbi
            dabi = dabi + gbi * sbr - gbr * sbi
            nfr, nfi = _cmul_conj(afr, afi, gfr, gfi)
            nbr, nbi = _cmul_conj(abr, abi, gbr, gbi)
            return (nfr + dp_ref[0, pl.ds(cf, 1), 0:w], nfi + dp_ref[0, pl.ds(cf, 1), w:2 * w],
                    nbr + dp_ref[0, pl.ds(cb, 1), 2 * w:3 * w], nbi + dp_ref[0, pl.ds(cb, 1), 3 * w:4 * w],
                    dafr, dafi, dabr, dabi)

        def steps(i, carry):
            for k in range(_SCAN_UNROLL):
                carry = step(i * _SCAN_UNROLL + k, carry)
            return carry

        z = jnp.zeros((1, w), F32)
        res = lax.fori_loop(0, nc // _SCAN_UNROLL, steps, (z,) * 8)
        da_ref[0] = jnp.concatenate(res[4:], axis=1)

    spec = pl.BlockSpec((1, nc, STATE_W), lambda b: (b, 0, 0))
    aspec = pl.BlockSpec((1, 1, STATE_W), lambda b: (b, 0, 0))
    return pl.pallas_call(
        body, name="ssm_state_scan_bwd", grid=(N_BLOCKS,),
        in_specs=[spec, spec, aspec], out_specs=[spec, aspec],
        out_shape=[jax.ShapeDtypeStruct(d_prev.shape, F32), jax.ShapeDtypeStruct((N_BLOCKS, 1, STATE_W), F32)],
        compiler_params=_cparams(("arbitrary",)),
    )(d_prev, s_prev, a16)


NA_PAIR = 2 * GRID_W
NA_WIN_ROWS = NA_ROWS + 2
NA_WIN = NA_WIN_ROWS * GRID_W
NA_PAIRS_PER_STEP = 16
NA_CASES = 5
NA_MASKED = -1e30


def _na_pair_window(m, rows):
    rs0 = jnp.clip(2 * m - NA_ROWS // 2, 0, rows - NA_ROWS)
    ws = jnp.minimum(rs0, rows - NA_WIN_ROWS)
    last = rows // 2 - 1
    case = jnp.where(m == 0, 0, jnp.where(m == 1, 1, jnp.where(m == last - 1, 3, jnp.where(m == last, 4, 2))))
    return ws, case


def _na_row_offsets(rows):
    last = rows // 2 - 1
    geom = []
    for m in (0, 1, 2, last - 1, last):
        ws = min(max(2 * m - NA_ROWS // 2, 0), rows - NA_ROWS, rows - NA_WIN_ROWS)
        per_case = []
        for i in range(NA_WIN_ROWS):
            pair = []
            for rr in range(2):
                r = 2 * m + rr
                rs = min(max(r - NA_ROWS // 2, 0), rows - NA_ROWS)
                pair.append(ws + i - r + NA_ROWS - 1 if rs <= ws + i < rs + NA_ROWS else None)
            per_case.append(pair)
        geom.append(per_case)
    return geom


def _na_col_select():
    qc = np.arange(NA_PAIR)[None, :] % GRID_W
    kc = np.arange(GRID_W)[:, None]
    dc = np.clip(kc - qc + NA_COLS - 1, 0, 2 * NA_COLS - 2)
    return jnp.asarray((np.arange(2 * NA_COLS - 1)[:, None, None] == dc[None]).astype(np.float32))


def _na_bias_rows(rpb):
    return jnp.einsum("hrd,dkl->hrkl", rpb, _na_col_select(), precision=HIGHEST)


def _na_col_window():
    qc = lax.broadcasted_iota(jnp.int32, (GRID_W, NA_PAIR), 1) % GRID_W
    kc = lax.broadcasted_iota(jnp.int32, (GRID_W, NA_PAIR), 0)
    cs = jnp.clip(qc - NA_COLS // 2, 0, GRID_W - NA_COLS)
    first_row = lax.broadcasted_iota(jnp.int32, (GRID_W, NA_PAIR), 1) < GRID_W
    return (kc >= cs) & (kc < cs + NA_COLS), first_row


def _na_bias_table(bias_rows, rows):
    geom = _na_row_offsets(rows)

    def body(br_ref, tab_ref):
        col_ok, first_row = _na_col_window()
        masked = jnp.full((GRID_W, NA_PAIR), NA_MASKED, F32)
        for case in range(NA_CASES):
            for i in range(NA_WIN_ROWS):
                d0, d1 = geom[case][i]
                t0 = masked if d0 is None else br_ref[0, d0]
                t1 = masked if d1 is None else br_ref[0, d1]
                tile = jnp.where(col_ok, jnp.where(first_row, t0, t1), NA_MASKED)
                tab_ref[0, case, i * GRID_W:(i + 1) * GRID_W, :] = tile

    return pl.pallas_call(
        body, name="na_bias_table", grid=(NA_HEADS,),
        in_specs=[pl.BlockSpec((1, 2 * NA_ROWS - 1, GRID_W, NA_PAIR), lambda h: (h, 0, 0, 0))],
        out_specs=pl.BlockSpec((1, NA_CASES, NA_WIN, NA_PAIR), lambda h: (h, 0, 0, 0)),
        out_shape=jax.ShapeDtypeStruct((NA_HEADS, NA_CASES, NA_WIN, NA_PAIR), F32),
        compiler_params=_cparams(("arbitrary",)),
    )(bias_rows)


def _na_bias_table_bwd(d_tab, rows):
    geom = _na_row_offsets(rows)

    def body(dt_ref, dbr_ref):
        col_ok, first_row = _na_col_window()
        acc = [None] * (2 * NA_ROWS - 1)
        for case in range(NA_CASES):
            for i in range(NA_WIN_ROWS):
                tile = jnp.where(col_ok, dt_ref[0, case, i * GRID_W:(i + 1) * GRID_W, :], 0.0)
                for rr, d in enumerate(geom[case][i]):
                    if d is not None:
                        part = jnp.where(first_row if rr == 0 else ~first_row, tile, 0.0)
                        acc[d] = part if acc[d] is None else acc[d] + part
        for d, a in enumerate(acc):
            dbr_ref[0, d] = jnp.zeros((GRID_W, NA_PAIR), F32) if a is None else a

    return pl.pallas_call(
        body, name="na_bias_table_bwd", grid=(NA_HEADS,),
        in_specs=[pl.BlockSpec((1, NA_CASES, NA_WIN, NA_PAIR), lambda h: (h, 0, 0, 0))],
        out_specs=pl.BlockSpec((1, 2 * NA_ROWS - 1, GRID_W, NA_PAIR), lambda h: (h, 0, 0, 0)),
        out_shape=jax.ShapeDtypeStruct((NA_HEADS, 2 * NA_ROWS - 1, GRID_W, NA_PAIR), F32),
        compiler_params=_cparams(("arbitrary",)),
    )(d_tab)


NA_BLK = 64


def _na_blocks():
    return [slice(i * NA_BLK, (i + 1) * NA_BLK) for i in range(NA_WIN // NA_BLK)]


def _na_softmax(qk, bias_ref, hh, case):
    m = jnp.full((NA_BLK, NA_PAIR), -jnp.inf, F32)
    scores = []
    for blk in _na_blocks():
        s = qk[blk, :] + bias_ref[hh, case, blk, :]
        scores.append(s)
        m = jnp.maximum(m, s)
    m = jnp.max(m, axis=0, keepdims=True)
    l = jnp.zeros((NA_BLK, NA_PAIR), F32)
    exps = []
    for s in scores:
        e = jnp.exp(s - m)
        exps.append(e)
        l = l + e
    return exps, jnp.sum(l, axis=0, keepdims=True)


def _na_units(step, rows):
    units = []
    for pp in range(NA_PAIRS_PER_STEP):
        ws, case = _na_pair_window(step * NA_PAIRS_PER_STEP + pp, rows)
        win = pl.ds(pl.multiple_of(ws * GRID_W, NA_PAIR), NA_WIN)
        lanes = slice(pp * NA_PAIR, (pp + 1) * NA_PAIR)
        for hh in range(2):
            units.append((pp, hh, case, win, lanes, slice(hh * NA_HEAD_DIM, (hh + 1) * NA_HEAD_DIM)))
    return units


def _na_pipeline(n, before, middle, after, lookahead):
    for u in range(min(lookahead, n)):
        for f in before:
            f(u)
    for u in range(n):
        middle(u)
        if u + lookahead < n:
            for f in before:
                f(u + lookahead)
        for f in after:
            f(u)


def _head_rows(t, hh):
    row_head = lax.broadcasted_iota(jnp.int32, t.shape, 0) // NA_HEAD_DIM
    return jnp.where(row_head == hh, t, jnp.zeros_like(t))


def _heads_block_diag(t):
    lane_head = lax.broadcasted_iota(jnp.int32, t.shape, 1) // NA_HEAD_DIM
    zero = jnp.zeros_like(t)
    return jnp.concatenate([jnp.where(lane_head == 0, t, zero), jnp.where(lane_head == 1, t, zero)], axis=0)


def _na_fwd(q_t, k, v_t, bias_tab):
    L = k.shape[0]
    rows = L // GRID_W
    step_w = NA_PAIRS_PER_STEP * NA_PAIR

    def body(q_ref, k_ref, v_ref, bt_ref, o_ref):
        units = _na_units(pl.program_id(1), rows)
        qk, probs = {}, {}

        def scores(u):
            _, hh, _, win, lanes, _ = units[u]
            qk[u] = _dot(k_ref[win, :], _head_rows(q_ref[:, lanes], hh))

        def softmax(u):
            _, hh, case, _, _, _ = units[u]
            exps, l = _na_softmax(qk.pop(u), bt_ref, hh, case)
            probs[u] = jnp.concatenate([t.astype(BF16) for t in exps], axis=0), l

        def output(u):
            _, _, _, win, lanes, hrows = units[u]
            e, l = probs.pop(u)
            o_ref[hrows, lanes] = _dot(v_ref[hrows, win], e) / l

        _na_pipeline(len(units), [scores], softmax, [output], lookahead=3)

    q_spec = pl.BlockSpec((NA_PAIR, step_w), lambda h, s: (h, s))
    return pl.pallas_call(
        body, name="na_fwd", grid=(NA_HEADS // 2, L // step_w),
        in_specs=[q_spec, pl.BlockSpec((L, NA_PAIR), lambda h, s: (0, h)),
                  pl.BlockSpec((NA_PAIR, L), lambda h, s: (h, 0)),
                  pl.BlockSpec((2, NA_CASES, NA_WIN, NA_PAIR), lambda h, s: (h, 0, 0, 0))],
        out_specs=q_spec,
        out_shape=jax.ShapeDtypeStruct((D_NA, L), F32),
        compiler_params=_cparams(("arbitrary", "arbitrary")),
    )(q_t, k, v_t, bias_tab)


def _na_bwd(q_t, q, k_t, k, v, bias_tab, out_t, d_out_t, d_out):
    L = k.shape[0]
    rows = L // GRID_W
    step_w = NA_PAIRS_PER_STEP * NA_PAIR

    def body(qt_ref, q_ref, kt_ref, k_ref, v_ref, bt_ref, ot_ref, dot_ref, do_ref, dq_ref, dk_ref, dv_ref, dbt_ref):
        @pl.when(pl.program_id(1) == 0)
        def _():
            dk_ref[...] = jnp.zeros_like(dk_ref)
            dv_ref[...] = jnp.zeros_like(dv_ref)
            dbt_ref[...] = jnp.zeros_like(dbt_ref)

        units = _na_units(pl.program_id(1), rows)
        qk, dp, dsb, pb = {}, {}, {}, {}

        def scores(u):
            _, hh, _, win, lanes, _ = units[u]
            qk[u] = _dot(k_ref[win, :], _head_rows(qt_ref[:, lanes], hh))

        def d_probs(u):
            _, hh, _, win, lanes, _ = units[u]
            dp[u] = _dot(v_ref[win, :], _head_rows(dot_ref[:, lanes].astype(BF16), hh))

        def softmax_bwd(u):
            _, hh, case, _, lanes, hrows = units[u]
            exps, l = _na_softmax(qk.pop(u), bt_ref, hh, case)
            inv_l = 1.0 / l
            delta = jnp.sum(dot_ref[hrows, lanes] * ot_ref[hrows, lanes], axis=0, keepdims=True)
            d_p = dp.pop(u)
            ds_blocks, p_blocks = [], []
            for blk, e in zip(_na_blocks(), exps):
                p = e * inv_l
                ds = p * (d_p[blk, :] - delta)
                dbt_ref[hh, case, blk, :] += ds
                ds_blocks.append(ds.astype(BF16))
                p_blocks.append(p.astype(BF16))
            dsb[u] = jnp.concatenate(ds_blocks, axis=0)
            pb[u] = jnp.concatenate(p_blocks, axis=0)

        def d_query(u):
            _, _, _, win, lanes, hrows = units[u]
            dq_ref[hrows, lanes] = _dot(kt_ref[hrows, win], dsb[u]) * (NA_HEAD_DIM ** -0.5)

        def d_keys_values(u):
            pp, hh, _, win, _, _ = units[u]
            if hh == 1:
                tokens = slice(pp * NA_PAIR, (pp + 1) * NA_PAIR)
                dk_ref[win, :] += _dot(jnp.concatenate([dsb.pop(u - 1), dsb.pop(u)], axis=1), _heads_block_diag(q_ref[tokens, :]))
                dv_ref[win, :] += _dot(jnp.concatenate([pb.pop(u - 1), pb.pop(u)], axis=1), _heads_block_diag(do_ref[tokens, :]))

        _na_pipeline(len(units), [scores, d_probs], softmax_bwd, [d_query, d_keys_values], lookahead=2)

    t_tile = pl.BlockSpec((NA_PAIR, step_w), lambda h, s: (h, s))
    tile = pl.BlockSpec((step_w, NA_PAIR), lambda h, s: (s, h))
    t_full = pl.BlockSpec((NA_PAIR, L), lambda h, s: (h, 0))
    full = pl.BlockSpec((L, NA_PAIR), lambda h, s: (0, h))
    bt = pl.BlockSpec((2, NA_CASES, NA_WIN, NA_PAIR), lambda h, s: (h, 0, 0, 0))
    tok = jax.ShapeDtypeStruct((L, D_NA), F32)
    return pl.pallas_call(
        body, name="na_bwd", grid=(NA_HEADS // 2, L // step_w),
        in_specs=[t_tile, tile, t_full, full, full, bt, t_tile, t_tile, tile],
        out_specs=[t_tile, full, full, bt],
        out_shape=[jax.ShapeDtypeStruct((D_NA, L), F32), tok, tok, jax.ShapeDtypeStruct(bias_tab.shape, F32)],
        compiler_params=_cparams(("arbitrary", "arbitrary")),
    )(q_t, q, k_t, k, v, bias_tab, out_t, d_out_t, d_out)


def _branch_fwd_values(ys, zs, yn, zn, wglu, bglu):
    g1, t = _gelu_parts(ys)
    lin = _dot(g1.astype(BF16), wglu) + bglu
    sg = _sigmoid(lin)
    ys2 = g1 * sg
    sz, szs = _silu_parts(zs)
    sn, sns = _silu_parts(zn)
    return g1, t, sg, ys2, sz, szs, sn, sns


def _branch_fwd(y_ssm_c, z_s, y_na_t, z_n, w_glu, b_glu, tm=512):
    L = z_s.shape[0]

    def body(ys_ref, zs_ref, yn_ref, zn_ref, w_ref, b_ref, cat_ref, scr):
        yn = yn_ref[...].T
        g1, t, sg, ys2, sz, szs, sn, sns = _branch_fwd_values(
            _load_chunks(ys_ref, scr), zs_ref[...], yn, zn_ref[...], w_ref[...], b_ref[...])
        cat_ref[:, 0:512] = (ys2 * sz).astype(BF16)
        cat_ref[:, 512:1024] = (yn * sn).astype(BF16)

    tile = pl.BlockSpec((tm, 512), lambda i: (i, 0))
    return pl.pallas_call(
        body, name="branch_fwd", grid=(L // tm,),
        in_specs=[_chunk_spec(tm), tile, _heads_t_spec(tm), tile, pl.BlockSpec((512, 512), lambda i: (0, 0)),
                  pl.BlockSpec((1, 512), lambda i: (0, 0))],
        out_specs=pl.BlockSpec((tm, 1024), lambda i: (i, 0)),
        out_shape=jax.ShapeDtypeStruct((L, 1024), BF16),
        scratch_shapes=[_chunk_scratch(tm)],
        compiler_params=_cparams(("arbitrary",)),
    )(y_ssm_c, z_s, y_na_t, z_n, w_glu, b_glu)


def _branch_bwd(y_ssm_c, z_s, y_na_t, z_n, w_glu, b_glu, d_cat, tm=512):
    L = z_s.shape[0]

    def body(ys_ref, zs_ref, yn_ref, zn_ref, w_ref, b_ref, dc_ref,
             dys_ref, dzs_ref, dynt_ref, dyn_ref, dzn_ref, dw_ref, db_ref, scr):
        @pl.when(pl.program_id(0) == 0)
        def _():
            dw_ref[...] = jnp.zeros_like(dw_ref)
            db_ref[...] = jnp.zeros_like(db_ref)

        ys, zs, yn, zn = _load_chunks(ys_ref, scr), zs_ref[...], yn_ref[...].T, zn_ref[...]
        w = w_ref[...]
        g1, t, sg, ys2, sz, szs, sn, sns = _branch_fwd_values(ys, zs, yn, zn, w, b_ref[...])
        dys3 = dc_ref[:, 0:512]
        dyn2 = dc_ref[:, 512:1024]
        dzs_ref[...] = (dys3 * ys2 * _silu_grad(zs, szs)).astype(BF16)
        dys2 = dys3 * sz
        dlin = dys2 * g1 * sg * (1.0 - sg)
        dlb = dlin.astype(BF16)
        dg1 = dys2 * sg + _dot_nt(dlb, w)
        dw_ref[...] += _dot_tn(g1.astype(BF16), dlb)
        db_ref[...] += jnp.sum(dlin, axis=0, keepdims=True)
        _store_chunks(dg1 * _gelu_grad(ys, t), scr, dys_ref, BF16)
        dyn = dyn2 * sn
        dynt_ref[...] = dyn.T
        dyn_ref[...] = dyn.astype(BF16)
        dzn_ref[...] = (dyn2 * yn * _silu_grad(zn, sns)).astype(BF16)

    tile = pl.BlockSpec((tm, 512), lambda i: (i, 0))
    wspec = pl.BlockSpec((512, 512), lambda i: (0, 0))
    bspec = pl.BlockSpec((1, 512), lambda i: (0, 0))
    tok = jax.ShapeDtypeStruct((L, 512), BF16)
    return pl.pallas_call(
        body, name="branch_bwd", grid=(L // tm,),
        in_specs=[_chunk_spec(tm), tile, _heads_t_spec(tm), tile, wspec, bspec, pl.BlockSpec((tm, 1024), lambda i: (i, 0))],
        out_specs=[_chunk_spec(tm), tile, _heads_t_spec(tm), tile, tile, wspec, bspec],
        out_shape=[jax.ShapeDtypeStruct((N_BLOCKS, L // CHUNK, CHUNK_W), BF16), tok, jax.ShapeDtypeStruct((D_NA, L), F32),
                   tok, tok,
                   jax.ShapeDtypeStruct((512, 512), F32), jax.ShapeDtypeStruct((1, 512), F32)],
        scratch_shapes=[_chunk_scratch(tm)],
        compiler_params=_cparams(("arbitrary",)),
    )(y_ssm_c, z_s, y_na_t, z_n, w_glu, b_glu, d_cat)


def _head(x, p, target, cat, w_out, g_post, w_ple_g, g_ple, w_pg, tm=512):
    L = x.shape[0]
    pw = w_ple_g.shape[2]

    def body(x_ref, p_ref, t_ref, cat_ref, wo_ref, gpo_ref, wp_ref, gpl_ref, wg_ref,
             loss_ref, dh1_ref, dcat_ref, dwo_ref, dgpo_ref, dwp_ref, dgpl_ref, dwg_ref):
        @pl.when(pl.program_id(0) == 0)
        def _():
            for r in (loss_ref, dwo_ref, dgpo_ref, dwp_ref, dgpl_ref, dwg_ref):
                r[...] = jnp.zeros_like(r)

        cat_b = cat_ref[...]
        wo, wg = wo_ref[...], wg_ref[...]
        g_po, g_pl = gpo_ref[...], gpl_ref[...]
        mix = _dot(cat_b, wo)
        p_b = p_ref[...].astype(BF16)
        ep = jnp.concatenate([_dot(p_b, wp_ref[j]) for j in range(N_CHIPS)], axis=1)
        nm, r2 = _rms(mix)
        h1 = x_ref[...] + nm * g_po
        ne, r3 = _rms(ep)
        e = ne * g_pl
        h1_b = h1.astype(BF16)
        gate = _sigmoid(_dot(h1_b, wg))
        h2 = h1 + gate * e
        diff = h2 - t_ref[...]
        loss_ref[...] += (0.5 / D_MODEL) * jnp.sum(diff * diff).reshape(1, 1)

        dh2 = diff * (1.0 / D_MODEL)
        de = dh2 * gate
        dgl = (dh2 * e * gate * (1.0 - gate)).astype(BF16)
        dh1 = dh2 + _dot_nt(dgl, wg)
        dwg_ref[...] += _dot_tn(h1_b, dgl)
        dgpo_ref[...] += jnp.sum(dh1 * nm, axis=0, keepdims=True)
        dmix = _rms_bwd(dh1 * g_po, nm, r2).astype(BF16)
        dcat_ref[...] = _dot_nt(dmix, wo)
        dwo_ref[...] += _dot_tn(cat_b, dmix)
        dh1_ref[...] = dh1
        dgpl_ref[...] += jnp.sum(de * ne, axis=0, keepdims=True)
        dep = _rms_bwd(de * g_pl, ne, r3).astype(BF16)
        for j in range(N_CHIPS):
            dwp_ref[j] += _dot_tn(p_b, dep[:, j * pw:(j + 1) * pw])

    tile = lambda w: pl.BlockSpec((tm, w), lambda i: (i, 0))
    const = _resident
    sds = jax.ShapeDtypeStruct
    return pl.pallas_call(
        body, name="head", grid=(L // tm,),
        in_specs=[tile(D_MODEL), tile(D_PLE), tile(D_MODEL), tile(1024), const(1024, D_MODEL), const(1, D_MODEL),
                  const(N_CHIPS, D_PLE, pw), const(1, D_MODEL), const(D_MODEL, D_MODEL)],
        out_specs=[const(1, 1), tile(D_MODEL), tile(1024), const(1024, D_MODEL), const(1, D_MODEL),
                   const(N_CHIPS, D_PLE, pw), const(1, D_MODEL), const(D_MODEL, D_MODEL)],
        out_shape=[sds((1, 1), F32), sds((L, D_MODEL), F32), sds((L, 1024), F32), sds((1024, D_MODEL), F32),
                   sds((1, D_MODEL), F32), sds((N_CHIPS, D_PLE, pw), F32), sds((1, D_MODEL), F32),
                   sds((D_MODEL, D_MODEL), F32)],
        compiler_params=_cparams(("arbitrary",)),
    )(x, p, target, cat, w_out, g_post, w_ple_g, g_ple, w_pg)


def _dproj_specs(tm):
    tile = pl.BlockSpec((tm, 512), lambda i: (i, 0))
    return [_chunk_spec(tm), tile, _heads_t_spec(tm), tile, tile, tile]


_DPROJ_ORDER = (3, 4, 5, 1, 2, 0)


def _dproj_part(refs, scr, i):
    if i == 0:
        val = _load_chunks(refs[0], scr)
    elif i == 2:
        val = refs[2][...].T
    else:
        val = refs[i][...]
    return val.astype(BF16)


def _dproj_pieces(i, wn):
    lo, hi = 512 * i, 512 * (i + 1)
    pieces = []
    for j in range(N_CHIPS):
        a, b = max(lo, j * wn), min(hi, (j + 1) * wn)
        if a < b:
            pieces.append((j, slice(a - j * wn, b - j * wn), slice(a - lo, b - lo)))
    return pieces


def _in_proj_bwd_w(x, g_col, w_in_g, dparts, tm=512):
    L = x.shape[0]
    wn = D_IN_PROJ // N_CHIPS
    steps = L // tm

    def body(x_ref, g_ref, w_ref, *refs):
        dw_ref, dg_ref, scr = refs[-3], refs[-2], refs[-1]

        @pl.when(pl.program_id(0) == 0)
        def _():
            dw_ref[...] = jnp.zeros_like(dw_ref)

        n, _ = _rms(x_ref[...])
        nb = n.astype(BF16)
        for i in _DPROJ_ORDER:
            part = _dproj_part(refs[:-3], scr, i)
            for j, w_cols, p_cols in _dproj_pieces(i, wn):
                dw_ref[j, :, w_cols] += _dot_tn(nb, part[:, p_cols])

        @pl.when(pl.program_id(0) == steps - 1)
        def _():
            g = g_ref[...]
            dg = jnp.zeros_like(g)
            for j in range(N_CHIPS):
                a = dw_ref[j]
                dg = dg + jnp.sum(a * w_ref[j].astype(F32), axis=1, keepdims=True)
                dw_ref[j] = a * g
            dg_ref[...] = dg

    return pl.pallas_call(
        body, name="in_proj_bwd_w", grid=(steps,),
        in_specs=[pl.BlockSpec((tm, D_MODEL), lambda i: (i, 0)), _resident(D_MODEL, 1), _resident(N_CHIPS, D_MODEL, wn)]
        + _dproj_specs(tm),
        out_specs=[_resident(N_CHIPS, D_MODEL, wn), _resident(D_MODEL, 1)],
        out_shape=[jax.ShapeDtypeStruct((N_CHIPS, D_MODEL, wn), F32), jax.ShapeDtypeStruct((D_MODEL, 1), F32)],
        scratch_shapes=[_chunk_scratch(tm)],
        compiler_params=_cparams(("arbitrary",)),
    )(x, g_col, w_in_g, *dparts)


_IN_BUFFERS = 3


def _in_proj_bwd_x(x, g_pre, w_in_g, d_h1, dparts, pair_sums, tm=512):
    L = x.shape[0]
    wn = w_in_g.shape[2]
    n_ps = len(pair_sums)
    steps = L // tm

    def body(*refs):
        x_hbm, g_ref, w_ref, dh1_hbm = refs[:4]
        dparts_refs = refs[4:10]
        dx_ref = refs[10 + n_ps]
        scr = refs[11 + 2 * n_ps]
        scatter = _ChipScatter(refs[10:10 + n_ps], refs[11 + n_ps:11 + 2 * n_ps], refs[12 + 2 * n_ps:16 + 2 * n_ps],
                               refs[16 + 2 * n_ps:16 + 3 * n_ps])
        x_buf, dh1_buf, x_sem, dh1_sem = refs[16 + 3 * n_ps:]
        step = pl.program_id(0)
        pl.when(step == 0)(scatter.start)
        pl.when(step == steps - 1)(scatter.finish)

        def tile_copies(t):
            rows = pl.ds(pl.multiple_of(t * tm, tm), tm)
            slot = t % _IN_BUFFERS
            return (pltpu.make_async_copy(x_hbm.at[rows], x_buf.at[slot], x_sem.at[slot]),
                    pltpu.make_async_copy(dh1_hbm.at[rows], dh1_buf.at[slot], dh1_sem.at[slot]))

        @pl.when(step == 0)
        def _():
            for t in range(_IN_BUFFERS - 1):
                for cp in tile_copies(t):
                    cp.start()

        @pl.when(step + _IN_BUFFERS - 1 < steps)
        def _():
            for cp in tile_copies(step + _IN_BUFFERS - 1):
                cp.start()

        for cp in tile_copies(step):
            cp.wait()
        x_ref, dh1_ref = x_buf.at[step % _IN_BUFFERS], dh1_buf.at[step % _IN_BUFFERS]

        dhn = None
        for i in _DPROJ_ORDER:
            part = _dproj_part(dparts_refs, scr, i)
            for j, w_cols, p_cols in _dproj_pieces(i, wn):
                term = _dot_nt(part[:, p_cols], w_ref[j, :, w_cols])
                dhn = term if dhn is None else dhn + term
        n, r = _rms(x_ref[...])
        dx_ref[...] = dh1_ref[...] + _rms_bwd(dhn * g_ref[...], n, r)

    wide = pl.BlockSpec((tm, D_MODEL), lambda i: (i, 0))
    hbm = pl.BlockSpec(memory_space=pl.ANY)
    tiles = pltpu.VMEM((_IN_BUFFERS, tm, D_MODEL), F32)
    sems = pltpu.SemaphoreType.DMA((_IN_BUFFERS,))
    outs = pl.pallas_call(
        body, name="in_proj_bwd_x", grid=(steps,),
        in_specs=[hbm, _resident(1, D_MODEL), _resident(N_CHIPS, D_MODEL, wn), hbm] + _dproj_specs(tm) + _hbm_specs(n_ps),
        out_specs=[wide] + _hbm_specs(n_ps),
        out_shape=[jax.ShapeDtypeStruct((L, D_MODEL), F32)] + [jax.ShapeDtypeStruct(p.shape, p.dtype) for p in pair_sums],
        scratch_shapes=[_chunk_scratch(tm)] + _scatter_scratch(pair_sums) + [tiles, tiles, sems, sems],
        compiler_params=_cparams(("arbitrary",), has_side_effects=True),
    )(x, g_pre, w_in_g, d_h1, *dparts, *pair_sums)
    return outs[0], outs[1:]


def _mesh_position():
    x, y, c = lax.axis_index("x"), lax.axis_index("y"), lax.axis_index("c")
    chips = [(1 - x, y), (x, 1 - y), (1 - x, 1 - y)]
    return x, y, c, chips


def _chip_index(cx, cy):
    return 2 * cx + cy


def _hbm_specs(n):
    return [pl.BlockSpec(memory_space=pl.ANY)] * n


def _gather_chips(shards, name):
    n = len(shards)

    def body(*refs):
        gather = _ChipGather(refs[:n], refs[n:2 * n], refs[2 * n:])
        gather.start()
        gather.forward()
        gather.finish()

    return pl.pallas_call(
        body, name=name, in_specs=_hbm_specs(n), out_specs=_hbm_specs(n),
        out_shape=_gather_out_shapes(shards), scratch_shapes=_gather_semaphores(n),
        compiler_params=pltpu.CompilerParams(has_side_effects=True),
    )(*shards)


def _gather_out_shapes(shards):
    return [jax.ShapeDtypeStruct((N_CHIPS,) + s.shape, s.dtype) for s in shards]


def _gather_semaphores(n):
    sem = pltpu.SemaphoreType.DMA
    return [sem((n, 3)), sem((n, 3)), sem((n, 3)), sem((n, 3)), sem((n,)), sem((n,))]


class _ChipGather:
    def __init__(self, ins, outs, sems):
        self.ins, self.outs = ins, outs
        self.send1, self.recv1, self.send2, self.recv2, self.send3, self.recv3 = sems
        self.x, self.y, self.c, self.chips = _mesh_position()
        self.me = _chip_index(self.x, self.y)
        self.sibling = (self.x, self.y, 1 - self.c)

    def _half(self, a, chip, core):
        hr = self.outs[a].shape[1] // 2
        return self.outs[a].at[chip, pl.ds(core * hr, hr)]

    def _own(self, a):
        return pltpu.make_async_remote_copy(
            src_ref=self.ins[a], dst_ref=self.outs[a].at[self.me], send_sem=self.send3.at[a], recv_sem=self.recv3.at[a],
            device_id=self.sibling, device_id_type=MESH)

    def _to_chip(self, a, j):
        hr = self.ins[a].shape[0] // 2
        return pltpu.make_async_remote_copy(
            src_ref=self.ins[a].at[pl.ds(self.c * hr, hr)], dst_ref=self._half(a, self.me, self.c),
            send_sem=self.send1.at[a, j], recv_sem=self.recv1.at[a, j], device_id=(*self.chips[j], self.c), device_id_type=MESH)

    def _from_chip(self, a, j):
        landed = self._half(a, _chip_index(*self.chips[j]), self.c)
        return pltpu.make_async_remote_copy(
            src_ref=landed, dst_ref=landed, send_sem=self.send1.at[a, j], recv_sem=self.recv1.at[a, j],
            device_id=(*self.chips[j], self.c), device_id_type=MESH)

    def _to_sibling(self, a, j, core):
        part = self._half(a, _chip_index(*self.chips[j]), core)
        return pltpu.make_async_remote_copy(
            src_ref=part, dst_ref=part, send_sem=self.send2.at[a, j], recv_sem=self.recv2.at[a, j],
            device_id=self.sibling, device_id_type=MESH)

    def _each(self):
        return [(a, j) for a in range(len(self.ins)) for j in range(3)]

    def start(self):
        for a in range(len(self.ins)):
            self._own(a).start()
        for a, j in self._each():
            self._to_chip(a, j).start()

    def forward(self):
        for a, j in self._each():
            self._from_chip(a, j).wait_recv()
            self._to_sibling(a, j, self.c).start()

    def finish(self):
        for a, j in self._each():
            self._to_sibling(a, j, 1 - self.c).wait_recv()
        for a, j in self._each():
            self._to_chip(a, j).wait_send()
            self._to_sibling(a, j, self.c).wait_send()
        for a in range(len(self.ins)):
            self._own(a).wait()


def _pair_exchange(grads):
    n = len(grads)

    def body(*refs):
        ins, outs = refs[:n], refs[n:2 * n]
        send, recv = refs[2 * n:]
        x, y, c, _ = _mesh_position()
        copies = []
        for a in range(n):
            hr = ins[a].shape[1] // 2
            cp = pltpu.make_async_remote_copy(
                src_ref=ins[a].at[:, pl.ds((1 - c) * hr, hr)], dst_ref=outs[a],
                send_sem=send.at[a], recv_sem=recv.at[a], device_id=(x, y, 1 - c), device_id_type=MESH)
            cp.start()
            copies.append(cp)
        for cp in copies:
            cp.wait()

    sem = pltpu.SemaphoreType.DMA
    return pl.pallas_call(
        body, name="pair_exchange", in_specs=_hbm_specs(n), out_specs=_hbm_specs(n),
        out_shape=[jax.ShapeDtypeStruct((g.shape[0], g.shape[1] // 2, g.shape[2]), g.dtype) for g in grads],
        scratch_shapes=[sem((n,)), sem((n,))],
        compiler_params=pltpu.CompilerParams(has_side_effects=True),
    )(*grads)


def _pair_add(core, grad, other, tr, out_dtype):
    hr = other.shape[1]
    cdim = other.shape[2]
    nb = hr // tr

    def body(core_ref, g_ref, o_ref, out_ref):
        out_ref[...] = (g_ref[...] + o_ref[...]).astype(out_dtype)

    return pl.pallas_call(
        body, name="pair_add",
        grid_spec=pltpu.PrefetchScalarGridSpec(
            num_scalar_prefetch=1, grid=(N_CHIPS, nb),
            in_specs=[pl.BlockSpec((1, tr, cdim), lambda j, i, core_ref: (j, core_ref[0] * nb + i, 0)),
                      pl.BlockSpec((1, tr, cdim), lambda j, i, core_ref: (j, i, 0))],
            out_specs=pl.BlockSpec((1, tr, cdim), lambda j, i, core_ref: (j, i, 0))),
        out_shape=jax.ShapeDtypeStruct(other.shape, out_dtype),
        compiler_params=_cparams(("arbitrary", "arbitrary")),
    )(core, grad, other)


def _scatter_scratch(parts):
    sem = pltpu.SemaphoreType.DMA
    n = len(parts)
    return [sem((n, 3)), sem((n, 3)), sem((n,)), sem((n,))] + [pltpu.VMEM(p.shape[1:], p.dtype) for p in parts]


class _ChipScatter:
    def __init__(self, ins, outs, sems, staged):
        self.ins, self.outs, self.staged = ins, outs, staged
        self.send, self.recv, self.load_sem, self.store_sem = sems
        self.x, self.y, self.c, self.chips = _mesh_position()
        self.me = _chip_index(self.x, self.y)

    def _load(self, a):
        return pltpu.make_async_copy(self.ins[a].at[self.me], self.staged[a], self.load_sem.at[a])

    def _store(self, a):
        return pltpu.make_async_copy(self.staged[a], self.outs[a].at[self.me], self.store_sem.at[a])

    def _to_chip(self, a, j):
        return pltpu.make_async_remote_copy(
            src_ref=self.ins[a].at[_chip_index(*self.chips[j])], dst_ref=self.outs[a].at[self.me],
            send_sem=self.send.at[a, j], recv_sem=self.recv.at[a, j], device_id=(*self.chips[j], self.c), device_id_type=MESH)

    def start(self):
        for a in range(len(self.ins)):
            self._load(a).start()
            for j in range(3):
                self._to_chip(a, j).start()

    def finish(self):
        for a in range(len(self.ins)):
            self._load(a).wait()
            self._store(a).start()
        for a in range(len(self.ins)):
            for j in range(3):
                self._to_chip(a, j).wait()
            self._store(a).wait()


def _chip_add(core, recv, tr):
    hr, cdim = recv.shape[1], recv.shape[2]
    nb = hr // tr

    def body(core_ref, r_ref, out_ref):
        out_ref[...] = ((r_ref[0].astype(F32) + r_ref[1].astype(F32)) + r_ref[2].astype(F32)) + r_ref[3].astype(F32)

    return pl.pallas_call(
        body, name="chip_add",
        grid_spec=pltpu.PrefetchScalarGridSpec(
            num_scalar_prefetch=1, grid=(nb,),
            in_specs=[pl.BlockSpec((N_CHIPS, tr, cdim), lambda i, core_ref: (0, i, 0))],
            out_specs=pl.BlockSpec((tr, cdim), lambda i, core_ref: (core_ref[0] * nb + i, 0))),
        out_shape=jax.ShapeDtypeStruct((2 * hr, cdim), F32),
        compiler_params=_cparams(("arbitrary",)),
    )(core, recv)


def _pair_gather(fulls):
    n = len(fulls)

    def body(*refs):
        outs = refs[n:2 * n]
        send, recv = refs[2 * n:]
        x, y, c, _ = _mesh_position()
        copies = []
        for a in range(n):
            hr = outs[a].shape[0] // 2
            mine = outs[a].at[pl.ds(c * hr, hr)]
            cp = pltpu.make_async_remote_copy(
                src_ref=mine, dst_ref=mine, send_sem=send.at[a], recv_sem=recv.at[a],
                device_id=(x, y, 1 - c), device_id_type=MESH)
            cp.start()
            copies.append(cp)
        for cp in copies:
            cp.wait()

    sem = pltpu.SemaphoreType.DMA
    return pl.pallas_call(
        body, name="pair_gather", in_specs=_hbm_specs(n), out_specs=_hbm_specs(n),
        out_shape=[jax.ShapeDtypeStruct(f.shape, f.dtype) for f in fulls],
        input_output_aliases={a: a for a in range(n)},
        scratch_shapes=[sem((n,)), sem((n,))],
        compiler_params=pltpu.CompilerParams(has_side_effects=True),
    )(*fulls)


def _row_tile(rows):
    if rows <= 512:
        return rows
    for t in (512, 256, 128, 64, 32, 16, 8):
        if rows % t == 0:
            return t
    raise ValueError(rows)


def _pair_sums(core, grads, ici_dtypes):
    others = _pair_exchange(grads)
    return [_pair_add(core, g, o, _row_tile(o.shape[1]), dt) for g, o, dt in zip(grads, others, ici_dtypes)]


def _finish_reduce(core, landed):
    return _pair_gather([_chip_add(core, r, _row_tile(r.shape[1])) for r in landed])


def _adamw(w, g, m, v):
    rows, cols = w.shape
    one_block = rows % 8 != 0 or rows * max(cols, 128) * 4 <= (1 << 20)
    tr = rows if one_block else _row_tile(rows)

    def body(w_ref, g_ref, m_ref, v_ref, d_ref, nm_ref, nv_ref):
        g_ = g_ref[...]
        m_ = ADAM_B1 * m_ref[...] + (1.0 - ADAM_B1) * g_
        v_ = ADAM_B2 * v_ref[...] + (1.0 - ADAM_B2) * (g_ * g_)
        m_hat = m_ / (1.0 - ADAM_B1 ** ADAM_STEP)
        v_hat = v_ / (1.0 - ADAM_B2 ** ADAM_STEP)
        d_ref[...] = -ADAM_LR * (m_hat / (jnp.sqrt(v_hat) + ADAM_EPS) + ADAM_WD * w_ref[...])
        nm_ref[...] = m_
        nv_ref[...] = v_

    spec = pl.BlockSpec((tr, cols), lambda i: (i, 0))
    shp = jax.ShapeDtypeStruct((rows, cols), F32)
    return pl.pallas_call(
        body, name="adamw", grid=(rows // tr,), in_specs=[spec] * 4, out_specs=[spec] * 3,
        out_shape=[shp] * 3, compiler_params=_cparams(("arbitrary",)),
    )(w, g, m, v)


_SMALL = ["norm_pre", "norm_post", "ssm_a_re", "ssm_a_im", "ssm_log_dt", "ssm_b_re", "ssm_b_im",
          "ssm_c_re", "ssm_c_im", "ssm_d", "b_glu", "na_rpb", "ple_norm"]
_BIG = ["w_in", "w_glu", "w_out", "w_ple", "w_ple_gate"]
_WEIGHTS = ["norm_pre", "norm_post", "w_in", "ssm_a_re", "ssm_a_im", "ssm_log_dt", "ssm_b_re", "ssm_b_im",
            "ssm_c_re", "ssm_c_im", "ssm_d", "w_glu", "b_glu", "na_rpb", "w_out", "w_ple", "ple_norm", "w_ple_gate"]
_SMALL_ROWS = 2176


def _pack_small(tensors, tail=None):
    parts = [tensors[n].reshape(-1) for n in _SMALL] + ([] if tail is None else [tail.reshape(-1)])
    flat = jnp.concatenate(parts)
    flat = jnp.pad(flat, (0, _SMALL_ROWS * 128 - flat.shape[0]))
    return flat.reshape(_SMALL_ROWS, 128)


def _unpack_small(packed, shapes):
    flat = packed.reshape(-1)
    out, off = {}, 0
    for n in _SMALL:
        size = int(np.prod(shapes[n]))
        out[n] = flat[off:off + size].reshape(shapes[n])
        off += size
    return out


def _local_grads(x, p, target, wts):
    ssm_names = ["ssm_a_re", "ssm_a_im", "ssm_log_dt", "ssm_b_re", "ssm_b_im", "ssm_c_re", "ssm_c_im", "ssm_d"]
    ssm_params = [wts[n][0] for n in ssm_names]
    blk, blk_vjp = jax.vjp(_ssm_block_params, *ssm_params)
    shard = lambda n: wts[n][0].astype(BF16)
    (m_mat, ws_mat, wot_mat, a16), (w_in_g,) = _ssm_chunk_matrices(blk, [shard("w_in")])
    seq = x.shape[0]
    bias_rows, bias_rows_vjp = jax.vjp(_na_bias_rows, wts["na_rpb"][0])
    bias_tab = _na_bias_table(bias_rows, seq // GRID_W)

    (u_c, z_s, q_t, q, k_t, k, v_t, v, z_n), gathered = _in_proj(
        x, wts["norm_pre"], w_in_g, [shard(n) for n in _BIG if n != "w_in"])
    w_glu, w_out, w_ple_g, w_pg = (gathered[0].reshape(512, 512), gathered[1].reshape(1024, 1024), gathered[2],
                                   gathered[3].reshape(1024, 1024))
    s_in = _block_matmul([(u_c, ws_mat, False)], "ssm_chunk_states", tn=2048)
    s_prev = _ssm_state_scan(s_in, a16)
    y_ssm_c = _block_matmul([(u_c, m_mat, False), (s_prev, wot_mat, True)], "ssm_chunk_out")
    y_na_t = _na_fwd(q_t, k, v_t, bias_tab)
    cat = _branch_fwd(y_ssm_c, z_s, y_na_t, z_n, w_glu, wts["b_glu"])

    (loss, d_h1, d_cat, d_w_out, d_g_post, d_w_ple, d_g_ple, d_w_pg) = _head(
        x, p, target, cat, w_out, wts["norm_post"], w_ple_g, wts["ple_norm"], w_pg)
    dy_c, d_z_s, d_y_na_t, d_y_na, d_z_n, d_w_glu, d_b_glu = _branch_bwd(
        y_ssm_c, z_s, y_na_t, z_n, w_glu, wts["b_glu"], d_cat)
    d_q_t, d_k, d_v, d_bias_tab = _na_bwd(q_t, q, k_t, k, v, bias_tab, y_na_t, d_y_na_t, d_y_na)

    d_prev = _block_matmul([(dy_c, wot_mat, False)], "ssm_bwd_states", tn=2048)
    g_st, d_a16 = _ssm_state_scan_bwd(d_prev, s_prev, a16)
    d_u_c = _block_matmul([(dy_c, m_mat, True), (g_st, ws_mat, True)], "ssm_bwd_in", out_dtype=BF16)
    d_m = _block_matmul_tn(u_c, dy_c, "ssm_grad_m")
    d_ws = _block_matmul_tn(u_c, g_st, "ssm_grad_ws")
    d_wot = _block_matmul_tn(dy_c, s_prev, "ssm_grad_wot")
    d_ssm = blk_vjp(tuple(_ssm_chunk_matrices_bwd(blk, d_m, d_ws, d_wot, d_a16)))
    (d_rpb,) = bias_rows_vjp(_na_bias_table_bwd(d_bias_tab, seq // GRID_W))

    dparts = [d_u_c, d_z_s, d_q_t, d_k, d_v, d_z_n]
    d_w_in, d_g_pre = _in_proj_bwd_w(x, wts["norm_pre"].reshape(D_MODEL, 1), w_in_g, dparts)

    small = {"norm_pre": d_g_pre, "norm_post": d_g_post, "b_glu": d_b_glu, "na_rpb": d_rpb, "ple_norm": d_g_ple}
    for n, g in zip(ssm_names, d_ssm):
        small[n] = g
    big = {"w_in": d_w_in, "w_glu": d_w_glu.reshape(N_CHIPS, 128, 512), "w_out": d_w_out.reshape(N_CHIPS, 256, 1024),
           "w_ple": d_w_ple, "w_ple_gate": d_w_pg.reshape(N_CHIPS, 256, 1024)}
    return loss, small, big, (x, wts["norm_pre"], w_in_g, d_h1, dparts)


def kernel(x, p, norm_pre, norm_post, w_in, ssm_a_re, ssm_a_im, ssm_log_dt, ssm_b_re, ssm_b_im, ssm_c_re, ssm_c_im, ssm_d, w_glu, b_glu, na_rpb, w_out, w_ple, ple_norm, w_ple_gate, loss_target, m_norm_pre, m_norm_post, m_w_in, m_ssm_a_re, m_ssm_a_im, m_ssm_log_dt, m_ssm_b_re, m_ssm_b_im, m_ssm_c_re, m_ssm_c_im, m_ssm_d, m_w_glu, m_b_glu, m_na_rpb, m_w_out, m_w_ple, m_ple_norm, m_w_ple_gate, v_norm_pre, v_norm_post, v_w_in, v_ssm_a_re, v_ssm_a_im, v_ssm_log_dt, v_ssm_b_re, v_ssm_b_im, v_ssm_c_re, v_ssm_c_im, v_ssm_d, v_w_glu, v_b_glu, v_na_rpb, v_w_out, v_w_ple, v_ple_norm, v_w_ple_gate):
    wts = dict(norm_pre=norm_pre, norm_post=norm_post, w_in=w_in, ssm_a_re=ssm_a_re, ssm_a_im=ssm_a_im,
               ssm_log_dt=ssm_log_dt, ssm_b_re=ssm_b_re, ssm_b_im=ssm_b_im, ssm_c_re=ssm_c_re, ssm_c_im=ssm_c_im,
               ssm_d=ssm_d, w_glu=w_glu, b_glu=b_glu, na_rpb=na_rpb, w_out=w_out, w_ple=w_ple, ple_norm=ple_norm,
               w_ple_gate=w_ple_gate)
    mom_m = dict(norm_pre=m_norm_pre, norm_post=m_norm_post, w_in=m_w_in, ssm_a_re=m_ssm_a_re, ssm_a_im=m_ssm_a_im,
                 ssm_log_dt=m_ssm_log_dt, ssm_b_re=m_ssm_b_re, ssm_b_im=m_ssm_b_im, ssm_c_re=m_ssm_c_re,
                 ssm_c_im=m_ssm_c_im, ssm_d=m_ssm_d, w_glu=m_w_glu, b_glu=m_b_glu, na_rpb=m_na_rpb, w_out=m_w_out,
                 w_ple=m_w_ple, ple_norm=m_ple_norm, w_ple_gate=m_w_ple_gate)
    mom_v = dict(norm_pre=v_norm_pre, norm_post=v_norm_post, w_in=v_w_in, ssm_a_re=v_ssm_a_re, ssm_a_im=v_ssm_a_im,
                 ssm_log_dt=v_ssm_log_dt, ssm_b_re=v_ssm_b_re, ssm_b_im=v_ssm_b_im, ssm_c_re=v_ssm_c_re,
                 ssm_c_im=v_ssm_c_im, ssm_d=v_ssm_d, w_glu=v_w_glu, b_glu=v_b_glu, na_rpb=v_na_rpb, w_out=v_w_out,
                 w_ple=v_w_ple, ple_norm=v_ple_norm, w_ple_gate=v_w_ple_gate)

    loss_part, small, big, input_grad_args = _local_grads(x[0], p[0, 0], loss_target[0], wts)

    core = lax.axis_index("c").astype(jnp.int32).reshape(1)
    small_packed = _pack_small(small, tail=loss_part).reshape(N_CHIPS, _SMALL_ROWS // N_CHIPS, 128)
    pair = _pair_sums(core, [big[n] for n in _BIG] + [small_packed], [BF16] * len(_BIG) + [F32])
    grad_x, landed = _in_proj_bwd_x(*input_grad_args, pair)
    reduced = _finish_reduce(core, landed)
    grads = dict(zip(_BIG, reduced[:-1]))
    (small_all,) = _gather_chips([reduced[-1]], "gather_small_grads")
    small_all = small_all.reshape(_SMALL_ROWS, 128)
    loss = small_all.reshape(-1)[sum(int(np.prod(wts[n].shape)) for n in _SMALL)]

    delta, new_m, new_v = {}, {}, {}
    for n in _BIG:
        shp = wts[n].shape
        d_, m_, v_ = _adamw(wts[n][0], grads[n], mom_m[n][0], mom_v[n][0])
        grads[n] = grads[n].reshape(shp)
        delta[n], new_m[n], new_v[n] = d_.reshape(shp), m_.reshape(shp), v_.reshape(shp)
    grads.update(_unpack_small(small_all, {n: wts[n].shape for n in _SMALL}))
    for n in _SMALL:
        shp = wts[n].shape
        swap = shp[-1] < shp[-2]
        view_shape = shp[:-2] + (shp[-1], shp[-2]) if swap else shp
        rows_cols = (int(np.prod(view_shape[:-1])), view_shape[-1])
        view = lambda t: (jnp.swapaxes(t, -1, -2) if swap else t).reshape(rows_cols)
        back = lambda t: jnp.swapaxes(t.reshape(view_shape), -1, -2) if swap else t.reshape(shp)
        d_, m_, v_ = _adamw(*[view(t) for t in (wts[n], grads[n], mom_m[n], mom_v[n])])
        delta[n], new_m[n], new_v[n] = back(d_), back(m_), back(v_)

    return (loss, grad_x[None], *[grads[n] for n in _WEIGHTS], *[delta[n] for n in _WEIGHTS],
            *[new_m[n] for n in _WEIGHTS], *[new_v[n] for n in _WEIGHTS])
```
